```python
import math
import jax, jax.numpy as jnp
from jax import lax
import numpy as np

D_MODEL = 1024
BATCH = 8
SEQ = 4096
DEPTH = 2

HEAD_DIM = 64
SB_HEADS = 8
SW_HEADS = 8
SW_KV_HEADS = 2
SW_GROUP = SW_HEADS // SW_KV_HEADS
WINDOW = 128
BLOCK = 128
D_FF = 4 * D_MODEL
PLE_DIM = 256
N_BUCKETS = 32
MAX_DISTANCE = 128
EPS = 1e-6

SB_W = SB_HEADS * HEAD_DIM
SW_QW = SW_HEADS * HEAD_DIM
SW_KVW = SW_KV_HEADS * HEAD_DIM
IN_COLS = 3 * SB_W + SW_QW + 2 * SW_KVW + 2 * D_MODEL

kernel_name = "stick_breaking_swa_sink_hybrid_block"


def rmsnorm(x, g):
    xf = x.astype(jnp.float32)
    r = lax.rsqrt(jnp.mean(xf * xf, axis=-1, keepdims=True) + EPS)
    return (xf * r).astype(x.dtype) * g


def t5_causal_bucket(dist):
    max_exact = N_BUCKETS // 2
    d = jnp.maximum(dist, 0)
    df = jnp.maximum(d, 1).astype(jnp.float32)
    large = max_exact + (jnp.log(df / max_exact) / math.log(MAX_DISTANCE / max_exact)
                         * (N_BUCKETS - max_exact)).astype(jnp.int32)
    large = jnp.minimum(large, N_BUCKETS - 1)
    return jnp.where(d < max_exact, d, large)


def stick_breaking_attention(q, k, v):
    b_, s_, h_, dh = q.shape
    nb = s_ // BLOCK
    scale = dh ** -0.5
    qb = q.reshape(b_, nb, BLOCK, h_, dh).transpose(1, 0, 3, 2, 4)
    kpos = jnp.arange(s_)

    def one_block(args):
        qblk, n = args
        z = jnp.einsum('bhqd,bshd->bhqs', qblk, k).astype(jnp.float32) * scale
        qpos = n * BLOCK + jnp.arange(BLOCK)
        causal = kpos[None, :] < qpos[:, None]
        log_beta = jax.nn.log_sigmoid(z)
        log_1mb = jnp.where(causal, jax.nn.log_sigmoid(-z), 0.0)
        suffix = lax.cumsum(log_1mb, axis=3, reverse=True) - log_1mb
        a = jnp.where(causal, jnp.exp(log_beta + suffix), 0.0)
        return jnp.einsum('bhqs,bshd->bqhd', a.astype(v.dtype), v)

    o = lax.map(one_block, (qb, jnp.arange(nb)))
    return o.transpose(1, 0, 2, 3, 4).reshape(b_, s_, h_ * dh)


def sliding_window_sink_attention(q, k, v, sinks, bias):
    b_, s_ = q.shape[:2]
    nb = s_ // BLOCK
    scale = HEAD_DIM ** -0.5
    qb = q.reshape(b_, nb, BLOCK, SW_KV_HEADS, SW_GROUP, HEAD_DIM)
    pad = jnp.zeros((b_, BLOCK, SW_KV_HEADS, HEAD_DIM), k.dtype)
    kp = jnp.concatenate([pad, k], axis=1).reshape(b_, nb + 1, BLOCK, SW_KV_HEADS, HEAD_DIM)
    vp = jnp.concatenate([pad, v], axis=1).reshape(b_, nb + 1, BLOCK, SW_KV_HEADS, HEAD_DIM)
    kb = jnp.concatenate([kp[:, :-1], kp[:, 1:]], axis=2)
    vb = jnp.concatenate([vp[:, :-1], vp[:, 1:]], axis=2)
    s = jnp.einsum('bnqkgd,bnskd->bnkgqs', qb, kb).astype(jnp.float32) * scale + bias
    i = jnp.arange(BLOCK)[:, None]
    j = jnp.arange(2 * BLOCK)[None, :]
    dist = BLOCK + i - j
    band = (dist >= 0) & (dist < WINDOW)
    key_abs = (jnp.arange(nb)[:, None, None] - 1) * BLOCK + j[None]
    valid = band[None] & (key_abs >= 0)
    s = jnp.where(valid[None, :, None, None], s, -jnp.inf)
    sink = sinks.astype(jnp.float32).reshape(SW_KV_HEADS, SW_GROUP)[None, None, :, :, None, None]
    m = jnp.maximum(jnp.max(s, axis=-1, keepdims=True), sink)
    e = jnp.exp(s - m)
    denom = jnp.sum(e, axis=-1, keepdims=True) + jnp.exp(sink - m)
    pr = (e / denom).astype(v.dtype)
    o = jnp.einsum('bnkgqs,bnskd->bnqkgd', pr, vb)
    return o.reshape(b_, s_, SW_HEADS * HEAD_DIM)


def window_bias(rel_bias):
    i = jnp.arange(BLOCK)[:, None]
    j = jnp.arange(2 * BLOCK)[None, :]
    dist = BLOCK + i - j
    bias = rel_bias[t5_causal_bucket(dist)]
    return bias.transpose(2, 0, 1).reshape(SW_KV_HEADS, SW_GROUP, BLOCK, 2 * BLOCK)


def _fwd_setup_inputs(seed: int = 0) -> dict:
    key = jax.random.key(seed)
    ks = jax.random.split(key, 20)
    f32 = jnp.float32

    def nrm(k, shape, fan_in):
        return jax.random.normal(k, shape, f32) * (fan_in ** -0.5)

    return {
        "x": jax.random.normal(ks[0], (BATCH, SEQ, D_MODEL), f32),
        "p": jax.random.normal(ks[1], (DEPTH, BATCH, SEQ, PLE_DIM), f32),
        "w_in": nrm(ks[2], (DEPTH, D_MODEL, IN_COLS), D_MODEL),
        "w_up_a": nrm(ks[3], (DEPTH, SB_W, D_MODEL), SB_W),
        "w_up_b": nrm(ks[4], (DEPTH, SW_QW, D_MODEL), SW_QW),
        "w_o": nrm(ks[5], (DEPTH, D_MODEL, D_MODEL), D_MODEL),
        "w_ff1": nrm(ks[6], (DEPTH, D_MODEL, D_FF), D_MODEL),
        "w_ff2": nrm(ks[7], (DEPTH, D_FF, D_MODEL), D_FF),
        "w_pe": nrm(ks[8], (DEPTH, PLE_DIM, D_MODEL), PLE_DIM),
        "w_pg": nrm(ks[9], (DEPTH, D_MODEL, D_MODEL), D_MODEL),
        "g_mix": 1.0 + 0.01 * jax.random.normal(ks[10], (DEPTH, D_MODEL), f32),
        "g_mlp": 1.0 + 0.01 * jax.random.normal(ks[11], (DEPTH, D_MODEL), f32),
        "g_pe": 1.0 + 0.01 * jax.random.normal(ks[12], (DEPTH, D_MODEL), f32),
        "g_final": 1.0 + 0.01 * jax.random.normal(ks[13], (D_MODEL,), f32),
        "sinks": 0.5 * jax.random.normal(ks[14], (DEPTH, SW_HEADS), f32),
        "rel_bias": 0.5 * jax.random.normal(ks[15], (N_BUCKETS, SW_HEADS), f32),
    }


def _fwd_reference(x, p, w_in, w_up_a, w_up_b, w_o, w_ff1, w_ff2, w_pe, w_pg,
              g_mix, g_mlp, g_pe, g_final, sinks, rel_bias):
    b_, s_, _ = x.shape
    bias = window_bias(rel_bias)
    o1 = SB_W
    o2 = o1 + SB_W
    o3 = o2 + SB_W
    o4 = o3 + SW_QW
    o5 = o4 + SW_KVW
    o6 = o5 + SW_KVW
    o7 = o6 + D_MODEL
    for i in range(DEPTH):
        h = rmsnorm(x, g_mix[i])
        proj = h @ w_in[i]
        q_a = proj[..., :o1].reshape(b_, s_, SB_HEADS, HEAD_DIM)
        k_a = proj[..., o1:o2].reshape(b_, s_, SB_HEADS, HEAD_DIM)
        v_a = proj[..., o2:o3].reshape(b_, s_, SB_HEADS, HEAD_DIM)
        q_b = proj[..., o3:o4].reshape(b_, s_, SW_HEADS, HEAD_DIM)
        k_b = proj[..., o4:o5].reshape(b_, s_, SW_KV_HEADS, HEAD_DIM)
        v_b = proj[..., o5:o6].reshape(b_, s_, SW_KV_HEADS, HEAD_DIM)
        gate_a = proj[..., o6:o7]
        gate_b = proj[..., o7:]
        y_a = stick_breaking_attention(q_a, k_a, v_a) @ w_up_a[i]
        y_b = sliding_window_sink_attention(q_b, k_b, v_b, sinks[i], bias) @ w_up_b[i]
        merged = jax.nn.sigmoid(gate_a) * y_a + jax.nn.sigmoid(gate_b) * y_b
        x = x + merged @ w_o[i]
        h = rmsnorm(x, g_mlp[i])
        x = x + jnp.square(jax.nn.relu(h @ w_ff1[i])) @ w_ff2[i]
        pe = p[i] @ w_pe[i]
        x = x + pe * jax.nn.sigmoid(rmsnorm(x, g_pe[i]) @ w_pg[i])
    return rmsnorm(x, g_final)


import jax as _jax
import jax.numpy as _jnp

TWIN_FORMAT = 'train_step'
FWD_PARAMS = ['x', 'p', 'w_in', 'w_up_a', 'w_up_b', 'w_o', 'w_ff1', 'w_ff2', 'w_pe', 'w_pg', 'g_mix', 'g_mlp', 'g_pe', 'g_final', 'sinks', 'rel_bias']
TWIN_WEIGHTS = ['w_in', 'w_up_a', 'w_up_b', 'w_o', 'w_ff1', 'w_ff2', 'w_pe', 'w_pg', 'g_mix', 'g_mlp', 'g_pe', 'g_final', 'sinks', 'rel_bias']
TWIN_DIFF_INPUT = 'x'
TWIN_INPUTS = ['x', 'p', 'w_in', 'w_up_a', 'w_up_b', 'w_o', 'w_ff1', 'w_ff2', 'w_pe', 'w_pg', 'g_mix', 'g_mlp', 'g_pe', 'g_final', 'sinks', 'rel_bias', 'loss_target', 'm_w_in', 'm_w_up_a', 'm_w_up_b', 'm_w_o', 'm_w_ff1', 'm_w_ff2', 'm_w_pe', 'm_w_pg', 'm_g_mix', 'm_g_mlp', 'm_g_pe', 'm_g_final', 'm_sinks', 'm_rel_bias', 'v_w_in', 'v_w_up_a', 'v_w_up_b', 'v_w_o', 'v_w_ff1', 'v_w_ff2', 'v_w_pe', 'v_w_pg', 'v_g_mix', 'v_g_mlp', 'v_g_pe', 'v_g_final', 'v_sinks', 'v_rel_bias']
TWIN_OUTPUTS = ['loss', 'grad_x', 'grad_w_in', 'grad_w_up_a', 'grad_w_up_b', 'grad_w_o', 'grad_w_ff1', 'grad_w_ff2', 'grad_w_pe', 'grad_w_pg', 'grad_g_mix', 'grad_g_mlp', 'grad_g_pe', 'grad_g_final', 'grad_sinks', 'grad_rel_bias', 'delta_w_in', 'delta_w_up_a', 'delta_w_up_b', 'delta_w_o', 'delta_w_ff1', 'delta_w_ff2', 'delta_w_pe', 'delta_w_pg', 'delta_g_mix', 'delta_g_mlp', 'delta_g_pe', 'delta_g_final', 'delta_sinks', 'delta_rel_bias', 'new_m_w_in', 'new_m_w_up_a', 'new_m_w_up_b', 'new_m_w_o', 'new_m_w_ff1', 'new_m_w_ff2', 'new_m_w_pe', 'new_m_w_pg', 'new_m_g_mix', 'new_m_g_mlp', 'new_m_g_pe', 'new_m_g_final', 'new_m_sinks', 'new_m_rel_bias', 'new_v_w_in', 'new_v_w_up_a', 'new_v_w_up_b', 'new_v_w_o', 'new_v_w_ff1', 'new_v_w_ff2', 'new_v_w_pe', 'new_v_w_pg', 'new_v_g_mix', 'new_v_g_mlp', 'new_v_g_pe', 'new_v_g_final', 'new_v_sinks', 'new_v_rel_bias']
TWIN_LEAF_KINDS = {'loss': 'loss', 'grad_x': 'grad_x', 'grad_w_in': 'grad_w', 'grad_w_up_a': 'grad_w', 'grad_w_up_b': 'grad_w', 'grad_w_o': 'grad_w', 'grad_w_ff1': 'grad_w', 'grad_w_ff2': 'grad_w', 'grad_w_pe': 'grad_w', 'grad_w_pg': 'grad_w', 'grad_g_mix': 'grad_w', 'grad_g_mlp': 'grad_w', 'grad_g_pe': 'grad_w', 'grad_g_final': 'grad_w', 'grad_sinks': 'grad_w', 'grad_rel_bias': 'grad_w', 'delta_w_in': 'delta_w', 'delta_w_up_a': 'delta_w', 'delta_w_up_b': 'delta_w', 'delta_w_o': 'delta_w', 'delta_w_ff1': 'delta_w', 'delta_w_ff2': 'delta_w', 'delta_w_pe': 'delta_w', 'delta_w_pg': 'delta_w', 'delta_g_mix': 'delta_w', 'delta_g_mlp': 'delta_w', 'delta_g_pe': 'delta_w', 'delta_g_final': 'delta_w', 'delta_sinks': 'delta_w', 'delta_rel_bias': 'delta_w', 'new_m_w_in': 'new_m', 'new_m_w_up_a': 'new_m', 'new_m_w_up_b': 'new_m', 'new_m_w_o': 'new_m', 'new_m_w_ff1': 'new_m', 'new_m_w_ff2': 'new_m', 'new_m_w_pe': 'new_m', 'new_m_w_pg': 'new_m', 'new_m_g_mix': 'new_m', 'new_m_g_mlp': 'new_m', 'new_m_g_pe': 'new_m', 'new_m_g_final': 'new_m', 'new_m_sinks': 'new_m', 'new_m_rel_bias': 'new_m', 'new_v_w_in': 'new_v', 'new_v_w_up_a': 'new_v', 'new_v_w_up_b': 'new_v', 'new_v_w_o': 'new_v', 'new_v_w_ff1': 'new_v', 'new_v_w_ff2': 'new_v', 'new_v_w_pe': 'new_v', 'new_v_w_pg': 'new_v', 'new_v_g_mix': 'new_v', 'new_v_g_mlp': 'new_v', 'new_v_g_pe': 'new_v', 'new_v_g_final': 'new_v', 'new_v_sinks': 'new_v', 'new_v_rel_bias': 'new_v'}


def _forward(args):
    return _fwd_reference(*[args[k] for k in FWD_PARAMS])


def _output_shape():
    def fwd():
        inp = _fwd_setup_inputs(0)
        return _fwd_reference(*[inp[k] for k in FWD_PARAMS])
    out = _jax.eval_shape(fwd)
    return out.shape, out.dtype

N_MICROBATCH = 1
ADAM_LR = 0.001
ADAM_B1 = 0.9
ADAM_B2 = 0.999
ADAM_EPS = 1e-08
ADAM_WD = 0.01
ADAM_STEP = 10
PER_EXAMPLE_BATCH_AXIS = {'x': 0, 'p': 1, 'loss_target': 0}
SHARED_INPUTS = []
_WEIGHT_DTYPES = {'w_in': _jnp.float32, 'w_up_a': _jnp.float32, 'w_up_b': _jnp.float32, 'w_o': _jnp.float32, 'w_ff1': _jnp.float32, 'w_ff2': _jnp.float32, 'w_pe': _jnp.float32, 'w_pg': _jnp.float32, 'g_mix': _jnp.float32, 'g_mlp': _jnp.float32, 'g_pe': _jnp.float32, 'g_final': _jnp.float32, 'sinks': _jnp.float32, 'rel_bias': _jnp.float32}
MOMENT_SCALE = {'w_in': 3.452498e-02, 'w_up_a': 5.291607e-02, 'w_up_b': 2.114242e-02, 'w_o': 5.610791e-02, 'w_ff1': 6.752866e-02, 'w_ff2': 1.264709e-01, 'w_pe': 5.110596e-02, 'w_pg': 2.011899e-02, 'g_mix': 7.185154e-02, 'g_mlp': 1.411449e-01, 'g_pe': 2.020704e-02, 'g_final': 3.262850e+01, 'sinks': 1.592106e-02, 'rel_bias': 3.481839e-02}


def _to_microbatches(a, axis):
    t = _jnp.moveaxis(a, axis, 0)
    t = t.reshape((N_MICROBATCH, t.shape[0] // N_MICROBATCH) + t.shape[1:])
    return _jnp.moveaxis(t, 1, axis + 1)


def setup_inputs(seed: int = 0) -> dict:
    inp = _fwd_setup_inputs(seed)
    key = _jax.random.fold_in(_jax.random.key(seed), 7919)
    shape, _ = _output_shape()
    out = dict(inp)
    out["loss_target"] = _jax.random.normal(_jax.random.fold_in(key, 0), shape, _jnp.float32)
    for i, name in enumerate(TWIN_WEIGHTS):
        w = inp[name].astype(_jnp.float32)
        if MOMENT_SCALE is None:
            s = _jnp.sqrt(_jnp.mean(_jnp.square(w)) + 1e-30)
        else:
            s = MOMENT_SCALE[name]
        km, kv = _jax.random.split(_jax.random.fold_in(key, i + 1))
        out[name] = w
        out["m_" + name] = s * _jax.random.normal(km, w.shape, _jnp.float32)
        out["v_" + name] = (s * s) * _jax.random.uniform(kv, w.shape, _jnp.float32, 0.5, 1.5)
    if N_MICROBATCH > 1:
        for name, axis in PER_EXAMPLE_BATCH_AXIS.items():
            out[name] = _to_microbatches(out[name], axis)
    return {'x': out['x'], 'p': out['p'], 'w_in': out['w_in'], 'w_up_a': out['w_up_a'], 'w_up_b': out['w_up_b'], 'w_o': out['w_o'], 'w_ff1': out['w_ff1'], 'w_ff2': out['w_ff2'], 'w_pe': out['w_pe'], 'w_pg': out['w_pg'], 'g_mix': out['g_mix'], 'g_mlp': out['g_mlp'], 'g_pe': out['g_pe'], 'g_final': out['g_final'], 'sinks': out['sinks'], 'rel_bias': out['rel_bias'], 'loss_target': out['loss_target'], 'm_w_in': out['m_w_in'], 'm_w_up_a': out['m_w_up_a'], 'm_w_up_b': out['m_w_up_b'], 'm_w_o': out['m_w_o'], 'm_w_ff1': out['m_w_ff1'], 'm_w_ff2': out['m_w_ff2'], 'm_w_pe': out['m_w_pe'], 'm_w_pg': out['m_w_pg'], 'm_g_mix': out['m_g_mix'], 'm_g_mlp': out['m_g_mlp'], 'm_g_pe': out['m_g_pe'], 'm_g_final': out['m_g_final'], 'm_sinks': out['m_sinks'], 'm_rel_bias': out['m_rel_bias'], 'v_w_in': out['v_w_in'], 'v_w_up_a': out['v_w_up_a'], 'v_w_up_b': out['v_w_up_b'], 'v_w_o': out['v_w_o'], 'v_w_ff1': out['v_w_ff1'], 'v_w_ff2': out['v_w_ff2'], 'v_w_pe': out['v_w_pe'], 'v_w_pg': out['v_w_pg'], 'v_g_mix': out['v_g_mix'], 'v_g_mlp': out['v_g_mlp'], 'v_g_pe': out['v_g_pe'], 'v_g_final': out['v_g_final'], 'v_sinks': out['v_sinks'], 'v_rel_bias': out['v_rel_bias']}


def _loss(weights, diff, rest, loss_target):
    with _jax.named_scope("forward"):
        args = {**rest, TWIN_DIFF_INPUT: diff, **{k: w.astype(_WEIGHT_DTYPES[k]) for k, w in weights.items()}}
        y = _forward(args)
    with _jax.named_scope("loss_head"):
        err = _jnp.square(y.astype(_jnp.float32) - loss_target)
        return 0.5 * _jnp.sum(_jnp.mean(err, axis=-1)) if err.ndim else 0.5 * err


def _adamw(w, g, m, v):
    m = ADAM_B1 * m + (1.0 - ADAM_B1) * g
    v = ADAM_B2 * v + (1.0 - ADAM_B2) * _jnp.square(g)
    m_hat = m / (1.0 - ADAM_B1 ** ADAM_STEP)
    v_hat = v / (1.0 - ADAM_B2 ** ADAM_STEP)
    delta = -ADAM_LR * (m_hat / (_jnp.sqrt(v_hat) + ADAM_EPS) + ADAM_WD * w)
    return delta, m, v


def reference(x, p, w_in, w_up_a, w_up_b, w_o, w_ff1, w_ff2, w_pe, w_pg, g_mix, g_mlp, g_pe, g_final, sinks, rel_bias, loss_target, m_w_in, m_w_up_a, m_w_up_b, m_w_o, m_w_ff1, m_w_ff2, m_w_pe, m_w_pg, m_g_mix, m_g_mlp, m_g_pe, m_g_final, m_sinks, m_rel_bias, v_w_in, v_w_up_a, v_w_up_b, v_w_o, v_w_ff1, v_w_ff2, v_w_pe, v_w_pg, v_g_mix, v_g_mlp, v_g_pe, v_g_final, v_sinks, v_rel_bias):
    given = dict(x=x, p=p, w_in=w_in, w_up_a=w_up_a, w_up_b=w_up_b, w_o=w_o, w_ff1=w_ff1, w_ff2=w_ff2, w_pe=w_pe, w_pg=w_pg, g_mix=g_mix, g_mlp=g_mlp, g_pe=g_pe, g_final=g_final, sinks=sinks, rel_bias=rel_bias, loss_target=loss_target, m_w_in=m_w_in, m_w_up_a=m_w_up_a, m_w_up_b=m_w_up_b, m_w_o=m_w_o, m_w_ff1=m_w_ff1, m_w_ff2=m_w_ff2, m_w_pe=m_w_pe, m_w_pg=m_w_pg, m_g_mix=m_g_mix, m_g_mlp=m_g_mlp, m_g_pe=m_g_pe, m_g_final=m_g_final, m_sinks=m_sinks, m_rel_bias=m_rel_bias, v_w_in=v_w_in, v_w_up_a=v_w_up_a, v_w_up_b=v_w_up_b, v_w_o=v_w_o, v_w_ff1=v_w_ff1, v_w_ff2=v_w_ff2, v_w_pe=v_w_pe, v_w_pg=v_w_pg, v_g_mix=v_g_mix, v_g_mlp=v_g_mlp, v_g_pe=v_g_pe, v_g_final=v_g_final, v_sinks=v_sinks, v_rel_bias=v_rel_bias)
    weights = {n: given[n] for n in TWIN_WEIGHTS}
    shared = {n: given[n] for n in SHARED_INPUTS}
    per_example = {n: given[n] for n in ['x', 'p']}
    grad_fn = _jax.value_and_grad(_loss, argnums=(0, 1))

    def one_microbatch(ex, loss_target):
        ex = dict(ex)
        diff = ex.pop(TWIN_DIFF_INPUT)
        return grad_fn(weights, diff, {**shared, **ex}, loss_target)

    if N_MICROBATCH == 1:
        loss, (grad_w, grad_x) = one_microbatch(per_example, given["loss_target"])
    else:
        def body(carry, xs):
            loss_sum, grad_sum = carry
            l_k, (gw_k, gx_k) = one_microbatch(xs[0], xs[1])
            with _jax.named_scope("update"):
                return (loss_sum + l_k, _jax.tree.map(_jnp.add, grad_sum, gw_k)), gx_k

        init = (_jnp.zeros((), _jnp.float32), _jax.tree.map(_jnp.zeros_like, weights))
        (loss, grad_w), grad_x = _jax.lax.scan(body, init, (per_example, given["loss_target"]))
    with _jax.named_scope("update"):
        delta_w, new_m, new_v = {}, {}, {}
        for n in TWIN_WEIGHTS:
            delta_w[n], new_m[n], new_v[n] = _adamw(weights[n], grad_w[n], given["m_" + n], given["v_" + n])
    return (loss, grad_x, *[grad_w[n] for n in TWIN_WEIGHTS], *[delta_w[n] for n in TWIN_WEIGHTS],
            *[new_m[n] for n in TWIN_WEIGHTS], *[new_v[n] for n in TWIN_WEIGHTS])
```

```python
import functools
import math

import numpy as np
import jax
import jax.numpy as jnp
from jax import lax
from jax.experimental import pallas as pl
from jax.experimental.pallas import tpu as pltpu

F32 = jnp.float32
MXU = jnp.bfloat16
WIRE = jnp.bfloat16

HEAD_DIM = 64
SB_HEADS = 8
SW_HEADS = 8
SW_KV = 2
SW_GROUP = SW_HEADS // SW_KV
BLOCK = 128
N_BUCKETS = 32
MAX_DISTANCE = 128
EPS = 1e-6
SCALE = HEAD_DIM ** -0.5
SB_W = SB_HEADS * HEAD_DIM
QKV_COLS = 3 * SB_W + SW_HEADS * HEAD_DIM + 2 * SW_KV * HEAD_DIM
N_QKV_HEADS = QKV_COLS // HEAD_DIM
N_DEV = 8
LANES = 128
NEG = -1e30

ADAM_LR = 0.001
ADAM_B1 = 0.9
ADAM_B2 = 0.999
ADAM_EPS = 1e-08
ADAM_WD = 0.01
ADAM_STEP = 10

VMEM_LIMIT = 48 * 1024 * 1024
SB_TQ = 256

WEIGHTS = ("w_in", "w_up_a", "w_up_b", "w_o", "w_ff1", "w_ff2", "w_pe", "w_pg")
COL_SHARDED = ("w_in", "w_up_a", "w_up_b", "w_ff1", "w_pe")
SMALL = ("g_mix", "g_mlp", "g_pe", "g_final", "sinks", "rel_bias")


def _cparams(**kw):
    return pltpu.CompilerParams(vmem_limit_bytes=VMEM_LIMIT, **kw)


def _dot(a, b):
    return jnp.dot(a, b, preferred_element_type=F32)


def _dot_nt(a, b):
    return lax.dot_general(a, b, (((1,), (1,)), ((), ())), preferred_element_type=F32)


def _dot_tn(a, b):
    return lax.dot_general(a, b, (((0,), (0,)), ((), ())), preferred_element_type=F32)


def _tile(n, target):
    if n <= target:
        return n
    t = (target // LANES) * LANES
    while t > LANES and n % t:
        t -= LANES
    assert n % t == 0, (n, target)
    return t


def _sigmoid(x):
    return 1.0 / (1.0 + jnp.exp(-x))


def _mm(a, b, *, ta=False, tb=False, extras=(), epi=None, out_dtypes=(F32,),
        tm=512, tn=512, tk=1024, name):
    if ta:
        kdim, m = a.shape
    else:
        m, kdim = a.shape
    n = b.shape[0] if tb else b.shape[1]
    assert (b.shape[1] if tb else b.shape[0]) == kdim
    tm, tn, tk = _tile(m, tm), _tile(n, tn), _tile(kdim, tk)
    nk = kdim // tk
    n_ex, n_out = len(extras), len(out_dtypes)

    a_spec = (pl.BlockSpec((tk, tm), lambda i, j, k: (k, i)) if ta
              else pl.BlockSpec((tm, tk), lambda i, j, k: (i, k)))
    b_spec = (pl.BlockSpec((tn, tk), lambda i, j, k: (j, k)) if tb
              else pl.BlockSpec((tk, tn), lambda i, j, k: (k, j)))
    ex_specs = []
    for e in extras:
        assert e.shape == (m, n), (e.shape, m, n)
        ex_specs.append(pl.BlockSpec((tm, tn), lambda i, j, k: (i, j)))
    out_spec = pl.BlockSpec((tm, tn), lambda i, j, k: (i, j))

    def body(a_ref, b_ref, *rest):
        ex_refs = rest[:n_ex]
        out_refs = rest[n_ex:n_ex + n_out]
        acc = rest[-1]
        k = pl.program_id(2)

        @pl.when(k == 0)
        def _():
            acc[...] = jnp.zeros_like(acc)

        av = a_ref[...].astype(MXU)
        bv = b_ref[...].astype(MXU)
        if ta:
            acc[...] += _dot_tn(av, bv)
        elif tb:
            acc[...] += _dot_nt(av, bv)
        else:
            acc[...] += _dot(av, bv)

        @pl.when(k == nk - 1)
        def _():
            res = acc[...]
            if epi is not None:
                res = epi(res, *[e[...] for e in ex_refs])
            if not isinstance(res, tuple):
                res = (res,)
            for o_ref, r in zip(out_refs, res):
                o_ref[...] = r.astype(o_ref.dtype)

    outs = pl.pallas_call(
        body,
        grid=(m // tm, n // tn, nk),
        in_specs=[a_spec, b_spec] + ex_specs,
        out_specs=[out_spec] * n_out,
        out_shape=[jax.ShapeDtypeStruct((m, n), dt) for dt in out_dtypes],
        scratch_shapes=[pltpu.VMEM((tm, tn), F32)],
        compiler_params=_cparams(),
        name=name,
    )(a, b, *extras)
    return outs[0] if n_out == 1 else tuple(outs)


def _rms_fwd(x, g, name):
    s, d = x.shape
    tr = _tile(s, 256)

    def body(x_ref, g_ref, h_ref, r_ref):
        xf = x_ref[...]
        r = lax.rsqrt(jnp.mean(xf * xf, axis=-1, keepdims=True) + EPS)
        h_ref[...] = ((xf * r) * g_ref[...]).astype(h_ref.dtype)
        r_ref[...] = r

    return pl.pallas_call(
        body,
        grid=(s // tr,),
        in_specs=[pl.BlockSpec((tr, d), lambda i: (i, 0)), pl.BlockSpec((1, d), lambda i: (0, 0))],
        out_specs=[pl.BlockSpec((tr, d), lambda i: (i, 0)), pl.BlockSpec((tr, 1), lambda i: (i, 0))],
        out_shape=[jax.ShapeDtypeStruct((s, d), MXU), jax.ShapeDtypeStruct((s, 1), F32)],
        compiler_params=_cparams(),
        name=name,
    )(x, g)


def _rms_bwd(x, r, g, dh, dres, name):
    s, d = x.shape
    tr = _tile(s, 256)

    def body(x_ref, r_ref, g_ref, dh_ref, dres_ref, dx_ref, dxb_ref, dg_ref):
        @pl.when(pl.program_id(0) == 0)
        def _():
            dg_ref[...] = jnp.zeros_like(dg_ref)

        rr = r_ref[...]
        xhat = x_ref[...] * rr
        dh_v = dh_ref[...]
        dxhat = dh_v * g_ref[...]
        mean = jnp.mean(dxhat * xhat, axis=-1, keepdims=True)
        dx = dres_ref[...] + rr * (dxhat - xhat * mean)
        dx_ref[...] = dx
        dxb_ref[...] = dx.astype(dxb_ref.dtype)
        dg_ref[...] += jnp.sum(dh_v * xhat, axis=0, keepdims=True)

    row = pl.BlockSpec((tr, d), lambda i: (i, 0))
    vec = pl.BlockSpec((1, d), lambda i: (0, 0))
    return pl.pallas_call(
        body,
        grid=(s // tr,),
        in_specs=[row, pl.BlockSpec((tr, 1), lambda i: (i, 0)), vec, row, row],
        out_specs=[row, row, vec],
        out_shape=[jax.ShapeDtypeStruct((s, d), F32), jax.ShapeDtypeStruct((s, d), MXU),
                   jax.ShapeDtypeStruct((1, d), F32)],
        compiler_params=_cparams(),
        name=name,
    )(x, r, g, dh, dres)


def _loss_head(x, g, target, name):
    s, d = x.shape
    tr = _tile(s, 256)

    def body(x_ref, g_ref, t_ref, loss_ref, dx_ref, dg_ref):
        @pl.when(pl.program_id(0) == 0)
        def _():
            dg_ref[...] = jnp.zeros_like(dg_ref)
            loss_ref[...] = jnp.zeros_like(loss_ref)

        xf = x_ref[...]
        gv = g_ref[...]
        r = lax.rsqrt(jnp.mean(xf * xf, axis=-1, keepdims=True) + EPS)
        xhat = xf * r
        err = xhat * gv - t_ref[...]
        loss_ref[...] += 0.5 * jnp.sum(jnp.mean(err * err, axis=-1, keepdims=True), axis=0, keepdims=True)
        dy = err * (1.0 / d)
        dxhat = dy * gv
        mean = jnp.mean(dxhat * xhat, axis=-1, keepdims=True)
        dx_ref[...] = r * (dxhat - xhat * mean)
        dg_ref[...] += jnp.sum(dy * xhat, axis=0, keepdims=True)

    row = pl.BlockSpec((tr, d), lambda i: (i, 0))
    vec = pl.BlockSpec((1, d), lambda i: (0, 0))
    return pl.pallas_call(
        body,
        grid=(s // tr,),
        in_specs=[row, vec, row],
        out_specs=[pl.BlockSpec((1, LANES), lambda i: (0, 0)), row, vec],
        out_shape=[jax.ShapeDtypeStruct((1, LANES), F32), jax.ShapeDtypeStruct((s, d), F32),
                   jax.ShapeDtypeStruct((1, d), F32)],
        compiler_params=_cparams(),
        name=name,
    )(x, g, target)


def _mix_fwd(oa, ob, wa, wb, gates, name):
    s, kd = oa.shape
    d = wa.shape[1]
    tm, tn = _tile(s, 512), _tile(d, 512)
    nj = d // tn

    def body(oa_ref, ob_ref, wa_ref, wb_ref, ga_ref, gb_ref, out_ref):
        ya = _dot(oa_ref[...], wa_ref[...])
        yb = _dot(ob_ref[...], wb_ref[...])
        out_ref[...] = (_sigmoid(ga_ref[...]) * ya + _sigmoid(gb_ref[...]) * yb).astype(out_ref.dtype)

    o_spec = pl.BlockSpec((tm, kd), lambda i, j: (i, 0))
    w_spec = pl.BlockSpec((kd, tn), lambda i, j: (0, j))
    return pl.pallas_call(
        body,
        grid=(s // tm, nj),
        in_specs=[o_spec, o_spec, w_spec, w_spec,
                  pl.BlockSpec((tm, tn), lambda i, j: (i, j)),
                  pl.BlockSpec((tm, tn), lambda i, j: (i, j + nj))],
        out_specs=pl.BlockSpec((tm, tn), lambda i, j: (i, j)),
        out_shape=jax.ShapeDtypeStruct((s, d), MXU),
        compiler_params=_cparams(),
        name=name,
    )(oa, ob, wa, wb, gates, gates)


def _mix_bwd(dx, w_o, oa, ob, wa, wb, gates, name):
    s, kd = oa.shape
    d = wa.shape[1]
    tm, tn = _tile(s, 512), _tile(d, 512)
    nj = d // tn

    def body(dx_ref, wo_ref, oa_ref, ob_ref, wa_ref, wb_ref, ga_ref, gb_ref,
             dya_ref, dyb_ref, dga_ref, dgb_ref):
        dm = _dot_nt(dx_ref[...], wo_ref[...])
        ya = _dot(oa_ref[...], wa_ref[...])
        yb = _dot(ob_ref[...], wb_ref[...])
        sa = _sigmoid(ga_ref[...])
        sb = _sigmoid(gb_ref[...])
        dya_ref[...] = (dm * sa).astype(dya_ref.dtype)
        dyb_ref[...] = (dm * sb).astype(dyb_ref.dtype)
        dga_ref[...] = (dm * ya * sa * (1.0 - sa)).astype(dga_ref.dtype)
        dgb_ref[...] = (dm * yb * sb * (1.0 - sb)).astype(dgb_ref.dtype)

    o_spec = pl.BlockSpec((tm, kd), lambda i, j: (i, 0))
    w_spec = pl.BlockSpec((kd, tn), lambda i, j: (0, j))
    t_spec = pl.BlockSpec((tm, tn), lambda i, j: (i, j))
    return pl.pallas_call(
        body,
        grid=(s // tm, nj),
        in_specs=[pl.BlockSpec((tm, d), lambda i, j: (i, 0)),
                  pl.BlockSpec((tn, d), lambda i, j: (j, 0)),
                  o_spec, o_spec, w_spec, w_spec, t_spec,
                  pl.BlockSpec((tm, tn), lambda i, j: (i, j + nj))],
        out_specs=[t_spec] * 4,
        out_shape=[jax.ShapeDtypeStruct((s, d), MXU)] * 4,
        compiler_params=_cparams(),
        name=name,
    )(dx, w_o, oa, ob, wa, wb, gates, gates)


def _ple(p, w_pe, h, w_pg, other, *, backward, name):
    s, kp = p.shape
    d = w_pe.shape[1]
    tm, tn = _tile(s, 512), _tile(d, 512)

    def body(p_ref, wpe_ref, h_ref, wpg_ref, other_ref, *out_refs):
        pe = _dot(p_ref[...].astype(MXU), wpe_ref[...])
        gt = _dot(h_ref[...], wpg_ref[...])
        sg = _sigmoid(gt)
        if backward:
            dout = other_ref[...]
            out_refs[0][...] = (dout * sg).astype(out_refs[0].dtype)
            out_refs[1][...] = (dout * pe * sg * (1.0 - sg)).astype(out_refs[1].dtype)
        else:
            out_refs[0][...] = other_ref[...] + pe * sg

    t_spec = pl.BlockSpec((tm, tn), lambda i, j: (i, j))
    if backward:
        out_specs, out_shape = [t_spec, t_spec], [jax.ShapeDtypeStruct((s, d), MXU)] * 2
    else:
        out_specs, out_shape = [t_spec], [jax.ShapeDtypeStruct((s, d), F32)]
    outs = pl.pallas_call(
        body,
        grid=(s // tm, d // tn),
        in_specs=[pl.BlockSpec((tm, kp), lambda i, j: (i, 0)),
                  pl.BlockSpec((kp, tn), lambda i, j: (0, j)),
                  pl.BlockSpec((tm, d), lambda i, j: (i, 0)),
                  pl.BlockSpec((d, tn), lambda i, j: (0, j)),
                  t_spec],
        out_specs=out_specs,
        out_shape=out_shape,
        compiler_params=_cparams(),
        name=name,
    )(p, w_pe, h, w_pg, other)
    return tuple(outs) if backward else outs[0]


def _split_dot(x, tri):
    hi = x.astype(jnp.bfloat16)
    r1 = x - hi.astype(F32)
    mid = r1.astype(jnp.bfloat16)
    lo = (r1 - mid.astype(F32)).astype(jnp.bfloat16)
    return _dot(hi, tri) + _dot(mid, tri) + _dot(lo, tri)


def _log_sigmoids(z):
    t = jnp.log1p(jnp.exp(-jnp.abs(z)))
    return jnp.minimum(z, 0.0) - t, jnp.minimum(-z, 0.0) - t


def _sb_fwd(q, k, v, name):
    nh, s, dh = q.shape
    tq = _tile(s, SB_TQ)

    def body(q_ref, k_ref, v_ref, o_ref, l_ref):
        i = pl.program_id(1)
        qs = (q_ref[...].astype(F32) * SCALE).astype(MXU)
        row = lax.broadcasted_iota(jnp.int32, (tq, tq), 0)
        col = lax.broadcasted_iota(jnp.int32, (tq, tq), 1)
        causal = col < row
        tri = jnp.where(row > col, 1.0, 0.0).astype(jnp.bfloat16)

        def block(kb, c, acc, masked):
            rows = pl.ds(pl.multiple_of(kb * tq, tq), tq)
            ks = k_ref[rows, :]
            vs = v_ref[rows, :]
            z = _dot_nt(qs, ks)
            lb, lm = _log_sigmoids(z)
            if masked:
                lm = jnp.where(causal, lm, 0.0)
            a = jnp.exp(lb + _split_dot(lm, tri) + c)
            if masked:
                a = jnp.where(causal, a, 0.0)
            acc = acc + _dot(a.astype(MXU), vs)
            c = c + jnp.sum(lm, axis=1, keepdims=True)
            return c, acc

        c, acc = block(i, jnp.zeros((tq, 1), F32), jnp.zeros((tq, dh), F32), True)
        c, acc = lax.fori_loop(0, i, lambda n, cr: block(i - 1 - n, cr[0], cr[1], False), (c, acc))
        o_ref[...] = acc.astype(o_ref.dtype)
        l_ref[...] = c

    blk = pl.BlockSpec((None, tq, dh), lambda h, i: (h, i, 0))
    full = pl.BlockSpec((None, s, dh), lambda h, i: (h, 0, 0))
    return pl.pallas_call(
        body,
        grid=(nh, s // tq),
        in_specs=[blk, full, full],
        out_specs=[blk, pl.BlockSpec((None, tq, 1), lambda h, i: (h, i, 0))],
        out_shape=[jax.ShapeDtypeStruct((nh, s, dh), MXU), jax.ShapeDtypeStruct((nh, s, 1), F32)],
        compiler_params=_cparams(),
        name=name,
    )(q, k, v)


def _sb_bwd(q, k, v, do, lsum, name):
    nh, s, dh = q.shape
    tq = _tile(s, SB_TQ)
    nq = s // tq

    def body(q_ref, k_ref, v_ref, do_ref, l_ref, dq_ref, dk_ref, dv_ref, dk_acc, dv_acc):
        i = pl.program_id(1)

        @pl.when(i == 0)
        def _():
            dk_acc[...] = jnp.zeros_like(dk_acc)
            dv_acc[...] = jnp.zeros_like(dv_acc)

        qs = (q_ref[...].astype(F32) * SCALE).astype(MXU)
        dov = do_ref[...]
        ltot = l_ref[...]
        row = lax.broadcasted_iota(jnp.int32, (tq, tq), 0)
        col = lax.broadcasted_iota(jnp.int32, (tq, tq), 1)
        causal = col < row
        tri_incl = jnp.where(row <= col, 1.0, 0.0).astype(jnp.bfloat16)
        tri_excl = jnp.where(row < col, 1.0, 0.0).astype(jnp.bfloat16)

        def block(kb, pre, gpre, dq, masked):
            rows = pl.ds(pl.multiple_of(kb * tq, tq), tq)
            ks = k_ref[rows, :]
            vs = v_ref[rows, :]
            z = _dot_nt(qs, ks)
            lb, lm = _log_sigmoids(z)
            if masked:
                lm = jnp.where(causal, lm, 0.0)
            a = jnp.exp(lb + (ltot - (pre + _split_dot(lm, tri_incl))))
            if masked:
                a = jnp.where(causal, a, 0.0)
            g = a * _dot_nt(dov, vs)
            gsum = gpre + _split_dot(g, tri_excl)
            dz = g - (g + gsum) * jnp.exp(lb)
            if masked:
                dz = jnp.where(causal, dz, 0.0)
            dzb = dz.astype(MXU)
            dq = dq + _dot(dzb, ks)
            dk_acc[rows, :] += _dot_tn(dzb, qs)
            dv_acc[rows, :] += _dot_tn(a.astype(MXU), dov)
            pre = pre + jnp.sum(lm, axis=1, keepdims=True)
            gpre = gpre + jnp.sum(g, axis=1, keepdims=True)
            return pre, gpre, dq

        zero = jnp.zeros((tq, 1), F32)
        carry = lax.fori_loop(0, i, lambda kb, cr: block(kb, cr[0], cr[1], cr[2], False),
                              (zero, zero, jnp.zeros((tq, dh), F32)))
        _, _, dq = block(i, carry[0], carry[1], carry[2], True)
        dq_ref[...] = (dq * SCALE).astype(dq_ref.dtype)

        @pl.when(i == nq - 1)
        def _():
            dk_ref[...] = dk_acc[...].astype(dk_ref.dtype)
            dv_ref[...] = dv_acc[...].astype(dv_ref.dtype)

    blk = pl.BlockSpec((None, tq, dh), lambda h, i: (h, i, 0))
    full = pl.BlockSpec((None, s, dh), lambda h, i: (h, 0, 0))
    return pl.pallas_call(
        body,
        grid=(nh, nq),
        in_specs=[blk, full, full, blk, pl.BlockSpec((None, tq, 1), lambda h, i: (h, i, 0))],
        out_specs=[blk, full, full],
        out_shape=[jax.ShapeDtypeStruct((nh, s, dh), MXU)] * 3,
        scratch_shapes=[pltpu.VMEM((s, dh), F32), pltpu.VMEM((s, dh), F32)],
        compiler_params=_cparams(),
        name=name,
    )(q, k, v, do, lsum)


def _bucket_table():
    i = np.arange(BLOCK)[:, None]
    j = np.arange(2 * BLOCK)[None, :]
    d = np.maximum(BLOCK + i - j, 0)
    max_exact = N_BUCKETS // 2
    df = np.maximum(d, 1).astype(np.float32)
    large = max_exact + (np.log(df / max_exact) / math.log(MAX_DISTANCE / max_exact)
                         * (N_BUCKETS - max_exact)).astype(np.int32)
    large = np.minimum(large, N_BUCKETS - 1)
    return np.where(d < max_exact, d, large).astype(np.int32)


def _build_bias(rel_bias, buckets, name):
    def body(rb_ref, bk_ref, out_ref):
        h = pl.program_id(0)
        bk = bk_ref[...]
        acc = jnp.zeros(bk.shape, F32)
        for b in range(N_BUCKETS):
            acc = jnp.where(bk == b, rb_ref[b, h], acc)
        out_ref[...] = acc

    return pl.pallas_call(
        body,
        grid=(SW_HEADS,),
        in_specs=[pl.BlockSpec(memory_space=pltpu.SMEM),
                  pl.BlockSpec((BLOCK, 2 * BLOCK), lambda h: (0, 0))],
        out_specs=pl.BlockSpec((None, BLOCK, 2 * BLOCK), lambda h: (h, 0, 0)),
        out_shape=jax.ShapeDtypeStruct((SW_HEADS, BLOCK, 2 * BLOCK), F32),
        name=name,
    )(rel_bias, buckets)


def _bias_grad(dbias, buckets, name):
    def body(db_ref, bk_ref, out_ref):
        bk = bk_ref[...]
        db = db_ref[...]
        lane = lax.broadcasted_iota(jnp.int32, (1, LANES), 1)
        acc = jnp.zeros((1, LANES), F32)
        for b in range(N_BUCKETS):
            part = jnp.sum(jnp.where(bk == b, db, 0.0), axis=1, keepdims=True)
            tot = jnp.sum(part, axis=0, keepdims=True)
            acc = jnp.where(lane == b, tot, acc)
        out_ref[...] = acc

    return pl.pallas_call(
        body,
        grid=(SW_HEADS,),
        in_specs=[pl.BlockSpec((None, BLOCK, 2 * BLOCK), lambda h: (h, 0, 0)),
                  pl.BlockSpec((BLOCK, 2 * BLOCK), lambda h: (0, 0))],
        out_specs=pl.BlockSpec((None, 1, LANES), lambda h: (h, 0, 0)),
        out_shape=jax.ShapeDtypeStruct((SW_HEADS, 1, LANES), F32),
        name=name,
    )(dbias, buckets)


def _swa_scores(qb, kp, kc, bias_ref, n):
    row = lax.broadcasted_iota(jnp.int32, (BLOCK, BLOCK), 0)
    col = lax.broadcasted_iota(jnp.int32, (BLOCK, BLOCK), 1)
    s1 = _dot_nt(qb, kp) + bias_ref[:, :BLOCK]
    s2 = _dot_nt(qb, kc) + bias_ref[:, BLOCK:]
    no_prev = jnp.where(n > 0, 0, BLOCK)
    s1 = jnp.where(col > row + no_prev, s1, NEG)
    s2 = jnp.where(col <= row, s2, NEG)
    return s1, s2


def _swa_fwd(q, k, v, bias, sinks, name):
    nh, s, dh = q.shape
    nb = s // BLOCK

    def body(sink_ref, q_ref, k_ref, v_ref, bias_ref, o_ref, lse_ref):
        sink = sink_ref[pl.program_id(0)]

        def step(n, carry):
            r0 = pl.multiple_of(n * BLOCK, BLOCK)
            p0 = pl.multiple_of(jnp.maximum(n - 1, 0) * BLOCK, BLOCK)
            qb = (q_ref[pl.ds(r0, BLOCK), :].astype(F32) * SCALE).astype(MXU)
            s1, s2 = _swa_scores(qb, k_ref[pl.ds(p0, BLOCK), :], k_ref[pl.ds(r0, BLOCK), :], bias_ref, n)
            m = jnp.maximum(jnp.maximum(jnp.max(s1, axis=1, keepdims=True),
                                        jnp.max(s2, axis=1, keepdims=True)), sink)
            e1 = jnp.exp(s1 - m)
            e2 = jnp.exp(s2 - m)
            den = jnp.sum(e1, axis=1, keepdims=True) + jnp.sum(e2, axis=1, keepdims=True) + jnp.exp(sink - m)
            o = (_dot((e1 / den).astype(MXU), v_ref[pl.ds(p0, BLOCK), :])
                 + _dot((e2 / den).astype(MXU), v_ref[pl.ds(r0, BLOCK), :]))
            o_ref[pl.ds(r0, BLOCK), :] = o.astype(o_ref.dtype)
            lse_ref[pl.ds(r0, BLOCK), :] = m + jnp.log(den)
            return carry

        lax.fori_loop(0, nb, step, 0)

    qspec = pl.BlockSpec((None, s, dh), lambda h: (h, 0, 0))
    kvspec = pl.BlockSpec((None, s, dh), lambda h: (h // SW_GROUP, 0, 0))
    return pl.pallas_call(
        body,
        grid=(nh,),
        in_specs=[pl.BlockSpec(memory_space=pltpu.SMEM), qspec, kvspec, kvspec,
                  pl.BlockSpec((None, BLOCK, 2 * BLOCK), lambda h: (h, 0, 0))],
        out_specs=[qspec, pl.BlockSpec((None, s, 1), lambda h: (h, 0, 0))],
        out_shape=[jax.ShapeDtypeStruct((nh, s, dh), MXU), jax.ShapeDtypeStruct((nh, s, 1), F32)],
        compiler_params=_cparams(),
        name=name,
    )(sinks, q, k, v, bias)


def _swa_bwd(q, k, v, bias, sinks, do, lse, name):
    nh, s, dh = q.shape
    nkv = k.shape[0]
    nb = s // BLOCK

    def body(sink_ref, q_ref, k_ref, v_ref, bias_ref, do_ref, lse_ref,
             dq_ref, dk_ref, dv_ref, dbias_ref, dsink_ref, dk_acc, dv_acc):
        h = pl.program_id(0)
        sink = sink_ref[h]

        @pl.when(h % SW_GROUP == 0)
        def _():
            dk_acc[...] = jnp.zeros_like(dk_acc)
            dv_acc[...] = jnp.zeros_like(dv_acc)

        dbias_ref[...] = jnp.zeros_like(dbias_ref)

        def step(n, dsink_rows):
            r0 = pl.multiple_of(n * BLOCK, BLOCK)
            p0 = pl.multiple_of(jnp.maximum(n - 1, 0) * BLOCK, BLOCK)
            cur, prev = pl.ds(r0, BLOCK), pl.ds(p0, BLOCK)
            qb = (q_ref[cur, :].astype(F32) * SCALE).astype(MXU)
            kp, kc, vp, vc = k_ref[prev, :], k_ref[cur, :], v_ref[prev, :], v_ref[cur, :]
            dob = do_ref[cur, :]
            lse_b = lse_ref[cur, :]
            s1, s2 = _swa_scores(qb, kp, kc, bias_ref, n)
            pr1 = jnp.exp(s1 - lse_b)
            pr2 = jnp.exp(s2 - lse_b)
            dpr1 = _dot_nt(dob, vp)
            dpr2 = _dot_nt(dob, vc)
            delta = (jnp.sum(pr1 * dpr1, axis=1, keepdims=True)
                     + jnp.sum(pr2 * dpr2, axis=1, keepdims=True))
            ds1 = pr1 * (dpr1 - delta)
            ds2 = pr2 * (dpr2 - delta)
            dbias_ref[:, :BLOCK] += ds1
            dbias_ref[:, BLOCK:] += ds2
            ds1b, ds2b = ds1.astype(MXU), ds2.astype(MXU)
            dq_ref[cur, :] = ((_dot(ds1b, kp) + _dot(ds2b, kc)) * SCALE).astype(dq_ref.dtype)
            dk_acc[prev, :] += _dot_tn(ds1b, qb)
            dk_acc[cur, :] += _dot_tn(ds2b, qb)
            dv_acc[prev, :] += _dot_tn(pr1.astype(MXU), dob)
            dv_acc[cur, :] += _dot_tn(pr2.astype(MXU), dob)
            return dsink_rows - jnp.exp(sink - lse_b) * delta

        dsink_rows = lax.fori_loop(0, nb, step, jnp.zeros((BLOCK, 1), F32))
        dsink_ref[...] = jnp.broadcast_to(jnp.sum(dsink_rows, axis=0, keepdims=True), (1, LANES))

        @pl.when(h % SW_GROUP == SW_GROUP - 1)
        def _():
            dk_ref[...] = dk_acc[...].astype(dk_ref.dtype)
            dv_ref[...] = dv_acc[...].astype(dv_ref.dtype)

    qspec = pl.BlockSpec((None, s, dh), lambda h: (h, 0, 0))
    kvspec = pl.BlockSpec((None, s, dh), lambda h: (h // SW_GROUP, 0, 0))
    bspec = pl.BlockSpec((None, BLOCK, 2 * BLOCK), lambda h: (h, 0, 0))
    return pl.pallas_call(
        body,
        grid=(nh,),
        in_specs=[pl.BlockSpec(memory_space=pltpu.SMEM), qspec, kvspec, kvspec, bspec, qspec,
                  pl.BlockSpec((None, s, 1), lambda h: (h, 0, 0))],
        out_specs=[qspec, kvspec, kvspec, bspec, pl.BlockSpec((None, 1, LANES), lambda h: (h, 0, 0))],
        out_shape=[jax.ShapeDtypeStruct((nh, s, dh), MXU),
                   jax.ShapeDtypeStruct((nkv, s, dh), MXU),
                   jax.ShapeDtypeStruct((nkv, s, dh), MXU),
                   jax.ShapeDtypeStruct((nh, BLOCK, 2 * BLOCK), F32),
                   jax.ShapeDtypeStruct((nh, 1, LANES), F32)],
        scratch_shapes=[pltpu.VMEM((s, dh), F32), pltpu.VMEM((s, dh), F32)],
        compiler_params=_cparams(),
        name=name,
    )(sinks, q, k, v, bias, do, lse)


def _to_heads(a):
    s, w = a.shape
    return a.reshape(s, w // HEAD_DIM, HEAD_DIM).transpose(1, 0, 2)


def _from_heads(a):
    nh, s, dh = a.shape
    return a.transpose(1, 0, 2).reshape(s, nh * dh)


def _layer_fwd(x, p, w, g_mix, g_mlp, g_pe, sinks, bias, tag):
    h1, r1 = _rms_fwd(x, g_mix, f"rms_mix_{tag}")
    qkv = _mm(h1, w["w_qkv"], out_dtypes=(MXU,), tn=768, name=f"proj_qkv_{tag}")
    gates = _mm(h1, w["w_gate"], name=f"proj_gate_{tag}")
    heads = _to_heads(qkv)
    qa, ka, va = heads[0:8], heads[8:16], heads[16:24]
    qb, kb, vb = heads[24:32], heads[32:34], heads[34:36]
    oa_h, lsum = _sb_fwd(qa, ka, va, f"sb_fwd_{tag}")
    ob_h, lse = _swa_fwd(qb, kb, vb, bias, sinks, f"swa_fwd_{tag}")
    oa, ob = _from_heads(oa_h), _from_heads(ob_h)
    merged = _mix_fwd(oa, ob, w["w_up_a"], w["w_up_b"], gates, f"mix_fwd_{tag}")
    x1 = _mm(merged, w["w_o"], extras=(x,), epi=lambda acc, res: res + acc, name=f"out_proj_{tag}")
    h2, r2 = _rms_fwd(x1, g_mlp, f"rms_mlp_{tag}")
    u, act = _mm(h2, w["w_ff1"], epi=lambda acc: (acc, jnp.square(jnp.maximum(acc, 0.0))),
                 out_dtypes=(F32, MXU), name=f"ff1_{tag}")
    x2 = _mm(act, w["w_ff2"], extras=(x1,), epi=lambda acc, res: res + acc, name=f"ff2_{tag}")
    h3, r3 = _rms_fwd(x2, g_pe, f"rms_pe_{tag}")
    x3 = _ple(p, w["w_pe"], h3, w["w_pg"], x2, backward=False, name=f"ple_fwd_{tag}")
    saved = dict(x=x, h1=h1, r1=r1, gates=gates, qa=qa, ka=ka, va=va, qb=qb, kb=kb, vb=vb,
                 lsum=lsum, lse=lse, oa=oa, ob=ob, merged=merged, x1=x1, h2=h2, r2=r2, u=u, act=act,
                 x2=x2, h3=h3, r3=r3)
    return x3, saved


def _layer_bwd(dx3, sv, p, w, g_mix, g_mlp, g_pe, sinks, bias, tag):
    gw = {}
    dpe, dgt = _ple(p, w["w_pe"], sv["h3"], w["w_pg"], dx3, backward=True, name=f"ple_bwd_{tag}")
    gw["w_pe"] = _mm(p, dpe, ta=True, name=f"dw_pe_{tag}")
    gw["w_pg"] = _mm(sv["h3"], dgt, ta=True, name=f"dw_pg_{tag}")
    dh3 = _mm(dgt, w["w_pg"], tb=True, name=f"dh_pe_{tag}")
    dx2, dx2b, dg_pe = _rms_bwd(sv["x2"], sv["r3"], g_pe, dh3, dx3, f"rms_pe_bwd_{tag}")
    gw["w_ff2"] = _mm(sv["act"], dx2b, ta=True, name=f"dw_ff2_{tag}")
    du = _mm(dx2b, w["w_ff2"], tb=True, extras=(sv["u"],),
             epi=lambda acc, u: acc * (2.0 * jnp.maximum(u, 0.0)), out_dtypes=(MXU,), name=f"dact_{tag}")
    gw["w_ff1"] = _mm(sv["h2"], du, ta=True, name=f"dw_ff1_{tag}")
    dh2 = _mm(du, w["w_ff1"], tb=True, name=f"dh_mlp_{tag}")
    dx1, dx1b, dg_mlp = _rms_bwd(sv["x1"], sv["r2"], g_mlp, dh2, dx2, f"rms_mlp_bwd_{tag}")
    gw["w_o"] = _mm(sv["merged"], dx1b, ta=True, name=f"dw_o_{tag}")
    dya, dyb, dga, dgb = _mix_bwd(dx1b, w["w_o"], sv["oa"], sv["ob"], w["w_up_a"], w["w_up_b"],
                                  sv["gates"], f"mix_bwd_{tag}")
    gw["w_up_a"] = _mm(sv["oa"], dya, ta=True, name=f"dw_up_a_{tag}")
    gw["w_up_b"] = _mm(sv["ob"], dyb, ta=True, name=f"dw_up_b_{tag}")
    doa = _mm(dya, w["w_up_a"], tb=True, out_dtypes=(MXU,), name=f"do_a_{tag}")
    dob = _mm(dyb, w["w_up_b"], tb=True, out_dtypes=(MXU,), name=f"do_b_{tag}")
    dqb, dkb, dvb, dbias, dsink = _swa_bwd(sv["qb"], sv["kb"], sv["vb"], bias, sinks, _to_heads(dob),
                                           sv["lse"], f"swa_bwd_{tag}")
    dqa, dka, dva = _sb_bwd(sv["qa"], sv["ka"], sv["va"], _to_heads(doa), sv["lsum"], f"sb_bwd_{tag}")
    dqkv = _from_heads(jnp.concatenate([dqa, dka, dva, dqb, dkb, dvb], axis=0))
    gw_qkv = _mm(sv["h1"], dqkv, ta=True, tn=768, name=f"dw_qkv_{tag}")
    gw_ga = _mm(sv["h1"], dga, ta=True, name=f"dw_ga_{tag}")
    gw_gb = _mm(sv["h1"], dgb, ta=True, name=f"dw_gb_{tag}")
    gw["w_in"] = jnp.concatenate([gw_qkv, gw_ga, gw_gb], axis=1)
    d = dga.shape[1]
    add = lambda acc, res: res + acc
    dh1 = _mm(dqkv, w["w_qkv"], tb=True, tk=768, name=f"dh_qkv_{tag}")
    dh1 = _mm(dga, w["w_gate"][:, :d], tb=True, extras=(dh1,), epi=add, name=f"dh_ga_{tag}")
    dh1 = _mm(dgb, w["w_gate"][:, d:], tb=True, extras=(dh1,), epi=add, name=f"dh_gb_{tag}")
    dx, _, dg_mix = _rms_bwd(sv["x"], sv["r1"], g_mix, dh1, dx1, f"rms_mix_bwd_{tag}")
    small = dict(g_mix=dg_mix, g_mlp=dg_mlp, g_pe=dg_pe, sinks=dsink[:, 0, 0], dbias=dbias)
    return dx, gw, small


def _local_step(x, p, target, wfull, g_mix, g_mlp, g_pe, g_final, sinks, rel_bias):
    depth = len(wfull)
    buckets = jnp.asarray(_bucket_table())
    bias = _build_bias(rel_bias, buckets, "build_bias")
    saved = []
    h = x
    for l in range(depth):
        h, sv = _layer_fwd(h, p[l], wfull[l], g_mix[l:l + 1], g_mlp[l:l + 1], g_pe[l:l + 1],
                           sinks[l], bias, f"l{l}")
        saved.append(sv)
    loss_row, dx, dg_final = _loss_head(h, g_final[None, :], target, "loss_head")
    gws = [None] * depth
    smalls = [None] * depth
    for l in reversed(range(depth)):
        dx, gws[l], smalls[l] = _layer_bwd(dx, saved[l], p[l], wfull[l], g_mix[l:l + 1], g_mlp[l:l + 1],
                                           g_pe[l:l + 1], sinks[l], bias, f"l{l}")
    dbias = smalls[0]["dbias"]
    for l in range(1, depth):
        dbias = dbias + smalls[l]["dbias"]
    drel = _bias_grad(dbias, buckets, "bias_grad")[:, 0, :N_BUCKETS].T
    small = dict(
        g_mix=jnp.concatenate([sm["g_mix"] for sm in smalls], axis=0),
        g_mlp=jnp.concatenate([sm["g_mlp"] for sm in smalls], axis=0),
        g_pe=jnp.concatenate([sm["g_pe"] for sm in smalls], axis=0),
        g_final=dg_final[0],
        sinks=jnp.stack([sm["sinks"] for sm in smalls], axis=0),
        rel_bias=drel,
    )
    return loss_row, dx, gws, small


MESH_ID = pl.DeviceIdType.MESH
ANY = pl.BlockSpec(memory_space=pl.ANY)


def _position():
    return lax.axis_index("x"), lax.axis_index("y"), lax.axis_index("c")


def _all_gather(shard, name):
    def body(x_ref, out_ref, send_sems, recv_sems, local_sem):
        x, y, c = _position()
        me, sibling = (x, y, c), (x, y, 1 - c)
        chips = [(1 - x, y), (x, 1 - y), (1 - x, 1 - y)]

        def slot(px, py, pc):
            return out_ref.at[4 * px + 2 * py + pc]

        def copy(k, block, to, src=None):
            return pltpu.make_async_remote_copy(
                src_ref=slot(*block) if src is None else src, dst_ref=slot(*block),
                send_sem=send_sems.at[k], recv_sem=recv_sems.at[k],
                device_id=to, device_id_type=MESH_ID)

        mine = pltpu.make_async_copy(x_ref, slot(*me), local_sem)
        mine.start()
        first = [copy(0, me, sibling, src=x_ref)]
        first += [copy(1 + j, me, (*chip, c), src=x_ref) for j, chip in enumerate(chips)]
        for cp in first:
            cp.start()
        passed = [copy(4 + j, (*chip, c), sibling) for j, chip in enumerate(chips)]
        for j, chip in enumerate(chips):
            copy(1 + j, (*chip, c), me).wait_recv()
            passed[j].start()
        copy(0, sibling, me).wait_recv()
        for j, chip in enumerate(chips):
            copy(4 + j, (*chip, 1 - c), me).wait_recv()
        for cp in first + passed:
            cp.wait_send()
        mine.wait()

    return pl.pallas_call(
        body,
        out_shape=jax.ShapeDtypeStruct((N_DEV,) + shard.shape, shard.dtype),
        in_specs=[ANY],
        out_specs=ANY,
        scratch_shapes=[pltpu.SemaphoreType.DMA((7,)), pltpu.SemaphoreType.DMA((7,)),
                        pltpu.SemaphoreType.DMA(())],
        name=name,
    )(shard)


def _rs_sibling(g, name):
    _, r, lanes = g.shape

    def body(g_ref, out_ref, send_sems, recv_sems):
        x, y, c = _position()
        copies = []
        for j in range(4):
            copies.append(pltpu.make_async_remote_copy(
                src_ref=g_ref.at[2 * j + (1 - c)], dst_ref=out_ref.at[j],
                send_sem=send_sems.at[j], recv_sem=recv_sems.at[j],
                device_id=(x, y, 1 - c), device_id_type=MESH_ID))
        for cp in copies:
            cp.start()
        for cp in copies:
            cp.wait()

    return pl.pallas_call(
        body,
        out_shape=jax.ShapeDtypeStruct((4, r, lanes), g.dtype),
        in_specs=[ANY],
        out_specs=ANY,
        scratch_shapes=[pltpu.SemaphoreType.DMA((4,)), pltpu.SemaphoreType.DMA((4,))],
        name=name,
    )(g)


def _chip_of(k, x, y):
    return x ^ ((k + 1) & 1), y ^ (((k + 1) >> 1) & 1)


def _chip_partials(pos, g, recv, name):
    _, r, lanes = g.shape
    tr = _tile(r, 1024)

    def body(pos_ref, g_ref, r_ref, out_ref):
        out_ref[...] = (g_ref[...].astype(F32) + r_ref[...].astype(F32)).astype(out_ref.dtype)

    def g_map(k, i, pos_ref):
        cx, cy = _chip_of(k, pos_ref[0], pos_ref[1])
        return (4 * cx + 2 * cy + pos_ref[2], i, 0)

    def r_map(k, i, pos_ref):
        cx, cy = _chip_of(k, pos_ref[0], pos_ref[1])
        return (2 * cx + cy, i, 0)

    return pl.pallas_call(
        body,
        grid_spec=pltpu.PrefetchScalarGridSpec(
            num_scalar_prefetch=1,
            grid=(4, r // tr),
            in_specs=[pl.BlockSpec((None, tr, lanes), g_map), pl.BlockSpec((None, tr, lanes), r_map)],
            out_specs=pl.BlockSpec((None, tr, lanes), lambda k, i, pos_ref: (k, i, 0)),
        ),
        out_shape=jax.ShapeDtypeStruct((4, r, lanes), g.dtype),
        compiler_params=_cparams(),
        name=name,
    )(pos, g, recv)


def _rs_chips(part, name):
    _, r, lanes = part.shape

    def body(p_ref, out_ref, send_sems, recv_sems):
        x, y, c = _position()
        copies = []
        for k in range(3):
            cx, cy = _chip_of(k, x, y)
            copies.append(pltpu.make_async_remote_copy(
                src_ref=p_ref.at[k], dst_ref=out_ref.at[k],
                send_sem=send_sems.at[k], recv_sem=recv_sems.at[k],
                device_id=(cx, cy, c), device_id_type=MESH_ID))
        for cp in copies:
            cp.start()
        for cp in copies:
            cp.wait()

    return pl.pallas_call(
        body,
        out_shape=jax.ShapeDtypeStruct((3, r, lanes), part.dtype),
        in_specs=[ANY],
        out_specs=ANY,
        scratch_shapes=[pltpu.SemaphoreType.DMA((3,)), pltpu.SemaphoreType.DMA((3,))],
        name=name,
    )(part)


def _adamw_math(w, g, m, v):
    m = ADAM_B1 * m + (1.0 - ADAM_B1) * g
    v = ADAM_B2 * v + (1.0 - ADAM_B2) * (g * g)
    m_hat = m / (1.0 - ADAM_B1 ** ADAM_STEP)
    v_hat = v / (1.0 - ADAM_B2 ** ADAM_STEP)
    delta = -ADAM_LR * (m_hat / (jnp.sqrt(v_hat) + ADAM_EPS) + ADAM_WD * w)
    return delta, m, v


def _adamw_sharded(part, recv, w, m, v, name):
    r, lanes = w.shape
    tr = _tile(r, 1024)

    def body(p_ref, r_ref, w_ref, m_ref, v_ref, g_out, d_out, m_out, v_out):
        g = p_ref[...].astype(F32)
        for k in range(3):
            g = g + r_ref[k].astype(F32)
        delta, m_new, v_new = _adamw_math(w_ref[...], g, m_ref[...], v_ref[...])
        g_out[...] = g
        d_out[...] = delta
        m_out[...] = m_new
        v_out[...] = v_new

    row = pl.BlockSpec((tr, lanes), lambda i: (i, 0))
    return pl.pallas_call(
        body,
        grid=(r // tr,),
        in_specs=[pl.BlockSpec((None, tr, lanes), lambda i: (3, i, 0)),
                  pl.BlockSpec((3, tr, lanes), lambda i: (0, i, 0)), row, row, row],
        out_specs=[row] * 4,
        out_shape=[jax.ShapeDtypeStruct((r, lanes), F32)] * 4,
        compiler_params=_cparams(),
        name=name,
    )(part, recv, w, m, v)


def _adamw_replicated(gathered, w, m, v, name):
    r, lanes = w.shape

    def body(g_ref, w_ref, m_ref, v_ref, g_out, d_out, m_out, v_out):
        g = g_ref[0]
        for k in range(1, N_DEV):
            g = g + g_ref[k]
        delta, m_new, v_new = _adamw_math(w_ref[...], g, m_ref[...], v_ref[...])
        g_out[...] = g
        d_out[...] = delta
        m_out[...] = m_new
        v_out[...] = v_new

    return pl.pallas_call(
        body,
        out_shape=[jax.ShapeDtypeStruct((r, lanes), F32)] * 4,
        name=name,
    )(gathered, w, m, v)


def _pack_shards(arrs, dtype):
    return jnp.concatenate([a.astype(dtype).reshape(-1, LANES) for a in arrs], axis=0)


def _unpack_shards(packed, shapes):
    out, off = [], 0
    for shp in shapes:
        rows = math.prod(shp) // LANES
        out.append(packed[off:off + rows].reshape(shp))
        off += rows
    return out


def _unpack_gathered(gathered, shapes):
    depth = shapes[0][0]
    full = [dict() for _ in range(depth)]
    off = 0
    for name, shp in zip(WEIGHTS, shapes):
        rows = math.prod(shp) // LANES
        blk = gathered[:, off:off + rows].reshape((N_DEV,) + shp)
        off += rows
        _, a, b = shp
        for l in range(depth):
            if name in COL_SHARDED:
                full[l][name] = blk[:, l].transpose(1, 0, 2).reshape(a, N_DEV * b)
            else:
                full[l][name] = blk[:, l].reshape(N_DEV * a, b)
    return full


def _pack_full_grads(gws, shapes, dtype):
    depth = len(gws)
    parts = []
    for name, shp in zip(WEIGHTS, shapes):
        _, a, b = shp
        per_layer = []
        for l in range(depth):
            gfull = gws[l][name].astype(dtype)
            if name in COL_SHARDED:
                per_layer.append(gfull.reshape(a, N_DEV, b).transpose(1, 0, 2))
            else:
                per_layer.append(gfull.reshape(N_DEV, a, b))
        parts.append(jnp.stack(per_layer, axis=1).reshape(N_DEV, -1, LANES))
    return jnp.concatenate(parts, axis=1)


def _pack_small(arrs):
    rows = []
    for a in arrs:
        flat = a.astype(F32).reshape(-1)
        pad = (-flat.shape[0]) % LANES
        rows.append(jnp.pad(flat, (0, pad)).reshape(-1, LANES))
    packed = jnp.concatenate(rows, axis=0)
    return jnp.pad(packed, ((0, (-packed.shape[0]) % 8), (0, 0)))


def _unpack_small(packed, shapes):
    out, off = [], 0
    for shp in shapes:
        n = math.prod(shp)
        rows = -(-n // LANES)
        out.append(packed[off:off + rows].reshape(-1)[:n].reshape(shp))
        off += rows
    return out


def kernel(x, p, w_in, w_up_a, w_up_b, w_o, w_ff1, w_ff2, w_pe, w_pg, g_mix, g_mlp, g_pe, g_final, sinks, rel_bias, loss_target, m_w_in, m_w_up_a, m_w_up_b, m_w_o, m_w_ff1, m_w_ff2, m_w_pe, m_w_pg, m_g_mix, m_g_mlp, m_g_pe, m_g_final, m_sinks, m_rel_bias, v_w_in, v_w_up_a, v_w_up_b, v_w_o, v_w_ff1, v_w_ff2, v_w_pe, v_w_pg, v_g_mix, v_g_mlp, v_g_pe, v_g_final, v_sinks, v_rel_bias):
    w_sh = [w_in, w_up_a, w_up_b, w_o, w_ff1, w_ff2, w_pe, w_pg]
    m_sh = [m_w_in, m_w_up_a, m_w_up_b, m_w_o, m_w_ff1, m_w_ff2, m_w_pe, m_w_pg]
    v_sh = [v_w_in, v_w_up_a, v_w_up_b, v_w_o, v_w_ff1, v_w_ff2, v_w_pe, v_w_pg]
    shapes = [a.shape for a in w_sh]
    d_model = x.shape[-1]

    gathered = _all_gather(_pack_shards(w_sh, WIRE), "gather_weights")
    wfull = _unpack_gathered(gathered, shapes)
    for wl in wfull:
        w_all = wl.pop("w_in")
        wl["w_qkv"] = w_all[:, :QKV_COLS]
        wl["w_gate"] = w_all[:, QKV_COLS:]
        assert wl["w_gate"].shape[1] == 2 * d_model

    loss_row, grad_x, gws, small = _local_step(
        x[0], p[:, 0], loss_target[0], wfull, g_mix, g_mlp, g_pe, g_final, sinks, rel_bias)

    px, py, pc = _position()
    pos = jnp.stack([px, py, pc]).astype(jnp.int32)
    g_by_target = _pack_full_grads(gws, shapes, WIRE)
    from_sibling = _rs_sibling(g_by_target, "reduce_sibling")
    partials = _chip_partials(pos, g_by_target, from_sibling, "chip_partials")
    from_chips = _rs_chips(partials, "reduce_chips")

    packed = _adamw_sharded(partials, from_chips, _pack_shards(w_sh, F32), _pack_shards(m_sh, F32),
                            _pack_shards(v_sh, F32), "adamw_sharded")
    grad_w, delta_w, new_m, new_v = [_unpack_shards(t, shapes) for t in packed]

    small_w = [g_mix, g_mlp, g_pe, g_final, sinks, rel_bias]
    small_m = [m_g_mix, m_g_mlp, m_g_pe, m_g_final, m_sinks, m_rel_bias]
    small_v = [v_g_mix, v_g_mlp, v_g_pe, v_g_final, v_sinks, v_rel_bias]
    small_shapes = [a.shape for a in small_w] + [(1,)]
    zero = jnp.zeros((1,), F32)
    small_g = _pack_small([small[n] for n in SMALL] + [loss_row[0, :1]])
    small_all = _all_gather(small_g, "gather_small")
    packed_s = _adamw_replicated(small_all, _pack_small(small_w + [zero]), _pack_small(small_m + [zero]),
                                 _pack_small(small_v + [zero + 1.0]), "adamw_replicated")
    sg, sd, sm, sv = [_unpack_small(t, small_shapes) for t in packed_s]
    loss = sg[-1][0]

    return (loss, grad_x[None], *grad_w, *sg[:-1], *delta_w, *sd[:-1], *new_m, *sm[:-1], *new_v, *sv[:-1])
```

```python
import functools
import math

import numpy as np
import jax
import jax.numpy as jnp
from jax import lax
from jax.experimental import pallas as pl
from jax.experimental.pallas import tpu as pltpu

F32 = jnp.float32
MXU = jnp.bfloat16
WIRE = jnp.bfloat16

HEAD_DIM = 64
SB_HEADS = 8
SW_HEADS = 8
SW_KV = 2
SW_GROUP = SW_HEADS // SW_KV
BLOCK = 128
N_BUCKETS = 32
MAX_DISTANCE = 128
EPS = 1e-6
SCALE = HEAD_DIM ** -0.5
SB_W = SB_HEADS * HEAD_DIM
QKV_COLS = 3 * SB_W + SW_HEADS * HEAD_DIM + 2 * SW_KV * HEAD_DIM
N_QKV_HEADS = QKV_COLS // HEAD_DIM
N_DEV = 8
LANES = 128
NEG = -1e30

ADAM_LR = 0.001
ADAM_B1 = 0.9
ADAM_B2 = 0.999
ADAM_EPS = 1e-08
ADAM_WD = 0.01
ADAM_STEP = 10

VMEM_LIMIT = 48 * 1024 * 1024
SB_TQ = 256
SB_DEAD = -105.0

WEIGHTS = ("w_in", "w_up_a", "w_up_b", "w_o", "w_ff1", "w_ff2", "w_pe", "w_pg")
COL_SHARDED = ("w_in", "w_up_a", "w_up_b", "w_ff1", "w_pe")
SMALL = ("g_mix", "g_mlp", "g_pe", "g_final", "sinks", "rel_bias")


def _cparams(**kw):
    return pltpu.CompilerParams(vmem_limit_bytes=VMEM_LIMIT, **kw)


def _dot(a, b):
    return jnp.dot(a, b, preferred_element_type=F32)


def _dot_nt(a, b):
    return lax.dot_general(a, b, (((1,), (1,)), ((), ())), preferred_element_type=F32)


def _dot_tn(a, b):
    return lax.dot_general(a, b, (((0,), (0,)), ((), ())), preferred_element_type=F32)


def _tile(n, target):
    if n <= target:
        return n
    t = (target // LANES) * LANES
    while t > LANES and n % t:
        t -= LANES
    assert n % t == 0, (n, target)
    return t


def _sigmoid(x):
    return 1.0 / (1.0 + jnp.exp(-x))


def _mm(a, b, *, ta=False, tb=False, extras=(), epi=None, out_dtypes=(F32,),
        tm=512, tn=512, tk=1024, name):
    if ta:
        kdim, m = a.shape
    else:
        m, kdim = a.shape
    n = b.shape[0] if tb else b.shape[1]
    assert (b.shape[1] if tb else b.shape[0]) == kdim
    tm, tn, tk = _tile(m, tm), _tile(n, tn), _tile(kdim, tk)
    nk = kdim // tk
    n_ex, n_out = len(extras), len(out_dtypes)

    a_spec = (pl.BlockSpec((tk, tm), lambda i, j, k: (k, i)) if ta
              else pl.BlockSpec((tm, tk), lambda i, j, k: (i, k)))
    b_spec = (pl.BlockSpec((tn, tk), lambda i, j, k: (j, k)) if tb
              else pl.BlockSpec((tk, tn), lambda i, j, k: (k, j)))
    ex_specs = []
    for e in extras:
        assert e.shape == (m, n), (e.shape, m, n)
        ex_specs.append(pl.BlockSpec((tm, tn), lambda i, j, k: (i, j)))
    out_spec = pl.BlockSpec((tm, tn), lambda i, j, k: (i, j))

    def body(a_ref, b_ref, *rest):
        ex_refs = rest[:n_ex]
        out_refs = rest[n_ex:n_ex + n_out]
        acc = rest[-1]
        k = pl.program_id(2)

        @pl.when(k == 0)
        def _():
            acc[...] = jnp.zeros_like(acc)

        av = a_ref[...].astype(MXU)
        bv = b_ref[...].astype(MXU)
        if ta:
            acc[...] += _dot_tn(av, bv)
        elif tb:
            acc[...] += _dot_nt(av, bv)
        else:
            acc[...] += _dot(av, bv)

        @pl.when(k == nk - 1)
        def _():
            res = acc[...]
            if epi is not None:
                res = epi(res, *[e[...] for e in ex_refs])
            if not isinstance(res, tuple):
                res = (res,)
            for o_ref, r in zip(out_refs, res):
                o_ref[...] = r.astype(o_ref.dtype)

    outs = pl.pallas_call(
        body,
        grid=(m // tm, n // tn, nk),
        in_specs=[a_spec, b_spec] + ex_specs,
        out_specs=[out_spec] * n_out,
        out_shape=[jax.ShapeDtypeStruct((m, n), dt) for dt in out_dtypes],
        scratch_shapes=[pltpu.VMEM((tm, tn), F32)],
        compiler_params=_cparams(),
        name=name,
    )(a, b, *extras)
    return outs[0] if n_out == 1 else tuple(outs)


def _rms_fwd(x, g, name):
    s, d = x.shape
    tr = _tile(s, 256)

    def body(x_ref, g_ref, h_ref, r_ref):
        xf = x_ref[...]
        r = lax.rsqrt(jnp.mean(xf * xf, axis=-1, keepdims=True) + EPS)
        h_ref[...] = ((xf * r) * g_ref[...]).astype(h_ref.dtype)
        r_ref[...] = r

    return pl.pallas_call(
        body,
        grid=(s // tr,),
        in_specs=[pl.BlockSpec((tr, d), lambda i: (i, 0)), pl.BlockSpec((1, d), lambda i: (0, 0))],
        out_specs=[pl.BlockSpec((tr, d), lambda i: (i, 0)), pl.BlockSpec((tr, 1), lambda i: (i, 0))],
        out_shape=[jax.ShapeDtypeStruct((s, d), MXU), jax.ShapeDtypeStruct((s, 1), F32)],
        compiler_params=_cparams(),
        name=name,
    )(x, g)


def _rms_bwd(x, r, g, dh, dres, name):
    s, d = x.shape
    tr = _tile(s, 256)

    def body(x_ref, r_ref, g_ref, dh_ref, dres_ref, dx_ref, dxb_ref, dg_ref):
        @pl.when(pl.program_id(0) == 0)
        def _():
            dg_ref[...] = jnp.zeros_like(dg_ref)

        rr = r_ref[...]
        xhat = x_ref[...] * rr
        dh_v = dh_ref[...]
        dxhat = dh_v * g_ref[...]
        mean = jnp.mean(dxhat * xhat, axis=-1, keepdims=True)
        dx = dres_ref[...] + rr * (dxhat - xhat * mean)
        dx_ref[...] = dx
        dxb_ref[...] = dx.astype(dxb_ref.dtype)
        dg_ref[...] += jnp.sum(dh_v * xhat, axis=0, keepdims=True)

    row = pl.BlockSpec((tr, d), lambda i: (i, 0))
    vec = pl.BlockSpec((1, d), lambda i: (0, 0))
    return pl.pallas_call(
        body,
        grid=(s // tr,),
        in_specs=[row, pl.BlockSpec((tr, 1), lambda i: (i, 0)), vec, row, row],
        out_specs=[row, row, vec],
        out_shape=[jax.ShapeDtypeStruct((s, d), F32), jax.ShapeDtypeStruct((s, d), MXU),
                   jax.ShapeDtypeStruct((1, d), F32)],
        compiler_params=_cparams(),
        name=name,
    )(x, r, g, dh, dres)


def _loss_head(x, g, target, name):
    s, d = x.shape
    tr = _tile(s, 256)

    def body(x_ref, g_ref, t_ref, loss_ref, dx_ref, dg_ref):
        @pl.when(pl.program_id(0) == 0)
        def _():
            dg_ref[...] = jnp.zeros_like(dg_ref)
            loss_ref[...] = jnp.zeros_like(loss_ref)

        xf = x_ref[...]
        gv = g_ref[...]
        r = lax.rsqrt(jnp.mean(xf * xf, axis=-1, keepdims=True) + EPS)
        xhat = xf * r
        err = xhat * gv - t_ref[...]
        loss_ref[...] += 0.5 * jnp.sum(jnp.mean(err * err, axis=-1, keepdims=True), axis=0, keepdims=True)
        dy = err * (1.0 / d)
        dxhat = dy * gv
        mean = jnp.mean(dxhat * xhat, axis=-1, keepdims=True)
        dx_ref[...] = r * (dxhat - xhat * mean)
        dg_ref[...] += jnp.sum(dy * xhat, axis=0, keepdims=True)

    row = pl.BlockSpec((tr, d), lambda i: (i, 0))
    vec = pl.BlockSpec((1, d), lambda i: (0, 0))
    return pl.pallas_call(
        body,
        grid=(s // tr,),
        in_specs=[row, vec, row],
        out_specs=[pl.BlockSpec((1, LANES), lambda i: (0, 0)), row, vec],
        out_shape=[jax.ShapeDtypeStruct((1, LANES), F32), jax.ShapeDtypeStruct((s, d), F32),
                   jax.ShapeDtypeStruct((1, d), F32)],
        compiler_params=_cparams(),
        name=name,
    )(x, g, target)


def _mix_fwd(oa, ob, wa, wb, gates, name):
    s, kd = oa.shape
    d = wa.shape[1]
    tm, tn = _tile(s, 512), _tile(d, 512)
    nj = d // tn

    def body(oa_ref, ob_ref, wa_ref, wb_ref, ga_ref, gb_ref, out_ref):
        ya = _dot(oa_ref[...], wa_ref[...])
        yb = _dot(ob_ref[...], wb_ref[...])
        out_ref[...] = (_sigmoid(ga_ref[...]) * ya + _sigmoid(gb_ref[...]) * yb).astype(out_ref.dtype)

    o_spec = pl.BlockSpec((tm, kd), lambda i, j: (i, 0))
    w_spec = pl.BlockSpec((kd, tn), lambda i, j: (0, j))
    return pl.pallas_call(
        body,
        grid=(s // tm, nj),
        in_specs=[o_spec, o_spec, w_spec, w_spec,
                  pl.BlockSpec((tm, tn), lambda i, j: (i, j)),
                  pl.BlockSpec((tm, tn), lambda i, j: (i, j + nj))],
        out_specs=pl.BlockSpec((tm, tn), lambda i, j: (i, j)),
        out_shape=jax.ShapeDtypeStruct((s, d), MXU),
        compiler_params=_cparams(),
        name=name,
    )(oa, ob, wa, wb, gates, gates)


def _mix_bwd(dx, w_o, oa, ob, wa, wb, gates, name):
    s, kd = oa.shape
    d = wa.shape[1]
    tm, tn = _tile(s, 512), _tile(d, 512)
    nj = d // tn

    def body(dx_ref, wo_ref, oa_ref, ob_ref, wa_ref, wb_ref, ga_ref, gb_ref,
             dya_ref, dyb_ref, dga_ref, dgb_ref):
        dm = _dot_nt(dx_ref[...], wo_ref[...])
        ya = _dot(oa_ref[...], wa_ref[...])
        yb = _dot(ob_ref[...], wb_ref[...])
        sa = _sigmoid(ga_ref[...])
        sb = _sigmoid(gb_ref[...])
        dya_ref[...] = (dm * sa).astype(dya_ref.dtype)
        dyb_ref[...] = (dm * sb).astype(dyb_ref.dtype)
        dga_ref[...] = (dm * ya * sa * (1.0 - sa)).astype(dga_ref.dtype)
        dgb_ref[...] = (dm * yb * sb * (1.0 - sb)).astype(dgb_ref.dtype)

    o_spec = pl.BlockSpec((tm, kd), lambda i, j: (i, 0))
    w_spec = pl.BlockSpec((kd, tn), lambda i, j: (0, j))
    t_spec = pl.BlockSpec((tm, tn), lambda i, j: (i, j))
    return pl.pallas_call(
        body,
        grid=(s // tm, nj),
        in_specs=[pl.BlockSpec((tm, d), lambda i, j: (i, 0)),
                  pl.BlockSpec((tn, d), lambda i, j: (j, 0)),
                  o_spec, o_spec, w_spec, w_spec, t_spec,
                  pl.BlockSpec((tm, tn), lambda i, j: (i, j + nj))],
        out_specs=[t_spec] * 4,
        out_shape=[jax.ShapeDtypeStruct((s, d), MXU)] * 4,
        compiler_params=_cparams(),
        name=name,
    )(dx, w_o, oa, ob, wa, wb, gates, gates)


def _ple(p, w_pe, h, w_pg, other, *, backward, name):
    s, kp = p.shape
    d = w_pe.shape[1]
    tm, tn = _tile(s, 512), _tile(d, 512)

    def body(p_ref, wpe_ref, h_ref, wpg_ref, other_ref, *out_refs):
        pe = _dot(p_ref[...].astype(MXU), wpe_ref[...])
        gt = _dot(h_ref[...], wpg_ref[...])
        sg = _sigmoid(gt)
        if backward:
            dout = other_ref[...]
            out_refs[0][...] = (dout * sg).astype(out_refs[0].dtype)
            out_refs[1][...] = (dout * pe * sg * (1.0 - sg)).astype(out_refs[1].dtype)
        else:
            out_refs[0][...] = other_ref[...] + pe * sg

    t_spec = pl.BlockSpec((tm, tn), lambda i, j: (i, j))
    if backward:
        out_specs, out_shape = [t_spec, t_spec], [jax.ShapeDtypeStruct((s, d), MXU)] * 2
    else:
        out_specs, out_shape = [t_spec], [jax.ShapeDtypeStruct((s, d), F32)]
    outs = pl.pallas_call(
        body,
        grid=(s // tm, d // tn),
        in_specs=[pl.BlockSpec((tm, kp), lambda i, j: (i, 0)),
                  pl.BlockSpec((kp, tn), lambda i, j: (0, j)),
                  pl.BlockSpec((tm, d), lambda i, j: (i, 0)),
                  pl.BlockSpec((d, tn), lambda i, j: (0, j)),
                  t_spec],
        out_specs=out_specs,
        out_shape=out_shape,
        compiler_params=_cparams(),
        name=name,
    )(p, w_pe, h, w_pg, other)
    return tuple(outs) if backward else outs[0]


def _split_dot(x, tri):
    hi = x.astype(jnp.bfloat16)
    r1 = x - hi.astype(F32)
    mid = r1.astype(jnp.bfloat16)
    lo = (r1 - mid.astype(F32)).astype(jnp.bfloat16)
    return _dot(hi, tri) + _dot(mid, tri) + _dot(lo, tri)


def _log_sigmoids(z):
    t = jnp.log1p(jnp.exp(-jnp.abs(z)))
    return jnp.minimum(z, 0.0) - t, jnp.minimum(-z, 0.0) - t


def _sb_fwd(q, k, v, name):
    nh, s, dh = q.shape
    tq = _tile(s, SB_TQ)

    def body(q_ref, k_ref, v_ref, o_ref):
        i = pl.program_id(1)
        qs = (q_ref[...].astype(F32) * SCALE).astype(MXU)
        row = lax.broadcasted_iota(jnp.int32, (tq, tq), 0)
        col = lax.broadcasted_iota(jnp.int32, (tq, tq), 1)
        causal = col < row
        tri = jnp.where(row > col, 1.0, 0.0).astype(jnp.bfloat16)

        def block(kb, c, acc, masked):
            rows = pl.ds(pl.multiple_of(kb * tq, tq), tq)
            ks = k_ref[rows, :]
            vs = v_ref[rows, :]
            z = _dot_nt(qs, ks)
            lb, lm = _log_sigmoids(z)
            if masked:
                lm = jnp.where(causal, lm, 0.0)
            a = jnp.exp(lb + _split_dot(lm, tri) + c)
            if masked:
                a = jnp.where(causal, a, 0.0)
            acc = acc + _dot(a.astype(MXU), vs)
            c = c + jnp.sum(lm, axis=1, keepdims=True)
            return c, acc

        c, acc = block(i, jnp.zeros((tq, 1), F32), jnp.zeros((tq, dh), F32), True)

        def live(st):
            return jnp.logical_and(st[0] >= 0, st[1] > SB_DEAD)

        def walk(st):
            c, acc = block(st[0], st[2], st[3], False)
            return st[0] - 1, jnp.max(c), c, acc

        st = lax.while_loop(live, walk, (i - 1, jnp.max(c), c, acc))
        o_ref[...] = st[3].astype(o_ref.dtype)

    blk = pl.BlockSpec((None, tq, dh), lambda h, i: (h, i, 0))
    full = pl.BlockSpec((None, s, dh), lambda h, i: (h, 0, 0))
    return pl.pallas_call(
        body,
        grid=(nh, s // tq),
        in_specs=[blk, full, full],
        out_specs=blk,
        out_shape=jax.ShapeDtypeStruct((nh, s, dh), MXU),
        compiler_params=_cparams(),
        name=name,
    )(q, k, v)


def _sb_bwd(q, k, v, do, name):
    nh, s, dh = q.shape
    tq = _tile(s, SB_TQ)
    nq = s // tq

    def body(q_ref, k_ref, v_ref, do_ref, dq_ref, dk_ref, dv_ref, dk_acc, dv_acc, carries):
        i = pl.program_id(1)

        @pl.when(i == 0)
        def _():
            dk_acc[...] = jnp.zeros_like(dk_acc)
            dv_acc[...] = jnp.zeros_like(dv_acc)

        qs = (q_ref[...].astype(F32) * SCALE).astype(MXU)
        dov = do_ref[...]
        row = lax.broadcasted_iota(jnp.int32, (tq, tq), 0)
        col = lax.broadcasted_iota(jnp.int32, (tq, tq), 1)
        causal = col < row
        tri_rev = jnp.where(row > col, 1.0, 0.0).astype(jnp.bfloat16)
        tri_excl = jnp.where(row < col, 1.0, 0.0).astype(jnp.bfloat16)

        def row_sum(kb, masked):
            rows = pl.ds(pl.multiple_of(kb * tq, tq), tq)
            _, lm = _log_sigmoids(_dot_nt(qs, k_ref[rows, :]))
            if masked:
                lm = jnp.where(causal, lm, 0.0)
            return jnp.sum(lm, axis=1, keepdims=True)

        def live(st):
            return jnp.logical_and(st[0] >= 0, st[1] > SB_DEAD)

        def record(st):
            kb, c = st[0], st[2]
            carries[kb] = c
            c = c + row_sum(kb, False)
            return kb - 1, jnp.max(c), c

        c_diag = row_sum(i, True)
        first = lax.while_loop(live, record, (i - 1, jnp.max(c_diag), c_diag))[0] + 1

        def block(kb, c, gpre, dq, masked):
            rows = pl.ds(pl.multiple_of(kb * tq, tq), tq)
            ks = k_ref[rows, :]
            vs = v_ref[rows, :]
            z = _dot_nt(qs, ks)
            lb, lm = _log_sigmoids(z)
            if masked:
                lm = jnp.where(causal, lm, 0.0)
            a = jnp.exp(lb + _split_dot(lm, tri_rev) + c)
            if masked:
                a = jnp.where(causal, a, 0.0)
            g = a * _dot_nt(dov, vs)
            gsum = gpre + _split_dot(g, tri_excl)
            dz = g - (g + gsum) * jnp.exp(lb)
            if masked:
                dz = jnp.where(causal, dz, 0.0)
            dzb = dz.astype(MXU)
            dq = dq + _dot(dzb, ks)
            dk_acc[rows, :] += _dot_tn(dzb, qs)
            dv_acc[rows, :] += _dot_tn(a.astype(MXU), dov)
            gpre = gpre + jnp.sum(g, axis=1, keepdims=True)
            return gpre, dq

        zero = jnp.zeros((tq, 1), F32)
        gpre, dq = lax.fori_loop(first, i, lambda kb, cr: block(kb, carries[kb], cr[0], cr[1], False),
                                 (zero, jnp.zeros((tq, dh), F32)))
        _, dq = block(i, zero, gpre, dq, True)
        dq_ref[...] = (dq * SCALE).astype(dq_ref.dtype)

        @pl.when(i == nq - 1)
        def _():
            dk_ref[...] = dk_acc[...].astype(dk_ref.dtype)
            dv_ref[...] = dv_acc[...].astype(dv_ref.dtype)

    blk = pl.BlockSpec((None, tq, dh), lambda h, i: (h, i, 0))
    full = pl.BlockSpec((None, s, dh), lambda h, i: (h, 0, 0))
    return pl.pallas_call(
        body,
        grid=(nh, nq),
        in_specs=[blk, full, full, blk],
        out_specs=[blk, full, full],
        out_shape=[jax.ShapeDtypeStruct((nh, s, dh), MXU)] * 3,
        scratch_shapes=[pltpu.VMEM((s, dh), F32), pltpu.VMEM((s, dh), F32),
                        pltpu.VMEM((nq, tq, 1), F32)],
        compiler_params=_cparams(),
        name=name,
    )(q, k, v, do)


def _bucket_table():
    i = np.arange(BLOCK)[:, None]
    j = np.arange(2 * BLOCK)[None, :]
    d = np.maximum(BLOCK + i - j, 0)
    max_exact = N_BUCKETS // 2
    df = np.maximum(d, 1).astype(np.float32)
    large = max_exact + (np.log(df / max_exact) / math.log(MAX_DISTANCE / max_exact)
                         * (N_BUCKETS - max_exact)).astype(np.int32)
    large = np.minimum(large, N_BUCKETS - 1)
    return np.where(d < max_exact, d, large).astype(np.int32)


def _build_bias(rel_bias, buckets, name):
    def body(rb_ref, bk_ref, out_ref):
        h = pl.program_id(0)
        bk = bk_ref[...]
        acc = jnp.zeros(bk.shape, F32)
        for b in range(N_BUCKETS):
            acc = jnp.where(bk == b, rb_ref[b, h], acc)
        out_ref[...] = acc

    return pl.pallas_call(
        body,
        grid=(SW_HEADS,),
        in_specs=[pl.BlockSpec(memory_space=pltpu.SMEM),
                  pl.BlockSpec((BLOCK, 2 * BLOCK), lambda h: (0, 0))],
        out_specs=pl.BlockSpec((None, BLOCK, 2 * BLOCK), lambda h: (h, 0, 0)),
        out_shape=jax.ShapeDtypeStruct((SW_HEADS, BLOCK, 2 * BLOCK), F32),
        name=name,
    )(rel_bias, buckets)


def _bias_grad(dbias, buckets, name):
    def body(db_ref, bk_ref, out_ref):
        bk = bk_ref[...]
        db = db_ref[...]
        lane = lax.broadcasted_iota(jnp.int32, (1, LANES), 1)
        acc = jnp.zeros((1, LANES), F32)
        for b in range(N_BUCKETS):
            part = jnp.sum(jnp.where(bk == b, db, 0.0), axis=1, keepdims=True)
            tot = jnp.sum(part, axis=0, keepdims=True)
            acc = jnp.where(lane == b, tot, acc)
        out_ref[...] = acc

    return pl.pallas_call(
        body,
        grid=(SW_HEADS,),
        in_specs=[pl.BlockSpec((None, BLOCK, 2 * BLOCK), lambda h: (h, 0, 0)),
                  pl.BlockSpec((BLOCK, 2 * BLOCK), lambda h: (0, 0))],
        out_specs=pl.BlockSpec((None, 1, LANES), lambda h: (h, 0, 0)),
        out_shape=jax.ShapeDtypeStruct((SW_HEADS, 1, LANES), F32),
        name=name,
    )(dbias, buckets)


def _swa_scores(qb, kp, kc, bias_ref, n):
    row = lax.broadcasted_iota(jnp.int32, (BLOCK, BLOCK), 0)
    col = lax.broadcasted_iota(jnp.int32, (BLOCK, BLOCK), 1)
    s1 = _dot_nt(qb, kp) + bias_ref[:, :BLOCK]
    s2 = _dot_nt(qb, kc) + bias_ref[:, BLOCK:]
    no_prev = jnp.where(n > 0, 0, BLOCK)
    s1 = jnp.where(col > row + no_prev, s1, NEG)
    s2 = jnp.where(col <= row, s2, NEG)
    return s1, s2


def _swa_fwd(q, k, v, bias, sinks, name):
    nh, s, dh = q.shape
    nb = s // BLOCK

    def body(sink_ref, q_ref, k_ref, v_ref, bias_ref, o_ref, lse_ref):
        sink = sink_ref[pl.program_id(0)]

        def step(n, carry):
            r0 = pl.multiple_of(n * BLOCK, BLOCK)
            p0 = pl.multiple_of(jnp.maximum(n - 1, 0) * BLOCK, BLOCK)
            qb = (q_ref[pl.ds(r0, BLOCK), :].astype(F32) * SCALE).astype(MXU)
            s1, s2 = _swa_scores(qb, k_ref[pl.ds(p0, BLOCK), :], k_ref[pl.ds(r0, BLOCK), :], bias_ref, n)
            m = jnp.maximum(jnp.maximum(jnp.max(s1, axis=1, keepdims=True),
                                        jnp.max(s2, axis=1, keepdims=True)), sink)
            e1 = jnp.exp(s1 - m)
            e2 = jnp.exp(s2 - m)
            den = jnp.sum(e1, axis=1, keepdims=True) + jnp.sum(e2, axis=1, keepdims=True) + jnp.exp(sink - m)
            o = (_dot((e1 / den).astype(MXU), v_ref[pl.ds(p0, BLOCK), :])
                 + _dot((e2 / den).astype(MXU), v_ref[pl.ds(r0, BLOCK), :]))
            o_ref[pl.ds(r0, BLOCK), :] = o.astype(o_ref.dtype)
            lse_ref[pl.ds(r0, BLOCK), :] = m + jnp.log(den)
            return carry

        lax.fori_loop(0, nb, step, 0)

    qspec = pl.BlockSpec((None, s, dh), lambda h: (h, 0, 0))
    kvspec = pl.BlockSpec((None, s, dh), lambda h: (h // SW_GROUP, 0, 0))
    return pl.pallas_call(
        body,
        grid=(nh,),
        in_specs=[pl.BlockSpec(memory_space=pltpu.SMEM), qspec, kvspec, kvspec,
                  pl.BlockSpec((None, BLOCK, 2 * BLOCK), lambda h: (h, 0, 0))],
        out_specs=[qspec, pl.BlockSpec((None, s, 1), lambda h: (h, 0, 0))],
        out_shape=[jax.ShapeDtypeStruct((nh, s, dh), MXU), jax.ShapeDtypeStruct((nh, s, 1), F32)],
        compiler_params=_cparams(),
        name=name,
    )(sinks, q, k, v, bias)


def _swa_bwd(q, k, v, bias, sinks, do, lse, name):
    nh, s, dh = q.shape
    nkv = k.shape[0]
    nb = s // BLOCK

    def body(sink_ref, q_ref, k_ref, v_ref, bias_ref, do_ref, lse_ref,
             dq_ref, dk_ref, dv_ref, dbias_ref, dsink_ref, dk_acc, dv_acc):
        h = pl.program_id(0)
        sink = sink_ref[h]

        @pl.when(h % SW_GROUP == 0)
        def _():
            dk_acc[...] = jnp.zeros_like(dk_acc)
            dv_acc[...] = jnp.zeros_like(dv_acc)

        dbias_ref[...] = jnp.zeros_like(dbias_ref)

        def step(n, dsink_rows):
            r0 = pl.multiple_of(n * BLOCK, BLOCK)
            p0 = pl.multiple_of(jnp.maximum(n - 1, 0) * BLOCK, BLOCK)
            cur, prev = pl.ds(r0, BLOCK), pl.ds(p0, BLOCK)
            qb = (q_ref[cur, :].astype(F32) * SCALE).astype(MXU)
            kp, kc, vp, vc = k_ref[prev, :], k_ref[cur, :], v_ref[prev, :], v_ref[cur, :]
            dob = do_ref[cur, :]
            lse_b = lse_ref[cur, :]
            s1, s2 = _swa_scores(qb, kp, kc, bias_ref, n)
            pr1 = jnp.exp(s1 - lse_b)
            pr2 = jnp.exp(s2 - lse_b)
            dpr1 = _dot_nt(dob, vp)
            dpr2 = _dot_nt(dob, vc)
            delta = (jnp.sum(pr1 * dpr1, axis=1, keepdims=True)
                     + jnp.sum(pr2 * dpr2, axis=1, keepdims=True))
            ds1 = pr1 * (dpr1 - delta)
            ds2 = pr2 * (dpr2 - delta)
            dbias_ref[:, :BLOCK] += ds1
            dbias_ref[:, BLOCK:] += ds2
            ds1b, ds2b = ds1.astype(MXU), ds2.astype(MXU)
            dq_ref[cur, :] = ((_dot(ds1b, kp) + _dot(ds2b, kc)) * SCALE).astype(dq_ref.dtype)
            dk_acc[prev, :] += _dot_tn(ds1b, qb)
            dk_acc[cur, :] += _dot_tn(ds2b, qb)
            dv_acc[prev, :] += _dot_tn(pr1.astype(MXU), dob)
            dv_acc[cur, :] += _dot_tn(pr2.astype(MXU), dob)
            return dsink_rows - jnp.exp(sink - lse_b) * delta

        dsink_rows = lax.fori_loop(0, nb, step, jnp.zeros((BLOCK, 1), F32))
        dsink_ref[...] = jnp.broadcast_to(jnp.sum(dsink_rows, axis=0, keepdims=True), (1, LANES))

        @pl.when(h % SW_GROUP == SW_GROUP - 1)
        def _():
            dk_ref[...] = dk_acc[...].astype(dk_ref.dtype)
            dv_ref[...] = dv_acc[...].astype(dv_ref.dtype)

    qspec = pl.BlockSpec((None, s, dh), lambda h: (h, 0, 0))
    kvspec = pl.BlockSpec((None, s, dh), lambda h: (h // SW_GROUP, 0, 0))
    bspec = pl.BlockSpec((None, BLOCK, 2 * BLOCK), lambda h: (h, 0, 0))
    return pl.pallas_call(
        body,
        grid=(nh,),
        in_specs=[pl.BlockSpec(memory_space=pltpu.SMEM), qspec, kvspec, kvspec, bspec, qspec,
                  pl.BlockSpec((None, s, 1), lambda h: (h, 0, 0))],
        out_specs=[qspec, kvspec, kvspec, bspec, pl.BlockSpec((None, 1, LANES), lambda h: (h, 0, 0))],
        out_shape=[jax.ShapeDtypeStruct((nh, s, dh), MXU),
                   jax.ShapeDtypeStruct((nkv, s, dh), MXU),
                   jax.ShapeDtypeStruct((nkv, s, dh), MXU),
                   jax.ShapeDtypeStruct((nh, BLOCK, 2 * BLOCK), F32),
                   jax.ShapeDtypeStruct((nh, 1, LANES), F32)],
        scratch_shapes=[pltpu.VMEM((s, dh), F32), pltpu.VMEM((s, dh), F32)],
        compiler_params=_cparams(),
        name=name,
    )(sinks, q, k, v, bias, do, lse)


def _to_heads(a):
    s, w = a.shape
    return a.reshape(s, w // HEAD_DIM, HEAD_DIM).transpose(1, 0, 2)


def _from_heads(a):
    nh, s, dh = a.shape
    return a.transpose(1, 0, 2).reshape(s, nh * dh)


def _layer_fwd(x, p, w, g_mix, g_mlp, g_pe, sinks, bias, tag):
    h1, r1 = _rms_fwd(x, g_mix, f"rms_mix_{tag}")
    qkv = _mm(h1, w["w_qkv"], out_dtypes=(MXU,), tn=768, name=f"proj_qkv_{tag}")
    gates = _mm(h1, w["w_gate"], name=f"proj_gate_{tag}")
    heads = _to_heads(qkv)
    qa, ka, va = heads[0:8], heads[8:16], heads[16:24]
    qb, kb, vb = heads[24:32], heads[32:34], heads[34:36]
    oa_h = _sb_fwd(qa, ka, va, f"sb_fwd_{tag}")
    ob_h, lse = _swa_fwd(qb, kb, vb, bias, sinks, f"swa_fwd_{tag}")
    oa, ob = _from_heads(oa_h), _from_heads(ob_h)
    merged = _mix_fwd(oa, ob, w["w_up_a"], w["w_up_b"], gates, f"mix_fwd_{tag}")
    x1 = _mm(merged, w["w_o"], extras=(x,), epi=lambda acc, res: res + acc, name=f"out_proj_{tag}")
    h2, r2 = _rms_fwd(x1, g_mlp, f"rms_mlp_{tag}")
    u, act = _mm(h2, w["w_ff1"], epi=lambda acc: (acc, jnp.square(jnp.maximum(acc, 0.0))),
                 out_dtypes=(F32, MXU), name=f"ff1_{tag}")
    x2 = _mm(act, w["w_ff2"], extras=(x1,), epi=lambda acc, res: res + acc, name=f"ff2_{tag}")
    h3, r3 = _rms_fwd(x2, g_pe, f"rms_pe_{tag}")
    x3 = _ple(p, w["w_pe"], h3, w["w_pg"], x2, backward=False, name=f"ple_fwd_{tag}")
    saved = dict(x=x, h1=h1, r1=r1, gates=gates, qa=qa, ka=ka, va=va, qb=qb, kb=kb, vb=vb,
                 lse=lse, oa=oa, ob=ob, merged=merged, x1=x1, h2=h2, r2=r2, u=u, act=act,
                 x2=x2, h3=h3, r3=r3)
    return x3, saved


def _layer_bwd(dx3, sv, p, w, g_mix, g_mlp, g_pe, sinks, bias, tag):
    gw = {}
    dpe, dgt = _ple(p, w["w_pe"], sv["h3"], w["w_pg"], dx3, backward=True, name=f"ple_bwd_{tag}")
    gw["w_pe"] = _mm(p, dpe, ta=True, name=f"dw_pe_{tag}")
    gw["w_pg"] = _mm(sv["h3"], dgt, ta=True, name=f"dw_pg_{tag}")
    dh3 = _mm(dgt, w["w_pg"], tb=True, name=f"dh_pe_{tag}")
    dx2, dx2b, dg_pe = _rms_bwd(sv["x2"], sv["r3"], g_pe, dh3, dx3, f"rms_pe_bwd_{tag}")
    gw["w_ff2"] = _mm(sv["act"], dx2b, ta=True, name=f"dw_ff2_{tag}")
    du = _mm(dx2b, w["w_ff2"], tb=True, extras=(sv["u"],),
             epi=lambda acc, u: acc * (2.0 * jnp.maximum(u, 0.0)), out_dtypes=(MXU,), name=f"dact_{tag}")
    gw["w_ff1"] = _mm(sv["h2"], du, ta=True, name=f"dw_ff1_{tag}")
    dh2 = _mm(du, w["w_ff1"], tb=True, name=f"dh_mlp_{tag}")
    dx1, dx1b, dg_mlp = _rms_bwd(sv["x1"], sv["r2"], g_mlp, dh2, dx2, f"rms_mlp_bwd_{tag}")
    gw["w_o"] = _mm(sv["merged"], dx1b, ta=True, name=f"dw_o_{tag}")
    dya, dyb, dga, dgb = _mix_bwd(dx1b, w["w_o"], sv["oa"], sv["ob"], w["w_up_a"], w["w_up_b"],
                                  sv["gates"], f"mix_bwd_{tag}")
    gw["w_up_a"] = _mm(sv["oa"], dya, ta=True, name=f"dw_up_a_{tag}")
    gw["w_up_b"] = _mm(sv["ob"], dyb, ta=True, name=f"dw_up_b_{tag}")
    doa = _mm(dya, w["w_up_a"], tb=True, out_dtypes=(MXU,), name=f"do_a_{tag}")
    dob = _mm(dyb, w["w_up_b"], tb=True, out_dtypes=(MXU,), name=f"do_b_{tag}")
    dqb, dkb, dvb, dbias, dsink = _swa_bwd(sv["qb"], sv["kb"], sv["vb"], bias, sinks, _to_heads(dob),
                                           sv["lse"], f"swa_bwd_{tag}")
    dqa, dka, dva = _sb_bwd(sv["qa"], sv["ka"], sv["va"], _to_heads(doa), f"sb_bwd_{tag}")
    dqkv = _from_heads(jnp.concatenate([dqa, dka, dva, dqb, dkb, dvb], axis=0))
    gw_qkv = _mm(sv["h1"], dqkv, ta=True, tn=768, name=f"dw_qkv_{tag}")
    gw_ga = _mm(sv["h1"], dga, ta=True, name=f"dw_ga_{tag}")
    gw_gb = _mm(sv["h1"], dgb, ta=True, name=f"dw_gb_{tag}")
    gw["w_in"] = jnp.concatenate([gw_qkv, gw_ga, gw_gb], axis=1)
    d = dga.shape[1]
    add = lambda acc, res: res + acc
    dh1 = _mm(dqkv, w["w_qkv"], tb=True, tk=768, name=f"dh_qkv_{tag}")
    dh1 = _mm(dga, w["w_gate"][:, :d], tb=True, extras=(dh1,), epi=add, name=f"dh_ga_{tag}")
    dh1 = _mm(dgb, w["w_gate"][:, d:], tb=True, extras=(dh1,), epi=add, name=f"dh_gb_{tag}")
    dx, _, dg_mix = _rms_bwd(sv["x"], sv["r1"], g_mix, dh1, dx1, f"rms_mix_bwd_{tag}")
    small = dict(g_mix=dg_mix, g_mlp=dg_mlp, g_pe=dg_pe, sinks=dsink[:, 0, 0], dbias=dbias)
    return dx, gw, small


def _local_step(x, p, target, wfull, g_mix, g_mlp, g_pe, g_final, sinks, rel_bias):
    depth = len(wfull)
    buckets = jnp.asarray(_bucket_table())
    bias = _build_bias(rel_bias, buckets, "build_bias")
    saved = []
    h = x
    for l in range(depth):
        h, sv = _layer_fwd(h, p[l], wfull[l], g_mix[l:l + 1], g_mlp[l:l + 1], g_pe[l:l + 1],
                           sinks[l], bias, f"l{l}")
        saved.append(sv)
    loss_row, dx, dg_final = _loss_head(h, g_final[None, :], target, "loss_head")
    gws = [None] * depth
    smalls = [None] * depth
    for l in reversed(range(depth)):
        dx, gws[l], smalls[l] = _layer_bwd(dx, saved[l], p[l], wfull[l], g_mix[l:l + 1], g_mlp[l:l + 1],
                                           g_pe[l:l + 1], sinks[l], bias, f"l{l}")
    dbias = smalls[0]["dbias"]
    for l in range(1, depth):
        dbias = dbias + smalls[l]["dbias"]
    drel = _bias_grad(dbias, buckets, "bias_grad")[:, 0, :N_BUCKETS].T
    small = dict(
        g_mix=jnp.concatenate([sm["g_mix"] for sm in smalls], axis=0),
        g_mlp=jnp.concatenate([sm["g_mlp"] for sm in smalls], axis=0),
        g_pe=jnp.concatenate([sm["g_pe"] for sm in smalls], axis=0),
        g_final=dg_final[0],
        sinks=jnp.stack([sm["sinks"] for sm in smalls], axis=0),
        rel_bias=drel,
    )
    return loss_row, dx, gws, small


MESH_ID = pl.DeviceIdType.MESH
ANY = pl.BlockSpec(memory_space=pl.ANY)


def _position():
    return lax.axis_index("x"), lax.axis_index("y"), lax.axis_index("c")


def _all_gather(shard, name):
    def body(x_ref, out_ref, send_sems, recv_sems, local_sem):
        x, y, c = _position()
        me, sibling = (x, y, c), (x, y, 1 - c)
        chips = [(1 - x, y), (x, 1 - y), (1 - x, 1 - y)]

        def slot(px, py, pc):
            return out_ref.at[4 * px + 2 * py + pc]

        def copy(k, block, to, src=None):
            return pltpu.make_async_remote_copy(
                src_ref=slot(*block) if src is None else src, dst_ref=slot(*block),
                send_sem=send_sems.at[k], recv_sem=recv_sems.at[k],
                device_id=to, device_id_type=MESH_ID)

        mine = pltpu.make_async_copy(x_ref, slot(*me), local_sem)
        mine.start()
        first = [copy(0, me, sibling, src=x_ref)]
        first += [copy(1 + j, me, (*chip, c), src=x_ref) for j, chip in enumerate(chips)]
        for cp in first:
            cp.start()
        passed = [copy(4 + j, (*chip, c), sibling) for j, chip in enumerate(chips)]
        for j, chip in enumerate(chips):
            copy(1 + j, (*chip, c), me).wait_recv()
            passed[j].start()
        copy(0, sibling, me).wait_recv()
        for j, chip in enumerate(chips):
            copy(4 + j, (*chip, 1 - c), me).wait_recv()
        for cp in first + passed:
            cp.wait_send()
        mine.wait()

    return pl.pallas_call(
        body,
        out_shape=jax.ShapeDtypeStruct((N_DEV,) + shard.shape, shard.dtype),
        in_specs=[ANY],
        out_specs=ANY,
        scratch_shapes=[pltpu.SemaphoreType.DMA((7,)), pltpu.SemaphoreType.DMA((7,)),
                        pltpu.SemaphoreType.DMA(())],
        name=name,
    )(shard)


def _rs_sibling(g, name):
    _, r, lanes = g.shape

    def body(g_ref, out_ref, send_sems, recv_sems):
        x, y, c = _position()
        copies = []
        for j in range(4):
            copies.append(pltpu.make_async_remote_copy(
                src_ref=g_ref.at[2 * j + (1 - c)], dst_ref=out_ref.at[j],
                send_sem=send_sems.at[j], recv_sem=recv_sems.at[j],
                device_id=(x, y, 1 - c), device_id_type=MESH_ID))
        for cp in copies:
            cp.start()
        for cp in copies:
            cp.wait()

    return pl.pallas_call(
        body,
        out_shape=jax.ShapeDtypeStruct((4, r, lanes), g.dtype),
        in_specs=[ANY],
        out_specs=ANY,
        scratch_shapes=[pltpu.SemaphoreType.DMA((4,)), pltpu.SemaphoreType.DMA((4,))],
        name=name,
    )(g)


def _chip_of(k, x, y):
    return x ^ ((k + 1) & 1), y ^ (((k + 1) >> 1) & 1)


def _chip_partials(pos, g, recv, name):
    _, r, lanes = g.shape
    tr = _tile(r, 1024)

    def body(pos_ref, g_ref, r_ref, out_ref):
        out_ref[...] = (g_ref[...].astype(F32) + r_ref[...].astype(F32)).astype(out_ref.dtype)

    def g_map(k, i, pos_ref):
        cx, cy = _chip_of(k, pos_ref[0], pos_ref[1])
        return (4 * cx + 2 * cy + pos_ref[2], i, 0)

    def r_map(k, i, pos_ref):
        cx, cy = _chip_of(k, pos_ref[0], pos_ref[1])
        return (2 * cx + cy, i, 0)

    return pl.pallas_call(
        body,
        grid_spec=pltpu.PrefetchScalarGridSpec(
            num_scalar_prefetch=1,
            grid=(4, r // tr),
            in_specs=[pl.BlockSpec((None, tr, lanes), g_map), pl.BlockSpec((None, tr, lanes), r_map)],
            out_specs=pl.BlockSpec((None, tr, lanes), lambda k, i, pos_ref: (k, i, 0)),
        ),
        out_shape=jax.ShapeDtypeStruct((4, r, lanes), g.dtype),
        compiler_params=_cparams(),
        name=name,
    )(pos, g, recv)


def _rs_chips(part, name):
    _, r, lanes = part.shape

    def body(p_ref, out_ref, send_sems, recv_sems):
        x, y, c = _position()
        copies = []
        for k in range(3):
            cx, cy = _chip_of(k, x, y)
            copies.append(pltpu.make_async_remote_copy(
                src_ref=p_ref.at[k], dst_ref=out_ref.at[k],
                send_sem=send_sems.at[k], recv_sem=recv_sems.at[k],
                device_id=(cx, cy, c), device_id_type=MESH_ID))
        for cp in copies:
            cp.start()
        for cp in copies:
            cp.wait()

    return pl.pallas_call(
        body,
        out_shape=jax.ShapeDtypeStruct((3, r, lanes), part.dtype),
        in_specs=[ANY],
        out_specs=ANY,
        scratch_shapes=[pltpu.SemaphoreType.DMA((3,)), pltpu.SemaphoreType.DMA((3,))],
        name=name,
    )(part)


def _adamw_math(w, g, m, v):
    m = ADAM_B1 * m + (1.0 - ADAM_B1) * g
    v = ADAM_B2 * v + (1.0 - ADAM_B2) * (g * g)
    m_hat = m / (1.0 - ADAM_B1 ** ADAM_STEP)
    v_hat = v / (1.0 - ADAM_B2 ** ADAM_STEP)
    delta = -ADAM_LR * (m_hat / (jnp.sqrt(v_hat) + ADAM_EPS) + ADAM_WD * w)
    return delta, m, v


def _adamw_sharded(part, recv, w, m, v, name):
    r, lanes = w.shape
    tr = _tile(r, 1024)

    def body(p_ref, r_ref, w_ref, m_ref, v_ref, g_out, d_out, m_out, v_out):
        g = p_ref[...].astype(F32)
        for k in range(3):
            g = g + r_ref[k].astype(F32)
        delta, m_new, v_new = _adamw_math(w_ref[...], g, m_ref[...], v_ref[...])
        g_out[...] = g
        d_out[...] = delta
        m_out[...] = m_new
        v_out[...] = v_new

    row = pl.BlockSpec((tr, lanes), lambda i: (i, 0))
    return pl.pallas_call(
        body,
        grid=(r // tr,),
        in_specs=[pl.BlockSpec((None, tr, lanes), lambda i: (3, i, 0)),
                  pl.BlockSpec((3, tr, lanes), lambda i: (0, i, 0)), row, row, row],
        out_specs=[row] * 4,
        out_shape=[jax.ShapeDtypeStruct((r, lanes), F32)] * 4,
        compiler_params=_cparams(),
        name=name,
    )(part, recv, w, m, v)


def _adamw_replicated(gathered, w, m, v, name):
    r, lanes = w.shape

    def body(g_ref, w_ref, m_ref, v_ref, g_out, d_out, m_out, v_out):
        g = g_ref[0]
        for k in range(1, N_DEV):
            g = g + g_ref[k]
        delta, m_new, v_new = _adamw_math(w_ref[...], g, m_ref[...], v_ref[...])
        g_out[...] = g
        d_out[...] = delta
        m_out[...] = m_new
        v_out[...] = v_new

    return pl.pallas_call(
        body,
        out_shape=[jax.ShapeDtypeStruct((r, lanes), F32)] * 4,
        name=name,
    )(gathered, w, m, v)


def _pack_shards(arrs, dtype):
    return jnp.concatenate([a.astype(dtype).reshape(-1, LANES) for a in arrs], axis=0)


def _unpack_shards(packed, shapes):
    out, off = [], 0
    for shp in shapes:
        rows = math.prod(shp) // LANES
        out.append(packed[off:off + rows].reshape(shp))
        off += rows
    return out


def _unpack_gathered(gathered, shapes):
    depth = shapes[0][0]
    full = [dict() for _ in range(depth)]
    off = 0
    for name, shp in zip(WEIGHTS, shapes):
        rows = math.prod(shp) // LANES
        blk = gathered[:, off:off + rows].reshape((N_DEV,) + shp)
        off += rows
        _, a, b = shp
        for l in range(depth):
            if name in COL_SHARDED:
                full[l][name] = blk[:, l].transpose(1, 0, 2).reshape(a, N_DEV * b)
            else:
                full[l][name] = blk[:, l].reshape(N_DEV * a, b)
    return full


def _pack_full_grads(gws, shapes, dtype):
    depth = len(gws)
    parts = []
    for name, shp in zip(WEIGHTS, shapes):
        _, a, b = shp
        per_layer = []
        for l in range(depth):
            gfull = gws[l][name].astype(dtype)
            if name in COL_SHARDED:
                per_layer.append(gfull.reshape(a, N_DEV, b).transpose(1, 0, 2))
            else:
                per_layer.append(gfull.reshape(N_DEV, a, b))
        parts.append(jnp.stack(per_layer, axis=1).reshape(N_DEV, -1, LANES))
    return jnp.concatenate(parts, axis=1)


def _pack_small(arrs):
    rows = []
    for a in arrs:
        flat = a.astype(F32).reshape(-1)
        pad = (-flat.shape[0]) % LANES
        rows.append(jnp.pad(flat, (0, pad)).reshape(-1, LANES))
    packed = jnp.concatenate(rows, axis=0)
    return jnp.pad(packed, ((0, (-packed.shape[0]) % 8), (0, 0)))


def _unpack_small(packed, shapes):
    out, off = [], 0
    for shp in shapes:
        n = math.prod(shp)
        rows = -(-n // LANES)
        out.append(packed[off:off + rows].reshape(-1)[:n].reshape(shp))
        off += rows
    return out


def kernel(x, p, w_in, w_up_a, w_up_b, w_o, w_ff1, w_ff2, w_pe, w_pg, g_mix, g_mlp, g_pe, g_final, sinks, rel_bias, loss_target, m_w_in, m_w_up_a, m_w_up_b, m_w_o, m_w_ff1, m_w_ff2, m_w_pe, m_w_pg, m_g_mix, m_g_mlp, m_g_pe, m_g_final, m_sinks, m_rel_bias, v_w_in, v_w_up_a, v_w_up_b, v_w_o, v_w_ff1, v_w_ff2, v_w_pe, v_w_pg, v_g_mix, v_g_mlp, v_g_pe, v_g_final, v_sinks, v_rel_bias):
    w_sh = [w_in, w_up_a, w_up_b, w_o, w_ff1, w_ff2, w_pe, w_pg]
    m_sh = [m_w_in, m_w_up_a, m_w_up_b, m_w_o, m_w_ff1, m_w_ff2, m_w_pe, m_w_pg]
    v_sh = [v_w_in, v_w_up_a, v_w_up_b, v_w_o, v_w_ff1, v_w_ff2, v_w_pe, v_w_pg]
    shapes = [a.shape for a in w_sh]
    d_model = x.shape[-1]

    gathered = _all_gather(_pack_shards(w_sh, WIRE), "gather_weights")
    wfull = _unpack_gathered(gathered, shapes)
    for wl in wfull:
        w_all = wl.pop("w_in")
        wl["w_qkv"] = w_all[:, :QKV_COLS]
        wl["w_gate"] = w_all[:, QKV_COLS:]
        assert wl["w_gate"].shape[1] == 2 * d_model

    loss_row, grad_x, gws, small = _local_step(
        x[0], p[:, 0], loss_target[0], wfull, g_mix, g_mlp, g_pe, g_final, sinks, rel_bias)

    px, py, pc = _position()
    pos = jnp.stack([px, py, pc]).astype(jnp.int32)
    g_by_target = _pack_full_grads(gws, shapes, WIRE)
    from_sibling = _rs_sibling(g_by_target, "reduce_sibling")
    partials = _chip_partials(pos, g_by_target, from_sibling, "chip_partials")
    from_chips = _rs_chips(partials, "reduce_chips")

    packed = _adamw_sharded(partials, from_chips, _pack_shards(w_sh, F32), _pack_shards(m_sh, F32),
                            _pack_shards(v_sh, F32), "adamw_sharded")
    grad_w, delta_w, new_m, new_v = [_unpack_shards(t, shapes) for t in packed]

    small_w = [g_mix, g_mlp, g_pe, g_final, sinks, rel_bias]
    small_m = [m_g_mix, m_g_mlp, m_g_pe, m_g_final, m_sinks, m_rel_bias]
    small_v = [v_g_mix, v_g_mlp, v_g_pe, v_g_final, v_sinks, v_rel_bias]
    small_shapes = [a.shape for a in small_w] + [(1,)]
    zero = jnp.zeros((1,), F32)
    small_g = _pack_small([small[n] for n in SMALL] + [loss_row[0, :1]])
    small_all = _all_gather(small_g, "gather_small")
    packed_s = _adamw_replicated(small_all, _pack_small(small_w + [zero]), _pack_small(small_m + [zero]),
                                 _pack_small(small_v + [zero + 1.0]), "adamw_replicated")
    sg, sd, sm, sv = [_unpack_small(t, small_shapes) for t in packed_s]
    loss = sg[-1][0]

    return (loss, grad_x[None], *grad_w, *sg[:-1], *delta_w, *sd[:-1], *new_m, *sm[:-1], *new_v, *sv[:-1])
```

```python
import functools
import math

import numpy as np
import jax
import jax.numpy as jnp
from jax import lax
from jax.experimental import pallas as pl
from jax.experimental.pallas import tpu as pltpu

F32 = jnp.float32
MXU = jnp.bfloat16
WIRE = jnp.bfloat16

HEAD_DIM = 64
SB_HEADS = 8
SW_HEADS = 8
SW_KV = 2
SW_GROUP = SW_HEADS // SW_KV
BLOCK = 128
N_BUCKETS = 32
MAX_DISTANCE = 128
EPS = 1e-6
SCALE = HEAD_DIM ** -0.5
SB_W = SB_HEADS * HEAD_DIM
QKV_COLS = 3 * SB_W + SW_HEADS * HEAD_DIM + 2 * SW_KV * HEAD_DIM
N_QKV_HEADS = QKV_COLS // HEAD_DIM
N_DEV = 8
LANES = 128
NEG = -1e30

ADAM_LR = 0.001
ADAM_B1 = 0.9
ADAM_B2 = 0.999
ADAM_EPS = 1e-08
ADAM_WD = 0.01
ADAM_STEP = 10

VMEM_LIMIT = 48 * 1024 * 1024
SB_TQ = 256
SB_DEAD = -105.0

WEIGHTS = ("w_in", "w_up_a", "w_up_b", "w_o", "w_ff1", "w_ff2", "w_pe", "w_pg")
COL_SHARDED = ("w_in", "w_up_a", "w_up_b", "w_ff1", "w_pe")
SMALL = ("g_mix", "g_mlp", "g_pe", "g_final", "sinks", "rel_bias")


def _cparams(**kw):
    return pltpu.CompilerParams(vmem_limit_bytes=VMEM_LIMIT, **kw)


def _dot(a, b):
    return jnp.dot(a, b, preferred_element_type=F32)


def _dot_nt(a, b):
    return lax.dot_general(a, b, (((1,), (1,)), ((), ())), preferred_element_type=F32)


def _dot_tn(a, b):
    return lax.dot_general(a, b, (((0,), (0,)), ((), ())), preferred_element_type=F32)


def _tile(n, target):
    if n <= target:
        return n
    t = (target // LANES) * LANES
    while t > LANES and n % t:
        t -= LANES
    assert n % t == 0, (n, target)
    return t


def _sigmoid(x):
    return 1.0 / (1.0 + jnp.exp(-x))


def _mm(a, b, *, ta=False, tb=False, extras=(), epi=None, out_dtypes=(F32,),
        tm=1024, tn=1024, tk=512, name):
    if ta:
        kdim, m = a.shape
    else:
        m, kdim = a.shape
    n = b.shape[0] if tb else b.shape[1]
    assert (b.shape[1] if tb else b.shape[0]) == kdim
    tm, tn, tk = _tile(m, tm), _tile(n, tn), _tile(kdim, tk)
    nk = kdim // tk
    n_ex, n_out = len(extras), len(out_dtypes)

    a_spec = (pl.BlockSpec((tk, tm), lambda i, j, k: (k, i)) if ta
              else pl.BlockSpec((tm, tk), lambda i, j, k: (i, k)))
    b_spec = (pl.BlockSpec((tn, tk), lambda i, j, k: (j, k)) if tb
              else pl.BlockSpec((tk, tn), lambda i, j, k: (k, j)))
    ex_specs = []
    for e in extras:
        assert e.shape == (m, n), (e.shape, m, n)
        ex_specs.append(pl.BlockSpec((tm, tn), lambda i, j, k: (i, j)))
    out_spec = pl.BlockSpec((tm, tn), lambda i, j, k: (i, j))

    def body(a_ref, b_ref, *rest):
        ex_refs = rest[:n_ex]
        out_refs = rest[n_ex:n_ex + n_out]
        acc = rest[-1]
        k = pl.program_id(2)

        @pl.when(k == 0)
        def _():
            acc[...] = jnp.zeros_like(acc)

        av = a_ref[...].astype(MXU)
        bv = b_ref[...].astype(MXU)
        if ta:
            acc[...] += _dot_tn(av, bv)
        elif tb:
            acc[...] += _dot_nt(av, bv)
        else:
            acc[...] += _dot(av, bv)

        @pl.when(k == nk - 1)
        def _():
            res = acc[...]
            if epi is not None:
                res = epi(res, *[e[...] for e in ex_refs])
            if not isinstance(res, tuple):
                res = (res,)
            for o_ref, r in zip(out_refs, res):
                o_ref[...] = r.astype(o_ref.dtype)

    outs = pl.pallas_call(
        body,
        grid=(m // tm, n // tn, nk),
        in_specs=[a_spec, b_spec] + ex_specs,
        out_specs=[out_spec] * n_out,
        out_shape=[jax.ShapeDtypeStruct((m, n), dt) for dt in out_dtypes],
        scratch_shapes=[pltpu.VMEM((tm, tn), F32)],
        compiler_params=_cparams(),
        name=name,
    )(a, b, *extras)
    return outs[0] if n_out == 1 else tuple(outs)


def _rms_fwd(x, g, name):
    s, d = x.shape
    tr = _tile(s, 256)

    def body(x_ref, g_ref, h_ref, r_ref):
        xf = x_ref[...]
        r = lax.rsqrt(jnp.mean(xf * xf, axis=-1, keepdims=True) + EPS)
        h_ref[...] = ((xf * r) * g_ref[...]).astype(h_ref.dtype)
        r_ref[...] = r

    return pl.pallas_call(
        body,
        grid=(s // tr,),
        in_specs=[pl.BlockSpec((tr, d), lambda i: (i, 0)), pl.BlockSpec((1, d), lambda i: (0, 0))],
        out_specs=[pl.BlockSpec((tr, d), lambda i: (i, 0)), pl.BlockSpec((tr, 1), lambda i: (i, 0))],
        out_shape=[jax.ShapeDtypeStruct((s, d), MXU), jax.ShapeDtypeStruct((s, 1), F32)],
        compiler_params=_cparams(),
        name=name,
    )(x, g)


def _rms_bwd(x, r, g, dh, dres, name):
    s, d = x.shape
    tr = _tile(s, 256)

    def body(x_ref, r_ref, g_ref, dh_ref, dres_ref, dx_ref, dxb_ref, dg_ref):
        @pl.when(pl.program_id(0) == 0)
        def _():
            dg_ref[...] = jnp.zeros_like(dg_ref)

        rr = r_ref[...]
        xhat = x_ref[...] * rr
        dh_v = dh_ref[...]
        dxhat = dh_v * g_ref[...]
        mean = jnp.mean(dxhat * xhat, axis=-1, keepdims=True)
        dx = dres_ref[...] + rr * (dxhat - xhat * mean)
        dx_ref[...] = dx
        dxb_ref[...] = dx.astype(dxb_ref.dtype)
        dg_ref[...] += jnp.sum(dh_v * xhat, axis=0, keepdims=True)

    row = pl.BlockSpec((tr, d), lambda i: (i, 0))
    vec = pl.BlockSpec((1, d), lambda i: (0, 0))
    return pl.pallas_call(
        body,
        grid=(s // tr,),
        in_specs=[row, pl.BlockSpec((tr, 1), lambda i: (i, 0)), vec, row, row],
        out_specs=[row, row, vec],
        out_shape=[jax.ShapeDtypeStruct((s, d), F32), jax.ShapeDtypeStruct((s, d), MXU),
                   jax.ShapeDtypeStruct((1, d), F32)],
        compiler_params=_cparams(),
        name=name,
    )(x, r, g, dh, dres)


def _loss_head(x, g, target, name):
    s, d = x.shape
    tr = _tile(s, 256)

    def body(x_ref, g_ref, t_ref, loss_ref, dx_ref, dg_ref):
        @pl.when(pl.program_id(0) == 0)
        def _():
            dg_ref[...] = jnp.zeros_like(dg_ref)
            loss_ref[...] = jnp.zeros_like(loss_ref)

        xf = x_ref[...]
        gv = g_ref[...]
        r = lax.rsqrt(jnp.mean(xf * xf, axis=-1, keepdims=True) + EPS)
        xhat = xf * r
        err = xhat * gv - t_ref[...]
        loss_ref[...] += 0.5 * jnp.sum(jnp.mean(err * err, axis=-1, keepdims=True), axis=0, keepdims=True)
        dy = err * (1.0 / d)
        dxhat = dy * gv
        mean = jnp.mean(dxhat * xhat, axis=-1, keepdims=True)
        dx_ref[...] = r * (dxhat - xhat * mean)
        dg_ref[...] += jnp.sum(dy * xhat, axis=0, keepdims=True)

    row = pl.BlockSpec((tr, d), lambda i: (i, 0))
    vec = pl.BlockSpec((1, d), lambda i: (0, 0))
    return pl.pallas_call(
        body,
        grid=(s // tr,),
        in_specs=[row, vec, row],
        out_specs=[pl.BlockSpec((1, LANES), lambda i: (0, 0)), row, vec],
        out_shape=[jax.ShapeDtypeStruct((1, LANES), F32), jax.ShapeDtypeStruct((s, d), F32),
                   jax.ShapeDtypeStruct((1, d), F32)],
        compiler_params=_cparams(),
        name=name,
    )(x, g, target)


def _mix_fwd(oa, ob, wa, wb, gates, name):
    s, kd = oa.shape
    d = wa.shape[1]
    tm, tn = _tile(s, 1024), _tile(d, 512)
    nj = d // tn

    def body(oa_ref, ob_ref, wa_ref, wb_ref, ga_ref, gb_ref, out_ref):
        ya = _dot(oa_ref[...], wa_ref[...])
        yb = _dot(ob_ref[...], wb_ref[...])
        out_ref[...] = (_sigmoid(ga_ref[...]) * ya + _sigmoid(gb_ref[...]) * yb).astype(out_ref.dtype)

    o_spec = pl.BlockSpec((tm, kd), lambda i, j: (i, 0))
    w_spec = pl.BlockSpec((kd, tn), lambda i, j: (0, j))
    return pl.pallas_call(
        body,
        grid=(s // tm, nj),
        in_specs=[o_spec, o_spec, w_spec, w_spec,
                  pl.BlockSpec((tm, tn), lambda i, j: (i, j)),
                  pl.BlockSpec((tm, tn), lambda i, j: (i, j + nj))],
        out_specs=pl.BlockSpec((tm, tn), lambda i, j: (i, j)),
        out_shape=jax.ShapeDtypeStruct((s, d), MXU),
        compiler_params=_cparams(),
        name=name,
    )(oa, ob, wa, wb, gates, gates)


def _mix_bwd(dx, w_o, oa, ob, wa, wb, gates, name):
    s, kd = oa.shape
    d = wa.shape[1]
    tm, tn = _tile(s, 1024), _tile(d, 512)
    nj = d // tn

    def body(dx_ref, wo_ref, oa_ref, ob_ref, wa_ref, wb_ref, ga_ref, gb_ref,
             dya_ref, dyb_ref, dga_ref, dgb_ref):
        dm = _dot_nt(dx_ref[...], wo_ref[...])
        ya = _dot(oa_ref[...], wa_ref[...])
        yb = _dot(ob_ref[...], wb_ref[...])
        sa = _sigmoid(ga_ref[...])
        sb = _sigmoid(gb_ref[...])
        dya_ref[...] = (dm * sa).astype(dya_ref.dtype)
        dyb_ref[...] = (dm * sb).astype(dyb_ref.dtype)
        dga_ref[...] = (dm * ya * sa * (1.0 - sa)).astype(dga_ref.dtype)
        dgb_ref[...] = (dm * yb * sb * (1.0 - sb)).astype(dgb_ref.dtype)

    o_spec = pl.BlockSpec((tm, kd), lambda i, j: (i, 0))
    w_spec = pl.BlockSpec((kd, tn), lambda i, j: (0, j))
    t_spec = pl.BlockSpec((tm, tn), lambda i, j: (i, j))
    return pl.pallas_call(
        body,
        grid=(s // tm, nj),
        in_specs=[pl.BlockSpec((tm, d), lambda i, j: (i, 0)),
                  pl.BlockSpec((tn, d), lambda i, j: (j, 0)),
                  o_spec, o_spec, w_spec, w_spec, t_spec,
                  pl.BlockSpec((tm, tn), lambda i, j: (i, j + nj))],
        out_specs=[t_spec] * 4,
        out_shape=[jax.ShapeDtypeStruct((s, d), MXU)] * 4,
        compiler_params=_cparams(),
        name=name,
    )(dx, w_o, oa, ob, wa, wb, gates, gates)


def _ple(p, w_pe, h, w_pg, other, *, backward, name):
    s, kp = p.shape
    d = w_pe.shape[1]
    tm, tn = _tile(s, 1024), _tile(d, 512)

    def body(p_ref, wpe_ref, h_ref, wpg_ref, other_ref, *out_refs):
        pe = _dot(p_ref[...].astype(MXU), wpe_ref[...])
        gt = _dot(h_ref[...], wpg_ref[...])
        sg = _sigmoid(gt)
        if backward:
            dout = other_ref[...]
            out_refs[0][...] = (dout * sg).astype(out_refs[0].dtype)
            out_refs[1][...] = (dout * pe * sg * (1.0 - sg)).astype(out_refs[1].dtype)
        else:
            out_refs[0][...] = other_ref[...] + pe * sg

    t_spec = pl.BlockSpec((tm, tn), lambda i, j: (i, j))
    if backward:
        out_specs, out_shape = [t_spec, t_spec], [jax.ShapeDtypeStruct((s, d), MXU)] * 2
    else:
        out_specs, out_shape = [t_spec], [jax.ShapeDtypeStruct((s, d), F32)]
    outs = pl.pallas_call(
        body,
        grid=(s // tm, d // tn),
        in_specs=[pl.BlockSpec((tm, kp), lambda i, j: (i, 0)),
                  pl.BlockSpec((kp, tn), lambda i, j: (0, j)),
                  pl.BlockSpec((tm, d), lambda i, j: (i, 0)),
                  pl.BlockSpec((d, tn), lambda i, j: (0, j)),
                  t_spec],
        out_specs=out_specs,
        out_shape=out_shape,
        compiler_params=_cparams(),
        name=name,
    )(p, w_pe, h, w_pg, other)
    return tuple(outs) if backward else outs[0]


def _split_dot(x, tri):
    hi = x.astype(jnp.bfloat16)
    r1 = x - hi.astype(F32)
    mid = r1.astype(jnp.bfloat16)
    lo = (r1 - mid.astype(F32)).astype(jnp.bfloat16)
    return _dot(hi, tri) + _dot(mid, tri) + _dot(lo, tri)


def _log_sigmoids(z):
    t = jnp.log1p(jnp.exp(-jnp.abs(z)))
    return jnp.minimum(z, 0.0) - t, jnp.minimum(-z, 0.0) - t


def _sb_fwd(q, k, v, name):
    nh, s, dh = q.shape
    tq = _tile(s, SB_TQ)

    def body(q_ref, k_ref, v_ref, o_ref):
        i = pl.program_id(1)
        qs = (q_ref[...].astype(F32) * SCALE).astype(MXU)
        row = lax.broadcasted_iota(jnp.int32, (tq, tq), 0)
        col = lax.broadcasted_iota(jnp.int32, (tq, tq), 1)
        causal = col < row
        tri = jnp.where(row > col, 1.0, 0.0).astype(jnp.bfloat16)

        def block(kb, c, acc, masked):
            rows = pl.ds(pl.multiple_of(kb * tq, tq), tq)
            ks = k_ref[rows, :]
            vs = v_ref[rows, :]
            z = _dot_nt(qs, ks)
            lb, lm = _log_sigmoids(z)
            if masked:
                lm = jnp.where(causal, lm, 0.0)
            a = jnp.exp(lb + _split_dot(lm, tri) + c)
            if masked:
                a = jnp.where(causal, a, 0.0)
            acc = acc + _dot(a.astype(MXU), vs)
            c = c + jnp.sum(lm, axis=1, keepdims=True)
            return c, acc

        c, acc = block(i, jnp.zeros((tq, 1), F32), jnp.zeros((tq, dh), F32), True)

        def live(st):
            return jnp.logical_and(st[0] >= 0, st[1] > SB_DEAD)

        def walk(st):
            c, acc = block(st[0], st[2], st[3], False)
            return st[0] - 1, jnp.max(c), c, acc

        st = lax.while_loop(live, walk, (i - 1, jnp.max(c), c, acc))
        o_ref[...] = st[3].astype(o_ref.dtype)

    blk = pl.BlockSpec((None, tq, dh), lambda h, i: (h, i, 0))
    full = pl.BlockSpec((None, s, dh), lambda h, i: (h, 0, 0))
    return pl.pallas_call(
        body,
        grid=(nh, s // tq),
        in_specs=[blk, full, full],
        out_specs=blk,
        out_shape=jax.ShapeDtypeStruct((nh, s, dh), MXU),
        compiler_params=_cparams(),
        name=name,
    )(q, k, v)


def _sb_bwd(q, k, v, do, name):
    nh, s, dh = q.shape
    tq = _tile(s, SB_TQ)
    nq = s // tq

    def body(q_ref, k_ref, v_ref, do_ref, dq_ref, dk_ref, dv_ref, dk_acc, dv_acc, carries):
        i = pl.program_id(1)

        @pl.when(i == 0)
        def _():
            dk_acc[...] = jnp.zeros_like(dk_acc)
            dv_acc[...] = jnp.zeros_like(dv_acc)

        qs = (q_ref[...].astype(F32) * SCALE).astype(MXU)
        dov = do_ref[...]
        row = lax.broadcasted_iota(jnp.int32, (tq, tq), 0)
        col = lax.broadcasted_iota(jnp.int32, (tq, tq), 1)
        causal = col < row
        tri_rev = jnp.where(row > col, 1.0, 0.0).astype(jnp.bfloat16)
        tri_excl = jnp.where(row < col, 1.0, 0.0).astype(jnp.bfloat16)

        def row_sum(kb, masked):
            rows = pl.ds(pl.multiple_of(kb * tq, tq), tq)
            _, lm = _log_sigmoids(_dot_nt(qs, k_ref[rows, :]))
            if masked:
                lm = jnp.where(causal, lm, 0.0)
            return jnp.sum(lm, axis=1, keepdims=True)

        def live(st):
            return jnp.logical_and(st[0] >= 0, st[1] > SB_DEAD)

        def record(st):
            kb, c = st[0], st[2]
            carries[kb] = c
            c = c + row_sum(kb, False)
            return kb - 1, jnp.max(c), c

        c_diag = row_sum(i, True)
        first = lax.while_loop(live, record, (i - 1, jnp.max(c_diag), c_diag))[0] + 1

        def block(kb, c, gpre, dq, masked):
            rows = pl.ds(pl.multiple_of(kb * tq, tq), tq)
            ks = k_ref[rows, :]
            vs = v_ref[rows, :]
            z = _dot_nt(qs, ks)
            lb, lm = _log_sigmoids(z)
            if masked:
                lm = jnp.where(causal, lm, 0.0)
            a = jnp.exp(lb + _split_dot(lm, tri_rev) + c)
            if masked:
                a = jnp.where(causal, a, 0.0)
            g = a * _dot_nt(dov, vs)
            gsum = gpre + _split_dot(g, tri_excl)
            dz = g - (g + gsum) * jnp.exp(lb)
            if masked:
                dz = jnp.where(causal, dz, 0.0)
            dzb = dz.astype(MXU)
            dq = dq + _dot(dzb, ks)
            dk_acc[rows, :] += _dot_tn(dzb, qs)
            dv_acc[rows, :] += _dot_tn(a.astype(MXU), dov)
            gpre = gpre + jnp.sum(g, axis=1, keepdims=True)
            return gpre, dq

        zero = jnp.zeros((tq, 1), F32)
        gpre, dq = lax.fori_loop(first, i, lambda kb, cr: block(kb, carries[kb], cr[0], cr[1], False),
                                 (zero, jnp.zeros((tq, dh), F32)))
        _, dq = block(i, zero, gpre, dq, True)
        dq_ref[...] = (dq * SCALE).astype(dq_ref.dtype)

        @pl.when(i == nq - 1)
        def _():
            dk_ref[...] = dk_acc[...].astype(dk_ref.dtype)
            dv_ref[...] = dv_acc[...].astype(dv_ref.dtype)

    blk = pl.BlockSpec((None, tq, dh), lambda h, i: (h, i, 0))
    full = pl.BlockSpec((None, s, dh), lambda h, i: (h, 0, 0))
    return pl.pallas_call(
        body,
        grid=(nh, nq),
        in_specs=[blk, full, full, blk],
        out_specs=[blk, full, full],
        out_shape=[jax.ShapeDtypeStruct((nh, s, dh), MXU)] * 3,
        scratch_shapes=[pltpu.VMEM((s, dh), F32), pltpu.VMEM((s, dh), F32),
                        pltpu.VMEM((nq, tq, 1), F32)],
        compiler_params=_cparams(),
        name=name,
    )(q, k, v, do)


def _bucket_table():
    i = np.arange(BLOCK)[:, None]
    j = np.arange(2 * BLOCK)[None, :]
    d = np.maximum(BLOCK + i - j, 0)
    max_exact = N_BUCKETS // 2
    df = np.maximum(d, 1).astype(np.float32)
    large = max_exact + (np.log(df / max_exact) / math.log(MAX_DISTANCE / max_exact)
                         * (N_BUCKETS - max_exact)).astype(np.int32)
    large = np.minimum(large, N_BUCKETS - 1)
    return np.where(d < max_exact, d, large).astype(np.int32)


def _build_bias(rel_bias, buckets, name):
    def body(rb_ref, bk_ref, out_ref):
        h = pl.program_id(0)
        bk = bk_ref[...]
        acc = jnp.zeros(bk.shape, F32)
        for b in range(N_BUCKETS):
            acc = jnp.where(bk == b, rb_ref[b, h], acc)
        out_ref[...] = acc

    return pl.pallas_call(
        body,
        grid=(SW_HEADS,),
        in_specs=[pl.BlockSpec(memory_space=pltpu.SMEM),
                  pl.BlockSpec((BLOCK, 2 * BLOCK), lambda h: (0, 0))],
        out_specs=pl.BlockSpec((None, BLOCK, 2 * BLOCK), lambda h: (h, 0, 0)),
        out_shape=jax.ShapeDtypeStruct((SW_HEADS, BLOCK, 2 * BLOCK), F32),
        name=name,
    )(rel_bias, buckets)


def _bias_grad(dbias, buckets, name):
    def body(db_ref, bk_ref, out_ref):
        bk = bk_ref[...]
        db = db_ref[...]
        lane = lax.broadcasted_iota(jnp.int32, (1, LANES), 1)
        acc = jnp.zeros((1, LANES), F32)
        for b in range(N_BUCKETS):
            part = jnp.sum(jnp.where(bk == b, db, 0.0), axis=1, keepdims=True)
            tot = jnp.sum(part, axis=0, keepdims=True)
            acc = jnp.where(lane == b, tot, acc)
        out_ref[...] = acc

    return pl.pallas_call(
        body,
        grid=(SW_HEADS,),
        in_specs=[pl.BlockSpec((None, BLOCK, 2 * BLOCK), lambda h: (h, 0, 0)),
                  pl.BlockSpec((BLOCK, 2 * BLOCK), lambda h: (0, 0))],
        out_specs=pl.BlockSpec((None, 1, LANES), lambda h: (h, 0, 0)),
        out_shape=jax.ShapeDtypeStruct((SW_HEADS, 1, LANES), F32),
        name=name,
    )(dbias, buckets)


def _swa_scores(qb, kp, kc, bias_ref, n):
    row = lax.broadcasted_iota(jnp.int32, (BLOCK, BLOCK), 0)
    col = lax.broadcasted_iota(jnp.int32, (BLOCK, BLOCK), 1)
    s1 = _dot_nt(qb, kp) + bias_ref[:, :BLOCK]
    s2 = _dot_nt(qb, kc) + bias_ref[:, BLOCK:]
    no_prev = jnp.where(n > 0, 0, BLOCK)
    s1 = jnp.where(col > row + no_prev, s1, NEG)
    s2 = jnp.where(col <= row, s2, NEG)
    return s1, s2


def _swa_fwd(q, k, v, bias, sinks, name):
    nh, s, dh = q.shape
    nb = s // BLOCK

    def body(sink_ref, q_ref, k_ref, v_ref, bias_ref, o_ref, lse_ref):
        sink = sink_ref[pl.program_id(0)]

        def step(n, carry):
            r0 = pl.multiple_of(n * BLOCK, BLOCK)
            p0 = pl.multiple_of(jnp.maximum(n - 1, 0) * BLOCK, BLOCK)
            qb = (q_ref[pl.ds(r0, BLOCK), :].astype(F32) * SCALE).astype(MXU)
            s1, s2 = _swa_scores(qb, k_ref[pl.ds(p0, BLOCK), :], k_ref[pl.ds(r0, BLOCK), :], bias_ref, n)
            m = jnp.maximum(jnp.maximum(jnp.max(s1, axis=1, keepdims=True),
                                        jnp.max(s2, axis=1, keepdims=True)), sink)
            e1 = jnp.exp(s1 - m)
            e2 = jnp.exp(s2 - m)
            den = jnp.sum(e1, axis=1, keepdims=True) + jnp.sum(e2, axis=1, keepdims=True) + jnp.exp(sink - m)
            o = (_dot((e1 / den).astype(MXU), v_ref[pl.ds(p0, BLOCK), :])
                 + _dot((e2 / den).astype(MXU), v_ref[pl.ds(r0, BLOCK), :]))
            o_ref[pl.ds(r0, BLOCK), :] = o.astype(o_ref.dtype)
            lse_ref[pl.ds(r0, BLOCK), :] = m + jnp.log(den)
            return carry

        lax.fori_loop(0, nb, step, 0)

    qspec = pl.BlockSpec((None, s, dh), lambda h: (h, 0, 0))
    kvspec = pl.BlockSpec((None, s, dh), lambda h: (h // SW_GROUP, 0, 0))
    return pl.pallas_call(
        body,
        grid=(nh,),
        in_specs=[pl.BlockSpec(memory_space=pltpu.SMEM), qspec, kvspec, kvspec,
                  pl.BlockSpec((None, BLOCK, 2 * BLOCK), lambda h: (h, 0, 0))],
        out_specs=[qspec, pl.BlockSpec((None, s, 1), lambda h: (h, 0, 0))],
        out_shape=[jax.ShapeDtypeStruct((nh, s, dh), MXU), jax.ShapeDtypeStruct((nh, s, 1), F32)],
        compiler_params=_cparams(),
        name=name,
    )(sinks, q, k, v, bias)


def _swa_bwd(q, k, v, bias, sinks, do, lse, name):
    nh, s, dh = q.shape
    nkv = k.shape[0]
    nb = s // BLOCK

    def body(sink_ref, q_ref, k_ref, v_ref, bias_ref, do_ref, lse_ref,
             dq_ref, dk_ref, dv_ref, dbias_ref, dsink_ref, dk_acc, dv_acc):
        h = pl.program_id(0)
        sink = sink_ref[h]

        @pl.when(h % SW_GROUP == 0)
        def _():
            dk_acc[...] = jnp.zeros_like(dk_acc)
            dv_acc[...] = jnp.zeros_like(dv_acc)

        dbias_ref[...] = jnp.zeros_like(dbias_ref)

        def step(n, dsink_rows):
            r0 = pl.multiple_of(n * BLOCK, BLOCK)
            p0 = pl.multiple_of(jnp.maximum(n - 1, 0) * BLOCK, BLOCK)
            cur, prev = pl.ds(r0, BLOCK), pl.ds(p0, BLOCK)
            qb = (q_ref[cur, :].astype(F32) * SCALE).astype(MXU)
            kp, kc, vp, vc = k_ref[prev, :], k_ref[cur, :], v_ref[prev, :], v_ref[cur, :]
            dob = do_ref[cur, :]
            lse_b = lse_ref[cur, :]
            s1, s2 = _swa_scores(qb, kp, kc, bias_ref, n)
            pr1 = jnp.exp(s1 - lse_b)
            pr2 = jnp.exp(s2 - lse_b)
            dpr1 = _dot_nt(dob, vp)
            dpr2 = _dot_nt(dob, vc)
            delta = (jnp.sum(pr1 * dpr1, axis=1, keepdims=True)
                     + jnp.sum(pr2 * dpr2, axis=1, keepdims=True))
            ds1 = pr1 * (dpr1 - delta)
            ds2 = pr2 * (dpr2 - delta)
            dbias_ref[:, :BLOCK] += ds1
            dbias_ref[:, BLOCK:] += ds2
            ds1b, ds2b = ds1.astype(MXU), ds2.astype(MXU)
            dq_ref[cur, :] = ((_dot(ds1b, kp) + _dot(ds2b, kc)) * SCALE).astype(dq_ref.dtype)
            dk_acc[prev, :] += _dot_tn(ds1b, qb)
            dk_acc[cur, :] += _dot_tn(ds2b, qb)
            dv_acc[prev, :] += _dot_tn(pr1.astype(MXU), dob)
            dv_acc[cur, :] += _dot_tn(pr2.astype(MXU), dob)
            return dsink_rows - jnp.exp(sink - lse_b) * delta

        dsink_rows = lax.fori_loop(0, nb, step, jnp.zeros((BLOCK, 1), F32))
        dsink_ref[...] = jnp.broadcast_to(jnp.sum(dsink_rows, axis=0, keepdims=True), (1, LANES))

        @pl.when(h % SW_GROUP == SW_GROUP - 1)
        def _():
            dk_ref[...] = dk_acc[...].astype(dk_ref.dtype)
            dv_ref[...] = dv_acc[...].astype(dv_ref.dtype)

    qspec = pl.BlockSpec((None, s, dh), lambda h: (h, 0, 0))
    kvspec = pl.BlockSpec((None, s, dh), lambda h: (h // SW_GROUP, 0, 0))
    bspec = pl.BlockSpec((None, BLOCK, 2 * BLOCK), lambda h: (h, 0, 0))
    return pl.pallas_call(
        body,
        grid=(nh,),
        in_specs=[pl.BlockSpec(memory_space=pltpu.SMEM), qspec, kvspec, kvspec, bspec, qspec,
                  pl.BlockSpec((None, s, 1), lambda h: (h, 0, 0))],
        out_specs=[qspec, kvspec, kvspec, bspec, pl.BlockSpec((None, 1, LANES), lambda h: (h, 0, 0))],
        out_shape=[jax.ShapeDtypeStruct((nh, s, dh), MXU),
                   jax.ShapeDtypeStruct((nkv, s, dh), MXU),
                   jax.ShapeDtypeStruct((nkv, s, dh), MXU),
                   jax.ShapeDtypeStruct((nh, BLOCK, 2 * BLOCK), F32),
                   jax.ShapeDtypeStruct((nh, 1, LANES), F32)],
        scratch_shapes=[pltpu.VMEM((s, dh), F32), pltpu.VMEM((s, dh), F32)],
        compiler_params=_cparams(),
        name=name,
    )(sinks, q, k, v, bias, do, lse)


def _to_heads(a):
    s, w = a.shape
    return a.reshape(s, w // HEAD_DIM, HEAD_DIM).transpose(1, 0, 2)


def _from_heads(a):
    nh, s, dh = a.shape
    return a.transpose(1, 0, 2).reshape(s, nh * dh)


def _layer_fwd(x, p, w, g_mix, g_mlp, g_pe, sinks, bias, tag):
    h1, r1 = _rms_fwd(x, g_mix, f"rms_mix_{tag}")
    qkv = _mm(h1, w["w_qkv"], out_dtypes=(MXU,), tn=768, name=f"proj_qkv_{tag}")
    gates = _mm(h1, w["w_gate"], name=f"proj_gate_{tag}")
    heads = _to_heads(qkv)
    qa, ka, va = heads[0:8], heads[8:16], heads[16:24]
    qb, kb, vb = heads[24:32], heads[32:34], heads[34:36]
    oa_h = _sb_fwd(qa, ka, va, f"sb_fwd_{tag}")
    ob_h, lse = _swa_fwd(qb, kb, vb, bias, sinks, f"swa_fwd_{tag}")
    oa, ob = _from_heads(oa_h), _from_heads(ob_h)
    merged = _mix_fwd(oa, ob, w["w_up_a"], w["w_up_b"], gates, f"mix_fwd_{tag}")
    x1 = _mm(merged, w["w_o"], extras=(x,), epi=lambda acc, res: res + acc, name=f"out_proj_{tag}")
    h2, r2 = _rms_fwd(x1, g_mlp, f"rms_mlp_{tag}")
    u, act = _mm(h2, w["w_ff1"], epi=lambda acc: (acc, jnp.square(jnp.maximum(acc, 0.0))),
                 out_dtypes=(F32, MXU), name=f"ff1_{tag}")
    x2 = _mm(act, w["w_ff2"], extras=(x1,), epi=lambda acc, res: res + acc, name=f"ff2_{tag}")
    h3, r3 = _rms_fwd(x2, g_pe, f"rms_pe_{tag}")
    x3 = _ple(p, w["w_pe"], h3, w["w_pg"], x2, backward=False, name=f"ple_fwd_{tag}")
    saved = dict(x=x, h1=h1, r1=r1, gates=gates, qa=qa, ka=ka, va=va, qb=qb, kb=kb, vb=vb,
                 lse=lse, oa=oa, ob=ob, merged=merged, x1=x1, h2=h2, r2=r2, u=u, act=act,
                 x2=x2, h3=h3, r3=r3)
    return x3, saved


def _layer_bwd(dx3, sv, p, w, g_mix, g_mlp, g_pe, sinks, bias, tag):
    gw = {}
    dpe, dgt = _ple(p, w["w_pe"], sv["h3"], w["w_pg"], dx3, backward=True, name=f"ple_bwd_{tag}")
    gw["w_pe"] = _mm(p, dpe, ta=True, name=f"dw_pe_{tag}")
    gw["w_pg"] = _mm(sv["h3"], dgt, ta=True, name=f"dw_pg_{tag}")
    dh3 = _mm(dgt, w["w_pg"], tb=True, name=f"dh_pe_{tag}")
    dx2, dx2b, dg_pe = _rms_bwd(sv["x2"], sv["r3"], g_pe, dh3, dx3, f"rms_pe_bwd_{tag}")
    gw["w_ff2"] = _mm(sv["act"], dx2b, ta=True, name=f"dw_ff2_{tag}")
    du = _mm(dx2b, w["w_ff2"], tb=True, extras=(sv["u"],),
             epi=lambda acc, u: acc * (2.0 * jnp.maximum(u, 0.0)), out_dtypes=(MXU,), name=f"dact_{tag}")
    gw["w_ff1"] = _mm(sv["h2"], du, ta=True, name=f"dw_ff1_{tag}")
    dh2 = _mm(du, w["w_ff1"], tb=True, name=f"dh_mlp_{tag}")
    dx1, dx1b, dg_mlp = _rms_bwd(sv["x1"], sv["r2"], g_mlp, dh2, dx2, f"rms_mlp_bwd_{tag}")
    gw["w_o"] = _mm(sv["merged"], dx1b, ta=True, name=f"dw_o_{tag}")
    dya, dyb, dga, dgb = _mix_bwd(dx1b, w["w_o"], sv["oa"], sv["ob"], w["w_up_a"], w["w_up_b"],
                                  sv["gates"], f"mix_bwd_{tag}")
    gw["w_up_a"] = _mm(sv["oa"], dya, ta=True, name=f"dw_up_a_{tag}")
    gw["w_up_b"] = _mm(sv["ob"], dyb, ta=True, name=f"dw_up_b_{tag}")
    doa = _mm(dya, w["w_up_a"], tb=True, out_dtypes=(MXU,), name=f"do_a_{tag}")
    dob = _mm(dyb, w["w_up_b"], tb=True, out_dtypes=(MXU,), name=f"do_b_{tag}")
    dqb, dkb, dvb, dbias, dsink = _swa_bwd(sv["qb"], sv["kb"], sv["vb"], bias, sinks, _to_heads(dob),
                                           sv["lse"], f"swa_bwd_{tag}")
    dqa, dka, dva = _sb_bwd(sv["qa"], sv["ka"], sv["va"], _to_heads(doa), f"sb_bwd_{tag}")
    dqkv = _from_heads(jnp.concatenate([dqa, dka, dva, dqb, dkb, dvb], axis=0))
    gw_qkv = _mm(sv["h1"], dqkv, ta=True, tn=768, name=f"dw_qkv_{tag}")
    gw_ga = _mm(sv["h1"], dga, ta=True, name=f"dw_ga_{tag}")
    gw_gb = _mm(sv["h1"], dgb, ta=True, name=f"dw_gb_{tag}")
    gw["w_in"] = jnp.concatenate([gw_qkv, gw_ga, gw_gb], axis=1)
    d = dga.shape[1]
    add = lambda acc, res: res + acc
    dh1 = _mm(dqkv, w["w_qkv"], tb=True, tk=768, name=f"dh_qkv_{tag}")
    dh1 = _mm(dga, w["w_gate"][:, :d], tb=True, extras=(dh1,), epi=add, name=f"dh_ga_{tag}")
    dh1 = _mm(dgb, w["w_gate"][:, d:], tb=True, extras=(dh1,), epi=add, name=f"dh_gb_{tag}")
    dx, _, dg_mix = _rms_bwd(sv["x"], sv["r1"], g_mix, dh1, dx1, f"rms_mix_bwd_{tag}")
    small = dict(g_mix=dg_mix, g_mlp=dg_mlp, g_pe=dg_pe, sinks=dsink[:, 0, 0], dbias=dbias)
    return dx, gw, small


def _local_step(x, p, target, wfull, g_mix, g_mlp, g_pe, g_final, sinks, rel_bias):
    depth = len(wfull)
    buckets = jnp.asarray(_bucket_table())
    bias = _build_bias(rel_bias, buckets, "build_bias")
    saved = []
    h = x
    for l in range(depth):
        h, sv = _layer_fwd(h, p[l], wfull[l], g_mix[l:l + 1], g_mlp[l:l + 1], g_pe[l:l + 1],
                           sinks[l], bias, f"l{l}")
        saved.append(sv)
    loss_row, dx, dg_final = _loss_head(h, g_final[None, :], target, "loss_head")
    gws = [None] * depth
    smalls = [None] * depth
    for l in reversed(range(depth)):
        dx, gws[l], smalls[l] = _layer_bwd(dx, saved[l], p[l], wfull[l], g_mix[l:l + 1], g_mlp[l:l + 1],
                                           g_pe[l:l + 1], sinks[l], bias, f"l{l}")
    dbias = smalls[0]["dbias"]
    for l in range(1, depth):
        dbias = dbias + smalls[l]["dbias"]
    drel = _bias_grad(dbias, buckets, "bias_grad")[:, 0, :N_BUCKETS].T
    small = dict(
        g_mix=jnp.concatenate([sm["g_mix"] for sm in smalls], axis=0),
        g_mlp=jnp.concatenate([sm["g_mlp"] for sm in smalls], axis=0),
        g_pe=jnp.concatenate([sm["g_pe"] for sm in smalls], axis=0),
        g_final=dg_final[0],
        sinks=jnp.stack([sm["sinks"] for sm in smalls], axis=0),
        rel_bias=drel,
    )
    return loss_row, dx, gws, small


MESH_ID = pl.DeviceIdType.MESH
ANY = pl.BlockSpec(memory_space=pl.ANY)


def _position():
    return lax.axis_index("x"), lax.axis_index("y"), lax.axis_index("c")


def _all_gather(shard, name):
    def body(x_ref, out_ref, send_sems, recv_sems, local_sem):
        x, y, c = _position()
        me, sibling = (x, y, c), (x, y, 1 - c)
        chips = [(1 - x, y), (x, 1 - y), (1 - x, 1 - y)]

        def slot(px, py, pc):
            return out_ref.at[4 * px + 2 * py + pc]

        def copy(k, block, to, src=None):
            return pltpu.make_async_remote_copy(
                src_ref=slot(*block) if src is None else src, dst_ref=slot(*block),
                send_sem=send_sems.at[k], recv_sem=recv_sems.at[k],
                device_id=to, device_id_type=MESH_ID)

        mine = pltpu.make_async_copy(x_ref, slot(*me), local_sem)
        mine.start()
        first = [copy(0, me, sibling, src=x_ref)]
        first += [copy(1 + j, me, (*chip, c), src=x_ref) for j, chip in enumerate(chips)]
        for cp in first:
            cp.start()
        passed = [copy(4 + j, (*chip, c), sibling) for j, chip in enumerate(chips)]
        for j, chip in enumerate(chips):
            copy(1 + j, (*chip, c), me).wait_recv()
            passed[j].start()
        copy(0, sibling, me).wait_recv()
        for j, chip in enumerate(chips):
            copy(4 + j, (*chip, 1 - c), me).wait_recv()
        for cp in first + passed:
            cp.wait_send()
        mine.wait()

    return pl.pallas_call(
        body,
        out_shape=jax.ShapeDtypeStruct((N_DEV,) + shard.shape, shard.dtype),
        in_specs=[ANY],
        out_specs=ANY,
        scratch_shapes=[pltpu.SemaphoreType.DMA((7,)), pltpu.SemaphoreType.DMA((7,)),
                        pltpu.SemaphoreType.DMA(())],
        name=name,
    )(shard)


def _rs_sibling(g, name):
    _, r, lanes = g.shape

    def body(g_ref, out_ref, send_sems, recv_sems):
        x, y, c = _position()
        copies = []
        for j in range(4):
            copies.append(pltpu.make_async_remote_copy(
                src_ref=g_ref.at[2 * j + (1 - c)], dst_ref=out_ref.at[j],
                send_sem=send_sems.at[j], recv_sem=recv_sems.at[j],
                device_id=(x, y, 1 - c), device_id_type=MESH_ID))
        for cp in copies:
            cp.start()
        for cp in copies:
            cp.wait()

    return pl.pallas_call(
        body,
        out_shape=jax.ShapeDtypeStruct((4, r, lanes), g.dtype),
        in_specs=[ANY],
        out_specs=ANY,
        scratch_shapes=[pltpu.SemaphoreType.DMA((4,)), pltpu.SemaphoreType.DMA((4,))],
        name=name,
    )(g)


def _chip_of(k, x, y):
    return x ^ ((k + 1) & 1), y ^ (((k + 1) >> 1) & 1)


def _chip_partials(pos, g, recv, name):
    _, r, lanes = g.shape
    tr = _tile(r, 1024)

    def body(pos_ref, g_ref, r_ref, out_ref):
        out_ref[...] = (g_ref[...].astype(F32) + r_ref[...].astype(F32)).astype(out_ref.dtype)

    def g_map(k, i, pos_ref):
        cx, cy = _chip_of(k, pos_ref[0], pos_ref[1])
        return (4 * cx + 2 * cy + pos_ref[2], i, 0)

    def r_map(k, i, pos_ref):
        cx, cy = _chip_of(k, pos_ref[0], pos_ref[1])
        return (2 * cx + cy, i, 0)

    return pl.pallas_call(
        body,
        grid_spec=pltpu.PrefetchScalarGridSpec(
            num_scalar_prefetch=1,
            grid=(4, r // tr),
            in_specs=[pl.BlockSpec((None, tr, lanes), g_map), pl.BlockSpec((None, tr, lanes), r_map)],
            out_specs=pl.BlockSpec((None, tr, lanes), lambda k, i, pos_ref: (k, i, 0)),
        ),
        out_shape=jax.ShapeDtypeStruct((4, r, lanes), g.dtype),
        compiler_params=_cparams(),
        name=name,
    )(pos, g, recv)


def _rs_chips(part, name):
    _, r, lanes = part.shape

    def body(p_ref, out_ref, send_sems, recv_sems):
        x, y, c = _position()
        copies = []
        for k in range(3):
            cx, cy = _chip_of(k, x, y)
            copies.append(pltpu.make_async_remote_copy(
                src_ref=p_ref.at[k], dst_ref=out_ref.at[k],
                send_sem=send_sems.at[k], recv_sem=recv_sems.at[k],
                device_id=(cx, cy, c), device_id_type=MESH_ID))
        for cp in copies:
            cp.start()
        for cp in copies:
            cp.wait()

    return pl.pallas_call(
        body,
        out_shape=jax.ShapeDtypeStruct((3, r, lanes), part.dtype),
        in_specs=[ANY],
        out_specs=ANY,
        scratch_shapes=[pltpu.SemaphoreType.DMA((3,)), pltpu.SemaphoreType.DMA((3,))],
        name=name,
    )(part)


def _adamw_math(w, g, m, v):
    m = ADAM_B1 * m + (1.0 - ADAM_B1) * g
    v = ADAM_B2 * v + (1.0 - ADAM_B2) * (g * g)
    m_hat = m / (1.0 - ADAM_B1 ** ADAM_STEP)
    v_hat = v / (1.0 - ADAM_B2 ** ADAM_STEP)
    delta = -ADAM_LR * (m_hat / (jnp.sqrt(v_hat) + ADAM_EPS) + ADAM_WD * w)
    return delta, m, v


def _adamw_sharded(part, recv, w, m, v, name):
    r, lanes = w.shape
    tr = _tile(r, 1024)

    def body(p_ref, r_ref, w_ref, m_ref, v_ref, g_out, d_out, m_out, v_out):
        g = p_ref[...].astype(F32)
        for k in range(3):
            g = g + r_ref[k].astype(F32)
        delta, m_new, v_new = _adamw_math(w_ref[...], g, m_ref[...], v_ref[...])
        g_out[...] = g
        d_out[...] = delta
        m_out[...] = m_new
        v_out[...] = v_new

    row = pl.BlockSpec((tr, lanes), lambda i: (i, 0))
    return pl.pallas_call(
        body,
        grid=(r // tr,),
        in_specs=[pl.BlockSpec((None, tr, lanes), lambda i: (3, i, 0)),
                  pl.BlockSpec((3, tr, lanes), lambda i: (0, i, 0)), row, row, row],
        out_specs=[row] * 4,
        out_shape=[jax.ShapeDtypeStruct((r, lanes), F32)] * 4,
        compiler_params=_cparams(),
        name=name,
    )(part, recv, w, m, v)


def _adamw_replicated(gathered, w, m, v, name):
    r, lanes = w.shape

    def body(g_ref, w_ref, m_ref, v_ref, g_out, d_out, m_out, v_out):
        g = g_ref[0]
        for k in range(1, N_DEV):
            g = g + g_ref[k]
        delta, m_new, v_new = _adamw_math(w_ref[...], g, m_ref[...], v_ref[...])
        g_out[...] = g
        d_out[...] = delta
        m_out[...] = m_new
        v_out[...] = v_new

    return pl.pallas_call(
        body,
        out_shape=[jax.ShapeDtypeStruct((r, lanes), F32)] * 4,
        name=name,
    )(gathered, w, m, v)


def _pack_shards(arrs, dtype):
    return jnp.concatenate([a.astype(dtype).reshape(-1, LANES) for a in arrs], axis=0)


def _unpack_shards(packed, shapes):
    out, off = [], 0
    for shp in shapes:
        rows = math.prod(shp) // LANES
        out.append(packed[off:off + rows].reshape(shp))
        off += rows
    return out


def _unpack_gathered(gathered, shapes):
    depth = shapes[0][0]
    full = [dict() for _ in range(depth)]
    off = 0
    for name, shp in zip(WEIGHTS, shapes):
        rows = math.prod(shp) // LANES
        blk = gathered[:, off:off + rows].reshape((N_DEV,) + shp)
        off += rows
        _, a, b = shp
        for l in range(depth):
            if name in COL_SHARDED:
                full[l][name] = blk[:, l].transpose(1, 0, 2).reshape(a, N_DEV * b)
            else:
                full[l][name] = blk[:, l].reshape(N_DEV * a, b)
    return full


def _pack_full_grads(gws, shapes, dtype):
    depth = len(gws)
    parts = []
    for name, shp in zip(WEIGHTS, shapes):
        _, a, b = shp
        per_layer = []
        for l in range(depth):
            gfull = gws[l][name].astype(dtype)
            if name in COL_SHARDED:
                per_layer.append(gfull.reshape(a, N_DEV, b).transpose(1, 0, 2))
            else:
                per_layer.append(gfull.reshape(N_DEV, a, b))
        parts.append(jnp.stack(per_layer, axis=1).reshape(N_DEV, -1, LANES))
    return jnp.concatenate(parts, axis=1)


def _pack_small(arrs):
    rows = []
    for a in arrs:
        flat = a.astype(F32).reshape(-1)
        pad = (-flat.shape[0]) % LANES
        rows.append(jnp.pad(flat, (0, pad)).reshape(-1, LANES))
    packed = jnp.concatenate(rows, axis=0)
    return jnp.pad(packed, ((0, (-packed.shape[0]) % 8), (0, 0)))


def _unpack_small(packed, shapes):
    out, off = [], 0
    for shp in shapes:
        n = math.prod(shp)
        rows = -(-n // LANES)
        out.append(packed[off:off + rows].reshape(-1)[:n].reshape(shp))
        off += rows
    return out


def kernel(x, p, w_in, w_up_a, w_up_b, w_o, w_ff1, w_ff2, w_pe, w_pg, g_mix, g_mlp, g_pe, g_final, sinks, rel_bias, loss_target, m_w_in, m_w_up_a, m_w_up_b, m_w_o, m_w_ff1, m_w_ff2, m_w_pe, m_w_pg, m_g_mix, m_g_mlp, m_g_pe, m_g_final, m_sinks, m_rel_bias, v_w_in, v_w_up_a, v_w_up_b, v_w_o, v_w_ff1, v_w_ff2, v_w_pe, v_w_pg, v_g_mix, v_g_mlp, v_g_pe, v_g_final, v_sinks, v_rel_bias):
    w_sh = [w_in, w_up_a, w_up_b, w_o, w_ff1, w_ff2, w_pe, w_pg]
    m_sh = [m_w_in, m_w_up_a, m_w_up_b, m_w_o, m_w_ff1, m_w_ff2, m_w_pe, m_w_pg]
    v_sh = [v_w_in, v_w_up_a, v_w_up_b, v_w_o, v_w_ff1, v_w_ff2, v_w_pe, v_w_pg]
    shapes = [a.shape for a in w_sh]
    d_model = x.shape[-1]

    gathered = _all_gather(_pack_shards(w_sh, WIRE), "gather_weights")
    wfull = _unpack_gathered(gathered, shapes)
    for wl in wfull:
        w_all = wl.pop("w_in")
        wl["w_qkv"] = w_all[:, :QKV_COLS]
        wl["w_gate"] = w_all[:, QKV_COLS:]
        assert wl["w_gate"].shape[1] == 2 * d_model

    loss_row, grad_x, gws, small = _local_step(
        x[0], p[:, 0], loss_target[0], wfull, g_mix, g_mlp, g_pe, g_final, sinks, rel_bias)

    px, py, pc = _position()
    pos = jnp.stack([px, py, pc]).astype(jnp.int32)
    g_by_target = _pack_full_grads(gws, shapes, WIRE)
    from_sibling = _rs_sibling(g_by_target, "reduce_sibling")
    partials = _chip_partials(pos, g_by_target, from_sibling, "chip_partials")
    from_chips = _rs_chips(partials, "reduce_chips")

    packed = _adamw_sharded(partials, from_chips, _pack_shards(w_sh, F32), _pack_shards(m_sh, F32),
                            _pack_shards(v_sh, F32), "adamw_sharded")
    grad_w, delta_w, new_m, new_v = [_unpack_shards(t, shapes) for t in packed]

    small_w = [g_mix, g_mlp, g_pe, g_final, sinks, rel_bias]
    small_m = [m_g_mix, m_g_mlp, m_g_pe, m_g_final, m_sinks, m_rel_bias]
    small_v = [v_g_mix, v_g_mlp, v_g_pe, v_g_final, v_sinks, v_rel_bias]
    small_shapes = [a.shape for a in small_w] + [(1,)]
    zero = jnp.zeros((1,), F32)
    small_g = _pack_small([small[n] for n in SMALL] + [loss_row[0, :1]])
    small_all = _all_gather(small_g, "gather_small")
    packed_s = _adamw_replicated(small_all, _pack_small(small_w + [zero]), _pack_small(small_m + [zero]),
                                 _pack_small(small_v + [zero + 1.0]), "adamw_replicated")
    sg, sd, sm, sv = [_unpack_small(t, small_shapes) for t in packed_s]
    loss = sg[-1][0]

    return (loss, grad_x[None], *grad_w, *sg[:-1], *delta_w, *sd[:-1], *new_m, *sm[:-1], *new_v, *sv[:-1])
```

```python
import functools
import math

import numpy as np
import jax
import jax.numpy as jnp
from jax import lax
from jax.experimental import pallas as pl
from jax.experimental.pallas import tpu as pltpu

F32 = jnp.float32
MXU = jnp.bfloat16
WIRE = jnp.bfloat16

HEAD_DIM = 64
SB_HEADS = 8
SW_HEADS = 8
SW_KV = 2
SW_GROUP = SW_HEADS // SW_KV
BLOCK = 128
N_BUCKETS = 32
MAX_DISTANCE = 128
EPS = 1e-6
SCALE = HEAD_DIM ** -0.5
SB_W = SB_HEADS * HEAD_DIM
SW_W = SW_HEADS * HEAD_DIM
QKV_COLS = 3 * SB_W + SW_W + 2 * SW_KV * HEAD_DIM
N_DEV = 8
LANES = 128
N_PAIR = SB_HEADS // 2
NEG = -1e30

ADAM_LR = 0.001
ADAM_B1 = 0.9
ADAM_B2 = 0.999
ADAM_EPS = 1e-08
ADAM_WD = 0.01
ADAM_STEP = 10

VMEM_LIMIT = 48 * 1024 * 1024
SB_TQ = 256
SB_DEAD = -105.0

WEIGHTS = ("w_in", "w_up_a", "w_up_b", "w_o", "w_ff1", "w_ff2", "w_pe", "w_pg")
COL_SHARDED = ("w_in", "w_up_a", "w_up_b", "w_ff1", "w_pe")
SMALL = ("g_mix", "g_mlp", "g_pe", "g_final", "sinks", "rel_bias")


def _cparams(**kw):
    return pltpu.CompilerParams(vmem_limit_bytes=VMEM_LIMIT, **kw)


def _dot(a, b):
    return jnp.dot(a, b, preferred_element_type=F32)


def _dot_nt(a, b):
    return lax.dot_general(a, b, (((1,), (1,)), ((), ())), preferred_element_type=F32)


def _dot_tn(a, b):
    return lax.dot_general(a, b, (((0,), (0,)), ((), ())), preferred_element_type=F32)


def _tile(n, target, unit=LANES):
    if n <= target:
        return n
    t = (target // unit) * unit
    while t > unit and n % t:
        t -= unit
    assert n % t == 0, (n, target)
    return t


def _sigmoid(x):
    return 1.0 / (1.0 + jnp.exp(-x))


def _mm(a, b, *, ta=False, tb=False, extras=(), epi=None, out_dtypes=(F32,),
        tm=1024, tn=1024, tk=512, name):
    if ta:
        kdim, m = a.shape
    else:
        m, kdim = a.shape
    n = b.shape[0] if tb else b.shape[1]
    assert (b.shape[1] if tb else b.shape[0]) == kdim
    tm, tn, tk = _tile(m, tm), _tile(n, tn), _tile(kdim, tk)
    nk = kdim // tk
    n_ex, n_out = len(extras), len(out_dtypes)

    a_spec = (pl.BlockSpec((tk, tm), lambda i, j, k: (k, i)) if ta
              else pl.BlockSpec((tm, tk), lambda i, j, k: (i, k)))
    b_spec = (pl.BlockSpec((tn, tk), lambda i, j, k: (j, k)) if tb
              else pl.BlockSpec((tk, tn), lambda i, j, k: (k, j)))
    ex_specs = []
    for e in extras:
        assert e.shape == (m, n), (e.shape, m, n)
        ex_specs.append(pl.BlockSpec((tm, tn), lambda i, j, k: (i, j)))
    out_spec = pl.BlockSpec((tm, tn), lambda i, j, k: (i, j))

    def body(a_ref, b_ref, *rest):
        ex_refs = rest[:n_ex]
        out_refs = rest[n_ex:n_ex + n_out]
        acc = rest[-1]
        k = pl.program_id(2)

        @pl.when(k == 0)
        def _():
            acc[...] = jnp.zeros_like(acc)

        av = a_ref[...].astype(MXU)
        bv = b_ref[...].astype(MXU)
        if ta:
            acc[...] += _dot_tn(av, bv)
        elif tb:
            acc[...] += _dot_nt(av, bv)
        else:
            acc[...] += _dot(av, bv)

        @pl.when(k == nk - 1)
        def _():
            res = acc[...]
            if epi is not None:
                res = epi(res, *[e[...] for e in ex_refs])
            if not isinstance(res, tuple):
                res = (res,)
            for o_ref, r in zip(out_refs, res):
                o_ref[...] = r.astype(o_ref.dtype)

    outs = pl.pallas_call(
        body,
        grid=(m // tm, n // tn, nk),
        in_specs=[a_spec, b_spec] + ex_specs,
        out_specs=[out_spec] * n_out,
        out_shape=[jax.ShapeDtypeStruct((m, n), dt) for dt in out_dtypes],
        scratch_shapes=[pltpu.VMEM((tm, tn), F32)],
        compiler_params=_cparams(),
        name=name,
    )(a, b, *extras)
    return outs[0] if n_out == 1 else tuple(outs)


def _rms_fwd(x, g, name):
    s, d = x.shape
    tr = _tile(s, 256)

    def body(x_ref, g_ref, h_ref, r_ref):
        xf = x_ref[...]
        r = lax.rsqrt(jnp.mean(xf * xf, axis=-1, keepdims=True) + EPS)
        h_ref[...] = ((xf * r) * g_ref[...]).astype(h_ref.dtype)
        r_ref[...] = r

    return pl.pallas_call(
        body,
        grid=(s // tr,),
        in_specs=[pl.BlockSpec((tr, d), lambda i: (i, 0)), pl.BlockSpec((1, d), lambda i: (0, 0))],
        out_specs=[pl.BlockSpec((tr, d), lambda i: (i, 0)), pl.BlockSpec((tr, 1), lambda i: (i, 0))],
        out_shape=[jax.ShapeDtypeStruct((s, d), MXU), jax.ShapeDtypeStruct((s, 1), F32)],
        compiler_params=_cparams(),
        name=name,
    )(x, g)


def _rms_bwd(x, r, g, dh, dres, name):
    s, d = x.shape
    tr = _tile(s, 256)

    def body(x_ref, r_ref, g_ref, dh_ref, dres_ref, dx_ref, dxb_ref, dg_ref):
        @pl.when(pl.program_id(0) == 0)
        def _():
            dg_ref[...] = jnp.zeros_like(dg_ref)

        rr = r_ref[...]
        xhat = x_ref[...] * rr
        dh_v = dh_ref[...]
        dxhat = dh_v * g_ref[...]
        mean = jnp.mean(dxhat * xhat, axis=-1, keepdims=True)
        dx = dres_ref[...] + rr * (dxhat - xhat * mean)
        dx_ref[...] = dx
        dxb_ref[...] = dx.astype(dxb_ref.dtype)
        dg_ref[...] += jnp.sum(dh_v * xhat, axis=0, keepdims=True)

    row = pl.BlockSpec((tr, d), lambda i: (i, 0))
    vec = pl.BlockSpec((1, d), lambda i: (0, 0))
    return pl.pallas_call(
        body,
        grid=(s // tr,),
        in_specs=[row, pl.BlockSpec((tr, 1), lambda i: (i, 0)), vec, row, row],
        out_specs=[row, row, vec],
        out_shape=[jax.ShapeDtypeStruct((s, d), F32), jax.ShapeDtypeStruct((s, d), MXU),
                   jax.ShapeDtypeStruct((1, d), F32)],
        compiler_params=_cparams(),
        name=name,
    )(x, r, g, dh, dres)


def _loss_head(x, g, target, name):
    s, d = x.shape
    tr = _tile(s, 256)

    def body(x_ref, g_ref, t_ref, loss_ref, dx_ref, dg_ref):
        @pl.when(pl.program_id(0) == 0)
        def _():
            dg_ref[...] = jnp.zeros_like(dg_ref)
            loss_ref[...] = jnp.zeros_like(loss_ref)

        xf = x_ref[...]
        gv = g_ref[...]
        r = lax.rsqrt(jnp.mean(xf * xf, axis=-1, keepdims=True) + EPS)
        xhat = xf * r
        err = xhat * gv - t_ref[...]
        loss_ref[...] += 0.5 * jnp.sum(jnp.mean(err * err, axis=-1, keepdims=True), axis=0, keepdims=True)
        dy = err * (1.0 / d)
        dxhat = dy * gv
        mean = jnp.mean(dxhat * xhat, axis=-1, keepdims=True)
        dx_ref[...] = r * (dxhat - xhat * mean)
        dg_ref[...] += jnp.sum(dy * xhat, axis=0, keepdims=True)

    row = pl.BlockSpec((tr, d), lambda i: (i, 0))
    vec = pl.BlockSpec((1, d), lambda i: (0, 0))
    return pl.pallas_call(
        body,
        grid=(s // tr,),
        in_specs=[row, vec, row],
        out_specs=[pl.BlockSpec((1, LANES), lambda i: (0, 0)), row, vec],
        out_shape=[jax.ShapeDtypeStruct((1, LANES), F32), jax.ShapeDtypeStruct((s, d), F32),
                   jax.ShapeDtypeStruct((1, d), F32)],
        compiler_params=_cparams(),
        name=name,
    )(x, g, target)


def _mix_fwd(oa, ob, wa, wb, gates, name):
    s, kd = oa.shape
    d = wa.shape[1]
    tm, tn = _tile(s, 1024), _tile(d, 512)
    nj = d // tn

    def body(oa_ref, ob_ref, wa_ref, wb_ref, ga_ref, gb_ref, out_ref):
        ya = _dot(oa_ref[...], wa_ref[...])
        yb = _dot(ob_ref[...], wb_ref[...])
        out_ref[...] = (_sigmoid(ga_ref[...]) * ya + _sigmoid(gb_ref[...]) * yb).astype(out_ref.dtype)

    o_spec = pl.BlockSpec((tm, kd), lambda i, j: (i, 0))
    w_spec = pl.BlockSpec((kd, tn), lambda i, j: (0, j))
    return pl.pallas_call(
        body,
        grid=(s // tm, nj),
        in_specs=[o_spec, o_spec, w_spec, w_spec,
                  pl.BlockSpec((tm, tn), lambda i, j: (i, j)),
                  pl.BlockSpec((tm, tn), lambda i, j: (i, j + nj))],
        out_specs=pl.BlockSpec((tm, tn), lambda i, j: (i, j)),
        out_shape=jax.ShapeDtypeStruct((s, d), MXU),
        compiler_params=_cparams(),
        name=name,
    )(oa, ob, wa, wb, gates, gates)


def _mix_bwd(dx, w_o, oa, ob, wa, wb, gates, name):
    s, kd = oa.shape
    d = wa.shape[1]
    tm, tn = _tile(s, 1024), _tile(d, 512)
    nj = d // tn

    def body(dx_ref, wo_ref, oa_ref, ob_ref, wa_ref, wb_ref, ga_ref, gb_ref,
             dya_ref, dyb_ref, dga_ref, dgb_ref):
        dm = _dot_nt(dx_ref[...], wo_ref[...])
        ya = _dot(oa_ref[...], wa_ref[...])
        yb = _dot(ob_ref[...], wb_ref[...])
        sa = _sigmoid(ga_ref[...])
        sb = _sigmoid(gb_ref[...])
        dya_ref[...] = (dm * sa).astype(dya_ref.dtype)
        dyb_ref[...] = (dm * sb).astype(dyb_ref.dtype)
        dga_ref[...] = (dm * ya * sa * (1.0 - sa)).astype(dga_ref.dtype)
        dgb_ref[...] = (dm * yb * sb * (1.0 - sb)).astype(dgb_ref.dtype)

    o_spec = pl.BlockSpec((tm, kd), lambda i, j: (i, 0))
    w_spec = pl.BlockSpec((kd, tn), lambda i, j: (0, j))
    t_spec = pl.BlockSpec((tm, tn), lambda i, j: (i, j))
    return pl.pallas_call(
        body,
        grid=(s // tm, nj),
        in_specs=[pl.BlockSpec((tm, d), lambda i, j: (i, 0)),
                  pl.BlockSpec((tn, d), lambda i, j: (j, 0)),
                  o_spec, o_spec, w_spec, w_spec, t_spec,
                  pl.BlockSpec((tm, tn), lambda i, j: (i, j + nj))],
        out_specs=[t_spec] * 4,
        out_shape=[jax.ShapeDtypeStruct((s, d), MXU)] * 4,
        compiler_params=_cparams(),
        name=name,
    )(dx, w_o, oa, ob, wa, wb, gates, gates)


def _ple(p, w_pe, h, w_pg, other, *, backward, name):
    s, kp = p.shape
    d = w_pe.shape[1]
    tm, tn = _tile(s, 1024), _tile(d, 512)

    def body(p_ref, wpe_ref, h_ref, wpg_ref, other_ref, *out_refs):
        pe = _dot(p_ref[...].astype(MXU), wpe_ref[...])
        gt = _dot(h_ref[...], wpg_ref[...])
        sg = _sigmoid(gt)
        if backward:
            dout = other_ref[...]
            out_refs[0][...] = (dout * sg).astype(out_refs[0].dtype)
            out_refs[1][...] = (dout * pe * sg * (1.0 - sg)).astype(out_refs[1].dtype)
        else:
            out_refs[0][...] = other_ref[...] + pe * sg

    t_spec = pl.BlockSpec((tm, tn), lambda i, j: (i, j))
    if backward:
        out_specs, out_shape = [t_spec, t_spec], [jax.ShapeDtypeStruct((s, d), MXU)] * 2
    else:
        out_specs, out_shape = [t_spec], [jax.ShapeDtypeStruct((s, d), F32)]
    outs = pl.pallas_call(
        body,
        grid=(s // tm, d // tn),
        in_specs=[pl.BlockSpec((tm, kp), lambda i, j: (i, 0)),
                  pl.BlockSpec((kp, tn), lambda i, j: (0, j)),
                  pl.BlockSpec((tm, d), lambda i, j: (i, 0)),
                  pl.BlockSpec((d, tn), lambda i, j: (0, j)),
                  t_spec],
        out_specs=out_specs,
        out_shape=out_shape,
        compiler_params=_cparams(),
        name=name,
    )(p, w_pe, h, w_pg, other)
    return tuple(outs) if backward else outs[0]


def _split_dot(x, tri):
    hi = x.astype(jnp.bfloat16)
    r1 = x - hi.astype(F32)
    mid = r1.astype(jnp.bfloat16)
    lo = (r1 - mid.astype(F32)).astype(jnp.bfloat16)
    return _dot(hi, tri) + _dot(mid, tri) + _dot(lo, tri)


def _log_sigmoids(z):
    t = jnp.log1p(jnp.exp(-jnp.abs(z)))
    return jnp.minimum(z, 0.0) - t, jnp.minimum(-z, 0.0) - t


def _head_lanes(hh):
    lane = lax.broadcasted_iota(jnp.int32, (1, LANES), 1)
    return jnp.logical_and(lane >= hh * HEAD_DIM, lane < (hh + 1) * HEAD_DIM)


def _sb_fwd(qkv, name):
    s = qkv.shape[0]
    tq = _tile(s, SB_TQ)

    def body(q_ref, k_ref, v_ref, o_ref):
        i = pl.program_id(1)
        qf = q_ref[...].astype(F32) * SCALE
        row = lax.broadcasted_iota(jnp.int32, (tq, tq), 0)
        col = lax.broadcasted_iota(jnp.int32, (tq, tq), 1)
        causal = col < row
        tri = jnp.where(row > col, 1.0, 0.0).astype(jnp.bfloat16)

        def one_head(hh):
            qm = jnp.where(_head_lanes(hh), qf, 0.0).astype(MXU)

            def block(kb, c, acc, masked):
                rows = pl.ds(pl.multiple_of(kb * tq, tq), tq)
                z = _dot_nt(qm, k_ref[rows, :])
                lb, lm = _log_sigmoids(z)
                if masked:
                    lm = jnp.where(causal, lm, 0.0)
                a = jnp.exp(lb + _split_dot(lm, tri) + c)
                if masked:
                    a = jnp.where(causal, a, 0.0)
                acc = acc + _dot(a.astype(MXU), v_ref[rows, :])
                c = c + jnp.sum(lm, axis=1, keepdims=True)
                return c, acc

            c, acc = block(i, jnp.zeros((tq, 1), F32), jnp.zeros((tq, LANES), F32), True)

            def live(st):
                return jnp.logical_and(st[0] >= 0, st[1] > SB_DEAD)

            def walk(st):
                c, acc = block(st[0], st[2], st[3], False)
                return st[0] - 1, jnp.max(c), c, acc

            return lax.while_loop(live, walk, (i - 1, jnp.max(c), c, acc))[3]

        o_ref[...] = jnp.where(_head_lanes(0), one_head(0), one_head(1)).astype(o_ref.dtype)

    return pl.pallas_call(
        body,
        grid=(N_PAIR, s // tq),
        in_specs=[pl.BlockSpec((tq, LANES), lambda p, i: (i, p)),
                  pl.BlockSpec((s, LANES), lambda p, i: (0, N_PAIR + p)),
                  pl.BlockSpec((s, LANES), lambda p, i: (0, 2 * N_PAIR + p))],
        out_specs=pl.BlockSpec((tq, LANES), lambda p, i: (i, p)),
        out_shape=jax.ShapeDtypeStruct((s, SB_W), MXU),
        compiler_params=_cparams(),
        name=name,
    )(qkv, qkv, qkv)


def _sb_bwd(qkv, do, name):
    s = qkv.shape[0]
    tq = _tile(s, SB_TQ)
    nq = s // tq

    def body(q_ref, k_ref, v_ref, do_ref, dq_ref, dk_ref, dv_ref, dk_acc, dv_acc, carries):
        i = pl.program_id(1)

        @pl.when(i == 0)
        def _():
            dk_acc[...] = jnp.zeros_like(dk_acc)
            dv_acc[...] = jnp.zeros_like(dv_acc)

        qf = q_ref[...].astype(F32) * SCALE
        dof = do_ref[...]
        row = lax.broadcasted_iota(jnp.int32, (tq, tq), 0)
        col = lax.broadcasted_iota(jnp.int32, (tq, tq), 1)
        causal = col < row
        tri_rev = jnp.where(row > col, 1.0, 0.0).astype(jnp.bfloat16)
        tri_excl = jnp.where(row < col, 1.0, 0.0).astype(jnp.bfloat16)

        def one_head(hh):
            in_head = _head_lanes(hh)
            qm = jnp.where(in_head, qf, 0.0).astype(MXU)
            dom = jnp.where(in_head, dof, jnp.zeros_like(dof))

            def row_sum(kb, masked):
                rows = pl.ds(pl.multiple_of(kb * tq, tq), tq)
                _, lm = _log_sigmoids(_dot_nt(qm, k_ref[rows, :]))
                if masked:
                    lm = jnp.where(causal, lm, 0.0)
                return jnp.sum(lm, axis=1, keepdims=True)

            def live(st):
                return jnp.logical_and(st[0] >= 0, st[1] > SB_DEAD)

            def record(st):
                kb, c = st[0], st[2]
                carries[kb] = c
                c = c + row_sum(kb, False)
                return kb - 1, jnp.max(c), c

            c_diag = row_sum(i, True)
            first = lax.while_loop(live, record, (i - 1, jnp.max(c_diag), c_diag))[0] + 1

            def block(kb, c, gpre, dq, masked):
                rows = pl.ds(pl.multiple_of(kb * tq, tq), tq)
                ks = k_ref[rows, :]
                vs = v_ref[rows, :]
                z = _dot_nt(qm, ks)
                lb, lm = _log_sigmoids(z)
                if masked:
                    lm = jnp.where(causal, lm, 0.0)
                a = jnp.exp(lb + _split_dot(lm, tri_rev) + c)
                if masked:
                    a = jnp.where(causal, a, 0.0)
                g = a * _dot_nt(dom, vs)
                gsum = gpre + _split_dot(g, tri_excl)
                dz = g - (g + gsum) * jnp.exp(lb)
                if masked:
                    dz = jnp.where(causal, dz, 0.0)
                dzb = dz.astype(MXU)
                dq = dq + _dot(dzb, ks)
                dk_acc[rows, :] += _dot_tn(dzb, qm)
                dv_acc[rows, :] += _dot_tn(a.astype(MXU), dom)
                gpre = gpre + jnp.sum(g, axis=1, keepdims=True)
                return gpre, dq

            zero = jnp.zeros((tq, 1), F32)
            gpre, dq = lax.fori_loop(first, i, lambda kb, cr: block(kb, carries[kb], cr[0], cr[1], False),
                                     (zero, jnp.zeros((tq, LANES), F32)))
            return block(i, zero, gpre, dq, True)[1]

        dq_ref[...] = (jnp.where(_head_lanes(0), one_head(0), one_head(1)) * SCALE).astype(dq_ref.dtype)

        @pl.when(i == nq - 1)
        def _():
            dk_ref[...] = dk_acc[...].astype(dk_ref.dtype)
            dv_ref[...] = dv_acc[...].astype(dv_ref.dtype)

    blk = pl.BlockSpec((tq, LANES), lambda p, i: (i, p))
    full = pl.BlockSpec((s, LANES), lambda p, i: (0, p))
    return pl.pallas_call(
        body,
        grid=(N_PAIR, nq),
        in_specs=[blk,
                  pl.BlockSpec((s, LANES), lambda p, i: (0, N_PAIR + p)),
                  pl.BlockSpec((s, LANES), lambda p, i: (0, 2 * N_PAIR + p)),
                  blk],
        out_specs=[blk, full, full],
        out_shape=[jax.ShapeDtypeStruct((s, SB_W), MXU)] * 3,
        scratch_shapes=[pltpu.VMEM((s, LANES), F32), pltpu.VMEM((s, LANES), F32),
                        pltpu.VMEM((nq, tq, 1), F32)],
        compiler_params=_cparams(),
        name=name,
    )(qkv, qkv, qkv, do)


def _bucket_table():
    i = np.arange(BLOCK)[:, None]
    j = np.arange(2 * BLOCK)[None, :]
    d = np.maximum(BLOCK + i - j, 0)
    max_exact = N_BUCKETS // 2
    df = np.maximum(d, 1).astype(np.float32)
    large = max_exact + (np.log(df / max_exact) / math.log(MAX_DISTANCE / max_exact)
                         * (N_BUCKETS - max_exact)).astype(np.int32)
    large = np.minimum(large, N_BUCKETS - 1)
    return np.where(d < max_exact, d, large).astype(np.int32)


def _build_bias(rel_bias, buckets, name):
    def body(rb_ref, bk_ref, out_ref):
        h = pl.program_id(0)
        bk = bk_ref[...]
        acc = jnp.zeros(bk.shape, F32)
        for b in range(N_BUCKETS):
            acc = jnp.where(bk == b, rb_ref[b, h], acc)
        out_ref[...] = acc

    return pl.pallas_call(
        body,
        grid=(SW_HEADS,),
        in_specs=[pl.BlockSpec(memory_space=pltpu.SMEM),
                  pl.BlockSpec((BLOCK, 2 * BLOCK), lambda h: (0, 0))],
        out_specs=pl.BlockSpec((None, BLOCK, 2 * BLOCK), lambda h: (h, 0, 0)),
        out_shape=jax.ShapeDtypeStruct((SW_HEADS, BLOCK, 2 * BLOCK), F32),
        name=name,
    )(rel_bias, buckets)


def _bias_grad(dbias_layers, buckets, name):
    n_l = len(dbias_layers)

    def body(*refs):
        bk = refs[n_l][...]
        out_ref = refs[n_l + 1]
        db = refs[0][...]
        for r in refs[1:n_l]:
            db = db + r[...]
        lane = lax.broadcasted_iota(jnp.int32, (1, LANES), 1)
        acc = jnp.zeros((1, LANES), F32)
        for b in range(N_BUCKETS):
            part = jnp.sum(jnp.where(bk == b, db, 0.0), axis=1, keepdims=True)
            tot = jnp.sum(part, axis=0, keepdims=True)
            acc = jnp.where(lane == b, tot, acc)
        out_ref[...] = acc

    hspec = pl.BlockSpec((None, BLOCK, 2 * BLOCK), lambda h: (h, 0, 0))
    return pl.pallas_call(
        body,
        grid=(SW_HEADS,),
        in_specs=[hspec] * n_l + [pl.BlockSpec((BLOCK, 2 * BLOCK), lambda h: (0, 0))],
        out_specs=pl.BlockSpec((None, 1, LANES), lambda h: (h, 0, 0)),
        out_shape=jax.ShapeDtypeStruct((SW_HEADS, 1, LANES), F32),
        name=name,
    )(*dbias_layers, buckets)


def _swa_scores(qb, kp, kc, bias_ref, hh, n):
    row = lax.broadcasted_iota(jnp.int32, (BLOCK, BLOCK), 0)
    col = lax.broadcasted_iota(jnp.int32, (BLOCK, BLOCK), 1)
    s1 = _dot_nt(qb, kp) + bias_ref[hh, :, :BLOCK]
    s2 = _dot_nt(qb, kc) + bias_ref[hh, :, BLOCK:]
    no_prev = jnp.where(n > 0, 0, BLOCK)
    s1 = jnp.where(col > row + no_prev, s1, NEG)
    s2 = jnp.where(col <= row, s2, NEG)
    return s1, s2


def _swa_lanes(p, hh):
    lane = lax.broadcasted_iota(jnp.int32, (1, LANES), 1)
    kv_half = jnp.zeros((1, LANES), jnp.int32) + (2 * p + hh) // SW_GROUP
    return jnp.where(lane >= HEAD_DIM, 1, 0) == kv_half, kv_half == hh


def _to_kv_lanes(x, kv_lanes, aligned):
    return jnp.where(kv_lanes, jnp.where(aligned, x, pltpu.roll(x, HEAD_DIM, 1)), 0.0)


def _swa_fwd(qkv, bias, sinks, name):
    s = qkv.shape[0]
    nb = s // BLOCK
    q0 = 3 * N_PAIR
    kblk, vblk = q0 + N_PAIR, q0 + N_PAIR + 1

    def body(sink_ref, q_ref, k_ref, v_ref, bias_ref, o_ref, lse_ref):
        p = pl.program_id(0)

        def step(n, carry):
            r0 = pl.multiple_of(n * BLOCK, BLOCK)
            p0 = pl.multiple_of(jnp.maximum(n - 1, 0) * BLOCK, BLOCK)
            cur, prev = pl.ds(r0, BLOCK), pl.ds(p0, BLOCK)
            qf = q_ref[cur, :].astype(F32) * SCALE
            kp, kc, vp, vc = k_ref[prev, :], k_ref[cur, :], v_ref[prev, :], v_ref[cur, :]
            outs = []
            for hh in range(2):
                kv_lanes, aligned = _swa_lanes(p, hh)
                sink = sink_ref[2 * p + hh]
                qb = _to_kv_lanes(qf, kv_lanes, aligned).astype(MXU)
                s1, s2 = _swa_scores(qb, kp, kc, bias_ref, hh, n)
                m = jnp.maximum(jnp.maximum(jnp.max(s1, axis=1, keepdims=True),
                                            jnp.max(s2, axis=1, keepdims=True)), sink)
                e1 = jnp.exp(s1 - m)
                e2 = jnp.exp(s2 - m)
                den = (jnp.sum(e1, axis=1, keepdims=True) + jnp.sum(e2, axis=1, keepdims=True)
                       + jnp.exp(sink - m))
                o = _dot((e1 / den).astype(MXU), vp) + _dot((e2 / den).astype(MXU), vc)
                outs.append(jnp.where(aligned, o, pltpu.roll(o, HEAD_DIM, 1)))
                lse_ref[hh, cur, :] = m + jnp.log(den)
            o_ref[cur, :] = jnp.where(_head_lanes(0), outs[0], outs[1]).astype(o_ref.dtype)
            return carry

        lax.fori_loop(0, nb, step, 0)

    return pl.pallas_call(
        body,
        grid=(N_PAIR,),
        in_specs=[pl.BlockSpec(memory_space=pltpu.SMEM),
                  pl.BlockSpec((s, LANES), lambda p: (0, q0 + p)),
                  pl.BlockSpec((s, LANES), lambda p: (0, kblk)),
                  pl.BlockSpec((s, LANES), lambda p: (0, vblk)),
                  pl.BlockSpec((2, BLOCK, 2 * BLOCK), lambda p: (p, 0, 0))],
        out_specs=[pl.BlockSpec((s, LANES), lambda p: (0, p)),
                   pl.BlockSpec((None, 2, s, 1), lambda p: (p, 0, 0, 0))],
        out_shape=[jax.ShapeDtypeStruct((s, SW_W), MXU), jax.ShapeDtypeStruct((N_PAIR, 2, s, 1), F32)],
        compiler_params=_cparams(),
        name=name,
    )(sinks, qkv, qkv, qkv, bias)


def _swa_bwd(qkv, bias, sinks, do, lse, name):
    s = qkv.shape[0]
    nb = s // BLOCK
    q0 = 3 * N_PAIR
    kblk, vblk = q0 + N_PAIR, q0 + N_PAIR + 1

    def body(sink_ref, q_ref, k_ref, v_ref, bias_ref, do_ref, lse_ref,
             dq_ref, dk_ref, dv_ref, dbias_ref, dsink_ref, dk_acc, dv_acc):
        p = pl.program_id(0)

        @pl.when(p == 0)
        def _():
            dk_acc[...] = jnp.zeros_like(dk_acc)
            dv_acc[...] = jnp.zeros_like(dv_acc)

        dbias_ref[...] = jnp.zeros_like(dbias_ref)

        def step(n, dsink_rows):
            r0 = pl.multiple_of(n * BLOCK, BLOCK)
            p0 = pl.multiple_of(jnp.maximum(n - 1, 0) * BLOCK, BLOCK)
            cur, prev = pl.ds(r0, BLOCK), pl.ds(p0, BLOCK)
            qf = q_ref[cur, :].astype(F32) * SCALE
            dof = do_ref[cur, :].astype(F32)
            kp, kc, vp, vc = k_ref[prev, :], k_ref[cur, :], v_ref[prev, :], v_ref[cur, :]
            dqs, new_rows = [], []
            for hh in range(2):
                kv_lanes, aligned = _swa_lanes(p, hh)
                sink = sink_ref[2 * p + hh]
                qb = _to_kv_lanes(qf, kv_lanes, aligned).astype(MXU)
                dob = _to_kv_lanes(dof, kv_lanes, aligned).astype(MXU)
                lse_b = lse_ref[hh, cur, :]
                s1, s2 = _swa_scores(qb, kp, kc, bias_ref, hh, n)
                pr1 = jnp.exp(s1 - lse_b)
                pr2 = jnp.exp(s2 - lse_b)
                dpr1 = _dot_nt(dob, vp)
                dpr2 = _dot_nt(dob, vc)
                delta = (jnp.sum(pr1 * dpr1, axis=1, keepdims=True)
                         + jnp.sum(pr2 * dpr2, axis=1, keepdims=True))
                ds1 = pr1 * (dpr1 - delta)
                ds2 = pr2 * (dpr2 - delta)
                dbias_ref[hh, :, :BLOCK] += ds1
                dbias_ref[hh, :, BLOCK:] += ds2
                ds1b, ds2b = ds1.astype(MXU), ds2.astype(MXU)
                dq = _dot(ds1b, kp) + _dot(ds2b, kc)
                dqs.append(jnp.where(aligned, dq, pltpu.roll(dq, HEAD_DIM, 1)))
                dk_acc[prev, :] += _dot_tn(ds1b, qb)
                dk_acc[cur, :] += _dot_tn(ds2b, qb)
                dv_acc[prev, :] += _dot_tn(pr1.astype(MXU), dob)
                dv_acc[cur, :] += _dot_tn(pr2.astype(MXU), dob)
                new_rows.append(dsink_rows[hh] - jnp.exp(sink - lse_b) * delta)
            dq_ref[cur, :] = (jnp.where(_head_lanes(0), dqs[0], dqs[1]) * SCALE).astype(dq_ref.dtype)
            return tuple(new_rows)

        zero = jnp.zeros((BLOCK, 1), F32)
        rows = lax.fori_loop(0, nb, step, (zero, zero))
        for hh in range(2):
            dsink_ref[hh] = jnp.broadcast_to(jnp.sum(rows[hh], axis=0, keepdims=True), (1, LANES))

        @pl.when(p == N_PAIR - 1)
        def _():
            dk_ref[...] = dk_acc[...].astype(dk_ref.dtype)
            dv_ref[...] = dv_acc[...].astype(dv_ref.dtype)

    pair = pl.BlockSpec((s, LANES), lambda p: (0, p))
    kv_out = pl.BlockSpec((s, LANES), lambda p: (0, 0))
    bspec = pl.BlockSpec((2, BLOCK, 2 * BLOCK), lambda p: (p, 0, 0))
    return pl.pallas_call(
        body,
        grid=(N_PAIR,),
        in_specs=[pl.BlockSpec(memory_space=pltpu.SMEM),
                  pl.BlockSpec((s, LANES), lambda p: (0, q0 + p)),
                  pl.BlockSpec((s, LANES), lambda p: (0, kblk)),
                  pl.BlockSpec((s, LANES), lambda p: (0, vblk)),
                  bspec, pair,
                  pl.BlockSpec((None, 2, s, 1), lambda p: (p, 0, 0, 0))],
        out_specs=[pair, kv_out, kv_out, bspec, pl.BlockSpec((2, 1, LANES), lambda p: (p, 0, 0))],
        out_shape=[jax.ShapeDtypeStruct((s, SW_W), MXU),
                   jax.ShapeDtypeStruct((s, LANES), MXU),
                   jax.ShapeDtypeStruct((s, LANES), MXU),
                   jax.ShapeDtypeStruct((SW_HEADS, BLOCK, 2 * BLOCK), F32),
                   jax.ShapeDtypeStruct((SW_HEADS, 1, LANES), F32)],
        scratch_shapes=[pltpu.VMEM((s, LANES), F32), pltpu.VMEM((s, LANES), F32)],
        compiler_params=_cparams(),
        name=name,
    )(sinks, qkv, qkv, qkv, bias, do, lse)


def _layer_fwd(x, p, w, g_mix, g_mlp, g_pe, sinks, bias, tag):
    h1, r1 = _rms_fwd(x, g_mix, f"rms_mix_{tag}")
    qkv = _mm(h1, w["w_qkv"], out_dtypes=(MXU,), name=f"proj_qkv_{tag}")
    gates = _mm(h1, w["w_gate"], name=f"proj_gate_{tag}")
    oa = _sb_fwd(qkv, f"sb_fwd_{tag}")
    ob, lse = _swa_fwd(qkv, bias, sinks, f"swa_fwd_{tag}")
    merged = _mix_fwd(oa, ob, w["w_up_a"], w["w_up_b"], gates, f"mix_fwd_{tag}")
    x1 = _mm(merged, w["w_o"], extras=(x,), epi=lambda acc, res: res + acc, name=f"out_proj_{tag}")
    h2, r2 = _rms_fwd(x1, g_mlp, f"rms_mlp_{tag}")
    u, act = _mm(h2, w["w_ff1"], epi=lambda acc: (acc, jnp.square(jnp.maximum(acc, 0.0))),
                 out_dtypes=(F32, MXU), name=f"ff1_{tag}")
    x2 = _mm(act, w["w_ff2"], extras=(x1,), epi=lambda acc, res: res + acc, name=f"ff2_{tag}")
    h3, r3 = _rms_fwd(x2, g_pe, f"rms_pe_{tag}")
    x3 = _ple(p, w["w_pe"], h3, w["w_pg"], x2, backward=False, name=f"ple_fwd_{tag}")
    saved = dict(x=x, h1=h1, r1=r1, gates=gates, qkv=qkv, lse=lse, oa=oa, ob=ob, merged=merged,
                 x1=x1, h2=h2, r2=r2, u=u, act=act, x2=x2, h3=h3, r3=r3)
    return x3, saved


def _layer_bwd(dx3, sv, p, w, g_mix, g_mlp, g_pe, sinks, bias, tag):
    gw = {}
    wire = (WIRE,)
    dpe, dgt = _ple(p, w["w_pe"], sv["h3"], w["w_pg"], dx3, backward=True, name=f"ple_bwd_{tag}")
    gw["w_pe"] = _mm(p, dpe, ta=True, out_dtypes=wire, name=f"dw_pe_{tag}")
    gw["w_pg"] = _mm(sv["h3"], dgt, ta=True, out_dtypes=wire, name=f"dw_pg_{tag}")
    dh3 = _mm(dgt, w["w_pg"], tb=True, name=f"dh_pe_{tag}")
    dx2, dx2b, dg_pe = _rms_bwd(sv["x2"], sv["r3"], g_pe, dh3, dx3, f"rms_pe_bwd_{tag}")
    gw["w_ff2"] = _mm(sv["act"], dx2b, ta=True, out_dtypes=wire, name=f"dw_ff2_{tag}")
    du = _mm(dx2b, w["w_ff2"], tb=True, extras=(sv["u"],),
             epi=lambda acc, u: acc * (2.0 * jnp.maximum(u, 0.0)), out_dtypes=(MXU,), name=f"dact_{tag}")
    gw["w_ff1"] = _mm(sv["h2"], du, ta=True, out_dtypes=wire, name=f"dw_ff1_{tag}")
    dh2 = _mm(du, w["w_ff1"], tb=True, name=f"dh_mlp_{tag}")
    dx1, dx1b, dg_mlp = _rms_bwd(sv["x1"], sv["r2"], g_mlp, dh2, dx2, f"rms_mlp_bwd_{tag}")
    gw["w_o"] = _mm(sv["merged"], dx1b, ta=True, out_dtypes=wire, name=f"dw_o_{tag}")
    dya, dyb, dga, dgb = _mix_bwd(dx1b, w["w_o"], sv["oa"], sv["ob"], w["w_up_a"], w["w_up_b"],
                                  sv["gates"], f"mix_bwd_{tag}")
    gw["w_up_a"] = _mm(sv["oa"], dya, ta=True, out_dtypes=wire, name=f"dw_up_a_{tag}")
    gw["w_up_b"] = _mm(sv["ob"], dyb, ta=True, out_dtypes=wire, name=f"dw_up_b_{tag}")
    doa = _mm(dya, w["w_up_a"], tb=True, out_dtypes=(MXU,), name=f"do_a_{tag}")
    dob = _mm(dyb, w["w_up_b"], tb=True, out_dtypes=(MXU,), name=f"do_b_{tag}")
    dqb, dkb, dvb, dbias, dsink = _swa_bwd(sv["qkv"], bias, sinks, dob, sv["lse"], f"swa_bwd_{tag}")
    dqa, dka, dva = _sb_bwd(sv["qkv"], doa, f"sb_bwd_{tag}")
    dqkv = jnp.concatenate([dqa, dka, dva, dqb, dkb, dvb], axis=1)
    gw_qkv = _mm(sv["h1"], dqkv, ta=True, out_dtypes=wire, name=f"dw_qkv_{tag}")
    gw_ga = _mm(sv["h1"], dga, ta=True, out_dtypes=wire, name=f"dw_ga_{tag}")
    gw_gb = _mm(sv["h1"], dgb, ta=True, out_dtypes=wire, name=f"dw_gb_{tag}")
    gw["w_in"] = jnp.concatenate([gw_qkv, gw_ga, gw_gb], axis=1)
    d = dga.shape[1]
    add = lambda acc, res: res + acc
    dh1 = _mm(dqkv, w["w_qkv"], tb=True, tk=768, name=f"dh_qkv_{tag}")
    dh1 = _mm(dga, w["w_gate"][:, :d], tb=True, extras=(dh1,), epi=add, name=f"dh_ga_{tag}")
    dh1 = _mm(dgb, w["w_gate"][:, d:], tb=True, extras=(dh1,), epi=add, name=f"dh_gb_{tag}")
    dx, _, dg_mix = _rms_bwd(sv["x"], sv["r1"], g_mix, dh1, dx1, f"rms_mix_bwd_{tag}")
    small = dict(g_mix=dg_mix, g_mlp=dg_mlp, g_pe=dg_pe, sinks=dsink[:, 0, 0], dbias=dbias)
    return dx, gw, small


def _local_step(x, p, target, wfull, g_mix, g_mlp, g_pe, g_final, sinks, rel_bias, after_layer_bwd=None):
    depth = len(wfull)
    buckets = jnp.asarray(_bucket_table())
    bias = _build_bias(rel_bias, buckets, "build_bias")
    saved = []
    h = x
    for l in range(depth):
        h, sv = _layer_fwd(h, p[l], wfull[l], g_mix[l:l + 1], g_mlp[l:l + 1], g_pe[l:l + 1],
                           sinks[l], bias, f"l{l}")
        saved.append(sv)
    loss_row, dx, dg_final = _loss_head(h, g_final[None, :], target, "loss_head")
    gws = [None] * depth
    smalls = [None] * depth
    for l in reversed(range(depth)):
        dx, gws[l], smalls[l] = _layer_bwd(dx, saved[l], p[l], wfull[l], g_mix[l:l + 1], g_mlp[l:l + 1],
                                           g_pe[l:l + 1], sinks[l], bias, f"l{l}")
        if after_layer_bwd is not None:
            gws[l] = after_layer_bwd(l, gws[l])
    drel = _bias_grad([sm["dbias"] for sm in smalls], buckets, "bias_grad")[:, 0, :N_BUCKETS].T
    small = dict(
        g_mix=jnp.concatenate([sm["g_mix"] for sm in smalls], axis=0),
        g_mlp=jnp.concatenate([sm["g_mlp"] for sm in smalls], axis=0),
        g_pe=jnp.concatenate([sm["g_pe"] for sm in smalls], axis=0),
        g_final=dg_final[0],
        sinks=jnp.stack([sm["sinks"] for sm in smalls], axis=0),
        rel_bias=drel,
    )
    return loss_row, dx, gws, small


MESH_ID = pl.DeviceIdType.MESH
ANY = pl.BlockSpec(memory_space=pl.ANY)


def _position():
    return lax.axis_index("x"), lax.axis_index("y"), lax.axis_index("c")


def _all_gather(shards, name):
    n = len(shards)

    def body(*refs):
        x_refs, out_refs = refs[:n], refs[n:2 * n]
        send_sems, recv_sems, local_sems = refs[2 * n:]
        x, y, c = _position()
        me, sibling = (x, y, c), (x, y, 1 - c)
        chips = [(1 - x, y), (x, 1 - y), (1 - x, 1 - y)]

        def slot(a, px, py, pc):
            return out_refs[a].at[4 * px + 2 * py + pc]

        def copy(a, k, block, to, src=None):
            return pltpu.make_async_remote_copy(
                src_ref=slot(a, *block) if src is None else src, dst_ref=slot(a, *block),
                send_sem=send_sems.at[a, k], recv_sem=recv_sems.at[a, k],
                device_id=to, device_id_type=MESH_ID)

        mine = [pltpu.make_async_copy(x_refs[a], slot(a, *me), local_sems.at[a]) for a in range(n)]
        for cp in mine:
            cp.start()
        first = []
        for a in range(n):
            first.append(copy(a, 0, me, sibling, src=x_refs[a]))
            first += [copy(a, 1 + j, me, (*chip, c), src=x_refs[a]) for j, chip in enumerate(chips)]
        for cp in first:
            cp.start()
        passed = []
        for j, chip in enumerate(chips):
            for a in range(n):
                copy(a, 1 + j, (*chip, c), me).wait_recv()
                fwd = copy(a, 4 + j, (*chip, c), sibling)
                fwd.start()
                passed.append(fwd)
        for a in range(n):
            copy(a, 0, sibling, me).wait_recv()
            for j, chip in enumerate(chips):
                copy(a, 4 + j, (*chip, 1 - c), me).wait_recv()
        for cp in first + passed:
            cp.wait_send()
        for cp in mine:
            cp.wait()

    return pl.pallas_call(
        body,
        out_shape=[jax.ShapeDtypeStruct((N_DEV,) + s.shape, s.dtype) for s in shards],
        in_specs=[ANY] * n,
        out_specs=[ANY] * n,
        scratch_shapes=[pltpu.SemaphoreType.DMA((n, 7)), pltpu.SemaphoreType.DMA((n, 7)),
                        pltpu.SemaphoreType.DMA((n,))],
        name=name,
    )(*shards)


def _rs_sibling(gs, name):
    n = len(gs)

    def body(*refs):
        g_refs, out_refs = refs[:n], refs[n:2 * n]
        send_sems, recv_sems = refs[2 * n:]
        x, y, c = _position()
        copies = []
        for a in range(n):
            for j in range(4):
                copies.append(pltpu.make_async_remote_copy(
                    src_ref=g_refs[a].at[2 * j + (1 - c)], dst_ref=out_refs[a].at[j],
                    send_sem=send_sems.at[a, j], recv_sem=recv_sems.at[a, j],
                    device_id=(x, y, 1 - c), device_id_type=MESH_ID))
        for cp in copies:
            cp.start()
        for cp in copies:
            cp.wait()

    return pl.pallas_call(
        body,
        out_shape=[jax.ShapeDtypeStruct((4,) + g.shape[1:], g.dtype) for g in gs],
        in_specs=[ANY] * n,
        out_specs=[ANY] * n,
        scratch_shapes=[pltpu.SemaphoreType.DMA((n, 4)), pltpu.SemaphoreType.DMA((n, 4))],
        name=name,
    )(*gs)


def _chip_of(k, x, y):
    return x ^ ((k + 1) & 1), y ^ (((k + 1) >> 1) & 1)


def _chip_partials(pos, gs, recvs, name):
    n = len(gs)

    def body(pos_ref, *refs):
        for a in range(n):
            refs[2 * n + a][...] = (refs[a][...].astype(F32) + refs[n + a][...].astype(F32)
                                    ).astype(refs[2 * n + a].dtype)

    def g_map(k, pos_ref):
        cx, cy = _chip_of(k, pos_ref[0], pos_ref[1])
        return (4 * cx + 2 * cy + pos_ref[2], 0, 0)

    def r_map(k, pos_ref):
        cx, cy = _chip_of(k, pos_ref[0], pos_ref[1])
        return (2 * cx + cy, 0, 0)

    slab = [(None,) + g.shape[1:] for g in gs]
    return pl.pallas_call(
        body,
        grid_spec=pltpu.PrefetchScalarGridSpec(
            num_scalar_prefetch=1,
            grid=(4,),
            in_specs=[pl.BlockSpec(sh, g_map) for sh in slab] + [pl.BlockSpec(sh, r_map) for sh in slab],
            out_specs=[pl.BlockSpec(sh, lambda k, pos_ref: (k, 0, 0)) for sh in slab],
        ),
        out_shape=[jax.ShapeDtypeStruct((4,) + g.shape[1:], g.dtype) for g in gs],
        compiler_params=_cparams(),
        name=name,
    )(pos, *gs, *recvs)


def _rs_chips(parts, name):
    n = len(parts)

    def body(*refs):
        p_refs, out_refs = refs[:n], refs[n:2 * n]
        send_sems, recv_sems = refs[2 * n:]
        x, y, c = _position()
        copies = []
        for a in range(n):
            for k in range(3):
                cx, cy = _chip_of(k, x, y)
                copies.append(pltpu.make_async_remote_copy(
                    src_ref=p_refs[a].at[k], dst_ref=out_refs[a].at[k],
                    send_sem=send_sems.at[a, k], recv_sem=recv_sems.at[a, k],
                    device_id=(cx, cy, c), device_id_type=MESH_ID))
        for cp in copies:
            cp.start()
        for cp in copies:
            cp.wait()

    return pl.pallas_call(
        body,
        out_shape=[jax.ShapeDtypeStruct((3,) + p.shape[1:], p.dtype) for p in parts],
        in_specs=[ANY] * n,
        out_specs=[ANY] * n,
        scratch_shapes=[pltpu.SemaphoreType.DMA((n, 3)), pltpu.SemaphoreType.DMA((n, 3))],
        name=name,
    )(*parts)


def _adamw_math(w, g, m, v):
    m = ADAM_B1 * m + (1.0 - ADAM_B1) * g
    v = ADAM_B2 * v + (1.0 - ADAM_B2) * (g * g)
    m_hat = m / (1.0 - ADAM_B1 ** ADAM_STEP)
    v_hat = v / (1.0 - ADAM_B2 ** ADAM_STEP)
    delta = -ADAM_LR * (m_hat / (jnp.sqrt(v_hat) + ADAM_EPS) + ADAM_WD * w)
    return delta, m, v


def _adamw_weight(parts, recvs, w, m, v, name):
    depth, a, b = w.shape
    ta = _tile(a, 256, unit=16)
    ni = a // ta

    def body(*refs):
        p_refs, r_refs = refs[:depth], refs[depth:2 * depth]
        w_ref, m_ref, v_ref = refs[2 * depth:2 * depth + 3]
        g_out, d_out, m_out, v_out = refs[2 * depth + 3:]
        layer = pl.program_id(0)
        g = jnp.zeros((ta, b), F32)
        for l in range(depth):
            gl = p_refs[l][...].astype(F32)
            for k in range(3):
                gl = gl + r_refs[l][k].astype(F32)
            g = jnp.where(layer == l, gl, g)
        delta, m_new, v_new = _adamw_math(w_ref[...], g, m_ref[...], v_ref[...])
        g_out[...] = g
        d_out[...] = delta
        m_out[...] = m_new
        v_out[...] = v_new

    def hold(l):
        return lambda layer, i: jnp.where(layer == l, i, jnp.where(layer < l, 0, ni - 1))

    p_specs = [pl.BlockSpec((None, ta, b), (lambda layer, i, f=hold(l): (3, f(layer, i), 0))) for l in range(depth)]
    r_specs = [pl.BlockSpec((3, ta, b), (lambda layer, i, f=hold(l): (0, f(layer, i), 0))) for l in range(depth)]
    row = pl.BlockSpec((None, ta, b), lambda layer, i: (layer, i, 0))
    return pl.pallas_call(
        body,
        grid=(depth, ni),
        in_specs=p_specs + r_specs + [row, row, row],
        out_specs=[row] * 4,
        out_shape=[jax.ShapeDtypeStruct(w.shape, F32)] * 4,
        compiler_params=_cparams(),
        name=name,
    )(*parts, *recvs, w, m, v)


def _adamw_replicated(gathered, w, m, v, name):
    r, lanes = w.shape

    def body(g_ref, w_ref, m_ref, v_ref, g_out, d_out, m_out, v_out):
        g = g_ref[0]
        for k in range(1, N_DEV):
            g = g + g_ref[k]
        delta, m_new, v_new = _adamw_math(w_ref[...], g, m_ref[...], v_ref[...])
        g_out[...] = g
        d_out[...] = delta
        m_out[...] = m_new
        v_out[...] = v_new

    return pl.pallas_call(
        body,
        out_shape=[jax.ShapeDtypeStruct((r, lanes), F32)] * 4,
        name=name,
    )(gathered, w, m, v)


def _full_weight(name, gathered, l):
    _, _, a, b = gathered.shape
    if name in COL_SHARDED:
        return gathered[:, l].transpose(1, 0, 2).reshape(a, N_DEV * b)
    return gathered[:, l].reshape(N_DEV * a, b)


def _to_slabs(name, gfull, shard_shape):
    a, b = shard_shape
    if name in COL_SHARDED:
        return gfull.reshape(a, N_DEV, b).transpose(1, 0, 2)
    return gfull.reshape(N_DEV, a, b)


def _pack_small(arrs):
    rows = []
    for a in arrs:
        flat = a.astype(F32).reshape(-1)
        pad = (-flat.shape[0]) % LANES
        rows.append(jnp.pad(flat, (0, pad)).reshape(-1, LANES))
    packed = jnp.concatenate(rows, axis=0)
    return jnp.pad(packed, ((0, (-packed.shape[0]) % 8), (0, 0)))


def _unpack_small(packed, shapes):
    out, off = [], 0
    for shp in shapes:
        n = math.prod(shp)
        rows = -(-n // LANES)
        out.append(packed[off:off + rows].reshape(-1)[:n].reshape(shp))
        off += rows
    return out


def kernel(x, p, w_in, w_up_a, w_up_b, w_o, w_ff1, w_ff2, w_pe, w_pg, g_mix, g_mlp, g_pe, g_final, sinks, rel_bias, loss_target, m_w_in, m_w_up_a, m_w_up_b, m_w_o, m_w_ff1, m_w_ff2, m_w_pe, m_w_pg, m_g_mix, m_g_mlp, m_g_pe, m_g_final, m_sinks, m_rel_bias, v_w_in, v_w_up_a, v_w_up_b, v_w_o, v_w_ff1, v_w_ff2, v_w_pe, v_w_pg, v_g_mix, v_g_mlp, v_g_pe, v_g_final, v_sinks, v_rel_bias):
    w_sh = [w_in, w_up_a, w_up_b, w_o, w_ff1, w_ff2, w_pe, w_pg]
    m_sh = [m_w_in, m_w_up_a, m_w_up_b, m_w_o, m_w_ff1, m_w_ff2, m_w_pe, m_w_pg]
    v_sh = [v_w_in, v_w_up_a, v_w_up_b, v_w_o, v_w_ff1, v_w_ff2, v_w_pe, v_w_pg]
    depth = w_in.shape[0]
    d_model = x.shape[-1]

    gathered = _all_gather([w.astype(WIRE) for w in w_sh], "gather_weights")
    wfull = []
    for l in range(depth):
        wl = {n: _full_weight(n, g, l) for n, g in zip(WEIGHTS, gathered)}
        w_all = wl.pop("w_in")
        wl["w_qkv"] = w_all[:, :QKV_COLS]
        wl["w_gate"] = w_all[:, QKV_COLS:]
        assert wl["w_gate"].shape[1] == 2 * d_model
        wfull.append(wl)

    px, py, pc = _position()
    pos = jnp.stack([px, py, pc]).astype(jnp.int32)

    def reduce_layer(l, gw):
        slabs = [_to_slabs(n, gw[n], w.shape[1:]) for n, w in zip(WEIGHTS, w_sh)]
        from_sibling = _rs_sibling(slabs, f"reduce_sibling_l{l}")
        parts = _chip_partials(pos, slabs, from_sibling, f"chip_partials_l{l}")
        return parts, _rs_chips(parts, f"reduce_chips_l{l}")

    loss_row, grad_x, reduced, small = _local_step(
        x[0], p[:, 0], loss_target[0], wfull, g_mix, g_mlp, g_pe, g_final, sinks, rel_bias,
        after_layer_bwd=reduce_layer)

    grad_w, delta_w, new_m, new_v = [], [], [], []
    for a, name in enumerate(WEIGHTS):
        outs = _adamw_weight([reduced[l][0][a] for l in range(depth)], [reduced[l][1][a] for l in range(depth)],
                             w_sh[a], m_sh[a], v_sh[a], f"adamw_{name}")
        for lst, o in zip((grad_w, delta_w, new_m, new_v), outs):
            lst.append(o)

    small_w = [g_mix, g_mlp, g_pe, g_final, sinks, rel_bias]
    small_m = [m_g_mix, m_g_mlp, m_g_pe, m_g_final, m_sinks, m_rel_bias]
    small_v = [v_g_mix, v_g_mlp, v_g_pe, v_g_final, v_sinks, v_rel_bias]
    small_shapes = [a.shape for a in small_w] + [(1,)]
    zero = jnp.zeros((1,), F32)
    small_g = _pack_small([small[n] for n in SMALL] + [loss_row[0, :1]])
    small_all = _all_gather([small_g], "gather_small")[0]
    packed_s = _adamw_replicated(small_all, _pack_small(small_w + [zero]), _pack_small(small_m + [zero]),
                                 _pack_small(small_v + [zero + 1.0]), "adamw_replicated")
    sg, sd, sm, sv = [_unpack_small(t, small_shapes) for t in packed_s]
    loss = sg[-1][0]

    return (loss, grad_x[None], *grad_w, *sg[:-1], *delta_w, *sd[:-1], *new_m, *sm[:-1], *new_v, *sv[:-1])
```

```python
import functools
import math

import numpy as np
import jax
import jax.numpy as jnp
from jax import lax
from jax.experimental import pallas as pl
from jax.experimental.pallas import tpu as pltpu

F32 = jnp.float32
MXU = jnp.bfloat16
WIRE = jnp.bfloat16

HEAD_DIM = 64
SB_HEADS = 8
SW_HEADS = 8
SW_KV = 2
SW_GROUP = SW_HEADS // SW_KV
BLOCK = 128
N_BUCKETS = 32
MAX_DISTANCE = 128
EPS = 1e-6
SCALE = HEAD_DIM ** -0.5
SB_W = SB_HEADS * HEAD_DIM
SW_W = SW_HEADS * HEAD_DIM
QKV_COLS = 3 * SB_W + SW_W + 2 * SW_KV * HEAD_DIM
N_DEV = 8
LANES = 128
N_PAIR = SB_HEADS // 2
NEG = -1e30

ADAM_LR = 0.001
ADAM_B1 = 0.9
ADAM_B2 = 0.999
ADAM_EPS = 1e-08
ADAM_WD = 0.01
ADAM_STEP = 10

VMEM_LIMIT = 48 * 1024 * 1024
SB_TQ = 256
SB_DEAD = -105.0

WEIGHTS = ("w_in", "w_up_a", "w_up_b", "w_o", "w_ff1", "w_ff2", "w_pe", "w_pg")
COL_SHARDED = ("w_in", "w_up_a", "w_up_b", "w_ff1", "w_pe")
SMALL = ("g_mix", "g_mlp", "g_pe", "g_final", "sinks", "rel_bias")


def _cparams(**kw):
    return pltpu.CompilerParams(vmem_limit_bytes=VMEM_LIMIT, **kw)


def _dot(a, b):
    return jnp.dot(a, b, preferred_element_type=F32)


def _dot_nt(a, b):
    return lax.dot_general(a, b, (((1,), (1,)), ((), ())), preferred_element_type=F32)


def _dot_tn(a, b):
    return lax.dot_general(a, b, (((0,), (0,)), ((), ())), preferred_element_type=F32)


def _tile(n, target, unit=LANES):
    if n <= target:
        return n
    t = (target // unit) * unit
    while t > unit and n % t:
        t -= unit
    assert n % t == 0, (n, target)
    return t


def _sigmoid(x):
    return 1.0 / (1.0 + jnp.exp(-x))


class _Comm:
    def __init__(self, inputs, out_shapes, sems, start, finish):
        self.inputs, self.out_shapes, self.sems = list(inputs), list(out_shapes), list(sems)
        self.start, self.finish = start, finish


def _call(body, *, grid, in_specs, out_specs, out_shape, scratch_shapes=(), args, name, comm=None):
    n_in, n_out, n_scr = len(in_specs), len(out_shape), len(scratch_shapes)
    if comm is None:
        outs = pl.pallas_call(body, grid=grid, in_specs=list(in_specs), out_specs=list(out_specs),
                              out_shape=list(out_shape), scratch_shapes=list(scratch_shapes),
                              compiler_params=_cparams(), name=name)(*args)
        return list(outs), None
    ci, co = len(comm.inputs), len(comm.out_shapes)
    any_spec = pl.BlockSpec(memory_space=pl.ANY)

    def wrapped(*refs):
        ins, cin = refs[:n_in], refs[n_in:n_in + ci]
        o0 = n_in + ci
        outs, cout = refs[o0:o0 + n_out], refs[o0 + n_out:o0 + n_out + co]
        s0 = o0 + n_out + co
        scr, csem = refs[s0:s0 + n_scr], refs[s0 + n_scr:]
        ids = [pl.program_id(d) for d in range(len(grid))]
        first = functools.reduce(jnp.logical_and, [i == 0 for i in ids])
        last = functools.reduce(jnp.logical_and, [i == g - 1 for i, g in zip(ids, grid)])
        pos = (lax.axis_index("x"), lax.axis_index("y"), lax.axis_index("c"))

        @pl.when(first)
        def _():
            comm.start(pos, cin, cout, csem)

        body(*ins, *outs, *scr)

        @pl.when(last)
        def _():
            comm.finish(pos, cin, cout, csem)

    outs = pl.pallas_call(wrapped, grid=grid, in_specs=list(in_specs) + [any_spec] * ci,
                          out_specs=list(out_specs) + [any_spec] * co,
                          out_shape=list(out_shape) + comm.out_shapes,
                          scratch_shapes=list(scratch_shapes) + comm.sems,
                          compiler_params=_cparams(), name=name)(*args, *comm.inputs)
    return list(outs[:n_out]), list(outs[n_out:])


def _mm(a, b, *, ta=False, tb=False, extras=(), epi=None, out_dtypes=(F32,),
        tm=1024, tn=1024, tk=512, name, comm=None):
    if ta:
        kdim, m = a.shape
    else:
        m, kdim = a.shape
    n = b.shape[0] if tb else b.shape[1]
    assert (b.shape[1] if tb else b.shape[0]) == kdim
    tm, tn, tk = _tile(m, tm), _tile(n, tn), _tile(kdim, tk)
    nk = kdim // tk
    n_ex, n_out = len(extras), len(out_dtypes)

    a_spec = (pl.BlockSpec((tk, tm), lambda i, j, k: (k, i)) if ta
              else pl.BlockSpec((tm, tk), lambda i, j, k: (i, k)))
    b_spec = (pl.BlockSpec((tn, tk), lambda i, j, k: (j, k)) if tb
              else pl.BlockSpec((tk, tn), lambda i, j, k: (k, j)))
    ex_specs = []
    for e in extras:
        assert e.shape == (m, n), (e.shape, m, n)
        ex_specs.append(pl.BlockSpec((tm, tn), lambda i, j, k: (i, j)))
    out_spec = pl.BlockSpec((tm, tn), lambda i, j, k: (i, j))

    def body(a_ref, b_ref, *rest):
        ex_refs = rest[:n_ex]
        out_refs = rest[n_ex:n_ex + n_out]
        acc = rest[-1]
        k = pl.program_id(2)

        @pl.when(k == 0)
        def _():
            acc[...] = jnp.zeros_like(acc)

        av = a_ref[...].astype(MXU)
        bv = b_ref[...].astype(MXU)
        if ta:
            acc[...] += _dot_tn(av, bv)
        elif tb:
            acc[...] += _dot_nt(av, bv)
        else:
            acc[...] += _dot(av, bv)

        @pl.when(k == nk - 1)
        def _():
            res = acc[...]
            if epi is not None:
                res = epi(res, *[e[...] for e in ex_refs])
            if not isinstance(res, tuple):
                res = (res,)
            for o_ref, r in zip(out_refs, res):
                o_ref[...] = r.astype(o_ref.dtype)

    outs, couts = _call(
        body,
        grid=(m // tm, n // tn, nk),
        in_specs=[a_spec, b_spec] + ex_specs,
        out_specs=[out_spec] * n_out,
        out_shape=[jax.ShapeDtypeStruct((m, n), dt) for dt in out_dtypes],
        scratch_shapes=[pltpu.VMEM((tm, tn), F32)],
        args=(a, b, *extras), name=name, comm=comm)
    res = outs[0] if n_out == 1 else tuple(outs)
    return res if comm is None else (res, couts)


def _rms_fwd(x, g, name):
    s, d = x.shape
    tr = _tile(s, 256)

    def body(x_ref, g_ref, h_ref, r_ref):
        xf = x_ref[...]
        r = lax.rsqrt(jnp.mean(xf * xf, axis=-1, keepdims=True) + EPS)
        h_ref[...] = ((xf * r) * g_ref[...]).astype(h_ref.dtype)
        r_ref[...] = r

    return pl.pallas_call(
        body,
        grid=(s // tr,),
        in_specs=[pl.BlockSpec((tr, d), lambda i: (i, 0)), pl.BlockSpec((1, d), lambda i: (0, 0))],
        out_specs=[pl.BlockSpec((tr, d), lambda i: (i, 0)), pl.BlockSpec((tr, 1), lambda i: (i, 0))],
        out_shape=[jax.ShapeDtypeStruct((s, d), MXU), jax.ShapeDtypeStruct((s, 1), F32)],
        compiler_params=_cparams(),
        name=name,
    )(x, g)


def _rms_bwd(x, r, g, dh, dres, name):
    s, d = x.shape
    tr = _tile(s, 256)

    def body(x_ref, r_ref, g_ref, dh_ref, dres_ref, dx_ref, dxb_ref, dg_ref):
        @pl.when(pl.program_id(0) == 0)
        def _():
            dg_ref[...] = jnp.zeros_like(dg_ref)

        rr = r_ref[...]
        xhat = x_ref[...] * rr
        dh_v = dh_ref[...]
        dxhat = dh_v * g_ref[...]
        mean = jnp.mean(dxhat * xhat, axis=-1, keepdims=True)
        dx = dres_ref[...] + rr * (dxhat - xhat * mean)
        dx_ref[...] = dx
        dxb_ref[...] = dx.astype(dxb_ref.dtype)
        dg_ref[...] += jnp.sum(dh_v * xhat, axis=0, keepdims=True)

    row = pl.BlockSpec((tr, d), lambda i: (i, 0))
    vec = pl.BlockSpec((1, d), lambda i: (0, 0))
    return pl.pallas_call(
        body,
        grid=(s // tr,),
        in_specs=[row, pl.BlockSpec((tr, 1), lambda i: (i, 0)), vec, row, row],
        out_specs=[row, row, vec],
        out_shape=[jax.ShapeDtypeStruct((s, d), F32), jax.ShapeDtypeStruct((s, d), MXU),
                   jax.ShapeDtypeStruct((1, d), F32)],
        compiler_params=_cparams(),
        name=name,
    )(x, r, g, dh, dres)


def _loss_head(x, g, target, name):
    s, d = x.shape
    tr = _tile(s, 256)

    def body(x_ref, g_ref, t_ref, loss_ref, dx_ref, dg_ref):
        @pl.when(pl.program_id(0) == 0)
        def _():
            dg_ref[...] = jnp.zeros_like(dg_ref)
            loss_ref[...] = jnp.zeros_like(loss_ref)

        xf = x_ref[...]
        gv = g_ref[...]
        r = lax.rsqrt(jnp.mean(xf * xf, axis=-1, keepdims=True) + EPS)
        xhat = xf * r
        err = xhat * gv - t_ref[...]
        loss_ref[...] += 0.5 * jnp.sum(jnp.mean(err * err, axis=-1, keepdims=True), axis=0, keepdims=True)
        dy = err * (1.0 / d)
        dxhat = dy * gv
        mean = jnp.mean(dxhat * xhat, axis=-1, keepdims=True)
        dx_ref[...] = r * (dxhat - xhat * mean)
        dg_ref[...] += jnp.sum(dy * xhat, axis=0, keepdims=True)

    row = pl.BlockSpec((tr, d), lambda i: (i, 0))
    vec = pl.BlockSpec((1, d), lambda i: (0, 0))
    return pl.pallas_call(
        body,
        grid=(s // tr,),
        in_specs=[row, vec, row],
        out_specs=[pl.BlockSpec((1, LANES), lambda i: (0, 0)), row, vec],
        out_shape=[jax.ShapeDtypeStruct((1, LANES), F32), jax.ShapeDtypeStruct((s, d), F32),
                   jax.ShapeDtypeStruct((1, d), F32)],
        compiler_params=_cparams(),
        name=name,
    )(x, g, target)


def _mix_fwd(oa, ob, wa, wb, gates, name):
    s, kd = oa.shape
    d = wa.shape[1]
    tm, tn = _tile(s, 1024), _tile(d, 512)
    nj = d // tn

    def body(oa_ref, ob_ref, wa_ref, wb_ref, ga_ref, gb_ref, out_ref):
        ya = _dot(oa_ref[...], wa_ref[...])
        yb = _dot(ob_ref[...], wb_ref[...])
        out_ref[...] = (_sigmoid(ga_ref[...]) * ya + _sigmoid(gb_ref[...]) * yb).astype(out_ref.dtype)

    o_spec = pl.BlockSpec((tm, kd), lambda i, j: (i, 0))
    w_spec = pl.BlockSpec((kd, tn), lambda i, j: (0, j))
    return pl.pallas_call(
        body,
        grid=(s // tm, nj),
        in_specs=[o_spec, o_spec, w_spec, w_spec,
                  pl.BlockSpec((tm, tn), lambda i, j: (i, j)),
                  pl.BlockSpec((tm, tn), lambda i, j: (i, j + nj))],
        out_specs=pl.BlockSpec((tm, tn), lambda i, j: (i, j)),
        out_shape=jax.ShapeDtypeStruct((s, d), MXU),
        compiler_params=_cparams(),
        name=name,
    )(oa, ob, wa, wb, gates, gates)


def _mix_bwd(dx, w_o, oa, ob, wa, wb, gates, name):
    s, kd = oa.shape
    d = wa.shape[1]
    tm, tn = _tile(s, 1024), _tile(d, 512)
    nj = d // tn

    def body(dx_ref, wo_ref, oa_ref, ob_ref, wa_ref, wb_ref, ga_ref, gb_ref,
             dya_ref, dyb_ref, dga_ref, dgb_ref):
        dm = _dot_nt(dx_ref[...], wo_ref[...])
        ya = _dot(oa_ref[...], wa_ref[...])
        yb = _dot(ob_ref[...], wb_ref[...])
        sa = _sigmoid(ga_ref[...])
        sb = _sigmoid(gb_ref[...])
        dya_ref[...] = (dm * sa).astype(dya_ref.dtype)
        dyb_ref[...] = (dm * sb).astype(dyb_ref.dtype)
        dga_ref[...] = (dm * ya * sa * (1.0 - sa)).astype(dga_ref.dtype)
        dgb_ref[...] = (dm * yb * sb * (1.0 - sb)).astype(dgb_ref.dtype)

    o_spec = pl.BlockSpec((tm, kd), lambda i, j: (i, 0))
    w_spec = pl.BlockSpec((kd, tn), lambda i, j: (0, j))
    t_spec = pl.BlockSpec((tm, tn), lambda i, j: (i, j))
    return pl.pallas_call(
        body,
        grid=(s // tm, nj),
        in_specs=[pl.BlockSpec((tm, d), lambda i, j: (i, 0)),
                  pl.BlockSpec((tn, d), lambda i, j: (j, 0)),
                  o_spec, o_spec, w_spec, w_spec, t_spec,
                  pl.BlockSpec((tm, tn), lambda i, j: (i, j + nj))],
        out_specs=[t_spec] * 4,
        out_shape=[jax.ShapeDtypeStruct((s, d), MXU)] * 4,
        compiler_params=_cparams(),
        name=name,
    )(dx, w_o, oa, ob, wa, wb, gates, gates)


def _ple(p, w_pe, h, w_pg, other, *, backward, name):
    s, kp = p.shape
    d = w_pe.shape[1]
    tm, tn = _tile(s, 1024), _tile(d, 512)

    def body(p_ref, wpe_ref, h_ref, wpg_ref, other_ref, *out_refs):
        pe = _dot(p_ref[...].astype(MXU), wpe_ref[...])
        gt = _dot(h_ref[...], wpg_ref[...])
        sg = _sigmoid(gt)
        if backward:
            dout = other_ref[...]
            out_refs[0][...] = (dout * sg).astype(out_refs[0].dtype)
            out_refs[1][...] = (dout * pe * sg * (1.0 - sg)).astype(out_refs[1].dtype)
        else:
            out_refs[0][...] = other_ref[...] + pe * sg

    t_spec = pl.BlockSpec((tm, tn), lambda i, j: (i, j))
    if backward:
        out_specs, out_shape = [t_spec, t_spec], [jax.ShapeDtypeStruct((s, d), MXU)] * 2
    else:
        out_specs, out_shape = [t_spec], [jax.ShapeDtypeStruct((s, d), F32)]
    outs = pl.pallas_call(
        body,
        grid=(s // tm, d // tn),
        in_specs=[pl.BlockSpec((tm, kp), lambda i, j: (i, 0)),
                  pl.BlockSpec((kp, tn), lambda i, j: (0, j)),
                  pl.BlockSpec((tm, d), lambda i, j: (i, 0)),
                  pl.BlockSpec((d, tn), lambda i, j: (0, j)),
                  t_spec],
        out_specs=out_specs,
        out_shape=out_shape,
        compiler_params=_cparams(),
        name=name,
    )(p, w_pe, h, w_pg, other)
    return tuple(outs) if backward else outs[0]


def _split_dot(x, tri):
    hi = x.astype(jnp.bfloat16)
    r1 = x - hi.astype(F32)
    mid = r1.astype(jnp.bfloat16)
    lo = (r1 - mid.astype(F32)).astype(jnp.bfloat16)
    return _dot(hi, tri) + _dot(mid, tri) + _dot(lo, tri)


def _log_sigmoids(z):
    t = jnp.log1p(jnp.exp(-jnp.abs(z)))
    return jnp.minimum(z, 0.0) - t, jnp.minimum(-z, 0.0) - t


def _head_lanes(hh):
    lane = lax.broadcasted_iota(jnp.int32, (1, LANES), 1)
    return jnp.logical_and(lane >= hh * HEAD_DIM, lane < (hh + 1) * HEAD_DIM)


def _sb_fwd(qkv, name, comm=None):
    s = qkv.shape[0]
    tq = _tile(s, SB_TQ)

    def body(q_ref, k_ref, v_ref, o_ref):
        i = pl.program_id(1)
        qf = q_ref[...].astype(F32) * SCALE
        row = lax.broadcasted_iota(jnp.int32, (tq, tq), 0)
        col = lax.broadcasted_iota(jnp.int32, (tq, tq), 1)
        causal = col < row
        tri = jnp.where(row > col, 1.0, 0.0).astype(jnp.bfloat16)

        def one_head(hh):
            qm = jnp.where(_head_lanes(hh), qf, 0.0).astype(MXU)

            def block(kb, c, acc, masked):
                rows = pl.ds(pl.multiple_of(kb * tq, tq), tq)
                z = _dot_nt(qm, k_ref[rows, :])
                lb, lm = _log_sigmoids(z)
                if masked:
                    lm = jnp.where(causal, lm, 0.0)
                a = jnp.exp(lb + _split_dot(lm, tri) + c)
                if masked:
                    a = jnp.where(causal, a, 0.0)
                acc = acc + _dot(a.astype(MXU), v_ref[rows, :])
                c = c + jnp.sum(lm, axis=1, keepdims=True)
                return c, acc

            c, acc = block(i, jnp.zeros((tq, 1), F32), jnp.zeros((tq, LANES), F32), True)

            def live(st):
                return jnp.logical_and(st[0] >= 0, st[1] > SB_DEAD)

            def walk(st):
                c, acc = block(st[0], st[2], st[3], False)
                return st[0] - 1, jnp.max(c), c, acc

            return lax.while_loop(live, walk, (i - 1, jnp.max(c), c, acc))[3]

        o_ref[...] = jnp.where(_head_lanes(0), one_head(0), one_head(1)).astype(o_ref.dtype)

    outs, couts = _call(
        body,
        grid=(N_PAIR, s // tq),
        in_specs=[pl.BlockSpec((tq, LANES), lambda p, i: (i, p)),
                  pl.BlockSpec((s, LANES), lambda p, i: (0, N_PAIR + p)),
                  pl.BlockSpec((s, LANES), lambda p, i: (0, 2 * N_PAIR + p))],
        out_specs=[pl.BlockSpec((tq, LANES), lambda p, i: (i, p))],
        out_shape=[jax.ShapeDtypeStruct((s, SB_W), MXU)],
        args=(qkv, qkv, qkv), name=name, comm=comm)
    return outs[0] if comm is None else (outs[0], couts)


def _sb_bwd(qkv, do, name, comm=None):
    s = qkv.shape[0]
    tq = _tile(s, SB_TQ)
    nq = s // tq

    def body(q_ref, k_ref, v_ref, do_ref, dq_ref, dk_ref, dv_ref, dk_acc, dv_acc, carries):
        i = pl.program_id(1)

        @pl.when(i == 0)
        def _():
            dk_acc[...] = jnp.zeros_like(dk_acc)
            dv_acc[...] = jnp.zeros_like(dv_acc)

        qf = q_ref[...].astype(F32) * SCALE
        dof = do_ref[...]
        row = lax.broadcasted_iota(jnp.int32, (tq, tq), 0)
        col = lax.broadcasted_iota(jnp.int32, (tq, tq), 1)
        causal = col < row
        tri_rev = jnp.where(row > col, 1.0, 0.0).astype(jnp.bfloat16)
        tri_excl = jnp.where(row < col, 1.0, 0.0).astype(jnp.bfloat16)

        def one_head(hh):
            in_head = _head_lanes(hh)
            qm = jnp.where(in_head, qf, 0.0).astype(MXU)
            dom = jnp.where(in_head, dof, jnp.zeros_like(dof))

            def row_sum(kb, masked):
                rows = pl.ds(pl.multiple_of(kb * tq, tq), tq)
                _, lm = _log_sigmoids(_dot_nt(qm, k_ref[rows, :]))
                if masked:
                    lm = jnp.where(causal, lm, 0.0)
                return jnp.sum(lm, axis=1, keepdims=True)

            def live(st):
                return jnp.logical_and(st[0] >= 0, st[1] > SB_DEAD)

            def record(st):
                kb, c = st[0], st[2]
                carries[kb] = c
                c = c + row_sum(kb, False)
                return kb - 1, jnp.max(c), c

            c_diag = row_sum(i, True)
            first = lax.while_loop(live, record, (i - 1, jnp.max(c_diag), c_diag))[0] + 1

            def block(kb, c, gpre, dq, masked):
                rows = pl.ds(pl.multiple_of(kb * tq, tq), tq)
                ks = k_ref[rows, :]
                vs = v_ref[rows, :]
                z = _dot_nt(qm, ks)
                lb, lm = _log_sigmoids(z)
                if masked:
                    lm = jnp.where(causal, lm, 0.0)
                a = jnp.exp(lb + _split_dot(lm, tri_rev) + c)
                if masked:
                    a = jnp.where(causal, a, 0.0)
                g = a * _dot_nt(dom, vs)
                gsum = gpre + _split_dot(g, tri_excl)
                dz = g - (g + gsum) * jnp.exp(lb)
                if masked:
                    dz = jnp.where(causal, dz, 0.0)
                dzb = dz.astype(MXU)
                dq = dq + _dot(dzb, ks)
                dk_acc[rows, :] += _dot_tn(dzb, qm)
                dv_acc[rows, :] += _dot_tn(a.astype(MXU), dom)
                gpre = gpre + jnp.sum(g, axis=1, keepdims=True)
                return gpre, dq

            zero = jnp.zeros((tq, 1), F32)
            gpre, dq = lax.fori_loop(first, i, lambda kb, cr: block(kb, carries[kb], cr[0], cr[1], False),
                                     (zero, jnp.zeros((tq, LANES), F32)))
            return block(i, zero, gpre, dq, True)[1]

        dq_ref[...] = (jnp.where(_head_lanes(0), one_head(0), one_head(1)) * SCALE).astype(dq_ref.dtype)

        @pl.when(i == nq - 1)
        def _():
            dk_ref[...] = dk_acc[...].astype(dk_ref.dtype)
            dv_ref[...] = dv_acc[...].astype(dv_ref.dtype)

    blk = pl.BlockSpec((tq, LANES), lambda p, i: (i, p))
    full = pl.BlockSpec((s, LANES), lambda p, i: (0, p))
    outs, couts = _call(
        body,
        grid=(N_PAIR, nq),
        in_specs=[blk,
                  pl.BlockSpec((s, LANES), lambda p, i: (0, N_PAIR + p)),
                  pl.BlockSpec((s, LANES), lambda p, i: (0, 2 * N_PAIR + p)),
                  blk],
        out_specs=[blk, full, full],
        out_shape=[jax.ShapeDtypeStruct((s, SB_W), MXU)] * 3,
        scratch_shapes=[pltpu.VMEM((s, LANES), F32), pltpu.VMEM((s, LANES), F32),
                        pltpu.VMEM((nq, tq, 1), F32)],
        args=(qkv, qkv, qkv, do), name=name, comm=comm)
    return tuple(outs) if comm is None else (tuple(outs), couts)


def _bucket_table():
    i = np.arange(BLOCK)[:, None]
    j = np.arange(2 * BLOCK)[None, :]
    d = np.maximum(BLOCK + i - j, 0)
    max_exact = N_BUCKETS // 2
    df = np.maximum(d, 1).astype(np.float32)
    large = max_exact + (np.log(df / max_exact) / math.log(MAX_DISTANCE / max_exact)
                         * (N_BUCKETS - max_exact)).astype(np.int32)
    large = np.minimum(large, N_BUCKETS - 1)
    return np.where(d < max_exact, d, large).astype(np.int32)


def _build_bias(rel_bias, buckets, name):
    def body(rb_ref, bk_ref, out_ref):
        h = pl.program_id(0)
        bk = bk_ref[...]
        acc = jnp.zeros(bk.shape, F32)
        for b in range(N_BUCKETS):
            acc = jnp.where(bk == b, rb_ref[b, h], acc)
        out_ref[...] = acc

    return pl.pallas_call(
        body,
        grid=(SW_HEADS,),
        in_specs=[pl.BlockSpec(memory_space=pltpu.SMEM),
                  pl.BlockSpec((BLOCK, 2 * BLOCK), lambda h: (0, 0))],
        out_specs=pl.BlockSpec((None, BLOCK, 2 * BLOCK), lambda h: (h, 0, 0)),
        out_shape=jax.ShapeDtypeStruct((SW_HEADS, BLOCK, 2 * BLOCK), F32),
        name=name,
    )(rel_bias, buckets)


def _bias_grad(dbias_layers, buckets, name):
    n_l = len(dbias_layers)

    def body(*refs):
        bk = refs[n_l][...]
        out_ref = refs[n_l + 1]
        db = refs[0][...]
        for r in refs[1:n_l]:
            db = db + r[...]
        lane = lax.broadcasted_iota(jnp.int32, (1, LANES), 1)
        acc = jnp.zeros((1, LANES), F32)
        for b in range(N_BUCKETS):
            part = jnp.sum(jnp.where(bk == b, db, 0.0), axis=1, keepdims=True)
            tot = jnp.sum(part, axis=0, keepdims=True)
            acc = jnp.where(lane == b, tot, acc)
        out_ref[...] = acc

    hspec = pl.BlockSpec((None, BLOCK, 2 * BLOCK), lambda h: (h, 0, 0))
    return pl.pallas_call(
        body,
        grid=(SW_HEADS,),
        in_specs=[hspec] * n_l + [pl.BlockSpec((BLOCK, 2 * BLOCK), lambda h: (0, 0))],
        out_specs=pl.BlockSpec((None, 1, LANES), lambda h: (h, 0, 0)),
        out_shape=jax.ShapeDtypeStruct((SW_HEADS, 1, LANES), F32),
        name=name,
    )(*dbias_layers, buckets)


def _swa_scores(qb, kp, kc, bias_ref, hh, n):
    row = lax.broadcasted_iota(jnp.int32, (BLOCK, BLOCK), 0)
    col = lax.broadcasted_iota(jnp.int32, (BLOCK, BLOCK), 1)
    s1 = _dot_nt(qb, kp) + bias_ref[hh, :, :BLOCK]
    s2 = _dot_nt(qb, kc) + bias_ref[hh, :, BLOCK:]
    no_prev = jnp.where(n > 0, 0, BLOCK)
    s1 = jnp.where(col > row + no_prev, s1, NEG)
    s2 = jnp.where(col <= row, s2, NEG)
    return s1, s2


def _swa_lanes(p, hh):
    lane = lax.broadcasted_iota(jnp.int32, (1, LANES), 1)
    kv_half = jnp.zeros((1, LANES), jnp.int32) + (2 * p + hh) // SW_GROUP
    return jnp.where(lane >= HEAD_DIM, 1, 0) == kv_half, kv_half == hh


def _to_kv_lanes(x, kv_lanes, aligned):
    return jnp.where(kv_lanes, jnp.where(aligned, x, pltpu.roll(x, HEAD_DIM, 1)), 0.0)


def _swa_fwd(qkv, bias, sinks, name, comm=None):
    s = qkv.shape[0]
    nb = s // BLOCK
    q0 = 3 * N_PAIR
    kblk, vblk = q0 + N_PAIR, q0 + N_PAIR + 1

    def body(sink_ref, q_ref, k_ref, v_ref, bias_ref, o_ref, lse_ref):
        p = pl.program_id(0)

        def step(n, carry):
            r0 = pl.multiple_of(n * BLOCK, BLOCK)
            p0 = pl.multiple_of(jnp.maximum(n - 1, 0) * BLOCK, BLOCK)
            cur, prev = pl.ds(r0, BLOCK), pl.ds(p0, BLOCK)
            qf = q_ref[cur, :].astype(F32) * SCALE
            kp, kc, vp, vc = k_ref[prev, :], k_ref[cur, :], v_ref[prev, :], v_ref[cur, :]
            outs = []
            for hh in range(2):
                kv_lanes, aligned = _swa_lanes(p, hh)
                sink = sink_ref[2 * p + hh]
                qb = _to_kv_lanes(qf, kv_lanes, aligned).astype(MXU)
                s1, s2 = _swa_scores(qb, kp, kc, bias_ref, hh, n)
                m = jnp.maximum(jnp.maximum(jnp.max(s1, axis=1, keepdims=True),
                                            jnp.max(s2, axis=1, keepdims=True)), sink)
                e1 = jnp.exp(s1 - m)
                e2 = jnp.exp(s2 - m)
                den = (jnp.sum(e1, axis=1, keepdims=True) + jnp.sum(e2, axis=1, keepdims=True)
                       + jnp.exp(sink - m))
                o = _dot((e1 / den).astype(MXU), vp) + _dot((e2 / den).astype(MXU), vc)
                outs.append(jnp.where(aligned, o, pltpu.roll(o, HEAD_DIM, 1)))
                lse_ref[hh, cur, :] = m + jnp.log(den)
            o_ref[cur, :] = jnp.where(_head_lanes(0), outs[0], outs[1]).astype(o_ref.dtype)
            return carry

        lax.fori_loop(0, nb, step, 0)

    outs, couts = _call(
        body,
        grid=(N_PAIR,),
        in_specs=[pl.BlockSpec(memory_space=pltpu.SMEM),
                  pl.BlockSpec((s, LANES), lambda p: (0, q0 + p)),
                  pl.BlockSpec((s, LANES), lambda p: (0, kblk)),
                  pl.BlockSpec((s, LANES), lambda p: (0, vblk)),
                  pl.BlockSpec((2, BLOCK, 2 * BLOCK), lambda p: (p, 0, 0))],
        out_specs=[pl.BlockSpec((s, LANES), lambda p: (0, p)),
                   pl.BlockSpec((None, 2, s, 1), lambda p: (p, 0, 0, 0))],
        out_shape=[jax.ShapeDtypeStruct((s, SW_W), MXU), jax.ShapeDtypeStruct((N_PAIR, 2, s, 1), F32)],
        args=(sinks, qkv, qkv, qkv, bias), name=name, comm=comm)
    return tuple(outs) if comm is None else (tuple(outs), couts)


def _swa_bwd(qkv, bias, sinks, do, lse, name, comm=None):
    s = qkv.shape[0]
    nb = s // BLOCK
    q0 = 3 * N_PAIR
    kblk, vblk = q0 + N_PAIR, q0 + N_PAIR + 1

    def body(sink_ref, q_ref, k_ref, v_ref, bias_ref, do_ref, lse_ref,
             dq_ref, dk_ref, dv_ref, dbias_ref, dsink_ref, dk_acc, dv_acc):
        p = pl.program_id(0)

        @pl.when(p == 0)
        def _():
            dk_acc[...] = jnp.zeros_like(dk_acc)
            dv_acc[...] = jnp.zeros_like(dv_acc)

        dbias_ref[...] = jnp.zeros_like(dbias_ref)

        def step(n, dsink_rows):
            r0 = pl.multiple_of(n * BLOCK, BLOCK)
            p0 = pl.multiple_of(jnp.maximum(n - 1, 0) * BLOCK, BLOCK)
            cur, prev = pl.ds(r0, BLOCK), pl.ds(p0, BLOCK)
            qf = q_ref[cur, :].astype(F32) * SCALE
            dof = do_ref[cur, :].astype(F32)
            kp, kc, vp, vc = k_ref[prev, :], k_ref[cur, :], v_ref[prev, :], v_ref[cur, :]
            dqs, new_rows = [], []
            for hh in range(2):
                kv_lanes, aligned = _swa_lanes(p, hh)
                sink = sink_ref[2 * p + hh]
                qb = _to_kv_lanes(qf, kv_lanes, aligned).astype(MXU)
                dob = _to_kv_lanes(dof, kv_lanes, aligned).astype(MXU)
                lse_b = lse_ref[hh, cur, :]
                s1, s2 = _swa_scores(qb, kp, kc, bias_ref, hh, n)
                pr1 = jnp.exp(s1 - lse_b)
                pr2 = jnp.exp(s2 - lse_b)
                dpr1 = _dot_nt(dob, vp)
                dpr2 = _dot_nt(dob, vc)
                delta = (jnp.sum(pr1 * dpr1, axis=1, keepdims=True)
                         + jnp.sum(pr2 * dpr2, axis=1, keepdims=True))
                ds1 = pr1 * (dpr1 - delta)
                ds2 = pr2 * (dpr2 - delta)
                dbias_ref[hh, :, :BLOCK] += ds1
                dbias_ref[hh, :, BLOCK:] += ds2
                ds1b, ds2b = ds1.astype(MXU), ds2.astype(MXU)
                dq = _dot(ds1b, kp) + _dot(ds2b, kc)
                dqs.append(jnp.where(aligned, dq, pltpu.roll(dq, HEAD_DIM, 1)))
                dk_acc[prev, :] += _dot_tn(ds1b, qb)
                dk_acc[cur, :] += _dot_tn(ds2b, qb)
                dv_acc[prev, :] += _dot_tn(pr1.astype(MXU), dob)
                dv_acc[cur, :] += _dot_tn(pr2.astype(MXU), dob)
                new_rows.append(dsink_rows[hh] - jnp.exp(sink - lse_b) * delta)
            dq_ref[cur, :] = (jnp.where(_head_lanes(0), dqs[0], dqs[1]) * SCALE).astype(dq_ref.dtype)
            return tuple(new_rows)

        zero = jnp.zeros((BLOCK, 1), F32)
        rows = lax.fori_loop(0, nb, step, (zero, zero))
        for hh in range(2):
            dsink_ref[hh] = jnp.broadcast_to(jnp.sum(rows[hh], axis=0, keepdims=True), (1, LANES))

        @pl.when(p == N_PAIR - 1)
        def _():
            dk_ref[...] = dk_acc[...].astype(dk_ref.dtype)
            dv_ref[...] = dv_acc[...].astype(dv_ref.dtype)

    pair = pl.BlockSpec((s, LANES), lambda p: (0, p))
    kv_out = pl.BlockSpec((s, LANES), lambda p: (0, 0))
    bspec = pl.BlockSpec((2, BLOCK, 2 * BLOCK), lambda p: (p, 0, 0))
    outs, couts = _call(
        body,
        grid=(N_PAIR,),
        in_specs=[pl.BlockSpec(memory_space=pltpu.SMEM),
                  pl.BlockSpec((s, LANES), lambda p: (0, q0 + p)),
                  pl.BlockSpec((s, LANES), lambda p: (0, kblk)),
                  pl.BlockSpec((s, LANES), lambda p: (0, vblk)),
                  bspec, pair,
                  pl.BlockSpec((None, 2, s, 1), lambda p: (p, 0, 0, 0))],
        out_specs=[pair, kv_out, kv_out, bspec, pl.BlockSpec((2, 1, LANES), lambda p: (p, 0, 0))],
        out_shape=[jax.ShapeDtypeStruct((s, SW_W), MXU),
                   jax.ShapeDtypeStruct((s, LANES), MXU),
                   jax.ShapeDtypeStruct((s, LANES), MXU),
                   jax.ShapeDtypeStruct((SW_HEADS, BLOCK, 2 * BLOCK), F32),
                   jax.ShapeDtypeStruct((SW_HEADS, 1, LANES), F32)],
        scratch_shapes=[pltpu.VMEM((s, LANES), F32), pltpu.VMEM((s, LANES), F32)],
        args=(sinks, qkv, qkv, qkv, bias, do, lse), name=name, comm=comm)
    return tuple(outs) if comm is None else (tuple(outs), couts)


class _NoPlan:
    def comm(self, name):
        return None

    def done(self, name, outs):
        pass

    def grad(self, layer, name, value):
        pass

    def layer_done(self, layer):
        pass


def _run(plan, fn, *args, name, **kw):
    comm = plan.comm(name)
    if comm is None:
        return fn(*args, name=name, **kw)
    res, outs = fn(*args, name=name, comm=comm, **kw)
    plan.done(name, outs)
    return res


def _layer_fwd(x, p, w, g_mix, g_mlp, g_pe, sinks, bias, tag, plan):
    h1, r1 = _rms_fwd(x, g_mix, f"rms_mix_{tag}")
    qkv = _run(plan, _mm, h1, w["w_qkv"], out_dtypes=(MXU,), name=f"proj_qkv_{tag}")
    gates = _run(plan, _mm, h1, w["w_gate"], name=f"proj_gate_{tag}")
    oa = _run(plan, _sb_fwd, qkv, name=f"sb_fwd_{tag}")
    ob, lse = _run(plan, _swa_fwd, qkv, bias, sinks, name=f"swa_fwd_{tag}")
    merged = _mix_fwd(oa, ob, w["w_up_a"], w["w_up_b"], gates, f"mix_fwd_{tag}")
    x1 = _run(plan, _mm, merged, w["w_o"], extras=(x,), epi=lambda acc, res: res + acc, name=f"out_proj_{tag}")
    h2, r2 = _rms_fwd(x1, g_mlp, f"rms_mlp_{tag}")
    u, act = _run(plan, _mm, h2, w["w_ff1"], epi=lambda acc: (acc, jnp.square(jnp.maximum(acc, 0.0))),
                  out_dtypes=(F32, MXU), name=f"ff1_{tag}")
    x2 = _run(plan, _mm, act, w["w_ff2"], extras=(x1,), epi=lambda acc, res: res + acc, name=f"ff2_{tag}")
    h3, r3 = _rms_fwd(x2, g_pe, f"rms_pe_{tag}")
    x3 = _ple(p, w["w_pe"], h3, w["w_pg"], x2, backward=False, name=f"ple_fwd_{tag}")
    saved = dict(x=x, h1=h1, r1=r1, gates=gates, qkv=qkv, lse=lse, oa=oa, ob=ob, merged=merged,
                 x1=x1, h2=h2, r2=r2, u=u, act=act, x2=x2, h3=h3, r3=r3)
    return x3, saved


def _layer_bwd(dx3, sv, p, w, g_mix, g_mlp, g_pe, sinks, bias, layer, plan):
    tag = f"l{layer}"
    gw = {}
    wire = (WIRE,)

    def dw(name, a, b):
        gw[name] = _run(plan, _mm, a, b, ta=True, out_dtypes=wire, name=f"d{name}_{tag}")
        plan.grad(layer, name, gw[name])

    dpe, dgt = _ple(p, w["w_pe"], sv["h3"], w["w_pg"], dx3, backward=True, name=f"ple_bwd_{tag}")
    dw("w_pe", p, dpe)
    dw("w_pg", sv["h3"], dgt)
    dh3 = _run(plan, _mm, dgt, w["w_pg"], tb=True, name=f"dh_pe_{tag}")
    dx2, dx2b, dg_pe = _rms_bwd(sv["x2"], sv["r3"], g_pe, dh3, dx3, f"rms_pe_bwd_{tag}")
    dw("w_ff2", sv["act"], dx2b)
    du = _run(plan, _mm, dx2b, w["w_ff2"], tb=True, extras=(sv["u"],),
              epi=lambda acc, u: acc * (2.0 * jnp.maximum(u, 0.0)), out_dtypes=(MXU,), name=f"dact_{tag}")
    dw("w_ff1", sv["h2"], du)
    dh2 = _run(plan, _mm, du, w["w_ff1"], tb=True, name=f"dh_mlp_{tag}")
    dx1, dx1b, dg_mlp = _rms_bwd(sv["x1"], sv["r2"], g_mlp, dh2, dx2, f"rms_mlp_bwd_{tag}")
    dw("w_o", sv["merged"], dx1b)
    dya, dyb, dga, dgb = _mix_bwd(dx1b, w["w_o"], sv["oa"], sv["ob"], w["w_up_a"], w["w_up_b"],
                                  sv["gates"], f"mix_bwd_{tag}")
    dw("w_up_a", sv["oa"], dya)
    dw("w_up_b", sv["ob"], dyb)
    doa = _run(plan, _mm, dya, w["w_up_a"], tb=True, out_dtypes=(MXU,), name=f"do_a_{tag}")
    dob = _run(plan, _mm, dyb, w["w_up_b"], tb=True, out_dtypes=(MXU,), name=f"do_b_{tag}")
    dqb, dkb, dvb, dbias, dsink = _run(plan, _swa_bwd, sv["qkv"], bias, sinks, dob, sv["lse"],
                                       name=f"swa_bwd_{tag}")
    dqa, dka, dva = _run(plan, _sb_bwd, sv["qkv"], doa, name=f"sb_bwd_{tag}")
    dqkv = jnp.concatenate([dqa, dka, dva, dqb, dkb, dvb], axis=1)
    gw_qkv = _mm(sv["h1"], dqkv, ta=True, out_dtypes=wire, name=f"dw_qkv_{tag}")
    gw_ga = _mm(sv["h1"], dga, ta=True, out_dtypes=wire, name=f"dw_ga_{tag}")
    gw_gb = _mm(sv["h1"], dgb, ta=True, out_dtypes=wire, name=f"dw_gb_{tag}")
    gw["w_in"] = jnp.concatenate([gw_qkv, gw_ga, gw_gb], axis=1)
    plan.grad(layer, "w_in", gw["w_in"])
    d = dga.shape[1]
    add = lambda acc, res: res + acc
    dh1 = _mm(dqkv, w["w_qkv"], tb=True, tk=768, name=f"dh_qkv_{tag}")
    dh1 = _mm(dga, w["w_gate"][:, :d], tb=True, extras=(dh1,), epi=add, name=f"dh_ga_{tag}")
    dh1 = _mm(dgb, w["w_gate"][:, d:], tb=True, extras=(dh1,), epi=add, name=f"dh_gb_{tag}")
    dx, _, dg_mix = _rms_bwd(sv["x"], sv["r1"], g_mix, dh1, dx1, f"rms_mix_bwd_{tag}")
    small = dict(g_mix=dg_mix, g_mlp=dg_mlp, g_pe=dg_pe, sinks=dsink[:, 0, 0], dbias=dbias)
    return dx, gw, small


def _local_step(x, p, target, weights, g_mix, g_mlp, g_pe, g_final, sinks, rel_bias, plan=None):
    plan = _NoPlan() if plan is None else plan
    depth = g_mix.shape[0]
    buckets = jnp.asarray(_bucket_table())
    bias = _build_bias(rel_bias, buckets, "build_bias")
    saved, wfull = [], []
    h = x
    for l in range(depth):
        wfull.append(weights(l))
        h, sv = _layer_fwd(h, p[l], wfull[l], g_mix[l:l + 1], g_mlp[l:l + 1], g_pe[l:l + 1],
                           sinks[l], bias, f"l{l}", plan)
        saved.append(sv)
    loss_row, dx, dg_final = _loss_head(h, g_final[None, :], target, "loss_head")
    gws = [None] * depth
    smalls = [None] * depth
    for l in reversed(range(depth)):
        dx, gws[l], smalls[l] = _layer_bwd(dx, saved[l], p[l], wfull[l], g_mix[l:l + 1], g_mlp[l:l + 1],
                                           g_pe[l:l + 1], sinks[l], bias, l, plan)
        plan.layer_done(l)
    drel = _bias_grad([sm["dbias"] for sm in smalls], buckets, "bias_grad")[:, 0, :N_BUCKETS].T
    small = dict(
        g_mix=jnp.concatenate([sm["g_mix"] for sm in smalls], axis=0),
        g_mlp=jnp.concatenate([sm["g_mlp"] for sm in smalls], axis=0),
        g_pe=jnp.concatenate([sm["g_pe"] for sm in smalls], axis=0),
        g_final=dg_final[0],
        sinks=jnp.stack([sm["sinks"] for sm in smalls], axis=0),
        rel_bias=drel,
    )
    return loss_row, dx, gws, small


MESH_ID = pl.DeviceIdType.MESH
ANY = pl.BlockSpec(memory_space=pl.ANY)


def _position():
    return lax.axis_index("x"), lax.axis_index("y"), lax.axis_index("c")


def _run_comm(comm, name):
    ci, co = len(comm.inputs), len(comm.out_shapes)

    def body(*refs):
        cin, cout, csem = refs[:ci], refs[ci:ci + co], refs[ci + co:]
        pos = _position()
        comm.start(pos, cin, cout, csem)
        comm.finish(pos, cin, cout, csem)

    return pl.pallas_call(body, out_shape=comm.out_shapes, in_specs=[ANY] * ci, out_specs=[ANY] * co,
                          scratch_shapes=comm.sems, name=name)(*comm.inputs)


def _gather_comm(shards):
    n = len(shards)

    def copies(pos, x_refs, out_refs, sems):
        send_sems, recv_sems, local_sems = sems
        x, y, c = pos
        me, sibling = (x, y, c), (x, y, 1 - c)
        chips = [(1 - x, y), (x, 1 - y), (1 - x, 1 - y)]

        def slot(a, px, py, pc):
            return out_refs[a].at[4 * px + 2 * py + pc]

        def copy(a, k, block, to, src=None):
            return pltpu.make_async_remote_copy(
                src_ref=slot(a, *block) if src is None else src, dst_ref=slot(a, *block),
                send_sem=send_sems.at[a, k], recv_sem=recv_sems.at[a, k],
                device_id=to, device_id_type=MESH_ID)

        mine = [pltpu.make_async_copy(x_refs[a], slot(a, *me), local_sems.at[a]) for a in range(n)]
        first = []
        for a in range(n):
            first.append(copy(a, 0, me, sibling, src=x_refs[a]))
            first += [copy(a, 1 + j, me, (*chip, c), src=x_refs[a]) for j, chip in enumerate(chips)]
        return me, sibling, chips, copy, mine, first

    def start(pos, x_refs, out_refs, sems):
        _, _, _, _, mine, first = copies(pos, x_refs, out_refs, sems)
        for cp in mine + first:
            cp.start()

    def finish(pos, x_refs, out_refs, sems):
        me, sibling, chips, copy, mine, first = copies(pos, x_refs, out_refs, sems)
        c = pos[2]
        passed = []
        for j, chip in enumerate(chips):
            for a in range(n):
                copy(a, 1 + j, (*chip, c), me).wait_recv()
                fwd = copy(a, 4 + j, (*chip, c), sibling)
                fwd.start()
                passed.append(fwd)
        for a in range(n):
            copy(a, 0, sibling, me).wait_recv()
            for j, chip in enumerate(chips):
                copy(a, 4 + j, (*chip, 1 - c), me).wait_recv()
        for cp in first + passed:
            cp.wait_send()
        for cp in mine:
            cp.wait()

    return _Comm(shards, [jax.ShapeDtypeStruct((N_DEV,) + s.shape, s.dtype) for s in shards],
                 [pltpu.SemaphoreType.DMA((n, 7)), pltpu.SemaphoreType.DMA((n, 7)),
                  pltpu.SemaphoreType.DMA((n,))], start, finish)


def _exchange_comm(arrays, n_slots, route):
    n = len(arrays)

    def copies(pos, in_refs, out_refs, sems):
        send_sems, recv_sems = sems
        out = []
        for a in range(n):
            for j in range(n_slots):
                src_slot, peer = route(pos, j)
                out.append(pltpu.make_async_remote_copy(
                    src_ref=in_refs[a].at[src_slot], dst_ref=out_refs[a].at[j],
                    send_sem=send_sems.at[a, j], recv_sem=recv_sems.at[a, j],
                    device_id=peer, device_id_type=MESH_ID))
        return out

    def start(pos, in_refs, out_refs, sems):
        for cp in copies(pos, in_refs, out_refs, sems):
            cp.start()

    def finish(pos, in_refs, out_refs, sems):
        for cp in copies(pos, in_refs, out_refs, sems):
            cp.wait()

    return _Comm(arrays, [jax.ShapeDtypeStruct((n_slots,) + g.shape[1:], g.dtype) for g in arrays],
                 [pltpu.SemaphoreType.DMA((n, n_slots)), pltpu.SemaphoreType.DMA((n, n_slots))], start, finish)


def _rs_sibling_comm(gs):
    return _exchange_comm(gs, 4, lambda pos, j: (2 * j + (1 - pos[2]), (pos[0], pos[1], 1 - pos[2])))


def _chip_of(k, x, y):
    return x ^ ((k + 1) & 1), y ^ (((k + 1) >> 1) & 1)


def _chip_partials(pos, gs, recvs, name):
    n = len(gs)

    def body(pos_ref, *refs):
        for a in range(n):
            refs[2 * n + a][...] = (refs[a][...].astype(F32) + refs[n + a][...].astype(F32)
                                    ).astype(refs[2 * n + a].dtype)

    def g_map(k, pos_ref):
        cx, cy = _chip_of(k, pos_ref[0], pos_ref[1])
        return (4 * cx + 2 * cy + pos_ref[2], 0, 0)

    def r_map(k, pos_ref):
        cx, cy = _chip_of(k, pos_ref[0], pos_ref[1])
        return (2 * cx + cy, 0, 0)

    slab = [(None,) + g.shape[1:] for g in gs]
    return pl.pallas_call(
        body,
        grid_spec=pltpu.PrefetchScalarGridSpec(
            num_scalar_prefetch=1,
            grid=(4,),
            in_specs=[pl.BlockSpec(sh, g_map) for sh in slab] + [pl.BlockSpec(sh, r_map) for sh in slab],
            out_specs=[pl.BlockSpec(sh, lambda k, pos_ref: (k, 0, 0)) for sh in slab],
        ),
        out_shape=[jax.ShapeDtypeStruct((4,) + g.shape[1:], g.dtype) for g in gs],
        compiler_params=_cparams(),
        name=name,
    )(pos, *gs, *recvs)


def _rs_chips_comm(parts):
    return _exchange_comm(parts, 3, lambda pos, k: (k, (*_chip_of(k, pos[0], pos[1]), pos[2])))


def _adamw_math(w, g, m, v):
    m = ADAM_B1 * m + (1.0 - ADAM_B1) * g
    v = ADAM_B2 * v + (1.0 - ADAM_B2) * (g * g)
    m_hat = m / (1.0 - ADAM_B1 ** ADAM_STEP)
    v_hat = v / (1.0 - ADAM_B2 ** ADAM_STEP)
    delta = -ADAM_LR * (m_hat / (jnp.sqrt(v_hat) + ADAM_EPS) + ADAM_WD * w)
    return delta, m, v


def _adamw_weight(parts, recvs, w, m, v, name):
    depth, a, b = w.shape
    ta = _tile(a, 256, unit=16)
    ni = a // ta

    def body(*refs):
        p_refs, r_refs = refs[:depth], refs[depth:2 * depth]
        w_ref, m_ref, v_ref = refs[2 * depth:2 * depth + 3]
        g_out, d_out, m_out, v_out = refs[2 * depth + 3:]
        layer = pl.program_id(0)
        g = jnp.zeros((ta, b), F32)
        for l in range(depth):
            gl = p_refs[l][...].astype(F32)
            for k in range(3):
                gl = gl + r_refs[l][k].astype(F32)
            g = jnp.where(layer == l, gl, g)
        delta, m_new, v_new = _adamw_math(w_ref[...], g, m_ref[...], v_ref[...])
        g_out[...] = g
        d_out[...] = delta
        m_out[...] = m_new
        v_out[...] = v_new

    def hold(l):
        return lambda layer, i: jnp.where(layer == l, i, jnp.where(layer < l, 0, ni - 1))

    p_specs = [pl.BlockSpec((None, ta, b), (lambda layer, i, f=hold(l): (3, f(layer, i), 0))) for l in range(depth)]
    r_specs = [pl.BlockSpec((3, ta, b), (lambda layer, i, f=hold(l): (0, f(layer, i), 0))) for l in range(depth)]
    row = pl.BlockSpec((None, ta, b), lambda layer, i: (layer, i, 0))
    return pl.pallas_call(
        body,
        grid=(depth, ni),
        in_specs=p_specs + r_specs + [row, row, row],
        out_specs=[row] * 4,
        out_shape=[jax.ShapeDtypeStruct(w.shape, F32)] * 4,
        compiler_params=_cparams(),
        name=name,
    )(*parts, *recvs, w, m, v)


def _adamw_replicated(gathered, w, m, v, name):
    r, lanes = w.shape

    def body(g_ref, w_ref, m_ref, v_ref, g_out, d_out, m_out, v_out):
        g = g_ref[0]
        for k in range(1, N_DEV):
            g = g + g_ref[k]
        delta, m_new, v_new = _adamw_math(w_ref[...], g, m_ref[...], v_ref[...])
        g_out[...] = g
        d_out[...] = delta
        m_out[...] = m_new
        v_out[...] = v_new

    return pl.pallas_call(
        body,
        out_shape=[jax.ShapeDtypeStruct((r, lanes), F32)] * 4,
        name=name,
    )(gathered, w, m, v)


def _full_weight(name, gathered):
    _, a, b = gathered.shape
    if name in COL_SHARDED:
        return gathered.transpose(1, 0, 2).reshape(a, N_DEV * b)
    return gathered.reshape(N_DEV * a, b)


def _to_slabs(name, gfull, shard_shape):
    a, b = shard_shape
    if name in COL_SHARDED:
        return gfull.reshape(a, N_DEV, b).transpose(1, 0, 2)
    return gfull.reshape(N_DEV, a, b)


def _pack_small(arrs):
    rows = []
    for a in arrs:
        flat = a.astype(F32).reshape(-1)
        pad = (-flat.shape[0]) % LANES
        rows.append(jnp.pad(flat, (0, pad)).reshape(-1, LANES))
    packed = jnp.concatenate(rows, axis=0)
    return jnp.pad(packed, ((0, (-packed.shape[0]) % 8), (0, 0)))


def _unpack_small(packed, shapes):
    out, off = [], 0
    for shp in shapes:
        n = math.prod(shp)
        rows = -(-n // LANES)
        out.append(packed[off:off + rows].reshape(-1)[:n].reshape(shp))
        off += rows
    return out


GATHERS = (
    (None, 0, WEIGHTS),
    ("sb_fwd_l0", 1, WEIGHTS),
)
REDUCES = (
    (1, WEIGHTS, "dact_l0", "sb_bwd_l0"),
    (0, WEIGHTS, None, None),
)


class _Plan:
    def __init__(self, w_sh, pos):
        self.w_sh = dict(zip(WEIGHTS, w_sh))
        self.pos = pos
        self.full, self.gw, self.parts, self.recv = {}, {}, {}, {}
        self.slabs = {}
        self.hosted = {}
        for i, (host, _, _) in enumerate(GATHERS):
            if host is not None:
                self.hosted[host] = ("gather", i)
        for i, (_, _, sib_host, chip_host) in enumerate(REDUCES):
            assert (sib_host is None) == (chip_host is None)
            if sib_host is not None:
                self.hosted[sib_host] = ("sibling", i)
                self.hosted[chip_host] = ("chips", i)

    def _gather(self, i):
        _, layer, names = GATHERS[i]
        return _gather_comm([self.w_sh[n][layer].astype(WIRE) for n in names])

    def _gathered(self, i, outs):
        _, layer, names = GATHERS[i]
        for n, g in zip(names, outs):
            self.full[(layer, n)] = _full_weight(n, g)

    def weights(self, layer):
        for i, (host, l, _) in enumerate(GATHERS):
            if host is None and l == layer:
                self._gathered(i, _run_comm(self._gather(i), f"gather_{i}_l{l}"))
        wl = {n: self.full[(layer, n)] for n in WEIGHTS}
        w_all = wl.pop("w_in")
        wl["w_qkv"] = w_all[:, :QKV_COLS]
        wl["w_gate"] = w_all[:, QKV_COLS:]
        return wl

    def grad(self, layer, name, value):
        self.gw[(layer, name)] = value

    def _sibling(self, i):
        layer, names, _, _ = REDUCES[i]
        self.slabs[i] = [_to_slabs(n, self.gw[(layer, n)], self.w_sh[n].shape[1:]) for n in names]
        return _rs_sibling_comm(self.slabs[i])

    def _sibling_done(self, i, outs):
        layer, names, _, _ = REDUCES[i]
        parts = _chip_partials(self.pos, self.slabs[i], outs, f"chip_partials_{i}_l{layer}")
        for n, part in zip(names, parts):
            self.parts[(layer, n)] = part

    def _chips(self, i):
        layer, names, _, _ = REDUCES[i]
        return _rs_chips_comm([self.parts[(layer, n)] for n in names])

    def _chips_done(self, i, outs):
        layer, names, _, _ = REDUCES[i]
        for n, r in zip(names, outs):
            self.recv[(layer, n)] = r

    def layer_done(self, layer):
        for i, (l, _, sib_host, _) in enumerate(REDUCES):
            if sib_host is None and l == layer:
                self._sibling_done(i, _run_comm(self._sibling(i), f"reduce_sibling_{i}_l{l}"))
                self._chips_done(i, _run_comm(self._chips(i), f"reduce_chips_{i}_l{l}"))

    def comm(self, name):
        if name not in self.hosted:
            return None
        kind, i = self.hosted[name]
        return {"gather": self._gather, "sibling": self._sibling, "chips": self._chips}[kind](i)

    def done(self, name, outs):
        kind, i = self.hosted[name]
        {"gather": self._gathered, "sibling": self._sibling_done, "chips": self._chips_done}[kind](i, outs)


def kernel(x, p, w_in, w_up_a, w_up_b, w_o, w_ff1, w_ff2, w_pe, w_pg, g_mix, g_mlp, g_pe, g_final, sinks, rel_bias, loss_target, m_w_in, m_w_up_a, m_w_up_b, m_w_o, m_w_ff1, m_w_ff2, m_w_pe, m_w_pg, m_g_mix, m_g_mlp, m_g_pe, m_g_final, m_sinks, m_rel_bias, v_w_in, v_w_up_a, v_w_up_b, v_w_o, v_w_ff1, v_w_ff2, v_w_pe, v_w_pg, v_g_mix, v_g_mlp, v_g_pe, v_g_final, v_sinks, v_rel_bias):
    w_sh = [w_in, w_up_a, w_up_b, w_o, w_ff1, w_ff2, w_pe, w_pg]
    m_sh = [m_w_in, m_w_up_a, m_w_up_b, m_w_o, m_w_ff1, m_w_ff2, m_w_pe, m_w_pg]
    v_sh = [v_w_in, v_w_up_a, v_w_up_b, v_w_o, v_w_ff1, v_w_ff2, v_w_pe, v_w_pg]
    depth = w_in.shape[0]
    assert depth == 2 and x.shape[-1] * 2 + QKV_COLS == w_in.shape[2] * N_DEV

    px, py, pc = _position()
    plan = _Plan(w_sh, jnp.stack([px, py, pc]).astype(jnp.int32))
    loss_row, grad_x, _, small = _local_step(
        x[0], p[:, 0], loss_target[0], plan.weights, g_mix, g_mlp, g_pe, g_final, sinks, rel_bias, plan=plan)

    grad_w, delta_w, new_m, new_v = [], [], [], []
    for a, name in enumerate(WEIGHTS):
        outs = _adamw_weight([plan.parts[(l, name)] for l in range(depth)],
                             [plan.recv[(l, name)] for l in range(depth)],
                             w_sh[a], m_sh[a], v_sh[a], f"adamw_{name}")
        for lst, o in zip((grad_w, delta_w, new_m, new_v), outs):
            lst.append(o)

    small_w = [g_mix, g_mlp, g_pe, g_final, sinks, rel_bias]
    small_m = [m_g_mix, m_g_mlp, m_g_pe, m_g_final, m_sinks, m_rel_bias]
    small_v = [v_g_mix, v_g_mlp, v_g_pe, v_g_final, v_sinks, v_rel_bias]
    small_shapes = [a.shape for a in small_w] + [(1,)]
    zero = jnp.zeros((1,), F32)
    small_g = _pack_small([small[n] for n in SMALL] + [loss_row[0, :1]])
    small_all = _run_comm(_gather_comm([small_g]), "gather_small")[0]
    packed_s = _adamw_replicated(small_all, _pack_small(small_w + [zero]), _pack_small(small_m + [zero]),
                                 _pack_small(small_v + [zero + 1.0]), "adamw_replicated")
    sg, sd, sm, sv = [_unpack_small(t, small_shapes) for t in packed_s]
    loss = sg[-1][0]

    return (loss, grad_x[None], *grad_w, *sg[:-1], *delta_w, *sd[:-1], *new_m, *sm[:-1], *new_v, *sv[:-1])
```

```python
import functools
import math

import numpy as np
import jax
import jax.numpy as jnp
from jax import lax
from jax.experimental import pallas as pl
from jax.experimental.pallas import tpu as pltpu

F32 = jnp.float32
MXU = jnp.bfloat16
WIRE = jnp.bfloat16

HEAD_DIM = 64
SB_HEADS = 8
SW_HEADS = 8
SW_KV = 2
SW_GROUP = SW_HEADS // SW_KV
BLOCK = 128
N_BUCKETS = 32
MAX_DISTANCE = 128
EPS = 1e-6
SCALE = HEAD_DIM ** -0.5
SB_W = SB_HEADS * HEAD_DIM
SW_W = SW_HEADS * HEAD_DIM
QKV_COLS = 3 * SB_W + SW_W + 2 * SW_KV * HEAD_DIM
N_DEV = 8
LANES = 128
N_PAIR = SB_HEADS // 2
NEG = -1e30

ADAM_LR = 0.001
ADAM_B1 = 0.9
ADAM_B2 = 0.999
ADAM_EPS = 1e-08
ADAM_WD = 0.01
ADAM_STEP = 10

VMEM_LIMIT = 48 * 1024 * 1024
SB_TQ = 256
SB_DEAD = -105.0

WEIGHTS = ("w_in", "w_up_a", "w_up_b", "w_o", "w_ff1", "w_ff2", "w_pe", "w_pg")
COL_SHARDED = ("w_in", "w_up_a", "w_up_b", "w_ff1", "w_pe")
SMALL = ("g_mix", "g_mlp", "g_pe", "g_final", "sinks", "rel_bias")


def _cparams(**kw):
    return pltpu.CompilerParams(vmem_limit_bytes=VMEM_LIMIT, **kw)


def _dot(a, b):
    return jnp.dot(a, b, preferred_element_type=F32)


def _dot_nt(a, b):
    return lax.dot_general(a, b, (((1,), (1,)), ((), ())), preferred_element_type=F32)


def _dot_tn(a, b):
    return lax.dot_general(a, b, (((0,), (0,)), ((), ())), preferred_element_type=F32)


def _tile(n, target, unit=LANES):
    if n <= target:
        return n
    t = (target // unit) * unit
    while t > unit and n % t:
        t -= unit
    assert n % t == 0, (n, target)
    return t


def _sigmoid(x):
    return 1.0 / (1.0 + jnp.exp(-x))


class _Comm:
    def __init__(self, inputs, out_shapes, sems, start, finish):
        self.inputs, self.out_shapes, self.sems = list(inputs), list(out_shapes), list(sems)
        self.start, self.finish = start, finish


def _call(body, *, grid, in_specs, out_specs, out_shape, scratch_shapes=(), args, name, comm=None):
    n_in, n_out, n_scr = len(in_specs), len(out_shape), len(scratch_shapes)
    if comm is None:
        outs = pl.pallas_call(body, grid=grid, in_specs=list(in_specs), out_specs=list(out_specs),
                              out_shape=list(out_shape), scratch_shapes=list(scratch_shapes),
                              compiler_params=_cparams(), name=name)(*args)
        return list(outs), None
    ci, co = len(comm.inputs), len(comm.out_shapes)
    any_spec = pl.BlockSpec(memory_space=pl.ANY)

    def wrapped(*refs):
        ins, cin = refs[:n_in], refs[n_in:n_in + ci]
        o0 = n_in + ci
        outs, cout = refs[o0:o0 + n_out], refs[o0 + n_out:o0 + n_out + co]
        s0 = o0 + n_out + co
        scr, csem = refs[s0:s0 + n_scr], refs[s0 + n_scr:]
        ids = [pl.program_id(d) for d in range(len(grid))]
        first = functools.reduce(jnp.logical_and, [i == 0 for i in ids])
        last = functools.reduce(jnp.logical_and, [i == g - 1 for i, g in zip(ids, grid)])
        pos = (lax.axis_index("x"), lax.axis_index("y"), lax.axis_index("c"))

        @pl.when(first)
        def _():
            comm.start(pos, cin, cout, csem)

        body(*ins, *outs, *scr)

        @pl.when(last)
        def _():
            comm.finish(pos, cin, cout, csem)

    outs = pl.pallas_call(wrapped, grid=grid, in_specs=list(in_specs) + [any_spec] * ci,
                          out_specs=list(out_specs) + [any_spec] * co,
                          out_shape=list(out_shape) + comm.out_shapes,
                          scratch_shapes=list(scratch_shapes) + comm.sems,
                          compiler_params=_cparams(), name=name)(*args, *comm.inputs)
    return list(outs[:n_out]), list(outs[n_out:])


def _mm(a, b, *, ta=False, tb=False, extras=(), epi=None, out_dtypes=(F32,),
        tm=1024, tn=1024, tk=512, name, comm=None):
    if ta:
        kdim, m = a.shape
    else:
        m, kdim = a.shape
    n = b.shape[0] if tb else b.shape[1]
    assert (b.shape[1] if tb else b.shape[0]) == kdim
    tm, tn, tk = _tile(m, tm), _tile(n, tn), _tile(kdim, tk)
    nk = kdim // tk
    n_ex, n_out = len(extras), len(out_dtypes)

    a_spec = (pl.BlockSpec((tk, tm), lambda i, j, k: (k, i)) if ta
              else pl.BlockSpec((tm, tk), lambda i, j, k: (i, k)))
    b_spec = (pl.BlockSpec((tn, tk), lambda i, j, k: (j, k)) if tb
              else pl.BlockSpec((tk, tn), lambda i, j, k: (k, j)))
    ex_specs = []
    for e in extras:
        assert e.shape == (m, n), (e.shape, m, n)
        ex_specs.append(pl.BlockSpec((tm, tn), lambda i, j, k: (i, j)))
    out_spec = pl.BlockSpec((tm, tn), lambda i, j, k: (i, j))

    def body(a_ref, b_ref, *rest):
        ex_refs = rest[:n_ex]
        out_refs = rest[n_ex:n_ex + n_out]
        acc = rest[-1]
        k = pl.program_id(2)

        @pl.when(k == 0)
        def _():
            acc[...] = jnp.zeros_like(acc)

        av = a_ref[...].astype(MXU)
        bv = b_ref[...].astype(MXU)
        if ta:
            acc[...] += _dot_tn(av, bv)
        elif tb:
            acc[...] += _dot_nt(av, bv)
        else:
            acc[...] += _dot(av, bv)

        @pl.when(k == nk - 1)
        def _():
            res = acc[...]
            if epi is not None:
                res = epi(res, *[e[...] for e in ex_refs])
            if not isinstance(res, tuple):
                res = (res,)
            for o_ref, r in zip(out_refs, res):
                o_ref[...] = r.astype(o_ref.dtype)

    outs, couts = _call(
        body,
        grid=(m // tm, n // tn, nk),
        in_specs=[a_spec, b_spec] + ex_specs,
        out_specs=[out_spec] * n_out,
        out_shape=[jax.ShapeDtypeStruct((m, n), dt) for dt in out_dtypes],
        scratch_shapes=[pltpu.VMEM((tm, tn), F32)],
        args=(a, b, *extras), name=name, comm=comm)
    res = outs[0] if n_out == 1 else tuple(outs)
    return res if comm is None else (res, couts)


def _rms_fwd(x, g, name):
    s, d = x.shape
    tr = _tile(s, 256)

    def body(x_ref, g_ref, h_ref, r_ref):
        xf = x_ref[...]
        r = lax.rsqrt(jnp.mean(xf * xf, axis=-1, keepdims=True) + EPS)
        h_ref[...] = ((xf * r) * g_ref[...]).astype(h_ref.dtype)
        r_ref[...] = r

    return pl.pallas_call(
        body,
        grid=(s // tr,),
        in_specs=[pl.BlockSpec((tr, d), lambda i: (i, 0)), pl.BlockSpec((1, d), lambda i: (0, 0))],
        out_specs=[pl.BlockSpec((tr, d), lambda i: (i, 0)), pl.BlockSpec((tr, 1), lambda i: (i, 0))],
        out_shape=[jax.ShapeDtypeStruct((s, d), MXU), jax.ShapeDtypeStruct((s, 1), F32)],
        compiler_params=_cparams(),
        name=name,
    )(x, g)


def _rms_bwd(x, r, g, dh, dres, name):
    s, d = x.shape
    tr = _tile(s, 256)

    def body(x_ref, r_ref, g_ref, dh_ref, dres_ref, dx_ref, dxb_ref, dg_ref):
        @pl.when(pl.program_id(0) == 0)
        def _():
            dg_ref[...] = jnp.zeros_like(dg_ref)

        rr = r_ref[...]
        xhat = x_ref[...] * rr
        dh_v = dh_ref[...]
        dxhat = dh_v * g_ref[...]
        mean = jnp.mean(dxhat * xhat, axis=-1, keepdims=True)
        dx = dres_ref[...] + rr * (dxhat - xhat * mean)
        dx_ref[...] = dx
        dxb_ref[...] = dx.astype(dxb_ref.dtype)
        dg_ref[...] += jnp.sum(dh_v * xhat, axis=0, keepdims=True)

    row = pl.BlockSpec((tr, d), lambda i: (i, 0))
    vec = pl.BlockSpec((1, d), lambda i: (0, 0))
    return pl.pallas_call(
        body,
        grid=(s // tr,),
        in_specs=[row, pl.BlockSpec((tr, 1), lambda i: (i, 0)), vec, row, row],
        out_specs=[row, row, vec],
        out_shape=[jax.ShapeDtypeStruct((s, d), F32), jax.ShapeDtypeStruct((s, d), MXU),
                   jax.ShapeDtypeStruct((1, d), F32)],
        compiler_params=_cparams(),
        name=name,
    )(x, r, g, dh, dres)


def _loss_head(x, g, target, name):
    s, d = x.shape
    tr = _tile(s, 256)

    def body(x_ref, g_ref, t_ref, loss_ref, dx_ref, dg_ref):
        @pl.when(pl.program_id(0) == 0)
        def _():
            dg_ref[...] = jnp.zeros_like(dg_ref)
            loss_ref[...] = jnp.zeros_like(loss_ref)

        xf = x_ref[...]
        gv = g_ref[...]
        r = lax.rsqrt(jnp.mean(xf * xf, axis=-1, keepdims=True) + EPS)
        xhat = xf * r
        err = xhat * gv - t_ref[...]
        loss_ref[...] += 0.5 * jnp.sum(jnp.mean(err * err, axis=-1, keepdims=True), axis=0, keepdims=True)
        dy = err * (1.0 / d)
        dxhat = dy * gv
        mean = jnp.mean(dxhat * xhat, axis=-1, keepdims=True)
        dx_ref[...] = r * (dxhat - xhat * mean)
        dg_ref[...] += jnp.sum(dy * xhat, axis=0, keepdims=True)

    row = pl.BlockSpec((tr, d), lambda i: (i, 0))
    vec = pl.BlockSpec((1, d), lambda i: (0, 0))
    return pl.pallas_call(
        body,
        grid=(s // tr,),
        in_specs=[row, vec, row],
        out_specs=[pl.BlockSpec((1, LANES), lambda i: (0, 0)), row, vec],
        out_shape=[jax.ShapeDtypeStruct((1, LANES), F32), jax.ShapeDtypeStruct((s, d), F32),
                   jax.ShapeDtypeStruct((1, d), F32)],
        compiler_params=_cparams(),
        name=name,
    )(x, g, target)


def _mix_fwd(oa, ob, wa, wb, gates, name):
    s, kd = oa.shape
    d = wa.shape[1]
    tm, tn = _tile(s, 1024), _tile(d, 512)
    nj = d // tn

    def body(oa_ref, ob_ref, wa_ref, wb_ref, ga_ref, gb_ref, out_ref):
        ya = _dot(oa_ref[...], wa_ref[...])
        yb = _dot(ob_ref[...], wb_ref[...])
        out_ref[...] = (_sigmoid(ga_ref[...]) * ya + _sigmoid(gb_ref[...]) * yb).astype(out_ref.dtype)

    o_spec = pl.BlockSpec((tm, kd), lambda i, j: (i, 0))
    w_spec = pl.BlockSpec((kd, tn), lambda i, j: (0, j))
    return pl.pallas_call(
        body,
        grid=(s // tm, nj),
        in_specs=[o_spec, o_spec, w_spec, w_spec,
                  pl.BlockSpec((tm, tn), lambda i, j: (i, j)),
                  pl.BlockSpec((tm, tn), lambda i, j: (i, j + nj))],
        out_specs=pl.BlockSpec((tm, tn), lambda i, j: (i, j)),
        out_shape=jax.ShapeDtypeStruct((s, d), MXU),
        compiler_params=_cparams(),
        name=name,
    )(oa, ob, wa, wb, gates, gates)


def _mix_bwd(dx, w_o, oa, ob, wa, wb, gates, name):
    s, kd = oa.shape
    d = wa.shape[1]
    tm, tn = _tile(s, 1024), _tile(d, 512)
    nj = d // tn

    def body(dx_ref, wo_ref, oa_ref, ob_ref, wa_ref, wb_ref, ga_ref, gb_ref,
             dya_ref, dyb_ref, dga_ref, dgb_ref):
        dm = _dot_nt(dx_ref[...], wo_ref[...])
        ya = _dot(oa_ref[...], wa_ref[...])
        yb = _dot(ob_ref[...], wb_ref[...])
        sa = _sigmoid(ga_ref[...])
        sb = _sigmoid(gb_ref[...])
        dya_ref[...] = (dm * sa).astype(dya_ref.dtype)
        dyb_ref[...] = (dm * sb).astype(dyb_ref.dtype)
        dga_ref[...] = (dm * ya * sa * (1.0 - sa)).astype(dga_ref.dtype)
        dgb_ref[...] = (dm * yb * sb * (1.0 - sb)).astype(dgb_ref.dtype)

    o_spec = pl.BlockSpec((tm, kd), lambda i, j: (i, 0))
    w_spec = pl.BlockSpec((kd, tn), lambda i, j: (0, j))
    t_spec = pl.BlockSpec((tm, tn), lambda i, j: (i, j))
    return pl.pallas_call(
        body,
        grid=(s // tm, nj),
        in_specs=[pl.BlockSpec((tm, d), lambda i, j: (i, 0)),
                  pl.BlockSpec((tn, d), lambda i, j: (j, 0)),
                  o_spec, o_spec, w_spec, w_spec, t_spec,
                  pl.BlockSpec((tm, tn), lambda i, j: (i, j + nj))],
        out_specs=[t_spec] * 4,
        out_shape=[jax.ShapeDtypeStruct((s, d), MXU)] * 4,
        compiler_params=_cparams(),
        name=name,
    )(dx, w_o, oa, ob, wa, wb, gates, gates)


def _ple(p, w_pe, h, w_pg, other, *, backward, name):
    s, kp = p.shape
    d = w_pe.shape[1]
    tm, tn = _tile(s, 1024), _tile(d, 512)

    def body(p_ref, wpe_ref, h_ref, wpg_ref, other_ref, *out_refs):
        pe = _dot(p_ref[...].astype(MXU), wpe_ref[...])
        gt = _dot(h_ref[...], wpg_ref[...])
        sg = _sigmoid(gt)
        if backward:
            dout = other_ref[...]
            out_refs[0][...] = (dout * sg).astype(out_refs[0].dtype)
            out_refs[1][...] = (dout * pe * sg * (1.0 - sg)).astype(out_refs[1].dtype)
        else:
            out_refs[0][...] = other_ref[...] + pe * sg

    t_spec = pl.BlockSpec((tm, tn), lambda i, j: (i, j))
    if backward:
        out_specs, out_shape = [t_spec, t_spec], [jax.ShapeDtypeStruct((s, d), MXU)] * 2
    else:
        out_specs, out_shape = [t_spec], [jax.ShapeDtypeStruct((s, d), F32)]
    outs = pl.pallas_call(
        body,
        grid=(s // tm, d // tn),
        in_specs=[pl.BlockSpec((tm, kp), lambda i, j: (i, 0)),
                  pl.BlockSpec((kp, tn), lambda i, j: (0, j)),
                  pl.BlockSpec((tm, d), lambda i, j: (i, 0)),
                  pl.BlockSpec((d, tn), lambda i, j: (0, j)),
                  t_spec],
        out_specs=out_specs,
        out_shape=out_shape,
        compiler_params=_cparams(),
        name=name,
    )(p, w_pe, h, w_pg, other)
    return tuple(outs) if backward else outs[0]


def _split_dot(x, tri):
    hi = x.astype(jnp.bfloat16)
    r1 = x - hi.astype(F32)
    mid = r1.astype(jnp.bfloat16)
    lo = (r1 - mid.astype(F32)).astype(jnp.bfloat16)
    return _dot(hi, tri) + _dot(mid, tri) + _dot(lo, tri)


def _log_sigmoids(z):
    t = jnp.log1p(jnp.exp(-jnp.abs(z)))
    return jnp.minimum(z, 0.0) - t, jnp.minimum(-z, 0.0) - t


def _head_lanes(hh):
    lane = lax.broadcasted_iota(jnp.int32, (1, LANES), 1)
    return jnp.logical_and(lane >= hh * HEAD_DIM, lane < (hh + 1) * HEAD_DIM)


def _sb_fwd(qkv, name, comm=None):
    s = qkv.shape[0]
    tq = _tile(s, SB_TQ)

    def body(q_ref, k_ref, v_ref, o_ref):
        i = pl.program_id(1)
        qf = q_ref[...].astype(F32) * SCALE
        row = lax.broadcasted_iota(jnp.int32, (tq, tq), 0)
        col = lax.broadcasted_iota(jnp.int32, (tq, tq), 1)
        causal = col < row
        tri = jnp.where(row > col, 1.0, 0.0).astype(jnp.bfloat16)

        def one_head(hh):
            qm = jnp.where(_head_lanes(hh), qf, 0.0).astype(MXU)

            def block(kb, c, acc, masked):
                rows = pl.ds(pl.multiple_of(kb * tq, tq), tq)
                z = _dot_nt(qm, k_ref[rows, :])
                lb, lm = _log_sigmoids(z)
                if masked:
                    lm = jnp.where(causal, lm, 0.0)
                a = jnp.exp(lb + _split_dot(lm, tri) + c)
                if masked:
                    a = jnp.where(causal, a, 0.0)
                acc = acc + _dot(a.astype(MXU), v_ref[rows, :])
                c = c + jnp.sum(lm, axis=1, keepdims=True)
                return c, acc

            c, acc = block(i, jnp.zeros((tq, 1), F32), jnp.zeros((tq, LANES), F32), True)

            def live(st):
                return jnp.logical_and(st[0] >= 0, st[1] > SB_DEAD)

            def walk(st):
                c, acc = block(st[0], st[2], st[3], False)
                return st[0] - 1, jnp.max(c), c, acc

            return lax.while_loop(live, walk, (i - 1, jnp.max(c), c, acc))[3]

        o_ref[...] = jnp.where(_head_lanes(0), one_head(0), one_head(1)).astype(o_ref.dtype)

    outs, couts = _call(
        body,
        grid=(N_PAIR, s // tq),
        in_specs=[pl.BlockSpec((tq, LANES), lambda p, i: (i, p)),
                  pl.BlockSpec((s, LANES), lambda p, i: (0, N_PAIR + p)),
                  pl.BlockSpec((s, LANES), lambda p, i: (0, 2 * N_PAIR + p))],
        out_specs=[pl.BlockSpec((tq, LANES), lambda p, i: (i, p))],
        out_shape=[jax.ShapeDtypeStruct((s, SB_W), MXU)],
        args=(qkv, qkv, qkv), name=name, comm=comm)
    return outs[0] if comm is None else (outs[0], couts)


def _sb_bwd(qkv, do, name, comm=None):
    s = qkv.shape[0]
    tq = _tile(s, SB_TQ)
    nq = s // tq

    def body(q_ref, k_ref, v_ref, do_ref, dq_ref, dk_ref, dv_ref, dk_acc, dv_acc, carries):
        i = pl.program_id(1)

        @pl.when(i == 0)
        def _():
            dk_acc[...] = jnp.zeros_like(dk_acc)
            dv_acc[...] = jnp.zeros_like(dv_acc)

        qf = q_ref[...].astype(F32) * SCALE
        dof = do_ref[...]
        row = lax.broadcasted_iota(jnp.int32, (tq, tq), 0)
        col = lax.broadcasted_iota(jnp.int32, (tq, tq), 1)
        causal = col < row
        tri_rev = jnp.where(row > col, 1.0, 0.0).astype(jnp.bfloat16)
        tri_excl = jnp.where(row < col, 1.0, 0.0).astype(jnp.bfloat16)

        def one_head(hh):
            in_head = _head_lanes(hh)
            qm = jnp.where(in_head, qf, 0.0).astype(MXU)
            dom = jnp.where(in_head, dof, jnp.zeros_like(dof))

            def row_sum(kb, masked):
                rows = pl.ds(pl.multiple_of(kb * tq, tq), tq)
                _, lm = _log_sigmoids(_dot_nt(qm, k_ref[rows, :]))
                if masked:
                    lm = jnp.where(causal, lm, 0.0)
                return jnp.sum(lm, axis=1, keepdims=True)

            def live(st):
                return jnp.logical_and(st[0] >= 0, st[1] > SB_DEAD)

            def record(st):
                kb, c = st[0], st[2]
                carries[kb] = c
                c = c + row_sum(kb, False)
                return kb - 1, jnp.max(c), c

            c_diag = row_sum(i, True)
            first = lax.while_loop(live, record, (i - 1, jnp.max(c_diag), c_diag))[0] + 1

            def block(kb, c, gpre, dq, masked):
                rows = pl.ds(pl.multiple_of(kb * tq, tq), tq)
                ks = k_ref[rows, :]
                vs = v_ref[rows, :]
                z = _dot_nt(qm, ks)
                lb, lm = _log_sigmoids(z)
                if masked:
                    lm = jnp.where(causal, lm, 0.0)
                a = jnp.exp(lb + _split_dot(lm, tri_rev) + c)
                if masked:
                    a = jnp.where(causal, a, 0.0)
                g = a * _dot_nt(dom, vs)
                gsum = gpre + _split_dot(g, tri_excl)
                dz = g - (g + gsum) * jnp.exp(lb)
                if masked:
                    dz = jnp.where(causal, dz, 0.0)
                dzb = dz.astype(MXU)
                dq = dq + _dot(dzb, ks)
                dk_acc[rows, :] += _dot_tn(dzb, qm)
                dv_acc[rows, :] += _dot_tn(a.astype(MXU), dom)
                gpre = gpre + jnp.sum(g, axis=1, keepdims=True)
                return gpre, dq

            zero = jnp.zeros((tq, 1), F32)
            gpre, dq = lax.fori_loop(first, i, lambda kb, cr: block(kb, carries[kb], cr[0], cr[1], False),
                                     (zero, jnp.zeros((tq, LANES), F32)))
            return block(i, zero, gpre, dq, True)[1]

        dq_ref[...] = (jnp.where(_head_lanes(0), one_head(0), one_head(1)) * SCALE).astype(dq_ref.dtype)

        @pl.when(i == nq - 1)
        def _():
            dk_ref[...] = dk_acc[...].astype(dk_ref.dtype)
            dv_ref[...] = dv_acc[...].astype(dv_ref.dtype)

    blk = pl.BlockSpec((tq, LANES), lambda p, i: (i, p))
    full = pl.BlockSpec((s, LANES), lambda p, i: (0, p))
    outs, couts = _call(
        body,
        grid=(N_PAIR, nq),
        in_specs=[blk,
                  pl.BlockSpec((s, LANES), lambda p, i: (0, N_PAIR + p)),
                  pl.BlockSpec((s, LANES), lambda p, i: (0, 2 * N_PAIR + p)),
                  blk],
        out_specs=[blk, full, full],
        out_shape=[jax.ShapeDtypeStruct((s, SB_W), MXU)] * 3,
        scratch_shapes=[pltpu.VMEM((s, LANES), F32), pltpu.VMEM((s, LANES), F32),
                        pltpu.VMEM((nq, tq, 1), F32)],
        args=(qkv, qkv, qkv, do), name=name, comm=comm)
    return tuple(outs) if comm is None else (tuple(outs), couts)


def _bucket_table():
    i = np.arange(BLOCK)[:, None]
    j = np.arange(2 * BLOCK)[None, :]
    d = np.maximum(BLOCK + i - j, 0)
    max_exact = N_BUCKETS // 2
    df = np.maximum(d, 1).astype(np.float32)
    large = max_exact + (np.log(df / max_exact) / math.log(MAX_DISTANCE / max_exact)
                         * (N_BUCKETS - max_exact)).astype(np.int32)
    large = np.minimum(large, N_BUCKETS - 1)
    return np.where(d < max_exact, d, large).astype(np.int32)


def _build_bias(rel_bias, buckets, name):
    def body(rb_ref, bk_ref, out_ref):
        h = pl.program_id(0)
        bk = bk_ref[...]
        acc = jnp.zeros(bk.shape, F32)
        for b in range(N_BUCKETS):
            acc = jnp.where(bk == b, rb_ref[b, h], acc)
        out_ref[...] = acc

    return pl.pallas_call(
        body,
        grid=(SW_HEADS,),
        in_specs=[pl.BlockSpec(memory_space=pltpu.SMEM),
                  pl.BlockSpec((BLOCK, 2 * BLOCK), lambda h: (0, 0))],
        out_specs=pl.BlockSpec((None, BLOCK, 2 * BLOCK), lambda h: (h, 0, 0)),
        out_shape=jax.ShapeDtypeStruct((SW_HEADS, BLOCK, 2 * BLOCK), F32),
        name=name,
    )(rel_bias, buckets)


def _bias_grad(dbias_layers, buckets, name):
    n_l = len(dbias_layers)

    def body(*refs):
        bk = refs[n_l][...]
        out_ref = refs[n_l + 1]
        db = refs[0][...]
        for r in refs[1:n_l]:
            db = db + r[...]
        lane = lax.broadcasted_iota(jnp.int32, (1, LANES), 1)
        acc = jnp.zeros((1, LANES), F32)
        for b in range(N_BUCKETS):
            part = jnp.sum(jnp.where(bk == b, db, 0.0), axis=1, keepdims=True)
            tot = jnp.sum(part, axis=0, keepdims=True)
            acc = jnp.where(lane == b, tot, acc)
        out_ref[...] = acc

    hspec = pl.BlockSpec((None, BLOCK, 2 * BLOCK), lambda h: (h, 0, 0))
    return pl.pallas_call(
        body,
        grid=(SW_HEADS,),
        in_specs=[hspec] * n_l + [pl.BlockSpec((BLOCK, 2 * BLOCK), lambda h: (0, 0))],
        out_specs=pl.BlockSpec((None, 1, LANES), lambda h: (h, 0, 0)),
        out_shape=jax.ShapeDtypeStruct((SW_HEADS, 1, LANES), F32),
        name=name,
    )(*dbias_layers, buckets)


def _swa_scores(qb, kp, kc, bias_ref, hh, n):
    row = lax.broadcasted_iota(jnp.int32, (BLOCK, BLOCK), 0)
    col = lax.broadcasted_iota(jnp.int32, (BLOCK, BLOCK), 1)
    s1 = _dot_nt(qb, kp) + bias_ref[hh, :, :BLOCK]
    s2 = _dot_nt(qb, kc) + bias_ref[hh, :, BLOCK:]
    no_prev = jnp.where(n > 0, 0, BLOCK)
    s1 = jnp.where(col > row + no_prev, s1, NEG)
    s2 = jnp.where(col <= row, s2, NEG)
    return s1, s2


def _swa_lanes(p, hh):
    lane = lax.broadcasted_iota(jnp.int32, (1, LANES), 1)
    kv_half = jnp.zeros((1, LANES), jnp.int32) + (2 * p + hh) // SW_GROUP
    return jnp.where(lane >= HEAD_DIM, 1, 0) == kv_half, kv_half == hh


def _to_kv_lanes(x, kv_lanes, aligned):
    return jnp.where(kv_lanes, jnp.where(aligned, x, pltpu.roll(x, HEAD_DIM, 1)), 0.0)


def _swa_fwd(qkv, bias, sinks, name, comm=None):
    s = qkv.shape[0]
    nb = s // BLOCK
    q0 = 3 * N_PAIR
    kblk, vblk = q0 + N_PAIR, q0 + N_PAIR + 1

    def body(sink_ref, q_ref, k_ref, v_ref, bias_ref, o_ref, lse_ref):
        p = pl.program_id(0)

        def step(n, carry):
            r0 = pl.multiple_of(n * BLOCK, BLOCK)
            p0 = pl.multiple_of(jnp.maximum(n - 1, 0) * BLOCK, BLOCK)
            cur, prev = pl.ds(r0, BLOCK), pl.ds(p0, BLOCK)
            qf = q_ref[cur, :].astype(F32) * SCALE
            kp, kc, vp, vc = k_ref[prev, :], k_ref[cur, :], v_ref[prev, :], v_ref[cur, :]
            outs = []
            for hh in range(2):
                kv_lanes, aligned = _swa_lanes(p, hh)
                sink = sink_ref[2 * p + hh]
                qb = _to_kv_lanes(qf, kv_lanes, aligned).astype(MXU)
                s1, s2 = _swa_scores(qb, kp, kc, bias_ref, hh, n)
                m = jnp.maximum(jnp.maximum(jnp.max(s1, axis=1, keepdims=True),
                                            jnp.max(s2, axis=1, keepdims=True)), sink)
                e1 = jnp.exp(s1 - m)
                e2 = jnp.exp(s2 - m)
                den = (jnp.sum(e1, axis=1, keepdims=True) + jnp.sum(e2, axis=1, keepdims=True)
                       + jnp.exp(sink - m))
                o = _dot((e1 / den).astype(MXU), vp) + _dot((e2 / den).astype(MXU), vc)
                outs.append(jnp.where(aligned, o, pltpu.roll(o, HEAD_DIM, 1)))
                lse_ref[hh, cur, :] = m + jnp.log(den)
            o_ref[cur, :] = jnp.where(_head_lanes(0), outs[0], outs[1]).astype(o_ref.dtype)
            return carry

        lax.fori_loop(0, nb, step, 0)

    outs, couts = _call(
        body,
        grid=(N_PAIR,),
        in_specs=[pl.BlockSpec(memory_space=pltpu.SMEM),
                  pl.BlockSpec((s, LANES), lambda p: (0, q0 + p)),
                  pl.BlockSpec((s, LANES), lambda p: (0, kblk)),
                  pl.BlockSpec((s, LANES), lambda p: (0, vblk)),
                  pl.BlockSpec((2, BLOCK, 2 * BLOCK), lambda p: (p, 0, 0))],
        out_specs=[pl.BlockSpec((s, LANES), lambda p: (0, p)),
                   pl.BlockSpec((None, 2, s, 1), lambda p: (p, 0, 0, 0))],
        out_shape=[jax.ShapeDtypeStruct((s, SW_W), MXU), jax.ShapeDtypeStruct((N_PAIR, 2, s, 1), F32)],
        args=(sinks, qkv, qkv, qkv, bias), name=name, comm=comm)
    return tuple(outs) if comm is None else (tuple(outs), couts)


def _swa_bwd(qkv, bias, sinks, do, lse, name, comm=None):
    s = qkv.shape[0]
    nb = s // BLOCK
    q0 = 3 * N_PAIR
    kblk, vblk = q0 + N_PAIR, q0 + N_PAIR + 1

    def body(sink_ref, q_ref, k_ref, v_ref, bias_ref, do_ref, lse_ref,
             dq_ref, dk_ref, dv_ref, dbias_ref, dsink_ref, dk_acc, dv_acc):
        p = pl.program_id(0)

        @pl.when(p == 0)
        def _():
            dk_acc[...] = jnp.zeros_like(dk_acc)
            dv_acc[...] = jnp.zeros_like(dv_acc)

        dbias_ref[...] = jnp.zeros_like(dbias_ref)

        def step(n, dsink_rows):
            r0 = pl.multiple_of(n * BLOCK, BLOCK)
            p0 = pl.multiple_of(jnp.maximum(n - 1, 0) * BLOCK, BLOCK)
            cur, prev = pl.ds(r0, BLOCK), pl.ds(p0, BLOCK)
            qf = q_ref[cur, :].astype(F32) * SCALE
            dof = do_ref[cur, :].astype(F32)
            kp, kc, vp, vc = k_ref[prev, :], k_ref[cur, :], v_ref[prev, :], v_ref[cur, :]
            dqs, new_rows = [], []
            for hh in range(2):
                kv_lanes, aligned = _swa_lanes(p, hh)
                sink = sink_ref[2 * p + hh]
                qb = _to_kv_lanes(qf, kv_lanes, aligned).astype(MXU)
                dob = _to_kv_lanes(dof, kv_lanes, aligned).astype(MXU)
                lse_b = lse_ref[hh, cur, :]
                s1, s2 = _swa_scores(qb, kp, kc, bias_ref, hh, n)
                pr1 = jnp.exp(s1 - lse_b)
                pr2 = jnp.exp(s2 - lse_b)
                dpr1 = _dot_nt(dob, vp)
                dpr2 = _dot_nt(dob, vc)
                delta = (jnp.sum(pr1 * dpr1, axis=1, keepdims=True)
                         + jnp.sum(pr2 * dpr2, axis=1, keepdims=True))
                ds1 = pr1 * (dpr1 - delta)
                ds2 = pr2 * (dpr2 - delta)
                dbias_ref[hh, :, :BLOCK] += ds1
                dbias_ref[hh, :, BLOCK:] += ds2
                ds1b, ds2b = ds1.astype(MXU), ds2.astype(MXU)
                dq = _dot(ds1b, kp) + _dot(ds2b, kc)
                dqs.append(jnp.where(aligned, dq, pltpu.roll(dq, HEAD_DIM, 1)))
                dk_acc[prev, :] += _dot_tn(ds1b, qb)
                dk_acc[cur, :] += _dot_tn(ds2b, qb)
                dv_acc[prev, :] += _dot_tn(pr1.astype(MXU), dob)
                dv_acc[cur, :] += _dot_tn(pr2.astype(MXU), dob)
                new_rows.append(dsink_rows[hh] - jnp.exp(sink - lse_b) * delta)
            dq_ref[cur, :] = (jnp.where(_head_lanes(0), dqs[0], dqs[1]) * SCALE).astype(dq_ref.dtype)
            return tuple(new_rows)

        zero = jnp.zeros((BLOCK, 1), F32)
        rows = lax.fori_loop(0, nb, step, (zero, zero))
        for hh in range(2):
            dsink_ref[hh] = jnp.broadcast_to(jnp.sum(rows[hh], axis=0, keepdims=True), (1, LANES))

        @pl.when(p == N_PAIR - 1)
        def _():
            dk_ref[...] = dk_acc[...].astype(dk_ref.dtype)
            dv_ref[...] = dv_acc[...].astype(dv_ref.dtype)

    pair = pl.BlockSpec((s, LANES), lambda p: (0, p))
    kv_out = pl.BlockSpec((s, LANES), lambda p: (0, 0))
    bspec = pl.BlockSpec((2, BLOCK, 2 * BLOCK), lambda p: (p, 0, 0))
    outs, couts = _call(
        body,
        grid=(N_PAIR,),
        in_specs=[pl.BlockSpec(memory_space=pltpu.SMEM),
                  pl.BlockSpec((s, LANES), lambda p: (0, q0 + p)),
                  pl.BlockSpec((s, LANES), lambda p: (0, kblk)),
                  pl.BlockSpec((s, LANES), lambda p: (0, vblk)),
                  bspec, pair,
                  pl.BlockSpec((None, 2, s, 1), lambda p: (p, 0, 0, 0))],
        out_specs=[pair, kv_out, kv_out, bspec, pl.BlockSpec((2, 1, LANES), lambda p: (p, 0, 0))],
        out_shape=[jax.ShapeDtypeStruct((s, SW_W), MXU),
                   jax.ShapeDtypeStruct((s, LANES), MXU),
                   jax.ShapeDtypeStruct((s, LANES), MXU),
                   jax.ShapeDtypeStruct((SW_HEADS, BLOCK, 2 * BLOCK), F32),
                   jax.ShapeDtypeStruct((SW_HEADS, 1, LANES), F32)],
        scratch_shapes=[pltpu.VMEM((s, LANES), F32), pltpu.VMEM((s, LANES), F32)],
        args=(sinks, qkv, qkv, qkv, bias, do, lse), name=name, comm=comm)
    return tuple(outs) if comm is None else (tuple(outs), couts)


class _NoPlan:
    def comm(self, name):
        return None

    def done(self, name, outs):
        pass

    def grad(self, layer, name, value):
        pass

    def layer_done(self, layer):
        pass


def _run(plan, fn, *args, name, **kw):
    comm = plan.comm(name)
    if comm is None:
        return fn(*args, name=name, **kw)
    res, outs = fn(*args, name=name, comm=comm, **kw)
    plan.done(name, outs)
    return res


def _layer_fwd(x, p, w, g_mix, g_mlp, g_pe, sinks, bias, tag, plan):
    h1, r1 = _rms_fwd(x, g_mix, f"rms_mix_{tag}")
    qkv = _run(plan, _mm, h1, w["w_qkv"], out_dtypes=(MXU,), name=f"proj_qkv_{tag}")
    gates = _run(plan, _mm, h1, w["w_gate"], name=f"proj_gate_{tag}")
    oa = _run(plan, _sb_fwd, qkv, name=f"sb_fwd_{tag}")
    ob, lse = _run(plan, _swa_fwd, qkv, bias, sinks, name=f"swa_fwd_{tag}")
    merged = _mix_fwd(oa, ob, w["w_up_a"], w["w_up_b"], gates, f"mix_fwd_{tag}")
    x1 = _run(plan, _mm, merged, w["w_o"], extras=(x,), epi=lambda acc, res: res + acc, name=f"out_proj_{tag}")
    h2, r2 = _rms_fwd(x1, g_mlp, f"rms_mlp_{tag}")
    u, act = _run(plan, _mm, h2, w["w_ff1"], epi=lambda acc: (acc, jnp.square(jnp.maximum(acc, 0.0))),
                  out_dtypes=(F32, MXU), name=f"ff1_{tag}")
    x2 = _run(plan, _mm, act, w["w_ff2"], extras=(x1,), epi=lambda acc, res: res + acc, name=f"ff2_{tag}")
    h3, r3 = _rms_fwd(x2, g_pe, f"rms_pe_{tag}")
    x3 = _ple(p, w["w_pe"], h3, w["w_pg"], x2, backward=False, name=f"ple_fwd_{tag}")
    saved = dict(x=x, h1=h1, r1=r1, gates=gates, qkv=qkv, lse=lse, oa=oa, ob=ob, merged=merged,
                 x1=x1, h2=h2, r2=r2, u=u, act=act, x2=x2, h3=h3, r3=r3)
    return x3, saved


def _layer_bwd(dx3, sv, p, w, g_mix, g_mlp, g_pe, sinks, bias, layer, plan):
    tag = f"l{layer}"
    gw = {}
    wire = (WIRE,)

    def dw(name, a, b):
        gw[name] = _run(plan, _mm, a, b, ta=True, out_dtypes=wire, name=f"d{name}_{tag}")
        plan.grad(layer, name, gw[name])

    dpe, dgt = _ple(p, w["w_pe"], sv["h3"], w["w_pg"], dx3, backward=True, name=f"ple_bwd_{tag}")
    dw("w_pe", p, dpe)
    dw("w_pg", sv["h3"], dgt)
    dh3 = _run(plan, _mm, dgt, w["w_pg"], tb=True, name=f"dh_pe_{tag}")
    dx2, dx2b, dg_pe = _rms_bwd(sv["x2"], sv["r3"], g_pe, dh3, dx3, f"rms_pe_bwd_{tag}")
    dw("w_ff2", sv["act"], dx2b)
    du = _run(plan, _mm, dx2b, w["w_ff2"], tb=True, extras=(sv["u"],),
              epi=lambda acc, u: acc * (2.0 * jnp.maximum(u, 0.0)), out_dtypes=(MXU,), name=f"dact_{tag}")
    dw("w_ff1", sv["h2"], du)
    dh2 = _run(plan, _mm, du, w["w_ff1"], tb=True, name=f"dh_mlp_{tag}")
    dx1, dx1b, dg_mlp = _rms_bwd(sv["x1"], sv["r2"], g_mlp, dh2, dx2, f"rms_mlp_bwd_{tag}")
    dw("w_o", sv["merged"], dx1b)
    dya, dyb, dga, dgb = _mix_bwd(dx1b, w["w_o"], sv["oa"], sv["ob"], w["w_up_a"], w["w_up_b"],
                                  sv["gates"], f"mix_bwd_{tag}")
    dw("w_up_a", sv["oa"], dya)
    dw("w_up_b", sv["ob"], dyb)
    doa = _run(plan, _mm, dya, w["w_up_a"], tb=True, out_dtypes=(MXU,), name=f"do_a_{tag}")
    dob = _run(plan, _mm, dyb, w["w_up_b"], tb=True, out_dtypes=(MXU,), name=f"do_b_{tag}")
    dqb, dkb, dvb, dbias, dsink = _run(plan, _swa_bwd, sv["qkv"], bias, sinks, dob, sv["lse"],
                                       name=f"swa_bwd_{tag}")
    dqa, dka, dva = _run(plan, _sb_bwd, sv["qkv"], doa, name=f"sb_bwd_{tag}")
    dqkv = jnp.concatenate([dqa, dka, dva, dqb, dkb, dvb], axis=1)
    gw_qkv = _mm(sv["h1"], dqkv, ta=True, out_dtypes=wire, name=f"dw_qkv_{tag}")
    gw_ga = _mm(sv["h1"], dga, ta=True, out_dtypes=wire, name=f"dw_ga_{tag}")
    gw_gb = _mm(sv["h1"], dgb, ta=True, out_dtypes=wire, name=f"dw_gb_{tag}")
    gw["w_in"] = jnp.concatenate([gw_qkv, gw_ga, gw_gb], axis=1)
    plan.grad(layer, "w_in", gw["w_in"])
    d = dga.shape[1]
    add = lambda acc, res: res + acc
    dh1 = _mm(dqkv, w["w_qkv"], tb=True, tk=768, name=f"dh_qkv_{tag}")
    dh1 = _mm(dga, w["w_gate"][:, :d], tb=True, extras=(dh1,), epi=add, name=f"dh_ga_{tag}")
    dh1 = _mm(dgb, w["w_gate"][:, d:], tb=True, extras=(dh1,), epi=add, name=f"dh_gb_{tag}")
    dx, _, dg_mix = _rms_bwd(sv["x"], sv["r1"], g_mix, dh1, dx1, f"rms_mix_bwd_{tag}")
    small = dict(g_mix=dg_mix, g_mlp=dg_mlp, g_pe=dg_pe, sinks=dsink[:, 0, 0], dbias=dbias)
    return dx, gw, small


def _local_step(x, p, target, weights, g_mix, g_mlp, g_pe, g_final, sinks, rel_bias, plan=None):
    plan = _NoPlan() if plan is None else plan
    depth = g_mix.shape[0]
    buckets = jnp.asarray(_bucket_table())
    bias = _build_bias(rel_bias, buckets, "build_bias")
    saved, wfull = [], []
    h = x
    for l in range(depth):
        wfull.append(weights(l))
        h, sv = _layer_fwd(h, p[l], wfull[l], g_mix[l:l + 1], g_mlp[l:l + 1], g_pe[l:l + 1],
                           sinks[l], bias, f"l{l}", plan)
        saved.append(sv)
    loss_row, dx, dg_final = _loss_head(h, g_final[None, :], target, "loss_head")
    gws = [None] * depth
    smalls = [None] * depth
    for l in reversed(range(depth)):
        dx, gws[l], smalls[l] = _layer_bwd(dx, saved[l], p[l], wfull[l], g_mix[l:l + 1], g_mlp[l:l + 1],
                                           g_pe[l:l + 1], sinks[l], bias, l, plan)
        plan.layer_done(l)
    drel = _bias_grad([sm["dbias"] for sm in smalls], buckets, "bias_grad")[:, 0, :N_BUCKETS].T
    small = dict(
        g_mix=jnp.concatenate([sm["g_mix"] for sm in smalls], axis=0),
        g_mlp=jnp.concatenate([sm["g_mlp"] for sm in smalls], axis=0),
        g_pe=jnp.concatenate([sm["g_pe"] for sm in smalls], axis=0),
        g_final=dg_final[0],
        sinks=jnp.stack([sm["sinks"] for sm in smalls], axis=0),
        rel_bias=drel,
    )
    return loss_row, dx, gws, small


MESH_ID = pl.DeviceIdType.MESH
ANY = pl.BlockSpec(memory_space=pl.ANY)


def _position():
    return lax.axis_index("x"), lax.axis_index("y"), lax.axis_index("c")


def _run_comm(comm, name):
    ci, co = len(comm.inputs), len(comm.out_shapes)

    def body(*refs):
        cin, cout, csem = refs[:ci], refs[ci:ci + co], refs[ci + co:]
        pos = _position()
        comm.start(pos, cin, cout, csem)
        comm.finish(pos, cin, cout, csem)

    return pl.pallas_call(body, out_shape=comm.out_shapes, in_specs=[ANY] * ci, out_specs=[ANY] * co,
                          scratch_shapes=comm.sems, name=name)(*comm.inputs)


def _gather_comm(shards):
    n = len(shards)

    def copies(pos, x_refs, out_refs, sems):
        send_sems, recv_sems, local_sems = sems
        x, y, c = pos
        me, sibling = (x, y, c), (x, y, 1 - c)
        chips = [(1 - x, y), (x, 1 - y), (1 - x, 1 - y)]

        def slot(a, px, py, pc):
            return out_refs[a].at[4 * px + 2 * py + pc]

        def copy(a, k, block, to, src=None):
            return pltpu.make_async_remote_copy(
                src_ref=slot(a, *block) if src is None else src, dst_ref=slot(a, *block),
                send_sem=send_sems.at[a, k], recv_sem=recv_sems.at[a, k],
                device_id=to, device_id_type=MESH_ID)

        mine = [pltpu.make_async_copy(x_refs[a], slot(a, *me), local_sems.at[a]) for a in range(n)]
        first = []
        for a in range(n):
            first.append(copy(a, 0, me, sibling, src=x_refs[a]))
            first += [copy(a, 1 + j, me, (*chip, c), src=x_refs[a]) for j, chip in enumerate(chips)]
        return me, sibling, chips, copy, mine, first

    def start(pos, x_refs, out_refs, sems):
        _, _, _, _, mine, first = copies(pos, x_refs, out_refs, sems)
        for cp in mine + first:
            cp.start()

    def finish(pos, x_refs, out_refs, sems):
        me, sibling, chips, copy, mine, first = copies(pos, x_refs, out_refs, sems)
        c = pos[2]
        passed = []
        for j, chip in enumerate(chips):
            for a in range(n):
                copy(a, 1 + j, (*chip, c), me).wait_recv()
                fwd = copy(a, 4 + j, (*chip, c), sibling)
                fwd.start()
                passed.append(fwd)
        for a in range(n):
            copy(a, 0, sibling, me).wait_recv()
            for j, chip in enumerate(chips):
                copy(a, 4 + j, (*chip, 1 - c), me).wait_recv()
        for cp in first + passed:
            cp.wait_send()
        for cp in mine:
            cp.wait()

    return _Comm(shards, [jax.ShapeDtypeStruct((N_DEV,) + s.shape, s.dtype) for s in shards],
                 [pltpu.SemaphoreType.DMA((n, 7)), pltpu.SemaphoreType.DMA((n, 7)),
                  pltpu.SemaphoreType.DMA((n,))], start, finish)


def _exchange_comm(arrays, n_slots, route):
    n = len(arrays)

    def copies(pos, in_refs, out_refs, sems):
        send_sems, recv_sems = sems
        out = []
        for a in range(n):
            for j in range(n_slots):
                src_slot, peer = route(pos, j)
                out.append(pltpu.make_async_remote_copy(
                    src_ref=in_refs[a].at[src_slot], dst_ref=out_refs[a].at[j],
                    send_sem=send_sems.at[a, j], recv_sem=recv_sems.at[a, j],
                    device_id=peer, device_id_type=MESH_ID))
        return out

    def start(pos, in_refs, out_refs, sems):
        for cp in copies(pos, in_refs, out_refs, sems):
            cp.start()

    def finish(pos, in_refs, out_refs, sems):
        for cp in copies(pos, in_refs, out_refs, sems):
            cp.wait()

    return _Comm(arrays, [jax.ShapeDtypeStruct((n_slots,) + g.shape[1:], g.dtype) for g in arrays],
                 [pltpu.SemaphoreType.DMA((n, n_slots)), pltpu.SemaphoreType.DMA((n, n_slots))], start, finish)


def _rs_sibling_comm(gs):
    return _exchange_comm(gs, 4, lambda pos, j: (2 * j + (1 - pos[2]), (pos[0], pos[1], 1 - pos[2])))


def _chip_of(k, x, y):
    return x ^ ((k + 1) & 1), y ^ (((k + 1) >> 1) & 1)


def _chip_partials(pos, gs, recvs, name):
    n = len(gs)

    def body(pos_ref, *refs):
        for a in range(n):
            refs[2 * n + a][...] = (refs[a][...].astype(F32) + refs[n + a][...].astype(F32)
                                    ).astype(refs[2 * n + a].dtype)

    def g_map(k, pos_ref):
        cx, cy = _chip_of(k, pos_ref[0], pos_ref[1])
        return (4 * cx + 2 * cy + pos_ref[2], 0, 0)

    def r_map(k, pos_ref):
        cx, cy = _chip_of(k, pos_ref[0], pos_ref[1])
        return (2 * cx + cy, 0, 0)

    slab = [(None,) + g.shape[1:] for g in gs]
    return pl.pallas_call(
        body,
        grid_spec=pltpu.PrefetchScalarGridSpec(
            num_scalar_prefetch=1,
            grid=(4,),
            in_specs=[pl.BlockSpec(sh, g_map) for sh in slab] + [pl.BlockSpec(sh, r_map) for sh in slab],
            out_specs=[pl.BlockSpec(sh, lambda k, pos_ref: (k, 0, 0)) for sh in slab],
        ),
        out_shape=[jax.ShapeDtypeStruct((4,) + g.shape[1:], g.dtype) for g in gs],
        compiler_params=_cparams(),
        name=name,
    )(pos, *gs, *recvs)


def _rs_chips_comm(parts):
    return _exchange_comm(parts, 3, lambda pos, k: (k, (*_chip_of(k, pos[0], pos[1]), pos[2])))


def _adamw_math(w, g, m, v):
    m = ADAM_B1 * m + (1.0 - ADAM_B1) * g
    v = ADAM_B2 * v + (1.0 - ADAM_B2) * (g * g)
    m_hat = m / (1.0 - ADAM_B1 ** ADAM_STEP)
    v_hat = v / (1.0 - ADAM_B2 ** ADAM_STEP)
    delta = -ADAM_LR * (m_hat / (jnp.sqrt(v_hat) + ADAM_EPS) + ADAM_WD * w)
    return delta, m, v


def _adamw_weight(parts, recvs, w, m, v, name):
    depth, a, b = w.shape
    ta = _tile(a, 256, unit=16)
    ni = a // ta

    def body(*refs):
        p_refs, r_refs = refs[:depth], refs[depth:2 * depth]
        w_ref, m_ref, v_ref = refs[2 * depth:2 * depth + 3]
        g_out, d_out, m_out, v_out = refs[2 * depth + 3:]
        layer = pl.program_id(0)
        g = jnp.zeros((ta, b), F32)
        for l in range(depth):
            gl = p_refs[l][...].astype(F32)
            for k in range(3):
                gl = gl + r_refs[l][k].astype(F32)
            g = jnp.where(layer == l, gl, g)
        delta, m_new, v_new = _adamw_math(w_ref[...], g, m_ref[...], v_ref[...])
        g_out[...] = g
        d_out[...] = delta
        m_out[...] = m_new
        v_out[...] = v_new

    def hold(l):
        return lambda layer, i: jnp.where(layer == l, i, jnp.where(layer < l, 0, ni - 1))

    p_specs = [pl.BlockSpec((None, ta, b), (lambda layer, i, f=hold(l): (3, f(layer, i), 0))) for l in range(depth)]
    r_specs = [pl.BlockSpec((3, ta, b), (lambda layer, i, f=hold(l): (0, f(layer, i), 0))) for l in range(depth)]
    row = pl.BlockSpec((None, ta, b), lambda layer, i: (layer, i, 0))
    return pl.pallas_call(
        body,
        grid=(depth, ni),
        in_specs=p_specs + r_specs + [row, row, row],
        out_specs=[row] * 4,
        out_shape=[jax.ShapeDtypeStruct(w.shape, F32)] * 4,
        compiler_params=_cparams(),
        name=name,
    )(*parts, *recvs, w, m, v)


def _adamw_replicated(gathered, w, m, v, name):
    r, lanes = w.shape

    def body(g_ref, w_ref, m_ref, v_ref, g_out, d_out, m_out, v_out):
        g = g_ref[0]
        for k in range(1, N_DEV):
            g = g + g_ref[k]
        delta, m_new, v_new = _adamw_math(w_ref[...], g, m_ref[...], v_ref[...])
        g_out[...] = g
        d_out[...] = delta
        m_out[...] = m_new
        v_out[...] = v_new

    return pl.pallas_call(
        body,
        out_shape=[jax.ShapeDtypeStruct((r, lanes), F32)] * 4,
        name=name,
    )(gathered, w, m, v)


def _full_weight(name, gathered):
    _, a, b = gathered.shape
    if name in COL_SHARDED:
        return gathered.transpose(1, 0, 2).reshape(a, N_DEV * b)
    return gathered.reshape(N_DEV * a, b)


def _to_slabs(name, gfull, shard_shape):
    a, b = shard_shape
    if name in COL_SHARDED:
        return gfull.reshape(a, N_DEV, b).transpose(1, 0, 2)
    return gfull.reshape(N_DEV, a, b)


def _pack_small(arrs):
    rows = []
    for a in arrs:
        flat = a.astype(F32).reshape(-1)
        pad = (-flat.shape[0]) % LANES
        rows.append(jnp.pad(flat, (0, pad)).reshape(-1, LANES))
    packed = jnp.concatenate(rows, axis=0)
    return jnp.pad(packed, ((0, (-packed.shape[0]) % 8), (0, 0)))


def _unpack_small(packed, shapes):
    out, off = [], 0
    for shp in shapes:
        n = math.prod(shp)
        rows = -(-n // LANES)
        out.append(packed[off:off + rows].reshape(-1)[:n].reshape(shp))
        off += rows
    return out


def _of(layer, *names):
    return tuple((layer, n) for n in names)


MIXER_W = ("w_in", "w_up_a", "w_up_b", "w_o")
MLP_W = ("w_ff1", "w_ff2", "w_pe", "w_pg")

GATHERS = (
    (None, _of(0, *MIXER_W)),
    ("sb_fwd_l0", _of(0, *MLP_W) + _of(1, "w_ff1")),
    ("swa_fwd_l0", _of(1, *MIXER_W)),
    ("ff1_l0", _of(1, "w_ff2", "w_pe", "w_pg")),
)
REDUCES = (
    (_of(1, *WEIGHTS), "dact_l0", "sb_bwd_l0"),
    (_of(0, *MLP_W), "dh_mlp_l0", "swa_bwd_l0"),
    (_of(0, "w_o", "w_up_a", "w_up_b"), "do_a_l0", "sb_bwd_l0"),
    (_of(0, "w_in"), None, None),
)


def _merge_comms(comms):
    if len(comms) == 1:
        return comms[0]

    def cuts(counts):
        edges = [0]
        for c in counts:
            edges.append(edges[-1] + c)
        return [slice(a, b) for a, b in zip(edges[:-1], edges[1:])]

    s_in = cuts([len(c.inputs) for c in comms])
    s_out = cuts([len(c.out_shapes) for c in comms])
    s_sem = cuts([len(c.sems) for c in comms])

    def start(pos, cin, cout, csem):
        for c, i, o, s in zip(comms, s_in, s_out, s_sem):
            c.start(pos, cin[i], cout[o], csem[s])

    def finish(pos, cin, cout, csem):
        for c, i, o, s in zip(comms, s_in, s_out, s_sem):
            c.finish(pos, cin[i], cout[o], csem[s])

    return _Comm(sum([c.inputs for c in comms], []), sum([c.out_shapes for c in comms], []),
                 sum([c.sems for c in comms], []), start, finish)


class _LayerWeights:
    def __init__(self, full, layer):
        self.full, self.layer, self.cache = full, layer, {}

    def __getitem__(self, name):
        if name not in self.cache:
            if name == "w_qkv":
                self.cache[name] = self.full[(self.layer, "w_in")][:, :QKV_COLS]
            elif name == "w_gate":
                self.cache[name] = self.full[(self.layer, "w_in")][:, QKV_COLS:]
            else:
                self.cache[name] = self.full[(self.layer, name)]
        return self.cache[name]


class _Plan:
    def __init__(self, w_sh, pos):
        self.w_sh = dict(zip(WEIGHTS, w_sh))
        self.pos = pos
        self.full, self.gw, self.parts, self.recv = {}, {}, {}, {}
        self.slabs = {}
        self.hosted = {}
        for i, (host, _) in enumerate(GATHERS):
            if host is not None:
                self.hosted.setdefault(host, []).append(("gather", i))
        for i, (_, sib_host, chip_host) in enumerate(REDUCES):
            assert (sib_host is None) == (chip_host is None)
            if sib_host is not None:
                self.hosted.setdefault(sib_host, []).append(("sibling", i))
                self.hosted.setdefault(chip_host, []).append(("chips", i))

    def _gather(self, i):
        return _gather_comm([self.w_sh[n][layer].astype(WIRE) for layer, n in GATHERS[i][1]])

    def _gathered(self, i, outs):
        for (layer, n), g in zip(GATHERS[i][1], outs):
            self.full[(layer, n)] = _full_weight(n, g)

    def weights(self, layer):
        for i, (host, items) in enumerate(GATHERS):
            if host is None and items[0][0] == layer:
                self._gathered(i, _run_comm(self._gather(i), f"gather_{i}"))
        return _LayerWeights(self.full, layer)

    def grad(self, layer, name, value):
        self.gw[(layer, name)] = value

    def _sibling(self, i):
        self.slabs[i] = [_to_slabs(n, self.gw[(layer, n)], self.w_sh[n].shape[1:]) for layer, n in REDUCES[i][0]]
        return _rs_sibling_comm(self.slabs[i])

    def _sibling_done(self, i, outs):
        parts = _chip_partials(self.pos, self.slabs[i], outs, f"chip_partials_{i}")
        for item, part in zip(REDUCES[i][0], parts):
            self.parts[item] = part

    def _chips(self, i):
        return _rs_chips_comm([self.parts[item] for item in REDUCES[i][0]])

    def _chips_done(self, i, outs):
        for item, r in zip(REDUCES[i][0], outs):
            self.recv[item] = r

    def layer_done(self, layer):
        for i, (items, sib_host, _) in enumerate(REDUCES):
            if sib_host is None and items[0][0] == layer:
                self._sibling_done(i, _run_comm(self._sibling(i), f"reduce_sibling_{i}"))
                self._chips_done(i, _run_comm(self._chips(i), f"reduce_chips_{i}"))

    def comm(self, name):
        if name not in self.hosted:
            return None
        make = {"gather": self._gather, "sibling": self._sibling, "chips": self._chips}
        return _merge_comms([make[kind](i) for kind, i in self.hosted[name]])

    def done(self, name, outs):
        took = {"gather": self._gathered, "sibling": self._sibling_done, "chips": self._chips_done}
        off = 0
        for kind, i in self.hosted[name]:
            n = len(GATHERS[i][1]) if kind == "gather" else len(REDUCES[i][0])
            took[kind](i, outs[off:off + n])
            off += n


def kernel(x, p, w_in, w_up_a, w_up_b, w_o, w_ff1, w_ff2, w_pe, w_pg, g_mix, g_mlp, g_pe, g_final, sinks, rel_bias, loss_target, m_w_in, m_w_up_a, m_w_up_b, m_w_o, m_w_ff1, m_w_ff2, m_w_pe, m_w_pg, m_g_mix, m_g_mlp, m_g_pe, m_g_final, m_sinks, m_rel_bias, v_w_in, v_w_up_a, v_w_up_b, v_w_o, v_w_ff1, v_w_ff2, v_w_pe, v_w_pg, v_g_mix, v_g_mlp, v_g_pe, v_g_final, v_sinks, v_rel_bias):
    w_sh = [w_in, w_up_a, w_up_b, w_o, w_ff1, w_ff2, w_pe, w_pg]
    m_sh = [m_w_in, m_w_up_a, m_w_up_b, m_w_o, m_w_ff1, m_w_ff2, m_w_pe, m_w_pg]
    v_sh = [v_w_in, v_w_up_a, v_w_up_b, v_w_o, v_w_ff1, v_w_ff2, v_w_pe, v_w_pg]
    depth = w_in.shape[0]
    assert depth == 2 and x.shape[-1] * 2 + QKV_COLS == w_in.shape[2] * N_DEV

    px, py, pc = _position()
    plan = _Plan(w_sh, jnp.stack([px, py, pc]).astype(jnp.int32))
    loss_row, grad_x, _, small = _local_step(
        x[0], p[:, 0], loss_target[0], plan.weights, g_mix, g_mlp, g_pe, g_final, sinks, rel_bias, plan=plan)

    grad_w, delta_w, new_m, new_v = [], [], [], []
    for a, name in enumerate(WEIGHTS):
        outs = _adamw_weight([plan.parts[(l, name)] for l in range(depth)],
                             [plan.recv[(l, name)] for l in range(depth)],
                             w_sh[a], m_sh[a], v_sh[a], f"adamw_{name}")
        for lst, o in zip((grad_w, delta_w, new_m, new_v), outs):
            lst.append(o)

    small_w = [g_mix, g_mlp, g_pe, g_final, sinks, rel_bias]
    small_m = [m_g_mix, m_g_mlp, m_g_pe, m_g_final, m_sinks, m_rel_bias]
    small_v = [v_g_mix, v_g_mlp, v_g_pe, v_g_final, v_sinks, v_rel_bias]
    small_shapes = [a.shape for a in small_w] + [(1,)]
    zero = jnp.zeros((1,), F32)
    small_g = _pack_small([small[n] for n in SMALL] + [loss_row[0, :1]])
    small_all = _run_comm(_gather_comm([small_g]), "gather_small")[0]
    packed_s = _adamw_replicated(small_all, _pack_small(small_w + [zero]), _pack_small(small_m + [zero]),
                                 _pack_small(small_v + [zero + 1.0]), "adamw_replicated")
    sg, sd, sm, sv = [_unpack_small(t, small_shapes) for t in packed_s]
    loss = sg[-1][0]

    return (loss, grad_x[None], *grad_w, *sg[:-1], *delta_w, *sd[:-1], *new_m, *sm[:-1], *new_v, *sv[:-1])
```

```python
import functools
import math

import numpy as np
import jax
import jax.numpy as jnp
from jax import lax
from jax.experimental import pallas as pl
from jax.experimental.pallas import tpu as pltpu

F32 = jnp.float32
MXU = jnp.bfloat16
WIRE = jnp.bfloat16

HEAD_DIM = 64
SB_HEADS = 8
SW_HEADS = 8
SW_KV = 2
SW_GROUP = SW_HEADS // SW_KV
BLOCK = 128
N_BUCKETS = 32
MAX_DISTANCE = 128
EPS = 1e-6
SCALE = HEAD_DIM ** -0.5
SB_W = SB_HEADS * HEAD_DIM
SW_W = SW_HEADS * HEAD_DIM
QKV_COLS = 3 * SB_W + SW_W + 2 * SW_KV * HEAD_DIM
N_DEV = 8
LANES = 128
N_PAIR = SB_HEADS // 2
NEG = -1e30

ADAM_LR = 0.001
ADAM_B1 = 0.9
ADAM_B2 = 0.999
ADAM_EPS = 1e-08
ADAM_WD = 0.01
ADAM_STEP = 10

VMEM_LIMIT = 48 * 1024 * 1024
SB_TQ = 256
SB_DEAD = -105.0

WEIGHTS = ("w_in", "w_up_a", "w_up_b", "w_o", "w_ff1", "w_ff2", "w_pe", "w_pg")
COL_SHARDED = ("w_in", "w_up_a", "w_up_b", "w_ff1", "w_pe")
SMALL = ("g_mix", "g_mlp", "g_pe", "g_final", "sinks", "rel_bias")


def _cparams(**kw):
    return pltpu.CompilerParams(vmem_limit_bytes=VMEM_LIMIT, **kw)


def _dot(a, b):
    return jnp.dot(a, b, preferred_element_type=F32)


def _dot_nt(a, b):
    return lax.dot_general(a, b, (((1,), (1,)), ((), ())), preferred_element_type=F32)


def _dot_tn(a, b):
    return lax.dot_general(a, b, (((0,), (0,)), ((), ())), preferred_element_type=F32)


def _tile(n, target, unit=LANES):
    if n <= target:
        return n
    t = (target // unit) * unit
    while t > unit and n % t:
        t -= unit
    assert n % t == 0, (n, target)
    return t


def _sigmoid(x):
    return 1.0 / (1.0 + jnp.exp(-x))


class _Comm:
    def __init__(self, inputs, out_shapes, sems, start, finish):
        self.inputs, self.out_shapes, self.sems = list(inputs), list(out_shapes), list(sems)
        self.start, self.finish = start, finish


def _call(body, *, grid, in_specs, out_specs, out_shape, scratch_shapes=(), args, name, comm=None):
    n_in, n_out, n_scr = len(in_specs), len(out_shape), len(scratch_shapes)
    if comm is None:
        outs = pl.pallas_call(body, grid=grid, in_specs=list(in_specs), out_specs=list(out_specs),
                              out_shape=list(out_shape), scratch_shapes=list(scratch_shapes),
                              compiler_params=_cparams(), name=name)(*args)
        return list(outs), None
    ci, co = len(comm.inputs), len(comm.out_shapes)
    any_spec = pl.BlockSpec(memory_space=pl.ANY)

    def wrapped(*refs):
        ins, cin = refs[:n_in], refs[n_in:n_in + ci]
        o0 = n_in + ci
        outs, cout = refs[o0:o0 + n_out], refs[o0 + n_out:o0 + n_out + co]
        s0 = o0 + n_out + co
        scr, csem = refs[s0:s0 + n_scr], refs[s0 + n_scr:]
        ids = [pl.program_id(d) for d in range(len(grid))]
        first = functools.reduce(jnp.logical_and, [i == 0 for i in ids])
        last = functools.reduce(jnp.logical_and, [i == g - 1 for i, g in zip(ids, grid)])
        pos = (lax.axis_index("x"), lax.axis_index("y"), lax.axis_index("c"))

        @pl.when(first)
        def _():
            comm.start(pos, cin, cout, csem)

        body(*ins, *outs, *scr)

        @pl.when(last)
        def _():
            comm.finish(pos, cin, cout, csem)

    outs = pl.pallas_call(wrapped, grid=grid, in_specs=list(in_specs) + [any_spec] * ci,
                          out_specs=list(out_specs) + [any_spec] * co,
                          out_shape=list(out_shape) + comm.out_shapes,
                          scratch_shapes=list(scratch_shapes) + comm.sems,
                          compiler_params=_cparams(), name=name)(*args, *comm.inputs)
    return list(outs[:n_out]), list(outs[n_out:])


def _mm(a, b, *, ta=False, tb=False, extras=(), epi=None, out_dtypes=(F32,),
        tm=1024, tn=1024, tk=1024, name, comm=None):
    if ta:
        kdim, m = a.shape
    else:
        m, kdim = a.shape
    n = b.shape[0] if tb else b.shape[1]
    assert (b.shape[1] if tb else b.shape[0]) == kdim
    tm, tn, tk = _tile(m, tm), _tile(n, tn), _tile(kdim, tk)
    nk = kdim // tk
    n_ex, n_out = len(extras), len(out_dtypes)

    a_spec = (pl.BlockSpec((tk, tm), lambda i, j, k: (k, i)) if ta
              else pl.BlockSpec((tm, tk), lambda i, j, k: (i, k)))
    b_spec = (pl.BlockSpec((tn, tk), lambda i, j, k: (j, k)) if tb
              else pl.BlockSpec((tk, tn), lambda i, j, k: (k, j)))
    ex_specs = []
    for e in extras:
        assert e.shape == (m, n), (e.shape, m, n)
        ex_specs.append(pl.BlockSpec((tm, tn), lambda i, j, k: (i, j)))
    out_spec = pl.BlockSpec((tm, tn), lambda i, j, k: (i, j))

    def body(a_ref, b_ref, *rest):
        ex_refs = rest[:n_ex]
        out_refs = rest[n_ex:n_ex + n_out]
        acc = rest[-1]
        k = pl.program_id(2)

        @pl.when(k == 0)
        def _():
            acc[...] = jnp.zeros_like(acc)

        av = a_ref[...].astype(MXU)
        bv = b_ref[...].astype(MXU)
        if ta:
            acc[...] += _dot_tn(av, bv)
        elif tb:
            acc[...] += _dot_nt(av, bv)
        else:
            acc[...] += _dot(av, bv)

        @pl.when(k == nk - 1)
        def _():
            res = acc[...]
            if epi is not None:
                res = epi(res, *[e[...] for e in ex_refs])
            if not isinstance(res, tuple):
                res = (res,)
            for o_ref, r in zip(out_refs, res):
                o_ref[...] = r.astype(o_ref.dtype)

    outs, couts = _call(
        body,
        grid=(m // tm, n // tn, nk),
        in_specs=[a_spec, b_spec] + ex_specs,
        out_specs=[out_spec] * n_out,
        out_shape=[jax.ShapeDtypeStruct((m, n), dt) for dt in out_dtypes],
        scratch_shapes=[pltpu.VMEM((tm, tn), F32)],
        args=(a, b, *extras), name=name, comm=comm)
    res = outs[0] if n_out == 1 else tuple(outs)
    return res if comm is None else (res, couts)


def _rms_fwd(x, g, name):
    s, d = x.shape
    tr = _tile(s, 256)

    def body(x_ref, g_ref, h_ref, r_ref):
        xf = x_ref[...]
        r = lax.rsqrt(jnp.mean(xf * xf, axis=-1, keepdims=True) + EPS)
        h_ref[...] = ((xf * r) * g_ref[...]).astype(h_ref.dtype)
        r_ref[...] = r

    return pl.pallas_call(
        body,
        grid=(s // tr,),
        in_specs=[pl.BlockSpec((tr, d), lambda i: (i, 0)), pl.BlockSpec((1, d), lambda i: (0, 0))],
        out_specs=[pl.BlockSpec((tr, d), lambda i: (i, 0)), pl.BlockSpec((tr, 1), lambda i: (i, 0))],
        out_shape=[jax.ShapeDtypeStruct((s, d), MXU), jax.ShapeDtypeStruct((s, 1), F32)],
        compiler_params=_cparams(),
        name=name,
    )(x, g)


def _rms_bwd(x, r, g, dh, dres, name):
    s, d = x.shape
    tr = _tile(s, 256)

    def body(x_ref, r_ref, g_ref, dh_ref, dres_ref, dx_ref, dxb_ref, dg_ref):
        @pl.when(pl.program_id(0) == 0)
        def _():
            dg_ref[...] = jnp.zeros_like(dg_ref)

        rr = r_ref[...]
        xhat = x_ref[...] * rr
        dh_v = dh_ref[...]
        dxhat = dh_v * g_ref[...]
        mean = jnp.mean(dxhat * xhat, axis=-1, keepdims=True)
        dx = dres_ref[...] + rr * (dxhat - xhat * mean)
        dx_ref[...] = dx
        dxb_ref[...] = dx.astype(dxb_ref.dtype)
        dg_ref[...] += jnp.sum(dh_v * xhat, axis=0, keepdims=True)

    row = pl.BlockSpec((tr, d), lambda i: (i, 0))
    vec = pl.BlockSpec((1, d), lambda i: (0, 0))
    return pl.pallas_call(
        body,
        grid=(s // tr,),
        in_specs=[row, pl.BlockSpec((tr, 1), lambda i: (i, 0)), vec, row, row],
        out_specs=[row, row, vec],
        out_shape=[jax.ShapeDtypeStruct((s, d), F32), jax.ShapeDtypeStruct((s, d), MXU),
                   jax.ShapeDtypeStruct((1, d), F32)],
        compiler_params=_cparams(),
        name=name,
    )(x, r, g, dh, dres)


def _loss_head(x, g, target, name):
    s, d = x.shape
    tr = _tile(s, 256)

    def body(x_ref, g_ref, t_ref, loss_ref, dx_ref, dg_ref):
        @pl.when(pl.program_id(0) == 0)
        def _():
            dg_ref[...] = jnp.zeros_like(dg_ref)
            loss_ref[...] = jnp.zeros_like(loss_ref)

        xf = x_ref[...]
        gv = g_ref[...]
        r = lax.rsqrt(jnp.mean(xf * xf, axis=-1, keepdims=True) + EPS)
        xhat = xf * r
        err = xhat * gv - t_ref[...]
        loss_ref[...] += 0.5 * jnp.sum(jnp.mean(err * err, axis=-1, keepdims=True), axis=0, keepdims=True)
        dy = err * (1.0 / d)
        dxhat = dy * gv
        mean = jnp.mean(dxhat * xhat, axis=-1, keepdims=True)
        dx_ref[...] = r * (dxhat - xhat * mean)
        dg_ref[...] += jnp.sum(dy * xhat, axis=0, keepdims=True)

    row = pl.BlockSpec((tr, d), lambda i: (i, 0))
    vec = pl.BlockSpec((1, d), lambda i: (0, 0))
    return pl.pallas_call(
        body,
        grid=(s // tr,),
        in_specs=[row, vec, row],
        out_specs=[pl.BlockSpec((1, LANES), lambda i: (0, 0)), row, vec],
        out_shape=[jax.ShapeDtypeStruct((1, LANES), F32), jax.ShapeDtypeStruct((s, d), F32),
                   jax.ShapeDtypeStruct((1, d), F32)],
        compiler_params=_cparams(),
        name=name,
    )(x, g, target)


def _mix_fwd(oa, ob, wa, wb, gates, name):
    s, kd = oa.shape
    d = wa.shape[1]
    tm, tn = _tile(s, 1024), _tile(d, 512)
    nj = d // tn

    def body(oa_ref, ob_ref, wa_ref, wb_ref, ga_ref, gb_ref, out_ref):
        ya = _dot(oa_ref[...], wa_ref[...])
        yb = _dot(ob_ref[...], wb_ref[...])
        out_ref[...] = (_sigmoid(ga_ref[...]) * ya + _sigmoid(gb_ref[...]) * yb).astype(out_ref.dtype)

    o_spec = pl.BlockSpec((tm, kd), lambda i, j: (i, 0))
    w_spec = pl.BlockSpec((kd, tn), lambda i, j: (0, j))
    return pl.pallas_call(
        body,
        grid=(s // tm, nj),
        in_specs=[o_spec, o_spec, w_spec, w_spec,
                  pl.BlockSpec((tm, tn), lambda i, j: (i, j)),
                  pl.BlockSpec((tm, tn), lambda i, j: (i, j + nj))],
        out_specs=pl.BlockSpec((tm, tn), lambda i, j: (i, j)),
        out_shape=jax.ShapeDtypeStruct((s, d), MXU),
        compiler_params=_cparams(),
        name=name,
    )(oa, ob, wa, wb, gates, gates)


def _mix_bwd(dx, w_o, oa, ob, wa, wb, gates, name):
    s, kd = oa.shape
    d = wa.shape[1]
    tm, tn = _tile(s, 1024), _tile(d, 512)
    nj = d // tn

    def body(dx_ref, wo_ref, oa_ref, ob_ref, wa_ref, wb_ref, ga_ref, gb_ref,
             dya_ref, dyb_ref, dga_ref, dgb_ref):
        dm = _dot_nt(dx_ref[...], wo_ref[...])
        ya = _dot(oa_ref[...], wa_ref[...])
        yb = _dot(ob_ref[...], wb_ref[...])
        sa = _sigmoid(ga_ref[...])
        sb = _sigmoid(gb_ref[...])
        dya_ref[...] = (dm * sa).astype(dya_ref.dtype)
        dyb_ref[...] = (dm * sb).astype(dyb_ref.dtype)
        dga_ref[...] = (dm * ya * sa * (1.0 - sa)).astype(dga_ref.dtype)
        dgb_ref[...] = (dm * yb * sb * (1.0 - sb)).astype(dgb_ref.dtype)

    o_spec = pl.BlockSpec((tm, kd), lambda i, j: (i, 0))
    w_spec = pl.BlockSpec((kd, tn), lambda i, j: (0, j))
    t_spec = pl.BlockSpec((tm, tn), lambda i, j: (i, j))
    return pl.pallas_call(
        body,
        grid=(s // tm, nj),
        in_specs=[pl.BlockSpec((tm, d), lambda i, j: (i, 0)),
                  pl.BlockSpec((tn, d), lambda i, j: (j, 0)),
                  o_spec, o_spec, w_spec, w_spec, t_spec,
                  pl.BlockSpec((tm, tn), lambda i, j: (i, j + nj))],
        out_specs=[t_spec] * 4,
        out_shape=[jax.ShapeDtypeStruct((s, d), MXU)] * 4,
        compiler_params=_cparams(),
        name=name,
    )(dx, w_o, oa, ob, wa, wb, gates, gates)


def _ple(p, w_pe, h, w_pg, other, *, backward, name):
    s, kp = p.shape
    d = w_pe.shape[1]
    tm, tn = _tile(s, 1024), _tile(d, 512)

    def body(p_ref, wpe_ref, h_ref, wpg_ref, other_ref, *out_refs):
        pe = _dot(p_ref[...].astype(MXU), wpe_ref[...])
        gt = _dot(h_ref[...], wpg_ref[...])
        sg = _sigmoid(gt)
        if backward:
            dout = other_ref[...]
            out_refs[0][...] = (dout * sg).astype(out_refs[0].dtype)
            out_refs[1][...] = (dout * pe * sg * (1.0 - sg)).astype(out_refs[1].dtype)
        else:
            out_refs[0][...] = other_ref[...] + pe * sg

    t_spec = pl.BlockSpec((tm, tn), lambda i, j: (i, j))
    if backward:
        out_specs, out_shape = [t_spec, t_spec], [jax.ShapeDtypeStruct((s, d), MXU)] * 2
    else:
        out_specs, out_shape = [t_spec], [jax.ShapeDtypeStruct((s, d), F32)]
    outs = pl.pallas_call(
        body,
        grid=(s // tm, d // tn),
        in_specs=[pl.BlockSpec((tm, kp), lambda i, j: (i, 0)),
                  pl.BlockSpec((kp, tn), lambda i, j: (0, j)),
                  pl.BlockSpec((tm, d), lambda i, j: (i, 0)),
                  pl.BlockSpec((d, tn), lambda i, j: (0, j)),
                  t_spec],
        out_specs=out_specs,
        out_shape=out_shape,
        compiler_params=_cparams(),
        name=name,
    )(p, w_pe, h, w_pg, other)
    return tuple(outs) if backward else outs[0]


def _split_dot(x, tri):
    hi = x.astype(jnp.bfloat16)
    r1 = x - hi.astype(F32)
    mid = r1.astype(jnp.bfloat16)
    lo = (r1 - mid.astype(F32)).astype(jnp.bfloat16)
    return _dot(hi, tri) + _dot(mid, tri) + _dot(lo, tri)


def _log_sigmoids(z):
    t = jnp.log1p(jnp.exp(-jnp.abs(z)))
    return jnp.minimum(z, 0.0) - t, jnp.minimum(-z, 0.0) - t


def _head_lanes(hh):
    lane = lax.broadcasted_iota(jnp.int32, (1, LANES), 1)
    return jnp.logical_and(lane >= hh * HEAD_DIM, lane < (hh + 1) * HEAD_DIM)


def _sb_fwd(qkv, name, comm=None):
    s = qkv.shape[0]
    tq = _tile(s, SB_TQ)

    def body(q_ref, k_ref, v_ref, o_ref):
        i = pl.program_id(1)
        qf = q_ref[...].astype(F32) * SCALE
        row = lax.broadcasted_iota(jnp.int32, (tq, tq), 0)
        col = lax.broadcasted_iota(jnp.int32, (tq, tq), 1)
        causal = col < row
        tri = jnp.where(row > col, 1.0, 0.0).astype(jnp.bfloat16)

        qms = [jnp.where(_head_lanes(hh), qf, 0.0).astype(MXU) for hh in range(2)]

        def block(kb, cs, accs, masked):
            rows = pl.ds(pl.multiple_of(kb * tq, tq), tq)
            ks, vs = k_ref[rows, :], v_ref[rows, :]
            new_c, new_acc = [], []
            for hh in range(2):
                lb, lm = _log_sigmoids(_dot_nt(qms[hh], ks))
                if masked:
                    lm = jnp.where(causal, lm, 0.0)
                a = jnp.exp(lb + _split_dot(lm, tri) + cs[hh])
                if masked:
                    a = jnp.where(causal, a, 0.0)
                new_acc.append(accs[hh] + _dot(a.astype(MXU), vs))
                new_c.append(cs[hh] + jnp.sum(lm, axis=1, keepdims=True))
            return tuple(new_c), tuple(new_acc)

        def top(cs):
            return jnp.maximum(jnp.max(cs[0]), jnp.max(cs[1]))

        zc, za = jnp.zeros((tq, 1), F32), jnp.zeros((tq, LANES), F32)
        cs, accs = block(i, (zc, zc), (za, za), True)

        def live(st):
            return jnp.logical_and(st[0] >= 0, st[1] > SB_DEAD)

        def walk(st):
            cs, accs = block(st[0], st[2], st[3], False)
            return st[0] - 1, top(cs), cs, accs

        accs = lax.while_loop(live, walk, (i - 1, top(cs), cs, accs))[3]
        o_ref[...] = jnp.where(_head_lanes(0), accs[0], accs[1]).astype(o_ref.dtype)

    outs, couts = _call(
        body,
        grid=(N_PAIR, s // tq),
        in_specs=[pl.BlockSpec((tq, LANES), lambda p, i: (i, p)),
                  pl.BlockSpec((s, LANES), lambda p, i: (0, N_PAIR + p)),
                  pl.BlockSpec((s, LANES), lambda p, i: (0, 2 * N_PAIR + p))],
        out_specs=[pl.BlockSpec((tq, LANES), lambda p, i: (i, p))],
        out_shape=[jax.ShapeDtypeStruct((s, SB_W), MXU)],
        args=(qkv, qkv, qkv), name=name, comm=comm)
    return outs[0] if comm is None else (outs[0], couts)


def _sb_bwd(qkv, do, name, comm=None):
    s = qkv.shape[0]
    tq = _tile(s, SB_TQ)
    nq = s // tq

    def body(q_ref, k_ref, v_ref, do_ref, dq_ref, dk_ref, dv_ref, dk_acc, dv_acc, carries):
        i = pl.program_id(1)

        @pl.when(i == 0)
        def _():
            dk_acc[...] = jnp.zeros_like(dk_acc)
            dv_acc[...] = jnp.zeros_like(dv_acc)

        qf = q_ref[...].astype(F32) * SCALE
        dof = do_ref[...]
        row = lax.broadcasted_iota(jnp.int32, (tq, tq), 0)
        col = lax.broadcasted_iota(jnp.int32, (tq, tq), 1)
        causal = col < row
        tri_rev = jnp.where(row > col, 1.0, 0.0).astype(jnp.bfloat16)
        tri_excl = jnp.where(row < col, 1.0, 0.0).astype(jnp.bfloat16)

        qms = [jnp.where(_head_lanes(hh), qf, 0.0).astype(MXU) for hh in range(2)]
        doms = [jnp.where(_head_lanes(hh), dof, jnp.zeros_like(dof)) for hh in range(2)]

        def row_sums(kb, masked):
            rows = pl.ds(pl.multiple_of(kb * tq, tq), tq)
            ks = k_ref[rows, :]
            out = []
            for hh in range(2):
                _, lm = _log_sigmoids(_dot_nt(qms[hh], ks))
                if masked:
                    lm = jnp.where(causal, lm, 0.0)
                out.append(jnp.sum(lm, axis=1, keepdims=True))
            return out

        def top(cs):
            return jnp.maximum(jnp.max(cs[0]), jnp.max(cs[1]))

        def live(st):
            return jnp.logical_and(st[0] >= 0, st[1] > SB_DEAD)

        def record(st):
            kb, cs = st[0], st[2]
            sums = row_sums(kb, False)
            for hh in range(2):
                carries[hh, kb] = cs[hh]
            cs = tuple(cs[hh] + sums[hh] for hh in range(2))
            return kb - 1, top(cs), cs

        c_diag = tuple(row_sums(i, True))
        first = lax.while_loop(live, record, (i - 1, top(c_diag), c_diag))[0] + 1

        def block(kb, cs, gpres, dqs, masked):
            rows = pl.ds(pl.multiple_of(kb * tq, tq), tq)
            ks, vs = k_ref[rows, :], v_ref[rows, :]
            new_g, new_dq = [], []
            dk_add, dv_add = None, None
            for hh in range(2):
                lb, lm = _log_sigmoids(_dot_nt(qms[hh], ks))
                if masked:
                    lm = jnp.where(causal, lm, 0.0)
                a = jnp.exp(lb + _split_dot(lm, tri_rev) + cs[hh])
                if masked:
                    a = jnp.where(causal, a, 0.0)
                g = a * _dot_nt(doms[hh], vs)
                gsum = gpres[hh] + _split_dot(g, tri_excl)
                dz = g - (g + gsum) * jnp.exp(lb)
                if masked:
                    dz = jnp.where(causal, dz, 0.0)
                dzb = dz.astype(MXU)
                new_dq.append(dqs[hh] + _dot(dzb, ks))
                dk_h = _dot_tn(dzb, qms[hh])
                dv_h = _dot_tn(a.astype(MXU), doms[hh])
                dk_add = dk_h if dk_add is None else dk_add + dk_h
                dv_add = dv_h if dv_add is None else dv_add + dv_h
                new_g.append(gpres[hh] + jnp.sum(g, axis=1, keepdims=True))
            dk_acc[rows, :] += dk_add
            dv_acc[rows, :] += dv_add
            return tuple(new_g), tuple(new_dq)

        zc, za = jnp.zeros((tq, 1), F32), jnp.zeros((tq, LANES), F32)
        gpres, dqs = lax.fori_loop(
            first, i, lambda kb, cr: block(kb, (carries[0, kb], carries[1, kb]), cr[0], cr[1], False),
            ((zc, zc), (za, za)))
        dqs = block(i, (zc, zc), gpres, dqs, True)[1]
        dq_ref[...] = (jnp.where(_head_lanes(0), dqs[0], dqs[1]) * SCALE).astype(dq_ref.dtype)

        @pl.when(i == nq - 1)
        def _():
            dk_ref[...] = dk_acc[...].astype(dk_ref.dtype)
            dv_ref[...] = dv_acc[...].astype(dv_ref.dtype)

    blk = pl.BlockSpec((tq, LANES), lambda p, i: (i, p))
    full = pl.BlockSpec((s, LANES), lambda p, i: (0, p))
    outs, couts = _call(
        body,
        grid=(N_PAIR, nq),
        in_specs=[blk,
                  pl.BlockSpec((s, LANES), lambda p, i: (0, N_PAIR + p)),
                  pl.BlockSpec((s, LANES), lambda p, i: (0, 2 * N_PAIR + p)),
                  blk],
        out_specs=[blk, full, full],
        out_shape=[jax.ShapeDtypeStruct((s, SB_W), MXU)] * 3,
        scratch_shapes=[pltpu.VMEM((s, LANES), F32), pltpu.VMEM((s, LANES), F32),
                        pltpu.VMEM((2, nq, tq, 1), F32)],
        args=(qkv, qkv, qkv, do), name=name, comm=comm)
    return tuple(outs) if comm is None else (tuple(outs), couts)


def _bucket_table():
    i = np.arange(BLOCK)[:, None]
    j = np.arange(2 * BLOCK)[None, :]
    d = np.maximum(BLOCK + i - j, 0)
    max_exact = N_BUCKETS // 2
    df = np.maximum(d, 1).astype(np.float32)
    large = max_exact + (np.log(df / max_exact) / math.log(MAX_DISTANCE / max_exact)
                         * (N_BUCKETS - max_exact)).astype(np.int32)
    large = np.minimum(large, N_BUCKETS - 1)
    return np.where(d < max_exact, d, large).astype(np.int32)


def _build_bias(rel_bias, buckets, name):
    def body(rb_ref, bk_ref, out_ref):
        h = pl.program_id(0)
        bk = bk_ref[...]
        acc = jnp.zeros(bk.shape, F32)
        for b in range(N_BUCKETS):
            acc = jnp.where(bk == b, rb_ref[b, h], acc)
        out_ref[...] = acc

    return pl.pallas_call(
        body,
        grid=(SW_HEADS,),
        in_specs=[pl.BlockSpec(memory_space=pltpu.SMEM),
                  pl.BlockSpec((BLOCK, 2 * BLOCK), lambda h: (0, 0))],
        out_specs=pl.BlockSpec((None, BLOCK, 2 * BLOCK), lambda h: (h, 0, 0)),
        out_shape=jax.ShapeDtypeStruct((SW_HEADS, BLOCK, 2 * BLOCK), F32),
        name=name,
    )(rel_bias, buckets)


def _bias_grad(dbias_layers, buckets, name):
    n_l = len(dbias_layers)

    def body(*refs):
        bk = refs[n_l][...]
        out_ref = refs[n_l + 1]
        db = refs[0][...]
        for r in refs[1:n_l]:
            db = db + r[...]
        lane = lax.broadcasted_iota(jnp.int32, (1, LANES), 1)
        acc = jnp.zeros((1, LANES), F32)
        for b in range(N_BUCKETS):
            part = jnp.sum(jnp.where(bk == b, db, 0.0), axis=1, keepdims=True)
            tot = jnp.sum(part, axis=0, keepdims=True)
            acc = jnp.where(lane == b, tot, acc)
        out_ref[...] = acc

    hspec = pl.BlockSpec((None, BLOCK, 2 * BLOCK), lambda h: (h, 0, 0))
    return pl.pallas_call(
        body,
        grid=(SW_HEADS,),
        in_specs=[hspec] * n_l + [pl.BlockSpec((BLOCK, 2 * BLOCK), lambda h: (0, 0))],
        out_specs=pl.BlockSpec((None, 1, LANES), lambda h: (h, 0, 0)),
        out_shape=jax.ShapeDtypeStruct((SW_HEADS, 1, LANES), F32),
        name=name,
    )(*dbias_layers, buckets)


def _swa_scores(qb, kp, kc, bias_ref, hh, n):
    row = lax.broadcasted_iota(jnp.int32, (BLOCK, BLOCK), 0)
    col = lax.broadcasted_iota(jnp.int32, (BLOCK, BLOCK), 1)
    s1 = _dot_nt(qb, kp) + bias_ref[hh, :, :BLOCK]
    s2 = _dot_nt(qb, kc) + bias_ref[hh, :, BLOCK:]
    no_prev = jnp.where(n > 0, 0, BLOCK)
    s1 = jnp.where(col > row + no_prev, s1, NEG)
    s2 = jnp.where(col <= row, s2, NEG)
    return s1, s2


def _swa_lanes(p, hh):
    lane = lax.broadcasted_iota(jnp.int32, (1, LANES), 1)
    kv_half = jnp.zeros((1, LANES), jnp.int32) + (2 * p + hh) // SW_GROUP
    return jnp.where(lane >= HEAD_DIM, 1, 0) == kv_half, kv_half == hh


def _to_kv_lanes(x, kv_lanes, aligned):
    return jnp.where(kv_lanes, jnp.where(aligned, x, pltpu.roll(x, HEAD_DIM, 1)), 0.0)


def _swa_fwd(qkv, bias, sinks, name, comm=None):
    s = qkv.shape[0]
    nb = s // BLOCK
    q0 = 3 * N_PAIR
    kblk, vblk = q0 + N_PAIR, q0 + N_PAIR + 1

    def body(sink_ref, q_ref, k_ref, v_ref, bias_ref, o_ref, lse_ref):
        p = pl.program_id(0)

        def step(n, carry):
            r0 = pl.multiple_of(n * BLOCK, BLOCK)
            p0 = pl.multiple_of(jnp.maximum(n - 1, 0) * BLOCK, BLOCK)
            cur, prev = pl.ds(r0, BLOCK), pl.ds(p0, BLOCK)
            qf = q_ref[cur, :].astype(F32) * SCALE
            kp, kc, vp, vc = k_ref[prev, :], k_ref[cur, :], v_ref[prev, :], v_ref[cur, :]
            outs = []
            for hh in range(2):
                kv_lanes, aligned = _swa_lanes(p, hh)
                sink = sink_ref[2 * p + hh]
                qb = _to_kv_lanes(qf, kv_lanes, aligned).astype(MXU)
                s1, s2 = _swa_scores(qb, kp, kc, bias_ref, hh, n)
                m = jnp.maximum(jnp.maximum(jnp.max(s1, axis=1, keepdims=True),
                                            jnp.max(s2, axis=1, keepdims=True)), sink)
                e1 = jnp.exp(s1 - m)
                e2 = jnp.exp(s2 - m)
                den = (jnp.sum(e1, axis=1, keepdims=True) + jnp.sum(e2, axis=1, keepdims=True)
                       + jnp.exp(sink - m))
                o = _dot((e1 / den).astype(MXU), vp) + _dot((e2 / den).astype(MXU), vc)
                outs.append(jnp.where(aligned, o, pltpu.roll(o, HEAD_DIM, 1)))
                lse_ref[hh, cur, :] = m + jnp.log(den)
            o_ref[cur, :] = jnp.where(_head_lanes(0), outs[0], outs[1]).astype(o_ref.dtype)
            return carry

        lax.fori_loop(0, nb, step, 0)

    outs, couts = _call(
        body,
        grid=(N_PAIR,),
        in_specs=[pl.BlockSpec(memory_space=pltpu.SMEM),
                  pl.BlockSpec((s, LANES), lambda p: (0, q0 + p)),
                  pl.BlockSpec((s, LANES), lambda p: (0, kblk)),
                  pl.BlockSpec((s, LANES), lambda p: (0, vblk)),
                  pl.BlockSpec((2, BLOCK, 2 * BLOCK), lambda p: (p, 0, 0))],
        out_specs=[pl.BlockSpec((s, LANES), lambda p: (0, p)),
                   pl.BlockSpec((None, 2, s, 1), lambda p: (p, 0, 0, 0))],
        out_shape=[jax.ShapeDtypeStruct((s, SW_W), MXU), jax.ShapeDtypeStruct((N_PAIR, 2, s, 1), F32)],
        args=(sinks, qkv, qkv, qkv, bias), name=name, comm=comm)
    return tuple(outs) if comm is None else (tuple(outs), couts)


def _swa_bwd(qkv, bias, sinks, do, lse, name, comm=None):
    s = qkv.shape[0]
    nb = s // BLOCK
    q0 = 3 * N_PAIR
    kblk, vblk = q0 + N_PAIR, q0 + N_PAIR + 1

    def body(sink_ref, q_ref, k_ref, v_ref, bias_ref, do_ref, lse_ref,
             dq_ref, dk_ref, dv_ref, dbias_ref, dsink_ref, dk_acc, dv_acc):
        p = pl.program_id(0)

        @pl.when(p == 0)
        def _():
            dk_acc[...] = jnp.zeros_like(dk_acc)
            dv_acc[...] = jnp.zeros_like(dv_acc)

        dbias_ref[...] = jnp.zeros_like(dbias_ref)

        def step(n, dsink_rows):
            r0 = pl.multiple_of(n * BLOCK, BLOCK)
            p0 = pl.multiple_of(jnp.maximum(n - 1, 0) * BLOCK, BLOCK)
            cur, prev = pl.ds(r0, BLOCK), pl.ds(p0, BLOCK)
            qf = q_ref[cur, :].astype(F32) * SCALE
            dof = do_ref[cur, :].astype(F32)
            kp, kc, vp, vc = k_ref[prev, :], k_ref[cur, :], v_ref[prev, :], v_ref[cur, :]
            dqs, new_rows = [], []
            for hh in range(2):
                kv_lanes, aligned = _swa_lanes(p, hh)
                sink = sink_ref[2 * p + hh]
                qb = _to_kv_lanes(qf, kv_lanes, aligned).astype(MXU)
                dob = _to_kv_lanes(dof, kv_lanes, aligned).astype(MXU)
                lse_b = lse_ref[hh, cur, :]
                s1, s2 = _swa_scores(qb, kp, kc, bias_ref, hh, n)
                pr1 = jnp.exp(s1 - lse_b)
                pr2 = jnp.exp(s2 - lse_b)
                dpr1 = _dot_nt(dob, vp)
                dpr2 = _dot_nt(dob, vc)
                delta = (jnp.sum(pr1 * dpr1, axis=1, keepdims=True)
                         + jnp.sum(pr2 * dpr2, axis=1, keepdims=True))
                ds1 = pr1 * (dpr1 - delta)
                ds2 = pr2 * (dpr2 - delta)
                dbias_ref[hh, :, :BLOCK] += ds1
                dbias_ref[hh, :, BLOCK:] += ds2
                ds1b, ds2b = ds1.astype(MXU), ds2.astype(MXU)
                dq = _dot(ds1b, kp) + _dot(ds2b, kc)
                dqs.append(jnp.where(aligned, dq, pltpu.roll(dq, HEAD_DIM, 1)))
                dk_acc[prev, :] += _dot_tn(ds1b, qb)
                dk_acc[cur, :] += _dot_tn(ds2b, qb)
                dv_acc[prev, :] += _dot_tn(pr1.astype(MXU), dob)
                dv_acc[cur, :] += _dot_tn(pr2.astype(MXU), dob)
                new_rows.append(dsink_rows[hh] - jnp.exp(sink - lse_b) * delta)
            dq_ref[cur, :] = (jnp.where(_head_lanes(0), dqs[0], dqs[1]) * SCALE).astype(dq_ref.dtype)
            return tuple(new_rows)

        zero = jnp.zeros((BLOCK, 1), F32)
        rows = lax.fori_loop(0, nb, step, (zero, zero))
        for hh in range(2):
            dsink_ref[hh] = jnp.broadcast_to(jnp.sum(rows[hh], axis=0, keepdims=True), (1, LANES))

        @pl.when(p == N_PAIR - 1)
        def _():
            dk_ref[...] = dk_acc[...].astype(dk_ref.dtype)
            dv_ref[...] = dv_acc[...].astype(dv_ref.dtype)

    pair = pl.BlockSpec((s, LANES), lambda p: (0, p))
    kv_out = pl.BlockSpec((s, LANES), lambda p: (0, 0))
    bspec = pl.BlockSpec((2, BLOCK, 2 * BLOCK), lambda p: (p, 0, 0))
    outs, couts = _call(
        body,
        grid=(N_PAIR,),
        in_specs=[pl.BlockSpec(memory_space=pltpu.SMEM),
                  pl.BlockSpec((s, LANES), lambda p: (0, q0 + p)),
                  pl.BlockSpec((s, LANES), lambda p: (0, kblk)),
                  pl.BlockSpec((s, LANES), lambda p: (0, vblk)),
                  bspec, pair,
                  pl.BlockSpec((None, 2, s, 1), lambda p: (p, 0, 0, 0))],
        out_specs=[pair, kv_out, kv_out, bspec, pl.BlockSpec((2, 1, LANES), lambda p: (p, 0, 0))],
        out_shape=[jax.ShapeDtypeStruct((s, SW_W), MXU),
                   jax.ShapeDtypeStruct((s, LANES), MXU),
                   jax.ShapeDtypeStruct((s, LANES), MXU),
                   jax.ShapeDtypeStruct((SW_HEADS, BLOCK, 2 * BLOCK), F32),
                   jax.ShapeDtypeStruct((SW_HEADS, 1, LANES), F32)],
        scratch_shapes=[pltpu.VMEM((s, LANES), F32), pltpu.VMEM((s, LANES), F32)],
        args=(sinks, qkv, qkv, qkv, bias, do, lse), name=name, comm=comm)
    return tuple(outs) if comm is None else (tuple(outs), couts)


class _NoPlan:
    def comm(self, name):
        return None

    def done(self, name, outs):
        pass

    def grad(self, layer, name, value):
        pass

    def layer_done(self, layer):
        pass


def _run(plan, fn, *args, name, **kw):
    comm = plan.comm(name)
    if comm is None:
        return fn(*args, name=name, **kw)
    res, outs = fn(*args, name=name, comm=comm, **kw)
    plan.done(name, outs)
    return res


def _layer_fwd(x, p, w, g_mix, g_mlp, g_pe, sinks, bias, tag, plan):
    h1, r1 = _rms_fwd(x, g_mix, f"rms_mix_{tag}")
    qkv = _run(plan, _mm, h1, w["w_qkv"], out_dtypes=(MXU,), name=f"proj_qkv_{tag}")
    gates = _run(plan, _mm, h1, w["w_gate"], name=f"proj_gate_{tag}")
    oa = _run(plan, _sb_fwd, qkv, name=f"sb_fwd_{tag}")
    ob, lse = _run(plan, _swa_fwd, qkv, bias, sinks, name=f"swa_fwd_{tag}")
    merged = _mix_fwd(oa, ob, w["w_up_a"], w["w_up_b"], gates, f"mix_fwd_{tag}")
    x1 = _run(plan, _mm, merged, w["w_o"], extras=(x,), epi=lambda acc, res: res + acc, name=f"out_proj_{tag}")
    h2, r2 = _rms_fwd(x1, g_mlp, f"rms_mlp_{tag}")
    u, act = _run(plan, _mm, h2, w["w_ff1"], epi=lambda acc: (acc, jnp.square(jnp.maximum(acc, 0.0))),
                  out_dtypes=(F32, MXU), name=f"ff1_{tag}")
    x2 = _run(plan, _mm, act, w["w_ff2"], extras=(x1,), epi=lambda acc, res: res + acc, name=f"ff2_{tag}")
    h3, r3 = _rms_fwd(x2, g_pe, f"rms_pe_{tag}")
    x3 = _ple(p, w["w_pe"], h3, w["w_pg"], x2, backward=False, name=f"ple_fwd_{tag}")
    saved = dict(x=x, h1=h1, r1=r1, gates=gates, qkv=qkv, lse=lse, oa=oa, ob=ob, merged=merged,
                 x1=x1, h2=h2, r2=r2, u=u, act=act, x2=x2, h3=h3, r3=r3)
    return x3, saved


def _layer_bwd(dx3, sv, p, w, g_mix, g_mlp, g_pe, sinks, bias, layer, plan):
    tag = f"l{layer}"
    gw = {}
    wire = (WIRE,)

    def dw(name, a, b):
        gw[name] = _run(plan, _mm, a, b, ta=True, out_dtypes=wire, name=f"d{name}_{tag}")
        plan.grad(layer, name, gw[name])

    dpe, dgt = _ple(p, w["w_pe"], sv["h3"], w["w_pg"], dx3, backward=True, name=f"ple_bwd_{tag}")
    dw("w_pe", p, dpe)
    dw("w_pg", sv["h3"], dgt)
    dh3 = _run(plan, _mm, dgt, w["w_pg"], tb=True, name=f"dh_pe_{tag}")
    dx2, dx2b, dg_pe = _rms_bwd(sv["x2"], sv["r3"], g_pe, dh3, dx3, f"rms_pe_bwd_{tag}")
    dw("w_ff2", sv["act"], dx2b)
    du = _run(plan, _mm, dx2b, w["w_ff2"], tb=True, extras=(sv["u"],),
              epi=lambda acc, u: acc * (2.0 * jnp.maximum(u, 0.0)), out_dtypes=(MXU,), name=f"dact_{tag}")
    dw("w_ff1", sv["h2"], du)
    dh2 = _run(plan, _mm, du, w["w_ff1"], tb=True, name=f"dh_mlp_{tag}")
    dx1, dx1b, dg_mlp = _rms_bwd(sv["x1"], sv["r2"], g_mlp, dh2, dx2, f"rms_mlp_bwd_{tag}")
    dw("w_o", sv["merged"], dx1b)
    dya, dyb, dga, dgb = _mix_bwd(dx1b, w["w_o"], sv["oa"], sv["ob"], w["w_up_a"], w["w_up_b"],
                                  sv["gates"], f"mix_bwd_{tag}")
    dw("w_up_a", sv["oa"], dya)
    dw("w_up_b", sv["ob"], dyb)
    doa = _run(plan, _mm, dya, w["w_up_a"], tb=True, out_dtypes=(MXU,), name=f"do_a_{tag}")
    dob = _run(plan, _mm, dyb, w["w_up_b"], tb=True, out_dtypes=(MXU,), name=f"do_b_{tag}")
    dqb, dkb, dvb, dbias, dsink = _run(plan, _swa_bwd, sv["qkv"], bias, sinks, dob, sv["lse"],
                                       name=f"swa_bwd_{tag}")
    dqa, dka, dva = _run(plan, _sb_bwd, sv["qkv"], doa, name=f"sb_bwd_{tag}")
    dqkv = jnp.concatenate([dqa, dka, dva, dqb, dkb, dvb], axis=1)
    gw_qkv = _mm(sv["h1"], dqkv, ta=True, out_dtypes=wire, name=f"dw_qkv_{tag}")
    gw_ga = _mm(sv["h1"], dga, ta=True, out_dtypes=wire, name=f"dw_ga_{tag}")
    gw_gb = _mm(sv["h1"], dgb, ta=True, out_dtypes=wire, name=f"dw_gb_{tag}")
    gw["w_in"] = jnp.concatenate([gw_qkv, gw_ga, gw_gb], axis=1)
    plan.grad(layer, "w_in", gw["w_in"])
    d = dga.shape[1]
    add = lambda acc, res: res + acc
    dh1 = _mm(dqkv, w["w_qkv"], tb=True, tk=768, name=f"dh_qkv_{tag}")
    dh1 = _mm(dga, w["w_gate"][:, :d], tb=True, extras=(dh1,), epi=add, name=f"dh_ga_{tag}")
    dh1 = _mm(dgb, w["w_gate"][:, d:], tb=True, extras=(dh1,), epi=add, name=f"dh_gb_{tag}")
    dx, _, dg_mix = _rms_bwd(sv["x"], sv["r1"], g_mix, dh1, dx1, f"rms_mix_bwd_{tag}")
    small = dict(g_mix=dg_mix, g_mlp=dg_mlp, g_pe=dg_pe, sinks=dsink[:, 0, 0], dbias=dbias)
    return dx, gw, small


def _local_step(x, p, target, weights, g_mix, g_mlp, g_pe, g_final, sinks, rel_bias, plan=None):
    plan = _NoPlan() if plan is None else plan
    depth = g_mix.shape[0]
    buckets = jnp.asarray(_bucket_table())
    bias = _build_bias(rel_bias, buckets, "build_bias")
    saved, wfull = [], []
    h = x
    for l in range(depth):
        wfull.append(weights(l))
        h, sv = _layer_fwd(h, p[l], wfull[l], g_mix[l:l + 1], g_mlp[l:l + 1], g_pe[l:l + 1],
                           sinks[l], bias, f"l{l}", plan)
        saved.append(sv)
    loss_row, dx, dg_final = _loss_head(h, g_final[None, :], target, "loss_head")
    gws = [None] * depth
    smalls = [None] * depth
    for l in reversed(range(depth)):
        dx, gws[l], smalls[l] = _layer_bwd(dx, saved[l], p[l], wfull[l], g_mix[l:l + 1], g_mlp[l:l + 1],
                                           g_pe[l:l + 1], sinks[l], bias, l, plan)
        plan.layer_done(l)
    drel = _bias_grad([sm["dbias"] for sm in smalls], buckets, "bias_grad")[:, 0, :N_BUCKETS].T
    small = dict(
        g_mix=jnp.concatenate([sm["g_mix"] for sm in smalls], axis=0),
        g_mlp=jnp.concatenate([sm["g_mlp"] for sm in smalls], axis=0),
        g_pe=jnp.concatenate([sm["g_pe"] for sm in smalls], axis=0),
        g_final=dg_final[0],
        sinks=jnp.stack([sm["sinks"] for sm in smalls], axis=0),
        rel_bias=drel,
    )
    return loss_row, dx, gws, small


MESH_ID = pl.DeviceIdType.MESH
ANY = pl.BlockSpec(memory_space=pl.ANY)


def _position():
    return lax.axis_index("x"), lax.axis_index("y"), lax.axis_index("c")


def _run_comm(comm, name):
    ci, co = len(comm.inputs), len(comm.out_shapes)

    def body(*refs):
        cin, cout, csem = refs[:ci], refs[ci:ci + co], refs[ci + co:]
        pos = _position()
        comm.start(pos, cin, cout, csem)
        comm.finish(pos, cin, cout, csem)

    return pl.pallas_call(body, out_shape=comm.out_shapes, in_specs=[ANY] * ci, out_specs=[ANY] * co,
                          scratch_shapes=comm.sems, name=name)(*comm.inputs)


def _gather_comm(shards):
    n = len(shards)

    def copies(pos, x_refs, out_refs, sems):
        send_sems, recv_sems, local_sems = sems
        x, y, c = pos
        me, sibling = (x, y, c), (x, y, 1 - c)
        chips = [(1 - x, y), (x, 1 - y), (1 - x, 1 - y)]

        def slot(a, px, py, pc):
            return out_refs[a].at[4 * px + 2 * py + pc]

        def copy(a, k, block, to, src=None):
            return pltpu.make_async_remote_copy(
                src_ref=slot(a, *block) if src is None else src, dst_ref=slot(a, *block),
                send_sem=send_sems.at[a, k], recv_sem=recv_sems.at[a, k],
                device_id=to, device_id_type=MESH_ID)

        mine = [pltpu.make_async_copy(x_refs[a], slot(a, *me), local_sems.at[a]) for a in range(n)]
        first = []
        for a in range(n):
            first.append(copy(a, 0, me, sibling, src=x_refs[a]))
            first += [copy(a, 1 + j, me, (*chip, c), src=x_refs[a]) for j, chip in enumerate(chips)]
        return me, sibling, chips, copy, mine, first

    def start(pos, x_refs, out_refs, sems):
        _, _, _, _, mine, first = copies(pos, x_refs, out_refs, sems)
        for cp in mine + first:
            cp.start()

    def finish(pos, x_refs, out_refs, sems):
        me, sibling, chips, copy, mine, first = copies(pos, x_refs, out_refs, sems)
        c = pos[2]
        passed = []
        for j, chip in enumerate(chips):
            for a in range(n):
                copy(a, 1 + j, (*chip, c), me).wait_recv()
                fwd = copy(a, 4 + j, (*chip, c), sibling)
                fwd.start()
                passed.append(fwd)
        for a in range(n):
            copy(a, 0, sibling, me).wait_recv()
            for j, chip in enumerate(chips):
                copy(a, 4 + j, (*chip, 1 - c), me).wait_recv()
        for cp in first + passed:
            cp.wait_send()
        for cp in mine:
            cp.wait()

    return _Comm(shards, [jax.ShapeDtypeStruct((N_DEV,) + s.shape, s.dtype) for s in shards],
                 [pltpu.SemaphoreType.DMA((n, 7)), pltpu.SemaphoreType.DMA((n, 7)),
                  pltpu.SemaphoreType.DMA((n,))], start, finish)


def _exchange_comm(arrays, n_slots, route):
    n = len(arrays)

    def copies(pos, in_refs, out_refs, sems):
        send_sems, recv_sems = sems
        out = []
        for a in range(n):
            for j in range(n_slots):
                src_slot, peer = route(pos, j)
                out.append(pltpu.make_async_remote_copy(
                    src_ref=in_refs[a].at[src_slot], dst_ref=out_refs[a].at[j],
                    send_sem=send_sems.at[a, j], recv_sem=recv_sems.at[a, j],
                    device_id=peer, device_id_type=MESH_ID))
        return out

    def start(pos, in_refs, out_refs, sems):
        for cp in copies(pos, in_refs, out_refs, sems):
            cp.start()

    def finish(pos, in_refs, out_refs, sems):
        for cp in copies(pos, in_refs, out_refs, sems):
            cp.wait()

    return _Comm(arrays, [jax.ShapeDtypeStruct((n_slots,) + g.shape[1:], g.dtype) for g in arrays],
                 [pltpu.SemaphoreType.DMA((n, n_slots)), pltpu.SemaphoreType.DMA((n, n_slots))], start, finish)


def _rs_sibling_comm(gs):
    return _exchange_comm(gs, 4, lambda pos, j: (2 * j + (1 - pos[2]), (pos[0], pos[1], 1 - pos[2])))


def _chip_of(k, x, y):
    return x ^ ((k + 1) & 1), y ^ (((k + 1) >> 1) & 1)


def _chip_partials(pos, gs, recvs, name):
    n = len(gs)

    def body(pos_ref, *refs):
        for a in range(n):
            refs[2 * n + a][...] = (refs[a][...].astype(F32) + refs[n + a][...].astype(F32)
                                    ).astype(refs[2 * n + a].dtype)

    def g_map(k, pos_ref):
        cx, cy = _chip_of(k, pos_ref[0], pos_ref[1])
        return (4 * cx + 2 * cy + pos_ref[2], 0, 0)

    def r_map(k, pos_ref):
        cx, cy = _chip_of(k, pos_ref[0], pos_ref[1])
        return (2 * cx + cy, 0, 0)

    slab = [(None,) + g.shape[1:] for g in gs]
    return pl.pallas_call(
        body,
        grid_spec=pltpu.PrefetchScalarGridSpec(
            num_scalar_prefetch=1,
            grid=(4,),
            in_specs=[pl.BlockSpec(sh, g_map) for sh in slab] + [pl.BlockSpec(sh, r_map) for sh in slab],
            out_specs=[pl.BlockSpec(sh, lambda k, pos_ref: (k, 0, 0)) for sh in slab],
        ),
        out_shape=[jax.ShapeDtypeStruct((4,) + g.shape[1:], g.dtype) for g in gs],
        compiler_params=_cparams(),
        name=name,
    )(pos, *gs, *recvs)


def _rs_chips_comm(parts):
    return _exchange_comm(parts, 3, lambda pos, k: (k, (*_chip_of(k, pos[0], pos[1]), pos[2])))


def _adamw_math(w, g, m, v):
    m = ADAM_B1 * m + (1.0 - ADAM_B1) * g
    v = ADAM_B2 * v + (1.0 - ADAM_B2) * (g * g)
    m_hat = m / (1.0 - ADAM_B1 ** ADAM_STEP)
    v_hat = v / (1.0 - ADAM_B2 ** ADAM_STEP)
    delta = -ADAM_LR * (m_hat / (jnp.sqrt(v_hat) + ADAM_EPS) + ADAM_WD * w)
    return delta, m, v


def _adamw_weight(parts, recvs, w, m, v, name):
    depth, a, b = w.shape
    ta = _tile(a, 256, unit=16)
    ni = a // ta

    def body(*refs):
        p_refs, r_refs = refs[:depth], refs[depth:2 * depth]
        w_ref, m_ref, v_ref = refs[2 * depth:2 * depth + 3]
        g_out, d_out, m_out, v_out = refs[2 * depth + 3:]
        layer = pl.program_id(0)
        g = jnp.zeros((ta, b), F32)
        for l in range(depth):
            gl = p_refs[l][...].astype(F32)
            for k in range(3):
                gl = gl + r_refs[l][k].astype(F32)
            g = jnp.where(layer == l, gl, g)
        delta, m_new, v_new = _adamw_math(w_ref[...], g, m_ref[...], v_ref[...])
        g_out[...] = g
        d_out[...] = delta
        m_out[...] = m_new
        v_out[...] = v_new

    def hold(l):
        return lambda layer, i: jnp.where(layer == l, i, jnp.where(layer < l, 0, ni - 1))

    p_specs = [pl.BlockSpec((None, ta, b), (lambda layer, i, f=hold(l): (3, f(layer, i), 0))) for l in range(depth)]
    r_specs = [pl.BlockSpec((3, ta, b), (lambda layer, i, f=hold(l): (0, f(layer, i), 0))) for l in range(depth)]
    row = pl.BlockSpec((None, ta, b), lambda layer, i: (layer, i, 0))
    return pl.pallas_call(
        body,
        grid=(depth, ni),
        in_specs=p_specs + r_specs + [row, row, row],
        out_specs=[row] * 4,
        out_shape=[jax.ShapeDtypeStruct(w.shape, F32)] * 4,
        compiler_params=_cparams(),
        name=name,
    )(*parts, *recvs, w, m, v)


def _adamw_replicated(gathered, w, m, v, name):
    r, lanes = w.shape

    def body(g_ref, w_ref, m_ref, v_ref, g_out, d_out, m_out, v_out):
        g = g_ref[0]
        for k in range(1, N_DEV):
            g = g + g_ref[k]
        delta, m_new, v_new = _adamw_math(w_ref[...], g, m_ref[...], v_ref[...])
        g_out[...] = g
        d_out[...] = delta
        m_out[...] = m_new
        v_out[...] = v_new

    return pl.pallas_call(
        body,
        out_shape=[jax.ShapeDtypeStruct((r, lanes), F32)] * 4,
        name=name,
    )(gathered, w, m, v)


def _full_weight(name, gathered):
    _, a, b = gathered.shape
    if name in COL_SHARDED:
        return gathered.transpose(1, 0, 2).reshape(a, N_DEV * b)
    return gathered.reshape(N_DEV * a, b)


def _to_slabs(name, gfull, shard_shape):
    a, b = shard_shape
    if name in COL_SHARDED:
        return gfull.reshape(a, N_DEV, b).transpose(1, 0, 2)
    return gfull.reshape(N_DEV, a, b)


def _pack_small(arrs):
    rows = []
    for a in arrs:
        flat = a.astype(F32).reshape(-1)
        pad = (-flat.shape[0]) % LANES
        rows.append(jnp.pad(flat, (0, pad)).reshape(-1, LANES))
    packed = jnp.concatenate(rows, axis=0)
    return jnp.pad(packed, ((0, (-packed.shape[0]) % 8), (0, 0)))


def _unpack_small(packed, shapes):
    out, off = [], 0
    for shp in shapes:
        n = math.prod(shp)
        rows = -(-n // LANES)
        out.append(packed[off:off + rows].reshape(-1)[:n].reshape(shp))
        off += rows
    return out


def _of(layer, *names):
    return tuple((layer, n) for n in names)


MIXER_W = ("w_in", "w_up_a", "w_up_b", "w_o")
MLP_W = ("w_ff1", "w_ff2", "w_pe", "w_pg")

GATHERS = (
    (None, _of(0, *MIXER_W)),
    ("sb_fwd_l0", _of(0, *MLP_W) + _of(1, "w_ff1")),
    ("swa_fwd_l0", _of(1, *MIXER_W)),
    ("ff1_l0", _of(1, "w_ff2", "w_pe", "w_pg")),
)
REDUCES = (
    (_of(1, *WEIGHTS), "dact_l0", "sb_bwd_l0"),
    (_of(0, *MLP_W), "dh_mlp_l0", "swa_bwd_l0"),
    (_of(0, "w_o", "w_up_a", "w_up_b"), "do_a_l0", "sb_bwd_l0"),
    (_of(0, "w_in"), None, None),
)


def _merge_comms(comms):
    if len(comms) == 1:
        return comms[0]

    def cuts(counts):
        edges = [0]
        for c in counts:
            edges.append(edges[-1] + c)
        return [slice(a, b) for a, b in zip(edges[:-1], edges[1:])]

    s_in = cuts([len(c.inputs) for c in comms])
    s_out = cuts([len(c.out_shapes) for c in comms])
    s_sem = cuts([len(c.sems) for c in comms])

    def start(pos, cin, cout, csem):
        for c, i, o, s in zip(comms, s_in, s_out, s_sem):
            c.start(pos, cin[i], cout[o], csem[s])

    def finish(pos, cin, cout, csem):
        for c, i, o, s in zip(comms, s_in, s_out, s_sem):
            c.finish(pos, cin[i], cout[o], csem[s])

    return _Comm(sum([c.inputs for c in comms], []), sum([c.out_shapes for c in comms], []),
                 sum([c.sems for c in comms], []), start, finish)


class _LayerWeights:
    def __init__(self, full, layer):
        self.full, self.layer, self.cache = full, layer, {}

    def __getitem__(self, name):
        if name not in self.cache:
            if name == "w_qkv":
                self.cache[name] = self.full[(self.layer, "w_in")][:, :QKV_COLS]
            elif name == "w_gate":
                self.cache[name] = self.full[(self.layer, "w_in")][:, QKV_COLS:]
            else:
                self.cache[name] = self.full[(self.layer, name)]
        return self.cache[name]


class _Plan:
    def __init__(self, w_sh, pos):
        self.w_sh = dict(zip(WEIGHTS, w_sh))
        self.pos = pos
        self.full, self.gw, self.parts, self.recv = {}, {}, {}, {}
        self.slabs = {}
        self.hosted = {}
        for i, (host, _) in enumerate(GATHERS):
            if host is not None:
                self.hosted.setdefault(host, []).append(("gather", i))
        for i, (_, sib_host, chip_host) in enumerate(REDUCES):
            assert (sib_host is None) == (chip_host is None)
            if sib_host is not None:
                self.hosted.setdefault(sib_host, []).append(("sibling", i))
                self.hosted.setdefault(chip_host, []).append(("chips", i))

    def _gather(self, i):
        return _gather_comm([self.w_sh[n][layer].astype(WIRE) for layer, n in GATHERS[i][1]])

    def _gathered(self, i, outs):
        for (layer, n), g in zip(GATHERS[i][1], outs):
            self.full[(layer, n)] = _full_weight(n, g)

    def weights(self, layer):
        for i, (host, items) in enumerate(GATHERS):
            if host is None and items[0][0] == layer:
                self._gathered(i, _run_comm(self._gather(i), f"gather_{i}"))
        return _LayerWeights(self.full, layer)

    def grad(self, layer, name, value):
        self.gw[(layer, name)] = value

    def _sibling(self, i):
        self.slabs[i] = [_to_slabs(n, self.gw[(layer, n)], self.w_sh[n].shape[1:]) for layer, n in REDUCES[i][0]]
        return _rs_sibling_comm(self.slabs[i])

    def _sibling_done(self, i, outs):
        parts = _chip_partials(self.pos, self.slabs[i], outs, f"chip_partials_{i}")
        for item, part in zip(REDUCES[i][0], parts):
            self.parts[item] = part

    def _chips(self, i):
        return _rs_chips_comm([self.parts[item] for item in REDUCES[i][0]])

    def _chips_done(self, i, outs):
        for item, r in zip(REDUCES[i][0], outs):
            self.recv[item] = r

    def layer_done(self, layer):
        for i, (items, sib_host, _) in enumerate(REDUCES):
            if sib_host is None and items[0][0] == layer:
                self._sibling_done(i, _run_comm(self._sibling(i), f"reduce_sibling_{i}"))
                self._chips_done(i, _run_comm(self._chips(i), f"reduce_chips_{i}"))

    def comm(self, name):
        if name not in self.hosted:
            return None
        make = {"gather": self._gather, "sibling": self._sibling, "chips": self._chips}
        return _merge_comms([make[kind](i) for kind, i in self.hosted[name]])

    def done(self, name, outs):
        took = {"gather": self._gathered, "sibling": self._sibling_done, "chips": self._chips_done}
        off = 0
        for kind, i in self.hosted[name]:
            n = len(GATHERS[i][1]) if kind == "gather" else len(REDUCES[i][0])
            took[kind](i, outs[off:off + n])
            off += n


def kernel(x, p, w_in, w_up_a, w_up_b, w_o, w_ff1, w_ff2, w_pe, w_pg, g_mix, g_mlp, g_pe, g_final, sinks, rel_bias, loss_target, m_w_in, m_w_up_a, m_w_up_b, m_w_o, m_w_ff1, m_w_ff2, m_w_pe, m_w_pg, m_g_mix, m_g_mlp, m_g_pe, m_g_final, m_sinks, m_rel_bias, v_w_in, v_w_up_a, v_w_up_b, v_w_o, v_w_ff1, v_w_ff2, v_w_pe, v_w_pg, v_g_mix, v_g_mlp, v_g_pe, v_g_final, v_sinks, v_rel_bias):
    w_sh = [w_in, w_up_a, w_up_b, w_o, w_ff1, w_ff2, w_pe, w_pg]
    m_sh = [m_w_in, m_w_up_a, m_w_up_b, m_w_o, m_w_ff1, m_w_ff2, m_w_pe, m_w_pg]
    v_sh = [v_w_in, v_w_up_a, v_w_up_b, v_w_o, v_w_ff1, v_w_ff2, v_w_pe, v_w_pg]
    depth = w_in.shape[0]
    assert depth == 2 and x.shape[-1] * 2 + QKV_COLS == w_in.shape[2] * N_DEV

    px, py, pc = _position()
    plan = _Plan(w_sh, jnp.stack([px, py, pc]).astype(jnp.int32))
    loss_row, grad_x, _, small = _local_step(
        x[0], p[:, 0], loss_target[0], plan.weights, g_mix, g_mlp, g_pe, g_final, sinks, rel_bias, plan=plan)

    grad_w, delta_w, new_m, new_v = [], [], [], []
    for a, name in enumerate(WEIGHTS):
        outs = _adamw_weight([plan.parts[(l, name)] for l in range(depth)],
                             [plan.recv[(l, name)] for l in range(depth)],
                             w_sh[a], m_sh[a], v_sh[a], f"adamw_{name}")
        for lst, o in zip((grad_w, delta_w, new_m, new_v), outs):
            lst.append(o)

    small_w = [g_mix, g_mlp, g_pe, g_final, sinks, rel_bias]
    small_m = [m_g_mix, m_g_mlp, m_g_pe, m_g_final, m_sinks, m_rel_bias]
    small_v = [v_g_mix, v_g_mlp, v_g_pe, v_g_final, v_sinks, v_rel_bias]
    small_shapes = [a.shape for a in small_w] + [(1,)]
    zero = jnp.zeros((1,), F32)
    small_g = _pack_small([small[n] for n in SMALL] + [loss_row[0, :1]])
    small_all = _run_comm(_gather_comm([small_g]), "gather_small")[0]
    packed_s = _adamw_replicated(small_all, _pack_small(small_w + [zero]), _pack_small(small_m + [zero]),
                                 _pack_small(small_v + [zero + 1.0]), "adamw_replicated")
    sg, sd, sm, sv = [_unpack_small(t, small_shapes) for t in packed_s]
    loss = sg[-1][0]

    return (loss, grad_x[None], *grad_w, *sg[:-1], *delta_w, *sd[:-1], *new_m, *sm[:-1], *new_v, *sv[:-1])
```

```python
import functools
import math

import numpy as np
import jax
import jax.numpy as jnp
from jax import lax
from jax.experimental import pallas as pl
from jax.experimental.pallas import tpu as pltpu

F32 = jnp.float32
MXU = jnp.bfloat16
WIRE = jnp.bfloat16

HEAD_DIM = 64
SB_HEADS = 8
SW_HEADS = 8
SW_KV = 2
SW_GROUP = SW_HEADS // SW_KV
BLOCK = 128
N_BUCKETS = 32
MAX_DISTANCE = 128
EPS = 1e-6
SCALE = HEAD_DIM ** -0.5
SB_W = SB_HEADS * HEAD_DIM
SW_W = SW_HEADS * HEAD_DIM
QKV_COLS = 3 * SB_W + SW_W + 2 * SW_KV * HEAD_DIM
N_DEV = 8
LANES = 128
N_PAIR = SB_HEADS // 2
NEG = -1e30

ADAM_LR = 0.001
ADAM_B1 = 0.9
ADAM_B2 = 0.999
ADAM_EPS = 1e-08
ADAM_WD = 0.01
ADAM_STEP = 10

VMEM_LIMIT = 48 * 1024 * 1024
SB_TQ = 256
SB_DEAD = -105.0

WEIGHTS = ("w_in", "w_up_a", "w_up_b", "w_o", "w_ff1", "w_ff2", "w_pe", "w_pg")
COL_SHARDED = ("w_in", "w_up_a", "w_up_b", "w_ff1", "w_pe")
SMALL = ("g_mix", "g_mlp", "g_pe", "g_final", "sinks", "rel_bias")


def _cparams(**kw):
    return pltpu.CompilerParams(vmem_limit_bytes=VMEM_LIMIT, **kw)


def _dot(a, b):
    return jnp.dot(a, b, preferred_element_type=F32)


def _dot_nt(a, b):
    return lax.dot_general(a, b, (((1,), (1,)), ((), ())), preferred_element_type=F32)


def _dot_tn(a, b):
    return lax.dot_general(a, b, (((0,), (0,)), ((), ())), preferred_element_type=F32)


def _tile(n, target, unit=LANES):
    if n <= target:
        return n
    t = (target // unit) * unit
    while t > unit and n % t:
        t -= unit
    assert n % t == 0, (n, target)
    return t


def _sigmoid(x):
    return 1.0 / (1.0 + jnp.exp(-x))


class _Comm:
    def __init__(self, inputs, out_shapes, sems, start, finish):
        self.inputs, self.out_shapes, self.sems = list(inputs), list(out_shapes), list(sems)
        self.start, self.finish = start, finish


def _call(body, *, grid, in_specs, out_specs, out_shape, scratch_shapes=(), args, name, comm=None):
    n_in, n_out, n_scr = len(in_specs), len(out_shape), len(scratch_shapes)
    if comm is None:
        outs = pl.pallas_call(body, grid=grid, in_specs=list(in_specs), out_specs=list(out_specs),
                              out_shape=list(out_shape), scratch_shapes=list(scratch_shapes),
                              compiler_params=_cparams(), name=name)(*args)
        return list(outs), None
    ci, co = len(comm.inputs), len(comm.out_shapes)
    any_spec = pl.BlockSpec(memory_space=pl.ANY)

    def wrapped(*refs):
        ins, cin = refs[:n_in], refs[n_in:n_in + ci]
        o0 = n_in + ci
        outs, cout = refs[o0:o0 + n_out], refs[o0 + n_out:o0 + n_out + co]
        s0 = o0 + n_out + co
        scr, csem = refs[s0:s0 + n_scr], refs[s0 + n_scr:]
        ids = [pl.program_id(d) for d in range(len(grid))]
        first = functools.reduce(jnp.logical_and, [i == 0 for i in ids])
        last = functools.reduce(jnp.logical_and, [i == g - 1 for i, g in zip(ids, grid)])
        pos = (lax.axis_index("x"), lax.axis_index("y"), lax.axis_index("c"))

        @pl.when(first)
        def _():
            comm.start(pos, cin, cout, csem)

        body(*ins, *outs, *scr)

        @pl.when(last)
        def _():
            comm.finish(pos, cin, cout, csem)

    outs = pl.pallas_call(wrapped, grid=grid, in_specs=list(in_specs) + [any_spec] * ci,
                          out_specs=list(out_specs) + [any_spec] * co,
                          out_shape=list(out_shape) + comm.out_shapes,
                          scratch_shapes=list(scratch_shapes) + comm.sems,
                          compiler_params=_cparams(), name=name)(*args, *comm.inputs)
    return list(outs[:n_out]), list(outs[n_out:])


def _mm(a, b, *, ta=False, tb=False, extras=(), epi=None, out_dtypes=(F32,),
        tm=1024, tn=1024, tk=1024, name, comm=None):
    if ta:
        kdim, m = a.shape
    else:
        m, kdim = a.shape
    n = b.shape[0] if tb else b.shape[1]
    assert (b.shape[1] if tb else b.shape[0]) == kdim
    tm, tn, tk = _tile(m, tm), _tile(n, tn), _tile(kdim, tk)
    nk = kdim // tk
    n_ex, n_out = len(extras), len(out_dtypes)

    a_spec = (pl.BlockSpec((tk, tm), lambda i, j, k: (k, i)) if ta
              else pl.BlockSpec((tm, tk), lambda i, j, k: (i, k)))
    b_spec = (pl.BlockSpec((tn, tk), lambda i, j, k: (j, k)) if tb
              else pl.BlockSpec((tk, tn), lambda i, j, k: (k, j)))
    ex_specs = []
    for e in extras:
        assert e.shape == (m, n), (e.shape, m, n)
        ex_specs.append(pl.BlockSpec((tm, tn), lambda i, j, k: (i, j)))
    out_spec = pl.BlockSpec((tm, tn), lambda i, j, k: (i, j))

    def body(a_ref, b_ref, *rest):
        ex_refs = rest[:n_ex]
        out_refs = rest[n_ex:n_ex + n_out]
        acc = rest[-1]
        k = pl.program_id(2)

        @pl.when(k == 0)
        def _():
            acc[...] = jnp.zeros_like(acc)

        av = a_ref[...].astype(MXU)
        bv = b_ref[...].astype(MXU)
        if ta:
            acc[...] += _dot_tn(av, bv)
        elif tb:
            acc[...] += _dot_nt(av, bv)
        else:
            acc[...] += _dot(av, bv)

        @pl.when(k == nk - 1)
        def _():
            res = acc[...]
            if epi is not None:
                res = epi(res, *[e[...] for e in ex_refs])
            if not isinstance(res, tuple):
                res = (res,)
            for o_ref, r in zip(out_refs, res):
                o_ref[...] = r.astype(o_ref.dtype)

    outs, couts = _call(
        body,
        grid=(m // tm, n // tn, nk),
        in_specs=[a_spec, b_spec] + ex_specs,
        out_specs=[out_spec] * n_out,
        out_shape=[jax.ShapeDtypeStruct((m, n), dt) for dt in out_dtypes],
        scratch_shapes=[pltpu.VMEM((tm, tn), F32)],
        args=(a, b, *extras), name=name, comm=comm)
    res = outs[0] if n_out == 1 else tuple(outs)
    return res if comm is None else (res, couts)


def _rms_fwd(x, g, name):
    s, d = x.shape
    tr = _tile(s, 256)

    def body(x_ref, g_ref, h_ref, r_ref):
        xf = x_ref[...]
        r = lax.rsqrt(jnp.mean(xf * xf, axis=-1, keepdims=True) + EPS)
        h_ref[...] = ((xf * r) * g_ref[...]).astype(h_ref.dtype)
        r_ref[...] = r

    return pl.pallas_call(
        body,
        grid=(s // tr,),
        in_specs=[pl.BlockSpec((tr, d), lambda i: (i, 0)), pl.BlockSpec((1, d), lambda i: (0, 0))],
        out_specs=[pl.BlockSpec((tr, d), lambda i: (i, 0)), pl.BlockSpec((tr, 1), lambda i: (i, 0))],
        out_shape=[jax.ShapeDtypeStruct((s, d), MXU), jax.ShapeDtypeStruct((s, 1), F32)],
        compiler_params=_cparams(),
        name=name,
    )(x, g)


def _rms_bwd(x, r, g, dh, dres, name):
    s, d = x.shape
    tr = _tile(s, 256)

    def body(x_ref, r_ref, g_ref, dh_ref, dres_ref, dx_ref, dxb_ref, dg_ref):
        @pl.when(pl.program_id(0) == 0)
        def _():
            dg_ref[...] = jnp.zeros_like(dg_ref)

        rr = r_ref[...]
        xhat = x_ref[...] * rr
        dh_v = dh_ref[...]
        dxhat = dh_v * g_ref[...]
        mean = jnp.mean(dxhat * xhat, axis=-1, keepdims=True)
        dx = dres_ref[...] + rr * (dxhat - xhat * mean)
        dx_ref[...] = dx
        dxb_ref[...] = dx.astype(dxb_ref.dtype)
        dg_ref[...] += jnp.sum(dh_v * xhat, axis=0, keepdims=True)

    row = pl.BlockSpec((tr, d), lambda i: (i, 0))
    vec = pl.BlockSpec((1, d), lambda i: (0, 0))
    return pl.pallas_call(
        body,
        grid=(s // tr,),
        in_specs=[row, pl.BlockSpec((tr, 1), lambda i: (i, 0)), vec, row, row],
        out_specs=[row, row, vec],
        out_shape=[jax.ShapeDtypeStruct((s, d), F32), jax.ShapeDtypeStruct((s, d), MXU),
                   jax.ShapeDtypeStruct((1, d), F32)],
        compiler_params=_cparams(),
        name=name,
    )(x, r, g, dh, dres)


def _loss_head(x, g, target, name):
    s, d = x.shape
    tr = _tile(s, 256)

    def body(x_ref, g_ref, t_ref, loss_ref, dx_ref, dg_ref):
        @pl.when(pl.program_id(0) == 0)
        def _():
            dg_ref[...] = jnp.zeros_like(dg_ref)
            loss_ref[...] = jnp.zeros_like(loss_ref)

        xf = x_ref[...]
        gv = g_ref[...]
        r = lax.rsqrt(jnp.mean(xf * xf, axis=-1, keepdims=True) + EPS)
        xhat = xf * r
        err = xhat * gv - t_ref[...]
        loss_ref[...] += 0.5 * jnp.sum(jnp.mean(err * err, axis=-1, keepdims=True), axis=0, keepdims=True)
        dy = err * (1.0 / d)
        dxhat = dy * gv
        mean = jnp.mean(dxhat * xhat, axis=-1, keepdims=True)
        dx_ref[...] = r * (dxhat - xhat * mean)
        dg_ref[...] += jnp.sum(dy * xhat, axis=0, keepdims=True)

    row = pl.BlockSpec((tr, d), lambda i: (i, 0))
    vec = pl.BlockSpec((1, d), lambda i: (0, 0))
    return pl.pallas_call(
        body,
        grid=(s // tr,),
        in_specs=[row, vec, row],
        out_specs=[pl.BlockSpec((1, LANES), lambda i: (0, 0)), row, vec],
        out_shape=[jax.ShapeDtypeStruct((1, LANES), F32), jax.ShapeDtypeStruct((s, d), F32),
                   jax.ShapeDtypeStruct((1, d), F32)],
        compiler_params=_cparams(),
        name=name,
    )(x, g, target)


def _mix_fwd(oa, ob, wa, wb, gates, name):
    s, kd = oa.shape
    d = wa.shape[1]
    tm, tn = _tile(s, 1024), _tile(d, 512)
    nj = d // tn

    def body(oa_ref, ob_ref, wa_ref, wb_ref, ga_ref, gb_ref, out_ref):
        ya = _dot(oa_ref[...], wa_ref[...])
        yb = _dot(ob_ref[...], wb_ref[...])
        out_ref[...] = (_sigmoid(ga_ref[...]) * ya + _sigmoid(gb_ref[...]) * yb).astype(out_ref.dtype)

    o_spec = pl.BlockSpec((tm, kd), lambda i, j: (i, 0))
    w_spec = pl.BlockSpec((kd, tn), lambda i, j: (0, j))
    return pl.pallas_call(
        body,
        grid=(s // tm, nj),
        in_specs=[o_spec, o_spec, w_spec, w_spec,
                  pl.BlockSpec((tm, tn), lambda i, j: (i, j)),
                  pl.BlockSpec((tm, tn), lambda i, j: (i, j + nj))],
        out_specs=pl.BlockSpec((tm, tn), lambda i, j: (i, j)),
        out_shape=jax.ShapeDtypeStruct((s, d), MXU),
        compiler_params=_cparams(),
        name=name,
    )(oa, ob, wa, wb, gates, gates)


def _mix_bwd(dx, w_o, oa, ob, wa, wb, gates, name):
    s, kd = oa.shape
    d = wa.shape[1]
    tm, tn = _tile(s, 1024), _tile(d, 512)
    nj = d // tn

    def body(dx_ref, wo_ref, oa_ref, ob_ref, wa_ref, wb_ref, ga_ref, gb_ref,
             dya_ref, dyb_ref, dga_ref, dgb_ref):
        dm = _dot_nt(dx_ref[...], wo_ref[...])
        ya = _dot(oa_ref[...], wa_ref[...])
        yb = _dot(ob_ref[...], wb_ref[...])
        sa = _sigmoid(ga_ref[...])
        sb = _sigmoid(gb_ref[...])
        dya_ref[...] = (dm * sa).astype(dya_ref.dtype)
        dyb_ref[...] = (dm * sb).astype(dyb_ref.dtype)
        dga_ref[...] = (dm * ya * sa * (1.0 - sa)).astype(dga_ref.dtype)
        dgb_ref[...] = (dm * yb * sb * (1.0 - sb)).astype(dgb_ref.dtype)

    o_spec = pl.BlockSpec((tm, kd), lambda i, j: (i, 0))
    w_spec = pl.BlockSpec((kd, tn), lambda i, j: (0, j))
    t_spec = pl.BlockSpec((tm, tn), lambda i, j: (i, j))
    return pl.pallas_call(
        body,
        grid=(s // tm, nj),
        in_specs=[pl.BlockSpec((tm, d), lambda i, j: (i, 0)),
                  pl.BlockSpec((tn, d), lambda i, j: (j, 0)),
                  o_spec, o_spec, w_spec, w_spec, t_spec,
                  pl.BlockSpec((tm, tn), lambda i, j: (i, j + nj))],
        out_specs=[t_spec] * 4,
        out_shape=[jax.ShapeDtypeStruct((s, d), MXU)] * 4,
        compiler_params=_cparams(),
        name=name,
    )(dx, w_o, oa, ob, wa, wb, gates, gates)


def _ple(p, w_pe, h, w_pg, other, *, backward, name):
    s, kp = p.shape
    d = w_pe.shape[1]
    tm, tn = _tile(s, 1024), _tile(d, 512)

    def body(p_ref, wpe_ref, h_ref, wpg_ref, other_ref, *out_refs):
        pe = _dot(p_ref[...].astype(MXU), wpe_ref[...])
        gt = _dot(h_ref[...], wpg_ref[...])
        sg = _sigmoid(gt)
        if backward:
            dout = other_ref[...]
            out_refs[0][...] = (dout * sg).astype(out_refs[0].dtype)
            out_refs[1][...] = (dout * pe * sg * (1.0 - sg)).astype(out_refs[1].dtype)
        else:
            out_refs[0][...] = other_ref[...] + pe * sg

    t_spec = pl.BlockSpec((tm, tn), lambda i, j: (i, j))
    if backward:
        out_specs, out_shape = [t_spec, t_spec], [jax.ShapeDtypeStruct((s, d), MXU)] * 2
    else:
        out_specs, out_shape = [t_spec], [jax.ShapeDtypeStruct((s, d), F32)]
    outs = pl.pallas_call(
        body,
        grid=(s // tm, d // tn),
        in_specs=[pl.BlockSpec((tm, kp), lambda i, j: (i, 0)),
                  pl.BlockSpec((kp, tn), lambda i, j: (0, j)),
                  pl.BlockSpec((tm, d), lambda i, j: (i, 0)),
                  pl.BlockSpec((d, tn), lambda i, j: (0, j)),
                  t_spec],
        out_specs=out_specs,
        out_shape=out_shape,
        compiler_params=_cparams(),
        name=name,
    )(p, w_pe, h, w_pg, other)
    return tuple(outs) if backward else outs[0]


def _split_dot(x, tri):
    hi = x.astype(jnp.bfloat16)
    lo = (x - hi.astype(F32)).astype(jnp.bfloat16)
    return _dot(hi, tri) + _dot(lo, tri)


def _log_sigmoids(z):
    t = jnp.log1p(jnp.exp(-jnp.abs(z)))
    return jnp.minimum(z, 0.0) - t, jnp.minimum(-z, 0.0) - t


def _head_lanes(hh):
    lane = lax.broadcasted_iota(jnp.int32, (1, LANES), 1)
    return jnp.logical_and(lane >= hh * HEAD_DIM, lane < (hh + 1) * HEAD_DIM)


def _sb_fwd(qkv, name, comm=None):
    s = qkv.shape[0]
    tq = _tile(s, SB_TQ)

    def body(q_ref, k_ref, v_ref, o_ref):
        i = pl.program_id(1)
        qf = q_ref[...].astype(F32) * SCALE
        row = lax.broadcasted_iota(jnp.int32, (tq, tq), 0)
        col = lax.broadcasted_iota(jnp.int32, (tq, tq), 1)
        causal = col < row
        tri = jnp.where(row > col, 1.0, 0.0).astype(jnp.bfloat16)

        qms = [jnp.where(_head_lanes(hh), qf, 0.0).astype(MXU) for hh in range(2)]

        def block(kb, cs, accs, masked):
            rows = pl.ds(pl.multiple_of(kb * tq, tq), tq)
            ks, vs = k_ref[rows, :], v_ref[rows, :]
            new_c, new_acc = [], []
            for hh in range(2):
                lb, lm = _log_sigmoids(_dot_nt(qms[hh], ks))
                if masked:
                    lm = jnp.where(causal, lm, 0.0)
                a = jnp.exp(lb + _split_dot(lm, tri) + cs[hh])
                if masked:
                    a = jnp.where(causal, a, 0.0)
                new_acc.append(accs[hh] + _dot(a.astype(MXU), vs))
                new_c.append(cs[hh] + jnp.sum(lm, axis=1, keepdims=True))
            return tuple(new_c), tuple(new_acc)

        def top(cs):
            return jnp.maximum(jnp.max(cs[0]), jnp.max(cs[1]))

        zc, za = jnp.zeros((tq, 1), F32), jnp.zeros((tq, LANES), F32)
        cs, accs = block(i, (zc, zc), (za, za), True)

        def live(st):
            return jnp.logical_and(st[0] >= 0, st[1] > SB_DEAD)

        def walk(st):
            cs, accs = block(st[0], st[2], st[3], False)
            return st[0] - 1, top(cs), cs, accs

        accs = lax.while_loop(live, walk, (i - 1, top(cs), cs, accs))[3]
        o_ref[...] = jnp.where(_head_lanes(0), accs[0], accs[1]).astype(o_ref.dtype)

    outs, couts = _call(
        body,
        grid=(N_PAIR, s // tq),
        in_specs=[pl.BlockSpec((tq, LANES), lambda p, i: (i, p)),
                  pl.BlockSpec((s, LANES), lambda p, i: (0, N_PAIR + p)),
                  pl.BlockSpec((s, LANES), lambda p, i: (0, 2 * N_PAIR + p))],
        out_specs=[pl.BlockSpec((tq, LANES), lambda p, i: (i, p))],
        out_shape=[jax.ShapeDtypeStruct((s, SB_W), MXU)],
        args=(qkv, qkv, qkv), name=name, comm=comm)
    return outs[0] if comm is None else (outs[0], couts)


def _sb_bwd(qkv, do, name, comm=None):
    s = qkv.shape[0]
    tq = _tile(s, SB_TQ)
    nq = s // tq

    def body(q_ref, k_ref, v_ref, do_ref, dq_ref, dk_ref, dv_ref, dk_acc, dv_acc, carries):
        i = pl.program_id(1)

        @pl.when(i == 0)
        def _():
            dk_acc[...] = jnp.zeros_like(dk_acc)
            dv_acc[...] = jnp.zeros_like(dv_acc)

        qf = q_ref[...].astype(F32) * SCALE
        dof = do_ref[...]
        row = lax.broadcasted_iota(jnp.int32, (tq, tq), 0)
        col = lax.broadcasted_iota(jnp.int32, (tq, tq), 1)
        causal = col < row
        tri_rev = jnp.where(row > col, 1.0, 0.0).astype(jnp.bfloat16)
        tri_excl = jnp.where(row < col, 1.0, 0.0).astype(jnp.bfloat16)

        qms = [jnp.where(_head_lanes(hh), qf, 0.0).astype(MXU) for hh in range(2)]
        doms = [jnp.where(_head_lanes(hh), dof, jnp.zeros_like(dof)) for hh in range(2)]

        def row_sums(kb, masked):
            rows = pl.ds(pl.multiple_of(kb * tq, tq), tq)
            ks = k_ref[rows, :]
            out = []
            for hh in range(2):
                _, lm = _log_sigmoids(_dot_nt(qms[hh], ks))
                if masked:
                    lm = jnp.where(causal, lm, 0.0)
                out.append(jnp.sum(lm, axis=1, keepdims=True))
            return out

        def top(cs):
            return jnp.maximum(jnp.max(cs[0]), jnp.max(cs[1]))

        def live(st):
            return jnp.logical_and(st[0] >= 0, st[1] > SB_DEAD)

        def record(st):
            kb, cs = st[0], st[2]
            sums = row_sums(kb, False)
            for hh in range(2):
                carries[hh, kb] = cs[hh]
            cs = tuple(cs[hh] + sums[hh] for hh in range(2))
            return kb - 1, top(cs), cs

        c_diag = tuple(row_sums(i, True))
        first = lax.while_loop(live, record, (i - 1, top(c_diag), c_diag))[0] + 1

        def block(kb, cs, gpres, dqs, masked):
            rows = pl.ds(pl.multiple_of(kb * tq, tq), tq)
            ks, vs = k_ref[rows, :], v_ref[rows, :]
            new_g, new_dq = [], []
            dk_add, dv_add = None, None
            for hh in range(2):
                lb, lm = _log_sigmoids(_dot_nt(qms[hh], ks))
                if masked:
                    lm = jnp.where(causal, lm, 0.0)
                a = jnp.exp(lb + _split_dot(lm, tri_rev) + cs[hh])
                if masked:
                    a = jnp.where(causal, a, 0.0)
                g = a * _dot_nt(doms[hh], vs)
                gsum = gpres[hh] + _split_dot(g, tri_excl)
                dz = g - (g + gsum) * jnp.exp(lb)
                if masked:
                    dz = jnp.where(causal, dz, 0.0)
                dzb = dz.astype(MXU)
                new_dq.append(dqs[hh] + _dot(dzb, ks))
                dk_h = _dot_tn(dzb, qms[hh])
                dv_h = _dot_tn(a.astype(MXU), doms[hh])
                dk_add = dk_h if dk_add is None else dk_add + dk_h
                dv_add = dv_h if dv_add is None else dv_add + dv_h
                new_g.append(gpres[hh] + jnp.sum(g, axis=1, keepdims=True))
            dk_acc[rows, :] += dk_add
            dv_acc[rows, :] += dv_add
            return tuple(new_g), tuple(new_dq)

        zc, za = jnp.zeros((tq, 1), F32), jnp.zeros((tq, LANES), F32)
        gpres, dqs = lax.fori_loop(
            first, i, lambda kb, cr: block(kb, (carries[0, kb], carries[1, kb]), cr[0], cr[1], False),
            ((zc, zc), (za, za)))
        dqs = block(i, (zc, zc), gpres, dqs, True)[1]
        dq_ref[...] = (jnp.where(_head_lanes(0), dqs[0], dqs[1]) * SCALE).astype(dq_ref.dtype)

        @pl.when(i == nq - 1)
        def _():
            dk_ref[...] = dk_acc[...].astype(dk_ref.dtype)
            dv_ref[...] = dv_acc[...].astype(dv_ref.dtype)

    blk = pl.BlockSpec((tq, LANES), lambda p, i: (i, p))
    full = pl.BlockSpec((s, LANES), lambda p, i: (0, p))
    outs, couts = _call(
        body,
        grid=(N_PAIR, nq),
        in_specs=[blk,
                  pl.BlockSpec((s, LANES), lambda p, i: (0, N_PAIR + p)),
                  pl.BlockSpec((s, LANES), lambda p, i: (0, 2 * N_PAIR + p)),
                  blk],
        out_specs=[blk, full, full],
        out_shape=[jax.ShapeDtypeStruct((s, SB_W), MXU)] * 3,
        scratch_shapes=[pltpu.VMEM((s, LANES), F32), pltpu.VMEM((s, LANES), F32),
                        pltpu.VMEM((2, nq, tq, 1), F32)],
        args=(qkv, qkv, qkv, do), name=name, comm=comm)
    return tuple(outs) if comm is None else (tuple(outs), couts)


def _bucket_table():
    i = np.arange(BLOCK)[:, None]
    j = np.arange(2 * BLOCK)[None, :]
    d = np.maximum(BLOCK + i - j, 0)
    max_exact = N_BUCKETS // 2
    df = np.maximum(d, 1).astype(np.float32)
    large = max_exact + (np.log(df / max_exact) / math.log(MAX_DISTANCE / max_exact)
                         * (N_BUCKETS - max_exact)).astype(np.int32)
    large = np.minimum(large, N_BUCKETS - 1)
    return np.where(d < max_exact, d, large).astype(np.int32)


def _build_bias(rel_bias, buckets, name):
    def body(rb_ref, bk_ref, out_ref):
        h = pl.program_id(0)
        bk = bk_ref[...]
        acc = jnp.zeros(bk.shape, F32)
        for b in range(N_BUCKETS):
            acc = jnp.where(bk == b, rb_ref[b, h], acc)
        out_ref[...] = acc

    return pl.pallas_call(
        body,
        grid=(SW_HEADS,),
        in_specs=[pl.BlockSpec(memory_space=pltpu.SMEM),
                  pl.BlockSpec((BLOCK, 2 * BLOCK), lambda h: (0, 0))],
        out_specs=pl.BlockSpec((None, BLOCK, 2 * BLOCK), lambda h: (h, 0, 0)),
        out_shape=jax.ShapeDtypeStruct((SW_HEADS, BLOCK, 2 * BLOCK), F32),
        name=name,
    )(rel_bias, buckets)


def _bias_grad(dbias_layers, buckets, name):
    n_l = len(dbias_layers)

    def body(*refs):
        bk = refs[n_l][...]
        out_ref = refs[n_l + 1]
        db = refs[0][...]
        for r in refs[1:n_l]:
            db = db + r[...]
        lane = lax.broadcasted_iota(jnp.int32, (1, LANES), 1)
        acc = jnp.zeros((1, LANES), F32)
        for b in range(N_BUCKETS):
            part = jnp.sum(jnp.where(bk == b, db, 0.0), axis=1, keepdims=True)
            tot = jnp.sum(part, axis=0, keepdims=True)
            acc = jnp.where(lane == b, tot, acc)
        out_ref[...] = acc

    hspec = pl.BlockSpec((None, BLOCK, 2 * BLOCK), lambda h: (h, 0, 0))
    return pl.pallas_call(
        body,
        grid=(SW_HEADS,),
        in_specs=[hspec] * n_l + [pl.BlockSpec((BLOCK, 2 * BLOCK), lambda h: (0, 0))],
        out_specs=pl.BlockSpec((None, 1, LANES), lambda h: (h, 0, 0)),
        out_shape=jax.ShapeDtypeStruct((SW_HEADS, 1, LANES), F32),
        name=name,
    )(*dbias_layers, buckets)


def _swa_scores(qb, kp, kc, bias_ref, hh, n):
    row = lax.broadcasted_iota(jnp.int32, (BLOCK, BLOCK), 0)
    col = lax.broadcasted_iota(jnp.int32, (BLOCK, BLOCK), 1)
    s1 = _dot_nt(qb, kp) + bias_ref[hh, :, :BLOCK]
    s2 = _dot_nt(qb, kc) + bias_ref[hh, :, BLOCK:]
    no_prev = jnp.where(n > 0, 0, BLOCK)
    s1 = jnp.where(col > row + no_prev, s1, NEG)
    s2 = jnp.where(col <= row, s2, NEG)
    return s1, s2


def _swa_lanes(p, hh):
    lane = lax.broadcasted_iota(jnp.int32, (1, LANES), 1)
    kv_half = jnp.zeros((1, LANES), jnp.int32) + (2 * p + hh) // SW_GROUP
    return jnp.where(lane >= HEAD_DIM, 1, 0) == kv_half, kv_half == hh


def _to_kv_lanes(x, kv_lanes, aligned):
    return jnp.where(kv_lanes, jnp.where(aligned, x, pltpu.roll(x, HEAD_DIM, 1)), 0.0)


def _swa_fwd(qkv, bias, sinks, name, comm=None):
    s = qkv.shape[0]
    nb = s // BLOCK
    q0 = 3 * N_PAIR
    kblk, vblk = q0 + N_PAIR, q0 + N_PAIR + 1

    def body(sink_ref, q_ref, k_ref, v_ref, bias_ref, o_ref, lse_ref):
        p = pl.program_id(0)

        def step(n, carry):
            r0 = pl.multiple_of(n * BLOCK, BLOCK)
            p0 = pl.multiple_of(jnp.maximum(n - 1, 0) * BLOCK, BLOCK)
            cur, prev = pl.ds(r0, BLOCK), pl.ds(p0, BLOCK)
            qf = q_ref[cur, :].astype(F32) * SCALE
            kp, kc, vp, vc = k_ref[prev, :], k_ref[cur, :], v_ref[prev, :], v_ref[cur, :]
            outs = []
            for hh in range(2):
                kv_lanes, aligned = _swa_lanes(p, hh)
                sink = sink_ref[2 * p + hh]
                qb = _to_kv_lanes(qf, kv_lanes, aligned).astype(MXU)
                s1, s2 = _swa_scores(qb, kp, kc, bias_ref, hh, n)
                m = jnp.maximum(jnp.maximum(jnp.max(s1, axis=1, keepdims=True),
                                            jnp.max(s2, axis=1, keepdims=True)), sink)
                e1 = jnp.exp(s1 - m)
                e2 = jnp.exp(s2 - m)
                den = (jnp.sum(e1, axis=1, keepdims=True) + jnp.sum(e2, axis=1, keepdims=True)
                       + jnp.exp(sink - m))
                o = _dot((e1 / den).astype(MXU), vp) + _dot((e2 / den).astype(MXU), vc)
                outs.append(jnp.where(aligned, o, pltpu.roll(o, HEAD_DIM, 1)))
                lse_ref[hh, cur, :] = m + jnp.log(den)
            o_ref[cur, :] = jnp.where(_head_lanes(0), outs[0], outs[1]).astype(o_ref.dtype)
            return carry

        lax.fori_loop(0, nb, step, 0, unroll=2)

    outs, couts = _call(
        body,
        grid=(N_PAIR,),
        in_specs=[pl.BlockSpec(memory_space=pltpu.SMEM),
                  pl.BlockSpec((s, LANES), lambda p: (0, q0 + p)),
                  pl.BlockSpec((s, LANES), lambda p: (0, kblk)),
                  pl.BlockSpec((s, LANES), lambda p: (0, vblk)),
                  pl.BlockSpec((2, BLOCK, 2 * BLOCK), lambda p: (p, 0, 0))],
        out_specs=[pl.BlockSpec((s, LANES), lambda p: (0, p)),
                   pl.BlockSpec((None, 2, s, 1), lambda p: (p, 0, 0, 0))],
        out_shape=[jax.ShapeDtypeStruct((s, SW_W), MXU), jax.ShapeDtypeStruct((N_PAIR, 2, s, 1), F32)],
        args=(sinks, qkv, qkv, qkv, bias), name=name, comm=comm)
    return tuple(outs) if comm is None else (tuple(outs), couts)


def _swa_bwd(qkv, bias, sinks, do, lse, name, comm=None):
    s = qkv.shape[0]
    nb = s // BLOCK
    q0 = 3 * N_PAIR
    kblk, vblk = q0 + N_PAIR, q0 + N_PAIR + 1

    def body(sink_ref, q_ref, k_ref, v_ref, bias_ref, do_ref, lse_ref,
             dq_ref, dk_ref, dv_ref, dbias_ref, dsink_ref, dk_acc, dv_acc):
        p = pl.program_id(0)

        @pl.when(p == 0)
        def _():
            dk_acc[...] = jnp.zeros_like(dk_acc)
            dv_acc[...] = jnp.zeros_like(dv_acc)

        dbias_ref[...] = jnp.zeros_like(dbias_ref)

        def step(n, dsink_rows):
            r0 = pl.multiple_of(n * BLOCK, BLOCK)
            p0 = pl.multiple_of(jnp.maximum(n - 1, 0) * BLOCK, BLOCK)
            cur, prev = pl.ds(r0, BLOCK), pl.ds(p0, BLOCK)
            qf = q_ref[cur, :].astype(F32) * SCALE
            dof = do_ref[cur, :].astype(F32)
            kp, kc, vp, vc = k_ref[prev, :], k_ref[cur, :], v_ref[prev, :], v_ref[cur, :]
            dqs, new_rows = [], []
            for hh in range(2):
                kv_lanes, aligned = _swa_lanes(p, hh)
                sink = sink_ref[2 * p + hh]
                qb = _to_kv_lanes(qf, kv_lanes, aligned).astype(MXU)
                dob = _to_kv_lanes(dof, kv_lanes, aligned).astype(MXU)
                lse_b = lse_ref[hh, cur, :]
                s1, s2 = _swa_scores(qb, kp, kc, bias_ref, hh, n)
                pr1 = jnp.exp(s1 - lse_b)
                pr2 = jnp.exp(s2 - lse_b)
                dpr1 = _dot_nt(dob, vp)
                dpr2 = _dot_nt(dob, vc)
                delta = (jnp.sum(pr1 * dpr1, axis=1, keepdims=True)
                         + jnp.sum(pr2 * dpr2, axis=1, keepdims=True))
                ds1 = pr1 * (dpr1 - delta)
                ds2 = pr2 * (dpr2 - delta)
                dbias_ref[hh, :, :BLOCK] += ds1
                dbias_ref[hh, :, BLOCK:] += ds2
                ds1b, ds2b = ds1.astype(MXU), ds2.astype(MXU)
                dq = _dot(ds1b, kp) + _dot(ds2b, kc)
                dqs.append(jnp.where(aligned, dq, pltpu.roll(dq, HEAD_DIM, 1)))
                dk_acc[prev, :] += _dot_tn(ds1b, qb)
                dk_acc[cur, :] += _dot_tn(ds2b, qb)
                dv_acc[prev, :] += _dot_tn(pr1.astype(MXU), dob)
                dv_acc[cur, :] += _dot_tn(pr2.astype(MXU), dob)
                new_rows.append(dsink_rows[hh] - jnp.exp(sink - lse_b) * delta)
            dq_ref[cur, :] = (jnp.where(_head_lanes(0), dqs[0], dqs[1]) * SCALE).astype(dq_ref.dtype)
            return tuple(new_rows)

        zero = jnp.zeros((BLOCK, 1), F32)
        rows = lax.fori_loop(0, nb, step, (zero, zero), unroll=2)
        for hh in range(2):
            dsink_ref[hh] = jnp.broadcast_to(jnp.sum(rows[hh], axis=0, keepdims=True), (1, LANES))

        @pl.when(p == N_PAIR - 1)
        def _():
            dk_ref[...] = dk_acc[...].astype(dk_ref.dtype)
            dv_ref[...] = dv_acc[...].astype(dv_ref.dtype)

    pair = pl.BlockSpec((s, LANES), lambda p: (0, p))
    kv_out = pl.BlockSpec((s, LANES), lambda p: (0, 0))
    bspec = pl.BlockSpec((2, BLOCK, 2 * BLOCK), lambda p: (p, 0, 0))
    outs, couts = _call(
        body,
        grid=(N_PAIR,),
        in_specs=[pl.BlockSpec(memory_space=pltpu.SMEM),
                  pl.BlockSpec((s, LANES), lambda p: (0, q0 + p)),
                  pl.BlockSpec((s, LANES), lambda p: (0, kblk)),
                  pl.BlockSpec((s, LANES), lambda p: (0, vblk)),
                  bspec, pair,
                  pl.BlockSpec((None, 2, s, 1), lambda p: (p, 0, 0, 0))],
        out_specs=[pair, kv_out, kv_out, bspec, pl.BlockSpec((2, 1, LANES), lambda p: (p, 0, 0))],
        out_shape=[jax.ShapeDtypeStruct((s, SW_W), MXU),
                   jax.ShapeDtypeStruct((s, LANES), MXU),
                   jax.ShapeDtypeStruct((s, LANES), MXU),
                   jax.ShapeDtypeStruct((SW_HEADS, BLOCK, 2 * BLOCK), F32),
                   jax.ShapeDtypeStruct((SW_HEADS, 1, LANES), F32)],
        scratch_shapes=[pltpu.VMEM((s, LANES), F32), pltpu.VMEM((s, LANES), F32)],
        args=(sinks, qkv, qkv, qkv, bias, do, lse), name=name, comm=comm)
    return tuple(outs) if comm is None else (tuple(outs), couts)


class _NoPlan:
    def comm(self, name):
        return None

    def done(self, name, outs):
        pass

    def grad(self, layer, name, value):
        pass

    def layer_done(self, layer):
        pass


def _run(plan, fn, *args, name, **kw):
    comm = plan.comm(name)
    if comm is None:
        return fn(*args, name=name, **kw)
    res, outs = fn(*args, name=name, comm=comm, **kw)
    plan.done(name, outs)
    return res


def _layer_fwd(x, p, w, g_mix, g_mlp, g_pe, sinks, bias, tag, plan):
    h1, r1 = _rms_fwd(x, g_mix, f"rms_mix_{tag}")
    qkv = _run(plan, _mm, h1, w["w_qkv"], out_dtypes=(MXU,), name=f"proj_qkv_{tag}")
    gates = _run(plan, _mm, h1, w["w_gate"], name=f"proj_gate_{tag}")
    oa = _run(plan, _sb_fwd, qkv, name=f"sb_fwd_{tag}")
    ob, lse = _run(plan, _swa_fwd, qkv, bias, sinks, name=f"swa_fwd_{tag}")
    merged = _mix_fwd(oa, ob, w["w_up_a"], w["w_up_b"], gates, f"mix_fwd_{tag}")
    x1 = _run(plan, _mm, merged, w["w_o"], extras=(x,), epi=lambda acc, res: res + acc, name=f"out_proj_{tag}")
    h2, r2 = _rms_fwd(x1, g_mlp, f"rms_mlp_{tag}")
    u, act = _run(plan, _mm, h2, w["w_ff1"], epi=lambda acc: (acc, jnp.square(jnp.maximum(acc, 0.0))),
                  out_dtypes=(F32, MXU), name=f"ff1_{tag}")
    x2 = _run(plan, _mm, act, w["w_ff2"], extras=(x1,), epi=lambda acc, res: res + acc, name=f"ff2_{tag}")
    h3, r3 = _rms_fwd(x2, g_pe, f"rms_pe_{tag}")
    x3 = _ple(p, w["w_pe"], h3, w["w_pg"], x2, backward=False, name=f"ple_fwd_{tag}")
    saved = dict(x=x, h1=h1, r1=r1, gates=gates, qkv=qkv, lse=lse, oa=oa, ob=ob, merged=merged,
                 x1=x1, h2=h2, r2=r2, u=u, act=act, x2=x2, h3=h3, r3=r3)
    return x3, saved


def _layer_bwd(dx3, sv, p, w, g_mix, g_mlp, g_pe, sinks, bias, layer, plan):
    tag = f"l{layer}"
    gw = {}
    wire = (WIRE,)

    def dw(name, a, b):
        gw[name] = _run(plan, _mm, a, b, ta=True, out_dtypes=wire, name=f"d{name}_{tag}")
        plan.grad(layer, name, gw[name])

    dpe, dgt = _ple(p, w["w_pe"], sv["h3"], w["w_pg"], dx3, backward=True, name=f"ple_bwd_{tag}")
    dw("w_pe", p, dpe)
    dw("w_pg", sv["h3"], dgt)
    dh3 = _run(plan, _mm, dgt, w["w_pg"], tb=True, name=f"dh_pe_{tag}")
    dx2, dx2b, dg_pe = _rms_bwd(sv["x2"], sv["r3"], g_pe, dh3, dx3, f"rms_pe_bwd_{tag}")
    dw("w_ff2", sv["act"], dx2b)
    du = _run(plan, _mm, dx2b, w["w_ff2"], tb=True, extras=(sv["u"],),
              epi=lambda acc, u: acc * (2.0 * jnp.maximum(u, 0.0)), out_dtypes=(MXU,), name=f"dact_{tag}")
    dw("w_ff1", sv["h2"], du)
    dh2 = _run(plan, _mm, du, w["w_ff1"], tb=True, name=f"dh_mlp_{tag}")
    dx1, dx1b, dg_mlp = _rms_bwd(sv["x1"], sv["r2"], g_mlp, dh2, dx2, f"rms_mlp_bwd_{tag}")
    dw("w_o", sv["merged"], dx1b)
    dya, dyb, dga, dgb = _mix_bwd(dx1b, w["w_o"], sv["oa"], sv["ob"], w["w_up_a"], w["w_up_b"],
                                  sv["gates"], f"mix_bwd_{tag}")
    dw("w_up_a", sv["oa"], dya)
    dw("w_up_b", sv["ob"], dyb)
    doa = _run(plan, _mm, dya, w["w_up_a"], tb=True, out_dtypes=(MXU,), name=f"do_a_{tag}")
    dob = _run(plan, _mm, dyb, w["w_up_b"], tb=True, out_dtypes=(MXU,), name=f"do_b_{tag}")
    dqb, dkb, dvb, dbias, dsink = _run(plan, _swa_bwd, sv["qkv"], bias, sinks, dob, sv["lse"],
                                       name=f"swa_bwd_{tag}")
    dqa, dka, dva = _run(plan, _sb_bwd, sv["qkv"], doa, name=f"sb_bwd_{tag}")
    dqkv = jnp.concatenate([dqa, dka, dva, dqb, dkb, dvb], axis=1)
    gw_qkv = _mm(sv["h1"], dqkv, ta=True, out_dtypes=wire, name=f"dw_qkv_{tag}")
    gw_ga = _mm(sv["h1"], dga, ta=True, out_dtypes=wire, name=f"dw_ga_{tag}")
    gw_gb = _mm(sv["h1"], dgb, ta=True, out_dtypes=wire, name=f"dw_gb_{tag}")
    gw["w_in"] = jnp.concatenate([gw_qkv, gw_ga, gw_gb], axis=1)
    plan.grad(layer, "w_in", gw["w_in"])
    d = dga.shape[1]
    add = lambda acc, res: res + acc
    dh1 = _mm(dqkv, w["w_qkv"], tb=True, tk=768, name=f"dh_qkv_{tag}")
    dh1 = _mm(dga, w["w_gate"][:, :d], tb=True, extras=(dh1,), epi=add, name=f"dh_ga_{tag}")
    dh1 = _mm(dgb, w["w_gate"][:, d:], tb=True, extras=(dh1,), epi=add, name=f"dh_gb_{tag}")
    dx, _, dg_mix = _rms_bwd(sv["x"], sv["r1"], g_mix, dh1, dx1, f"rms_mix_bwd_{tag}")
    small = dict(g_mix=dg_mix, g_mlp=dg_mlp, g_pe=dg_pe, sinks=dsink[:, 0, 0], dbias=dbias)
    return dx, gw, small


def _local_step(x, p, target, weights, g_mix, g_mlp, g_pe, g_final, sinks, rel_bias, plan=None):
    plan = _NoPlan() if plan is None else plan
    depth = g_mix.shape[0]
    buckets = jnp.asarray(_bucket_table())
    bias = _build_bias(rel_bias, buckets, "build_bias")
    saved, wfull = [], []
    h = x
    for l in range(depth):
        wfull.append(weights(l))
        h, sv = _layer_fwd(h, p[l], wfull[l], g_mix[l:l + 1], g_mlp[l:l + 1], g_pe[l:l + 1],
                           sinks[l], bias, f"l{l}", plan)
        saved.append(sv)
    loss_row, dx, dg_final = _loss_head(h, g_final[None, :], target, "loss_head")
    gws = [None] * depth
    smalls = [None] * depth
    for l in reversed(range(depth)):
        dx, gws[l], smalls[l] = _layer_bwd(dx, saved[l], p[l], wfull[l], g_mix[l:l + 1], g_mlp[l:l + 1],
                                           g_pe[l:l + 1], sinks[l], bias, l, plan)
        plan.layer_done(l)
    drel = _bias_grad([sm["dbias"] for sm in smalls], buckets, "bias_grad")[:, 0, :N_BUCKETS].T
    small = dict(
        g_mix=jnp.concatenate([sm["g_mix"] for sm in smalls], axis=0),
        g_mlp=jnp.concatenate([sm["g_mlp"] for sm in smalls], axis=0),
        g_pe=jnp.concatenate([sm["g_pe"] for sm in smalls], axis=0),
        g_final=dg_final[0],
        sinks=jnp.stack([sm["sinks"] for sm in smalls], axis=0),
        rel_bias=drel,
    )
    return loss_row, dx, gws, small


MESH_ID = pl.DeviceIdType.MESH
ANY = pl.BlockSpec(memory_space=pl.ANY)


def _position():
    return lax.axis_index("x"), lax.axis_index("y"), lax.axis_index("c")


def _run_comm(comm, name):
    ci, co = len(comm.inputs), len(comm.out_shapes)

    def body(*refs):
        cin, cout, csem = refs[:ci], refs[ci:ci + co], refs[ci + co:]
        pos = _position()
        comm.start(pos, cin, cout, csem)
        comm.finish(pos, cin, cout, csem)

    return pl.pallas_call(body, out_shape=comm.out_shapes, in_specs=[ANY] * ci, out_specs=[ANY] * co,
                          scratch_shapes=comm.sems, name=name)(*comm.inputs)


def _gather_comm(shards):
    n = len(shards)

    def copies(pos, x_refs, out_refs, sems):
        send_sems, recv_sems, local_sems = sems
        x, y, c = pos
        me, sibling = (x, y, c), (x, y, 1 - c)
        chips = [(1 - x, y), (x, 1 - y), (1 - x, 1 - y)]

        def slot(a, px, py, pc):
            return out_refs[a].at[4 * px + 2 * py + pc]

        def copy(a, k, block, to, src=None):
            return pltpu.make_async_remote_copy(
                src_ref=slot(a, *block) if src is None else src, dst_ref=slot(a, *block),
                send_sem=send_sems.at[a, k], recv_sem=recv_sems.at[a, k],
                device_id=to, device_id_type=MESH_ID)

        mine = [pltpu.make_async_copy(x_refs[a], slot(a, *me), local_sems.at[a]) for a in range(n)]
        first = []
        for a in range(n):
            first.append(copy(a, 0, me, sibling, src=x_refs[a]))
            first += [copy(a, 1 + j, me, (*chip, c), src=x_refs[a]) for j, chip in enumerate(chips)]
        return me, sibling, chips, copy, mine, first

    def start(pos, x_refs, out_refs, sems):
        _, _, _, _, mine, first = copies(pos, x_refs, out_refs, sems)
        for cp in mine + first:
            cp.start()

    def finish(pos, x_refs, out_refs, sems):
        me, sibling, chips, copy, mine, first = copies(pos, x_refs, out_refs, sems)
        c = pos[2]
        passed = []
        for j, chip in enumerate(chips):
            for a in range(n):
                copy(a, 1 + j, (*chip, c), me).wait_recv()
                fwd = copy(a, 4 + j, (*chip, c), sibling)
                fwd.start()
                passed.append(fwd)
        for a in range(n):
            copy(a, 0, sibling, me).wait_recv()
            for j, chip in enumerate(chips):
                copy(a, 4 + j, (*chip, 1 - c), me).wait_recv()
        for cp in first + passed:
            cp.wait_send()
        for cp in mine:
            cp.wait()

    return _Comm(shards, [jax.ShapeDtypeStruct((N_DEV,) + s.shape, s.dtype) for s in shards],
                 [pltpu.SemaphoreType.DMA((n, 7)), pltpu.SemaphoreType.DMA((n, 7)),
                  pltpu.SemaphoreType.DMA((n,))], start, finish)


def _exchange_comm(arrays, n_slots, route):
    n = len(arrays)

    def copies(pos, in_refs, out_refs, sems):
        send_sems, recv_sems = sems
        out = []
        for a in range(n):
            for j in range(n_slots):
                src_slot, peer = route(pos, j)
                out.append(pltpu.make_async_remote_copy(
                    src_ref=in_refs[a].at[src_slot], dst_ref=out_refs[a].at[j],
                    send_sem=send_sems.at[a, j], recv_sem=recv_sems.at[a, j],
                    device_id=peer, device_id_type=MESH_ID))
        return out

    def start(pos, in_refs, out_refs, sems):
        for cp in copies(pos, in_refs, out_refs, sems):
            cp.start()

    def finish(pos, in_refs, out_refs, sems):
        for cp in copies(pos, in_refs, out_refs, sems):
            cp.wait()

    return _Comm(arrays, [jax.ShapeDtypeStruct((n_slots,) + g.shape[1:], g.dtype) for g in arrays],
                 [pltpu.SemaphoreType.DMA((n, n_slots)), pltpu.SemaphoreType.DMA((n, n_slots))], start, finish)


def _rs_sibling_comm(gs):
    return _exchange_comm(gs, 4, lambda pos, j: (2 * j + (1 - pos[2]), (pos[0], pos[1], 1 - pos[2])))


def _chip_of(k, x, y):
    return x ^ ((k + 1) & 1), y ^ (((k + 1) >> 1) & 1)


def _chip_partials(pos, gs, recvs, name):
    n = len(gs)

    def body(pos_ref, *refs):
        for a in range(n):
            refs[2 * n + a][...] = (refs[a][...].astype(F32) + refs[n + a][...].astype(F32)
                                    ).astype(refs[2 * n + a].dtype)

    def g_map(k, pos_ref):
        cx, cy = _chip_of(k, pos_ref[0], pos_ref[1])
        return (4 * cx + 2 * cy + pos_ref[2], 0, 0)

    def r_map(k, pos_ref):
        cx, cy = _chip_of(k, pos_ref[0], pos_ref[1])
        return (2 * cx + cy, 0, 0)

    slab = [(None,) + g.shape[1:] for g in gs]
    return pl.pallas_call(
        body,
        grid_spec=pltpu.PrefetchScalarGridSpec(
            num_scalar_prefetch=1,
            grid=(4,),
            in_specs=[pl.BlockSpec(sh, g_map) for sh in slab] + [pl.BlockSpec(sh, r_map) for sh in slab],
            out_specs=[pl.BlockSpec(sh, lambda k, pos_ref: (k, 0, 0)) for sh in slab],
        ),
        out_shape=[jax.ShapeDtypeStruct((4,) + g.shape[1:], g.dtype) for g in gs],
        compiler_params=_cparams(),
        name=name,
    )(pos, *gs, *recvs)


def _rs_chips_comm(parts):
    return _exchange_comm(parts, 3, lambda pos, k: (k, (*_chip_of(k, pos[0], pos[1]), pos[2])))


def _adamw_math(w, g, m, v):
    m = ADAM_B1 * m + (1.0 - ADAM_B1) * g
    v = ADAM_B2 * v + (1.0 - ADAM_B2) * (g * g)
    m_hat = m / (1.0 - ADAM_B1 ** ADAM_STEP)
    v_hat = v / (1.0 - ADAM_B2 ** ADAM_STEP)
    delta = -ADAM_LR * (m_hat / (jnp.sqrt(v_hat) + ADAM_EPS) + ADAM_WD * w)
    return delta, m, v


def _adamw_weight(parts, recvs, w, m, v, name):
    depth, a, b = w.shape
    ta = _tile(a, 256, unit=16)
    ni = a // ta

    def body(*refs):
        p_refs, r_refs = refs[:depth], refs[depth:2 * depth]
        w_ref, m_ref, v_ref = refs[2 * depth:2 * depth + 3]
        g_out, d_out, m_out, v_out = refs[2 * depth + 3:]
        layer = pl.program_id(0)
        g = jnp.zeros((ta, b), F32)
        for l in range(depth):
            gl = p_refs[l][...].astype(F32)
            for k in range(3):
                gl = gl + r_refs[l][k].astype(F32)
            g = jnp.where(layer == l, gl, g)
        delta, m_new, v_new = _adamw_math(w_ref[...], g, m_ref[...], v_ref[...])
        g_out[...] = g
        d_out[...] = delta
        m_out[...] = m_new
        v_out[...] = v_new

    def hold(l):
        return lambda layer, i: jnp.where(layer == l, i, jnp.where(layer < l, 0, ni - 1))

    p_specs = [pl.BlockSpec((None, ta, b), (lambda layer, i, f=hold(l): (3, f(layer, i), 0))) for l in range(depth)]
    r_specs = [pl.BlockSpec((3, ta, b), (lambda layer, i, f=hold(l): (0, f(layer, i), 0))) for l in range(depth)]
    row = pl.BlockSpec((None, ta, b), lambda layer, i: (layer, i, 0))
    return pl.pallas_call(
        body,
        grid=(depth, ni),
        in_specs=p_specs + r_specs + [row, row, row],
        out_specs=[row] * 4,
        out_shape=[jax.ShapeDtypeStruct(w.shape, F32)] * 4,
        compiler_params=_cparams(),
        name=name,
    )(*parts, *recvs, w, m, v)


def _adamw_replicated(gathered, w, m, v, name):
    r, lanes = w.shape

    def body(g_ref, w_ref, m_ref, v_ref, g_out, d_out, m_out, v_out):
        g = g_ref[0]
        for k in range(1, N_DEV):
            g = g + g_ref[k]
        delta, m_new, v_new = _adamw_math(w_ref[...], g, m_ref[...], v_ref[...])
        g_out[...] = g
        d_out[...] = delta
        m_out[...] = m_new
        v_out[...] = v_new

    return pl.pallas_call(
        body,
        out_shape=[jax.ShapeDtypeStruct((r, lanes), F32)] * 4,
        name=name,
    )(gathered, w, m, v)


def _full_weight(name, gathered):
    _, a, b = gathered.shape
    if name in COL_SHARDED:
        return gathered.transpose(1, 0, 2).reshape(a, N_DEV * b)
    return gathered.reshape(N_DEV * a, b)


def _to_slabs(name, gfull, shard_shape):
    a, b = shard_shape
    if name in COL_SHARDED:
        return gfull.reshape(a, N_DEV, b).transpose(1, 0, 2)
    return gfull.reshape(N_DEV, a, b)


def _pack_small(arrs):
    rows = []
    for a in arrs:
        flat = a.astype(F32).reshape(-1)
        pad = (-flat.shape[0]) % LANES
        rows.append(jnp.pad(flat, (0, pad)).reshape(-1, LANES))
    packed = jnp.concatenate(rows, axis=0)
    return jnp.pad(packed, ((0, (-packed.shape[0]) % 8), (0, 0)))


def _unpack_small(packed, shapes):
    out, off = [], 0
    for shp in shapes:
        n = math.prod(shp)
        rows = -(-n // LANES)
        out.append(packed[off:off + rows].reshape(-1)[:n].reshape(shp))
        off += rows
    return out


def _of(layer, *names):
    return tuple((layer, n) for n in names)


MIXER_W = ("w_in", "w_up_a", "w_up_b", "w_o")
MLP_W = ("w_ff1", "w_ff2", "w_pe", "w_pg")

GATHERS = (
    (None, _of(0, "w_in")),
    ("proj_qkv_l0", _of(0, "w_up_a", "w_up_b", "w_o")),
    ("sb_fwd_l0", _of(0, *MLP_W) + _of(1, "w_ff1")),
    ("swa_fwd_l0", _of(1, *MIXER_W)),
    ("ff1_l0", _of(1, "w_ff2", "w_pe", "w_pg")),
)
REDUCES = (
    (_of(1, *WEIGHTS), "dact_l0", "sb_bwd_l0"),
    (_of(0, *MLP_W), "dh_mlp_l0", "swa_bwd_l0"),
    (_of(0, "w_o", "w_up_a", "w_up_b"), "do_a_l0", "sb_bwd_l0"),
    (_of(0, "w_in"), None, None),
)


def _merge_comms(comms):
    if len(comms) == 1:
        return comms[0]

    def cuts(counts):
        edges = [0]
        for c in counts:
            edges.append(edges[-1] + c)
        return [slice(a, b) for a, b in zip(edges[:-1], edges[1:])]

    s_in = cuts([len(c.inputs) for c in comms])
    s_out = cuts([len(c.out_shapes) for c in comms])
    s_sem = cuts([len(c.sems) for c in comms])

    def start(pos, cin, cout, csem):
        for c, i, o, s in zip(comms, s_in, s_out, s_sem):
            c.start(pos, cin[i], cout[o], csem[s])

    def finish(pos, cin, cout, csem):
        for c, i, o, s in zip(comms, s_in, s_out, s_sem):
            c.finish(pos, cin[i], cout[o], csem[s])

    return _Comm(sum([c.inputs for c in comms], []), sum([c.out_shapes for c in comms], []),
                 sum([c.sems for c in comms], []), start, finish)


class _LayerWeights:
    def __init__(self, full, layer):
        self.full, self.layer, self.cache = full, layer, {}

    def __getitem__(self, name):
        if name not in self.cache:
            if name == "w_qkv":
                self.cache[name] = self.full[(self.layer, "w_in")][:, :QKV_COLS]
            elif name == "w_gate":
                self.cache[name] = self.full[(self.layer, "w_in")][:, QKV_COLS:]
            else:
                self.cache[name] = self.full[(self.layer, name)]
        return self.cache[name]


class _Plan:
    def __init__(self, w_sh, pos):
        self.w_sh = dict(zip(WEIGHTS, w_sh))
        self.pos = pos
        self.full, self.gw, self.parts, self.recv = {}, {}, {}, {}
        self.slabs = {}
        self.hosted = {}
        for i, (host, _) in enumerate(GATHERS):
            if host is not None:
                self.hosted.setdefault(host, []).append(("gather", i))
        for i, (_, sib_host, chip_host) in enumerate(REDUCES):
            assert (sib_host is None) == (chip_host is None)
            if sib_host is not None:
                self.hosted.setdefault(sib_host, []).append(("sibling", i))
                self.hosted.setdefault(chip_host, []).append(("chips", i))

    def _gather(self, i):
        return _gather_comm([self.w_sh[n][layer].astype(WIRE) for layer, n in GATHERS[i][1]])

    def _gathered(self, i, outs):
        for (layer, n), g in zip(GATHERS[i][1], outs):
            self.full[(layer, n)] = _full_weight(n, g)

    def weights(self, layer):
        for i, (host, items) in enumerate(GATHERS):
            if host is None and items[0][0] == layer:
                self._gathered(i, _run_comm(self._gather(i), f"gather_{i}"))
        return _LayerWeights(self.full, layer)

    def grad(self, layer, name, value):
        self.gw[(layer, name)] = value

    def _sibling(self, i):
        self.slabs[i] = [_to_slabs(n, self.gw[(layer, n)], self.w_sh[n].shape[1:]) for layer, n in REDUCES[i][0]]
        return _rs_sibling_comm(self.slabs[i])

    def _sibling_done(self, i, outs):
        parts = _chip_partials(self.pos, self.slabs[i], outs, f"chip_partials_{i}")
        for item, part in zip(REDUCES[i][0], parts):
            self.parts[item] = part

    def _chips(self, i):
        return _rs_chips_comm([self.parts[item] for item in REDUCES[i][0]])

    def _chips_done(self, i, outs):
        for item, r in zip(REDUCES[i][0], outs):
            self.recv[item] = r

    def layer_done(self, layer):
        for i, (items, sib_host, _) in enumerate(REDUCES):
            if sib_host is None and items[0][0] == layer:
                self._sibling_done(i, _run_comm(self._sibling(i), f"reduce_sibling_{i}"))
                self._chips_done(i, _run_comm(self._chips(i), f"reduce_chips_{i}"))

    def comm(self, name):
        if name not in self.hosted:
            return None
        make = {"gather": self._gather, "sibling": self._sibling, "chips": self._chips}
        return _merge_comms([make[kind](i) for kind, i in self.hosted[name]])

    def done(self, name, outs):
        took = {"gather": self._gathered, "sibling": self._sibling_done, "chips": self._chips_done}
        off = 0
        for kind, i in self.hosted[name]:
            n = len(GATHERS[i][1]) if kind == "gather" else len(REDUCES[i][0])
            took[kind](i, outs[off:off + n])
            off += n


def kernel(x, p, w_in, w_up_a, w_up_b, w_o, w_ff1, w_ff2, w_pe, w_pg, g_mix, g_mlp, g_pe, g_final, sinks, rel_bias, loss_target, m_w_in, m_w_up_a, m_w_up_b, m_w_o, m_w_ff1, m_w_ff2, m_w_pe, m_w_pg, m_g_mix, m_g_mlp, m_g_pe, m_g_final, m_sinks, m_rel_bias, v_w_in, v_w_up_a, v_w_up_b, v_w_o, v_w_ff1, v_w_ff2, v_w_pe, v_w_pg, v_g_mix, v_g_mlp, v_g_pe, v_g_final, v_sinks, v_rel_bias):
    w_sh = [w_in, w_up_a, w_up_b, w_o, w_ff1, w_ff2, w_pe, w_pg]
    m_sh = [m_w_in, m_w_up_a, m_w_up_b, m_w_o, m_w_ff1, m_w_ff2, m_w_pe, m_w_pg]
    v_sh = [v_w_in, v_w_up_a, v_w_up_b, v_w_o, v_w_ff1, v_w_ff2, v_w_pe, v_w_pg]
    depth = w_in.shape[0]
    assert depth == 2 and x.shape[-1] * 2 + QKV_COLS == w_in.shape[2] * N_DEV

    px, py, pc = _position()
    plan = _Plan(w_sh, jnp.stack([px, py, pc]).astype(jnp.int32))
    loss_row, grad_x, _, small = _local_step(
        x[0], p[:, 0], loss_target[0], plan.weights, g_mix, g_mlp, g_pe, g_final, sinks, rel_bias, plan=plan)

    grad_w, delta_w, new_m, new_v = [], [], [], []
    for a, name in enumerate(WEIGHTS):
        outs = _adamw_weight([plan.parts[(l, name)] for l in range(depth)],
                             [plan.recv[(l, name)] for l in range(depth)],
                             w_sh[a], m_sh[a], v_sh[a], f"adamw_{name}")
        for lst, o in zip((grad_w, delta_w, new_m, new_v), outs):
            lst.append(o)

    small_w = [g_mix, g_mlp, g_pe, g_final, sinks, rel_bias]
    small_m = [m_g_mix, m_g_mlp, m_g_pe, m_g_final, m_sinks, m_rel_bias]
    small_v = [v_g_mix, v_g_mlp, v_g_pe, v_g_final, v_sinks, v_rel_bias]
    small_shapes = [a.shape for a in small_w] + [(1,)]
    zero = jnp.zeros((1,), F32)
    small_g = _pack_small([small[n] for n in SMALL] + [loss_row[0, :1]])
    small_all = _run_comm(_gather_comm([small_g]), "gather_small")[0]
    packed_s = _adamw_replicated(small_all, _pack_small(small_w + [zero]), _pack_small(small_m + [zero]),
                                 _pack_small(small_v + [zero + 1.0]), "adamw_replicated")
    sg, sd, sm, sv = [_unpack_small(t, small_shapes) for t in packed_s]
    loss = sg[-1][0]

    return (loss, grad_x[None], *grad_w, *sg[:-1], *delta_w, *sd[:-1], *new_m, *sm[:-1], *new_v, *sv[:-1])
```

```python
import functools
import math

import numpy as np
import jax
import jax.numpy as jnp
from jax import lax
from jax.experimental import pallas as pl
from jax.experimental.pallas import tpu as pltpu

F32 = jnp.float32
MXU = jnp.bfloat16
WIRE = jnp.bfloat16

HEAD_DIM = 64
SB_HEADS = 8
SW_HEADS = 8
SW_KV = 2
SW_GROUP = SW_HEADS // SW_KV
BLOCK = 128
N_BUCKETS = 32
MAX_DISTANCE = 128
EPS = 1e-6
SCALE = HEAD_DIM ** -0.5
SB_W = SB_HEADS * HEAD_DIM
SW_W = SW_HEADS * HEAD_DIM
QKV_COLS = 3 * SB_W + SW_W + 2 * SW_KV * HEAD_DIM
N_DEV = 8
LANES = 128
N_PAIR = SB_HEADS // 2
NEG = -1e30

ADAM_LR = 0.001
ADAM_B1 = 0.9
ADAM_B2 = 0.999
ADAM_EPS = 1e-08
ADAM_WD = 0.01
ADAM_STEP = 10

VMEM_LIMIT = 48 * 1024 * 1024
SB_TQ = 256
SB_DEAD = -105.0

WEIGHTS = ("w_in", "w_up_a", "w_up_b", "w_o", "w_ff1", "w_ff2", "w_pe", "w_pg")
COL_SHARDED = ("w_in", "w_up_a", "w_up_b", "w_ff1", "w_pe")
SMALL = ("g_mix", "g_mlp", "g_pe", "g_final", "sinks", "rel_bias")


def _cparams(**kw):
    return pltpu.CompilerParams(vmem_limit_bytes=VMEM_LIMIT, **kw)


def _dot(a, b):
    return jnp.dot(a, b, preferred_element_type=F32)


def _dot_nt(a, b):
    return lax.dot_general(a, b, (((1,), (1,)), ((), ())), preferred_element_type=F32)


def _dot_tn(a, b):
    return lax.dot_general(a, b, (((0,), (0,)), ((), ())), preferred_element_type=F32)


def _tile(n, target, unit=LANES):
    if n <= target:
        return n
    t = (target // unit) * unit
    while t > unit and n % t:
        t -= unit
    assert n % t == 0, (n, target)
    return t


def _sigmoid(x):
    return 1.0 / (1.0 + jnp.exp(-x))


class _Comm:
    def __init__(self, inputs, out_shapes, sems, start, finish):
        self.inputs, self.out_shapes, self.sems = list(inputs), list(out_shapes), list(sems)
        self.start, self.finish = start, finish


def _call(body, *, grid, in_specs, out_specs, out_shape, scratch_shapes=(), args, name, comm=None):
    n_in, n_out, n_scr = len(in_specs), len(out_shape), len(scratch_shapes)
    if comm is None:
        outs = pl.pallas_call(body, grid=grid, in_specs=list(in_specs), out_specs=list(out_specs),
                              out_shape=list(out_shape), scratch_shapes=list(scratch_shapes),
                              compiler_params=_cparams(), name=name)(*args)
        return list(outs), None
    ci, co = len(comm.inputs), len(comm.out_shapes)
    any_spec = pl.BlockSpec(memory_space=pl.ANY)

    def wrapped(*refs):
        ins, cin = refs[:n_in], refs[n_in:n_in + ci]
        o0 = n_in + ci
        outs, cout = refs[o0:o0 + n_out], refs[o0 + n_out:o0 + n_out + co]
        s0 = o0 + n_out + co
        scr, csem = refs[s0:s0 + n_scr], refs[s0 + n_scr:]
        ids = [pl.program_id(d) for d in range(len(grid))]
        first = functools.reduce(jnp.logical_and, [i == 0 for i in ids])
        last = functools.reduce(jnp.logical_and, [i == g - 1 for i, g in zip(ids, grid)])
        pos = (lax.axis_index("x"), lax.axis_index("y"), lax.axis_index("c"))

        @pl.when(first)
        def _():
            comm.start(pos, cin, cout, csem)

        body(*ins, *outs, *scr)

        @pl.when(last)
        def _():
            comm.finish(pos, cin, cout, csem)

    outs = pl.pallas_call(wrapped, grid=grid, in_specs=list(in_specs) + [any_spec] * ci,
                          out_specs=list(out_specs) + [any_spec] * co,
                          out_shape=list(out_shape) + comm.out_shapes,
                          scratch_shapes=list(scratch_shapes) + comm.sems,
                          compiler_params=_cparams(), name=name)(*args, *comm.inputs)
    return list(outs[:n_out]), list(outs[n_out:])


def _mm(a, b, *, ta=False, tb=False, extras=(), epi=None, out_dtypes=(F32,),
        tm=1024, tn=1024, tk=1024, name, comm=None):
    if ta:
        kdim, m = a.shape
    else:
        m, kdim = a.shape
    n = b.shape[0] if tb else b.shape[1]
    assert (b.shape[1] if tb else b.shape[0]) == kdim
    tm, tn, tk = _tile(m, tm), _tile(n, tn), _tile(kdim, tk)
    nk = kdim // tk
    n_ex, n_out = len(extras), len(out_dtypes)

    a_spec = (pl.BlockSpec((tk, tm), lambda i, j, k: (k, i)) if ta
              else pl.BlockSpec((tm, tk), lambda i, j, k: (i, k)))
    b_spec = (pl.BlockSpec((tn, tk), lambda i, j, k: (j, k)) if tb
              else pl.BlockSpec((tk, tn), lambda i, j, k: (k, j)))
    ex_specs = []
    for e in extras:
        assert e.shape == (m, n), (e.shape, m, n)
        ex_specs.append(pl.BlockSpec((tm, tn), lambda i, j, k: (i, j)))
    out_spec = pl.BlockSpec((tm, tn), lambda i, j, k: (i, j))

    def body(a_ref, b_ref, *rest):
        ex_refs = rest[:n_ex]
        out_refs = rest[n_ex:n_ex + n_out]
        acc = rest[-1]
        k = pl.program_id(2)

        @pl.when(k == 0)
        def _():
            acc[...] = jnp.zeros_like(acc)

        av = a_ref[...].astype(MXU)
        bv = b_ref[...].astype(MXU)
        if ta:
            acc[...] += _dot_tn(av, bv)
        elif tb:
            acc[...] += _dot_nt(av, bv)
        else:
            acc[...] += _dot(av, bv)

        @pl.when(k == nk - 1)
        def _():
            res = acc[...]
            if epi is not None:
                res = epi(res, *[e[...] for e in ex_refs])
            if not isinstance(res, tuple):
                res = (res,)
            for o_ref, r in zip(out_refs, res):
                o_ref[...] = r.astype(o_ref.dtype)

    outs, couts = _call(
        body,
        grid=(m // tm, n // tn, nk),
        in_specs=[a_spec, b_spec] + ex_specs,
        out_specs=[out_spec] * n_out,
        out_shape=[jax.ShapeDtypeStruct((m, n), dt) for dt in out_dtypes],
        scratch_shapes=[pltpu.VMEM((tm, tn), F32)],
        args=(a, b, *extras), name=name, comm=comm)
    res = outs[0] if n_out == 1 else tuple(outs)
    return res if comm is None else (res, couts)


def _rms_fwd(x, g, name):
    s, d = x.shape
    tr = _tile(s, 256)

    def body(x_ref, g_ref, h_ref, r_ref):
        xf = x_ref[...]
        r = lax.rsqrt(jnp.mean(xf * xf, axis=-1, keepdims=True) + EPS)
        h_ref[...] = ((xf * r) * g_ref[...]).astype(h_ref.dtype)
        r_ref[...] = r

    return pl.pallas_call(
        body,
        grid=(s // tr,),
        in_specs=[pl.BlockSpec((tr, d), lambda i: (i, 0)), pl.BlockSpec((1, d), lambda i: (0, 0))],
        out_specs=[pl.BlockSpec((tr, d), lambda i: (i, 0)), pl.BlockSpec((tr, 1), lambda i: (i, 0))],
        out_shape=[jax.ShapeDtypeStruct((s, d), MXU), jax.ShapeDtypeStruct((s, 1), F32)],
        compiler_params=_cparams(),
        name=name,
    )(x, g)


def _rms_bwd(x, r, g, dh, dres, name):
    s, d = x.shape
    tr = _tile(s, 256)

    def body(x_ref, r_ref, g_ref, dh_ref, dres_ref, dx_ref, dxb_ref, dg_ref):
        @pl.when(pl.program_id(0) == 0)
        def _():
            dg_ref[...] = jnp.zeros_like(dg_ref)

        rr = r_ref[...]
        xhat = x_ref[...] * rr
        dh_v = dh_ref[...]
        dxhat = dh_v * g_ref[...]
        mean = jnp.mean(dxhat * xhat, axis=-1, keepdims=True)
        dx = dres_ref[...] + rr * (dxhat - xhat * mean)
        dx_ref[...] = dx
        dxb_ref[...] = dx.astype(dxb_ref.dtype)
        dg_ref[...] += jnp.sum(dh_v * xhat, axis=0, keepdims=True)

    row = pl.BlockSpec((tr, d), lambda i: (i, 0))
    vec = pl.BlockSpec((1, d), lambda i: (0, 0))
    return pl.pallas_call(
        body,
        grid=(s // tr,),
        in_specs=[row, pl.BlockSpec((tr, 1), lambda i: (i, 0)), vec, row, row],
        out_specs=[row, row, vec],
        out_shape=[jax.ShapeDtypeStruct((s, d), F32), jax.ShapeDtypeStruct((s, d), MXU),
                   jax.ShapeDtypeStruct((1, d), F32)],
        compiler_params=_cparams(),
        name=name,
    )(x, r, g, dh, dres)


def _loss_head(x, g, target, name):
    s, d = x.shape
    tr = _tile(s, 256)

    def body(x_ref, g_ref, t_ref, loss_ref, dx_ref, dg_ref):
        @pl.when(pl.program_id(0) == 0)
        def _():
            dg_ref[...] = jnp.zeros_like(dg_ref)
            loss_ref[...] = jnp.zeros_like(loss_ref)

        xf = x_ref[...]
        gv = g_ref[...]
        r = lax.rsqrt(jnp.mean(xf * xf, axis=-1, keepdims=True) + EPS)
        xhat = xf * r
        err = xhat * gv - t_ref[...]
        loss_ref[...] += 0.5 * jnp.sum(jnp.mean(err * err, axis=-1, keepdims=True), axis=0, keepdims=True)
        dy = err * (1.0 / d)
        dxhat = dy * gv
        mean = jnp.mean(dxhat * xhat, axis=-1, keepdims=True)
        dx_ref[...] = r * (dxhat - xhat * mean)
        dg_ref[...] += jnp.sum(dy * xhat, axis=0, keepdims=True)

    row = pl.BlockSpec((tr, d), lambda i: (i, 0))
    vec = pl.BlockSpec((1, d), lambda i: (0, 0))
    return pl.pallas_call(
        body,
        grid=(s // tr,),
        in_specs=[row, vec, row],
        out_specs=[pl.BlockSpec((1, LANES), lambda i: (0, 0)), row, vec],
        out_shape=[jax.ShapeDtypeStruct((1, LANES), F32), jax.ShapeDtypeStruct((s, d), F32),
                   jax.ShapeDtypeStruct((1, d), F32)],
        compiler_params=_cparams(),
        name=name,
    )(x, g, target)


def _mix_fwd(oa, ob, wa, wb, gates, name):
    s, kd = oa.shape
    d = wa.shape[1]
    tm, tn = _tile(s, 1024), _tile(d, 512)
    nj = d // tn

    def body(oa_ref, ob_ref, wa_ref, wb_ref, ga_ref, gb_ref, out_ref):
        ya = _dot(oa_ref[...], wa_ref[...])
        yb = _dot(ob_ref[...], wb_ref[...])
        out_ref[...] = (_sigmoid(ga_ref[...]) * ya + _sigmoid(gb_ref[...]) * yb).astype(out_ref.dtype)

    o_spec = pl.BlockSpec((tm, kd), lambda i, j: (i, 0))
    w_spec = pl.BlockSpec((kd, tn), lambda i, j: (0, j))
    return pl.pallas_call(
        body,
        grid=(s // tm, nj),
        in_specs=[o_spec, o_spec, w_spec, w_spec,
                  pl.BlockSpec((tm, tn), lambda i, j: (i, j)),
                  pl.BlockSpec((tm, tn), lambda i, j: (i, j + nj))],
        out_specs=pl.BlockSpec((tm, tn), lambda i, j: (i, j)),
        out_shape=jax.ShapeDtypeStruct((s, d), MXU),
        compiler_params=_cparams(),
        name=name,
    )(oa, ob, wa, wb, gates, gates)


def _mix_bwd(dx, w_o, oa, ob, wa, wb, gates, name):
    s, kd = oa.shape
    d = wa.shape[1]
    tm, tn = _tile(s, 1024), _tile(d, 512)
    nj = d // tn

    def body(dx_ref, wo_ref, oa_ref, ob_ref, wa_ref, wb_ref, ga_ref, gb_ref,
             dya_ref, dyb_ref, dga_ref, dgb_ref):
        dm = _dot_nt(dx_ref[...], wo_ref[...])
        ya = _dot(oa_ref[...], wa_ref[...])
        yb = _dot(ob_ref[...], wb_ref[...])
        sa = _sigmoid(ga_ref[...])
        sb = _sigmoid(gb_ref[...])
        dya_ref[...] = (dm * sa).astype(dya_ref.dtype)
        dyb_ref[...] = (dm * sb).astype(dyb_ref.dtype)
        dga_ref[...] = (dm * ya * sa * (1.0 - sa)).astype(dga_ref.dtype)
        dgb_ref[...] = (dm * yb * sb * (1.0 - sb)).astype(dgb_ref.dtype)

    o_spec = pl.BlockSpec((tm, kd), lambda i, j: (i, 0))
    w_spec = pl.BlockSpec((kd, tn), lambda i, j: (0, j))
    t_spec = pl.BlockSpec((tm, tn), lambda i, j: (i, j))
    return pl.pallas_call(
        body,
        grid=(s // tm, nj),
        in_specs=[pl.BlockSpec((tm, d), lambda i, j: (i, 0)),
                  pl.BlockSpec((tn, d), lambda i, j: (j, 0)),
                  o_spec, o_spec, w_spec, w_spec, t_spec,
                  pl.BlockSpec((tm, tn), lambda i, j: (i, j + nj))],
        out_specs=[t_spec] * 4,
        out_shape=[jax.ShapeDtypeStruct((s, d), MXU)] * 4,
        compiler_params=_cparams(),
        name=name,
    )(dx, w_o, oa, ob, wa, wb, gates, gates)


def _ple(p, w_pe, h, w_pg, other, *, backward, name):
    s, kp = p.shape
    d = w_pe.shape[1]
    tm, tn = _tile(s, 1024), _tile(d, 512)

    def body(p_ref, wpe_ref, h_ref, wpg_ref, other_ref, *out_refs):
        pe = _dot(p_ref[...].astype(MXU), wpe_ref[...])
        gt = _dot(h_ref[...], wpg_ref[...])
        sg = _sigmoid(gt)
        if backward:
            dout = other_ref[...]
            out_refs[0][...] = (dout * sg).astype(out_refs[0].dtype)
            out_refs[1][...] = (dout * pe * sg * (1.0 - sg)).astype(out_refs[1].dtype)
        else:
            out_refs[0][...] = other_ref[...] + pe * sg

    t_spec = pl.BlockSpec((tm, tn), lambda i, j: (i, j))
    if backward:
        out_specs, out_shape = [t_spec, t_spec], [jax.ShapeDtypeStruct((s, d), MXU)] * 2
    else:
        out_specs, out_shape = [t_spec], [jax.ShapeDtypeStruct((s, d), F32)]
    outs = pl.pallas_call(
        body,
        grid=(s // tm, d // tn),
        in_specs=[pl.BlockSpec((tm, kp), lambda i, j: (i, 0)),
                  pl.BlockSpec((kp, tn), lambda i, j: (0, j)),
                  pl.BlockSpec((tm, d), lambda i, j: (i, 0)),
                  pl.BlockSpec((d, tn), lambda i, j: (0, j)),
                  t_spec],
        out_specs=out_specs,
        out_shape=out_shape,
        compiler_params=_cparams(),
        name=name,
    )(p, w_pe, h, w_pg, other)
    return tuple(outs) if backward else outs[0]


def _split_dot(x, tri):
    hi = x.astype(jnp.bfloat16)
    lo = (x - hi.astype(F32)).astype(jnp.bfloat16)
    return _dot(hi, tri) + _dot(lo, tri)


def _log_sigmoids(z):
    t = jnp.log1p(jnp.exp(-jnp.abs(z)))
    return jnp.minimum(z, 0.0) - t, jnp.minimum(-z, 0.0) - t


def _head_lanes(hh):
    lane = lax.broadcasted_iota(jnp.int32, (1, LANES), 1)
    return jnp.logical_and(lane >= hh * HEAD_DIM, lane < (hh + 1) * HEAD_DIM)


def _sb_fwd(qkv, name, comm=None):
    s = qkv.shape[0]
    tq = _tile(s, SB_TQ)

    def body(q_ref, k_ref, v_ref, o_ref):
        i = pl.program_id(1)
        qf = q_ref[...].astype(F32) * SCALE
        row = lax.broadcasted_iota(jnp.int32, (tq, tq), 0)
        col = lax.broadcasted_iota(jnp.int32, (tq, tq), 1)
        causal = col < row
        tri = jnp.where(row > col, 1.0, 0.0).astype(jnp.bfloat16)

        qms = [jnp.where(_head_lanes(hh), qf, 0.0).astype(MXU) for hh in range(2)]

        def block(kb, cs, accs, masked):
            rows = pl.ds(pl.multiple_of(kb * tq, tq), tq)
            ks, vs = k_ref[rows, :], v_ref[rows, :]
            new_c, new_acc = [], []
            for hh in range(2):
                lb, lm = _log_sigmoids(_dot_nt(qms[hh], ks))
                if masked:
                    lm = jnp.where(causal, lm, 0.0)
                a = jnp.exp(lb + _split_dot(lm, tri) + cs[hh])
                if masked:
                    a = jnp.where(causal, a, 0.0)
                new_acc.append(accs[hh] + _dot(a.astype(MXU), vs))
                new_c.append(cs[hh] + jnp.sum(lm, axis=1, keepdims=True))
            return tuple(new_c), tuple(new_acc)

        def top(cs):
            return jnp.maximum(jnp.max(cs[0]), jnp.max(cs[1]))

        zc, za = jnp.zeros((tq, 1), F32), jnp.zeros((tq, LANES), F32)
        cs, accs = block(i, (zc, zc), (za, za), True)

        def live(st):
            return jnp.logical_and(st[0] >= 0, st[1] > SB_DEAD)

        def walk(st):
            cs, accs = block(st[0], st[2], st[3], False)
            return st[0] - 1, top(cs), cs, accs

        accs = lax.while_loop(live, walk, (i - 1, top(cs), cs, accs))[3]
        o_ref[...] = jnp.where(_head_lanes(0), accs[0], accs[1]).astype(o_ref.dtype)

    outs, couts = _call(
        body,
        grid=(N_PAIR, s // tq),
        in_specs=[pl.BlockSpec((tq, LANES), lambda p, i: (i, p)),
                  pl.BlockSpec((s, LANES), lambda p, i: (0, N_PAIR + p)),
                  pl.BlockSpec((s, LANES), lambda p, i: (0, 2 * N_PAIR + p))],
        out_specs=[pl.BlockSpec((tq, LANES), lambda p, i: (i, p))],
        out_shape=[jax.ShapeDtypeStruct((s, SB_W), MXU)],
        args=(qkv, qkv, qkv), name=name, comm=comm)
    return outs[0] if comm is None else (outs[0], couts)


def _sb_bwd(qkv, do, name, comm=None):
    s = qkv.shape[0]
    tq = _tile(s, SB_TQ)
    nq = s // tq

    def body(q_ref, k_ref, v_ref, do_ref, dq_ref, dk_ref, dv_ref, dk_acc, dv_acc, carries):
        i = pl.program_id(1)

        @pl.when(i == 0)
        def _():
            dk_acc[...] = jnp.zeros_like(dk_acc)
            dv_acc[...] = jnp.zeros_like(dv_acc)

        qf = q_ref[...].astype(F32) * SCALE
        dof = do_ref[...]
        row = lax.broadcasted_iota(jnp.int32, (tq, tq), 0)
        col = lax.broadcasted_iota(jnp.int32, (tq, tq), 1)
        causal = col < row
        tri_rev = jnp.where(row > col, 1.0, 0.0).astype(jnp.bfloat16)
        tri_excl = jnp.where(row < col, 1.0, 0.0).astype(jnp.bfloat16)

        qms = [jnp.where(_head_lanes(hh), qf, 0.0).astype(MXU) for hh in range(2)]
        doms = [jnp.where(_head_lanes(hh), dof, jnp.zeros_like(dof)) for hh in range(2)]

        def row_sums(kb, masked):
            rows = pl.ds(pl.multiple_of(kb * tq, tq), tq)
            ks = k_ref[rows, :]
            out = []
            for hh in range(2):
                _, lm = _log_sigmoids(_dot_nt(qms[hh], ks))
                if masked:
                    lm = jnp.where(causal, lm, 0.0)
                out.append(jnp.sum(lm, axis=1, keepdims=True))
            return out

        def top(cs):
            return jnp.maximum(jnp.max(cs[0]), jnp.max(cs[1]))

        def live(st):
            return jnp.logical_and(st[0] >= 0, st[1] > SB_DEAD)

        def record(st):
            kb, cs = st[0], st[2]
            sums = row_sums(kb, False)
            for hh in range(2):
                carries[hh, kb] = cs[hh]
            cs = tuple(cs[hh] + sums[hh] for hh in range(2))
            return kb - 1, top(cs), cs

        c_diag = tuple(row_sums(i, True))
        first = lax.while_loop(live, record, (i - 1, top(c_diag), c_diag))[0] + 1

        def block(kb, cs, gpres, dqs, masked):
            rows = pl.ds(pl.multiple_of(kb * tq, tq), tq)
            ks, vs = k_ref[rows, :], v_ref[rows, :]
            new_g, new_dq = [], []
            dk_add, dv_add = None, None
            for hh in range(2):
                lb, lm = _log_sigmoids(_dot_nt(qms[hh], ks))
                if masked:
                    lm = jnp.where(causal, lm, 0.0)
                a = jnp.exp(lb + _split_dot(lm, tri_rev) + cs[hh])
                if masked:
                    a = jnp.where(causal, a, 0.0)
                g = a * _dot_nt(doms[hh], vs)
                gsum = gpres[hh] + _split_dot(g, tri_excl)
                dz = g - (g + gsum) * jnp.exp(lb)
                if masked:
                    dz = jnp.where(causal, dz, 0.0)
                dzb = dz.astype(MXU)
                new_dq.append(dqs[hh] + _dot(dzb, ks))
                dk_h = _dot_tn(dzb, qms[hh])
                dv_h = _dot_tn(a.astype(MXU), doms[hh])
                dk_add = dk_h if dk_add is None else dk_add + dk_h
                dv_add = dv_h if dv_add is None else dv_add + dv_h
                new_g.append(gpres[hh] + jnp.sum(g, axis=1, keepdims=True))
            dk_acc[rows, :] += dk_add
            dv_acc[rows, :] += dv_add
            return tuple(new_g), tuple(new_dq)

        zc, za = jnp.zeros((tq, 1), F32), jnp.zeros((tq, LANES), F32)
        gpres, dqs = lax.fori_loop(
            first, i, lambda kb, cr: block(kb, (carries[0, kb], carries[1, kb]), cr[0], cr[1], False),
            ((zc, zc), (za, za)))
        dqs = block(i, (zc, zc), gpres, dqs, True)[1]
        dq_ref[...] = (jnp.where(_head_lanes(0), dqs[0], dqs[1]) * SCALE).astype(dq_ref.dtype)

        @pl.when(i == nq - 1)
        def _():
            dk_ref[...] = dk_acc[...].astype(dk_ref.dtype)
            dv_ref[...] = dv_acc[...].astype(dv_ref.dtype)

    blk = pl.BlockSpec((tq, LANES), lambda p, i: (i, p))
    full = pl.BlockSpec((s, LANES), lambda p, i: (0, p))
    outs, couts = _call(
        body,
        grid=(N_PAIR, nq),
        in_specs=[blk,
                  pl.BlockSpec((s, LANES), lambda p, i: (0, N_PAIR + p)),
                  pl.BlockSpec((s, LANES), lambda p, i: (0, 2 * N_PAIR + p)),
                  blk],
        out_specs=[blk, full, full],
        out_shape=[jax.ShapeDtypeStruct((s, SB_W), MXU)] * 3,
        scratch_shapes=[pltpu.VMEM((s, LANES), F32), pltpu.VMEM((s, LANES), F32),
                        pltpu.VMEM((2, nq, tq, 1), F32)],
        args=(qkv, qkv, qkv, do), name=name, comm=comm)
    return tuple(outs) if comm is None else (tuple(outs), couts)


def _bucket_table():
    i = np.arange(BLOCK)[:, None]
    j = np.arange(2 * BLOCK)[None, :]
    d = np.maximum(BLOCK + i - j, 0)
    max_exact = N_BUCKETS // 2
    df = np.maximum(d, 1).astype(np.float32)
    large = max_exact + (np.log(df / max_exact) / math.log(MAX_DISTANCE / max_exact)
                         * (N_BUCKETS - max_exact)).astype(np.int32)
    large = np.minimum(large, N_BUCKETS - 1)
    return np.where(d < max_exact, d, large).astype(np.int32)


def _build_bias(rel_bias, buckets, name):
    def body(rb_ref, bk_ref, out_ref):
        h = pl.program_id(0)
        bk = bk_ref[...]
        acc = jnp.zeros(bk.shape, F32)
        for b in range(N_BUCKETS):
            acc = jnp.where(bk == b, rb_ref[b, h], acc)
        out_ref[...] = acc

    return pl.pallas_call(
        body,
        grid=(SW_HEADS,),
        in_specs=[pl.BlockSpec(memory_space=pltpu.SMEM),
                  pl.BlockSpec((BLOCK, 2 * BLOCK), lambda h: (0, 0))],
        out_specs=pl.BlockSpec((None, BLOCK, 2 * BLOCK), lambda h: (h, 0, 0)),
        out_shape=jax.ShapeDtypeStruct((SW_HEADS, BLOCK, 2 * BLOCK), F32),
        name=name,
    )(rel_bias, buckets)


def _bias_grad(dbias_layers, buckets, name):
    n_l = len(dbias_layers)

    def body(*refs):
        bk = refs[n_l][...]
        out_ref = refs[n_l + 1]
        db = refs[0][...]
        for r in refs[1:n_l]:
            db = db + r[...]
        lane = lax.broadcasted_iota(jnp.int32, (1, LANES), 1)
        acc = jnp.zeros((1, LANES), F32)
        for b in range(N_BUCKETS):
            part = jnp.sum(jnp.where(bk == b, db, 0.0), axis=1, keepdims=True)
            tot = jnp.sum(part, axis=0, keepdims=True)
            acc = jnp.where(lane == b, tot, acc)
        out_ref[...] = acc

    hspec = pl.BlockSpec((None, BLOCK, 2 * BLOCK), lambda h: (h, 0, 0))
    return pl.pallas_call(
        body,
        grid=(SW_HEADS,),
        in_specs=[hspec] * n_l + [pl.BlockSpec((BLOCK, 2 * BLOCK), lambda h: (0, 0))],
        out_specs=pl.BlockSpec((None, 1, LANES), lambda h: (h, 0, 0)),
        out_shape=jax.ShapeDtypeStruct((SW_HEADS, 1, LANES), F32),
        name=name,
    )(*dbias_layers, buckets)


GROUP_ROWS = SW_GROUP * BLOCK


def _group_lanes(g):
    lane = lax.broadcasted_iota(jnp.int32, (1, LANES), 1)
    gvec = jnp.zeros((1, LANES), jnp.int32) + g
    return jnp.where(lane >= HEAD_DIM, 1, 0) == gvec, gvec


def _stack_heads(x, g):
    kv_lanes, gvec = _group_lanes(g)
    parts = []
    for j in range(SW_GROUP):
        half = x[:, (j // 2) * LANES:(j // 2 + 1) * LANES]
        moved = jnp.where(gvec == j % 2, half, pltpu.roll(half, HEAD_DIM, 1))
        parts.append(jnp.where(kv_lanes, moved, 0.0))
    return jnp.concatenate(parts, axis=0)


def _unstack_heads(y, g):
    _, gvec = _group_lanes(g)
    heads = []
    for j in range(SW_GROUP):
        yj = y[j * BLOCK:(j + 1) * BLOCK]
        heads.append(jnp.where(gvec == j % 2, yj, pltpu.roll(yj, HEAD_DIM, 1)))
    pairs = [jnp.where(_head_lanes(0), heads[2 * p], heads[2 * p + 1]) for p in range(SW_GROUP // 2)]
    return jnp.concatenate(pairs, axis=1)


def _per_head_col(values):
    return jnp.concatenate([jnp.zeros((BLOCK, 1), F32) + v for v in values], axis=0)


def _swa_scores(qs, kp, kc, bias_ref, n):
    row = jnp.bitwise_and(lax.broadcasted_iota(jnp.int32, (GROUP_ROWS, BLOCK), 0), BLOCK - 1)
    col = lax.broadcasted_iota(jnp.int32, (GROUP_ROWS, BLOCK), 1)
    bias = bias_ref[...].reshape(GROUP_ROWS, 2 * BLOCK)
    s1 = _dot_nt(qs, kp) + bias[:, :BLOCK]
    s2 = _dot_nt(qs, kc) + bias[:, BLOCK:]
    no_prev = jnp.where(n > 0, 0, BLOCK)
    s1 = jnp.where(col > row + no_prev, s1, NEG)
    s2 = jnp.where(col <= row, s2, NEG)
    return s1, s2


def _swa_specs(s):
    q_blk = 3 * SB_W // (2 * LANES)
    k_blk = (3 * SB_W + SW_W) // LANES
    return (pl.BlockSpec((s, 2 * LANES), lambda g: (0, q_blk + g)),
            pl.BlockSpec((s, LANES), lambda g: (0, k_blk)),
            pl.BlockSpec((s, LANES), lambda g: (0, k_blk + 1)))


def _swa_fwd(qkv, bias, sinks, name, comm=None):
    s = qkv.shape[0]
    nb = s // BLOCK

    def body(sink_ref, q_ref, k_ref, v_ref, bias_ref, o_ref, lse_ref):
        g = pl.program_id(0)
        sink = _per_head_col([sink_ref[SW_GROUP * g + j] for j in range(SW_GROUP)])
        lane = lax.broadcasted_iota(jnp.int32, (1, LANES), 1)

        def step(n, carry):
            r0 = pl.multiple_of(n * BLOCK, BLOCK)
            p0 = pl.multiple_of(jnp.maximum(n - 1, 0) * BLOCK, BLOCK)
            cur, prev = pl.ds(r0, BLOCK), pl.ds(p0, BLOCK)
            qs = _stack_heads(q_ref[cur, :].astype(F32) * SCALE, g).astype(MXU)
            s1, s2 = _swa_scores(qs, k_ref[prev, :], k_ref[cur, :], bias_ref, n)
            m = jnp.maximum(jnp.maximum(jnp.max(s1, axis=1, keepdims=True),
                                        jnp.max(s2, axis=1, keepdims=True)), sink)
            e1 = jnp.exp(s1 - m)
            e2 = jnp.exp(s2 - m)
            den = jnp.sum(e1, axis=1, keepdims=True) + jnp.sum(e2, axis=1, keepdims=True) + jnp.exp(sink - m)
            o = _dot((e1 / den).astype(MXU), v_ref[prev, :]) + _dot((e2 / den).astype(MXU), v_ref[cur, :])
            o_ref[cur, :] = _unstack_heads(o, g).astype(o_ref.dtype)
            lse = m + jnp.log(den)
            lse_row = jnp.zeros((BLOCK, LANES), F32)
            for j in range(SW_GROUP):
                lse_row = jnp.where(lane == j, lse[j * BLOCK:(j + 1) * BLOCK], lse_row)
            lse_ref[cur, :] = lse_row
            return carry

        lax.fori_loop(0, nb, step, 0)

    outs, couts = _call(
        body,
        grid=(SW_KV,),
        in_specs=[pl.BlockSpec(memory_space=pltpu.SMEM), *_swa_specs(s),
                  pl.BlockSpec((SW_GROUP, BLOCK, 2 * BLOCK), lambda g: (g, 0, 0))],
        out_specs=[pl.BlockSpec((s, 2 * LANES), lambda g: (0, g)),
                   pl.BlockSpec((None, s, LANES), lambda g: (g, 0, 0))],
        out_shape=[jax.ShapeDtypeStruct((s, SW_W), MXU), jax.ShapeDtypeStruct((SW_KV, s, LANES), F32)],
        args=(sinks, qkv, qkv, qkv, bias), name=name, comm=comm)
    return tuple(outs) if comm is None else (tuple(outs), couts)


def _swa_bwd(qkv, bias, sinks, do, lse, name, comm=None):
    s = qkv.shape[0]
    nb = s // BLOCK

    def body(sink_ref, q_ref, k_ref, v_ref, bias_ref, do_ref, lse_ref,
             dq_ref, dk_ref, dv_ref, dbias_ref, dsink_ref, dk_acc, dv_acc):
        g = pl.program_id(0)
        sink = _per_head_col([sink_ref[SW_GROUP * g + j] for j in range(SW_GROUP)])
        lane = lax.broadcasted_iota(jnp.int32, (1, LANES), 1)

        @pl.when(g == 0)
        def _():
            dk_acc[...] = jnp.zeros_like(dk_acc)
            dv_acc[...] = jnp.zeros_like(dv_acc)

        dbias_ref[...] = jnp.zeros_like(dbias_ref)

        def step(n, dsink_rows):
            r0 = pl.multiple_of(n * BLOCK, BLOCK)
            p0 = pl.multiple_of(jnp.maximum(n - 1, 0) * BLOCK, BLOCK)
            cur, prev = pl.ds(r0, BLOCK), pl.ds(p0, BLOCK)
            qs = _stack_heads(q_ref[cur, :].astype(F32) * SCALE, g).astype(MXU)
            dos = _stack_heads(do_ref[cur, :].astype(F32), g).astype(MXU)
            kp, kc, vp, vc = k_ref[prev, :], k_ref[cur, :], v_ref[prev, :], v_ref[cur, :]
            lse_row = lse_ref[cur, :]
            lse = jnp.concatenate([jnp.sum(jnp.where(lane == j, lse_row, 0.0), axis=1, keepdims=True)
                                   for j in range(SW_GROUP)], axis=0)
            s1, s2 = _swa_scores(qs, kp, kc, bias_ref, n)
            pr1 = jnp.exp(s1 - lse)
            pr2 = jnp.exp(s2 - lse)
            dpr1 = _dot_nt(dos, vp)
            dpr2 = _dot_nt(dos, vc)
            delta = jnp.sum(pr1 * dpr1, axis=1, keepdims=True) + jnp.sum(pr2 * dpr2, axis=1, keepdims=True)
            ds1 = pr1 * (dpr1 - delta)
            ds2 = pr2 * (dpr2 - delta)
            dbias_ref[:, :, :BLOCK] += ds1.reshape(SW_GROUP, BLOCK, BLOCK)
            dbias_ref[:, :, BLOCK:] += ds2.reshape(SW_GROUP, BLOCK, BLOCK)
            ds1b, ds2b = ds1.astype(MXU), ds2.astype(MXU)
            dq = _dot(ds1b, kp) + _dot(ds2b, kc)
            dq_ref[cur, :] = (_unstack_heads(dq, g) * SCALE).astype(dq_ref.dtype)
            dk_acc[prev, :] += _dot_tn(ds1b, qs)
            dk_acc[cur, :] += _dot_tn(ds2b, qs)
            dv_acc[prev, :] += _dot_tn(pr1.astype(MXU), dos)
            dv_acc[cur, :] += _dot_tn(pr2.astype(MXU), dos)
            return dsink_rows - jnp.exp(sink - lse) * delta

        rows = lax.fori_loop(0, nb, step, jnp.zeros((GROUP_ROWS, 1), F32))
        for j in range(SW_GROUP):
            dsink_ref[j] = jnp.broadcast_to(jnp.sum(rows[j * BLOCK:(j + 1) * BLOCK], axis=0, keepdims=True),
                                            (1, LANES))

        @pl.when(g == SW_KV - 1)
        def _():
            dk_ref[...] = dk_acc[...].astype(dk_ref.dtype)
            dv_ref[...] = dv_acc[...].astype(dv_ref.dtype)

    grp = pl.BlockSpec((s, 2 * LANES), lambda g: (0, g))
    kv_out = pl.BlockSpec((s, LANES), lambda g: (0, 0))
    bspec = pl.BlockSpec((SW_GROUP, BLOCK, 2 * BLOCK), lambda g: (g, 0, 0))
    outs, couts = _call(
        body,
        grid=(SW_KV,),
        in_specs=[pl.BlockSpec(memory_space=pltpu.SMEM), *_swa_specs(s), bspec, grp,
                  pl.BlockSpec((None, s, LANES), lambda g: (g, 0, 0))],
        out_specs=[grp, kv_out, kv_out, bspec, pl.BlockSpec((SW_GROUP, 1, LANES), lambda g: (g, 0, 0))],
        out_shape=[jax.ShapeDtypeStruct((s, SW_W), MXU),
                   jax.ShapeDtypeStruct((s, LANES), MXU),
                   jax.ShapeDtypeStruct((s, LANES), MXU),
                   jax.ShapeDtypeStruct((SW_HEADS, BLOCK, 2 * BLOCK), F32),
                   jax.ShapeDtypeStruct((SW_HEADS, 1, LANES), F32)],
        scratch_shapes=[pltpu.VMEM((s, LANES), F32), pltpu.VMEM((s, LANES), F32)],
        args=(sinks, qkv, qkv, qkv, bias, do, lse), name=name, comm=comm)
    return tuple(outs) if comm is None else (tuple(outs), couts)


class _NoPlan:
    def comm(self, name):
        return None

    def done(self, name, outs):
        pass

    def grad(self, layer, name, value):
        pass

    def layer_done(self, layer):
        pass


def _run(plan, fn, *args, name, **kw):
    comm = plan.comm(name)
    if comm is None:
        return fn(*args, name=name, **kw)
    res, outs = fn(*args, name=name, comm=comm, **kw)
    plan.done(name, outs)
    return res


def _layer_fwd(x, p, w, g_mix, g_mlp, g_pe, sinks, bias, tag, plan):
    h1, r1 = _rms_fwd(x, g_mix, f"rms_mix_{tag}")
    qkv = _run(plan, _mm, h1, w["w_qkv"], out_dtypes=(MXU,), name=f"proj_qkv_{tag}")
    gates = _run(plan, _mm, h1, w["w_gate"], name=f"proj_gate_{tag}")
    oa = _run(plan, _sb_fwd, qkv, name=f"sb_fwd_{tag}")
    ob, lse = _run(plan, _swa_fwd, qkv, bias, sinks, name=f"swa_fwd_{tag}")
    merged = _mix_fwd(oa, ob, w["w_up_a"], w["w_up_b"], gates, f"mix_fwd_{tag}")
    x1 = _run(plan, _mm, merged, w["w_o"], extras=(x,), epi=lambda acc, res: res + acc, name=f"out_proj_{tag}")
    h2, r2 = _rms_fwd(x1, g_mlp, f"rms_mlp_{tag}")
    u, act = _run(plan, _mm, h2, w["w_ff1"], epi=lambda acc: (acc, jnp.square(jnp.maximum(acc, 0.0))),
                  out_dtypes=(F32, MXU), name=f"ff1_{tag}")
    x2 = _run(plan, _mm, act, w["w_ff2"], extras=(x1,), epi=lambda acc, res: res + acc, name=f"ff2_{tag}")
    h3, r3 = _rms_fwd(x2, g_pe, f"rms_pe_{tag}")
    x3 = _ple(p, w["w_pe"], h3, w["w_pg"], x2, backward=False, name=f"ple_fwd_{tag}")
    saved = dict(x=x, h1=h1, r1=r1, gates=gates, qkv=qkv, lse=lse, oa=oa, ob=ob, merged=merged,
                 x1=x1, h2=h2, r2=r2, u=u, act=act, x2=x2, h3=h3, r3=r3)
    return x3, saved


def _layer_bwd(dx3, sv, p, w, g_mix, g_mlp, g_pe, sinks, bias, layer, plan):
    tag = f"l{layer}"
    gw = {}
    wire = (WIRE,)

    def dw(name, a, b):
        gw[name] = _run(plan, _mm, a, b, ta=True, out_dtypes=wire, name=f"d{name}_{tag}")
        plan.grad(layer, name, gw[name])

    dpe, dgt = _ple(p, w["w_pe"], sv["h3"], w["w_pg"], dx3, backward=True, name=f"ple_bwd_{tag}")
    dw("w_pe", p, dpe)
    dw("w_pg", sv["h3"], dgt)
    dh3 = _run(plan, _mm, dgt, w["w_pg"], tb=True, name=f"dh_pe_{tag}")
    dx2, dx2b, dg_pe = _rms_bwd(sv["x2"], sv["r3"], g_pe, dh3, dx3, f"rms_pe_bwd_{tag}")
    dw("w_ff2", sv["act"], dx2b)
    du = _run(plan, _mm, dx2b, w["w_ff2"], tb=True, extras=(sv["u"],),
              epi=lambda acc, u: acc * (2.0 * jnp.maximum(u, 0.0)), out_dtypes=(MXU,), name=f"dact_{tag}")
    dw("w_ff1", sv["h2"], du)
    dh2 = _run(plan, _mm, du, w["w_ff1"], tb=True, name=f"dh_mlp_{tag}")
    dx1, dx1b, dg_mlp = _rms_bwd(sv["x1"], sv["r2"], g_mlp, dh2, dx2, f"rms_mlp_bwd_{tag}")
    dw("w_o", sv["merged"], dx1b)
    dya, dyb, dga, dgb = _mix_bwd(dx1b, w["w_o"], sv["oa"], sv["ob"], w["w_up_a"], w["w_up_b"],
                                  sv["gates"], f"mix_bwd_{tag}")
    dw("w_up_a", sv["oa"], dya)
    dw("w_up_b", sv["ob"], dyb)
    doa = _run(plan, _mm, dya, w["w_up_a"], tb=True, out_dtypes=(MXU,), name=f"do_a_{tag}")
    dob = _run(plan, _mm, dyb, w["w_up_b"], tb=True, out_dtypes=(MXU,), name=f"do_b_{tag}")
    dqb, dkb, dvb, dbias, dsink = _run(plan, _swa_bwd, sv["qkv"], bias, sinks, dob, sv["lse"],
                                       name=f"swa_bwd_{tag}")
    dqa, dka, dva = _run(plan, _sb_bwd, sv["qkv"], doa, name=f"sb_bwd_{tag}")
    dqkv = jnp.concatenate([dqa, dka, dva, dqb, dkb, dvb], axis=1)
    gw_qkv = _mm(sv["h1"], dqkv, ta=True, out_dtypes=wire, name=f"dw_qkv_{tag}")
    gw_ga = _mm(sv["h1"], dga, ta=True, out_dtypes=wire, name=f"dw_ga_{tag}")
    gw_gb = _mm(sv["h1"], dgb, ta=True, out_dtypes=wire, name=f"dw_gb_{tag}")
    gw["w_in"] = jnp.concatenate([gw_qkv, gw_ga, gw_gb], axis=1)
    plan.grad(layer, "w_in", gw["w_in"])
    d = dga.shape[1]
    add = lambda acc, res: res + acc
    dh1 = _mm(dqkv, w["w_qkv"], tb=True, tk=768, name=f"dh_qkv_{tag}")
    dh1 = _mm(dga, w["w_gate"][:, :d], tb=True, extras=(dh1,), epi=add, name=f"dh_ga_{tag}")
    dh1 = _mm(dgb, w["w_gate"][:, d:], tb=True, extras=(dh1,), epi=add, name=f"dh_gb_{tag}")
    dx, _, dg_mix = _rms_bwd(sv["x"], sv["r1"], g_mix, dh1, dx1, f"rms_mix_bwd_{tag}")
    small = dict(g_mix=dg_mix, g_mlp=dg_mlp, g_pe=dg_pe, sinks=dsink[:, 0, 0], dbias=dbias)
    return dx, gw, small


def _local_step(x, p, target, weights, g_mix, g_mlp, g_pe, g_final, sinks, rel_bias, plan=None):
    plan = _NoPlan() if plan is None else plan
    depth = g_mix.shape[0]
    buckets = jnp.asarray(_bucket_table())
    bias = _build_bias(rel_bias, buckets, "build_bias")
    saved, wfull = [], []
    h = x
    for l in range(depth):
        wfull.append(weights(l))
        h, sv = _layer_fwd(h, p[l], wfull[l], g_mix[l:l + 1], g_mlp[l:l + 1], g_pe[l:l + 1],
                           sinks[l], bias, f"l{l}", plan)
        saved.append(sv)
    loss_row, dx, dg_final = _loss_head(h, g_final[None, :], target, "loss_head")
    gws = [None] * depth
    smalls = [None] * depth
    for l in reversed(range(depth)):
        dx, gws[l], smalls[l] = _layer_bwd(dx, saved[l], p[l], wfull[l], g_mix[l:l + 1], g_mlp[l:l + 1],
                                           g_pe[l:l + 1], sinks[l], bias, l, plan)
        plan.layer_done(l)
    drel = _bias_grad([sm["dbias"] for sm in smalls], buckets, "bias_grad")[:, 0, :N_BUCKETS].T
    small = dict(
        g_mix=jnp.concatenate([sm["g_mix"] for sm in smalls], axis=0),
        g_mlp=jnp.concatenate([sm["g_mlp"] for sm in smalls], axis=0),
        g_pe=jnp.concatenate([sm["g_pe"] for sm in smalls], axis=0),
        g_final=dg_final[0],
        sinks=jnp.stack([sm["sinks"] for sm in smalls], axis=0),
        rel_bias=drel,
    )
    return loss_row, dx, gws, small


MESH_ID = pl.DeviceIdType.MESH
ANY = pl.BlockSpec(memory_space=pl.ANY)


def _position():
    return lax.axis_index("x"), lax.axis_index("y"), lax.axis_index("c")


def _run_comm(comm, name):
    ci, co = len(comm.inputs), len(comm.out_shapes)

    def body(*refs):
        cin, cout, csem = refs[:ci], refs[ci:ci + co], refs[ci + co:]
        pos = _position()
        comm.start(pos, cin, cout, csem)
        comm.finish(pos, cin, cout, csem)

    return pl.pallas_call(body, out_shape=comm.out_shapes, in_specs=[ANY] * ci, out_specs=[ANY] * co,
                          scratch_shapes=comm.sems, name=name)(*comm.inputs)


def _gather_comm(shards):
    n = len(shards)

    def copies(pos, x_refs, out_refs, sems):
        send_sems, recv_sems, local_sems = sems
        x, y, c = pos
        me, sibling = (x, y, c), (x, y, 1 - c)
        chips = [(1 - x, y), (x, 1 - y), (1 - x, 1 - y)]

        def slot(a, px, py, pc):
            return out_refs[a].at[4 * px + 2 * py + pc]

        def copy(a, k, block, to, src=None):
            return pltpu.make_async_remote_copy(
                src_ref=slot(a, *block) if src is None else src, dst_ref=slot(a, *block),
                send_sem=send_sems.at[a, k], recv_sem=recv_sems.at[a, k],
                device_id=to, device_id_type=MESH_ID)

        mine = [pltpu.make_async_copy(x_refs[a], slot(a, *me), local_sems.at[a]) for a in range(n)]
        first = []
        for a in range(n):
            first.append(copy(a, 0, me, sibling, src=x_refs[a]))
            first += [copy(a, 1 + j, me, (*chip, c), src=x_refs[a]) for j, chip in enumerate(chips)]
        return me, sibling, chips, copy, mine, first

    def start(pos, x_refs, out_refs, sems):
        _, _, _, _, mine, first = copies(pos, x_refs, out_refs, sems)
        for cp in mine + first:
            cp.start()

    def finish(pos, x_refs, out_refs, sems):
        me, sibling, chips, copy, mine, first = copies(pos, x_refs, out_refs, sems)
        c = pos[2]
        passed = []
        for j, chip in enumerate(chips):
            for a in range(n):
                copy(a, 1 + j, (*chip, c), me).wait_recv()
                fwd = copy(a, 4 + j, (*chip, c), sibling)
                fwd.start()
                passed.append(fwd)
        for a in range(n):
            copy(a, 0, sibling, me).wait_recv()
            for j, chip in enumerate(chips):
                copy(a, 4 + j, (*chip, 1 - c), me).wait_recv()
        for cp in first + passed:
            cp.wait_send()
        for cp in mine:
            cp.wait()

    return _Comm(shards, [jax.ShapeDtypeStruct((N_DEV,) + s.shape, s.dtype) for s in shards],
                 [pltpu.SemaphoreType.DMA((n, 7)), pltpu.SemaphoreType.DMA((n, 7)),
                  pltpu.SemaphoreType.DMA((n,))], start, finish)


def _exchange_comm(arrays, n_slots, route):
    n = len(arrays)

    def copies(pos, in_refs, out_refs, sems):
        send_sems, recv_sems = sems
        out = []
        for a in range(n):
            for j in range(n_slots):
                src_slot, peer = route(pos, j)
                out.append(pltpu.make_async_remote_copy(
                    src_ref=in_refs[a].at[src_slot], dst_ref=out_refs[a].at[j],
                    send_sem=send_sems.at[a, j], recv_sem=recv_sems.at[a, j],
                    device_id=peer, device_id_type=MESH_ID))
        return out

    def start(pos, in_refs, out_refs, sems):
        for cp in copies(pos, in_refs, out_refs, sems):
            cp.start()

    def finish(pos, in_refs, out_refs, sems):
        for cp in copies(pos, in_refs, out_refs, sems):
            cp.wait()

    return _Comm(arrays, [jax.ShapeDtypeStruct((n_slots,) + g.shape[1:], g.dtype) for g in arrays],
                 [pltpu.SemaphoreType.DMA((n, n_slots)), pltpu.SemaphoreType.DMA((n, n_slots))], start, finish)


def _rs_sibling_comm(gs):
    return _exchange_comm(gs, 4, lambda pos, j: (2 * j + (1 - pos[2]), (pos[0], pos[1], 1 - pos[2])))


def _chip_of(k, x, y):
    return x ^ ((k + 1) & 1), y ^ (((k + 1) >> 1) & 1)


def _chip_partials(pos, gs, recvs, name):
    n = len(gs)

    def body(pos_ref, *refs):
        for a in range(n):
            refs[2 * n + a][...] = (refs[a][...].astype(F32) + refs[n + a][...].astype(F32)
                                    ).astype(refs[2 * n + a].dtype)

    def g_map(k, pos_ref):
        cx, cy = _chip_of(k, pos_ref[0], pos_ref[1])
        return (4 * cx + 2 * cy + pos_ref[2], 0, 0)

    def r_map(k, pos_ref):
        cx, cy = _chip_of(k, pos_ref[0], pos_ref[1])
        return (2 * cx + cy, 0, 0)

    slab = [(None,) + g.shape[1:] for g in gs]
    return pl.pallas_call(
        body,
        grid_spec=pltpu.PrefetchScalarGridSpec(
            num_scalar_prefetch=1,
            grid=(4,),
            in_specs=[pl.BlockSpec(sh, g_map) for sh in slab] + [pl.BlockSpec(sh, r_map) for sh in slab],
            out_specs=[pl.BlockSpec(sh, lambda k, pos_ref: (k, 0, 0)) for sh in slab],
        ),
        out_shape=[jax.ShapeDtypeStruct((4,) + g.shape[1:], g.dtype) for g in gs],
        compiler_params=_cparams(),
        name=name,
    )(pos, *gs, *recvs)


def _rs_chips_comm(parts):
    return _exchange_comm(parts, 3, lambda pos, k: (k, (*_chip_of(k, pos[0], pos[1]), pos[2])))


def _adamw_math(w, g, m, v):
    m = ADAM_B1 * m + (1.0 - ADAM_B1) * g
    v = ADAM_B2 * v + (1.0 - ADAM_B2) * (g * g)
    m_hat = m / (1.0 - ADAM_B1 ** ADAM_STEP)
    v_hat = v / (1.0 - ADAM_B2 ** ADAM_STEP)
    delta = -ADAM_LR * (m_hat / (jnp.sqrt(v_hat) + ADAM_EPS) + ADAM_WD * w)
    return delta, m, v


def _adamw_weight(parts, recvs, w, m, v, name):
    depth, a, b = w.shape
    ta = _tile(a, 256, unit=16)
    ni = a // ta

    def body(*refs):
        p_refs, r_refs = refs[:depth], refs[depth:2 * depth]
        w_ref, m_ref, v_ref = refs[2 * depth:2 * depth + 3]
        g_out, d_out, m_out, v_out = refs[2 * depth + 3:]
        layer = pl.program_id(0)
        g = jnp.zeros((ta, b), F32)
        for l in range(depth):
            gl = p_refs[l][...].astype(F32)
            for k in range(3):
                gl = gl + r_refs[l][k].astype(F32)
            g = jnp.where(layer == l, gl, g)
        delta, m_new, v_new = _adamw_math(w_ref[...], g, m_ref[...], v_ref[...])
        g_out[...] = g
        d_out[...] = delta
        m_out[...] = m_new
        v_out[...] = v_new

    def hold(l):
        return lambda layer, i: jnp.where(layer == l, i, jnp.where(layer < l, 0, ni - 1))

    p_specs = [pl.BlockSpec((None, ta, b), (lambda layer, i, f=hold(l): (3, f(layer, i), 0))) for l in range(depth)]
    r_specs = [pl.BlockSpec((3, ta, b), (lambda layer, i, f=hold(l): (0, f(layer, i), 0))) for l in range(depth)]
    row = pl.BlockSpec((None, ta, b), lambda layer, i: (layer, i, 0))
    return pl.pallas_call(
        body,
        grid=(depth, ni),
        in_specs=p_specs + r_specs + [row, row, row],
        out_specs=[row] * 4,
        out_shape=[jax.ShapeDtypeStruct(w.shape, F32)] * 4,
        compiler_params=_cparams(),
        name=name,
    )(*parts, *recvs, w, m, v)


def _adamw_replicated(gathered, w, m, v, name):
    r, lanes = w.shape

    def body(g_ref, w_ref, m_ref, v_ref, g_out, d_out, m_out, v_out):
        g = g_ref[0]
        for k in range(1, N_DEV):
            g = g + g_ref[k]
        delta, m_new, v_new = _adamw_math(w_ref[...], g, m_ref[...], v_ref[...])
        g_out[...] = g
        d_out[...] = delta
        m_out[...] = m_new
        v_out[...] = v_new

    return pl.pallas_call(
        body,
        out_shape=[jax.ShapeDtypeStruct((r, lanes), F32)] * 4,
        name=name,
    )(gathered, w, m, v)


def _full_weight(name, gathered):
    _, a, b = gathered.shape
    if name in COL_SHARDED:
        return gathered.transpose(1, 0, 2).reshape(a, N_DEV * b)
    return gathered.reshape(N_DEV * a, b)


def _to_slabs(name, gfull, shard_shape):
    a, b = shard_shape
    if name in COL_SHARDED:
        return gfull.reshape(a, N_DEV, b).transpose(1, 0, 2)
    return gfull.reshape(N_DEV, a, b)


def _pack_small(arrs):
    rows = []
    for a in arrs:
        flat = a.astype(F32).reshape(-1)
        pad = (-flat.shape[0]) % LANES
        rows.append(jnp.pad(flat, (0, pad)).reshape(-1, LANES))
    packed = jnp.concatenate(rows, axis=0)
    return jnp.pad(packed, ((0, (-packed.shape[0]) % 8), (0, 0)))


def _unpack_small(packed, shapes):
    out, off = [], 0
    for shp in shapes:
        n = math.prod(shp)
        rows = -(-n // LANES)
        out.append(packed[off:off + rows].reshape(-1)[:n].reshape(shp))
        off += rows
    return out


def _of(layer, *names):
    return tuple((layer, n) for n in names)


MIXER_W = ("w_in", "w_up_a", "w_up_b", "w_o")
MLP_W = ("w_ff1", "w_ff2", "w_pe", "w_pg")

GATHERS = (
    (None, _of(0, "w_in")),
    ("proj_qkv_l0", _of(0, "w_up_a", "w_up_b", "w_o")),
    ("sb_fwd_l0", _of(0, *MLP_W) + _of(1, "w_ff1")),
    ("swa_fwd_l0", _of(1, *MIXER_W)),
    ("ff1_l0", _of(1, "w_ff2", "w_pe", "w_pg")),
)
REDUCES = (
    (_of(1, *WEIGHTS), "dact_l0", "sb_bwd_l0"),
    (_of(0, *MLP_W), "dh_mlp_l0", "swa_bwd_l0"),
    (_of(0, "w_o", "w_up_a", "w_up_b"), "do_a_l0", "sb_bwd_l0"),
    (_of(0, "w_in"), None, None),
)


def _merge_comms(comms):
    if len(comms) == 1:
        return comms[0]

    def cuts(counts):
        edges = [0]
        for c in counts:
            edges.append(edges[-1] + c)
        return [slice(a, b) for a, b in zip(edges[:-1], edges[1:])]

    s_in = cuts([len(c.inputs) for c in comms])
    s_out = cuts([len(c.out_shapes) for c in comms])
    s_sem = cuts([len(c.sems) for c in comms])

    def start(pos, cin, cout, csem):
        for c, i, o, s in zip(comms, s_in, s_out, s_sem):
            c.start(pos, cin[i], cout[o], csem[s])

    def finish(pos, cin, cout, csem):
        for c, i, o, s in zip(comms, s_in, s_out, s_sem):
            c.finish(pos, cin[i], cout[o], csem[s])

    return _Comm(sum([c.inputs for c in comms], []), sum([c.out_shapes for c in comms], []),
                 sum([c.sems for c in comms], []), start, finish)


class _LayerWeights:
    def __init__(self, full, layer):
        self.full, self.layer, self.cache = full, layer, {}

    def __getitem__(self, name):
        if name not in self.cache:
            if name == "w_qkv":
                self.cache[name] = self.full[(self.layer, "w_in")][:, :QKV_COLS]
            elif name == "w_gate":
                self.cache[name] = self.full[(self.layer, "w_in")][:, QKV_COLS:]
            else:
                self.cache[name] = self.full[(self.layer, name)]
        return self.cache[name]


class _Plan:
    def __init__(self, w_sh, pos):
        self.w_sh = dict(zip(WEIGHTS, w_sh))
        self.pos = pos
        self.full, self.gw, self.parts, self.recv = {}, {}, {}, {}
        self.slabs = {}
        self.hosted = {}
        for i, (host, _) in enumerate(GATHERS):
            if host is not None:
                self.hosted.setdefault(host, []).append(("gather", i))
        for i, (_, sib_host, chip_host) in enumerate(REDUCES):
            assert (sib_host is None) == (chip_host is None)
            if sib_host is not None:
                self.hosted.setdefault(sib_host, []).append(("sibling", i))
                self.hosted.setdefault(chip_host, []).append(("chips", i))

    def _gather(self, i):
        return _gather_comm([self.w_sh[n][layer].astype(WIRE) for layer, n in GATHERS[i][1]])

    def _gathered(self, i, outs):
        for (layer, n), g in zip(GATHERS[i][1], outs):
            self.full[(layer, n)] = _full_weight(n, g)

    def weights(self, layer):
        for i, (host, items) in enumerate(GATHERS):
            if host is None and items[0][0] == layer:
                self._gathered(i, _run_comm(self._gather(i), f"gather_{i}"))
        return _LayerWeights(self.full, layer)

    def grad(self, layer, name, value):
        self.gw[(layer, name)] = value

    def _sibling(self, i):
        self.slabs[i] = [_to_slabs(n, self.gw[(layer, n)], self.w_sh[n].shape[1:]) for layer, n in REDUCES[i][0]]
        return _rs_sibling_comm(self.slabs[i])

    def _sibling_done(self, i, outs):
        parts = _chip_partials(self.pos, self.slabs[i], outs, f"chip_partials_{i}")
        for item, part in zip(REDUCES[i][0], parts):
            self.parts[item] = part

    def _chips(self, i):
        return _rs_chips_comm([self.parts[item] for item in REDUCES[i][0]])

    def _chips_done(self, i, outs):
        for item, r in zip(REDUCES[i][0], outs):
            self.recv[item] = r

    def layer_done(self, layer):
        for i, (items, sib_host, _) in enumerate(REDUCES):
            if sib_host is None and items[0][0] == layer:
                self._sibling_done(i, _run_comm(self._sibling(i), f"reduce_sibling_{i}"))
                self._chips_done(i, _run_comm(self._chips(i), f"reduce_chips_{i}"))

    def comm(self, name):
        if name not in self.hosted:
            return None
        make = {"gather": self._gather, "sibling": self._sibling, "chips": self._chips}
        return _merge_comms([make[kind](i) for kind, i in self.hosted[name]])

    def done(self, name, outs):
        took = {"gather": self._gathered, "sibling": self._sibling_done, "chips": self._chips_done}
        off = 0
        for kind, i in self.hosted[name]:
            n = len(GATHERS[i][1]) if kind == "gather" else len(REDUCES[i][0])
            took[kind](i, outs[off:off + n])
            off += n


def kernel(x, p, w_in, w_up_a, w_up_b, w_o, w_ff1, w_ff2, w_pe, w_pg, g_mix, g_mlp, g_pe, g_final, sinks, rel_bias, loss_target, m_w_in, m_w_up_a, m_w_up_b, m_w_o, m_w_ff1, m_w_ff2, m_w_pe, m_w_pg, m_g_mix, m_g_mlp, m_g_pe, m_g_final, m_sinks, m_rel_bias, v_w_in, v_w_up_a, v_w_up_b, v_w_o, v_w_ff1, v_w_ff2, v_w_pe, v_w_pg, v_g_mix, v_g_mlp, v_g_pe, v_g_final, v_sinks, v_rel_bias):
    w_sh = [w_in, w_up_a, w_up_b, w_o, w_ff1, w_ff2, w_pe, w_pg]
    m_sh = [m_w_in, m_w_up_a, m_w_up_b, m_w_o, m_w_ff1, m_w_ff2, m_w_pe, m_w_pg]
    v_sh = [v_w_in, v_w_up_a, v_w_up_b, v_w_o, v_w_ff1, v_w_ff2, v_w_pe, v_w_pg]
    depth = w_in.shape[0]
    assert depth == 2 and x.shape[-1] * 2 + QKV_COLS == w_in.shape[2] * N_DEV

    px, py, pc = _position()
    plan = _Plan(w_sh, jnp.stack([px, py, pc]).astype(jnp.int32))
    loss_row, grad_x, _, small = _local_step(
        x[0], p[:, 0], loss_target[0], plan.weights, g_mix, g_mlp, g_pe, g_final, sinks, rel_bias, plan=plan)

    grad_w, delta_w, new_m, new_v = [], [], [], []
    for a, name in enumerate(WEIGHTS):
        outs = _adamw_weight([plan.parts[(l, name)] for l in range(depth)],
                             [plan.recv[(l, name)] for l in range(depth)],
                             w_sh[a], m_sh[a], v_sh[a], f"adamw_{name}")
        for lst, o in zip((grad_w, delta_w, new_m, new_v), outs):
            lst.append(o)

    small_w = [g_mix, g_mlp, g_pe, g_final, sinks, rel_bias]
    small_m = [m_g_mix, m_g_mlp, m_g_pe, m_g_final, m_sinks, m_rel_bias]
    small_v = [v_g_mix, v_g_mlp, v_g_pe, v_g_final, v_sinks, v_rel_bias]
    small_shapes = [a.shape for a in small_w] + [(1,)]
    zero = jnp.zeros((1,), F32)
    small_g = _pack_small([small[n] for n in SMALL] + [loss_row[0, :1]])
    small_all = _run_comm(_gather_comm([small_g]), "gather_small")[0]
    packed_s = _adamw_replicated(small_all, _pack_small(small_w + [zero]), _pack_small(small_m + [zero]),
                                 _pack_small(small_v + [zero + 1.0]), "adamw_replicated")
    sg, sd, sm, sv = [_unpack_small(t, small_shapes) for t in packed_s]
    loss = sg[-1][0]

    return (loss, grad_x[None], *grad_w, *sg[:-1], *delta_w, *sd[:-1], *new_m, *sm[:-1], *new_v, *sv[:-1])
```

```python
import functools
import math

import numpy as np
import jax
import jax.numpy as jnp
from jax import lax
from jax.experimental import pallas as pl
from jax.experimental.pallas import tpu as pltpu

F32 = jnp.float32
MXU = jnp.bfloat16
WIRE = jnp.bfloat16

HEAD_DIM = 64
SB_HEADS = 8
SW_HEADS = 8
SW_KV = 2
SW_GROUP = SW_HEADS // SW_KV
BLOCK = 128
N_BUCKETS = 32
MAX_DISTANCE = 128
EPS = 1e-6
SCALE = HEAD_DIM ** -0.5
SB_W = SB_HEADS * HEAD_DIM
SW_W = SW_HEADS * HEAD_DIM
QKV_COLS = 3 * SB_W + SW_W + 2 * SW_KV * HEAD_DIM
N_DEV = 8
LANES = 128
N_PAIR = SB_HEADS // 2
NEG = -1e30

ADAM_LR = 0.001
ADAM_B1 = 0.9
ADAM_B2 = 0.999
ADAM_EPS = 1e-08
ADAM_WD = 0.01
ADAM_STEP = 10

VMEM_LIMIT = 48 * 1024 * 1024
SB_TQ = 256
SB_DEAD = -105.0

WEIGHTS = ("w_in", "w_up_a", "w_up_b", "w_o", "w_ff1", "w_ff2", "w_pe", "w_pg")
COL_SHARDED = ("w_in", "w_up_a", "w_up_b", "w_ff1", "w_pe")
SMALL = ("g_mix", "g_mlp", "g_pe", "g_final", "sinks", "rel_bias")


def _cparams(**kw):
    return pltpu.CompilerParams(vmem_limit_bytes=VMEM_LIMIT, **kw)


def _dot(a, b):
    return jnp.dot(a, b, preferred_element_type=F32)


def _dot_nt(a, b):
    return lax.dot_general(a, b, (((1,), (1,)), ((), ())), preferred_element_type=F32)


def _dot_tn(a, b):
    return lax.dot_general(a, b, (((0,), (0,)), ((), ())), preferred_element_type=F32)


def _tile(n, target, unit=LANES):
    if n <= target:
        return n
    t = (target // unit) * unit
    while t > unit and n % t:
        t -= unit
    assert n % t == 0, (n, target)
    return t


def _sigmoid(x):
    return 1.0 / (1.0 + jnp.exp(-x))


class _Comm:
    def __init__(self, inputs, out_shapes, sems, start, finish):
        self.inputs, self.out_shapes, self.sems = list(inputs), list(out_shapes), list(sems)
        self.start, self.finish = start, finish


def _call(body, *, grid, in_specs, out_specs, out_shape, scratch_shapes=(), args, name, comm=None):
    n_in, n_out, n_scr = len(in_specs), len(out_shape), len(scratch_shapes)
    if comm is None:
        outs = pl.pallas_call(body, grid=grid, in_specs=list(in_specs), out_specs=list(out_specs),
                              out_shape=list(out_shape), scratch_shapes=list(scratch_shapes),
                              compiler_params=_cparams(), name=name)(*args)
        return list(outs), None
    ci, co = len(comm.inputs), len(comm.out_shapes)
    any_spec = pl.BlockSpec(memory_space=pl.ANY)

    def wrapped(*refs):
        ins, cin = refs[:n_in], refs[n_in:n_in + ci]
        o0 = n_in + ci
        outs, cout = refs[o0:o0 + n_out], refs[o0 + n_out:o0 + n_out + co]
        s0 = o0 + n_out + co
        scr, csem = refs[s0:s0 + n_scr], refs[s0 + n_scr:]
        ids = [pl.program_id(d) for d in range(len(grid))]
        first = functools.reduce(jnp.logical_and, [i == 0 for i in ids])
        last = functools.reduce(jnp.logical_and, [i == g - 1 for i, g in zip(ids, grid)])
        pos = (lax.axis_index("x"), lax.axis_index("y"), lax.axis_index("c"))

        @pl.when(first)
        def _():
            comm.start(pos, cin, cout, csem)

        body(*ins, *outs, *scr)

        @pl.when(last)
        def _():
            comm.finish(pos, cin, cout, csem)

    outs = pl.pallas_call(wrapped, grid=grid, in_specs=list(in_specs) + [any_spec] * ci,
                          out_specs=list(out_specs) + [any_spec] * co,
                          out_shape=list(out_shape) + comm.out_shapes,
                          scratch_shapes=list(scratch_shapes) + comm.sems,
                          compiler_params=_cparams(), name=name)(*args, *comm.inputs)
    return list(outs[:n_out]), list(outs[n_out:])


def _mm(a, b, *, ta=False, tb=False, extras=(), epi=None, out_dtypes=(F32,),
        tm=1024, tn=1024, tk=1024, name, comm=None):
    if ta:
        kdim, m = a.shape
    else:
        m, kdim = a.shape
    n = b.shape[0] if tb else b.shape[1]
    assert (b.shape[1] if tb else b.shape[0]) == kdim
    tm, tn, tk = _tile(m, tm), _tile(n, tn), _tile(kdim, tk)
    nk = kdim // tk
    n_ex, n_out = len(extras), len(out_dtypes)

    a_spec = (pl.BlockSpec((tk, tm), lambda i, j, k: (k, i)) if ta
              else pl.BlockSpec((tm, tk), lambda i, j, k: (i, k)))
    b_spec = (pl.BlockSpec((tn, tk), lambda i, j, k: (j, k)) if tb
              else pl.BlockSpec((tk, tn), lambda i, j, k: (k, j)))
    ex_specs = []
    for e in extras:
        assert e.shape == (m, n), (e.shape, m, n)
        ex_specs.append(pl.BlockSpec((tm, tn), lambda i, j, k: (i, j)))
    out_spec = pl.BlockSpec((tm, tn), lambda i, j, k: (i, j))

    def body(a_ref, b_ref, *rest):
        ex_refs = rest[:n_ex]
        out_refs = rest[n_ex:n_ex + n_out]
        acc = rest[-1]
        k = pl.program_id(2)

        @pl.when(k == 0)
        def _():
            acc[...] = jnp.zeros_like(acc)

        av = a_ref[...].astype(MXU)
        bv = b_ref[...].astype(MXU)
        if ta:
            acc[...] += _dot_tn(av, bv)
        elif tb:
            acc[...] += _dot_nt(av, bv)
        else:
            acc[...] += _dot(av, bv)

        @pl.when(k == nk - 1)
        def _():
            res = acc[...]
            if epi is not None:
                res = epi(res, *[e[...] for e in ex_refs])
            if not isinstance(res, tuple):
                res = (res,)
            for o_ref, r in zip(out_refs, res):
                o_ref[...] = r.astype(o_ref.dtype)

    outs, couts = _call(
        body,
        grid=(m // tm, n // tn, nk),
        in_specs=[a_spec, b_spec] + ex_specs,
        out_specs=[out_spec] * n_out,
        out_shape=[jax.ShapeDtypeStruct((m, n), dt) for dt in out_dtypes],
        scratch_shapes=[pltpu.VMEM((tm, tn), F32)],
        args=(a, b, *extras), name=name, comm=comm)
    res = outs[0] if n_out == 1 else tuple(outs)
    return res if comm is None else (res, couts)


def _rms_fwd(x, g, name):
    s, d = x.shape
    tr = _tile(s, 256)

    def body(x_ref, g_ref, h_ref, r_ref):
        xf = x_ref[...]
        r = lax.rsqrt(jnp.mean(xf * xf, axis=-1, keepdims=True) + EPS)
        h_ref[...] = ((xf * r) * g_ref[...]).astype(h_ref.dtype)
        r_ref[...] = r

    return pl.pallas_call(
        body,
        grid=(s // tr,),
        in_specs=[pl.BlockSpec((tr, d), lambda i: (i, 0)), pl.BlockSpec((1, d), lambda i: (0, 0))],
        out_specs=[pl.BlockSpec((tr, d), lambda i: (i, 0)), pl.BlockSpec((tr, 1), lambda i: (i, 0))],
        out_shape=[jax.ShapeDtypeStruct((s, d), MXU), jax.ShapeDtypeStruct((s, 1), F32)],
        compiler_params=_cparams(),
        name=name,
    )(x, g)


def _rms_bwd(x, r, g, dh, dres, name):
    s, d = x.shape
    tr = _tile(s, 256)

    def body(x_ref, r_ref, g_ref, dh_ref, dres_ref, dx_ref, dxb_ref, dg_ref):
        @pl.when(pl.program_id(0) == 0)
        def _():
            dg_ref[...] = jnp.zeros_like(dg_ref)

        rr = r_ref[...]
        xhat = x_ref[...] * rr
        dh_v = dh_ref[...]
        dxhat = dh_v * g_ref[...]
        mean = jnp.mean(dxhat * xhat, axis=-1, keepdims=True)
        dx = dres_ref[...] + rr * (dxhat - xhat * mean)
        dx_ref[...] = dx
        dxb_ref[...] = dx.astype(dxb_ref.dtype)
        dg_ref[...] += jnp.sum(dh_v * xhat, axis=0, keepdims=True)

    row = pl.BlockSpec((tr, d), lambda i: (i, 0))
    vec = pl.BlockSpec((1, d), lambda i: (0, 0))
    return pl.pallas_call(
        body,
        grid=(s // tr,),
        in_specs=[row, pl.BlockSpec((tr, 1), lambda i: (i, 0)), vec, row, row],
        out_specs=[row, row, vec],
        out_shape=[jax.ShapeDtypeStruct((s, d), F32), jax.ShapeDtypeStruct((s, d), MXU),
                   jax.ShapeDtypeStruct((1, d), F32)],
        compiler_params=_cparams(),
        name=name,
    )(x, r, g, dh, dres)


def _loss_head(x, g, target, name):
    s, d = x.shape
    tr = _tile(s, 256)

    def body(x_ref, g_ref, t_ref, loss_ref, dx_ref, dg_ref):
        @pl.when(pl.program_id(0) == 0)
        def _():
            dg_ref[...] = jnp.zeros_like(dg_ref)
            loss_ref[...] = jnp.zeros_like(loss_ref)

        xf = x_ref[...]
        gv = g_ref[...]
        r = lax.rsqrt(jnp.mean(xf * xf, axis=-1, keepdims=True) + EPS)
        xhat = xf * r
        err = xhat * gv - t_ref[...]
        loss_ref[...] += 0.5 * jnp.sum(jnp.mean(err * err, axis=-1, keepdims=True), axis=0, keepdims=True)
        dy = err * (1.0 / d)
        dxhat = dy * gv
        mean = jnp.mean(dxhat * xhat, axis=-1, keepdims=True)
        dx_ref[...] = r * (dxhat - xhat * mean)
        dg_ref[...] += jnp.sum(dy * xhat, axis=0, keepdims=True)

    row = pl.BlockSpec((tr, d), lambda i: (i, 0))
    vec = pl.BlockSpec((1, d), lambda i: (0, 0))
    return pl.pallas_call(
        body,
        grid=(s // tr,),
        in_specs=[row, vec, row],
        out_specs=[pl.BlockSpec((1, LANES), lambda i: (0, 0)), row, vec],
        out_shape=[jax.ShapeDtypeStruct((1, LANES), F32), jax.ShapeDtypeStruct((s, d), F32),
                   jax.ShapeDtypeStruct((1, d), F32)],
        compiler_params=_cparams(),
        name=name,
    )(x, g, target)


def _mix_fwd(oa, ob, wa, wb, gates, name):
    s, kd = oa.shape
    d = wa.shape[1]
    tm, tn = _tile(s, 1024), _tile(d, 512)
    nj = d // tn

    def body(oa_ref, ob_ref, wa_ref, wb_ref, ga_ref, gb_ref, out_ref):
        ya = _dot(oa_ref[...], wa_ref[...])
        yb = _dot(ob_ref[...], wb_ref[...])
        out_ref[...] = (_sigmoid(ga_ref[...]) * ya + _sigmoid(gb_ref[...]) * yb).astype(out_ref.dtype)

    o_spec = pl.BlockSpec((tm, kd), lambda i, j: (i, 0))
    w_spec = pl.BlockSpec((kd, tn), lambda i, j: (0, j))
    return pl.pallas_call(
        body,
        grid=(s // tm, nj),
        in_specs=[o_spec, o_spec, w_spec, w_spec,
                  pl.BlockSpec((tm, tn), lambda i, j: (i, j)),
                  pl.BlockSpec((tm, tn), lambda i, j: (i, j + nj))],
        out_specs=pl.BlockSpec((tm, tn), lambda i, j: (i, j)),
        out_shape=jax.ShapeDtypeStruct((s, d), MXU),
        compiler_params=_cparams(),
        name=name,
    )(oa, ob, wa, wb, gates, gates)


def _mix_bwd(dx, w_o, oa, ob, wa, wb, gates, name):
    s, kd = oa.shape
    d = wa.shape[1]
    tm, tn = _tile(s, 1024), _tile(d, 512)
    nj = d // tn

    def body(dx_ref, wo_ref, oa_ref, ob_ref, wa_ref, wb_ref, ga_ref, gb_ref,
             dya_ref, dyb_ref, dga_ref, dgb_ref):
        dm = _dot_nt(dx_ref[...], wo_ref[...])
        ya = _dot(oa_ref[...], wa_ref[...])
        yb = _dot(ob_ref[...], wb_ref[...])
        sa = _sigmoid(ga_ref[...])
        sb = _sigmoid(gb_ref[...])
        dya_ref[...] = (dm * sa).astype(dya_ref.dtype)
        dyb_ref[...] = (dm * sb).astype(dyb_ref.dtype)
        dga_ref[...] = (dm * ya * sa * (1.0 - sa)).astype(dga_ref.dtype)
        dgb_ref[...] = (dm * yb * sb * (1.0 - sb)).astype(dgb_ref.dtype)

    o_spec = pl.BlockSpec((tm, kd), lambda i, j: (i, 0))
    w_spec = pl.BlockSpec((kd, tn), lambda i, j: (0, j))
    t_spec = pl.BlockSpec((tm, tn), lambda i, j: (i, j))
    return pl.pallas_call(
        body,
        grid=(s // tm, nj),
        in_specs=[pl.BlockSpec((tm, d), lambda i, j: (i, 0)),
                  pl.BlockSpec((tn, d), lambda i, j: (j, 0)),
                  o_spec, o_spec, w_spec, w_spec, t_spec,
                  pl.BlockSpec((tm, tn), lambda i, j: (i, j + nj))],
        out_specs=[t_spec] * 4,
        out_shape=[jax.ShapeDtypeStruct((s, d), MXU)] * 4,
        compiler_params=_cparams(),
        name=name,
    )(dx, w_o, oa, ob, wa, wb, gates, gates)


def _ple(p, w_pe, h, w_pg, other, *, backward, name):
    s, kp = p.shape
    d = w_pe.shape[1]
    tm, tn = _tile(s, 1024), _tile(d, 512)

    def body(p_ref, wpe_ref, h_ref, wpg_ref, other_ref, *out_refs):
        pe = _dot(p_ref[...].astype(MXU), wpe_ref[...])
        gt = _dot(h_ref[...], wpg_ref[...])
        sg = _sigmoid(gt)
        if backward:
            dout = other_ref[...]
            out_refs[0][...] = (dout * sg).astype(out_refs[0].dtype)
            out_refs[1][...] = (dout * pe * sg * (1.0 - sg)).astype(out_refs[1].dtype)
        else:
            out_refs[0][...] = other_ref[...] + pe * sg

    t_spec = pl.BlockSpec((tm, tn), lambda i, j: (i, j))
    if backward:
        out_specs, out_shape = [t_spec, t_spec], [jax.ShapeDtypeStruct((s, d), MXU)] * 2
    else:
        out_specs, out_shape = [t_spec], [jax.ShapeDtypeStruct((s, d), F32)]
    outs = pl.pallas_call(
        body,
        grid=(s // tm, d // tn),
        in_specs=[pl.BlockSpec((tm, kp), lambda i, j: (i, 0)),
                  pl.BlockSpec((kp, tn), lambda i, j: (0, j)),
                  pl.BlockSpec((tm, d), lambda i, j: (i, 0)),
                  pl.BlockSpec((d, tn), lambda i, j: (0, j)),
                  t_spec],
        out_specs=out_specs,
        out_shape=out_shape,
        compiler_params=_cparams(),
        name=name,
    )(p, w_pe, h, w_pg, other)
    return tuple(outs) if backward else outs[0]


def _split_dot(x, tri):
    hi = x.astype(jnp.bfloat16)
    lo = (x - hi.astype(F32)).astype(jnp.bfloat16)
    return _dot(hi, tri) + _dot(lo, tri)


def _log_sigmoids(z):
    t = jnp.log1p(jnp.exp(-jnp.abs(z)))
    return jnp.minimum(z, 0.0) - t, jnp.minimum(-z, 0.0) - t


def _head_lanes(hh):
    lane = lax.broadcasted_iota(jnp.int32, (1, LANES), 1)
    return jnp.logical_and(lane >= hh * HEAD_DIM, lane < (hh + 1) * HEAD_DIM)


def _sb_fwd(qkv, name, comm=None):
    s = qkv.shape[0]
    tq = _tile(s, SB_TQ)

    def body(q_ref, k_ref, v_ref, o_ref):
        i = pl.program_id(1)
        qf = q_ref[...].astype(F32) * SCALE
        row = lax.broadcasted_iota(jnp.int32, (tq, tq), 0)
        col = lax.broadcasted_iota(jnp.int32, (tq, tq), 1)
        causal = col < row
        tri = jnp.where(row > col, 1.0, 0.0).astype(jnp.bfloat16)

        qms = [jnp.where(_head_lanes(hh), qf, 0.0).astype(MXU) for hh in range(2)]

        def block(kb, cs, accs, masked):
            rows = pl.ds(pl.multiple_of(kb * tq, tq), tq)
            ks, vs = k_ref[rows, :], v_ref[rows, :]
            new_c, new_acc = [], []
            for hh in range(2):
                lb, lm = _log_sigmoids(_dot_nt(qms[hh], ks))
                if masked:
                    lm = jnp.where(causal, lm, 0.0)
                a = jnp.exp(lb + _split_dot(lm, tri) + cs[hh])
                if masked:
                    a = jnp.where(causal, a, 0.0)
                new_acc.append(accs[hh] + _dot(a.astype(MXU), vs))
                new_c.append(cs[hh] + jnp.sum(lm, axis=1, keepdims=True))
            return tuple(new_c), tuple(new_acc)

        def top(cs):
            return jnp.maximum(jnp.max(cs[0]), jnp.max(cs[1]))

        zc, za = jnp.zeros((tq, 1), F32), jnp.zeros((tq, LANES), F32)
        cs, accs = block(i, (zc, zc), (za, za), True)

        def live(st):
            return jnp.logical_and(st[0] >= 0, st[1] > SB_DEAD)

        def walk(st):
            cs, accs = block(st[0], st[2], st[3], False)
            return st[0] - 1, top(cs), cs, accs

        accs = lax.while_loop(live, walk, (i - 1, top(cs), cs, accs))[3]
        o_ref[...] = jnp.where(_head_lanes(0), accs[0], accs[1]).astype(o_ref.dtype)

    outs, couts = _call(
        body,
        grid=(N_PAIR, s // tq),
        in_specs=[pl.BlockSpec((tq, LANES), lambda p, i: (i, p)),
                  pl.BlockSpec((s, LANES), lambda p, i: (0, N_PAIR + p)),
                  pl.BlockSpec((s, LANES), lambda p, i: (0, 2 * N_PAIR + p))],
        out_specs=[pl.BlockSpec((tq, LANES), lambda p, i: (i, p))],
        out_shape=[jax.ShapeDtypeStruct((s, SB_W), MXU)],
        args=(qkv, qkv, qkv), name=name, comm=comm)
    return outs[0] if comm is None else (outs[0], couts)


def _sb_bwd(qkv, do, name, comm=None):
    s = qkv.shape[0]
    tq = _tile(s, SB_TQ)
    nq = s // tq

    def body(q_ref, k_ref, v_ref, do_ref, dq_ref, dk_ref, dv_ref, dk_acc, dv_acc, carries):
        i = pl.program_id(1)

        @pl.when(i == 0)
        def _():
            dk_acc[...] = jnp.zeros_like(dk_acc)
            dv_acc[...] = jnp.zeros_like(dv_acc)

        qf = q_ref[...].astype(F32) * SCALE
        dof = do_ref[...]
        row = lax.broadcasted_iota(jnp.int32, (tq, tq), 0)
        col = lax.broadcasted_iota(jnp.int32, (tq, tq), 1)
        causal = col < row
        tri_rev = jnp.where(row > col, 1.0, 0.0).astype(jnp.bfloat16)
        tri_excl = jnp.where(row < col, 1.0, 0.0).astype(jnp.bfloat16)

        qms = [jnp.where(_head_lanes(hh), qf, 0.0).astype(MXU) for hh in range(2)]
        doms = [jnp.where(_head_lanes(hh), dof, jnp.zeros_like(dof)) for hh in range(2)]

        def row_sums(kb, masked):
            rows = pl.ds(pl.multiple_of(kb * tq, tq), tq)
            ks = k_ref[rows, :]
            out = []
            for hh in range(2):
                _, lm = _log_sigmoids(_dot_nt(qms[hh], ks))
                if masked:
                    lm = jnp.where(causal, lm, 0.0)
                out.append(jnp.sum(lm, axis=1, keepdims=True))
            return out

        def top(cs):
            return jnp.maximum(jnp.max(cs[0]), jnp.max(cs[1]))

        def live(st):
            return jnp.logical_and(st[0] >= 0, st[1] > SB_DEAD)

        def record(st):
            kb, cs = st[0], st[2]
            sums = row_sums(kb, False)
            for hh in range(2):
                carries[hh, kb] = cs[hh]
            cs = tuple(cs[hh] + sums[hh] for hh in range(2))
            return kb - 1, top(cs), cs

        c_diag = tuple(row_sums(i, True))
        first = lax.while_loop(live, record, (i - 1, top(c_diag), c_diag))[0] + 1

        def block(kb, cs, gpres, dqs, masked):
            rows = pl.ds(pl.multiple_of(kb * tq, tq), tq)
            ks, vs = k_ref[rows, :], v_ref[rows, :]
            new_g, new_dq = [], []
            dk_add, dv_add = None, None
            for hh in range(2):
                lb, lm = _log_sigmoids(_dot_nt(qms[hh], ks))
                if masked:
                    lm = jnp.where(causal, lm, 0.0)
                a = jnp.exp(lb + _split_dot(lm, tri_rev) + cs[hh])
                if masked:
                    a = jnp.where(causal, a, 0.0)
                g = a * _dot_nt(doms[hh], vs)
                gsum = gpres[hh] + _split_dot(g, tri_excl)
                dz = g - (g + gsum) * jnp.exp(lb)
                if masked:
                    dz = jnp.where(causal, dz, 0.0)
                dzb = dz.astype(MXU)
                new_dq.append(dqs[hh] + _dot(dzb, ks))
                dk_h = _dot_tn(dzb, qms[hh])
                dv_h = _dot_tn(a.astype(MXU), doms[hh])
                dk_add = dk_h if dk_add is None else dk_add + dk_h
                dv_add = dv_h if dv_add is None else dv_add + dv_h
                new_g.append(gpres[hh] + jnp.sum(g, axis=1, keepdims=True))
            dk_acc[rows, :] += dk_add
            dv_acc[rows, :] += dv_add
            return tuple(new_g), tuple(new_dq)

        zc, za = jnp.zeros((tq, 1), F32), jnp.zeros((tq, LANES), F32)
        gpres, dqs = lax.fori_loop(
            first, i, lambda kb, cr: block(kb, (carries[0, kb], carries[1, kb]), cr[0], cr[1], False),
            ((zc, zc), (za, za)))
        dqs = block(i, (zc, zc), gpres, dqs, True)[1]
        dq_ref[...] = (jnp.where(_head_lanes(0), dqs[0], dqs[1]) * SCALE).astype(dq_ref.dtype)

        @pl.when(i == nq - 1)
        def _():
            dk_ref[...] = dk_acc[...].astype(dk_ref.dtype)
            dv_ref[...] = dv_acc[...].astype(dv_ref.dtype)

    blk = pl.BlockSpec((tq, LANES), lambda p, i: (i, p))
    full = pl.BlockSpec((s, LANES), lambda p, i: (0, p))
    outs, couts = _call(
        body,
        grid=(N_PAIR, nq),
        in_specs=[blk,
                  pl.BlockSpec((s, LANES), lambda p, i: (0, N_PAIR + p)),
                  pl.BlockSpec((s, LANES), lambda p, i: (0, 2 * N_PAIR + p)),
                  blk],
        out_specs=[blk, full, full],
        out_shape=[jax.ShapeDtypeStruct((s, SB_W), MXU)] * 3,
        scratch_shapes=[pltpu.VMEM((s, LANES), F32), pltpu.VMEM((s, LANES), F32),
                        pltpu.VMEM((2, nq, tq, 1), F32)],
        args=(qkv, qkv, qkv, do), name=name, comm=comm)
    return tuple(outs) if comm is None else (tuple(outs), couts)


def _bucket_table():
    i = np.arange(BLOCK)[:, None]
    j = np.arange(2 * BLOCK)[None, :]
    d = np.maximum(BLOCK + i - j, 0)
    max_exact = N_BUCKETS // 2
    df = np.maximum(d, 1).astype(np.float32)
    large = max_exact + (np.log(df / max_exact) / math.log(MAX_DISTANCE / max_exact)
                         * (N_BUCKETS - max_exact)).astype(np.int32)
    large = np.minimum(large, N_BUCKETS - 1)
    return np.where(d < max_exact, d, large).astype(np.int32)


def _build_bias(rel_bias, buckets, name):
    def body(rb_ref, bk_ref, out_ref):
        h = pl.program_id(0)
        bk = bk_ref[...]
        acc = jnp.zeros(bk.shape, F32)
        for b in range(N_BUCKETS):
            acc = jnp.where(bk == b, rb_ref[b, h], acc)
        out_ref[...] = acc

    return pl.pallas_call(
        body,
        grid=(SW_HEADS,),
        in_specs=[pl.BlockSpec(memory_space=pltpu.SMEM),
                  pl.BlockSpec((BLOCK, 2 * BLOCK), lambda h: (0, 0))],
        out_specs=pl.BlockSpec((None, BLOCK, 2 * BLOCK), lambda h: (h, 0, 0)),
        out_shape=jax.ShapeDtypeStruct((SW_HEADS, BLOCK, 2 * BLOCK), F32),
        name=name,
    )(rel_bias, buckets)


def _bias_grad(dbias_layers, buckets, name):
    n_l = len(dbias_layers)

    def body(*refs):
        bk = refs[n_l][...]
        out_ref = refs[n_l + 1]
        db = refs[0][...]
        for r in refs[1:n_l]:
            db = db + r[...]
        lane = lax.broadcasted_iota(jnp.int32, (1, LANES), 1)
        acc = jnp.zeros((1, LANES), F32)
        for b in range(N_BUCKETS):
            part = jnp.sum(jnp.where(bk == b, db, 0.0), axis=1, keepdims=True)
            tot = jnp.sum(part, axis=0, keepdims=True)
            acc = jnp.where(lane == b, tot, acc)
        out_ref[...] = acc

    hspec = pl.BlockSpec((None, BLOCK, 2 * BLOCK), lambda h: (h, 0, 0))
    return pl.pallas_call(
        body,
        grid=(SW_HEADS,),
        in_specs=[hspec] * n_l + [pl.BlockSpec((BLOCK, 2 * BLOCK), lambda h: (0, 0))],
        out_specs=pl.BlockSpec((None, 1, LANES), lambda h: (h, 0, 0)),
        out_shape=jax.ShapeDtypeStruct((SW_HEADS, 1, LANES), F32),
        name=name,
    )(*dbias_layers, buckets)


GROUP_ROWS = SW_GROUP * BLOCK


def _group_lanes(g):
    lane = lax.broadcasted_iota(jnp.int32, (1, LANES), 1)
    gvec = jnp.zeros((1, LANES), jnp.int32) + g
    return jnp.where(lane >= HEAD_DIM, 1, 0) == gvec, gvec


def _stack_heads(x, g):
    kv_lanes, gvec = _group_lanes(g)
    parts = []
    for j in range(SW_GROUP):
        half = x[:, (j // 2) * LANES:(j // 2 + 1) * LANES]
        moved = jnp.where(gvec == j % 2, half, pltpu.roll(half, HEAD_DIM, 1))
        parts.append(jnp.where(kv_lanes, moved, 0.0))
    return jnp.concatenate(parts, axis=0)


def _unstack_heads(y, g):
    _, gvec = _group_lanes(g)
    heads = []
    for j in range(SW_GROUP):
        yj = y[j * BLOCK:(j + 1) * BLOCK]
        heads.append(jnp.where(gvec == j % 2, yj, pltpu.roll(yj, HEAD_DIM, 1)))
    pairs = [jnp.where(_head_lanes(0), heads[2 * p], heads[2 * p + 1]) for p in range(SW_GROUP // 2)]
    return jnp.concatenate(pairs, axis=1)


def _per_head_col(values):
    return jnp.concatenate([jnp.zeros((BLOCK, 1), F32) + v for v in values], axis=0)


def _swa_scores(qs, kp, kc, bias_ref, n):
    row = jnp.bitwise_and(lax.broadcasted_iota(jnp.int32, (GROUP_ROWS, BLOCK), 0), BLOCK - 1)
    col = lax.broadcasted_iota(jnp.int32, (GROUP_ROWS, BLOCK), 1)
    bias = bias_ref[...].reshape(GROUP_ROWS, 2 * BLOCK)
    s1 = _dot_nt(qs, kp) + bias[:, :BLOCK]
    s2 = _dot_nt(qs, kc) + bias[:, BLOCK:]
    no_prev = jnp.where(n > 0, 0, BLOCK)
    s1 = jnp.where(col > row + no_prev, s1, NEG)
    s2 = jnp.where(col <= row, s2, NEG)
    return s1, s2


def _swa_specs(s):
    q_blk = 3 * SB_W // (2 * LANES)
    k_blk = (3 * SB_W + SW_W) // LANES
    return (pl.BlockSpec((s, 2 * LANES), lambda g: (0, q_blk + g)),
            pl.BlockSpec((s, LANES), lambda g: (0, k_blk)),
            pl.BlockSpec((s, LANES), lambda g: (0, k_blk + 1)))


def _swa_fwd(qkv, bias, sinks, name, comm=None):
    s = qkv.shape[0]
    nb = s // BLOCK

    def body(sink_ref, q_ref, k_ref, v_ref, bias_ref, o_ref, lse_ref):
        g = pl.program_id(0)
        sink = _per_head_col([sink_ref[SW_GROUP * g + j] for j in range(SW_GROUP)])
        lane = lax.broadcasted_iota(jnp.int32, (1, LANES), 1)

        def step(n, carry):
            r0 = pl.multiple_of(n * BLOCK, BLOCK)
            p0 = pl.multiple_of(jnp.maximum(n - 1, 0) * BLOCK, BLOCK)
            cur, prev = pl.ds(r0, BLOCK), pl.ds(p0, BLOCK)
            qs = _stack_heads(q_ref[cur, :].astype(F32) * SCALE, g).astype(MXU)
            s1, s2 = _swa_scores(qs, k_ref[prev, :], k_ref[cur, :], bias_ref, n)
            m = jnp.maximum(jnp.maximum(jnp.max(s1, axis=1, keepdims=True),
                                        jnp.max(s2, axis=1, keepdims=True)), sink)
            e1 = jnp.exp(s1 - m)
            e2 = jnp.exp(s2 - m)
            den = jnp.sum(e1, axis=1, keepdims=True) + jnp.sum(e2, axis=1, keepdims=True) + jnp.exp(sink - m)
            o = _dot((e1 / den).astype(MXU), v_ref[prev, :]) + _dot((e2 / den).astype(MXU), v_ref[cur, :])
            o_ref[cur, :] = _unstack_heads(o, g).astype(o_ref.dtype)
            lse = m + jnp.log(den)
            lse_row = jnp.zeros((BLOCK, LANES), F32)
            for j in range(SW_GROUP):
                lse_row = jnp.where(lane == j, lse[j * BLOCK:(j + 1) * BLOCK], lse_row)
            lse_ref[cur, :] = lse_row
            return carry

        lax.fori_loop(0, nb, step, 0)

    outs, couts = _call(
        body,
        grid=(SW_KV,),
        in_specs=[pl.BlockSpec(memory_space=pltpu.SMEM), *_swa_specs(s),
                  pl.BlockSpec((SW_GROUP, BLOCK, 2 * BLOCK), lambda g: (g, 0, 0))],
        out_specs=[pl.BlockSpec((s, 2 * LANES), lambda g: (0, g)),
                   pl.BlockSpec((None, s, LANES), lambda g: (g, 0, 0))],
        out_shape=[jax.ShapeDtypeStruct((s, SW_W), MXU), jax.ShapeDtypeStruct((SW_KV, s, LANES), F32)],
        args=(sinks, qkv, qkv, qkv, bias), name=name, comm=comm)
    return tuple(outs) if comm is None else (tuple(outs), couts)


def _swa_bwd(qkv, bias, sinks, do, lse, name, comm=None):
    s = qkv.shape[0]
    nb = s // BLOCK

    def body(sink_ref, q_ref, k_ref, v_ref, bias_ref, do_ref, lse_ref,
             dq_ref, dk_ref, dv_ref, dbias_ref, dsink_ref, dk_acc, dv_acc):
        g = pl.program_id(0)
        sink = _per_head_col([sink_ref[SW_GROUP * g + j] for j in range(SW_GROUP)])
        lane = lax.broadcasted_iota(jnp.int32, (1, LANES), 1)

        @pl.when(g == 0)
        def _():
            dk_acc[...] = jnp.zeros_like(dk_acc)
            dv_acc[...] = jnp.zeros_like(dv_acc)

        dbias_ref[...] = jnp.zeros_like(dbias_ref)

        def step(n, dsink_rows):
            r0 = pl.multiple_of(n * BLOCK, BLOCK)
            p0 = pl.multiple_of(jnp.maximum(n - 1, 0) * BLOCK, BLOCK)
            cur, prev = pl.ds(r0, BLOCK), pl.ds(p0, BLOCK)
            qs = _stack_heads(q_ref[cur, :].astype(F32) * SCALE, g).astype(MXU)
            dos = _stack_heads(do_ref[cur, :].astype(F32), g).astype(MXU)
            kp, kc, vp, vc = k_ref[prev, :], k_ref[cur, :], v_ref[prev, :], v_ref[cur, :]
            lse_row = lse_ref[cur, :]
            lse = jnp.concatenate([jnp.sum(jnp.where(lane == j, lse_row, 0.0), axis=1, keepdims=True)
                                   for j in range(SW_GROUP)], axis=0)
            s1, s2 = _swa_scores(qs, kp, kc, bias_ref, n)
            pr1 = jnp.exp(s1 - lse)
            pr2 = jnp.exp(s2 - lse)
            dpr1 = _dot_nt(dos, vp)
            dpr2 = _dot_nt(dos, vc)
            delta = jnp.sum(pr1 * dpr1, axis=1, keepdims=True) + jnp.sum(pr2 * dpr2, axis=1, keepdims=True)
            ds1 = pr1 * (dpr1 - delta)
            ds2 = pr2 * (dpr2 - delta)
            dbias_ref[:, :, :BLOCK] += ds1.reshape(SW_GROUP, BLOCK, BLOCK)
            dbias_ref[:, :, BLOCK:] += ds2.reshape(SW_GROUP, BLOCK, BLOCK)
            ds1b, ds2b = ds1.astype(MXU), ds2.astype(MXU)
            dq = _dot(ds1b, kp) + _dot(ds2b, kc)
            dq_ref[cur, :] = (_unstack_heads(dq, g) * SCALE).astype(dq_ref.dtype)
            dk_acc[prev, :] += _dot_tn(ds1b, qs)
            dk_acc[cur, :] += _dot_tn(ds2b, qs)
            dv_acc[prev, :] += _dot_tn(pr1.astype(MXU), dos)
            dv_acc[cur, :] += _dot_tn(pr2.astype(MXU), dos)
            return dsink_rows - jnp.exp(sink - lse) * delta

        rows = lax.fori_loop(0, nb, step, jnp.zeros((GROUP_ROWS, 1), F32))
        for j in range(SW_GROUP):
            dsink_ref[j] = jnp.broadcast_to(jnp.sum(rows[j * BLOCK:(j + 1) * BLOCK], axis=0, keepdims=True),
                                            (1, LANES))

        @pl.when(g == SW_KV - 1)
        def _():
            dk_ref[...] = dk_acc[...].astype(dk_ref.dtype)
            dv_ref[...] = dv_acc[...].astype(dv_ref.dtype)

    grp = pl.BlockSpec((s, 2 * LANES), lambda g: (0, g))
    kv_out = pl.BlockSpec((s, LANES), lambda g: (0, 0))
    bspec = pl.BlockSpec((SW_GROUP, BLOCK, 2 * BLOCK), lambda g: (g, 0, 0))
    outs, couts = _call(
        body,
        grid=(SW_KV,),
        in_specs=[pl.BlockSpec(memory_space=pltpu.SMEM), *_swa_specs(s), bspec, grp,
                  pl.BlockSpec((None, s, LANES), lambda g: (g, 0, 0))],
        out_specs=[grp, kv_out, kv_out, bspec, pl.BlockSpec((SW_GROUP, 1, LANES), lambda g: (g, 0, 0))],
        out_shape=[jax.ShapeDtypeStruct((s, SW_W), MXU),
                   jax.ShapeDtypeStruct((s, LANES), MXU),
                   jax.ShapeDtypeStruct((s, LANES), MXU),
                   jax.ShapeDtypeStruct((SW_HEADS, BLOCK, 2 * BLOCK), F32),
                   jax.ShapeDtypeStruct((SW_HEADS, 1, LANES), F32)],
        scratch_shapes=[pltpu.VMEM((s, LANES), F32), pltpu.VMEM((s, LANES), F32)],
        args=(sinks, qkv, qkv, qkv, bias, do, lse), name=name, comm=comm)
    return tuple(outs) if comm is None else (tuple(outs), couts)


class _NoPlan:
    def comm(self, name):
        return None

    def done(self, name, outs):
        pass

    def grad(self, layer, name, value):
        pass

    def layer_done(self, layer):
        pass


def _run(plan, fn, *args, name, **kw):
    comm = plan.comm(name)
    if comm is None:
        return fn(*args, name=name, **kw)
    res, outs = fn(*args, name=name, comm=comm, **kw)
    plan.done(name, outs)
    return res


def _layer_fwd(x, p, w, g_mix, g_mlp, g_pe, sinks, bias, tag, plan):
    h1, r1 = _rms_fwd(x, g_mix, f"rms_mix_{tag}")
    qkv = _run(plan, _mm, h1, w["w_qkv"], out_dtypes=(MXU,), name=f"proj_qkv_{tag}")
    gates = _run(plan, _mm, h1, w["w_gate"], name=f"proj_gate_{tag}")
    oa = _run(plan, _sb_fwd, qkv, name=f"sb_fwd_{tag}")
    ob, lse = _run(plan, _swa_fwd, qkv, bias, sinks, name=f"swa_fwd_{tag}")
    merged = _mix_fwd(oa, ob, w["w_up_a"], w["w_up_b"], gates, f"mix_fwd_{tag}")
    x1 = _run(plan, _mm, merged, w["w_o"], extras=(x,), epi=lambda acc, res: res + acc, name=f"out_proj_{tag}")
    h2, r2 = _rms_fwd(x1, g_mlp, f"rms_mlp_{tag}")
    u, act = _run(plan, _mm, h2, w["w_ff1"], epi=lambda acc: (acc, jnp.square(jnp.maximum(acc, 0.0))),
                  out_dtypes=(F32, MXU), name=f"ff1_{tag}")
    x2 = _run(plan, _mm, act, w["w_ff2"], extras=(x1,), epi=lambda acc, res: res + acc, name=f"ff2_{tag}")
    h3, r3 = _rms_fwd(x2, g_pe, f"rms_pe_{tag}")
    x3 = _ple(p, w["w_pe"], h3, w["w_pg"], x2, backward=False, name=f"ple_fwd_{tag}")
    saved = dict(x=x, h1=h1, r1=r1, gates=gates, qkv=qkv, lse=lse, oa=oa, ob=ob, merged=merged,
                 x1=x1, h2=h2, r2=r2, u=u, act=act, x2=x2, h3=h3, r3=r3)
    return x3, saved


def _layer_bwd(dx3, sv, p, w, g_mix, g_mlp, g_pe, sinks, bias, layer, plan):
    tag = f"l{layer}"
    gw = {}
    wire = (WIRE,)

    def dw(name, a, b):
        gw[name] = _run(plan, _mm, a, b, ta=True, out_dtypes=wire, name=f"d{name}_{tag}")
        plan.grad(layer, name, gw[name])

    dpe, dgt = _ple(p, w["w_pe"], sv["h3"], w["w_pg"], dx3, backward=True, name=f"ple_bwd_{tag}")
    dw("w_pe", p, dpe)
    dw("w_pg", sv["h3"], dgt)
    dh3 = _run(plan, _mm, dgt, w["w_pg"], tb=True, name=f"dh_pe_{tag}")
    dx2, dx2b, dg_pe = _rms_bwd(sv["x2"], sv["r3"], g_pe, dh3, dx3, f"rms_pe_bwd_{tag}")
    dw("w_ff2", sv["act"], dx2b)
    du = _run(plan, _mm, dx2b, w["w_ff2"], tb=True, extras=(sv["u"],),
              epi=lambda acc, u: acc * (2.0 * jnp.maximum(u, 0.0)), out_dtypes=(MXU,), name=f"dact_{tag}")
    dw("w_ff1", sv["h2"], du)
    dh2 = _run(plan, _mm, du, w["w_ff1"], tb=True, name=f"dh_mlp_{tag}")
    dx1, dx1b, dg_mlp = _rms_bwd(sv["x1"], sv["r2"], g_mlp, dh2, dx2, f"rms_mlp_bwd_{tag}")
    dw("w_o", sv["merged"], dx1b)
    dya, dyb, dga, dgb = _mix_bwd(dx1b, w["w_o"], sv["oa"], sv["ob"], w["w_up_a"], w["w_up_b"],
                                  sv["gates"], f"mix_bwd_{tag}")
    dw("w_up_a", sv["oa"], dya)
    dw("w_up_b", sv["ob"], dyb)
    doa = _run(plan, _mm, dya, w["w_up_a"], tb=True, out_dtypes=(MXU,), name=f"do_a_{tag}")
    dob = _run(plan, _mm, dyb, w["w_up_b"], tb=True, out_dtypes=(MXU,), name=f"do_b_{tag}")
    dqb, dkb, dvb, dbias, dsink = _run(plan, _swa_bwd, sv["qkv"], bias, sinks, dob, sv["lse"],
                                       name=f"swa_bwd_{tag}")
    dqa, dka, dva = _run(plan, _sb_bwd, sv["qkv"], doa, name=f"sb_bwd_{tag}")
    dqkv = jnp.concatenate([dqa, dka, dva, dqb, dkb, dvb], axis=1)
    gw_qkv = _mm(sv["h1"], dqkv, ta=True, out_dtypes=wire, name=f"dw_qkv_{tag}")
    gw_ga = _mm(sv["h1"], dga, ta=True, out_dtypes=wire, name=f"dw_ga_{tag}")
    gw_gb = _mm(sv["h1"], dgb, ta=True, out_dtypes=wire, name=f"dw_gb_{tag}")
    gw["w_in"] = jnp.concatenate([gw_qkv, gw_ga, gw_gb], axis=1)
    plan.grad(layer, "w_in", gw["w_in"])
    d = dga.shape[1]
    add = lambda acc, res: res + acc
    dh1 = _mm(dqkv, w["w_qkv"], tb=True, tk=768, name=f"dh_qkv_{tag}")
    dh1 = _mm(dga, w["w_gate"][:, :d], tb=True, extras=(dh1,), epi=add, name=f"dh_ga_{tag}")
    dh1 = _mm(dgb, w["w_gate"][:, d:], tb=True, extras=(dh1,), epi=add, name=f"dh_gb_{tag}")
    dx, _, dg_mix = _rms_bwd(sv["x"], sv["r1"], g_mix, dh1, dx1, f"rms_mix_bwd_{tag}")
    small = dict(g_mix=dg_mix, g_mlp=dg_mlp, g_pe=dg_pe, sinks=dsink[:, 0, 0], dbias=dbias)
    return dx, gw, small


def _local_step(x, p, target, weights, g_mix, g_mlp, g_pe, g_final, sinks, rel_bias, plan=None):
    plan = _NoPlan() if plan is None else plan
    depth = g_mix.shape[0]
    buckets = jnp.asarray(_bucket_table())
    bias = _build_bias(rel_bias, buckets, "build_bias")
    saved, wfull = [], []
    h = x
    for l in range(depth):
        wfull.append(weights(l))
        h, sv = _layer_fwd(h, p[l], wfull[l], g_mix[l:l + 1], g_mlp[l:l + 1], g_pe[l:l + 1],
                           sinks[l], bias, f"l{l}", plan)
        saved.append(sv)
    loss_row, dx, dg_final = _loss_head(h, g_final[None, :], target, "loss_head")
    gws = [None] * depth
    smalls = [None] * depth
    for l in reversed(range(depth)):
        dx, gws[l], smalls[l] = _layer_bwd(dx, saved[l], p[l], wfull[l], g_mix[l:l + 1], g_mlp[l:l + 1],
                                           g_pe[l:l + 1], sinks[l], bias, l, plan)
        plan.layer_done(l)
    drel = _bias_grad([sm["dbias"] for sm in smalls], buckets, "bias_grad")[:, 0, :N_BUCKETS].T
    small = dict(
        g_mix=jnp.concatenate([sm["g_mix"] for sm in smalls], axis=0),
        g_mlp=jnp.concatenate([sm["g_mlp"] for sm in smalls], axis=0),
        g_pe=jnp.concatenate([sm["g_pe"] for sm in smalls], axis=0),
        g_final=dg_final[0],
        sinks=jnp.stack([sm["sinks"] for sm in smalls], axis=0),
        rel_bias=drel,
    )
    return loss_row, dx, gws, small


MESH_ID = pl.DeviceIdType.MESH
ANY = pl.BlockSpec(memory_space=pl.ANY)


def _position():
    return lax.axis_index("x"), lax.axis_index("y"), lax.axis_index("c")


def _run_comm(comm, name):
    ci, co = len(comm.inputs), len(comm.out_shapes)

    def body(*refs):
        cin, cout, csem = refs[:ci], refs[ci:ci + co], refs[ci + co:]
        pos = _position()
        comm.start(pos, cin, cout, csem)
        comm.finish(pos, cin, cout, csem)

    return pl.pallas_call(body, out_shape=comm.out_shapes, in_specs=[ANY] * ci, out_specs=[ANY] * co,
                          scratch_shapes=comm.sems, name=name)(*comm.inputs)


def _gather_comm(shards):
    n = len(shards)

    def copies(pos, x_refs, out_refs, sems):
        send_sems, recv_sems, local_sems = sems
        x, y, c = pos
        me, sibling = (x, y, c), (x, y, 1 - c)
        chips = [(1 - x, y), (x, 1 - y), (1 - x, 1 - y)]

        def slot(a, px, py, pc):
            return out_refs[a].at[4 * px + 2 * py + pc]

        def copy(a, k, block, to, src=None):
            return pltpu.make_async_remote_copy(
                src_ref=slot(a, *block) if src is None else src, dst_ref=slot(a, *block),
                send_sem=send_sems.at[a, k], recv_sem=recv_sems.at[a, k],
                device_id=to, device_id_type=MESH_ID)

        mine = [pltpu.make_async_copy(x_refs[a], slot(a, *me), local_sems.at[a]) for a in range(n)]
        first = []
        for a in range(n):
            first.append(copy(a, 0, me, sibling, src=x_refs[a]))
            first += [copy(a, 1 + j, me, (*chip, c), src=x_refs[a]) for j, chip in enumerate(chips)]
        return me, sibling, chips, copy, mine, first

    def start(pos, x_refs, out_refs, sems):
        _, _, _, _, mine, first = copies(pos, x_refs, out_refs, sems)
        for cp in mine + first:
            cp.start()

    def finish(pos, x_refs, out_refs, sems):
        me, sibling, chips, copy, mine, first = copies(pos, x_refs, out_refs, sems)
        c = pos[2]
        passed = []
        for j, chip in enumerate(chips):
            for a in range(n):
                copy(a, 1 + j, (*chip, c), me).wait_recv()
                fwd = copy(a, 4 + j, (*chip, c), sibling)
                fwd.start()
                passed.append(fwd)
        for a in range(n):
            copy(a, 0, sibling, me).wait_recv()
            for j, chip in enumerate(chips):
                copy(a, 4 + j, (*chip, 1 - c), me).wait_recv()
        for cp in first + passed:
            cp.wait_send()
        for cp in mine:
            cp.wait()

    return _Comm(shards, [jax.ShapeDtypeStruct((N_DEV,) + s.shape, s.dtype) for s in shards],
                 [pltpu.SemaphoreType.DMA((n, 7)), pltpu.SemaphoreType.DMA((n, 7)),
                  pltpu.SemaphoreType.DMA((n,))], start, finish)


def _exchange_comm(arrays, n_slots, route):
    n = len(arrays)

    def copies(pos, in_refs, out_refs, sems):
        send_sems, recv_sems = sems
        out = []
        for a in range(n):
            for j in range(n_slots):
                src_slot, peer = route(pos, j)
                out.append(pltpu.make_async_remote_copy(
                    src_ref=in_refs[a].at[src_slot], dst_ref=out_refs[a].at[j],
                    send_sem=send_sems.at[a, j], recv_sem=recv_sems.at[a, j],
                    device_id=peer, device_id_type=MESH_ID))
        return out

    def start(pos, in_refs, out_refs, sems):
        for cp in copies(pos, in_refs, out_refs, sems):
            cp.start()

    def finish(pos, in_refs, out_refs, sems):
        for cp in copies(pos, in_refs, out_refs, sems):
            cp.wait()

    return _Comm(arrays, [jax.ShapeDtypeStruct((n_slots,) + g.shape[1:], g.dtype) for g in arrays],
                 [pltpu.SemaphoreType.DMA((n, n_slots)), pltpu.SemaphoreType.DMA((n, n_slots))], start, finish)


def _rs_sibling_comm(gs):
    return _exchange_comm(gs, 4, lambda pos, j: (2 * j + (1 - pos[2]), (pos[0], pos[1], 1 - pos[2])))


def _chip_of(k, x, y):
    return x ^ ((k + 1) & 1), y ^ (((k + 1) >> 1) & 1)


def _chip_partials(pos, gs, recvs, name):
    n = len(gs)

    def body(pos_ref, *refs):
        for a in range(n):
            refs[2 * n + a][...] = (refs[a][...].astype(F32) + refs[n + a][...].astype(F32)
                                    ).astype(refs[2 * n + a].dtype)

    def g_map(k, pos_ref):
        cx, cy = _chip_of(k, pos_ref[0], pos_ref[1])
        return (4 * cx + 2 * cy + pos_ref[2], 0, 0)

    def r_map(k, pos_ref):
        cx, cy = _chip_of(k, pos_ref[0], pos_ref[1])
        return (2 * cx + cy, 0, 0)

    slab = [(None,) + g.shape[1:] for g in gs]
    return pl.pallas_call(
        body,
        grid_spec=pltpu.PrefetchScalarGridSpec(
            num_scalar_prefetch=1,
            grid=(4,),
            in_specs=[pl.BlockSpec(sh, g_map) for sh in slab] + [pl.BlockSpec(sh, r_map) for sh in slab],
            out_specs=[pl.BlockSpec(sh, lambda k, pos_ref: (k, 0, 0)) for sh in slab],
        ),
        out_shape=[jax.ShapeDtypeStruct((4,) + g.shape[1:], g.dtype) for g in gs],
        compiler_params=_cparams(),
        name=name,
    )(pos, *gs, *recvs)


def _rs_chips_comm(parts):
    return _exchange_comm(parts, 3, lambda pos, k: (k, (*_chip_of(k, pos[0], pos[1]), pos[2])))


def _adamw_math(w, g, m, v):
    m = ADAM_B1 * m + (1.0 - ADAM_B1) * g
    v = ADAM_B2 * v + (1.0 - ADAM_B2) * (g * g)
    m_hat = m / (1.0 - ADAM_B1 ** ADAM_STEP)
    v_hat = v / (1.0 - ADAM_B2 ** ADAM_STEP)
    delta = -ADAM_LR * (m_hat / (jnp.sqrt(v_hat) + ADAM_EPS) + ADAM_WD * w)
    return delta, m, v


def _adamw_weight(parts, recvs, w, m, v, name):
    depth, a, b = w.shape
    ta = _tile(a, 256, unit=16)
    ni = a // ta

    def body(*refs):
        p_refs, r_refs = refs[:depth], refs[depth:2 * depth]
        w_ref, m_ref, v_ref = refs[2 * depth:2 * depth + 3]
        g_out, d_out, m_out, v_out = refs[2 * depth + 3:]
        layer = pl.program_id(0)
        g = jnp.zeros((ta, b), F32)
        for l in range(depth):
            gl = p_refs[l][...].astype(F32)
            for k in range(3):
                gl = gl + r_refs[l][k].astype(F32)
            g = jnp.where(layer == l, gl, g)
        delta, m_new, v_new = _adamw_math(w_ref[...], g, m_ref[...], v_ref[...])
        g_out[...] = g
        d_out[...] = delta
        m_out[...] = m_new
        v_out[...] = v_new

    def hold(l):
        return lambda layer, i: jnp.where(layer == l, i, jnp.where(layer < l, 0, ni - 1))

    p_specs = [pl.BlockSpec((None, ta, b), (lambda layer, i, f=hold(l): (3, f(layer, i), 0))) for l in range(depth)]
    r_specs = [pl.BlockSpec((3, ta, b), (lambda layer, i, f=hold(l): (0, f(layer, i), 0))) for l in range(depth)]
    row = pl.BlockSpec((None, ta, b), lambda layer, i: (layer, i, 0))
    return pl.pallas_call(
        body,
        grid=(depth, ni),
        in_specs=p_specs + r_specs + [row, row, row],
        out_specs=[row] * 4,
        out_shape=[jax.ShapeDtypeStruct(w.shape, F32)] * 4,
        compiler_params=_cparams(),
        name=name,
    )(*parts, *recvs, w, m, v)


def _adamw_replicated(gathered, w, m, v, name):
    r, lanes = w.shape

    def body(g_ref, w_ref, m_ref, v_ref, g_out, d_out, m_out, v_out):
        g = g_ref[0]
        for k in range(1, N_DEV):
            g = g + g_ref[k]
        delta, m_new, v_new = _adamw_math(w_ref[...], g, m_ref[...], v_ref[...])
        g_out[...] = g
        d_out[...] = delta
        m_out[...] = m_new
        v_out[...] = v_new

    return pl.pallas_call(
        body,
        out_shape=[jax.ShapeDtypeStruct((r, lanes), F32)] * 4,
        name=name,
    )(gathered, w, m, v)


def _full_weight(name, gathered):
    _, a, b = gathered.shape
    if name in COL_SHARDED:
        return gathered.transpose(1, 0, 2).reshape(a, N_DEV * b)
    return gathered.reshape(N_DEV * a, b)


def _to_slabs(name, gfull, shard_shape):
    a, b = shard_shape
    if name in COL_SHARDED:
        return gfull.reshape(a, N_DEV, b).transpose(1, 0, 2)
    return gfull.reshape(N_DEV, a, b)


def _pack_small(arrs):
    rows = []
    for a in arrs:
        flat = a.astype(F32).reshape(-1)
        pad = (-flat.shape[0]) % LANES
        rows.append(jnp.pad(flat, (0, pad)).reshape(-1, LANES))
    packed = jnp.concatenate(rows, axis=0)
    return jnp.pad(packed, ((0, (-packed.shape[0]) % 8), (0, 0)))


def _unpack_small(packed, shapes):
    out, off = [], 0
    for shp in shapes:
        n = math.prod(shp)
        rows = -(-n // LANES)
        out.append(packed[off:off + rows].reshape(-1)[:n].reshape(shp))
        off += rows
    return out


def _of(layer, *names):
    return tuple((layer, n) for n in names)


MIXER_W = ("w_in", "w_up_a", "w_up_b", "w_o")
MLP_W = ("w_ff1", "w_ff2", "w_pe", "w_pg")

GATHERS = (
    (None, _of(0, "w_in")),
    ("proj_qkv_l0", _of(0, "w_up_a", "w_up_b", "w_o")),
    ("sb_fwd_l0", _of(0, *MLP_W) + _of(1, "w_ff1")),
    ("swa_fwd_l0", _of(1, "w_in")),
    ("ff1_l0", _of(1, "w_ff2")),
    ("ff2_l0", _of(1, "w_up_a", "w_up_b", "w_o", "w_pe", "w_pg")),
)
REDUCES = (
    (_of(1, "w_ff1"), "dw_ff2_l0", "dact_l0"),
    (_of(1, "w_ff2"), "dw_ff2_l0", "dw_ff1_l0"),
    (_of(1, "w_in"), "dw_ff2_l0", "dh_mlp_l0"),
    (_of(1, "w_up_a", "w_up_b", "w_o", "w_pe", "w_pg"), "dw_ff2_l0", "swa_bwd_l0"),
    (_of(0, *MLP_W), "dh_mlp_l0", "sb_bwd_l0"),
    (_of(0, "w_o", "w_up_a", "w_up_b"), "do_a_l0", "sb_bwd_l0"),
    (_of(0, "w_in"), None, None),
)


def _merge_comms(comms):
    if len(comms) == 1:
        return comms[0]

    def cuts(counts):
        edges = [0]
        for c in counts:
            edges.append(edges[-1] + c)
        return [slice(a, b) for a, b in zip(edges[:-1], edges[1:])]

    s_in = cuts([len(c.inputs) for c in comms])
    s_out = cuts([len(c.out_shapes) for c in comms])
    s_sem = cuts([len(c.sems) for c in comms])

    def start(pos, cin, cout, csem):
        for c, i, o, s in zip(comms, s_in, s_out, s_sem):
            c.start(pos, cin[i], cout[o], csem[s])

    def finish(pos, cin, cout, csem):
        for c, i, o, s in zip(comms, s_in, s_out, s_sem):
            c.finish(pos, cin[i], cout[o], csem[s])

    return _Comm(sum([c.inputs for c in comms], []), sum([c.out_shapes for c in comms], []),
                 sum([c.sems for c in comms], []), start, finish)


class _LayerWeights:
    def __init__(self, full, layer):
        self.full, self.layer, self.cache = full, layer, {}

    def __getitem__(self, name):
        if name not in self.cache:
            if name == "w_qkv":
                self.cache[name] = self.full[(self.layer, "w_in")][:, :QKV_COLS]
            elif name == "w_gate":
                self.cache[name] = self.full[(self.layer, "w_in")][:, QKV_COLS:]
            else:
                self.cache[name] = self.full[(self.layer, name)]
        return self.cache[name]


class _Plan:
    def __init__(self, w_sh, pos):
        self.w_sh = dict(zip(WEIGHTS, w_sh))
        self.pos = pos
        self.full, self.gw, self.parts, self.recv = {}, {}, {}, {}
        self.slabs = {}
        self.hosted = {}
        for i, (host, _) in enumerate(GATHERS):
            if host is not None:
                self.hosted.setdefault(host, []).append(("gather", i))
        for i, (_, sib_host, chip_host) in enumerate(REDUCES):
            assert (sib_host is None) == (chip_host is None)
            if sib_host is not None:
                self.hosted.setdefault(sib_host, []).append(("sibling", i))
                self.hosted.setdefault(chip_host, []).append(("chips", i))

    def _gather(self, i):
        return _gather_comm([self.w_sh[n][layer].astype(WIRE) for layer, n in GATHERS[i][1]])

    def _gathered(self, i, outs):
        for (layer, n), g in zip(GATHERS[i][1], outs):
            self.full[(layer, n)] = _full_weight(n, g)

    def weights(self, layer):
        for i, (host, items) in enumerate(GATHERS):
            if host is None and items[0][0] == layer:
                self._gathered(i, _run_comm(self._gather(i), f"gather_{i}"))
        return _LayerWeights(self.full, layer)

    def grad(self, layer, name, value):
        self.gw[(layer, name)] = value

    def _sibling(self, i):
        self.slabs[i] = [_to_slabs(n, self.gw[(layer, n)], self.w_sh[n].shape[1:]) for layer, n in REDUCES[i][0]]
        return _rs_sibling_comm(self.slabs[i])

    def _sibling_done(self, i, outs):
        parts = _chip_partials(self.pos, self.slabs[i], outs, f"chip_partials_{i}")
        for item, part in zip(REDUCES[i][0], parts):
            self.parts[item] = part

    def _chips(self, i):
        return _rs_chips_comm([self.parts[item] for item in REDUCES[i][0]])

    def _chips_done(self, i, outs):
        for item, r in zip(REDUCES[i][0], outs):
            self.recv[item] = r

    def layer_done(self, layer):
        for i, (items, sib_host, _) in enumerate(REDUCES):
            if sib_host is None and items[0][0] == layer:
                self._sibling_done(i, _run_comm(self._sibling(i), f"reduce_sibling_{i}"))
                self._chips_done(i, _run_comm(self._chips(i), f"reduce_chips_{i}"))

    def comm(self, name):
        if name not in self.hosted:
            return None
        make = {"gather": self._gather, "sibling": self._sibling, "chips": self._chips}
        return _merge_comms([make[kind](i) for kind, i in self.hosted[name]])

    def done(self, name, outs):
        took = {"gather": self._gathered, "sibling": self._sibling_done, "chips": self._chips_done}
        off = 0
        for kind, i in self.hosted[name]:
            n = len(GATHERS[i][1]) if kind == "gather" else len(REDUCES[i][0])
            took[kind](i, outs[off:off + n])
            off += n


def kernel(x, p, w_in, w_up_a, w_up_b, w_o, w_ff1, w_ff2, w_pe, w_pg, g_mix, g_mlp, g_pe, g_final, sinks, rel_bias, loss_target, m_w_in, m_w_up_a, m_w_up_b, m_w_o, m_w_ff1, m_w_ff2, m_w_pe, m_w_pg, m_g_mix, m_g_mlp, m_g_pe, m_g_final, m_sinks, m_rel_bias, v_w_in, v_w_up_a, v_w_up_b, v_w_o, v_w_ff1, v_w_ff2, v_w_pe, v_w_pg, v_g_mix, v_g_mlp, v_g_pe, v_g_final, v_sinks, v_rel_bias):
    w_sh = [w_in, w_up_a, w_up_b, w_o, w_ff1, w_ff2, w_pe, w_pg]
    m_sh = [m_w_in, m_w_up_a, m_w_up_b, m_w_o, m_w_ff1, m_w_ff2, m_w_pe, m_w_pg]
    v_sh = [v_w_in, v_w_up_a, v_w_up_b, v_w_o, v_w_ff1, v_w_ff2, v_w_pe, v_w_pg]
    depth = w_in.shape[0]
    assert depth == 2 and x.shape[-1] * 2 + QKV_COLS == w_in.shape[2] * N_DEV

    px, py, pc = _position()
    plan = _Plan(w_sh, jnp.stack([px, py, pc]).astype(jnp.int32))
    loss_row, grad_x, _, small = _local_step(
        x[0], p[:, 0], loss_target[0], plan.weights, g_mix, g_mlp, g_pe, g_final, sinks, rel_bias, plan=plan)

    grad_w, delta_w, new_m, new_v = [], [], [], []
    for a, name in enumerate(WEIGHTS):
        outs = _adamw_weight([plan.parts[(l, name)] for l in range(depth)],
                             [plan.recv[(l, name)] for l in range(depth)],
                             w_sh[a], m_sh[a], v_sh[a], f"adamw_{name}")
        for lst, o in zip((grad_w, delta_w, new_m, new_v), outs):
            lst.append(o)

    small_w = [g_mix, g_mlp, g_pe, g_final, sinks, rel_bias]
    small_m = [m_g_mix, m_g_mlp, m_g_pe, m_g_final, m_sinks, m_rel_bias]
    small_v = [v_g_mix, v_g_mlp, v_g_pe, v_g_final, v_sinks, v_rel_bias]
    small_shapes = [a.shape for a in small_w] + [(1,)]
    zero = jnp.zeros((1,), F32)
    small_g = _pack_small([small[n] for n in SMALL] + [loss_row[0, :1]])
    small_all = _run_comm(_gather_comm([small_g]), "gather_small")[0]
    packed_s = _adamw_replicated(small_all, _pack_small(small_w + [zero]), _pack_small(small_m + [zero]),
                                 _pack_small(small_v + [zero + 1.0]), "adamw_replicated")
    sg, sd, sm, sv = [_unpack_small(t, small_shapes) for t in packed_s]
    loss = sg[-1][0]

    return (loss, grad_x[None], *grad_w, *sg[:-1], *delta_w, *sd[:-1], *new_m, *sm[:-1], *new_v, *sv[:-1])
```

```python
import functools
import math

import numpy as np
import jax
import jax.numpy as jnp
from jax import lax
from jax.experimental import pallas as pl
from jax.experimental.pallas import tpu as pltpu

F32 = jnp.float32
MXU = jnp.bfloat16
WIRE = jnp.bfloat16

HEAD_DIM = 64
SB_HEADS = 8
SW_HEADS = 8
SW_KV = 2
SW_GROUP = SW_HEADS // SW_KV
BLOCK = 128
N_BUCKETS = 32
MAX_DISTANCE = 128
EPS = 1e-6
SCALE = HEAD_DIM ** -0.5
SB_W = SB_HEADS * HEAD_DIM
SW_W = SW_HEADS * HEAD_DIM
QKV_COLS = 3 * SB_W + SW_W + 2 * SW_KV * HEAD_DIM
N_DEV = 8
LANES = 128
N_PAIR = SB_HEADS // 2
NEG = -1e30

ADAM_LR = 0.001
ADAM_B1 = 0.9
ADAM_B2 = 0.999
ADAM_EPS = 1e-08
ADAM_WD = 0.01
ADAM_STEP = 10

VMEM_LIMIT = 48 * 1024 * 1024
SB_TQ = 256
SB_DEAD = -105.0

WEIGHTS = ("w_in", "w_up_a", "w_up_b", "w_o", "w_ff1", "w_ff2", "w_pe", "w_pg")
COL_SHARDED = ("w_in", "w_up_a", "w_up_b", "w_ff1", "w_pe")
SMALL = ("g_mix", "g_mlp", "g_pe", "g_final", "sinks", "rel_bias")


def _cparams(**kw):
    return pltpu.CompilerParams(vmem_limit_bytes=VMEM_LIMIT, **kw)


def _dot(a, b):
    return jnp.dot(a, b, preferred_element_type=F32)


def _dot_nt(a, b):
    return lax.dot_general(a, b, (((1,), (1,)), ((), ())), preferred_element_type=F32)


def _dot_tn(a, b):
    return lax.dot_general(a, b, (((0,), (0,)), ((), ())), preferred_element_type=F32)


def _tile(n, target, unit=LANES):
    if n <= target:
        return n
    t = (target // unit) * unit
    while t > unit and n % t:
        t -= unit
    assert n % t == 0, (n, target)
    return t


def _sigmoid(x):
    return 1.0 / (1.0 + jnp.exp(-x))


class _Comm:
    def __init__(self, inputs, out_shapes, sems, start, finish):
        self.inputs, self.out_shapes, self.sems = list(inputs), list(out_shapes), list(sems)
        self.start, self.finish = start, finish


def _call(body, *, grid, in_specs, out_specs, out_shape, scratch_shapes=(), args, name, comm=None):
    n_in, n_out, n_scr = len(in_specs), len(out_shape), len(scratch_shapes)
    if comm is None:
        outs = pl.pallas_call(body, grid=grid, in_specs=list(in_specs), out_specs=list(out_specs),
                              out_shape=list(out_shape), scratch_shapes=list(scratch_shapes),
                              compiler_params=_cparams(), name=name)(*args)
        return list(outs), None
    ci, co = len(comm.inputs), len(comm.out_shapes)
    any_spec = pl.BlockSpec(memory_space=pl.ANY)

    def wrapped(*refs):
        ins, cin = refs[:n_in], refs[n_in:n_in + ci]
        o0 = n_in + ci
        outs, cout = refs[o0:o0 + n_out], refs[o0 + n_out:o0 + n_out + co]
        s0 = o0 + n_out + co
        scr, csem = refs[s0:s0 + n_scr], refs[s0 + n_scr:]
        ids = [pl.program_id(d) for d in range(len(grid))]
        first = functools.reduce(jnp.logical_and, [i == 0 for i in ids])
        last = functools.reduce(jnp.logical_and, [i == g - 1 for i, g in zip(ids, grid)])
        pos = (lax.axis_index("x"), lax.axis_index("y"), lax.axis_index("c"))

        @pl.when(first)
        def _():
            comm.start(pos, cin, cout, csem)

        body(*ins, *outs, *scr)

        @pl.when(last)
        def _():
            comm.finish(pos, cin, cout, csem)

    outs = pl.pallas_call(wrapped, grid=grid, in_specs=list(in_specs) + [any_spec] * ci,
                          out_specs=list(out_specs) + [any_spec] * co,
                          out_shape=list(out_shape) + comm.out_shapes,
                          scratch_shapes=list(scratch_shapes) + comm.sems,
                          compiler_params=_cparams(), name=name)(*args, *comm.inputs)
    return list(outs[:n_out]), list(outs[n_out:])


def _mm(a, b, *, ta=False, tb=False, extras=(), epi=None, out_dtypes=(F32,),
        tm=1024, tn=1024, tk=1024, name, comm=None):
    if ta:
        kdim, m = a.shape
    else:
        m, kdim = a.shape
    n = b.shape[0] if tb else b.shape[1]
    assert (b.shape[1] if tb else b.shape[0]) == kdim
    tm, tn, tk = _tile(m, tm), _tile(n, tn), _tile(kdim, tk)
    nk = kdim // tk
    n_ex, n_out = len(extras), len(out_dtypes)

    a_spec = (pl.BlockSpec((tk, tm), lambda i, j, k: (k, i)) if ta
              else pl.BlockSpec((tm, tk), lambda i, j, k: (i, k)))
    b_spec = (pl.BlockSpec((tn, tk), lambda i, j, k: (j, k)) if tb
              else pl.BlockSpec((tk, tn), lambda i, j, k: (k, j)))
    ex_specs = []
    for e in extras:
        assert e.shape == (m, n), (e.shape, m, n)
        ex_specs.append(pl.BlockSpec((tm, tn), lambda i, j, k: (i, j)))
    out_spec = pl.BlockSpec((tm, tn), lambda i, j, k: (i, j))

    def body(a_ref, b_ref, *rest):
        ex_refs = rest[:n_ex]
        out_refs = rest[n_ex:n_ex + n_out]
        acc = rest[-1]
        k = pl.program_id(2)

        @pl.when(k == 0)
        def _():
            acc[...] = jnp.zeros_like(acc)

        av = a_ref[...].astype(MXU)
        bv = b_ref[...].astype(MXU)
        if ta:
            acc[...] += _dot_tn(av, bv)
        elif tb:
            acc[...] += _dot_nt(av, bv)
        else:
            acc[...] += _dot(av, bv)

        @pl.when(k == nk - 1)
        def _():
            res = acc[...]
            if epi is not None:
                res = epi(res, *[e[...] for e in ex_refs])
            if not isinstance(res, tuple):
                res = (res,)
            for o_ref, r in zip(out_refs, res):
                o_ref[...] = r.astype(o_ref.dtype)

    outs, couts = _call(
        body,
        grid=(m // tm, n // tn, nk),
        in_specs=[a_spec, b_spec] + ex_specs,
        out_specs=[out_spec] * n_out,
        out_shape=[jax.ShapeDtypeStruct((m, n), dt) for dt in out_dtypes],
        scratch_shapes=[pltpu.VMEM((tm, tn), F32)],
        args=(a, b, *extras), name=name, comm=comm)
    res = outs[0] if n_out == 1 else tuple(outs)
    return res if comm is None else (res, couts)


def _rms_fwd(x, g, name):
    s, d = x.shape
    tr = _tile(s, 256)

    def body(x_ref, g_ref, h_ref, r_ref):
        xf = x_ref[...]
        r = lax.rsqrt(jnp.mean(xf * xf, axis=-1, keepdims=True) + EPS)
        h_ref[...] = ((xf * r) * g_ref[...]).astype(h_ref.dtype)
        r_ref[...] = r

    return pl.pallas_call(
        body,
        grid=(s // tr,),
        in_specs=[pl.BlockSpec((tr, d), lambda i: (i, 0)), pl.BlockSpec((1, d), lambda i: (0, 0))],
        out_specs=[pl.BlockSpec((tr, d), lambda i: (i, 0)), pl.BlockSpec((tr, 1), lambda i: (i, 0))],
        out_shape=[jax.ShapeDtypeStruct((s, d), MXU), jax.ShapeDtypeStruct((s, 1), F32)],
        compiler_params=_cparams(),
        name=name,
    )(x, g)


def _rms_bwd(x, r, g, dh, dres, name):
    s, d = x.shape
    tr = _tile(s, 256)

    def body(x_ref, r_ref, g_ref, dh_ref, dres_ref, dx_ref, dxb_ref, dg_ref):
        @pl.when(pl.program_id(0) == 0)
        def _():
            dg_ref[...] = jnp.zeros_like(dg_ref)

        rr = r_ref[...]
        xhat = x_ref[...] * rr
        dh_v = dh_ref[...]
        dxhat = dh_v * g_ref[...]
        mean = jnp.mean(dxhat * xhat, axis=-1, keepdims=True)
        dx = dres_ref[...] + rr * (dxhat - xhat * mean)
        dx_ref[...] = dx
        dxb_ref[...] = dx.astype(dxb_ref.dtype)
        dg_ref[...] += jnp.sum(dh_v * xhat, axis=0, keepdims=True)

    row = pl.BlockSpec((tr, d), lambda i: (i, 0))
    vec = pl.BlockSpec((1, d), lambda i: (0, 0))
    return pl.pallas_call(
        body,
        grid=(s // tr,),
        in_specs=[row, pl.BlockSpec((tr, 1), lambda i: (i, 0)), vec, row, row],
        out_specs=[row, row, vec],
        out_shape=[jax.ShapeDtypeStruct((s, d), F32), jax.ShapeDtypeStruct((s, d), MXU),
                   jax.ShapeDtypeStruct((1, d), F32)],
        compiler_params=_cparams(),
        name=name,
    )(x, r, g, dh, dres)


def _loss_head(x, g, target, name):
    s, d = x.shape
    tr = _tile(s, 256)

    def body(x_ref, g_ref, t_ref, loss_ref, dx_ref, dg_ref):
        @pl.when(pl.program_id(0) == 0)
        def _():
            dg_ref[...] = jnp.zeros_like(dg_ref)
            loss_ref[...] = jnp.zeros_like(loss_ref)

        xf = x_ref[...]
        gv = g_ref[...]
        r = lax.rsqrt(jnp.mean(xf * xf, axis=-1, keepdims=True) + EPS)
        xhat = xf * r
        err = xhat * gv - t_ref[...]
        loss_ref[...] += 0.5 * jnp.sum(jnp.mean(err * err, axis=-1, keepdims=True), axis=0, keepdims=True)
        dy = err * (1.0 / d)
        dxhat = dy * gv
        mean = jnp.mean(dxhat * xhat, axis=-1, keepdims=True)
        dx_ref[...] = r * (dxhat - xhat * mean)
        dg_ref[...] += jnp.sum(dy * xhat, axis=0, keepdims=True)

    row = pl.BlockSpec((tr, d), lambda i: (i, 0))
    vec = pl.BlockSpec((1, d), lambda i: (0, 0))
    return pl.pallas_call(
        body,
        grid=(s // tr,),
        in_specs=[row, vec, row],
        out_specs=[pl.BlockSpec((1, LANES), lambda i: (0, 0)), row, vec],
        out_shape=[jax.ShapeDtypeStruct((1, LANES), F32), jax.ShapeDtypeStruct((s, d), F32),
                   jax.ShapeDtypeStruct((1, d), F32)],
        compiler_params=_cparams(),
        name=name,
    )(x, g, target)


def _mix_fwd(oa, ob, wa, wb, gates, name):
    s, kd = oa.shape
    d = wa.shape[1]
    tm, tn = _tile(s, 1024), _tile(d, 512)
    nj = d // tn

    def body(oa_ref, ob_ref, wa_ref, wb_ref, ga_ref, gb_ref, out_ref):
        ya = _dot(oa_ref[...], wa_ref[...])
        yb = _dot(ob_ref[...], wb_ref[...])
        out_ref[...] = (_sigmoid(ga_ref[...]) * ya + _sigmoid(gb_ref[...]) * yb).astype(out_ref.dtype)

    o_spec = pl.BlockSpec((tm, kd), lambda i, j: (i, 0))
    w_spec = pl.BlockSpec((kd, tn), lambda i, j: (0, j))
    return pl.pallas_call(
        body,
        grid=(s // tm, nj),
        in_specs=[o_spec, o_spec, w_spec, w_spec,
                  pl.BlockSpec((tm, tn), lambda i, j: (i, j)),
                  pl.BlockSpec((tm, tn), lambda i, j: (i, j + nj))],
        out_specs=pl.BlockSpec((tm, tn), lambda i, j: (i, j)),
        out_shape=jax.ShapeDtypeStruct((s, d), MXU),
        compiler_params=_cparams(),
        name=name,
    )(oa, ob, wa, wb, gates, gates)


def _mix_bwd(dx, w_o, oa, ob, wa, wb, gates, name):
    s, kd = oa.shape
    d = wa.shape[1]
    tm, tn = _tile(s, 1024), _tile(d, 512)
    nj = d // tn

    def body(dx_ref, wo_ref, oa_ref, ob_ref, wa_ref, wb_ref, ga_ref, gb_ref,
             dya_ref, dyb_ref, dga_ref, dgb_ref):
        dm = _dot_nt(dx_ref[...], wo_ref[...])
        ya = _dot(oa_ref[...], wa_ref[...])
        yb = _dot(ob_ref[...], wb_ref[...])
        sa = _sigmoid(ga_ref[...])
        sb = _sigmoid(gb_ref[...])
        dya_ref[...] = (dm * sa).astype(dya_ref.dtype)
        dyb_ref[...] = (dm * sb).astype(dyb_ref.dtype)
        dga_ref[...] = (dm * ya * sa * (1.0 - sa)).astype(dga_ref.dtype)
        dgb_ref[...] = (dm * yb * sb * (1.0 - sb)).astype(dgb_ref.dtype)

    o_spec = pl.BlockSpec((tm, kd), lambda i, j: (i, 0))
    w_spec = pl.BlockSpec((kd, tn), lambda i, j: (0, j))
    t_spec = pl.BlockSpec((tm, tn), lambda i, j: (i, j))
    return pl.pallas_call(
        body,
        grid=(s // tm, nj),
        in_specs=[pl.BlockSpec((tm, d), lambda i, j: (i, 0)),
                  pl.BlockSpec((tn, d), lambda i, j: (j, 0)),
                  o_spec, o_spec, w_spec, w_spec, t_spec,
                  pl.BlockSpec((tm, tn), lambda i, j: (i, j + nj))],
        out_specs=[t_spec] * 4,
        out_shape=[jax.ShapeDtypeStruct((s, d), MXU)] * 4,
        compiler_params=_cparams(),
        name=name,
    )(dx, w_o, oa, ob, wa, wb, gates, gates)


def _ple(p, w_pe, h, w_pg, other, *, backward, name):
    s, kp = p.shape
    d = w_pe.shape[1]
    tm, tn = _tile(s, 1024), _tile(d, 512)

    def body(p_ref, wpe_ref, h_ref, wpg_ref, other_ref, *out_refs):
        pe = _dot(p_ref[...].astype(MXU), wpe_ref[...])
        gt = _dot(h_ref[...], wpg_ref[...])
        sg = _sigmoid(gt)
        if backward:
            dout = other_ref[...]
            out_refs[0][...] = (dout * sg).astype(out_refs[0].dtype)
            out_refs[1][...] = (dout * pe * sg * (1.0 - sg)).astype(out_refs[1].dtype)
        else:
            out_refs[0][...] = other_ref[...] + pe * sg

    t_spec = pl.BlockSpec((tm, tn), lambda i, j: (i, j))
    if backward:
        out_specs, out_shape = [t_spec, t_spec], [jax.ShapeDtypeStruct((s, d), MXU)] * 2
    else:
        out_specs, out_shape = [t_spec], [jax.ShapeDtypeStruct((s, d), F32)]
    outs = pl.pallas_call(
        body,
        grid=(s // tm, d // tn),
        in_specs=[pl.BlockSpec((tm, kp), lambda i, j: (i, 0)),
                  pl.BlockSpec((kp, tn), lambda i, j: (0, j)),
                  pl.BlockSpec((tm, d), lambda i, j: (i, 0)),
                  pl.BlockSpec((d, tn), lambda i, j: (0, j)),
                  t_spec],
        out_specs=out_specs,
        out_shape=out_shape,
        compiler_params=_cparams(),
        name=name,
    )(p, w_pe, h, w_pg, other)
    return tuple(outs) if backward else outs[0]


def _split_dot(x, tri):
    hi = x.astype(jnp.bfloat16)
    lo = (x - hi.astype(F32)).astype(jnp.bfloat16)
    return _dot(hi, tri) + _dot(lo, tri)


def _log_sigmoids(z):
    t = jnp.log1p(jnp.exp(-jnp.abs(z)))
    return jnp.minimum(z, 0.0) - t, jnp.minimum(-z, 0.0) - t


def _head_lanes(hh):
    lane = lax.broadcasted_iota(jnp.int32, (1, LANES), 1)
    return jnp.logical_and(lane >= hh * HEAD_DIM, lane < (hh + 1) * HEAD_DIM)


def _sb_fwd(qkv, name, comm=None):
    s = qkv.shape[0]
    tq = _tile(s, SB_TQ)

    def body(q_ref, k_ref, v_ref, o_ref):
        i = pl.program_id(1)
        qf = q_ref[...].astype(F32) * SCALE
        row = lax.broadcasted_iota(jnp.int32, (tq, tq), 0)
        col = lax.broadcasted_iota(jnp.int32, (tq, tq), 1)
        causal = col < row
        tri = jnp.where(row > col, 1.0, 0.0).astype(jnp.bfloat16)

        qms = [jnp.where(_head_lanes(hh), qf, 0.0).astype(MXU) for hh in range(2)]

        def block(kb, cs, accs, masked, gate=None):
            rows = pl.ds(pl.multiple_of(kb * tq, tq), tq)
            ks, vs = k_ref[rows, :], v_ref[rows, :]
            new_c, new_acc = [], []
            for hh in range(2):
                lb, lm = _log_sigmoids(_dot_nt(qms[hh], ks))
                if masked:
                    lm = jnp.where(causal, lm, 0.0)
                if gate is not None:
                    lm = lm * gate
                a = jnp.exp(lb + _split_dot(lm, tri) + cs[hh])
                if masked:
                    a = jnp.where(causal, a, 0.0)
                if gate is not None:
                    a = a * gate
                new_acc.append(accs[hh] + _dot(a.astype(MXU), vs))
                new_c.append(cs[hh] + jnp.sum(lm, axis=1, keepdims=True))
            return tuple(new_c), tuple(new_acc)

        def top(cs):
            return jnp.maximum(jnp.max(cs[0]), jnp.max(cs[1]))

        zc, za = jnp.zeros((tq, 1), F32), jnp.zeros((tq, LANES), F32)
        cs, accs = block(i, (zc, zc), (za, za), True)
        cs, accs = block(jnp.maximum(i - 1, 0), cs, accs, False, jnp.where(i > 0, 1.0, 0.0))

        def live(st):
            return jnp.logical_and(st[0] >= 0, st[1] > SB_DEAD)

        def walk(st):
            cs, accs = block(st[0], st[2], st[3], False)
            return st[0] - 1, top(cs), cs, accs

        accs = lax.while_loop(live, walk, (i - 2, top(cs), cs, accs))[3]
        o_ref[...] = jnp.where(_head_lanes(0), accs[0], accs[1]).astype(o_ref.dtype)

    outs, couts = _call(
        body,
        grid=(N_PAIR, s // tq),
        in_specs=[pl.BlockSpec((tq, LANES), lambda p, i: (i, p)),
                  pl.BlockSpec((s, LANES), lambda p, i: (0, N_PAIR + p)),
                  pl.BlockSpec((s, LANES), lambda p, i: (0, 2 * N_PAIR + p))],
        out_specs=[pl.BlockSpec((tq, LANES), lambda p, i: (i, p))],
        out_shape=[jax.ShapeDtypeStruct((s, SB_W), MXU)],
        args=(qkv, qkv, qkv), name=name, comm=comm)
    return outs[0] if comm is None else (outs[0], couts)


def _sb_bwd(qkv, do, name, comm=None):
    s = qkv.shape[0]
    tq = _tile(s, SB_TQ)
    nq = s // tq

    def body(q_ref, k_ref, v_ref, do_ref, dq_ref, dk_ref, dv_ref, dk_acc, dv_acc, carries):
        i = pl.program_id(1)

        @pl.when(i == 0)
        def _():
            dk_acc[...] = jnp.zeros_like(dk_acc)
            dv_acc[...] = jnp.zeros_like(dv_acc)

        qf = q_ref[...].astype(F32) * SCALE
        dof = do_ref[...]
        row = lax.broadcasted_iota(jnp.int32, (tq, tq), 0)
        col = lax.broadcasted_iota(jnp.int32, (tq, tq), 1)
        causal = col < row
        tri_rev = jnp.where(row > col, 1.0, 0.0).astype(jnp.bfloat16)
        tri_excl = jnp.where(row < col, 1.0, 0.0).astype(jnp.bfloat16)

        qms = [jnp.where(_head_lanes(hh), qf, 0.0).astype(MXU) for hh in range(2)]
        doms = [jnp.where(_head_lanes(hh), dof, jnp.zeros_like(dof)) for hh in range(2)]

        def terms(kb, masked):
            rows = pl.ds(pl.multiple_of(kb * tq, tq), tq)
            ks = k_ref[rows, :]
            out = []
            for hh in range(2):
                lb, lm = _log_sigmoids(_dot_nt(qms[hh], ks))
                if masked:
                    lm = jnp.where(causal, lm, 0.0)
                out.append((lb, lm))
            return out

        def row_sums(kb, masked, pre=None):
            pre = terms(kb, masked) if pre is None else pre
            return [jnp.sum(lm, axis=1, keepdims=True) for _, lm in pre]

        def top(cs):
            return jnp.maximum(jnp.max(cs[0]), jnp.max(cs[1]))

        def live(st):
            return jnp.logical_and(st[0] >= 0, st[1] > SB_DEAD)

        def record(st):
            kb, cs = st[0], st[2]
            sums = row_sums(kb, False)
            for hh in range(2):
                carries[hh, kb] = cs[hh]
            cs = tuple(cs[hh] + sums[hh] for hh in range(2))
            return kb - 1, top(cs), cs

        prev = jnp.maximum(i - 1, 0)
        gate = jnp.where(i > 0, 1.0, 0.0)
        t_diag, t_prev = terms(i, True), terms(prev, False)
        c_diag = row_sums(i, True, t_diag)
        sums = row_sums(prev, False, t_prev)
        c_prev = tuple(c_diag[hh] + sums[hh] * gate for hh in range(2))
        first = lax.while_loop(live, record, (i - 2, top(c_prev), c_prev))[0] + 1

        def block(kb, cs, gpres, dqs, masked, gate=None, pre=None):
            rows = pl.ds(pl.multiple_of(kb * tq, tq), tq)
            ks, vs = k_ref[rows, :], v_ref[rows, :]
            pre = terms(kb, masked) if pre is None else pre
            new_g, new_dq = [], []
            dk_add, dv_add = None, None
            for hh in range(2):
                lb, lm = pre[hh]
                a = jnp.exp(lb + _split_dot(lm, tri_rev) + cs[hh])
                if masked:
                    a = jnp.where(causal, a, 0.0)
                if gate is not None:
                    a = a * gate
                g = a * _dot_nt(doms[hh], vs)
                gsum = gpres[hh] + _split_dot(g, tri_excl)
                dz = g - (g + gsum) * jnp.exp(lb)
                if masked:
                    dz = jnp.where(causal, dz, 0.0)
                if gate is not None:
                    dz = dz * gate
                dzb = dz.astype(MXU)
                new_dq.append(dqs[hh] + _dot(dzb, ks))
                dk_h = _dot_tn(dzb, qms[hh])
                dv_h = _dot_tn(a.astype(MXU), doms[hh])
                dk_add = dk_h if dk_add is None else dk_add + dk_h
                dv_add = dv_h if dv_add is None else dv_add + dv_h
                new_g.append(gpres[hh] + jnp.sum(g, axis=1, keepdims=True))
            dk_acc[rows, :] += dk_add
            dv_acc[rows, :] += dv_add
            return tuple(new_g), tuple(new_dq)

        zc, za = jnp.zeros((tq, 1), F32), jnp.zeros((tq, LANES), F32)
        gpres, dqs = lax.fori_loop(
            first, i - 1, lambda kb, cr: block(kb, (carries[0, kb], carries[1, kb]), cr[0], cr[1], False),
            ((zc, zc), (za, za)))
        gpres, dqs = block(prev, c_diag, gpres, dqs, False, gate, t_prev)
        dqs = block(i, (zc, zc), gpres, dqs, True, None, t_diag)[1]
        dq_ref[...] = (jnp.where(_head_lanes(0), dqs[0], dqs[1]) * SCALE).astype(dq_ref.dtype)

        @pl.when(i == nq - 1)
        def _():
            dk_ref[...] = dk_acc[...].astype(dk_ref.dtype)
            dv_ref[...] = dv_acc[...].astype(dv_ref.dtype)

    blk = pl.BlockSpec((tq, LANES), lambda p, i: (i, p))
    full = pl.BlockSpec((s, LANES), lambda p, i: (0, p))
    outs, couts = _call(
        body,
        grid=(N_PAIR, nq),
        in_specs=[blk,
                  pl.BlockSpec((s, LANES), lambda p, i: (0, N_PAIR + p)),
                  pl.BlockSpec((s, LANES), lambda p, i: (0, 2 * N_PAIR + p)),
                  blk],
        out_specs=[blk, full, full],
        out_shape=[jax.ShapeDtypeStruct((s, SB_W), MXU)] * 3,
        scratch_shapes=[pltpu.VMEM((s, LANES), F32), pltpu.VMEM((s, LANES), F32),
                        pltpu.VMEM((2, nq, tq, 1), F32)],
        args=(qkv, qkv, qkv, do), name=name, comm=comm)
    return tuple(outs) if comm is None else (tuple(outs), couts)


def _bucket_table():
    i = np.arange(BLOCK)[:, None]
    j = np.arange(2 * BLOCK)[None, :]
    d = np.maximum(BLOCK + i - j, 0)
    max_exact = N_BUCKETS // 2
    df = np.maximum(d, 1).astype(np.float32)
    large = max_exact + (np.log(df / max_exact) / math.log(MAX_DISTANCE / max_exact)
                         * (N_BUCKETS - max_exact)).astype(np.int32)
    large = np.minimum(large, N_BUCKETS - 1)
    return np.where(d < max_exact, d, large).astype(np.int32)


def _build_bias(rel_bias, buckets, name):
    def body(rb_ref, bk_ref, out_ref):
        h = pl.program_id(0)
        bk = bk_ref[...]
        acc = jnp.zeros(bk.shape, F32)
        for b in range(N_BUCKETS):
            acc = jnp.where(bk == b, rb_ref[b, h], acc)
        out_ref[...] = acc

    return pl.pallas_call(
        body,
        grid=(SW_HEADS,),
        in_specs=[pl.BlockSpec(memory_space=pltpu.SMEM),
                  pl.BlockSpec((BLOCK, 2 * BLOCK), lambda h: (0, 0))],
        out_specs=pl.BlockSpec((None, BLOCK, 2 * BLOCK), lambda h: (h, 0, 0)),
        out_shape=jax.ShapeDtypeStruct((SW_HEADS, BLOCK, 2 * BLOCK), F32),
        name=name,
    )(rel_bias, buckets)


def _bias_grad(dbias_layers, buckets, name):
    n_l = len(dbias_layers)

    def body(*refs):
        bk = refs[n_l][...]
        out_ref = refs[n_l + 1]
        db = refs[0][...]
        for r in refs[1:n_l]:
            db = db + r[...]
        lane = lax.broadcasted_iota(jnp.int32, (1, LANES), 1)
        acc = jnp.zeros((1, LANES), F32)
        for b in range(N_BUCKETS):
            part = jnp.sum(jnp.where(bk == b, db, 0.0), axis=1, keepdims=True)
            tot = jnp.sum(part, axis=0, keepdims=True)
            acc = jnp.where(lane == b, tot, acc)
        out_ref[...] = acc

    hspec = pl.BlockSpec((None, BLOCK, 2 * BLOCK), lambda h: (h, 0, 0))
    return pl.pallas_call(
        body,
        grid=(SW_HEADS,),
        in_specs=[hspec] * n_l + [pl.BlockSpec((BLOCK, 2 * BLOCK), lambda h: (0, 0))],
        out_specs=pl.BlockSpec((None, 1, LANES), lambda h: (h, 0, 0)),
        out_shape=jax.ShapeDtypeStruct((SW_HEADS, 1, LANES), F32),
        name=name,
    )(*dbias_layers, buckets)


GROUP_ROWS = SW_GROUP * BLOCK


def _group_lanes(g):
    lane = lax.broadcasted_iota(jnp.int32, (1, LANES), 1)
    gvec = jnp.zeros((1, LANES), jnp.int32) + g
    return jnp.where(lane >= HEAD_DIM, 1, 0) == gvec, gvec


def _stack_heads(x, g):
    kv_lanes, gvec = _group_lanes(g)
    parts = []
    for j in range(SW_GROUP):
        half = x[:, (j // 2) * LANES:(j // 2 + 1) * LANES]
        moved = jnp.where(gvec == j % 2, half, pltpu.roll(half, HEAD_DIM, 1))
        parts.append(jnp.where(kv_lanes, moved, 0.0))
    return jnp.concatenate(parts, axis=0)


def _unstack_heads(y, g):
    _, gvec = _group_lanes(g)
    heads = []
    for j in range(SW_GROUP):
        yj = y[j * BLOCK:(j + 1) * BLOCK]
        heads.append(jnp.where(gvec == j % 2, yj, pltpu.roll(yj, HEAD_DIM, 1)))
    pairs = [jnp.where(_head_lanes(0), heads[2 * p], heads[2 * p + 1]) for p in range(SW_GROUP // 2)]
    return jnp.concatenate(pairs, axis=1)


def _per_head_col(values):
    return jnp.concatenate([jnp.zeros((BLOCK, 1), F32) + v for v in values], axis=0)


def _swa_scores(qs, kp, kc, bias_ref, n):
    row = jnp.bitwise_and(lax.broadcasted_iota(jnp.int32, (GROUP_ROWS, BLOCK), 0), BLOCK - 1)
    col = lax.broadcasted_iota(jnp.int32, (GROUP_ROWS, BLOCK), 1)
    bias = bias_ref[...].reshape(GROUP_ROWS, 2 * BLOCK)
    s1 = _dot_nt(qs, kp) + bias[:, :BLOCK]
    s2 = _dot_nt(qs, kc) + bias[:, BLOCK:]
    no_prev = jnp.where(n > 0, 0, BLOCK)
    s1 = jnp.where(col > row + no_prev, s1, NEG)
    s2 = jnp.where(col <= row, s2, NEG)
    return s1, s2


def _swa_specs(s):
    q_blk = 3 * SB_W // (2 * LANES)
    k_blk = (3 * SB_W + SW_W) // LANES
    return (pl.BlockSpec((s, 2 * LANES), lambda g: (0, q_blk + g)),
            pl.BlockSpec((s, LANES), lambda g: (0, k_blk)),
            pl.BlockSpec((s, LANES), lambda g: (0, k_blk + 1)))


def _swa_fwd(qkv, bias, sinks, name, comm=None):
    s = qkv.shape[0]
    nb = s // BLOCK

    def body(sink_ref, q_ref, k_ref, v_ref, bias_ref, o_ref, lse_ref):
        g = pl.program_id(0)
        sink = _per_head_col([sink_ref[SW_GROUP * g + j] for j in range(SW_GROUP)])
        lane = lax.broadcasted_iota(jnp.int32, (1, LANES), 1)

        def step(n, carry):
            r0 = pl.multiple_of(n * BLOCK, BLOCK)
            p0 = pl.multiple_of(jnp.maximum(n - 1, 0) * BLOCK, BLOCK)
            cur, prev = pl.ds(r0, BLOCK), pl.ds(p0, BLOCK)
            qs = _stack_heads(q_ref[cur, :].astype(F32) * SCALE, g).astype(MXU)
            s1, s2 = _swa_scores(qs, k_ref[prev, :], k_ref[cur, :], bias_ref, n)
            m = jnp.maximum(jnp.maximum(jnp.max(s1, axis=1, keepdims=True),
                                        jnp.max(s2, axis=1, keepdims=True)), sink)
            e1 = jnp.exp(s1 - m)
            e2 = jnp.exp(s2 - m)
            den = jnp.sum(e1, axis=1, keepdims=True) + jnp.sum(e2, axis=1, keepdims=True) + jnp.exp(sink - m)
            o = _dot((e1 / den).astype(MXU), v_ref[prev, :]) + _dot((e2 / den).astype(MXU), v_ref[cur, :])
            o_ref[cur, :] = _unstack_heads(o, g).astype(o_ref.dtype)
            lse = m + jnp.log(den)
            lse_row = jnp.zeros((BLOCK, LANES), F32)
            for j in range(SW_GROUP):
                lse_row = jnp.where(lane == j, lse[j * BLOCK:(j + 1) * BLOCK], lse_row)
            lse_ref[cur, :] = lse_row
            return carry

        lax.fori_loop(0, nb, step, 0)

    outs, couts = _call(
        body,
        grid=(SW_KV,),
        in_specs=[pl.BlockSpec(memory_space=pltpu.SMEM), *_swa_specs(s),
                  pl.BlockSpec((SW_GROUP, BLOCK, 2 * BLOCK), lambda g: (g, 0, 0))],
        out_specs=[pl.BlockSpec((s, 2 * LANES), lambda g: (0, g)),
                   pl.BlockSpec((None, s, LANES), lambda g: (g, 0, 0))],
        out_shape=[jax.ShapeDtypeStruct((s, SW_W), MXU), jax.ShapeDtypeStruct((SW_KV, s, LANES), F32)],
        args=(sinks, qkv, qkv, qkv, bias), name=name, comm=comm)
    return tuple(outs) if comm is None else (tuple(outs), couts)


def _swa_bwd(qkv, bias, sinks, do, lse, name, comm=None):
    s = qkv.shape[0]
    nb = s // BLOCK

    def body(sink_ref, q_ref, k_ref, v_ref, bias_ref, do_ref, lse_ref,
             dq_ref, dk_ref, dv_ref, dbias_ref, dsink_ref, dk_acc, dv_acc):
        g = pl.program_id(0)
        sink = _per_head_col([sink_ref[SW_GROUP * g + j] for j in range(SW_GROUP)])
        lane = lax.broadcasted_iota(jnp.int32, (1, LANES), 1)

        @pl.when(g == 0)
        def _():
            dk_acc[...] = jnp.zeros_like(dk_acc)
            dv_acc[...] = jnp.zeros_like(dv_acc)

        dbias_ref[...] = jnp.zeros_like(dbias_ref)

        def step(n, dsink_rows):
            r0 = pl.multiple_of(n * BLOCK, BLOCK)
            p0 = pl.multiple_of(jnp.maximum(n - 1, 0) * BLOCK, BLOCK)
            cur, prev = pl.ds(r0, BLOCK), pl.ds(p0, BLOCK)
            qs = _stack_heads(q_ref[cur, :].astype(F32) * SCALE, g).astype(MXU)
            dos = _stack_heads(do_ref[cur, :].astype(F32), g).astype(MXU)
            kp, kc, vp, vc = k_ref[prev, :], k_ref[cur, :], v_ref[prev, :], v_ref[cur, :]
            lse_row = lse_ref[cur, :]
            lse = jnp.concatenate([jnp.sum(jnp.where(lane == j, lse_row, 0.0), axis=1, keepdims=True)
                                   for j in range(SW_GROUP)], axis=0)
            s1, s2 = _swa_scores(qs, kp, kc, bias_ref, n)
            pr1 = jnp.exp(s1 - lse)
            pr2 = jnp.exp(s2 - lse)
            dpr1 = _dot_nt(dos, vp)
            dpr2 = _dot_nt(dos, vc)
            delta = jnp.sum(pr1 * dpr1, axis=1, keepdims=True) + jnp.sum(pr2 * dpr2, axis=1, keepdims=True)
            ds1 = pr1 * (dpr1 - delta)
            ds2 = pr2 * (dpr2 - delta)
            dbias_ref[:, :, :BLOCK] += ds1.reshape(SW_GROUP, BLOCK, BLOCK)
            dbias_ref[:, :, BLOCK:] += ds2.reshape(SW_GROUP, BLOCK, BLOCK)
            ds1b, ds2b = ds1.astype(MXU), ds2.astype(MXU)
            dq = _dot(ds1b, kp) + _dot(ds2b, kc)
            dq_ref[cur, :] = (_unstack_heads(dq, g) * SCALE).astype(dq_ref.dtype)
            dk_acc[prev, :] += _dot_tn(ds1b, qs)
            dk_acc[cur, :] += _dot_tn(ds2b, qs)
            dv_acc[prev, :] += _dot_tn(pr1.astype(MXU), dos)
            dv_acc[cur, :] += _dot_tn(pr2.astype(MXU), dos)
            return dsink_rows - jnp.exp(sink - lse) * delta

        rows = lax.fori_loop(0, nb, step, jnp.zeros((GROUP_ROWS, 1), F32))
        for j in range(SW_GROUP):
            dsink_ref[j] = jnp.broadcast_to(jnp.sum(rows[j * BLOCK:(j + 1) * BLOCK], axis=0, keepdims=True),
                                            (1, LANES))

        @pl.when(g == SW_KV - 1)
        def _():
            dk_ref[...] = dk_acc[...].astype(dk_ref.dtype)
            dv_ref[...] = dv_acc[...].astype(dv_ref.dtype)

    grp = pl.BlockSpec((s, 2 * LANES), lambda g: (0, g))
    kv_out = pl.BlockSpec((s, LANES), lambda g: (0, 0))
    bspec = pl.BlockSpec((SW_GROUP, BLOCK, 2 * BLOCK), lambda g: (g, 0, 0))
    outs, couts = _call(
        body,
        grid=(SW_KV,),
        in_specs=[pl.BlockSpec(memory_space=pltpu.SMEM), *_swa_specs(s), bspec, grp,
                  pl.BlockSpec((None, s, LANES), lambda g: (g, 0, 0))],
        out_specs=[grp, kv_out, kv_out, bspec, pl.BlockSpec((SW_GROUP, 1, LANES), lambda g: (g, 0, 0))],
        out_shape=[jax.ShapeDtypeStruct((s, SW_W), MXU),
                   jax.ShapeDtypeStruct((s, LANES), MXU),
                   jax.ShapeDtypeStruct((s, LANES), MXU),
                   jax.ShapeDtypeStruct((SW_HEADS, BLOCK, 2 * BLOCK), F32),
                   jax.ShapeDtypeStruct((SW_HEADS, 1, LANES), F32)],
        scratch_shapes=[pltpu.VMEM((s, LANES), F32), pltpu.VMEM((s, LANES), F32)],
        args=(sinks, qkv, qkv, qkv, bias, do, lse), name=name, comm=comm)
    return tuple(outs) if comm is None else (tuple(outs), couts)


class _NoPlan:
    def comm(self, name):
        return None

    def done(self, name, outs):
        pass

    def grad(self, layer, name, value):
        pass

    def layer_done(self, layer):
        pass


def _run(plan, fn, *args, name, **kw):
    comm = plan.comm(name)
    if comm is None:
        return fn(*args, name=name, **kw)
    res, outs = fn(*args, name=name, comm=comm, **kw)
    plan.done(name, outs)
    return res


def _layer_fwd(x, p, w, g_mix, g_mlp, g_pe, sinks, bias, tag, plan):
    h1, r1 = _rms_fwd(x, g_mix, f"rms_mix_{tag}")
    qkv = _run(plan, _mm, h1, w["w_qkv"], out_dtypes=(MXU,), name=f"proj_qkv_{tag}")
    gates = _run(plan, _mm, h1, w["w_gate"], name=f"proj_gate_{tag}")
    oa = _run(plan, _sb_fwd, qkv, name=f"sb_fwd_{tag}")
    ob, lse = _run(plan, _swa_fwd, qkv, bias, sinks, name=f"swa_fwd_{tag}")
    merged = _mix_fwd(oa, ob, w["w_up_a"], w["w_up_b"], gates, f"mix_fwd_{tag}")
    x1 = _run(plan, _mm, merged, w["w_o"], extras=(x,), epi=lambda acc, res: res + acc, name=f"out_proj_{tag}")
    h2, r2 = _rms_fwd(x1, g_mlp, f"rms_mlp_{tag}")
    u, act = _run(plan, _mm, h2, w["w_ff1"], epi=lambda acc: (acc, jnp.square(jnp.maximum(acc, 0.0))),
                  out_dtypes=(MXU, MXU), name=f"ff1_{tag}")
    x2 = _run(plan, _mm, act, w["w_ff2"], extras=(x1,), epi=lambda acc, res: res + acc, name=f"ff2_{tag}")
    h3, r3 = _rms_fwd(x2, g_pe, f"rms_pe_{tag}")
    x3 = _ple(p, w["w_pe"], h3, w["w_pg"], x2, backward=False, name=f"ple_fwd_{tag}")
    saved = dict(x=x, h1=h1, r1=r1, gates=gates, qkv=qkv, lse=lse, oa=oa, ob=ob, merged=merged,
                 x1=x1, h2=h2, r2=r2, u=u, act=act, x2=x2, h3=h3, r3=r3)
    return x3, saved


def _layer_bwd(dx3, sv, p, w, g_mix, g_mlp, g_pe, sinks, bias, layer, plan):
    tag = f"l{layer}"
    gw = {}
    wire = (WIRE,)

    def dw(name, a, b):
        gw[name] = _run(plan, _mm, a, b, ta=True, out_dtypes=wire, name=f"d{name}_{tag}")
        plan.grad(layer, name, gw[name])

    dpe, dgt = _ple(p, w["w_pe"], sv["h3"], w["w_pg"], dx3, backward=True, name=f"ple_bwd_{tag}")
    dw("w_pe", p, dpe)
    dw("w_pg", sv["h3"], dgt)
    dh3 = _run(plan, _mm, dgt, w["w_pg"], tb=True, name=f"dh_pe_{tag}")
    dx2, dx2b, dg_pe = _rms_bwd(sv["x2"], sv["r3"], g_pe, dh3, dx3, f"rms_pe_bwd_{tag}")
    dw("w_ff2", sv["act"], dx2b)
    du = _run(plan, _mm, dx2b, w["w_ff2"], tb=True, extras=(sv["u"],),
              epi=lambda acc, u: acc * (2.0 * jnp.maximum(u.astype(F32), 0.0)), out_dtypes=(MXU,),
              name=f"dact_{tag}")
    dw("w_ff1", sv["h2"], du)
    dh2 = _run(plan, _mm, du, w["w_ff1"], tb=True, name=f"dh_mlp_{tag}")
    dx1, dx1b, dg_mlp = _rms_bwd(sv["x1"], sv["r2"], g_mlp, dh2, dx2, f"rms_mlp_bwd_{tag}")
    dw("w_o", sv["merged"], dx1b)
    dya, dyb, dga, dgb = _mix_bwd(dx1b, w["w_o"], sv["oa"], sv["ob"], w["w_up_a"], w["w_up_b"],
                                  sv["gates"], f"mix_bwd_{tag}")
    dw("w_up_a", sv["oa"], dya)
    dw("w_up_b", sv["ob"], dyb)
    doa = _run(plan, _mm, dya, w["w_up_a"], tb=True, out_dtypes=(MXU,), name=f"do_a_{tag}")
    dob = _run(plan, _mm, dyb, w["w_up_b"], tb=True, out_dtypes=(MXU,), name=f"do_b_{tag}")
    dqb, dkb, dvb, dbias, dsink = _run(plan, _swa_bwd, sv["qkv"], bias, sinks, dob, sv["lse"],
                                       name=f"swa_bwd_{tag}")
    dqa, dka, dva = _run(plan, _sb_bwd, sv["qkv"], doa, name=f"sb_bwd_{tag}")
    dqkv = jnp.concatenate([dqa, dka, dva, dqb, dkb, dvb], axis=1)
    gw_qkv = _mm(sv["h1"], dqkv, ta=True, out_dtypes=wire, name=f"dw_qkv_{tag}")
    gw_ga = _mm(sv["h1"], dga, ta=True, out_dtypes=wire, name=f"dw_ga_{tag}")
    gw_gb = _mm(sv["h1"], dgb, ta=True, out_dtypes=wire, name=f"dw_gb_{tag}")
    gw["w_in"] = jnp.concatenate([gw_qkv, gw_ga, gw_gb], axis=1)
    plan.grad(layer, "w_in", gw["w_in"])
    d = dga.shape[1]
    add = lambda acc, res: res + acc
    dh1 = _mm(dqkv, w["w_qkv"], tb=True, tk=768, name=f"dh_qkv_{tag}")
    dh1 = _mm(dga, w["w_gate"][:, :d], tb=True, extras=(dh1,), epi=add, name=f"dh_ga_{tag}")
    dh1 = _mm(dgb, w["w_gate"][:, d:], tb=True, extras=(dh1,), epi=add, name=f"dh_gb_{tag}")
    dx, _, dg_mix = _rms_bwd(sv["x"], sv["r1"], g_mix, dh1, dx1, f"rms_mix_bwd_{tag}")
    small = dict(g_mix=dg_mix, g_mlp=dg_mlp, g_pe=dg_pe, sinks=dsink[:, 0, 0], dbias=dbias)
    return dx, gw, small


def _local_step(x, p, target, weights, g_mix, g_mlp, g_pe, g_final, sinks, rel_bias, plan=None):
    plan = _NoPlan() if plan is None else plan
    depth = g_mix.shape[0]
    buckets = jnp.asarray(_bucket_table())
    bias = _build_bias(rel_bias, buckets, "build_bias")
    saved, wfull = [], []
    h = x
    for l in range(depth):
        wfull.append(weights(l))
        h, sv = _layer_fwd(h, p[l], wfull[l], g_mix[l:l + 1], g_mlp[l:l + 1], g_pe[l:l + 1],
                           sinks[l], bias, f"l{l}", plan)
        saved.append(sv)
    loss_row, dx, dg_final = _loss_head(h, g_final[None, :], target, "loss_head")
    gws = [None] * depth
    smalls = [None] * depth
    for l in reversed(range(depth)):
        dx, gws[l], smalls[l] = _layer_bwd(dx, saved[l], p[l], wfull[l], g_mix[l:l + 1], g_mlp[l:l + 1],
                                           g_pe[l:l + 1], sinks[l], bias, l, plan)
        plan.layer_done(l)
    drel = _bias_grad([sm["dbias"] for sm in smalls], buckets, "bias_grad")[:, 0, :N_BUCKETS].T
    small = dict(
        g_mix=jnp.concatenate([sm["g_mix"] for sm in smalls], axis=0),
        g_mlp=jnp.concatenate([sm["g_mlp"] for sm in smalls], axis=0),
        g_pe=jnp.concatenate([sm["g_pe"] for sm in smalls], axis=0),
        g_final=dg_final[0],
        sinks=jnp.stack([sm["sinks"] for sm in smalls], axis=0),
        rel_bias=drel,
    )
    return loss_row, dx, gws, small


MESH_ID = pl.DeviceIdType.MESH
ANY = pl.BlockSpec(memory_space=pl.ANY)


def _position():
    return lax.axis_index("x"), lax.axis_index("y"), lax.axis_index("c")


def _run_comm(comm, name):
    ci, co = len(comm.inputs), len(comm.out_shapes)

    def body(*refs):
        cin, cout, csem = refs[:ci], refs[ci:ci + co], refs[ci + co:]
        pos = _position()
        comm.start(pos, cin, cout, csem)
        comm.finish(pos, cin, cout, csem)

    return pl.pallas_call(body, out_shape=comm.out_shapes, in_specs=[ANY] * ci, out_specs=[ANY] * co,
                          scratch_shapes=comm.sems, name=name)(*comm.inputs)


def _gather_comm(shards):
    n = len(shards)

    def copies(pos, x_refs, out_refs, sems):
        send_sems, recv_sems, local_sems = sems
        x, y, c = pos
        me, sibling = (x, y, c), (x, y, 1 - c)
        chips = [(1 - x, y), (x, 1 - y), (1 - x, 1 - y)]

        def slot(a, px, py, pc):
            return out_refs[a].at[4 * px + 2 * py + pc]

        def copy(a, k, block, to, src=None):
            return pltpu.make_async_remote_copy(
                src_ref=slot(a, *block) if src is None else src, dst_ref=slot(a, *block),
                send_sem=send_sems.at[a, k], recv_sem=recv_sems.at[a, k],
                device_id=to, device_id_type=MESH_ID)

        mine = [pltpu.make_async_copy(x_refs[a], slot(a, *me), local_sems.at[a]) for a in range(n)]
        first = []
        for a in range(n):
            first.append(copy(a, 0, me, sibling, src=x_refs[a]))
            first += [copy(a, 1 + j, me, (*chip, c), src=x_refs[a]) for j, chip in enumerate(chips)]
        return me, sibling, chips, copy, mine, first

    def start(pos, x_refs, out_refs, sems):
        _, _, _, _, mine, first = copies(pos, x_refs, out_refs, sems)
        for cp in mine + first:
            cp.start()

    def finish(pos, x_refs, out_refs, sems):
        me, sibling, chips, copy, mine, first = copies(pos, x_refs, out_refs, sems)
        c = pos[2]
        passed = []
        for j, chip in enumerate(chips):
            for a in range(n):
                copy(a, 1 + j, (*chip, c), me).wait_recv()
                fwd = copy(a, 4 + j, (*chip, c), sibling)
                fwd.start()
                passed.append(fwd)
        for a in range(n):
            copy(a, 0, sibling, me).wait_recv()
            for j, chip in enumerate(chips):
                copy(a, 4 + j, (*chip, 1 - c), me).wait_recv()
        for cp in first + passed:
            cp.wait_send()
        for cp in mine:
            cp.wait()

    return _Comm(shards, [jax.ShapeDtypeStruct((N_DEV,) + s.shape, s.dtype) for s in shards],
                 [pltpu.SemaphoreType.DMA((n, 7)), pltpu.SemaphoreType.DMA((n, 7)),
                  pltpu.SemaphoreType.DMA((n,))], start, finish)


def _exchange_comm(arrays, n_slots, route):
    n = len(arrays)

    def copies(pos, in_refs, out_refs, sems):
        send_sems, recv_sems = sems
        out = []
        for a in range(n):
            for j in range(n_slots):
                src_slot, peer = route(pos, j)
                out.append(pltpu.make_async_remote_copy(
                    src_ref=in_refs[a].at[src_slot], dst_ref=out_refs[a].at[j],
                    send_sem=send_sems.at[a, j], recv_sem=recv_sems.at[a, j],
                    device_id=peer, device_id_type=MESH_ID))
        return out

    def start(pos, in_refs, out_refs, sems):
        for cp in copies(pos, in_refs, out_refs, sems):
            cp.start()

    def finish(pos, in_refs, out_refs, sems):
        for cp in copies(pos, in_refs, out_refs, sems):
            cp.wait()

    return _Comm(arrays, [jax.ShapeDtypeStruct((n_slots,) + g.shape[1:], g.dtype) for g in arrays],
                 [pltpu.SemaphoreType.DMA((n, n_slots)), pltpu.SemaphoreType.DMA((n, n_slots))], start, finish)


def _rs_sibling_comm(gs):
    return _exchange_comm(gs, 4, lambda pos, j: (2 * j + (1 - pos[2]), (pos[0], pos[1], 1 - pos[2])))


def _chip_of(k, x, y):
    return x ^ ((k + 1) & 1), y ^ (((k + 1) >> 1) & 1)


def _chip_partials(pos, gs, recvs, name):
    n = len(gs)

    def body(pos_ref, *refs):
        for a in range(n):
            refs[2 * n + a][...] = (refs[a][...].astype(F32) + refs[n + a][...].astype(F32)
                                    ).astype(refs[2 * n + a].dtype)

    def g_map(k, pos_ref):
        cx, cy = _chip_of(k, pos_ref[0], pos_ref[1])
        return (4 * cx + 2 * cy + pos_ref[2], 0, 0)

    def r_map(k, pos_ref):
        cx, cy = _chip_of(k, pos_ref[0], pos_ref[1])
        return (2 * cx + cy, 0, 0)

    slab = [(None,) + g.shape[1:] for g in gs]
    return pl.pallas_call(
        body,
        grid_spec=pltpu.PrefetchScalarGridSpec(
            num_scalar_prefetch=1,
            grid=(4,),
            in_specs=[pl.BlockSpec(sh, g_map) for sh in slab] + [pl.BlockSpec(sh, r_map) for sh in slab],
            out_specs=[pl.BlockSpec(sh, lambda k, pos_ref: (k, 0, 0)) for sh in slab],
        ),
        out_shape=[jax.ShapeDtypeStruct((4,) + g.shape[1:], g.dtype) for g in gs],
        compiler_params=_cparams(),
        name=name,
    )(pos, *gs, *recvs)


def _rs_chips_comm(parts):
    return _exchange_comm(parts, 3, lambda pos, k: (k, (*_chip_of(k, pos[0], pos[1]), pos[2])))


def _adamw_math(w, g, m, v):
    m = ADAM_B1 * m + (1.0 - ADAM_B1) * g
    v = ADAM_B2 * v + (1.0 - ADAM_B2) * (g * g)
    m_hat = m / (1.0 - ADAM_B1 ** ADAM_STEP)
    v_hat = v / (1.0 - ADAM_B2 ** ADAM_STEP)
    delta = -ADAM_LR * (m_hat / (jnp.sqrt(v_hat) + ADAM_EPS) + ADAM_WD * w)
    return delta, m, v


def _adamw_weight(parts, recvs, w, m, v, name):
    depth, a, b = w.shape
    ta = _tile(a, 256, unit=16)
    ni = a // ta

    def body(*refs):
        p_refs, r_refs = refs[:depth], refs[depth:2 * depth]
        w_ref, m_ref, v_ref = refs[2 * depth:2 * depth + 3]
        g_out, d_out, m_out, v_out = refs[2 * depth + 3:]
        layer = pl.program_id(0)
        g = jnp.zeros((ta, b), F32)
        for l in range(depth):
            gl = p_refs[l][...].astype(F32)
            for k in range(3):
                gl = gl + r_refs[l][k].astype(F32)
            g = jnp.where(layer == l, gl, g)
        delta, m_new, v_new = _adamw_math(w_ref[...], g, m_ref[...], v_ref[...])
        g_out[...] = g
        d_out[...] = delta
        m_out[...] = m_new
        v_out[...] = v_new

    def hold(l):
        return lambda layer, i: jnp.where(layer == l, i, jnp.where(layer < l, 0, ni - 1))

    p_specs = [pl.BlockSpec((None, ta, b), (lambda layer, i, f=hold(l): (3, f(layer, i), 0))) for l in range(depth)]
    r_specs = [pl.BlockSpec((3, ta, b), (lambda layer, i, f=hold(l): (0, f(layer, i), 0))) for l in range(depth)]
    row = pl.BlockSpec((None, ta, b), lambda layer, i: (layer, i, 0))
    return pl.pallas_call(
        body,
        grid=(depth, ni),
        in_specs=p_specs + r_specs + [row, row, row],
        out_specs=[row] * 4,
        out_shape=[jax.ShapeDtypeStruct(w.shape, F32)] * 4,
        compiler_params=_cparams(),
        name=name,
    )(*parts, *recvs, w, m, v)


def _adamw_replicated(gathered, w, m, v, name):
    r, lanes = w.shape

    def body(g_ref, w_ref, m_ref, v_ref, g_out, d_out, m_out, v_out):
        g = g_ref[0]
        for k in range(1, N_DEV):
            g = g + g_ref[k]
        delta, m_new, v_new = _adamw_math(w_ref[...], g, m_ref[...], v_ref[...])
        g_out[...] = g
        d_out[...] = delta
        m_out[...] = m_new
        v_out[...] = v_new

    return pl.pallas_call(
        body,
        out_shape=[jax.ShapeDtypeStruct((r, lanes), F32)] * 4,
        name=name,
    )(gathered, w, m, v)


def _full_weight(name, gathered):
    _, a, b = gathered.shape
    if name in COL_SHARDED:
        return gathered.transpose(1, 0, 2).reshape(a, N_DEV * b)
    return gathered.reshape(N_DEV * a, b)


def _to_slabs(name, gfull, shard_shape):
    a, b = shard_shape
    if name in COL_SHARDED:
        return gfull.reshape(a, N_DEV, b).transpose(1, 0, 2)
    return gfull.reshape(N_DEV, a, b)


def _pack_small(arrs):
    rows = []
    for a in arrs:
        flat = a.astype(F32).reshape(-1)
        pad = (-flat.shape[0]) % LANES
        rows.append(jnp.pad(flat, (0, pad)).reshape(-1, LANES))
    packed = jnp.concatenate(rows, axis=0)
    return jnp.pad(packed, ((0, (-packed.shape[0]) % 8), (0, 0)))


def _unpack_small(packed, shapes):
    out, off = [], 0
    for shp in shapes:
        n = math.prod(shp)
        rows = -(-n // LANES)
        out.append(packed[off:off + rows].reshape(-1)[:n].reshape(shp))
        off += rows
    return out


def _of(layer, *names):
    return tuple((layer, n) for n in names)


MIXER_W = ("w_in", "w_up_a", "w_up_b", "w_o")
MLP_W = ("w_ff1", "w_ff2", "w_pe", "w_pg")

GATHERS = (
    (None, _of(0, "w_in")),
    ("proj_qkv_l0", _of(0, "w_up_a", "w_up_b", "w_o")),
    ("sb_fwd_l0", _of(0, *MLP_W) + _of(1, "w_ff1")),
    ("swa_fwd_l0", _of(1, "w_in")),
    ("ff1_l0", _of(1, "w_ff2")),
    ("ff2_l0", _of(1, "w_up_a", "w_up_b", "w_o", "w_pe", "w_pg")),
)
REDUCES = (
    (_of(1, "w_ff1"), "dw_ff2_l0", "dact_l0"),
    (_of(1, "w_ff2"), "dw_ff2_l0", "dw_ff1_l0"),
    (_of(1, "w_in"), "dw_ff2_l0", "dh_mlp_l0"),
    (_of(1, "w_up_a", "w_up_b", "w_o", "w_pe", "w_pg"), "dw_ff2_l0", "swa_bwd_l0"),
    (_of(0, *MLP_W), "dh_mlp_l0", "sb_bwd_l0"),
    (_of(0, "w_o", "w_up_a", "w_up_b"), "do_a_l0", "sb_bwd_l0"),
    (_of(0, "w_in"), None, None),
)


def _merge_comms(comms):
    if len(comms) == 1:
        return comms[0]

    def cuts(counts):
        edges = [0]
        for c in counts:
            edges.append(edges[-1] + c)
        return [slice(a, b) for a, b in zip(edges[:-1], edges[1:])]

    s_in = cuts([len(c.inputs) for c in comms])
    s_out = cuts([len(c.out_shapes) for c in comms])
    s_sem = cuts([len(c.sems) for c in comms])

    def start(pos, cin, cout, csem):
        for c, i, o, s in zip(comms, s_in, s_out, s_sem):
            c.start(pos, cin[i], cout[o], csem[s])

    def finish(pos, cin, cout, csem):
        for c, i, o, s in zip(comms, s_in, s_out, s_sem):
            c.finish(pos, cin[i], cout[o], csem[s])

    return _Comm(sum([c.inputs for c in comms], []), sum([c.out_shapes for c in comms], []),
                 sum([c.sems for c in comms], []), start, finish)


class _LayerWeights:
    def __init__(self, full, layer):
        self.full, self.layer, self.cache = full, layer, {}

    def __getitem__(self, name):
        if name not in self.cache:
            if name == "w_qkv":
                self.cache[name] = self.full[(self.layer, "w_in")][:, :QKV_COLS]
            elif name == "w_gate":
                self.cache[name] = self.full[(self.layer, "w_in")][:, QKV_COLS:]
            else:
                self.cache[name] = self.full[(self.layer, name)]
        return self.cache[name]


class _Plan:
    def __init__(self, w_sh, pos):
        self.w_sh = dict(zip(WEIGHTS, w_sh))
        self.pos = pos
        self.full, self.gw, self.parts, self.recv = {}, {}, {}, {}
        self.slabs = {}
        self.hosted = {}
        for i, (host, _) in enumerate(GATHERS):
            if host is not None:
                self.hosted.setdefault(host, []).append(("gather", i))
        for i, (_, sib_host, chip_host) in enumerate(REDUCES):
            assert (sib_host is None) == (chip_host is None)
            if sib_host is not None:
                self.hosted.setdefault(sib_host, []).append(("sibling", i))
                self.hosted.setdefault(chip_host, []).append(("chips", i))

    def _gather(self, i):
        return _gather_comm([self.w_sh[n][layer].astype(WIRE) for layer, n in GATHERS[i][1]])

    def _gathered(self, i, outs):
        for (layer, n), g in zip(GATHERS[i][1], outs):
            self.full[(layer, n)] = _full_weight(n, g)

    def weights(self, layer):
        for i, (host, items) in enumerate(GATHERS):
            if host is None and items[0][0] == layer:
                self._gathered(i, _run_comm(self._gather(i), f"gather_{i}"))
        return _LayerWeights(self.full, layer)

    def grad(self, layer, name, value):
        self.gw[(layer, name)] = value

    def _sibling(self, i):
        self.slabs[i] = [_to_slabs(n, self.gw[(layer, n)], self.w_sh[n].shape[1:]) for layer, n in REDUCES[i][0]]
        return _rs_sibling_comm(self.slabs[i])

    def _sibling_done(self, i, outs):
        parts = _chip_partials(self.pos, self.slabs[i], outs, f"chip_partials_{i}")
        for item, part in zip(REDUCES[i][0], parts):
            self.parts[item] = part

    def _chips(self, i):
        return _rs_chips_comm([self.parts[item] for item in REDUCES[i][0]])

    def _chips_done(self, i, outs):
        for item, r in zip(REDUCES[i][0], outs):
            self.recv[item] = r

    def layer_done(self, layer):
        for i, (items, sib_host, _) in enumerate(REDUCES):
            if sib_host is None and items[0][0] == layer:
                self._sibling_done(i, _run_comm(self._sibling(i), f"reduce_sibling_{i}"))
                self._chips_done(i, _run_comm(self._chips(i), f"reduce_chips_{i}"))

    def comm(self, name):
        if name not in self.hosted:
            return None
        make = {"gather": self._gather, "sibling": self._sibling, "chips": self._chips}
        return _merge_comms([make[kind](i) for kind, i in self.hosted[name]])

    def done(self, name, outs):
        took = {"gather": self._gathered, "sibling": self._sibling_done, "chips": self._chips_done}
        off = 0
        for kind, i in self.hosted[name]:
            n = len(GATHERS[i][1]) if kind == "gather" else len(REDUCES[i][0])
            took[kind](i, outs[off:off + n])
            off += n


def kernel(x, p, w_in, w_up_a, w_up_b, w_o, w_ff1, w_ff2, w_pe, w_pg, g_mix, g_mlp, g_pe, g_final, sinks, rel_bias, loss_target, m_w_in, m_w_up_a, m_w_up_b, m_w_o, m_w_ff1, m_w_ff2, m_w_pe, m_w_pg, m_g_mix, m_g_mlp, m_g_pe, m_g_final, m_sinks, m_rel_bias, v_w_in, v_w_up_a, v_w_up_b, v_w_o, v_w_ff1, v_w_ff2, v_w_pe, v_w_pg, v_g_mix, v_g_mlp, v_g_pe, v_g_final, v_sinks, v_rel_bias):
    w_sh = [w_in, w_up_a, w_up_b, w_o, w_ff1, w_ff2, w_pe, w_pg]
    m_sh = [m_w_in, m_w_up_a, m_w_up_b, m_w_o, m_w_ff1, m_w_ff2, m_w_pe, m_w_pg]
    v_sh = [v_w_in, v_w_up_a, v_w_up_b, v_w_o, v_w_ff1, v_w_ff2, v_w_pe, v_w_pg]
    depth = w_in.shape[0]
    assert depth == 2 and x.shape[-1] * 2 + QKV_COLS == w_in.shape[2] * N_DEV

    px, py, pc = _position()
    plan = _Plan(w_sh, jnp.stack([px, py, pc]).astype(jnp.int32))
    loss_row, grad_x, _, small = _local_step(
        x[0], p[:, 0], loss_target[0], plan.weights, g_mix, g_mlp, g_pe, g_final, sinks, rel_bias, plan=plan)

    grad_w, delta_w, new_m, new_v = [], [], [], []
    for a, name in enumerate(WEIGHTS):
        outs = _adamw_weight([plan.parts[(l, name)] for l in range(depth)],
                             [plan.recv[(l, name)] for l in range(depth)],
                             w_sh[a], m_sh[a], v_sh[a], f"adamw_{name}")
        for lst, o in zip((grad_w, delta_w, new_m, new_v), outs):
            lst.append(o)

    small_w = [g_mix, g_mlp, g_pe, g_final, sinks, rel_bias]
    small_m = [m_g_mix, m_g_mlp, m_g_pe, m_g_final, m_sinks, m_rel_bias]
    small_v = [v_g_mix, v_g_mlp, v_g_pe, v_g_final, v_sinks, v_rel_bias]
    small_shapes = [a.shape for a in small_w] + [(1,)]
    zero = jnp.zeros((1,), F32)
    small_g = _pack_small([small[n] for n in SMALL] + [loss_row[0, :1]])
    small_all = _run_comm(_gather_comm([small_g]), "gather_small")[0]
    packed_s = _adamw_replicated(small_all, _pack_small(small_w + [zero]), _pack_small(small_m + [zero]),
                                 _pack_small(small_v + [zero + 1.0]), "adamw_replicated")
    sg, sd, sm, sv = [_unpack_small(t, small_shapes) for t in packed_s]
    loss = sg[-1][0]

    return (loss, grad_x[None], *grad_w, *sg[:-1], *delta_w, *sd[:-1], *new_m, *sm[:-1], *new_v, *sv[:-1])
```

```python
import functools
import math

import numpy as np
import jax
import jax.numpy as jnp
from jax import lax
from jax.experimental import pallas as pl
from jax.experimental.pallas import tpu as pltpu

F32 = jnp.float32
MXU = jnp.bfloat16
WIRE = jnp.bfloat16

HEAD_DIM = 64
SB_HEADS = 8
SW_HEADS = 8
SW_KV = 2
SW_GROUP = SW_HEADS // SW_KV
BLOCK = 128
N_BUCKETS = 32
MAX_DISTANCE = 128
EPS = 1e-6
SCALE = HEAD_DIM ** -0.5
SB_W = SB_HEADS * HEAD_DIM
SW_W = SW_HEADS * HEAD_DIM
QKV_COLS = 3 * SB_W + SW_W + 2 * SW_KV * HEAD_DIM
N_DEV = 8
LANES = 128
N_PAIR = SB_HEADS // 2
NEG = -1e30

ADAM_LR = 0.001
ADAM_B1 = 0.9
ADAM_B2 = 0.999
ADAM_EPS = 1e-08
ADAM_WD = 0.01
ADAM_STEP = 10

VMEM_LIMIT = 48 * 1024 * 1024
SB_TQ = 256
SB_DEAD = -105.0

WEIGHTS = ("w_in", "w_up_a", "w_up_b", "w_o", "w_ff1", "w_ff2", "w_pe", "w_pg")
COL_SHARDED = ("w_in", "w_up_a", "w_up_b", "w_ff1", "w_pe")
SMALL = ("g_mix", "g_mlp", "g_pe", "g_final", "sinks", "rel_bias")


def _cparams(**kw):
    return pltpu.CompilerParams(vmem_limit_bytes=VMEM_LIMIT, **kw)


def _dot(a, b):
    return jnp.dot(a, b, preferred_element_type=F32)


def _dot_nt(a, b):
    return lax.dot_general(a, b, (((1,), (1,)), ((), ())), preferred_element_type=F32)


def _dot_tn(a, b):
    return lax.dot_general(a, b, (((0,), (0,)), ((), ())), preferred_element_type=F32)


def _tile(n, target, unit=LANES):
    if n <= target:
        return n
    t = (target // unit) * unit
    while t > unit and n % t:
        t -= unit
    assert n % t == 0, (n, target)
    return t


def _sigmoid(x):
    return 1.0 / (1.0 + jnp.exp(-x))


class _Comm:
    def __init__(self, inputs, out_shapes, sems, start, finish):
        self.inputs, self.out_shapes, self.sems = list(inputs), list(out_shapes), list(sems)
        self.start, self.finish = start, finish


def _call(body, *, grid, in_specs, out_specs, out_shape, scratch_shapes=(), args, name, comm=None):
    n_in, n_out, n_scr = len(in_specs), len(out_shape), len(scratch_shapes)
    if comm is None:
        outs = pl.pallas_call(body, grid=grid, in_specs=list(in_specs), out_specs=list(out_specs),
                              out_shape=list(out_shape), scratch_shapes=list(scratch_shapes),
                              compiler_params=_cparams(), name=name)(*args)
        return list(outs), None
    ci, co = len(comm.inputs), len(comm.out_shapes)
    any_spec = pl.BlockSpec(memory_space=pl.ANY)

    def wrapped(*refs):
        ins, cin = refs[:n_in], refs[n_in:n_in + ci]
        o0 = n_in + ci
        outs, cout = refs[o0:o0 + n_out], refs[o0 + n_out:o0 + n_out + co]
        s0 = o0 + n_out + co
        scr, csem = refs[s0:s0 + n_scr], refs[s0 + n_scr:]
        ids = [pl.program_id(d) for d in range(len(grid))]
        first = functools.reduce(jnp.logical_and, [i == 0 for i in ids])
        last = functools.reduce(jnp.logical_and, [i == g - 1 for i, g in zip(ids, grid)])
        pos = (lax.axis_index("x"), lax.axis_index("y"), lax.axis_index("c"))

        @pl.when(first)
        def _():
            comm.start(pos, cin, cout, csem)

        body(*ins, *outs, *scr)

        @pl.when(last)
        def _():
            comm.finish(pos, cin, cout, csem)

    outs = pl.pallas_call(wrapped, grid=grid, in_specs=list(in_specs) + [any_spec] * ci,
                          out_specs=list(out_specs) + [any_spec] * co,
                          out_shape=list(out_shape) + comm.out_shapes,
                          scratch_shapes=list(scratch_shapes) + comm.sems,
                          compiler_params=_cparams(), name=name)(*args, *comm.inputs)
    return list(outs[:n_out]), list(outs[n_out:])


def _mm(a, b, *, ta=False, tb=False, extras=(), epi=None, out_dtypes=(F32,),
        tm=1024, tn=1024, tk=1024, name, comm=None):
    if ta:
        kdim, m = a.shape
    else:
        m, kdim = a.shape
    n = b.shape[0] if tb else b.shape[1]
    assert (b.shape[1] if tb else b.shape[0]) == kdim
    tm, tn, tk = _tile(m, tm), _tile(n, tn), _tile(kdim, tk)
    nk = kdim // tk
    n_ex, n_out = len(extras), len(out_dtypes)

    a_spec = (pl.BlockSpec((tk, tm), lambda i, j, k: (k, i)) if ta
              else pl.BlockSpec((tm, tk), lambda i, j, k: (i, k)))
    b_spec = (pl.BlockSpec((tn, tk), lambda i, j, k: (j, k)) if tb
              else pl.BlockSpec((tk, tn), lambda i, j, k: (k, j)))
    ex_specs = []
    for e in extras:
        assert e.shape == (m, n), (e.shape, m, n)
        ex_specs.append(pl.BlockSpec((tm, tn), lambda i, j, k: (i, j)))
    out_spec = pl.BlockSpec((tm, tn), lambda i, j, k: (i, j))

    def body(a_ref, b_ref, *rest):
        ex_refs = rest[:n_ex]
        out_refs = rest[n_ex:n_ex + n_out]
        acc = rest[-1]
        k = pl.program_id(2)

        @pl.when(k == 0)
        def _():
            acc[...] = jnp.zeros_like(acc)

        av = a_ref[...].astype(MXU)
        bv = b_ref[...].astype(MXU)
        if ta:
            acc[...] += _dot_tn(av, bv)
        elif tb:
            acc[...] += _dot_nt(av, bv)
        else:
            acc[...] += _dot(av, bv)

        @pl.when(k == nk - 1)
        def _():
            res = acc[...]
            if epi is not None:
                res = epi(res, *[e[...] for e in ex_refs])
            if not isinstance(res, tuple):
                res = (res,)
            for o_ref, r in zip(out_refs, res):
                o_ref[...] = r.astype(o_ref.dtype)

    outs, couts = _call(
        body,
        grid=(m // tm, n // tn, nk),
        in_specs=[a_spec, b_spec] + ex_specs,
        out_specs=[out_spec] * n_out,
        out_shape=[jax.ShapeDtypeStruct((m, n), dt) for dt in out_dtypes],
        scratch_shapes=[pltpu.VMEM((tm, tn), F32)],
        args=(a, b, *extras), name=name, comm=comm)
    res = outs[0] if n_out == 1 else tuple(outs)
    return res if comm is None else (res, couts)


def _rms_fwd(x, g, name):
    s, d = x.shape
    tr = _tile(s, 256)

    def body(x_ref, g_ref, h_ref, r_ref):
        xf = x_ref[...]
        r = lax.rsqrt(jnp.mean(xf * xf, axis=-1, keepdims=True) + EPS)
        h_ref[...] = ((xf * r) * g_ref[...]).astype(h_ref.dtype)
        r_ref[...] = r

    return pl.pallas_call(
        body,
        grid=(s // tr,),
        in_specs=[pl.BlockSpec((tr, d), lambda i: (i, 0)), pl.BlockSpec((1, d), lambda i: (0, 0))],
        out_specs=[pl.BlockSpec((tr, d), lambda i: (i, 0)), pl.BlockSpec((tr, 1), lambda i: (i, 0))],
        out_shape=[jax.ShapeDtypeStruct((s, d), MXU), jax.ShapeDtypeStruct((s, 1), F32)],
        compiler_params=_cparams(),
        name=name,
    )(x, g)


def _rms_bwd(x, r, g, dh, dres, name):
    s, d = x.shape
    tr = _tile(s, 256)

    def body(x_ref, r_ref, g_ref, dh_ref, dres_ref, dx_ref, dxb_ref, dg_ref):
        @pl.when(pl.program_id(0) == 0)
        def _():
            dg_ref[...] = jnp.zeros_like(dg_ref)

        rr = r_ref[...]
        xhat = x_ref[...] * rr
        dh_v = dh_ref[...]
        dxhat = dh_v * g_ref[...]
        mean = jnp.mean(dxhat * xhat, axis=-1, keepdims=True)
        dx = dres_ref[...] + rr * (dxhat - xhat * mean)
        dx_ref[...] = dx
        dxb_ref[...] = dx.astype(dxb_ref.dtype)
        dg_ref[...] += jnp.sum(dh_v * xhat, axis=0, keepdims=True)

    row = pl.BlockSpec((tr, d), lambda i: (i, 0))
    vec = pl.BlockSpec((1, d), lambda i: (0, 0))
    return pl.pallas_call(
        body,
        grid=(s // tr,),
        in_specs=[row, pl.BlockSpec((tr, 1), lambda i: (i, 0)), vec, row, row],
        out_specs=[row, row, vec],
        out_shape=[jax.ShapeDtypeStruct((s, d), F32), jax.ShapeDtypeStruct((s, d), MXU),
                   jax.ShapeDtypeStruct((1, d), F32)],
        compiler_params=_cparams(),
        name=name,
    )(x, r, g, dh, dres)


def _loss_head(x, g, target, name):
    s, d = x.shape
    tr = _tile(s, 256)

    def body(x_ref, g_ref, t_ref, loss_ref, dx_ref, dg_ref):
        @pl.when(pl.program_id(0) == 0)
        def _():
            dg_ref[...] = jnp.zeros_like(dg_ref)
            loss_ref[...] = jnp.zeros_like(loss_ref)

        xf = x_ref[...]
        gv = g_ref[...]
        r = lax.rsqrt(jnp.mean(xf * xf, axis=-1, keepdims=True) + EPS)
        xhat = xf * r
        err = xhat * gv - t_ref[...]
        loss_ref[...] += 0.5 * jnp.sum(jnp.mean(err * err, axis=-1, keepdims=True), axis=0, keepdims=True)
        dy = err * (1.0 / d)
        dxhat = dy * gv
        mean = jnp.mean(dxhat * xhat, axis=-1, keepdims=True)
        dx_ref[...] = r * (dxhat - xhat * mean)
        dg_ref[...] += jnp.sum(dy * xhat, axis=0, keepdims=True)

    row = pl.BlockSpec((tr, d), lambda i: (i, 0))
    vec = pl.BlockSpec((1, d), lambda i: (0, 0))
    return pl.pallas_call(
        body,
        grid=(s // tr,),
        in_specs=[row, vec, row],
        out_specs=[pl.BlockSpec((1, LANES), lambda i: (0, 0)), row, vec],
        out_shape=[jax.ShapeDtypeStruct((1, LANES), F32), jax.ShapeDtypeStruct((s, d), F32),
                   jax.ShapeDtypeStruct((1, d), F32)],
        compiler_params=_cparams(),
        name=name,
    )(x, g, target)


def _mix_fwd(oa, ob, wa, wb, gates, name):
    s, kd = oa.shape
    d = wa.shape[1]
    tm, tn = _tile(s, 1024), _tile(d, 512)
    nj = d // tn

    def body(oa_ref, ob_ref, wa_ref, wb_ref, ga_ref, gb_ref, out_ref):
        ya = _dot(oa_ref[...], wa_ref[...])
        yb = _dot(ob_ref[...], wb_ref[...])
        out_ref[...] = (_sigmoid(ga_ref[...]) * ya + _sigmoid(gb_ref[...]) * yb).astype(out_ref.dtype)

    o_spec = pl.BlockSpec((tm, kd), lambda i, j: (i, 0))
    w_spec = pl.BlockSpec((kd, tn), lambda i, j: (0, j))
    return pl.pallas_call(
        body,
        grid=(s // tm, nj),
        in_specs=[o_spec, o_spec, w_spec, w_spec,
                  pl.BlockSpec((tm, tn), lambda i, j: (i, j)),
                  pl.BlockSpec((tm, tn), lambda i, j: (i, j + nj))],
        out_specs=pl.BlockSpec((tm, tn), lambda i, j: (i, j)),
        out_shape=jax.ShapeDtypeStruct((s, d), MXU),
        compiler_params=_cparams(),
        name=name,
    )(oa, ob, wa, wb, gates, gates)


def _mix_bwd(dx, w_o, oa, ob, wa, wb, gates, name):
    s, kd = oa.shape
    d = wa.shape[1]
    tm, tn = _tile(s, 1024), _tile(d, 512)
    nj = d // tn

    def body(dx_ref, wo_ref, oa_ref, ob_ref, wa_ref, wb_ref, ga_ref, gb_ref,
             dya_ref, dyb_ref, dga_ref, dgb_ref):
        dm = _dot_nt(dx_ref[...], wo_ref[...])
        ya = _dot(oa_ref[...], wa_ref[...])
        yb = _dot(ob_ref[...], wb_ref[...])
        sa = _sigmoid(ga_ref[...])
        sb = _sigmoid(gb_ref[...])
        dya_ref[...] = (dm * sa).astype(dya_ref.dtype)
        dyb_ref[...] = (dm * sb).astype(dyb_ref.dtype)
        dga_ref[...] = (dm * ya * sa * (1.0 - sa)).astype(dga_ref.dtype)
        dgb_ref[...] = (dm * yb * sb * (1.0 - sb)).astype(dgb_ref.dtype)

    o_spec = pl.BlockSpec((tm, kd), lambda i, j: (i, 0))
    w_spec = pl.BlockSpec((kd, tn), lambda i, j: (0, j))
    t_spec = pl.BlockSpec((tm, tn), lambda i, j: (i, j))
    return pl.pallas_call(
        body,
        grid=(s // tm, nj),
        in_specs=[pl.BlockSpec((tm, d), lambda i, j: (i, 0)),
                  pl.BlockSpec((tn, d), lambda i, j: (j, 0)),
                  o_spec, o_spec, w_spec, w_spec, t_spec,
                  pl.BlockSpec((tm, tn), lambda i, j: (i, j + nj))],
        out_specs=[t_spec] * 4,
        out_shape=[jax.ShapeDtypeStruct((s, d), MXU)] * 4,
        compiler_params=_cparams(),
        name=name,
    )(dx, w_o, oa, ob, wa, wb, gates, gates)


def _ple(p, w_pe, h, w_pg, other, *, backward, name):
    s, kp = p.shape
    d = w_pe.shape[1]
    tm, tn = _tile(s, 1024), _tile(d, 512)

    def body(p_ref, wpe_ref, h_ref, wpg_ref, other_ref, *out_refs):
        pe = _dot(p_ref[...].astype(MXU), wpe_ref[...])
        gt = _dot(h_ref[...], wpg_ref[...])
        sg = _sigmoid(gt)
        if backward:
            dout = other_ref[...]
            out_refs[0][...] = (dout * sg).astype(out_refs[0].dtype)
            out_refs[1][...] = (dout * pe * sg * (1.0 - sg)).astype(out_refs[1].dtype)
        else:
            out_refs[0][...] = other_ref[...] + pe * sg

    t_spec = pl.BlockSpec((tm, tn), lambda i, j: (i, j))
    if backward:
        out_specs, out_shape = [t_spec, t_spec], [jax.ShapeDtypeStruct((s, d), MXU)] * 2
    else:
        out_specs, out_shape = [t_spec], [jax.ShapeDtypeStruct((s, d), F32)]
    outs = pl.pallas_call(
        body,
        grid=(s // tm, d // tn),
        in_specs=[pl.BlockSpec((tm, kp), lambda i, j: (i, 0)),
                  pl.BlockSpec((kp, tn), lambda i, j: (0, j)),
                  pl.BlockSpec((tm, d), lambda i, j: (i, 0)),
                  pl.BlockSpec((d, tn), lambda i, j: (0, j)),
                  t_spec],
        out_specs=out_specs,
        out_shape=out_shape,
        compiler_params=_cparams(),
        name=name,
    )(p, w_pe, h, w_pg, other)
    return tuple(outs) if backward else outs[0]


def _split_dot(x, tri):
    hi = x.astype(jnp.bfloat16)
    lo = (x - hi.astype(F32)).astype(jnp.bfloat16)
    return _dot(hi, tri) + _dot(lo, tri)


def _log_sigmoids(z):
    t = jnp.log1p(jnp.exp(-jnp.abs(z)))
    return jnp.minimum(z, 0.0) - t, jnp.minimum(-z, 0.0) - t


def _head_lanes(hh):
    lane = lax.broadcasted_iota(jnp.int32, (1, LANES), 1)
    return jnp.logical_and(lane >= hh * HEAD_DIM, lane < (hh + 1) * HEAD_DIM)


def _sb_fwd(qkv, name, comm=None):
    s = qkv.shape[0]
    tq = _tile(s, SB_TQ)

    def body(q_ref, k_ref, v_ref, o_ref):
        i = pl.program_id(1)
        qf = q_ref[...].astype(F32) * SCALE
        row = lax.broadcasted_iota(jnp.int32, (tq, tq), 0)
        col = lax.broadcasted_iota(jnp.int32, (tq, tq), 1)
        causal = col < row
        tri = jnp.where(row > col, 1.0, 0.0).astype(jnp.bfloat16)

        qms = [jnp.where(_head_lanes(hh), qf, 0.0).astype(MXU) for hh in range(2)]

        def block(kb, cs, accs, masked, gate=None):
            rows = pl.ds(pl.multiple_of(kb * tq, tq), tq)
            ks, vs = k_ref[rows, :], v_ref[rows, :]
            new_c, new_acc = [], []
            for hh in range(2):
                lb, lm = _log_sigmoids(_dot_nt(qms[hh], ks))
                if masked:
                    lm = jnp.where(causal, lm, 0.0)
                if gate is not None:
                    lm = lm * gate
                a = jnp.exp(lb + _split_dot(lm, tri) + cs[hh])
                if masked:
                    a = jnp.where(causal, a, 0.0)
                if gate is not None:
                    a = a * gate
                new_acc.append(accs[hh] + _dot(a.astype(MXU), vs))
                new_c.append(cs[hh] + jnp.sum(lm, axis=1, keepdims=True))
            return tuple(new_c), tuple(new_acc)

        def top(cs):
            return jnp.maximum(jnp.max(cs[0]), jnp.max(cs[1]))

        zc, za = jnp.zeros((tq, 1), F32), jnp.zeros((tq, LANES), F32)
        cs, accs = block(i, (zc, zc), (za, za), True)
        cs, accs = block(jnp.maximum(i - 1, 0), cs, accs, False, jnp.where(i > 0, 1.0, 0.0))

        def live(st):
            return jnp.logical_and(st[0] >= 0, st[1] > SB_DEAD)

        def walk(st):
            cs, accs = block(st[0], st[2], st[3], False)
            return st[0] - 1, top(cs), cs, accs

        accs = lax.while_loop(live, walk, (i - 2, top(cs), cs, accs))[3]
        o_ref[...] = jnp.where(_head_lanes(0), accs[0], accs[1]).astype(o_ref.dtype)

    outs, couts = _call(
        body,
        grid=(N_PAIR, s // tq),
        in_specs=[pl.BlockSpec((tq, LANES), lambda p, i: (i, p)),
                  pl.BlockSpec((s, LANES), lambda p, i: (0, N_PAIR + p)),
                  pl.BlockSpec((s, LANES), lambda p, i: (0, 2 * N_PAIR + p))],
        out_specs=[pl.BlockSpec((tq, LANES), lambda p, i: (i, p))],
        out_shape=[jax.ShapeDtypeStruct((s, SB_W), MXU)],
        args=(qkv, qkv, qkv), name=name, comm=comm)
    return outs[0] if comm is None else (outs[0], couts)


def _sb_bwd(qkv, do, name, comm=None):
    s = qkv.shape[0]
    tq = _tile(s, SB_TQ)
    nq = s // tq

    def body(q_ref, k_ref, v_ref, do_ref, dq_ref, dk_ref, dv_ref, dk_acc, dv_acc, carries):
        i = pl.program_id(1)

        @pl.when(i == 0)
        def _():
            dk_acc[...] = jnp.zeros_like(dk_acc)
            dv_acc[...] = jnp.zeros_like(dv_acc)

        qf = q_ref[...].astype(F32) * SCALE
        dof = do_ref[...]
        row = lax.broadcasted_iota(jnp.int32, (tq, tq), 0)
        col = lax.broadcasted_iota(jnp.int32, (tq, tq), 1)
        causal = col < row
        tri_rev = jnp.where(row > col, 1.0, 0.0).astype(jnp.bfloat16)
        tri_excl = jnp.where(row < col, 1.0, 0.0).astype(jnp.bfloat16)

        qms = [jnp.where(_head_lanes(hh), qf, 0.0).astype(MXU) for hh in range(2)]
        doms = [jnp.where(_head_lanes(hh), dof, jnp.zeros_like(dof)) for hh in range(2)]

        def terms(kb, masked):
            rows = pl.ds(pl.multiple_of(kb * tq, tq), tq)
            ks = k_ref[rows, :]
            out = []
            for hh in range(2):
                lb, lm = _log_sigmoids(_dot_nt(qms[hh], ks))
                if masked:
                    lm = jnp.where(causal, lm, 0.0)
                out.append((lb, lm))
            return out

        def row_sums(kb, masked, pre=None):
            pre = terms(kb, masked) if pre is None else pre
            return [jnp.sum(lm, axis=1, keepdims=True) for _, lm in pre]

        def top(cs):
            return jnp.maximum(jnp.max(cs[0]), jnp.max(cs[1]))

        def live(st):
            return jnp.logical_and(st[0] >= 0, st[1] > SB_DEAD)

        def record(st):
            kb, cs = st[0], st[2]
            sums = row_sums(kb, False)
            for hh in range(2):
                carries[hh, kb] = cs[hh]
            cs = tuple(cs[hh] + sums[hh] for hh in range(2))
            return kb - 1, top(cs), cs

        prev = jnp.maximum(i - 1, 0)
        gate = jnp.where(i > 0, 1.0, 0.0)
        t_diag, t_prev = terms(i, True), terms(prev, False)
        c_diag = row_sums(i, True, t_diag)
        sums = row_sums(prev, False, t_prev)
        c_prev = tuple(c_diag[hh] + sums[hh] * gate for hh in range(2))
        first = lax.while_loop(live, record, (i - 2, top(c_prev), c_prev))[0] + 1

        def block(kb, cs, gpres, dqs, masked, gate=None, pre=None):
            rows = pl.ds(pl.multiple_of(kb * tq, tq), tq)
            ks, vs = k_ref[rows, :], v_ref[rows, :]
            pre = terms(kb, masked) if pre is None else pre
            new_g, new_dq = [], []
            dk_add, dv_add = None, None
            for hh in range(2):
                lb, lm = pre[hh]
                a = jnp.exp(lb + _split_dot(lm, tri_rev) + cs[hh])
                if masked:
                    a = jnp.where(causal, a, 0.0)
                if gate is not None:
                    a = a * gate
                g = a * _dot_nt(doms[hh], vs)
                gsum = gpres[hh] + _split_dot(g, tri_excl)
                dz = g - (g + gsum) * jnp.exp(lb)
                if masked:
                    dz = jnp.where(causal, dz, 0.0)
                if gate is not None:
                    dz = dz * gate
                dzb = dz.astype(MXU)
                new_dq.append(dqs[hh] + _dot(dzb, ks))
                dk_h = _dot_tn(dzb, qms[hh])
                dv_h = _dot_tn(a.astype(MXU), doms[hh])
                dk_add = dk_h if dk_add is None else dk_add + dk_h
                dv_add = dv_h if dv_add is None else dv_add + dv_h
                new_g.append(gpres[hh] + jnp.sum(g, axis=1, keepdims=True))
            dk_acc[rows, :] += dk_add
            dv_acc[rows, :] += dv_add
            return tuple(new_g), tuple(new_dq)

        zc, za = jnp.zeros((tq, 1), F32), jnp.zeros((tq, LANES), F32)
        gpres, dqs = lax.fori_loop(
            first, i - 1, lambda kb, cr: block(kb, (carries[0, kb], carries[1, kb]), cr[0], cr[1], False),
            ((zc, zc), (za, za)))
        gpres, dqs = block(prev, c_diag, gpres, dqs, False, gate, t_prev)
        dqs = block(i, (zc, zc), gpres, dqs, True, None, t_diag)[1]
        dq_ref[...] = (jnp.where(_head_lanes(0), dqs[0], dqs[1]) * SCALE).astype(dq_ref.dtype)

        @pl.when(i == nq - 1)
        def _():
            dk_ref[...] = dk_acc[...].astype(dk_ref.dtype)
            dv_ref[...] = dv_acc[...].astype(dv_ref.dtype)

    blk = pl.BlockSpec((tq, LANES), lambda p, i: (i, p))
    full = pl.BlockSpec((s, LANES), lambda p, i: (0, p))
    outs, couts = _call(
        body,
        grid=(N_PAIR, nq),
        in_specs=[blk,
                  pl.BlockSpec((s, LANES), lambda p, i: (0, N_PAIR + p)),
                  pl.BlockSpec((s, LANES), lambda p, i: (0, 2 * N_PAIR + p)),
                  blk],
        out_specs=[blk, full, full],
        out_shape=[jax.ShapeDtypeStruct((s, SB_W), MXU)] * 3,
        scratch_shapes=[pltpu.VMEM((s, LANES), F32), pltpu.VMEM((s, LANES), F32),
                        pltpu.VMEM((2, nq, tq, 1), F32)],
        args=(qkv, qkv, qkv, do), name=name, comm=comm)
    return tuple(outs) if comm is None else (tuple(outs), couts)


def _bucket_table():
    i = np.arange(BLOCK)[:, None]
    j = np.arange(2 * BLOCK)[None, :]
    d = np.maximum(BLOCK + i - j, 0)
    max_exact = N_BUCKETS // 2
    df = np.maximum(d, 1).astype(np.float32)
    large = max_exact + (np.log(df / max_exact) / math.log(MAX_DISTANCE / max_exact)
                         * (N_BUCKETS - max_exact)).astype(np.int32)
    large = np.minimum(large, N_BUCKETS - 1)
    return np.where(d < max_exact, d, large).astype(np.int32)


def _build_bias(rel_bias, buckets, name):
    def body(rb_ref, bk_ref, out_ref):
        h = pl.program_id(0)
        bk = bk_ref[...]
        acc = jnp.zeros(bk.shape, F32)
        for b in range(N_BUCKETS):
            acc = jnp.where(bk == b, rb_ref[b, h], acc)
        out_ref[...] = acc

    return pl.pallas_call(
        body,
        grid=(SW_HEADS,),
        in_specs=[pl.BlockSpec(memory_space=pltpu.SMEM),
                  pl.BlockSpec((BLOCK, 2 * BLOCK), lambda h: (0, 0))],
        out_specs=pl.BlockSpec((None, BLOCK, 2 * BLOCK), lambda h: (h, 0, 0)),
        out_shape=jax.ShapeDtypeStruct((SW_HEADS, BLOCK, 2 * BLOCK), F32),
        name=name,
    )(rel_bias, buckets)


def _bias_grad(dbias_layers, buckets, name):
    n_l = len(dbias_layers)

    def body(*refs):
        bk = refs[n_l][...]
        out_ref = refs[n_l + 1]
        db = refs[0][...]
        for r in refs[1:n_l]:
            db = db + r[...]
        lane = lax.broadcasted_iota(jnp.int32, (1, LANES), 1)
        acc = jnp.zeros((1, LANES), F32)
        for b in range(N_BUCKETS):
            part = jnp.sum(jnp.where(bk == b, db, 0.0), axis=1, keepdims=True)
            tot = jnp.sum(part, axis=0, keepdims=True)
            acc = jnp.where(lane == b, tot, acc)
        out_ref[...] = acc

    hspec = pl.BlockSpec((None, BLOCK, 2 * BLOCK), lambda h: (h, 0, 0))
    return pl.pallas_call(
        body,
        grid=(SW_HEADS,),
        in_specs=[hspec] * n_l + [pl.BlockSpec((BLOCK, 2 * BLOCK), lambda h: (0, 0))],
        out_specs=pl.BlockSpec((None, 1, LANES), lambda h: (h, 0, 0)),
        out_shape=jax.ShapeDtypeStruct((SW_HEADS, 1, LANES), F32),
        name=name,
    )(*dbias_layers, buckets)


GROUP_ROWS = SW_GROUP * BLOCK


def _group_lanes(g):
    lane = lax.broadcasted_iota(jnp.int32, (1, LANES), 1)
    gvec = jnp.zeros((1, LANES), jnp.int32) + g
    return jnp.where(lane >= HEAD_DIM, 1, 0) == gvec, gvec


def _stack_heads(x, g):
    kv_lanes, gvec = _group_lanes(g)
    parts = []
    for j in range(SW_GROUP):
        half = x[:, (j // 2) * LANES:(j // 2 + 1) * LANES]
        moved = jnp.where(gvec == j % 2, half, pltpu.roll(half, HEAD_DIM, 1))
        parts.append(jnp.where(kv_lanes, moved, 0.0))
    return jnp.concatenate(parts, axis=0)


def _unstack_heads(y, g):
    _, gvec = _group_lanes(g)
    heads = []
    for j in range(SW_GROUP):
        yj = y[j * BLOCK:(j + 1) * BLOCK]
        heads.append(jnp.where(gvec == j % 2, yj, pltpu.roll(yj, HEAD_DIM, 1)))
    pairs = [jnp.where(_head_lanes(0), heads[2 * p], heads[2 * p + 1]) for p in range(SW_GROUP // 2)]
    return jnp.concatenate(pairs, axis=1)


def _per_head_col(values):
    return jnp.concatenate([jnp.zeros((BLOCK, 1), F32) + v for v in values], axis=0)


def _swa_scores(qs, kp, kc, bias_ref, n):
    row = jnp.bitwise_and(lax.broadcasted_iota(jnp.int32, (GROUP_ROWS, BLOCK), 0), BLOCK - 1)
    col = lax.broadcasted_iota(jnp.int32, (GROUP_ROWS, BLOCK), 1)
    bias = bias_ref[...].reshape(GROUP_ROWS, 2 * BLOCK)
    s1 = _dot_nt(qs, kp) + bias[:, :BLOCK]
    s2 = _dot_nt(qs, kc) + bias[:, BLOCK:]
    no_prev = jnp.where(n > 0, 0, BLOCK)
    s1 = jnp.where(col > row + no_prev, s1, NEG)
    s2 = jnp.where(col <= row, s2, NEG)
    return s1, s2


def _swa_specs(s):
    q_blk = 3 * SB_W // (2 * LANES)
    k_blk = (3 * SB_W + SW_W) // LANES
    return (pl.BlockSpec((s, 2 * LANES), lambda g: (0, q_blk + g)),
            pl.BlockSpec((s, LANES), lambda g: (0, k_blk)),
            pl.BlockSpec((s, LANES), lambda g: (0, k_blk + 1)))


def _swa_fwd(qkv, bias, sinks, name, comm=None):
    s = qkv.shape[0]
    nb = s // BLOCK

    def body(sink_ref, q_ref, k_ref, v_ref, bias_ref, o_ref, lse_ref):
        g = pl.program_id(0)
        sink = _per_head_col([sink_ref[SW_GROUP * g + j] for j in range(SW_GROUP)])
        lane = lax.broadcasted_iota(jnp.int32, (1, LANES), 1)

        def step(n, carry):
            r0 = pl.multiple_of(n * BLOCK, BLOCK)
            p0 = pl.multiple_of(jnp.maximum(n - 1, 0) * BLOCK, BLOCK)
            cur, prev = pl.ds(r0, BLOCK), pl.ds(p0, BLOCK)
            qs = _stack_heads(q_ref[cur, :].astype(F32) * SCALE, g).astype(MXU)
            s1, s2 = _swa_scores(qs, k_ref[prev, :], k_ref[cur, :], bias_ref, n)
            m = jnp.maximum(jnp.maximum(jnp.max(s1, axis=1, keepdims=True),
                                        jnp.max(s2, axis=1, keepdims=True)), sink)
            e1 = jnp.exp(s1 - m)
            e2 = jnp.exp(s2 - m)
            den = jnp.sum(e1, axis=1, keepdims=True) + jnp.sum(e2, axis=1, keepdims=True) + jnp.exp(sink - m)
            o = _dot((e1 / den).astype(MXU), v_ref[prev, :]) + _dot((e2 / den).astype(MXU), v_ref[cur, :])
            o_ref[cur, :] = _unstack_heads(o, g).astype(o_ref.dtype)
            lse = m + jnp.log(den)
            lse_row = jnp.zeros((BLOCK, LANES), F32)
            for j in range(SW_GROUP):
                lse_row = jnp.where(lane == j, lse[j * BLOCK:(j + 1) * BLOCK], lse_row)
            lse_ref[cur, :] = lse_row
            return carry

        lax.fori_loop(0, nb, step, 0)

    outs, couts = _call(
        body,
        grid=(SW_KV,),
        in_specs=[pl.BlockSpec(memory_space=pltpu.SMEM), *_swa_specs(s),
                  pl.BlockSpec((SW_GROUP, BLOCK, 2 * BLOCK), lambda g: (g, 0, 0))],
        out_specs=[pl.BlockSpec((s, 2 * LANES), lambda g: (0, g)),
                   pl.BlockSpec((None, s, LANES), lambda g: (g, 0, 0))],
        out_shape=[jax.ShapeDtypeStruct((s, SW_W), MXU), jax.ShapeDtypeStruct((SW_KV, s, LANES), F32)],
        args=(sinks, qkv, qkv, qkv, bias), name=name, comm=comm)
    return tuple(outs) if comm is None else (tuple(outs), couts)


def _swa_bwd(qkv, bias, sinks, do, lse, name, comm=None):
    s = qkv.shape[0]
    nb = s // BLOCK

    def body(sink_ref, q_ref, k_ref, v_ref, bias_ref, do_ref, lse_ref,
             dq_ref, dk_ref, dv_ref, dbias_ref, dsink_ref, dk_acc, dv_acc):
        g = pl.program_id(0)
        sink = _per_head_col([sink_ref[SW_GROUP * g + j] for j in range(SW_GROUP)])
        lane = lax.broadcasted_iota(jnp.int32, (1, LANES), 1)

        @pl.when(g == 0)
        def _():
            dk_acc[...] = jnp.zeros_like(dk_acc)
            dv_acc[...] = jnp.zeros_like(dv_acc)

        dbias_ref[...] = jnp.zeros_like(dbias_ref)

        def step(n, dsink_rows):
            r0 = pl.multiple_of(n * BLOCK, BLOCK)
            p0 = pl.multiple_of(jnp.maximum(n - 1, 0) * BLOCK, BLOCK)
            cur, prev = pl.ds(r0, BLOCK), pl.ds(p0, BLOCK)
            qs = _stack_heads(q_ref[cur, :].astype(F32) * SCALE, g).astype(MXU)
            dos = _stack_heads(do_ref[cur, :].astype(F32), g).astype(MXU)
            kp, kc, vp, vc = k_ref[prev, :], k_ref[cur, :], v_ref[prev, :], v_ref[cur, :]
            lse_row = lse_ref[cur, :]
            lse = jnp.concatenate([jnp.sum(jnp.where(lane == j, lse_row, 0.0), axis=1, keepdims=True)
                                   for j in range(SW_GROUP)], axis=0)
            s1, s2 = _swa_scores(qs, kp, kc, bias_ref, n)
            pr1 = jnp.exp(s1 - lse)
            pr2 = jnp.exp(s2 - lse)
            dpr1 = _dot_nt(dos, vp)
            dpr2 = _dot_nt(dos, vc)
            delta = jnp.sum(pr1 * dpr1, axis=1, keepdims=True) + jnp.sum(pr2 * dpr2, axis=1, keepdims=True)
            ds1 = pr1 * (dpr1 - delta)
            ds2 = pr2 * (dpr2 - delta)
            dbias_ref[:, :, :BLOCK] += ds1.reshape(SW_GROUP, BLOCK, BLOCK)
            dbias_ref[:, :, BLOCK:] += ds2.reshape(SW_GROUP, BLOCK, BLOCK)
            ds1b, ds2b = ds1.astype(MXU), ds2.astype(MXU)
            dq = _dot(ds1b, kp) + _dot(ds2b, kc)
            dq_ref[cur, :] = (_unstack_heads(dq, g) * SCALE).astype(dq_ref.dtype)
            dk_acc[prev, :] += _dot_tn(ds1b, qs)
            dk_acc[cur, :] += _dot_tn(ds2b, qs)
            dv_acc[prev, :] += _dot_tn(pr1.astype(MXU), dos)
            dv_acc[cur, :] += _dot_tn(pr2.astype(MXU), dos)
            return dsink_rows - jnp.exp(sink - lse) * delta

        rows = lax.fori_loop(0, nb, step, jnp.zeros((GROUP_ROWS, 1), F32))
        for j in range(SW_GROUP):
            dsink_ref[j] = jnp.broadcast_to(jnp.sum(rows[j * BLOCK:(j + 1) * BLOCK], axis=0, keepdims=True),
                                            (1, LANES))

        @pl.when(g == SW_KV - 1)
        def _():
            dk_ref[...] = dk_acc[...].astype(dk_ref.dtype)
            dv_ref[...] = dv_acc[...].astype(dv_ref.dtype)

    grp = pl.BlockSpec((s, 2 * LANES), lambda g: (0, g))
    kv_out = pl.BlockSpec((s, LANES), lambda g: (0, 0))
    bspec = pl.BlockSpec((SW_GROUP, BLOCK, 2 * BLOCK), lambda g: (g, 0, 0))
    outs, couts = _call(
        body,
        grid=(SW_KV,),
        in_specs=[pl.BlockSpec(memory_space=pltpu.SMEM), *_swa_specs(s), bspec, grp,
                  pl.BlockSpec((None, s, LANES), lambda g: (g, 0, 0))],
        out_specs=[grp, kv_out, kv_out, bspec, pl.BlockSpec((SW_GROUP, 1, LANES), lambda g: (g, 0, 0))],
        out_shape=[jax.ShapeDtypeStruct((s, SW_W), MXU),
                   jax.ShapeDtypeStruct((s, LANES), MXU),
                   jax.ShapeDtypeStruct((s, LANES), MXU),
                   jax.ShapeDtypeStruct((SW_HEADS, BLOCK, 2 * BLOCK), F32),
                   jax.ShapeDtypeStruct((SW_HEADS, 1, LANES), F32)],
        scratch_shapes=[pltpu.VMEM((s, LANES), F32), pltpu.VMEM((s, LANES), F32)],
        args=(sinks, qkv, qkv, qkv, bias, do, lse), name=name, comm=comm)
    return tuple(outs) if comm is None else (tuple(outs), couts)


class _NoPlan:
    def comm(self, name):
        return None

    def done(self, name, outs):
        pass

    def grad(self, layer, name, value):
        pass

    def layer_done(self, layer):
        pass


def _run(plan, fn, *args, name, **kw):
    comm = plan.comm(name)
    if comm is None:
        return fn(*args, name=name, **kw)
    res, outs = fn(*args, name=name, comm=comm, **kw)
    plan.done(name, outs)
    return res


def _layer_fwd(x, p, w, g_mix, g_mlp, g_pe, sinks, bias, tag, plan):
    h1, r1 = _rms_fwd(x, g_mix, f"rms_mix_{tag}")
    qkv = _run(plan, _mm, h1, w["w_qkv"], out_dtypes=(MXU,), name=f"proj_qkv_{tag}")
    gates = _run(plan, _mm, h1, w["w_gate"], name=f"proj_gate_{tag}")
    oa = _run(plan, _sb_fwd, qkv, name=f"sb_fwd_{tag}")
    ob, lse = _run(plan, _swa_fwd, qkv, bias, sinks, name=f"swa_fwd_{tag}")
    merged = _mix_fwd(oa, ob, w["w_up_a"], w["w_up_b"], gates, f"mix_fwd_{tag}")
    x1 = _run(plan, _mm, merged, w["w_o"], extras=(x,), epi=lambda acc, res: res + acc, name=f"out_proj_{tag}")
    h2, r2 = _rms_fwd(x1, g_mlp, f"rms_mlp_{tag}")
    u, act = _run(plan, _mm, h2, w["w_ff1"], epi=lambda acc: (acc, jnp.square(jnp.maximum(acc, 0.0))),
                  out_dtypes=(MXU, MXU), name=f"ff1_{tag}")
    x2 = _run(plan, _mm, act, w["w_ff2"], extras=(x1,), epi=lambda acc, res: res + acc, name=f"ff2_{tag}")
    h3, r3 = _rms_fwd(x2, g_pe, f"rms_pe_{tag}")
    x3 = _ple(p, w["w_pe"], h3, w["w_pg"], x2, backward=False, name=f"ple_fwd_{tag}")
    saved = dict(x=x, h1=h1, r1=r1, gates=gates, qkv=qkv, lse=lse, oa=oa, ob=ob, merged=merged,
                 x1=x1, h2=h2, r2=r2, u=u, act=act, x2=x2, h3=h3, r3=r3)
    return x3, saved


def _layer_bwd(dx3, sv, p, w, g_mix, g_mlp, g_pe, sinks, bias, layer, plan):
    tag = f"l{layer}"
    gw = {}
    wire = (WIRE,)

    def dw(name, a, b):
        gw[name] = _run(plan, _mm, a, b, ta=True, out_dtypes=wire, name=f"d{name}_{tag}")
        plan.grad(layer, name, gw[name])

    dpe, dgt = _ple(p, w["w_pe"], sv["h3"], w["w_pg"], dx3, backward=True, name=f"ple_bwd_{tag}")
    dw("w_pe", p, dpe)
    dw("w_pg", sv["h3"], dgt)
    dh3 = _run(plan, _mm, dgt, w["w_pg"], tb=True, name=f"dh_pe_{tag}")
    dx2, dx2b, dg_pe = _rms_bwd(sv["x2"], sv["r3"], g_pe, dh3, dx3, f"rms_pe_bwd_{tag}")
    dw("w_ff2", sv["act"], dx2b)
    du = _run(plan, _mm, dx2b, w["w_ff2"], tb=True, extras=(sv["u"],),
              epi=lambda acc, u: acc * (2.0 * jnp.maximum(u.astype(F32), 0.0)), out_dtypes=(MXU,),
              name=f"dact_{tag}")
    dw("w_ff1", sv["h2"], du)
    dh2 = _run(plan, _mm, du, w["w_ff1"], tb=True, name=f"dh_mlp_{tag}")
    dx1, dx1b, dg_mlp = _rms_bwd(sv["x1"], sv["r2"], g_mlp, dh2, dx2, f"rms_mlp_bwd_{tag}")
    dw("w_o", sv["merged"], dx1b)
    dya, dyb, dga, dgb = _mix_bwd(dx1b, w["w_o"], sv["oa"], sv["ob"], w["w_up_a"], w["w_up_b"],
                                  sv["gates"], f"mix_bwd_{tag}")
    dw("w_up_a", sv["oa"], dya)
    dw("w_up_b", sv["ob"], dyb)
    doa = _run(plan, _mm, dya, w["w_up_a"], tb=True, out_dtypes=(MXU,), name=f"do_a_{tag}")
    dob = _run(plan, _mm, dyb, w["w_up_b"], tb=True, out_dtypes=(MXU,), name=f"do_b_{tag}")
    dqb, dkb, dvb, dbias, dsink = _run(plan, _swa_bwd, sv["qkv"], bias, sinks, dob, sv["lse"],
                                       name=f"swa_bwd_{tag}")
    dqa, dka, dva = _run(plan, _sb_bwd, sv["qkv"], doa, name=f"sb_bwd_{tag}")
    dqkv = jnp.concatenate([dqa, dka, dva, dqb, dkb, dvb], axis=1)
    gw_qkv = _mm(sv["h1"], dqkv, ta=True, out_dtypes=wire, name=f"dw_qkv_{tag}")
    gw_ga = _mm(sv["h1"], dga, ta=True, out_dtypes=wire, name=f"dw_ga_{tag}")
    gw_gb = _mm(sv["h1"], dgb, ta=True, out_dtypes=wire, name=f"dw_gb_{tag}")
    gw["w_in"] = jnp.concatenate([gw_qkv, gw_ga, gw_gb], axis=1)
    plan.grad(layer, "w_in", gw["w_in"])
    d = dga.shape[1]
    add = lambda acc, res: res + acc
    dh1 = _mm(dqkv, w["w_qkv"], tb=True, tk=768, name=f"dh_qkv_{tag}")
    dh1 = _mm(dga, w["w_gate"][:, :d], tb=True, extras=(dh1,), epi=add, name=f"dh_ga_{tag}")
    dh1 = _mm(dgb, w["w_gate"][:, d:], tb=True, extras=(dh1,), epi=add, name=f"dh_gb_{tag}")
    dx, _, dg_mix = _rms_bwd(sv["x"], sv["r1"], g_mix, dh1, dx1, f"rms_mix_bwd_{tag}")
    small = dict(g_mix=dg_mix, g_mlp=dg_mlp, g_pe=dg_pe, sinks=dsink[:, 0, 0], dbias=dbias)
    return dx, gw, small


def _local_step(x, p, target, weights, g_mix, g_mlp, g_pe, g_final, sinks, rel_bias, plan=None):
    plan = _NoPlan() if plan is None else plan
    depth = g_mix.shape[0]
    buckets = jnp.asarray(_bucket_table())
    bias = _build_bias(rel_bias, buckets, "build_bias")
    saved, wfull = [], []
    h = x
    for l in range(depth):
        wfull.append(weights(l))
        h, sv = _layer_fwd(h, p[l], wfull[l], g_mix[l:l + 1], g_mlp[l:l + 1], g_pe[l:l + 1],
                           sinks[l], bias, f"l{l}", plan)
        saved.append(sv)
    loss_row, dx, dg_final = _loss_head(h, g_final[None, :], target, "loss_head")
    gws = [None] * depth
    smalls = [None] * depth
    for l in reversed(range(depth)):
        dx, gws[l], smalls[l] = _layer_bwd(dx, saved[l], p[l], wfull[l], g_mix[l:l + 1], g_mlp[l:l + 1],
                                           g_pe[l:l + 1], sinks[l], bias, l, plan)
        plan.layer_done(l)
    drel = _bias_grad([sm["dbias"] for sm in smalls], buckets, "bias_grad")[:, 0, :N_BUCKETS].T
    small = dict(
        g_mix=jnp.concatenate([sm["g_mix"] for sm in smalls], axis=0),
        g_mlp=jnp.concatenate([sm["g_mlp"] for sm in smalls], axis=0),
        g_pe=jnp.concatenate([sm["g_pe"] for sm in smalls], axis=0),
        g_final=dg_final[0],
        sinks=jnp.stack([sm["sinks"] for sm in smalls], axis=0),
        rel_bias=drel,
    )
    return loss_row, dx, gws, small


MESH_ID = pl.DeviceIdType.MESH
ANY = pl.BlockSpec(memory_space=pl.ANY)


def _position():
    return lax.axis_index("x"), lax.axis_index("y"), lax.axis_index("c")


def _run_comm(comm, name):
    ci, co = len(comm.inputs), len(comm.out_shapes)

    def body(*refs):
        cin, cout, csem = refs[:ci], refs[ci:ci + co], refs[ci + co:]
        pos = _position()
        comm.start(pos, cin, cout, csem)
        comm.finish(pos, cin, cout, csem)

    return pl.pallas_call(body, out_shape=comm.out_shapes, in_specs=[ANY] * ci, out_specs=[ANY] * co,
                          scratch_shapes=comm.sems, name=name)(*comm.inputs)


def _gather_comm(shards):
    n = len(shards)

    def copies(pos, x_refs, out_refs, sems):
        send_sems, recv_sems, local_sems = sems
        x, y, c = pos
        me, sibling = (x, y, c), (x, y, 1 - c)
        chips = [(1 - x, y), (x, 1 - y), (1 - x, 1 - y)]

        def slot(a, px, py, pc):
            return out_refs[a].at[4 * px + 2 * py + pc]

        def copy(a, k, block, to, src=None):
            return pltpu.make_async_remote_copy(
                src_ref=slot(a, *block) if src is None else src, dst_ref=slot(a, *block),
                send_sem=send_sems.at[a, k], recv_sem=recv_sems.at[a, k],
                device_id=to, device_id_type=MESH_ID)

        mine = [pltpu.make_async_copy(x_refs[a], slot(a, *me), local_sems.at[a]) for a in range(n)]
        first = []
        for a in range(n):
            first.append(copy(a, 0, me, sibling, src=x_refs[a]))
            first += [copy(a, 1 + j, me, (*chip, c), src=x_refs[a]) for j, chip in enumerate(chips)]
        return me, sibling, chips, copy, mine, first

    def start(pos, x_refs, out_refs, sems):
        _, _, _, _, mine, first = copies(pos, x_refs, out_refs, sems)
        for cp in mine + first:
            cp.start()

    def finish(pos, x_refs, out_refs, sems):
        me, sibling, chips, copy, mine, first = copies(pos, x_refs, out_refs, sems)
        c = pos[2]
        passed = []
        for j, chip in enumerate(chips):
            for a in range(n):
                copy(a, 1 + j, (*chip, c), me).wait_recv()
                fwd = copy(a, 4 + j, (*chip, c), sibling)
                fwd.start()
                passed.append(fwd)
        for a in range(n):
            copy(a, 0, sibling, me).wait_recv()
            for j, chip in enumerate(chips):
                copy(a, 4 + j, (*chip, 1 - c), me).wait_recv()
        for cp in first + passed:
            cp.wait_send()
        for cp in mine:
            cp.wait()

    return _Comm(shards, [jax.ShapeDtypeStruct((N_DEV,) + s.shape, s.dtype) for s in shards],
                 [pltpu.SemaphoreType.DMA((n, 7)), pltpu.SemaphoreType.DMA((n, 7)),
                  pltpu.SemaphoreType.DMA((n,))], start, finish)


def _exchange_comm(arrays, n_slots, route):
    n = len(arrays)

    def copies(pos, in_refs, out_refs, sems):
        send_sems, recv_sems = sems
        out = []
        for a in range(n):
            for j in range(n_slots):
                src_slot, peer = route(pos, j)
                out.append(pltpu.make_async_remote_copy(
                    src_ref=in_refs[a].at[src_slot], dst_ref=out_refs[a].at[j],
                    send_sem=send_sems.at[a, j], recv_sem=recv_sems.at[a, j],
                    device_id=peer, device_id_type=MESH_ID))
        return out

    def start(pos, in_refs, out_refs, sems):
        for cp in copies(pos, in_refs, out_refs, sems):
            cp.start()

    def finish(pos, in_refs, out_refs, sems):
        for cp in copies(pos, in_refs, out_refs, sems):
            cp.wait()

    return _Comm(arrays, [jax.ShapeDtypeStruct((n_slots,) + g.shape[1:], g.dtype) for g in arrays],
                 [pltpu.SemaphoreType.DMA((n, n_slots)), pltpu.SemaphoreType.DMA((n, n_slots))], start, finish)


def _rs_sibling_comm(gs):
    return _exchange_comm(gs, 4, lambda pos, j: (2 * j + (1 - pos[2]), (pos[0], pos[1], 1 - pos[2])))


def _chip_of(k, x, y):
    return x ^ ((k + 1) & 1), y ^ (((k + 1) >> 1) & 1)


def _chip_partials(pos, gs, recvs, name):
    n = len(gs)

    def body(pos_ref, *refs):
        for a in range(n):
            refs[2 * n + a][...] = (refs[a][...].astype(F32) + refs[n + a][...].astype(F32)
                                    ).astype(refs[2 * n + a].dtype)

    def g_map(k, pos_ref):
        cx, cy = _chip_of(k, pos_ref[0], pos_ref[1])
        return (4 * cx + 2 * cy + pos_ref[2], 0, 0)

    def r_map(k, pos_ref):
        cx, cy = _chip_of(k, pos_ref[0], pos_ref[1])
        return (2 * cx + cy, 0, 0)

    slab = [(None,) + g.shape[1:] for g in gs]
    return pl.pallas_call(
        body,
        grid_spec=pltpu.PrefetchScalarGridSpec(
            num_scalar_prefetch=1,
            grid=(4,),
            in_specs=[pl.BlockSpec(sh, g_map) for sh in slab] + [pl.BlockSpec(sh, r_map) for sh in slab],
            out_specs=[pl.BlockSpec(sh, lambda k, pos_ref: (k, 0, 0)) for sh in slab],
        ),
        out_shape=[jax.ShapeDtypeStruct((4,) + g.shape[1:], g.dtype) for g in gs],
        compiler_params=_cparams(),
        name=name,
    )(pos, *gs, *recvs)


def _rs_chips_comm(parts):
    return _exchange_comm(parts, 3, lambda pos, k: (k, (*_chip_of(k, pos[0], pos[1]), pos[2])))


def _adamw_math(w, g, m, v):
    m = ADAM_B1 * m + (1.0 - ADAM_B1) * g
    v = ADAM_B2 * v + (1.0 - ADAM_B2) * (g * g)
    m_hat = m / (1.0 - ADAM_B1 ** ADAM_STEP)
    v_hat = v / (1.0 - ADAM_B2 ** ADAM_STEP)
    delta = -ADAM_LR * (m_hat / (jnp.sqrt(v_hat) + ADAM_EPS) + ADAM_WD * w)
    return delta, m, v


def _adamw_weight(parts, recvs, w, m, v, name):
    depth, a, b = w.shape
    ta = _tile(a, 256, unit=16)
    ni = a // ta

    def body(*refs):
        p_refs, r_refs = refs[:depth], refs[depth:2 * depth]
        w_ref, m_ref, v_ref = refs[2 * depth:2 * depth + 3]
        g_out, d_out, m_out, v_out = refs[2 * depth + 3:]
        layer = pl.program_id(0)
        g = jnp.zeros((ta, b), F32)
        for l in range(depth):
            gl = p_refs[l][...].astype(F32)
            for k in range(3):
                gl = gl + r_refs[l][k].astype(F32)
            g = jnp.where(layer == l, gl, g)
        delta, m_new, v_new = _adamw_math(w_ref[...], g, m_ref[...], v_ref[...])
        g_out[...] = g
        d_out[...] = delta
        m_out[...] = m_new
        v_out[...] = v_new

    def hold(l):
        return lambda layer, i: jnp.where(layer == l, i, jnp.where(layer < l, 0, ni - 1))

    p_specs = [pl.BlockSpec((None, ta, b), (lambda layer, i, f=hold(l): (3, f(layer, i), 0))) for l in range(depth)]
    r_specs = [pl.BlockSpec((3, ta, b), (lambda layer, i, f=hold(l): (0, f(layer, i), 0))) for l in range(depth)]
    row = pl.BlockSpec((None, ta, b), lambda layer, i: (layer, i, 0))
    return pl.pallas_call(
        body,
        grid=(depth, ni),
        in_specs=p_specs + r_specs + [row, row, row],
        out_specs=[row] * 4,
        out_shape=[jax.ShapeDtypeStruct(w.shape, F32)] * 4,
        compiler_params=_cparams(),
        name=name,
    )(*parts, *recvs, w, m, v)


def _adamw_replicated(gathered, w, m, v, name):
    r, lanes = w.shape

    def body(g_ref, w_ref, m_ref, v_ref, g_out, d_out, m_out, v_out):
        g = g_ref[0]
        for k in range(1, N_DEV):
            g = g + g_ref[k]
        delta, m_new, v_new = _adamw_math(w_ref[...], g, m_ref[...], v_ref[...])
        g_out[...] = g
        d_out[...] = delta
        m_out[...] = m_new
        v_out[...] = v_new

    return pl.pallas_call(
        body,
        out_shape=[jax.ShapeDtypeStruct((r, lanes), F32)] * 4,
        name=name,
    )(gathered, w, m, v)


def _full_weight(name, gathered):
    _, a, b = gathered.shape
    if name in COL_SHARDED:
        return gathered.transpose(1, 0, 2).reshape(a, N_DEV * b)
    return gathered.reshape(N_DEV * a, b)


def _to_slabs(name, gfull, shard_shape):
    a, b = shard_shape
    if name in COL_SHARDED:
        return gfull.reshape(a, N_DEV, b).transpose(1, 0, 2)
    return gfull.reshape(N_DEV, a, b)


def _pack_small(arrs):
    rows = []
    for a in arrs:
        flat = a.astype(F32).reshape(-1)
        pad = (-flat.shape[0]) % LANES
        rows.append(jnp.pad(flat, (0, pad)).reshape(-1, LANES))
    packed = jnp.concatenate(rows, axis=0)
    return jnp.pad(packed, ((0, (-packed.shape[0]) % 8), (0, 0)))


def _unpack_small(packed, shapes):
    out, off = [], 0
    for shp in shapes:
        n = math.prod(shp)
        rows = -(-n // LANES)
        out.append(packed[off:off + rows].reshape(-1)[:n].reshape(shp))
        off += rows
    return out


def _of(layer, *names):
    return tuple((layer, n) for n in names)


MIXER_W = ("w_in", "w_up_a", "w_up_b", "w_o")
MLP_W = ("w_ff1", "w_ff2", "w_pe", "w_pg")

GATHERS = (
    (None, _of(0, "w_in")),
    ("proj_qkv_l0", _of(0, "w_up_a", "w_up_b", "w_o")),
    ("proj_gate_l0", _of(0, "w_pe", "w_pg")),
    ("sb_fwd_l0", _of(0, "w_ff1", "w_ff2")),
    ("swa_fwd_l0", _of(1, "w_in")),
    ("sb_fwd_l1", _of(1, "w_up_a", "w_up_b", "w_o", "w_pe", "w_pg", "w_ff1")),
    ("swa_fwd_l1", _of(1, "w_ff2")),
)
REDUCES = (
    (_of(1, "w_ff1"), "dw_ff2_l0", "dact_l0"),
    (_of(1, "w_ff2"), "dw_ff2_l0", "dw_ff1_l0"),
    (_of(1, "w_in"), "dw_ff2_l0", "sb_bwd_l0"),
    (_of(1, "w_up_a", "w_up_b", "w_o", "w_pe", "w_pg"), "dw_ff2_l0", "swa_bwd_l0"),
    (_of(0, *MLP_W), "dh_mlp_l0", "sb_bwd_l0"),
    (_of(0, "w_o", "w_up_a", "w_up_b"), "do_a_l0", "sb_bwd_l0"),
    (_of(0, "w_in"), None, None),
)


def _merge_comms(comms):
    if len(comms) == 1:
        return comms[0]

    def cuts(counts):
        edges = [0]
        for c in counts:
            edges.append(edges[-1] + c)
        return [slice(a, b) for a, b in zip(edges[:-1], edges[1:])]

    s_in = cuts([len(c.inputs) for c in comms])
    s_out = cuts([len(c.out_shapes) for c in comms])
    s_sem = cuts([len(c.sems) for c in comms])

    def start(pos, cin, cout, csem):
        for c, i, o, s in zip(comms, s_in, s_out, s_sem):
            c.start(pos, cin[i], cout[o], csem[s])

    def finish(pos, cin, cout, csem):
        for c, i, o, s in zip(comms, s_in, s_out, s_sem):
            c.finish(pos, cin[i], cout[o], csem[s])

    return _Comm(sum([c.inputs for c in comms], []), sum([c.out_shapes for c in comms], []),
                 sum([c.sems for c in comms], []), start, finish)


class _LayerWeights:
    def __init__(self, full, layer):
        self.full, self.layer, self.cache = full, layer, {}

    def __getitem__(self, name):
        if name not in self.cache:
            if name == "w_qkv":
                self.cache[name] = self.full[(self.layer, "w_in")][:, :QKV_COLS]
            elif name == "w_gate":
                self.cache[name] = self.full[(self.layer, "w_in")][:, QKV_COLS:]
            else:
                self.cache[name] = self.full[(self.layer, name)]
        return self.cache[name]


class _Plan:
    def __init__(self, w_sh, pos):
        self.w_sh = dict(zip(WEIGHTS, w_sh))
        self.pos = pos
        self.full, self.gw, self.parts, self.recv = {}, {}, {}, {}
        self.slabs = {}
        self.hosted = {}
        for i, (host, _) in enumerate(GATHERS):
            if host is not None:
                self.hosted.setdefault(host, []).append(("gather", i))
        for i, (_, sib_host, chip_host) in enumerate(REDUCES):
            assert (sib_host is None) == (chip_host is None)
            if sib_host is not None:
                self.hosted.setdefault(sib_host, []).append(("sibling", i))
                self.hosted.setdefault(chip_host, []).append(("chips", i))

    def _gather(self, i):
        return _gather_comm([self.w_sh[n][layer].astype(WIRE) for layer, n in GATHERS[i][1]])

    def _gathered(self, i, outs):
        for (layer, n), g in zip(GATHERS[i][1], outs):
            self.full[(layer, n)] = _full_weight(n, g)

    def weights(self, layer):
        for i, (host, items) in enumerate(GATHERS):
            if host is None and items[0][0] == layer:
                self._gathered(i, _run_comm(self._gather(i), f"gather_{i}"))
        return _LayerWeights(self.full, layer)

    def grad(self, layer, name, value):
        self.gw[(layer, name)] = value

    def _sibling(self, i):
        self.slabs[i] = [_to_slabs(n, self.gw[(layer, n)], self.w_sh[n].shape[1:]) for layer, n in REDUCES[i][0]]
        return _rs_sibling_comm(self.slabs[i])

    def _sibling_done(self, i, outs):
        parts = _chip_partials(self.pos, self.slabs[i], outs, f"chip_partials_{i}")
        for item, part in zip(REDUCES[i][0], parts):
            self.parts[item] = part

    def _chips(self, i):
        return _rs_chips_comm([self.parts[item] for item in REDUCES[i][0]])

    def _chips_done(self, i, outs):
        for item, r in zip(REDUCES[i][0], outs):
            self.recv[item] = r

    def layer_done(self, layer):
        for i, (items, sib_host, _) in enumerate(REDUCES):
            if sib_host is None and items[0][0] == layer:
                self._sibling_done(i, _run_comm(self._sibling(i), f"reduce_sibling_{i}"))
                self._chips_done(i, _run_comm(self._chips(i), f"reduce_chips_{i}"))

    def comm(self, name):
        if name not in self.hosted:
            return None
        make = {"gather": self._gather, "sibling": self._sibling, "chips": self._chips}
        return _merge_comms([make[kind](i) for kind, i in self.hosted[name]])

    def done(self, name, outs):
        took = {"gather": self._gathered, "sibling": self._sibling_done, "chips": self._chips_done}
        off = 0
        for kind, i in self.hosted[name]:
            n = len(GATHERS[i][1]) if kind == "gather" else len(REDUCES[i][0])
            took[kind](i, outs[off:off + n])
            off += n


def kernel(x, p, w_in, w_up_a, w_up_b, w_o, w_ff1, w_ff2, w_pe, w_pg, g_mix, g_mlp, g_pe, g_final, sinks, rel_bias, loss_target, m_w_in, m_w_up_a, m_w_up_b, m_w_o, m_w_ff1, m_w_ff2, m_w_pe, m_w_pg, m_g_mix, m_g_mlp, m_g_pe, m_g_final, m_sinks, m_rel_bias, v_w_in, v_w_up_a, v_w_up_b, v_w_o, v_w_ff1, v_w_ff2, v_w_pe, v_w_pg, v_g_mix, v_g_mlp, v_g_pe, v_g_final, v_sinks, v_rel_bias):
    w_sh = [w_in, w_up_a, w_up_b, w_o, w_ff1, w_ff2, w_pe, w_pg]
    m_sh = [m_w_in, m_w_up_a, m_w_up_b, m_w_o, m_w_ff1, m_w_ff2, m_w_pe, m_w_pg]
    v_sh = [v_w_in, v_w_up_a, v_w_up_b, v_w_o, v_w_ff1, v_w_ff2, v_w_pe, v_w_pg]
    depth = w_in.shape[0]
    assert depth == 2 and x.shape[-1] * 2 + QKV_COLS == w_in.shape[2] * N_DEV

    px, py, pc = _position()
    plan = _Plan(w_sh, jnp.stack([px, py, pc]).astype(jnp.int32))
    loss_row, grad_x, _, small = _local_step(
        x[0], p[:, 0], loss_target[0], plan.weights, g_mix, g_mlp, g_pe, g_final, sinks, rel_bias, plan=plan)

    grad_w, delta_w, new_m, new_v = [], [], [], []
    for a, name in enumerate(WEIGHTS):
        outs = _adamw_weight([plan.parts[(l, name)] for l in range(depth)],
                             [plan.recv[(l, name)] for l in range(depth)],
                             w_sh[a], m_sh[a], v_sh[a], f"adamw_{name}")
        for lst, o in zip((grad_w, delta_w, new_m, new_v), outs):
            lst.append(o)

    small_w = [g_mix, g_mlp, g_pe, g_final, sinks, rel_bias]
    small_m = [m_g_mix, m_g_mlp, m_g_pe, m_g_final, m_sinks, m_rel_bias]
    small_v = [v_g_mix, v_g_mlp, v_g_pe, v_g_final, v_sinks, v_rel_bias]
    small_shapes = [a.shape for a in small_w] + [(1,)]
    zero = jnp.zeros((1,), F32)
    small_g = _pack_small([small[n] for n in SMALL] + [loss_row[0, :1]])
    small_all = _run_comm(_gather_comm([small_g]), "gather_small")[0]
    packed_s = _adamw_replicated(small_all, _pack_small(small_w + [zero]), _pack_small(small_m + [zero]),
                                 _pack_small(small_v + [zero + 1.0]), "adamw_replicated")
    sg, sd, sm, sv = [_unpack_small(t, small_shapes) for t in packed_s]
    loss = sg[-1][0]

    return (loss, grad_x[None], *grad_w, *sg[:-1], *delta_w, *sd[:-1], *new_m, *sm[:-1], *new_v, *sv[:-1])
```

```python
import functools
import math

import numpy as np
import jax
import jax.numpy as jnp
from jax import lax
from jax.experimental import pallas as pl
from jax.experimental.pallas import tpu as pltpu

F32 = jnp.float32
MXU = jnp.bfloat16
WIRE = jnp.bfloat16

HEAD_DIM = 64
SB_HEADS = 8
SW_HEADS = 8
SW_KV = 2
SW_GROUP = SW_HEADS // SW_KV
BLOCK = 128
N_BUCKETS = 32
MAX_DISTANCE = 128
EPS = 1e-6
SCALE = HEAD_DIM ** -0.5
SB_W = SB_HEADS * HEAD_DIM
SW_W = SW_HEADS * HEAD_DIM
QKV_COLS = 3 * SB_W + SW_W + 2 * SW_KV * HEAD_DIM
N_DEV = 8
LANES = 128
N_PAIR = SB_HEADS // 2
NEG = -1e30

ADAM_LR = 0.001
ADAM_B1 = 0.9
ADAM_B2 = 0.999
ADAM_EPS = 1e-08
ADAM_WD = 0.01
ADAM_STEP = 10

VMEM_LIMIT = 48 * 1024 * 1024
SB_TQ = 256
SB_DEAD = -105.0

WEIGHTS = ("w_in", "w_up_a", "w_up_b", "w_o", "w_ff1", "w_ff2", "w_pe", "w_pg")
COL_SHARDED = ("w_in", "w_up_a", "w_up_b", "w_ff1", "w_pe")
SMALL = ("g_mix", "g_mlp", "g_pe", "g_final", "sinks", "rel_bias")


def _cparams(**kw):
    return pltpu.CompilerParams(vmem_limit_bytes=VMEM_LIMIT, **kw)


def _dot(a, b):
    return jnp.dot(a, b, preferred_element_type=F32)


def _dot_nt(a, b):
    return lax.dot_general(a, b, (((1,), (1,)), ((), ())), preferred_element_type=F32)


def _dot_tn(a, b):
    return lax.dot_general(a, b, (((0,), (0,)), ((), ())), preferred_element_type=F32)


def _tile(n, target, unit=LANES):
    if n <= target:
        return n
    t = (target // unit) * unit
    while t > unit and n % t:
        t -= unit
    assert n % t == 0, (n, target)
    return t


def _sigmoid(x):
    return 1.0 / (1.0 + jnp.exp(-x))


class _Comm:
    def __init__(self, inputs, out_shapes, sems, start, finish):
        self.inputs, self.out_shapes, self.sems = list(inputs), list(out_shapes), list(sems)
        self.start, self.finish = start, finish


def _call(body, *, grid, in_specs, out_specs, out_shape, scratch_shapes=(), args, name, comm=None):
    n_in, n_out, n_scr = len(in_specs), len(out_shape), len(scratch_shapes)
    if comm is None:
        outs = pl.pallas_call(body, grid=grid, in_specs=list(in_specs), out_specs=list(out_specs),
                              out_shape=list(out_shape), scratch_shapes=list(scratch_shapes),
                              compiler_params=_cparams(), name=name)(*args)
        return list(outs), None
    ci, co = len(comm.inputs), len(comm.out_shapes)
    any_spec = pl.BlockSpec(memory_space=pl.ANY)

    def wrapped(*refs):
        ins, cin = refs[:n_in], refs[n_in:n_in + ci]
        o0 = n_in + ci
        outs, cout = refs[o0:o0 + n_out], refs[o0 + n_out:o0 + n_out + co]
        s0 = o0 + n_out + co
        scr, csem = refs[s0:s0 + n_scr], refs[s0 + n_scr:]
        ids = [pl.program_id(d) for d in range(len(grid))]
        first = functools.reduce(jnp.logical_and, [i == 0 for i in ids])
        last = functools.reduce(jnp.logical_and, [i == g - 1 for i, g in zip(ids, grid)])
        pos = (lax.axis_index("x"), lax.axis_index("y"), lax.axis_index("c"))

        @pl.when(first)
        def _():
            comm.start(pos, cin, cout, csem)

        body(*ins, *outs, *scr)

        @pl.when(last)
        def _():
            comm.finish(pos, cin, cout, csem)

    outs = pl.pallas_call(wrapped, grid=grid, in_specs=list(in_specs) + [any_spec] * ci,
                          out_specs=list(out_specs) + [any_spec] * co,
                          out_shape=list(out_shape) + comm.out_shapes,
                          scratch_shapes=list(scratch_shapes) + comm.sems,
                          compiler_params=_cparams(), name=name)(*args, *comm.inputs)
    return list(outs[:n_out]), list(outs[n_out:])


def _mm(a, b, *, ta=False, tb=False, extras=(), epi=None, out_dtypes=(F32,),
        tm=1024, tn=1024, tk=1024, name, comm=None):
    if ta:
        kdim, m = a.shape
    else:
        m, kdim = a.shape
    n = b.shape[0] if tb else b.shape[1]
    assert (b.shape[1] if tb else b.shape[0]) == kdim
    tm, tn, tk = _tile(m, tm), _tile(n, tn), _tile(kdim, tk)
    nk = kdim // tk
    n_ex, n_out = len(extras), len(out_dtypes)

    a_spec = (pl.BlockSpec((tk, tm), lambda i, j, k: (k, i)) if ta
              else pl.BlockSpec((tm, tk), lambda i, j, k: (i, k)))
    b_spec = (pl.BlockSpec((tn, tk), lambda i, j, k: (j, k)) if tb
              else pl.BlockSpec((tk, tn), lambda i, j, k: (k, j)))
    ex_specs = []
    for e in extras:
        assert e.shape == (m, n), (e.shape, m, n)
        ex_specs.append(pl.BlockSpec((tm, tn), lambda i, j, k: (i, j)))
    out_spec = pl.BlockSpec((tm, tn), lambda i, j, k: (i, j))

    def body(a_ref, b_ref, *rest):
        ex_refs = rest[:n_ex]
        out_refs = rest[n_ex:n_ex + n_out]
        acc = rest[-1]
        k = pl.program_id(2)

        @pl.when(k == 0)
        def _():
            acc[...] = jnp.zeros_like(acc)

        av = a_ref[...].astype(MXU)
        bv = b_ref[...].astype(MXU)
        if ta:
            acc[...] += _dot_tn(av, bv)
        elif tb:
            acc[...] += _dot_nt(av, bv)
        else:
            acc[...] += _dot(av, bv)

        @pl.when(k == nk - 1)
        def _():
            res = acc[...]
            if epi is not None:
                res = epi(res, *[e[...] for e in ex_refs])
            if not isinstance(res, tuple):
                res = (res,)
            for o_ref, r in zip(out_refs, res):
                o_ref[...] = r.astype(o_ref.dtype)

    outs, couts = _call(
        body,
        grid=(m // tm, n // tn, nk),
        in_specs=[a_spec, b_spec] + ex_specs,
        out_specs=[out_spec] * n_out,
        out_shape=[jax.ShapeDtypeStruct((m, n), dt) for dt in out_dtypes],
        scratch_shapes=[pltpu.VMEM((tm, tn), F32)],
        args=(a, b, *extras), name=name, comm=comm)
    res = outs[0] if n_out == 1 else tuple(outs)
    return res if comm is None else (res, couts)


def _rms_fwd(x, g, name, comm=None):
    s, d = x.shape
    tr = _tile(s, 256)

    def body(x_ref, g_ref, h_ref, r_ref):
        xf = x_ref[...]
        r = lax.rsqrt(jnp.mean(xf * xf, axis=-1, keepdims=True) + EPS)
        h_ref[...] = ((xf * r) * g_ref[...]).astype(h_ref.dtype)
        r_ref[...] = r

    outs, couts = _call(
        body,
        grid=(s // tr,),
        in_specs=[pl.BlockSpec((tr, d), lambda i: (i, 0)), pl.BlockSpec((1, d), lambda i: (0, 0))],
        out_specs=[pl.BlockSpec((tr, d), lambda i: (i, 0)), pl.BlockSpec((tr, 1), lambda i: (i, 0))],
        out_shape=[jax.ShapeDtypeStruct((s, d), MXU), jax.ShapeDtypeStruct((s, 1), F32)],
        args=(x, g), name=name, comm=comm)
    return tuple(outs) if comm is None else (tuple(outs), couts)


def _rms_bwd(x, r, g, dh, dres, name):
    s, d = x.shape
    tr = _tile(s, 256)

    def body(x_ref, r_ref, g_ref, dh_ref, dres_ref, dx_ref, dxb_ref, dg_ref):
        @pl.when(pl.program_id(0) == 0)
        def _():
            dg_ref[...] = jnp.zeros_like(dg_ref)

        rr = r_ref[...]
        xhat = x_ref[...] * rr
        dh_v = dh_ref[...]
        dxhat = dh_v * g_ref[...]
        mean = jnp.mean(dxhat * xhat, axis=-1, keepdims=True)
        dx = dres_ref[...] + rr * (dxhat - xhat * mean)
        dx_ref[...] = dx
        dxb_ref[...] = dx.astype(dxb_ref.dtype)
        dg_ref[...] += jnp.sum(dh_v * xhat, axis=0, keepdims=True)

    row = pl.BlockSpec((tr, d), lambda i: (i, 0))
    vec = pl.BlockSpec((1, d), lambda i: (0, 0))
    return pl.pallas_call(
        body,
        grid=(s // tr,),
        in_specs=[row, pl.BlockSpec((tr, 1), lambda i: (i, 0)), vec, row, row],
        out_specs=[row, row, vec],
        out_shape=[jax.ShapeDtypeStruct((s, d), F32), jax.ShapeDtypeStruct((s, d), MXU),
                   jax.ShapeDtypeStruct((1, d), F32)],
        compiler_params=_cparams(),
        name=name,
    )(x, r, g, dh, dres)


def _loss_head(x, g, target, name):
    s, d = x.shape
    tr = _tile(s, 256)

    def body(x_ref, g_ref, t_ref, loss_ref, dx_ref, dg_ref):
        @pl.when(pl.program_id(0) == 0)
        def _():
            dg_ref[...] = jnp.zeros_like(dg_ref)
            loss_ref[...] = jnp.zeros_like(loss_ref)

        xf = x_ref[...]
        gv = g_ref[...]
        r = lax.rsqrt(jnp.mean(xf * xf, axis=-1, keepdims=True) + EPS)
        xhat = xf * r
        err = xhat * gv - t_ref[...]
        loss_ref[...] += 0.5 * jnp.sum(jnp.mean(err * err, axis=-1, keepdims=True), axis=0, keepdims=True)
        dy = err * (1.0 / d)
        dxhat = dy * gv
        mean = jnp.mean(dxhat * xhat, axis=-1, keepdims=True)
        dx_ref[...] = r * (dxhat - xhat * mean)
        dg_ref[...] += jnp.sum(dy * xhat, axis=0, keepdims=True)

    row = pl.BlockSpec((tr, d), lambda i: (i, 0))
    vec = pl.BlockSpec((1, d), lambda i: (0, 0))
    return pl.pallas_call(
        body,
        grid=(s // tr,),
        in_specs=[row, vec, row],
        out_specs=[pl.BlockSpec((1, LANES), lambda i: (0, 0)), row, vec],
        out_shape=[jax.ShapeDtypeStruct((1, LANES), F32), jax.ShapeDtypeStruct((s, d), F32),
                   jax.ShapeDtypeStruct((1, d), F32)],
        compiler_params=_cparams(),
        name=name,
    )(x, g, target)


def _mix_fwd(oa, ob, wa, wb, gates, name):
    s, kd = oa.shape
    d = wa.shape[1]
    tm, tn = _tile(s, 1024), _tile(d, 512)
    nj = d // tn

    def body(oa_ref, ob_ref, wa_ref, wb_ref, ga_ref, gb_ref, out_ref):
        ya = _dot(oa_ref[...], wa_ref[...])
        yb = _dot(ob_ref[...], wb_ref[...])
        out_ref[...] = (_sigmoid(ga_ref[...].astype(F32)) * ya
                        + _sigmoid(gb_ref[...].astype(F32)) * yb).astype(out_ref.dtype)

    o_spec = pl.BlockSpec((tm, kd), lambda i, j: (i, 0))
    w_spec = pl.BlockSpec((kd, tn), lambda i, j: (0, j))
    return pl.pallas_call(
        body,
        grid=(s // tm, nj),
        in_specs=[o_spec, o_spec, w_spec, w_spec,
                  pl.BlockSpec((tm, tn), lambda i, j: (i, j)),
                  pl.BlockSpec((tm, tn), lambda i, j: (i, j + nj))],
        out_specs=pl.BlockSpec((tm, tn), lambda i, j: (i, j)),
        out_shape=jax.ShapeDtypeStruct((s, d), MXU),
        compiler_params=_cparams(),
        name=name,
    )(oa, ob, wa, wb, gates, gates)


def _mix_bwd(dx, w_o, oa, ob, wa, wb, gates, name):
    s, kd = oa.shape
    d = wa.shape[1]
    tm, tn = _tile(s, 1024), _tile(d, 512)
    nj = d // tn

    def body(dx_ref, wo_ref, oa_ref, ob_ref, wa_ref, wb_ref, ga_ref, gb_ref,
             dya_ref, dyb_ref, dga_ref, dgb_ref):
        dm = _dot_nt(dx_ref[...], wo_ref[...])
        ya = _dot(oa_ref[...], wa_ref[...])
        yb = _dot(ob_ref[...], wb_ref[...])
        sa = _sigmoid(ga_ref[...].astype(F32))
        sb = _sigmoid(gb_ref[...].astype(F32))
        dya_ref[...] = (dm * sa).astype(dya_ref.dtype)
        dyb_ref[...] = (dm * sb).astype(dyb_ref.dtype)
        dga_ref[...] = (dm * ya * sa * (1.0 - sa)).astype(dga_ref.dtype)
        dgb_ref[...] = (dm * yb * sb * (1.0 - sb)).astype(dgb_ref.dtype)

    o_spec = pl.BlockSpec((tm, kd), lambda i, j: (i, 0))
    w_spec = pl.BlockSpec((kd, tn), lambda i, j: (0, j))
    t_spec = pl.BlockSpec((tm, tn), lambda i, j: (i, j))
    return pl.pallas_call(
        body,
        grid=(s // tm, nj),
        in_specs=[pl.BlockSpec((tm, d), lambda i, j: (i, 0)),
                  pl.BlockSpec((tn, d), lambda i, j: (j, 0)),
                  o_spec, o_spec, w_spec, w_spec, t_spec,
                  pl.BlockSpec((tm, tn), lambda i, j: (i, j + nj))],
        out_specs=[t_spec] * 4,
        out_shape=[jax.ShapeDtypeStruct((s, d), MXU)] * 4,
        compiler_params=_cparams(),
        name=name,
    )(dx, w_o, oa, ob, wa, wb, gates, gates)


def _ple(p, w_pe, h, w_pg, other, *, backward, name):
    s, kp = p.shape
    d = w_pe.shape[1]
    tm, tn = _tile(s, 1024), _tile(d, 512)

    def body(p_ref, wpe_ref, h_ref, wpg_ref, other_ref, *out_refs):
        pe = _dot(p_ref[...].astype(MXU), wpe_ref[...])
        gt = _dot(h_ref[...], wpg_ref[...])
        sg = _sigmoid(gt)
        if backward:
            dout = other_ref[...]
            out_refs[0][...] = (dout * sg).astype(out_refs[0].dtype)
            out_refs[1][...] = (dout * pe * sg * (1.0 - sg)).astype(out_refs[1].dtype)
        else:
            out_refs[0][...] = other_ref[...] + pe * sg

    t_spec = pl.BlockSpec((tm, tn), lambda i, j: (i, j))
    if backward:
        out_specs, out_shape = [t_spec, t_spec], [jax.ShapeDtypeStruct((s, d), MXU)] * 2
    else:
        out_specs, out_shape = [t_spec], [jax.ShapeDtypeStruct((s, d), F32)]
    outs = pl.pallas_call(
        body,
        grid=(s // tm, d // tn),
        in_specs=[pl.BlockSpec((tm, kp), lambda i, j: (i, 0)),
                  pl.BlockSpec((kp, tn), lambda i, j: (0, j)),
                  pl.BlockSpec((tm, d), lambda i, j: (i, 0)),
                  pl.BlockSpec((d, tn), lambda i, j: (0, j)),
                  t_spec],
        out_specs=out_specs,
        out_shape=out_shape,
        compiler_params=_cparams(),
        name=name,
    )(p, w_pe, h, w_pg, other)
    return tuple(outs) if backward else outs[0]


def _split_dot(x, tri):
    hi = x.astype(jnp.bfloat16)
    lo = (x - hi.astype(F32)).astype(jnp.bfloat16)
    return _dot(hi, tri) + _dot(lo, tri)


def _log_sigmoids(z):
    t = jnp.log1p(jnp.exp(-jnp.abs(z)))
    return jnp.minimum(z, 0.0) - t, jnp.minimum(-z, 0.0) - t


def _head_lanes(hh):
    lane = lax.broadcasted_iota(jnp.int32, (1, LANES), 1)
    return jnp.logical_and(lane >= hh * HEAD_DIM, lane < (hh + 1) * HEAD_DIM)


def _sb_fwd(qkv, name, comm=None):
    s = qkv.shape[0]
    tq = _tile(s, SB_TQ)

    def body(q_ref, k_ref, v_ref, o_ref):
        i = pl.program_id(1)
        qf = q_ref[...].astype(F32) * SCALE
        row = lax.broadcasted_iota(jnp.int32, (tq, tq), 0)
        col = lax.broadcasted_iota(jnp.int32, (tq, tq), 1)
        causal = col < row
        tri = jnp.where(row > col, 1.0, 0.0).astype(jnp.bfloat16)

        qms = [jnp.where(_head_lanes(hh), qf, 0.0).astype(MXU) for hh in range(2)]

        def block(kb, cs, accs, masked, gate=None):
            rows = pl.ds(pl.multiple_of(kb * tq, tq), tq)
            ks, vs = k_ref[rows, :], v_ref[rows, :]
            new_c, new_acc = [], []
            for hh in range(2):
                lb, lm = _log_sigmoids(_dot_nt(qms[hh], ks))
                if masked:
                    lm = jnp.where(causal, lm, 0.0)
                if gate is not None:
                    lm = lm * gate
                a = jnp.exp(lb + _split_dot(lm, tri) + cs[hh])
                if masked:
                    a = jnp.where(causal, a, 0.0)
                if gate is not None:
                    a = a * gate
                new_acc.append(accs[hh] + _dot(a.astype(MXU), vs))
                new_c.append(cs[hh] + jnp.sum(lm, axis=1, keepdims=True))
            return tuple(new_c), tuple(new_acc)

        def top(cs):
            return jnp.maximum(jnp.max(cs[0]), jnp.max(cs[1]))

        zc, za = jnp.zeros((tq, 1), F32), jnp.zeros((tq, LANES), F32)
        cs, accs = block(i, (zc, zc), (za, za), True)
        cs, accs = block(jnp.maximum(i - 1, 0), cs, accs, False, jnp.where(i > 0, 1.0, 0.0))

        def live(st):
            return jnp.logical_and(st[0] >= 0, st[1] > SB_DEAD)

        def walk(st):
            cs, accs = block(st[0], st[2], st[3], False)
            return st[0] - 1, top(cs), cs, accs

        accs = lax.while_loop(live, walk, (i - 2, top(cs), cs, accs))[3]
        o_ref[...] = jnp.where(_head_lanes(0), accs[0], accs[1]).astype(o_ref.dtype)

    outs, couts = _call(
        body,
        grid=(N_PAIR, s // tq),
        in_specs=[pl.BlockSpec((tq, LANES), lambda p, i: (i, p)),
                  pl.BlockSpec((s, LANES), lambda p, i: (0, N_PAIR + p)),
                  pl.BlockSpec((s, LANES), lambda p, i: (0, 2 * N_PAIR + p))],
        out_specs=[pl.BlockSpec((tq, LANES), lambda p, i: (i, p))],
        out_shape=[jax.ShapeDtypeStruct((s, SB_W), MXU)],
        args=(qkv, qkv, qkv), name=name, comm=comm)
    return outs[0] if comm is None else (outs[0], couts)


def _sb_bwd(qkv, do, name, comm=None):
    s = qkv.shape[0]
    tq = _tile(s, SB_TQ)
    nq = s // tq

    def body(q_ref, k_ref, v_ref, do_ref, dq_ref, dk_ref, dv_ref, dk_acc, dv_acc, carries):
        i = pl.program_id(1)

        @pl.when(i == 0)
        def _():
            dk_acc[...] = jnp.zeros_like(dk_acc)
            dv_acc[...] = jnp.zeros_like(dv_acc)

        qf = q_ref[...].astype(F32) * SCALE
        dof = do_ref[...]
        row = lax.broadcasted_iota(jnp.int32, (tq, tq), 0)
        col = lax.broadcasted_iota(jnp.int32, (tq, tq), 1)
        causal = col < row
        tri_rev = jnp.where(row > col, 1.0, 0.0).astype(jnp.bfloat16)
        tri_excl = jnp.where(row < col, 1.0, 0.0).astype(jnp.bfloat16)

        qms = [jnp.where(_head_lanes(hh), qf, 0.0).astype(MXU) for hh in range(2)]
        doms = [jnp.where(_head_lanes(hh), dof, jnp.zeros_like(dof)) for hh in range(2)]

        def terms(kb, masked):
            rows = pl.ds(pl.multiple_of(kb * tq, tq), tq)
            ks = k_ref[rows, :]
            out = []
            for hh in range(2):
                lb, lm = _log_sigmoids(_dot_nt(qms[hh], ks))
                if masked:
                    lm = jnp.where(causal, lm, 0.0)
                out.append((lb, lm))
            return out

        def row_sums(kb, masked, pre=None):
            pre = terms(kb, masked) if pre is None else pre
            return [jnp.sum(lm, axis=1, keepdims=True) for _, lm in pre]

        def top(cs):
            return jnp.maximum(jnp.max(cs[0]), jnp.max(cs[1]))

        def live(st):
            return jnp.logical_and(st[0] >= 0, st[1] > SB_DEAD)

        def record(st):
            kb, cs = st[0], st[2]
            sums = row_sums(kb, False)
            for hh in range(2):
                carries[hh, kb] = cs[hh]
            cs = tuple(cs[hh] + sums[hh] for hh in range(2))
            return kb - 1, top(cs), cs

        prev = jnp.maximum(i - 1, 0)
        gate = jnp.where(i > 0, 1.0, 0.0)
        t_diag, t_prev = terms(i, True), terms(prev, False)
        c_diag = row_sums(i, True, t_diag)
        sums = row_sums(prev, False, t_prev)
        c_prev = tuple(c_diag[hh] + sums[hh] * gate for hh in range(2))
        first = lax.while_loop(live, record, (i - 2, top(c_prev), c_prev))[0] + 1

        def block(kb, cs, gpres, dqs, masked, gate=None, pre=None):
            rows = pl.ds(pl.multiple_of(kb * tq, tq), tq)
            ks, vs = k_ref[rows, :], v_ref[rows, :]
            pre = terms(kb, masked) if pre is None else pre
            new_g, new_dq = [], []
            dk_add, dv_add = None, None
            for hh in range(2):
                lb, lm = pre[hh]
                a = jnp.exp(lb + _split_dot(lm, tri_rev) + cs[hh])
                if masked:
                    a = jnp.where(causal, a, 0.0)
                if gate is not None:
                    a = a * gate
                g = a * _dot_nt(doms[hh], vs)
                gsum = gpres[hh] + _split_dot(g, tri_excl)
                dz = g - (g + gsum) * jnp.exp(lb)
                if masked:
                    dz = jnp.where(causal, dz, 0.0)
                if gate is not None:
                    dz = dz * gate
                dzb = dz.astype(MXU)
                new_dq.append(dqs[hh] + _dot(dzb, ks))
                dk_h = _dot_tn(dzb, qms[hh])
                dv_h = _dot_tn(a.astype(MXU), doms[hh])
                dk_add = dk_h if dk_add is None else dk_add + dk_h
                dv_add = dv_h if dv_add is None else dv_add + dv_h
                new_g.append(gpres[hh] + jnp.sum(g, axis=1, keepdims=True))
            dk_acc[rows, :] += dk_add
            dv_acc[rows, :] += dv_add
            return tuple(new_g), tuple(new_dq)

        zc, za = jnp.zeros((tq, 1), F32), jnp.zeros((tq, LANES), F32)
        gpres, dqs = lax.fori_loop(
            first, i - 1, lambda kb, cr: block(kb, (carries[0, kb], carries[1, kb]), cr[0], cr[1], False),
            ((zc, zc), (za, za)))
        gpres, dqs = block(prev, c_diag, gpres, dqs, False, gate, t_prev)
        dqs = block(i, (zc, zc), gpres, dqs, True, None, t_diag)[1]
        dq_ref[...] = (jnp.where(_head_lanes(0), dqs[0], dqs[1]) * SCALE).astype(dq_ref.dtype)

        @pl.when(i == nq - 1)
        def _():
            dk_ref[...] = dk_acc[...].astype(dk_ref.dtype)
            dv_ref[...] = dv_acc[...].astype(dv_ref.dtype)

    blk = pl.BlockSpec((tq, LANES), lambda p, i: (i, p))
    full = pl.BlockSpec((s, LANES), lambda p, i: (0, p))
    outs, couts = _call(
        body,
        grid=(N_PAIR, nq),
        in_specs=[blk,
                  pl.BlockSpec((s, LANES), lambda p, i: (0, N_PAIR + p)),
                  pl.BlockSpec((s, LANES), lambda p, i: (0, 2 * N_PAIR + p)),
                  blk],
        out_specs=[blk, full, full],
        out_shape=[jax.ShapeDtypeStruct((s, SB_W), MXU)] * 3,
        scratch_shapes=[pltpu.VMEM((s, LANES), F32), pltpu.VMEM((s, LANES), F32),
                        pltpu.VMEM((2, nq, tq, 1), F32)],
        args=(qkv, qkv, qkv, do), name=name, comm=comm)
    return tuple(outs) if comm is None else (tuple(outs), couts)


def _bucket_table():
    i = np.arange(BLOCK)[:, None]
    j = np.arange(2 * BLOCK)[None, :]
    d = np.maximum(BLOCK + i - j, 0)
    max_exact = N_BUCKETS // 2
    df = np.maximum(d, 1).astype(np.float32)
    large = max_exact + (np.log(df / max_exact) / math.log(MAX_DISTANCE / max_exact)
                         * (N_BUCKETS - max_exact)).astype(np.int32)
    large = np.minimum(large, N_BUCKETS - 1)
    return np.where(d < max_exact, d, large).astype(np.int32)


def _build_bias(rel_bias, buckets, name):
    def body(rb_ref, bk_ref, out_ref):
        h = pl.program_id(0)
        bk = bk_ref[...]
        acc = jnp.zeros(bk.shape, F32)
        for b in range(N_BUCKETS):
            acc = jnp.where(bk == b, rb_ref[b, h], acc)
        out_ref[...] = acc

    return pl.pallas_call(
        body,
        grid=(SW_HEADS,),
        in_specs=[pl.BlockSpec(memory_space=pltpu.SMEM),
                  pl.BlockSpec((BLOCK, 2 * BLOCK), lambda h: (0, 0))],
        out_specs=pl.BlockSpec((None, BLOCK, 2 * BLOCK), lambda h: (h, 0, 0)),
        out_shape=jax.ShapeDtypeStruct((SW_HEADS, BLOCK, 2 * BLOCK), F32),
        name=name,
    )(rel_bias, buckets)


def _bias_grad(dbias_layers, buckets, name):
    n_l = len(dbias_layers)

    def body(*refs):
        bk = refs[n_l][...]
        out_ref = refs[n_l + 1]
        db = refs[0][...]
        for r in refs[1:n_l]:
            db = db + r[...]
        lane = lax.broadcasted_iota(jnp.int32, (1, LANES), 1)
        acc = jnp.zeros((1, LANES), F32)
        for b in range(N_BUCKETS):
            part = jnp.sum(jnp.where(bk == b, db, 0.0), axis=1, keepdims=True)
            tot = jnp.sum(part, axis=0, keepdims=True)
            acc = jnp.where(lane == b, tot, acc)
        out_ref[...] = acc

    hspec = pl.BlockSpec((None, BLOCK, 2 * BLOCK), lambda h: (h, 0, 0))
    return pl.pallas_call(
        body,
        grid=(SW_HEADS,),
        in_specs=[hspec] * n_l + [pl.BlockSpec((BLOCK, 2 * BLOCK), lambda h: (0, 0))],
        out_specs=pl.BlockSpec((None, 1, LANES), lambda h: (h, 0, 0)),
        out_shape=jax.ShapeDtypeStruct((SW_HEADS, 1, LANES), F32),
        name=name,
    )(*dbias_layers, buckets)


GROUP_ROWS = SW_GROUP * BLOCK


def _group_lanes(g):
    lane = lax.broadcasted_iota(jnp.int32, (1, LANES), 1)
    gvec = jnp.zeros((1, LANES), jnp.int32) + g
    return jnp.where(lane >= HEAD_DIM, 1, 0) == gvec, gvec


def _stack_heads(x, g):
    kv_lanes, gvec = _group_lanes(g)
    parts = []
    for j in range(SW_GROUP):
        half = x[:, (j // 2) * LANES:(j // 2 + 1) * LANES]
        moved = jnp.where(gvec == j % 2, half, pltpu.roll(half, HEAD_DIM, 1))
        parts.append(jnp.where(kv_lanes, moved, 0.0))
    return jnp.concatenate(parts, axis=0)


def _unstack_heads(y, g):
    _, gvec = _group_lanes(g)
    heads = []
    for j in range(SW_GROUP):
        yj = y[j * BLOCK:(j + 1) * BLOCK]
        heads.append(jnp.where(gvec == j % 2, yj, pltpu.roll(yj, HEAD_DIM, 1)))
    pairs = [jnp.where(_head_lanes(0), heads[2 * p], heads[2 * p + 1]) for p in range(SW_GROUP // 2)]
    return jnp.concatenate(pairs, axis=1)


def _per_head_col(values):
    return jnp.concatenate([jnp.zeros((BLOCK, 1), F32) + v for v in values], axis=0)


def _swa_scores(qs, kp, kc, bias_ref, n):
    row = jnp.bitwise_and(lax.broadcasted_iota(jnp.int32, (GROUP_ROWS, BLOCK), 0), BLOCK - 1)
    col = lax.broadcasted_iota(jnp.int32, (GROUP_ROWS, BLOCK), 1)
    bias = bias_ref[...].reshape(GROUP_ROWS, 2 * BLOCK)
    s1 = _dot_nt(qs, kp) + bias[:, :BLOCK]
    s2 = _dot_nt(qs, kc) + bias[:, BLOCK:]
    no_prev = jnp.where(n > 0, 0, BLOCK)
    s1 = jnp.where(col > row + no_prev, s1, NEG)
    s2 = jnp.where(col <= row, s2, NEG)
    return s1, s2


def _swa_specs(s):
    q_blk = 3 * SB_W // (2 * LANES)
    k_blk = (3 * SB_W + SW_W) // LANES
    return (pl.BlockSpec((s, 2 * LANES), lambda g: (0, q_blk + g)),
            pl.BlockSpec((s, LANES), lambda g: (0, k_blk)),
            pl.BlockSpec((s, LANES), lambda g: (0, k_blk + 1)))


def _swa_fwd(qkv, bias, sinks, name, comm=None):
    s = qkv.shape[0]
    nb = s // BLOCK

    def body(sink_ref, q_ref, k_ref, v_ref, bias_ref, o_ref, lse_ref):
        g = pl.program_id(0)
        sink = _per_head_col([sink_ref[SW_GROUP * g + j] for j in range(SW_GROUP)])
        lane = lax.broadcasted_iota(jnp.int32, (1, LANES), 1)

        def step(n, carry):
            r0 = pl.multiple_of(n * BLOCK, BLOCK)
            p0 = pl.multiple_of(jnp.maximum(n - 1, 0) * BLOCK, BLOCK)
            cur, prev = pl.ds(r0, BLOCK), pl.ds(p0, BLOCK)
            qs = _stack_heads(q_ref[cur, :].astype(F32) * SCALE, g).astype(MXU)
            s1, s2 = _swa_scores(qs, k_ref[prev, :], k_ref[cur, :], bias_ref, n)
            m = jnp.maximum(jnp.maximum(jnp.max(s1, axis=1, keepdims=True),
                                        jnp.max(s2, axis=1, keepdims=True)), sink)
            e1 = jnp.exp(s1 - m)
            e2 = jnp.exp(s2 - m)
            den = jnp.sum(e1, axis=1, keepdims=True) + jnp.sum(e2, axis=1, keepdims=True) + jnp.exp(sink - m)
            o = _dot((e1 / den).astype(MXU), v_ref[prev, :]) + _dot((e2 / den).astype(MXU), v_ref[cur, :])
            o_ref[cur, :] = _unstack_heads(o, g).astype(o_ref.dtype)
            lse = m + jnp.log(den)
            lse_row = jnp.zeros((BLOCK, LANES), F32)
            for j in range(SW_GROUP):
                lse_row = jnp.where(lane == j, lse[j * BLOCK:(j + 1) * BLOCK], lse_row)
            lse_ref[cur, :] = lse_row
            return carry

        lax.fori_loop(0, nb, step, 0, unroll=2)

    outs, couts = _call(
        body,
        grid=(SW_KV,),
        in_specs=[pl.BlockSpec(memory_space=pltpu.SMEM), *_swa_specs(s),
                  pl.BlockSpec((SW_GROUP, BLOCK, 2 * BLOCK), lambda g: (g, 0, 0))],
        out_specs=[pl.BlockSpec((s, 2 * LANES), lambda g: (0, g)),
                   pl.BlockSpec((None, s, LANES), lambda g: (g, 0, 0))],
        out_shape=[jax.ShapeDtypeStruct((s, SW_W), MXU), jax.ShapeDtypeStruct((SW_KV, s, LANES), F32)],
        args=(sinks, qkv, qkv, qkv, bias), name=name, comm=comm)
    return tuple(outs) if comm is None else (tuple(outs), couts)


def _swa_bwd(qkv, bias, sinks, do, lse, name, comm=None):
    s = qkv.shape[0]
    nb = s // BLOCK

    def body(sink_ref, q_ref, k_ref, v_ref, bias_ref, do_ref, lse_ref,
             dq_ref, dk_ref, dv_ref, dbias_ref, dsink_ref, dk_acc, dv_acc):
        g = pl.program_id(0)
        sink = _per_head_col([sink_ref[SW_GROUP * g + j] for j in range(SW_GROUP)])
        lane = lax.broadcasted_iota(jnp.int32, (1, LANES), 1)

        @pl.when(g == 0)
        def _():
            dk_acc[...] = jnp.zeros_like(dk_acc)
            dv_acc[...] = jnp.zeros_like(dv_acc)

        dbias_ref[...] = jnp.zeros_like(dbias_ref)

        def step(n, dsink_rows):
            r0 = pl.multiple_of(n * BLOCK, BLOCK)
            p0 = pl.multiple_of(jnp.maximum(n - 1, 0) * BLOCK, BLOCK)
            cur, prev = pl.ds(r0, BLOCK), pl.ds(p0, BLOCK)
            qs = _stack_heads(q_ref[cur, :].astype(F32) * SCALE, g).astype(MXU)
            dos = _stack_heads(do_ref[cur, :].astype(F32), g).astype(MXU)
            kp, kc, vp, vc = k_ref[prev, :], k_ref[cur, :], v_ref[prev, :], v_ref[cur, :]
            lse_row = lse_ref[cur, :]
            lse = jnp.concatenate([jnp.sum(jnp.where(lane == j, lse_row, 0.0), axis=1, keepdims=True)
                                   for j in range(SW_GROUP)], axis=0)
            s1, s2 = _swa_scores(qs, kp, kc, bias_ref, n)
            pr1 = jnp.exp(s1 - lse)
            pr2 = jnp.exp(s2 - lse)
            dpr1 = _dot_nt(dos, vp)
            dpr2 = _dot_nt(dos, vc)
            delta = jnp.sum(pr1 * dpr1, axis=1, keepdims=True) + jnp.sum(pr2 * dpr2, axis=1, keepdims=True)
            ds1 = pr1 * (dpr1 - delta)
            ds2 = pr2 * (dpr2 - delta)
            dbias_ref[:, :, :BLOCK] += ds1.reshape(SW_GROUP, BLOCK, BLOCK)
            dbias_ref[:, :, BLOCK:] += ds2.reshape(SW_GROUP, BLOCK, BLOCK)
            ds1b, ds2b = ds1.astype(MXU), ds2.astype(MXU)
            dq = _dot(ds1b, kp) + _dot(ds2b, kc)
            dq_ref[cur, :] = (_unstack_heads(dq, g) * SCALE).astype(dq_ref.dtype)
            dk_acc[prev, :] += _dot_tn(ds1b, qs)
            dk_acc[cur, :] += _dot_tn(ds2b, qs)
            dv_acc[prev, :] += _dot_tn(pr1.astype(MXU), dos)
            dv_acc[cur, :] += _dot_tn(pr2.astype(MXU), dos)
            return dsink_rows - jnp.exp(sink - lse) * delta

        rows = lax.fori_loop(0, nb, step, jnp.zeros((GROUP_ROWS, 1), F32), unroll=2)
        for j in range(SW_GROUP):
            dsink_ref[j] = jnp.broadcast_to(jnp.sum(rows[j * BLOCK:(j + 1) * BLOCK], axis=0, keepdims=True),
                                            (1, LANES))

        @pl.when(g == SW_KV - 1)
        def _():
            dk_ref[...] = dk_acc[...].astype(dk_ref.dtype)
            dv_ref[...] = dv_acc[...].astype(dv_ref.dtype)

    grp = pl.BlockSpec((s, 2 * LANES), lambda g: (0, g))
    kv_out = pl.BlockSpec((s, LANES), lambda g: (0, 0))
    bspec = pl.BlockSpec((SW_GROUP, BLOCK, 2 * BLOCK), lambda g: (g, 0, 0))
    outs, couts = _call(
        body,
        grid=(SW_KV,),
        in_specs=[pl.BlockSpec(memory_space=pltpu.SMEM), *_swa_specs(s), bspec, grp,
                  pl.BlockSpec((None, s, LANES), lambda g: (g, 0, 0))],
        out_specs=[grp, kv_out, kv_out, bspec, pl.BlockSpec((SW_GROUP, 1, LANES), lambda g: (g, 0, 0))],
        out_shape=[jax.ShapeDtypeStruct((s, SW_W), MXU),
                   jax.ShapeDtypeStruct((s, LANES), MXU),
                   jax.ShapeDtypeStruct((s, LANES), MXU),
                   jax.ShapeDtypeStruct((SW_HEADS, BLOCK, 2 * BLOCK), F32),
                   jax.ShapeDtypeStruct((SW_HEADS, 1, LANES), F32)],
        scratch_shapes=[pltpu.VMEM((s, LANES), F32), pltpu.VMEM((s, LANES), F32)],
        args=(sinks, qkv, qkv, qkv, bias, do, lse), name=name, comm=comm)
    return tuple(outs) if comm is None else (tuple(outs), couts)


class _NoPlan:
    def comm(self, name):
        return None

    def done(self, name, outs):
        pass

    def grad(self, layer, name, value):
        pass

    def layer_done(self, layer):
        pass


def _run(plan, fn, *args, name, **kw):
    comm = plan.comm(name)
    if comm is None:
        return fn(*args, name=name, **kw)
    res, outs = fn(*args, name=name, comm=comm, **kw)
    plan.done(name, outs)
    return res


def _layer_fwd(x, p, w, g_mix, g_mlp, g_pe, sinks, bias, tag, plan):
    h1, r1 = _run(plan, _rms_fwd, x, g_mix, name=f"rms_mix_{tag}")
    qkv = _run(plan, _mm, h1, w["w_qkv"], out_dtypes=(MXU,), name=f"proj_qkv_{tag}")
    gates = _run(plan, _mm, h1, w["w_gate"], out_dtypes=(MXU,), name=f"proj_gate_{tag}")
    oa = _run(plan, _sb_fwd, qkv, name=f"sb_fwd_{tag}")
    ob, lse = _run(plan, _swa_fwd, qkv, bias, sinks, name=f"swa_fwd_{tag}")
    merged = _mix_fwd(oa, ob, w["w_up_a"], w["w_up_b"], gates, f"mix_fwd_{tag}")
    x1 = _run(plan, _mm, merged, w["w_o"], extras=(x,), epi=lambda acc, res: res + acc, name=f"out_proj_{tag}")
    h2, r2 = _rms_fwd(x1, g_mlp, f"rms_mlp_{tag}")
    u, act = _run(plan, _mm, h2, w["w_ff1"], epi=lambda acc: (acc, jnp.square(jnp.maximum(acc, 0.0))),
                  out_dtypes=(MXU, MXU), name=f"ff1_{tag}")
    x2 = _run(plan, _mm, act, w["w_ff2"], extras=(x1,), epi=lambda acc, res: res + acc, name=f"ff2_{tag}")
    h3, r3 = _rms_fwd(x2, g_pe, f"rms_pe_{tag}")
    x3 = _ple(p, w["w_pe"], h3, w["w_pg"], x2, backward=False, name=f"ple_fwd_{tag}")
    saved = dict(x=x, h1=h1, r1=r1, gates=gates, qkv=qkv, lse=lse, oa=oa, ob=ob, merged=merged,
                 x1=x1, h2=h2, r2=r2, u=u, act=act, x2=x2, h3=h3, r3=r3)
    return x3, saved


def _layer_bwd(dx3, sv, p, w, g_mix, g_mlp, g_pe, sinks, bias, layer, plan):
    tag = f"l{layer}"
    gw = {}
    wire = (WIRE,)

    def dw(name, a, b):
        gw[name] = _run(plan, _mm, a, b, ta=True, out_dtypes=wire, name=f"d{name}_{tag}")
        plan.grad(layer, name, gw[name])

    dpe, dgt = _ple(p, w["w_pe"], sv["h3"], w["w_pg"], dx3, backward=True, name=f"ple_bwd_{tag}")
    dw("w_pe", p, dpe)
    dw("w_pg", sv["h3"], dgt)
    dh3 = _run(plan, _mm, dgt, w["w_pg"], tb=True, name=f"dh_pe_{tag}")
    dx2, dx2b, dg_pe = _rms_bwd(sv["x2"], sv["r3"], g_pe, dh3, dx3, f"rms_pe_bwd_{tag}")
    dw("w_ff2", sv["act"], dx2b)
    du = _run(plan, _mm, dx2b, w["w_ff2"], tb=True, extras=(sv["u"],),
              epi=lambda acc, u: acc * (2.0 * jnp.maximum(u.astype(F32), 0.0)), out_dtypes=(MXU,),
              name=f"dact_{tag}")
    dw("w_ff1", sv["h2"], du)
    dh2 = _run(plan, _mm, du, w["w_ff1"], tb=True, name=f"dh_mlp_{tag}")
    dx1, dx1b, dg_mlp = _rms_bwd(sv["x1"], sv["r2"], g_mlp, dh2, dx2, f"rms_mlp_bwd_{tag}")
    dw("w_o", sv["merged"], dx1b)
    dya, dyb, dga, dgb = _mix_bwd(dx1b, w["w_o"], sv["oa"], sv["ob"], w["w_up_a"], w["w_up_b"],
                                  sv["gates"], f"mix_bwd_{tag}")
    dw("w_up_a", sv["oa"], dya)
    dw("w_up_b", sv["ob"], dyb)
    doa = _run(plan, _mm, dya, w["w_up_a"], tb=True, out_dtypes=(MXU,), name=f"do_a_{tag}")
    dob = _run(plan, _mm, dyb, w["w_up_b"], tb=True, out_dtypes=(MXU,), name=f"do_b_{tag}")
    dqb, dkb, dvb, dbias, dsink = _run(plan, _swa_bwd, sv["qkv"], bias, sinks, dob, sv["lse"],
                                       name=f"swa_bwd_{tag}")
    dqa, dka, dva = _run(plan, _sb_bwd, sv["qkv"], doa, name=f"sb_bwd_{tag}")
    dqkv = jnp.concatenate([dqa, dka, dva, dqb, dkb, dvb], axis=1)
    gw_qkv = _mm(sv["h1"], dqkv, ta=True, out_dtypes=wire, name=f"dw_qkv_{tag}")
    gw_ga = _mm(sv["h1"], dga, ta=True, out_dtypes=wire, name=f"dw_ga_{tag}")
    gw_gb = _mm(sv["h1"], dgb, ta=True, out_dtypes=wire, name=f"dw_gb_{tag}")
    gw["w_in"] = jnp.concatenate([gw_qkv, gw_ga, gw_gb], axis=1)
    plan.grad(layer, "w_in", gw["w_in"])
    d = dga.shape[1]
    add = lambda acc, res: res + acc
    dh1 = _mm(dqkv, w["w_qkv"], tb=True, tk=768, name=f"dh_qkv_{tag}")
    dh1 = _mm(dga, w["w_gate"][:, :d], tb=True, extras=(dh1,), epi=add, name=f"dh_ga_{tag}")
    dh1 = _mm(dgb, w["w_gate"][:, d:], tb=True, extras=(dh1,), epi=add, name=f"dh_gb_{tag}")
    dx, _, dg_mix = _rms_bwd(sv["x"], sv["r1"], g_mix, dh1, dx1, f"rms_mix_bwd_{tag}")
    small = dict(g_mix=dg_mix, g_mlp=dg_mlp, g_pe=dg_pe, sinks=dsink[:, 0, 0], dbias=dbias)
    return dx, gw, small


def _local_step(x, p, target, weights, g_mix, g_mlp, g_pe, g_final, sinks, rel_bias, plan=None):
    plan = _NoPlan() if plan is None else plan
    depth = g_mix.shape[0]
    buckets = jnp.asarray(_bucket_table())
    bias = _build_bias(rel_bias, buckets, "build_bias")
    saved, wfull = [], []
    h = x
    for l in range(depth):
        wfull.append(weights(l))
        h, sv = _layer_fwd(h, p[l], wfull[l], g_mix[l:l + 1], g_mlp[l:l + 1], g_pe[l:l + 1],
                           sinks[l], bias, f"l{l}", plan)
        saved.append(sv)
    loss_row, dx, dg_final = _loss_head(h, g_final[None, :], target, "loss_head")
    gws = [None] * depth
    smalls = [None] * depth
    for l in reversed(range(depth)):
        dx, gws[l], smalls[l] = _layer_bwd(dx, saved[l], p[l], wfull[l], g_mix[l:l + 1], g_mlp[l:l + 1],
                                           g_pe[l:l + 1], sinks[l], bias, l, plan)
        plan.layer_done(l)
    drel = _bias_grad([sm["dbias"] for sm in smalls], buckets, "bias_grad")[:, 0, :N_BUCKETS].T
    small = dict(
        g_mix=jnp.concatenate([sm["g_mix"] for sm in smalls], axis=0),
        g_mlp=jnp.concatenate([sm["g_mlp"] for sm in smalls], axis=0),
        g_pe=jnp.concatenate([sm["g_pe"] for sm in smalls], axis=0),
        g_final=dg_final[0],
        sinks=jnp.stack([sm["sinks"] for sm in smalls], axis=0),
        rel_bias=drel,
    )
    return loss_row, dx, gws, small


MESH_ID = pl.DeviceIdType.MESH
ANY = pl.BlockSpec(memory_space=pl.ANY)


def _position():
    return lax.axis_index("x"), lax.axis_index("y"), lax.axis_index("c")


def _run_comm(comm, name):
    ci, co = len(comm.inputs), len(comm.out_shapes)

    def body(*refs):
        cin, cout, csem = refs[:ci], refs[ci:ci + co], refs[ci + co:]
        pos = _position()
        comm.start(pos, cin, cout, csem)
        comm.finish(pos, cin, cout, csem)

    return pl.pallas_call(body, out_shape=comm.out_shapes, in_specs=[ANY] * ci, out_specs=[ANY] * co,
                          scratch_shapes=comm.sems, name=name)(*comm.inputs)


def _gather_comm(shards):
    n = len(shards)

    def copies(pos, x_refs, out_refs, sems):
        send_sems, recv_sems, local_sems = sems
        x, y, c = pos
        me, sibling = (x, y, c), (x, y, 1 - c)
        chips = [(1 - x, y), (x, 1 - y), (1 - x, 1 - y)]

        def slot(a, px, py, pc):
            return out_refs[a].at[4 * px + 2 * py + pc]

        def copy(a, k, block, to, src=None):
            return pltpu.make_async_remote_copy(
                src_ref=slot(a, *block) if src is None else src, dst_ref=slot(a, *block),
                send_sem=send_sems.at[a, k], recv_sem=recv_sems.at[a, k],
                device_id=to, device_id_type=MESH_ID)

        mine = [pltpu.make_async_copy(x_refs[a], slot(a, *me), local_sems.at[a]) for a in range(n)]
        first = []
        for a in range(n):
            first.append(copy(a, 0, me, sibling, src=x_refs[a]))
            first += [copy(a, 1 + j, me, (*chip, c), src=x_refs[a]) for j, chip in enumerate(chips)]
        return me, sibling, chips, copy, mine, first

    def start(pos, x_refs, out_refs, sems):
        _, _, _, _, mine, first = copies(pos, x_refs, out_refs, sems)
        for cp in mine + first:
            cp.start()

    def finish(pos, x_refs, out_refs, sems):
        me, sibling, chips, copy, mine, first = copies(pos, x_refs, out_refs, sems)
        c = pos[2]
        passed = []
        for j, chip in enumerate(chips):
            for a in range(n):
                copy(a, 1 + j, (*chip, c), me).wait_recv()
                fwd = copy(a, 4 + j, (*chip, c), sibling)
                fwd.start()
                passed.append(fwd)
        for a in range(n):
            copy(a, 0, sibling, me).wait_recv()
            for j, chip in enumerate(chips):
                copy(a, 4 + j, (*chip, 1 - c), me).wait_recv()
        for cp in first + passed:
            cp.wait_send()
        for cp in mine:
            cp.wait()

    return _Comm(shards, [jax.ShapeDtypeStruct((N_DEV,) + s.shape, s.dtype) for s in shards],
                 [pltpu.SemaphoreType.DMA((n, 7)), pltpu.SemaphoreType.DMA((n, 7)),
                  pltpu.SemaphoreType.DMA((n,))], start, finish)


def _exchange_comm(arrays, n_slots, route):
    n = len(arrays)

    def copies(pos, in_refs, out_refs, sems):
        send_sems, recv_sems = sems
        out = []
        for a in range(n):
            for j in range(n_slots):
                src_slot, peer = route(pos, j)
                out.append(pltpu.make_async_remote_copy(
                    src_ref=in_refs[a].at[src_slot], dst_ref=out_refs[a].at[j],
                    send_sem=send_sems.at[a, j], recv_sem=recv_sems.at[a, j],
                    device_id=peer, device_id_type=MESH_ID))
        return out

    def start(pos, in_refs, out_refs, sems):
        for cp in copies(pos, in_refs, out_refs, sems):
            cp.start()

    def finish(pos, in_refs, out_refs, sems):
        for cp in copies(pos, in_refs, out_refs, sems):
            cp.wait()

    return _Comm(arrays, [jax.ShapeDtypeStruct((n_slots,) + g.shape[1:], g.dtype) for g in arrays],
                 [pltpu.SemaphoreType.DMA((n, n_slots)), pltpu.SemaphoreType.DMA((n, n_slots))], start, finish)


def _rs_sibling_comm(gs):
    return _exchange_comm(gs, 4, lambda pos, j: (2 * j + (1 - pos[2]), (pos[0], pos[1], 1 - pos[2])))


def _chip_of(k, x, y):
    return x ^ ((k + 1) & 1), y ^ (((k + 1) >> 1) & 1)


def _chip_partials(pos, gs, recvs, name):
    n = len(gs)

    def body(pos_ref, *refs):
        for a in range(n):
            refs[2 * n + a][...] = (refs[a][...].astype(F32) + refs[n + a][...].astype(F32)
                                    ).astype(refs[2 * n + a].dtype)

    def g_map(k, pos_ref):
        cx, cy = _chip_of(k, pos_ref[0], pos_ref[1])
        return (4 * cx + 2 * cy + pos_ref[2], 0, 0)

    def r_map(k, pos_ref):
        cx, cy = _chip_of(k, pos_ref[0], pos_ref[1])
        return (2 * cx + cy, 0, 0)

    slab = [(None,) + g.shape[1:] for g in gs]
    return pl.pallas_call(
        body,
        grid_spec=pltpu.PrefetchScalarGridSpec(
            num_scalar_prefetch=1,
            grid=(4,),
            in_specs=[pl.BlockSpec(sh, g_map) for sh in slab] + [pl.BlockSpec(sh, r_map) for sh in slab],
            out_specs=[pl.BlockSpec(sh, lambda k, pos_ref: (k, 0, 0)) for sh in slab],
        ),
        out_shape=[jax.ShapeDtypeStruct((4,) + g.shape[1:], g.dtype) for g in gs],
        compiler_params=_cparams(),
        name=name,
    )(pos, *gs, *recvs)


def _rs_chips_comm(parts):
    return _exchange_comm(parts, 3, lambda pos, k: (k, (*_chip_of(k, pos[0], pos[1]), pos[2])))


def _adamw_math(w, g, m, v):
    m = ADAM_B1 * m + (1.0 - ADAM_B1) * g
    v = ADAM_B2 * v + (1.0 - ADAM_B2) * (g * g)
    m_hat = m / (1.0 - ADAM_B1 ** ADAM_STEP)
    v_hat = v / (1.0 - ADAM_B2 ** ADAM_STEP)
    delta = -ADAM_LR * (m_hat / (jnp.sqrt(v_hat) + ADAM_EPS) + ADAM_WD * w)
    return delta, m, v


def _adamw_weight(parts, recvs, w, m, v, name):
    depth, a, b = w.shape
    ta = _tile(a, 256, unit=16)
    ni = a // ta

    def body(*refs):
        p_refs, r_refs = refs[:depth], refs[depth:2 * depth]
        w_ref, m_ref, v_ref = refs[2 * depth:2 * depth + 3]
        g_out, d_out, m_out, v_out = refs[2 * depth + 3:]
        layer = pl.program_id(0)
        g = jnp.zeros((ta, b), F32)
        for l in range(depth):
            gl = p_refs[l][...].astype(F32)
            for k in range(3):
                gl = gl + r_refs[l][k].astype(F32)
            g = jnp.where(layer == l, gl, g)
        delta, m_new, v_new = _adamw_math(w_ref[...], g, m_ref[...], v_ref[...])
        g_out[...] = g
        d_out[...] = delta
        m_out[...] = m_new
        v_out[...] = v_new

    def hold(l):
        return lambda layer, i: jnp.where(layer == l, i, jnp.where(layer < l, 0, ni - 1))

    p_specs = [pl.BlockSpec((None, ta, b), (lambda layer, i, f=hold(l): (3, f(layer, i), 0))) for l in range(depth)]
    r_specs = [pl.BlockSpec((3, ta, b), (lambda layer, i, f=hold(l): (0, f(layer, i), 0))) for l in range(depth)]
    row = pl.BlockSpec((None, ta, b), lambda layer, i: (layer, i, 0))
    return pl.pallas_call(
        body,
        grid=(depth, ni),
        in_specs=p_specs + r_specs + [row, row, row],
        out_specs=[row] * 4,
        out_shape=[jax.ShapeDtypeStruct(w.shape, F32)] * 4,
        compiler_params=_cparams(),
        name=name,
    )(*parts, *recvs, w, m, v)


def _adamw_replicated(gathered, w, m, v, name):
    r, lanes = w.shape

    def body(g_ref, w_ref, m_ref, v_ref, g_out, d_out, m_out, v_out):
        g = g_ref[0]
        for k in range(1, N_DEV):
            g = g + g_ref[k]
        delta, m_new, v_new = _adamw_math(w_ref[...], g, m_ref[...], v_ref[...])
        g_out[...] = g
        d_out[...] = delta
        m_out[...] = m_new
        v_out[...] = v_new

    return pl.pallas_call(
        body,
        out_shape=[jax.ShapeDtypeStruct((r, lanes), F32)] * 4,
        name=name,
    )(gathered, w, m, v)


def _full_weight(name, gathered):
    _, a, b = gathered.shape
    if name in COL_SHARDED:
        return gathered.transpose(1, 0, 2).reshape(a, N_DEV * b)
    return gathered.reshape(N_DEV * a, b)


def _to_slabs(name, gfull, shard_shape):
    a, b = shard_shape
    if name in COL_SHARDED:
        return gfull.reshape(a, N_DEV, b).transpose(1, 0, 2)
    return gfull.reshape(N_DEV, a, b)


def _pack_small(arrs):
    rows = []
    for a in arrs:
        flat = a.astype(F32).reshape(-1)
        pad = (-flat.shape[0]) % LANES
        rows.append(jnp.pad(flat, (0, pad)).reshape(-1, LANES))
    packed = jnp.concatenate(rows, axis=0)
    return jnp.pad(packed, ((0, (-packed.shape[0]) % 8), (0, 0)))


def _unpack_small(packed, shapes):
    out, off = [], 0
    for shp in shapes:
        n = math.prod(shp)
        rows = -(-n // LANES)
        out.append(packed[off:off + rows].reshape(-1)[:n].reshape(shp))
        off += rows
    return out


def _of(layer, *names):
    return tuple((layer, n) for n in names)


MIXER_W = ("w_in", "w_up_a", "w_up_b", "w_o")
MLP_W = ("w_ff1", "w_ff2", "w_pe", "w_pg")

GATHERS = (
    ("rms_mix_l0", _of(0, "w_in")),
    ("proj_qkv_l0", _of(0, "w_up_a", "w_up_b", "w_o")),
    ("proj_gate_l0", _of(0, "w_pe", "w_pg")),
    ("sb_fwd_l0", _of(0, "w_ff1", "w_ff2")),
    ("swa_fwd_l0", _of(1, "w_in")),
    ("sb_fwd_l1", _of(1, "w_up_a", "w_up_b", "w_o", "w_pe", "w_pg", "w_ff1")),
    ("swa_fwd_l1", _of(1, "w_ff2")),
)
REDUCES = (
    (_of(1, "w_ff1"), "dw_ff2_l0", "dact_l0"),
    (_of(1, "w_ff2"), "dw_ff2_l0", "dw_ff1_l0"),
    (_of(1, "w_in"), "dw_ff2_l0", "sb_bwd_l0"),
    (_of(1, "w_up_a", "w_up_b", "w_o", "w_pe", "w_pg"), "dw_ff2_l0", "swa_bwd_l0"),
    (_of(0, *MLP_W), "dh_mlp_l0", "sb_bwd_l0"),
    (_of(0, "w_o", "w_up_a", "w_up_b"), "do_a_l0", "sb_bwd_l0"),
    (_of(0, "w_in"), None, None),
)


def _merge_comms(comms):
    if len(comms) == 1:
        return comms[0]

    def cuts(counts):
        edges = [0]
        for c in counts:
            edges.append(edges[-1] + c)
        return [slice(a, b) for a, b in zip(edges[:-1], edges[1:])]

    s_in = cuts([len(c.inputs) for c in comms])
    s_out = cuts([len(c.out_shapes) for c in comms])
    s_sem = cuts([len(c.sems) for c in comms])

    def start(pos, cin, cout, csem):
        for c, i, o, s in zip(comms, s_in, s_out, s_sem):
            c.start(pos, cin[i], cout[o], csem[s])

    def finish(pos, cin, cout, csem):
        for c, i, o, s in zip(comms, s_in, s_out, s_sem):
            c.finish(pos, cin[i], cout[o], csem[s])

    return _Comm(sum([c.inputs for c in comms], []), sum([c.out_shapes for c in comms], []),
                 sum([c.sems for c in comms], []), start, finish)


class _LayerWeights:
    def __init__(self, full, layer):
        self.full, self.layer, self.cache = full, layer, {}

    def __getitem__(self, name):
        if name not in self.cache:
            if name == "w_qkv":
                self.cache[name] = self.full[(self.layer, "w_in")][:, :QKV_COLS]
            elif name == "w_gate":
                self.cache[name] = self.full[(self.layer, "w_in")][:, QKV_COLS:]
            else:
                self.cache[name] = self.full[(self.layer, name)]
        return self.cache[name]


class _Plan:
    def __init__(self, w_sh, pos):
        self.w_sh = dict(zip(WEIGHTS, w_sh))
        self.pos = pos
        self.full, self.gw, self.parts, self.recv = {}, {}, {}, {}
        self.slabs = {}
        self.hosted = {}
        for i, (host, _) in enumerate(GATHERS):
            if host is not None:
                self.hosted.setdefault(host, []).append(("gather", i))
        for i, (_, sib_host, chip_host) in enumerate(REDUCES):
            assert (sib_host is None) == (chip_host is None)
            if sib_host is not None:
                self.hosted.setdefault(sib_host, []).append(("sibling", i))
                self.hosted.setdefault(chip_host, []).append(("chips", i))

    def _gather(self, i):
        return _gather_comm([self.w_sh[n][layer].astype(WIRE) for layer, n in GATHERS[i][1]])

    def _gathered(self, i, outs):
        for (layer, n), g in zip(GATHERS[i][1], outs):
            self.full[(layer, n)] = _full_weight(n, g)

    def weights(self, layer):
        for i, (host, items) in enumerate(GATHERS):
            if host is None and items[0][0] == layer:
                self._gathered(i, _run_comm(self._gather(i), f"gather_{i}"))
        return _LayerWeights(self.full, layer)

    def grad(self, layer, name, value):
        self.gw[(layer, name)] = value

    def _sibling(self, i):
        self.slabs[i] = [_to_slabs(n, self.gw[(layer, n)], self.w_sh[n].shape[1:]) for layer, n in REDUCES[i][0]]
        return _rs_sibling_comm(self.slabs[i])

    def _sibling_done(self, i, outs):
        parts = _chip_partials(self.pos, self.slabs[i], outs, f"chip_partials_{i}")
        for item, part in zip(REDUCES[i][0], parts):
            self.parts[item] = part

    def _chips(self, i):
        return _rs_chips_comm([self.parts[item] for item in REDUCES[i][0]])

    def _chips_done(self, i, outs):
        for item, r in zip(REDUCES[i][0], outs):
            self.recv[item] = r

    def layer_done(self, layer):
        for i, (items, sib_host, _) in enumerate(REDUCES):
            if sib_host is None and items[0][0] == layer:
                self._sibling_done(i, _run_comm(self._sibling(i), f"reduce_sibling_{i}"))
                self._chips_done(i, _run_comm(self._chips(i), f"reduce_chips_{i}"))

    def comm(self, name):
        if name not in self.hosted:
            return None
        make = {"gather": self._gather, "sibling": self._sibling, "chips": self._chips}
        return _merge_comms([make[kind](i) for kind, i in self.hosted[name]])

    def done(self, name, outs):
        took = {"gather": self._gathered, "sibling": self._sibling_done, "chips": self._chips_done}
        off = 0
        for kind, i in self.hosted[name]:
            n = len(GATHERS[i][1]) if kind == "gather" else len(REDUCES[i][0])
            took[kind](i, outs[off:off + n])
            off += n


def kernel(x, p, w_in, w_up_a, w_up_b, w_o, w_ff1, w_ff2, w_pe, w_pg, g_mix, g_mlp, g_pe, g_final, sinks, rel_bias, loss_target, m_w_in, m_w_up_a, m_w_up_b, m_w_o, m_w_ff1, m_w_ff2, m_w_pe, m_w_pg, m_g_mix, m_g_mlp, m_g_pe, m_g_final, m_sinks, m_rel_bias, v_w_in, v_w_up_a, v_w_up_b, v_w_o, v_w_ff1, v_w_ff2, v_w_pe, v_w_pg, v_g_mix, v_g_mlp, v_g_pe, v_g_final, v_sinks, v_rel_bias):
    w_sh = [w_in, w_up_a, w_up_b, w_o, w_ff1, w_ff2, w_pe, w_pg]
    m_sh = [m_w_in, m_w_up_a, m_w_up_b, m_w_o, m_w_ff1, m_w_ff2, m_w_pe, m_w_pg]
    v_sh = [v_w_in, v_w_up_a, v_w_up_b, v_w_o, v_w_ff1, v_w_ff2, v_w_pe, v_w_pg]
    depth = w_in.shape[0]
    assert depth == 2 and x.shape[-1] * 2 + QKV_COLS == w_in.shape[2] * N_DEV

    px, py, pc = _position()
    plan = _Plan(w_sh, jnp.stack([px, py, pc]).astype(jnp.int32))
    loss_row, grad_x, _, small = _local_step(
        x[0], p[:, 0], loss_target[0], plan.weights, g_mix, g_mlp, g_pe, g_final, sinks, rel_bias, plan=plan)

    grad_w, delta_w, new_m, new_v = [], [], [], []
    for a, name in enumerate(WEIGHTS):
        outs = _adamw_weight([plan.parts[(l, name)] for l in range(depth)],
                             [plan.recv[(l, name)] for l in range(depth)],
                             w_sh[a], m_sh[a], v_sh[a], f"adamw_{name}")
        for lst, o in zip((grad_w, delta_w, new_m, new_v), outs):
            lst.append(o)

    small_w = [g_mix, g_mlp, g_pe, g_final, sinks, rel_bias]
    small_m = [m_g_mix, m_g_mlp, m_g_pe, m_g_final, m_sinks, m_rel_bias]
    small_v = [v_g_mix, v_g_mlp, v_g_pe, v_g_final, v_sinks, v_rel_bias]
    small_shapes = [a.shape for a in small_w] + [(1,)]
    zero = jnp.zeros((1,), F32)
    small_g = _pack_small([small[n] for n in SMALL] + [loss_row[0, :1]])
    small_all = _run_comm(_gather_comm([small_g]), "gather_small")[0]
    packed_s = _adamw_replicated(small_all, _pack_small(small_w + [zero]), _pack_small(small_m + [zero]),
                                 _pack_small(small_v + [zero + 1.0]), "adamw_replicated")
    sg, sd, sm, sv = [_unpack_small(t, small_shapes) for t in packed_s]
    loss = sg[-1][0]

    return (loss, grad_x[None], *grad_w, *sg[:-1], *delta_w, *sd[:-1], *new_m, *sm[:-1], *new_v, *sv[:-1])
```

```python
import functools
import math

import numpy as np
import jax
import jax.numpy as jnp
from jax import lax
from jax.experimental import pallas as pl
from jax.experimental.pallas import tpu as pltpu

F32 = jnp.float32
MXU = jnp.bfloat16
WIRE = jnp.bfloat16

HEAD_DIM = 64
SB_HEADS = 8
SW_HEADS = 8
SW_KV = 2
SW_GROUP = SW_HEADS // SW_KV
BLOCK = 128
N_BUCKETS = 32
MAX_DISTANCE = 128
EPS = 1e-6
SCALE = HEAD_DIM ** -0.5
SB_W = SB_HEADS * HEAD_DIM
SW_W = SW_HEADS * HEAD_DIM
QKV_COLS = 3 * SB_W + SW_W + 2 * SW_KV * HEAD_DIM
N_DEV = 8
LANES = 128
N_PAIR = SB_HEADS // 2
NEG = -1e30

ADAM_LR = 0.001
ADAM_B1 = 0.9
ADAM_B2 = 0.999
ADAM_EPS = 1e-08
ADAM_WD = 0.01
ADAM_STEP = 10

VMEM_LIMIT = 48 * 1024 * 1024
SB_TQ = 256
SB_DEAD = -105.0

WEIGHTS = ("w_in", "w_up_a", "w_up_b", "w_o", "w_ff1", "w_ff2", "w_pe", "w_pg")
COL_SHARDED = ("w_in", "w_up_a", "w_up_b", "w_ff1", "w_pe")
SMALL = ("g_mix", "g_mlp", "g_pe", "g_final", "sinks", "rel_bias")


def _cparams(**kw):
    return pltpu.CompilerParams(vmem_limit_bytes=VMEM_LIMIT, **kw)


def _dot(a, b):
    return jnp.dot(a, b, preferred_element_type=F32)


def _dot_nt(a, b):
    return lax.dot_general(a, b, (((1,), (1,)), ((), ())), preferred_element_type=F32)


def _dot_tn(a, b):
    return lax.dot_general(a, b, (((0,), (0,)), ((), ())), preferred_element_type=F32)


def _tile(n, target, unit=LANES):
    if n <= target:
        return n
    t = (target // unit) * unit
    while t > unit and n % t:
        t -= unit
    assert n % t == 0, (n, target)
    return t


def _sigmoid(x):
    return 1.0 / (1.0 + jnp.exp(-x))


class _Comm:
    def __init__(self, inputs, out_shapes, sems, start, finish):
        self.inputs, self.out_shapes, self.sems = list(inputs), list(out_shapes), list(sems)
        self.start, self.finish = start, finish


def _call(body, *, grid, in_specs, out_specs, out_shape, scratch_shapes=(), args, name, comm=None):
    n_in, n_out, n_scr = len(in_specs), len(out_shape), len(scratch_shapes)
    if comm is None:
        outs = pl.pallas_call(body, grid=grid, in_specs=list(in_specs), out_specs=list(out_specs),
                              out_shape=list(out_shape), scratch_shapes=list(scratch_shapes),
                              compiler_params=_cparams(), name=name)(*args)
        return list(outs), None
    ci, co = len(comm.inputs), len(comm.out_shapes)
    any_spec = pl.BlockSpec(memory_space=pl.ANY)

    def wrapped(*refs):
        ins, cin = refs[:n_in], refs[n_in:n_in + ci]
        o0 = n_in + ci
        outs, cout = refs[o0:o0 + n_out], refs[o0 + n_out:o0 + n_out + co]
        s0 = o0 + n_out + co
        scr, csem = refs[s0:s0 + n_scr], refs[s0 + n_scr:]
        ids = [pl.program_id(d) for d in range(len(grid))]
        first = functools.reduce(jnp.logical_and, [i == 0 for i in ids])
        last = functools.reduce(jnp.logical_and, [i == g - 1 for i, g in zip(ids, grid)])
        pos = (lax.axis_index("x"), lax.axis_index("y"), lax.axis_index("c"))

        @pl.when(first)
        def _():
            comm.start(pos, cin, cout, csem)

        body(*ins, *outs, *scr)

        @pl.when(last)
        def _():
            comm.finish(pos, cin, cout, csem)

    outs = pl.pallas_call(wrapped, grid=grid, in_specs=list(in_specs) + [any_spec] * ci,
                          out_specs=list(out_specs) + [any_spec] * co,
                          out_shape=list(out_shape) + comm.out_shapes,
                          scratch_shapes=list(scratch_shapes) + comm.sems,
                          compiler_params=_cparams(), name=name)(*args, *comm.inputs)
    return list(outs[:n_out]), list(outs[n_out:])


def _mm(a, b, *, ta=False, tb=False, extras=(), epi=None, out_dtypes=(F32,),
        tm=1024, tn=1024, tk=1024, name, comm=None):
    if ta:
        kdim, m = a.shape
    else:
        m, kdim = a.shape
    n = b.shape[0] if tb else b.shape[1]
    assert (b.shape[1] if tb else b.shape[0]) == kdim
    tm, tn, tk = _tile(m, tm), _tile(n, tn), _tile(kdim, tk)
    nk = kdim // tk
    n_ex, n_out = len(extras), len(out_dtypes)

    a_spec = (pl.BlockSpec((tk, tm), lambda i, j, k: (k, i)) if ta
              else pl.BlockSpec((tm, tk), lambda i, j, k: (i, k)))
    b_spec = (pl.BlockSpec((tn, tk), lambda i, j, k: (j, k)) if tb
              else pl.BlockSpec((tk, tn), lambda i, j, k: (k, j)))
    ex_specs = []
    for e in extras:
        assert e.shape == (m, n), (e.shape, m, n)
        ex_specs.append(pl.BlockSpec((tm, tn), lambda i, j, k: (i, j)))
    out_spec = pl.BlockSpec((tm, tn), lambda i, j, k: (i, j))

    def body(a_ref, b_ref, *rest):
        ex_refs = rest[:n_ex]
        out_refs = rest[n_ex:n_ex + n_out]
        acc = rest[-1]
        k = pl.program_id(2)

        @pl.when(k == 0)
        def _():
            acc[...] = jnp.zeros_like(acc)

        av = a_ref[...].astype(MXU)
        bv = b_ref[...].astype(MXU)
        if ta:
            acc[...] += _dot_tn(av, bv)
        elif tb:
            acc[...] += _dot_nt(av, bv)
        else:
            acc[...] += _dot(av, bv)

        @pl.when(k == nk - 1)
        def _():
            res = acc[...]
            if epi is not None:
                res = epi(res, *[e[...] for e in ex_refs])
            if not isinstance(res, tuple):
                res = (res,)
            for o_ref, r in zip(out_refs, res):
                o_ref[...] = r.astype(o_ref.dtype)

    outs, couts = _call(
        body,
        grid=(m // tm, n // tn, nk),
        in_specs=[a_spec, b_spec] + ex_specs,
        out_specs=[out_spec] * n_out,
        out_shape=[jax.ShapeDtypeStruct((m, n), dt) for dt in out_dtypes],
        scratch_shapes=[pltpu.VMEM((tm, tn), F32)],
        args=(a, b, *extras), name=name, comm=comm)
    res = outs[0] if n_out == 1 else tuple(outs)
    return res if comm is None else (res, couts)


def _rms_fwd(x, g, name, comm=None):
    s, d = x.shape
    tr = _tile(s, 256)

    def body(x_ref, g_ref, h_ref, r_ref):
        xf = x_ref[...]
        r = lax.rsqrt(jnp.mean(xf * xf, axis=-1, keepdims=True) + EPS)
        h_ref[...] = ((xf * r) * g_ref[...]).astype(h_ref.dtype)
        r_ref[...] = r

    outs, couts = _call(
        body,
        grid=(s // tr,),
        in_specs=[pl.BlockSpec((tr, d), lambda i: (i, 0)), pl.BlockSpec((1, d), lambda i: (0, 0))],
        out_specs=[pl.BlockSpec((tr, d), lambda i: (i, 0)), pl.BlockSpec((tr, 1), lambda i: (i, 0))],
        out_shape=[jax.ShapeDtypeStruct((s, d), MXU), jax.ShapeDtypeStruct((s, 1), F32)],
        args=(x, g), name=name, comm=comm)
    return tuple(outs) if comm is None else (tuple(outs), couts)


def _rms_bwd(x, r, g, dh, dres, name):
    s, d = x.shape
    tr = _tile(s, 256)

    def body(x_ref, r_ref, g_ref, dh_ref, dres_ref, dx_ref, dxb_ref, dg_ref):
        @pl.when(pl.program_id(0) == 0)
        def _():
            dg_ref[...] = jnp.zeros_like(dg_ref)

        rr = r_ref[...]
        xhat = x_ref[...] * rr
        dh_v = dh_ref[...]
        dxhat = dh_v * g_ref[...]
        mean = jnp.mean(dxhat * xhat, axis=-1, keepdims=True)
        dx = dres_ref[...] + rr * (dxhat - xhat * mean)
        dx_ref[...] = dx
        dxb_ref[...] = dx.astype(dxb_ref.dtype)
        dg_ref[...] += jnp.sum(dh_v * xhat, axis=0, keepdims=True)

    row = pl.BlockSpec((tr, d), lambda i: (i, 0))
    vec = pl.BlockSpec((1, d), lambda i: (0, 0))
    return pl.pallas_call(
        body,
        grid=(s // tr,),
        in_specs=[row, pl.BlockSpec((tr, 1), lambda i: (i, 0)), vec, row, row],
        out_specs=[row, row, vec],
        out_shape=[jax.ShapeDtypeStruct((s, d), F32), jax.ShapeDtypeStruct((s, d), MXU),
                   jax.ShapeDtypeStruct((1, d), F32)],
        compiler_params=_cparams(),
        name=name,
    )(x, r, g, dh, dres)


def _loss_head(x, g, target, name):
    s, d = x.shape
    tr = _tile(s, 256)

    def body(x_ref, g_ref, t_ref, loss_ref, dx_ref, dg_ref):
        @pl.when(pl.program_id(0) == 0)
        def _():
            dg_ref[...] = jnp.zeros_like(dg_ref)
            loss_ref[...] = jnp.zeros_like(loss_ref)

        xf = x_ref[...]
        gv = g_ref[...]
        r = lax.rsqrt(jnp.mean(xf * xf, axis=-1, keepdims=True) + EPS)
        xhat = xf * r
        err = xhat * gv - t_ref[...]
        loss_ref[...] += 0.5 * jnp.sum(jnp.mean(err * err, axis=-1, keepdims=True), axis=0, keepdims=True)
        dy = err * (1.0 / d)
        dxhat = dy * gv
        mean = jnp.mean(dxhat * xhat, axis=-1, keepdims=True)
        dx_ref[...] = r * (dxhat - xhat * mean)
        dg_ref[...] += jnp.sum(dy * xhat, axis=0, keepdims=True)

    row = pl.BlockSpec((tr, d), lambda i: (i, 0))
    vec = pl.BlockSpec((1, d), lambda i: (0, 0))
    return pl.pallas_call(
        body,
        grid=(s // tr,),
        in_specs=[row, vec, row],
        out_specs=[pl.BlockSpec((1, LANES), lambda i: (0, 0)), row, vec],
        out_shape=[jax.ShapeDtypeStruct((1, LANES), F32), jax.ShapeDtypeStruct((s, d), F32),
                   jax.ShapeDtypeStruct((1, d), F32)],
        compiler_params=_cparams(),
        name=name,
    )(x, g, target)


def _mix_fwd(oa, ob, wa_t, wb_t, gates, name):
    s, kd = oa.shape
    d = wa_t.shape[0]
    tm, tn = _tile(s, 1024), _tile(d, 512)
    nj = d // tn

    def body(oa_ref, ob_ref, wa_ref, wb_ref, ga_ref, gb_ref, out_ref):
        ya = _dot_nt(oa_ref[...], wa_ref[...])
        yb = _dot_nt(ob_ref[...], wb_ref[...])
        out_ref[...] = (_sigmoid(ga_ref[...].astype(F32)) * ya
                        + _sigmoid(gb_ref[...].astype(F32)) * yb).astype(out_ref.dtype)

    o_spec = pl.BlockSpec((tm, kd), lambda i, j: (i, 0))
    w_spec = pl.BlockSpec((tn, kd), lambda i, j: (j, 0))
    return pl.pallas_call(
        body,
        grid=(s // tm, nj),
        in_specs=[o_spec, o_spec, w_spec, w_spec,
                  pl.BlockSpec((tm, tn), lambda i, j: (i, j)),
                  pl.BlockSpec((tm, tn), lambda i, j: (i, j + nj))],
        out_specs=pl.BlockSpec((tm, tn), lambda i, j: (i, j)),
        out_shape=jax.ShapeDtypeStruct((s, d), MXU),
        compiler_params=_cparams(),
        name=name,
    )(oa, ob, wa_t, wb_t, gates, gates)


def _mix_bwd(dx, w_o, oa, ob, wa_t, wb_t, gates, name):
    s, kd = oa.shape
    d = wa_t.shape[0]
    tm, tn = _tile(s, 1024), _tile(d, 512)
    nj = d // tn

    def body(dx_ref, wo_ref, oa_ref, ob_ref, wa_ref, wb_ref, ga_ref, gb_ref,
             dya_ref, dyb_ref, dga_ref, dgb_ref):
        dm = _dot_nt(dx_ref[...], wo_ref[...])
        ya = _dot_nt(oa_ref[...], wa_ref[...])
        yb = _dot_nt(ob_ref[...], wb_ref[...])
        sa = _sigmoid(ga_ref[...].astype(F32))
        sb = _sigmoid(gb_ref[...].astype(F32))
        dya_ref[...] = (dm * sa).astype(dya_ref.dtype)
        dyb_ref[...] = (dm * sb).astype(dyb_ref.dtype)
        dga_ref[...] = (dm * ya * sa * (1.0 - sa)).astype(dga_ref.dtype)
        dgb_ref[...] = (dm * yb * sb * (1.0 - sb)).astype(dgb_ref.dtype)

    o_spec = pl.BlockSpec((tm, kd), lambda i, j: (i, 0))
    w_spec = pl.BlockSpec((tn, kd), lambda i, j: (j, 0))
    t_spec = pl.BlockSpec((tm, tn), lambda i, j: (i, j))
    return pl.pallas_call(
        body,
        grid=(s // tm, nj),
        in_specs=[pl.BlockSpec((tm, d), lambda i, j: (i, 0)),
                  pl.BlockSpec((tn, d), lambda i, j: (j, 0)),
                  o_spec, o_spec, w_spec, w_spec, t_spec,
                  pl.BlockSpec((tm, tn), lambda i, j: (i, j + nj))],
        out_specs=[t_spec] * 4,
        out_shape=[jax.ShapeDtypeStruct((s, d), MXU)] * 4,
        compiler_params=_cparams(),
        name=name,
    )(dx, w_o, oa, ob, wa_t, wb_t, gates, gates)


def _ple(p, w_pe_t, h, w_pg, other, *, backward, name):
    s, kp = p.shape
    d = w_pe_t.shape[0]
    tm, tn = _tile(s, 1024), _tile(d, 512)

    def body(p_ref, wpe_ref, h_ref, wpg_ref, other_ref, *out_refs):
        pe = _dot_nt(p_ref[...].astype(MXU), wpe_ref[...])
        gt = _dot(h_ref[...], wpg_ref[...])
        sg = _sigmoid(gt)
        if backward:
            dout = other_ref[...]
            out_refs[0][...] = (dout * sg).astype(out_refs[0].dtype)
            out_refs[1][...] = (dout * pe * sg * (1.0 - sg)).astype(out_refs[1].dtype)
        else:
            out_refs[0][...] = other_ref[...] + pe * sg

    t_spec = pl.BlockSpec((tm, tn), lambda i, j: (i, j))
    if backward:
        out_specs, out_shape = [t_spec, t_spec], [jax.ShapeDtypeStruct((s, d), MXU)] * 2
    else:
        out_specs, out_shape = [t_spec], [jax.ShapeDtypeStruct((s, d), F32)]
    outs = pl.pallas_call(
        body,
        grid=(s // tm, d // tn),
        in_specs=[pl.BlockSpec((tm, kp), lambda i, j: (i, 0)),
                  pl.BlockSpec((tn, kp), lambda i, j: (j, 0)),
                  pl.BlockSpec((tm, d), lambda i, j: (i, 0)),
                  pl.BlockSpec((d, tn), lambda i, j: (0, j)),
                  t_spec],
        out_specs=out_specs,
        out_shape=out_shape,
        compiler_params=_cparams(),
        name=name,
    )(p, w_pe_t, h, w_pg, other)
    return tuple(outs) if backward else outs[0]


def _split_dot(x, tri):
    hi = x.astype(jnp.bfloat16)
    lo = (x - hi.astype(F32)).astype(jnp.bfloat16)
    return _dot(hi, tri) + _dot(lo, tri)


def _log_sigmoids(z):
    t = jnp.log1p(jnp.exp(-jnp.abs(z)))
    return jnp.minimum(z, 0.0) - t, jnp.minimum(-z, 0.0) - t


def _head_lanes(hh):
    lane = lax.broadcasted_iota(jnp.int32, (1, LANES), 1)
    return jnp.logical_and(lane >= hh * HEAD_DIM, lane < (hh + 1) * HEAD_DIM)


def _sb_fwd(qkv, name, comm=None):
    s = qkv.shape[0]
    tq = _tile(s, SB_TQ)

    def body(q_ref, k_ref, v_ref, o_ref):
        i = pl.program_id(1)
        qf = q_ref[...].astype(F32) * SCALE
        row = lax.broadcasted_iota(jnp.int32, (tq, tq), 0)
        col = lax.broadcasted_iota(jnp.int32, (tq, tq), 1)
        causal = col < row
        tri = jnp.where(row > col, 1.0, 0.0).astype(jnp.bfloat16)

        qms = [jnp.where(_head_lanes(hh), qf, 0.0).astype(MXU) for hh in range(2)]

        def block(kb, cs, accs, masked, gate=None):
            rows = pl.ds(pl.multiple_of(kb * tq, tq), tq)
            ks, vs = k_ref[rows, :], v_ref[rows, :]
            new_c, new_acc = [], []
            for hh in range(2):
                lb, lm = _log_sigmoids(_dot_nt(qms[hh], ks))
                if masked:
                    lm = jnp.where(causal, lm, 0.0)
                if gate is not None:
                    lm = lm * gate
                a = jnp.exp(lb + _split_dot(lm, tri) + cs[hh])
                if masked:
                    a = jnp.where(causal, a, 0.0)
                if gate is not None:
                    a = a * gate
                new_acc.append(accs[hh] + _dot(a.astype(MXU), vs))
                new_c.append(cs[hh] + jnp.sum(lm, axis=1, keepdims=True))
            return tuple(new_c), tuple(new_acc)

        def top(cs):
            return jnp.maximum(jnp.max(cs[0]), jnp.max(cs[1]))

        zc, za = jnp.zeros((tq, 1), F32), jnp.zeros((tq, LANES), F32)
        cs, accs = block(i, (zc, zc), (za, za), True)
        cs, accs = block(jnp.maximum(i - 1, 0), cs, accs, False, jnp.where(i > 0, 1.0, 0.0))

        def live(st):
            return jnp.logical_and(st[0] >= 0, st[1] > SB_DEAD)

        def walk(st):
            cs, accs = block(st[0], st[2], st[3], False)
            return st[0] - 1, top(cs), cs, accs

        accs = lax.while_loop(live, walk, (i - 2, top(cs), cs, accs))[3]
        o_ref[...] = jnp.where(_head_lanes(0), accs[0], accs[1]).astype(o_ref.dtype)

    outs, couts = _call(
        body,
        grid=(N_PAIR, s // tq),
        in_specs=[pl.BlockSpec((tq, LANES), lambda p, i: (i, p)),
                  pl.BlockSpec((s, LANES), lambda p, i: (0, N_PAIR + p)),
                  pl.BlockSpec((s, LANES), lambda p, i: (0, 2 * N_PAIR + p))],
        out_specs=[pl.BlockSpec((tq, LANES), lambda p, i: (i, p))],
        out_shape=[jax.ShapeDtypeStruct((s, SB_W), MXU)],
        args=(qkv, qkv, qkv), name=name, comm=comm)
    return outs[0] if comm is None else (outs[0], couts)


def _sb_bwd(qkv, do, name, comm=None):
    s = qkv.shape[0]
    tq = _tile(s, SB_TQ)
    nq = s // tq

    def body(q_ref, k_ref, v_ref, do_ref, dq_ref, dk_ref, dv_ref, dk_acc, dv_acc, carries):
        i = pl.program_id(1)

        @pl.when(i == 0)
        def _():
            dk_acc[...] = jnp.zeros_like(dk_acc)
            dv_acc[...] = jnp.zeros_like(dv_acc)

        qf = q_ref[...].astype(F32) * SCALE
        dof = do_ref[...]
        row = lax.broadcasted_iota(jnp.int32, (tq, tq), 0)
        col = lax.broadcasted_iota(jnp.int32, (tq, tq), 1)
        causal = col < row
        tri_rev = jnp.where(row > col, 1.0, 0.0).astype(jnp.bfloat16)
        tri_excl = jnp.where(row < col, 1.0, 0.0).astype(jnp.bfloat16)

        qms = [jnp.where(_head_lanes(hh), qf, 0.0).astype(MXU) for hh in range(2)]
        doms = [jnp.where(_head_lanes(hh), dof, jnp.zeros_like(dof)) for hh in range(2)]

        def terms(kb, masked):
            rows = pl.ds(pl.multiple_of(kb * tq, tq), tq)
            ks = k_ref[rows, :]
            out = []
            for hh in range(2):
                lb, lm = _log_sigmoids(_dot_nt(qms[hh], ks))
                if masked:
                    lm = jnp.where(causal, lm, 0.0)
                out.append((lb, lm))
            return out

        def row_sums(kb, masked, pre=None):
            pre = terms(kb, masked) if pre is None else pre
            return [jnp.sum(lm, axis=1, keepdims=True) for _, lm in pre]

        def top(cs):
            return jnp.maximum(jnp.max(cs[0]), jnp.max(cs[1]))

        def live(st):
            return jnp.logical_and(st[0] >= 0, st[1] > SB_DEAD)

        def record(st):
            kb, cs = st[0], st[2]
            sums = row_sums(kb, False)
            for hh in range(2):
                carries[hh, kb] = cs[hh]
            cs = tuple(cs[hh] + sums[hh] for hh in range(2))
            return kb - 1, top(cs), cs

        prev = jnp.maximum(i - 1, 0)
        gate = jnp.where(i > 0, 1.0, 0.0)
        t_diag, t_prev = terms(i, True), terms(prev, False)
        c_diag = row_sums(i, True, t_diag)
        sums = row_sums(prev, False, t_prev)
        c_prev = tuple(c_diag[hh] + sums[hh] * gate for hh in range(2))
        first = lax.while_loop(live, record, (i - 2, top(c_prev), c_prev))[0] + 1

        def block(kb, cs, gpres, dqs, masked, gate=None, pre=None):
            rows = pl.ds(pl.multiple_of(kb * tq, tq), tq)
            ks, vs = k_ref[rows, :], v_ref[rows, :]
            pre = terms(kb, masked) if pre is None else pre
            new_g, new_dq = [], []
            dk_add, dv_add = None, None
            for hh in range(2):
                lb, lm = pre[hh]
                a = jnp.exp(lb + _split_dot(lm, tri_rev) + cs[hh])
                if masked:
                    a = jnp.where(causal, a, 0.0)
                if gate is not None:
                    a = a * gate
                g = a * _dot_nt(doms[hh], vs)
                gsum = gpres[hh] + _split_dot(g, tri_excl)
                dz = g - (g + gsum) * jnp.exp(lb)
                if masked:
                    dz = jnp.where(causal, dz, 0.0)
                if gate is not None:
                    dz = dz * gate
                dzb = dz.astype(MXU)
                new_dq.append(dqs[hh] + _dot(dzb, ks))
                dk_h = _dot_tn(dzb, qms[hh])
                dv_h = _dot_tn(a.astype(MXU), doms[hh])
                dk_add = dk_h if dk_add is None else dk_add + dk_h
                dv_add = dv_h if dv_add is None else dv_add + dv_h
                new_g.append(gpres[hh] + jnp.sum(g, axis=1, keepdims=True))
            dk_acc[rows, :] += dk_add
            dv_acc[rows, :] += dv_add
            return tuple(new_g), tuple(new_dq)

        zc, za = jnp.zeros((tq, 1), F32), jnp.zeros((tq, LANES), F32)
        gpres, dqs = lax.fori_loop(
            first, i - 1, lambda kb, cr: block(kb, (carries[0, kb], carries[1, kb]), cr[0], cr[1], False),
            ((zc, zc), (za, za)))
        gpres, dqs = block(prev, c_diag, gpres, dqs, False, gate, t_prev)
        dqs = block(i, (zc, zc), gpres, dqs, True, None, t_diag)[1]
        dq_ref[...] = (jnp.where(_head_lanes(0), dqs[0], dqs[1]) * SCALE).astype(dq_ref.dtype)

        @pl.when(i == nq - 1)
        def _():
            dk_ref[...] = dk_acc[...].astype(dk_ref.dtype)
            dv_ref[...] = dv_acc[...].astype(dv_ref.dtype)

    blk = pl.BlockSpec((tq, LANES), lambda p, i: (i, p))
    full = pl.BlockSpec((s, LANES), lambda p, i: (0, p))
    outs, couts = _call(
        body,
        grid=(N_PAIR, nq),
        in_specs=[blk,
                  pl.BlockSpec((s, LANES), lambda p, i: (0, N_PAIR + p)),
                  pl.BlockSpec((s, LANES), lambda p, i: (0, 2 * N_PAIR + p)),
                  blk],
        out_specs=[blk, full, full],
        out_shape=[jax.ShapeDtypeStruct((s, SB_W), MXU)] * 3,
        scratch_shapes=[pltpu.VMEM((s, LANES), F32), pltpu.VMEM((s, LANES), F32),
                        pltpu.VMEM((2, nq, tq, 1), F32)],
        args=(qkv, qkv, qkv, do), name=name, comm=comm)
    return tuple(outs) if comm is None else (tuple(outs), couts)


def _bucket_table():
    i = np.arange(BLOCK)[:, None]
    j = np.arange(2 * BLOCK)[None, :]
    d = np.maximum(BLOCK + i - j, 0)
    max_exact = N_BUCKETS // 2
    df = np.maximum(d, 1).astype(np.float32)
    large = max_exact + (np.log(df / max_exact) / math.log(MAX_DISTANCE / max_exact)
                         * (N_BUCKETS - max_exact)).astype(np.int32)
    large = np.minimum(large, N_BUCKETS - 1)
    return np.where(d < max_exact, d, large).astype(np.int32)


def _build_bias(rel_bias, buckets, name):
    def body(rb_ref, bk_ref, out_ref):
        h = pl.program_id(0)
        bk = bk_ref[...]
        acc = jnp.zeros(bk.shape, F32)
        for b in range(N_BUCKETS):
            acc = jnp.where(bk == b, rb_ref[b, h], acc)
        out_ref[...] = acc

    return pl.pallas_call(
        body,
        grid=(SW_HEADS,),
        in_specs=[pl.BlockSpec(memory_space=pltpu.SMEM),
                  pl.BlockSpec((BLOCK, 2 * BLOCK), lambda h: (0, 0))],
        out_specs=pl.BlockSpec((None, BLOCK, 2 * BLOCK), lambda h: (h, 0, 0)),
        out_shape=jax.ShapeDtypeStruct((SW_HEADS, BLOCK, 2 * BLOCK), F32),
        name=name,
    )(rel_bias, buckets)


def _bias_grad(dbias_layers, buckets, name):
    n_l = len(dbias_layers)

    def body(*refs):
        bk = refs[n_l][...]
        out_ref = refs[n_l + 1]
        db = refs[0][...]
        for r in refs[1:n_l]:
            db = db + r[...]
        lane = lax.broadcasted_iota(jnp.int32, (1, LANES), 1)
        acc = jnp.zeros((1, LANES), F32)
        for b in range(N_BUCKETS):
            part = jnp.sum(jnp.where(bk == b, db, 0.0), axis=1, keepdims=True)
            tot = jnp.sum(part, axis=0, keepdims=True)
            acc = jnp.where(lane == b, tot, acc)
        out_ref[...] = acc

    hspec = pl.BlockSpec((None, BLOCK, 2 * BLOCK), lambda h: (h, 0, 0))
    return pl.pallas_call(
        body,
        grid=(SW_HEADS,),
        in_specs=[hspec] * n_l + [pl.BlockSpec((BLOCK, 2 * BLOCK), lambda h: (0, 0))],
        out_specs=pl.BlockSpec((None, 1, LANES), lambda h: (h, 0, 0)),
        out_shape=jax.ShapeDtypeStruct((SW_HEADS, 1, LANES), F32),
        name=name,
    )(*dbias_layers, buckets)


GROUP_ROWS = SW_GROUP * BLOCK


def _group_lanes(g):
    lane = lax.broadcasted_iota(jnp.int32, (1, LANES), 1)
    gvec = jnp.zeros((1, LANES), jnp.int32) + g
    return jnp.where(lane >= HEAD_DIM, 1, 0) == gvec, gvec


def _stack_heads(x, g):
    kv_lanes, gvec = _group_lanes(g)
    parts = []
    for j in range(SW_GROUP):
        half = x[:, (j // 2) * LANES:(j // 2 + 1) * LANES]
        moved = jnp.where(gvec == j % 2, half, pltpu.roll(half, HEAD_DIM, 1))
        parts.append(jnp.where(kv_lanes, moved, 0.0))
    return jnp.concatenate(parts, axis=0)


def _unstack_heads(y, g):
    _, gvec = _group_lanes(g)
    heads = []
    for j in range(SW_GROUP):
        yj = y[j * BLOCK:(j + 1) * BLOCK]
        heads.append(jnp.where(gvec == j % 2, yj, pltpu.roll(yj, HEAD_DIM, 1)))
    pairs = [jnp.where(_head_lanes(0), heads[2 * p], heads[2 * p + 1]) for p in range(SW_GROUP // 2)]
    return jnp.concatenate(pairs, axis=1)


def _per_head_col(values):
    return jnp.concatenate([jnp.zeros((BLOCK, 1), F32) + v for v in values], axis=0)


def _swa_scores(qs, kp, kc, bias_ref, n):
    row = jnp.bitwise_and(lax.broadcasted_iota(jnp.int32, (GROUP_ROWS, BLOCK), 0), BLOCK - 1)
    col = lax.broadcasted_iota(jnp.int32, (GROUP_ROWS, BLOCK), 1)
    bias = bias_ref[...].reshape(GROUP_ROWS, 2 * BLOCK)
    s1 = _dot_nt(qs, kp) + bias[:, :BLOCK]
    s2 = _dot_nt(qs, kc) + bias[:, BLOCK:]
    no_prev = jnp.where(n > 0, 0, BLOCK)
    s1 = jnp.where(col > row + no_prev, s1, NEG)
    s2 = jnp.where(col <= row, s2, NEG)
    return s1, s2


def _swa_specs(s):
    q_blk = 3 * SB_W // (2 * LANES)
    k_blk = (3 * SB_W + SW_W) // LANES
    return (pl.BlockSpec((s, 2 * LANES), lambda g: (0, q_blk + g)),
            pl.BlockSpec((s, LANES), lambda g: (0, k_blk)),
            pl.BlockSpec((s, LANES), lambda g: (0, k_blk + 1)))


def _swa_fwd(qkv, bias, sinks, name, comm=None):
    s = qkv.shape[0]
    nb = s // BLOCK

    def body(sink_ref, q_ref, k_ref, v_ref, bias_ref, o_ref, lse_ref):
        g = pl.program_id(0)
        sink = _per_head_col([sink_ref[SW_GROUP * g + j] for j in range(SW_GROUP)])
        lane = lax.broadcasted_iota(jnp.int32, (1, LANES), 1)

        def step(n, carry):
            r0 = pl.multiple_of(n * BLOCK, BLOCK)
            p0 = pl.multiple_of(jnp.maximum(n - 1, 0) * BLOCK, BLOCK)
            cur, prev = pl.ds(r0, BLOCK), pl.ds(p0, BLOCK)
            qs = _stack_heads(q_ref[cur, :].astype(F32) * SCALE, g).astype(MXU)
            s1, s2 = _swa_scores(qs, k_ref[prev, :], k_ref[cur, :], bias_ref, n)
            m = jnp.maximum(jnp.maximum(jnp.max(s1, axis=1, keepdims=True),
                                        jnp.max(s2, axis=1, keepdims=True)), sink)
            e1 = jnp.exp(s1 - m)
            e2 = jnp.exp(s2 - m)
            den = jnp.sum(e1, axis=1, keepdims=True) + jnp.sum(e2, axis=1, keepdims=True) + jnp.exp(sink - m)
            o = _dot((e1 / den).astype(MXU), v_ref[prev, :]) + _dot((e2 / den).astype(MXU), v_ref[cur, :])
            o_ref[cur, :] = _unstack_heads(o, g).astype(o_ref.dtype)
            lse = m + jnp.log(den)
            lse_row = jnp.zeros((BLOCK, LANES), F32)
            for j in range(SW_GROUP):
                lse_row = jnp.where(lane == j, lse[j * BLOCK:(j + 1) * BLOCK], lse_row)
            lse_ref[cur, :] = lse_row
            return carry

        lax.fori_loop(0, nb, step, 0, unroll=2)

    outs, couts = _call(
        body,
        grid=(SW_KV,),
        in_specs=[pl.BlockSpec(memory_space=pltpu.SMEM), *_swa_specs(s),
                  pl.BlockSpec((SW_GROUP, BLOCK, 2 * BLOCK), lambda g: (g, 0, 0))],
        out_specs=[pl.BlockSpec((s, 2 * LANES), lambda g: (0, g)),
                   pl.BlockSpec((None, s, LANES), lambda g: (g, 0, 0))],
        out_shape=[jax.ShapeDtypeStruct((s, SW_W), MXU), jax.ShapeDtypeStruct((SW_KV, s, LANES), F32)],
        args=(sinks, qkv, qkv, qkv, bias), name=name, comm=comm)
    return tuple(outs) if comm is None else (tuple(outs), couts)


def _swa_bwd(qkv, bias, sinks, do, lse, name, comm=None):
    s = qkv.shape[0]
    nb = s // BLOCK

    def body(sink_ref, q_ref, k_ref, v_ref, bias_ref, do_ref, lse_ref,
             dq_ref, dk_ref, dv_ref, dbias_ref, dsink_ref, dk_acc, dv_acc):
        g = pl.program_id(0)
        sink = _per_head_col([sink_ref[SW_GROUP * g + j] for j in range(SW_GROUP)])
        lane = lax.broadcasted_iota(jnp.int32, (1, LANES), 1)

        @pl.when(g == 0)
        def _():
            dk_acc[...] = jnp.zeros_like(dk_acc)
            dv_acc[...] = jnp.zeros_like(dv_acc)

        dbias_ref[...] = jnp.zeros_like(dbias_ref)

        def step(n, dsink_rows):
            r0 = pl.multiple_of(n * BLOCK, BLOCK)
            p0 = pl.multiple_of(jnp.maximum(n - 1, 0) * BLOCK, BLOCK)
            cur, prev = pl.ds(r0, BLOCK), pl.ds(p0, BLOCK)
            qs = _stack_heads(q_ref[cur, :].astype(F32) * SCALE, g).astype(MXU)
            dos = _stack_heads(do_ref[cur, :].astype(F32), g).astype(MXU)
            kp, kc, vp, vc = k_ref[prev, :], k_ref[cur, :], v_ref[prev, :], v_ref[cur, :]
            lse_row = lse_ref[cur, :]
            lse = jnp.concatenate([jnp.sum(jnp.where(lane == j, lse_row, 0.0), axis=1, keepdims=True)
                                   for j in range(SW_GROUP)], axis=0)
            s1, s2 = _swa_scores(qs, kp, kc, bias_ref, n)
            pr1 = jnp.exp(s1 - lse)
            pr2 = jnp.exp(s2 - lse)
            dpr1 = _dot_nt(dos, vp)
            dpr2 = _dot_nt(dos, vc)
            delta = jnp.sum(pr1 * dpr1, axis=1, keepdims=True) + jnp.sum(pr2 * dpr2, axis=1, keepdims=True)
            ds1 = pr1 * (dpr1 - delta)
            ds2 = pr2 * (dpr2 - delta)
            dbias_ref[:, :, :BLOCK] += ds1.reshape(SW_GROUP, BLOCK, BLOCK)
            dbias_ref[:, :, BLOCK:] += ds2.reshape(SW_GROUP, BLOCK, BLOCK)
            ds1b, ds2b = ds1.astype(MXU), ds2.astype(MXU)
            dq = _dot(ds1b, kp) + _dot(ds2b, kc)
            dq_ref[cur, :] = (_unstack_heads(dq, g) * SCALE).astype(dq_ref.dtype)
            dk_acc[prev, :] += _dot_tn(ds1b, qs)
            dk_acc[cur, :] += _dot_tn(ds2b, qs)
            dv_acc[prev, :] += _dot_tn(pr1.astype(MXU), dos)
            dv_acc[cur, :] += _dot_tn(pr2.astype(MXU), dos)
            return dsink_rows - jnp.exp(sink - lse) * delta

        rows = lax.fori_loop(0, nb, step, jnp.zeros((GROUP_ROWS, 1), F32), unroll=2)
        for j in range(SW_GROUP):
            dsink_ref[j] = jnp.broadcast_to(jnp.sum(rows[j * BLOCK:(j + 1) * BLOCK], axis=0, keepdims=True),
                                            (1, LANES))

        @pl.when(g == SW_KV - 1)
        def _():
            dk_ref[...] = dk_acc[...].astype(dk_ref.dtype)
            dv_ref[...] = dv_acc[...].astype(dv_ref.dtype)

    grp = pl.BlockSpec((s, 2 * LANES), lambda g: (0, g))
    kv_out = pl.BlockSpec((s, LANES), lambda g: (0, 0))
    bspec = pl.BlockSpec((SW_GROUP, BLOCK, 2 * BLOCK), lambda g: (g, 0, 0))
    outs, couts = _call(
        body,
        grid=(SW_KV,),
        in_specs=[pl.BlockSpec(memory_space=pltpu.SMEM), *_swa_specs(s), bspec, grp,
                  pl.BlockSpec((None, s, LANES), lambda g: (g, 0, 0))],
        out_specs=[grp, kv_out, kv_out, bspec, pl.BlockSpec((SW_GROUP, 1, LANES), lambda g: (g, 0, 0))],
        out_shape=[jax.ShapeDtypeStruct((s, SW_W), MXU),
                   jax.ShapeDtypeStruct((s, LANES), MXU),
                   jax.ShapeDtypeStruct((s, LANES), MXU),
                   jax.ShapeDtypeStruct((SW_HEADS, BLOCK, 2 * BLOCK), F32),
                   jax.ShapeDtypeStruct((SW_HEADS, 1, LANES), F32)],
        scratch_shapes=[pltpu.VMEM((s, LANES), F32), pltpu.VMEM((s, LANES), F32)],
        args=(sinks, qkv, qkv, qkv, bias, do, lse), name=name, comm=comm)
    return tuple(outs) if comm is None else (tuple(outs), couts)


class _NoPlan:
    def comm(self, name):
        return None

    def done(self, name, outs):
        pass

    def grad(self, layer, name, value):
        pass

    def layer_done(self, layer):
        pass


def _run(plan, fn, *args, name, **kw):
    comm = plan.comm(name)
    if comm is None:
        return fn(*args, name=name, **kw)
    res, outs = fn(*args, name=name, comm=comm, **kw)
    plan.done(name, outs)
    return res


def _layer_fwd(x, p, w, g_mix, g_mlp, g_pe, sinks, bias, tag, plan):
    h1, r1 = _run(plan, _rms_fwd, x, g_mix, name=f"rms_mix_{tag}")
    qkv = _run(plan, _mm, h1, w["w_qkv_t"], tb=True, out_dtypes=(MXU,), name=f"proj_qkv_{tag}")
    gates = _run(plan, _mm, h1, w["w_gate_t"], tb=True, out_dtypes=(MXU,), name=f"proj_gate_{tag}")
    oa = _run(plan, _sb_fwd, qkv, name=f"sb_fwd_{tag}")
    ob, lse = _run(plan, _swa_fwd, qkv, bias, sinks, name=f"swa_fwd_{tag}")
    merged = _mix_fwd(oa, ob, w["w_up_a_t"], w["w_up_b_t"], gates, f"mix_fwd_{tag}")
    x1 = _run(plan, _mm, merged, w["w_o"], extras=(x,), epi=lambda acc, res: res + acc, name=f"out_proj_{tag}")
    h2, r2 = _rms_fwd(x1, g_mlp, f"rms_mlp_{tag}")
    u, act = _run(plan, _mm, h2, w["w_ff1_t"], tb=True,
                  epi=lambda acc: (acc, jnp.square(jnp.maximum(acc, 0.0))),
                  out_dtypes=(MXU, MXU), name=f"ff1_{tag}")
    x2 = _run(plan, _mm, act, w["w_ff2"], extras=(x1,), epi=lambda acc, res: res + acc, name=f"ff2_{tag}")
    h3, r3 = _rms_fwd(x2, g_pe, f"rms_pe_{tag}")
    x3 = _ple(p, w["w_pe_t"], h3, w["w_pg"], x2, backward=False, name=f"ple_fwd_{tag}")
    saved = dict(x=x, h1=h1, r1=r1, gates=gates, qkv=qkv, lse=lse, oa=oa, ob=ob, merged=merged,
                 x1=x1, h2=h2, r2=r2, u=u, act=act, x2=x2, h3=h3, r3=r3)
    return x3, saved


def _layer_bwd(dx3, sv, p, w, g_mix, g_mlp, g_pe, sinks, bias, layer, plan):
    tag = f"l{layer}"
    gw = {}
    wire = (WIRE,)

    def dw(name, a, b):
        gw[name] = _run(plan, _mm, a, b, ta=True, out_dtypes=wire, name=f"d{name}_{tag}")
        plan.grad(layer, name, gw[name])

    dpe, dgt = _ple(p, w["w_pe_t"], sv["h3"], w["w_pg"], dx3, backward=True, name=f"ple_bwd_{tag}")
    dw("w_pe", dpe, p)
    dw("w_pg", sv["h3"], dgt)
    dh3 = _run(plan, _mm, dgt, w["w_pg"], tb=True, name=f"dh_pe_{tag}")
    dx2, dx2b, dg_pe = _rms_bwd(sv["x2"], sv["r3"], g_pe, dh3, dx3, f"rms_pe_bwd_{tag}")
    dw("w_ff2", sv["act"], dx2b)
    du = _run(plan, _mm, dx2b, w["w_ff2"], tb=True, extras=(sv["u"],),
              epi=lambda acc, u: acc * (2.0 * jnp.maximum(u.astype(F32), 0.0)), out_dtypes=(MXU,),
              name=f"dact_{tag}")
    dw("w_ff1", du, sv["h2"])
    dh2 = _run(plan, _mm, du, w["w_ff1_t"], name=f"dh_mlp_{tag}")
    dx1, dx1b, dg_mlp = _rms_bwd(sv["x1"], sv["r2"], g_mlp, dh2, dx2, f"rms_mlp_bwd_{tag}")
    dw("w_o", sv["merged"], dx1b)
    dya, dyb, dga, dgb = _mix_bwd(dx1b, w["w_o"], sv["oa"], sv["ob"], w["w_up_a_t"], w["w_up_b_t"],
                                  sv["gates"], f"mix_bwd_{tag}")
    dw("w_up_a", dya, sv["oa"])
    dw("w_up_b", dyb, sv["ob"])
    doa = _run(plan, _mm, dya, w["w_up_a_t"], out_dtypes=(MXU,), name=f"do_a_{tag}")
    dob = _run(plan, _mm, dyb, w["w_up_b_t"], out_dtypes=(MXU,), name=f"do_b_{tag}")
    dqb, dkb, dvb, dbias, dsink = _run(plan, _swa_bwd, sv["qkv"], bias, sinks, dob, sv["lse"],
                                       name=f"swa_bwd_{tag}")
    dqa, dka, dva = _run(plan, _sb_bwd, sv["qkv"], doa, name=f"sb_bwd_{tag}")
    dqkv = jnp.concatenate([dqa, dka, dva, dqb, dkb, dvb], axis=1)
    gw_qkv = _mm(dqkv, sv["h1"], ta=True, out_dtypes=wire, name=f"dw_qkv_{tag}")
    gw_ga = _mm(dga, sv["h1"], ta=True, out_dtypes=wire, name=f"dw_ga_{tag}")
    gw_gb = _mm(dgb, sv["h1"], ta=True, out_dtypes=wire, name=f"dw_gb_{tag}")
    gw["w_in"] = jnp.concatenate([gw_qkv, gw_ga, gw_gb], axis=0)
    plan.grad(layer, "w_in", gw["w_in"])
    d = dga.shape[1]
    add = lambda acc, res: res + acc
    dh1 = _mm(dqkv, w["w_qkv_t"], tk=768, name=f"dh_qkv_{tag}")
    dh1 = _mm(dga, w["w_gate_t"][:d], extras=(dh1,), epi=add, name=f"dh_ga_{tag}")
    dh1 = _mm(dgb, w["w_gate_t"][d:], extras=(dh1,), epi=add, name=f"dh_gb_{tag}")
    dx, _, dg_mix = _rms_bwd(sv["x"], sv["r1"], g_mix, dh1, dx1, f"rms_mix_bwd_{tag}")
    small = dict(g_mix=dg_mix, g_mlp=dg_mlp, g_pe=dg_pe, sinks=dsink[:, 0, 0], dbias=dbias)
    return dx, gw, small


def _local_step(x, p, target, weights, g_mix, g_mlp, g_pe, g_final, sinks, rel_bias, plan=None):
    plan = _NoPlan() if plan is None else plan
    depth = g_mix.shape[0]
    buckets = jnp.asarray(_bucket_table())
    bias = _build_bias(rel_bias, buckets, "build_bias")
    saved, wfull = [], []
    h = x
    for l in range(depth):
        wfull.append(weights(l))
        h, sv = _layer_fwd(h, p[l], wfull[l], g_mix[l:l + 1], g_mlp[l:l + 1], g_pe[l:l + 1],
                           sinks[l], bias, f"l{l}", plan)
        saved.append(sv)
    loss_row, dx, dg_final = _loss_head(h, g_final[None, :], target, "loss_head")
    gws = [None] * depth
    smalls = [None] * depth
    for l in reversed(range(depth)):
        dx, gws[l], smalls[l] = _layer_bwd(dx, saved[l], p[l], wfull[l], g_mix[l:l + 1], g_mlp[l:l + 1],
                                           g_pe[l:l + 1], sinks[l], bias, l, plan)
        plan.layer_done(l)
    drel = _bias_grad([sm["dbias"] for sm in smalls], buckets, "bias_grad")[:, 0, :N_BUCKETS].T
    small = dict(
        g_mix=jnp.concatenate([sm["g_mix"] for sm in smalls], axis=0),
        g_mlp=jnp.concatenate([sm["g_mlp"] for sm in smalls], axis=0),
        g_pe=jnp.concatenate([sm["g_pe"] for sm in smalls], axis=0),
        g_final=dg_final[0],
        sinks=jnp.stack([sm["sinks"] for sm in smalls], axis=0),
        rel_bias=drel,
    )
    return loss_row, dx, gws, small


MESH_ID = pl.DeviceIdType.MESH
ANY = pl.BlockSpec(memory_space=pl.ANY)


def _position():
    return lax.axis_index("x"), lax.axis_index("y"), lax.axis_index("c")


def _run_comm(comm, name):
    ci, co = len(comm.inputs), len(comm.out_shapes)

    def body(*refs):
        cin, cout, csem = refs[:ci], refs[ci:ci + co], refs[ci + co:]
        pos = _position()
        comm.start(pos, cin, cout, csem)
        comm.finish(pos, cin, cout, csem)

    return pl.pallas_call(body, out_shape=comm.out_shapes, in_specs=[ANY] * ci, out_specs=[ANY] * co,
                          scratch_shapes=comm.sems, name=name)(*comm.inputs)


def _gather_comm(shards):
    n = len(shards)

    def copies(pos, x_refs, out_refs, sems):
        send_sems, recv_sems, local_sems = sems
        x, y, c = pos
        me, sibling = (x, y, c), (x, y, 1 - c)
        chips = [(1 - x, y), (x, 1 - y), (1 - x, 1 - y)]

        def slot(a, px, py, pc):
            return out_refs[a].at[4 * px + 2 * py + pc]

        def copy(a, k, block, to, src=None):
            return pltpu.make_async_remote_copy(
                src_ref=slot(a, *block) if src is None else src, dst_ref=slot(a, *block),
                send_sem=send_sems.at[a, k], recv_sem=recv_sems.at[a, k],
                device_id=to, device_id_type=MESH_ID)

        mine = [pltpu.make_async_copy(x_refs[a], slot(a, *me), local_sems.at[a]) for a in range(n)]
        first = []
        for a in range(n):
            first.append(copy(a, 0, me, sibling, src=x_refs[a]))
            first += [copy(a, 1 + j, me, (*chip, c), src=x_refs[a]) for j, chip in enumerate(chips)]
        return me, sibling, chips, copy, mine, first

    def start(pos, x_refs, out_refs, sems):
        _, _, _, _, mine, first = copies(pos, x_refs, out_refs, sems)
        for cp in mine + first:
            cp.start()

    def finish(pos, x_refs, out_refs, sems):
        me, sibling, chips, copy, mine, first = copies(pos, x_refs, out_refs, sems)
        c = pos[2]
        passed = []
        for j, chip in enumerate(chips):
            for a in range(n):
                copy(a, 1 + j, (*chip, c), me).wait_recv()
                fwd = copy(a, 4 + j, (*chip, c), sibling)
                fwd.start()
                passed.append(fwd)
        for a in range(n):
            copy(a, 0, sibling, me).wait_recv()
            for j, chip in enumerate(chips):
                copy(a, 4 + j, (*chip, 1 - c), me).wait_recv()
        for cp in first + passed:
            cp.wait_send()
        for cp in mine:
            cp.wait()

    return _Comm(shards, [jax.ShapeDtypeStruct((N_DEV,) + s.shape, s.dtype) for s in shards],
                 [pltpu.SemaphoreType.DMA((n, 7)), pltpu.SemaphoreType.DMA((n, 7)),
                  pltpu.SemaphoreType.DMA((n,))], start, finish)


def _exchange_comm(arrays, n_slots, route):
    n = len(arrays)

    def copies(pos, in_refs, out_refs, sems):
        send_sems, recv_sems = sems
        out = []
        for a in range(n):
            for j in range(n_slots):
                src_slot, peer = route(pos, j)
                out.append(pltpu.make_async_remote_copy(
                    src_ref=in_refs[a].at[src_slot], dst_ref=out_refs[a].at[j],
                    send_sem=send_sems.at[a, j], recv_sem=recv_sems.at[a, j],
                    device_id=peer, device_id_type=MESH_ID))
        return out

    def start(pos, in_refs, out_refs, sems):
        for cp in copies(pos, in_refs, out_refs, sems):
            cp.start()

    def finish(pos, in_refs, out_refs, sems):
        for cp in copies(pos, in_refs, out_refs, sems):
            cp.wait()

    return _Comm(arrays, [jax.ShapeDtypeStruct((n_slots,) + g.shape[1:], g.dtype) for g in arrays],
                 [pltpu.SemaphoreType.DMA((n, n_slots)), pltpu.SemaphoreType.DMA((n, n_slots))], start, finish)


def _rs_sibling_comm(gs):
    return _exchange_comm(gs, 4, lambda pos, j: (2 * j + (1 - pos[2]), (pos[0], pos[1], 1 - pos[2])))


def _chip_of(k, x, y):
    return x ^ ((k + 1) & 1), y ^ (((k + 1) >> 1) & 1)


def _chip_partials(pos, gs, recvs, name):
    n = len(gs)

    def body(pos_ref, *refs):
        for a in range(n):
            refs[2 * n + a][...] = (refs[a][...].astype(F32) + refs[n + a][...].astype(F32)
                                    ).astype(refs[2 * n + a].dtype)

    def g_map(k, pos_ref):
        cx, cy = _chip_of(k, pos_ref[0], pos_ref[1])
        return (4 * cx + 2 * cy + pos_ref[2], 0, 0)

    def r_map(k, pos_ref):
        cx, cy = _chip_of(k, pos_ref[0], pos_ref[1])
        return (2 * cx + cy, 0, 0)

    slab = [(None,) + g.shape[1:] for g in gs]
    return pl.pallas_call(
        body,
        grid_spec=pltpu.PrefetchScalarGridSpec(
            num_scalar_prefetch=1,
            grid=(4,),
            in_specs=[pl.BlockSpec(sh, g_map) for sh in slab] + [pl.BlockSpec(sh, r_map) for sh in slab],
            out_specs=[pl.BlockSpec(sh, lambda k, pos_ref: (k, 0, 0)) for sh in slab],
        ),
        out_shape=[jax.ShapeDtypeStruct((4,) + g.shape[1:], g.dtype) for g in gs],
        compiler_params=_cparams(),
        name=name,
    )(pos, *gs, *recvs)


def _rs_chips_comm(parts):
    return _exchange_comm(parts, 3, lambda pos, k: (k, (*_chip_of(k, pos[0], pos[1]), pos[2])))


def _adamw_math(w, g, m, v):
    m = ADAM_B1 * m + (1.0 - ADAM_B1) * g
    v = ADAM_B2 * v + (1.0 - ADAM_B2) * (g * g)
    m_hat = m / (1.0 - ADAM_B1 ** ADAM_STEP)
    v_hat = v / (1.0 - ADAM_B2 ** ADAM_STEP)
    delta = -ADAM_LR * (m_hat / (jnp.sqrt(v_hat) + ADAM_EPS) + ADAM_WD * w)
    return delta, m, v


def _adamw_weight(parts, recvs, w, m, v, name, grad_t=False):
    depth, a, b = w.shape
    ta = _tile(a, 288, unit=LANES if grad_t else 16)
    ni = a // ta
    g_block = (b, ta) if grad_t else (ta, b)

    def body(*refs):
        p_refs, r_refs = refs[:depth], refs[depth:2 * depth]
        w_ref, m_ref, v_ref = refs[2 * depth:2 * depth + 3]
        g_out, d_out, m_out, v_out = refs[2 * depth + 3:]
        layer = pl.program_id(0)
        g = jnp.zeros(g_block, F32)
        for l in range(depth):
            gl = p_refs[l][...].astype(F32)
            for k in range(3):
                gl = gl + r_refs[l][k].astype(F32)
            g = jnp.where(layer == l, gl, g)
        if grad_t:
            g = g.T
        delta, m_new, v_new = _adamw_math(w_ref[...], g, m_ref[...], v_ref[...])
        g_out[...] = g
        d_out[...] = delta
        m_out[...] = m_new
        v_out[...] = v_new

    def hold(l):
        return lambda layer, i: jnp.where(layer == l, i, jnp.where(layer < l, 0, ni - 1))

    def g_index(slot, f):
        if grad_t:
            return lambda layer, i: (slot, 0, f(layer, i))
        return lambda layer, i: (slot, f(layer, i), 0)

    p_specs = [pl.BlockSpec((None,) + g_block, g_index(3, hold(l))) for l in range(depth)]
    r_specs = [pl.BlockSpec((3,) + g_block, g_index(0, hold(l))) for l in range(depth)]
    row = pl.BlockSpec((None, ta, b), lambda layer, i: (layer, i, 0))
    return pl.pallas_call(
        body,
        grid=(depth, ni),
        in_specs=p_specs + r_specs + [row, row, row],
        out_specs=[row] * 4,
        out_shape=[jax.ShapeDtypeStruct(w.shape, F32)] * 4,
        compiler_params=_cparams(),
        name=name,
    )(*parts, *recvs, w, m, v)


def _adamw_replicated(gathered, w, m, v, name):
    r, lanes = w.shape

    def body(g_ref, w_ref, m_ref, v_ref, g_out, d_out, m_out, v_out):
        g = g_ref[0]
        for k in range(1, N_DEV):
            g = g + g_ref[k]
        delta, m_new, v_new = _adamw_math(w_ref[...], g, m_ref[...], v_ref[...])
        g_out[...] = g
        d_out[...] = delta
        m_out[...] = m_new
        v_out[...] = v_new

    return pl.pallas_call(
        body,
        out_shape=[jax.ShapeDtypeStruct((r, lanes), F32)] * 4,
        name=name,
    )(gathered, w, m, v)


def _wire_shard(name, shard):
    return (shard.T if name in COL_SHARDED else shard).astype(WIRE)


def _full_weight(gathered):
    return gathered.reshape(N_DEV * gathered.shape[1], gathered.shape[2])


def _to_slabs(gfull):
    return gfull.reshape(N_DEV, gfull.shape[0] // N_DEV, gfull.shape[1])


def _pack_small(arrs):
    rows = []
    for a in arrs:
        flat = a.astype(F32).reshape(-1)
        pad = (-flat.shape[0]) % LANES
        rows.append(jnp.pad(flat, (0, pad)).reshape(-1, LANES))
    packed = jnp.concatenate(rows, axis=0)
    return jnp.pad(packed, ((0, (-packed.shape[0]) % 8), (0, 0)))


def _unpack_small(packed, shapes):
    out, off = [], 0
    for shp in shapes:
        n = math.prod(shp)
        rows = -(-n // LANES)
        out.append(packed[off:off + rows].reshape(-1)[:n].reshape(shp))
        off += rows
    return out


def _of(layer, *names):
    return tuple((layer, n) for n in names)


MIXER_W = ("w_in", "w_up_a", "w_up_b", "w_o")
MLP_W = ("w_ff1", "w_ff2", "w_pe", "w_pg")

GATHERS = (
    ("rms_mix_l0", _of(0, "w_in")),
    ("proj_qkv_l0", _of(0, "w_up_a", "w_up_b", "w_o")),
    ("proj_gate_l0", _of(0, "w_pe", "w_pg")),
    ("sb_fwd_l0", _of(0, "w_ff1", "w_ff2")),
    ("swa_fwd_l0", _of(1, "w_in")),
    ("sb_fwd_l1", _of(1, "w_up_a", "w_up_b", "w_o", "w_pe", "w_pg", "w_ff1")),
    ("swa_fwd_l1", _of(1, "w_ff2")),
)
REDUCES = (
    (_of(1, "w_ff1"), "dw_ff2_l0", "dact_l0"),
    (_of(1, "w_ff2"), "dw_ff2_l0", "dw_ff1_l0"),
    (_of(1, "w_in"), "dw_ff2_l0", "sb_bwd_l0"),
    (_of(1, "w_up_a", "w_up_b", "w_o", "w_pe", "w_pg"), "dw_ff2_l0", "swa_bwd_l0"),
    (_of(0, *MLP_W), "dh_mlp_l0", "sb_bwd_l0"),
    (_of(0, "w_o", "w_up_a", "w_up_b"), "do_a_l0", "sb_bwd_l0"),
    (_of(0, "w_in"), None, None),
)


def _merge_comms(comms):
    if len(comms) == 1:
        return comms[0]

    def cuts(counts):
        edges = [0]
        for c in counts:
            edges.append(edges[-1] + c)
        return [slice(a, b) for a, b in zip(edges[:-1], edges[1:])]

    s_in = cuts([len(c.inputs) for c in comms])
    s_out = cuts([len(c.out_shapes) for c in comms])
    s_sem = cuts([len(c.sems) for c in comms])

    def start(pos, cin, cout, csem):
        for c, i, o, s in zip(comms, s_in, s_out, s_sem):
            c.start(pos, cin[i], cout[o], csem[s])

    def finish(pos, cin, cout, csem):
        for c, i, o, s in zip(comms, s_in, s_out, s_sem):
            c.finish(pos, cin[i], cout[o], csem[s])

    return _Comm(sum([c.inputs for c in comms], []), sum([c.out_shapes for c in comms], []),
                 sum([c.sems for c in comms], []), start, finish)


class _LayerWeights:
    def __init__(self, full, layer):
        self.full, self.layer, self.cache = full, layer, {}

    def __getitem__(self, name):
        if name not in self.cache:
            if name == "w_qkv_t":
                self.cache[name] = self.full[(self.layer, "w_in")][:QKV_COLS]
            elif name == "w_gate_t":
                self.cache[name] = self.full[(self.layer, "w_in")][QKV_COLS:]
            else:
                base = name[:-2] if name.endswith("_t") else name
                assert (base in COL_SHARDED) == name.endswith("_t"), name
                self.cache[name] = self.full[(self.layer, base)]
        return self.cache[name]


class _Plan:
    def __init__(self, w_sh, pos):
        self.w_sh = dict(zip(WEIGHTS, w_sh))
        self.pos = pos
        self.full, self.gw, self.parts, self.recv = {}, {}, {}, {}
        self.slabs = {}
        self.hosted = {}
        for i, (host, _) in enumerate(GATHERS):
            if host is not None:
                self.hosted.setdefault(host, []).append(("gather", i))
        for i, (_, sib_host, chip_host) in enumerate(REDUCES):
            assert (sib_host is None) == (chip_host is None)
            if sib_host is not None:
                self.hosted.setdefault(sib_host, []).append(("sibling", i))
                self.hosted.setdefault(chip_host, []).append(("chips", i))

    def _gather(self, i):
        return _gather_comm([_wire_shard(n, self.w_sh[n][layer]) for layer, n in GATHERS[i][1]])

    def _gathered(self, i, outs):
        for (layer, n), g in zip(GATHERS[i][1], outs):
            self.full[(layer, n)] = _full_weight(g)

    def weights(self, layer):
        for i, (host, items) in enumerate(GATHERS):
            if host is None and items[0][0] == layer:
                self._gathered(i, _run_comm(self._gather(i), f"gather_{i}"))
        return _LayerWeights(self.full, layer)

    def grad(self, layer, name, value):
        self.gw[(layer, name)] = value

    def _sibling(self, i):
        self.slabs[i] = [_to_slabs(self.gw[item]) for item in REDUCES[i][0]]
        return _rs_sibling_comm(self.slabs[i])

    def _sibling_done(self, i, outs):
        parts = _chip_partials(self.pos, self.slabs[i], outs, f"chip_partials_{i}")
        for item, part in zip(REDUCES[i][0], parts):
            self.parts[item] = part

    def _chips(self, i):
        return _rs_chips_comm([self.parts[item] for item in REDUCES[i][0]])

    def _chips_done(self, i, outs):
        for item, r in zip(REDUCES[i][0], outs):
            self.recv[item] = r

    def layer_done(self, layer):
        for i, (items, sib_host, _) in enumerate(REDUCES):
            if sib_host is None and items[0][0] == layer:
                self._sibling_done(i, _run_comm(self._sibling(i), f"reduce_sibling_{i}"))
                self._chips_done(i, _run_comm(self._chips(i), f"reduce_chips_{i}"))

    def comm(self, name):
        if name not in self.hosted:
            return None
        make = {"gather": self._gather, "sibling": self._sibling, "chips": self._chips}
        return _merge_comms([make[kind](i) for kind, i in self.hosted[name]])

    def done(self, name, outs):
        took = {"gather": self._gathered, "sibling": self._sibling_done, "chips": self._chips_done}
        off = 0
        for kind, i in self.hosted[name]:
            n = len(GATHERS[i][1]) if kind == "gather" else len(REDUCES[i][0])
            took[kind](i, outs[off:off + n])
            off += n


def kernel(x, p, w_in, w_up_a, w_up_b, w_o, w_ff1, w_ff2, w_pe, w_pg, g_mix, g_mlp, g_pe, g_final, sinks, rel_bias, loss_target, m_w_in, m_w_up_a, m_w_up_b, m_w_o, m_w_ff1, m_w_ff2, m_w_pe, m_w_pg, m_g_mix, m_g_mlp, m_g_pe, m_g_final, m_sinks, m_rel_bias, v_w_in, v_w_up_a, v_w_up_b, v_w_o, v_w_ff1, v_w_ff2, v_w_pe, v_w_pg, v_g_mix, v_g_mlp, v_g_pe, v_g_final, v_sinks, v_rel_bias):
    w_sh = [w_in, w_up_a, w_up_b, w_o, w_ff1, w_ff2, w_pe, w_pg]
    m_sh = [m_w_in, m_w_up_a, m_w_up_b, m_w_o, m_w_ff1, m_w_ff2, m_w_pe, m_w_pg]
    v_sh = [v_w_in, v_w_up_a, v_w_up_b, v_w_o, v_w_ff1, v_w_ff2, v_w_pe, v_w_pg]
    depth = w_in.shape[0]
    assert depth == 2 and x.shape[-1] * 2 + QKV_COLS == w_in.shape[2] * N_DEV

    px, py, pc = _position()
    plan = _Plan(w_sh, jnp.stack([px, py, pc]).astype(jnp.int32))
    loss_row, grad_x, _, small = _local_step(
        x[0], p[:, 0], loss_target[0], plan.weights, g_mix, g_mlp, g_pe, g_final, sinks, rel_bias, plan=plan)

    grad_w, delta_w, new_m, new_v = [], [], [], []
    for a, name in enumerate(WEIGHTS):
        parts = [plan.parts[(l, name)] for l in range(depth)]
        recvs = [plan.recv[(l, name)] for l in range(depth)]
        if name == "w_in":
            flip = lambda t: t.transpose(0, 2, 1)
            outs = [flip(o) for o in _adamw_weight(parts, recvs, flip(w_sh[a]), flip(m_sh[a]), flip(v_sh[a]),
                                                   f"adamw_{name}")]
        else:
            outs = _adamw_weight(parts, recvs, w_sh[a], m_sh[a], v_sh[a], f"adamw_{name}",
                                 grad_t=name in COL_SHARDED)
        for lst, o in zip((grad_w, delta_w, new_m, new_v), outs):
            lst.append(o)

    small_w = [g_mix, g_mlp, g_pe, g_final, sinks, rel_bias]
    small_m = [m_g_mix, m_g_mlp, m_g_pe, m_g_final, m_sinks, m_rel_bias]
    small_v = [v_g_mix, v_g_mlp, v_g_pe, v_g_final, v_sinks, v_rel_bias]
    small_shapes = [a.shape for a in small_w] + [(1,)]
    zero = jnp.zeros((1,), F32)
    small_g = _pack_small([small[n] for n in SMALL] + [loss_row[0, :1]])
    small_all = _run_comm(_gather_comm([small_g]), "gather_small")[0]
    packed_s = _adamw_replicated(small_all, _pack_small(small_w + [zero]), _pack_small(small_m + [zero]),
                                 _pack_small(small_v + [zero + 1.0]), "adamw_replicated")
    sg, sd, sm, sv = [_unpack_small(t, small_shapes) for t in packed_s]
    loss = sg[-1][0]

    return (loss, grad_x[None], *grad_w, *sg[:-1], *delta_w, *sd[:-1], *new_m, *sm[:-1], *new_v, *sv[:-1])
```

```python
import functools
import math

import numpy as np
import jax
import jax.numpy as jnp
from jax import lax
from jax.experimental import pallas as pl
from jax.experimental.pallas import tpu as pltpu

F32 = jnp.float32
MXU = jnp.bfloat16
WIRE = jnp.bfloat16

HEAD_DIM = 64
SB_HEADS = 8
SW_HEADS = 8
SW_KV = 2
SW_GROUP = SW_HEADS // SW_KV
BLOCK = 128
N_BUCKETS = 32
MAX_DISTANCE = 128
EPS = 1e-6
SCALE = HEAD_DIM ** -0.5
SB_W = SB_HEADS * HEAD_DIM
SW_W = SW_HEADS * HEAD_DIM
QKV_COLS = 3 * SB_W + SW_W + 2 * SW_KV * HEAD_DIM
N_DEV = 8
LANES = 128
N_PAIR = SB_HEADS // 2
NEG = -1e30

ADAM_LR = 0.001
ADAM_B1 = 0.9
ADAM_B2 = 0.999
ADAM_EPS = 1e-08
ADAM_WD = 0.01
ADAM_STEP = 10

VMEM_LIMIT = 48 * 1024 * 1024
SB_TQ = 256
SB_DEAD = -105.0

WEIGHTS = ("w_in", "w_up_a", "w_up_b", "w_o", "w_ff1", "w_ff2", "w_pe", "w_pg")
COL_SHARDED = ("w_in", "w_up_a", "w_up_b", "w_ff1", "w_pe")
SMALL = ("g_mix", "g_mlp", "g_pe", "g_final", "sinks", "rel_bias")


def _cparams(**kw):
    return pltpu.CompilerParams(vmem_limit_bytes=VMEM_LIMIT, **kw)


def _dot(a, b):
    return jnp.dot(a, b, preferred_element_type=F32)


def _dot_nt(a, b):
    return lax.dot_general(a, b, (((1,), (1,)), ((), ())), preferred_element_type=F32)


def _dot_tn(a, b):
    return lax.dot_general(a, b, (((0,), (0,)), ((), ())), preferred_element_type=F32)


def _tile(n, target, unit=LANES):
    if n <= target:
        return n
    t = (target // unit) * unit
    while t > unit and n % t:
        t -= unit
    assert n % t == 0, (n, target)
    return t


def _sigmoid(x):
    return 1.0 / (1.0 + jnp.exp(-x))


class _Comm:
    def __init__(self, inputs, out_shapes, sems, start, finish):
        self.inputs, self.out_shapes, self.sems = list(inputs), list(out_shapes), list(sems)
        self.start, self.finish = start, finish


def _call(body, *, grid, in_specs, out_specs, out_shape, scratch_shapes=(), args, name, comm=None):
    n_in, n_out, n_scr = len(in_specs), len(out_shape), len(scratch_shapes)
    if comm is None:
        outs = pl.pallas_call(body, grid=grid, in_specs=list(in_specs), out_specs=list(out_specs),
                              out_shape=list(out_shape), scratch_shapes=list(scratch_shapes),
                              compiler_params=_cparams(), name=name)(*args)
        return list(outs), None
    ci, co = len(comm.inputs), len(comm.out_shapes)
    any_spec = pl.BlockSpec(memory_space=pl.ANY)

    def wrapped(*refs):
        ins, cin = refs[:n_in], refs[n_in:n_in + ci]
        o0 = n_in + ci
        outs, cout = refs[o0:o0 + n_out], refs[o0 + n_out:o0 + n_out + co]
        s0 = o0 + n_out + co
        scr, csem = refs[s0:s0 + n_scr], refs[s0 + n_scr:]
        ids = [pl.program_id(d) for d in range(len(grid))]
        first = functools.reduce(jnp.logical_and, [i == 0 for i in ids])
        last = functools.reduce(jnp.logical_and, [i == g - 1 for i, g in zip(ids, grid)])
        pos = (lax.axis_index("x"), lax.axis_index("y"), lax.axis_index("c"))

        @pl.when(first)
        def _():
            comm.start(pos, cin, cout, csem)

        body(*ins, *outs, *scr)

        @pl.when(last)
        def _():
            comm.finish(pos, cin, cout, csem)

    outs = pl.pallas_call(wrapped, grid=grid, in_specs=list(in_specs) + [any_spec] * ci,
                          out_specs=list(out_specs) + [any_spec] * co,
                          out_shape=list(out_shape) + comm.out_shapes,
                          scratch_shapes=list(scratch_shapes) + comm.sems,
                          compiler_params=_cparams(), name=name)(*args, *comm.inputs)
    return list(outs[:n_out]), list(outs[n_out:])


def _mm(a, b, *, ta=False, tb=False, extras=(), epi=None, out_dtypes=(F32,),
        tm=1024, tn=1024, tk=1024, name, comm=None):
    if ta:
        kdim, m = a.shape
    else:
        m, kdim = a.shape
    n = b.shape[0] if tb else b.shape[1]
    assert (b.shape[1] if tb else b.shape[0]) == kdim
    tm, tn, tk = _tile(m, tm), _tile(n, tn), _tile(kdim, tk)
    nk = kdim // tk
    n_ex, n_out = len(extras), len(out_dtypes)

    a_spec = (pl.BlockSpec((tk, tm), lambda i, j, k: (k, i)) if ta
              else pl.BlockSpec((tm, tk), lambda i, j, k: (i, k)))
    b_spec = (pl.BlockSpec((tn, tk), lambda i, j, k: (j, k)) if tb
              else pl.BlockSpec((tk, tn), lambda i, j, k: (k, j)))
    ex_specs = []
    for e in extras:
        assert e.shape == (m, n), (e.shape, m, n)
        ex_specs.append(pl.BlockSpec((tm, tn), lambda i, j, k: (i, j)))
    out_spec = pl.BlockSpec((tm, tn), lambda i, j, k: (i, j))

    def body(a_ref, b_ref, *rest):
        ex_refs = rest[:n_ex]
        out_refs = rest[n_ex:n_ex + n_out]
        acc = rest[-1]
        k = pl.program_id(2)

        @pl.when(k == 0)
        def _():
            acc[...] = jnp.zeros_like(acc)

        av = a_ref[...].astype(MXU)
        bv = b_ref[...].astype(MXU)
        if ta:
            acc[...] += _dot_tn(av, bv)
        elif tb:
            acc[...] += _dot_nt(av, bv)
        else:
            acc[...] += _dot(av, bv)

        @pl.when(k == nk - 1)
        def _():
            res = acc[...]
            if epi is not None:
                res = epi(res, *[e[...] for e in ex_refs])
            if not isinstance(res, tuple):
                res = (res,)
            for o_ref, r in zip(out_refs, res):
                o_ref[...] = r.astype(o_ref.dtype)

    outs, couts = _call(
        body,
        grid=(m // tm, n // tn, nk),
        in_specs=[a_spec, b_spec] + ex_specs,
        out_specs=[out_spec] * n_out,
        out_shape=[jax.ShapeDtypeStruct((m, n), dt) for dt in out_dtypes],
        scratch_shapes=[pltpu.VMEM((tm, tn), F32)],
        args=(a, b, *extras), name=name, comm=comm)
    res = outs[0] if n_out == 1 else tuple(outs)
    return res if comm is None else (res, couts)


def _rms_fwd(x, g, name, comm=None):
    s, d = x.shape
    tr = _tile(s, 256)

    def body(x_ref, g_ref, h_ref, r_ref):
        xf = x_ref[...]
        r = lax.rsqrt(jnp.mean(xf * xf, axis=-1, keepdims=True) + EPS)
        h_ref[...] = ((xf * r) * g_ref[...]).astype(h_ref.dtype)
        r_ref[...] = r

    outs, couts = _call(
        body,
        grid=(s // tr,),
        in_specs=[pl.BlockSpec((tr, d), lambda i: (i, 0)), pl.BlockSpec((1, d), lambda i: (0, 0))],
        out_specs=[pl.BlockSpec((tr, d), lambda i: (i, 0)), pl.BlockSpec((tr, 1), lambda i: (i, 0))],
        out_shape=[jax.ShapeDtypeStruct((s, d), MXU), jax.ShapeDtypeStruct((s, 1), F32)],
        args=(x, g), name=name, comm=comm)
    return tuple(outs) if comm is None else (tuple(outs), couts)


def _rms_bwd(x, r, g, dh, dres, name):
    s, d = x.shape
    tr = _tile(s, 256)

    def body(x_ref, r_ref, g_ref, dh_ref, dres_ref, dx_ref, dxb_ref, dg_ref):
        @pl.when(pl.program_id(0) == 0)
        def _():
            dg_ref[...] = jnp.zeros_like(dg_ref)

        rr = r_ref[...]
        xhat = x_ref[...] * rr
        dh_v = dh_ref[...]
        dxhat = dh_v * g_ref[...]
        mean = jnp.mean(dxhat * xhat, axis=-1, keepdims=True)
        dx = dres_ref[...] + rr * (dxhat - xhat * mean)
        dx_ref[...] = dx
        dxb_ref[...] = dx.astype(dxb_ref.dtype)
        dg_ref[...] += jnp.sum(dh_v * xhat, axis=0, keepdims=True)

    row = pl.BlockSpec((tr, d), lambda i: (i, 0))
    vec = pl.BlockSpec((1, d), lambda i: (0, 0))
    return pl.pallas_call(
        body,
        grid=(s // tr,),
        in_specs=[row, pl.BlockSpec((tr, 1), lambda i: (i, 0)), vec, row, row],
        out_specs=[row, row, vec],
        out_shape=[jax.ShapeDtypeStruct((s, d), F32), jax.ShapeDtypeStruct((s, d), MXU),
                   jax.ShapeDtypeStruct((1, d), F32)],
        compiler_params=_cparams(),
        name=name,
    )(x, r, g, dh, dres)


def _loss_head(x, g, target, name):
    s, d = x.shape
    tr = _tile(s, 256)

    def body(x_ref, g_ref, t_ref, loss_ref, dx_ref, dg_ref):
        @pl.when(pl.program_id(0) == 0)
        def _():
            dg_ref[...] = jnp.zeros_like(dg_ref)
            loss_ref[...] = jnp.zeros_like(loss_ref)

        xf = x_ref[...]
        gv = g_ref[...]
        r = lax.rsqrt(jnp.mean(xf * xf, axis=-1, keepdims=True) + EPS)
        xhat = xf * r
        err = xhat * gv - t_ref[...]
        loss_ref[...] += 0.5 * jnp.sum(jnp.mean(err * err, axis=-1, keepdims=True), axis=0, keepdims=True)
        dy = err * (1.0 / d)
        dxhat = dy * gv
        mean = jnp.mean(dxhat * xhat, axis=-1, keepdims=True)
        dx_ref[...] = r * (dxhat - xhat * mean)
        dg_ref[...] += jnp.sum(dy * xhat, axis=0, keepdims=True)

    row = pl.BlockSpec((tr, d), lambda i: (i, 0))
    vec = pl.BlockSpec((1, d), lambda i: (0, 0))
    return pl.pallas_call(
        body,
        grid=(s // tr,),
        in_specs=[row, vec, row],
        out_specs=[pl.BlockSpec((1, LANES), lambda i: (0, 0)), row, vec],
        out_shape=[jax.ShapeDtypeStruct((1, LANES), F32), jax.ShapeDtypeStruct((s, d), F32),
                   jax.ShapeDtypeStruct((1, d), F32)],
        compiler_params=_cparams(),
        name=name,
    )(x, g, target)


def _mix_fwd(oa, ob, wa_t, wb_t, gates, name):
    s, kd = oa.shape
    d = wa_t.shape[0]
    tm, tn = _tile(s, 1024), _tile(d, 512)
    nj = d // tn

    def body(oa_ref, ob_ref, wa_ref, wb_ref, ga_ref, gb_ref, out_ref):
        ya = _dot_nt(oa_ref[...], wa_ref[...])
        yb = _dot_nt(ob_ref[...], wb_ref[...])
        out_ref[...] = (_sigmoid(ga_ref[...].astype(F32)) * ya
                        + _sigmoid(gb_ref[...].astype(F32)) * yb).astype(out_ref.dtype)

    o_spec = pl.BlockSpec((tm, kd), lambda i, j: (i, 0))
    w_spec = pl.BlockSpec((tn, kd), lambda i, j: (j, 0))
    return pl.pallas_call(
        body,
        grid=(s // tm, nj),
        in_specs=[o_spec, o_spec, w_spec, w_spec,
                  pl.BlockSpec((tm, tn), lambda i, j: (i, j)),
                  pl.BlockSpec((tm, tn), lambda i, j: (i, j + nj))],
        out_specs=pl.BlockSpec((tm, tn), lambda i, j: (i, j)),
        out_shape=jax.ShapeDtypeStruct((s, d), MXU),
        compiler_params=_cparams(),
        name=name,
    )(oa, ob, wa_t, wb_t, gates, gates)


def _mix_bwd(dx, w_o, oa, ob, wa_t, wb_t, gates, name):
    s, kd = oa.shape
    d = wa_t.shape[0]
    tm, tn = _tile(s, 1024), _tile(d, 512)
    nj = d // tn

    def body(dx_ref, wo_ref, oa_ref, ob_ref, wa_ref, wb_ref, ga_ref, gb_ref,
             dya_ref, dyb_ref, dga_ref, dgb_ref):
        dm = _dot_nt(dx_ref[...], wo_ref[...])
        ya = _dot_nt(oa_ref[...], wa_ref[...])
        yb = _dot_nt(ob_ref[...], wb_ref[...])
        sa = _sigmoid(ga_ref[...].astype(F32))
        sb = _sigmoid(gb_ref[...].astype(F32))
        dya_ref[...] = (dm * sa).astype(dya_ref.dtype)
        dyb_ref[...] = (dm * sb).astype(dyb_ref.dtype)
        dga_ref[...] = (dm * ya * sa * (1.0 - sa)).astype(dga_ref.dtype)
        dgb_ref[...] = (dm * yb * sb * (1.0 - sb)).astype(dgb_ref.dtype)

    o_spec = pl.BlockSpec((tm, kd), lambda i, j: (i, 0))
    w_spec = pl.BlockSpec((tn, kd), lambda i, j: (j, 0))
    t_spec = pl.BlockSpec((tm, tn), lambda i, j: (i, j))
    return pl.pallas_call(
        body,
        grid=(s // tm, nj),
        in_specs=[pl.BlockSpec((tm, d), lambda i, j: (i, 0)),
                  pl.BlockSpec((tn, d), lambda i, j: (j, 0)),
                  o_spec, o_spec, w_spec, w_spec, t_spec,
                  pl.BlockSpec((tm, tn), lambda i, j: (i, j + nj))],
        out_specs=[t_spec] * 4,
        out_shape=[jax.ShapeDtypeStruct((s, d), MXU)] * 4,
        compiler_params=_cparams(),
        name=name,
    )(dx, w_o, oa, ob, wa_t, wb_t, gates, gates)


def _ple(p, w_pe_t, h, w_pg, other, *, backward, name):
    s, kp = p.shape
    d = w_pe_t.shape[0]
    tm, tn = _tile(s, 1024), _tile(d, 512)

    def body(p_ref, wpe_ref, h_ref, wpg_ref, other_ref, *out_refs):
        pe = _dot_nt(p_ref[...].astype(MXU), wpe_ref[...])
        gt = _dot(h_ref[...], wpg_ref[...])
        sg = _sigmoid(gt)
        if backward:
            dout = other_ref[...]
            out_refs[0][...] = (dout * sg).astype(out_refs[0].dtype)
            out_refs[1][...] = (dout * pe * sg * (1.0 - sg)).astype(out_refs[1].dtype)
        else:
            out_refs[0][...] = other_ref[...] + pe * sg

    t_spec = pl.BlockSpec((tm, tn), lambda i, j: (i, j))
    if backward:
        out_specs, out_shape = [t_spec, t_spec], [jax.ShapeDtypeStruct((s, d), MXU)] * 2
    else:
        out_specs, out_shape = [t_spec], [jax.ShapeDtypeStruct((s, d), F32)]
    outs = pl.pallas_call(
        body,
        grid=(s // tm, d // tn),
        in_specs=[pl.BlockSpec((tm, kp), lambda i, j: (i, 0)),
                  pl.BlockSpec((tn, kp), lambda i, j: (j, 0)),
                  pl.BlockSpec((tm, d), lambda i, j: (i, 0)),
                  pl.BlockSpec((d, tn), lambda i, j: (0, j)),
                  t_spec],
        out_specs=out_specs,
        out_shape=out_shape,
        compiler_params=_cparams(),
        name=name,
    )(p, w_pe_t, h, w_pg, other)
    return tuple(outs) if backward else outs[0]


def _split_dot(x, tri):
    hi = x.astype(jnp.bfloat16)
    lo = (x - hi.astype(F32)).astype(jnp.bfloat16)
    return _dot(hi, tri) + _dot(lo, tri)


def _log_sigmoids(z):
    t = jnp.log1p(jnp.exp(-jnp.abs(z)))
    return jnp.minimum(z, 0.0) - t, jnp.minimum(-z, 0.0) - t


def _head_lanes(hh):
    lane = lax.broadcasted_iota(jnp.int32, (1, LANES), 1)
    return jnp.logical_and(lane >= hh * HEAD_DIM, lane < (hh + 1) * HEAD_DIM)


def _sb_fwd(qkv, name, comm=None):
    s = qkv.shape[0]
    tq = _tile(s, SB_TQ)

    def body(q_ref, k_ref, v_ref, o_ref):
        i = pl.program_id(1)
        qf = q_ref[...].astype(F32) * SCALE
        row = lax.broadcasted_iota(jnp.int32, (tq, tq), 0)
        col = lax.broadcasted_iota(jnp.int32, (tq, tq), 1)
        causal = col < row
        tri = jnp.where(row > col, 1.0, 0.0).astype(jnp.bfloat16)

        qms = [jnp.where(_head_lanes(hh), qf, 0.0).astype(MXU) for hh in range(2)]

        def block(kb, cs, accs, masked, gate=None):
            rows = pl.ds(pl.multiple_of(kb * tq, tq), tq)
            ks, vs = k_ref[rows, :], v_ref[rows, :]
            new_c, new_acc = [], []
            for hh in range(2):
                lb, lm = _log_sigmoids(_dot_nt(qms[hh], ks))
                if masked:
                    lm = jnp.where(causal, lm, 0.0)
                if gate is not None:
                    lm = lm * gate
                a = jnp.exp(lb + _split_dot(lm, tri) + cs[hh])
                if masked:
                    a = jnp.where(causal, a, 0.0)
                if gate is not None:
                    a = a * gate
                new_acc.append(accs[hh] + _dot(a.astype(MXU), vs))
                new_c.append(cs[hh] + jnp.sum(lm, axis=1, keepdims=True))
            return tuple(new_c), tuple(new_acc)

        def top(cs):
            return jnp.maximum(jnp.max(cs[0]), jnp.max(cs[1]))

        zc, za = jnp.zeros((tq, 1), F32), jnp.zeros((tq, LANES), F32)
        cs, accs = block(i, (zc, zc), (za, za), True)
        cs, accs = block(jnp.maximum(i - 1, 0), cs, accs, False, jnp.where(i > 0, 1.0, 0.0))

        def live(st):
            return jnp.logical_and(st[0] >= 0, st[1] > SB_DEAD)

        def walk(st):
            cs, accs = block(st[0], st[2], st[3], False)
            return st[0] - 1, top(cs), cs, accs

        accs = lax.while_loop(live, walk, (i - 2, top(cs), cs, accs))[3]
        o_ref[...] = jnp.where(_head_lanes(0), accs[0], accs[1]).astype(o_ref.dtype)

    outs, couts = _call(
        body,
        grid=(N_PAIR, s // tq),
        in_specs=[pl.BlockSpec((tq, LANES), lambda p, i: (i, p)),
                  pl.BlockSpec((s, LANES), lambda p, i: (0, N_PAIR + p)),
                  pl.BlockSpec((s, LANES), lambda p, i: (0, 2 * N_PAIR + p))],
        out_specs=[pl.BlockSpec((tq, LANES), lambda p, i: (i, p))],
        out_shape=[jax.ShapeDtypeStruct((s, SB_W), MXU)],
        args=(qkv, qkv, qkv), name=name, comm=comm)
    return outs[0] if comm is None else (outs[0], couts)


def _sb_bwd(qkv, do, name, comm=None):
    s = qkv.shape[0]
    tq = _tile(s, SB_TQ)
    nq = s // tq

    def body(q_ref, k_ref, v_ref, do_ref, dq_ref, dk_ref, dv_ref, dk_acc, dv_acc, carries):
        i = pl.program_id(1)

        @pl.when(i == 0)
        def _():
            dk_acc[...] = jnp.zeros_like(dk_acc)
            dv_acc[...] = jnp.zeros_like(dv_acc)

        qf = q_ref[...].astype(F32) * SCALE
        dof = do_ref[...]
        row = lax.broadcasted_iota(jnp.int32, (tq, tq), 0)
        col = lax.broadcasted_iota(jnp.int32, (tq, tq), 1)
        causal = col < row
        tri_rev = jnp.where(row > col, 1.0, 0.0).astype(jnp.bfloat16)
        tri_excl = jnp.where(row < col, 1.0, 0.0).astype(jnp.bfloat16)

        qms = [jnp.where(_head_lanes(hh), qf, 0.0).astype(MXU) for hh in range(2)]
        doms = [jnp.where(_head_lanes(hh), dof, jnp.zeros_like(dof)) for hh in range(2)]

        def terms(kb, masked):
            rows = pl.ds(pl.multiple_of(kb * tq, tq), tq)
            ks = k_ref[rows, :]
            out = []
            for hh in range(2):
                lb, lm = _log_sigmoids(_dot_nt(qms[hh], ks))
                if masked:
                    lm = jnp.where(causal, lm, 0.0)
                out.append((lb, lm))
            return out

        def row_sums(kb, masked, pre=None):
            pre = terms(kb, masked) if pre is None else pre
            return [jnp.sum(lm, axis=1, keepdims=True) for _, lm in pre]

        def top(cs):
            return jnp.maximum(jnp.max(cs[0]), jnp.max(cs[1]))

        def live(st):
            return jnp.logical_and(st[0] >= 0, st[1] > SB_DEAD)

        def record(st):
            kb, cs = st[0], st[2]
            sums = row_sums(kb, False)
            for hh in range(2):
                carries[hh, kb] = cs[hh]
            cs = tuple(cs[hh] + sums[hh] for hh in range(2))
            return kb - 1, top(cs), cs

        prev = jnp.maximum(i - 1, 0)
        gate = jnp.where(i > 0, 1.0, 0.0)
        t_diag, t_prev = terms(i, True), terms(prev, False)
        c_diag = row_sums(i, True, t_diag)
        sums = row_sums(prev, False, t_prev)
        c_prev = tuple(c_diag[hh] + sums[hh] * gate for hh in range(2))
        first = lax.while_loop(live, record, (i - 2, top(c_prev), c_prev))[0] + 1

        def block(kb, cs, gpres, dqs, masked, gate=None, pre=None):
            rows = pl.ds(pl.multiple_of(kb * tq, tq), tq)
            ks, vs = k_ref[rows, :], v_ref[rows, :]
            pre = terms(kb, masked) if pre is None else pre
            new_g, new_dq = [], []
            dk_add, dv_add = None, None
            for hh in range(2):
                lb, lm = pre[hh]
                a = jnp.exp(lb + _split_dot(lm, tri_rev) + cs[hh])
                if masked:
                    a = jnp.where(causal, a, 0.0)
                if gate is not None:
                    a = a * gate
                g = a * _dot_nt(doms[hh], vs)
                gsum = gpres[hh] + _split_dot(g, tri_excl)
                dz = g - (g + gsum) * jnp.exp(lb)
                if masked:
                    dz = jnp.where(causal, dz, 0.0)
                if gate is not None:
                    dz = dz * gate
                dzb = dz.astype(MXU)
                new_dq.append(dqs[hh] + _dot(dzb, ks))
                dk_h = _dot_tn(dzb, qms[hh])
                dv_h = _dot_tn(a.astype(MXU), doms[hh])
                dk_add = dk_h if dk_add is None else dk_add + dk_h
                dv_add = dv_h if dv_add is None else dv_add + dv_h
                new_g.append(gpres[hh] + jnp.sum(g, axis=1, keepdims=True))
            dk_acc[rows, :] += dk_add
            dv_acc[rows, :] += dv_add
            return tuple(new_g), tuple(new_dq)

        zc, za = jnp.zeros((tq, 1), F32), jnp.zeros((tq, LANES), F32)
        gpres, dqs = lax.fori_loop(
            first, i - 1, lambda kb, cr: block(kb, (carries[0, kb], carries[1, kb]), cr[0], cr[1], False),
            ((zc, zc), (za, za)))
        gpres, dqs = block(prev, c_diag, gpres, dqs, False, gate, t_prev)
        dqs = block(i, (zc, zc), gpres, dqs, True, None, t_diag)[1]
        dq_ref[...] = (jnp.where(_head_lanes(0), dqs[0], dqs[1]) * SCALE).astype(dq_ref.dtype)

        @pl.when(i == nq - 1)
        def _():
            dk_ref[...] = dk_acc[...].astype(dk_ref.dtype)
            dv_ref[...] = dv_acc[...].astype(dv_ref.dtype)

    blk = pl.BlockSpec((tq, LANES), lambda p, i: (i, p))
    full = pl.BlockSpec((s, LANES), lambda p, i: (0, p))
    outs, couts = _call(
        body,
        grid=(N_PAIR, nq),
        in_specs=[blk,
                  pl.BlockSpec((s, LANES), lambda p, i: (0, N_PAIR + p)),
                  pl.BlockSpec((s, LANES), lambda p, i: (0, 2 * N_PAIR + p)),
                  blk],
        out_specs=[blk, full, full],
        out_shape=[jax.ShapeDtypeStruct((s, SB_W), MXU)] * 3,
        scratch_shapes=[pltpu.VMEM((s, LANES), F32), pltpu.VMEM((s, LANES), F32),
                        pltpu.VMEM((2, nq, tq, 1), F32)],
        args=(qkv, qkv, qkv, do), name=name, comm=comm)
    return tuple(outs) if comm is None else (tuple(outs), couts)


def _bucket_table():
    i = np.arange(BLOCK)[:, None]
    j = np.arange(2 * BLOCK)[None, :]
    d = np.maximum(BLOCK + i - j, 0)
    max_exact = N_BUCKETS // 2
    df = np.maximum(d, 1).astype(np.float32)
    large = max_exact + (np.log(df / max_exact) / math.log(MAX_DISTANCE / max_exact)
                         * (N_BUCKETS - max_exact)).astype(np.int32)
    large = np.minimum(large, N_BUCKETS - 1)
    return np.where(d < max_exact, d, large).astype(np.int32)


def _build_bias(rel_bias, buckets, name):
    def body(rb_ref, bk_ref, out_ref):
        h = pl.program_id(0)
        bk = bk_ref[...]
        acc = jnp.zeros(bk.shape, F32)
        for b in range(N_BUCKETS):
            acc = jnp.where(bk == b, rb_ref[b, h], acc)
        out_ref[...] = acc

    return pl.pallas_call(
        body,
        grid=(SW_HEADS,),
        in_specs=[pl.BlockSpec(memory_space=pltpu.SMEM),
                  pl.BlockSpec((BLOCK, 2 * BLOCK), lambda h: (0, 0))],
        out_specs=pl.BlockSpec((None, BLOCK, 2 * BLOCK), lambda h: (h, 0, 0)),
        out_shape=jax.ShapeDtypeStruct((SW_HEADS, BLOCK, 2 * BLOCK), F32),
        name=name,
    )(rel_bias, buckets)


def _bias_grad(dbias_layers, buckets, name):
    n_l = len(dbias_layers)

    def body(*refs):
        bk = refs[n_l][...]
        out_ref = refs[n_l + 1]
        db = refs[0][...]
        for r in refs[1:n_l]:
            db = db + r[...]
        lane = lax.broadcasted_iota(jnp.int32, (1, LANES), 1)
        acc = jnp.zeros((1, LANES), F32)
        for b in range(N_BUCKETS):
            part = jnp.sum(jnp.where(bk == b, db, 0.0), axis=1, keepdims=True)
            tot = jnp.sum(part, axis=0, keepdims=True)
            acc = jnp.where(lane == b, tot, acc)
        out_ref[...] = acc

    hspec = pl.BlockSpec((None, BLOCK, 2 * BLOCK), lambda h: (h, 0, 0))
    return pl.pallas_call(
        body,
        grid=(SW_HEADS,),
        in_specs=[hspec] * n_l + [pl.BlockSpec((BLOCK, 2 * BLOCK), lambda h: (0, 0))],
        out_specs=pl.BlockSpec((None, 1, LANES), lambda h: (h, 0, 0)),
        out_shape=jax.ShapeDtypeStruct((SW_HEADS, 1, LANES), F32),
        name=name,
    )(*dbias_layers, buckets)


GROUP_ROWS = SW_GROUP * BLOCK


def _group_lanes(g):
    lane = lax.broadcasted_iota(jnp.int32, (1, LANES), 1)
    gvec = jnp.zeros((1, LANES), jnp.int32) + g
    return jnp.where(lane >= HEAD_DIM, 1, 0) == gvec, gvec


def _stack_heads(x, g):
    kv_lanes, gvec = _group_lanes(g)
    parts = []
    for j in range(SW_GROUP):
        half = x[:, (j // 2) * LANES:(j // 2 + 1) * LANES]
        moved = jnp.where(gvec == j % 2, half, pltpu.roll(half, HEAD_DIM, 1))
        parts.append(jnp.where(kv_lanes, moved, 0.0))
    return jnp.concatenate(parts, axis=0)


def _unstack_heads(y, g):
    _, gvec = _group_lanes(g)
    heads = []
    for j in range(SW_GROUP):
        yj = y[j * BLOCK:(j + 1) * BLOCK]
        heads.append(jnp.where(gvec == j % 2, yj, pltpu.roll(yj, HEAD_DIM, 1)))
    pairs = [jnp.where(_head_lanes(0), heads[2 * p], heads[2 * p + 1]) for p in range(SW_GROUP // 2)]
    return jnp.concatenate(pairs, axis=1)


def _per_head_col(values):
    return jnp.concatenate([jnp.zeros((BLOCK, 1), F32) + v for v in values], axis=0)


def _swa_scores(qs, kp, kc, bias_ref, n):
    row = jnp.bitwise_and(lax.broadcasted_iota(jnp.int32, (GROUP_ROWS, BLOCK), 0), BLOCK - 1)
    col = lax.broadcasted_iota(jnp.int32, (GROUP_ROWS, BLOCK), 1)
    bias = bias_ref[...].reshape(GROUP_ROWS, 2 * BLOCK)
    s1 = _dot_nt(qs, kp) + bias[:, :BLOCK]
    s2 = _dot_nt(qs, kc) + bias[:, BLOCK:]
    no_prev = jnp.where(n > 0, 0, BLOCK)
    s1 = jnp.where(col > row + no_prev, s1, NEG)
    s2 = jnp.where(col <= row, s2, NEG)
    return s1, s2


def _swa_specs(s):
    q_blk = 3 * SB_W // (2 * LANES)
    k_blk = (3 * SB_W + SW_W) // LANES
    return (pl.BlockSpec((s, 2 * LANES), lambda g: (0, q_blk + g)),
            pl.BlockSpec((s, LANES), lambda g: (0, k_blk)),
            pl.BlockSpec((s, LANES), lambda g: (0, k_blk + 1)))


def _swa_fwd(qkv, bias, sinks, name, comm=None):
    s = qkv.shape[0]
    nb = s // BLOCK

    def body(sink_ref, q_ref, k_ref, v_ref, bias_ref, o_ref, lse_ref):
        g = pl.program_id(0)
        sink = _per_head_col([sink_ref[SW_GROUP * g + j] for j in range(SW_GROUP)])
        lane = lax.broadcasted_iota(jnp.int32, (1, LANES), 1)

        def step(n, carry):
            r0 = pl.multiple_of(n * BLOCK, BLOCK)
            p0 = pl.multiple_of(jnp.maximum(n - 1, 0) * BLOCK, BLOCK)
            cur, prev = pl.ds(r0, BLOCK), pl.ds(p0, BLOCK)
            qs = _stack_heads(q_ref[cur, :].astype(F32) * SCALE, g).astype(MXU)
            s1, s2 = _swa_scores(qs, k_ref[prev, :], k_ref[cur, :], bias_ref, n)
            m = jnp.maximum(jnp.max(jnp.maximum(s1, s2), axis=1, keepdims=True), sink)
            e1 = jnp.exp(s1 - m)
            e2 = jnp.exp(s2 - m)
            den = jnp.sum(e1 + e2, axis=1, keepdims=True) + jnp.exp(sink - m)
            o = _dot((e1 / den).astype(MXU), v_ref[prev, :]) + _dot((e2 / den).astype(MXU), v_ref[cur, :])
            o_ref[cur, :] = _unstack_heads(o, g).astype(o_ref.dtype)
            lse = m + jnp.log(den)
            lse_row = jnp.zeros((BLOCK, LANES), F32)
            for j in range(SW_GROUP):
                lse_row = jnp.where(lane == j, lse[j * BLOCK:(j + 1) * BLOCK], lse_row)
            lse_ref[cur, :] = lse_row
            return carry

        lax.fori_loop(0, nb, step, 0, unroll=2)

    outs, couts = _call(
        body,
        grid=(SW_KV,),
        in_specs=[pl.BlockSpec(memory_space=pltpu.SMEM), *_swa_specs(s),
                  pl.BlockSpec((SW_GROUP, BLOCK, 2 * BLOCK), lambda g: (g, 0, 0))],
        out_specs=[pl.BlockSpec((s, 2 * LANES), lambda g: (0, g)),
                   pl.BlockSpec((None, s, LANES), lambda g: (g, 0, 0))],
        out_shape=[jax.ShapeDtypeStruct((s, SW_W), MXU), jax.ShapeDtypeStruct((SW_KV, s, LANES), F32)],
        args=(sinks, qkv, qkv, qkv, bias), name=name, comm=comm)
    return tuple(outs) if comm is None else (tuple(outs), couts)


def _swa_bwd(qkv, bias, sinks, do, lse, name, comm=None):
    s = qkv.shape[0]
    nb = s // BLOCK

    def body(sink_ref, q_ref, k_ref, v_ref, bias_ref, do_ref, lse_ref,
             dq_ref, dk_ref, dv_ref, dbias_ref, dsink_ref, dk_acc, dv_acc):
        g = pl.program_id(0)
        sink = _per_head_col([sink_ref[SW_GROUP * g + j] for j in range(SW_GROUP)])
        lane = lax.broadcasted_iota(jnp.int32, (1, LANES), 1)

        @pl.when(g == 0)
        def _():
            dk_acc[...] = jnp.zeros_like(dk_acc)
            dv_acc[...] = jnp.zeros_like(dv_acc)

        dbias_ref[...] = jnp.zeros_like(dbias_ref)

        def step(n, dsink_rows):
            r0 = pl.multiple_of(n * BLOCK, BLOCK)
            p0 = pl.multiple_of(jnp.maximum(n - 1, 0) * BLOCK, BLOCK)
            cur, prev = pl.ds(r0, BLOCK), pl.ds(p0, BLOCK)
            qs = _stack_heads(q_ref[cur, :].astype(F32) * SCALE, g).astype(MXU)
            dos = _stack_heads(do_ref[cur, :].astype(F32), g).astype(MXU)
            kp, kc, vp, vc = k_ref[prev, :], k_ref[cur, :], v_ref[prev, :], v_ref[cur, :]
            lse_row = lse_ref[cur, :]
            lse = jnp.concatenate([jnp.sum(jnp.where(lane == j, lse_row, 0.0), axis=1, keepdims=True)
                                   for j in range(SW_GROUP)], axis=0)
            s1, s2 = _swa_scores(qs, kp, kc, bias_ref, n)
            pr1 = jnp.exp(s1 - lse)
            pr2 = jnp.exp(s2 - lse)
            dpr1 = _dot_nt(dos, vp)
            dpr2 = _dot_nt(dos, vc)
            delta = jnp.sum(pr1 * dpr1 + pr2 * dpr2, axis=1, keepdims=True)
            ds1 = pr1 * (dpr1 - delta)
            ds2 = pr2 * (dpr2 - delta)
            dbias_ref[:, :, :BLOCK] += ds1.reshape(SW_GROUP, BLOCK, BLOCK)
            dbias_ref[:, :, BLOCK:] += ds2.reshape(SW_GROUP, BLOCK, BLOCK)
            ds1b, ds2b = ds1.astype(MXU), ds2.astype(MXU)
            dq = _dot(ds1b, kp) + _dot(ds2b, kc)
            dq_ref[cur, :] = (_unstack_heads(dq, g) * SCALE).astype(dq_ref.dtype)
            dk_acc[prev, :] += _dot_tn(ds1b, qs)
            dk_acc[cur, :] += _dot_tn(ds2b, qs)
            dv_acc[prev, :] += _dot_tn(pr1.astype(MXU), dos)
            dv_acc[cur, :] += _dot_tn(pr2.astype(MXU), dos)
            return dsink_rows - jnp.exp(sink - lse) * delta

        rows = lax.fori_loop(0, nb, step, jnp.zeros((GROUP_ROWS, 1), F32), unroll=2)
        for j in range(SW_GROUP):
            dsink_ref[j] = jnp.broadcast_to(jnp.sum(rows[j * BLOCK:(j + 1) * BLOCK], axis=0, keepdims=True),
                                            (1, LANES))

        @pl.when(g == SW_KV - 1)
        def _():
            dk_ref[...] = dk_acc[...].astype(dk_ref.dtype)
            dv_ref[...] = dv_acc[...].astype(dv_ref.dtype)

    grp = pl.BlockSpec((s, 2 * LANES), lambda g: (0, g))
    kv_out = pl.BlockSpec((s, LANES), lambda g: (0, 0))
    bspec = pl.BlockSpec((SW_GROUP, BLOCK, 2 * BLOCK), lambda g: (g, 0, 0))
    outs, couts = _call(
        body,
        grid=(SW_KV,),
        in_specs=[pl.BlockSpec(memory_space=pltpu.SMEM), *_swa_specs(s), bspec, grp,
                  pl.BlockSpec((None, s, LANES), lambda g: (g, 0, 0))],
        out_specs=[grp, kv_out, kv_out, bspec, pl.BlockSpec((SW_GROUP, 1, LANES), lambda g: (g, 0, 0))],
        out_shape=[jax.ShapeDtypeStruct((s, SW_W), MXU),
                   jax.ShapeDtypeStruct((s, LANES), MXU),
                   jax.ShapeDtypeStruct((s, LANES), MXU),
                   jax.ShapeDtypeStruct((SW_HEADS, BLOCK, 2 * BLOCK), F32),
                   jax.ShapeDtypeStruct((SW_HEADS, 1, LANES), F32)],
        scratch_shapes=[pltpu.VMEM((s, LANES), F32), pltpu.VMEM((s, LANES), F32)],
        args=(sinks, qkv, qkv, qkv, bias, do, lse), name=name, comm=comm)
    return tuple(outs) if comm is None else (tuple(outs), couts)


class _NoPlan:
    def comm(self, name):
        return None

    def done(self, name, outs):
        pass

    def grad(self, layer, name, value):
        pass

    def layer_done(self, layer):
        pass


def _run(plan, fn, *args, name, **kw):
    comm = plan.comm(name)
    if comm is None:
        return fn(*args, name=name, **kw)
    res, outs = fn(*args, name=name, comm=comm, **kw)
    plan.done(name, outs)
    return res


def _layer_fwd(x, p, w, g_mix, g_mlp, g_pe, sinks, bias, tag, plan):
    h1, r1 = _run(plan, _rms_fwd, x, g_mix, name=f"rms_mix_{tag}")
    qkv = _run(plan, _mm, h1, w["w_qkv_t"], tb=True, out_dtypes=(MXU,), name=f"proj_qkv_{tag}")
    gates = _run(plan, _mm, h1, w["w_gate_t"], tb=True, out_dtypes=(MXU,), name=f"proj_gate_{tag}")
    oa = _run(plan, _sb_fwd, qkv, name=f"sb_fwd_{tag}")
    ob, lse = _run(plan, _swa_fwd, qkv, bias, sinks, name=f"swa_fwd_{tag}")
    merged = _mix_fwd(oa, ob, w["w_up_a_t"], w["w_up_b_t"], gates, f"mix_fwd_{tag}")
    x1 = _run(plan, _mm, merged, w["w_o"], extras=(x,), epi=lambda acc, res: res + acc, name=f"out_proj_{tag}")
    h2, r2 = _rms_fwd(x1, g_mlp, f"rms_mlp_{tag}")
    u, act = _run(plan, _mm, h2, w["w_ff1_t"], tb=True,
                  epi=lambda acc: (acc, jnp.square(jnp.maximum(acc, 0.0))),
                  out_dtypes=(MXU, MXU), name=f"ff1_{tag}")
    x2 = _run(plan, _mm, act, w["w_ff2"], extras=(x1,), epi=lambda acc, res: res + acc, name=f"ff2_{tag}")
    h3, r3 = _rms_fwd(x2, g_pe, f"rms_pe_{tag}")
    x3 = _ple(p, w["w_pe_t"], h3, w["w_pg"], x2, backward=False, name=f"ple_fwd_{tag}")
    saved = dict(x=x, h1=h1, r1=r1, gates=gates, qkv=qkv, lse=lse, oa=oa, ob=ob, merged=merged,
                 x1=x1, h2=h2, r2=r2, u=u, act=act, x2=x2, h3=h3, r3=r3)
    return x3, saved


def _layer_bwd(dx3, sv, p, w, g_mix, g_mlp, g_pe, sinks, bias, layer, plan):
    tag = f"l{layer}"
    gw = {}
    wire = (WIRE,)

    def dw(name, a, b):
        gw[name] = _run(plan, _mm, a, b, ta=True, out_dtypes=wire, name=f"d{name}_{tag}")
        plan.grad(layer, name, gw[name])

    dpe, dgt = _ple(p, w["w_pe_t"], sv["h3"], w["w_pg"], dx3, backward=True, name=f"ple_bwd_{tag}")
    dw("w_pe", dpe, p)
    dw("w_pg", sv["h3"], dgt)
    dh3 = _run(plan, _mm, dgt, w["w_pg"], tb=True, name=f"dh_pe_{tag}")
    dx2, dx2b, dg_pe = _rms_bwd(sv["x2"], sv["r3"], g_pe, dh3, dx3, f"rms_pe_bwd_{tag}")
    dw("w_ff2", sv["act"], dx2b)
    du = _run(plan, _mm, dx2b, w["w_ff2"], tb=True, extras=(sv["u"],),
              epi=lambda acc, u: acc * (2.0 * jnp.maximum(u.astype(F32), 0.0)), out_dtypes=(MXU,),
              name=f"dact_{tag}")
    dw("w_ff1", du, sv["h2"])
    dh2 = _run(plan, _mm, du, w["w_ff1_t"], name=f"dh_mlp_{tag}")
    dx1, dx1b, dg_mlp = _rms_bwd(sv["x1"], sv["r2"], g_mlp, dh2, dx2, f"rms_mlp_bwd_{tag}")
    dw("w_o", sv["merged"], dx1b)
    dya, dyb, dga, dgb = _mix_bwd(dx1b, w["w_o"], sv["oa"], sv["ob"], w["w_up_a_t"], w["w_up_b_t"],
                                  sv["gates"], f"mix_bwd_{tag}")
    dw("w_up_a", dya, sv["oa"])
    dw("w_up_b", dyb, sv["ob"])
    doa = _run(plan, _mm, dya, w["w_up_a_t"], out_dtypes=(MXU,), name=f"do_a_{tag}")
    dob = _run(plan, _mm, dyb, w["w_up_b_t"], out_dtypes=(MXU,), name=f"do_b_{tag}")
    dqb, dkb, dvb, dbias, dsink = _run(plan, _swa_bwd, sv["qkv"], bias, sinks, dob, sv["lse"],
                                       name=f"swa_bwd_{tag}")
    dqa, dka, dva = _run(plan, _sb_bwd, sv["qkv"], doa, name=f"sb_bwd_{tag}")
    dqkv = jnp.concatenate([dqa, dka, dva, dqb, dkb, dvb], axis=1)
    gw_qkv = _mm(dqkv, sv["h1"], ta=True, out_dtypes=wire, name=f"dw_qkv_{tag}")
    gw_ga = _mm(dga, sv["h1"], ta=True, out_dtypes=wire, name=f"dw_ga_{tag}")
    gw_gb = _mm(dgb, sv["h1"], ta=True, out_dtypes=wire, name=f"dw_gb_{tag}")
    gw["w_in"] = jnp.concatenate([gw_qkv, gw_ga, gw_gb], axis=0)
    plan.grad(layer, "w_in", gw["w_in"])
    d = dga.shape[1]
    add = lambda acc, res: res + acc
    dh1 = _run(plan, _mm, dga, w["w_gate_t"][:d], name=f"dh_ga_{tag}")
    dh1 = _run(plan, _mm, dgb, w["w_gate_t"][d:], extras=(dh1,), epi=add, name=f"dh_gb_{tag}")
    dh1 = _run(plan, _mm, dqkv, w["w_qkv_t"], tk=768, extras=(dh1,), epi=add, name=f"dh_qkv_{tag}")
    dx, _, dg_mix = _rms_bwd(sv["x"], sv["r1"], g_mix, dh1, dx1, f"rms_mix_bwd_{tag}")
    small = dict(g_mix=dg_mix, g_mlp=dg_mlp, g_pe=dg_pe, sinks=dsink[:, 0, 0], dbias=dbias)
    return dx, gw, small


def _local_step(x, p, target, weights, g_mix, g_mlp, g_pe, g_final, sinks, rel_bias, plan=None):
    plan = _NoPlan() if plan is None else plan
    depth = g_mix.shape[0]
    buckets = jnp.asarray(_bucket_table())
    bias = _build_bias(rel_bias, buckets, "build_bias")
    saved, wfull = [], []
    h = x
    for l in range(depth):
        wfull.append(weights(l))
        h, sv = _layer_fwd(h, p[l], wfull[l], g_mix[l:l + 1], g_mlp[l:l + 1], g_pe[l:l + 1],
                           sinks[l], bias, f"l{l}", plan)
        saved.append(sv)
    loss_row, dx, dg_final = _loss_head(h, g_final[None, :], target, "loss_head")
    gws = [None] * depth
    smalls = [None] * depth
    for l in reversed(range(depth)):
        dx, gws[l], smalls[l] = _layer_bwd(dx, saved[l], p[l], wfull[l], g_mix[l:l + 1], g_mlp[l:l + 1],
                                           g_pe[l:l + 1], sinks[l], bias, l, plan)
        plan.layer_done(l)
    drel = _bias_grad([sm["dbias"] for sm in smalls], buckets, "bias_grad")[:, 0, :N_BUCKETS].T
    small = dict(
        g_mix=jnp.concatenate([sm["g_mix"] for sm in smalls], axis=0),
        g_mlp=jnp.concatenate([sm["g_mlp"] for sm in smalls], axis=0),
        g_pe=jnp.concatenate([sm["g_pe"] for sm in smalls], axis=0),
        g_final=dg_final[0],
        sinks=jnp.stack([sm["sinks"] for sm in smalls], axis=0),
        rel_bias=drel,
    )
    return loss_row, dx, gws, small


MESH_ID = pl.DeviceIdType.MESH
ANY = pl.BlockSpec(memory_space=pl.ANY)


def _position():
    return lax.axis_index("x"), lax.axis_index("y"), lax.axis_index("c")


def _run_comm(comm, name):
    ci, co = len(comm.inputs), len(comm.out_shapes)

    def body(*refs):
        cin, cout, csem = refs[:ci], refs[ci:ci + co], refs[ci + co:]
        pos = _position()
        comm.start(pos, cin, cout, csem)
        comm.finish(pos, cin, cout, csem)

    return pl.pallas_call(body, out_shape=comm.out_shapes, in_specs=[ANY] * ci, out_specs=[ANY] * co,
                          scratch_shapes=comm.sems, name=name)(*comm.inputs)


def _gather_comm(shards):
    n = len(shards)

    def copies(pos, x_refs, out_refs, sems):
        send_sems, recv_sems, local_sems = sems
        x, y, c = pos
        me, sibling = (x, y, c), (x, y, 1 - c)
        chips = [(1 - x, y), (x, 1 - y), (1 - x, 1 - y)]

        def slot(a, px, py, pc):
            return out_refs[a].at[4 * px + 2 * py + pc]

        def copy(a, k, block, to, src=None):
            return pltpu.make_async_remote_copy(
                src_ref=slot(a, *block) if src is None else src, dst_ref=slot(a, *block),
                send_sem=send_sems.at[a, k], recv_sem=recv_sems.at[a, k],
                device_id=to, device_id_type=MESH_ID)

        mine = [pltpu.make_async_copy(x_refs[a], slot(a, *me), local_sems.at[a]) for a in range(n)]
        first = []
        for a in range(n):
            first.append(copy(a, 0, me, sibling, src=x_refs[a]))
            first += [copy(a, 1 + j, me, (*chip, c), src=x_refs[a]) for j, chip in enumerate(chips)]
        return me, sibling, chips, copy, mine, first

    def start(pos, x_refs, out_refs, sems):
        _, _, _, _, mine, first = copies(pos, x_refs, out_refs, sems)
        for cp in mine + first:
            cp.start()

    def finish(pos, x_refs, out_refs, sems):
        me, sibling, chips, copy, mine, first = copies(pos, x_refs, out_refs, sems)
        c = pos[2]
        passed = []
        for j, chip in enumerate(chips):
            for a in range(n):
                copy(a, 1 + j, (*chip, c), me).wait_recv()
                fwd = copy(a, 4 + j, (*chip, c), sibling)
                fwd.start()
                passed.append(fwd)
        for a in range(n):
            copy(a, 0, sibling, me).wait_recv()
            for j, chip in enumerate(chips):
                copy(a, 4 + j, (*chip, 1 - c), me).wait_recv()
        for cp in first + passed:
            cp.wait_send()
        for cp in mine:
            cp.wait()

    return _Comm(shards, [jax.ShapeDtypeStruct((N_DEV,) + s.shape, s.dtype) for s in shards],
                 [pltpu.SemaphoreType.DMA((n, 7)), pltpu.SemaphoreType.DMA((n, 7)),
                  pltpu.SemaphoreType.DMA((n,))], start, finish)


def _exchange_comm(arrays, n_slots, route):
    n = len(arrays)

    def copies(pos, in_refs, out_refs, sems):
        send_sems, recv_sems = sems
        out = []
        for a in range(n):
            for j in range(n_slots):
                src_slot, peer = route(pos, j)
                out.append(pltpu.make_async_remote_copy(
                    src_ref=in_refs[a].at[src_slot], dst_ref=out_refs[a].at[j],
                    send_sem=send_sems.at[a, j], recv_sem=recv_sems.at[a, j],
                    device_id=peer, device_id_type=MESH_ID))
        return out

    def start(pos, in_refs, out_refs, sems):
        for cp in copies(pos, in_refs, out_refs, sems):
            cp.start()

    def finish(pos, in_refs, out_refs, sems):
        for cp in copies(pos, in_refs, out_refs, sems):
            cp.wait()

    return _Comm(arrays, [jax.ShapeDtypeStruct((n_slots,) + g.shape[1:], g.dtype) for g in arrays],
                 [pltpu.SemaphoreType.DMA((n, n_slots)), pltpu.SemaphoreType.DMA((n, n_slots))], start, finish)


def _rs_sibling_comm(gs):
    return _exchange_comm(gs, 4, lambda pos, j: (2 * j + (1 - pos[2]), (pos[0], pos[1], 1 - pos[2])))


def _chip_of(k, x, y):
    return x ^ ((k + 1) & 1), y ^ (((k + 1) >> 1) & 1)


def _chip_partials(pos, gs, recvs, name):
    n = len(gs)

    def body(pos_ref, *refs):
        for a in range(n):
            refs[2 * n + a][...] = (refs[a][...].astype(F32) + refs[n + a][...].astype(F32)
                                    ).astype(refs[2 * n + a].dtype)

    def g_map(k, pos_ref):
        cx, cy = _chip_of(k, pos_ref[0], pos_ref[1])
        return (4 * cx + 2 * cy + pos_ref[2], 0, 0)

    def r_map(k, pos_ref):
        cx, cy = _chip_of(k, pos_ref[0], pos_ref[1])
        return (2 * cx + cy, 0, 0)

    slab = [(None,) + g.shape[1:] for g in gs]
    return pl.pallas_call(
        body,
        grid_spec=pltpu.PrefetchScalarGridSpec(
            num_scalar_prefetch=1,
            grid=(4,),
            in_specs=[pl.BlockSpec(sh, g_map) for sh in slab] + [pl.BlockSpec(sh, r_map) for sh in slab],
            out_specs=[pl.BlockSpec(sh, lambda k, pos_ref: (k, 0, 0)) for sh in slab],
        ),
        out_shape=[jax.ShapeDtypeStruct((4,) + g.shape[1:], g.dtype) for g in gs],
        compiler_params=_cparams(),
        name=name,
    )(pos, *gs, *recvs)


def _rs_chips_comm(parts):
    return _exchange_comm(parts, 3, lambda pos, k: (k, (*_chip_of(k, pos[0], pos[1]), pos[2])))


def _adamw_math(w, g, m, v):
    m = ADAM_B1 * m + (1.0 - ADAM_B1) * g
    v = ADAM_B2 * v + (1.0 - ADAM_B2) * (g * g)
    m_hat = m / (1.0 - ADAM_B1 ** ADAM_STEP)
    v_hat = v / (1.0 - ADAM_B2 ** ADAM_STEP)
    delta = -ADAM_LR * (m_hat / (jnp.sqrt(v_hat) + ADAM_EPS) + ADAM_WD * w)
    return delta, m, v


def _adamw_weight(parts, recvs, w, m, v, name, grad_t=False):
    depth, a, b = w.shape
    ta = _tile(a, 288, unit=LANES if grad_t else 16)
    ni = a // ta
    g_block = (b, ta) if grad_t else (ta, b)

    def body(*refs):
        p_refs, r_refs = refs[:depth], refs[depth:2 * depth]
        w_ref, m_ref, v_ref = refs[2 * depth:2 * depth + 3]
        g_out, d_out, m_out, v_out = refs[2 * depth + 3:]
        layer = pl.program_id(0)
        g = jnp.zeros(g_block, F32)
        for l in range(depth):
            gl = p_refs[l][...].astype(F32)
            for k in range(3):
                gl = gl + r_refs[l][k].astype(F32)
            g = jnp.where(layer == l, gl, g)
        if grad_t:
            g = g.T
        delta, m_new, v_new = _adamw_math(w_ref[...], g, m_ref[...], v_ref[...])
        g_out[...] = g
        d_out[...] = delta
        m_out[...] = m_new
        v_out[...] = v_new

    def hold(l):
        return lambda layer, i: jnp.where(layer == l, i, jnp.where(layer < l, 0, ni - 1))

    def g_index(slot, f):
        if grad_t:
            return lambda layer, i: (slot, 0, f(layer, i))
        return lambda layer, i: (slot, f(layer, i), 0)

    p_specs = [pl.BlockSpec((None,) + g_block, g_index(3, hold(l))) for l in range(depth)]
    r_specs = [pl.BlockSpec((3,) + g_block, g_index(0, hold(l))) for l in range(depth)]
    row = pl.BlockSpec((None, ta, b), lambda layer, i: (layer, i, 0))
    return pl.pallas_call(
        body,
        grid=(depth, ni),
        in_specs=p_specs + r_specs + [row, row, row],
        out_specs=[row] * 4,
        out_shape=[jax.ShapeDtypeStruct(w.shape, F32)] * 4,
        compiler_params=_cparams(),
        name=name,
    )(*parts, *recvs, w, m, v)


def _adamw_replicated(gathered, w, m, v, name):
    r, lanes = w.shape

    def body(g_ref, w_ref, m_ref, v_ref, g_out, d_out, m_out, v_out):
        g = g_ref[0]
        for k in range(1, N_DEV):
            g = g + g_ref[k]
        delta, m_new, v_new = _adamw_math(w_ref[...], g, m_ref[...], v_ref[...])
        g_out[...] = g
        d_out[...] = delta
        m_out[...] = m_new
        v_out[...] = v_new

    return pl.pallas_call(
        body,
        out_shape=[jax.ShapeDtypeStruct((r, lanes), F32)] * 4,
        name=name,
    )(gathered, w, m, v)


def _wire_shard(name, shard):
    return (shard.T if name in COL_SHARDED else shard).astype(WIRE)


def _full_weight(gathered):
    return gathered.reshape(N_DEV * gathered.shape[1], gathered.shape[2])


def _to_slabs(gfull):
    return gfull.reshape(N_DEV, gfull.shape[0] // N_DEV, gfull.shape[1])


def _pack_small(arrs):
    rows = []
    for a in arrs:
        flat = a.astype(F32).reshape(-1)
        pad = (-flat.shape[0]) % LANES
        rows.append(jnp.pad(flat, (0, pad)).reshape(-1, LANES))
    packed = jnp.concatenate(rows, axis=0)
    return jnp.pad(packed, ((0, (-packed.shape[0]) % 8), (0, 0)))


def _unpack_small(packed, shapes):
    out, off = [], 0
    for shp in shapes:
        n = math.prod(shp)
        rows = -(-n // LANES)
        out.append(packed[off:off + rows].reshape(-1)[:n].reshape(shp))
        off += rows
    return out


def _of(layer, *names):
    return tuple((layer, n) for n in names)


MIXER_W = ("w_in", "w_up_a", "w_up_b", "w_o")
MLP_W = ("w_ff1", "w_ff2", "w_pe", "w_pg")

GATHERS = (
    ("rms_mix_l0", _of(0, "w_in")),
    ("proj_qkv_l0", _of(0, "w_up_a", "w_up_b", "w_o")),
    ("proj_gate_l0", _of(0, "w_pe", "w_pg")),
    ("sb_fwd_l0", _of(0, "w_ff1", "w_ff2")),
    ("swa_fwd_l0", _of(1, "w_in")),
    ("sb_fwd_l1", _of(1, "w_up_a", "w_up_b", "w_o", "w_pe", "w_pg", "w_ff1")),
    ("swa_fwd_l1", _of(1, "w_ff2")),
)
REDUCES = (
    (_of(1, "w_ff1"), "dw_ff2_l0", "dact_l0"),
    (_of(1, "w_ff2"), "dw_ff2_l0", "dw_ff1_l0"),
    (_of(1, "w_in"), "dw_ff2_l0", "sb_bwd_l0"),
    (_of(1, "w_up_a", "w_up_b", "w_o", "w_pe", "w_pg"), "dw_ff2_l0", "swa_bwd_l0"),
    (_of(0, *MLP_W), "dh_mlp_l0", "sb_bwd_l0"),
    (_of(0, "w_o", "w_up_a", "w_up_b"), "do_a_l0", "sb_bwd_l0"),
    (_of(0, "w_in"), "dh_ga_l0", "dh_qkv_l0"),
)


def _merge_comms(comms):
    if len(comms) == 1:
        return comms[0]

    def cuts(counts):
        edges = [0]
        for c in counts:
            edges.append(edges[-1] + c)
        return [slice(a, b) for a, b in zip(edges[:-1], edges[1:])]

    s_in = cuts([len(c.inputs) for c in comms])
    s_out = cuts([len(c.out_shapes) for c in comms])
    s_sem = cuts([len(c.sems) for c in comms])

    def start(pos, cin, cout, csem):
        for c, i, o, s in zip(comms, s_in, s_out, s_sem):
            c.start(pos, cin[i], cout[o], csem[s])

    def finish(pos, cin, cout, csem):
        for c, i, o, s in zip(comms, s_in, s_out, s_sem):
            c.finish(pos, cin[i], cout[o], csem[s])

    return _Comm(sum([c.inputs for c in comms], []), sum([c.out_shapes for c in comms], []),
                 sum([c.sems for c in comms], []), start, finish)


class _LayerWeights:
    def __init__(self, full, layer):
        self.full, self.layer, self.cache = full, layer, {}

    def __getitem__(self, name):
        if name not in self.cache:
            if name == "w_qkv_t":
                self.cache[name] = self.full[(self.layer, "w_in")][:QKV_COLS]
            elif name == "w_gate_t":
                self.cache[name] = self.full[(self.layer, "w_in")][QKV_COLS:]
            else:
                base = name[:-2] if name.endswith("_t") else name
                assert (base in COL_SHARDED) == name.endswith("_t"), name
                self.cache[name] = self.full[(self.layer, base)]
        return self.cache[name]


class _Plan:
    def __init__(self, w_sh, pos):
        self.w_sh = dict(zip(WEIGHTS, w_sh))
        self.pos = pos
        self.full, self.gw, self.parts, self.recv = {}, {}, {}, {}
        self.slabs = {}
        self.hosted = {}
        for i, (host, _) in enumerate(GATHERS):
            if host is not None:
                self.hosted.setdefault(host, []).append(("gather", i))
        for i, (_, sib_host, chip_host) in enumerate(REDUCES):
            assert (sib_host is None) == (chip_host is None)
            if sib_host is not None:
                self.hosted.setdefault(sib_host, []).append(("sibling", i))
                self.hosted.setdefault(chip_host, []).append(("chips", i))

    def _gather(self, i):
        return _gather_comm([_wire_shard(n, self.w_sh[n][layer]) for layer, n in GATHERS[i][1]])

    def _gathered(self, i, outs):
        for (layer, n), g in zip(GATHERS[i][1], outs):
            self.full[(layer, n)] = _full_weight(g)

    def weights(self, layer):
        for i, (host, items) in enumerate(GATHERS):
            if host is None and items[0][0] == layer:
                self._gathered(i, _run_comm(self._gather(i), f"gather_{i}"))
        return _LayerWeights(self.full, layer)

    def grad(self, layer, name, value):
        self.gw[(layer, name)] = value

    def _sibling(self, i):
        self.slabs[i] = [_to_slabs(self.gw[item]) for item in REDUCES[i][0]]
        return _rs_sibling_comm(self.slabs[i])

    def _sibling_done(self, i, outs):
        parts = _chip_partials(self.pos, self.slabs[i], outs, f"chip_partials_{i}")
        for item, part in zip(REDUCES[i][0], parts):
            self.parts[item] = part

    def _chips(self, i):
        return _rs_chips_comm([self.parts[item] for item in REDUCES[i][0]])

    def _chips_done(self, i, outs):
        for item, r in zip(REDUCES[i][0], outs):
            self.recv[item] = r

    def layer_done(self, layer):
        for i, (items, sib_host, _) in enumerate(REDUCES):
            if sib_host is None and items[0][0] == layer:
                self._sibling_done(i, _run_comm(self._sibling(i), f"reduce_sibling_{i}"))
                self._chips_done(i, _run_comm(self._chips(i), f"reduce_chips_{i}"))

    def comm(self, name):
        if name not in self.hosted:
            return None
        make = {"gather": self._gather, "sibling": self._sibling, "chips": self._chips}
        return _merge_comms([make[kind](i) for kind, i in self.hosted[name]])

    def done(self, name, outs):
        took = {"gather": self._gathered, "sibling": self._sibling_done, "chips": self._chips_done}
        off = 0
        for kind, i in self.hosted[name]:
            n = len(GATHERS[i][1]) if kind == "gather" else len(REDUCES[i][0])
            took[kind](i, outs[off:off + n])
            off += n


def kernel(x, p, w_in, w_up_a, w_up_b, w_o, w_ff1, w_ff2, w_pe, w_pg, g_mix, g_mlp, g_pe, g_final, sinks, rel_bias, loss_target, m_w_in, m_w_up_a, m_w_up_b, m_w_o, m_w_ff1, m_w_ff2, m_w_pe, m_w_pg, m_g_mix, m_g_mlp, m_g_pe, m_g_final, m_sinks, m_rel_bias, v_w_in, v_w_up_a, v_w_up_b, v_w_o, v_w_ff1, v_w_ff2, v_w_pe, v_w_pg, v_g_mix, v_g_mlp, v_g_pe, v_g_final, v_sinks, v_rel_bias):
    w_sh = [w_in, w_up_a, w_up_b, w_o, w_ff1, w_ff2, w_pe, w_pg]
    m_sh = [m_w_in, m_w_up_a, m_w_up_b, m_w_o, m_w_ff1, m_w_ff2, m_w_pe, m_w_pg]
    v_sh = [v_w_in, v_w_up_a, v_w_up_b, v_w_o, v_w_ff1, v_w_ff2, v_w_pe, v_w_pg]
    depth = w_in.shape[0]
    assert depth == 2 and x.shape[-1] * 2 + QKV_COLS == w_in.shape[2] * N_DEV

    px, py, pc = _position()
    plan = _Plan(w_sh, jnp.stack([px, py, pc]).astype(jnp.int32))
    loss_row, grad_x, _, small = _local_step(
        x[0], p[:, 0], loss_target[0], plan.weights, g_mix, g_mlp, g_pe, g_final, sinks, rel_bias, plan=plan)

    grad_w, delta_w, new_m, new_v = [], [], [], []
    for a, name in enumerate(WEIGHTS):
        parts = [plan.parts[(l, name)] for l in range(depth)]
        recvs = [plan.recv[(l, name)] for l in range(depth)]
        if name == "w_in":
            flip = lambda t: t.transpose(0, 2, 1)
            outs = [flip(o) for o in _adamw_weight(parts, recvs, flip(w_sh[a]), flip(m_sh[a]), flip(v_sh[a]),
                                                   f"adamw_{name}")]
        else:
            outs = _adamw_weight(parts, recvs, w_sh[a], m_sh[a], v_sh[a], f"adamw_{name}",
                                 grad_t=name in COL_SHARDED)
        for lst, o in zip((grad_w, delta_w, new_m, new_v), outs):
            lst.append(o)

    small_w = [g_mix, g_mlp, g_pe, g_final, sinks, rel_bias]
    small_m = [m_g_mix, m_g_mlp, m_g_pe, m_g_final, m_sinks, m_rel_bias]
    small_v = [v_g_mix, v_g_mlp, v_g_pe, v_g_final, v_sinks, v_rel_bias]
    small_shapes = [a.shape for a in small_w] + [(1,)]
    zero = jnp.zeros((1,), F32)
    small_g = _pack_small([small[n] for n in SMALL] + [loss_row[0, :1]])
    small_all = _run_comm(_gather_comm([small_g]), "gather_small")[0]
    packed_s = _adamw_replicated(small_all, _pack_small(small_w + [zero]), _pack_small(small_m + [zero]),
                                 _pack_small(small_v + [zero + 1.0]), "adamw_replicated")
    sg, sd, sm, sv = [_unpack_small(t, small_shapes) for t in packed_s]
    loss = sg[-1][0]

    return (loss, grad_x[None], *grad_w, *sg[:-1], *delta_w, *sd[:-1], *new_m, *sm[:-1], *new_v, *sv[:-1])
```

```python
import functools
import math

import numpy as np
import jax
import jax.numpy as jnp
from jax import lax
from jax.experimental import pallas as pl
from jax.experimental.pallas import tpu as pltpu

F32 = jnp.float32
MXU = jnp.bfloat16
WIRE = jnp.bfloat16

HEAD_DIM = 64
SB_HEADS = 8
SW_HEADS = 8
SW_KV = 2
SW_GROUP = SW_HEADS // SW_KV
BLOCK = 128
N_BUCKETS = 32
MAX_DISTANCE = 128
EPS = 1e-6
SCALE = HEAD_DIM ** -0.5
SB_W = SB_HEADS * HEAD_DIM
SW_W = SW_HEADS * HEAD_DIM
QKV_COLS = 3 * SB_W + SW_W + 2 * SW_KV * HEAD_DIM
N_DEV = 8
LANES = 128
N_PAIR = SB_HEADS // 2
NEG = -1e30

ADAM_LR = 0.001
ADAM_B1 = 0.9
ADAM_B2 = 0.999
ADAM_EPS = 1e-08
ADAM_WD = 0.01
ADAM_STEP = 10

VMEM_LIMIT = 48 * 1024 * 1024
SB_TQ = 256
SB_DEAD = -105.0

WEIGHTS = ("w_in", "w_up_a", "w_up_b", "w_o", "w_ff1", "w_ff2", "w_pe", "w_pg")
COL_SHARDED = ("w_in", "w_up_a", "w_up_b", "w_ff1", "w_pe")
SMALL = ("g_mix", "g_mlp", "g_pe", "g_final", "sinks", "rel_bias")


def _cparams(**kw):
    return pltpu.CompilerParams(vmem_limit_bytes=VMEM_LIMIT, **kw)


def _dot(a, b):
    return jnp.dot(a, b, preferred_element_type=F32)


def _dot_nt(a, b):
    return lax.dot_general(a, b, (((1,), (1,)), ((), ())), preferred_element_type=F32)


def _dot_tn(a, b):
    return lax.dot_general(a, b, (((0,), (0,)), ((), ())), preferred_element_type=F32)


def _tile(n, target, unit=LANES):
    if n <= target:
        return n
    t = (target // unit) * unit
    while t > unit and n % t:
        t -= unit
    assert n % t == 0, (n, target)
    return t


def _sigmoid(x):
    return 1.0 / (1.0 + jnp.exp(-x))


class _Comm:
    def __init__(self, inputs, out_shapes, sems, start, finish):
        self.inputs, self.out_shapes, self.sems = list(inputs), list(out_shapes), list(sems)
        self.start, self.finish = start, finish


def _call(body, *, grid, in_specs, out_specs, out_shape, scratch_shapes=(), args, name, comm=None):
    n_in, n_out, n_scr = len(in_specs), len(out_shape), len(scratch_shapes)
    if comm is None:
        outs = pl.pallas_call(body, grid=grid, in_specs=list(in_specs), out_specs=list(out_specs),
                              out_shape=list(out_shape), scratch_shapes=list(scratch_shapes),
                              compiler_params=_cparams(), name=name)(*args)
        return list(outs), None
    ci, co = len(comm.inputs), len(comm.out_shapes)
    any_spec = pl.BlockSpec(memory_space=pl.ANY)

    def wrapped(*refs):
        ins, cin = refs[:n_in], refs[n_in:n_in + ci]
        o0 = n_in + ci
        outs, cout = refs[o0:o0 + n_out], refs[o0 + n_out:o0 + n_out + co]
        s0 = o0 + n_out + co
        scr, csem = refs[s0:s0 + n_scr], refs[s0 + n_scr:]
        ids = [pl.program_id(d) for d in range(len(grid))]
        first = functools.reduce(jnp.logical_and, [i == 0 for i in ids])
        last = functools.reduce(jnp.logical_and, [i == g - 1 for i, g in zip(ids, grid)])
        pos = (lax.axis_index("x"), lax.axis_index("y"), lax.axis_index("c"))

        @pl.when(first)
        def _():
            comm.start(pos, cin, cout, csem)

        body(*ins, *outs, *scr)

        @pl.when(last)
        def _():
            comm.finish(pos, cin, cout, csem)

    outs = pl.pallas_call(wrapped, grid=grid, in_specs=list(in_specs) + [any_spec] * ci,
                          out_specs=list(out_specs) + [any_spec] * co,
                          out_shape=list(out_shape) + comm.out_shapes,
                          scratch_shapes=list(scratch_shapes) + comm.sems,
                          compiler_params=_cparams(), name=name)(*args, *comm.inputs)
    return list(outs[:n_out]), list(outs[n_out:])


def _mm(a, b, *, ta=False, tb=False, extras=(), epi=None, out_dtypes=(F32,),
        tm=1024, tn=1024, tk=1024, name, comm=None):
    if ta:
        kdim, m = a.shape
    else:
        m, kdim = a.shape
    n = b.shape[0] if tb else b.shape[1]
    assert (b.shape[1] if tb else b.shape[0]) == kdim
    tm, tn, tk = _tile(m, tm), _tile(n, tn), _tile(kdim, tk)
    nk = kdim // tk
    n_ex, n_out = len(extras), len(out_dtypes)

    a_spec = (pl.BlockSpec((tk, tm), lambda i, j, k: (k, i)) if ta
              else pl.BlockSpec((tm, tk), lambda i, j, k: (i, k)))
    b_spec = (pl.BlockSpec((tn, tk), lambda i, j, k: (j, k)) if tb
              else pl.BlockSpec((tk, tn), lambda i, j, k: (k, j)))
    ex_specs = []
    for e in extras:
        assert e.shape in ((m, n), (1, n)), (e.shape, m, n)
        ex_specs.append(pl.BlockSpec((tm, tn), lambda i, j, k: (i, j)) if e.shape[0] == m
                        else pl.BlockSpec((1, tn), lambda i, j, k: (0, j)))
    out_specs, out_shape = [], []
    for dt in out_dtypes:
        if isinstance(dt, tuple):
            assert dt[1] == "col" and tn == n, "a per-row output needs whole rows in one tile"
            out_specs.append(pl.BlockSpec((tm, 1), lambda i, j, k: (i, 0)))
            out_shape.append(jax.ShapeDtypeStruct((m, 1), dt[0]))
        else:
            out_specs.append(pl.BlockSpec((tm, tn), lambda i, j, k: (i, j)))
            out_shape.append(jax.ShapeDtypeStruct((m, n), dt))

    def body(a_ref, b_ref, *rest):
        ex_refs = rest[:n_ex]
        out_refs = rest[n_ex:n_ex + n_out]
        acc = rest[-1]
        k = pl.program_id(2)

        def prod():
            av = a_ref[...].astype(MXU)
            bv = b_ref[...].astype(MXU)
            return _dot_tn(av, bv) if ta else (_dot_nt(av, bv) if tb else _dot(av, bv))

        def finish(res):
            if epi is not None:
                res = epi(res, *[e[...] for e in ex_refs])
            if not isinstance(res, tuple):
                res = (res,)
            for o_ref, r in zip(out_refs, res):
                o_ref[...] = r.astype(o_ref.dtype)

        if nk == 1:
            finish(prod())
            return

        @pl.when(k == 0)
        def _():
            acc[...] = prod()

        @pl.when(jnp.logical_and(k > 0, k < nk - 1))
        def _():
            acc[...] += prod()

        @pl.when(k == nk - 1)
        def _():
            finish(acc[...] + prod())

    outs, couts = _call(
        body,
        grid=(m // tm, n // tn, nk),
        in_specs=[a_spec, b_spec] + ex_specs,
        out_specs=out_specs,
        out_shape=out_shape,
        scratch_shapes=[pltpu.VMEM((tm, tn), F32)],
        args=(a, b, *extras), name=name, comm=comm)
    res = outs[0] if n_out == 1 else tuple(outs)
    return res if comm is None else (res, couts)


def _rms_fwd(x, g, name, comm=None):
    s, d = x.shape
    tr = _tile(s, 256)

    def body(x_ref, g_ref, h_ref, r_ref):
        xf = x_ref[...]
        r = lax.rsqrt(jnp.mean(xf * xf, axis=-1, keepdims=True) + EPS)
        h_ref[...] = ((xf * r) * g_ref[...]).astype(h_ref.dtype)
        r_ref[...] = r

    outs, couts = _call(
        body,
        grid=(s // tr,),
        in_specs=[pl.BlockSpec((tr, d), lambda i: (i, 0)), pl.BlockSpec((1, d), lambda i: (0, 0))],
        out_specs=[pl.BlockSpec((tr, d), lambda i: (i, 0)), pl.BlockSpec((tr, 1), lambda i: (i, 0))],
        out_shape=[jax.ShapeDtypeStruct((s, d), MXU), jax.ShapeDtypeStruct((s, 1), F32)],
        args=(x, g), name=name, comm=comm)
    return tuple(outs) if comm is None else (tuple(outs), couts)


def _rms_bwd(x, r, g, dh, dres, name):
    s, d = x.shape
    tr = _tile(s, 256)

    def body(x_ref, r_ref, g_ref, dh_ref, dres_ref, dx_ref, dxb_ref, dg_ref):
        @pl.when(pl.program_id(0) == 0)
        def _():
            dg_ref[...] = jnp.zeros_like(dg_ref)

        rr = r_ref[...]
        xhat = x_ref[...] * rr
        dh_v = dh_ref[...]
        dxhat = dh_v * g_ref[...]
        mean = jnp.mean(dxhat * xhat, axis=-1, keepdims=True)
        dx = dres_ref[...] + rr * (dxhat - xhat * mean)
        dx_ref[...] = dx
        dxb_ref[...] = dx.astype(dxb_ref.dtype)
        dg_ref[...] += jnp.sum(dh_v * xhat, axis=0, keepdims=True)

    row = pl.BlockSpec((tr, d), lambda i: (i, 0))
    vec = pl.BlockSpec((1, d), lambda i: (0, 0))
    return pl.pallas_call(
        body,
        grid=(s // tr,),
        in_specs=[row, pl.BlockSpec((tr, 1), lambda i: (i, 0)), vec, row, row],
        out_specs=[row, row, vec],
        out_shape=[jax.ShapeDtypeStruct((s, d), F32), jax.ShapeDtypeStruct((s, d), MXU),
                   jax.ShapeDtypeStruct((1, d), F32)],
        compiler_params=_cparams(),
        name=name,
    )(x, r, g, dh, dres)


def _loss_head(x, g, target, name):
    s, d = x.shape
    tr = _tile(s, 256)

    def body(x_ref, g_ref, t_ref, loss_ref, dx_ref, dg_ref):
        @pl.when(pl.program_id(0) == 0)
        def _():
            dg_ref[...] = jnp.zeros_like(dg_ref)
            loss_ref[...] = jnp.zeros_like(loss_ref)

        xf = x_ref[...]
        gv = g_ref[...]
        r = lax.rsqrt(jnp.mean(xf * xf, axis=-1, keepdims=True) + EPS)
        xhat = xf * r
        err = xhat * gv - t_ref[...]
        loss_ref[...] += 0.5 * jnp.sum(jnp.mean(err * err, axis=-1, keepdims=True), axis=0, keepdims=True)
        dy = err * (1.0 / d)
        dxhat = dy * gv
        mean = jnp.mean(dxhat * xhat, axis=-1, keepdims=True)
        dx_ref[...] = r * (dxhat - xhat * mean)
        dg_ref[...] += jnp.sum(dy * xhat, axis=0, keepdims=True)

    row = pl.BlockSpec((tr, d), lambda i: (i, 0))
    vec = pl.BlockSpec((1, d), lambda i: (0, 0))
    return pl.pallas_call(
        body,
        grid=(s // tr,),
        in_specs=[row, vec, row],
        out_specs=[pl.BlockSpec((1, LANES), lambda i: (0, 0)), row, vec],
        out_shape=[jax.ShapeDtypeStruct((1, LANES), F32), jax.ShapeDtypeStruct((s, d), F32),
                   jax.ShapeDtypeStruct((1, d), F32)],
        compiler_params=_cparams(),
        name=name,
    )(x, g, target)


def _mix_fwd(oa, ob, wa_t, wb_t, gates, name):
    s, kd = oa.shape
    d = wa_t.shape[0]
    tm, tn = _tile(s, 1024), _tile(d, 512)
    nj = d // tn

    def body(oa_ref, ob_ref, wa_ref, wb_ref, ga_ref, gb_ref, out_ref):
        ya = _dot_nt(oa_ref[...], wa_ref[...])
        yb = _dot_nt(ob_ref[...], wb_ref[...])
        out_ref[...] = (_sigmoid(ga_ref[...].astype(F32)) * ya
                        + _sigmoid(gb_ref[...].astype(F32)) * yb).astype(out_ref.dtype)

    o_spec = pl.BlockSpec((tm, kd), lambda i, j: (i, 0))
    w_spec = pl.BlockSpec((tn, kd), lambda i, j: (j, 0))
    return pl.pallas_call(
        body,
        grid=(s // tm, nj),
        in_specs=[o_spec, o_spec, w_spec, w_spec,
                  pl.BlockSpec((tm, tn), lambda i, j: (i, j)),
                  pl.BlockSpec((tm, tn), lambda i, j: (i, j + nj))],
        out_specs=pl.BlockSpec((tm, tn), lambda i, j: (i, j)),
        out_shape=jax.ShapeDtypeStruct((s, d), MXU),
        compiler_params=_cparams(),
        name=name,
    )(oa, ob, wa_t, wb_t, gates, gates)


def _mix_bwd(dx, w_o, oa, ob, wa_t, wb_t, gates, name):
    s, kd = oa.shape
    d = wa_t.shape[0]
    tm, tn = _tile(s, 1024), _tile(d, 512)
    nj = d // tn

    def body(dx_ref, wo_ref, oa_ref, ob_ref, wa_ref, wb_ref, ga_ref, gb_ref,
             dya_ref, dyb_ref, dga_ref, dgb_ref):
        dm = _dot_nt(dx_ref[...], wo_ref[...])
        ya = _dot_nt(oa_ref[...], wa_ref[...])
        yb = _dot_nt(ob_ref[...], wb_ref[...])
        sa = _sigmoid(ga_ref[...].astype(F32))
        sb = _sigmoid(gb_ref[...].astype(F32))
        dya_ref[...] = (dm * sa).astype(dya_ref.dtype)
        dyb_ref[...] = (dm * sb).astype(dyb_ref.dtype)
        dga_ref[...] = (dm * ya * sa * (1.0 - sa)).astype(dga_ref.dtype)
        dgb_ref[...] = (dm * yb * sb * (1.0 - sb)).astype(dgb_ref.dtype)

    o_spec = pl.BlockSpec((tm, kd), lambda i, j: (i, 0))
    w_spec = pl.BlockSpec((tn, kd), lambda i, j: (j, 0))
    t_spec = pl.BlockSpec((tm, tn), lambda i, j: (i, j))
    return pl.pallas_call(
        body,
        grid=(s // tm, nj),
        in_specs=[pl.BlockSpec((tm, d), lambda i, j: (i, 0)),
                  pl.BlockSpec((tn, d), lambda i, j: (j, 0)),
                  o_spec, o_spec, w_spec, w_spec, t_spec,
                  pl.BlockSpec((tm, tn), lambda i, j: (i, j + nj))],
        out_specs=[t_spec] * 4,
        out_shape=[jax.ShapeDtypeStruct((s, d), MXU)] * 4,
        compiler_params=_cparams(),
        name=name,
    )(dx, w_o, oa, ob, wa_t, wb_t, gates, gates)


def _ple(p, w_pe_t, h, w_pg, other, *, backward, name):
    s, kp = p.shape
    d = w_pe_t.shape[0]
    tm, tn = _tile(s, 1024), _tile(d, 512)

    def body(p_ref, wpe_ref, h_ref, wpg_ref, other_ref, *out_refs):
        pe = _dot_nt(p_ref[...].astype(MXU), wpe_ref[...])
        gt = _dot(h_ref[...], wpg_ref[...])
        sg = _sigmoid(gt)
        if backward:
            dout = other_ref[...]
            out_refs[0][...] = (dout * sg).astype(out_refs[0].dtype)
            out_refs[1][...] = (dout * pe * sg * (1.0 - sg)).astype(out_refs[1].dtype)
        else:
            out_refs[0][...] = other_ref[...] + pe * sg

    t_spec = pl.BlockSpec((tm, tn), lambda i, j: (i, j))
    if backward:
        out_specs, out_shape = [t_spec, t_spec], [jax.ShapeDtypeStruct((s, d), MXU)] * 2
    else:
        out_specs, out_shape = [t_spec], [jax.ShapeDtypeStruct((s, d), F32)]
    outs = pl.pallas_call(
        body,
        grid=(s // tm, d // tn),
        in_specs=[pl.BlockSpec((tm, kp), lambda i, j: (i, 0)),
                  pl.BlockSpec((tn, kp), lambda i, j: (j, 0)),
                  pl.BlockSpec((tm, d), lambda i, j: (i, 0)),
                  pl.BlockSpec((d, tn), lambda i, j: (0, j)),
                  t_spec],
        out_specs=out_specs,
        out_shape=out_shape,
        compiler_params=_cparams(),
        name=name,
    )(p, w_pe_t, h, w_pg, other)
    return tuple(outs) if backward else outs[0]


def _split_dot(x, tri):
    hi = x.astype(jnp.bfloat16)
    lo = (x - hi.astype(F32)).astype(jnp.bfloat16)
    return _dot(hi, tri) + _dot(lo, tri)


def _log_sigmoids(z):
    lb = jnp.minimum(z, 0.0) - jnp.log1p(jnp.exp(-jnp.abs(z)))
    return lb, lb - z


def _head_lanes(hh):
    lane = lax.broadcasted_iota(jnp.int32, (1, LANES), 1)
    return jnp.logical_and(lane >= hh * HEAD_DIM, lane < (hh + 1) * HEAD_DIM)


def _sb_fwd(qkv, name, comm=None):
    s = qkv.shape[0]
    tq = _tile(s, SB_TQ)

    def body(q_ref, k_ref, v_ref, o_ref):
        i = pl.program_id(1)
        qf = q_ref[...].astype(F32) * SCALE
        row = lax.broadcasted_iota(jnp.int32, (tq, tq), 0)
        col = lax.broadcasted_iota(jnp.int32, (tq, tq), 1)
        causal = col < row
        tri = jnp.where(row > col, 1.0, 0.0).astype(jnp.bfloat16)

        qms = [jnp.where(_head_lanes(hh), qf, 0.0).astype(MXU) for hh in range(2)]

        def block(kb, cs, accs, masked, gate=None):
            rows = pl.ds(pl.multiple_of(kb * tq, tq), tq)
            ks, vs = k_ref[rows, :], v_ref[rows, :]
            new_c, new_acc = [], []
            for hh in range(2):
                lb, lm = _log_sigmoids(_dot_nt(qms[hh], ks))
                if masked:
                    lm = jnp.where(causal, lm, 0.0)
                if gate is not None:
                    lm = lm * gate
                a = jnp.exp(lb + _split_dot(lm, tri) + cs[hh])
                if masked:
                    a = jnp.where(causal, a, 0.0)
                if gate is not None:
                    a = a * gate
                new_acc.append(accs[hh] + _dot(a.astype(MXU), vs))
                new_c.append(cs[hh] + jnp.sum(lm, axis=1, keepdims=True))
            return tuple(new_c), tuple(new_acc)

        def top(cs):
            return jnp.maximum(jnp.max(cs[0]), jnp.max(cs[1]))

        zc, za = jnp.zeros((tq, 1), F32), jnp.zeros((tq, LANES), F32)
        cs, accs = block(i, (zc, zc), (za, za), True)
        cs, accs = block(jnp.maximum(i - 1, 0), cs, accs, False, jnp.where(i > 0, 1.0, 0.0))

        def live(st):
            return jnp.logical_and(st[0] >= 0, st[1] > SB_DEAD)

        def walk(st):
            cs, accs = block(st[0], st[2], st[3], False)
            return st[0] - 1, top(cs), cs, accs

        accs = lax.while_loop(live, walk, (i - 2, top(cs), cs, accs))[3]
        o_ref[...] = jnp.where(_head_lanes(0), accs[0], accs[1]).astype(o_ref.dtype)

    outs, couts = _call(
        body,
        grid=(N_PAIR, s // tq),
        in_specs=[pl.BlockSpec((tq, LANES), lambda p, i: (i, p)),
                  pl.BlockSpec((s, LANES), lambda p, i: (0, N_PAIR + p)),
                  pl.BlockSpec((s, LANES), lambda p, i: (0, 2 * N_PAIR + p))],
        out_specs=[pl.BlockSpec((tq, LANES), lambda p, i: (i, p))],
        out_shape=[jax.ShapeDtypeStruct((s, SB_W), MXU)],
        args=(qkv, qkv, qkv), name=name, comm=comm)
    return outs[0] if comm is None else (outs[0], couts)


def _sb_bwd(qkv, do, name, comm=None):
    s = qkv.shape[0]
    tq = _tile(s, SB_TQ)
    nq = s // tq

    def body(q_ref, k_ref, v_ref, do_ref, dq_ref, dk_ref, dv_ref, dk_acc, dv_acc, carries):
        i = pl.program_id(1)

        @pl.when(i == 0)
        def _():
            dk_acc[...] = jnp.zeros_like(dk_acc)
            dv_acc[...] = jnp.zeros_like(dv_acc)

        qf = q_ref[...].astype(F32) * SCALE
        dof = do_ref[...]
        row = lax.broadcasted_iota(jnp.int32, (tq, tq), 0)
        col = lax.broadcasted_iota(jnp.int32, (tq, tq), 1)
        causal = col < row
        tri_rev = jnp.where(row > col, 1.0, 0.0).astype(jnp.bfloat16)
        tri_excl = jnp.where(row < col, 1.0, 0.0).astype(jnp.bfloat16)

        qms = [jnp.where(_head_lanes(hh), qf, 0.0).astype(MXU) for hh in range(2)]
        doms = [jnp.where(_head_lanes(hh), dof, jnp.zeros_like(dof)) for hh in range(2)]

        def terms(kb, masked):
            rows = pl.ds(pl.multiple_of(kb * tq, tq), tq)
            ks = k_ref[rows, :]
            out = []
            for hh in range(2):
                lb, lm = _log_sigmoids(_dot_nt(qms[hh], ks))
                if masked:
                    lm = jnp.where(causal, lm, 0.0)
                out.append((lb, lm))
            return out

        def row_sums(kb, masked, pre=None):
            pre = terms(kb, masked) if pre is None else pre
            return [jnp.sum(lm, axis=1, keepdims=True) for _, lm in pre]

        def top(cs):
            return jnp.maximum(jnp.max(cs[0]), jnp.max(cs[1]))

        def live(st):
            return jnp.logical_and(st[0] >= 0, st[1] > SB_DEAD)

        def record(st):
            kb, cs = st[0], st[2]
            sums = row_sums(kb, False)
            for hh in range(2):
                carries[hh, kb] = cs[hh]
            cs = tuple(cs[hh] + sums[hh] for hh in range(2))
            return kb - 1, top(cs), cs

        prev = jnp.maximum(i - 1, 0)
        gate = jnp.where(i > 0, 1.0, 0.0)
        t_diag, t_prev = terms(i, True), terms(prev, False)
        c_diag = row_sums(i, True, t_diag)
        sums = row_sums(prev, False, t_prev)
        c_prev = tuple(c_diag[hh] + sums[hh] * gate for hh in range(2))
        first = lax.while_loop(live, record, (i - 2, top(c_prev), c_prev))[0] + 1

        def block(kb, cs, gpres, dqs, masked, gate=None, pre=None):
            rows = pl.ds(pl.multiple_of(kb * tq, tq), tq)
            ks, vs = k_ref[rows, :], v_ref[rows, :]
            pre = terms(kb, masked) if pre is None else pre
            new_g, new_dq = [], []
            dk_add, dv_add = None, None
            for hh in range(2):
                lb, lm = pre[hh]
                a = jnp.exp(lb + _split_dot(lm, tri_rev) + cs[hh])
                if masked:
                    a = jnp.where(causal, a, 0.0)
                if gate is not None:
                    a = a * gate
                g = a * _dot_nt(doms[hh], vs)
                gsum = gpres[hh] + _split_dot(g, tri_excl)
                dz = g - (g + gsum) * jnp.exp(lb)
                if masked:
                    dz = jnp.where(causal, dz, 0.0)
                if gate is not None:
                    dz = dz * gate
                dzb = dz.astype(MXU)
                new_dq.append(dqs[hh] + _dot(dzb, ks))
                dk_h = _dot_tn(dzb, qms[hh])
                dv_h = _dot_tn(a.astype(MXU), doms[hh])
                dk_add = dk_h if dk_add is None else dk_add + dk_h
                dv_add = dv_h if dv_add is None else dv_add + dv_h
                new_g.append(gpres[hh] + jnp.sum(g, axis=1, keepdims=True))
            dk_acc[rows, :] += dk_add
            dv_acc[rows, :] += dv_add
            return tuple(new_g), tuple(new_dq)

        zc, za = jnp.zeros((tq, 1), F32), jnp.zeros((tq, LANES), F32)
        gpres, dqs = lax.fori_loop(
            first, i - 1, lambda kb, cr: block(kb, (carries[0, kb], carries[1, kb]), cr[0], cr[1], False),
            ((zc, zc), (za, za)))
        gpres, dqs = block(prev, c_diag, gpres, dqs, False, gate, t_prev)
        dqs = block(i, (zc, zc), gpres, dqs, True, None, t_diag)[1]
        dq_ref[...] = (jnp.where(_head_lanes(0), dqs[0], dqs[1]) * SCALE).astype(dq_ref.dtype)

        @pl.when(i == nq - 1)
        def _():
            dk_ref[...] = dk_acc[...].astype(dk_ref.dtype)
            dv_ref[...] = dv_acc[...].astype(dv_ref.dtype)

    blk = pl.BlockSpec((tq, LANES), lambda p, i: (i, p))
    full = pl.BlockSpec((s, LANES), lambda p, i: (0, p))
    outs, couts = _call(
        body,
        grid=(N_PAIR, nq),
        in_specs=[blk,
                  pl.BlockSpec((s, LANES), lambda p, i: (0, N_PAIR + p)),
                  pl.BlockSpec((s, LANES), lambda p, i: (0, 2 * N_PAIR + p)),
                  blk],
        out_specs=[blk, full, full],
        out_shape=[jax.ShapeDtypeStruct((s, SB_W), MXU)] * 3,
        scratch_shapes=[pltpu.VMEM((s, LANES), F32), pltpu.VMEM((s, LANES), F32),
                        pltpu.VMEM((2, nq, tq, 1), F32)],
        args=(qkv, qkv, qkv, do), name=name, comm=comm)
    return tuple(outs) if comm is None else (tuple(outs), couts)


def _bucket_table():
    i = np.arange(BLOCK)[:, None]
    j = np.arange(2 * BLOCK)[None, :]
    d = np.maximum(BLOCK + i - j, 0)
    max_exact = N_BUCKETS // 2
    df = np.maximum(d, 1).astype(np.float32)
    large = max_exact + (np.log(df / max_exact) / math.log(MAX_DISTANCE / max_exact)
                         * (N_BUCKETS - max_exact)).astype(np.int32)
    large = np.minimum(large, N_BUCKETS - 1)
    return np.where(d < max_exact, d, large).astype(np.int32)


def _build_bias(rel_bias, buckets, name):
    def body(rb_ref, bk_ref, out_ref):
        h = pl.program_id(0)
        bk = bk_ref[...]
        acc = jnp.zeros(bk.shape, F32)
        for b in range(N_BUCKETS):
            acc = jnp.where(bk == b, rb_ref[b, h], acc)
        out_ref[...] = acc

    return pl.pallas_call(
        body,
        grid=(SW_HEADS,),
        in_specs=[pl.BlockSpec(memory_space=pltpu.SMEM),
                  pl.BlockSpec((BLOCK, 2 * BLOCK), lambda h: (0, 0))],
        out_specs=pl.BlockSpec((None, BLOCK, 2 * BLOCK), lambda h: (h, 0, 0)),
        out_shape=jax.ShapeDtypeStruct((SW_HEADS, BLOCK, 2 * BLOCK), F32),
        name=name,
    )(rel_bias, buckets)


def _bias_grad(dbias_layers, buckets, name):
    n_l = len(dbias_layers)

    def body(*refs):
        bk = refs[n_l][...]
        out_ref = refs[n_l + 1]
        db = refs[0][...]
        for r in refs[1:n_l]:
            db = db + r[...]
        lane = lax.broadcasted_iota(jnp.int32, (1, LANES), 1)
        acc = jnp.zeros((1, LANES), F32)
        for b in range(N_BUCKETS):
            part = jnp.sum(jnp.where(bk == b, db, 0.0), axis=1, keepdims=True)
            tot = jnp.sum(part, axis=0, keepdims=True)
            acc = jnp.where(lane == b, tot, acc)
        out_ref[...] = acc

    hspec = pl.BlockSpec((None, BLOCK, 2 * BLOCK), lambda h: (h, 0, 0))
    return pl.pallas_call(
        body,
        grid=(SW_HEADS,),
        in_specs=[hspec] * n_l + [pl.BlockSpec((BLOCK, 2 * BLOCK), lambda h: (0, 0))],
        out_specs=pl.BlockSpec((None, 1, LANES), lambda h: (h, 0, 0)),
        out_shape=jax.ShapeDtypeStruct((SW_HEADS, 1, LANES), F32),
        name=name,
    )(*dbias_layers, buckets)


GROUP_ROWS = SW_GROUP * BLOCK


def _group_lanes(g):
    lane = lax.broadcasted_iota(jnp.int32, (1, LANES), 1)
    gvec = jnp.zeros((1, LANES), jnp.int32) + g
    return jnp.where(lane >= HEAD_DIM, 1, 0) == gvec, gvec


def _stack_heads(x, g):
    kv_lanes, gvec = _group_lanes(g)
    parts = []
    for j in range(SW_GROUP):
        half = x[:, (j // 2) * LANES:(j // 2 + 1) * LANES]
        moved = jnp.where(gvec == j % 2, half, pltpu.roll(half, HEAD_DIM, 1))
        parts.append(jnp.where(kv_lanes, moved, 0.0))
    return jnp.concatenate(parts, axis=0)


def _unstack_heads(y, g):
    _, gvec = _group_lanes(g)
    heads = []
    for j in range(SW_GROUP):
        yj = y[j * BLOCK:(j + 1) * BLOCK]
        heads.append(jnp.where(gvec == j % 2, yj, pltpu.roll(yj, HEAD_DIM, 1)))
    pairs = [jnp.where(_head_lanes(0), heads[2 * p], heads[2 * p + 1]) for p in range(SW_GROUP // 2)]
    return jnp.concatenate(pairs, axis=1)


def _per_head_col(values):
    return jnp.concatenate([jnp.zeros((BLOCK, 1), F32) + v for v in values], axis=0)


def _swa_scores(qs, kp, kc, bias_ref, n):
    row = jnp.bitwise_and(lax.broadcasted_iota(jnp.int32, (GROUP_ROWS, BLOCK), 0), BLOCK - 1)
    col = lax.broadcasted_iota(jnp.int32, (GROUP_ROWS, BLOCK), 1)
    bias = bias_ref[...].reshape(GROUP_ROWS, 2 * BLOCK)
    s1 = _dot_nt(qs, kp) + bias[:, :BLOCK]
    s2 = _dot_nt(qs, kc) + bias[:, BLOCK:]
    no_prev = jnp.where(n > 0, 0, BLOCK)
    s1 = jnp.where(col > row + no_prev, s1, NEG)
    s2 = jnp.where(col <= row, s2, NEG)
    return s1, s2


def _swa_specs(s):
    q_blk = 3 * SB_W // (2 * LANES)
    k_blk = (3 * SB_W + SW_W) // LANES
    return (pl.BlockSpec((s, 2 * LANES), lambda g: (0, q_blk + g)),
            pl.BlockSpec((s, LANES), lambda g: (0, k_blk)),
            pl.BlockSpec((s, LANES), lambda g: (0, k_blk + 1)))


def _swa_fwd(qkv, bias, sinks, name, comm=None):
    s = qkv.shape[0]
    nb = s // BLOCK

    def body(sink_ref, q_ref, k_ref, v_ref, bias_ref, o_ref, lse_ref):
        g = pl.program_id(0)
        sink = _per_head_col([sink_ref[SW_GROUP * g + j] for j in range(SW_GROUP)])
        lane = lax.broadcasted_iota(jnp.int32, (1, LANES), 1)

        def step(n, carry):
            r0 = pl.multiple_of(n * BLOCK, BLOCK)
            p0 = pl.multiple_of(jnp.maximum(n - 1, 0) * BLOCK, BLOCK)
            cur, prev = pl.ds(r0, BLOCK), pl.ds(p0, BLOCK)
            qs = _stack_heads(q_ref[cur, :].astype(F32) * SCALE, g).astype(MXU)
            s1, s2 = _swa_scores(qs, k_ref[prev, :], k_ref[cur, :], bias_ref, n)
            m = jnp.maximum(jnp.max(jnp.maximum(s1, s2), axis=1, keepdims=True), sink)
            e1 = jnp.exp(s1 - m)
            e2 = jnp.exp(s2 - m)
            den = jnp.sum(e1 + e2, axis=1, keepdims=True) + jnp.exp(sink - m)
            o = _dot((e1 / den).astype(MXU), v_ref[prev, :]) + _dot((e2 / den).astype(MXU), v_ref[cur, :])
            o_ref[cur, :] = _unstack_heads(o, g).astype(o_ref.dtype)
            lse = m + jnp.log(den)
            lse_row = jnp.zeros((BLOCK, LANES), F32)
            for j in range(SW_GROUP):
                lse_row = jnp.where(lane == j, lse[j * BLOCK:(j + 1) * BLOCK], lse_row)
            lse_ref[cur, :] = lse_row
            return carry

        lax.fori_loop(0, nb, step, 0, unroll=2)

    outs, couts = _call(
        body,
        grid=(SW_KV,),
        in_specs=[pl.BlockSpec(memory_space=pltpu.SMEM), *_swa_specs(s),
                  pl.BlockSpec((SW_GROUP, BLOCK, 2 * BLOCK), lambda g: (g, 0, 0))],
        out_specs=[pl.BlockSpec((s, 2 * LANES), lambda g: (0, g)),
                   pl.BlockSpec((None, s, LANES), lambda g: (g, 0, 0))],
        out_shape=[jax.ShapeDtypeStruct((s, SW_W), MXU), jax.ShapeDtypeStruct((SW_KV, s, LANES), F32)],
        args=(sinks, qkv, qkv, qkv, bias), name=name, comm=comm)
    return tuple(outs) if comm is None else (tuple(outs), couts)


def _swa_bwd(qkv, bias, sinks, do, lse, name, comm=None):
    s = qkv.shape[0]
    nb = s // BLOCK

    def body(sink_ref, q_ref, k_ref, v_ref, bias_ref, do_ref, lse_ref,
             dq_ref, dk_ref, dv_ref, dbias_ref, dsink_ref, dk_acc, dv_acc):
        g = pl.program_id(0)
        sink = _per_head_col([sink_ref[SW_GROUP * g + j] for j in range(SW_GROUP)])
        lane = lax.broadcasted_iota(jnp.int32, (1, LANES), 1)

        @pl.when(g == 0)
        def _():
            dk_acc[...] = jnp.zeros_like(dk_acc)
            dv_acc[...] = jnp.zeros_like(dv_acc)

        dbias_ref[...] = jnp.zeros_like(dbias_ref)

        def step(n, dsink_rows):
            r0 = pl.multiple_of(n * BLOCK, BLOCK)
            p0 = pl.multiple_of(jnp.maximum(n - 1, 0) * BLOCK, BLOCK)
            cur, prev = pl.ds(r0, BLOCK), pl.ds(p0, BLOCK)
            qs = _stack_heads(q_ref[cur, :].astype(F32) * SCALE, g).astype(MXU)
            dos = _stack_heads(do_ref[cur, :].astype(F32), g).astype(MXU)
            kp, kc, vp, vc = k_ref[prev, :], k_ref[cur, :], v_ref[prev, :], v_ref[cur, :]
            lse_row = lse_ref[cur, :]
            lse = jnp.concatenate([jnp.sum(jnp.where(lane == j, lse_row, 0.0), axis=1, keepdims=True)
                                   for j in range(SW_GROUP)], axis=0)
            s1, s2 = _swa_scores(qs, kp, kc, bias_ref, n)
            pr1 = jnp.exp(s1 - lse)
            pr2 = jnp.exp(s2 - lse)
            dpr1 = _dot_nt(dos, vp)
            dpr2 = _dot_nt(dos, vc)
            delta = jnp.sum(pr1 * dpr1 + pr2 * dpr2, axis=1, keepdims=True)
            ds1 = pr1 * (dpr1 - delta)
            ds2 = pr2 * (dpr2 - delta)
            dbias_ref[:, :, :BLOCK] += ds1.reshape(SW_GROUP, BLOCK, BLOCK)
            dbias_ref[:, :, BLOCK:] += ds2.reshape(SW_GROUP, BLOCK, BLOCK)
            ds1b, ds2b = ds1.astype(MXU), ds2.astype(MXU)
            dq = _dot(ds1b, kp) + _dot(ds2b, kc)
            dq_ref[cur, :] = (_unstack_heads(dq, g) * SCALE).astype(dq_ref.dtype)
            dk_acc[prev, :] += _dot_tn(ds1b, qs)
            dk_acc[cur, :] += _dot_tn(ds2b, qs)
            dv_acc[prev, :] += _dot_tn(pr1.astype(MXU), dos)
            dv_acc[cur, :] += _dot_tn(pr2.astype(MXU), dos)
            return dsink_rows - jnp.exp(sink - lse) * delta

        rows = lax.fori_loop(0, nb, step, jnp.zeros((GROUP_ROWS, 1), F32), unroll=2)
        for j in range(SW_GROUP):
            dsink_ref[j] = jnp.broadcast_to(jnp.sum(rows[j * BLOCK:(j + 1) * BLOCK], axis=0, keepdims=True),
                                            (1, LANES))

        @pl.when(g == SW_KV - 1)
        def _():
            dk_ref[...] = dk_acc[...].astype(dk_ref.dtype)
            dv_ref[...] = dv_acc[...].astype(dv_ref.dtype)

    grp = pl.BlockSpec((s, 2 * LANES), lambda g: (0, g))
    kv_out = pl.BlockSpec((s, LANES), lambda g: (0, 0))
    bspec = pl.BlockSpec((SW_GROUP, BLOCK, 2 * BLOCK), lambda g: (g, 0, 0))
    outs, couts = _call(
        body,
        grid=(SW_KV,),
        in_specs=[pl.BlockSpec(memory_space=pltpu.SMEM), *_swa_specs(s), bspec, grp,
                  pl.BlockSpec((None, s, LANES), lambda g: (g, 0, 0))],
        out_specs=[grp, kv_out, kv_out, bspec, pl.BlockSpec((SW_GROUP, 1, LANES), lambda g: (g, 0, 0))],
        out_shape=[jax.ShapeDtypeStruct((s, SW_W), MXU),
                   jax.ShapeDtypeStruct((s, LANES), MXU),
                   jax.ShapeDtypeStruct((s, LANES), MXU),
                   jax.ShapeDtypeStruct((SW_HEADS, BLOCK, 2 * BLOCK), F32),
                   jax.ShapeDtypeStruct((SW_HEADS, 1, LANES), F32)],
        scratch_shapes=[pltpu.VMEM((s, LANES), F32), pltpu.VMEM((s, LANES), F32)],
        args=(sinks, qkv, qkv, qkv, bias, do, lse), name=name, comm=comm)
    return tuple(outs) if comm is None else (tuple(outs), couts)


class _NoPlan:
    def comm(self, name):
        return None

    def done(self, name, outs):
        pass

    def grad(self, layer, name, value):
        pass

    def layer_done(self, layer):
        pass


def _run(plan, fn, *args, name, **kw):
    comm = plan.comm(name)
    if comm is None:
        return fn(*args, name=name, **kw)
    res, outs = fn(*args, name=name, comm=comm, **kw)
    plan.done(name, outs)
    return res


def _residual_norm(acc, res, g):
    x = res + acc
    r = lax.rsqrt(jnp.mean(x * x, axis=-1, keepdims=True) + EPS)
    return x, (x * r) * g, r


def _layer_fwd(x, p, w, g_mix, g_mlp, g_pe, sinks, bias, tag, plan):
    h1, r1 = _run(plan, _rms_fwd, x, g_mix, name=f"rms_mix_{tag}")
    qkv = _run(plan, _mm, h1, w["w_qkv_t"], tb=True, out_dtypes=(MXU,), name=f"proj_qkv_{tag}")
    gates = _run(plan, _mm, h1, w["w_gate_t"], tb=True, out_dtypes=(MXU,), name=f"proj_gate_{tag}")
    oa = _run(plan, _sb_fwd, qkv, name=f"sb_fwd_{tag}")
    ob, lse = _run(plan, _swa_fwd, qkv, bias, sinks, name=f"swa_fwd_{tag}")
    merged = _mix_fwd(oa, ob, w["w_up_a_t"], w["w_up_b_t"], gates, f"mix_fwd_{tag}")
    d = x.shape[1]
    normed = (F32, MXU, (F32, "col"))
    x1, h2, r2 = _run(plan, _mm, merged, w["w_o"], extras=(x, g_mlp), epi=_residual_norm, out_dtypes=normed,
                      tn=d, name=f"out_proj_{tag}")
    u, act = _run(plan, _mm, h2, w["w_ff1_t"], tb=True,
                  epi=lambda acc: (acc, jnp.square(jnp.maximum(acc, 0.0))),
                  out_dtypes=(MXU, MXU), name=f"ff1_{tag}")
    x2, h3, r3 = _run(plan, _mm, act, w["w_ff2"], extras=(x1, g_pe), epi=_residual_norm, out_dtypes=normed,
                      tn=d, name=f"ff2_{tag}")
    x3 = _ple(p, w["w_pe_t"], h3, w["w_pg"], x2, backward=False, name=f"ple_fwd_{tag}")
    saved = dict(x=x, h1=h1, r1=r1, gates=gates, qkv=qkv, lse=lse, oa=oa, ob=ob, merged=merged,
                 x1=x1, h2=h2, r2=r2, u=u, act=act, x2=x2, h3=h3, r3=r3)
    return x3, saved


def _layer_bwd(dx3, sv, p, w, g_mix, g_mlp, g_pe, sinks, bias, layer, plan):
    tag = f"l{layer}"
    gw = {}
    wire = (WIRE,)

    def dw(name, a, b):
        gw[name] = _run(plan, _mm, a, b, ta=True, out_dtypes=wire, name=f"d{name}_{tag}")
        plan.grad(layer, name, gw[name])

    dpe, dgt = _ple(p, w["w_pe_t"], sv["h3"], w["w_pg"], dx3, backward=True, name=f"ple_bwd_{tag}")
    dw("w_pe", dpe, p)
    dw("w_pg", sv["h3"], dgt)
    dh3 = _run(plan, _mm, dgt, w["w_pg"], tb=True, name=f"dh_pe_{tag}")
    dx2, dx2b, dg_pe = _rms_bwd(sv["x2"], sv["r3"], g_pe, dh3, dx3, f"rms_pe_bwd_{tag}")
    dw("w_ff2", sv["act"], dx2b)
    du = _run(plan, _mm, dx2b, w["w_ff2"], tb=True, extras=(sv["u"],),
              epi=lambda acc, u: acc * (2.0 * jnp.maximum(u.astype(F32), 0.0)), out_dtypes=(MXU,),
              name=f"dact_{tag}")
    dw("w_ff1", du, sv["h2"])
    dh2 = _run(plan, _mm, du, w["w_ff1_t"], name=f"dh_mlp_{tag}")
    dx1, dx1b, dg_mlp = _rms_bwd(sv["x1"], sv["r2"], g_mlp, dh2, dx2, f"rms_mlp_bwd_{tag}")
    dw("w_o", sv["merged"], dx1b)
    dya, dyb, dga, dgb = _mix_bwd(dx1b, w["w_o"], sv["oa"], sv["ob"], w["w_up_a_t"], w["w_up_b_t"],
                                  sv["gates"], f"mix_bwd_{tag}")
    dw("w_up_a", dya, sv["oa"])
    dw("w_up_b", dyb, sv["ob"])
    doa = _run(plan, _mm, dya, w["w_up_a_t"], out_dtypes=(MXU,), name=f"do_a_{tag}")
    dob = _run(plan, _mm, dyb, w["w_up_b_t"], out_dtypes=(MXU,), name=f"do_b_{tag}")
    dqb, dkb, dvb, dbias, dsink = _run(plan, _swa_bwd, sv["qkv"], bias, sinks, dob, sv["lse"],
                                       name=f"swa_bwd_{tag}")
    dqa, dka, dva = _run(plan, _sb_bwd, sv["qkv"], doa, name=f"sb_bwd_{tag}")
    dqkv = jnp.concatenate([dqa, dka, dva, dqb, dkb, dvb], axis=1)
    gw_qkv = _mm(dqkv, sv["h1"], ta=True, out_dtypes=wire, name=f"dw_qkv_{tag}")
    gw_ga = _mm(dga, sv["h1"], ta=True, out_dtypes=wire, name=f"dw_ga_{tag}")
    gw_gb = _mm(dgb, sv["h1"], ta=True, out_dtypes=wire, name=f"dw_gb_{tag}")
    gw["w_in"] = jnp.concatenate([gw_qkv, gw_ga, gw_gb], axis=0)
    plan.grad(layer, "w_in", gw["w_in"])
    d = dga.shape[1]
    add = lambda acc, res: res + acc
    dh1 = _run(plan, _mm, dga, w["w_gate_t"][:d], name=f"dh_ga_{tag}")
    dh1 = _run(plan, _mm, dgb, w["w_gate_t"][d:], extras=(dh1,), epi=add, name=f"dh_gb_{tag}")
    dh1 = _run(plan, _mm, dqkv, w["w_qkv_t"], tk=768, extras=(dh1,), epi=add, name=f"dh_qkv_{tag}")
    dx, _, dg_mix = _rms_bwd(sv["x"], sv["r1"], g_mix, dh1, dx1, f"rms_mix_bwd_{tag}")
    small = dict(g_mix=dg_mix, g_mlp=dg_mlp, g_pe=dg_pe, sinks=dsink[:, 0, 0], dbias=dbias)
    return dx, gw, small


def _local_step(x, p, target, weights, g_mix, g_mlp, g_pe, g_final, sinks, rel_bias, plan=None):
    plan = _NoPlan() if plan is None else plan
    depth = g_mix.shape[0]
    buckets = jnp.asarray(_bucket_table())
    bias = _build_bias(rel_bias, buckets, "build_bias")
    saved, wfull = [], []
    h = x
    for l in range(depth):
        wfull.append(weights(l))
        h, sv = _layer_fwd(h, p[l], wfull[l], g_mix[l:l + 1], g_mlp[l:l + 1], g_pe[l:l + 1],
                           sinks[l], bias, f"l{l}", plan)
        saved.append(sv)
    loss_row, dx, dg_final = _loss_head(h, g_final[None, :], target, "loss_head")
    gws = [None] * depth
    smalls = [None] * depth
    for l in reversed(range(depth)):
        dx, gws[l], smalls[l] = _layer_bwd(dx, saved[l], p[l], wfull[l], g_mix[l:l + 1], g_mlp[l:l + 1],
                                           g_pe[l:l + 1], sinks[l], bias, l, plan)
        plan.layer_done(l)
    drel = _bias_grad([sm["dbias"] for sm in smalls], buckets, "bias_grad")[:, 0, :N_BUCKETS].T
    small = dict(
        g_mix=jnp.concatenate([sm["g_mix"] for sm in smalls], axis=0),
        g_mlp=jnp.concatenate([sm["g_mlp"] for sm in smalls], axis=0),
        g_pe=jnp.concatenate([sm["g_pe"] for sm in smalls], axis=0),
        g_final=dg_final[0],
        sinks=jnp.stack([sm["sinks"] for sm in smalls], axis=0),
        rel_bias=drel,
    )
    return loss_row, dx, gws, small


MESH_ID = pl.DeviceIdType.MESH
ANY = pl.BlockSpec(memory_space=pl.ANY)


def _position():
    return lax.axis_index("x"), lax.axis_index("y"), lax.axis_index("c")


def _run_comm(comm, name):
    ci, co = len(comm.inputs), len(comm.out_shapes)

    def body(*refs):
        cin, cout, csem = refs[:ci], refs[ci:ci + co], refs[ci + co:]
        pos = _position()
        comm.start(pos, cin, cout, csem)
        comm.finish(pos, cin, cout, csem)

    return pl.pallas_call(body, out_shape=comm.out_shapes, in_specs=[ANY] * ci, out_specs=[ANY] * co,
                          scratch_shapes=comm.sems, name=name)(*comm.inputs)


def _gather_comm(shards):
    n = len(shards)

    def copies(pos, x_refs, out_refs, sems):
        send_sems, recv_sems, local_sems = sems
        x, y, c = pos
        me, sibling = (x, y, c), (x, y, 1 - c)
        chips = [(1 - x, y), (x, 1 - y), (1 - x, 1 - y)]

        def slot(a, px, py, pc):
            return out_refs[a].at[4 * px + 2 * py + pc]

        def copy(a, k, block, to, src=None):
            return pltpu.make_async_remote_copy(
                src_ref=slot(a, *block) if src is None else src, dst_ref=slot(a, *block),
                send_sem=send_sems.at[a, k], recv_sem=recv_sems.at[a, k],
                device_id=to, device_id_type=MESH_ID)

        mine = [pltpu.make_async_copy(x_refs[a], slot(a, *me), local_sems.at[a]) for a in range(n)]
        first = []
        for a in range(n):
            first.append(copy(a, 0, me, sibling, src=x_refs[a]))
            first += [copy(a, 1 + j, me, (*chip, c), src=x_refs[a]) for j, chip in enumerate(chips)]
        return me, sibling, chips, copy, mine, first

    def start(pos, x_refs, out_refs, sems):
        _, _, _, _, mine, first = copies(pos, x_refs, out_refs, sems)
        for cp in mine + first:
            cp.start()

    def finish(pos, x_refs, out_refs, sems):
        me, sibling, chips, copy, mine, first = copies(pos, x_refs, out_refs, sems)
        c = pos[2]
        passed = []
        for j, chip in enumerate(chips):
            for a in range(n):
                copy(a, 1 + j, (*chip, c), me).wait_recv()
                fwd = copy(a, 4 + j, (*chip, c), sibling)
                fwd.start()
                passed.append(fwd)
        for a in range(n):
            copy(a, 0, sibling, me).wait_recv()
            for j, chip in enumerate(chips):
                copy(a, 4 + j, (*chip, 1 - c), me).wait_recv()
        for cp in first + passed:
            cp.wait_send()
        for cp in mine:
            cp.wait()

    return _Comm(shards, [jax.ShapeDtypeStruct((N_DEV,) + s.shape, s.dtype) for s in shards],
                 [pltpu.SemaphoreType.DMA((n, 7)), pltpu.SemaphoreType.DMA((n, 7)),
                  pltpu.SemaphoreType.DMA((n,))], start, finish)


def _exchange_comm(arrays, n_slots, route):
    n = len(arrays)

    def copies(pos, in_refs, out_refs, sems):
        send_sems, recv_sems = sems
        out = []
        for a in range(n):
            for j in range(n_slots):
                src_slot, peer = route(pos, j)
                out.append(pltpu.make_async_remote_copy(
                    src_ref=in_refs[a].at[src_slot], dst_ref=out_refs[a].at[j],
                    send_sem=send_sems.at[a, j], recv_sem=recv_sems.at[a, j],
                    device_id=peer, device_id_type=MESH_ID))
        return out

    def start(pos, in_refs, out_refs, sems):
        for cp in copies(pos, in_refs, out_refs, sems):
            cp.start()

    def finish(pos, in_refs, out_refs, sems):
        for cp in copies(pos, in_refs, out_refs, sems):
            cp.wait()

    return _Comm(arrays, [jax.ShapeDtypeStruct((n_slots,) + g.shape[1:], g.dtype) for g in arrays],
                 [pltpu.SemaphoreType.DMA((n, n_slots)), pltpu.SemaphoreType.DMA((n, n_slots))], start, finish)


def _rs_sibling_comm(gs):
    return _exchange_comm(gs, 4, lambda pos, j: (2 * j + (1 - pos[2]), (pos[0], pos[1], 1 - pos[2])))


def _chip_of(k, x, y):
    return x ^ ((k + 1) & 1), y ^ (((k + 1) >> 1) & 1)


def _chip_partials(pos, gs, recvs, name):
    n = len(gs)

    def body(pos_ref, *refs):
        for a in range(n):
            refs[2 * n + a][...] = (refs[a][...].astype(F32) + refs[n + a][...].astype(F32)
                                    ).astype(refs[2 * n + a].dtype)

    def g_map(k, pos_ref):
        cx, cy = _chip_of(k, pos_ref[0], pos_ref[1])
        return (4 * cx + 2 * cy + pos_ref[2], 0, 0)

    def r_map(k, pos_ref):
        cx, cy = _chip_of(k, pos_ref[0], pos_ref[1])
        return (2 * cx + cy, 0, 0)

    slab = [(None,) + g.shape[1:] for g in gs]
    return pl.pallas_call(
        body,
        grid_spec=pltpu.PrefetchScalarGridSpec(
            num_scalar_prefetch=1,
            grid=(4,),
            in_specs=[pl.BlockSpec(sh, g_map) for sh in slab] + [pl.BlockSpec(sh, r_map) for sh in slab],
            out_specs=[pl.BlockSpec(sh, lambda k, pos_ref: (k, 0, 0)) for sh in slab],
        ),
        out_shape=[jax.ShapeDtypeStruct((4,) + g.shape[1:], g.dtype) for g in gs],
        compiler_params=_cparams(),
        name=name,
    )(pos, *gs, *recvs)


def _rs_chips_comm(parts):
    return _exchange_comm(parts, 3, lambda pos, k: (k, (*_chip_of(k, pos[0], pos[1]), pos[2])))


def _adamw_math(w, g, m, v):
    m = ADAM_B1 * m + (1.0 - ADAM_B1) * g
    v = ADAM_B2 * v + (1.0 - ADAM_B2) * (g * g)
    m_hat = m / (1.0 - ADAM_B1 ** ADAM_STEP)
    v_hat = v / (1.0 - ADAM_B2 ** ADAM_STEP)
    delta = -ADAM_LR * (m_hat / (jnp.sqrt(v_hat) + ADAM_EPS) + ADAM_WD * w)
    return delta, m, v


def _adamw_weight(parts, recvs, w, m, v, name, grad_t=False):
    depth, a, b = w.shape
    ta = _tile(a, 288, unit=LANES if grad_t else 16)
    ni = a // ta
    g_block = (b, ta) if grad_t else (ta, b)

    def body(*refs):
        p_refs, r_refs = refs[:depth], refs[depth:2 * depth]
        w_ref, m_ref, v_ref = refs[2 * depth:2 * depth + 3]
        g_out, d_out, m_out, v_out = refs[2 * depth + 3:]
        layer = pl.program_id(0)
        g = jnp.zeros(g_block, F32)
        for l in range(depth):
            gl = p_refs[l][...].astype(F32)
            for k in range(3):
                gl = gl + r_refs[l][k].astype(F32)
            g = jnp.where(layer == l, gl, g)
        if grad_t:
            g = g.T
        delta, m_new, v_new = _adamw_math(w_ref[...], g, m_ref[...], v_ref[...])
        g_out[...] = g
        d_out[...] = delta
        m_out[...] = m_new
        v_out[...] = v_new

    def hold(l):
        return lambda layer, i: jnp.where(layer == l, i, jnp.where(layer < l, 0, ni - 1))

    def g_index(slot, f):
        if grad_t:
            return lambda layer, i: (slot, 0, f(layer, i))
        return lambda layer, i: (slot, f(layer, i), 0)

    p_specs = [pl.BlockSpec((None,) + g_block, g_index(3, hold(l))) for l in range(depth)]
    r_specs = [pl.BlockSpec((3,) + g_block, g_index(0, hold(l))) for l in range(depth)]
    row = pl.BlockSpec((None, ta, b), lambda layer, i: (layer, i, 0))
    return pl.pallas_call(
        body,
        grid=(depth, ni),
        in_specs=p_specs + r_specs + [row, row, row],
        out_specs=[row] * 4,
        out_shape=[jax.ShapeDtypeStruct(w.shape, F32)] * 4,
        compiler_params=_cparams(),
        name=name,
    )(*parts, *recvs, w, m, v)


def _adamw_replicated(gathered, w, m, v, name):
    r, lanes = w.shape

    def body(g_ref, w_ref, m_ref, v_ref, g_out, d_out, m_out, v_out):
        g = g_ref[0]
        for k in range(1, N_DEV):
            g = g + g_ref[k]
        delta, m_new, v_new = _adamw_math(w_ref[...], g, m_ref[...], v_ref[...])
        g_out[...] = g
        d_out[...] = delta
        m_out[...] = m_new
        v_out[...] = v_new

    return pl.pallas_call(
        body,
        out_shape=[jax.ShapeDtypeStruct((r, lanes), F32)] * 4,
        name=name,
    )(gathered, w, m, v)


def _wire_shard(name, shard):
    return (shard.T if name in COL_SHARDED else shard).astype(WIRE)


def _full_weight(gathered):
    return gathered.reshape(N_DEV * gathered.shape[1], gathered.shape[2])


def _to_slabs(gfull):
    return gfull.reshape(N_DEV, gfull.shape[0] // N_DEV, gfull.shape[1])


def _pack_small(arrs):
    rows = []
    for a in arrs:
        flat = a.astype(F32).reshape(-1)
        pad = (-flat.shape[0]) % LANES
        rows.append(jnp.pad(flat, (0, pad)).reshape(-1, LANES))
    packed = jnp.concatenate(rows, axis=0)
    return jnp.pad(packed, ((0, (-packed.shape[0]) % 8), (0, 0)))


def _unpack_small(packed, shapes):
    out, off = [], 0
    for shp in shapes:
        n = math.prod(shp)
        rows = -(-n // LANES)
        out.append(packed[off:off + rows].reshape(-1)[:n].reshape(shp))
        off += rows
    return out


def _of(layer, *names):
    return tuple((layer, n) for n in names)


MIXER_W = ("w_in", "w_up_a", "w_up_b", "w_o")
MLP_W = ("w_ff1", "w_ff2", "w_pe", "w_pg")

GATHERS = (
    ("rms_mix_l0", _of(0, "w_in")),
    ("proj_qkv_l0", _of(0, "w_up_a", "w_up_b", "w_o")),
    ("proj_gate_l0", _of(0, "w_pe", "w_pg")),
    ("sb_fwd_l0", _of(0, "w_ff1", "w_ff2")),
    ("swa_fwd_l0", _of(1, "w_in")),
    ("sb_fwd_l1", _of(1, "w_up_a", "w_up_b", "w_o", "w_pe", "w_pg", "w_ff1")),
    ("swa_fwd_l1", _of(1, "w_ff2")),
)
REDUCES = (
    (_of(1, "w_ff1"), "dw_ff2_l0", "dact_l0"),
    (_of(1, "w_ff2"), "dw_ff2_l0", "dw_ff1_l0"),
    (_of(1, "w_in"), "dw_ff2_l0", "sb_bwd_l0"),
    (_of(1, "w_up_a", "w_up_b", "w_o", "w_pe", "w_pg"), "dw_ff2_l0", "swa_bwd_l0"),
    (_of(0, *MLP_W), "dh_mlp_l0", "sb_bwd_l0"),
    (_of(0, "w_o", "w_up_a", "w_up_b"), "do_a_l0", "sb_bwd_l0"),
    (_of(0, "w_in"), "dh_ga_l0", "dh_qkv_l0"),
)


def _merge_comms(comms):
    if len(comms) == 1:
        return comms[0]

    def cuts(counts):
        edges = [0]
        for c in counts:
            edges.append(edges[-1] + c)
        return [slice(a, b) for a, b in zip(edges[:-1], edges[1:])]

    s_in = cuts([len(c.inputs) for c in comms])
    s_out = cuts([len(c.out_shapes) for c in comms])
    s_sem = cuts([len(c.sems) for c in comms])

    def start(pos, cin, cout, csem):
        for c, i, o, s in zip(comms, s_in, s_out, s_sem):
            c.start(pos, cin[i], cout[o], csem[s])

    def finish(pos, cin, cout, csem):
        for c, i, o, s in zip(comms, s_in, s_out, s_sem):
            c.finish(pos, cin[i], cout[o], csem[s])

    return _Comm(sum([c.inputs for c in comms], []), sum([c.out_shapes for c in comms], []),
                 sum([c.sems for c in comms], []), start, finish)


class _LayerWeights:
    def __init__(self, full, layer):
        self.full, self.layer, self.cache = full, layer, {}

    def __getitem__(self, name):
        if name not in self.cache:
            if name == "w_qkv_t":
                self.cache[name] = self.full[(self.layer, "w_in")][:QKV_COLS]
            elif name == "w_gate_t":
                self.cache[name] = self.full[(self.layer, "w_in")][QKV_COLS:]
            else:
                base = name[:-2] if name.endswith("_t") else name
                assert (base in COL_SHARDED) == name.endswith("_t"), name
                self.cache[name] = self.full[(self.layer, base)]
        return self.cache[name]


class _Plan:
    def __init__(self, w_sh, pos):
        self.w_sh = dict(zip(WEIGHTS, w_sh))
        self.pos = pos
        self.full, self.gw, self.parts, self.recv = {}, {}, {}, {}
        self.slabs = {}
        self.hosted = {}
        for i, (host, _) in enumerate(GATHERS):
            if host is not None:
                self.hosted.setdefault(host, []).append(("gather", i))
        for i, (_, sib_host, chip_host) in enumerate(REDUCES):
            assert (sib_host is None) == (chip_host is None)
            if sib_host is not None:
                self.hosted.setdefault(sib_host, []).append(("sibling", i))
                self.hosted.setdefault(chip_host, []).append(("chips", i))

    def _gather(self, i):
        return _gather_comm([_wire_shard(n, self.w_sh[n][layer]) for layer, n in GATHERS[i][1]])

    def _gathered(self, i, outs):
        for (layer, n), g in zip(GATHERS[i][1], outs):
            self.full[(layer, n)] = _full_weight(g)

    def weights(self, layer):
        for i, (host, items) in enumerate(GATHERS):
            if host is None and items[0][0] == layer:
                self._gathered(i, _run_comm(self._gather(i), f"gather_{i}"))
        return _LayerWeights(self.full, layer)

    def grad(self, layer, name, value):
        self.gw[(layer, name)] = value

    def _sibling(self, i):
        self.slabs[i] = [_to_slabs(self.gw[item]) for item in REDUCES[i][0]]
        return _rs_sibling_comm(self.slabs[i])

    def _sibling_done(self, i, outs):
        parts = _chip_partials(self.pos, self.slabs[i], outs, f"chip_partials_{i}")
        for item, part in zip(REDUCES[i][0], parts):
            self.parts[item] = part

    def _chips(self, i):
        return _rs_chips_comm([self.parts[item] for item in REDUCES[i][0]])

    def _chips_done(self, i, outs):
        for item, r in zip(REDUCES[i][0], outs):
            self.recv[item] = r

    def layer_done(self, layer):
        for i, (items, sib_host, _) in enumerate(REDUCES):
            if sib_host is None and items[0][0] == layer:
                self._sibling_done(i, _run_comm(self._sibling(i), f"reduce_sibling_{i}"))
                self._chips_done(i, _run_comm(self._chips(i), f"reduce_chips_{i}"))

    def comm(self, name):
        if name not in self.hosted:
            return None
        make = {"gather": self._gather, "sibling": self._sibling, "chips": self._chips}
        return _merge_comms([make[kind](i) for kind, i in self.hosted[name]])

    def done(self, name, outs):
        took = {"gather": self._gathered, "sibling": self._sibling_done, "chips": self._chips_done}
        off = 0
        for kind, i in self.hosted[name]:
            n = len(GATHERS[i][1]) if kind == "gather" else len(REDUCES[i][0])
            took[kind](i, outs[off:off + n])
            off += n


def kernel(x, p, w_in, w_up_a, w_up_b, w_o, w_ff1, w_ff2, w_pe, w_pg, g_mix, g_mlp, g_pe, g_final, sinks, rel_bias, loss_target, m_w_in, m_w_up_a, m_w_up_b, m_w_o, m_w_ff1, m_w_ff2, m_w_pe, m_w_pg, m_g_mix, m_g_mlp, m_g_pe, m_g_final, m_sinks, m_rel_bias, v_w_in, v_w_up_a, v_w_up_b, v_w_o, v_w_ff1, v_w_ff2, v_w_pe, v_w_pg, v_g_mix, v_g_mlp, v_g_pe, v_g_final, v_sinks, v_rel_bias):
    w_sh = [w_in, w_up_a, w_up_b, w_o, w_ff1, w_ff2, w_pe, w_pg]
    m_sh = [m_w_in, m_w_up_a, m_w_up_b, m_w_o, m_w_ff1, m_w_ff2, m_w_pe, m_w_pg]
    v_sh = [v_w_in, v_w_up_a, v_w_up_b, v_w_o, v_w_ff1, v_w_ff2, v_w_pe, v_w_pg]
    depth = w_in.shape[0]
    assert depth == 2 and x.shape[-1] * 2 + QKV_COLS == w_in.shape[2] * N_DEV

    px, py, pc = _position()
    plan = _Plan(w_sh, jnp.stack([px, py, pc]).astype(jnp.int32))
    loss_row, grad_x, _, small = _local_step(
        x[0], p[:, 0], loss_target[0], plan.weights, g_mix, g_mlp, g_pe, g_final, sinks, rel_bias, plan=plan)

    grad_w, delta_w, new_m, new_v = [], [], [], []
    for a, name in enumerate(WEIGHTS):
        parts = [plan.parts[(l, name)] for l in range(depth)]
        recvs = [plan.recv[(l, name)] for l in range(depth)]
        if name == "w_in":
            flip = lambda t: t.transpose(0, 2, 1)
            outs = [flip(o) for o in _adamw_weight(parts, recvs, flip(w_sh[a]), flip(m_sh[a]), flip(v_sh[a]),
                                                   f"adamw_{name}")]
        else:
            outs = _adamw_weight(parts, recvs, w_sh[a], m_sh[a], v_sh[a], f"adamw_{name}",
                                 grad_t=name in COL_SHARDED)
        for lst, o in zip((grad_w, delta_w, new_m, new_v), outs):
            lst.append(o)

    small_w = [g_mix, g_mlp, g_pe, g_final, sinks, rel_bias]
    small_m = [m_g_mix, m_g_mlp, m_g_pe, m_g_final, m_sinks, m_rel_bias]
    small_v = [v_g_mix, v_g_mlp, v_g_pe, v_g_final, v_sinks, v_rel_bias]
    small_shapes = [a.shape for a in small_w] + [(1,)]
    zero = jnp.zeros((1,), F32)
    small_g = _pack_small([small[n] for n in SMALL] + [loss_row[0, :1]])
    small_all = _run_comm(_gather_comm([small_g]), "gather_small")[0]
    packed_s = _adamw_replicated(small_all, _pack_small(small_w + [zero]), _pack_small(small_m + [zero]),
                                 _pack_small(small_v + [zero + 1.0]), "adamw_replicated")
    sg, sd, sm, sv = [_unpack_small(t, small_shapes) for t in packed_s]
    loss = sg[-1][0]

    return (loss, grad_x[None], *grad_w, *sg[:-1], *delta_w, *sd[:-1], *new_m, *sm[:-1], *new_v, *sv[:-1])
```

```python
import functools
import math

import numpy as np
import jax
import jax.numpy as jnp
from jax import lax
from jax.experimental import pallas as pl
from jax.experimental.pallas import tpu as pltpu

F32 = jnp.float32
MXU = jnp.bfloat16
WIRE = jnp.bfloat16

HEAD_DIM = 64
SB_HEADS = 8
SW_HEADS = 8
SW_KV = 2
SW_GROUP = SW_HEADS // SW_KV
BLOCK = 128
N_BUCKETS = 32
MAX_DISTANCE = 128
EPS = 1e-6
SCALE = HEAD_DIM ** -0.5
SB_W = SB_HEADS * HEAD_DIM
SW_W = SW_HEADS * HEAD_DIM
QKV_COLS = 3 * SB_W + SW_W + 2 * SW_KV * HEAD_DIM
N_DEV = 8
LANES = 128
N_PAIR = SB_HEADS // 2
NEG = -1e30

ADAM_LR = 0.001
ADAM_B1 = 0.9
ADAM_B2 = 0.999
ADAM_EPS = 1e-08
ADAM_WD = 0.01
ADAM_STEP = 10

VMEM_LIMIT = 48 * 1024 * 1024
SB_TQ = 256
SB_DEAD = -105.0

WEIGHTS = ("w_in", "w_up_a", "w_up_b", "w_o", "w_ff1", "w_ff2", "w_pe", "w_pg")
COL_SHARDED = ("w_in", "w_up_a", "w_up_b", "w_ff1", "w_pe")
SMALL = ("g_mix", "g_mlp", "g_pe", "g_final", "sinks", "rel_bias")


def _cparams(**kw):
    return pltpu.CompilerParams(vmem_limit_bytes=VMEM_LIMIT, **kw)


def _dot(a, b):
    return jnp.dot(a, b, preferred_element_type=F32)


def _dot_nt(a, b):
    return lax.dot_general(a, b, (((1,), (1,)), ((), ())), preferred_element_type=F32)


def _dot_tn(a, b):
    return lax.dot_general(a, b, (((0,), (0,)), ((), ())), preferred_element_type=F32)


def _tile(n, target, unit=LANES):
    if n <= target:
        return n
    t = (target // unit) * unit
    while t > unit and n % t:
        t -= unit
    assert n % t == 0, (n, target)
    return t


def _sigmoid(x):
    return 1.0 / (1.0 + jnp.exp(-x))


class _Comm:
    def __init__(self, inputs, out_shapes, sems, start, finish):
        self.inputs, self.out_shapes, self.sems = list(inputs), list(out_shapes), list(sems)
        self.start, self.finish = start, finish


def _call(body, *, grid, in_specs, out_specs, out_shape, scratch_shapes=(), args, name, comm=None):
    n_in, n_out, n_scr = len(in_specs), len(out_shape), len(scratch_shapes)
    if comm is None:
        outs = pl.pallas_call(body, grid=grid, in_specs=list(in_specs), out_specs=list(out_specs),
                              out_shape=list(out_shape), scratch_shapes=list(scratch_shapes),
                              compiler_params=_cparams(), name=name)(*args)
        return list(outs), None
    ci, co = len(comm.inputs), len(comm.out_shapes)
    any_spec = pl.BlockSpec(memory_space=pl.ANY)

    def wrapped(*refs):
        ins, cin = refs[:n_in], refs[n_in:n_in + ci]
        o0 = n_in + ci
        outs, cout = refs[o0:o0 + n_out], refs[o0 + n_out:o0 + n_out + co]
        s0 = o0 + n_out + co
        scr, csem = refs[s0:s0 + n_scr], refs[s0 + n_scr:]
        ids = [pl.program_id(d) for d in range(len(grid))]
        first = functools.reduce(jnp.logical_and, [i == 0 for i in ids])
        last = functools.reduce(jnp.logical_and, [i == g - 1 for i, g in zip(ids, grid)])
        pos = (lax.axis_index("x"), lax.axis_index("y"), lax.axis_index("c"))

        @pl.when(first)
        def _():
            comm.start(pos, cin, cout, csem)

        body(*ins, *outs, *scr)

        @pl.when(last)
        def _():
            comm.finish(pos, cin, cout, csem)

    outs = pl.pallas_call(wrapped, grid=grid, in_specs=list(in_specs) + [any_spec] * ci,
                          out_specs=list(out_specs) + [any_spec] * co,
                          out_shape=list(out_shape) + comm.out_shapes,
                          scratch_shapes=list(scratch_shapes) + comm.sems,
                          compiler_params=_cparams(), name=name)(*args, *comm.inputs)
    return list(outs[:n_out]), list(outs[n_out:])


def _accumulate(o_ref, value, first):
    @pl.when(first)
    def _():
        o_ref[...] = value

    @pl.when(jnp.logical_not(first))
    def _():
        o_ref[...] += value


def _mm(a, b, *, ta=False, tb=False, extras=(), epi=None, out_dtypes=(F32,),
        tm=1024, tn=1024, tk=1024, name, comm=None):
    if ta:
        kdim, m = a.shape
    else:
        m, kdim = a.shape
    n = b.shape[0] if tb else b.shape[1]
    assert (b.shape[1] if tb else b.shape[0]) == kdim
    tm, tn, tk = _tile(m, tm), _tile(n, tn), _tile(kdim, tk)
    nk = kdim // tk
    n_ex, n_out = len(extras), len(out_dtypes)

    a_spec = (pl.BlockSpec((tk, tm), lambda i, j, k: (k, i)) if ta
              else pl.BlockSpec((tm, tk), lambda i, j, k: (i, k)))
    b_spec = (pl.BlockSpec((tn, tk), lambda i, j, k: (j, k)) if tb
              else pl.BlockSpec((tk, tn), lambda i, j, k: (k, j)))
    ex_specs = []
    for e in extras:
        assert e.shape in ((m, n), (1, n), (m, 1)), (e.shape, m, n)
        if e.shape == (m, n):
            ex_specs.append(pl.BlockSpec((tm, tn), lambda i, j, k: (i, j)))
        elif e.shape[0] == 1:
            ex_specs.append(pl.BlockSpec((1, tn), lambda i, j, k: (0, j)))
        else:
            ex_specs.append(pl.BlockSpec((tm, 1), lambda i, j, k: (i, 0)))
    out_specs, out_shape, row_sums = [], [], []
    for dt in out_dtypes:
        kind = dt[1] if isinstance(dt, tuple) else "tile"
        row_sums.append(kind == "rowsum")
        if kind == "tile":
            out_specs.append(pl.BlockSpec((tm, tn), lambda i, j, k: (i, j)))
            out_shape.append(jax.ShapeDtypeStruct((m, n), dt))
            continue
        assert tn == n, "per-row and summed outputs need whole rows in one tile"
        if kind == "col":
            out_specs.append(pl.BlockSpec((tm, 1), lambda i, j, k: (i, 0)))
            out_shape.append(jax.ShapeDtypeStruct((m, 1), dt[0]))
        else:
            out_specs.append(pl.BlockSpec((1, tn), lambda i, j, k: (0, 0)))
            out_shape.append(jax.ShapeDtypeStruct((1, n), dt[0]))

    def body(a_ref, b_ref, *rest):
        ex_refs = rest[:n_ex]
        out_refs = rest[n_ex:n_ex + n_out]
        acc = rest[-1]
        k = pl.program_id(2)
        first_rows = pl.program_id(0) == 0

        def prod():
            av = a_ref[...].astype(MXU)
            bv = b_ref[...].astype(MXU)
            return _dot_tn(av, bv) if ta else (_dot_nt(av, bv) if tb else _dot(av, bv))

        def finish(res):
            if epi is not None:
                res = epi(res, *[e[...] for e in ex_refs])
            if not isinstance(res, tuple):
                res = (res,)
            for o_ref, r, summed in zip(out_refs, res, row_sums):
                if summed:
                    _accumulate(o_ref, r.astype(o_ref.dtype), first_rows)
                else:
                    o_ref[...] = r.astype(o_ref.dtype)

        if nk == 1:
            finish(prod())
            return

        @pl.when(k == 0)
        def _():
            acc[...] = prod()

        @pl.when(jnp.logical_and(k > 0, k < nk - 1))
        def _():
            acc[...] += prod()

        @pl.when(k == nk - 1)
        def _():
            finish(acc[...] + prod())

    outs, couts = _call(
        body,
        grid=(m // tm, n // tn, nk),
        in_specs=[a_spec, b_spec] + ex_specs,
        out_specs=out_specs,
        out_shape=out_shape,
        scratch_shapes=[pltpu.VMEM((tm, tn), F32)],
        args=(a, b, *extras), name=name, comm=comm)
    res = outs[0] if n_out == 1 else tuple(outs)
    return res if comm is None else (res, couts)


def _rms_fwd(x, g, name, comm=None):
    s, d = x.shape
    tr = _tile(s, 256)

    def body(x_ref, g_ref, h_ref, r_ref):
        xf = x_ref[...]
        r = lax.rsqrt(jnp.mean(xf * xf, axis=-1, keepdims=True) + EPS)
        h_ref[...] = ((xf * r) * g_ref[...]).astype(h_ref.dtype)
        r_ref[...] = r

    outs, couts = _call(
        body,
        grid=(s // tr,),
        in_specs=[pl.BlockSpec((tr, d), lambda i: (i, 0)), pl.BlockSpec((1, d), lambda i: (0, 0))],
        out_specs=[pl.BlockSpec((tr, d), lambda i: (i, 0)), pl.BlockSpec((tr, 1), lambda i: (i, 0))],
        out_shape=[jax.ShapeDtypeStruct((s, d), MXU), jax.ShapeDtypeStruct((s, 1), F32)],
        args=(x, g), name=name, comm=comm)
    return tuple(outs) if comm is None else (tuple(outs), couts)


def _loss_head(x, g, target, name):
    s, d = x.shape
    tr = _tile(s, 256)

    def body(x_ref, g_ref, t_ref, loss_ref, dx_ref, dg_ref):
        @pl.when(pl.program_id(0) == 0)
        def _():
            dg_ref[...] = jnp.zeros_like(dg_ref)
            loss_ref[...] = jnp.zeros_like(loss_ref)

        xf = x_ref[...]
        gv = g_ref[...]
        r = lax.rsqrt(jnp.mean(xf * xf, axis=-1, keepdims=True) + EPS)
        xhat = xf * r
        err = xhat * gv - t_ref[...]
        loss_ref[...] += 0.5 * jnp.sum(jnp.mean(err * err, axis=-1, keepdims=True), axis=0, keepdims=True)
        dy = err * (1.0 / d)
        dxhat = dy * gv
        mean = jnp.mean(dxhat * xhat, axis=-1, keepdims=True)
        dx_ref[...] = r * (dxhat - xhat * mean)
        dg_ref[...] += jnp.sum(dy * xhat, axis=0, keepdims=True)

    row = pl.BlockSpec((tr, d), lambda i: (i, 0))
    vec = pl.BlockSpec((1, d), lambda i: (0, 0))
    return pl.pallas_call(
        body,
        grid=(s // tr,),
        in_specs=[row, vec, row],
        out_specs=[pl.BlockSpec((1, LANES), lambda i: (0, 0)), row, vec],
        out_shape=[jax.ShapeDtypeStruct((1, LANES), F32), jax.ShapeDtypeStruct((s, d), F32),
                   jax.ShapeDtypeStruct((1, d), F32)],
        compiler_params=_cparams(),
        name=name,
    )(x, g, target)


def _mix_fwd(oa, ob, wa_t, wb_t, gates, name):
    s, kd = oa.shape
    d = wa_t.shape[0]
    tm, tn = _tile(s, 1024), _tile(d, 512)
    nj = d // tn

    def body(oa_ref, ob_ref, wa_ref, wb_ref, ga_ref, gb_ref, out_ref):
        ya = _dot_nt(oa_ref[...], wa_ref[...])
        yb = _dot_nt(ob_ref[...], wb_ref[...])
        out_ref[...] = (_sigmoid(ga_ref[...].astype(F32)) * ya
                        + _sigmoid(gb_ref[...].astype(F32)) * yb).astype(out_ref.dtype)

    o_spec = pl.BlockSpec((tm, kd), lambda i, j: (i, 0))
    w_spec = pl.BlockSpec((tn, kd), lambda i, j: (j, 0))
    return pl.pallas_call(
        body,
        grid=(s // tm, nj),
        in_specs=[o_spec, o_spec, w_spec, w_spec,
                  pl.BlockSpec((tm, tn), lambda i, j: (i, j)),
                  pl.BlockSpec((tm, tn), lambda i, j: (i, j + nj))],
        out_specs=pl.BlockSpec((tm, tn), lambda i, j: (i, j)),
        out_shape=jax.ShapeDtypeStruct((s, d), MXU),
        compiler_params=_cparams(),
        name=name,
    )(oa, ob, wa_t, wb_t, gates, gates)


def _mix_bwd(dx, w_o, oa, ob, wa_t, wb_t, gates, name):
    s, kd = oa.shape
    d = wa_t.shape[0]
    tm, tn = _tile(s, 1024), _tile(d, 512)
    nj = d // tn

    def body(dx_ref, wo_ref, oa_ref, ob_ref, wa_ref, wb_ref, ga_ref, gb_ref,
             dya_ref, dyb_ref, dga_ref, dgb_ref):
        dm = _dot_nt(dx_ref[...], wo_ref[...])
        ya = _dot_nt(oa_ref[...], wa_ref[...])
        yb = _dot_nt(ob_ref[...], wb_ref[...])
        sa = _sigmoid(ga_ref[...].astype(F32))
        sb = _sigmoid(gb_ref[...].astype(F32))
        dya_ref[...] = (dm * sa).astype(dya_ref.dtype)
        dyb_ref[...] = (dm * sb).astype(dyb_ref.dtype)
        dga_ref[...] = (dm * ya * sa * (1.0 - sa)).astype(dga_ref.dtype)
        dgb_ref[...] = (dm * yb * sb * (1.0 - sb)).astype(dgb_ref.dtype)

    o_spec = pl.BlockSpec((tm, kd), lambda i, j: (i, 0))
    w_spec = pl.BlockSpec((tn, kd), lambda i, j: (j, 0))
    t_spec = pl.BlockSpec((tm, tn), lambda i, j: (i, j))
    return pl.pallas_call(
        body,
        grid=(s // tm, nj),
        in_specs=[pl.BlockSpec((tm, d), lambda i, j: (i, 0)),
                  pl.BlockSpec((tn, d), lambda i, j: (j, 0)),
                  o_spec, o_spec, w_spec, w_spec, t_spec,
                  pl.BlockSpec((tm, tn), lambda i, j: (i, j + nj))],
        out_specs=[t_spec] * 4,
        out_shape=[jax.ShapeDtypeStruct((s, d), MXU)] * 4,
        compiler_params=_cparams(),
        name=name,
    )(dx, w_o, oa, ob, wa_t, wb_t, gates, gates)


def _ple(p, w_pe_t, h, w_pg, other, *, backward, name):
    s, kp = p.shape
    d = w_pe_t.shape[0]
    tm, tn = _tile(s, 1024), _tile(d, 512)

    def body(p_ref, wpe_ref, h_ref, wpg_ref, other_ref, *out_refs):
        pe = _dot_nt(p_ref[...].astype(MXU), wpe_ref[...])
        gt = _dot(h_ref[...], wpg_ref[...])
        sg = _sigmoid(gt)
        if backward:
            dout = other_ref[...]
            out_refs[0][...] = (dout * sg).astype(out_refs[0].dtype)
            out_refs[1][...] = (dout * pe * sg * (1.0 - sg)).astype(out_refs[1].dtype)
        else:
            out_refs[0][...] = other_ref[...] + pe * sg

    t_spec = pl.BlockSpec((tm, tn), lambda i, j: (i, j))
    if backward:
        out_specs, out_shape = [t_spec, t_spec], [jax.ShapeDtypeStruct((s, d), MXU)] * 2
    else:
        out_specs, out_shape = [t_spec], [jax.ShapeDtypeStruct((s, d), F32)]
    outs = pl.pallas_call(
        body,
        grid=(s // tm, d // tn),
        in_specs=[pl.BlockSpec((tm, kp), lambda i, j: (i, 0)),
                  pl.BlockSpec((tn, kp), lambda i, j: (j, 0)),
                  pl.BlockSpec((tm, d), lambda i, j: (i, 0)),
                  pl.BlockSpec((d, tn), lambda i, j: (0, j)),
                  t_spec],
        out_specs=out_specs,
        out_shape=out_shape,
        compiler_params=_cparams(),
        name=name,
    )(p, w_pe_t, h, w_pg, other)
    return tuple(outs) if backward else outs[0]


def _split_dot(x, tri):
    hi = x.astype(jnp.bfloat16)
    lo = (x - hi.astype(F32)).astype(jnp.bfloat16)
    return _dot(hi, tri) + _dot(lo, tri)


def _log_sigmoids(z):
    lb = jnp.minimum(z, 0.0) - jnp.log1p(jnp.exp(-jnp.abs(z)))
    return lb, lb - z


def _head_lanes(hh):
    lane = lax.broadcasted_iota(jnp.int32, (1, LANES), 1)
    return jnp.logical_and(lane >= hh * HEAD_DIM, lane < (hh + 1) * HEAD_DIM)


def _sb_fwd(qkv, name, comm=None):
    s = qkv.shape[0]
    tq = _tile(s, SB_TQ)

    def body(q_ref, k_ref, v_ref, o_ref):
        i = pl.program_id(1)
        qf = q_ref[...].astype(F32) * SCALE
        row = lax.broadcasted_iota(jnp.int32, (tq, tq), 0)
        col = lax.broadcasted_iota(jnp.int32, (tq, tq), 1)
        causal = col < row
        tri = jnp.where(row > col, 1.0, 0.0).astype(jnp.bfloat16)

        qms = [jnp.where(_head_lanes(hh), qf, 0.0).astype(MXU) for hh in range(2)]

        def block(kb, cs, accs, masked, gate=None):
            rows = pl.ds(pl.multiple_of(kb * tq, tq), tq)
            ks, vs = k_ref[rows, :], v_ref[rows, :]
            new_c, new_acc = [], []
            for hh in range(2):
                lb, lm = _log_sigmoids(_dot_nt(qms[hh], ks))
                if masked:
                    lm = jnp.where(causal, lm, 0.0)
                if gate is not None:
                    lm = lm * gate
                a = jnp.exp(lb + _split_dot(lm, tri) + cs[hh])
                if masked:
                    a = jnp.where(causal, a, 0.0)
                if gate is not None:
                    a = a * gate
                new_acc.append(accs[hh] + _dot(a.astype(MXU), vs))
                new_c.append(cs[hh] + jnp.sum(lm, axis=1, keepdims=True))
            return tuple(new_c), tuple(new_acc)

        def top(cs):
            return jnp.maximum(jnp.max(cs[0]), jnp.max(cs[1]))

        zc, za = jnp.zeros((tq, 1), F32), jnp.zeros((tq, LANES), F32)
        cs, accs = block(i, (zc, zc), (za, za), True)
        cs, accs = block(jnp.maximum(i - 1, 0), cs, accs, False, jnp.where(i > 0, 1.0, 0.0))

        def live(st):
            return jnp.logical_and(st[0] >= 0, st[1] > SB_DEAD)

        def walk(st):
            cs, accs = block(st[0], st[2], st[3], False)
            return st[0] - 1, top(cs), cs, accs

        accs = lax.while_loop(live, walk, (i - 2, top(cs), cs, accs))[3]
        o_ref[...] = jnp.where(_head_lanes(0), accs[0], accs[1]).astype(o_ref.dtype)

    outs, couts = _call(
        body,
        grid=(N_PAIR, s // tq),
        in_specs=[pl.BlockSpec((tq, LANES), lambda p, i: (i, p)),
                  pl.BlockSpec((s, LANES), lambda p, i: (0, N_PAIR + p)),
                  pl.BlockSpec((s, LANES), lambda p, i: (0, 2 * N_PAIR + p))],
        out_specs=[pl.BlockSpec((tq, LANES), lambda p, i: (i, p))],
        out_shape=[jax.ShapeDtypeStruct((s, SB_W), MXU)],
        args=(qkv, qkv, qkv), name=name, comm=comm)
    return outs[0] if comm is None else (outs[0], couts)


def _sb_bwd(qkv, do, name, comm=None):
    s = qkv.shape[0]
    tq = _tile(s, SB_TQ)
    nq = s // tq

    def body(q_ref, k_ref, v_ref, do_ref, dq_ref, dk_ref, dv_ref, dk_acc, dv_acc, carries):
        i = pl.program_id(1)

        @pl.when(i == 0)
        def _():
            dk_acc[...] = jnp.zeros_like(dk_acc)
            dv_acc[...] = jnp.zeros_like(dv_acc)

        qf = q_ref[...].astype(F32) * SCALE
        dof = do_ref[...]
        row = lax.broadcasted_iota(jnp.int32, (tq, tq), 0)
        col = lax.broadcasted_iota(jnp.int32, (tq, tq), 1)
        causal = col < row
        tri_rev = jnp.where(row > col, 1.0, 0.0).astype(jnp.bfloat16)
        tri_excl = jnp.where(row < col, 1.0, 0.0).astype(jnp.bfloat16)

        qms = [jnp.where(_head_lanes(hh), qf, 0.0).astype(MXU) for hh in range(2)]
        doms = [jnp.where(_head_lanes(hh), dof, jnp.zeros_like(dof)) for hh in range(2)]

        def terms(kb, masked):
            rows = pl.ds(pl.multiple_of(kb * tq, tq), tq)
            ks = k_ref[rows, :]
            out = []
            for hh in range(2):
                lb, lm = _log_sigmoids(_dot_nt(qms[hh], ks))
                if masked:
                    lm = jnp.where(causal, lm, 0.0)
                out.append((lb, lm))
            return out

        def row_sums(kb, masked, pre=None):
            pre = terms(kb, masked) if pre is None else pre
            return [jnp.sum(lm, axis=1, keepdims=True) for _, lm in pre]

        def top(cs):
            return jnp.maximum(jnp.max(cs[0]), jnp.max(cs[1]))

        def live(st):
            return jnp.logical_and(st[0] >= 0, st[1] > SB_DEAD)

        def record(st):
            kb, cs = st[0], st[2]
            sums = row_sums(kb, False)
            for hh in range(2):
                carries[hh, kb] = cs[hh]
            cs = tuple(cs[hh] + sums[hh] for hh in range(2))
            return kb - 1, top(cs), cs

        prev = jnp.maximum(i - 1, 0)
        gate = jnp.where(i > 0, 1.0, 0.0)
        t_diag, t_prev = terms(i, True), terms(prev, False)
        c_diag = row_sums(i, True, t_diag)
        sums = row_sums(prev, False, t_prev)
        c_prev = tuple(c_diag[hh] + sums[hh] * gate for hh in range(2))
        first = lax.while_loop(live, record, (i - 2, top(c_prev), c_prev))[0] + 1

        def block(kb, cs, gpres, dqs, masked, gate=None, pre=None):
            rows = pl.ds(pl.multiple_of(kb * tq, tq), tq)
            ks, vs = k_ref[rows, :], v_ref[rows, :]
            pre = terms(kb, masked) if pre is None else pre
            new_g, new_dq = [], []
            dk_add, dv_add = None, None
            for hh in range(2):
                lb, lm = pre[hh]
                a = jnp.exp(lb + _split_dot(lm, tri_rev) + cs[hh])
                if masked:
                    a = jnp.where(causal, a, 0.0)
                if gate is not None:
                    a = a * gate
                g = a * _dot_nt(doms[hh], vs)
                gsum = gpres[hh] + _split_dot(g, tri_excl)
                dz = g - (g + gsum) * jnp.exp(lb)
                if masked:
                    dz = jnp.where(causal, dz, 0.0)
                if gate is not None:
                    dz = dz * gate
                dzb = dz.astype(MXU)
                new_dq.append(dqs[hh] + _dot(dzb, ks))
                dk_h = _dot_tn(dzb, qms[hh])
                dv_h = _dot_tn(a.astype(MXU), doms[hh])
                dk_add = dk_h if dk_add is None else dk_add + dk_h
                dv_add = dv_h if dv_add is None else dv_add + dv_h
                new_g.append(gpres[hh] + jnp.sum(g, axis=1, keepdims=True))
            dk_acc[rows, :] += dk_add
            dv_acc[rows, :] += dv_add
            return tuple(new_g), tuple(new_dq)

        zc, za = jnp.zeros((tq, 1), F32), jnp.zeros((tq, LANES), F32)
        gpres, dqs = lax.fori_loop(
            first, i - 1, lambda kb, cr: block(kb, (carries[0, kb], carries[1, kb]), cr[0], cr[1], False),
            ((zc, zc), (za, za)))
        gpres, dqs = block(prev, c_diag, gpres, dqs, False, gate, t_prev)
        dqs = block(i, (zc, zc), gpres, dqs, True, None, t_diag)[1]
        dq_ref[...] = (jnp.where(_head_lanes(0), dqs[0], dqs[1]) * SCALE).astype(dq_ref.dtype)

        @pl.when(i == nq - 1)
        def _():
            dk_ref[...] = dk_acc[...].astype(dk_ref.dtype)
            dv_ref[...] = dv_acc[...].astype(dv_ref.dtype)

    blk = pl.BlockSpec((tq, LANES), lambda p, i: (i, p))
    full = pl.BlockSpec((s, LANES), lambda p, i: (0, p))
    outs, couts = _call(
        body,
        grid=(N_PAIR, nq),
        in_specs=[blk,
                  pl.BlockSpec((s, LANES), lambda p, i: (0, N_PAIR + p)),
                  pl.BlockSpec((s, LANES), lambda p, i: (0, 2 * N_PAIR + p)),
                  blk],
        out_specs=[blk, full, full],
        out_shape=[jax.ShapeDtypeStruct((s, SB_W), MXU)] * 3,
        scratch_shapes=[pltpu.VMEM((s, LANES), F32), pltpu.VMEM((s, LANES), F32),
                        pltpu.VMEM((2, nq, tq, 1), F32)],
        args=(qkv, qkv, qkv, do), name=name, comm=comm)
    return tuple(outs) if comm is None else (tuple(outs), couts)


def _bucket_table():
    i = np.arange(BLOCK)[:, None]
    j = np.arange(2 * BLOCK)[None, :]
    d = np.maximum(BLOCK + i - j, 0)
    max_exact = N_BUCKETS // 2
    df = np.maximum(d, 1).astype(np.float32)
    large = max_exact + (np.log(df / max_exact) / math.log(MAX_DISTANCE / max_exact)
                         * (N_BUCKETS - max_exact)).astype(np.int32)
    large = np.minimum(large, N_BUCKETS - 1)
    return np.where(d < max_exact, d, large).astype(np.int32)


def _build_bias(rel_bias, buckets, name):
    def body(rb_ref, bk_ref, out_ref):
        h = pl.program_id(0)
        bk = bk_ref[...]
        acc = jnp.zeros(bk.shape, F32)
        for b in range(N_BUCKETS):
            acc = jnp.where(bk == b, rb_ref[b, h], acc)
        out_ref[...] = acc

    return pl.pallas_call(
        body,
        grid=(SW_HEADS,),
        in_specs=[pl.BlockSpec(memory_space=pltpu.SMEM),
                  pl.BlockSpec((BLOCK, 2 * BLOCK), lambda h: (0, 0))],
        out_specs=pl.BlockSpec((None, BLOCK, 2 * BLOCK), lambda h: (h, 0, 0)),
        out_shape=jax.ShapeDtypeStruct((SW_HEADS, BLOCK, 2 * BLOCK), F32),
        name=name,
    )(rel_bias, buckets)


def _bias_grad(dbias_layers, buckets, name):
    n_l = len(dbias_layers)

    def body(*refs):
        bk = refs[n_l][...]
        out_ref = refs[n_l + 1]
        db = refs[0][...]
        for r in refs[1:n_l]:
            db = db + r[...]
        lane = lax.broadcasted_iota(jnp.int32, (1, LANES), 1)
        acc = jnp.zeros((1, LANES), F32)
        for b in range(N_BUCKETS):
            part = jnp.sum(jnp.where(bk == b, db, 0.0), axis=1, keepdims=True)
            tot = jnp.sum(part, axis=0, keepdims=True)
            acc = jnp.where(lane == b, tot, acc)
        out_ref[...] = acc

    hspec = pl.BlockSpec((None, BLOCK, 2 * BLOCK), lambda h: (h, 0, 0))
    return pl.pallas_call(
        body,
        grid=(SW_HEADS,),
        in_specs=[hspec] * n_l + [pl.BlockSpec((BLOCK, 2 * BLOCK), lambda h: (0, 0))],
        out_specs=pl.BlockSpec((None, 1, LANES), lambda h: (h, 0, 0)),
        out_shape=jax.ShapeDtypeStruct((SW_HEADS, 1, LANES), F32),
        name=name,
    )(*dbias_layers, buckets)


GROUP_ROWS = SW_GROUP * BLOCK


def _group_lanes(g):
    lane = lax.broadcasted_iota(jnp.int32, (1, LANES), 1)
    gvec = jnp.zeros((1, LANES), jnp.int32) + g
    return jnp.where(lane >= HEAD_DIM, 1, 0) == gvec, gvec


def _stack_heads(x, g):
    kv_lanes, gvec = _group_lanes(g)
    parts = []
    for j in range(SW_GROUP):
        half = x[:, (j // 2) * LANES:(j // 2 + 1) * LANES]
        moved = jnp.where(gvec == j % 2, half, pltpu.roll(half, HEAD_DIM, 1))
        parts.append(jnp.where(kv_lanes, moved, 0.0))
    return jnp.concatenate(parts, axis=0)


def _unstack_heads(y, g):
    _, gvec = _group_lanes(g)
    heads = []
    for j in range(SW_GROUP):
        yj = y[j * BLOCK:(j + 1) * BLOCK]
        heads.append(jnp.where(gvec == j % 2, yj, pltpu.roll(yj, HEAD_DIM, 1)))
    pairs = [jnp.where(_head_lanes(0), heads[2 * p], heads[2 * p + 1]) for p in range(SW_GROUP // 2)]
    return jnp.concatenate(pairs, axis=1)


def _per_head_col(values):
    return jnp.concatenate([jnp.zeros((BLOCK, 1), F32) + v for v in values], axis=0)


def _swa_scores(qs, kp, kc, bias_ref, n):
    row = jnp.bitwise_and(lax.broadcasted_iota(jnp.int32, (GROUP_ROWS, BLOCK), 0), BLOCK - 1)
    col = lax.broadcasted_iota(jnp.int32, (GROUP_ROWS, BLOCK), 1)
    bias = bias_ref[...].reshape(GROUP_ROWS, 2 * BLOCK)
    s1 = _dot_nt(qs, kp) + bias[:, :BLOCK]
    s2 = _dot_nt(qs, kc) + bias[:, BLOCK:]
    no_prev = jnp.where(n > 0, 0, BLOCK)
    s1 = jnp.where(col > row + no_prev, s1, NEG)
    s2 = jnp.where(col <= row, s2, NEG)
    return s1, s2


def _swa_specs(s):
    q_blk = 3 * SB_W // (2 * LANES)
    k_blk = (3 * SB_W + SW_W) // LANES
    return (pl.BlockSpec((s, 2 * LANES), lambda g: (0, q_blk + g)),
            pl.BlockSpec((s, LANES), lambda g: (0, k_blk)),
            pl.BlockSpec((s, LANES), lambda g: (0, k_blk + 1)))


def _swa_fwd(qkv, bias, sinks, name, comm=None):
    s = qkv.shape[0]
    nb = s // BLOCK

    def body(sink_ref, q_ref, k_ref, v_ref, bias_ref, o_ref, lse_ref):
        g = pl.program_id(0)
        sink = _per_head_col([sink_ref[SW_GROUP * g + j] for j in range(SW_GROUP)])
        lane = lax.broadcasted_iota(jnp.int32, (1, LANES), 1)

        def step(n, carry):
            r0 = pl.multiple_of(n * BLOCK, BLOCK)
            p0 = pl.multiple_of(jnp.maximum(n - 1, 0) * BLOCK, BLOCK)
            cur, prev = pl.ds(r0, BLOCK), pl.ds(p0, BLOCK)
            qs = _stack_heads(q_ref[cur, :].astype(F32) * SCALE, g).astype(MXU)
            s1, s2 = _swa_scores(qs, k_ref[prev, :], k_ref[cur, :], bias_ref, n)
            m = jnp.maximum(jnp.max(jnp.maximum(s1, s2), axis=1, keepdims=True), sink)
            e1 = jnp.exp(s1 - m)
            e2 = jnp.exp(s2 - m)
            den = jnp.sum(e1 + e2, axis=1, keepdims=True) + jnp.exp(sink - m)
            o = _dot((e1 / den).astype(MXU), v_ref[prev, :]) + _dot((e2 / den).astype(MXU), v_ref[cur, :])
            o_ref[cur, :] = _unstack_heads(o, g).astype(o_ref.dtype)
            lse = m + jnp.log(den)
            lse_row = jnp.zeros((BLOCK, LANES), F32)
            for j in range(SW_GROUP):
                lse_row = jnp.where(lane == j, lse[j * BLOCK:(j + 1) * BLOCK], lse_row)
            lse_ref[cur, :] = lse_row
            return carry

        lax.fori_loop(0, nb, step, 0, unroll=2)

    outs, couts = _call(
        body,
        grid=(SW_KV,),
        in_specs=[pl.BlockSpec(memory_space=pltpu.SMEM), *_swa_specs(s),
                  pl.BlockSpec((SW_GROUP, BLOCK, 2 * BLOCK), lambda g: (g, 0, 0))],
        out_specs=[pl.BlockSpec((s, 2 * LANES), lambda g: (0, g)),
                   pl.BlockSpec((None, s, LANES), lambda g: (g, 0, 0))],
        out_shape=[jax.ShapeDtypeStruct((s, SW_W), MXU), jax.ShapeDtypeStruct((SW_KV, s, LANES), F32)],
        args=(sinks, qkv, qkv, qkv, bias), name=name, comm=comm)
    return tuple(outs) if comm is None else (tuple(outs), couts)


def _swa_bwd(qkv, bias, sinks, do, lse, name, comm=None):
    s = qkv.shape[0]
    nb = s // BLOCK

    def body(sink_ref, q_ref, k_ref, v_ref, bias_ref, do_ref, lse_ref,
             dq_ref, dk_ref, dv_ref, dbias_ref, dsink_ref, dk_acc, dv_acc):
        g = pl.program_id(0)
        sink = _per_head_col([sink_ref[SW_GROUP * g + j] for j in range(SW_GROUP)])
        lane = lax.broadcasted_iota(jnp.int32, (1, LANES), 1)

        @pl.when(g == 0)
        def _():
            dk_acc[...] = jnp.zeros_like(dk_acc)
            dv_acc[...] = jnp.zeros_like(dv_acc)

        dbias_ref[...] = jnp.zeros_like(dbias_ref)

        def step(n, dsink_rows):
            r0 = pl.multiple_of(n * BLOCK, BLOCK)
            p0 = pl.multiple_of(jnp.maximum(n - 1, 0) * BLOCK, BLOCK)
            cur, prev = pl.ds(r0, BLOCK), pl.ds(p0, BLOCK)
            qs = _stack_heads(q_ref[cur, :].astype(F32) * SCALE, g).astype(MXU)
            dos = _stack_heads(do_ref[cur, :].astype(F32), g).astype(MXU)
            kp, kc, vp, vc = k_ref[prev, :], k_ref[cur, :], v_ref[prev, :], v_ref[cur, :]
            lse_row = lse_ref[cur, :]
            lse = jnp.concatenate([jnp.sum(jnp.where(lane == j, lse_row, 0.0), axis=1, keepdims=True)
                                   for j in range(SW_GROUP)], axis=0)
            s1, s2 = _swa_scores(qs, kp, kc, bias_ref, n)
            pr1 = jnp.exp(s1 - lse)
            pr2 = jnp.exp(s2 - lse)
            dpr1 = _dot_nt(dos, vp)
            dpr2 = _dot_nt(dos, vc)
            delta = jnp.sum(pr1 * dpr1 + pr2 * dpr2, axis=1, keepdims=True)
            ds1 = pr1 * (dpr1 - delta)
            ds2 = pr2 * (dpr2 - delta)
            dbias_ref[:, :, :BLOCK] += ds1.reshape(SW_GROUP, BLOCK, BLOCK)
            dbias_ref[:, :, BLOCK:] += ds2.reshape(SW_GROUP, BLOCK, BLOCK)
            ds1b, ds2b = ds1.astype(MXU), ds2.astype(MXU)
            dq = _dot(ds1b, kp) + _dot(ds2b, kc)
            dq_ref[cur, :] = (_unstack_heads(dq, g) * SCALE).astype(dq_ref.dtype)
            dk_acc[prev, :] += _dot_tn(ds1b, qs)
            dk_acc[cur, :] += _dot_tn(ds2b, qs)
            dv_acc[prev, :] += _dot_tn(pr1.astype(MXU), dos)
            dv_acc[cur, :] += _dot_tn(pr2.astype(MXU), dos)
            return dsink_rows - jnp.exp(sink - lse) * delta

        rows = lax.fori_loop(0, nb, step, jnp.zeros((GROUP_ROWS, 1), F32), unroll=2)
        for j in range(SW_GROUP):
            dsink_ref[j] = jnp.broadcast_to(jnp.sum(rows[j * BLOCK:(j + 1) * BLOCK], axis=0, keepdims=True),
                                            (1, LANES))

        @pl.when(g == SW_KV - 1)
        def _():
            dk_ref[...] = dk_acc[...].astype(dk_ref.dtype)
            dv_ref[...] = dv_acc[...].astype(dv_ref.dtype)

    grp = pl.BlockSpec((s, 2 * LANES), lambda g: (0, g))
    kv_out = pl.BlockSpec((s, LANES), lambda g: (0, 0))
    bspec = pl.BlockSpec((SW_GROUP, BLOCK, 2 * BLOCK), lambda g: (g, 0, 0))
    outs, couts = _call(
        body,
        grid=(SW_KV,),
        in_specs=[pl.BlockSpec(memory_space=pltpu.SMEM), *_swa_specs(s), bspec, grp,
                  pl.BlockSpec((None, s, LANES), lambda g: (g, 0, 0))],
        out_specs=[grp, kv_out, kv_out, bspec, pl.BlockSpec((SW_GROUP, 1, LANES), lambda g: (g, 0, 0))],
        out_shape=[jax.ShapeDtypeStruct((s, SW_W), MXU),
                   jax.ShapeDtypeStruct((s, LANES), MXU),
                   jax.ShapeDtypeStruct((s, LANES), MXU),
                   jax.ShapeDtypeStruct((SW_HEADS, BLOCK, 2 * BLOCK), F32),
                   jax.ShapeDtypeStruct((SW_HEADS, 1, LANES), F32)],
        scratch_shapes=[pltpu.VMEM((s, LANES), F32), pltpu.VMEM((s, LANES), F32)],
        args=(sinks, qkv, qkv, qkv, bias, do, lse), name=name, comm=comm)
    return tuple(outs) if comm is None else (tuple(outs), couts)


class _NoPlan:
    def comm(self, name):
        return None

    def done(self, name, outs):
        pass

    def grad(self, layer, name, value):
        pass

    def layer_done(self, layer):
        pass


def _run(plan, fn, *args, name, **kw):
    comm = plan.comm(name)
    if comm is None:
        return fn(*args, name=name, **kw)
    res, outs = fn(*args, name=name, comm=comm, **kw)
    plan.done(name, outs)
    return res


def _norm_bwd(dh, x, dres, r, g):
    xhat = x * r
    dxhat = dh * g
    dx = dres + r * (dxhat - xhat * jnp.mean(dxhat * xhat, axis=-1, keepdims=True))
    return dx, dx, jnp.sum(dh * xhat, axis=0, keepdims=True)


def _residual_norm(acc, res, g):
    x = res + acc
    r = lax.rsqrt(jnp.mean(x * x, axis=-1, keepdims=True) + EPS)
    return x, (x * r) * g, r


def _layer_fwd(x, p, w, g_mix, g_mlp, g_pe, sinks, bias, tag, plan):
    h1, r1 = _run(plan, _rms_fwd, x, g_mix, name=f"rms_mix_{tag}")
    qkv = _run(plan, _mm, h1, w["w_qkv_t"], tb=True, out_dtypes=(MXU,), name=f"proj_qkv_{tag}")
    gates = _run(plan, _mm, h1, w["w_gate_t"], tb=True, out_dtypes=(MXU,), name=f"proj_gate_{tag}")
    oa = _run(plan, _sb_fwd, qkv, name=f"sb_fwd_{tag}")
    ob, lse = _run(plan, _swa_fwd, qkv, bias, sinks, name=f"swa_fwd_{tag}")
    merged = _mix_fwd(oa, ob, w["w_up_a_t"], w["w_up_b_t"], gates, f"mix_fwd_{tag}")
    d = x.shape[1]
    normed = (F32, MXU, (F32, "col"))
    x1, h2, r2 = _run(plan, _mm, merged, w["w_o"], extras=(x, g_mlp), epi=_residual_norm, out_dtypes=normed,
                      tn=d, name=f"out_proj_{tag}")
    u, act = _run(plan, _mm, h2, w["w_ff1_t"], tb=True,
                  epi=lambda acc: (acc, jnp.square(jnp.maximum(acc, 0.0))),
                  out_dtypes=(MXU, MXU), name=f"ff1_{tag}")
    x2, h3, r3 = _run(plan, _mm, act, w["w_ff2"], extras=(x1, g_pe), epi=_residual_norm, out_dtypes=normed,
                      tn=d, name=f"ff2_{tag}")
    x3 = _ple(p, w["w_pe_t"], h3, w["w_pg"], x2, backward=False, name=f"ple_fwd_{tag}")
    saved = dict(x=x, h1=h1, r1=r1, gates=gates, qkv=qkv, lse=lse, oa=oa, ob=ob, merged=merged,
                 x1=x1, h2=h2, r2=r2, u=u, act=act, x2=x2, h3=h3, r3=r3)
    return x3, saved


def _layer_bwd(dx3, sv, p, w, g_mix, g_mlp, g_pe, sinks, bias, layer, plan):
    tag = f"l{layer}"
    gw = {}
    wire = (WIRE,)

    def dw(name, a, b):
        gw[name] = _run(plan, _mm, a, b, ta=True, out_dtypes=wire, name=f"d{name}_{tag}")
        plan.grad(layer, name, gw[name])

    dpe, dgt = _ple(p, w["w_pe_t"], sv["h3"], w["w_pg"], dx3, backward=True, name=f"ple_bwd_{tag}")
    dw("w_pe", dpe, p)
    dw("w_pg", sv["h3"], dgt)
    d = dx3.shape[1]
    grads = (F32, MXU, (F32, "rowsum"))
    dx2, dx2b, dg_pe = _run(plan, _mm, dgt, w["w_pg"], tb=True, extras=(sv["x2"], dx3, sv["r3"], g_pe),
                            epi=_norm_bwd, out_dtypes=grads, tm=512, tn=d, name=f"dh_pe_{tag}")
    dw("w_ff2", sv["act"], dx2b)
    du = _run(plan, _mm, dx2b, w["w_ff2"], tb=True, extras=(sv["u"],),
              epi=lambda acc, u: acc * (2.0 * jnp.maximum(u.astype(F32), 0.0)), out_dtypes=(MXU,),
              name=f"dact_{tag}")
    dw("w_ff1", du, sv["h2"])
    dx1, dx1b, dg_mlp = _run(plan, _mm, du, w["w_ff1_t"], extras=(sv["x1"], dx2, sv["r2"], g_mlp),
                             epi=_norm_bwd, out_dtypes=grads, tm=512, tn=d, name=f"dh_mlp_{tag}")
    dw("w_o", sv["merged"], dx1b)
    dya, dyb, dga, dgb = _mix_bwd(dx1b, w["w_o"], sv["oa"], sv["ob"], w["w_up_a_t"], w["w_up_b_t"],
                                  sv["gates"], f"mix_bwd_{tag}")
    dw("w_up_a", dya, sv["oa"])
    dw("w_up_b", dyb, sv["ob"])
    doa = _run(plan, _mm, dya, w["w_up_a_t"], out_dtypes=(MXU,), name=f"do_a_{tag}")
    dob = _run(plan, _mm, dyb, w["w_up_b_t"], out_dtypes=(MXU,), name=f"do_b_{tag}")
    dqb, dkb, dvb, dbias, dsink = _run(plan, _swa_bwd, sv["qkv"], bias, sinks, dob, sv["lse"],
                                       name=f"swa_bwd_{tag}")
    dqa, dka, dva = _run(plan, _sb_bwd, sv["qkv"], doa, name=f"sb_bwd_{tag}")
    dqkv = jnp.concatenate([dqa, dka, dva, dqb, dkb, dvb], axis=1)
    gw_qkv = _mm(dqkv, sv["h1"], ta=True, out_dtypes=wire, name=f"dw_qkv_{tag}")
    gw_ga = _mm(dga, sv["h1"], ta=True, out_dtypes=wire, name=f"dw_ga_{tag}")
    gw_gb = _mm(dgb, sv["h1"], ta=True, out_dtypes=wire, name=f"dw_gb_{tag}")
    gw["w_in"] = jnp.concatenate([gw_qkv, gw_ga, gw_gb], axis=0)
    plan.grad(layer, "w_in", gw["w_in"])
    d = dga.shape[1]
    add = lambda acc, res: res + acc
    dh1 = _run(plan, _mm, dga, w["w_gate_t"][:d], name=f"dh_ga_{tag}")
    dh1 = _run(plan, _mm, dgb, w["w_gate_t"][d:], extras=(dh1,), epi=add, name=f"dh_gb_{tag}")
    dx, _, dg_mix = _run(plan, _mm, dqkv, w["w_qkv_t"], tk=768, extras=(dh1, sv["x"], dx1, sv["r1"], g_mix),
                         epi=lambda acc, prev, *rest: _norm_bwd(acc + prev, *rest), out_dtypes=grads,
                         tm=512, tn=d, name=f"dh_qkv_{tag}")
    small = dict(g_mix=dg_mix, g_mlp=dg_mlp, g_pe=dg_pe, sinks=dsink[:, 0, 0], dbias=dbias)
    return dx, gw, small


def _local_step(x, p, target, weights, g_mix, g_mlp, g_pe, g_final, sinks, rel_bias, plan=None):
    plan = _NoPlan() if plan is None else plan
    depth = g_mix.shape[0]
    buckets = jnp.asarray(_bucket_table())
    bias = _build_bias(rel_bias, buckets, "build_bias")
    saved, wfull = [], []
    h = x
    for l in range(depth):
        wfull.append(weights(l))
        h, sv = _layer_fwd(h, p[l], wfull[l], g_mix[l:l + 1], g_mlp[l:l + 1], g_pe[l:l + 1],
                           sinks[l], bias, f"l{l}", plan)
        saved.append(sv)
    loss_row, dx, dg_final = _loss_head(h, g_final[None, :], target, "loss_head")
    gws = [None] * depth
    smalls = [None] * depth
    for l in reversed(range(depth)):
        dx, gws[l], smalls[l] = _layer_bwd(dx, saved[l], p[l], wfull[l], g_mix[l:l + 1], g_mlp[l:l + 1],
                                           g_pe[l:l + 1], sinks[l], bias, l, plan)
        plan.layer_done(l)
    drel = _bias_grad([sm["dbias"] for sm in smalls], buckets, "bias_grad")[:, 0, :N_BUCKETS].T
    small = dict(
        g_mix=jnp.concatenate([sm["g_mix"] for sm in smalls], axis=0),
        g_mlp=jnp.concatenate([sm["g_mlp"] for sm in smalls], axis=0),
        g_pe=jnp.concatenate([sm["g_pe"] for sm in smalls], axis=0),
        g_final=dg_final[0],
        sinks=jnp.stack([sm["sinks"] for sm in smalls], axis=0),
        rel_bias=drel,
    )
    return loss_row, dx, gws, small


MESH_ID = pl.DeviceIdType.MESH
ANY = pl.BlockSpec(memory_space=pl.ANY)


def _position():
    return lax.axis_index("x"), lax.axis_index("y"), lax.axis_index("c")


def _run_comm(comm, name):
    ci, co = len(comm.inputs), len(comm.out_shapes)

    def body(*refs):
        cin, cout, csem = refs[:ci], refs[ci:ci + co], refs[ci + co:]
        pos = _position()
        comm.start(pos, cin, cout, csem)
        comm.finish(pos, cin, cout, csem)

    return pl.pallas_call(body, out_shape=comm.out_shapes, in_specs=[ANY] * ci, out_specs=[ANY] * co,
                          scratch_shapes=comm.sems, name=name)(*comm.inputs)


def _gather_comm(shards):
    n = len(shards)

    def copies(pos, x_refs, out_refs, sems):
        send_sems, recv_sems, local_sems = sems
        x, y, c = pos
        me, sibling = (x, y, c), (x, y, 1 - c)
        chips = [(1 - x, y), (x, 1 - y), (1 - x, 1 - y)]

        def slot(a, px, py, pc):
            return out_refs[a].at[4 * px + 2 * py + pc]

        def copy(a, k, block, to, src=None):
            return pltpu.make_async_remote_copy(
                src_ref=slot(a, *block) if src is None else src, dst_ref=slot(a, *block),
                send_sem=send_sems.at[a, k], recv_sem=recv_sems.at[a, k],
                device_id=to, device_id_type=MESH_ID)

        mine = [pltpu.make_async_copy(x_refs[a], slot(a, *me), local_sems.at[a]) for a in range(n)]
        first = []
        for a in range(n):
            first.append(copy(a, 0, me, sibling, src=x_refs[a]))
            first += [copy(a, 1 + j, me, (*chip, c), src=x_refs[a]) for j, chip in enumerate(chips)]
        return me, sibling, chips, copy, mine, first

    def start(pos, x_refs, out_refs, sems):
        _, _, _, _, mine, first = copies(pos, x_refs, out_refs, sems)
        for cp in mine + first:
            cp.start()

    def finish(pos, x_refs, out_refs, sems):
        me, sibling, chips, copy, mine, first = copies(pos, x_refs, out_refs, sems)
        c = pos[2]
        passed = []
        for j, chip in enumerate(chips):
            for a in range(n):
                copy(a, 1 + j, (*chip, c), me).wait_recv()
                fwd = copy(a, 4 + j, (*chip, c), sibling)
                fwd.start()
                passed.append(fwd)
        for a in range(n):
            copy(a, 0, sibling, me).wait_recv()
            for j, chip in enumerate(chips):
                copy(a, 4 + j, (*chip, 1 - c), me).wait_recv()
        for cp in first + passed:
            cp.wait_send()
        for cp in mine:
            cp.wait()

    return _Comm(shards, [jax.ShapeDtypeStruct((N_DEV,) + s.shape, s.dtype) for s in shards],
                 [pltpu.SemaphoreType.DMA((n, 7)), pltpu.SemaphoreType.DMA((n, 7)),
                  pltpu.SemaphoreType.DMA((n,))], start, finish)


def _exchange_comm(arrays, n_slots, route):
    n = len(arrays)

    def copies(pos, in_refs, out_refs, sems):
        send_sems, recv_sems = sems
        out = []
        for a in range(n):
            for j in range(n_slots):
                src_slot, peer = route(pos, j)
                out.append(pltpu.make_async_remote_copy(
                    src_ref=in_refs[a].at[src_slot], dst_ref=out_refs[a].at[j],
                    send_sem=send_sems.at[a, j], recv_sem=recv_sems.at[a, j],
                    device_id=peer, device_id_type=MESH_ID))
        return out

    def start(pos, in_refs, out_refs, sems):
        for cp in copies(pos, in_refs, out_refs, sems):
            cp.start()

    def finish(pos, in_refs, out_refs, sems):
        for cp in copies(pos, in_refs, out_refs, sems):
            cp.wait()

    return _Comm(arrays, [jax.ShapeDtypeStruct((n_slots,) + g.shape[1:], g.dtype) for g in arrays],
                 [pltpu.SemaphoreType.DMA((n, n_slots)), pltpu.SemaphoreType.DMA((n, n_slots))], start, finish)


def _rs_sibling_comm(gs):
    return _exchange_comm(gs, 4, lambda pos, j: (2 * j + (1 - pos[2]), (pos[0], pos[1], 1 - pos[2])))


def _chip_of(k, x, y):
    return x ^ ((k + 1) & 1), y ^ (((k + 1) >> 1) & 1)


def _chip_partials(pos, gs, recvs, name):
    n = len(gs)

    def body(pos_ref, *refs):
        for a in range(n):
            refs[2 * n + a][...] = (refs[a][...].astype(F32) + refs[n + a][...].astype(F32)
                                    ).astype(refs[2 * n + a].dtype)

    def g_map(k, pos_ref):
        cx, cy = _chip_of(k, pos_ref[0], pos_ref[1])
        return (4 * cx + 2 * cy + pos_ref[2], 0, 0)

    def r_map(k, pos_ref):
        cx, cy = _chip_of(k, pos_ref[0], pos_ref[1])
        return (2 * cx + cy, 0, 0)

    slab = [(None,) + g.shape[1:] for g in gs]
    return pl.pallas_call(
        body,
        grid_spec=pltpu.PrefetchScalarGridSpec(
            num_scalar_prefetch=1,
            grid=(4,),
            in_specs=[pl.BlockSpec(sh, g_map) for sh in slab] + [pl.BlockSpec(sh, r_map) for sh in slab],
            out_specs=[pl.BlockSpec(sh, lambda k, pos_ref: (k, 0, 0)) for sh in slab],
        ),
        out_shape=[jax.ShapeDtypeStruct((4,) + g.shape[1:], g.dtype) for g in gs],
        compiler_params=_cparams(),
        name=name,
    )(pos, *gs, *recvs)


def _rs_chips_comm(parts):
    return _exchange_comm(parts, 3, lambda pos, k: (k, (*_chip_of(k, pos[0], pos[1]), pos[2])))


def _adamw_math(w, g, m, v):
    m = ADAM_B1 * m + (1.0 - ADAM_B1) * g
    v = ADAM_B2 * v + (1.0 - ADAM_B2) * (g * g)
    m_hat = m / (1.0 - ADAM_B1 ** ADAM_STEP)
    v_hat = v / (1.0 - ADAM_B2 ** ADAM_STEP)
    delta = -ADAM_LR * (m_hat / (jnp.sqrt(v_hat) + ADAM_EPS) + ADAM_WD * w)
    return delta, m, v


def _adamw_weight(parts, recvs, w, m, v, name, grad_t=False):
    depth, a, b = w.shape
    ta = _tile(a, 288, unit=LANES if grad_t else 16)
    ni = a // ta
    g_block = (b, ta) if grad_t else (ta, b)

    def body(*refs):
        p_refs, r_refs = refs[:depth], refs[depth:2 * depth]
        w_ref, m_ref, v_ref = refs[2 * depth:2 * depth + 3]
        g_out, d_out, m_out, v_out = refs[2 * depth + 3:]
        layer = pl.program_id(0)
        g = jnp.zeros(g_block, F32)
        for l in range(depth):
            gl = p_refs[l][...].astype(F32)
            for k in range(3):
                gl = gl + r_refs[l][k].astype(F32)
            g = jnp.where(layer == l, gl, g)
        if grad_t:
            g = g.T
        delta, m_new, v_new = _adamw_math(w_ref[...], g, m_ref[...], v_ref[...])
        g_out[...] = g
        d_out[...] = delta
        m_out[...] = m_new
        v_out[...] = v_new

    def hold(l):
        return lambda layer, i: jnp.where(layer == l, i, jnp.where(layer < l, 0, ni - 1))

    def g_index(slot, f):
        if grad_t:
            return lambda layer, i: (slot, 0, f(layer, i))
        return lambda layer, i: (slot, f(layer, i), 0)

    p_specs = [pl.BlockSpec((None,) + g_block, g_index(3, hold(l))) for l in range(depth)]
    r_specs = [pl.BlockSpec((3,) + g_block, g_index(0, hold(l))) for l in range(depth)]
    row = pl.BlockSpec((None, ta, b), lambda layer, i: (layer, i, 0))
    return pl.pallas_call(
        body,
        grid=(depth, ni),
        in_specs=p_specs + r_specs + [row, row, row],
        out_specs=[row] * 4,
        out_shape=[jax.ShapeDtypeStruct(w.shape, F32)] * 4,
        compiler_params=_cparams(),
        name=name,
    )(*parts, *recvs, w, m, v)


def _adamw_replicated(gathered, w, m, v, name):
    r, lanes = w.shape

    def body(g_ref, w_ref, m_ref, v_ref, g_out, d_out, m_out, v_out):
        g = g_ref[0]
        for k in range(1, N_DEV):
            g = g + g_ref[k]
        delta, m_new, v_new = _adamw_math(w_ref[...], g, m_ref[...], v_ref[...])
        g_out[...] = g
        d_out[...] = delta
        m_out[...] = m_new
        v_out[...] = v_new

    return pl.pallas_call(
        body,
        out_shape=[jax.ShapeDtypeStruct((r, lanes), F32)] * 4,
        name=name,
    )(gathered, w, m, v)


def _wire_shard(name, shard):
    return (shard.T if name in COL_SHARDED else shard).astype(WIRE)


def _full_weight(gathered):
    return gathered.reshape(N_DEV * gathered.shape[1], gathered.shape[2])


def _to_slabs(gfull):
    return gfull.reshape(N_DEV, gfull.shape[0] // N_DEV, gfull.shape[1])


def _pack_small(arrs):
    rows = []
    for a in arrs:
        flat = a.astype(F32).reshape(-1)
        pad = (-flat.shape[0]) % LANES
        rows.append(jnp.pad(flat, (0, pad)).reshape(-1, LANES))
    packed = jnp.concatenate(rows, axis=0)
    return jnp.pad(packed, ((0, (-packed.shape[0]) % 8), (0, 0)))


def _unpack_small(packed, shapes):
    out, off = [], 0
    for shp in shapes:
        n = math.prod(shp)
        rows = -(-n // LANES)
        out.append(packed[off:off + rows].reshape(-1)[:n].reshape(shp))
        off += rows
    return out


def _of(layer, *names):
    return tuple((layer, n) for n in names)


MIXER_W = ("w_in", "w_up_a", "w_up_b", "w_o")
MLP_W = ("w_ff1", "w_ff2", "w_pe", "w_pg")

GATHERS = (
    ("rms_mix_l0", _of(0, "w_in")),
    ("proj_qkv_l0", _of(0, "w_up_a", "w_up_b", "w_o")),
    ("proj_gate_l0", _of(0, "w_pe", "w_pg")),
    ("sb_fwd_l0", _of(0, "w_ff1", "w_ff2")),
    ("swa_fwd_l0", _of(1, "w_in")),
    ("sb_fwd_l1", _of(1, "w_up_a", "w_up_b", "w_o", "w_pe", "w_pg", "w_ff1")),
    ("swa_fwd_l1", _of(1, "w_ff2")),
)
REDUCES = (
    (_of(1, "w_ff1"), "dw_ff2_l0", "dact_l0"),
    (_of(1, "w_ff2"), "dw_ff2_l0", "dw_ff1_l0"),
    (_of(1, "w_in"), "dw_ff2_l0", "sb_bwd_l0"),
    (_of(1, "w_up_a", "w_up_b", "w_o", "w_pe", "w_pg"), "dw_ff2_l0", "swa_bwd_l0"),
    (_of(0, *MLP_W), "dh_mlp_l0", "sb_bwd_l0"),
    (_of(0, "w_o", "w_up_a", "w_up_b"), "do_a_l0", "sb_bwd_l0"),
    (_of(0, "w_in"), "dh_ga_l0", "dh_qkv_l0"),
)


def _merge_comms(comms):
    if len(comms) == 1:
        return comms[0]

    def cuts(counts):
        edges = [0]
        for c in counts:
            edges.append(edges[-1] + c)
        return [slice(a, b) for a, b in zip(edges[:-1], edges[1:])]

    s_in = cuts([len(c.inputs) for c in comms])
    s_out = cuts([len(c.out_shapes) for c in comms])
    s_sem = cuts([len(c.sems) for c in comms])

    def start(pos, cin, cout, csem):
        for c, i, o, s in zip(comms, s_in, s_out, s_sem):
            c.start(pos, cin[i], cout[o], csem[s])

    def finish(pos, cin, cout, csem):
        for c, i, o, s in zip(comms, s_in, s_out, s_sem):
            c.finish(pos, cin[i], cout[o], csem[s])

    return _Comm(sum([c.inputs for c in comms], []), sum([c.out_shapes for c in comms], []),
                 sum([c.sems for c in comms], []), start, finish)


class _LayerWeights:
    def __init__(self, full, layer):
        self.full, self.layer, self.cache = full, layer, {}

    def __getitem__(self, name):
        if name not in self.cache:
            if name == "w_qkv_t":
                self.cache[name] = self.full[(self.layer, "w_in")][:QKV_COLS]
            elif name == "w_gate_t":
                self.cache[name] = self.full[(self.layer, "w_in")][QKV_COLS:]
            else:
                base = name[:-2] if name.endswith("_t") else name
                assert (base in COL_SHARDED) == name.endswith("_t"), name
                self.cache[name] = self.full[(self.layer, base)]
        return self.cache[name]


class _Plan:
    def __init__(self, w_sh, pos):
        self.w_sh = dict(zip(WEIGHTS, w_sh))
        self.pos = pos
        self.full, self.gw, self.parts, self.recv = {}, {}, {}, {}
        self.slabs = {}
        self.hosted = {}
        for i, (host, _) in enumerate(GATHERS):
            if host is not None:
                self.hosted.setdefault(host, []).append(("gather", i))
        for i, (_, sib_host, chip_host) in enumerate(REDUCES):
            assert (sib_host is None) == (chip_host is None)
            if sib_host is not None:
                self.hosted.setdefault(sib_host, []).append(("sibling", i))
                self.hosted.setdefault(chip_host, []).append(("chips", i))

    def _gather(self, i):
        return _gather_comm([_wire_shard(n, self.w_sh[n][layer]) for layer, n in GATHERS[i][1]])

    def _gathered(self, i, outs):
        for (layer, n), g in zip(GATHERS[i][1], outs):
            self.full[(layer, n)] = _full_weight(g)

    def weights(self, layer):
        for i, (host, items) in enumerate(GATHERS):
            if host is None and items[0][0] == layer:
                self._gathered(i, _run_comm(self._gather(i), f"gather_{i}"))
        return _LayerWeights(self.full, layer)

    def grad(self, layer, name, value):
        self.gw[(layer, name)] = value

    def _sibling(self, i):
        self.slabs[i] = [_to_slabs(self.gw[item]) for item in REDUCES[i][0]]
        return _rs_sibling_comm(self.slabs[i])

    def _sibling_done(self, i, outs):
        parts = _chip_partials(self.pos, self.slabs[i], outs, f"chip_partials_{i}")
        for item, part in zip(REDUCES[i][0], parts):
            self.parts[item] = part

    def _chips(self, i):
        return _rs_chips_comm([self.parts[item] for item in REDUCES[i][0]])

    def _chips_done(self, i, outs):
        for item, r in zip(REDUCES[i][0], outs):
            self.recv[item] = r

    def layer_done(self, layer):
        for i, (items, sib_host, _) in enumerate(REDUCES):
            if sib_host is None and items[0][0] == layer:
                self._sibling_done(i, _run_comm(self._sibling(i), f"reduce_sibling_{i}"))
                self._chips_done(i, _run_comm(self._chips(i), f"reduce_chips_{i}"))

    def comm(self, name):
        if name not in self.hosted:
            return None
        make = {"gather": self._gather, "sibling": self._sibling, "chips": self._chips}
        return _merge_comms([make[kind](i) for kind, i in self.hosted[name]])

    def done(self, name, outs):
        took = {"gather": self._gathered, "sibling": self._sibling_done, "chips": self._chips_done}
        off = 0
        for kind, i in self.hosted[name]:
            n = len(GATHERS[i][1]) if kind == "gather" else len(REDUCES[i][0])
            took[kind](i, outs[off:off + n])
            off += n


def kernel(x, p, w_in, w_up_a, w_up_b, w_o, w_ff1, w_ff2, w_pe, w_pg, g_mix, g_mlp, g_pe, g_final, sinks, rel_bias, loss_target, m_w_in, m_w_up_a, m_w_up_b, m_w_o, m_w_ff1, m_w_ff2, m_w_pe, m_w_pg, m_g_mix, m_g_mlp, m_g_pe, m_g_final, m_sinks, m_rel_bias, v_w_in, v_w_up_a, v_w_up_b, v_w_o, v_w_ff1, v_w_ff2, v_w_pe, v_w_pg, v_g_mix, v_g_mlp, v_g_pe, v_g_final, v_sinks, v_rel_bias):
    w_sh = [w_in, w_up_a, w_up_b, w_o, w_ff1, w_ff2, w_pe, w_pg]
    m_sh = [m_w_in, m_w_up_a, m_w_up_b, m_w_o, m_w_ff1, m_w_ff2, m_w_pe, m_w_pg]
    v_sh = [v_w_in, v_w_up_a, v_w_up_b, v_w_o, v_w_ff1, v_w_ff2, v_w_pe, v_w_pg]
    depth = w_in.shape[0]
    assert depth == 2 and x.shape[-1] * 2 + QKV_COLS == w_in.shape[2] * N_DEV

    px, py, pc = _position()
    plan = _Plan(w_sh, jnp.stack([px, py, pc]).astype(jnp.int32))
    loss_row, grad_x, _, small = _local_step(
        x[0], p[:, 0], loss_target[0], plan.weights, g_mix, g_mlp, g_pe, g_final, sinks, rel_bias, plan=plan)

    grad_w, delta_w, new_m, new_v = [], [], [], []
    for a, name in enumerate(WEIGHTS):
        parts = [plan.parts[(l, name)] for l in range(depth)]
        recvs = [plan.recv[(l, name)] for l in range(depth)]
        if name == "w_in":
            flip = lambda t: t.transpose(0, 2, 1)
            outs = [flip(o) for o in _adamw_weight(parts, recvs, flip(w_sh[a]), flip(m_sh[a]), flip(v_sh[a]),
                                                   f"adamw_{name}")]
        else:
            outs = _adamw_weight(parts, recvs, w_sh[a], m_sh[a], v_sh[a], f"adamw_{name}",
                                 grad_t=name in COL_SHARDED)
        for lst, o in zip((grad_w, delta_w, new_m, new_v), outs):
            lst.append(o)

    small_w = [g_mix, g_mlp, g_pe, g_final, sinks, rel_bias]
    small_m = [m_g_mix, m_g_mlp, m_g_pe, m_g_final, m_sinks, m_rel_bias]
    small_v = [v_g_mix, v_g_mlp, v_g_pe, v_g_final, v_sinks, v_rel_bias]
    small_shapes = [a.shape for a in small_w] + [(1,)]
    zero = jnp.zeros((1,), F32)
    small_g = _pack_small([small[n] for n in SMALL] + [loss_row[0, :1]])
    small_all = _run_comm(_gather_comm([small_g]), "gather_small")[0]
    packed_s = _adamw_replicated(small_all, _pack_small(small_w + [zero]), _pack_small(small_m + [zero]),
                                 _pack_small(small_v + [zero + 1.0]), "adamw_replicated")
    sg, sd, sm, sv = [_unpack_small(t, small_shapes) for t in packed_s]
    loss = sg[-1][0]

    return (loss, grad_x[None], *grad_w, *sg[:-1], *delta_w, *sd[:-1], *new_m, *sm[:-1], *new_v, *sv[:-1])
```

```python
import functools
import math

import numpy as np
import jax
import jax.numpy as jnp
from jax import lax
from jax.experimental import pallas as pl
from jax.experimental.pallas import tpu as pltpu

F32 = jnp.float32
MXU = jnp.bfloat16
WIRE = jnp.bfloat16

HEAD_DIM = 64
SB_HEADS = 8
SW_HEADS = 8
SW_KV = 2
SW_GROUP = SW_HEADS // SW_KV
BLOCK = 128
N_BUCKETS = 32
MAX_DISTANCE = 128
EPS = 1e-6
SCALE = HEAD_DIM ** -0.5
SB_W = SB_HEADS * HEAD_DIM
SW_W = SW_HEADS * HEAD_DIM
QKV_COLS = 3 * SB_W + SW_W + 2 * SW_KV * HEAD_DIM
N_DEV = 8
LANES = 128
N_PAIR = SB_HEADS // 2
NEG = -1e30

ADAM_LR = 0.001
ADAM_B1 = 0.9
ADAM_B2 = 0.999
ADAM_EPS = 1e-08
ADAM_WD = 0.01
ADAM_STEP = 10

VMEM_LIMIT = 48 * 1024 * 1024
SB_TQ = 256
SB_DEAD = -105.0

WEIGHTS = ("w_in", "w_up_a", "w_up_b", "w_o", "w_ff1", "w_ff2", "w_pe", "w_pg")
COL_SHARDED = ("w_in", "w_up_a", "w_up_b", "w_ff1", "w_pe")
SMALL = ("g_mix", "g_mlp", "g_pe", "g_final", "sinks", "rel_bias")


def _cparams(**kw):
    return pltpu.CompilerParams(vmem_limit_bytes=VMEM_LIMIT, **kw)


def _dot(a, b):
    return jnp.dot(a, b, preferred_element_type=F32)


def _dot_nt(a, b):
    return lax.dot_general(a, b, (((1,), (1,)), ((), ())), preferred_element_type=F32)


def _dot_tn(a, b):
    return lax.dot_general(a, b, (((0,), (0,)), ((), ())), preferred_element_type=F32)


def _tile(n, target, unit=LANES):
    if n <= target:
        return n
    t = (target // unit) * unit
    while t > unit and n % t:
        t -= unit
    assert n % t == 0, (n, target)
    return t


def _sigmoid(x):
    return 1.0 / (1.0 + jnp.exp(-x))


class _Comm:
    def __init__(self, inputs, out_shapes, sems, start, finish):
        self.inputs, self.out_shapes, self.sems = list(inputs), list(out_shapes), list(sems)
        self.start, self.finish = start, finish


def _call(body, *, grid, in_specs, out_specs, out_shape, scratch_shapes=(), args, name, comm=None):
    n_in, n_out, n_scr = len(in_specs), len(out_shape), len(scratch_shapes)
    if comm is None:
        outs = pl.pallas_call(body, grid=grid, in_specs=list(in_specs), out_specs=list(out_specs),
                              out_shape=list(out_shape), scratch_shapes=list(scratch_shapes),
                              compiler_params=_cparams(), name=name)(*args)
        return list(outs), None
    ci, co = len(comm.inputs), len(comm.out_shapes)
    any_spec = pl.BlockSpec(memory_space=pl.ANY)

    def wrapped(*refs):
        ins, cin = refs[:n_in], refs[n_in:n_in + ci]
        o0 = n_in + ci
        outs, cout = refs[o0:o0 + n_out], refs[o0 + n_out:o0 + n_out + co]
        s0 = o0 + n_out + co
        scr, csem = refs[s0:s0 + n_scr], refs[s0 + n_scr:]
        ids = [pl.program_id(d) for d in range(len(grid))]
        first = functools.reduce(jnp.logical_and, [i == 0 for i in ids])
        last = functools.reduce(jnp.logical_and, [i == g - 1 for i, g in zip(ids, grid)])
        pos = (lax.axis_index("x"), lax.axis_index("y"), lax.axis_index("c"))

        @pl.when(first)
        def _():
            comm.start(pos, cin, cout, csem)

        body(*ins, *outs, *scr)

        @pl.when(last)
        def _():
            comm.finish(pos, cin, cout, csem)

    outs = pl.pallas_call(wrapped, grid=grid, in_specs=list(in_specs) + [any_spec] * ci,
                          out_specs=list(out_specs) + [any_spec] * co,
                          out_shape=list(out_shape) + comm.out_shapes,
                          scratch_shapes=list(scratch_shapes) + comm.sems,
                          compiler_params=_cparams(), name=name)(*args, *comm.inputs)
    return list(outs[:n_out]), list(outs[n_out:])


def _accumulate(o_ref, value, first):
    @pl.when(first)
    def _():
        o_ref[...] = value

    @pl.when(jnp.logical_not(first))
    def _():
        o_ref[...] += value


def _mm(a, b, *, ta=False, tb=False, extras=(), epi=None, out_dtypes=(F32,),
        tm=1024, tn=1024, tk=1024, name, comm=None):
    if ta:
        kdim, m = a.shape
    else:
        m, kdim = a.shape
    n = b.shape[0] if tb else b.shape[1]
    assert (b.shape[1] if tb else b.shape[0]) == kdim
    tm, tn, tk = _tile(m, tm), _tile(n, tn), _tile(kdim, tk)
    nk = kdim // tk
    n_ex, n_out = len(extras), len(out_dtypes)

    a_spec = (pl.BlockSpec((tk, tm), lambda i, j, k: (k, i)) if ta
              else pl.BlockSpec((tm, tk), lambda i, j, k: (i, k)))
    b_spec = (pl.BlockSpec((tn, tk), lambda i, j, k: (j, k)) if tb
              else pl.BlockSpec((tk, tn), lambda i, j, k: (k, j)))
    ex_specs = []
    for e in extras:
        assert e.shape in ((m, n), (1, n), (m, 1)), (e.shape, m, n)
        if e.shape == (m, n):
            ex_specs.append(pl.BlockSpec((tm, tn), lambda i, j, k: (i, j)))
        elif e.shape[0] == 1:
            ex_specs.append(pl.BlockSpec((1, tn), lambda i, j, k: (0, j)))
        else:
            ex_specs.append(pl.BlockSpec((tm, 1), lambda i, j, k: (i, 0)))
    out_specs, out_shape, row_sums = [], [], []
    for dt in out_dtypes:
        kind = dt[1] if isinstance(dt, tuple) else "tile"
        row_sums.append(kind == "rowsum")
        if kind == "tile":
            out_specs.append(pl.BlockSpec((tm, tn), lambda i, j, k: (i, j)))
            out_shape.append(jax.ShapeDtypeStruct((m, n), dt))
            continue
        assert tn == n, "per-row and summed outputs need whole rows in one tile"
        if kind == "col":
            out_specs.append(pl.BlockSpec((tm, 1), lambda i, j, k: (i, 0)))
            out_shape.append(jax.ShapeDtypeStruct((m, 1), dt[0]))
        else:
            out_specs.append(pl.BlockSpec((1, tn), lambda i, j, k: (0, 0)))
            out_shape.append(jax.ShapeDtypeStruct((1, n), dt[0]))

    def body(a_ref, b_ref, *rest):
        ex_refs = rest[:n_ex]
        out_refs = rest[n_ex:n_ex + n_out]
        acc = rest[-1]
        k = pl.program_id(2)
        first_rows = pl.program_id(0) == 0

        def prod():
            av = a_ref[...].astype(MXU)
            bv = b_ref[...].astype(MXU)
            return _dot_tn(av, bv) if ta else (_dot_nt(av, bv) if tb else _dot(av, bv))

        def finish(res):
            if epi is not None:
                res = epi(res, *[e[...] for e in ex_refs])
            if not isinstance(res, tuple):
                res = (res,)
            for o_ref, r, summed in zip(out_refs, res, row_sums):
                if summed:
                    _accumulate(o_ref, r.astype(o_ref.dtype), first_rows)
                else:
                    o_ref[...] = r.astype(o_ref.dtype)

        if nk == 1:
            finish(prod())
            return

        @pl.when(k == 0)
        def _():
            acc[...] = prod()

        @pl.when(jnp.logical_and(k > 0, k < nk - 1))
        def _():
            acc[...] += prod()

        @pl.when(k == nk - 1)
        def _():
            finish(acc[...] + prod())

    outs, couts = _call(
        body,
        grid=(m // tm, n // tn, nk),
        in_specs=[a_spec, b_spec] + ex_specs,
        out_specs=out_specs,
        out_shape=out_shape,
        scratch_shapes=[pltpu.VMEM((tm, tn), F32)],
        args=(a, b, *extras), name=name, comm=comm)
    res = outs[0] if n_out == 1 else tuple(outs)
    return res if comm is None else (res, couts)


def _rms_fwd(x, g, name, comm=None):
    s, d = x.shape
    tr = _tile(s, 256)

    def body(x_ref, g_ref, h_ref, r_ref):
        xf = x_ref[...]
        r = lax.rsqrt(jnp.mean(xf * xf, axis=-1, keepdims=True) + EPS)
        h_ref[...] = ((xf * r) * g_ref[...]).astype(h_ref.dtype)
        r_ref[...] = r

    outs, couts = _call(
        body,
        grid=(s // tr,),
        in_specs=[pl.BlockSpec((tr, d), lambda i: (i, 0)), pl.BlockSpec((1, d), lambda i: (0, 0))],
        out_specs=[pl.BlockSpec((tr, d), lambda i: (i, 0)), pl.BlockSpec((tr, 1), lambda i: (i, 0))],
        out_shape=[jax.ShapeDtypeStruct((s, d), MXU), jax.ShapeDtypeStruct((s, 1), F32)],
        args=(x, g), name=name, comm=comm)
    return tuple(outs) if comm is None else (tuple(outs), couts)


def _loss_head(x, g, target, name):
    s, d = x.shape
    tr = _tile(s, 256)

    def body(x_ref, g_ref, t_ref, loss_ref, dx_ref, dg_ref):
        @pl.when(pl.program_id(0) == 0)
        def _():
            dg_ref[...] = jnp.zeros_like(dg_ref)
            loss_ref[...] = jnp.zeros_like(loss_ref)

        xf = x_ref[...]
        gv = g_ref[...]
        r = lax.rsqrt(jnp.mean(xf * xf, axis=-1, keepdims=True) + EPS)
        xhat = xf * r
        err = xhat * gv - t_ref[...]
        loss_ref[...] += 0.5 * jnp.sum(jnp.mean(err * err, axis=-1, keepdims=True), axis=0, keepdims=True)
        dy = err * (1.0 / d)
        dxhat = dy * gv
        mean = jnp.mean(dxhat * xhat, axis=-1, keepdims=True)
        dx_ref[...] = r * (dxhat - xhat * mean)
        dg_ref[...] += jnp.sum(dy * xhat, axis=0, keepdims=True)

    row = pl.BlockSpec((tr, d), lambda i: (i, 0))
    vec = pl.BlockSpec((1, d), lambda i: (0, 0))
    return pl.pallas_call(
        body,
        grid=(s // tr,),
        in_specs=[row, vec, row],
        out_specs=[pl.BlockSpec((1, LANES), lambda i: (0, 0)), row, vec],
        out_shape=[jax.ShapeDtypeStruct((1, LANES), F32), jax.ShapeDtypeStruct((s, d), F32),
                   jax.ShapeDtypeStruct((1, d), F32)],
        compiler_params=_cparams(),
        name=name,
    )(x, g, target)


def _mix_fwd(oa, ob, wa_t, wb_t, gates, name):
    s, kd = oa.shape
    d = wa_t.shape[0]
    tm, tn = _tile(s, 1024), _tile(d, 512)
    nj = d // tn

    def body(oa_ref, ob_ref, wa_ref, wb_ref, ga_ref, gb_ref, out_ref):
        ya = _dot_nt(oa_ref[...], wa_ref[...])
        yb = _dot_nt(ob_ref[...], wb_ref[...])
        out_ref[...] = (_sigmoid(ga_ref[...].astype(F32)) * ya
                        + _sigmoid(gb_ref[...].astype(F32)) * yb).astype(out_ref.dtype)

    o_spec = pl.BlockSpec((tm, kd), lambda i, j: (i, 0))
    w_spec = pl.BlockSpec((tn, kd), lambda i, j: (j, 0))
    return pl.pallas_call(
        body,
        grid=(s // tm, nj),
        in_specs=[o_spec, o_spec, w_spec, w_spec,
                  pl.BlockSpec((tm, tn), lambda i, j: (i, j)),
                  pl.BlockSpec((tm, tn), lambda i, j: (i, j + nj))],
        out_specs=pl.BlockSpec((tm, tn), lambda i, j: (i, j)),
        out_shape=jax.ShapeDtypeStruct((s, d), MXU),
        compiler_params=_cparams(),
        name=name,
    )(oa, ob, wa_t, wb_t, gates, gates)


def _mix_bwd(dx, w_o, oa, ob, wa_t, wb_t, gates, name):
    s, kd = oa.shape
    d = wa_t.shape[0]
    tm, tn = _tile(s, 1024), _tile(d, 512)
    nj = d // tn

    def body(dx_ref, wo_ref, oa_ref, ob_ref, wa_ref, wb_ref, ga_ref, gb_ref,
             dya_ref, dyb_ref, dga_ref, dgb_ref):
        dm = _dot_nt(dx_ref[...], wo_ref[...])
        ya = _dot_nt(oa_ref[...], wa_ref[...])
        yb = _dot_nt(ob_ref[...], wb_ref[...])
        sa = _sigmoid(ga_ref[...].astype(F32))
        sb = _sigmoid(gb_ref[...].astype(F32))
        dya_ref[...] = (dm * sa).astype(dya_ref.dtype)
        dyb_ref[...] = (dm * sb).astype(dyb_ref.dtype)
        dga_ref[...] = (dm * ya * sa * (1.0 - sa)).astype(dga_ref.dtype)
        dgb_ref[...] = (dm * yb * sb * (1.0 - sb)).astype(dgb_ref.dtype)

    o_spec = pl.BlockSpec((tm, kd), lambda i, j: (i, 0))
    w_spec = pl.BlockSpec((tn, kd), lambda i, j: (j, 0))
    t_spec = pl.BlockSpec((tm, tn), lambda i, j: (i, j))
    return pl.pallas_call(
        body,
        grid=(s // tm, nj),
        in_specs=[pl.BlockSpec((tm, d), lambda i, j: (i, 0)),
                  pl.BlockSpec((tn, d), lambda i, j: (j, 0)),
                  o_spec, o_spec, w_spec, w_spec, t_spec,
                  pl.BlockSpec((tm, tn), lambda i, j: (i, j + nj))],
        out_specs=[t_spec] * 4,
        out_shape=[jax.ShapeDtypeStruct((s, d), MXU)] * 4,
        compiler_params=_cparams(),
        name=name,
    )(dx, w_o, oa, ob, wa_t, wb_t, gates, gates)


def _ple(p, w_pe_t, h, w_pg, other, *, backward, name):
    s, kp = p.shape
    d = w_pe_t.shape[0]
    tm, tn = _tile(s, 1024), _tile(d, 512)

    def body(p_ref, wpe_ref, h_ref, wpg_ref, other_ref, *out_refs):
        pe = _dot_nt(p_ref[...].astype(MXU), wpe_ref[...])
        gt = _dot(h_ref[...], wpg_ref[...])
        sg = _sigmoid(gt)
        if backward:
            dout = other_ref[...]
            out_refs[0][...] = (dout * sg).astype(out_refs[0].dtype)
            out_refs[1][...] = (dout * pe * sg * (1.0 - sg)).astype(out_refs[1].dtype)
        else:
            out_refs[0][...] = other_ref[...] + pe * sg

    t_spec = pl.BlockSpec((tm, tn), lambda i, j: (i, j))
    if backward:
        out_specs, out_shape = [t_spec, t_spec], [jax.ShapeDtypeStruct((s, d), MXU)] * 2
    else:
        out_specs, out_shape = [t_spec], [jax.ShapeDtypeStruct((s, d), F32)]
    outs = pl.pallas_call(
        body,
        grid=(s // tm, d // tn),
        in_specs=[pl.BlockSpec((tm, kp), lambda i, j: (i, 0)),
                  pl.BlockSpec((tn, kp), lambda i, j: (j, 0)),
                  pl.BlockSpec((tm, d), lambda i, j: (i, 0)),
                  pl.BlockSpec((d, tn), lambda i, j: (0, j)),
                  t_spec],
        out_specs=out_specs,
        out_shape=out_shape,
        compiler_params=_cparams(),
        name=name,
    )(p, w_pe_t, h, w_pg, other)
    return tuple(outs) if backward else outs[0]


def _split_dot(x, tri):
    hi = x.astype(jnp.bfloat16)
    lo = (x - hi.astype(F32)).astype(jnp.bfloat16)
    return _dot(hi, tri) + _dot(lo, tri)


def _log_sigmoids(z):
    lb = jnp.minimum(z, 0.0) - jnp.log1p(jnp.exp(-jnp.abs(z)))
    return lb, lb - z


def _head_lanes(hh):
    lane = lax.broadcasted_iota(jnp.int32, (1, LANES), 1)
    return jnp.logical_and(lane >= hh * HEAD_DIM, lane < (hh + 1) * HEAD_DIM)


def _sb_fwd(qkv, name, comm=None):
    s = qkv.shape[0]
    tq = _tile(s, SB_TQ)

    def body(q_ref, k_ref, v_ref, o_ref):
        i = pl.program_id(1)
        qf = q_ref[...].astype(F32) * SCALE
        row = lax.broadcasted_iota(jnp.int32, (tq, tq), 0)
        col = lax.broadcasted_iota(jnp.int32, (tq, tq), 1)
        causal = col < row
        tri = jnp.where(row > col, 1.0, 0.0).astype(jnp.bfloat16)

        qms = [jnp.where(_head_lanes(hh), qf, 0.0).astype(MXU) for hh in range(2)]

        def block(kb, cs, accs, masked, gate=None):
            rows = pl.ds(pl.multiple_of(kb * tq, tq), tq)
            ks, vs = k_ref[rows, :], v_ref[rows, :]
            new_c, new_acc = [], []
            for hh in range(2):
                lb, lm = _log_sigmoids(_dot_nt(qms[hh], ks))
                if masked:
                    lm = jnp.where(causal, lm, 0.0)
                if gate is not None:
                    lm = lm * gate
                a = jnp.exp(lb + _split_dot(lm, tri) + cs[hh])
                if masked:
                    a = jnp.where(causal, a, 0.0)
                if gate is not None:
                    a = a * gate
                new_acc.append(accs[hh] + _dot(a.astype(MXU), vs))
                new_c.append(cs[hh] + jnp.sum(lm, axis=1, keepdims=True))
            return tuple(new_c), tuple(new_acc)

        def top(cs):
            return jnp.maximum(jnp.max(cs[0]), jnp.max(cs[1]))

        zc, za = jnp.zeros((tq, 1), F32), jnp.zeros((tq, LANES), F32)
        cs, accs = block(i, (zc, zc), (za, za), True)
        cs, accs = block(jnp.maximum(i - 1, 0), cs, accs, False, jnp.where(i > 0, 1.0, 0.0))

        def live(st):
            return jnp.logical_and(st[0] >= 0, st[1] > SB_DEAD)

        def walk(st):
            cs, accs = block(st[0], st[2], st[3], False)
            return st[0] - 1, top(cs), cs, accs

        accs = lax.while_loop(live, walk, (i - 2, top(cs), cs, accs))[3]
        o_ref[...] = jnp.where(_head_lanes(0), accs[0], accs[1]).astype(o_ref.dtype)

    outs, couts = _call(
        body,
        grid=(N_PAIR, s // tq),
        in_specs=[pl.BlockSpec((tq, LANES), lambda p, i: (i, p)),
                  pl.BlockSpec((s, LANES), lambda p, i: (0, N_PAIR + p)),
                  pl.BlockSpec((s, LANES), lambda p, i: (0, 2 * N_PAIR + p))],
        out_specs=[pl.BlockSpec((tq, LANES), lambda p, i: (i, p))],
        out_shape=[jax.ShapeDtypeStruct((s, SB_W), MXU)],
        args=(qkv, qkv, qkv), name=name, comm=comm)
    return outs[0] if comm is None else (outs[0], couts)


def _sb_bwd(qkv, do, name, comm=None):
    s = qkv.shape[0]
    tq = _tile(s, SB_TQ)
    nq = s // tq

    def body(q_ref, k_ref, v_ref, do_ref, dq_ref, dk_ref, dv_ref, dk_acc, dv_acc, carries):
        i = pl.program_id(1)

        @pl.when(i == 0)
        def _():
            dk_acc[...] = jnp.zeros_like(dk_acc)
            dv_acc[...] = jnp.zeros_like(dv_acc)

        qf = q_ref[...].astype(F32) * SCALE
        dof = do_ref[...]
        row = lax.broadcasted_iota(jnp.int32, (tq, tq), 0)
        col = lax.broadcasted_iota(jnp.int32, (tq, tq), 1)
        causal = col < row
        tri_rev = jnp.where(row > col, 1.0, 0.0).astype(jnp.bfloat16)
        tri_excl = jnp.where(row < col, 1.0, 0.0).astype(jnp.bfloat16)

        qms = [jnp.where(_head_lanes(hh), qf, 0.0).astype(MXU) for hh in range(2)]
        doms = [jnp.where(_head_lanes(hh), dof, jnp.zeros_like(dof)) for hh in range(2)]

        def terms(kb, masked):
            rows = pl.ds(pl.multiple_of(kb * tq, tq), tq)
            ks = k_ref[rows, :]
            out = []
            for hh in range(2):
                lb, lm = _log_sigmoids(_dot_nt(qms[hh], ks))
                if masked:
                    lm = jnp.where(causal, lm, 0.0)
                out.append((lb, lm))
            return out

        def row_sums(kb, masked, pre=None):
            pre = terms(kb, masked) if pre is None else pre
            return [jnp.sum(lm, axis=1, keepdims=True) for _, lm in pre]

        def top(cs):
            return jnp.maximum(jnp.max(cs[0]), jnp.max(cs[1]))

        def live(st):
            return jnp.logical_and(st[0] >= 0, st[1] > SB_DEAD)

        def record(st):
            kb, cs = st[0], st[2]
            sums = row_sums(kb, False)
            for hh in range(2):
                carries[hh, kb] = cs[hh]
            cs = tuple(cs[hh] + sums[hh] for hh in range(2))
            return kb - 1, top(cs), cs

        prev = jnp.maximum(i - 1, 0)
        gate = jnp.where(i > 0, 1.0, 0.0)
        t_diag, t_prev = terms(i, True), terms(prev, False)
        c_diag = row_sums(i, True, t_diag)
        sums = row_sums(prev, False, t_prev)
        c_prev = tuple(c_diag[hh] + sums[hh] * gate for hh in range(2))
        first = lax.while_loop(live, record, (i - 2, top(c_prev), c_prev))[0] + 1

        def block(kb, cs, gpres, dqs, masked, gate=None, pre=None):
            rows = pl.ds(pl.multiple_of(kb * tq, tq), tq)
            ks, vs = k_ref[rows, :], v_ref[rows, :]
            pre = terms(kb, masked) if pre is None else pre
            new_g, new_dq = [], []
            dk_add, dv_add = None, None
            for hh in range(2):
                lb, lm = pre[hh]
                a = jnp.exp(lb + _split_dot(lm, tri_rev) + cs[hh])
                if masked:
                    a = jnp.where(causal, a, 0.0)
                if gate is not None:
                    a = a * gate
                g = a * _dot_nt(doms[hh], vs)
                gsum = gpres[hh] + _split_dot(g, tri_excl)
                dz = g - (g + gsum) * jnp.exp(lb)
                if masked:
                    dz = jnp.where(causal, dz, 0.0)
                if gate is not None:
                    dz = dz * gate
                dzb = dz.astype(MXU)
                new_dq.append(dqs[hh] + _dot(dzb, ks))
                dk_h = _dot_tn(dzb, qms[hh])
                dv_h = _dot_tn(a.astype(MXU), doms[hh])
                dk_add = dk_h if dk_add is None else dk_add + dk_h
                dv_add = dv_h if dv_add is None else dv_add + dv_h
                new_g.append(gpres[hh] + jnp.sum(g, axis=1, keepdims=True))
            dk_acc[rows, :] += dk_add
            dv_acc[rows, :] += dv_add
            return tuple(new_g), tuple(new_dq)

        zc, za = jnp.zeros((tq, 1), F32), jnp.zeros((tq, LANES), F32)
        gpres, dqs = lax.fori_loop(
            first, i - 1, lambda kb, cr: block(kb, (carries[0, kb], carries[1, kb]), cr[0], cr[1], False),
            ((zc, zc), (za, za)))
        gpres, dqs = block(prev, c_diag, gpres, dqs, False, gate, t_prev)
        dqs = block(i, (zc, zc), gpres, dqs, True, None, t_diag)[1]
        dq_ref[...] = (jnp.where(_head_lanes(0), dqs[0], dqs[1]) * SCALE).astype(dq_ref.dtype)

        @pl.when(i == nq - 1)
        def _():
            dk_ref[...] = dk_acc[...].astype(dk_ref.dtype)
            dv_ref[...] = dv_acc[...].astype(dv_ref.dtype)

    blk = pl.BlockSpec((tq, LANES), lambda p, i: (i, p))
    full = pl.BlockSpec((s, LANES), lambda p, i: (0, p))
    outs, couts = _call(
        body,
        grid=(N_PAIR, nq),
        in_specs=[blk,
                  pl.BlockSpec((s, LANES), lambda p, i: (0, N_PAIR + p)),
                  pl.BlockSpec((s, LANES), lambda p, i: (0, 2 * N_PAIR + p)),
                  blk],
        out_specs=[blk, full, full],
        out_shape=[jax.ShapeDtypeStruct((s, SB_W), MXU)] * 3,
        scratch_shapes=[pltpu.VMEM((s, LANES), F32), pltpu.VMEM((s, LANES), F32),
                        pltpu.VMEM((2, nq, tq, 1), F32)],
        args=(qkv, qkv, qkv, do), name=name, comm=comm)
    return tuple(outs) if comm is None else (tuple(outs), couts)


def _bucket_table():
    i = np.arange(BLOCK)[:, None]
    j = np.arange(2 * BLOCK)[None, :]
    d = np.maximum(BLOCK + i - j, 0)
    max_exact = N_BUCKETS // 2
    df = np.maximum(d, 1).astype(np.float32)
    large = max_exact + (np.log(df / max_exact) / math.log(MAX_DISTANCE / max_exact)
                         * (N_BUCKETS - max_exact)).astype(np.int32)
    large = np.minimum(large, N_BUCKETS - 1)
    return np.where(d < max_exact, d, large).astype(np.int32)


def _build_bias(rel_bias, buckets, name):
    def body(rb_ref, bk_ref, out_ref):
        h = pl.program_id(0)
        bk = bk_ref[...]
        acc = jnp.zeros(bk.shape, F32)
        for b in range(N_BUCKETS):
            acc = jnp.where(bk == b, rb_ref[b, h], acc)
        out_ref[...] = acc

    return pl.pallas_call(
        body,
        grid=(SW_HEADS,),
        in_specs=[pl.BlockSpec(memory_space=pltpu.SMEM),
                  pl.BlockSpec((BLOCK, 2 * BLOCK), lambda h: (0, 0))],
        out_specs=pl.BlockSpec((None, BLOCK, 2 * BLOCK), lambda h: (h, 0, 0)),
        out_shape=jax.ShapeDtypeStruct((SW_HEADS, BLOCK, 2 * BLOCK), F32),
        name=name,
    )(rel_bias, buckets)


def _bias_grad(dbias_layers, buckets, name):
    n_l = len(dbias_layers)

    def body(*refs):
        bk = refs[n_l][...]
        out_ref = refs[n_l + 1]
        db = refs[0][...]
        for r in refs[1:n_l]:
            db = db + r[...]
        lane = lax.broadcasted_iota(jnp.int32, (1, LANES), 1)
        acc = jnp.zeros((1, LANES), F32)
        for b in range(N_BUCKETS):
            part = jnp.sum(jnp.where(bk == b, db, 0.0), axis=1, keepdims=True)
            tot = jnp.sum(part, axis=0, keepdims=True)
            acc = jnp.where(lane == b, tot, acc)
        out_ref[...] = acc

    hspec = pl.BlockSpec((None, BLOCK, 2 * BLOCK), lambda h: (h, 0, 0))
    return pl.pallas_call(
        body,
        grid=(SW_HEADS,),
        in_specs=[hspec] * n_l + [pl.BlockSpec((BLOCK, 2 * BLOCK), lambda h: (0, 0))],
        out_specs=pl.BlockSpec((None, 1, LANES), lambda h: (h, 0, 0)),
        out_shape=jax.ShapeDtypeStruct((SW_HEADS, 1, LANES), F32),
        name=name,
    )(*dbias_layers, buckets)


GROUP_ROWS = SW_GROUP * BLOCK


def _group_lanes(g):
    lane = lax.broadcasted_iota(jnp.int32, (1, LANES), 1)
    gvec = jnp.zeros((1, LANES), jnp.int32) + g
    return jnp.where(lane >= HEAD_DIM, 1, 0) == gvec, gvec


def _stack_heads(x, g):
    kv_lanes, gvec = _group_lanes(g)
    parts = []
    for j in range(SW_GROUP):
        half = x[:, (j // 2) * LANES:(j // 2 + 1) * LANES]
        moved = jnp.where(gvec == j % 2, half, pltpu.roll(half, HEAD_DIM, 1))
        parts.append(jnp.where(kv_lanes, moved, 0.0))
    return jnp.concatenate(parts, axis=0)


def _unstack_heads(y, g):
    _, gvec = _group_lanes(g)
    heads = []
    for j in range(SW_GROUP):
        yj = y[j * BLOCK:(j + 1) * BLOCK]
        heads.append(jnp.where(gvec == j % 2, yj, pltpu.roll(yj, HEAD_DIM, 1)))
    pairs = [jnp.where(_head_lanes(0), heads[2 * p], heads[2 * p + 1]) for p in range(SW_GROUP // 2)]
    return jnp.concatenate(pairs, axis=1)


def _per_head_col(values):
    return jnp.concatenate([jnp.zeros((BLOCK, 1), F32) + v for v in values], axis=0)


def _swa_scores(qs, kp, kc, bias_ref, n):
    row = jnp.bitwise_and(lax.broadcasted_iota(jnp.int32, (GROUP_ROWS, BLOCK), 0), BLOCK - 1)
    col = lax.broadcasted_iota(jnp.int32, (GROUP_ROWS, BLOCK), 1)
    bias = bias_ref[...].reshape(GROUP_ROWS, 2 * BLOCK)
    s1 = _dot_nt(qs, kp) + bias[:, :BLOCK]
    s2 = _dot_nt(qs, kc) + bias[:, BLOCK:]
    no_prev = jnp.where(n > 0, 0, BLOCK)
    s1 = jnp.where(col > row + no_prev, s1, NEG)
    s2 = jnp.where(col <= row, s2, NEG)
    return s1, s2


def _swa_specs(s):
    q_blk = 3 * SB_W // (2 * LANES)
    k_blk = (3 * SB_W + SW_W) // LANES
    return (pl.BlockSpec((s, 2 * LANES), lambda g: (0, q_blk + g)),
            pl.BlockSpec((s, LANES), lambda g: (0, k_blk)),
            pl.BlockSpec((s, LANES), lambda g: (0, k_blk + 1)))


def _swa_fwd(qkv, bias, sinks, name, comm=None):
    s = qkv.shape[0]
    nb = s // BLOCK

    def body(sink_ref, q_ref, k_ref, v_ref, bias_ref, o_ref, lse_ref):
        g = pl.program_id(0)
        sink = _per_head_col([sink_ref[SW_GROUP * g + j] for j in range(SW_GROUP)])
        lane = lax.broadcasted_iota(jnp.int32, (1, LANES), 1)

        def step(n, carry):
            r0 = pl.multiple_of(n * BLOCK, BLOCK)
            p0 = pl.multiple_of(jnp.maximum(n - 1, 0) * BLOCK, BLOCK)
            cur, prev = pl.ds(r0, BLOCK), pl.ds(p0, BLOCK)
            qs = _stack_heads(q_ref[cur, :].astype(F32) * SCALE, g).astype(MXU)
            s1, s2 = _swa_scores(qs, k_ref[prev, :], k_ref[cur, :], bias_ref, n)
            m = jnp.maximum(jnp.max(jnp.maximum(s1, s2), axis=1, keepdims=True), sink)
            e1 = jnp.exp(s1 - m)
            e2 = jnp.exp(s2 - m)
            den = jnp.sum(e1 + e2, axis=1, keepdims=True) + jnp.exp(sink - m)
            o = _dot((e1 / den).astype(MXU), v_ref[prev, :]) + _dot((e2 / den).astype(MXU), v_ref[cur, :])
            o_ref[cur, :] = _unstack_heads(o, g).astype(o_ref.dtype)
            lse = m + jnp.log(den)
            lse_row = jnp.zeros((BLOCK, LANES), F32)
            for j in range(SW_GROUP):
                lse_row = jnp.where(lane == j, lse[j * BLOCK:(j + 1) * BLOCK], lse_row)
            lse_ref[cur, :] = lse_row
            return carry

        lax.fori_loop(0, nb, step, 0, unroll=2)

    outs, couts = _call(
        body,
        grid=(SW_KV,),
        in_specs=[pl.BlockSpec(memory_space=pltpu.SMEM), *_swa_specs(s),
                  pl.BlockSpec((SW_GROUP, BLOCK, 2 * BLOCK), lambda g: (g, 0, 0))],
        out_specs=[pl.BlockSpec((s, 2 * LANES), lambda g: (0, g)),
                   pl.BlockSpec((None, s, LANES), lambda g: (g, 0, 0))],
        out_shape=[jax.ShapeDtypeStruct((s, SW_W), MXU), jax.ShapeDtypeStruct((SW_KV, s, LANES), F32)],
        args=(sinks, qkv, qkv, qkv, bias), name=name, comm=comm)
    return tuple(outs) if comm is None else (tuple(outs), couts)


def _swa_bwd(qkv, bias, sinks, do, lse, name, comm=None):
    s = qkv.shape[0]
    nb = s // BLOCK

    def body(sink_ref, q_ref, k_ref, v_ref, bias_ref, do_ref, lse_ref,
             dq_ref, dk_ref, dv_ref, dbias_ref, dsink_ref, dk_acc, dv_acc):
        g = pl.program_id(0)
        sink = _per_head_col([sink_ref[SW_GROUP * g + j] for j in range(SW_GROUP)])
        lane = lax.broadcasted_iota(jnp.int32, (1, LANES), 1)

        @pl.when(g == 0)
        def _():
            dk_acc[...] = jnp.zeros_like(dk_acc)
            dv_acc[...] = jnp.zeros_like(dv_acc)

        dbias_ref[...] = jnp.zeros_like(dbias_ref)

        def step(n, dsink_rows):
            r0 = pl.multiple_of(n * BLOCK, BLOCK)
            p0 = pl.multiple_of(jnp.maximum(n - 1, 0) * BLOCK, BLOCK)
            cur, prev = pl.ds(r0, BLOCK), pl.ds(p0, BLOCK)
            qs = _stack_heads(q_ref[cur, :].astype(F32) * SCALE, g).astype(MXU)
            dos = _stack_heads(do_ref[cur, :].astype(F32), g).astype(MXU)
            kp, kc, vp, vc = k_ref[prev, :], k_ref[cur, :], v_ref[prev, :], v_ref[cur, :]
            lse_row = lse_ref[cur, :]
            lse = jnp.concatenate([jnp.sum(jnp.where(lane == j, lse_row, 0.0), axis=1, keepdims=True)
                                   for j in range(SW_GROUP)], axis=0)
            s1, s2 = _swa_scores(qs, kp, kc, bias_ref, n)
            pr1 = jnp.exp(s1 - lse)
            pr2 = jnp.exp(s2 - lse)
            dpr1 = _dot_nt(dos, vp)
            dpr2 = _dot_nt(dos, vc)
            delta = jnp.sum(pr1 * dpr1 + pr2 * dpr2, axis=1, keepdims=True)
            ds1 = pr1 * (dpr1 - delta)
            ds2 = pr2 * (dpr2 - delta)
            dbias_ref[:, :, :BLOCK] += ds1.reshape(SW_GROUP, BLOCK, BLOCK)
            dbias_ref[:, :, BLOCK:] += ds2.reshape(SW_GROUP, BLOCK, BLOCK)
            ds1b, ds2b = ds1.astype(MXU), ds2.astype(MXU)
            dq = _dot(ds1b, kp) + _dot(ds2b, kc)
            dq_ref[cur, :] = (_unstack_heads(dq, g) * SCALE).astype(dq_ref.dtype)
            dk_acc[prev, :] += _dot_tn(ds1b, qs)
            dk_acc[cur, :] += _dot_tn(ds2b, qs)
            dv_acc[prev, :] += _dot_tn(pr1.astype(MXU), dos)
            dv_acc[cur, :] += _dot_tn(pr2.astype(MXU), dos)
            return dsink_rows - jnp.exp(sink - lse) * delta

        rows = lax.fori_loop(0, nb, step, jnp.zeros((GROUP_ROWS, 1), F32), unroll=2)
        for j in range(SW_GROUP):
            dsink_ref[j] = jnp.broadcast_to(jnp.sum(rows[j * BLOCK:(j + 1) * BLOCK], axis=0, keepdims=True),
                                            (1, LANES))

        @pl.when(g == SW_KV - 1)
        def _():
            dk_ref[...] = dk_acc[...].astype(dk_ref.dtype)
            dv_ref[...] = dv_acc[...].astype(dv_ref.dtype)

    grp = pl.BlockSpec((s, 2 * LANES), lambda g: (0, g))
    kv_out = pl.BlockSpec((s, LANES), lambda g: (0, 0))
    bspec = pl.BlockSpec((SW_GROUP, BLOCK, 2 * BLOCK), lambda g: (g, 0, 0))
    outs, couts = _call(
        body,
        grid=(SW_KV,),
        in_specs=[pl.BlockSpec(memory_space=pltpu.SMEM), *_swa_specs(s), bspec, grp,
                  pl.BlockSpec((None, s, LANES), lambda g: (g, 0, 0))],
        out_specs=[grp, kv_out, kv_out, bspec, pl.BlockSpec((SW_GROUP, 1, LANES), lambda g: (g, 0, 0))],
        out_shape=[jax.ShapeDtypeStruct((s, SW_W), MXU),
                   jax.ShapeDtypeStruct((s, LANES), MXU),
                   jax.ShapeDtypeStruct((s, LANES), MXU),
                   jax.ShapeDtypeStruct((SW_HEADS, BLOCK, 2 * BLOCK), F32),
                   jax.ShapeDtypeStruct((SW_HEADS, 1, LANES), F32)],
        scratch_shapes=[pltpu.VMEM((s, LANES), F32), pltpu.VMEM((s, LANES), F32)],
        args=(sinks, qkv, qkv, qkv, bias, do, lse), name=name, comm=comm)
    return tuple(outs) if comm is None else (tuple(outs), couts)


class _NoPlan:
    def comm(self, name):
        return None

    def done(self, name, outs):
        pass

    def grad(self, layer, name, value):
        pass

    def layer_done(self, layer):
        pass


def _run(plan, fn, *args, name, **kw):
    comm = plan.comm(name)
    if comm is None:
        return fn(*args, name=name, **kw)
    res, outs = fn(*args, name=name, comm=comm, **kw)
    plan.done(name, outs)
    return res


def _norm_bwd(dh, x, dres, r, g):
    xhat = x * r
    dxhat = dh * g
    dx = dres + r * (dxhat - xhat * jnp.mean(dxhat * xhat, axis=-1, keepdims=True))
    return dx, dx, jnp.sum(dh * xhat, axis=0, keepdims=True)


def _residual_norm(acc, res, g):
    x = res + acc
    r = lax.rsqrt(jnp.mean(x * x, axis=-1, keepdims=True) + EPS)
    return x, (x * r) * g, r


def _layer_fwd(x, p, w, g_mix, g_mlp, g_pe, sinks, bias, tag, plan):
    h1, r1 = _run(plan, _rms_fwd, x, g_mix, name=f"rms_mix_{tag}")
    qkv = _run(plan, _mm, h1, w["w_qkv_t"], tb=True, out_dtypes=(MXU,), name=f"proj_qkv_{tag}")
    gates = _run(plan, _mm, h1, w["w_gate_t"], tb=True, out_dtypes=(MXU,), name=f"proj_gate_{tag}")
    oa = _run(plan, _sb_fwd, qkv, name=f"sb_fwd_{tag}")
    ob, lse = _run(plan, _swa_fwd, qkv, bias, sinks, name=f"swa_fwd_{tag}")
    merged = _mix_fwd(oa, ob, w["w_up_a_t"], w["w_up_b_t"], gates, f"mix_fwd_{tag}")
    d = x.shape[1]
    normed = (F32, MXU, (F32, "col"))
    x1, h2, r2 = _run(plan, _mm, merged, w["w_o"], extras=(x, g_mlp), epi=_residual_norm, out_dtypes=normed,
                      tn=d, name=f"out_proj_{tag}")
    u, act = _run(plan, _mm, h2, w["w_ff1_t"], tb=True,
                  epi=lambda acc: (acc, jnp.square(jnp.maximum(acc, 0.0))),
                  out_dtypes=(MXU, MXU), name=f"ff1_{tag}")
    x2, h3, r3 = _run(plan, _mm, act, w["w_ff2"], extras=(x1, g_pe), epi=_residual_norm, out_dtypes=normed,
                      tn=d, name=f"ff2_{tag}")
    x3 = _ple(p, w["w_pe_t"], h3, w["w_pg"], x2, backward=False, name=f"ple_fwd_{tag}")
    saved = dict(x=x, h1=h1, r1=r1, gates=gates, qkv=qkv, lse=lse, oa=oa, ob=ob, merged=merged,
                 x1=x1, h2=h2, r2=r2, u=u, act=act, x2=x2, h3=h3, r3=r3)
    return x3, saved


def _layer_bwd(dx3, sv, p, w, g_mix, g_mlp, g_pe, sinks, bias, layer, plan):
    tag = f"l{layer}"
    gw = {}
    wire = (WIRE,)

    def dw(name, a, b):
        gw[name] = _run(plan, _mm, a, b, ta=True, out_dtypes=wire, tk=2048, name=f"d{name}_{tag}")
        plan.grad(layer, name, gw[name])

    dpe, dgt = _ple(p, w["w_pe_t"], sv["h3"], w["w_pg"], dx3, backward=True, name=f"ple_bwd_{tag}")
    dw("w_pe", dpe, p)
    dw("w_pg", sv["h3"], dgt)
    d = dx3.shape[1]
    grads = (F32, MXU, (F32, "rowsum"))
    dx2, dx2b, dg_pe = _run(plan, _mm, dgt, w["w_pg"], tb=True, extras=(sv["x2"], dx3, sv["r3"], g_pe),
                            epi=_norm_bwd, out_dtypes=grads, tm=512, tn=d, name=f"dh_pe_{tag}")
    dw("w_ff2", sv["act"], dx2b)
    du = _run(plan, _mm, dx2b, w["w_ff2"], tb=True, extras=(sv["u"],),
              epi=lambda acc, u: acc * (2.0 * jnp.maximum(u.astype(F32), 0.0)), out_dtypes=(MXU,),
              name=f"dact_{tag}")
    dw("w_ff1", du, sv["h2"])
    dx1, dx1b, dg_mlp = _run(plan, _mm, du, w["w_ff1_t"], extras=(sv["x1"], dx2, sv["r2"], g_mlp),
                             epi=_norm_bwd, out_dtypes=grads, tm=1024, tn=d, name=f"dh_mlp_{tag}")
    dw("w_o", sv["merged"], dx1b)
    dya, dyb, dga, dgb = _mix_bwd(dx1b, w["w_o"], sv["oa"], sv["ob"], w["w_up_a_t"], w["w_up_b_t"],
                                  sv["gates"], f"mix_bwd_{tag}")
    dw("w_up_a", dya, sv["oa"])
    dw("w_up_b", dyb, sv["ob"])
    doa = _run(plan, _mm, dya, w["w_up_a_t"], out_dtypes=(MXU,), name=f"do_a_{tag}")
    dob = _run(plan, _mm, dyb, w["w_up_b_t"], out_dtypes=(MXU,), name=f"do_b_{tag}")
    dqb, dkb, dvb, dbias, dsink = _run(plan, _swa_bwd, sv["qkv"], bias, sinks, dob, sv["lse"],
                                       name=f"swa_bwd_{tag}")
    dqa, dka, dva = _run(plan, _sb_bwd, sv["qkv"], doa, name=f"sb_bwd_{tag}")
    dqkv = jnp.concatenate([dqa, dka, dva, dqb, dkb, dvb], axis=1)
    gw_qkv = _mm(dqkv, sv["h1"], ta=True, out_dtypes=wire, tk=2048, name=f"dw_qkv_{tag}")
    gw_ga = _mm(dga, sv["h1"], ta=True, out_dtypes=wire, tk=2048, name=f"dw_ga_{tag}")
    gw_gb = _mm(dgb, sv["h1"], ta=True, out_dtypes=wire, tk=2048, name=f"dw_gb_{tag}")
    gw["w_in"] = jnp.concatenate([gw_qkv, gw_ga, gw_gb], axis=0)
    plan.grad(layer, "w_in", gw["w_in"])
    d = dga.shape[1]
    add = lambda acc, res: res + acc
    dh1 = _run(plan, _mm, dga, w["w_gate_t"][:d], name=f"dh_ga_{tag}")
    dh1 = _run(plan, _mm, dgb, w["w_gate_t"][d:], extras=(dh1,), epi=add, name=f"dh_gb_{tag}")
    dx, _, dg_mix = _run(plan, _mm, dqkv, w["w_qkv_t"], tk=768, extras=(dh1, sv["x"], dx1, sv["r1"], g_mix),
                         epi=lambda acc, prev, *rest: _norm_bwd(acc + prev, *rest), out_dtypes=grads,
                         tm=512, tn=d, name=f"dh_qkv_{tag}")
    small = dict(g_mix=dg_mix, g_mlp=dg_mlp, g_pe=dg_pe, sinks=dsink[:, 0, 0], dbias=dbias)
    return dx, gw, small


def _local_step(x, p, target, weights, g_mix, g_mlp, g_pe, g_final, sinks, rel_bias, plan=None):
    plan = _NoPlan() if plan is None else plan
    depth = g_mix.shape[0]
    buckets = jnp.asarray(_bucket_table())
    bias = _build_bias(rel_bias, buckets, "build_bias")
    saved, wfull = [], []
    h = x
    for l in range(depth):
        wfull.append(weights(l))
        h, sv = _layer_fwd(h, p[l], wfull[l], g_mix[l:l + 1], g_mlp[l:l + 1], g_pe[l:l + 1],
                           sinks[l], bias, f"l{l}", plan)
        saved.append(sv)
    loss_row, dx, dg_final = _loss_head(h, g_final[None, :], target, "loss_head")
    gws = [None] * depth
    smalls = [None] * depth
    for l in reversed(range(depth)):
        dx, gws[l], smalls[l] = _layer_bwd(dx, saved[l], p[l], wfull[l], g_mix[l:l + 1], g_mlp[l:l + 1],
                                           g_pe[l:l + 1], sinks[l], bias, l, plan)
        plan.layer_done(l)
    drel = _bias_grad([sm["dbias"] for sm in smalls], buckets, "bias_grad")[:, 0, :N_BUCKETS].T
    small = dict(
        g_mix=jnp.concatenate([sm["g_mix"] for sm in smalls], axis=0),
        g_mlp=jnp.concatenate([sm["g_mlp"] for sm in smalls], axis=0),
        g_pe=jnp.concatenate([sm["g_pe"] for sm in smalls], axis=0),
        g_final=dg_final[0],
        sinks=jnp.stack([sm["sinks"] for sm in smalls], axis=0),
        rel_bias=drel,
    )
    return loss_row, dx, gws, small


MESH_ID = pl.DeviceIdType.MESH
ANY = pl.BlockSpec(memory_space=pl.ANY)


def _position():
    return lax.axis_index("x"), lax.axis_index("y"), lax.axis_index("c")


def _run_comm(comm, name):
    ci, co = len(comm.inputs), len(comm.out_shapes)

    def body(*refs):
        cin, cout, csem = refs[:ci], refs[ci:ci + co], refs[ci + co:]
        pos = _position()
        comm.start(pos, cin, cout, csem)
        comm.finish(pos, cin, cout, csem)

    return pl.pallas_call(body, out_shape=comm.out_shapes, in_specs=[ANY] * ci, out_specs=[ANY] * co,
                          scratch_shapes=comm.sems, name=name)(*comm.inputs)


def _gather_comm(shards):
    n = len(shards)

    def copies(pos, x_refs, out_refs, sems):
        send_sems, recv_sems, local_sems = sems
        x, y, c = pos
        me, sibling = (x, y, c), (x, y, 1 - c)
        chips = [(1 - x, y), (x, 1 - y), (1 - x, 1 - y)]

        def slot(a, px, py, pc):
            return out_refs[a].at[4 * px + 2 * py + pc]

        def copy(a, k, block, to, src=None):
            return pltpu.make_async_remote_copy(
                src_ref=slot(a, *block) if src is None else src, dst_ref=slot(a, *block),
                send_sem=send_sems.at[a, k], recv_sem=recv_sems.at[a, k],
                device_id=to, device_id_type=MESH_ID)

        mine = [pltpu.make_async_copy(x_refs[a], slot(a, *me), local_sems.at[a]) for a in range(n)]
        first = []
        for a in range(n):
            first.append(copy(a, 0, me, sibling, src=x_refs[a]))
            first += [copy(a, 1 + j, me, (*chip, c), src=x_refs[a]) for j, chip in enumerate(chips)]
        return me, sibling, chips, copy, mine, first

    def start(pos, x_refs, out_refs, sems):
        _, _, _, _, mine, first = copies(pos, x_refs, out_refs, sems)
        for cp in mine + first:
            cp.start()

    def finish(pos, x_refs, out_refs, sems):
        me, sibling, chips, copy, mine, first = copies(pos, x_refs, out_refs, sems)
        c = pos[2]
        passed = []
        for j, chip in enumerate(chips):
            for a in range(n):
                copy(a, 1 + j, (*chip, c), me).wait_recv()
                fwd = copy(a, 4 + j, (*chip, c), sibling)
                fwd.start()
                passed.append(fwd)
        for a in range(n):
            copy(a, 0, sibling, me).wait_recv()
            for j, chip in enumerate(chips):
                copy(a, 4 + j, (*chip, 1 - c), me).wait_recv()
        for cp in first + passed:
            cp.wait_send()
        for cp in mine:
            cp.wait()

    return _Comm(shards, [jax.ShapeDtypeStruct((N_DEV,) + s.shape, s.dtype) for s in shards],
                 [pltpu.SemaphoreType.DMA((n, 7)), pltpu.SemaphoreType.DMA((n, 7)),
                  pltpu.SemaphoreType.DMA((n,))], start, finish)


def _exchange_comm(arrays, n_slots, route):
    n = len(arrays)

    def copies(pos, in_refs, out_refs, sems):
        send_sems, recv_sems = sems
        out = []
        for a in range(n):
            for j in range(n_slots):
                src_slot, peer = route(pos, j)
                out.append(pltpu.make_async_remote_copy(
                    src_ref=in_refs[a].at[src_slot], dst_ref=out_refs[a].at[j],
                    send_sem=send_sems.at[a, j], recv_sem=recv_sems.at[a, j],
                    device_id=peer, device_id_type=MESH_ID))
        return out

    def start(pos, in_refs, out_refs, sems):
        for cp in copies(pos, in_refs, out_refs, sems):
            cp.start()

    def finish(pos, in_refs, out_refs, sems):
        for cp in copies(pos, in_refs, out_refs, sems):
            cp.wait()

    return _Comm(arrays, [jax.ShapeDtypeStruct((n_slots,) + g.shape[1:], g.dtype) for g in arrays],
                 [pltpu.SemaphoreType.DMA((n, n_slots)), pltpu.SemaphoreType.DMA((n, n_slots))], start, finish)


def _rs_sibling_comm(gs):
    return _exchange_comm(gs, 4, lambda pos, j: (2 * j + (1 - pos[2]), (pos[0], pos[1], 1 - pos[2])))


def _chip_of(k, x, y):
    return x ^ ((k + 1) & 1), y ^ (((k + 1) >> 1) & 1)


def _chip_partials(pos, gs, recvs, name):
    n = len(gs)

    def body(pos_ref, *refs):
        for a in range(n):
            refs[2 * n + a][...] = (refs[a][...].astype(F32) + refs[n + a][...].astype(F32)
                                    ).astype(refs[2 * n + a].dtype)

    def g_map(k, pos_ref):
        cx, cy = _chip_of(k, pos_ref[0], pos_ref[1])
        return (4 * cx + 2 * cy + pos_ref[2], 0, 0)

    def r_map(k, pos_ref):
        cx, cy = _chip_of(k, pos_ref[0], pos_ref[1])
        return (2 * cx + cy, 0, 0)

    slab = [(None,) + g.shape[1:] for g in gs]
    return pl.pallas_call(
        body,
        grid_spec=pltpu.PrefetchScalarGridSpec(
            num_scalar_prefetch=1,
            grid=(4,),
            in_specs=[pl.BlockSpec(sh, g_map) for sh in slab] + [pl.BlockSpec(sh, r_map) for sh in slab],
            out_specs=[pl.BlockSpec(sh, lambda k, pos_ref: (k, 0, 0)) for sh in slab],
        ),
        out_shape=[jax.ShapeDtypeStruct((4,) + g.shape[1:], g.dtype) for g in gs],
        compiler_params=_cparams(),
        name=name,
    )(pos, *gs, *recvs)


def _rs_chips_comm(parts):
    return _exchange_comm(parts, 3, lambda pos, k: (k, (*_chip_of(k, pos[0], pos[1]), pos[2])))


def _adamw_math(w, g, m, v):
    m = ADAM_B1 * m + (1.0 - ADAM_B1) * g
    v = ADAM_B2 * v + (1.0 - ADAM_B2) * (g * g)
    m_hat = m / (1.0 - ADAM_B1 ** ADAM_STEP)
    v_hat = v / (1.0 - ADAM_B2 ** ADAM_STEP)
    delta = -ADAM_LR * (m_hat / (jnp.sqrt(v_hat) + ADAM_EPS) + ADAM_WD * w)
    return delta, m, v


def _adamw_weight(parts, recvs, w, m, v, name, grad_t=False, comm=None):
    depth, a, b = w.shape
    ta = _tile(a, 288, unit=LANES if grad_t else 16)
    ni = a // ta
    g_block = (b, ta) if grad_t else (ta, b)

    def body(*refs):
        p_refs, r_refs = refs[:depth], refs[depth:2 * depth]
        w_ref, m_ref, v_ref = refs[2 * depth:2 * depth + 3]
        g_out, d_out, m_out, v_out = refs[2 * depth + 3:]
        layer = pl.program_id(0)
        g = jnp.zeros(g_block, F32)
        for l in range(depth):
            gl = p_refs[l][...].astype(F32)
            for k in range(3):
                gl = gl + r_refs[l][k].astype(F32)
            g = jnp.where(layer == l, gl, g)
        if grad_t:
            g = g.T
        delta, m_new, v_new = _adamw_math(w_ref[...], g, m_ref[...], v_ref[...])
        g_out[...] = g
        d_out[...] = delta
        m_out[...] = m_new
        v_out[...] = v_new

    def hold(l):
        return lambda layer, i: jnp.where(layer == l, i, jnp.where(layer < l, 0, ni - 1))

    def g_index(slot, f):
        if grad_t:
            return lambda layer, i: (slot, 0, f(layer, i))
        return lambda layer, i: (slot, f(layer, i), 0)

    p_specs = [pl.BlockSpec((None,) + g_block, g_index(3, hold(l))) for l in range(depth)]
    r_specs = [pl.BlockSpec((3,) + g_block, g_index(0, hold(l))) for l in range(depth)]
    row = pl.BlockSpec((None, ta, b), lambda layer, i: (layer, i, 0))
    outs, couts = _call(
        body,
        grid=(depth, ni),
        in_specs=p_specs + r_specs + [row, row, row],
        out_specs=[row] * 4,
        out_shape=[jax.ShapeDtypeStruct(w.shape, F32)] * 4,
        args=(*parts, *recvs, w, m, v), name=name, comm=comm)
    return outs if comm is None else (outs, couts)


def _adamw_replicated(gathered, w, m, v, name):
    r, lanes = w.shape

    def body(g_ref, w_ref, m_ref, v_ref, g_out, d_out, m_out, v_out):
        g = g_ref[0]
        for k in range(1, N_DEV):
            g = g + g_ref[k]
        delta, m_new, v_new = _adamw_math(w_ref[...], g, m_ref[...], v_ref[...])
        g_out[...] = g
        d_out[...] = delta
        m_out[...] = m_new
        v_out[...] = v_new

    return pl.pallas_call(
        body,
        out_shape=[jax.ShapeDtypeStruct((r, lanes), F32)] * 4,
        name=name,
    )(gathered, w, m, v)


def _wire_shard(name, shard):
    return (shard.T if name in COL_SHARDED else shard).astype(WIRE)


def _full_weight(gathered):
    return gathered.reshape(N_DEV * gathered.shape[1], gathered.shape[2])


def _to_slabs(gfull):
    return gfull.reshape(N_DEV, gfull.shape[0] // N_DEV, gfull.shape[1])


def _pack_small(arrs):
    rows = []
    for a in arrs:
        flat = a.astype(F32).reshape(-1)
        pad = (-flat.shape[0]) % LANES
        rows.append(jnp.pad(flat, (0, pad)).reshape(-1, LANES))
    packed = jnp.concatenate(rows, axis=0)
    return jnp.pad(packed, ((0, (-packed.shape[0]) % 8), (0, 0)))


def _unpack_small(packed, shapes):
    out, off = [], 0
    for shp in shapes:
        n = math.prod(shp)
        rows = -(-n // LANES)
        out.append(packed[off:off + rows].reshape(-1)[:n].reshape(shp))
        off += rows
    return out


def _of(layer, *names):
    return tuple((layer, n) for n in names)


MIXER_W = ("w_in", "w_up_a", "w_up_b", "w_o")
MLP_W = ("w_ff1", "w_ff2", "w_pe", "w_pg")

GATHERS = (
    ("rms_mix_l0", _of(0, "w_in")),
    ("proj_qkv_l0", _of(0, "w_up_a", "w_up_b", "w_o")),
    ("proj_gate_l0", _of(0, "w_pe", "w_pg")),
    ("sb_fwd_l0", _of(0, "w_ff1", "w_ff2")),
    ("swa_fwd_l0", _of(1, "w_in")),
    ("sb_fwd_l1", _of(1, "w_up_a", "w_up_b", "w_o", "w_pe", "w_pg", "w_ff1")),
    ("swa_fwd_l1", _of(1, "w_ff2")),
)
REDUCES = (
    (_of(1, "w_ff1"), "dw_ff2_l0", "dact_l0"),
    (_of(1, "w_ff2"), "dw_ff2_l0", "dw_ff1_l0"),
    (_of(1, "w_in"), "dw_ff2_l0", "sb_bwd_l0"),
    (_of(1, "w_up_a", "w_up_b", "w_o", "w_pe", "w_pg"), "dw_ff2_l0", "swa_bwd_l0"),
    (_of(0, *MLP_W), "dh_mlp_l0", "sb_bwd_l0"),
    (_of(0, "w_o", "w_up_a", "w_up_b"), "do_a_l0", "sb_bwd_l0"),
    (_of(0, "w_in"), "dh_ga_l0", "dh_qkv_l0"),
)


def _merge_comms(comms):
    if len(comms) == 1:
        return comms[0]

    def cuts(counts):
        edges = [0]
        for c in counts:
            edges.append(edges[-1] + c)
        return [slice(a, b) for a, b in zip(edges[:-1], edges[1:])]

    s_in = cuts([len(c.inputs) for c in comms])
    s_out = cuts([len(c.out_shapes) for c in comms])
    s_sem = cuts([len(c.sems) for c in comms])

    def start(pos, cin, cout, csem):
        for c, i, o, s in zip(comms, s_in, s_out, s_sem):
            c.start(pos, cin[i], cout[o], csem[s])

    def finish(pos, cin, cout, csem):
        for c, i, o, s in zip(comms, s_in, s_out, s_sem):
            c.finish(pos, cin[i], cout[o], csem[s])

    return _Comm(sum([c.inputs for c in comms], []), sum([c.out_shapes for c in comms], []),
                 sum([c.sems for c in comms], []), start, finish)


class _LayerWeights:
    def __init__(self, full, layer):
        self.full, self.layer, self.cache = full, layer, {}

    def __getitem__(self, name):
        if name not in self.cache:
            if name == "w_qkv_t":
                self.cache[name] = self.full[(self.layer, "w_in")][:QKV_COLS]
            elif name == "w_gate_t":
                self.cache[name] = self.full[(self.layer, "w_in")][QKV_COLS:]
            else:
                base = name[:-2] if name.endswith("_t") else name
                assert (base in COL_SHARDED) == name.endswith("_t"), name
                self.cache[name] = self.full[(self.layer, base)]
        return self.cache[name]


class _Plan:
    def __init__(self, w_sh, pos):
        self.w_sh = dict(zip(WEIGHTS, w_sh))
        self.pos = pos
        self.full, self.gw, self.parts, self.recv = {}, {}, {}, {}
        self.slabs = {}
        self.hosted = {}
        for i, (host, _) in enumerate(GATHERS):
            if host is not None:
                self.hosted.setdefault(host, []).append(("gather", i))
        for i, (_, sib_host, chip_host) in enumerate(REDUCES):
            assert (sib_host is None) == (chip_host is None)
            if sib_host is not None:
                self.hosted.setdefault(sib_host, []).append(("sibling", i))
                self.hosted.setdefault(chip_host, []).append(("chips", i))

    def _gather(self, i):
        return _gather_comm([_wire_shard(n, self.w_sh[n][layer]) for layer, n in GATHERS[i][1]])

    def _gathered(self, i, outs):
        for (layer, n), g in zip(GATHERS[i][1], outs):
            self.full[(layer, n)] = _full_weight(g)

    def weights(self, layer):
        for i, (host, items) in enumerate(GATHERS):
            if host is None and items[0][0] == layer:
                self._gathered(i, _run_comm(self._gather(i), f"gather_{i}"))
        return _LayerWeights(self.full, layer)

    def grad(self, layer, name, value):
        self.gw[(layer, name)] = value

    def _sibling(self, i):
        self.slabs[i] = [_to_slabs(self.gw[item]) for item in REDUCES[i][0]]
        return _rs_sibling_comm(self.slabs[i])

    def _sibling_done(self, i, outs):
        parts = _chip_partials(self.pos, self.slabs[i], outs, f"chip_partials_{i}")
        for item, part in zip(REDUCES[i][0], parts):
            self.parts[item] = part

    def _chips(self, i):
        return _rs_chips_comm([self.parts[item] for item in REDUCES[i][0]])

    def _chips_done(self, i, outs):
        for item, r in zip(REDUCES[i][0], outs):
            self.recv[item] = r

    def layer_done(self, layer):
        for i, (items, sib_host, _) in enumerate(REDUCES):
            if sib_host is None and items[0][0] == layer:
                self._sibling_done(i, _run_comm(self._sibling(i), f"reduce_sibling_{i}"))
                self._chips_done(i, _run_comm(self._chips(i), f"reduce_chips_{i}"))

    def comm(self, name):
        if name not in self.hosted:
            return None
        make = {"gather": self._gather, "sibling": self._sibling, "chips": self._chips}
        return _merge_comms([make[kind](i) for kind, i in self.hosted[name]])

    def done(self, name, outs):
        took = {"gather": self._gathered, "sibling": self._sibling_done, "chips": self._chips_done}
        off = 0
        for kind, i in self.hosted[name]:
            n = len(GATHERS[i][1]) if kind == "gather" else len(REDUCES[i][0])
            took[kind](i, outs[off:off + n])
            off += n


def kernel(x, p, w_in, w_up_a, w_up_b, w_o, w_ff1, w_ff2, w_pe, w_pg, g_mix, g_mlp, g_pe, g_final, sinks, rel_bias, loss_target, m_w_in, m_w_up_a, m_w_up_b, m_w_o, m_w_ff1, m_w_ff2, m_w_pe, m_w_pg, m_g_mix, m_g_mlp, m_g_pe, m_g_final, m_sinks, m_rel_bias, v_w_in, v_w_up_a, v_w_up_b, v_w_o, v_w_ff1, v_w_ff2, v_w_pe, v_w_pg, v_g_mix, v_g_mlp, v_g_pe, v_g_final, v_sinks, v_rel_bias):
    w_sh = [w_in, w_up_a, w_up_b, w_o, w_ff1, w_ff2, w_pe, w_pg]
    m_sh = [m_w_in, m_w_up_a, m_w_up_b, m_w_o, m_w_ff1, m_w_ff2, m_w_pe, m_w_pg]
    v_sh = [v_w_in, v_w_up_a, v_w_up_b, v_w_o, v_w_ff1, v_w_ff2, v_w_pe, v_w_pg]
    depth = w_in.shape[0]
    assert depth == 2 and x.shape[-1] * 2 + QKV_COLS == w_in.shape[2] * N_DEV

    px, py, pc = _position()
    plan = _Plan(w_sh, jnp.stack([px, py, pc]).astype(jnp.int32))
    loss_row, grad_x, _, small = _local_step(
        x[0], p[:, 0], loss_target[0], plan.weights, g_mix, g_mlp, g_pe, g_final, sinks, rel_bias, plan=plan)

    small_g = _pack_small([small[n] for n in SMALL] + [loss_row[0, :1]])
    grad_w, delta_w, new_m, new_v = [], [], [], []
    for a, name in enumerate(WEIGHTS):
        parts = [plan.parts[(l, name)] for l in range(depth)]
        recvs = [plan.recv[(l, name)] for l in range(depth)]
        if name == "w_in":
            flip = lambda t: t.transpose(0, 2, 1)
            outs, (small_all,) = _adamw_weight(parts, recvs, flip(w_sh[a]), flip(m_sh[a]), flip(v_sh[a]),
                                               f"adamw_{name}", comm=_gather_comm([small_g]))
            outs = [flip(o) for o in outs]
        else:
            outs = _adamw_weight(parts, recvs, w_sh[a], m_sh[a], v_sh[a], f"adamw_{name}",
                                 grad_t=name in COL_SHARDED)
        for lst, o in zip((grad_w, delta_w, new_m, new_v), outs):
            lst.append(o)

    small_w = [g_mix, g_mlp, g_pe, g_final, sinks, rel_bias]
    small_m = [m_g_mix, m_g_mlp, m_g_pe, m_g_final, m_sinks, m_rel_bias]
    small_v = [v_g_mix, v_g_mlp, v_g_pe, v_g_final, v_sinks, v_rel_bias]
    small_shapes = [a.shape for a in small_w] + [(1,)]
    zero = jnp.zeros((1,), F32)
    packed_s = _adamw_replicated(small_all, _pack_small(small_w + [zero]), _pack_small(small_m + [zero]),
                                 _pack_small(small_v + [zero + 1.0]), "adamw_replicated")
    sg, sd, sm, sv = [_unpack_small(t, small_shapes) for t in packed_s]
    loss = sg[-1][0]

    return (loss, grad_x[None], *grad_w, *sg[:-1], *delta_w, *sd[:-1], *new_m, *sm[:-1], *new_v, *sv[:-1])
```

```python
import functools
import math

import numpy as np
import jax
import jax.numpy as jnp
from jax import lax
from jax.experimental import pallas as pl
from jax.experimental.pallas import tpu as pltpu

F32 = jnp.float32
MXU = jnp.bfloat16
WIRE = jnp.bfloat16

HEAD_DIM = 64
SB_HEADS = 8
SW_HEADS = 8
SW_KV = 2
SW_GROUP = SW_HEADS // SW_KV
BLOCK = 128
N_BUCKETS = 32
MAX_DISTANCE = 128
EPS = 1e-6
SCALE = HEAD_DIM ** -0.5
SB_W = SB_HEADS * HEAD_DIM
SW_W = SW_HEADS * HEAD_DIM
QKV_COLS = 3 * SB_W + SW_W + 2 * SW_KV * HEAD_DIM
N_DEV = 8
LANES = 128
N_PAIR = SB_HEADS // 2
NEG = -1e30

ADAM_LR = 0.001
ADAM_B1 = 0.9
ADAM_B2 = 0.999
ADAM_EPS = 1e-08
ADAM_WD = 0.01
ADAM_STEP = 10

VMEM_LIMIT = 48 * 1024 * 1024
SB_TQ = 256
SB_DEAD = -105.0
SB_SUB = 4

WEIGHTS = ("w_in", "w_up_a", "w_up_b", "w_o", "w_ff1", "w_ff2", "w_pe", "w_pg")
COL_SHARDED = ("w_in", "w_up_a", "w_up_b", "w_ff1", "w_pe")
SMALL = ("g_mix", "g_mlp", "g_pe", "g_final", "sinks", "rel_bias")


def _cparams(**kw):
    return pltpu.CompilerParams(vmem_limit_bytes=VMEM_LIMIT, **kw)


def _dot(a, b):
    return jnp.dot(a, b, preferred_element_type=F32)


def _dot_nt(a, b):
    return lax.dot_general(a, b, (((1,), (1,)), ((), ())), preferred_element_type=F32)


def _dot_tn(a, b):
    return lax.dot_general(a, b, (((0,), (0,)), ((), ())), preferred_element_type=F32)


def _tile(n, target, unit=LANES):
    if n <= target:
        return n
    t = (target // unit) * unit
    while t > unit and n % t:
        t -= unit
    assert n % t == 0, (n, target)
    return t


def _sigmoid(x):
    return 1.0 / (1.0 + jnp.exp(-x))


class _Comm:
    def __init__(self, inputs, out_shapes, sems, start, finish):
        self.inputs, self.out_shapes, self.sems = list(inputs), list(out_shapes), list(sems)
        self.start, self.finish = start, finish


def _call(body, *, grid, in_specs, out_specs, out_shape, scratch_shapes=(), args, name, comm=None):
    n_in, n_out, n_scr = len(in_specs), len(out_shape), len(scratch_shapes)
    if comm is None:
        outs = pl.pallas_call(body, grid=grid, in_specs=list(in_specs), out_specs=list(out_specs),
                              out_shape=list(out_shape), scratch_shapes=list(scratch_shapes),
                              compiler_params=_cparams(), name=name)(*args)
        return list(outs), None
    ci, co = len(comm.inputs), len(comm.out_shapes)
    any_spec = pl.BlockSpec(memory_space=pl.ANY)

    def wrapped(*refs):
        ins, cin = refs[:n_in], refs[n_in:n_in + ci]
        o0 = n_in + ci
        outs, cout = refs[o0:o0 + n_out], refs[o0 + n_out:o0 + n_out + co]
        s0 = o0 + n_out + co
        scr, csem = refs[s0:s0 + n_scr], refs[s0 + n_scr:]
        ids = [pl.program_id(d) for d in range(len(grid))]
        first = functools.reduce(jnp.logical_and, [i == 0 for i in ids])
        last = functools.reduce(jnp.logical_and, [i == g - 1 for i, g in zip(ids, grid)])
        pos = (lax.axis_index("x"), lax.axis_index("y"), lax.axis_index("c"))

        @pl.when(first)
        def _():
            comm.start(pos, cin, cout, csem)

        body(*ins, *outs, *scr)

        @pl.when(last)
        def _():
            comm.finish(pos, cin, cout, csem)

    outs = pl.pallas_call(wrapped, grid=grid, in_specs=list(in_specs) + [any_spec] * ci,
                          out_specs=list(out_specs) + [any_spec] * co,
                          out_shape=list(out_shape) + comm.out_shapes,
                          scratch_shapes=list(scratch_shapes) + comm.sems,
                          compiler_params=_cparams(), name=name)(*args, *comm.inputs)
    return list(outs[:n_out]), list(outs[n_out:])


def _accumulate(o_ref, value, first):
    @pl.when(first)
    def _():
        o_ref[...] = value

    @pl.when(jnp.logical_not(first))
    def _():
        o_ref[...] += value


def _mm(a, b, *, ta=False, tb=False, extras=(), epi=None, out_dtypes=(F32,),
        tm=1024, tn=1024, tk=1024, name, comm=None):
    if ta:
        kdim, m = a.shape
    else:
        m, kdim = a.shape
    n = b.shape[0] if tb else b.shape[1]
    assert (b.shape[1] if tb else b.shape[0]) == kdim
    tm, tn, tk = _tile(m, tm), _tile(n, tn), _tile(kdim, tk)
    nk = kdim // tk
    n_ex, n_out = len(extras), len(out_dtypes)

    a_spec = (pl.BlockSpec((tk, tm), lambda i, j, k: (k, i)) if ta
              else pl.BlockSpec((tm, tk), lambda i, j, k: (i, k)))
    b_spec = (pl.BlockSpec((tn, tk), lambda i, j, k: (j, k)) if tb
              else pl.BlockSpec((tk, tn), lambda i, j, k: (k, j)))
    ex_specs = []
    for e in extras:
        assert e.shape in ((m, n), (1, n), (m, 1)), (e.shape, m, n)
        if e.shape == (m, n):
            ex_specs.append(pl.BlockSpec((tm, tn), lambda i, j, k: (i, j)))
        elif e.shape[0] == 1:
            ex_specs.append(pl.BlockSpec((1, tn), lambda i, j, k: (0, j)))
        else:
            ex_specs.append(pl.BlockSpec((tm, 1), lambda i, j, k: (i, 0)))
    out_specs, out_shape, row_sums = [], [], []
    for dt in out_dtypes:
        kind = dt[1] if isinstance(dt, tuple) else "tile"
        row_sums.append(kind == "rowsum")
        if kind == "tile":
            out_specs.append(pl.BlockSpec((tm, tn), lambda i, j, k: (i, j)))
            out_shape.append(jax.ShapeDtypeStruct((m, n), dt))
            continue
        assert tn == n, "per-row and summed outputs need whole rows in one tile"
        if kind == "col":
            out_specs.append(pl.BlockSpec((tm, 1), lambda i, j, k: (i, 0)))
            out_shape.append(jax.ShapeDtypeStruct((m, 1), dt[0]))
        else:
            out_specs.append(pl.BlockSpec((1, tn), lambda i, j, k: (0, 0)))
            out_shape.append(jax.ShapeDtypeStruct((1, n), dt[0]))

    def body(a_ref, b_ref, *rest):
        ex_refs = rest[:n_ex]
        out_refs = rest[n_ex:n_ex + n_out]
        acc = rest[-1]
        k = pl.program_id(2)
        first_rows = pl.program_id(0) == 0

        def prod():
            av = a_ref[...].astype(MXU)
            bv = b_ref[...].astype(MXU)
            return _dot_tn(av, bv) if ta else (_dot_nt(av, bv) if tb else _dot(av, bv))

        def finish(res):
            if epi is not None:
                res = epi(res, *[e[...] for e in ex_refs])
            if not isinstance(res, tuple):
                res = (res,)
            for o_ref, r, summed in zip(out_refs, res, row_sums):
                if summed:
                    _accumulate(o_ref, r.astype(o_ref.dtype), first_rows)
                else:
                    o_ref[...] = r.astype(o_ref.dtype)

        if nk == 1:
            finish(prod())
            return

        @pl.when(k == 0)
        def _():
            acc[...] = prod()

        @pl.when(jnp.logical_and(k > 0, k < nk - 1))
        def _():
            acc[...] += prod()

        @pl.when(k == nk - 1)
        def _():
            finish(acc[...] + prod())

    outs, couts = _call(
        body,
        grid=(m // tm, n // tn, nk),
        in_specs=[a_spec, b_spec] + ex_specs,
        out_specs=out_specs,
        out_shape=out_shape,
        scratch_shapes=[pltpu.VMEM((tm, tn), F32)],
        args=(a, b, *extras), name=name, comm=comm)
    res = outs[0] if n_out == 1 else tuple(outs)
    return res if comm is None else (res, couts)


def _rms_fwd(x, g, name, comm=None):
    s, d = x.shape
    tr = _tile(s, 256)

    def body(x_ref, g_ref, h_ref, r_ref):
        xf = x_ref[...]
        r = lax.rsqrt(jnp.mean(xf * xf, axis=-1, keepdims=True) + EPS)
        h_ref[...] = ((xf * r) * g_ref[...]).astype(h_ref.dtype)
        r_ref[...] = r

    outs, couts = _call(
        body,
        grid=(s // tr,),
        in_specs=[pl.BlockSpec((tr, d), lambda i: (i, 0)), pl.BlockSpec((1, d), lambda i: (0, 0))],
        out_specs=[pl.BlockSpec((tr, d), lambda i: (i, 0)), pl.BlockSpec((tr, 1), lambda i: (i, 0))],
        out_shape=[jax.ShapeDtypeStruct((s, d), MXU), jax.ShapeDtypeStruct((s, 1), F32)],
        args=(x, g), name=name, comm=comm)
    return tuple(outs) if comm is None else (tuple(outs), couts)


def _loss_head(x, g, target, name):
    s, d = x.shape
    tr = _tile(s, 256)

    def body(x_ref, g_ref, t_ref, loss_ref, dx_ref, dg_ref):
        @pl.when(pl.program_id(0) == 0)
        def _():
            dg_ref[...] = jnp.zeros_like(dg_ref)
            loss_ref[...] = jnp.zeros_like(loss_ref)

        xf = x_ref[...]
        gv = g_ref[...]
        r = lax.rsqrt(jnp.mean(xf * xf, axis=-1, keepdims=True) + EPS)
        xhat = xf * r
        err = xhat * gv - t_ref[...]
        loss_ref[...] += 0.5 * jnp.sum(jnp.mean(err * err, axis=-1, keepdims=True), axis=0, keepdims=True)
        dy = err * (1.0 / d)
        dxhat = dy * gv
        mean = jnp.mean(dxhat * xhat, axis=-1, keepdims=True)
        dx_ref[...] = r * (dxhat - xhat * mean)
        dg_ref[...] += jnp.sum(dy * xhat, axis=0, keepdims=True)

    row = pl.BlockSpec((tr, d), lambda i: (i, 0))
    vec = pl.BlockSpec((1, d), lambda i: (0, 0))
    return pl.pallas_call(
        body,
        grid=(s // tr,),
        in_specs=[row, vec, row],
        out_specs=[pl.BlockSpec((1, LANES), lambda i: (0, 0)), row, vec],
        out_shape=[jax.ShapeDtypeStruct((1, LANES), F32), jax.ShapeDtypeStruct((s, d), F32),
                   jax.ShapeDtypeStruct((1, d), F32)],
        compiler_params=_cparams(),
        name=name,
    )(x, g, target)


def _mix_fwd(oa, ob, wa_t, wb_t, gates, name):
    s, kd = oa.shape
    d = wa_t.shape[0]
    tm, tn = _tile(s, 1024), _tile(d, 512)
    nj = d // tn

    def body(oa_ref, ob_ref, wa_ref, wb_ref, ga_ref, gb_ref, out_ref):
        ya = _dot_nt(oa_ref[...], wa_ref[...])
        yb = _dot_nt(ob_ref[...], wb_ref[...])
        out_ref[...] = (_sigmoid(ga_ref[...].astype(F32)) * ya
                        + _sigmoid(gb_ref[...].astype(F32)) * yb).astype(out_ref.dtype)

    o_spec = pl.BlockSpec((tm, kd), lambda i, j: (i, 0))
    w_spec = pl.BlockSpec((tn, kd), lambda i, j: (j, 0))
    return pl.pallas_call(
        body,
        grid=(s // tm, nj),
        in_specs=[o_spec, o_spec, w_spec, w_spec,
                  pl.BlockSpec((tm, tn), lambda i, j: (i, j)),
                  pl.BlockSpec((tm, tn), lambda i, j: (i, j + nj))],
        out_specs=pl.BlockSpec((tm, tn), lambda i, j: (i, j)),
        out_shape=jax.ShapeDtypeStruct((s, d), MXU),
        compiler_params=_cparams(),
        name=name,
    )(oa, ob, wa_t, wb_t, gates, gates)


def _mix_bwd(dx, w_o, oa, ob, wa_t, wb_t, gates, name):
    s, kd = oa.shape
    d = wa_t.shape[0]
    tm, tn = _tile(s, 1024), _tile(d, 512)
    nj = d // tn

    def body(dx_ref, wo_ref, oa_ref, ob_ref, wa_ref, wb_ref, ga_ref, gb_ref,
             dya_ref, dyb_ref, dga_ref, dgb_ref):
        dm = _dot_nt(dx_ref[...], wo_ref[...])
        ya = _dot_nt(oa_ref[...], wa_ref[...])
        yb = _dot_nt(ob_ref[...], wb_ref[...])
        sa = _sigmoid(ga_ref[...].astype(F32))
        sb = _sigmoid(gb_ref[...].astype(F32))
        dya_ref[...] = (dm * sa).astype(dya_ref.dtype)
        dyb_ref[...] = (dm * sb).astype(dyb_ref.dtype)
        dga_ref[...] = (dm * ya * sa * (1.0 - sa)).astype(dga_ref.dtype)
        dgb_ref[...] = (dm * yb * sb * (1.0 - sb)).astype(dgb_ref.dtype)

    o_spec = pl.BlockSpec((tm, kd), lambda i, j: (i, 0))
    w_spec = pl.BlockSpec((tn, kd), lambda i, j: (j, 0))
    t_spec = pl.BlockSpec((tm, tn), lambda i, j: (i, j))
    return pl.pallas_call(
        body,
        grid=(s // tm, nj),
        in_specs=[pl.BlockSpec((tm, d), lambda i, j: (i, 0)),
                  pl.BlockSpec((tn, d), lambda i, j: (j, 0)),
                  o_spec, o_spec, w_spec, w_spec, t_spec,
                  pl.BlockSpec((tm, tn), lambda i, j: (i, j + nj))],
        out_specs=[t_spec] * 4,
        out_shape=[jax.ShapeDtypeStruct((s, d), MXU)] * 4,
        compiler_params=_cparams(),
        name=name,
    )(dx, w_o, oa, ob, wa_t, wb_t, gates, gates)


def _ple(p, w_pe_t, h, w_pg, other, *, backward, name):
    s, kp = p.shape
    d = w_pe_t.shape[0]
    tm, tn = _tile(s, 1024), _tile(d, 512)

    def body(p_ref, wpe_ref, h_ref, wpg_ref, other_ref, *out_refs):
        pe = _dot_nt(p_ref[...].astype(MXU), wpe_ref[...])
        gt = _dot(h_ref[...], wpg_ref[...])
        sg = _sigmoid(gt)
        if backward:
            dout = other_ref[...]
            out_refs[0][...] = (dout * sg).astype(out_refs[0].dtype)
            out_refs[1][...] = (dout * pe * sg * (1.0 - sg)).astype(out_refs[1].dtype)
        else:
            out_refs[0][...] = other_ref[...] + pe * sg

    t_spec = pl.BlockSpec((tm, tn), lambda i, j: (i, j))
    if backward:
        out_specs, out_shape = [t_spec, t_spec], [jax.ShapeDtypeStruct((s, d), MXU)] * 2
    else:
        out_specs, out_shape = [t_spec], [jax.ShapeDtypeStruct((s, d), F32)]
    outs = pl.pallas_call(
        body,
        grid=(s // tm, d // tn),
        in_specs=[pl.BlockSpec((tm, kp), lambda i, j: (i, 0)),
                  pl.BlockSpec((tn, kp), lambda i, j: (j, 0)),
                  pl.BlockSpec((tm, d), lambda i, j: (i, 0)),
                  pl.BlockSpec((d, tn), lambda i, j: (0, j)),
                  t_spec],
        out_specs=out_specs,
        out_shape=out_shape,
        compiler_params=_cparams(),
        name=name,
    )(p, w_pe_t, h, w_pg, other)
    return tuple(outs) if backward else outs[0]


def _split_dot(x, tri):
    hi = x.astype(jnp.bfloat16)
    lo = (x - hi.astype(F32)).astype(jnp.bfloat16)
    return _dot(hi, tri) + _dot(lo, tri)


def _log_sigmoids(z):
    lb = jnp.minimum(z, 0.0) - jnp.log1p(jnp.exp(-jnp.abs(z)))
    return lb, lb - z


def _head_lanes(hh):
    lane = lax.broadcasted_iota(jnp.int32, (1, LANES), 1)
    return jnp.logical_and(lane >= hh * HEAD_DIM, lane < (hh + 1) * HEAD_DIM)


def _sb_fwd(qkv, name, comm=None):
    s = qkv.shape[0]
    tq = _tile(s, SB_TQ)
    nsub = SB_SUB if (s // tq) % SB_SUB == 0 else 1

    def body(q_ref, k_ref, v_ref, o_ref):
        row = lax.broadcasted_iota(jnp.int32, (tq, tq), 0)
        col = lax.broadcasted_iota(jnp.int32, (tq, tq), 1)
        causal = col < row
        tri = jnp.where(row > col, 1.0, 0.0).astype(jnp.bfloat16)
        started = [_sb_fwd_straight(q_ref, k_ref, v_ref, pl.program_id(1) * nsub + sub, sub, tq, causal, tri)
                   for sub in range(nsub)]
        for sub, (block, i, cs, accs) in enumerate(started):
            def top(cs):
                return jnp.maximum(jnp.max(cs[0]), jnp.max(cs[1]))

            def live(st):
                return jnp.logical_and(st[0] >= 0, st[1] > SB_DEAD)

            def walk(st, block=block):
                cs, accs = block(st[0], st[2], st[3], False)
                return st[0] - 1, top(cs), cs, accs

            accs = lax.while_loop(live, walk, (i - 2, top(cs), cs, accs))[3]
            o_ref[sub * tq:(sub + 1) * tq, :] = jnp.where(_head_lanes(0), accs[0], accs[1]).astype(o_ref.dtype)

    outs, couts = _call(
        body,
        grid=(N_PAIR, s // (nsub * tq)),
        in_specs=[pl.BlockSpec((nsub * tq, LANES), lambda p, i: (i, p)),
                  pl.BlockSpec((s, LANES), lambda p, i: (0, N_PAIR + p)),
                  pl.BlockSpec((s, LANES), lambda p, i: (0, 2 * N_PAIR + p))],
        out_specs=[pl.BlockSpec((nsub * tq, LANES), lambda p, i: (i, p))],
        out_shape=[jax.ShapeDtypeStruct((s, SB_W), MXU)],
        args=(qkv, qkv, qkv), name=name, comm=comm)
    return outs[0] if comm is None else (outs[0], couts)


def _sb_fwd_straight(q_ref, k_ref, v_ref, i, sub, tq, causal, tri):
    qf = q_ref[sub * tq:(sub + 1) * tq, :].astype(F32) * SCALE
    qms = [jnp.where(_head_lanes(hh), qf, 0.0).astype(MXU) for hh in range(2)]

    def block(kb, cs, accs, masked, gate=None):
        rows = pl.ds(pl.multiple_of(kb * tq, tq), tq)
        ks, vs = k_ref[rows, :], v_ref[rows, :]
        new_c, new_acc = [], []
        for hh in range(2):
            lb, lm = _log_sigmoids(_dot_nt(qms[hh], ks))
            if masked:
                lm = jnp.where(causal, lm, 0.0)
            if gate is not None:
                lm = lm * gate
            a = jnp.exp(lb + _split_dot(lm, tri) + cs[hh])
            if masked:
                a = jnp.where(causal, a, 0.0)
            if gate is not None:
                a = a * gate
            new_acc.append(accs[hh] + _dot(a.astype(MXU), vs))
            new_c.append(cs[hh] + jnp.sum(lm, axis=1, keepdims=True))
        return tuple(new_c), tuple(new_acc)

    zc, za = jnp.zeros((tq, 1), F32), jnp.zeros((tq, LANES), F32)
    cs, accs = block(i, (zc, zc), (za, za), True)
    cs, accs = block(jnp.maximum(i - 1, 0), cs, accs, False, jnp.where(i > 0, 1.0, 0.0))
    return block, i, cs, accs


def _sb_bwd(qkv, do, name, comm=None):
    s = qkv.shape[0]
    tq = _tile(s, SB_TQ)
    nq = s // tq
    nsub = SB_SUB if nq % SB_SUB == 0 else 1
    nsteps = nq // nsub

    def body(q_ref, k_ref, v_ref, do_ref, dq_ref, dk_ref, dv_ref, dk_acc, dv_acc, carries):
        step = pl.program_id(1)

        @pl.when(step == 0)
        def _():
            dk_acc[...] = jnp.zeros_like(dk_acc)
            dv_acc[...] = jnp.zeros_like(dv_acc)

        row = lax.broadcasted_iota(jnp.int32, (tq, tq), 0)
        col = lax.broadcasted_iota(jnp.int32, (tq, tq), 1)
        causal = col < row
        tri_rev = jnp.where(row > col, 1.0, 0.0).astype(jnp.bfloat16)
        tri_excl = jnp.where(row < col, 1.0, 0.0).astype(jnp.bfloat16)
        zc, za = jnp.zeros((tq, 1), F32), jnp.zeros((tq, LANES), F32)

        def top(cs):
            return jnp.maximum(jnp.max(cs[0]), jnp.max(cs[1]))

        def live(st):
            return jnp.logical_and(st[0] >= 0, st[1] > SB_DEAD)

        def row_sums(pre):
            return [jnp.sum(lm, axis=1, keepdims=True) for _, lm in pre]

        def query_block(sub):
            i = step * nsub + sub
            q_rows = slice(sub * tq, (sub + 1) * tq)
            qf = q_ref[q_rows, :].astype(F32) * SCALE
            dof = do_ref[q_rows, :]
            qms = [jnp.where(_head_lanes(hh), qf, 0.0).astype(MXU) for hh in range(2)]
            doms = [jnp.where(_head_lanes(hh), dof, jnp.zeros_like(dof)) for hh in range(2)]

            def terms(kb, masked):
                rows = pl.ds(pl.multiple_of(kb * tq, tq), tq)
                ks = k_ref[rows, :]
                out = []
                for hh in range(2):
                    lb, lm = _log_sigmoids(_dot_nt(qms[hh], ks))
                    if masked:
                        lm = jnp.where(causal, lm, 0.0)
                    out.append((lb, lm))
                return out

            def block(kb, cs, gpres, dqs, masked, gate=None, pre=None):
                rows = pl.ds(pl.multiple_of(kb * tq, tq), tq)
                ks, vs = k_ref[rows, :], v_ref[rows, :]
                pre = terms(kb, masked) if pre is None else pre
                new_g, new_dq = [], []
                dk_add, dv_add = None, None
                for hh in range(2):
                    lb, lm = pre[hh]
                    a = jnp.exp(lb + _split_dot(lm, tri_rev) + cs[hh])
                    if masked:
                        a = jnp.where(causal, a, 0.0)
                    if gate is not None:
                        a = a * gate
                    g = a * _dot_nt(doms[hh], vs)
                    gsum = gpres[hh] + _split_dot(g, tri_excl)
                    dz = g - (g + gsum) * jnp.exp(lb)
                    if masked:
                        dz = jnp.where(causal, dz, 0.0)
                    if gate is not None:
                        dz = dz * gate
                    dzb = dz.astype(MXU)
                    new_dq.append(dqs[hh] + _dot(dzb, ks))
                    dk_h = _dot_tn(dzb, qms[hh])
                    dv_h = _dot_tn(a.astype(MXU), doms[hh])
                    dk_add = dk_h if dk_add is None else dk_add + dk_h
                    dv_add = dv_h if dv_add is None else dv_add + dv_h
                    new_g.append(gpres[hh] + jnp.sum(g, axis=1, keepdims=True))
                dk_acc[rows, :] += dk_add
                dv_acc[rows, :] += dv_add
                return tuple(new_g), tuple(new_dq)

            prev = jnp.maximum(i - 1, 0)
            gate = jnp.where(i > 0, 1.0, 0.0)
            t_diag, t_prev = terms(i, True), terms(prev, False)
            c_diag = row_sums(t_diag)
            sums = row_sums(t_prev)
            c_prev = tuple(c_diag[hh] + sums[hh] * gate for hh in range(2))
            return dict(i=i, prev=prev, gate=gate, q_rows=q_rows, terms=terms, block=block,
                        t_diag=t_diag, t_prev=t_prev, c_diag=c_diag, c_prev=c_prev)

        blocks = [query_block(sub) for sub in range(nsub)]
        for qb in blocks:
            def record(st, qb=qb):
                kb, cs = st[0], st[2]
                sums = row_sums(qb["terms"](kb, False))
                for hh in range(2):
                    carries[hh, kb] = cs[hh]
                cs = tuple(cs[hh] + sums[hh] for hh in range(2))
                return kb - 1, top(cs), cs

            first = lax.while_loop(live, record, (qb["i"] - 2, top(qb["c_prev"]), qb["c_prev"]))[0] + 1
            qb["mid"] = lax.fori_loop(
                first, qb["i"] - 1,
                lambda kb, cr, qb=qb: qb["block"](kb, (carries[0, kb], carries[1, kb]), cr[0], cr[1], False),
                ((zc, zc), (za, za)))
        for qb in blocks:
            gpres, dqs = qb["block"](qb["prev"], qb["c_diag"], *qb["mid"], False, qb["gate"], qb["t_prev"])
            dqs = qb["block"](qb["i"], (zc, zc), gpres, dqs, True, None, qb["t_diag"])[1]
            dq_ref[qb["q_rows"], :] = (jnp.where(_head_lanes(0), dqs[0], dqs[1]) * SCALE).astype(dq_ref.dtype)

        @pl.when(step == nsteps - 1)
        def _():
            dk_ref[...] = dk_acc[...].astype(dk_ref.dtype)
            dv_ref[...] = dv_acc[...].astype(dv_ref.dtype)

    blk = pl.BlockSpec((nsub * tq, LANES), lambda p, i: (i, p))
    full = pl.BlockSpec((s, LANES), lambda p, i: (0, p))
    outs, couts = _call(
        body,
        grid=(N_PAIR, nsteps),
        in_specs=[blk,
                  pl.BlockSpec((s, LANES), lambda p, i: (0, N_PAIR + p)),
                  pl.BlockSpec((s, LANES), lambda p, i: (0, 2 * N_PAIR + p)),
                  blk],
        out_specs=[blk, full, full],
        out_shape=[jax.ShapeDtypeStruct((s, SB_W), MXU)] * 3,
        scratch_shapes=[pltpu.VMEM((s, LANES), F32), pltpu.VMEM((s, LANES), F32),
                        pltpu.VMEM((2, nq, tq, 1), F32)],
        args=(qkv, qkv, qkv, do), name=name, comm=comm)
    return tuple(outs) if comm is None else (tuple(outs), couts)


def _bucket_table():
    i = np.arange(BLOCK)[:, None]
    j = np.arange(2 * BLOCK)[None, :]
    d = np.maximum(BLOCK + i - j, 0)
    max_exact = N_BUCKETS // 2
    df = np.maximum(d, 1).astype(np.float32)
    large = max_exact + (np.log(df / max_exact) / math.log(MAX_DISTANCE / max_exact)
                         * (N_BUCKETS - max_exact)).astype(np.int32)
    large = np.minimum(large, N_BUCKETS - 1)
    return np.where(d < max_exact, d, large).astype(np.int32)


def _build_bias(rel_bias, buckets, name):
    def body(rb_ref, bk_ref, out_ref):
        h = pl.program_id(0)
        bk = bk_ref[...]
        acc = jnp.zeros(bk.shape, F32)
        for b in range(N_BUCKETS):
            acc = jnp.where(bk == b, rb_ref[b, h], acc)
        out_ref[...] = acc

    return pl.pallas_call(
        body,
        grid=(SW_HEADS,),
        in_specs=[pl.BlockSpec(memory_space=pltpu.SMEM),
                  pl.BlockSpec((BLOCK, 2 * BLOCK), lambda h: (0, 0))],
        out_specs=pl.BlockSpec((None, BLOCK, 2 * BLOCK), lambda h: (h, 0, 0)),
        out_shape=jax.ShapeDtypeStruct((SW_HEADS, BLOCK, 2 * BLOCK), F32),
        name=name,
    )(rel_bias, buckets)


def _bias_grad(dbias_layers, buckets, name):
    n_l = len(dbias_layers)

    def body(*refs):
        bk = refs[n_l][...]
        out_ref = refs[n_l + 1]
        db = refs[0][...]
        for r in refs[1:n_l]:
            db = db + r[...]
        lane = lax.broadcasted_iota(jnp.int32, (1, LANES), 1)
        acc = jnp.zeros((1, LANES), F32)
        for b in range(N_BUCKETS):
            part = jnp.sum(jnp.where(bk == b, db, 0.0), axis=1, keepdims=True)
            tot = jnp.sum(part, axis=0, keepdims=True)
            acc = jnp.where(lane == b, tot, acc)
        out_ref[...] = acc

    hspec = pl.BlockSpec((None, BLOCK, 2 * BLOCK), lambda h: (h, 0, 0))
    return pl.pallas_call(
        body,
        grid=(SW_HEADS,),
        in_specs=[hspec] * n_l + [pl.BlockSpec((BLOCK, 2 * BLOCK), lambda h: (0, 0))],
        out_specs=pl.BlockSpec((None, 1, LANES), lambda h: (h, 0, 0)),
        out_shape=jax.ShapeDtypeStruct((SW_HEADS, 1, LANES), F32),
        name=name,
    )(*dbias_layers, buckets)


GROUP_ROWS = SW_GROUP * BLOCK


def _group_lanes(g):
    lane = lax.broadcasted_iota(jnp.int32, (1, LANES), 1)
    gvec = jnp.zeros((1, LANES), jnp.int32) + g
    return jnp.where(lane >= HEAD_DIM, 1, 0) == gvec, gvec


def _stack_heads(x, g):
    kv_lanes, gvec = _group_lanes(g)
    parts = []
    for j in range(SW_GROUP):
        half = x[:, (j // 2) * LANES:(j // 2 + 1) * LANES]
        moved = jnp.where(gvec == j % 2, half, pltpu.roll(half, HEAD_DIM, 1))
        parts.append(jnp.where(kv_lanes, moved, 0.0))
    return jnp.concatenate(parts, axis=0)


def _unstack_heads(y, g):
    _, gvec = _group_lanes(g)
    heads = []
    for j in range(SW_GROUP):
        yj = y[j * BLOCK:(j + 1) * BLOCK]
        heads.append(jnp.where(gvec == j % 2, yj, pltpu.roll(yj, HEAD_DIM, 1)))
    pairs = [jnp.where(_head_lanes(0), heads[2 * p], heads[2 * p + 1]) for p in range(SW_GROUP // 2)]
    return jnp.concatenate(pairs, axis=1)


def _per_head_col(values):
    return jnp.concatenate([jnp.zeros((BLOCK, 1), F32) + v for v in values], axis=0)


def _swa_scores(qs, kp, kc, bias_ref, n):
    row = jnp.bitwise_and(lax.broadcasted_iota(jnp.int32, (GROUP_ROWS, BLOCK), 0), BLOCK - 1)
    col = lax.broadcasted_iota(jnp.int32, (GROUP_ROWS, BLOCK), 1)
    bias = bias_ref[...].reshape(GROUP_ROWS, 2 * BLOCK)
    s1 = _dot_nt(qs, kp) + bias[:, :BLOCK]
    s2 = _dot_nt(qs, kc) + bias[:, BLOCK:]
    no_prev = jnp.where(n > 0, 0, BLOCK)
    s1 = jnp.where(col > row + no_prev, s1, NEG)
    s2 = jnp.where(col <= row, s2, NEG)
    return s1, s2


def _swa_specs(s):
    q_blk = 3 * SB_W // (2 * LANES)
    k_blk = (3 * SB_W + SW_W) // LANES
    return (pl.BlockSpec((s, 2 * LANES), lambda g: (0, q_blk + g)),
            pl.BlockSpec((s, LANES), lambda g: (0, k_blk)),
            pl.BlockSpec((s, LANES), lambda g: (0, k_blk + 1)))


def _swa_fwd(qkv, bias, sinks, name, comm=None):
    s = qkv.shape[0]
    nb = s // BLOCK

    def body(sink_ref, q_ref, k_ref, v_ref, bias_ref, o_ref, lse_ref):
        g = pl.program_id(0)
        sink = _per_head_col([sink_ref[SW_GROUP * g + j] for j in range(SW_GROUP)])
        lane = lax.broadcasted_iota(jnp.int32, (1, LANES), 1)

        def step(n, carry):
            r0 = pl.multiple_of(n * BLOCK, BLOCK)
            p0 = pl.multiple_of(jnp.maximum(n - 1, 0) * BLOCK, BLOCK)
            cur, prev = pl.ds(r0, BLOCK), pl.ds(p0, BLOCK)
            qs = _stack_heads(q_ref[cur, :].astype(F32) * SCALE, g).astype(MXU)
            s1, s2 = _swa_scores(qs, k_ref[prev, :], k_ref[cur, :], bias_ref, n)
            m = jnp.maximum(jnp.max(jnp.maximum(s1, s2), axis=1, keepdims=True), sink)
            e1 = jnp.exp(s1 - m)
            e2 = jnp.exp(s2 - m)
            den = jnp.sum(e1 + e2, axis=1, keepdims=True) + jnp.exp(sink - m)
            o = _dot((e1 / den).astype(MXU), v_ref[prev, :]) + _dot((e2 / den).astype(MXU), v_ref[cur, :])
            o_ref[cur, :] = _unstack_heads(o, g).astype(o_ref.dtype)
            lse = m + jnp.log(den)
            lse_row = jnp.zeros((BLOCK, LANES), F32)
            for j in range(SW_GROUP):
                lse_row = jnp.where(lane == j, lse[j * BLOCK:(j + 1) * BLOCK], lse_row)
            lse_ref[cur, :] = lse_row
            return carry

        lax.fori_loop(0, nb, step, 0, unroll=2)

    outs, couts = _call(
        body,
        grid=(SW_KV,),
        in_specs=[pl.BlockSpec(memory_space=pltpu.SMEM), *_swa_specs(s),
                  pl.BlockSpec((SW_GROUP, BLOCK, 2 * BLOCK), lambda g: (g, 0, 0))],
        out_specs=[pl.BlockSpec((s, 2 * LANES), lambda g: (0, g)),
                   pl.BlockSpec((None, s, LANES), lambda g: (g, 0, 0))],
        out_shape=[jax.ShapeDtypeStruct((s, SW_W), MXU), jax.ShapeDtypeStruct((SW_KV, s, LANES), F32)],
        args=(sinks, qkv, qkv, qkv, bias), name=name, comm=comm)
    return tuple(outs) if comm is None else (tuple(outs), couts)


def _swa_bwd(qkv, bias, sinks, do, lse, name, comm=None):
    s = qkv.shape[0]
    nb = s // BLOCK

    def body(sink_ref, q_ref, k_ref, v_ref, bias_ref, do_ref, lse_ref,
             dq_ref, dk_ref, dv_ref, dbias_ref, dsink_ref, dk_acc, dv_acc):
        g = pl.program_id(0)
        sink = _per_head_col([sink_ref[SW_GROUP * g + j] for j in range(SW_GROUP)])
        lane = lax.broadcasted_iota(jnp.int32, (1, LANES), 1)

        @pl.when(g == 0)
        def _():
            dk_acc[...] = jnp.zeros_like(dk_acc)
            dv_acc[...] = jnp.zeros_like(dv_acc)

        dbias_ref[...] = jnp.zeros_like(dbias_ref)

        def step(n, dsink_rows):
            r0 = pl.multiple_of(n * BLOCK, BLOCK)
            p0 = pl.multiple_of(jnp.maximum(n - 1, 0) * BLOCK, BLOCK)
            cur, prev = pl.ds(r0, BLOCK), pl.ds(p0, BLOCK)
            qs = _stack_heads(q_ref[cur, :].astype(F32) * SCALE, g).astype(MXU)
            dos = _stack_heads(do_ref[cur, :].astype(F32), g).astype(MXU)
            kp, kc, vp, vc = k_ref[prev, :], k_ref[cur, :], v_ref[prev, :], v_ref[cur, :]
            lse_row = lse_ref[cur, :]
            lse = jnp.concatenate([jnp.sum(jnp.where(lane == j, lse_row, 0.0), axis=1, keepdims=True)
                                   for j in range(SW_GROUP)], axis=0)
            s1, s2 = _swa_scores(qs, kp, kc, bias_ref, n)
            pr1 = jnp.exp(s1 - lse)
            pr2 = jnp.exp(s2 - lse)
            dpr1 = _dot_nt(dos, vp)
            dpr2 = _dot_nt(dos, vc)
            delta = jnp.sum(pr1 * dpr1 + pr2 * dpr2, axis=1, keepdims=True)
            ds1 = pr1 * (dpr1 - delta)
            ds2 = pr2 * (dpr2 - delta)
            dbias_ref[:, :, :BLOCK] += ds1.reshape(SW_GROUP, BLOCK, BLOCK)
            dbias_ref[:, :, BLOCK:] += ds2.reshape(SW_GROUP, BLOCK, BLOCK)
            ds1b, ds2b = ds1.astype(MXU), ds2.astype(MXU)
            dq = _dot(ds1b, kp) + _dot(ds2b, kc)
            dq_ref[cur, :] = (_unstack_heads(dq, g) * SCALE).astype(dq_ref.dtype)
            dk_acc[prev, :] += _dot_tn(ds1b, qs)
            dk_acc[cur, :] += _dot_tn(ds2b, qs)
            dv_acc[prev, :] += _dot_tn(pr1.astype(MXU), dos)
            dv_acc[cur, :] += _dot_tn(pr2.astype(MXU), dos)
            return dsink_rows - jnp.exp(sink - lse) * delta

        rows = lax.fori_loop(0, nb, step, jnp.zeros((GROUP_ROWS, 1), F32), unroll=2)
        for j in range(SW_GROUP):
            dsink_ref[j] = jnp.broadcast_to(jnp.sum(rows[j * BLOCK:(j + 1) * BLOCK], axis=0, keepdims=True),
                                            (1, LANES))

        @pl.when(g == SW_KV - 1)
        def _():
            dk_ref[...] = dk_acc[...].astype(dk_ref.dtype)
            dv_ref[...] = dv_acc[...].astype(dv_ref.dtype)

    grp = pl.BlockSpec((s, 2 * LANES), lambda g: (0, g))
    kv_out = pl.BlockSpec((s, LANES), lambda g: (0, 0))
    bspec = pl.BlockSpec((SW_GROUP, BLOCK, 2 * BLOCK), lambda g: (g, 0, 0))
    outs, couts = _call(
        body,
        grid=(SW_KV,),
        in_specs=[pl.BlockSpec(memory_space=pltpu.SMEM), *_swa_specs(s), bspec, grp,
                  pl.BlockSpec((None, s, LANES), lambda g: (g, 0, 0))],
        out_specs=[grp, kv_out, kv_out, bspec, pl.BlockSpec((SW_GROUP, 1, LANES), lambda g: (g, 0, 0))],
        out_shape=[jax.ShapeDtypeStruct((s, SW_W), MXU),
                   jax.ShapeDtypeStruct((s, LANES), MXU),
                   jax.ShapeDtypeStruct((s, LANES), MXU),
                   jax.ShapeDtypeStruct((SW_HEADS, BLOCK, 2 * BLOCK), F32),
                   jax.ShapeDtypeStruct((SW_HEADS, 1, LANES), F32)],
        scratch_shapes=[pltpu.VMEM((s, LANES), F32), pltpu.VMEM((s, LANES), F32)],
        args=(sinks, qkv, qkv, qkv, bias, do, lse), name=name, comm=comm)
    return tuple(outs) if comm is None else (tuple(outs), couts)


class _NoPlan:
    def comm(self, name):
        return None

    def done(self, name, outs):
        pass

    def grad(self, layer, name, value):
        pass

    def layer_done(self, layer):
        pass


def _run(plan, fn, *args, name, **kw):
    comm = plan.comm(name)
    if comm is None:
        return fn(*args, name=name, **kw)
    res, outs = fn(*args, name=name, comm=comm, **kw)
    plan.done(name, outs)
    return res


def _norm_bwd(dh, x, dres, r, g):
    xhat = x * r
    dxhat = dh * g
    dx = dres + r * (dxhat - xhat * jnp.mean(dxhat * xhat, axis=-1, keepdims=True))
    return dx, dx, jnp.sum(dh * xhat, axis=0, keepdims=True)


def _residual_norm(acc, res, g):
    x = res + acc
    r = lax.rsqrt(jnp.mean(x * x, axis=-1, keepdims=True) + EPS)
    return x, (x * r) * g, r


def _layer_fwd(x, p, w, g_mix, g_mlp, g_pe, sinks, bias, tag, plan):
    h1, r1 = _run(plan, _rms_fwd, x, g_mix, name=f"rms_mix_{tag}")
    qkv = _run(plan, _mm, h1, w["w_qkv_t"], tb=True, out_dtypes=(MXU,), name=f"proj_qkv_{tag}")
    gates = _run(plan, _mm, h1, w["w_gate_t"], tb=True, out_dtypes=(MXU,), name=f"proj_gate_{tag}")
    oa = _run(plan, _sb_fwd, qkv, name=f"sb_fwd_{tag}")
    ob, lse = _run(plan, _swa_fwd, qkv, bias, sinks, name=f"swa_fwd_{tag}")
    merged = _mix_fwd(oa, ob, w["w_up_a_t"], w["w_up_b_t"], gates, f"mix_fwd_{tag}")
    d = x.shape[1]
    normed = (F32, MXU, (F32, "col"))
    x1, h2, r2 = _run(plan, _mm, merged, w["w_o"], extras=(x, g_mlp), epi=_residual_norm, out_dtypes=normed,
                      tn=d, name=f"out_proj_{tag}")
    u, act = _run(plan, _mm, h2, w["w_ff1_t"], tb=True,
                  epi=lambda acc: (acc, jnp.square(jnp.maximum(acc, 0.0))),
                  out_dtypes=(MXU, MXU), name=f"ff1_{tag}")
    x2, h3, r3 = _run(plan, _mm, act, w["w_ff2"], extras=(x1, g_pe), epi=_residual_norm, out_dtypes=normed,
                      tn=d, name=f"ff2_{tag}")
    x3 = _ple(p, w["w_pe_t"], h3, w["w_pg"], x2, backward=False, name=f"ple_fwd_{tag}")
    saved = dict(x=x, h1=h1, r1=r1, gates=gates, qkv=qkv, lse=lse, oa=oa, ob=ob, merged=merged,
                 x1=x1, h2=h2, r2=r2, u=u, act=act, x2=x2, h3=h3, r3=r3)
    return x3, saved


def _layer_bwd(dx3, sv, p, w, g_mix, g_mlp, g_pe, sinks, bias, layer, plan):
    tag = f"l{layer}"
    gw = {}
    wire = (WIRE,)

    def dw(name, a, b):
        gw[name] = _run(plan, _mm, a, b, ta=True, out_dtypes=wire, tk=2048, name=f"d{name}_{tag}")
        plan.grad(layer, name, gw[name])

    dpe, dgt = _ple(p, w["w_pe_t"], sv["h3"], w["w_pg"], dx3, backward=True, name=f"ple_bwd_{tag}")
    dw("w_pe", dpe, p)
    dw("w_pg", sv["h3"], dgt)
    d = dx3.shape[1]
    grads = (F32, MXU, (F32, "rowsum"))
    dx2, dx2b, dg_pe = _run(plan, _mm, dgt, w["w_pg"], tb=True, extras=(sv["x2"], dx3, sv["r3"], g_pe),
                            epi=_norm_bwd, out_dtypes=grads, tm=512, tn=d, name=f"dh_pe_{tag}")
    dw("w_ff2", sv["act"], dx2b)
    du = _run(plan, _mm, dx2b, w["w_ff2"], tb=True, extras=(sv["u"],),
              epi=lambda acc, u: acc * (2.0 * jnp.maximum(u.astype(F32), 0.0)), out_dtypes=(MXU,),
              name=f"dact_{tag}")
    dw("w_ff1", du, sv["h2"])
    dx1, dx1b, dg_mlp = _run(plan, _mm, du, w["w_ff1_t"], extras=(sv["x1"], dx2, sv["r2"], g_mlp),
                             epi=_norm_bwd, out_dtypes=grads, tm=1024, tn=d, name=f"dh_mlp_{tag}")
    dw("w_o", sv["merged"], dx1b)
    dya, dyb, dga, dgb = _mix_bwd(dx1b, w["w_o"], sv["oa"], sv["ob"], w["w_up_a_t"], w["w_up_b_t"],
                                  sv["gates"], f"mix_bwd_{tag}")
    dw("w_up_a", dya, sv["oa"])
    dw("w_up_b", dyb, sv["ob"])
    doa = _run(plan, _mm, dya, w["w_up_a_t"], out_dtypes=(MXU,), name=f"do_a_{tag}")
    dob = _run(plan, _mm, dyb, w["w_up_b_t"], out_dtypes=(MXU,), name=f"do_b_{tag}")
    dqb, dkb, dvb, dbias, dsink = _run(plan, _swa_bwd, sv["qkv"], bias, sinks, dob, sv["lse"],
                                       name=f"swa_bwd_{tag}")
    dqa, dka, dva = _run(plan, _sb_bwd, sv["qkv"], doa, name=f"sb_bwd_{tag}")
    dqkv = jnp.concatenate([dqa, dka, dva, dqb, dkb, dvb], axis=1)
    gw_qkv = _mm(dqkv, sv["h1"], ta=True, out_dtypes=wire, tk=2048, name=f"dw_qkv_{tag}")
    gw_ga = _mm(dga, sv["h1"], ta=True, out_dtypes=wire, tk=2048, name=f"dw_ga_{tag}")
    gw_gb = _mm(dgb, sv["h1"], ta=True, out_dtypes=wire, tk=2048, name=f"dw_gb_{tag}")
    gw["w_in"] = jnp.concatenate([gw_qkv, gw_ga, gw_gb], axis=0)
    plan.grad(layer, "w_in", gw["w_in"])
    d = dga.shape[1]
    add = lambda acc, res: res + acc
    dh1 = _run(plan, _mm, dga, w["w_gate_t"][:d], name=f"dh_ga_{tag}")
    dh1 = _run(plan, _mm, dgb, w["w_gate_t"][d:], extras=(dh1,), epi=add, name=f"dh_gb_{tag}")
    dx, _, dg_mix = _run(plan, _mm, dqkv, w["w_qkv_t"], tk=768, extras=(dh1, sv["x"], dx1, sv["r1"], g_mix),
                         epi=lambda acc, prev, *rest: _norm_bwd(acc + prev, *rest), out_dtypes=grads,
                         tm=512, tn=d, name=f"dh_qkv_{tag}")
    small = dict(g_mix=dg_mix, g_mlp=dg_mlp, g_pe=dg_pe, sinks=dsink[:, 0, 0], dbias=dbias)
    return dx, gw, small


def _local_step(x, p, target, weights, g_mix, g_mlp, g_pe, g_final, sinks, rel_bias, plan=None):
    plan = _NoPlan() if plan is None else plan
    depth = g_mix.shape[0]
    buckets = jnp.asarray(_bucket_table())
    bias = _build_bias(rel_bias, buckets, "build_bias")
    saved, wfull = [], []
    h = x
    for l in range(depth):
        wfull.append(weights(l))
        h, sv = _layer_fwd(h, p[l], wfull[l], g_mix[l:l + 1], g_mlp[l:l + 1], g_pe[l:l + 1],
                           sinks[l], bias, f"l{l}", plan)
        saved.append(sv)
    loss_row, dx, dg_final = _loss_head(h, g_final[None, :], target, "loss_head")
    gws = [None] * depth
    smalls = [None] * depth
    for l in reversed(range(depth)):
        dx, gws[l], smalls[l] = _layer_bwd(dx, saved[l], p[l], wfull[l], g_mix[l:l + 1], g_mlp[l:l + 1],
                                           g_pe[l:l + 1], sinks[l], bias, l, plan)
        plan.layer_done(l)
    drel = _bias_grad([sm["dbias"] for sm in smalls], buckets, "bias_grad")[:, 0, :N_BUCKETS].T
    small = dict(
        g_mix=jnp.concatenate([sm["g_mix"] for sm in smalls], axis=0),
        g_mlp=jnp.concatenate([sm["g_mlp"] for sm in smalls], axis=0),
        g_pe=jnp.concatenate([sm["g_pe"] for sm in smalls], axis=0),
        g_final=dg_final[0],
        sinks=jnp.stack([sm["sinks"] for sm in smalls], axis=0),
        rel_bias=drel,
    )
    return loss_row, dx, gws, small


MESH_ID = pl.DeviceIdType.MESH
ANY = pl.BlockSpec(memory_space=pl.ANY)


def _position():
    return lax.axis_index("x"), lax.axis_index("y"), lax.axis_index("c")


def _run_comm(comm, name):
    ci, co = len(comm.inputs), len(comm.out_shapes)

    def body(*refs):
        cin, cout, csem = refs[:ci], refs[ci:ci + co], refs[ci + co:]
        pos = _position()
        comm.start(pos, cin, cout, csem)
        comm.finish(pos, cin, cout, csem)

    return pl.pallas_call(body, out_shape=comm.out_shapes, in_specs=[ANY] * ci, out_specs=[ANY] * co,
                          scratch_shapes=comm.sems, name=name)(*comm.inputs)


def _gather_comm(shards):
    n = len(shards)

    def copies(pos, x_refs, out_refs, sems):
        send_sems, recv_sems, local_sems = sems
        x, y, c = pos
        me, sibling = (x, y, c), (x, y, 1 - c)
        chips = [(1 - x, y), (x, 1 - y), (1 - x, 1 - y)]

        def slot(a, px, py, pc):
            return out_refs[a].at[4 * px + 2 * py + pc]

        def copy(a, k, block, to, src=None):
            return pltpu.make_async_remote_copy(
                src_ref=slot(a, *block) if src is None else src, dst_ref=slot(a, *block),
                send_sem=send_sems.at[a, k], recv_sem=recv_sems.at[a, k],
                device_id=to, device_id_type=MESH_ID)

        mine = [pltpu.make_async_copy(x_refs[a], slot(a, *me), local_sems.at[a]) for a in range(n)]
        first = []
        for a in range(n):
            first.append(copy(a, 0, me, sibling, src=x_refs[a]))
            first += [copy(a, 1 + j, me, (*chip, c), src=x_refs[a]) for j, chip in enumerate(chips)]
        return me, sibling, chips, copy, mine, first

    def start(pos, x_refs, out_refs, sems):
        _, _, _, _, mine, first = copies(pos, x_refs, out_refs, sems)
        for cp in mine + first:
            cp.start()

    def finish(pos, x_refs, out_refs, sems):
        me, sibling, chips, copy, mine, first = copies(pos, x_refs, out_refs, sems)
        c = pos[2]
        passed = []
        for j, chip in enumerate(chips):
            for a in range(n):
                copy(a, 1 + j, (*chip, c), me).wait_recv()
                fwd = copy(a, 4 + j, (*chip, c), sibling)
                fwd.start()
                passed.append(fwd)
        for a in range(n):
            copy(a, 0, sibling, me).wait_recv()
            for j, chip in enumerate(chips):
                copy(a, 4 + j, (*chip, 1 - c), me).wait_recv()
        for cp in first + passed:
            cp.wait_send()
        for cp in mine:
            cp.wait()

    return _Comm(shards, [jax.ShapeDtypeStruct((N_DEV,) + s.shape, s.dtype) for s in shards],
                 [pltpu.SemaphoreType.DMA((n, 7)), pltpu.SemaphoreType.DMA((n, 7)),
                  pltpu.SemaphoreType.DMA((n,))], start, finish)


def _exchange_comm(arrays, n_slots, route):
    n = len(arrays)

    def copies(pos, in_refs, out_refs, sems):
        send_sems, recv_sems = sems
        out = []
        for a in range(n):
            for j in range(n_slots):
                src_slot, peer = route(pos, j)
                out.append(pltpu.make_async_remote_copy(
                    src_ref=in_refs[a].at[src_slot], dst_ref=out_refs[a].at[j],
                    send_sem=send_sems.at[a, j], recv_sem=recv_sems.at[a, j],
                    device_id=peer, device_id_type=MESH_ID))
        return out

    def start(pos, in_refs, out_refs, sems):
        for cp in copies(pos, in_refs, out_refs, sems):
            cp.start()

    def finish(pos, in_refs, out_refs, sems):
        for cp in copies(pos, in_refs, out_refs, sems):
            cp.wait()

    return _Comm(arrays, [jax.ShapeDtypeStruct((n_slots,) + g.shape[1:], g.dtype) for g in arrays],
                 [pltpu.SemaphoreType.DMA((n, n_slots)), pltpu.SemaphoreType.DMA((n, n_slots))], start, finish)


def _rs_sibling_comm(gs):
    return _exchange_comm(gs, 4, lambda pos, j: (2 * j + (1 - pos[2]), (pos[0], pos[1], 1 - pos[2])))


def _chip_of(k, x, y):
    return x ^ ((k + 1) & 1), y ^ (((k + 1) >> 1) & 1)


def _chip_partials(pos, gs, recvs, name):
    n = len(gs)

    def body(pos_ref, *refs):
        for a in range(n):
            refs[2 * n + a][...] = (refs[a][...].astype(F32) + refs[n + a][...].astype(F32)
                                    ).astype(refs[2 * n + a].dtype)

    def g_map(k, pos_ref):
        cx, cy = _chip_of(k, pos_ref[0], pos_ref[1])
        return (4 * cx + 2 * cy + pos_ref[2], 0, 0)

    def r_map(k, pos_ref):
        cx, cy = _chip_of(k, pos_ref[0], pos_ref[1])
        return (2 * cx + cy, 0, 0)

    slab = [(None,) + g.shape[1:] for g in gs]
    return pl.pallas_call(
        body,
        grid_spec=pltpu.PrefetchScalarGridSpec(
            num_scalar_prefetch=1,
            grid=(4,),
            in_specs=[pl.BlockSpec(sh, g_map) for sh in slab] + [pl.BlockSpec(sh, r_map) for sh in slab],
            out_specs=[pl.BlockSpec(sh, lambda k, pos_ref: (k, 0, 0)) for sh in slab],
        ),
        out_shape=[jax.ShapeDtypeStruct((4,) + g.shape[1:], g.dtype) for g in gs],
        compiler_params=_cparams(),
        name=name,
    )(pos, *gs, *recvs)


def _rs_chips_comm(parts):
    return _exchange_comm(parts, 3, lambda pos, k: (k, (*_chip_of(k, pos[0], pos[1]), pos[2])))


def _adamw_math(w, g, m, v):
    m = ADAM_B1 * m + (1.0 - ADAM_B1) * g
    v = ADAM_B2 * v + (1.0 - ADAM_B2) * (g * g)
    m_hat = m / (1.0 - ADAM_B1 ** ADAM_STEP)
    v_hat = v / (1.0 - ADAM_B2 ** ADAM_STEP)
    delta = -ADAM_LR * (m_hat / (jnp.sqrt(v_hat) + ADAM_EPS) + ADAM_WD * w)
    return delta, m, v


def _adamw_weight(parts, recvs, w, m, v, name, grad_t=False, comm=None):
    depth, a, b = w.shape
    ta = _tile(a, 288, unit=LANES if grad_t else 16)
    ni = a // ta
    g_block = (b, ta) if grad_t else (ta, b)

    def body(*refs):
        p_refs, r_refs = refs[:depth], refs[depth:2 * depth]
        w_ref, m_ref, v_ref = refs[2 * depth:2 * depth + 3]
        g_out, d_out, m_out, v_out = refs[2 * depth + 3:]
        layer = pl.program_id(0)
        g = jnp.zeros(g_block, F32)
        for l in range(depth):
            gl = p_refs[l][...].astype(F32)
            for k in range(3):
                gl = gl + r_refs[l][k].astype(F32)
            g = jnp.where(layer == l, gl, g)
        if grad_t:
            g = g.T
        delta, m_new, v_new = _adamw_math(w_ref[...], g, m_ref[...], v_ref[...])
        g_out[...] = g
        d_out[...] = delta
        m_out[...] = m_new
        v_out[...] = v_new

    def hold(l):
        return lambda layer, i: jnp.where(layer == l, i, jnp.where(layer < l, 0, ni - 1))

    def g_index(slot, f):
        if grad_t:
            return lambda layer, i: (slot, 0, f(layer, i))
        return lambda layer, i: (slot, f(layer, i), 0)

    p_specs = [pl.BlockSpec((None,) + g_block, g_index(3, hold(l))) for l in range(depth)]
    r_specs = [pl.BlockSpec((3,) + g_block, g_index(0, hold(l))) for l in range(depth)]
    row = pl.BlockSpec((None, ta, b), lambda layer, i: (layer, i, 0))
    outs, couts = _call(
        body,
        grid=(depth, ni),
        in_specs=p_specs + r_specs + [row, row, row],
        out_specs=[row] * 4,
        out_shape=[jax.ShapeDtypeStruct(w.shape, F32)] * 4,
        args=(*parts, *recvs, w, m, v), name=name, comm=comm)
    return outs if comm is None else (outs, couts)


def _adamw_replicated(gathered, w, m, v, name):
    r, lanes = w.shape

    def body(g_ref, w_ref, m_ref, v_ref, g_out, d_out, m_out, v_out):
        g = g_ref[0]
        for k in range(1, N_DEV):
            g = g + g_ref[k]
        delta, m_new, v_new = _adamw_math(w_ref[...], g, m_ref[...], v_ref[...])
        g_out[...] = g
        d_out[...] = delta
        m_out[...] = m_new
        v_out[...] = v_new

    return pl.pallas_call(
        body,
        out_shape=[jax.ShapeDtypeStruct((r, lanes), F32)] * 4,
        name=name,
    )(gathered, w, m, v)


def _wire_shard(name, shard):
    return (shard.T if name in COL_SHARDED else shard).astype(WIRE)


def _full_weight(gathered):
    return gathered.reshape(N_DEV * gathered.shape[1], gathered.shape[2])


def _to_slabs(gfull):
    return gfull.reshape(N_DEV, gfull.shape[0] // N_DEV, gfull.shape[1])


def _pack_small(arrs):
    rows = []
    for a in arrs:
        flat = a.astype(F32).reshape(-1)
        pad = (-flat.shape[0]) % LANES
        rows.append(jnp.pad(flat, (0, pad)).reshape(-1, LANES))
    packed = jnp.concatenate(rows, axis=0)
    return jnp.pad(packed, ((0, (-packed.shape[0]) % 8), (0, 0)))


def _unpack_small(packed, shapes):
    out, off = [], 0
    for shp in shapes:
        n = math.prod(shp)
        rows = -(-n // LANES)
        out.append(packed[off:off + rows].reshape(-1)[:n].reshape(shp))
        off += rows
    return out


def _of(layer, *names):
    return tuple((layer, n) for n in names)


MIXER_W = ("w_in", "w_up_a", "w_up_b", "w_o")
MLP_W = ("w_ff1", "w_ff2", "w_pe", "w_pg")

GATHERS = (
    ("rms_mix_l0", _of(0, "w_in")),
    ("proj_qkv_l0", _of(0, "w_up_a", "w_up_b", "w_o")),
    ("proj_gate_l0", _of(0, "w_pe", "w_pg")),
    ("sb_fwd_l0", _of(0, "w_ff1", "w_ff2")),
    ("swa_fwd_l0", _of(1, "w_in")),
    ("sb_fwd_l1", _of(1, "w_up_a", "w_up_b", "w_o", "w_pe", "w_pg", "w_ff1")),
    ("swa_fwd_l1", _of(1, "w_ff2")),
)
REDUCES = (
    (_of(1, "w_ff1"), "dw_ff2_l0", "dact_l0"),
    (_of(1, "w_ff2"), "dw_ff2_l0", "dw_ff1_l0"),
    (_of(1, "w_in"), "dw_ff2_l0", "sb_bwd_l0"),
    (_of(1, "w_up_a", "w_up_b", "w_o", "w_pe", "w_pg"), "dw_ff2_l0", "swa_bwd_l0"),
    (_of(0, *MLP_W), "dh_mlp_l0", "sb_bwd_l0"),
    (_of(0, "w_o", "w_up_a", "w_up_b"), "do_a_l0", "sb_bwd_l0"),
    (_of(0, "w_in"), "dh_ga_l0", "dh_qkv_l0"),
)


def _merge_comms(comms):
    if len(comms) == 1:
        return comms[0]

    def cuts(counts):
        edges = [0]
        for c in counts:
            edges.append(edges[-1] + c)
        return [slice(a, b) for a, b in zip(edges[:-1], edges[1:])]

    s_in = cuts([len(c.inputs) for c in comms])
    s_out = cuts([len(c.out_shapes) for c in comms])
    s_sem = cuts([len(c.sems) for c in comms])

    def start(pos, cin, cout, csem):
        for c, i, o, s in zip(comms, s_in, s_out, s_sem):
            c.start(pos, cin[i], cout[o], csem[s])

    def finish(pos, cin, cout, csem):
        for c, i, o, s in zip(comms, s_in, s_out, s_sem):
            c.finish(pos, cin[i], cout[o], csem[s])

    return _Comm(sum([c.inputs for c in comms], []), sum([c.out_shapes for c in comms], []),
                 sum([c.sems for c in comms], []), start, finish)


class _LayerWeights:
    def __init__(self, full, layer):
        self.full, self.layer, self.cache = full, layer, {}

    def __getitem__(self, name):
        if name not in self.cache:
            if name == "w_qkv_t":
                self.cache[name] = self.full[(self.layer, "w_in")][:QKV_COLS]
            elif name == "w_gate_t":
                self.cache[name] = self.full[(self.layer, "w_in")][QKV_COLS:]
            else:
                base = name[:-2] if name.endswith("_t") else name
                assert (base in COL_SHARDED) == name.endswith("_t"), name
                self.cache[name] = self.full[(self.layer, base)]
        return self.cache[name]


class _Plan:
    def __init__(self, w_sh, pos):
        self.w_sh = dict(zip(WEIGHTS, w_sh))
        self.pos = pos
        self.full, self.gw, self.parts, self.recv = {}, {}, {}, {}
        self.slabs = {}
        self.hosted = {}
        for i, (host, _) in enumerate(GATHERS):
            if host is not None:
                self.hosted.setdefault(host, []).append(("gather", i))
        for i, (_, sib_host, chip_host) in enumerate(REDUCES):
            assert (sib_host is None) == (chip_host is None)
            if sib_host is not None:
                self.hosted.setdefault(sib_host, []).append(("sibling", i))
                self.hosted.setdefault(chip_host, []).append(("chips", i))

    def _gather(self, i):
        return _gather_comm([_wire_shard(n, self.w_sh[n][layer]) for layer, n in GATHERS[i][1]])

    def _gathered(self, i, outs):
        for (layer, n), g in zip(GATHERS[i][1], outs):
            self.full[(layer, n)] = _full_weight(g)

    def weights(self, layer):
        for i, (host, items) in enumerate(GATHERS):
            if host is None and items[0][0] == layer:
                self._gathered(i, _run_comm(self._gather(i), f"gather_{i}"))
        return _LayerWeights(self.full, layer)

    def grad(self, layer, name, value):
        self.gw[(layer, name)] = value

    def _sibling(self, i):
        self.slabs[i] = [_to_slabs(self.gw[item]) for item in REDUCES[i][0]]
        return _rs_sibling_comm(self.slabs[i])

    def _sibling_done(self, i, outs):
        parts = _chip_partials(self.pos, self.slabs[i], outs, f"chip_partials_{i}")
        for item, part in zip(REDUCES[i][0], parts):
            self.parts[item] = part

    def _chips(self, i):
        return _rs_chips_comm([self.parts[item] for item in REDUCES[i][0]])

    def _chips_done(self, i, outs):
        for item, r in zip(REDUCES[i][0], outs):
            self.recv[item] = r

    def layer_done(self, layer):
        for i, (items, sib_host, _) in enumerate(REDUCES):
            if sib_host is None and items[0][0] == layer:
                self._sibling_done(i, _run_comm(self._sibling(i), f"reduce_sibling_{i}"))
                self._chips_done(i, _run_comm(self._chips(i), f"reduce_chips_{i}"))

    def comm(self, name):
        if name not in self.hosted:
            return None
        make = {"gather": self._gather, "sibling": self._sibling, "chips": self._chips}
        return _merge_comms([make[kind](i) for kind, i in self.hosted[name]])

    def done(self, name, outs):
        took = {"gather": self._gathered, "sibling": self._sibling_done, "chips": self._chips_done}
        off = 0
        for kind, i in self.hosted[name]:
            n = len(GATHERS[i][1]) if kind == "gather" else len(REDUCES[i][0])
            took[kind](i, outs[off:off + n])
            off += n


def kernel(x, p, w_in, w_up_a, w_up_b, w_o, w_ff1, w_ff2, w_pe, w_pg, g_mix, g_mlp, g_pe, g_final, sinks, rel_bias, loss_target, m_w_in, m_w_up_a, m_w_up_b, m_w_o, m_w_ff1, m_w_ff2, m_w_pe, m_w_pg, m_g_mix, m_g_mlp, m_g_pe, m_g_final, m_sinks, m_rel_bias, v_w_in, v_w_up_a, v_w_up_b, v_w_o, v_w_ff1, v_w_ff2, v_w_pe, v_w_pg, v_g_mix, v_g_mlp, v_g_pe, v_g_final, v_sinks, v_rel_bias):
    w_sh = [w_in, w_up_a, w_up_b, w_o, w_ff1, w_ff2, w_pe, w_pg]
    m_sh = [m_w_in, m_w_up_a, m_w_up_b, m_w_o, m_w_ff1, m_w_ff2, m_w_pe, m_w_pg]
    v_sh = [v_w_in, v_w_up_a, v_w_up_b, v_w_o, v_w_ff1, v_w_ff2, v_w_pe, v_w_pg]
    depth = w_in.shape[0]
    assert depth == 2 and x.shape[-1] * 2 + QKV_COLS == w_in.shape[2] * N_DEV

    px, py, pc = _position()
    plan = _Plan(w_sh, jnp.stack([px, py, pc]).astype(jnp.int32))
    loss_row, grad_x, _, small = _local_step(
        x[0], p[:, 0], loss_target[0], plan.weights, g_mix, g_mlp, g_pe, g_final, sinks, rel_bias, plan=plan)

    small_g = _pack_small([small[n] for n in SMALL] + [loss_row[0, :1]])
    grad_w, delta_w, new_m, new_v = [], [], [], []
    for a, name in enumerate(WEIGHTS):
        parts = [plan.parts[(l, name)] for l in range(depth)]
        recvs = [plan.recv[(l, name)] for l in range(depth)]
        if name == "w_in":
            flip = lambda t: t.transpose(0, 2, 1)
            outs, (small_all,) = _adamw_weight(parts, recvs, flip(w_sh[a]), flip(m_sh[a]), flip(v_sh[a]),
                                               f"adamw_{name}", comm=_gather_comm([small_g]))
            outs = [flip(o) for o in outs]
        else:
            outs = _adamw_weight(parts, recvs, w_sh[a], m_sh[a], v_sh[a], f"adamw_{name}",
                                 grad_t=name in COL_SHARDED)
        for lst, o in zip((grad_w, delta_w, new_m, new_v), outs):
            lst.append(o)

    small_w = [g_mix, g_mlp, g_pe, g_final, sinks, rel_bias]
    small_m = [m_g_mix, m_g_mlp, m_g_pe, m_g_final, m_sinks, m_rel_bias]
    small_v = [v_g_mix, v_g_mlp, v_g_pe, v_g_final, v_sinks, v_rel_bias]
    small_shapes = [a.shape for a in small_w] + [(1,)]
    zero = jnp.zeros((1,), F32)
    packed_s = _adamw_replicated(small_all, _pack_small(small_w + [zero]), _pack_small(small_m + [zero]),
                                 _pack_small(small_v + [zero + 1.0]), "adamw_replicated")
    sg, sd, sm, sv = [_unpack_small(t, small_shapes) for t in packed_s]
    loss = sg[-1][0]

    return (loss, grad_x[None], *grad_w, *sg[:-1], *delta_w, *sd[:-1], *new_m, *sm[:-1], *new_v, *sv[:-1])
```

```python
import functools
import math

import numpy as np
import jax
import jax.numpy as jnp
from jax import lax
from jax.experimental import pallas as pl
from jax.experimental.pallas import tpu as pltpu

F32 = jnp.float32
MXU = jnp.bfloat16
WIRE = jnp.bfloat16

HEAD_DIM = 64
SB_HEADS = 8
SW_HEADS = 8
SW_KV = 2
SW_GROUP = SW_HEADS // SW_KV
BLOCK = 128
N_BUCKETS = 32
MAX_DISTANCE = 128
EPS = 1e-6
SCALE = HEAD_DIM ** -0.5
SB_W = SB_HEADS * HEAD_DIM
SW_W = SW_HEADS * HEAD_DIM
QKV_COLS = 3 * SB_W + SW_W + 2 * SW_KV * HEAD_DIM
N_DEV = 8
LANES = 128
N_PAIR = SB_HEADS // 2
NEG = -1e30

ADAM_LR = 0.001
ADAM_B1 = 0.9
ADAM_B2 = 0.999
ADAM_EPS = 1e-08
ADAM_WD = 0.01
ADAM_STEP = 10

VMEM_LIMIT = 48 * 1024 * 1024
SB_TQ = 256
SB_DEAD = -105.0
SB_SUB = 4

WEIGHTS = ("w_in", "w_up_a", "w_up_b", "w_o", "w_ff1", "w_ff2", "w_pe", "w_pg")
COL_SHARDED = ("w_in", "w_up_a", "w_up_b", "w_ff1", "w_pe")
SMALL = ("g_mix", "g_mlp", "g_pe", "g_final", "sinks", "rel_bias")


def _cparams(**kw):
    return pltpu.CompilerParams(vmem_limit_bytes=VMEM_LIMIT, **kw)


def _dot(a, b):
    return jnp.dot(a, b, preferred_element_type=F32)


def _dot_nt(a, b):
    return lax.dot_general(a, b, (((1,), (1,)), ((), ())), preferred_element_type=F32)


def _dot_tn(a, b):
    return lax.dot_general(a, b, (((0,), (0,)), ((), ())), preferred_element_type=F32)


def _tile(n, target, unit=LANES):
    if n <= target:
        return n
    t = (target // unit) * unit
    while t > unit and n % t:
        t -= unit
    assert n % t == 0, (n, target)
    return t


def _sigmoid(x):
    return 1.0 / (1.0 + jnp.exp(-x))


class _Comm:
    def __init__(self, inputs, out_shapes, sems, start, finish):
        self.inputs, self.out_shapes, self.sems = list(inputs), list(out_shapes), list(sems)
        self.start, self.finish = start, finish


def _call(body, *, grid, in_specs, out_specs, out_shape, scratch_shapes=(), args, name, comm=None):
    n_in, n_out, n_scr = len(in_specs), len(out_shape), len(scratch_shapes)
    if comm is None:
        outs = pl.pallas_call(body, grid=grid, in_specs=list(in_specs), out_specs=list(out_specs),
                              out_shape=list(out_shape), scratch_shapes=list(scratch_shapes),
                              compiler_params=_cparams(), name=name)(*args)
        return list(outs), None
    ci, co = len(comm.inputs), len(comm.out_shapes)
    any_spec = pl.BlockSpec(memory_space=pl.ANY)

    def wrapped(*refs):
        ins, cin = refs[:n_in], refs[n_in:n_in + ci]
        o0 = n_in + ci
        outs, cout = refs[o0:o0 + n_out], refs[o0 + n_out:o0 + n_out + co]
        s0 = o0 + n_out + co
        scr, csem = refs[s0:s0 + n_scr], refs[s0 + n_scr:]
        ids = [pl.program_id(d) for d in range(len(grid))]
        first = functools.reduce(jnp.logical_and, [i == 0 for i in ids])
        last = functools.reduce(jnp.logical_and, [i == g - 1 for i, g in zip(ids, grid)])
        pos = (lax.axis_index("x"), lax.axis_index("y"), lax.axis_index("c"))

        @pl.when(first)
        def _():
            comm.start(pos, cin, cout, csem)

        body(*ins, *outs, *scr)

        @pl.when(last)
        def _():
            comm.finish(pos, cin, cout, csem)

    outs = pl.pallas_call(wrapped, grid=grid, in_specs=list(in_specs) + [any_spec] * ci,
                          out_specs=list(out_specs) + [any_spec] * co,
                          out_shape=list(out_shape) + comm.out_shapes,
                          scratch_shapes=list(scratch_shapes) + comm.sems,
                          compiler_params=_cparams(), name=name)(*args, *comm.inputs)
    return list(outs[:n_out]), list(outs[n_out:])


def _accumulate(o_ref, value, first):
    @pl.when(first)
    def _():
        o_ref[...] = value

    @pl.when(jnp.logical_not(first))
    def _():
        o_ref[...] += value


def _mm(a, b, *, ta=False, tb=False, extras=(), epi=None, out_dtypes=(F32,),
        tm=1024, tn=1024, tk=1024, name, comm=None):
    if ta:
        kdim, m = a.shape
    else:
        m, kdim = a.shape
    n = b.shape[0] if tb else b.shape[1]
    assert (b.shape[1] if tb else b.shape[0]) == kdim
    tm, tn, tk = _tile(m, tm), _tile(n, tn), _tile(kdim, tk)
    nk = kdim // tk
    n_ex, n_out = len(extras), len(out_dtypes)

    a_spec = (pl.BlockSpec((tk, tm), lambda i, j, k: (k, i)) if ta
              else pl.BlockSpec((tm, tk), lambda i, j, k: (i, k)))
    b_spec = (pl.BlockSpec((tn, tk), lambda i, j, k: (j, k)) if tb
              else pl.BlockSpec((tk, tn), lambda i, j, k: (k, j)))
    ex_specs = []
    for e in extras:
        assert e.shape in ((m, n), (1, n), (m, 1)), (e.shape, m, n)
        if e.shape == (m, n):
            ex_specs.append(pl.BlockSpec((tm, tn), lambda i, j, k: (i, j)))
        elif e.shape[0] == 1:
            ex_specs.append(pl.BlockSpec((1, tn), lambda i, j, k: (0, j)))
        else:
            ex_specs.append(pl.BlockSpec((tm, 1), lambda i, j, k: (i, 0)))
    out_specs, out_shape, row_sums = [], [], []
    for dt in out_dtypes:
        kind = dt[1] if isinstance(dt, tuple) else "tile"
        row_sums.append(kind == "rowsum")
        if kind == "tile":
            out_specs.append(pl.BlockSpec((tm, tn), lambda i, j, k: (i, j)))
            out_shape.append(jax.ShapeDtypeStruct((m, n), dt))
            continue
        assert tn == n, "per-row and summed outputs need whole rows in one tile"
        if kind == "col":
            out_specs.append(pl.BlockSpec((tm, 1), lambda i, j, k: (i, 0)))
            out_shape.append(jax.ShapeDtypeStruct((m, 1), dt[0]))
        else:
            out_specs.append(pl.BlockSpec((1, tn), lambda i, j, k: (0, 0)))
            out_shape.append(jax.ShapeDtypeStruct((1, n), dt[0]))

    def body(a_ref, b_ref, *rest):
        ex_refs = rest[:n_ex]
        out_refs = rest[n_ex:n_ex + n_out]
        acc = rest[-1]
        k = pl.program_id(2)
        first_rows = pl.program_id(0) == 0

        def prod():
            av = a_ref[...].astype(MXU)
            bv = b_ref[...].astype(MXU)
            return _dot_tn(av, bv) if ta else (_dot_nt(av, bv) if tb else _dot(av, bv))

        def finish(res):
            if epi is not None:
                res = epi(res, *[e[...] for e in ex_refs])
            if not isinstance(res, tuple):
                res = (res,)
            for o_ref, r, summed in zip(out_refs, res, row_sums):
                if summed:
                    _accumulate(o_ref, r.astype(o_ref.dtype), first_rows)
                else:
                    o_ref[...] = r.astype(o_ref.dtype)

        if nk == 1:
            finish(prod())
            return

        @pl.when(k == 0)
        def _():
            acc[...] = prod()

        @pl.when(jnp.logical_and(k > 0, k < nk - 1))
        def _():
            acc[...] += prod()

        @pl.when(k == nk - 1)
        def _():
            finish(acc[...] + prod())

    outs, couts = _call(
        body,
        grid=(m // tm, n // tn, nk),
        in_specs=[a_spec, b_spec] + ex_specs,
        out_specs=out_specs,
        out_shape=out_shape,
        scratch_shapes=[pltpu.VMEM((tm, tn), F32)],
        args=(a, b, *extras), name=name, comm=comm)
    res = outs[0] if n_out == 1 else tuple(outs)
    return res if comm is None else (res, couts)


def _rms_fwd(x, g, name, comm=None):
    s, d = x.shape
    tr = _tile(s, 256)

    def body(x_ref, g_ref, h_ref, r_ref):
        xf = x_ref[...]
        r = lax.rsqrt(jnp.mean(xf * xf, axis=-1, keepdims=True) + EPS)
        h_ref[...] = ((xf * r) * g_ref[...]).astype(h_ref.dtype)
        r_ref[...] = r

    outs, couts = _call(
        body,
        grid=(s // tr,),
        in_specs=[pl.BlockSpec((tr, d), lambda i: (i, 0)), pl.BlockSpec((1, d), lambda i: (0, 0))],
        out_specs=[pl.BlockSpec((tr, d), lambda i: (i, 0)), pl.BlockSpec((tr, 1), lambda i: (i, 0))],
        out_shape=[jax.ShapeDtypeStruct((s, d), MXU), jax.ShapeDtypeStruct((s, 1), F32)],
        args=(x, g), name=name, comm=comm)
    return tuple(outs) if comm is None else (tuple(outs), couts)


def _loss_head(x, g, target, name):
    s, d = x.shape
    tr = _tile(s, 256)

    def body(x_ref, g_ref, t_ref, loss_ref, dx_ref, dg_ref):
        @pl.when(pl.program_id(0) == 0)
        def _():
            dg_ref[...] = jnp.zeros_like(dg_ref)
            loss_ref[...] = jnp.zeros_like(loss_ref)

        xf = x_ref[...]
        gv = g_ref[...]
        r = lax.rsqrt(jnp.mean(xf * xf, axis=-1, keepdims=True) + EPS)
        xhat = xf * r
        err = xhat * gv - t_ref[...]
        loss_ref[...] += 0.5 * jnp.sum(jnp.mean(err * err, axis=-1, keepdims=True), axis=0, keepdims=True)
        dy = err * (1.0 / d)
        dxhat = dy * gv
        mean = jnp.mean(dxhat * xhat, axis=-1, keepdims=True)
        dx_ref[...] = r * (dxhat - xhat * mean)
        dg_ref[...] += jnp.sum(dy * xhat, axis=0, keepdims=True)

    row = pl.BlockSpec((tr, d), lambda i: (i, 0))
    vec = pl.BlockSpec((1, d), lambda i: (0, 0))
    return pl.pallas_call(
        body,
        grid=(s // tr,),
        in_specs=[row, vec, row],
        out_specs=[pl.BlockSpec((1, LANES), lambda i: (0, 0)), row, vec],
        out_shape=[jax.ShapeDtypeStruct((1, LANES), F32), jax.ShapeDtypeStruct((s, d), F32),
                   jax.ShapeDtypeStruct((1, d), F32)],
        compiler_params=_cparams(),
        name=name,
    )(x, g, target)


def _mix_fwd(oa, ob, wa_t, wb_t, gates, name):
    s, kd = oa.shape
    d = wa_t.shape[0]
    tm, tn = _tile(s, 1024), _tile(d, 512)
    nj = d // tn

    def body(oa_ref, ob_ref, wa_ref, wb_ref, ga_ref, gb_ref, out_ref):
        ya = _dot_nt(oa_ref[...], wa_ref[...])
        yb = _dot_nt(ob_ref[...], wb_ref[...])
        out_ref[...] = (_sigmoid(ga_ref[...].astype(F32)) * ya
                        + _sigmoid(gb_ref[...].astype(F32)) * yb).astype(out_ref.dtype)

    o_spec = pl.BlockSpec((tm, kd), lambda i, j: (i, 0))
    w_spec = pl.BlockSpec((tn, kd), lambda i, j: (j, 0))
    return pl.pallas_call(
        body,
        grid=(s // tm, nj),
        in_specs=[o_spec, o_spec, w_spec, w_spec,
                  pl.BlockSpec((tm, tn), lambda i, j: (i, j)),
                  pl.BlockSpec((tm, tn), lambda i, j: (i, j + nj))],
        out_specs=pl.BlockSpec((tm, tn), lambda i, j: (i, j)),
        out_shape=jax.ShapeDtypeStruct((s, d), MXU),
        compiler_params=_cparams(),
        name=name,
    )(oa, ob, wa_t, wb_t, gates, gates)


def _mix_bwd(dx, w_o, oa, ob, wa_t, wb_t, gates, name):
    s, kd = oa.shape
    d = wa_t.shape[0]
    tm, tn = _tile(s, 1024), _tile(d, 512)
    nj = d // tn

    def body(dx_ref, wo_ref, oa_ref, ob_ref, wa_ref, wb_ref, ga_ref, gb_ref,
             dya_ref, dyb_ref, dga_ref, dgb_ref):
        dm = _dot_nt(dx_ref[...], wo_ref[...])
        ya = _dot_nt(oa_ref[...], wa_ref[...])
        yb = _dot_nt(ob_ref[...], wb_ref[...])
        sa = _sigmoid(ga_ref[...].astype(F32))
        sb = _sigmoid(gb_ref[...].astype(F32))
        dya_ref[...] = (dm * sa).astype(dya_ref.dtype)
        dyb_ref[...] = (dm * sb).astype(dyb_ref.dtype)
        dga_ref[...] = (dm * ya * sa * (1.0 - sa)).astype(dga_ref.dtype)
        dgb_ref[...] = (dm * yb * sb * (1.0 - sb)).astype(dgb_ref.dtype)

    o_spec = pl.BlockSpec((tm, kd), lambda i, j: (i, 0))
    w_spec = pl.BlockSpec((tn, kd), lambda i, j: (j, 0))
    t_spec = pl.BlockSpec((tm, tn), lambda i, j: (i, j))
    return pl.pallas_call(
        body,
        grid=(s // tm, nj),
        in_specs=[pl.BlockSpec((tm, d), lambda i, j: (i, 0)),
                  pl.BlockSpec((tn, d), lambda i, j: (j, 0)),
                  o_spec, o_spec, w_spec, w_spec, t_spec,
                  pl.BlockSpec((tm, tn), lambda i, j: (i, j + nj))],
        out_specs=[t_spec] * 4,
        out_shape=[jax.ShapeDtypeStruct((s, d), MXU)] * 4,
        compiler_params=_cparams(),
        name=name,
    )(dx, w_o, oa, ob, wa_t, wb_t, gates, gates)


def _ple(p, w_pe_t, h, w_pg, other, *, backward, name):
    s, kp = p.shape
    d = w_pe_t.shape[0]
    tm, tn = _tile(s, 1024), _tile(d, 512)

    def body(p_ref, wpe_ref, h_ref, wpg_ref, other_ref, *out_refs):
        pe = _dot_nt(p_ref[...].astype(MXU), wpe_ref[...])
        gt = _dot(h_ref[...], wpg_ref[...])
        sg = _sigmoid(gt)
        if backward:
            dout = other_ref[...]
            out_refs[0][...] = (dout * sg).astype(out_refs[0].dtype)
            out_refs[1][...] = (dout * pe * sg * (1.0 - sg)).astype(out_refs[1].dtype)
        else:
            out_refs[0][...] = other_ref[...] + pe * sg

    t_spec = pl.BlockSpec((tm, tn), lambda i, j: (i, j))
    if backward:
        out_specs, out_shape = [t_spec, t_spec], [jax.ShapeDtypeStruct((s, d), MXU)] * 2
    else:
        out_specs, out_shape = [t_spec], [jax.ShapeDtypeStruct((s, d), F32)]
    outs = pl.pallas_call(
        body,
        grid=(s // tm, d // tn),
        in_specs=[pl.BlockSpec((tm, kp), lambda i, j: (i, 0)),
                  pl.BlockSpec((tn, kp), lambda i, j: (j, 0)),
                  pl.BlockSpec((tm, d), lambda i, j: (i, 0)),
                  pl.BlockSpec((d, tn), lambda i, j: (0, j)),
                  t_spec],
        out_specs=out_specs,
        out_shape=out_shape,
        compiler_params=_cparams(),
        name=name,
    )(p, w_pe_t, h, w_pg, other)
    return tuple(outs) if backward else outs[0]


def _split_dot(x, tri):
    hi = x.astype(jnp.bfloat16)
    lo = (x - hi.astype(F32)).astype(jnp.bfloat16)
    return _dot(hi, tri) + _dot(lo, tri)


def _log_sigmoids(z):
    lb = jnp.minimum(z, 0.0) - jnp.log1p(jnp.exp(-jnp.abs(z)))
    return lb, lb - z


def _head_lanes(hh):
    lane = lax.broadcasted_iota(jnp.int32, (1, LANES), 1)
    return jnp.logical_and(lane >= hh * HEAD_DIM, lane < (hh + 1) * HEAD_DIM)


def _sb_fwd(qkv, name, comm=None):
    s = qkv.shape[0]
    tq = _tile(s, SB_TQ)
    nsub = SB_SUB if (s // tq) % SB_SUB == 0 else 1

    def body(q_ref, k_ref, v_ref, o_ref):
        row = lax.broadcasted_iota(jnp.int32, (tq, tq), 0)
        col = lax.broadcasted_iota(jnp.int32, (tq, tq), 1)
        causal = col < row
        tri = jnp.where(row > col, 1.0, 0.0).astype(jnp.bfloat16)
        started = [_sb_fwd_straight(q_ref, k_ref, v_ref, pl.program_id(1) * nsub + sub, sub, tq, causal, tri)
                   for sub in range(nsub)]
        for sub, (block, i, cs, accs) in enumerate(started):
            def top(cs):
                return jnp.maximum(jnp.max(cs[0]), jnp.max(cs[1]))

            def live(st):
                return jnp.logical_and(st[0] >= 0, st[1] > SB_DEAD)

            def walk(st, block=block):
                cs, accs = block(st[0], st[2], st[3], False)
                return st[0] - 1, top(cs), cs, accs

            accs = lax.while_loop(live, walk, (i - 2, top(cs), cs, accs))[3]
            o_ref[sub * tq:(sub + 1) * tq, :] = jnp.where(_head_lanes(0), accs[0], accs[1]).astype(o_ref.dtype)

    outs, couts = _call(
        body,
        grid=(N_PAIR, s // (nsub * tq)),
        in_specs=[pl.BlockSpec((nsub * tq, LANES), lambda p, i: (i, p)),
                  pl.BlockSpec((s, LANES), lambda p, i: (0, N_PAIR + p)),
                  pl.BlockSpec((s, LANES), lambda p, i: (0, 2 * N_PAIR + p))],
        out_specs=[pl.BlockSpec((nsub * tq, LANES), lambda p, i: (i, p))],
        out_shape=[jax.ShapeDtypeStruct((s, SB_W), MXU)],
        args=(qkv, qkv, qkv), name=name, comm=comm)
    return outs[0] if comm is None else (outs[0], couts)


def _sb_fwd_straight(q_ref, k_ref, v_ref, i, sub, tq, causal, tri):
    qf = q_ref[sub * tq:(sub + 1) * tq, :].astype(F32) * SCALE
    qms = [jnp.where(_head_lanes(hh), qf, 0.0).astype(MXU) for hh in range(2)]

    def block(kb, cs, accs, masked, gate=None):
        rows = pl.ds(pl.multiple_of(kb * tq, tq), tq)
        ks, vs = k_ref[rows, :], v_ref[rows, :]
        new_c, new_acc = [], []
        for hh in range(2):
            lb, lm = _log_sigmoids(_dot_nt(qms[hh], ks))
            if masked:
                lm = jnp.where(causal, lm, 0.0)
            if gate is not None:
                lm = lm * gate
            a = jnp.exp(lb + _split_dot(lm, tri) + cs[hh])
            if masked:
                a = jnp.where(causal, a, 0.0)
            if gate is not None:
                a = a * gate
            new_acc.append(accs[hh] + _dot(a.astype(MXU), vs))
            new_c.append(cs[hh] + jnp.sum(lm, axis=1, keepdims=True))
        return tuple(new_c), tuple(new_acc)

    zc, za = jnp.zeros((tq, 1), F32), jnp.zeros((tq, LANES), F32)
    cs, accs = block(i, (zc, zc), (za, za), True)
    cs, accs = block(jnp.maximum(i - 1, 0), cs, accs, False, jnp.where(i > 0, 1.0, 0.0))
    return block, i, cs, accs


def _sb_bwd(qkv, do, name, comm=None):
    s = qkv.shape[0]
    tq = _tile(s, SB_TQ)
    nq = s // tq
    nsub = SB_SUB if nq % SB_SUB == 0 else 1
    nsteps = nq // nsub

    def body(q_ref, k_ref, v_ref, do_ref, dq_ref, dk_ref, dv_ref, dk_acc, dv_acc, carries):
        step = pl.program_id(1)

        @pl.when(step == 0)
        def _():
            dk_acc[...] = jnp.zeros_like(dk_acc)
            dv_acc[...] = jnp.zeros_like(dv_acc)

        row = lax.broadcasted_iota(jnp.int32, (tq, tq), 0)
        col = lax.broadcasted_iota(jnp.int32, (tq, tq), 1)
        causal = col < row
        tri_rev = jnp.where(row > col, 1.0, 0.0).astype(jnp.bfloat16)
        tri_excl = jnp.where(row < col, 1.0, 0.0).astype(jnp.bfloat16)
        zc, za = jnp.zeros((tq, 1), F32), jnp.zeros((tq, LANES), F32)

        def top(cs):
            return jnp.maximum(jnp.max(cs[0]), jnp.max(cs[1]))

        def live(st):
            return jnp.logical_and(st[0] >= 0, st[1] > SB_DEAD)

        def row_sums(pre):
            return [jnp.sum(lm, axis=1, keepdims=True) for _, lm in pre]

        def query_block(sub):
            i = step * nsub + sub
            q_rows = slice(sub * tq, (sub + 1) * tq)
            qf = q_ref[q_rows, :].astype(F32) * SCALE
            dof = do_ref[q_rows, :]
            qms = [jnp.where(_head_lanes(hh), qf, 0.0).astype(MXU) for hh in range(2)]
            doms = [jnp.where(_head_lanes(hh), dof, jnp.zeros_like(dof)) for hh in range(2)]

            def terms(kb, masked):
                rows = pl.ds(pl.multiple_of(kb * tq, tq), tq)
                ks = k_ref[rows, :]
                out = []
                for hh in range(2):
                    lb, lm = _log_sigmoids(_dot_nt(qms[hh], ks))
                    if masked:
                        lm = jnp.where(causal, lm, 0.0)
                    out.append((lb, lm))
                return out

            def block(kb, cs, gpres, dqs, masked, gate=None, pre=None):
                rows = pl.ds(pl.multiple_of(kb * tq, tq), tq)
                ks, vs = k_ref[rows, :], v_ref[rows, :]
                pre = terms(kb, masked) if pre is None else pre
                new_g, new_dq = [], []
                dk_add, dv_add = None, None
                for hh in range(2):
                    lb, lm = pre[hh]
                    a = jnp.exp(lb + _split_dot(lm, tri_rev) + cs[hh])
                    if masked:
                        a = jnp.where(causal, a, 0.0)
                    if gate is not None:
                        a = a * gate
                    g = a * _dot_nt(doms[hh], vs)
                    gsum = gpres[hh] + _split_dot(g, tri_excl)
                    dz = g - (g + gsum) * jnp.exp(lb)
                    if masked:
                        dz = jnp.where(causal, dz, 0.0)
                    if gate is not None:
                        dz = dz * gate
                    dzb = dz.astype(MXU)
                    new_dq.append(dqs[hh] + _dot(dzb, ks))
                    dk_h = _dot_tn(dzb, qms[hh])
                    dv_h = _dot_tn(a.astype(MXU), doms[hh])
                    dk_add = dk_h if dk_add is None else dk_add + dk_h
                    dv_add = dv_h if dv_add is None else dv_add + dv_h
                    new_g.append(gpres[hh] + jnp.sum(g, axis=1, keepdims=True))
                dk_acc[rows, :] += dk_add
                dv_acc[rows, :] += dv_add
                return tuple(new_g), tuple(new_dq)

            prev = jnp.maximum(i - 1, 0)
            gate = jnp.where(i > 0, 1.0, 0.0)
            t_diag, t_prev = terms(i, True), terms(prev, False)
            c_diag = row_sums(t_diag)
            sums = row_sums(t_prev)
            c_prev = tuple(c_diag[hh] + sums[hh] * gate for hh in range(2))
            return dict(i=i, prev=prev, gate=gate, q_rows=q_rows, terms=terms, block=block,
                        t_diag=t_diag, t_prev=t_prev, c_diag=c_diag, c_prev=c_prev)

        blocks = [query_block(sub) for sub in range(nsub)]
        for qb in blocks:
            def record(st, qb=qb):
                kb, cs = st[0], st[2]
                sums = row_sums(qb["terms"](kb, False))
                for hh in range(2):
                    carries[hh, kb] = cs[hh]
                cs = tuple(cs[hh] + sums[hh] for hh in range(2))
                return kb - 1, top(cs), cs

            first = lax.while_loop(live, record, (qb["i"] - 2, top(qb["c_prev"]), qb["c_prev"]))[0] + 1
            qb["mid"] = lax.fori_loop(
                first, qb["i"] - 1,
                lambda kb, cr, qb=qb: qb["block"](kb, (carries[0, kb], carries[1, kb]), cr[0], cr[1], False),
                ((zc, zc), (za, za)))
        for qb in blocks:
            gpres, dqs = qb["block"](qb["prev"], qb["c_diag"], *qb["mid"], False, qb["gate"], qb["t_prev"])
            dqs = qb["block"](qb["i"], (zc, zc), gpres, dqs, True, None, qb["t_diag"])[1]
            dq_ref[qb["q_rows"], :] = (jnp.where(_head_lanes(0), dqs[0], dqs[1]) * SCALE).astype(dq_ref.dtype)

        @pl.when(step == nsteps - 1)
        def _():
            dk_ref[...] = dk_acc[...].astype(dk_ref.dtype)
            dv_ref[...] = dv_acc[...].astype(dv_ref.dtype)

    blk = pl.BlockSpec((nsub * tq, LANES), lambda p, i: (i, p))
    full = pl.BlockSpec((s, LANES), lambda p, i: (0, p))
    outs, couts = _call(
        body,
        grid=(N_PAIR, nsteps),
        in_specs=[blk,
                  pl.BlockSpec((s, LANES), lambda p, i: (0, N_PAIR + p)),
                  pl.BlockSpec((s, LANES), lambda p, i: (0, 2 * N_PAIR + p)),
                  blk],
        out_specs=[blk, full, full],
        out_shape=[jax.ShapeDtypeStruct((s, SB_W), MXU)] * 3,
        scratch_shapes=[pltpu.VMEM((s, LANES), F32), pltpu.VMEM((s, LANES), F32),
                        pltpu.VMEM((2, nq, tq, 1), F32)],
        args=(qkv, qkv, qkv, do), name=name, comm=comm)
    return tuple(outs) if comm is None else (tuple(outs), couts)


def _bucket_table():
    i = np.arange(BLOCK)[:, None]
    j = np.arange(2 * BLOCK)[None, :]
    d = np.maximum(BLOCK + i - j, 0)
    max_exact = N_BUCKETS // 2
    df = np.maximum(d, 1).astype(np.float32)
    large = max_exact + (np.log(df / max_exact) / math.log(MAX_DISTANCE / max_exact)
                         * (N_BUCKETS - max_exact)).astype(np.int32)
    large = np.minimum(large, N_BUCKETS - 1)
    return np.where(d < max_exact, d, large).astype(np.int32)


def _build_bias(rel_bias, buckets, name):
    def body(rb_ref, bk_ref, out_ref):
        h = pl.program_id(0)
        bk = bk_ref[...]
        acc = jnp.zeros(bk.shape, F32)
        for b in range(N_BUCKETS):
            acc = jnp.where(bk == b, rb_ref[b, h], acc)
        out_ref[...] = acc

    return pl.pallas_call(
        body,
        grid=(SW_HEADS,),
        in_specs=[pl.BlockSpec(memory_space=pltpu.SMEM),
                  pl.BlockSpec((BLOCK, 2 * BLOCK), lambda h: (0, 0))],
        out_specs=pl.BlockSpec((None, BLOCK, 2 * BLOCK), lambda h: (h, 0, 0)),
        out_shape=jax.ShapeDtypeStruct((SW_HEADS, BLOCK, 2 * BLOCK), F32),
        name=name,
    )(rel_bias, buckets)


def _bias_grad(dbias_layers, buckets, name):
    n_l = len(dbias_layers)

    def body(*refs):
        bk = refs[n_l][...]
        out_ref = refs[n_l + 1]
        db = refs[0][...]
        for r in refs[1:n_l]:
            db = db + r[...]
        lane = lax.broadcasted_iota(jnp.int32, (1, LANES), 1)
        acc = jnp.zeros((1, LANES), F32)
        for b in range(N_BUCKETS):
            part = jnp.sum(jnp.where(bk == b, db, 0.0), axis=1, keepdims=True)
            tot = jnp.sum(part, axis=0, keepdims=True)
            acc = jnp.where(lane == b, tot, acc)
        out_ref[...] = acc

    hspec = pl.BlockSpec((None, BLOCK, 2 * BLOCK), lambda h: (h, 0, 0))
    return pl.pallas_call(
        body,
        grid=(SW_HEADS,),
        in_specs=[hspec] * n_l + [pl.BlockSpec((BLOCK, 2 * BLOCK), lambda h: (0, 0))],
        out_specs=pl.BlockSpec((None, 1, LANES), lambda h: (h, 0, 0)),
        out_shape=jax.ShapeDtypeStruct((SW_HEADS, 1, LANES), F32),
        name=name,
    )(*dbias_layers, buckets)


GROUP_ROWS = SW_GROUP * BLOCK


def _group_lanes(g):
    lane = lax.broadcasted_iota(jnp.int32, (1, LANES), 1)
    gvec = jnp.zeros((1, LANES), jnp.int32) + g
    return jnp.where(lane >= HEAD_DIM, 1, 0) == gvec, gvec


def _stack_heads(x, g):
    kv_lanes, gvec = _group_lanes(g)
    parts = []
    for j in range(SW_GROUP):
        half = x[:, (j // 2) * LANES:(j // 2 + 1) * LANES]
        moved = jnp.where(gvec == j % 2, half, pltpu.roll(half, HEAD_DIM, 1))
        parts.append(jnp.where(kv_lanes, moved, 0.0))
    return jnp.concatenate(parts, axis=0)


def _unstack_heads(y, g):
    _, gvec = _group_lanes(g)
    heads = []
    for j in range(SW_GROUP):
        yj = y[j * BLOCK:(j + 1) * BLOCK]
        heads.append(jnp.where(gvec == j % 2, yj, pltpu.roll(yj, HEAD_DIM, 1)))
    pairs = [jnp.where(_head_lanes(0), heads[2 * p], heads[2 * p + 1]) for p in range(SW_GROUP // 2)]
    return jnp.concatenate(pairs, axis=1)


def _per_head_col(values):
    return jnp.concatenate([jnp.zeros((BLOCK, 1), F32) + v for v in values], axis=0)


def _swa_scores(qs, kp, kc, bias_ref, n):
    row = jnp.bitwise_and(lax.broadcasted_iota(jnp.int32, (GROUP_ROWS, BLOCK), 0), BLOCK - 1)
    col = lax.broadcasted_iota(jnp.int32, (GROUP_ROWS, BLOCK), 1)
    bias = bias_ref[...].reshape(GROUP_ROWS, 2 * BLOCK)
    s1 = _dot_nt(qs, kp) + bias[:, :BLOCK]
    s2 = _dot_nt(qs, kc) + bias[:, BLOCK:]
    no_prev = jnp.where(n > 0, 0, BLOCK)
    s1 = jnp.where(col > row + no_prev, s1, NEG)
    s2 = jnp.where(col <= row, s2, NEG)
    return s1, s2


def _swa_specs(s):
    q_blk = 3 * SB_W // (2 * LANES)
    k_blk = (3 * SB_W + SW_W) // LANES
    return (pl.BlockSpec((s, 2 * LANES), lambda g: (0, q_blk + g)),
            pl.BlockSpec((s, LANES), lambda g: (0, k_blk)),
            pl.BlockSpec((s, LANES), lambda g: (0, k_blk + 1)))


def _swa_fwd(qkv, bias, sinks, name, comm=None):
    s = qkv.shape[0]
    nb = s // BLOCK

    def body(sink_ref, q_ref, k_ref, v_ref, bias_ref, o_ref, lse_ref):
        g = pl.program_id(0)
        sink = _per_head_col([sink_ref[SW_GROUP * g + j] for j in range(SW_GROUP)])
        lane = lax.broadcasted_iota(jnp.int32, (1, LANES), 1)

        def step(n, carry):
            r0 = pl.multiple_of(n * BLOCK, BLOCK)
            p0 = pl.multiple_of(jnp.maximum(n - 1, 0) * BLOCK, BLOCK)
            cur, prev = pl.ds(r0, BLOCK), pl.ds(p0, BLOCK)
            qs = _stack_heads(q_ref[cur, :].astype(F32) * SCALE, g).astype(MXU)
            s1, s2 = _swa_scores(qs, k_ref[prev, :], k_ref[cur, :], bias_ref, n)
            m = jnp.maximum(jnp.max(jnp.maximum(s1, s2), axis=1, keepdims=True), sink)
            e1 = jnp.exp(s1 - m)
            e2 = jnp.exp(s2 - m)
            den = jnp.sum(e1 + e2, axis=1, keepdims=True) + jnp.exp(sink - m)
            o = _dot((e1 / den).astype(MXU), v_ref[prev, :]) + _dot((e2 / den).astype(MXU), v_ref[cur, :])
            o_ref[cur, :] = _unstack_heads(o, g).astype(o_ref.dtype)
            lse = m + jnp.log(den)
            lse_row = jnp.zeros((BLOCK, LANES), F32)
            for j in range(SW_GROUP):
                lse_row = jnp.where(lane == j, lse[j * BLOCK:(j + 1) * BLOCK], lse_row)
            lse_ref[cur, :] = lse_row
            return carry

        lax.fori_loop(0, nb, step, 0)

    outs, couts = _call(
        body,
        grid=(SW_KV,),
        in_specs=[pl.BlockSpec(memory_space=pltpu.SMEM), *_swa_specs(s),
                  pl.BlockSpec((SW_GROUP, BLOCK, 2 * BLOCK), lambda g: (g, 0, 0))],
        out_specs=[pl.BlockSpec((s, 2 * LANES), lambda g: (0, g)),
                   pl.BlockSpec((None, s, LANES), lambda g: (g, 0, 0))],
        out_shape=[jax.ShapeDtypeStruct((s, SW_W), MXU), jax.ShapeDtypeStruct((SW_KV, s, LANES), F32)],
        args=(sinks, qkv, qkv, qkv, bias), name=name, comm=comm)
    return tuple(outs) if comm is None else (tuple(outs), couts)


def _swa_bwd(qkv, bias, sinks, do, lse, name, comm=None):
    s = qkv.shape[0]
    nb = s // BLOCK

    def body(sink_ref, q_ref, k_ref, v_ref, bias_ref, do_ref, lse_ref,
             dq_ref, dk_ref, dv_ref, dbias_ref, dsink_ref, dk_acc, dv_acc):
        g = pl.program_id(0)
        sink = _per_head_col([sink_ref[SW_GROUP * g + j] for j in range(SW_GROUP)])
        lane = lax.broadcasted_iota(jnp.int32, (1, LANES), 1)

        @pl.when(g == 0)
        def _():
            dk_acc[...] = jnp.zeros_like(dk_acc)
            dv_acc[...] = jnp.zeros_like(dv_acc)

        dbias_ref[...] = jnp.zeros_like(dbias_ref)

        def step(n, dsink_rows):
            r0 = pl.multiple_of(n * BLOCK, BLOCK)
            p0 = pl.multiple_of(jnp.maximum(n - 1, 0) * BLOCK, BLOCK)
            cur, prev = pl.ds(r0, BLOCK), pl.ds(p0, BLOCK)
            qs = _stack_heads(q_ref[cur, :].astype(F32) * SCALE, g).astype(MXU)
            dos = _stack_heads(do_ref[cur, :].astype(F32), g).astype(MXU)
            kp, kc, vp, vc = k_ref[prev, :], k_ref[cur, :], v_ref[prev, :], v_ref[cur, :]
            lse_row = lse_ref[cur, :]
            lse = jnp.concatenate([jnp.sum(jnp.where(lane == j, lse_row, 0.0), axis=1, keepdims=True)
                                   for j in range(SW_GROUP)], axis=0)
            s1, s2 = _swa_scores(qs, kp, kc, bias_ref, n)
            pr1 = jnp.exp(s1 - lse)
            pr2 = jnp.exp(s2 - lse)
            dpr1 = _dot_nt(dos, vp)
            dpr2 = _dot_nt(dos, vc)
            delta = jnp.sum(pr1 * dpr1 + pr2 * dpr2, axis=1, keepdims=True)
            ds1 = pr1 * (dpr1 - delta)
            ds2 = pr2 * (dpr2 - delta)
            dbias_ref[:, :, :BLOCK] += ds1.reshape(SW_GROUP, BLOCK, BLOCK)
            dbias_ref[:, :, BLOCK:] += ds2.reshape(SW_GROUP, BLOCK, BLOCK)
            ds1b, ds2b = ds1.astype(MXU), ds2.astype(MXU)
            dq = _dot(ds1b, kp) + _dot(ds2b, kc)
            dq_ref[cur, :] = (_unstack_heads(dq, g) * SCALE).astype(dq_ref.dtype)
            dk_acc[prev, :] += _dot_tn(ds1b, qs)
            dk_acc[cur, :] += _dot_tn(ds2b, qs)
            dv_acc[prev, :] += _dot_tn(pr1.astype(MXU), dos)
            dv_acc[cur, :] += _dot_tn(pr2.astype(MXU), dos)
            return dsink_rows - jnp.exp(sink - lse) * delta

        rows = lax.fori_loop(0, nb, step, jnp.zeros((GROUP_ROWS, 1), F32))
        for j in range(SW_GROUP):
            dsink_ref[j] = jnp.broadcast_to(jnp.sum(rows[j * BLOCK:(j + 1) * BLOCK], axis=0, keepdims=True),
                                            (1, LANES))

        @pl.when(g == SW_KV - 1)
        def _():
            dk_ref[...] = dk_acc[...].astype(dk_ref.dtype)
            dv_ref[...] = dv_acc[...].astype(dv_ref.dtype)

    grp = pl.BlockSpec((s, 2 * LANES), lambda g: (0, g))
    kv_out = pl.BlockSpec((s, LANES), lambda g: (0, 0))
    bspec = pl.BlockSpec((SW_GROUP, BLOCK, 2 * BLOCK), lambda g: (g, 0, 0))
    outs, couts = _call(
        body,
        grid=(SW_KV,),
        in_specs=[pl.BlockSpec(memory_space=pltpu.SMEM), *_swa_specs(s), bspec, grp,
                  pl.BlockSpec((None, s, LANES), lambda g: (g, 0, 0))],
        out_specs=[grp, kv_out, kv_out, bspec, pl.BlockSpec((SW_GROUP, 1, LANES), lambda g: (g, 0, 0))],
        out_shape=[jax.ShapeDtypeStruct((s, SW_W), MXU),
                   jax.ShapeDtypeStruct((s, LANES), MXU),
                   jax.ShapeDtypeStruct((s, LANES), MXU),
                   jax.ShapeDtypeStruct((SW_HEADS, BLOCK, 2 * BLOCK), F32),
                   jax.ShapeDtypeStruct((SW_HEADS, 1, LANES), F32)],
        scratch_shapes=[pltpu.VMEM((s, LANES), F32), pltpu.VMEM((s, LANES), F32)],
        args=(sinks, qkv, qkv, qkv, bias, do, lse), name=name, comm=comm)
    return tuple(outs) if comm is None else (tuple(outs), couts)


class _NoPlan:
    def comm(self, name):
        return None

    def done(self, name, outs):
        pass

    def grad(self, layer, name, value):
        pass


def _run(plan, fn, *args, name, **kw):
    comm = plan.comm(name)
    if comm is None:
        return fn(*args, name=name, **kw)
    res, outs = fn(*args, name=name, comm=comm, **kw)
    plan.done(name, outs)
    return res


def _norm_bwd(dh, x, dres, r, g):
    xhat = x * r
    dxhat = dh * g
    dx = dres + r * (dxhat - xhat * jnp.mean(dxhat * xhat, axis=-1, keepdims=True))
    return dx, dx, jnp.sum(dh * xhat, axis=0, keepdims=True)


def _residual_norm(acc, res, g):
    x = res + acc
    r = lax.rsqrt(jnp.mean(x * x, axis=-1, keepdims=True) + EPS)
    return x, (x * r) * g, r


def _layer_fwd(x, p, w, g_mix, g_mlp, g_pe, sinks, bias, tag, plan):
    h1, r1 = _run(plan, _rms_fwd, x, g_mix, name=f"rms_mix_{tag}")
    qkv = _run(plan, _mm, h1, w["w_qkv_t"], tb=True, out_dtypes=(MXU,), name=f"proj_qkv_{tag}")
    gates = _run(plan, _mm, h1, w["w_gate_t"], tb=True, out_dtypes=(MXU,), name=f"proj_gate_{tag}")
    oa = _run(plan, _sb_fwd, qkv, name=f"sb_fwd_{tag}")
    ob, lse = _run(plan, _swa_fwd, qkv, bias, sinks, name=f"swa_fwd_{tag}")
    merged = _mix_fwd(oa, ob, w["w_up_a_t"], w["w_up_b_t"], gates, f"mix_fwd_{tag}")
    d = x.shape[1]
    normed = (F32, MXU, (F32, "col"))
    x1, h2, r2 = _run(plan, _mm, merged, w["w_o"], extras=(x, g_mlp), epi=_residual_norm, out_dtypes=normed,
                      tn=d, name=f"out_proj_{tag}")
    u, act = _run(plan, _mm, h2, w["w_ff1_t"], tb=True,
                  epi=lambda acc: (acc, jnp.square(jnp.maximum(acc, 0.0))),
                  out_dtypes=(MXU, MXU), name=f"ff1_{tag}")
    x2, h3, r3 = _run(plan, _mm, act, w["w_ff2"], extras=(x1, g_pe), epi=_residual_norm, out_dtypes=normed,
                      tn=d, name=f"ff2_{tag}")
    x3 = _ple(p, w["w_pe_t"], h3, w["w_pg"], x2, backward=False, name=f"ple_fwd_{tag}")
    saved = dict(x=x, h1=h1, r1=r1, gates=gates, qkv=qkv, lse=lse, oa=oa, ob=ob, merged=merged,
                 x1=x1, h2=h2, r2=r2, u=u, act=act, x2=x2, h3=h3, r3=r3)
    return x3, saved


def _layer_bwd(dx3, sv, p, w, g_mix, g_mlp, g_pe, sinks, bias, layer, plan):
    tag = f"l{layer}"
    gw = {}
    wire = (WIRE,)

    def dw(name, a, b):
        gw[name] = _run(plan, _mm, a, b, ta=True, out_dtypes=wire, tk=2048, name=f"d{name}_{tag}")
        plan.grad(layer, name, gw[name])

    dpe, dgt = _ple(p, w["w_pe_t"], sv["h3"], w["w_pg"], dx3, backward=True, name=f"ple_bwd_{tag}")
    dw("w_pe", dpe, p)
    dw("w_pg", sv["h3"], dgt)
    d = dx3.shape[1]
    grads = (F32, MXU, (F32, "rowsum"))
    dx2, dx2b, dg_pe = _run(plan, _mm, dgt, w["w_pg"], tb=True, extras=(sv["x2"], dx3, sv["r3"], g_pe),
                            epi=_norm_bwd, out_dtypes=grads, tm=512, tn=d, name=f"dh_pe_{tag}")
    dw("w_ff2", sv["act"], dx2b)
    du = _run(plan, _mm, dx2b, w["w_ff2"], tb=True, extras=(sv["u"],),
              epi=lambda acc, u: acc * (2.0 * jnp.maximum(u.astype(F32), 0.0)), out_dtypes=(MXU,),
              name=f"dact_{tag}")
    dw("w_ff1", du, sv["h2"])
    dx1, dx1b, dg_mlp = _run(plan, _mm, du, w["w_ff1_t"], extras=(sv["x1"], dx2, sv["r2"], g_mlp),
                             epi=_norm_bwd, out_dtypes=grads, tm=1024, tn=d, name=f"dh_mlp_{tag}")
    dw("w_o", sv["merged"], dx1b)
    dya, dyb, dga, dgb = _mix_bwd(dx1b, w["w_o"], sv["oa"], sv["ob"], w["w_up_a_t"], w["w_up_b_t"],
                                  sv["gates"], f"mix_bwd_{tag}")
    dw("w_up_a", dya, sv["oa"])
    dw("w_up_b", dyb, sv["ob"])
    doa = _run(plan, _mm, dya, w["w_up_a_t"], out_dtypes=(MXU,), name=f"do_a_{tag}")
    dob = _run(plan, _mm, dyb, w["w_up_b_t"], out_dtypes=(MXU,), name=f"do_b_{tag}")
    dqb, dkb, dvb, dbias, dsink = _run(plan, _swa_bwd, sv["qkv"], bias, sinks, dob, sv["lse"],
                                       name=f"swa_bwd_{tag}")
    dqa, dka, dva = _run(plan, _sb_bwd, sv["qkv"], doa, name=f"sb_bwd_{tag}")
    dqkv = jnp.concatenate([dqa, dka, dva, dqb, dkb, dvb], axis=1)
    gw_qkv = _mm(dqkv, sv["h1"], ta=True, out_dtypes=wire, tk=2048, name=f"dw_qkv_{tag}")
    gw_ga = _mm(dga, sv["h1"], ta=True, out_dtypes=wire, tk=2048, name=f"dw_ga_{tag}")
    gw_gb = _mm(dgb, sv["h1"], ta=True, out_dtypes=wire, tk=2048, name=f"dw_gb_{tag}")
    gw["w_in"] = jnp.concatenate([gw_qkv, gw_ga, gw_gb], axis=0)
    plan.grad(layer, "w_in", gw["w_in"])
    d = dga.shape[1]
    add = lambda acc, res: res + acc
    dh1 = _run(plan, _mm, dga, w["w_gate_t"][:d], name=f"dh_ga_{tag}")
    dh1 = _run(plan, _mm, dgb, w["w_gate_t"][d:], extras=(dh1,), epi=add, name=f"dh_gb_{tag}")
    dx, _, dg_mix = _run(plan, _mm, dqkv, w["w_qkv_t"], tk=768, extras=(dh1, sv["x"], dx1, sv["r1"], g_mix),
                         epi=lambda acc, prev, *rest: _norm_bwd(acc + prev, *rest), out_dtypes=grads,
                         tm=512, tn=d, name=f"dh_qkv_{tag}")
    small = dict(g_mix=dg_mix, g_mlp=dg_mlp, g_pe=dg_pe, sinks=dsink[:, 0, 0], dbias=dbias)
    return dx, gw, small


def _local_step(x, p, target, weights, g_mix, g_mlp, g_pe, g_final, sinks, rel_bias, plan=None):
    plan = _NoPlan() if plan is None else plan
    depth = g_mix.shape[0]
    buckets = jnp.asarray(_bucket_table())
    bias = _build_bias(rel_bias, buckets, "build_bias")
    saved, wfull = [], []
    h = x
    for l in range(depth):
        wfull.append(weights(l))
        h, sv = _layer_fwd(h, p[l], wfull[l], g_mix[l:l + 1], g_mlp[l:l + 1], g_pe[l:l + 1],
                           sinks[l], bias, f"l{l}", plan)
        saved.append(sv)
    loss_row, dx, dg_final = _loss_head(h, g_final[None, :], target, "loss_head")
    gws = [None] * depth
    smalls = [None] * depth
    for l in reversed(range(depth)):
        dx, gws[l], smalls[l] = _layer_bwd(dx, saved[l], p[l], wfull[l], g_mix[l:l + 1], g_mlp[l:l + 1],
                                           g_pe[l:l + 1], sinks[l], bias, l, plan)
    drel = _bias_grad([sm["dbias"] for sm in smalls], buckets, "bias_grad")[:, 0, :N_BUCKETS].T
    small = dict(
        g_mix=jnp.concatenate([sm["g_mix"] for sm in smalls], axis=0),
        g_mlp=jnp.concatenate([sm["g_mlp"] for sm in smalls], axis=0),
        g_pe=jnp.concatenate([sm["g_pe"] for sm in smalls], axis=0),
        g_final=dg_final[0],
        sinks=jnp.stack([sm["sinks"] for sm in smalls], axis=0),
        rel_bias=drel,
    )
    return loss_row, dx, gws, small


MESH_ID = pl.DeviceIdType.MESH


def _position():
    return lax.axis_index("x"), lax.axis_index("y"), lax.axis_index("c")


def _gather_comm(shards):
    n = len(shards)

    def copies(pos, x_refs, out_refs, sems):
        send_sems, recv_sems, local_sems = sems
        x, y, c = pos
        me, sibling = (x, y, c), (x, y, 1 - c)
        chips = [(1 - x, y), (x, 1 - y), (1 - x, 1 - y)]

        def slot(a, px, py, pc):
            return out_refs[a].at[4 * px + 2 * py + pc]

        def copy(a, k, block, to, src=None):
            return pltpu.make_async_remote_copy(
                src_ref=slot(a, *block) if src is None else src, dst_ref=slot(a, *block),
                send_sem=send_sems.at[a, k], recv_sem=recv_sems.at[a, k],
                device_id=to, device_id_type=MESH_ID)

        mine = [pltpu.make_async_copy(x_refs[a], slot(a, *me), local_sems.at[a]) for a in range(n)]
        first = []
        for a in range(n):
            first.append(copy(a, 0, me, sibling, src=x_refs[a]))
            first += [copy(a, 1 + j, me, (*chip, c), src=x_refs[a]) for j, chip in enumerate(chips)]
        return me, sibling, chips, copy, mine, first

    def start(pos, x_refs, out_refs, sems):
        _, _, _, _, mine, first = copies(pos, x_refs, out_refs, sems)
        for cp in mine + first:
            cp.start()

    def finish(pos, x_refs, out_refs, sems):
        me, sibling, chips, copy, mine, first = copies(pos, x_refs, out_refs, sems)
        c = pos[2]
        passed = []
        for j, chip in enumerate(chips):
            for a in range(n):
                copy(a, 1 + j, (*chip, c), me).wait_recv()
                fwd = copy(a, 4 + j, (*chip, c), sibling)
                fwd.start()
                passed.append(fwd)
        for a in range(n):
            copy(a, 0, sibling, me).wait_recv()
            for j, chip in enumerate(chips):
                copy(a, 4 + j, (*chip, 1 - c), me).wait_recv()
        for cp in first + passed:
            cp.wait_send()
        for cp in mine:
            cp.wait()

    return _Comm(shards, [jax.ShapeDtypeStruct((N_DEV,) + s.shape, s.dtype) for s in shards],
                 [pltpu.SemaphoreType.DMA((n, 7)), pltpu.SemaphoreType.DMA((n, 7)),
                  pltpu.SemaphoreType.DMA((n,))], start, finish)


def _exchange_comm(arrays, n_slots, route):
    n = len(arrays)

    def copies(pos, in_refs, out_refs, sems):
        send_sems, recv_sems = sems
        out = []
        for a in range(n):
            for j in range(n_slots):
                src_slot, peer = route(pos, j)
                out.append(pltpu.make_async_remote_copy(
                    src_ref=in_refs[a].at[src_slot], dst_ref=out_refs[a].at[j],
                    send_sem=send_sems.at[a, j], recv_sem=recv_sems.at[a, j],
                    device_id=peer, device_id_type=MESH_ID))
        return out

    def start(pos, in_refs, out_refs, sems):
        for cp in copies(pos, in_refs, out_refs, sems):
            cp.start()

    def finish(pos, in_refs, out_refs, sems):
        for cp in copies(pos, in_refs, out_refs, sems):
            cp.wait()

    return _Comm(arrays, [jax.ShapeDtypeStruct((n_slots,) + g.shape[1:], g.dtype) for g in arrays],
                 [pltpu.SemaphoreType.DMA((n, n_slots)), pltpu.SemaphoreType.DMA((n, n_slots))], start, finish)


def _rs_sibling_comm(gs):
    return _exchange_comm(gs, 4, lambda pos, j: (2 * j + (1 - pos[2]), (pos[0], pos[1], 1 - pos[2])))


def _chip_of(k, x, y):
    return x ^ ((k + 1) & 1), y ^ (((k + 1) >> 1) & 1)


def _chip_partials(pos, gs, recvs, name):
    n = len(gs)

    def body(pos_ref, *refs):
        for a in range(n):
            refs[2 * n + a][...] = (refs[a][...].astype(F32) + refs[n + a][...].astype(F32)
                                    ).astype(refs[2 * n + a].dtype)

    def g_map(k, pos_ref):
        cx, cy = _chip_of(k, pos_ref[0], pos_ref[1])
        return (4 * cx + 2 * cy + pos_ref[2], 0, 0)

    def r_map(k, pos_ref):
        cx, cy = _chip_of(k, pos_ref[0], pos_ref[1])
        return (2 * cx + cy, 0, 0)

    slab = [(None,) + g.shape[1:] for g in gs]
    return pl.pallas_call(
        body,
        grid_spec=pltpu.PrefetchScalarGridSpec(
            num_scalar_prefetch=1,
            grid=(4,),
            in_specs=[pl.BlockSpec(sh, g_map) for sh in slab] + [pl.BlockSpec(sh, r_map) for sh in slab],
            out_specs=[pl.BlockSpec(sh, lambda k, pos_ref: (k, 0, 0)) for sh in slab],
        ),
        out_shape=[jax.ShapeDtypeStruct((4,) + g.shape[1:], g.dtype) for g in gs],
        compiler_params=_cparams(),
        name=name,
    )(pos, *gs, *recvs)


def _rs_chips_comm(parts):
    return _exchange_comm(parts, 3, lambda pos, k: (k, (*_chip_of(k, pos[0], pos[1]), pos[2])))


def _adamw_math(w, g, m, v):
    m = ADAM_B1 * m + (1.0 - ADAM_B1) * g
    v = ADAM_B2 * v + (1.0 - ADAM_B2) * (g * g)
    m_hat = m / (1.0 - ADAM_B1 ** ADAM_STEP)
    v_hat = v / (1.0 - ADAM_B2 ** ADAM_STEP)
    delta = -ADAM_LR * (m_hat / (jnp.sqrt(v_hat) + ADAM_EPS) + ADAM_WD * w)
    return delta, m, v


def _adamw_weight(parts, recvs, w, m, v, name, grad_t=False, comm=None):
    depth, a, b = w.shape
    ta = _tile(a, 288, unit=LANES if grad_t else 16)
    ni = a // ta
    g_block = (b, ta) if grad_t else (ta, b)

    def body(*refs):
        p_refs, r_refs = refs[:depth], refs[depth:2 * depth]
        w_ref, m_ref, v_ref = refs[2 * depth:2 * depth + 3]
        g_out, d_out, m_out, v_out = refs[2 * depth + 3:]
        layer = pl.program_id(0)
        g = jnp.zeros(g_block, F32)
        for l in range(depth):
            gl = p_refs[l][...].astype(F32)
            for k in range(3):
                gl = gl + r_refs[l][k].astype(F32)
            g = jnp.where(layer == l, gl, g)
        if grad_t:
            g = g.T
        delta, m_new, v_new = _adamw_math(w_ref[...], g, m_ref[...], v_ref[...])
        g_out[...] = g
        d_out[...] = delta
        m_out[...] = m_new
        v_out[...] = v_new

    def hold(l):
        return lambda layer, i: jnp.where(layer == l, i, jnp.where(layer < l, 0, ni - 1))

    def g_index(slot, f):
        if grad_t:
            return lambda layer, i: (slot, 0, f(layer, i))
        return lambda layer, i: (slot, f(layer, i), 0)

    p_specs = [pl.BlockSpec((None,) + g_block, g_index(3, hold(l))) for l in range(depth)]
    r_specs = [pl.BlockSpec((3,) + g_block, g_index(0, hold(l))) for l in range(depth)]
    row = pl.BlockSpec((None, ta, b), lambda layer, i: (layer, i, 0))
    outs, couts = _call(
        body,
        grid=(depth, ni),
        in_specs=p_specs + r_specs + [row, row, row],
        out_specs=[row] * 4,
        out_shape=[jax.ShapeDtypeStruct(w.shape, F32)] * 4,
        args=(*parts, *recvs, w, m, v), name=name, comm=comm)
    return outs if comm is None else (outs, couts)


def _adamw_replicated(gathered, w, m, v, name):
    r, lanes = w.shape

    def body(g_ref, w_ref, m_ref, v_ref, g_out, d_out, m_out, v_out):
        g = g_ref[0]
        for k in range(1, N_DEV):
            g = g + g_ref[k]
        delta, m_new, v_new = _adamw_math(w_ref[...], g, m_ref[...], v_ref[...])
        g_out[...] = g
        d_out[...] = delta
        m_out[...] = m_new
        v_out[...] = v_new

    return pl.pallas_call(
        body,
        out_shape=[jax.ShapeDtypeStruct((r, lanes), F32)] * 4,
        name=name,
    )(gathered, w, m, v)


def _wire_shard(name, shard):
    return (shard.T if name in COL_SHARDED else shard).astype(WIRE)


def _full_weight(gathered):
    return gathered.reshape(N_DEV * gathered.shape[1], gathered.shape[2])


def _to_slabs(gfull):
    return gfull.reshape(N_DEV, gfull.shape[0] // N_DEV, gfull.shape[1])


def _pack_small(arrs):
    rows = []
    for a in arrs:
        flat = a.astype(F32).reshape(-1)
        pad = (-flat.shape[0]) % LANES
        rows.append(jnp.pad(flat, (0, pad)).reshape(-1, LANES))
    packed = jnp.concatenate(rows, axis=0)
    return jnp.pad(packed, ((0, (-packed.shape[0]) % 8), (0, 0)))


def _unpack_small(packed, shapes):
    out, off = [], 0
    for shp in shapes:
        n = math.prod(shp)
        rows = -(-n // LANES)
        out.append(packed[off:off + rows].reshape(-1)[:n].reshape(shp))
        off += rows
    return out


def _of(layer, *names):
    return tuple((layer, n) for n in names)


MLP_W = ("w_ff1", "w_ff2", "w_pe", "w_pg")

GATHERS = (
    ("rms_mix_l0", _of(0, "w_in")),
    ("proj_qkv_l0", _of(0, "w_up_a", "w_up_b", "w_o")),
    ("proj_gate_l0", _of(0, "w_pe", "w_pg")),
    ("sb_fwd_l0", _of(0, "w_ff1", "w_ff2")),
    ("swa_fwd_l0", _of(1, "w_in")),
    ("sb_fwd_l1", _of(1, "w_up_a", "w_up_b", "w_o", "w_pe", "w_pg", "w_ff1")),
    ("swa_fwd_l1", _of(1, "w_ff2")),
)
REDUCES = (
    (_of(1, "w_ff1"), "dw_ff2_l0", "dact_l0"),
    (_of(1, "w_ff2"), "dw_ff2_l0", "dw_ff1_l0"),
    (_of(1, "w_in"), "dw_ff2_l0", "sb_bwd_l0"),
    (_of(1, "w_up_a", "w_up_b", "w_o", "w_pe", "w_pg"), "dw_ff2_l0", "swa_bwd_l0"),
    (_of(0, *MLP_W), "dh_mlp_l0", "sb_bwd_l0"),
    (_of(0, "w_o", "w_up_a", "w_up_b"), "do_a_l0", "sb_bwd_l0"),
    (_of(0, "w_in"), "dh_ga_l0", "dh_qkv_l0"),
)


def _merge_comms(comms):
    if len(comms) == 1:
        return comms[0]

    def cuts(counts):
        edges = [0]
        for c in counts:
            edges.append(edges[-1] + c)
        return [slice(a, b) for a, b in zip(edges[:-1], edges[1:])]

    s_in = cuts([len(c.inputs) for c in comms])
    s_out = cuts([len(c.out_shapes) for c in comms])
    s_sem = cuts([len(c.sems) for c in comms])

    def start(pos, cin, cout, csem):
        for c, i, o, s in zip(comms, s_in, s_out, s_sem):
            c.start(pos, cin[i], cout[o], csem[s])

    def finish(pos, cin, cout, csem):
        for c, i, o, s in zip(comms, s_in, s_out, s_sem):
            c.finish(pos, cin[i], cout[o], csem[s])

    return _Comm(sum([c.inputs for c in comms], []), sum([c.out_shapes for c in comms], []),
                 sum([c.sems for c in comms], []), start, finish)


class _LayerWeights:
    def __init__(self, full, layer):
        self.full, self.layer, self.cache = full, layer, {}

    def __getitem__(self, name):
        if name not in self.cache:
            if name == "w_qkv_t":
                self.cache[name] = self.full[(self.layer, "w_in")][:QKV_COLS]
            elif name == "w_gate_t":
                self.cache[name] = self.full[(self.layer, "w_in")][QKV_COLS:]
            else:
                base = name[:-2] if name.endswith("_t") else name
                assert (base in COL_SHARDED) == name.endswith("_t"), name
                self.cache[name] = self.full[(self.layer, base)]
        return self.cache[name]


class _Plan:
    def __init__(self, w_sh, pos):
        self.w_sh = dict(zip(WEIGHTS, w_sh))
        self.pos = pos
        self.full, self.gw, self.parts, self.recv = {}, {}, {}, {}
        self.slabs = {}
        self.hosted = {}
        for i, (host, _) in enumerate(GATHERS):
            self.hosted.setdefault(host, []).append(("gather", i))
        for i, (_, sib_host, chip_host) in enumerate(REDUCES):
            self.hosted.setdefault(sib_host, []).append(("sibling", i))
            self.hosted.setdefault(chip_host, []).append(("chips", i))

    def _gather(self, i):
        return _gather_comm([_wire_shard(n, self.w_sh[n][layer]) for layer, n in GATHERS[i][1]])

    def _gathered(self, i, outs):
        for (layer, n), g in zip(GATHERS[i][1], outs):
            self.full[(layer, n)] = _full_weight(g)

    def weights(self, layer):
        return _LayerWeights(self.full, layer)

    def grad(self, layer, name, value):
        self.gw[(layer, name)] = value

    def _sibling(self, i):
        self.slabs[i] = [_to_slabs(self.gw[item]) for item in REDUCES[i][0]]
        return _rs_sibling_comm(self.slabs[i])

    def _sibling_done(self, i, outs):
        parts = _chip_partials(self.pos, self.slabs[i], outs, f"chip_partials_{i}")
        for item, part in zip(REDUCES[i][0], parts):
            self.parts[item] = part

    def _chips(self, i):
        return _rs_chips_comm([self.parts[item] for item in REDUCES[i][0]])

    def _chips_done(self, i, outs):
        for item, r in zip(REDUCES[i][0], outs):
            self.recv[item] = r

    def comm(self, name):
        if name not in self.hosted:
            return None
        make = {"gather": self._gather, "sibling": self._sibling, "chips": self._chips}
        return _merge_comms([make[kind](i) for kind, i in self.hosted[name]])

    def done(self, name, outs):
        took = {"gather": self._gathered, "sibling": self._sibling_done, "chips": self._chips_done}
        off = 0
        for kind, i in self.hosted[name]:
            n = len(GATHERS[i][1]) if kind == "gather" else len(REDUCES[i][0])
            took[kind](i, outs[off:off + n])
            off += n


def kernel(x, p, w_in, w_up_a, w_up_b, w_o, w_ff1, w_ff2, w_pe, w_pg, g_mix, g_mlp, g_pe, g_final, sinks, rel_bias, loss_target, m_w_in, m_w_up_a, m_w_up_b, m_w_o, m_w_ff1, m_w_ff2, m_w_pe, m_w_pg, m_g_mix, m_g_mlp, m_g_pe, m_g_final, m_sinks, m_rel_bias, v_w_in, v_w_up_a, v_w_up_b, v_w_o, v_w_ff1, v_w_ff2, v_w_pe, v_w_pg, v_g_mix, v_g_mlp, v_g_pe, v_g_final, v_sinks, v_rel_bias):
    w_sh = [w_in, w_up_a, w_up_b, w_o, w_ff1, w_ff2, w_pe, w_pg]
    m_sh = [m_w_in, m_w_up_a, m_w_up_b, m_w_o, m_w_ff1, m_w_ff2, m_w_pe, m_w_pg]
    v_sh = [v_w_in, v_w_up_a, v_w_up_b, v_w_o, v_w_ff1, v_w_ff2, v_w_pe, v_w_pg]
    depth = w_in.shape[0]
    assert depth == 2 and x.shape[-1] * 2 + QKV_COLS == w_in.shape[2] * N_DEV

    px, py, pc = _position()
    plan = _Plan(w_sh, jnp.stack([px, py, pc]).astype(jnp.int32))
    loss_row, grad_x, _, small = _local_step(
        x[0], p[:, 0], loss_target[0], plan.weights, g_mix, g_mlp, g_pe, g_final, sinks, rel_bias, plan=plan)

    small_g = _pack_small([small[n] for n in SMALL] + [loss_row[0, :1]])
    grad_w, delta_w, new_m, new_v = [], [], [], []
    for a, name in enumerate(WEIGHTS):
        parts = [plan.parts[(l, name)] for l in range(depth)]
        recvs = [plan.recv[(l, name)] for l in range(depth)]
        if name == "w_in":
            flip = lambda t: t.transpose(0, 2, 1)
            outs, (small_all,) = _adamw_weight(parts, recvs, flip(w_sh[a]), flip(m_sh[a]), flip(v_sh[a]),
                                               f"adamw_{name}", comm=_gather_comm([small_g]))
            outs = [flip(o) for o in outs]
        else:
            outs = _adamw_weight(parts, recvs, w_sh[a], m_sh[a], v_sh[a], f"adamw_{name}",
                                 grad_t=name in COL_SHARDED)
        for lst, o in zip((grad_w, delta_w, new_m, new_v), outs):
            lst.append(o)

    small_w = [g_mix, g_mlp, g_pe, g_final, sinks, rel_bias]
    small_m = [m_g_mix, m_g_mlp, m_g_pe, m_g_final, m_sinks, m_rel_bias]
    small_v = [v_g_mix, v_g_mlp, v_g_pe, v_g_final, v_sinks, v_rel_bias]
    small_shapes = [a.shape for a in small_w] + [(1,)]
    zero = jnp.zeros((1,), F32)
    packed_s = _adamw_replicated(small_all, _pack_small(small_w + [zero]), _pack_small(small_m + [zero]),
                                 _pack_small(small_v + [zero + 1.0]), "adamw_replicated")
    sg, sd, sm, sv = [_unpack_small(t, small_shapes) for t in packed_s]
    loss = sg[-1][0]

    return (loss, grad_x[None], *grad_w, *sg[:-1], *delta_w, *sd[:-1], *new_m, *sm[:-1], *new_v, *sv[:-1])
```

```python
import functools
import math

import numpy as np
import jax
import jax.numpy as jnp
from jax import lax
from jax.experimental import pallas as pl
from jax.experimental.pallas import tpu as pltpu

F32 = jnp.float32
MXU = jnp.bfloat16
WIRE = jnp.bfloat16

HEAD_DIM = 64
SB_HEADS = 8
SW_HEADS = 8
SW_KV = 2
SW_GROUP = SW_HEADS // SW_KV
BLOCK = 128
N_BUCKETS = 32
MAX_DISTANCE = 128
EPS = 1e-6
SCALE = HEAD_DIM ** -0.5
SB_W = SB_HEADS * HEAD_DIM
SW_W = SW_HEADS * HEAD_DIM
QKV_COLS = 3 * SB_W + SW_W + 2 * SW_KV * HEAD_DIM
N_DEV = 8
LANES = 128
N_PAIR = SB_HEADS // 2
NEG = -1e30

ADAM_LR = 0.001
ADAM_B1 = 0.9
ADAM_B2 = 0.999
ADAM_EPS = 1e-08
ADAM_WD = 0.01
ADAM_STEP = 10

VMEM_LIMIT = 48 * 1024 * 1024
SB_TQ = 256
SB_DEAD = -105.0
SB_SUB = 4

WEIGHTS = ("w_in", "w_up_a", "w_up_b", "w_o", "w_ff1", "w_ff2", "w_pe", "w_pg")
COL_SHARDED = ("w_in", "w_up_a", "w_up_b", "w_ff1", "w_pe")
SMALL = ("g_mix", "g_mlp", "g_pe", "g_final", "sinks", "rel_bias")


def _cparams(**kw):
    return pltpu.CompilerParams(vmem_limit_bytes=VMEM_LIMIT, **kw)


def _dot(a, b):
    return jnp.dot(a, b, preferred_element_type=F32)


def _dot_nt(a, b):
    return lax.dot_general(a, b, (((1,), (1,)), ((), ())), preferred_element_type=F32)


def _dot_tn(a, b):
    return lax.dot_general(a, b, (((0,), (0,)), ((), ())), preferred_element_type=F32)


def _tile(n, target, unit=LANES):
    if n <= target:
        return n
    t = (target // unit) * unit
    while t > unit and n % t:
        t -= unit
    assert n % t == 0, (n, target)
    return t


def _sigmoid(x):
    return 1.0 / (1.0 + jnp.exp(-x))


class _Comm:
    def __init__(self, inputs, out_shapes, sems, start, finish):
        self.inputs, self.out_shapes, self.sems = list(inputs), list(out_shapes), list(sems)
        self.start, self.finish = start, finish


def _call(body, *, grid, in_specs, out_specs, out_shape, scratch_shapes=(), args, name, comm=None):
    n_in, n_out, n_scr = len(in_specs), len(out_shape), len(scratch_shapes)
    if comm is None:
        outs = pl.pallas_call(body, grid=grid, in_specs=list(in_specs), out_specs=list(out_specs),
                              out_shape=list(out_shape), scratch_shapes=list(scratch_shapes),
                              compiler_params=_cparams(), name=name)(*args)
        return list(outs), None
    ci, co = len(comm.inputs), len(comm.out_shapes)
    any_spec = pl.BlockSpec(memory_space=pl.ANY)

    def wrapped(*refs):
        ins, cin = refs[:n_in], refs[n_in:n_in + ci]
        o0 = n_in + ci
        outs, cout = refs[o0:o0 + n_out], refs[o0 + n_out:o0 + n_out + co]
        s0 = o0 + n_out + co
        scr, csem = refs[s0:s0 + n_scr], refs[s0 + n_scr:]
        ids = [pl.program_id(d) for d in range(len(grid))]
        first = functools.reduce(jnp.logical_and, [i == 0 for i in ids])
        last = functools.reduce(jnp.logical_and, [i == g - 1 for i, g in zip(ids, grid)])
        pos = (lax.axis_index("x"), lax.axis_index("y"), lax.axis_index("c"))

        @pl.when(first)
        def _():
            comm.start(pos, cin, cout, csem)

        body(*ins, *outs, *scr)

        @pl.when(last)
        def _():
            comm.finish(pos, cin, cout, csem)

    outs = pl.pallas_call(wrapped, grid=grid, in_specs=list(in_specs) + [any_spec] * ci,
                          out_specs=list(out_specs) + [any_spec] * co,
                          out_shape=list(out_shape) + comm.out_shapes,
                          scratch_shapes=list(scratch_shapes) + comm.sems,
                          compiler_params=_cparams(), name=name)(*args, *comm.inputs)
    return list(outs[:n_out]), list(outs[n_out:])


def _accumulate(o_ref, value, first):
    @pl.when(first)
    def _():
        o_ref[...] = value

    @pl.when(jnp.logical_not(first))
    def _():
        o_ref[...] += value


def _mm(a, b, *, ta=False, tb=False, extras=(), epi=None, out_dtypes=(F32,),
        tm=1024, tn=1024, tk=1024, name, comm=None):
    if ta:
        kdim, m = a.shape
    else:
        m, kdim = a.shape
    n = b.shape[0] if tb else b.shape[1]
    assert (b.shape[1] if tb else b.shape[0]) == kdim
    tm, tn, tk = _tile(m, tm), _tile(n, tn), _tile(kdim, tk)
    nk = kdim // tk
    n_ex, n_out = len(extras), len(out_dtypes)

    a_spec = (pl.BlockSpec((tk, tm), lambda i, j, k: (k, i)) if ta
              else pl.BlockSpec((tm, tk), lambda i, j, k: (i, k)))
    b_spec = (pl.BlockSpec((tn, tk), lambda i, j, k: (j, k)) if tb
              else pl.BlockSpec((tk, tn), lambda i, j, k: (k, j)))
    ex_specs = []
    for e in extras:
        assert e.shape in ((m, n), (1, n), (m, 1)), (e.shape, m, n)
        if e.shape == (m, n):
            ex_specs.append(pl.BlockSpec((tm, tn), lambda i, j, k: (i, j)))
        elif e.shape[0] == 1:
            ex_specs.append(pl.BlockSpec((1, tn), lambda i, j, k: (0, j)))
        else:
            ex_specs.append(pl.BlockSpec((tm, 1), lambda i, j, k: (i, 0)))
    out_specs, out_shape, row_sums = [], [], []
    for dt in out_dtypes:
        kind = dt[1] if isinstance(dt, tuple) else "tile"
        row_sums.append(kind == "rowsum")
        if kind == "tile":
            out_specs.append(pl.BlockSpec((tm, tn), lambda i, j, k: (i, j)))
            out_shape.append(jax.ShapeDtypeStruct((m, n), dt))
            continue
        assert tn == n, "per-row and summed outputs need whole rows in one tile"
        if kind == "col":
            out_specs.append(pl.BlockSpec((tm, 1), lambda i, j, k: (i, 0)))
            out_shape.append(jax.ShapeDtypeStruct((m, 1), dt[0]))
        else:
            out_specs.append(pl.BlockSpec((1, tn), lambda i, j, k: (0, 0)))
            out_shape.append(jax.ShapeDtypeStruct((1, n), dt[0]))

    def body(a_ref, b_ref, *rest):
        ex_refs = rest[:n_ex]
        out_refs = rest[n_ex:n_ex + n_out]
        acc = rest[-1]
        k = pl.program_id(2)
        first_rows = pl.program_id(0) == 0

        def prod():
            av = a_ref[...].astype(MXU)
            bv = b_ref[...].astype(MXU)
            return _dot_tn(av, bv) if ta else (_dot_nt(av, bv) if tb else _dot(av, bv))

        def finish(res):
            if epi is not None:
                res = epi(res, *[e[...] for e in ex_refs])
            if not isinstance(res, tuple):
                res = (res,)
            for o_ref, r, summed in zip(out_refs, res, row_sums):
                if summed:
                    _accumulate(o_ref, r.astype(o_ref.dtype), first_rows)
                else:
                    o_ref[...] = r.astype(o_ref.dtype)

        if nk == 1:
            finish(prod())
            return

        @pl.when(k == 0)
        def _():
            acc[...] = prod()

        @pl.when(jnp.logical_and(k > 0, k < nk - 1))
        def _():
            acc[...] += prod()

        @pl.when(k == nk - 1)
        def _():
            finish(acc[...] + prod())

    outs, couts = _call(
        body,
        grid=(m // tm, n // tn, nk),
        in_specs=[a_spec, b_spec] + ex_specs,
        out_specs=out_specs,
        out_shape=out_shape,
        scratch_shapes=[pltpu.VMEM((tm, tn), F32)],
        args=(a, b, *extras), name=name, comm=comm)
    res = outs[0] if n_out == 1 else tuple(outs)
    return res if comm is None else (res, couts)


def _rms_fwd(x, g, name, comm=None):
    s, d = x.shape
    tr = _tile(s, 256)

    def body(x_ref, g_ref, h_ref, r_ref):
        xf = x_ref[...]
        r = lax.rsqrt(jnp.mean(xf * xf, axis=-1, keepdims=True) + EPS)
        h_ref[...] = ((xf * r) * g_ref[...]).astype(h_ref.dtype)
        r_ref[...] = r

    outs, couts = _call(
        body,
        grid=(s // tr,),
        in_specs=[pl.BlockSpec((tr, d), lambda i: (i, 0)), pl.BlockSpec((1, d), lambda i: (0, 0))],
        out_specs=[pl.BlockSpec((tr, d), lambda i: (i, 0)), pl.BlockSpec((tr, 1), lambda i: (i, 0))],
        out_shape=[jax.ShapeDtypeStruct((s, d), MXU), jax.ShapeDtypeStruct((s, 1), F32)],
        args=(x, g), name=name, comm=comm)
    return tuple(outs) if comm is None else (tuple(outs), couts)


def _loss_head(x, g, target, name):
    s, d = x.shape
    tr = _tile(s, 256)

    def body(x_ref, g_ref, t_ref, loss_ref, dx_ref, dg_ref):
        @pl.when(pl.program_id(0) == 0)
        def _():
            dg_ref[...] = jnp.zeros_like(dg_ref)
            loss_ref[...] = jnp.zeros_like(loss_ref)

        xf = x_ref[...]
        gv = g_ref[...]
        r = lax.rsqrt(jnp.mean(xf * xf, axis=-1, keepdims=True) + EPS)
        xhat = xf * r
        err = xhat * gv - t_ref[...]
        loss_ref[...] += 0.5 * jnp.sum(jnp.mean(err * err, axis=-1, keepdims=True), axis=0, keepdims=True)
        dy = err * (1.0 / d)
        dxhat = dy * gv
        mean = jnp.mean(dxhat * xhat, axis=-1, keepdims=True)
        dx_ref[...] = r * (dxhat - xhat * mean)
        dg_ref[...] += jnp.sum(dy * xhat, axis=0, keepdims=True)

    row = pl.BlockSpec((tr, d), lambda i: (i, 0))
    vec = pl.BlockSpec((1, d), lambda i: (0, 0))
    return pl.pallas_call(
        body,
        grid=(s // tr,),
        in_specs=[row, vec, row],
        out_specs=[pl.BlockSpec((1, LANES), lambda i: (0, 0)), row, vec],
        out_shape=[jax.ShapeDtypeStruct((1, LANES), F32), jax.ShapeDtypeStruct((s, d), F32),
                   jax.ShapeDtypeStruct((1, d), F32)],
        compiler_params=_cparams(),
        name=name,
    )(x, g, target)


def _mix_fwd(oa, ob, wa_t, wb_t, gates, name):
    s, kd = oa.shape
    d = wa_t.shape[0]
    tm, tn = _tile(s, 1024), _tile(d, 512)
    nj = d // tn

    def body(oa_ref, ob_ref, wa_ref, wb_ref, ga_ref, gb_ref, out_ref):
        ya = _dot_nt(oa_ref[...], wa_ref[...])
        yb = _dot_nt(ob_ref[...], wb_ref[...])
        out_ref[...] = (_sigmoid(ga_ref[...].astype(F32)) * ya
                        + _sigmoid(gb_ref[...].astype(F32)) * yb).astype(out_ref.dtype)

    o_spec = pl.BlockSpec((tm, kd), lambda i, j: (i, 0))
    w_spec = pl.BlockSpec((tn, kd), lambda i, j: (j, 0))
    return pl.pallas_call(
        body,
        grid=(s // tm, nj),
        in_specs=[o_spec, o_spec, w_spec, w_spec,
                  pl.BlockSpec((tm, tn), lambda i, j: (i, j)),
                  pl.BlockSpec((tm, tn), lambda i, j: (i, j + nj))],
        out_specs=pl.BlockSpec((tm, tn), lambda i, j: (i, j)),
        out_shape=jax.ShapeDtypeStruct((s, d), MXU),
        compiler_params=_cparams(),
        name=name,
    )(oa, ob, wa_t, wb_t, gates, gates)


def _mix_bwd(dx, w_o, oa, ob, wa_t, wb_t, gates, name):
    s, kd = oa.shape
    d = wa_t.shape[0]
    tm, tn = _tile(s, 1024), _tile(d, 512)
    nj = d // tn

    def body(dx_ref, wo_ref, oa_ref, ob_ref, wa_ref, wb_ref, ga_ref, gb_ref,
             dya_ref, dyb_ref, dga_ref, dgb_ref):
        dm = _dot_nt(dx_ref[...], wo_ref[...])
        ya = _dot_nt(oa_ref[...], wa_ref[...])
        yb = _dot_nt(ob_ref[...], wb_ref[...])
        sa = _sigmoid(ga_ref[...].astype(F32))
        sb = _sigmoid(gb_ref[...].astype(F32))
        dya_ref[...] = (dm * sa).astype(dya_ref.dtype)
        dyb_ref[...] = (dm * sb).astype(dyb_ref.dtype)
        dga_ref[...] = (dm * ya * sa * (1.0 - sa)).astype(dga_ref.dtype)
        dgb_ref[...] = (dm * yb * sb * (1.0 - sb)).astype(dgb_ref.dtype)

    o_spec = pl.BlockSpec((tm, kd), lambda i, j: (i, 0))
    w_spec = pl.BlockSpec((tn, kd), lambda i, j: (j, 0))
    t_spec = pl.BlockSpec((tm, tn), lambda i, j: (i, j))
    return pl.pallas_call(
        body,
        grid=(s // tm, nj),
        in_specs=[pl.BlockSpec((tm, d), lambda i, j: (i, 0)),
                  pl.BlockSpec((tn, d), lambda i, j: (j, 0)),
                  o_spec, o_spec, w_spec, w_spec, t_spec,
                  pl.BlockSpec((tm, tn), lambda i, j: (i, j + nj))],
        out_specs=[t_spec] * 4,
        out_shape=[jax.ShapeDtypeStruct((s, d), MXU)] * 4,
        compiler_params=_cparams(),
        name=name,
    )(dx, w_o, oa, ob, wa_t, wb_t, gates, gates)


def _ple(p, w_pe_t, h, w_pg, other, *, backward, name):
    s, kp = p.shape
    d = w_pe_t.shape[0]
    tm, tn = _tile(s, 1024), _tile(d, 512)

    def body(p_ref, wpe_ref, h_ref, wpg_ref, other_ref, *out_refs):
        pe = _dot_nt(p_ref[...].astype(MXU), wpe_ref[...])
        gt = _dot(h_ref[...], wpg_ref[...])
        sg = _sigmoid(gt)
        if backward:
            dout = other_ref[...]
            out_refs[0][...] = (dout * sg).astype(out_refs[0].dtype)
            out_refs[1][...] = (dout * pe * sg * (1.0 - sg)).astype(out_refs[1].dtype)
        else:
            out_refs[0][...] = other_ref[...] + pe * sg

    t_spec = pl.BlockSpec((tm, tn), lambda i, j: (i, j))
    if backward:
        out_specs, out_shape = [t_spec, t_spec], [jax.ShapeDtypeStruct((s, d), MXU)] * 2
    else:
        out_specs, out_shape = [t_spec], [jax.ShapeDtypeStruct((s, d), F32)]
    outs = pl.pallas_call(
        body,
        grid=(s // tm, d // tn),
        in_specs=[pl.BlockSpec((tm, kp), lambda i, j: (i, 0)),
                  pl.BlockSpec((tn, kp), lambda i, j: (j, 0)),
                  pl.BlockSpec((tm, d), lambda i, j: (i, 0)),
                  pl.BlockSpec((d, tn), lambda i, j: (0, j)),
                  t_spec],
        out_specs=out_specs,
        out_shape=out_shape,
        compiler_params=_cparams(),
        name=name,
    )(p, w_pe_t, h, w_pg, other)
    return tuple(outs) if backward else outs[0]


def _split_dot(x, tri):
    hi = x.astype(jnp.bfloat16)
    lo = (x - hi.astype(F32)).astype(jnp.bfloat16)
    return _dot(hi, tri) + _dot(lo, tri)


def _log_sigmoids(z):
    lb = jnp.minimum(z, 0.0) - jnp.log(1.0 + jnp.exp(-jnp.abs(z)))
    return lb, lb - z


def _head_lanes(hh):
    lane = lax.broadcasted_iota(jnp.int32, (1, LANES), 1)
    return jnp.logical_and(lane >= hh * HEAD_DIM, lane < (hh + 1) * HEAD_DIM)


def _sb_fwd(qkv, name, comm=None):
    s = qkv.shape[0]
    tq = _tile(s, SB_TQ)
    nsub = SB_SUB if (s // tq) % SB_SUB == 0 else 1

    def body(q_ref, k_ref, v_ref, o_ref):
        row = lax.broadcasted_iota(jnp.int32, (tq, tq), 0)
        col = lax.broadcasted_iota(jnp.int32, (tq, tq), 1)
        causal = col < row
        tri = jnp.where(row > col, 1.0, 0.0).astype(jnp.bfloat16)
        started = [_sb_fwd_straight(q_ref, k_ref, v_ref, pl.program_id(1) * nsub + sub, sub, tq, causal, tri)
                   for sub in range(nsub)]
        for sub, (block, i, cs, accs) in enumerate(started):
            def top(cs):
                return jnp.maximum(jnp.max(cs[0]), jnp.max(cs[1]))

            def live(st):
                return jnp.logical_and(st[0] >= 0, st[1] > SB_DEAD)

            def walk(st, block=block):
                cs, accs = block(st[0], st[2], st[3], False)
                return st[0] - 1, top(cs), cs, accs

            accs = lax.while_loop(live, walk, (i - 2, top(cs), cs, accs))[3]
            o_ref[sub * tq:(sub + 1) * tq, :] = jnp.where(_head_lanes(0), accs[0], accs[1]).astype(o_ref.dtype)

    outs, couts = _call(
        body,
        grid=(N_PAIR, s // (nsub * tq)),
        in_specs=[pl.BlockSpec((nsub * tq, LANES), lambda p, i: (i, p)),
                  pl.BlockSpec((s, LANES), lambda p, i: (0, N_PAIR + p)),
                  pl.BlockSpec((s, LANES), lambda p, i: (0, 2 * N_PAIR + p))],
        out_specs=[pl.BlockSpec((nsub * tq, LANES), lambda p, i: (i, p))],
        out_shape=[jax.ShapeDtypeStruct((s, SB_W), MXU)],
        args=(qkv, qkv, qkv), name=name, comm=comm)
    return outs[0] if comm is None else (outs[0], couts)


def _sb_fwd_straight(q_ref, k_ref, v_ref, i, sub, tq, causal, tri):
    qf = q_ref[sub * tq:(sub + 1) * tq, :].astype(F32) * SCALE
    qms = [jnp.where(_head_lanes(hh), qf, 0.0).astype(MXU) for hh in range(2)]

    def block(kb, cs, accs, masked, gate=None):
        rows = pl.ds(pl.multiple_of(kb * tq, tq), tq)
        ks, vs = k_ref[rows, :], v_ref[rows, :]
        new_c, new_acc = [], []
        for hh in range(2):
            lb, lm = _log_sigmoids(_dot_nt(qms[hh], ks))
            if masked:
                lm = jnp.where(causal, lm, 0.0)
            if gate is not None:
                lm = lm * gate
            a = jnp.exp(lb + _split_dot(lm, tri) + cs[hh])
            if masked:
                a = jnp.where(causal, a, 0.0)
            if gate is not None:
                a = a * gate
            new_acc.append(accs[hh] + _dot(a.astype(MXU), vs))
            new_c.append(cs[hh] + jnp.sum(lm, axis=1, keepdims=True))
        return tuple(new_c), tuple(new_acc)

    zc, za = jnp.zeros((tq, 1), F32), jnp.zeros((tq, LANES), F32)
    cs, accs = block(i, (zc, zc), (za, za), True)
    cs, accs = block(jnp.maximum(i - 1, 0), cs, accs, False, jnp.where(i > 0, 1.0, 0.0))
    return block, i, cs, accs


def _sb_bwd(qkv, do, name, comm=None):
    s = qkv.shape[0]
    tq = _tile(s, SB_TQ)
    nq = s // tq
    nsub = SB_SUB if nq % SB_SUB == 0 else 1
    nsteps = nq // nsub

    def body(q_ref, k_ref, v_ref, do_ref, dq_ref, dk_ref, dv_ref, dk_acc, dv_acc, carries):
        step = pl.program_id(1)

        @pl.when(step == 0)
        def _():
            dk_acc[...] = jnp.zeros_like(dk_acc)
            dv_acc[...] = jnp.zeros_like(dv_acc)

        row = lax.broadcasted_iota(jnp.int32, (tq, tq), 0)
        col = lax.broadcasted_iota(jnp.int32, (tq, tq), 1)
        causal = col < row
        tri_rev = jnp.where(row > col, 1.0, 0.0).astype(jnp.bfloat16)
        tri_excl = jnp.where(row < col, 1.0, 0.0).astype(jnp.bfloat16)
        zc, za = jnp.zeros((tq, 1), F32), jnp.zeros((tq, LANES), F32)

        def top(cs):
            return jnp.maximum(jnp.max(cs[0]), jnp.max(cs[1]))

        def live(st):
            return jnp.logical_and(st[0] >= 0, st[1] > SB_DEAD)

        def row_sums(pre):
            return [jnp.sum(lm, axis=1, keepdims=True) for _, lm in pre]

        def query_block(sub):
            i = step * nsub + sub
            q_rows = slice(sub * tq, (sub + 1) * tq)
            qf = q_ref[q_rows, :].astype(F32) * SCALE
            dof = do_ref[q_rows, :]
            qms = [jnp.where(_head_lanes(hh), qf, 0.0).astype(MXU) for hh in range(2)]
            doms = [jnp.where(_head_lanes(hh), dof, jnp.zeros_like(dof)) for hh in range(2)]

            def terms(kb, masked):
                rows = pl.ds(pl.multiple_of(kb * tq, tq), tq)
                ks = k_ref[rows, :]
                out = []
                for hh in range(2):
                    lb, lm = _log_sigmoids(_dot_nt(qms[hh], ks))
                    if masked:
                        lm = jnp.where(causal, lm, 0.0)
                    out.append((lb, lm))
                return out

            def block(kb, cs, gpres, dqs, masked, gate=None, pre=None):
                rows = pl.ds(pl.multiple_of(kb * tq, tq), tq)
                ks, vs = k_ref[rows, :], v_ref[rows, :]
                pre = terms(kb, masked) if pre is None else pre
                new_g, new_dq = [], []
                dk_add, dv_add = None, None
                for hh in range(2):
                    lb, lm = pre[hh]
                    a = jnp.exp(lb + _split_dot(lm, tri_rev) + cs[hh])
                    if masked:
                        a = jnp.where(causal, a, 0.0)
                    if gate is not None:
                        a = a * gate
                    g = a * _dot_nt(doms[hh], vs)
                    gsum = gpres[hh] + _split_dot(g, tri_excl)
                    dz = g - (g + gsum) * jnp.exp(lb)
                    if masked:
                        dz = jnp.where(causal, dz, 0.0)
                    if gate is not None:
                        dz = dz * gate
                    dzb = dz.astype(MXU)
                    new_dq.append(dqs[hh] + _dot(dzb, ks))
                    dk_h = _dot_tn(dzb, qms[hh])
                    dv_h = _dot_tn(a.astype(MXU), doms[hh])
                    dk_add = dk_h if dk_add is None else dk_add + dk_h
                    dv_add = dv_h if dv_add is None else dv_add + dv_h
                    new_g.append(gpres[hh] + jnp.sum(g, axis=1, keepdims=True))
                dk_acc[rows, :] += dk_add
                dv_acc[rows, :] += dv_add
                return tuple(new_g), tuple(new_dq)

            prev = jnp.maximum(i - 1, 0)
            gate = jnp.where(i > 0, 1.0, 0.0)
            t_diag, t_prev = terms(i, True), terms(prev, False)
            c_diag = row_sums(t_diag)
            sums = row_sums(t_prev)
            c_prev = tuple(c_diag[hh] + sums[hh] * gate for hh in range(2))
            return dict(i=i, prev=prev, gate=gate, q_rows=q_rows, terms=terms, block=block,
                        t_diag=t_diag, t_prev=t_prev, c_diag=c_diag, c_prev=c_prev)

        blocks = [query_block(sub) for sub in range(nsub)]
        for qb in blocks:
            def record(st, qb=qb):
                kb, cs = st[0], st[2]
                sums = row_sums(qb["terms"](kb, False))
                for hh in range(2):
                    carries[hh, kb] = cs[hh]
                cs = tuple(cs[hh] + sums[hh] for hh in range(2))
                return kb - 1, top(cs), cs

            first = lax.while_loop(live, record, (qb["i"] - 2, top(qb["c_prev"]), qb["c_prev"]))[0] + 1
            qb["mid"] = lax.fori_loop(
                first, qb["i"] - 1,
                lambda kb, cr, qb=qb: qb["block"](kb, (carries[0, kb], carries[1, kb]), cr[0], cr[1], False),
                ((zc, zc), (za, za)))
        for qb in blocks:
            gpres, dqs = qb["block"](qb["prev"], qb["c_diag"], *qb["mid"], False, qb["gate"], qb["t_prev"])
            dqs = qb["block"](qb["i"], (zc, zc), gpres, dqs, True, None, qb["t_diag"])[1]
            dq_ref[qb["q_rows"], :] = (jnp.where(_head_lanes(0), dqs[0], dqs[1]) * SCALE).astype(dq_ref.dtype)

        @pl.when(step == nsteps - 1)
        def _():
            dk_ref[...] = dk_acc[...].astype(dk_ref.dtype)
            dv_ref[...] = dv_acc[...].astype(dv_ref.dtype)

    blk = pl.BlockSpec((nsub * tq, LANES), lambda p, i: (i, p))
    full = pl.BlockSpec((s, LANES), lambda p, i: (0, p))
    outs, couts = _call(
        body,
        grid=(N_PAIR, nsteps),
        in_specs=[blk,
                  pl.BlockSpec((s, LANES), lambda p, i: (0, N_PAIR + p)),
                  pl.BlockSpec((s, LANES), lambda p, i: (0, 2 * N_PAIR + p)),
                  blk],
        out_specs=[blk, full, full],
        out_shape=[jax.ShapeDtypeStruct((s, SB_W), MXU)] * 3,
        scratch_shapes=[pltpu.VMEM((s, LANES), F32), pltpu.VMEM((s, LANES), F32),
                        pltpu.VMEM((2, nq, tq, 1), F32)],
        args=(qkv, qkv, qkv, do), name=name, comm=comm)
    return tuple(outs) if comm is None else (tuple(outs), couts)


def _bucket_table():
    i = np.arange(BLOCK)[:, None]
    j = np.arange(2 * BLOCK)[None, :]
    d = np.maximum(BLOCK + i - j, 0)
    max_exact = N_BUCKETS // 2
    df = np.maximum(d, 1).astype(np.float32)
    large = max_exact + (np.log(df / max_exact) / math.log(MAX_DISTANCE / max_exact)
                         * (N_BUCKETS - max_exact)).astype(np.int32)
    large = np.minimum(large, N_BUCKETS - 1)
    return np.where(d < max_exact, d, large).astype(np.int32)


def _build_bias(rel_bias, buckets, name):
    def body(rb_ref, bk_ref, out_ref):
        h = pl.program_id(0)
        bk = bk_ref[...]
        acc = jnp.zeros(bk.shape, F32)
        for b in range(N_BUCKETS):
            acc = jnp.where(bk == b, rb_ref[b, h], acc)
        out_ref[...] = acc

    return pl.pallas_call(
        body,
        grid=(SW_HEADS,),
        in_specs=[pl.BlockSpec(memory_space=pltpu.SMEM),
                  pl.BlockSpec((BLOCK, 2 * BLOCK), lambda h: (0, 0))],
        out_specs=pl.BlockSpec((None, BLOCK, 2 * BLOCK), lambda h: (h, 0, 0)),
        out_shape=jax.ShapeDtypeStruct((SW_HEADS, BLOCK, 2 * BLOCK), F32),
        name=name,
    )(rel_bias, buckets)


def _bias_grad(dbias_layers, buckets, name):
    n_l = len(dbias_layers)

    def body(*refs):
        bk = refs[n_l][...]
        out_ref = refs[n_l + 1]
        db = refs[0][...]
        for r in refs[1:n_l]:
            db = db + r[...]
        lane = lax.broadcasted_iota(jnp.int32, (1, LANES), 1)
        acc = jnp.zeros((1, LANES), F32)
        for b in range(N_BUCKETS):
            part = jnp.sum(jnp.where(bk == b, db, 0.0), axis=1, keepdims=True)
            tot = jnp.sum(part, axis=0, keepdims=True)
            acc = jnp.where(lane == b, tot, acc)
        out_ref[...] = acc

    hspec = pl.BlockSpec((None, BLOCK, 2 * BLOCK), lambda h: (h, 0, 0))
    return pl.pallas_call(
        body,
        grid=(SW_HEADS,),
        in_specs=[hspec] * n_l + [pl.BlockSpec((BLOCK, 2 * BLOCK), lambda h: (0, 0))],
        out_specs=pl.BlockSpec((None, 1, LANES), lambda h: (h, 0, 0)),
        out_shape=jax.ShapeDtypeStruct((SW_HEADS, 1, LANES), F32),
        name=name,
    )(*dbias_layers, buckets)


GROUP_ROWS = SW_GROUP * BLOCK


def _group_lanes(g):
    lane = lax.broadcasted_iota(jnp.int32, (1, LANES), 1)
    gvec = jnp.zeros((1, LANES), jnp.int32) + g
    return jnp.where(lane >= HEAD_DIM, 1, 0) == gvec, gvec


def _stack_heads(x, g):
    kv_lanes, gvec = _group_lanes(g)
    parts = []
    for j in range(SW_GROUP):
        half = x[:, (j // 2) * LANES:(j // 2 + 1) * LANES]
        moved = jnp.where(gvec == j % 2, half, pltpu.roll(half, HEAD_DIM, 1))
        parts.append(jnp.where(kv_lanes, moved, 0.0))
    return jnp.concatenate(parts, axis=0)


def _unstack_heads(y, g):
    _, gvec = _group_lanes(g)
    heads = []
    for j in range(SW_GROUP):
        yj = y[j * BLOCK:(j + 1) * BLOCK]
        heads.append(jnp.where(gvec == j % 2, yj, pltpu.roll(yj, HEAD_DIM, 1)))
    pairs = [jnp.where(_head_lanes(0), heads[2 * p], heads[2 * p + 1]) for p in range(SW_GROUP // 2)]
    return jnp.concatenate(pairs, axis=1)


def _per_head_col(values):
    return jnp.concatenate([jnp.zeros((BLOCK, 1), F32) + v for v in values], axis=0)


def _swa_scores(qs, kp, kc, bias_ref, n):
    row = jnp.bitwise_and(lax.broadcasted_iota(jnp.int32, (GROUP_ROWS, BLOCK), 0), BLOCK - 1)
    col = lax.broadcasted_iota(jnp.int32, (GROUP_ROWS, BLOCK), 1)
    bias = bias_ref[...].reshape(GROUP_ROWS, 2 * BLOCK)
    s1 = _dot_nt(qs, kp) + bias[:, :BLOCK]
    s2 = _dot_nt(qs, kc) + bias[:, BLOCK:]
    no_prev = jnp.where(n > 0, 0, BLOCK)
    s1 = jnp.where(col > row + no_prev, s1, NEG)
    s2 = jnp.where(col <= row, s2, NEG)
    return s1, s2


def _swa_specs(s):
    q_blk = 3 * SB_W // (2 * LANES)
    k_blk = (3 * SB_W + SW_W) // LANES
    return (pl.BlockSpec((s, 2 * LANES), lambda g: (0, q_blk + g)),
            pl.BlockSpec((s, LANES), lambda g: (0, k_blk)),
            pl.BlockSpec((s, LANES), lambda g: (0, k_blk + 1)))


def _swa_fwd(qkv, bias, sinks, name, comm=None):
    s = qkv.shape[0]
    nb = s // BLOCK

    def body(sink_ref, q_ref, k_ref, v_ref, bias_ref, o_ref, lse_ref):
        g = pl.program_id(0)
        sink = _per_head_col([sink_ref[SW_GROUP * g + j] for j in range(SW_GROUP)])
        lane = lax.broadcasted_iota(jnp.int32, (1, LANES), 1)

        def step(n, carry):
            r0 = pl.multiple_of(n * BLOCK, BLOCK)
            p0 = pl.multiple_of(jnp.maximum(n - 1, 0) * BLOCK, BLOCK)
            cur, prev = pl.ds(r0, BLOCK), pl.ds(p0, BLOCK)
            qs = _stack_heads(q_ref[cur, :].astype(F32) * SCALE, g).astype(MXU)
            s1, s2 = _swa_scores(qs, k_ref[prev, :], k_ref[cur, :], bias_ref, n)
            m = jnp.maximum(jnp.max(jnp.maximum(s1, s2), axis=1, keepdims=True), sink)
            e1 = jnp.exp(s1 - m)
            e2 = jnp.exp(s2 - m)
            den = jnp.sum(e1 + e2, axis=1, keepdims=True) + jnp.exp(sink - m)
            o = _dot((e1 / den).astype(MXU), v_ref[prev, :]) + _dot((e2 / den).astype(MXU), v_ref[cur, :])
            o_ref[cur, :] = _unstack_heads(o, g).astype(o_ref.dtype)
            lse = m + jnp.log(den)
            lse_row = jnp.zeros((BLOCK, LANES), F32)
            for j in range(SW_GROUP):
                lse_row = jnp.where(lane == j, lse[j * BLOCK:(j + 1) * BLOCK], lse_row)
            lse_ref[cur, :] = lse_row
            return carry

        lax.fori_loop(0, nb, step, 0, unroll=2)

    outs, couts = _call(
        body,
        grid=(SW_KV,),
        in_specs=[pl.BlockSpec(memory_space=pltpu.SMEM), *_swa_specs(s),
                  pl.BlockSpec((SW_GROUP, BLOCK, 2 * BLOCK), lambda g: (g, 0, 0))],
        out_specs=[pl.BlockSpec((s, 2 * LANES), lambda g: (0, g)),
                   pl.BlockSpec((None, s, LANES), lambda g: (g, 0, 0))],
        out_shape=[jax.ShapeDtypeStruct((s, SW_W), MXU), jax.ShapeDtypeStruct((SW_KV, s, LANES), F32)],
        args=(sinks, qkv, qkv, qkv, bias), name=name, comm=comm)
    return tuple(outs) if comm is None else (tuple(outs), couts)


def _swa_bwd(qkv, bias, sinks, do, lse, name, comm=None):
    s = qkv.shape[0]
    nb = s // BLOCK

    def body(sink_ref, q_ref, k_ref, v_ref, bias_ref, do_ref, lse_ref,
             dq_ref, dk_ref, dv_ref, dbias_ref, dsink_ref, dk_acc, dv_acc):
        g = pl.program_id(0)
        sink = _per_head_col([sink_ref[SW_GROUP * g + j] for j in range(SW_GROUP)])
        lane = lax.broadcasted_iota(jnp.int32, (1, LANES), 1)

        @pl.when(g == 0)
        def _():
            dk_acc[...] = jnp.zeros_like(dk_acc)
            dv_acc[...] = jnp.zeros_like(dv_acc)

        dbias_ref[...] = jnp.zeros_like(dbias_ref)

        def step(n, dsink_rows):
            r0 = pl.multiple_of(n * BLOCK, BLOCK)
            p0 = pl.multiple_of(jnp.maximum(n - 1, 0) * BLOCK, BLOCK)
            cur, prev = pl.ds(r0, BLOCK), pl.ds(p0, BLOCK)
            qs = _stack_heads(q_ref[cur, :].astype(F32) * SCALE, g).astype(MXU)
            dos = _stack_heads(do_ref[cur, :].astype(F32), g).astype(MXU)
            kp, kc, vp, vc = k_ref[prev, :], k_ref[cur, :], v_ref[prev, :], v_ref[cur, :]
            lse_row = lse_ref[cur, :]
            lse = jnp.concatenate([jnp.sum(jnp.where(lane == j, lse_row, 0.0), axis=1, keepdims=True)
                                   for j in range(SW_GROUP)], axis=0)
            s1, s2 = _swa_scores(qs, kp, kc, bias_ref, n)
            pr1 = jnp.exp(s1 - lse)
            pr2 = jnp.exp(s2 - lse)
            dpr1 = _dot_nt(dos, vp)
            dpr2 = _dot_nt(dos, vc)
            delta = jnp.sum(pr1 * dpr1 + pr2 * dpr2, axis=1, keepdims=True)
            ds1 = pr1 * (dpr1 - delta)
            ds2 = pr2 * (dpr2 - delta)
            dbias_ref[:, :, :BLOCK] += ds1.reshape(SW_GROUP, BLOCK, BLOCK)
            dbias_ref[:, :, BLOCK:] += ds2.reshape(SW_GROUP, BLOCK, BLOCK)
            ds1b, ds2b = ds1.astype(MXU), ds2.astype(MXU)
            dq = _dot(ds1b, kp) + _dot(ds2b, kc)
            dq_ref[cur, :] = (_unstack_heads(dq, g) * SCALE).astype(dq_ref.dtype)
            dk_acc[prev, :] += _dot_tn(ds1b, qs)
            dk_acc[cur, :] += _dot_tn(ds2b, qs)
            dv_acc[prev, :] += _dot_tn(pr1.astype(MXU), dos)
            dv_acc[cur, :] += _dot_tn(pr2.astype(MXU), dos)
            return dsink_rows - jnp.exp(sink - lse) * delta

        rows = lax.fori_loop(0, nb, step, jnp.zeros((GROUP_ROWS, 1), F32), unroll=2)
        for j in range(SW_GROUP):
            dsink_ref[j] = jnp.broadcast_to(jnp.sum(rows[j * BLOCK:(j + 1) * BLOCK], axis=0, keepdims=True),
                                            (1, LANES))

        @pl.when(g == SW_KV - 1)
        def _():
            dk_ref[...] = dk_acc[...].astype(dk_ref.dtype)
            dv_ref[...] = dv_acc[...].astype(dv_ref.dtype)

    grp = pl.BlockSpec((s, 2 * LANES), lambda g: (0, g))
    kv_out = pl.BlockSpec((s, LANES), lambda g: (0, 0))
    bspec = pl.BlockSpec((SW_GROUP, BLOCK, 2 * BLOCK), lambda g: (g, 0, 0))
    outs, couts = _call(
        body,
        grid=(SW_KV,),
        in_specs=[pl.BlockSpec(memory_space=pltpu.SMEM), *_swa_specs(s), bspec, grp,
                  pl.BlockSpec((None, s, LANES), lambda g: (g, 0, 0))],
        out_specs=[grp, kv_out, kv_out, bspec, pl.BlockSpec((SW_GROUP, 1, LANES), lambda g: (g, 0, 0))],
        out_shape=[jax.ShapeDtypeStruct((s, SW_W), MXU),
                   jax.ShapeDtypeStruct((s, LANES), MXU),
                   jax.ShapeDtypeStruct((s, LANES), MXU),
                   jax.ShapeDtypeStruct((SW_HEADS, BLOCK, 2 * BLOCK), F32),
                   jax.ShapeDtypeStruct((SW_HEADS, 1, LANES), F32)],
        scratch_shapes=[pltpu.VMEM((s, LANES), F32), pltpu.VMEM((s, LANES), F32)],
        args=(sinks, qkv, qkv, qkv, bias, do, lse), name=name, comm=comm)
    return tuple(outs) if comm is None else (tuple(outs), couts)


class _NoPlan:
    def comm(self, name):
        return None

    def done(self, name, outs):
        pass

    def grad(self, layer, name, value):
        pass


def _run(plan, fn, *args, name, **kw):
    comm = plan.comm(name)
    if comm is None:
        return fn(*args, name=name, **kw)
    res, outs = fn(*args, name=name, comm=comm, **kw)
    plan.done(name, outs)
    return res


def _norm_bwd(dh, x, dres, r, g):
    xhat = x * r
    dxhat = dh * g
    dx = dres + r * (dxhat - xhat * jnp.mean(dxhat * xhat, axis=-1, keepdims=True))
    return dx, dx, jnp.sum(dh * xhat, axis=0, keepdims=True)


def _residual_norm(acc, res, g):
    x = res + acc
    r = lax.rsqrt(jnp.mean(x * x, axis=-1, keepdims=True) + EPS)
    return x, (x * r) * g, r


def _layer_fwd(x, p, w, g_mix, g_mlp, g_pe, sinks, bias, tag, plan):
    h1, r1 = _run(plan, _rms_fwd, x, g_mix, name=f"rms_mix_{tag}")
    qkv = _run(plan, _mm, h1, w["w_qkv_t"], tb=True, out_dtypes=(MXU,), name=f"proj_qkv_{tag}")
    gates = _run(plan, _mm, h1, w["w_gate_t"], tb=True, out_dtypes=(MXU,), name=f"proj_gate_{tag}")
    oa = _run(plan, _sb_fwd, qkv, name=f"sb_fwd_{tag}")
    ob, lse = _run(plan, _swa_fwd, qkv, bias, sinks, name=f"swa_fwd_{tag}")
    merged = _mix_fwd(oa, ob, w["w_up_a_t"], w["w_up_b_t"], gates, f"mix_fwd_{tag}")
    d = x.shape[1]
    normed = (F32, MXU, (F32, "col"))
    x1, h2, r2 = _run(plan, _mm, merged, w["w_o"], extras=(x, g_mlp), epi=_residual_norm, out_dtypes=normed,
                      tn=d, name=f"out_proj_{tag}")
    u, act = _run(plan, _mm, h2, w["w_ff1_t"], tb=True,
                  epi=lambda acc: (acc, jnp.square(jnp.maximum(acc, 0.0))),
                  out_dtypes=(MXU, MXU), name=f"ff1_{tag}")
    x2, h3, r3 = _run(plan, _mm, act, w["w_ff2"], extras=(x1, g_pe), epi=_residual_norm, out_dtypes=normed,
                      tn=d, name=f"ff2_{tag}")
    x3 = _ple(p, w["w_pe_t"], h3, w["w_pg"], x2, backward=False, name=f"ple_fwd_{tag}")
    saved = dict(x=x, h1=h1, r1=r1, gates=gates, qkv=qkv, lse=lse, oa=oa, ob=ob, merged=merged,
                 x1=x1, h2=h2, r2=r2, u=u, act=act, x2=x2, h3=h3, r3=r3)
    return x3, saved


def _layer_bwd(dx3, sv, p, w, g_mix, g_mlp, g_pe, sinks, bias, layer, plan):
    tag = f"l{layer}"
    gw = {}
    wire = (WIRE,)

    def dw(name, a, b):
        gw[name] = _run(plan, _mm, a, b, ta=True, out_dtypes=wire, tk=2048, name=f"d{name}_{tag}")
        plan.grad(layer, name, gw[name])

    dpe, dgt = _ple(p, w["w_pe_t"], sv["h3"], w["w_pg"], dx3, backward=True, name=f"ple_bwd_{tag}")
    dw("w_pe", dpe, p)
    dw("w_pg", sv["h3"], dgt)
    d = dx3.shape[1]
    grads = (F32, MXU, (F32, "rowsum"))
    dx2, dx2b, dg_pe = _run(plan, _mm, dgt, w["w_pg"], tb=True, extras=(sv["x2"], dx3, sv["r3"], g_pe),
                            epi=_norm_bwd, out_dtypes=grads, tm=512, tn=d, name=f"dh_pe_{tag}")
    dw("w_ff2", sv["act"], dx2b)
    du = _run(plan, _mm, dx2b, w["w_ff2"], tb=True, extras=(sv["u"],),
              epi=lambda acc, u: acc * (2.0 * jnp.maximum(u.astype(F32), 0.0)), out_dtypes=(MXU,),
              name=f"dact_{tag}")
    dw("w_ff1", du, sv["h2"])
    dx1, dx1b, dg_mlp = _run(plan, _mm, du, w["w_ff1_t"], extras=(sv["x1"], dx2, sv["r2"], g_mlp),
                             epi=_norm_bwd, out_dtypes=grads, tm=1024, tn=d, name=f"dh_mlp_{tag}")
    dw("w_o", sv["merged"], dx1b)
    dya, dyb, dga, dgb = _mix_bwd(dx1b, w["w_o"], sv["oa"], sv["ob"], w["w_up_a_t"], w["w_up_b_t"],
                                  sv["gates"], f"mix_bwd_{tag}")
    dw("w_up_a", dya, sv["oa"])
    dw("w_up_b", dyb, sv["ob"])
    doa = _run(plan, _mm, dya, w["w_up_a_t"], out_dtypes=(MXU,), name=f"do_a_{tag}")
    dob = _run(plan, _mm, dyb, w["w_up_b_t"], out_dtypes=(MXU,), name=f"do_b_{tag}")
    dqb, dkb, dvb, dbias, dsink = _run(plan, _swa_bwd, sv["qkv"], bias, sinks, dob, sv["lse"],
                                       name=f"swa_bwd_{tag}")
    dqa, dka, dva = _run(plan, _sb_bwd, sv["qkv"], doa, name=f"sb_bwd_{tag}")
    dqkv = jnp.concatenate([dqa, dka, dva, dqb, dkb, dvb], axis=1)
    gw_qkv = _mm(dqkv, sv["h1"], ta=True, out_dtypes=wire, tk=2048, name=f"dw_qkv_{tag}")
    gw_ga = _mm(dga, sv["h1"], ta=True, out_dtypes=wire, tk=2048, name=f"dw_ga_{tag}")
    gw_gb = _mm(dgb, sv["h1"], ta=True, out_dtypes=wire, tk=2048, name=f"dw_gb_{tag}")
    gw["w_in"] = jnp.concatenate([gw_qkv, gw_ga, gw_gb], axis=0)
    plan.grad(layer, "w_in", gw["w_in"])
    d = dga.shape[1]
    add = lambda acc, res: res + acc
    dh1 = _run(plan, _mm, dga, w["w_gate_t"][:d], name=f"dh_ga_{tag}")
    dh1 = _run(plan, _mm, dgb, w["w_gate_t"][d:], extras=(dh1,), epi=add, name=f"dh_gb_{tag}")
    dx, _, dg_mix = _run(plan, _mm, dqkv, w["w_qkv_t"], tk=768, extras=(dh1, sv["x"], dx1, sv["r1"], g_mix),
                         epi=lambda acc, prev, *rest: _norm_bwd(acc + prev, *rest), out_dtypes=grads,
                         tm=512, tn=d, name=f"dh_qkv_{tag}")
    small = dict(g_mix=dg_mix, g_mlp=dg_mlp, g_pe=dg_pe, sinks=dsink[:, 0, 0], dbias=dbias)
    return dx, gw, small


def _local_step(x, p, target, weights, g_mix, g_mlp, g_pe, g_final, sinks, rel_bias, plan=None):
    plan = _NoPlan() if plan is None else plan
    depth = g_mix.shape[0]
    buckets = jnp.asarray(_bucket_table())
    bias = _build_bias(rel_bias, buckets, "build_bias")
    saved, wfull = [], []
    h = x
    for l in range(depth):
        wfull.append(weights(l))
        h, sv = _layer_fwd(h, p[l], wfull[l], g_mix[l:l + 1], g_mlp[l:l + 1], g_pe[l:l + 1],
                           sinks[l], bias, f"l{l}", plan)
        saved.append(sv)
    loss_row, dx, dg_final = _loss_head(h, g_final[None, :], target, "loss_head")
    gws = [None] * depth
    smalls = [None] * depth
    for l in reversed(range(depth)):
        dx, gws[l], smalls[l] = _layer_bwd(dx, saved[l], p[l], wfull[l], g_mix[l:l + 1], g_mlp[l:l + 1],
                                           g_pe[l:l + 1], sinks[l], bias, l, plan)
    drel = _bias_grad([sm["dbias"] for sm in smalls], buckets, "bias_grad")[:, 0, :N_BUCKETS].T
    small = dict(
        g_mix=jnp.concatenate([sm["g_mix"] for sm in smalls], axis=0),
        g_mlp=jnp.concatenate([sm["g_mlp"] for sm in smalls], axis=0),
        g_pe=jnp.concatenate([sm["g_pe"] for sm in smalls], axis=0),
        g_final=dg_final[0],
        sinks=jnp.stack([sm["sinks"] for sm in smalls], axis=0),
        rel_bias=drel,
    )
    return loss_row, dx, gws, small


MESH_ID = pl.DeviceIdType.MESH


def _position():
    return lax.axis_index("x"), lax.axis_index("y"), lax.axis_index("c")


def _gather_comm(shards):
    n = len(shards)

    def copies(pos, x_refs, out_refs, sems):
        send_sems, recv_sems, local_sems = sems
        x, y, c = pos
        me, sibling = (x, y, c), (x, y, 1 - c)
        chips = [(1 - x, y), (x, 1 - y), (1 - x, 1 - y)]

        def slot(a, px, py, pc):
            return out_refs[a].at[4 * px + 2 * py + pc]

        def copy(a, k, block, to, src=None):
            return pltpu.make_async_remote_copy(
                src_ref=slot(a, *block) if src is None else src, dst_ref=slot(a, *block),
                send_sem=send_sems.at[a, k], recv_sem=recv_sems.at[a, k],
                device_id=to, device_id_type=MESH_ID)

        mine = [pltpu.make_async_copy(x_refs[a], slot(a, *me), local_sems.at[a]) for a in range(n)]
        first = []
        for a in range(n):
            first.append(copy(a, 0, me, sibling, src=x_refs[a]))
            first += [copy(a, 1 + j, me, (*chip, c), src=x_refs[a]) for j, chip in enumerate(chips)]
        return me, sibling, chips, copy, mine, first

    def start(pos, x_refs, out_refs, sems):
        _, _, _, _, mine, first = copies(pos, x_refs, out_refs, sems)
        for cp in mine + first:
            cp.start()

    def finish(pos, x_refs, out_refs, sems):
        me, sibling, chips, copy, mine, first = copies(pos, x_refs, out_refs, sems)
        c = pos[2]
        passed = []
        for j, chip in enumerate(chips):
            for a in range(n):
                copy(a, 1 + j, (*chip, c), me).wait_recv()
                fwd = copy(a, 4 + j, (*chip, c), sibling)
                fwd.start()
                passed.append(fwd)
        for a in range(n):
            copy(a, 0, sibling, me).wait_recv()
            for j, chip in enumerate(chips):
                copy(a, 4 + j, (*chip, 1 - c), me).wait_recv()
        for cp in first + passed:
            cp.wait_send()
        for cp in mine:
            cp.wait()

    return _Comm(shards, [jax.ShapeDtypeStruct((N_DEV,) + s.shape, s.dtype) for s in shards],
                 [pltpu.SemaphoreType.DMA((n, 7)), pltpu.SemaphoreType.DMA((n, 7)),
                  pltpu.SemaphoreType.DMA((n,))], start, finish)


def _exchange_comm(arrays, n_slots, route):
    n = len(arrays)

    def copies(pos, in_refs, out_refs, sems):
        send_sems, recv_sems = sems
        out = []
        for a in range(n):
            for j in range(n_slots):
                src_slot, peer = route(pos, j)
                out.append(pltpu.make_async_remote_copy(
                    src_ref=in_refs[a].at[src_slot], dst_ref=out_refs[a].at[j],
                    send_sem=send_sems.at[a, j], recv_sem=recv_sems.at[a, j],
                    device_id=peer, device_id_type=MESH_ID))
        return out

    def start(pos, in_refs, out_refs, sems):
        for cp in copies(pos, in_refs, out_refs, sems):
            cp.start()

    def finish(pos, in_refs, out_refs, sems):
        for cp in copies(pos, in_refs, out_refs, sems):
            cp.wait()

    return _Comm(arrays, [jax.ShapeDtypeStruct((n_slots,) + g.shape[1:], g.dtype) for g in arrays],
                 [pltpu.SemaphoreType.DMA((n, n_slots)), pltpu.SemaphoreType.DMA((n, n_slots))], start, finish)


def _rs_sibling_comm(gs):
    return _exchange_comm(gs, 4, lambda pos, j: (2 * j + (1 - pos[2]), (pos[0], pos[1], 1 - pos[2])))


def _chip_of(k, x, y):
    return x ^ ((k + 1) & 1), y ^ (((k + 1) >> 1) & 1)


def _chip_partials(pos, gs, recvs, name):
    n = len(gs)

    def body(pos_ref, *refs):
        for a in range(n):
            refs[2 * n + a][...] = (refs[a][...].astype(F32) + refs[n + a][...].astype(F32)
                                    ).astype(refs[2 * n + a].dtype)

    def g_map(k, pos_ref):
        cx, cy = _chip_of(k, pos_ref[0], pos_ref[1])
        return (4 * cx + 2 * cy + pos_ref[2], 0, 0)

    def r_map(k, pos_ref):
        cx, cy = _chip_of(k, pos_ref[0], pos_ref[1])
        return (2 * cx + cy, 0, 0)

    slab = [(None,) + g.shape[1:] for g in gs]
    return pl.pallas_call(
        body,
        grid_spec=pltpu.PrefetchScalarGridSpec(
            num_scalar_prefetch=1,
            grid=(4,),
            in_specs=[pl.BlockSpec(sh, g_map) for sh in slab] + [pl.BlockSpec(sh, r_map) for sh in slab],
            out_specs=[pl.BlockSpec(sh, lambda k, pos_ref: (k, 0, 0)) for sh in slab],
        ),
        out_shape=[jax.ShapeDtypeStruct((4,) + g.shape[1:], g.dtype) for g in gs],
        compiler_params=_cparams(),
        name=name,
    )(pos, *gs, *recvs)


def _rs_chips_comm(parts):
    return _exchange_comm(parts, 3, lambda pos, k: (k, (*_chip_of(k, pos[0], pos[1]), pos[2])))


def _adamw_math(w, g, m, v):
    m = ADAM_B1 * m + (1.0 - ADAM_B1) * g
    v = ADAM_B2 * v + (1.0 - ADAM_B2) * (g * g)
    m_hat = m / (1.0 - ADAM_B1 ** ADAM_STEP)
    v_hat = v / (1.0 - ADAM_B2 ** ADAM_STEP)
    delta = -ADAM_LR * (m_hat / (jnp.sqrt(v_hat) + ADAM_EPS) + ADAM_WD * w)
    return delta, m, v


def _adamw_weight(parts, recvs, w, m, v, name, grad_t=False, comm=None):
    depth, a, b = w.shape
    ta = _tile(a, 288, unit=LANES if grad_t else 16)
    ni = a // ta
    g_block = (b, ta) if grad_t else (ta, b)

    def body(*refs):
        p_refs, r_refs = refs[:depth], refs[depth:2 * depth]
        w_ref, m_ref, v_ref = refs[2 * depth:2 * depth + 3]
        g_out, d_out, m_out, v_out = refs[2 * depth + 3:]
        layer = pl.program_id(0)
        g = jnp.zeros(g_block, F32)
        for l in range(depth):
            gl = p_refs[l][...].astype(F32)
            for k in range(3):
                gl = gl + r_refs[l][k].astype(F32)
            g = jnp.where(layer == l, gl, g)
        if grad_t:
            g = g.T
        delta, m_new, v_new = _adamw_math(w_ref[...], g, m_ref[...], v_ref[...])
        g_out[...] = g
        d_out[...] = delta
        m_out[...] = m_new
        v_out[...] = v_new

    def hold(l):
        return lambda layer, i: jnp.where(layer == l, i, jnp.where(layer < l, 0, ni - 1))

    def g_index(slot, f):
        if grad_t:
            return lambda layer, i: (slot, 0, f(layer, i))
        return lambda layer, i: (slot, f(layer, i), 0)

    p_specs = [pl.BlockSpec((None,) + g_block, g_index(3, hold(l))) for l in range(depth)]
    r_specs = [pl.BlockSpec((3,) + g_block, g_index(0, hold(l))) for l in range(depth)]
    row = pl.BlockSpec((None, ta, b), lambda layer, i: (layer, i, 0))
    outs, couts = _call(
        body,
        grid=(depth, ni),
        in_specs=p_specs + r_specs + [row, row, row],
        out_specs=[row] * 4,
        out_shape=[jax.ShapeDtypeStruct(w.shape, F32)] * 4,
        args=(*parts, *recvs, w, m, v), name=name, comm=comm)
    return outs if comm is None else (outs, couts)


def _adamw_replicated(gathered, w, m, v, name):
    r, lanes = w.shape

    def body(g_ref, w_ref, m_ref, v_ref, g_out, d_out, m_out, v_out):
        g = g_ref[0]
        for k in range(1, N_DEV):
            g = g + g_ref[k]
        delta, m_new, v_new = _adamw_math(w_ref[...], g, m_ref[...], v_ref[...])
        g_out[...] = g
        d_out[...] = delta
        m_out[...] = m_new
        v_out[...] = v_new

    return pl.pallas_call(
        body,
        out_shape=[jax.ShapeDtypeStruct((r, lanes), F32)] * 4,
        name=name,
    )(gathered, w, m, v)


def _wire_shard(name, shard):
    return (shard.T if name in COL_SHARDED else shard).astype(WIRE)


def _full_weight(gathered):
    return gathered.reshape(N_DEV * gathered.shape[1], gathered.shape[2])


def _to_slabs(gfull):
    return gfull.reshape(N_DEV, gfull.shape[0] // N_DEV, gfull.shape[1])


def _pack_small(arrs):
    rows = []
    for a in arrs:
        flat = a.astype(F32).reshape(-1)
        pad = (-flat.shape[0]) % LANES
        rows.append(jnp.pad(flat, (0, pad)).reshape(-1, LANES))
    packed = jnp.concatenate(rows, axis=0)
    return jnp.pad(packed, ((0, (-packed.shape[0]) % 8), (0, 0)))


def _unpack_small(packed, shapes):
    out, off = [], 0
    for shp in shapes:
        n = math.prod(shp)
        rows = -(-n // LANES)
        out.append(packed[off:off + rows].reshape(-1)[:n].reshape(shp))
        off += rows
    return out


def _of(layer, *names):
    return tuple((layer, n) for n in names)


MLP_W = ("w_ff1", "w_ff2", "w_pe", "w_pg")

GATHERS = (
    ("rms_mix_l0", _of(0, "w_in")),
    ("proj_qkv_l0", _of(0, "w_up_a", "w_up_b", "w_o")),
    ("proj_gate_l0", _of(0, "w_pe", "w_pg")),
    ("sb_fwd_l0", _of(0, "w_ff1", "w_ff2")),
    ("swa_fwd_l0", _of(1, "w_in")),
    ("sb_fwd_l1", _of(1, "w_up_a", "w_up_b", "w_o", "w_pe", "w_pg", "w_ff1")),
    ("swa_fwd_l1", _of(1, "w_ff2")),
)
REDUCES = (
    (_of(1, "w_ff1"), "dw_ff2_l0", "dact_l0"),
    (_of(1, "w_ff2"), "dw_ff2_l0", "dw_ff1_l0"),
    (_of(1, "w_in"), "dw_ff2_l0", "sb_bwd_l0"),
    (_of(1, "w_up_a", "w_up_b", "w_o", "w_pe", "w_pg"), "dw_ff2_l0", "swa_bwd_l0"),
    (_of(0, *MLP_W), "dh_mlp_l0", "sb_bwd_l0"),
    (_of(0, "w_o", "w_up_a", "w_up_b"), "do_a_l0", "sb_bwd_l0"),
    (_of(0, "w_in"), "dh_ga_l0", "dh_qkv_l0"),
)


def _merge_comms(comms):
    if len(comms) == 1:
        return comms[0]

    def cuts(counts):
        edges = [0]
        for c in counts:
            edges.append(edges[-1] + c)
        return [slice(a, b) for a, b in zip(edges[:-1], edges[1:])]

    s_in = cuts([len(c.inputs) for c in comms])
    s_out = cuts([len(c.out_shapes) for c in comms])
    s_sem = cuts([len(c.sems) for c in comms])

    def start(pos, cin, cout, csem):
        for c, i, o, s in zip(comms, s_in, s_out, s_sem):
            c.start(pos, cin[i], cout[o], csem[s])

    def finish(pos, cin, cout, csem):
        for c, i, o, s in zip(comms, s_in, s_out, s_sem):
            c.finish(pos, cin[i], cout[o], csem[s])

    return _Comm(sum([c.inputs for c in comms], []), sum([c.out_shapes for c in comms], []),
                 sum([c.sems for c in comms], []), start, finish)


class _LayerWeights:
    def __init__(self, full, layer):
        self.full, self.layer, self.cache = full, layer, {}

    def __getitem__(self, name):
        if name not in self.cache:
            if name == "w_qkv_t":
                self.cache[name] = self.full[(self.layer, "w_in")][:QKV_COLS]
            elif name == "w_gate_t":
                self.cache[name] = self.full[(self.layer, "w_in")][QKV_COLS:]
            else:
                base = name[:-2] if name.endswith("_t") else name
                assert (base in COL_SHARDED) == name.endswith("_t"), name
                self.cache[name] = self.full[(self.layer, base)]
        return self.cache[name]


class _Plan:
    def __init__(self, w_sh, pos):
        self.w_sh = dict(zip(WEIGHTS, w_sh))
        self.pos = pos
        self.full, self.gw, self.parts, self.recv = {}, {}, {}, {}
        self.slabs = {}
        self.hosted = {}
        for i, (host, _) in enumerate(GATHERS):
            self.hosted.setdefault(host, []).append(("gather", i))
        for i, (_, sib_host, chip_host) in enumerate(REDUCES):
            self.hosted.setdefault(sib_host, []).append(("sibling", i))
            self.hosted.setdefault(chip_host, []).append(("chips", i))

    def _gather(self, i):
        return _gather_comm([_wire_shard(n, self.w_sh[n][layer]) for layer, n in GATHERS[i][1]])

    def _gathered(self, i, outs):
        for (layer, n), g in zip(GATHERS[i][1], outs):
            self.full[(layer, n)] = _full_weight(g)

    def weights(self, layer):
        return _LayerWeights(self.full, layer)

    def grad(self, layer, name, value):
        self.gw[(layer, name)] = value

    def _sibling(self, i):
        self.slabs[i] = [_to_slabs(self.gw[item]) for item in REDUCES[i][0]]
        return _rs_sibling_comm(self.slabs[i])

    def _sibling_done(self, i, outs):
        parts = _chip_partials(self.pos, self.slabs[i], outs, f"chip_partials_{i}")
        for item, part in zip(REDUCES[i][0], parts):
            self.parts[item] = part

    def _chips(self, i):
        return _rs_chips_comm([self.parts[item] for item in REDUCES[i][0]])

    def _chips_done(self, i, outs):
        for item, r in zip(REDUCES[i][0], outs):
            self.recv[item] = r

    def comm(self, name):
        if name not in self.hosted:
            return None
        make = {"gather": self._gather, "sibling": self._sibling, "chips": self._chips}
        return _merge_comms([make[kind](i) for kind, i in self.hosted[name]])

    def done(self, name, outs):
        took = {"gather": self._gathered, "sibling": self._sibling_done, "chips": self._chips_done}
        off = 0
        for kind, i in self.hosted[name]:
            n = len(GATHERS[i][1]) if kind == "gather" else len(REDUCES[i][0])
            took[kind](i, outs[off:off + n])
            off += n


def kernel(x, p, w_in, w_up_a, w_up_b, w_o, w_ff1, w_ff2, w_pe, w_pg, g_mix, g_mlp, g_pe, g_final, sinks, rel_bias, loss_target, m_w_in, m_w_up_a, m_w_up_b, m_w_o, m_w_ff1, m_w_ff2, m_w_pe, m_w_pg, m_g_mix, m_g_mlp, m_g_pe, m_g_final, m_sinks, m_rel_bias, v_w_in, v_w_up_a, v_w_up_b, v_w_o, v_w_ff1, v_w_ff2, v_w_pe, v_w_pg, v_g_mix, v_g_mlp, v_g_pe, v_g_final, v_sinks, v_rel_bias):
    w_sh = [w_in, w_up_a, w_up_b, w_o, w_ff1, w_ff2, w_pe, w_pg]
    m_sh = [m_w_in, m_w_up_a, m_w_up_b, m_w_o, m_w_ff1, m_w_ff2, m_w_pe, m_w_pg]
    v_sh = [v_w_in, v_w_up_a, v_w_up_b, v_w_o, v_w_ff1, v_w_ff2, v_w_pe, v_w_pg]
    depth = w_in.shape[0]
    assert depth == 2 and x.shape[-1] * 2 + QKV_COLS == w_in.shape[2] * N_DEV

    px, py, pc = _position()
    plan = _Plan(w_sh, jnp.stack([px, py, pc]).astype(jnp.int32))
    loss_row, grad_x, _, small = _local_step(
        x[0], p[:, 0], loss_target[0], plan.weights, g_mix, g_mlp, g_pe, g_final, sinks, rel_bias, plan=plan)

    small_g = _pack_small([small[n] for n in SMALL] + [loss_row[0, :1]])
    grad_w, delta_w, new_m, new_v = [], [], [], []
    for a, name in enumerate(WEIGHTS):
        parts = [plan.parts[(l, name)] for l in range(depth)]
        recvs = [plan.recv[(l, name)] for l in range(depth)]
        if name == "w_in":
            flip = lambda t: t.transpose(0, 2, 1)
            outs, (small_all,) = _adamw_weight(parts, recvs, flip(w_sh[a]), flip(m_sh[a]), flip(v_sh[a]),
                                               f"adamw_{name}", comm=_gather_comm([small_g]))
            outs = [flip(o) for o in outs]
        else:
            outs = _adamw_weight(parts, recvs, w_sh[a], m_sh[a], v_sh[a], f"adamw_{name}",
                                 grad_t=name in COL_SHARDED)
        for lst, o in zip((grad_w, delta_w, new_m, new_v), outs):
            lst.append(o)

    small_w = [g_mix, g_mlp, g_pe, g_final, sinks, rel_bias]
    small_m = [m_g_mix, m_g_mlp, m_g_pe, m_g_final, m_sinks, m_rel_bias]
    small_v = [v_g_mix, v_g_mlp, v_g_pe, v_g_final, v_sinks, v_rel_bias]
    small_shapes = [a.shape for a in small_w] + [(1,)]
    zero = jnp.zeros((1,), F32)
    packed_s = _adamw_replicated(small_all, _pack_small(small_w + [zero]), _pack_small(small_m + [zero]),
                                 _pack_small(small_v + [zero + 1.0]), "adamw_replicated")
    sg, sd, sm, sv = [_unpack_small(t, small_shapes) for t in packed_s]
    loss = sg[-1][0]

    return (loss, grad_x[None], *grad_w, *sg[:-1], *delta_w, *sd[:-1], *new_m, *sm[:-1], *new_v, *sv[:-1])
```

```python
import functools
import math

import numpy as np
import jax
import jax.numpy as jnp
from jax import lax
from jax.experimental import pallas as pl
from jax.experimental.pallas import tpu as pltpu

F32 = jnp.float32
MXU = jnp.bfloat16
WIRE = jnp.bfloat16

HEAD_DIM = 64
SB_HEADS = 8
SW_HEADS = 8
SW_KV = 2
SW_GROUP = SW_HEADS // SW_KV
BLOCK = 128
N_BUCKETS = 32
MAX_DISTANCE = 128
EPS = 1e-6
SCALE = HEAD_DIM ** -0.5
SB_W = SB_HEADS * HEAD_DIM
SW_W = SW_HEADS * HEAD_DIM
QKV_COLS = 3 * SB_W + SW_W + 2 * SW_KV * HEAD_DIM
N_DEV = 8
LANES = 128
N_PAIR = SB_HEADS // 2
NEG = -1e30

ADAM_LR = 0.001
ADAM_B1 = 0.9
ADAM_B2 = 0.999
ADAM_EPS = 1e-08
ADAM_WD = 0.01
ADAM_STEP = 10

VMEM_LIMIT = 48 * 1024 * 1024
SB_TQ = 256
SB_DEAD = -105.0
SB_SUB = 4

WEIGHTS = ("w_in", "w_up_a", "w_up_b", "w_o", "w_ff1", "w_ff2", "w_pe", "w_pg")
COL_SHARDED = ("w_in", "w_up_a", "w_up_b", "w_ff1", "w_pe")
SMALL = ("g_mix", "g_mlp", "g_pe", "g_final", "sinks", "rel_bias")


def _cparams(**kw):
    return pltpu.CompilerParams(vmem_limit_bytes=VMEM_LIMIT, **kw)


def _dot(a, b):
    return jnp.dot(a, b, preferred_element_type=F32)


def _dot_nt(a, b):
    return lax.dot_general(a, b, (((1,), (1,)), ((), ())), preferred_element_type=F32)


def _dot_tn(a, b):
    return lax.dot_general(a, b, (((0,), (0,)), ((), ())), preferred_element_type=F32)


def _tile(n, target, unit=LANES):
    if n <= target:
        return n
    t = (target // unit) * unit
    while t > unit and n % t:
        t -= unit
    assert n % t == 0, (n, target)
    return t


def _sigmoid(x):
    return 0.5 * jnp.tanh(0.5 * x) + 0.5


class _Comm:
    def __init__(self, inputs, out_shapes, sems, start, finish):
        self.inputs, self.out_shapes, self.sems = list(inputs), list(out_shapes), list(sems)
        self.start, self.finish = start, finish


def _call(body, *, grid, in_specs, out_specs, out_shape, scratch_shapes=(), args, name, comm=None):
    n_in, n_out, n_scr = len(in_specs), len(out_shape), len(scratch_shapes)
    if comm is None:
        outs = pl.pallas_call(body, grid=grid, in_specs=list(in_specs), out_specs=list(out_specs),
                              out_shape=list(out_shape), scratch_shapes=list(scratch_shapes),
                              compiler_params=_cparams(), name=name)(*args)
        return list(outs), None
    ci, co = len(comm.inputs), len(comm.out_shapes)
    any_spec = pl.BlockSpec(memory_space=pl.ANY)

    def wrapped(*refs):
        ins, cin = refs[:n_in], refs[n_in:n_in + ci]
        o0 = n_in + ci
        outs, cout = refs[o0:o0 + n_out], refs[o0 + n_out:o0 + n_out + co]
        s0 = o0 + n_out + co
        scr, csem = refs[s0:s0 + n_scr], refs[s0 + n_scr:]
        ids = [pl.program_id(d) for d in range(len(grid))]
        first = functools.reduce(jnp.logical_and, [i == 0 for i in ids])
        last = functools.reduce(jnp.logical_and, [i == g - 1 for i, g in zip(ids, grid)])
        pos = (lax.axis_index("x"), lax.axis_index("y"), lax.axis_index("c"))

        @pl.when(first)
        def _():
            comm.start(pos, cin, cout, csem)

        body(*ins, *outs, *scr)

        @pl.when(last)
        def _():
            comm.finish(pos, cin, cout, csem)

    outs = pl.pallas_call(wrapped, grid=grid, in_specs=list(in_specs) + [any_spec] * ci,
                          out_specs=list(out_specs) + [any_spec] * co,
                          out_shape=list(out_shape) + comm.out_shapes,
                          scratch_shapes=list(scratch_shapes) + comm.sems,
                          compiler_params=_cparams(), name=name)(*args, *comm.inputs)
    return list(outs[:n_out]), list(outs[n_out:])


def _accumulate(o_ref, value, first):
    @pl.when(first)
    def _():
        o_ref[...] = value

    @pl.when(jnp.logical_not(first))
    def _():
        o_ref[...] += value


def _mm(a, b, *, ta=False, tb=False, extras=(), epi=None, out_dtypes=(F32,),
        tm=1024, tn=1024, tk=1024, name, comm=None):
    if ta:
        kdim, m = a.shape
    else:
        m, kdim = a.shape
    n = b.shape[0] if tb else b.shape[1]
    assert (b.shape[1] if tb else b.shape[0]) == kdim
    tm, tn, tk = _tile(m, tm), _tile(n, tn), _tile(kdim, tk)
    nk = kdim // tk
    n_ex, n_out = len(extras), len(out_dtypes)

    a_spec = (pl.BlockSpec((tk, tm), lambda i, j, k: (k, i)) if ta
              else pl.BlockSpec((tm, tk), lambda i, j, k: (i, k)))
    b_spec = (pl.BlockSpec((tn, tk), lambda i, j, k: (j, k)) if tb
              else pl.BlockSpec((tk, tn), lambda i, j, k: (k, j)))
    ex_specs = []
    for e in extras:
        assert e.shape in ((m, n), (1, n), (m, 1)), (e.shape, m, n)
        if e.shape == (m, n):
            ex_specs.append(pl.BlockSpec((tm, tn), lambda i, j, k: (i, j)))
        elif e.shape[0] == 1:
            ex_specs.append(pl.BlockSpec((1, tn), lambda i, j, k: (0, j)))
        else:
            ex_specs.append(pl.BlockSpec((tm, 1), lambda i, j, k: (i, 0)))
    out_specs, out_shape, row_sums = [], [], []
    for dt in out_dtypes:
        kind = dt[1] if isinstance(dt, tuple) else "tile"
        row_sums.append(kind == "rowsum")
        if kind == "tile":
            out_specs.append(pl.BlockSpec((tm, tn), lambda i, j, k: (i, j)))
            out_shape.append(jax.ShapeDtypeStruct((m, n), dt))
            continue
        assert tn == n, "per-row and summed outputs need whole rows in one tile"
        if kind == "col":
            out_specs.append(pl.BlockSpec((tm, 1), lambda i, j, k: (i, 0)))
            out_shape.append(jax.ShapeDtypeStruct((m, 1), dt[0]))
        else:
            out_specs.append(pl.BlockSpec((1, tn), lambda i, j, k: (0, 0)))
            out_shape.append(jax.ShapeDtypeStruct((1, n), dt[0]))

    def body(a_ref, b_ref, *rest):
        ex_refs = rest[:n_ex]
        out_refs = rest[n_ex:n_ex + n_out]
        acc = rest[-1]
        k = pl.program_id(2)
        first_rows = pl.program_id(0) == 0

        def prod():
            av = a_ref[...].astype(MXU)
            bv = b_ref[...].astype(MXU)
            return _dot_tn(av, bv) if ta else (_dot_nt(av, bv) if tb else _dot(av, bv))

        def finish(res):
            if epi is not None:
                res = epi(res, *[e[...] for e in ex_refs])
            if not isinstance(res, tuple):
                res = (res,)
            for o_ref, r, summed in zip(out_refs, res, row_sums):
                if summed:
                    _accumulate(o_ref, r.astype(o_ref.dtype), first_rows)
                else:
                    o_ref[...] = r.astype(o_ref.dtype)

        if nk == 1:
            finish(prod())
            return

        @pl.when(k == 0)
        def _():
            acc[...] = prod()

        @pl.when(jnp.logical_and(k > 0, k < nk - 1))
        def _():
            acc[...] += prod()

        @pl.when(k == nk - 1)
        def _():
            finish(acc[...] + prod())

    outs, couts = _call(
        body,
        grid=(m // tm, n // tn, nk),
        in_specs=[a_spec, b_spec] + ex_specs,
        out_specs=out_specs,
        out_shape=out_shape,
        scratch_shapes=[pltpu.VMEM((tm, tn), F32)],
        args=(a, b, *extras), name=name, comm=comm)
    res = outs[0] if n_out == 1 else tuple(outs)
    return res if comm is None else (res, couts)


def _rms_fwd(x, g, name, comm=None):
    s, d = x.shape
    tr = _tile(s, 256)

    def body(x_ref, g_ref, h_ref, r_ref):
        xf = x_ref[...]
        r = lax.rsqrt(jnp.mean(xf * xf, axis=-1, keepdims=True) + EPS)
        h_ref[...] = ((xf * r) * g_ref[...]).astype(h_ref.dtype)
        r_ref[...] = r

    outs, couts = _call(
        body,
        grid=(s // tr,),
        in_specs=[pl.BlockSpec((tr, d), lambda i: (i, 0)), pl.BlockSpec((1, d), lambda i: (0, 0))],
        out_specs=[pl.BlockSpec((tr, d), lambda i: (i, 0)), pl.BlockSpec((tr, 1), lambda i: (i, 0))],
        out_shape=[jax.ShapeDtypeStruct((s, d), MXU), jax.ShapeDtypeStruct((s, 1), F32)],
        args=(x, g), name=name, comm=comm)
    return tuple(outs) if comm is None else (tuple(outs), couts)


def _loss_head(x, g, target, name):
    s, d = x.shape
    tr = _tile(s, 256)

    def body(x_ref, g_ref, t_ref, loss_ref, dx_ref, dg_ref):
        @pl.when(pl.program_id(0) == 0)
        def _():
            dg_ref[...] = jnp.zeros_like(dg_ref)
            loss_ref[...] = jnp.zeros_like(loss_ref)

        xf = x_ref[...]
        gv = g_ref[...]
        r = lax.rsqrt(jnp.mean(xf * xf, axis=-1, keepdims=True) + EPS)
        xhat = xf * r
        err = xhat * gv - t_ref[...]
        loss_ref[...] += 0.5 * jnp.sum(jnp.mean(err * err, axis=-1, keepdims=True), axis=0, keepdims=True)
        dy = err * (1.0 / d)
        dxhat = dy * gv
        mean = jnp.mean(dxhat * xhat, axis=-1, keepdims=True)
        dx_ref[...] = r * (dxhat - xhat * mean)
        dg_ref[...] += jnp.sum(dy * xhat, axis=0, keepdims=True)

    row = pl.BlockSpec((tr, d), lambda i: (i, 0))
    vec = pl.BlockSpec((1, d), lambda i: (0, 0))
    return pl.pallas_call(
        body,
        grid=(s // tr,),
        in_specs=[row, vec, row],
        out_specs=[pl.BlockSpec((1, LANES), lambda i: (0, 0)), row, vec],
        out_shape=[jax.ShapeDtypeStruct((1, LANES), F32), jax.ShapeDtypeStruct((s, d), F32),
                   jax.ShapeDtypeStruct((1, d), F32)],
        compiler_params=_cparams(),
        name=name,
    )(x, g, target)


def _mix_fwd(oa, ob, wa_t, wb_t, gates, name):
    s, kd = oa.shape
    d = wa_t.shape[0]
    tm, tn = _tile(s, 1024), _tile(d, 512)
    nj = d // tn

    def body(oa_ref, ob_ref, wa_ref, wb_ref, ga_ref, gb_ref, out_ref):
        ya = _dot_nt(oa_ref[...], wa_ref[...])
        yb = _dot_nt(ob_ref[...], wb_ref[...])
        out_ref[...] = (_sigmoid(ga_ref[...].astype(F32)) * ya
                        + _sigmoid(gb_ref[...].astype(F32)) * yb).astype(out_ref.dtype)

    o_spec = pl.BlockSpec((tm, kd), lambda i, j: (i, 0))
    w_spec = pl.BlockSpec((tn, kd), lambda i, j: (j, 0))
    return pl.pallas_call(
        body,
        grid=(s // tm, nj),
        in_specs=[o_spec, o_spec, w_spec, w_spec,
                  pl.BlockSpec((tm, tn), lambda i, j: (i, j)),
                  pl.BlockSpec((tm, tn), lambda i, j: (i, j + nj))],
        out_specs=pl.BlockSpec((tm, tn), lambda i, j: (i, j)),
        out_shape=jax.ShapeDtypeStruct((s, d), MXU),
        compiler_params=_cparams(),
        name=name,
    )(oa, ob, wa_t, wb_t, gates, gates)


def _mix_bwd(dx, w_o, oa, ob, wa_t, wb_t, gates, name):
    s, kd = oa.shape
    d = wa_t.shape[0]
    tm, tn = _tile(s, 1024), _tile(d, 512)
    nj = d // tn

    def body(dx_ref, wo_ref, oa_ref, ob_ref, wa_ref, wb_ref, ga_ref, gb_ref,
             dya_ref, dyb_ref, dga_ref, dgb_ref):
        dm = _dot_nt(dx_ref[...], wo_ref[...])
        ya = _dot_nt(oa_ref[...], wa_ref[...])
        yb = _dot_nt(ob_ref[...], wb_ref[...])
        sa = _sigmoid(ga_ref[...].astype(F32))
        sb = _sigmoid(gb_ref[...].astype(F32))
        dya_ref[...] = (dm * sa).astype(dya_ref.dtype)
        dyb_ref[...] = (dm * sb).astype(dyb_ref.dtype)
        dga_ref[...] = (dm * ya * sa * (1.0 - sa)).astype(dga_ref.dtype)
        dgb_ref[...] = (dm * yb * sb * (1.0 - sb)).astype(dgb_ref.dtype)

    o_spec = pl.BlockSpec((tm, kd), lambda i, j: (i, 0))
    w_spec = pl.BlockSpec((tn, kd), lambda i, j: (j, 0))
    t_spec = pl.BlockSpec((tm, tn), lambda i, j: (i, j))
    return pl.pallas_call(
        body,
        grid=(s // tm, nj),
        in_specs=[pl.BlockSpec((tm, d), lambda i, j: (i, 0)),
                  pl.BlockSpec((tn, d), lambda i, j: (j, 0)),
                  o_spec, o_spec, w_spec, w_spec, t_spec,
                  pl.BlockSpec((tm, tn), lambda i, j: (i, j + nj))],
        out_specs=[t_spec] * 4,
        out_shape=[jax.ShapeDtypeStruct((s, d), MXU)] * 4,
        compiler_params=_cparams(),
        name=name,
    )(dx, w_o, oa, ob, wa_t, wb_t, gates, gates)


def _ple(p, w_pe_t, h, w_pg, other, *, backward, name):
    s, kp = p.shape
    d = w_pe_t.shape[0]
    tm, tn = _tile(s, 1024), _tile(d, 512)

    def body(p_ref, wpe_ref, h_ref, wpg_ref, other_ref, *out_refs):
        pe = _dot_nt(p_ref[...].astype(MXU), wpe_ref[...])
        gt = _dot(h_ref[...], wpg_ref[...])
        sg = _sigmoid(gt)
        if backward:
            dout = other_ref[...]
            out_refs[0][...] = (dout * sg).astype(out_refs[0].dtype)
            out_refs[1][...] = (dout * pe * sg * (1.0 - sg)).astype(out_refs[1].dtype)
        else:
            out_refs[0][...] = other_ref[...] + pe * sg

    t_spec = pl.BlockSpec((tm, tn), lambda i, j: (i, j))
    if backward:
        out_specs, out_shape = [t_spec, t_spec], [jax.ShapeDtypeStruct((s, d), MXU)] * 2
    else:
        out_specs, out_shape = [t_spec], [jax.ShapeDtypeStruct((s, d), F32)]
    outs = pl.pallas_call(
        body,
        grid=(s // tm, d // tn),
        in_specs=[pl.BlockSpec((tm, kp), lambda i, j: (i, 0)),
                  pl.BlockSpec((tn, kp), lambda i, j: (j, 0)),
                  pl.BlockSpec((tm, d), lambda i, j: (i, 0)),
                  pl.BlockSpec((d, tn), lambda i, j: (0, j)),
                  t_spec],
        out_specs=out_specs,
        out_shape=out_shape,
        compiler_params=_cparams(),
        name=name,
    )(p, w_pe_t, h, w_pg, other)
    return tuple(outs) if backward else outs[0]


def _split_dot(x, tri):
    hi = x.astype(jnp.bfloat16)
    lo = (x - hi.astype(F32)).astype(jnp.bfloat16)
    return _dot(hi, tri) + _dot(lo, tri)


def _log_sigmoids(z):
    lb = jnp.minimum(z, 0.0) - jnp.log(1.0 + jnp.exp(-jnp.abs(z)))
    return lb, lb - z


def _head_lanes(hh):
    lane = lax.broadcasted_iota(jnp.int32, (1, LANES), 1)
    return jnp.logical_and(lane >= hh * HEAD_DIM, lane < (hh + 1) * HEAD_DIM)


def _sb_fwd(qkv, name, comm=None):
    s = qkv.shape[0]
    tq = _tile(s, SB_TQ)
    nsub = SB_SUB if (s // tq) % SB_SUB == 0 else 1

    def body(q_ref, k_ref, v_ref, o_ref):
        row = lax.broadcasted_iota(jnp.int32, (tq, tq), 0)
        col = lax.broadcasted_iota(jnp.int32, (tq, tq), 1)
        causal = col < row
        tri = jnp.where(row > col, 1.0, 0.0).astype(jnp.bfloat16)
        started = [_sb_fwd_straight(q_ref, k_ref, v_ref, pl.program_id(1) * nsub + sub, sub, tq, causal, tri)
                   for sub in range(nsub)]
        for sub, (block, i, cs, accs) in enumerate(started):
            def top(cs):
                return jnp.maximum(jnp.max(cs[0]), jnp.max(cs[1]))

            def live(st):
                return jnp.logical_and(st[0] >= 0, st[1] > SB_DEAD)

            def walk(st, block=block):
                cs, accs = block(st[0], st[2], st[3], False)
                return st[0] - 1, top(cs), cs, accs

            accs = lax.while_loop(live, walk, (i - 2, top(cs), cs, accs))[3]
            o_ref[sub * tq:(sub + 1) * tq, :] = jnp.where(_head_lanes(0), accs[0], accs[1]).astype(o_ref.dtype)

    outs, couts = _call(
        body,
        grid=(N_PAIR, s // (nsub * tq)),
        in_specs=[pl.BlockSpec((nsub * tq, LANES), lambda p, i: (i, p)),
                  pl.BlockSpec((s, LANES), lambda p, i: (0, N_PAIR + p)),
                  pl.BlockSpec((s, LANES), lambda p, i: (0, 2 * N_PAIR + p))],
        out_specs=[pl.BlockSpec((nsub * tq, LANES), lambda p, i: (i, p))],
        out_shape=[jax.ShapeDtypeStruct((s, SB_W), MXU)],
        args=(qkv, qkv, qkv), name=name, comm=comm)
    return outs[0] if comm is None else (outs[0], couts)


def _sb_fwd_straight(q_ref, k_ref, v_ref, i, sub, tq, causal, tri):
    qf = q_ref[sub * tq:(sub + 1) * tq, :].astype(F32) * SCALE
    qms = [jnp.where(_head_lanes(hh), qf, 0.0).astype(MXU) for hh in range(2)]

    def block(kb, cs, accs, masked, gate=None):
        rows = pl.ds(pl.multiple_of(kb * tq, tq), tq)
        ks, vs = k_ref[rows, :], v_ref[rows, :]
        new_c, new_acc = [], []
        for hh in range(2):
            lb, lm = _log_sigmoids(_dot_nt(qms[hh], ks))
            if masked:
                lm = jnp.where(causal, lm, 0.0)
            if gate is not None:
                lm = lm * gate
            a = jnp.exp(lb + _split_dot(lm, tri) + cs[hh])
            if masked:
                a = jnp.where(causal, a, 0.0)
            if gate is not None:
                a = a * gate
            new_acc.append(accs[hh] + _dot(a.astype(MXU), vs))
            new_c.append(cs[hh] + jnp.sum(lm, axis=1, keepdims=True))
        return tuple(new_c), tuple(new_acc)

    zc, za = jnp.zeros((tq, 1), F32), jnp.zeros((tq, LANES), F32)
    cs, accs = block(i, (zc, zc), (za, za), True)
    cs, accs = block(jnp.maximum(i - 1, 0), cs, accs, False, jnp.where(i > 0, 1.0, 0.0))
    return block, i, cs, accs


def _sb_bwd(qkv, do, name, comm=None):
    s = qkv.shape[0]
    tq = _tile(s, SB_TQ)
    nq = s // tq
    nsub = SB_SUB if nq % SB_SUB == 0 else 1
    nsteps = nq // nsub

    def body(q_ref, k_ref, v_ref, do_ref, dq_ref, dk_ref, dv_ref, dk_acc, dv_acc, carries):
        step = pl.program_id(1)

        @pl.when(step == 0)
        def _():
            dk_acc[...] = jnp.zeros_like(dk_acc)
            dv_acc[...] = jnp.zeros_like(dv_acc)

        row = lax.broadcasted_iota(jnp.int32, (tq, tq), 0)
        col = lax.broadcasted_iota(jnp.int32, (tq, tq), 1)
        causal = col < row
        tri_rev = jnp.where(row > col, 1.0, 0.0).astype(jnp.bfloat16)
        tri_excl = jnp.where(row < col, 1.0, 0.0).astype(jnp.bfloat16)
        zc, za = jnp.zeros((tq, 1), F32), jnp.zeros((tq, LANES), F32)

        def top(cs):
            return jnp.maximum(jnp.max(cs[0]), jnp.max(cs[1]))

        def live(st):
            return jnp.logical_and(st[0] >= 0, st[1] > SB_DEAD)

        def row_sums(pre):
            return [jnp.sum(lm, axis=1, keepdims=True) for _, lm in pre]

        def query_block(sub):
            i = step * nsub + sub
            q_rows = slice(sub * tq, (sub + 1) * tq)
            qf = q_ref[q_rows, :].astype(F32) * SCALE
            dof = do_ref[q_rows, :]
            qms = [jnp.where(_head_lanes(hh), qf, 0.0).astype(MXU) for hh in range(2)]
            doms = [jnp.where(_head_lanes(hh), dof, jnp.zeros_like(dof)) for hh in range(2)]

            def terms(kb, masked):
                rows = pl.ds(pl.multiple_of(kb * tq, tq), tq)
                ks = k_ref[rows, :]
                out = []
                for hh in range(2):
                    lb, lm = _log_sigmoids(_dot_nt(qms[hh], ks))
                    if masked:
                        lm = jnp.where(causal, lm, 0.0)
                    out.append((lb, lm))
                return out

            def block(kb, cs, gpres, dqs, masked, gate=None, pre=None):
                rows = pl.ds(pl.multiple_of(kb * tq, tq), tq)
                ks, vs = k_ref[rows, :], v_ref[rows, :]
                pre = terms(kb, masked) if pre is None else pre
                new_g, new_dq = [], []
                dk_add, dv_add = None, None
                for hh in range(2):
                    lb, lm = pre[hh]
                    a = jnp.exp(lb + _split_dot(lm, tri_rev) + cs[hh])
                    if masked:
                        a = jnp.where(causal, a, 0.0)
                    if gate is not None:
                        a = a * gate
                    g = a * _dot_nt(doms[hh], vs)
                    gsum = gpres[hh] + _split_dot(g, tri_excl)
                    dz = g - (g + gsum) * jnp.exp(lb)
                    if masked:
                        dz = jnp.where(causal, dz, 0.0)
                    if gate is not None:
                        dz = dz * gate
                    dzb = dz.astype(MXU)
                    new_dq.append(dqs[hh] + _dot(dzb, ks))
                    dk_h = _dot_tn(dzb, qms[hh])
                    dv_h = _dot_tn(a.astype(MXU), doms[hh])
                    dk_add = dk_h if dk_add is None else dk_add + dk_h
                    dv_add = dv_h if dv_add is None else dv_add + dv_h
                    new_g.append(gpres[hh] + jnp.sum(g, axis=1, keepdims=True))
                dk_acc[rows, :] += dk_add
                dv_acc[rows, :] += dv_add
                return tuple(new_g), tuple(new_dq)

            prev = jnp.maximum(i - 1, 0)
            gate = jnp.where(i > 0, 1.0, 0.0)
            t_diag, t_prev = terms(i, True), terms(prev, False)
            c_diag = row_sums(t_diag)
            sums = row_sums(t_prev)
            c_prev = tuple(c_diag[hh] + sums[hh] * gate for hh in range(2))
            return dict(i=i, prev=prev, gate=gate, q_rows=q_rows, terms=terms, block=block,
                        t_diag=t_diag, t_prev=t_prev, c_diag=c_diag, c_prev=c_prev)

        blocks = [query_block(sub) for sub in range(nsub)]
        for qb in blocks:
            def record(st, qb=qb):
                kb, cs = st[0], st[2]
                sums = row_sums(qb["terms"](kb, False))
                for hh in range(2):
                    carries[hh, kb] = cs[hh]
                cs = tuple(cs[hh] + sums[hh] for hh in range(2))
                return kb - 1, top(cs), cs

            first = lax.while_loop(live, record, (qb["i"] - 2, top(qb["c_prev"]), qb["c_prev"]))[0] + 1
            qb["mid"] = lax.fori_loop(
                first, qb["i"] - 1,
                lambda kb, cr, qb=qb: qb["block"](kb, (carries[0, kb], carries[1, kb]), cr[0], cr[1], False),
                ((zc, zc), (za, za)))
        for qb in blocks:
            gpres, dqs = qb["block"](qb["prev"], qb["c_diag"], *qb["mid"], False, qb["gate"], qb["t_prev"])
            dqs = qb["block"](qb["i"], (zc, zc), gpres, dqs, True, None, qb["t_diag"])[1]
            dq_ref[qb["q_rows"], :] = (jnp.where(_head_lanes(0), dqs[0], dqs[1]) * SCALE).astype(dq_ref.dtype)

        @pl.when(step == nsteps - 1)
        def _():
            dk_ref[...] = dk_acc[...].astype(dk_ref.dtype)
            dv_ref[...] = dv_acc[...].astype(dv_ref.dtype)

    blk = pl.BlockSpec((nsub * tq, LANES), lambda p, i: (i, p))
    full = pl.BlockSpec((s, LANES), lambda p, i: (0, p))
    outs, couts = _call(
        body,
        grid=(N_PAIR, nsteps),
        in_specs=[blk,
                  pl.BlockSpec((s, LANES), lambda p, i: (0, N_PAIR + p)),
                  pl.BlockSpec((s, LANES), lambda p, i: (0, 2 * N_PAIR + p)),
                  blk],
        out_specs=[blk, full, full],
        out_shape=[jax.ShapeDtypeStruct((s, SB_W), MXU)] * 3,
        scratch_shapes=[pltpu.VMEM((s, LANES), F32), pltpu.VMEM((s, LANES), F32),
                        pltpu.VMEM((2, nq, tq, 1), F32)],
        args=(qkv, qkv, qkv, do), name=name, comm=comm)
    return tuple(outs) if comm is None else (tuple(outs), couts)


def _bucket_table():
    i = np.arange(BLOCK)[:, None]
    j = np.arange(2 * BLOCK)[None, :]
    d = np.maximum(BLOCK + i - j, 0)
    max_exact = N_BUCKETS // 2
    df = np.maximum(d, 1).astype(np.float32)
    large = max_exact + (np.log(df / max_exact) / math.log(MAX_DISTANCE / max_exact)
                         * (N_BUCKETS - max_exact)).astype(np.int32)
    large = np.minimum(large, N_BUCKETS - 1)
    return np.where(d < max_exact, d, large).astype(np.int32)


def _build_bias(rel_bias, buckets, name):
    def body(rb_ref, bk_ref, out_ref):
        h = pl.program_id(0)
        bk = bk_ref[...]
        acc = jnp.zeros(bk.shape, F32)
        for b in range(N_BUCKETS):
            acc = jnp.where(bk == b, rb_ref[b, h], acc)
        out_ref[...] = acc

    return pl.pallas_call(
        body,
        grid=(SW_HEADS,),
        in_specs=[pl.BlockSpec(memory_space=pltpu.SMEM),
                  pl.BlockSpec((BLOCK, 2 * BLOCK), lambda h: (0, 0))],
        out_specs=pl.BlockSpec((None, BLOCK, 2 * BLOCK), lambda h: (h, 0, 0)),
        out_shape=jax.ShapeDtypeStruct((SW_HEADS, BLOCK, 2 * BLOCK), F32),
        name=name,
    )(rel_bias, buckets)


def _bias_grad(dbias_layers, buckets, name):
    n_l = len(dbias_layers)

    def body(*refs):
        bk = refs[n_l][...]
        out_ref = refs[n_l + 1]
        db = refs[0][...]
        for r in refs[1:n_l]:
            db = db + r[...]
        lane = lax.broadcasted_iota(jnp.int32, (1, LANES), 1)
        acc = jnp.zeros((1, LANES), F32)
        for b in range(N_BUCKETS):
            part = jnp.sum(jnp.where(bk == b, db, 0.0), axis=1, keepdims=True)
            tot = jnp.sum(part, axis=0, keepdims=True)
            acc = jnp.where(lane == b, tot, acc)
        out_ref[...] = acc

    hspec = pl.BlockSpec((None, BLOCK, 2 * BLOCK), lambda h: (h, 0, 0))
    return pl.pallas_call(
        body,
        grid=(SW_HEADS,),
        in_specs=[hspec] * n_l + [pl.BlockSpec((BLOCK, 2 * BLOCK), lambda h: (0, 0))],
        out_specs=pl.BlockSpec((None, 1, LANES), lambda h: (h, 0, 0)),
        out_shape=jax.ShapeDtypeStruct((SW_HEADS, 1, LANES), F32),
        name=name,
    )(*dbias_layers, buckets)


GROUP_ROWS = SW_GROUP * BLOCK


def _group_lanes(g):
    lane = lax.broadcasted_iota(jnp.int32, (1, LANES), 1)
    gvec = jnp.zeros((1, LANES), jnp.int32) + g
    return jnp.where(lane >= HEAD_DIM, 1, 0) == gvec, gvec


def _stack_heads(x, g):
    kv_lanes, gvec = _group_lanes(g)
    parts = []
    for j in range(SW_GROUP):
        half = x[:, (j // 2) * LANES:(j // 2 + 1) * LANES]
        moved = jnp.where(gvec == j % 2, half, pltpu.roll(half, HEAD_DIM, 1))
        parts.append(jnp.where(kv_lanes, moved, 0.0))
    return jnp.concatenate(parts, axis=0)


def _unstack_heads(y, g):
    _, gvec = _group_lanes(g)
    heads = []
    for j in range(SW_GROUP):
        yj = y[j * BLOCK:(j + 1) * BLOCK]
        heads.append(jnp.where(gvec == j % 2, yj, pltpu.roll(yj, HEAD_DIM, 1)))
    pairs = [jnp.where(_head_lanes(0), heads[2 * p], heads[2 * p + 1]) for p in range(SW_GROUP // 2)]
    return jnp.concatenate(pairs, axis=1)


def _per_head_col(values):
    return jnp.concatenate([jnp.zeros((BLOCK, 1), F32) + v for v in values], axis=0)


def _swa_scores(qs, kp, kc, bias_ref, n):
    row = jnp.bitwise_and(lax.broadcasted_iota(jnp.int32, (GROUP_ROWS, BLOCK), 0), BLOCK - 1)
    col = lax.broadcasted_iota(jnp.int32, (GROUP_ROWS, BLOCK), 1)
    bias = bias_ref[...].reshape(GROUP_ROWS, 2 * BLOCK)
    s1 = _dot_nt(qs, kp) + bias[:, :BLOCK]
    s2 = _dot_nt(qs, kc) + bias[:, BLOCK:]
    no_prev = jnp.where(n > 0, 0, BLOCK)
    s1 = jnp.where(col > row + no_prev, s1, NEG)
    s2 = jnp.where(col <= row, s2, NEG)
    return s1, s2


def _swa_specs(s):
    q_blk = 3 * SB_W // (2 * LANES)
    k_blk = (3 * SB_W + SW_W) // LANES
    return (pl.BlockSpec((s, 2 * LANES), lambda g: (0, q_blk + g)),
            pl.BlockSpec((s, LANES), lambda g: (0, k_blk)),
            pl.BlockSpec((s, LANES), lambda g: (0, k_blk + 1)))


def _swa_fwd(qkv, bias, sinks, name, comm=None):
    s = qkv.shape[0]
    nb = s // BLOCK

    def body(sink_ref, q_ref, k_ref, v_ref, bias_ref, o_ref, lse_ref):
        g = pl.program_id(0)
        sink = _per_head_col([sink_ref[SW_GROUP * g + j] for j in range(SW_GROUP)])
        lane = lax.broadcasted_iota(jnp.int32, (1, LANES), 1)

        def step(n, carry):
            r0 = pl.multiple_of(n * BLOCK, BLOCK)
            p0 = pl.multiple_of(jnp.maximum(n - 1, 0) * BLOCK, BLOCK)
            cur, prev = pl.ds(r0, BLOCK), pl.ds(p0, BLOCK)
            qs = _stack_heads(q_ref[cur, :].astype(F32) * SCALE, g).astype(MXU)
            s1, s2 = _swa_scores(qs, k_ref[prev, :], k_ref[cur, :], bias_ref, n)
            m = jnp.maximum(jnp.max(jnp.maximum(s1, s2), axis=1, keepdims=True), sink)
            e1 = jnp.exp(s1 - m)
            e2 = jnp.exp(s2 - m)
            den = jnp.sum(e1 + e2, axis=1, keepdims=True) + jnp.exp(sink - m)
            o = _dot((e1 / den).astype(MXU), v_ref[prev, :]) + _dot((e2 / den).astype(MXU), v_ref[cur, :])
            o_ref[cur, :] = _unstack_heads(o, g).astype(o_ref.dtype)
            lse = m + jnp.log(den)
            lse_row = jnp.zeros((BLOCK, LANES), F32)
            for j in range(SW_GROUP):
                lse_row = jnp.where(lane == j, lse[j * BLOCK:(j + 1) * BLOCK], lse_row)
            lse_ref[cur, :] = lse_row
            return carry

        lax.fori_loop(0, nb, step, 0, unroll=2)

    outs, couts = _call(
        body,
        grid=(SW_KV,),
        in_specs=[pl.BlockSpec(memory_space=pltpu.SMEM), *_swa_specs(s),
                  pl.BlockSpec((SW_GROUP, BLOCK, 2 * BLOCK), lambda g: (g, 0, 0))],
        out_specs=[pl.BlockSpec((s, 2 * LANES), lambda g: (0, g)),
                   pl.BlockSpec((None, s, LANES), lambda g: (g, 0, 0))],
        out_shape=[jax.ShapeDtypeStruct((s, SW_W), MXU), jax.ShapeDtypeStruct((SW_KV, s, LANES), F32)],
        args=(sinks, qkv, qkv, qkv, bias), name=name, comm=comm)
    return tuple(outs) if comm is None else (tuple(outs), couts)


def _swa_bwd(qkv, bias, sinks, do, lse, name, comm=None):
    s = qkv.shape[0]
    nb = s // BLOCK

    def body(sink_ref, q_ref, k_ref, v_ref, bias_ref, do_ref, lse_ref,
             dq_ref, dk_ref, dv_ref, dbias_ref, dsink_ref, dk_acc, dv_acc):
        g = pl.program_id(0)
        sink = _per_head_col([sink_ref[SW_GROUP * g + j] for j in range(SW_GROUP)])
        lane = lax.broadcasted_iota(jnp.int32, (1, LANES), 1)

        @pl.when(g == 0)
        def _():
            dk_acc[...] = jnp.zeros_like(dk_acc)
            dv_acc[...] = jnp.zeros_like(dv_acc)

        dbias_ref[...] = jnp.zeros_like(dbias_ref)

        def step(n, dsink_rows):
            r0 = pl.multiple_of(n * BLOCK, BLOCK)
            p0 = pl.multiple_of(jnp.maximum(n - 1, 0) * BLOCK, BLOCK)
            cur, prev = pl.ds(r0, BLOCK), pl.ds(p0, BLOCK)
            qs = _stack_heads(q_ref[cur, :].astype(F32) * SCALE, g).astype(MXU)
            dos = _stack_heads(do_ref[cur, :].astype(F32), g).astype(MXU)
            kp, kc, vp, vc = k_ref[prev, :], k_ref[cur, :], v_ref[prev, :], v_ref[cur, :]
            lse_row = lse_ref[cur, :]
            lse = jnp.concatenate([jnp.sum(jnp.where(lane == j, lse_row, 0.0), axis=1, keepdims=True)
                                   for j in range(SW_GROUP)], axis=0)
            s1, s2 = _swa_scores(qs, kp, kc, bias_ref, n)
            pr1 = jnp.exp(s1 - lse)
            pr2 = jnp.exp(s2 - lse)
            dpr1 = _dot_nt(dos, vp)
            dpr2 = _dot_nt(dos, vc)
            delta = jnp.sum(pr1 * dpr1 + pr2 * dpr2, axis=1, keepdims=True)
            ds1 = pr1 * (dpr1 - delta)
            ds2 = pr2 * (dpr2 - delta)
            dbias_ref[:, :, :BLOCK] += ds1.reshape(SW_GROUP, BLOCK, BLOCK)
            dbias_ref[:, :, BLOCK:] += ds2.reshape(SW_GROUP, BLOCK, BLOCK)
            ds1b, ds2b = ds1.astype(MXU), ds2.astype(MXU)
            dq = _dot(ds1b, kp) + _dot(ds2b, kc)
            dq_ref[cur, :] = (_unstack_heads(dq, g) * SCALE).astype(dq_ref.dtype)
            dk_acc[prev, :] += _dot_tn(ds1b, qs)
            dk_acc[cur, :] += _dot_tn(ds2b, qs)
            dv_acc[prev, :] += _dot_tn(pr1.astype(MXU), dos)
            dv_acc[cur, :] += _dot_tn(pr2.astype(MXU), dos)
            return dsink_rows - jnp.exp(sink - lse) * delta

        rows = lax.fori_loop(0, nb, step, jnp.zeros((GROUP_ROWS, 1), F32), unroll=2)
        for j in range(SW_GROUP):
            dsink_ref[j] = jnp.broadcast_to(jnp.sum(rows[j * BLOCK:(j + 1) * BLOCK], axis=0, keepdims=True),
                                            (1, LANES))

        @pl.when(g == SW_KV - 1)
        def _():
            dk_ref[...] = dk_acc[...].astype(dk_ref.dtype)
            dv_ref[...] = dv_acc[...].astype(dv_ref.dtype)

    grp = pl.BlockSpec((s, 2 * LANES), lambda g: (0, g))
    kv_out = pl.BlockSpec((s, LANES), lambda g: (0, 0))
    bspec = pl.BlockSpec((SW_GROUP, BLOCK, 2 * BLOCK), lambda g: (g, 0, 0))
    outs, couts = _call(
        body,
        grid=(SW_KV,),
        in_specs=[pl.BlockSpec(memory_space=pltpu.SMEM), *_swa_specs(s), bspec, grp,
                  pl.BlockSpec((None, s, LANES), lambda g: (g, 0, 0))],
        out_specs=[grp, kv_out, kv_out, bspec, pl.BlockSpec((SW_GROUP, 1, LANES), lambda g: (g, 0, 0))],
        out_shape=[jax.ShapeDtypeStruct((s, SW_W), MXU),
                   jax.ShapeDtypeStruct((s, LANES), MXU),
                   jax.ShapeDtypeStruct((s, LANES), MXU),
                   jax.ShapeDtypeStruct((SW_HEADS, BLOCK, 2 * BLOCK), F32),
                   jax.ShapeDtypeStruct((SW_HEADS, 1, LANES), F32)],
        scratch_shapes=[pltpu.VMEM((s, LANES), F32), pltpu.VMEM((s, LANES), F32)],
        args=(sinks, qkv, qkv, qkv, bias, do, lse), name=name, comm=comm)
    return tuple(outs) if comm is None else (tuple(outs), couts)


class _NoPlan:
    def comm(self, name):
        return None

    def done(self, name, outs):
        pass

    def grad(self, layer, name, value):
        pass


def _run(plan, fn, *args, name, **kw):
    comm = plan.comm(name)
    if comm is None:
        return fn(*args, name=name, **kw)
    res, outs = fn(*args, name=name, comm=comm, **kw)
    plan.done(name, outs)
    return res


def _norm_bwd(dh, x, dres, r, g):
    xhat = x * r
    dxhat = dh * g
    dx = dres + r * (dxhat - xhat * jnp.mean(dxhat * xhat, axis=-1, keepdims=True))
    return dx, dx, jnp.sum(dh * xhat, axis=0, keepdims=True)


def _residual_norm(acc, res, g):
    x = res + acc
    r = lax.rsqrt(jnp.mean(x * x, axis=-1, keepdims=True) + EPS)
    return x, (x * r) * g, r


def _layer_fwd(x, p, w, g_mix, g_mlp, g_pe, sinks, bias, tag, plan):
    h1, r1 = _run(plan, _rms_fwd, x, g_mix, name=f"rms_mix_{tag}")
    qkv = _run(plan, _mm, h1, w["w_qkv_t"], tb=True, out_dtypes=(MXU,), name=f"proj_qkv_{tag}")
    gates = _run(plan, _mm, h1, w["w_gate_t"], tb=True, out_dtypes=(MXU,), name=f"proj_gate_{tag}")
    oa = _run(plan, _sb_fwd, qkv, name=f"sb_fwd_{tag}")
    ob, lse = _run(plan, _swa_fwd, qkv, bias, sinks, name=f"swa_fwd_{tag}")
    merged = _mix_fwd(oa, ob, w["w_up_a_t"], w["w_up_b_t"], gates, f"mix_fwd_{tag}")
    d = x.shape[1]
    normed = (F32, MXU, (F32, "col"))
    x1, h2, r2 = _run(plan, _mm, merged, w["w_o"], extras=(x, g_mlp), epi=_residual_norm, out_dtypes=normed,
                      tn=d, name=f"out_proj_{tag}")
    u, act = _run(plan, _mm, h2, w["w_ff1_t"], tb=True,
                  epi=lambda acc: (acc, jnp.square(jnp.maximum(acc, 0.0))),
                  out_dtypes=(MXU, MXU), name=f"ff1_{tag}")
    x2, h3, r3 = _run(plan, _mm, act, w["w_ff2"], extras=(x1, g_pe), epi=_residual_norm, out_dtypes=normed,
                      tn=d, name=f"ff2_{tag}")
    x3 = _ple(p, w["w_pe_t"], h3, w["w_pg"], x2, backward=False, name=f"ple_fwd_{tag}")
    saved = dict(x=x, h1=h1, r1=r1, gates=gates, qkv=qkv, lse=lse, oa=oa, ob=ob, merged=merged,
                 x1=x1, h2=h2, r2=r2, u=u, act=act, x2=x2, h3=h3, r3=r3)
    return x3, saved


def _layer_bwd(dx3, sv, p, w, g_mix, g_mlp, g_pe, sinks, bias, layer, plan):
    tag = f"l{layer}"
    gw = {}
    wire = (WIRE,)

    def dw(name, a, b):
        gw[name] = _run(plan, _mm, a, b, ta=True, out_dtypes=wire, tk=2048, name=f"d{name}_{tag}")
        plan.grad(layer, name, gw[name])

    dpe, dgt = _ple(p, w["w_pe_t"], sv["h3"], w["w_pg"], dx3, backward=True, name=f"ple_bwd_{tag}")
    dw("w_pe", dpe, p)
    dw("w_pg", sv["h3"], dgt)
    d = dx3.shape[1]
    grads = (F32, MXU, (F32, "rowsum"))
    dx2, dx2b, dg_pe = _run(plan, _mm, dgt, w["w_pg"], tb=True, extras=(sv["x2"], dx3, sv["r3"], g_pe),
                            epi=_norm_bwd, out_dtypes=grads, tm=512, tn=d, name=f"dh_pe_{tag}")
    dw("w_ff2", sv["act"], dx2b)
    du = _run(plan, _mm, dx2b, w["w_ff2"], tb=True, extras=(sv["u"],),
              epi=lambda acc, u: acc * (2.0 * jnp.maximum(u.astype(F32), 0.0)), out_dtypes=(MXU,),
              name=f"dact_{tag}")
    dw("w_ff1", du, sv["h2"])
    dx1, dx1b, dg_mlp = _run(plan, _mm, du, w["w_ff1_t"], extras=(sv["x1"], dx2, sv["r2"], g_mlp),
                             epi=_norm_bwd, out_dtypes=grads, tm=1024, tn=d, name=f"dh_mlp_{tag}")
    dw("w_o", sv["merged"], dx1b)
    dya, dyb, dga, dgb = _mix_bwd(dx1b, w["w_o"], sv["oa"], sv["ob"], w["w_up_a_t"], w["w_up_b_t"],
                                  sv["gates"], f"mix_bwd_{tag}")
    dw("w_up_a", dya, sv["oa"])
    dw("w_up_b", dyb, sv["ob"])
    doa = _run(plan, _mm, dya, w["w_up_a_t"], out_dtypes=(MXU,), name=f"do_a_{tag}")
    dob = _run(plan, _mm, dyb, w["w_up_b_t"], out_dtypes=(MXU,), name=f"do_b_{tag}")
    dqb, dkb, dvb, dbias, dsink = _run(plan, _swa_bwd, sv["qkv"], bias, sinks, dob, sv["lse"],
                                       name=f"swa_bwd_{tag}")
    dqa, dka, dva = _run(plan, _sb_bwd, sv["qkv"], doa, name=f"sb_bwd_{tag}")
    dqkv = jnp.concatenate([dqa, dka, dva, dqb, dkb, dvb], axis=1)
    gw_qkv = _mm(dqkv, sv["h1"], ta=True, out_dtypes=wire, tk=2048, name=f"dw_qkv_{tag}")
    gw_ga = _mm(dga, sv["h1"], ta=True, out_dtypes=wire, tk=2048, name=f"dw_ga_{tag}")
    gw_gb = _mm(dgb, sv["h1"], ta=True, out_dtypes=wire, tk=2048, name=f"dw_gb_{tag}")
    gw["w_in"] = jnp.concatenate([gw_qkv, gw_ga, gw_gb], axis=0)
    plan.grad(layer, "w_in", gw["w_in"])
    d = dga.shape[1]
    add = lambda acc, res: res + acc
    dh1 = _run(plan, _mm, dga, w["w_gate_t"][:d], name=f"dh_ga_{tag}")
    dh1 = _run(plan, _mm, dgb, w["w_gate_t"][d:], extras=(dh1,), epi=add, name=f"dh_gb_{tag}")
    dx, _, dg_mix = _run(plan, _mm, dqkv, w["w_qkv_t"], tk=768, extras=(dh1, sv["x"], dx1, sv["r1"], g_mix),
                         epi=lambda acc, prev, *rest: _norm_bwd(acc + prev, *rest), out_dtypes=grads,
                         tm=512, tn=d, name=f"dh_qkv_{tag}")
    small = dict(g_mix=dg_mix, g_mlp=dg_mlp, g_pe=dg_pe, sinks=dsink[:, 0, 0], dbias=dbias)
    return dx, gw, small


def _local_step(x, p, target, weights, g_mix, g_mlp, g_pe, g_final, sinks, rel_bias, plan=None):
    plan = _NoPlan() if plan is None else plan
    depth = g_mix.shape[0]
    buckets = jnp.asarray(_bucket_table())
    bias = _build_bias(rel_bias, buckets, "build_bias")
    saved, wfull = [], []
    h = x
    for l in range(depth):
        wfull.append(weights(l))
        h, sv = _layer_fwd(h, p[l], wfull[l], g_mix[l:l + 1], g_mlp[l:l + 1], g_pe[l:l + 1],
                           sinks[l], bias, f"l{l}", plan)
        saved.append(sv)
    loss_row, dx, dg_final = _loss_head(h, g_final[None, :], target, "loss_head")
    gws = [None] * depth
    smalls = [None] * depth
    for l in reversed(range(depth)):
        dx, gws[l], smalls[l] = _layer_bwd(dx, saved[l], p[l], wfull[l], g_mix[l:l + 1], g_mlp[l:l + 1],
                                           g_pe[l:l + 1], sinks[l], bias, l, plan)
    drel = _bias_grad([sm["dbias"] for sm in smalls], buckets, "bias_grad")[:, 0, :N_BUCKETS].T
    small = dict(
        g_mix=jnp.concatenate([sm["g_mix"] for sm in smalls], axis=0),
        g_mlp=jnp.concatenate([sm["g_mlp"] for sm in smalls], axis=0),
        g_pe=jnp.concatenate([sm["g_pe"] for sm in smalls], axis=0),
        g_final=dg_final[0],
        sinks=jnp.stack([sm["sinks"] for sm in smalls], axis=0),
        rel_bias=drel,
    )
    return loss_row, dx, gws, small


MESH_ID = pl.DeviceIdType.MESH


def _position():
    return lax.axis_index("x"), lax.axis_index("y"), lax.axis_index("c")


def _gather_comm(shards):
    n = len(shards)

    def copies(pos, x_refs, out_refs, sems):
        send_sems, recv_sems, local_sems = sems
        x, y, c = pos
        me, sibling = (x, y, c), (x, y, 1 - c)
        chips = [(1 - x, y), (x, 1 - y), (1 - x, 1 - y)]

        def slot(a, px, py, pc):
            return out_refs[a].at[4 * px + 2 * py + pc]

        def copy(a, k, block, to, src=None):
            return pltpu.make_async_remote_copy(
                src_ref=slot(a, *block) if src is None else src, dst_ref=slot(a, *block),
                send_sem=send_sems.at[a, k], recv_sem=recv_sems.at[a, k],
                device_id=to, device_id_type=MESH_ID)

        mine = [pltpu.make_async_copy(x_refs[a], slot(a, *me), local_sems.at[a]) for a in range(n)]
        first = []
        for a in range(n):
            first.append(copy(a, 0, me, sibling, src=x_refs[a]))
            first += [copy(a, 1 + j, me, (*chip, c), src=x_refs[a]) for j, chip in enumerate(chips)]
        return me, sibling, chips, copy, mine, first

    def start(pos, x_refs, out_refs, sems):
        _, _, _, _, mine, first = copies(pos, x_refs, out_refs, sems)
        for cp in mine + first:
            cp.start()

    def finish(pos, x_refs, out_refs, sems):
        me, sibling, chips, copy, mine, first = copies(pos, x_refs, out_refs, sems)
        c = pos[2]
        passed = []
        for j, chip in enumerate(chips):
            for a in range(n):
                copy(a, 1 + j, (*chip, c), me).wait_recv()
                fwd = copy(a, 4 + j, (*chip, c), sibling)
                fwd.start()
                passed.append(fwd)
        for a in range(n):
            copy(a, 0, sibling, me).wait_recv()
            for j, chip in enumerate(chips):
                copy(a, 4 + j, (*chip, 1 - c), me).wait_recv()
        for cp in first + passed:
            cp.wait_send()
        for cp in mine:
            cp.wait()

    return _Comm(shards, [jax.ShapeDtypeStruct((N_DEV,) + s.shape, s.dtype) for s in shards],
                 [pltpu.SemaphoreType.DMA((n, 7)), pltpu.SemaphoreType.DMA((n, 7)),
                  pltpu.SemaphoreType.DMA((n,))], start, finish)


def _exchange_comm(arrays, n_slots, route):
    n = len(arrays)

    def copies(pos, in_refs, out_refs, sems):
        send_sems, recv_sems = sems
        out = []
        for a in range(n):
            for j in range(n_slots):
                src_slot, peer = route(pos, j)
                out.append(pltpu.make_async_remote_copy(
                    src_ref=in_refs[a].at[src_slot], dst_ref=out_refs[a].at[j],
                    send_sem=send_sems.at[a, j], recv_sem=recv_sems.at[a, j],
                    device_id=peer, device_id_type=MESH_ID))
        return out

    def start(pos, in_refs, out_refs, sems):
        for cp in copies(pos, in_refs, out_refs, sems):
            cp.start()

    def finish(pos, in_refs, out_refs, sems):
        for cp in copies(pos, in_refs, out_refs, sems):
            cp.wait()

    return _Comm(arrays, [jax.ShapeDtypeStruct((n_slots,) + g.shape[1:], g.dtype) for g in arrays],
                 [pltpu.SemaphoreType.DMA((n, n_slots)), pltpu.SemaphoreType.DMA((n, n_slots))], start, finish)


def _rs_sibling_comm(gs):
    return _exchange_comm(gs, 4, lambda pos, j: (2 * j + (1 - pos[2]), (pos[0], pos[1], 1 - pos[2])))


def _chip_of(k, x, y):
    return x ^ ((k + 1) & 1), y ^ (((k + 1) >> 1) & 1)


def _chip_partials(pos, gs, recvs, name):
    n = len(gs)

    def body(pos_ref, *refs):
        for a in range(n):
            refs[2 * n + a][...] = (refs[a][...].astype(F32) + refs[n + a][...].astype(F32)
                                    ).astype(refs[2 * n + a].dtype)

    def g_map(k, pos_ref):
        cx, cy = _chip_of(k, pos_ref[0], pos_ref[1])
        return (4 * cx + 2 * cy + pos_ref[2], 0, 0)

    def r_map(k, pos_ref):
        cx, cy = _chip_of(k, pos_ref[0], pos_ref[1])
        return (2 * cx + cy, 0, 0)

    slab = [(None,) + g.shape[1:] for g in gs]
    return pl.pallas_call(
        body,
        grid_spec=pltpu.PrefetchScalarGridSpec(
            num_scalar_prefetch=1,
            grid=(4,),
            in_specs=[pl.BlockSpec(sh, g_map) for sh in slab] + [pl.BlockSpec(sh, r_map) for sh in slab],
            out_specs=[pl.BlockSpec(sh, lambda k, pos_ref: (k, 0, 0)) for sh in slab],
        ),
        out_shape=[jax.ShapeDtypeStruct((4,) + g.shape[1:], g.dtype) for g in gs],
        compiler_params=_cparams(),
        name=name,
    )(pos, *gs, *recvs)


def _rs_chips_comm(parts):
    return _exchange_comm(parts, 3, lambda pos, k: (k, (*_chip_of(k, pos[0], pos[1]), pos[2])))


def _adamw_math(w, g, m, v):
    m = ADAM_B1 * m + (1.0 - ADAM_B1) * g
    v = ADAM_B2 * v + (1.0 - ADAM_B2) * (g * g)
    m_hat = m / (1.0 - ADAM_B1 ** ADAM_STEP)
    v_hat = v / (1.0 - ADAM_B2 ** ADAM_STEP)
    delta = -ADAM_LR * (m_hat / (jnp.sqrt(v_hat) + ADAM_EPS) + ADAM_WD * w)
    return delta, m, v


def _adamw_weight(parts, recvs, w, m, v, name, grad_t=False, comm=None):
    depth, a, b = w.shape
    ta = _tile(a, 288, unit=LANES if grad_t else 16)
    ni = a // ta
    g_block = (b, ta) if grad_t else (ta, b)

    def body(*refs):
        p_refs, r_refs = refs[:depth], refs[depth:2 * depth]
        w_ref, m_ref, v_ref = refs[2 * depth:2 * depth + 3]
        g_out, d_out, m_out, v_out = refs[2 * depth + 3:]
        layer = pl.program_id(0)
        g = jnp.zeros(g_block, F32)
        for l in range(depth):
            gl = p_refs[l][...].astype(F32)
            for k in range(3):
                gl = gl + r_refs[l][k].astype(F32)
            g = jnp.where(layer == l, gl, g)
        if grad_t:
            g = g.T
        delta, m_new, v_new = _adamw_math(w_ref[...], g, m_ref[...], v_ref[...])
        g_out[...] = g
        d_out[...] = delta
        m_out[...] = m_new
        v_out[...] = v_new

    def hold(l):
        return lambda layer, i: jnp.where(layer == l, i, jnp.where(layer < l, 0, ni - 1))

    def g_index(slot, f):
        if grad_t:
            return lambda layer, i: (slot, 0, f(layer, i))
        return lambda layer, i: (slot, f(layer, i), 0)

    p_specs = [pl.BlockSpec((None,) + g_block, g_index(3, hold(l))) for l in range(depth)]
    r_specs = [pl.BlockSpec((3,) + g_block, g_index(0, hold(l))) for l in range(depth)]
    row = pl.BlockSpec((None, ta, b), lambda layer, i: (layer, i, 0))
    outs, couts = _call(
        body,
        grid=(depth, ni),
        in_specs=p_specs + r_specs + [row, row, row],
        out_specs=[row] * 4,
        out_shape=[jax.ShapeDtypeStruct(w.shape, F32)] * 4,
        args=(*parts, *recvs, w, m, v), name=name, comm=comm)
    return outs if comm is None else (outs, couts)


def _adamw_replicated(gathered, w, m, v, name):
    r, lanes = w.shape

    def body(g_ref, w_ref, m_ref, v_ref, g_out, d_out, m_out, v_out):
        g = g_ref[0]
        for k in range(1, N_DEV):
            g = g + g_ref[k]
        delta, m_new, v_new = _adamw_math(w_ref[...], g, m_ref[...], v_ref[...])
        g_out[...] = g
        d_out[...] = delta
        m_out[...] = m_new
        v_out[...] = v_new

    return pl.pallas_call(
        body,
        out_shape=[jax.ShapeDtypeStruct((r, lanes), F32)] * 4,
        name=name,
    )(gathered, w, m, v)


def _wire_shard(name, shard):
    return (shard.T if name in COL_SHARDED else shard).astype(WIRE)


def _full_weight(gathered):
    return gathered.reshape(N_DEV * gathered.shape[1], gathered.shape[2])


def _to_slabs(gfull):
    return gfull.reshape(N_DEV, gfull.shape[0] // N_DEV, gfull.shape[1])


def _pack_small(arrs):
    rows = []
    for a in arrs:
        flat = a.astype(F32).reshape(-1)
        pad = (-flat.shape[0]) % LANES
        rows.append(jnp.pad(flat, (0, pad)).reshape(-1, LANES))
    packed = jnp.concatenate(rows, axis=0)
    return jnp.pad(packed, ((0, (-packed.shape[0]) % 8), (0, 0)))


def _unpack_small(packed, shapes):
    out, off = [], 0
    for shp in shapes:
        n = math.prod(shp)
        rows = -(-n // LANES)
        out.append(packed[off:off + rows].reshape(-1)[:n].reshape(shp))
        off += rows
    return out


def _of(layer, *names):
    return tuple((layer, n) for n in names)


MLP_W = ("w_ff1", "w_ff2", "w_pe", "w_pg")

GATHERS = (
    ("rms_mix_l0", _of(0, "w_in")),
    ("proj_qkv_l0", _of(0, "w_up_a", "w_up_b", "w_o")),
    ("proj_gate_l0", _of(0, "w_pe", "w_pg")),
    ("sb_fwd_l0", _of(0, "w_ff1", "w_ff2")),
    ("swa_fwd_l0", _of(1, "w_in")),
    ("sb_fwd_l1", _of(1, "w_up_a", "w_up_b", "w_o", "w_pe", "w_pg", "w_ff1")),
    ("swa_fwd_l1", _of(1, "w_ff2")),
)
REDUCES = (
    (_of(1, "w_ff1"), "dw_ff2_l0", "dact_l0"),
    (_of(1, "w_ff2"), "dw_ff2_l0", "dw_ff1_l0"),
    (_of(1, "w_in"), "dw_ff2_l0", "sb_bwd_l0"),
    (_of(1, "w_up_a", "w_up_b", "w_o", "w_pe", "w_pg"), "dw_ff2_l0", "swa_bwd_l0"),
    (_of(0, *MLP_W), "dh_mlp_l0", "sb_bwd_l0"),
    (_of(0, "w_o", "w_up_a", "w_up_b"), "do_a_l0", "sb_bwd_l0"),
    (_of(0, "w_in"), "dh_ga_l0", "dh_qkv_l0"),
)


def _merge_comms(comms):
    if len(comms) == 1:
        return comms[0]

    def cuts(counts):
        edges = [0]
        for c in counts:
            edges.append(edges[-1] + c)
        return [slice(a, b) for a, b in zip(edges[:-1], edges[1:])]

    s_in = cuts([len(c.inputs) for c in comms])
    s_out = cuts([len(c.out_shapes) for c in comms])
    s_sem = cuts([len(c.sems) for c in comms])

    def start(pos, cin, cout, csem):
        for c, i, o, s in zip(comms, s_in, s_out, s_sem):
            c.start(pos, cin[i], cout[o], csem[s])

    def finish(pos, cin, cout, csem):
        for c, i, o, s in zip(comms, s_in, s_out, s_sem):
            c.finish(pos, cin[i], cout[o], csem[s])

    return _Comm(sum([c.inputs for c in comms], []), sum([c.out_shapes for c in comms], []),
                 sum([c.sems for c in comms], []), start, finish)


class _LayerWeights:
    def __init__(self, full, layer):
        self.full, self.layer, self.cache = full, layer, {}

    def __getitem__(self, name):
        if name not in self.cache:
            if name == "w_qkv_t":
                self.cache[name] = self.full[(self.layer, "w_in")][:QKV_COLS]
            elif name == "w_gate_t":
                self.cache[name] = self.full[(self.layer, "w_in")][QKV_COLS:]
            else:
                base = name[:-2] if name.endswith("_t") else name
                assert (base in COL_SHARDED) == name.endswith("_t"), name
                self.cache[name] = self.full[(self.layer, base)]
        return self.cache[name]


class _Plan:
    def __init__(self, w_sh, pos):
        self.w_sh = dict(zip(WEIGHTS, w_sh))
        self.pos = pos
        self.full, self.gw, self.parts, self.recv = {}, {}, {}, {}
        self.slabs = {}
        self.hosted = {}
        for i, (host, _) in enumerate(GATHERS):
            self.hosted.setdefault(host, []).append(("gather", i))
        for i, (_, sib_host, chip_host) in enumerate(REDUCES):
            self.hosted.setdefault(sib_host, []).append(("sibling", i))
            self.hosted.setdefault(chip_host, []).append(("chips", i))

    def _gather(self, i):
        return _gather_comm([_wire_shard(n, self.w_sh[n][layer]) for layer, n in GATHERS[i][1]])

    def _gathered(self, i, outs):
        for (layer, n), g in zip(GATHERS[i][1], outs):
            self.full[(layer, n)] = _full_weight(g)

    def weights(self, layer):
        return _LayerWeights(self.full, layer)

    def grad(self, layer, name, value):
        self.gw[(layer, name)] = value

    def _sibling(self, i):
        self.slabs[i] = [_to_slabs(self.gw[item]) for item in REDUCES[i][0]]
        return _rs_sibling_comm(self.slabs[i])

    def _sibling_done(self, i, outs):
        parts = _chip_partials(self.pos, self.slabs[i], outs, f"chip_partials_{i}")
        for item, part in zip(REDUCES[i][0], parts):
            self.parts[item] = part

    def _chips(self, i):
        return _rs_chips_comm([self.parts[item] for item in REDUCES[i][0]])

    def _chips_done(self, i, outs):
        for item, r in zip(REDUCES[i][0], outs):
            self.recv[item] = r

    def comm(self, name):
        if name not in self.hosted:
            return None
        make = {"gather": self._gather, "sibling": self._sibling, "chips": self._chips}
        return _merge_comms([make[kind](i) for kind, i in self.hosted[name]])

    def done(self, name, outs):
        took = {"gather": self._gathered, "sibling": self._sibling_done, "chips": self._chips_done}
        off = 0
        for kind, i in self.hosted[name]:
            n = len(GATHERS[i][1]) if kind == "gather" else len(REDUCES[i][0])
            took[kind](i, outs[off:off + n])
            off += n


def kernel(x, p, w_in, w_up_a, w_up_b, w_o, w_ff1, w_ff2, w_pe, w_pg, g_mix, g_mlp, g_pe, g_final, sinks, rel_bias, loss_target, m_w_in, m_w_up_a, m_w_up_b, m_w_o, m_w_ff1, m_w_ff2, m_w_pe, m_w_pg, m_g_mix, m_g_mlp, m_g_pe, m_g_final, m_sinks, m_rel_bias, v_w_in, v_w_up_a, v_w_up_b, v_w_o, v_w_ff1, v_w_ff2, v_w_pe, v_w_pg, v_g_mix, v_g_mlp, v_g_pe, v_g_final, v_sinks, v_rel_bias):
    w_sh = [w_in, w_up_a, w_up_b, w_o, w_ff1, w_ff2, w_pe, w_pg]
    m_sh = [m_w_in, m_w_up_a, m_w_up_b, m_w_o, m_w_ff1, m_w_ff2, m_w_pe, m_w_pg]
    v_sh = [v_w_in, v_w_up_a, v_w_up_b, v_w_o, v_w_ff1, v_w_ff2, v_w_pe, v_w_pg]
    depth = w_in.shape[0]
    assert depth == 2 and x.shape[-1] * 2 + QKV_COLS == w_in.shape[2] * N_DEV

    px, py, pc = _position()
    plan = _Plan(w_sh, jnp.stack([px, py, pc]).astype(jnp.int32))
    loss_row, grad_x, _, small = _local_step(
        x[0], p[:, 0], loss_target[0], plan.weights, g_mix, g_mlp, g_pe, g_final, sinks, rel_bias, plan=plan)

    small_g = _pack_small([small[n] for n in SMALL] + [loss_row[0, :1]])
    grad_w, delta_w, new_m, new_v = [], [], [], []
    for a, name in enumerate(WEIGHTS):
        parts = [plan.parts[(l, name)] for l in range(depth)]
        recvs = [plan.recv[(l, name)] for l in range(depth)]
        if name == "w_in":
            flip = lambda t: t.transpose(0, 2, 1)
            outs, (small_all,) = _adamw_weight(parts, recvs, flip(w_sh[a]), flip(m_sh[a]), flip(v_sh[a]),
                                               f"adamw_{name}", comm=_gather_comm([small_g]))
            outs = [flip(o) for o in outs]
        else:
            outs = _adamw_weight(parts, recvs, w_sh[a], m_sh[a], v_sh[a], f"adamw_{name}",
                                 grad_t=name in COL_SHARDED)
        for lst, o in zip((grad_w, delta_w, new_m, new_v), outs):
            lst.append(o)

    small_w = [g_mix, g_mlp, g_pe, g_final, sinks, rel_bias]
    small_m = [m_g_mix, m_g_mlp, m_g_pe, m_g_final, m_sinks, m_rel_bias]
    small_v = [v_g_mix, v_g_mlp, v_g_pe, v_g_final, v_sinks, v_rel_bias]
    small_shapes = [a.shape for a in small_w] + [(1,)]
    zero = jnp.zeros((1,), F32)
    packed_s = _adamw_replicated(small_all, _pack_small(small_w + [zero]), _pack_small(small_m + [zero]),
                                 _pack_small(small_v + [zero + 1.0]), "adamw_replicated")
    sg, sd, sm, sv = [_unpack_small(t, small_shapes) for t in packed_s]
    loss = sg[-1][0]

    return (loss, grad_x[None], *grad_w, *sg[:-1], *delta_w, *sd[:-1], *new_m, *sm[:-1], *new_v, *sv[:-1])
```

```python
import functools
import math

import numpy as np
import jax
import jax.numpy as jnp
from jax import lax
from jax.experimental import pallas as pl
from jax.experimental.pallas import tpu as pltpu

F32 = jnp.float32
MXU = jnp.bfloat16
WIRE = jnp.bfloat16

HEAD_DIM = 64
SB_HEADS = 8
SW_HEADS = 8
SW_KV = 2
SW_GROUP = SW_HEADS // SW_KV
BLOCK = 128
N_BUCKETS = 32
MAX_DISTANCE = 128
EPS = 1e-6
SCALE = HEAD_DIM ** -0.5
SB_W = SB_HEADS * HEAD_DIM
SW_W = SW_HEADS * HEAD_DIM
QKV_COLS = 3 * SB_W + SW_W + 2 * SW_KV * HEAD_DIM
N_DEV = 8
LANES = 128
N_PAIR = SB_HEADS // 2
NEG = -1e30

ADAM_LR = 0.001
ADAM_B1 = 0.9
ADAM_B2 = 0.999
ADAM_EPS = 1e-08
ADAM_WD = 0.01
ADAM_STEP = 10

VMEM_LIMIT = 48 * 1024 * 1024
SB_TQ = 256
SB_DEAD = -105.0
SB_SUB = 4

WEIGHTS = ("w_in", "w_up_a", "w_up_b", "w_o", "w_ff1", "w_ff2", "w_pe", "w_pg")
COL_SHARDED = ("w_in", "w_up_a", "w_up_b", "w_ff1", "w_pe")
SMALL = ("g_mix", "g_mlp", "g_pe", "g_final", "sinks", "rel_bias")


def _cparams(**kw):
    return pltpu.CompilerParams(vmem_limit_bytes=VMEM_LIMIT, **kw)


def _dot(a, b):
    return jnp.dot(a, b, preferred_element_type=F32)


def _dot_nt(a, b):
    return lax.dot_general(a, b, (((1,), (1,)), ((), ())), preferred_element_type=F32)


def _dot_tn(a, b):
    return lax.dot_general(a, b, (((0,), (0,)), ((), ())), preferred_element_type=F32)


def _tile(n, target, unit=LANES):
    if n <= target:
        return n
    t = (target // unit) * unit
    while t > unit and n % t:
        t -= unit
    assert n % t == 0, (n, target)
    return t


def _sigmoid(x):
    return 0.5 * jnp.tanh(0.5 * x) + 0.5


class _Comm:
    def __init__(self, inputs, out_shapes, sems, start, finish):
        self.inputs, self.out_shapes, self.sems = list(inputs), list(out_shapes), list(sems)
        self.start, self.finish = start, finish


def _call(body, *, grid, in_specs, out_specs, out_shape, scratch_shapes=(), args, name, comm=None):
    n_in, n_out, n_scr = len(in_specs), len(out_shape), len(scratch_shapes)
    if comm is None:
        outs = pl.pallas_call(body, grid=grid, in_specs=list(in_specs), out_specs=list(out_specs),
                              out_shape=list(out_shape), scratch_shapes=list(scratch_shapes),
                              compiler_params=_cparams(), name=name)(*args)
        return list(outs), None
    ci, co = len(comm.inputs), len(comm.out_shapes)
    any_spec = pl.BlockSpec(memory_space=pl.ANY)

    def wrapped(*refs):
        ins, cin = refs[:n_in], refs[n_in:n_in + ci]
        o0 = n_in + ci
        outs, cout = refs[o0:o0 + n_out], refs[o0 + n_out:o0 + n_out + co]
        s0 = o0 + n_out + co
        scr, csem = refs[s0:s0 + n_scr], refs[s0 + n_scr:]
        ids = [pl.program_id(d) for d in range(len(grid))]
        first = functools.reduce(jnp.logical_and, [i == 0 for i in ids])
        last = functools.reduce(jnp.logical_and, [i == g - 1 for i, g in zip(ids, grid)])
        pos = (lax.axis_index("x"), lax.axis_index("y"), lax.axis_index("c"))

        @pl.when(first)
        def _():
            comm.start(pos, cin, cout, csem)

        body(*ins, *outs, *scr)

        @pl.when(last)
        def _():
            comm.finish(pos, cin, cout, csem)

    outs = pl.pallas_call(wrapped, grid=grid, in_specs=list(in_specs) + [any_spec] * ci,
                          out_specs=list(out_specs) + [any_spec] * co,
                          out_shape=list(out_shape) + comm.out_shapes,
                          scratch_shapes=list(scratch_shapes) + comm.sems,
                          compiler_params=_cparams(), name=name)(*args, *comm.inputs)
    return list(outs[:n_out]), list(outs[n_out:])


def _accumulate(o_ref, value, first):
    @pl.when(first)
    def _():
        o_ref[...] = value

    @pl.when(jnp.logical_not(first))
    def _():
        o_ref[...] += value


def _mm(a, b, *, ta=False, tb=False, extras=(), epi=None, out_dtypes=(F32,),
        tm=1024, tn=1024, tk=1024, name, comm=None):
    if ta:
        kdim, m = a.shape
    else:
        m, kdim = a.shape
    n = b.shape[0] if tb else b.shape[1]
    assert (b.shape[1] if tb else b.shape[0]) == kdim
    tm, tn, tk = _tile(m, tm), _tile(n, tn), _tile(kdim, tk)
    nk = kdim // tk
    n_ex, n_out = len(extras), len(out_dtypes)

    a_spec = (pl.BlockSpec((tk, tm), lambda i, j, k: (k, i)) if ta
              else pl.BlockSpec((tm, tk), lambda i, j, k: (i, k)))
    b_spec = (pl.BlockSpec((tn, tk), lambda i, j, k: (j, k)) if tb
              else pl.BlockSpec((tk, tn), lambda i, j, k: (k, j)))
    ex_specs = []
    for e in extras:
        assert e.shape in ((m, n), (1, n), (m, 1)), (e.shape, m, n)
        if e.shape == (m, n):
            ex_specs.append(pl.BlockSpec((tm, tn), lambda i, j, k: (i, j)))
        elif e.shape[0] == 1:
            ex_specs.append(pl.BlockSpec((1, tn), lambda i, j, k: (0, j)))
        else:
            ex_specs.append(pl.BlockSpec((tm, 1), lambda i, j, k: (i, 0)))
    out_specs, out_shape, row_sums = [], [], []
    for dt in out_dtypes:
        kind = dt[1] if isinstance(dt, tuple) else "tile"
        row_sums.append(kind == "rowsum")
        if kind == "tile":
            out_specs.append(pl.BlockSpec((tm, tn), lambda i, j, k: (i, j)))
            out_shape.append(jax.ShapeDtypeStruct((m, n), dt))
            continue
        assert tn == n, "per-row and summed outputs need whole rows in one tile"
        if kind == "col":
            out_specs.append(pl.BlockSpec((tm, 1), lambda i, j, k: (i, 0)))
            out_shape.append(jax.ShapeDtypeStruct((m, 1), dt[0]))
        else:
            out_specs.append(pl.BlockSpec((1, tn), lambda i, j, k: (0, 0)))
            out_shape.append(jax.ShapeDtypeStruct((1, n), dt[0]))

    def body(a_ref, b_ref, *rest):
        ex_refs = rest[:n_ex]
        out_refs = rest[n_ex:n_ex + n_out]
        acc = rest[-1]
        k = pl.program_id(2)
        first_rows = pl.program_id(0) == 0

        def prod():
            av = a_ref[...].astype(MXU)
            bv = b_ref[...].astype(MXU)
            return _dot_tn(av, bv) if ta else (_dot_nt(av, bv) if tb else _dot(av, bv))

        def finish(res):
            if epi is not None:
                res = epi(res, *[e[...] for e in ex_refs])
            if not isinstance(res, tuple):
                res = (res,)
            for o_ref, r, summed in zip(out_refs, res, row_sums):
                if summed:
                    _accumulate(o_ref, r.astype(o_ref.dtype), first_rows)
                else:
                    o_ref[...] = r.astype(o_ref.dtype)

        if nk == 1:
            finish(prod())
            return

        @pl.when(k == 0)
        def _():
            acc[...] = prod()

        @pl.when(jnp.logical_and(k > 0, k < nk - 1))
        def _():
            acc[...] += prod()

        @pl.when(k == nk - 1)
        def _():
            finish(acc[...] + prod())

    outs, couts = _call(
        body,
        grid=(m // tm, n // tn, nk),
        in_specs=[a_spec, b_spec] + ex_specs,
        out_specs=out_specs,
        out_shape=out_shape,
        scratch_shapes=[pltpu.VMEM((tm, tn), F32)],
        args=(a, b, *extras), name=name, comm=comm)
    res = outs[0] if n_out == 1 else tuple(outs)
    return res if comm is None else (res, couts)


def _rms_fwd(x, g, name, comm=None):
    s, d = x.shape
    tr = _tile(s, 256)

    def body(x_ref, g_ref, h_ref, r_ref):
        xf = x_ref[...]
        r = lax.rsqrt(jnp.mean(xf * xf, axis=-1, keepdims=True) + EPS)
        h_ref[...] = ((xf * r) * g_ref[...]).astype(h_ref.dtype)
        r_ref[...] = r

    outs, couts = _call(
        body,
        grid=(s // tr,),
        in_specs=[pl.BlockSpec((tr, d), lambda i: (i, 0)), pl.BlockSpec((1, d), lambda i: (0, 0))],
        out_specs=[pl.BlockSpec((tr, d), lambda i: (i, 0)), pl.BlockSpec((tr, 1), lambda i: (i, 0))],
        out_shape=[jax.ShapeDtypeStruct((s, d), MXU), jax.ShapeDtypeStruct((s, 1), F32)],
        args=(x, g), name=name, comm=comm)
    return tuple(outs) if comm is None else (tuple(outs), couts)


def _loss_head(x, g, target, name):
    s, d = x.shape
    tr = _tile(s, 256)

    def body(x_ref, g_ref, t_ref, loss_ref, dx_ref, dg_ref):
        @pl.when(pl.program_id(0) == 0)
        def _():
            dg_ref[...] = jnp.zeros_like(dg_ref)
            loss_ref[...] = jnp.zeros_like(loss_ref)

        xf = x_ref[...]
        gv = g_ref[...]
        r = lax.rsqrt(jnp.mean(xf * xf, axis=-1, keepdims=True) + EPS)
        xhat = xf * r
        err = xhat * gv - t_ref[...]
        loss_ref[...] += 0.5 * jnp.sum(jnp.mean(err * err, axis=-1, keepdims=True), axis=0, keepdims=True)
        dy = err * (1.0 / d)
        dxhat = dy * gv
        mean = jnp.mean(dxhat * xhat, axis=-1, keepdims=True)
        dx_ref[...] = r * (dxhat - xhat * mean)
        dg_ref[...] += jnp.sum(dy * xhat, axis=0, keepdims=True)

    row = pl.BlockSpec((tr, d), lambda i: (i, 0))
    vec = pl.BlockSpec((1, d), lambda i: (0, 0))
    return pl.pallas_call(
        body,
        grid=(s // tr,),
        in_specs=[row, vec, row],
        out_specs=[pl.BlockSpec((1, LANES), lambda i: (0, 0)), row, vec],
        out_shape=[jax.ShapeDtypeStruct((1, LANES), F32), jax.ShapeDtypeStruct((s, d), F32),
                   jax.ShapeDtypeStruct((1, d), F32)],
        compiler_params=_cparams(),
        name=name,
    )(x, g, target)


def _mix_fwd(oa, ob, wa_t, wb_t, gates, name):
    s, kd = oa.shape
    d = wa_t.shape[0]
    tm, tn = _tile(s, 1024), _tile(d, 512)
    nj = d // tn

    def body(oa_ref, ob_ref, wa_ref, wb_ref, ga_ref, gb_ref, out_ref):
        ya = _dot_nt(oa_ref[...], wa_ref[...])
        yb = _dot_nt(ob_ref[...], wb_ref[...])
        out_ref[...] = (_sigmoid(ga_ref[...].astype(F32)) * ya
                        + _sigmoid(gb_ref[...].astype(F32)) * yb).astype(out_ref.dtype)

    o_spec = pl.BlockSpec((tm, kd), lambda i, j: (i, 0))
    w_spec = pl.BlockSpec((tn, kd), lambda i, j: (j, 0))
    return pl.pallas_call(
        body,
        grid=(s // tm, nj),
        in_specs=[o_spec, o_spec, w_spec, w_spec,
                  pl.BlockSpec((tm, tn), lambda i, j: (i, j)),
                  pl.BlockSpec((tm, tn), lambda i, j: (i, j + nj))],
        out_specs=pl.BlockSpec((tm, tn), lambda i, j: (i, j)),
        out_shape=jax.ShapeDtypeStruct((s, d), MXU),
        compiler_params=_cparams(),
        name=name,
    )(oa, ob, wa_t, wb_t, gates, gates)


def _mix_bwd(dx, w_o, oa, ob, wa_t, wb_t, gates, name):
    s, kd = oa.shape
    d = wa_t.shape[0]
    tm, tn = _tile(s, 1024), _tile(d, 512)
    nj = d // tn

    def body(dx_ref, wo_ref, oa_ref, ob_ref, wa_ref, wb_ref, ga_ref, gb_ref,
             dya_ref, dyb_ref, dga_ref, dgb_ref):
        dm = _dot_nt(dx_ref[...], wo_ref[...])
        ya = _dot_nt(oa_ref[...], wa_ref[...])
        yb = _dot_nt(ob_ref[...], wb_ref[...])
        sa = _sigmoid(ga_ref[...].astype(F32))
        sb = _sigmoid(gb_ref[...].astype(F32))
        dya_ref[...] = (dm * sa).astype(dya_ref.dtype)
        dyb_ref[...] = (dm * sb).astype(dyb_ref.dtype)
        dga_ref[...] = (dm * ya * sa * (1.0 - sa)).astype(dga_ref.dtype)
        dgb_ref[...] = (dm * yb * sb * (1.0 - sb)).astype(dgb_ref.dtype)

    o_spec = pl.BlockSpec((tm, kd), lambda i, j: (i, 0))
    w_spec = pl.BlockSpec((tn, kd), lambda i, j: (j, 0))
    t_spec = pl.BlockSpec((tm, tn), lambda i, j: (i, j))
    return pl.pallas_call(
        body,
        grid=(s // tm, nj),
        in_specs=[pl.BlockSpec((tm, d), lambda i, j: (i, 0)),
                  pl.BlockSpec((tn, d), lambda i, j: (j, 0)),
                  o_spec, o_spec, w_spec, w_spec, t_spec,
                  pl.BlockSpec((tm, tn), lambda i, j: (i, j + nj))],
        out_specs=[t_spec] * 4,
        out_shape=[jax.ShapeDtypeStruct((s, d), MXU)] * 4,
        compiler_params=_cparams(),
        name=name,
    )(dx, w_o, oa, ob, wa_t, wb_t, gates, gates)


def _ple(p, w_pe_t, h, w_pg, other, *, backward, name):
    s, kp = p.shape
    d = w_pe_t.shape[0]
    tm, tn = _tile(s, 1024), _tile(d, 512)

    def body(p_ref, wpe_ref, h_ref, wpg_ref, other_ref, *out_refs):
        pe = _dot_nt(p_ref[...].astype(MXU), wpe_ref[...])
        gt = _dot(h_ref[...], wpg_ref[...])
        sg = _sigmoid(gt)
        if backward:
            dout = other_ref[...]
            out_refs[0][...] = (dout * sg).astype(out_refs[0].dtype)
            out_refs[1][...] = (dout * pe * sg * (1.0 - sg)).astype(out_refs[1].dtype)
        else:
            out_refs[0][...] = other_ref[...] + pe * sg

    t_spec = pl.BlockSpec((tm, tn), lambda i, j: (i, j))
    if backward:
        out_specs, out_shape = [t_spec, t_spec], [jax.ShapeDtypeStruct((s, d), MXU)] * 2
    else:
        out_specs, out_shape = [t_spec], [jax.ShapeDtypeStruct((s, d), F32)]
    outs = pl.pallas_call(
        body,
        grid=(s // tm, d // tn),
        in_specs=[pl.BlockSpec((tm, kp), lambda i, j: (i, 0)),
                  pl.BlockSpec((tn, kp), lambda i, j: (j, 0)),
                  pl.BlockSpec((tm, d), lambda i, j: (i, 0)),
                  pl.BlockSpec((d, tn), lambda i, j: (0, j)),
                  t_spec],
        out_specs=out_specs,
        out_shape=out_shape,
        compiler_params=_cparams(),
        name=name,
    )(p, w_pe_t, h, w_pg, other)
    return tuple(outs) if backward else outs[0]


def _split_dot(x, tri):
    hi = x.astype(jnp.bfloat16)
    lo = (x - hi.astype(F32)).astype(jnp.bfloat16)
    return _dot(hi, tri) + _dot(lo, tri)


def _log_sigmoids(z):
    lb = jnp.minimum(z, 0.0) - jnp.log(1.0 + jnp.exp(-jnp.abs(z)))
    return lb, lb - z


def _head_lanes(hh):
    lane = lax.broadcasted_iota(jnp.int32, (1, LANES), 1)
    return jnp.logical_and(lane >= hh * HEAD_DIM, lane < (hh + 1) * HEAD_DIM)


def _sb_fwd(qkv, name, comm=None):
    s = qkv.shape[0]
    tq = _tile(s, SB_TQ)
    nsub = SB_SUB if (s // tq) % SB_SUB == 0 else 1

    def body(q_ref, k_ref, v_ref, o_ref):
        row = lax.broadcasted_iota(jnp.int32, (tq, tq), 0)
        col = lax.broadcasted_iota(jnp.int32, (tq, tq), 1)
        causal = col < row
        tri = jnp.where(row > col, 1.0, 0.0).astype(jnp.bfloat16)
        started = [_sb_fwd_straight(q_ref, k_ref, v_ref, pl.program_id(1) * nsub + sub, sub, tq, causal, tri)
                   for sub in range(nsub)]
        for sub, (block, i, cs, accs) in enumerate(started):
            def top(cs):
                return jnp.maximum(jnp.max(cs[0]), jnp.max(cs[1]))

            def live(st):
                return jnp.logical_and(st[0] >= 0, st[1] > SB_DEAD)

            def walk(st, block=block):
                cs, accs = block(st[0], st[2], st[3], False)
                return st[0] - 1, top(cs), cs, accs

            accs = lax.while_loop(live, walk, (i - 2, top(cs), cs, accs))[3]
            o_ref[sub * tq:(sub + 1) * tq, :] = jnp.where(_head_lanes(0), accs[0], accs[1]).astype(o_ref.dtype)

    outs, couts = _call(
        body,
        grid=(N_PAIR, s // (nsub * tq)),
        in_specs=[pl.BlockSpec((nsub * tq, LANES), lambda p, i: (i, p)),
                  pl.BlockSpec((s, LANES), lambda p, i: (0, N_PAIR + p)),
                  pl.BlockSpec((s, LANES), lambda p, i: (0, 2 * N_PAIR + p))],
        out_specs=[pl.BlockSpec((nsub * tq, LANES), lambda p, i: (i, p))],
        out_shape=[jax.ShapeDtypeStruct((s, SB_W), MXU)],
        args=(qkv, qkv, qkv), name=name, comm=comm)
    return outs[0] if comm is None else (outs[0], couts)


def _sb_fwd_straight(q_ref, k_ref, v_ref, i, sub, tq, causal, tri):
    qf = q_ref[sub * tq:(sub + 1) * tq, :].astype(F32) * SCALE
    qms = [jnp.where(_head_lanes(hh), qf, 0.0).astype(MXU) for hh in range(2)]

    def block(kb, cs, accs, masked, gate=None):
        rows = pl.ds(pl.multiple_of(kb * tq, tq), tq)
        ks, vs = k_ref[rows, :], v_ref[rows, :]
        new_c, new_acc = [], []
        for hh in range(2):
            lb, lm = _log_sigmoids(_dot_nt(qms[hh], ks))
            if masked:
                lm = jnp.where(causal, lm, 0.0)
            if gate is not None:
                lm = lm * gate
            a = jnp.exp(lb + _split_dot(lm, tri) + cs[hh])
            if masked:
                a = jnp.where(causal, a, 0.0)
            if gate is not None:
                a = a * gate
            new_acc.append(accs[hh] + _dot(a.astype(MXU), vs))
            new_c.append(cs[hh] + jnp.sum(lm, axis=1, keepdims=True))
        return tuple(new_c), tuple(new_acc)

    zc, za = jnp.zeros((tq, 1), F32), jnp.zeros((tq, LANES), F32)
    cs, accs = block(i, (zc, zc), (za, za), True)
    cs, accs = block(jnp.maximum(i - 1, 0), cs, accs, False, jnp.where(i > 0, 1.0, 0.0))
    return block, i, cs, accs


def _sb_bwd(qkv, do, name, comm=None):
    s = qkv.shape[0]
    tq = _tile(s, SB_TQ)
    nq = s // tq
    nsub = SB_SUB if nq % SB_SUB == 0 else 1
    nsteps = nq // nsub

    def body(q_ref, k_ref, v_ref, do_ref, dq_ref, dk_ref, dv_ref, dk_acc, dv_acc, carries):
        step = pl.program_id(1)

        @pl.when(step == 0)
        def _():
            dk_acc[...] = jnp.zeros_like(dk_acc)
            dv_acc[...] = jnp.zeros_like(dv_acc)

        row = lax.broadcasted_iota(jnp.int32, (tq, tq), 0)
        col = lax.broadcasted_iota(jnp.int32, (tq, tq), 1)
        causal = col < row
        tri_rev = jnp.where(row > col, 1.0, 0.0).astype(jnp.bfloat16)
        tri_excl = jnp.where(row < col, 1.0, 0.0).astype(jnp.bfloat16)
        zc, za = jnp.zeros((tq, 1), F32), jnp.zeros((tq, LANES), F32)

        def top(cs):
            return jnp.maximum(jnp.max(cs[0]), jnp.max(cs[1]))

        def live(st):
            return jnp.logical_and(st[0] >= 0, st[1] > SB_DEAD)

        def row_sums(pre):
            return [jnp.sum(lm, axis=1, keepdims=True) for _, lm in pre]

        def query_block(sub):
            i = step * nsub + sub
            q_rows = slice(sub * tq, (sub + 1) * tq)
            qf = q_ref[q_rows, :].astype(F32) * SCALE
            dof = do_ref[q_rows, :]
            qms = [jnp.where(_head_lanes(hh), qf, 0.0).astype(MXU) for hh in range(2)]
            doms = [jnp.where(_head_lanes(hh), dof, jnp.zeros_like(dof)) for hh in range(2)]

            def terms(kb, masked):
                rows = pl.ds(pl.multiple_of(kb * tq, tq), tq)
                ks = k_ref[rows, :]
                out = []
                for hh in range(2):
                    lb, lm = _log_sigmoids(_dot_nt(qms[hh], ks))
                    if masked:
                        lm = jnp.where(causal, lm, 0.0)
                    out.append((lb, lm))
                return out

            def block(kb, cs, gpres, dqs, masked, gate=None, pre=None):
                rows = pl.ds(pl.multiple_of(kb * tq, tq), tq)
                ks, vs = k_ref[rows, :], v_ref[rows, :]
                pre = terms(kb, masked) if pre is None else pre
                new_g, new_dq = [], []
                dk_add, dv_add = None, None
                for hh in range(2):
                    lb, lm = pre[hh]
                    a = jnp.exp(lb + _split_dot(lm, tri_rev) + cs[hh])
                    if masked:
                        a = jnp.where(causal, a, 0.0)
                    if gate is not None:
                        a = a * gate
                    g = a * _dot_nt(doms[hh], vs)
                    gsum = gpres[hh] + _split_dot(g, tri_excl)
                    dz = g - (g + gsum) * jnp.exp(lb)
                    if masked:
                        dz = jnp.where(causal, dz, 0.0)
                    if gate is not None:
                        dz = dz * gate
                    dzb = dz.astype(MXU)
                    new_dq.append(dqs[hh] + _dot(dzb, ks))
                    dk_h = _dot_tn(dzb, qms[hh])
                    dv_h = _dot_tn(a.astype(MXU), doms[hh])
                    dk_add = dk_h if dk_add is None else dk_add + dk_h
                    dv_add = dv_h if dv_add is None else dv_add + dv_h
                    new_g.append(gpres[hh] + jnp.sum(g, axis=1, keepdims=True))
                dk_acc[rows, :] += dk_add
                dv_acc[rows, :] += dv_add
                return tuple(new_g), tuple(new_dq)

            prev = jnp.maximum(i - 1, 0)
            gate = jnp.where(i > 0, 1.0, 0.0)
            t_diag, t_prev = terms(i, True), terms(prev, False)
            c_diag = row_sums(t_diag)
            sums = row_sums(t_prev)
            c_prev = tuple(c_diag[hh] + sums[hh] * gate for hh in range(2))
            return dict(i=i, prev=prev, gate=gate, q_rows=q_rows, terms=terms, block=block,
                        t_diag=t_diag, t_prev=t_prev, c_diag=c_diag, c_prev=c_prev)

        blocks = [query_block(sub) for sub in range(nsub)]
        for qb in blocks:
            def record(st, qb=qb):
                kb, cs = st[0], st[2]
                sums = row_sums(qb["terms"](kb, False))
                for hh in range(2):
                    carries[hh, kb] = cs[hh]
                cs = tuple(cs[hh] + sums[hh] for hh in range(2))
                return kb - 1, top(cs), cs

            first = lax.while_loop(live, record, (qb["i"] - 2, top(qb["c_prev"]), qb["c_prev"]))[0] + 1
            qb["mid"] = lax.fori_loop(
                first, qb["i"] - 1,
                lambda kb, cr, qb=qb: qb["block"](kb, (carries[0, kb], carries[1, kb]), cr[0], cr[1], False),
                ((zc, zc), (za, za)))
        for qb in blocks:
            gpres, dqs = qb["block"](qb["prev"], qb["c_diag"], *qb["mid"], False, qb["gate"], qb["t_prev"])
            dqs = qb["block"](qb["i"], (zc, zc), gpres, dqs, True, None, qb["t_diag"])[1]
            dq_ref[qb["q_rows"], :] = (jnp.where(_head_lanes(0), dqs[0], dqs[1]) * SCALE).astype(dq_ref.dtype)

        @pl.when(step == nsteps - 1)
        def _():
            dk_ref[...] = dk_acc[...].astype(dk_ref.dtype)
            dv_ref[...] = dv_acc[...].astype(dv_ref.dtype)

    blk = pl.BlockSpec((nsub * tq, LANES), lambda p, i: (i, p))
    full = pl.BlockSpec((s, LANES), lambda p, i: (0, p))
    outs, couts = _call(
        body,
        grid=(N_PAIR, nsteps),
        in_specs=[blk,
                  pl.BlockSpec((s, LANES), lambda p, i: (0, N_PAIR + p)),
                  pl.BlockSpec((s, LANES), lambda p, i: (0, 2 * N_PAIR + p)),
                  blk],
        out_specs=[blk, full, full],
        out_shape=[jax.ShapeDtypeStruct((s, SB_W), MXU)] * 3,
        scratch_shapes=[pltpu.VMEM((s, LANES), F32), pltpu.VMEM((s, LANES), F32),
                        pltpu.VMEM((2, nq, tq, 1), F32)],
        args=(qkv, qkv, qkv, do), name=name, comm=comm)
    return tuple(outs) if comm is None else (tuple(outs), couts)


def _bucket_table():
    i = np.arange(BLOCK)[:, None]
    j = np.arange(2 * BLOCK)[None, :]
    d = np.maximum(BLOCK + i - j, 0)
    max_exact = N_BUCKETS // 2
    df = np.maximum(d, 1).astype(np.float32)
    large = max_exact + (np.log(df / max_exact) / math.log(MAX_DISTANCE / max_exact)
                         * (N_BUCKETS - max_exact)).astype(np.int32)
    large = np.minimum(large, N_BUCKETS - 1)
    return np.where(d < max_exact, d, large).astype(np.int32)


def _build_bias(rel_bias, buckets, name):
    def body(rb_ref, bk_ref, out_ref):
        h = pl.program_id(0)
        bk = bk_ref[...]
        acc = jnp.zeros(bk.shape, F32)
        for b in range(N_BUCKETS):
            acc = jnp.where(bk == b, rb_ref[b, h], acc)
        out_ref[...] = acc

    return pl.pallas_call(
        body,
        grid=(SW_HEADS,),
        in_specs=[pl.BlockSpec(memory_space=pltpu.SMEM),
                  pl.BlockSpec((BLOCK, 2 * BLOCK), lambda h: (0, 0))],
        out_specs=pl.BlockSpec((None, BLOCK, 2 * BLOCK), lambda h: (h, 0, 0)),
        out_shape=jax.ShapeDtypeStruct((SW_HEADS, BLOCK, 2 * BLOCK), F32),
        name=name,
    )(rel_bias, buckets)


def _bias_grad(dbias_layers, buckets, name):
    n_l = len(dbias_layers)

    def body(*refs):
        bk = refs[n_l][...]
        out_ref = refs[n_l + 1]
        db = refs[0][...]
        for r in refs[1:n_l]:
            db = db + r[...]
        lane = lax.broadcasted_iota(jnp.int32, (1, LANES), 1)
        acc = jnp.zeros((1, LANES), F32)
        for b in range(N_BUCKETS):
            part = jnp.sum(jnp.where(bk == b, db, 0.0), axis=1, keepdims=True)
            tot = jnp.sum(part, axis=0, keepdims=True)
            acc = jnp.where(lane == b, tot, acc)
        out_ref[...] = acc

    hspec = pl.BlockSpec((None, BLOCK, 2 * BLOCK), lambda h: (h, 0, 0))
    return pl.pallas_call(
        body,
        grid=(SW_HEADS,),
        in_specs=[hspec] * n_l + [pl.BlockSpec((BLOCK, 2 * BLOCK), lambda h: (0, 0))],
        out_specs=pl.BlockSpec((None, 1, LANES), lambda h: (h, 0, 0)),
        out_shape=jax.ShapeDtypeStruct((SW_HEADS, 1, LANES), F32),
        name=name,
    )(*dbias_layers, buckets)


GROUP_ROWS = SW_GROUP * BLOCK


def _group_lanes(g):
    lane = lax.broadcasted_iota(jnp.int32, (1, LANES), 1)
    gvec = jnp.zeros((1, LANES), jnp.int32) + g
    return jnp.where(lane >= HEAD_DIM, 1, 0) == gvec, gvec


def _stack_heads(x, g):
    kv_lanes, gvec = _group_lanes(g)
    parts = []
    for j in range(SW_GROUP):
        half = x[:, (j // 2) * LANES:(j // 2 + 1) * LANES]
        moved = jnp.where(gvec == j % 2, half, pltpu.roll(half, HEAD_DIM, 1))
        parts.append(jnp.where(kv_lanes, moved, 0.0))
    return jnp.concatenate(parts, axis=0)


def _unstack_heads(y, g):
    _, gvec = _group_lanes(g)
    heads = []
    for j in range(SW_GROUP):
        yj = y[j * BLOCK:(j + 1) * BLOCK]
        heads.append(jnp.where(gvec == j % 2, yj, pltpu.roll(yj, HEAD_DIM, 1)))
    pairs = [jnp.where(_head_lanes(0), heads[2 * p], heads[2 * p + 1]) for p in range(SW_GROUP // 2)]
    return jnp.concatenate(pairs, axis=1)


def _per_head_col(values):
    return jnp.concatenate([jnp.zeros((BLOCK, 1), F32) + v for v in values], axis=0)


def _swa_scores(qs, kp, kc, bias_ref, n):
    row = jnp.bitwise_and(lax.broadcasted_iota(jnp.int32, (GROUP_ROWS, BLOCK), 0), BLOCK - 1)
    col = lax.broadcasted_iota(jnp.int32, (GROUP_ROWS, BLOCK), 1)
    bias = bias_ref[...].reshape(GROUP_ROWS, 2 * BLOCK)
    s1 = _dot_nt(qs, kp) + bias[:, :BLOCK]
    s2 = _dot_nt(qs, kc) + bias[:, BLOCK:]
    no_prev = jnp.where(n > 0, 0, BLOCK)
    s1 = jnp.where(col > row + no_prev, s1, NEG)
    s2 = jnp.where(col <= row, s2, NEG)
    return s1, s2


def _swa_specs(s):
    q_blk = 3 * SB_W // (2 * LANES)
    k_blk = (3 * SB_W + SW_W) // LANES
    return (pl.BlockSpec((s, 2 * LANES), lambda g: (0, q_blk + g)),
            pl.BlockSpec((s, LANES), lambda g: (0, k_blk)),
            pl.BlockSpec((s, LANES), lambda g: (0, k_blk + 1)))


def _swa_fwd(qkv, bias, sinks, name, comm=None):
    s = qkv.shape[0]
    nb = s // BLOCK

    def body(sink_ref, q_ref, k_ref, v_ref, bias_ref, o_ref, lse_ref):
        g = pl.program_id(0)
        sink = _per_head_col([sink_ref[SW_GROUP * g + j] for j in range(SW_GROUP)])
        lane = lax.broadcasted_iota(jnp.int32, (1, LANES), 1)

        def step(n, carry):
            r0 = pl.multiple_of(n * BLOCK, BLOCK)
            p0 = pl.multiple_of(jnp.maximum(n - 1, 0) * BLOCK, BLOCK)
            cur, prev = pl.ds(r0, BLOCK), pl.ds(p0, BLOCK)
            qs = _stack_heads(q_ref[cur, :].astype(F32) * SCALE, g).astype(MXU)
            s1, s2 = _swa_scores(qs, k_ref[prev, :], k_ref[cur, :], bias_ref, n)
            m = jnp.maximum(jnp.max(jnp.maximum(s1, s2), axis=1, keepdims=True), sink)
            e1 = jnp.exp(s1 - m)
            e2 = jnp.exp(s2 - m)
            den = jnp.sum(e1 + e2, axis=1, keepdims=True) + jnp.exp(sink - m)
            o = _dot((e1 / den).astype(MXU), v_ref[prev, :]) + _dot((e2 / den).astype(MXU), v_ref[cur, :])
            o_ref[cur, :] = _unstack_heads(o, g).astype(o_ref.dtype)
            lse = m + jnp.log(den)
            lse_row = jnp.zeros((BLOCK, LANES), F32)
            for j in range(SW_GROUP):
                lse_row = jnp.where(lane == j, lse[j * BLOCK:(j + 1) * BLOCK], lse_row)
            lse_ref[cur, :] = lse_row
            return carry

        lax.fori_loop(0, nb, step, 0, unroll=2)

    outs, couts = _call(
        body,
        grid=(SW_KV,),
        in_specs=[pl.BlockSpec(memory_space=pltpu.SMEM), *_swa_specs(s),
                  pl.BlockSpec((SW_GROUP, BLOCK, 2 * BLOCK), lambda g: (g, 0, 0))],
        out_specs=[pl.BlockSpec((s, 2 * LANES), lambda g: (0, g)),
                   pl.BlockSpec((None, s, LANES), lambda g: (g, 0, 0))],
        out_shape=[jax.ShapeDtypeStruct((s, SW_W), MXU), jax.ShapeDtypeStruct((SW_KV, s, LANES), F32)],
        args=(sinks, qkv, qkv, qkv, bias), name=name, comm=comm)
    return tuple(outs) if comm is None else (tuple(outs), couts)


def _swa_bwd(qkv, bias, sinks, do, lse, name, comm=None):
    s = qkv.shape[0]
    nb = s // BLOCK

    def body(sink_ref, q_ref, k_ref, v_ref, bias_ref, do_ref, lse_ref,
             dq_ref, dk_ref, dv_ref, dbias_ref, dsink_ref, dk_acc, dv_acc):
        g = pl.program_id(0)
        sink = _per_head_col([sink_ref[SW_GROUP * g + j] for j in range(SW_GROUP)])
        lane = lax.broadcasted_iota(jnp.int32, (1, LANES), 1)

        @pl.when(g == 0)
        def _():
            dk_acc[...] = jnp.zeros_like(dk_acc)
            dv_acc[...] = jnp.zeros_like(dv_acc)

        dbias_ref[...] = jnp.zeros_like(dbias_ref)

        def step(n, dsink_rows):
            r0 = pl.multiple_of(n * BLOCK, BLOCK)
            p0 = pl.multiple_of(jnp.maximum(n - 1, 0) * BLOCK, BLOCK)
            cur, prev = pl.ds(r0, BLOCK), pl.ds(p0, BLOCK)
            qs = _stack_heads(q_ref[cur, :].astype(F32) * SCALE, g).astype(MXU)
            dos = _stack_heads(do_ref[cur, :].astype(F32), g).astype(MXU)
            kp, kc, vp, vc = k_ref[prev, :], k_ref[cur, :], v_ref[prev, :], v_ref[cur, :]
            lse_row = lse_ref[cur, :]
            lse = jnp.concatenate([jnp.sum(jnp.where(lane == j, lse_row, 0.0), axis=1, keepdims=True)
                                   for j in range(SW_GROUP)], axis=0)
            s1, s2 = _swa_scores(qs, kp, kc, bias_ref, n)
            pr1 = jnp.exp(s1 - lse)
            pr2 = jnp.exp(s2 - lse)
            dpr1 = _dot_nt(dos, vp)
            dpr2 = _dot_nt(dos, vc)
            delta = jnp.sum(pr1 * dpr1 + pr2 * dpr2, axis=1, keepdims=True)
            ds1 = pr1 * (dpr1 - delta)
            ds2 = pr2 * (dpr2 - delta)
            dbias_ref[:, :, :BLOCK] += ds1.reshape(SW_GROUP, BLOCK, BLOCK)
            dbias_ref[:, :, BLOCK:] += ds2.reshape(SW_GROUP, BLOCK, BLOCK)
            ds1b, ds2b = ds1.astype(MXU), ds2.astype(MXU)
            dq = _dot(ds1b, kp) + _dot(ds2b, kc)
            dq_ref[cur, :] = (_unstack_heads(dq, g) * SCALE).astype(dq_ref.dtype)
            dk_acc[prev, :] += _dot_tn(ds1b, qs)
            dk_acc[cur, :] += _dot_tn(ds2b, qs)
            dv_acc[prev, :] += _dot_tn(pr1.astype(MXU), dos)
            dv_acc[cur, :] += _dot_tn(pr2.astype(MXU), dos)
            return dsink_rows - jnp.exp(sink - lse) * delta

        rows = lax.fori_loop(0, nb, step, jnp.zeros((GROUP_ROWS, 1), F32), unroll=2)
        for j in range(SW_GROUP):
            dsink_ref[j] = jnp.broadcast_to(jnp.sum(rows[j * BLOCK:(j + 1) * BLOCK], axis=0, keepdims=True),
                                            (1, LANES))

        @pl.when(g == SW_KV - 1)
        def _():
            dk_ref[...] = dk_acc[...].astype(dk_ref.dtype)
            dv_ref[...] = dv_acc[...].astype(dv_ref.dtype)

    grp = pl.BlockSpec((s, 2 * LANES), lambda g: (0, g))
    kv_out = pl.BlockSpec((s, LANES), lambda g: (0, 0))
    bspec = pl.BlockSpec((SW_GROUP, BLOCK, 2 * BLOCK), lambda g: (g, 0, 0))
    outs, couts = _call(
        body,
        grid=(SW_KV,),
        in_specs=[pl.BlockSpec(memory_space=pltpu.SMEM), *_swa_specs(s), bspec, grp,
                  pl.BlockSpec((None, s, LANES), lambda g: (g, 0, 0))],
        out_specs=[grp, kv_out, kv_out, bspec, pl.BlockSpec((SW_GROUP, 1, LANES), lambda g: (g, 0, 0))],
        out_shape=[jax.ShapeDtypeStruct((s, SW_W), MXU),
                   jax.ShapeDtypeStruct((s, LANES), MXU),
                   jax.ShapeDtypeStruct((s, LANES), MXU),
                   jax.ShapeDtypeStruct((SW_HEADS, BLOCK, 2 * BLOCK), F32),
                   jax.ShapeDtypeStruct((SW_HEADS, 1, LANES), F32)],
        scratch_shapes=[pltpu.VMEM((s, LANES), F32), pltpu.VMEM((s, LANES), F32)],
        args=(sinks, qkv, qkv, qkv, bias, do, lse), name=name, comm=comm)
    return tuple(outs) if comm is None else (tuple(outs), couts)


class _NoPlan:
    def comm(self, name):
        return None

    def done(self, name, outs):
        pass

    def grad(self, layer, name, value):
        pass


def _run(plan, fn, *args, name, **kw):
    comm = plan.comm(name)
    if comm is None:
        return fn(*args, name=name, **kw)
    res, outs = fn(*args, name=name, comm=comm, **kw)
    plan.done(name, outs)
    return res


def _norm_bwd(dh, x, dres, r, g):
    xhat = x * r
    dxhat = dh * g
    dx = dres + r * (dxhat - xhat * jnp.mean(dxhat * xhat, axis=-1, keepdims=True))
    return dx, dx, jnp.sum(dh * xhat, axis=0, keepdims=True)


def _residual_norm(acc, res, g):
    x = res + acc
    r = lax.rsqrt(jnp.mean(x * x, axis=-1, keepdims=True) + EPS)
    return x, (x * r) * g, r


def _layer_fwd(x, p, w, g_mix, g_mlp, g_pe, sinks, bias, tag, plan):
    h1, r1 = _run(plan, _rms_fwd, x, g_mix, name=f"rms_mix_{tag}")
    qkv = _run(plan, _mm, h1, w["w_qkv_t"], tb=True, out_dtypes=(MXU,), tn=QKV_COLS, name=f"proj_qkv_{tag}")
    gates = _run(plan, _mm, h1, w["w_gate_t"], tb=True, out_dtypes=(MXU,), tn=2048, name=f"proj_gate_{tag}")
    oa = _run(plan, _sb_fwd, qkv, name=f"sb_fwd_{tag}")
    ob, lse = _run(plan, _swa_fwd, qkv, bias, sinks, name=f"swa_fwd_{tag}")
    merged = _mix_fwd(oa, ob, w["w_up_a_t"], w["w_up_b_t"], gates, f"mix_fwd_{tag}")
    d = x.shape[1]
    normed = (F32, MXU, (F32, "col"))
    x1, h2, r2 = _run(plan, _mm, merged, w["w_o"], extras=(x, g_mlp), epi=_residual_norm, out_dtypes=normed,
                      tn=d, name=f"out_proj_{tag}")
    u, act = _run(plan, _mm, h2, w["w_ff1_t"], tb=True,
                  epi=lambda acc: (acc, jnp.square(jnp.maximum(acc, 0.0))),
                  out_dtypes=(MXU, MXU), tn=2048, name=f"ff1_{tag}")
    x2, h3, r3 = _run(plan, _mm, act, w["w_ff2"], extras=(x1, g_pe), epi=_residual_norm, out_dtypes=normed,
                      tn=d, name=f"ff2_{tag}")
    x3 = _ple(p, w["w_pe_t"], h3, w["w_pg"], x2, backward=False, name=f"ple_fwd_{tag}")
    saved = dict(x=x, h1=h1, r1=r1, gates=gates, qkv=qkv, lse=lse, oa=oa, ob=ob, merged=merged,
                 x1=x1, h2=h2, r2=r2, u=u, act=act, x2=x2, h3=h3, r3=r3)
    return x3, saved


def _layer_bwd(dx3, sv, p, w, g_mix, g_mlp, g_pe, sinks, bias, layer, plan):
    tag = f"l{layer}"
    gw = {}
    wire = (WIRE,)

    def dw(name, a, b):
        gw[name] = _run(plan, _mm, a, b, ta=True, out_dtypes=wire, tk=2048, name=f"d{name}_{tag}")
        plan.grad(layer, name, gw[name])

    dpe, dgt = _ple(p, w["w_pe_t"], sv["h3"], w["w_pg"], dx3, backward=True, name=f"ple_bwd_{tag}")
    dw("w_pe", dpe, p)
    dw("w_pg", sv["h3"], dgt)
    d = dx3.shape[1]
    grads = (F32, MXU, (F32, "rowsum"))
    dx2, dx2b, dg_pe = _run(plan, _mm, dgt, w["w_pg"], tb=True, extras=(sv["x2"], dx3, sv["r3"], g_pe),
                            epi=_norm_bwd, out_dtypes=grads, tm=512, tn=d, name=f"dh_pe_{tag}")
    dw("w_ff2", sv["act"], dx2b)
    du = _run(plan, _mm, dx2b, w["w_ff2"], tb=True, extras=(sv["u"],),
              epi=lambda acc, u: acc * (2.0 * jnp.maximum(u.astype(F32), 0.0)), out_dtypes=(MXU,),
              tn=2048, name=f"dact_{tag}")
    dw("w_ff1", du, sv["h2"])
    dx1, dx1b, dg_mlp = _run(plan, _mm, du, w["w_ff1_t"], extras=(sv["x1"], dx2, sv["r2"], g_mlp),
                             epi=_norm_bwd, out_dtypes=grads, tm=1024, tn=d, name=f"dh_mlp_{tag}")
    dw("w_o", sv["merged"], dx1b)
    dya, dyb, dga, dgb = _mix_bwd(dx1b, w["w_o"], sv["oa"], sv["ob"], w["w_up_a_t"], w["w_up_b_t"],
                                  sv["gates"], f"mix_bwd_{tag}")
    dw("w_up_a", dya, sv["oa"])
    dw("w_up_b", dyb, sv["ob"])
    doa = _run(plan, _mm, dya, w["w_up_a_t"], out_dtypes=(MXU,), name=f"do_a_{tag}")
    dob = _run(plan, _mm, dyb, w["w_up_b_t"], out_dtypes=(MXU,), name=f"do_b_{tag}")
    dqb, dkb, dvb, dbias, dsink = _run(plan, _swa_bwd, sv["qkv"], bias, sinks, dob, sv["lse"],
                                       name=f"swa_bwd_{tag}")
    dqa, dka, dva = _run(plan, _sb_bwd, sv["qkv"], doa, name=f"sb_bwd_{tag}")
    dqkv = jnp.concatenate([dqa, dka, dva, dqb, dkb, dvb], axis=1)
    gw_qkv = _mm(dqkv, sv["h1"], ta=True, out_dtypes=wire, tk=2048, name=f"dw_qkv_{tag}")
    gw_ga = _mm(dga, sv["h1"], ta=True, out_dtypes=wire, tk=2048, name=f"dw_ga_{tag}")
    gw_gb = _mm(dgb, sv["h1"], ta=True, out_dtypes=wire, tk=2048, name=f"dw_gb_{tag}")
    gw["w_in"] = jnp.concatenate([gw_qkv, gw_ga, gw_gb], axis=0)
    plan.grad(layer, "w_in", gw["w_in"])
    d = dga.shape[1]
    add = lambda acc, res: res + acc
    dh1 = _run(plan, _mm, dga, w["w_gate_t"][:d], name=f"dh_ga_{tag}")
    dh1 = _run(plan, _mm, dgb, w["w_gate_t"][d:], extras=(dh1,), epi=add, name=f"dh_gb_{tag}")
    dx, _, dg_mix = _run(plan, _mm, dqkv, w["w_qkv_t"], tk=768, extras=(dh1, sv["x"], dx1, sv["r1"], g_mix),
                         epi=lambda acc, prev, *rest: _norm_bwd(acc + prev, *rest), out_dtypes=grads,
                         tm=512, tn=d, name=f"dh_qkv_{tag}")
    small = dict(g_mix=dg_mix, g_mlp=dg_mlp, g_pe=dg_pe, sinks=dsink[:, 0, 0], dbias=dbias)
    return dx, gw, small


def _local_step(x, p, target, weights, g_mix, g_mlp, g_pe, g_final, sinks, rel_bias, plan=None):
    plan = _NoPlan() if plan is None else plan
    depth = g_mix.shape[0]
    buckets = jnp.asarray(_bucket_table())
    bias = _build_bias(rel_bias, buckets, "build_bias")
    saved, wfull = [], []
    h = x
    for l in range(depth):
        wfull.append(weights(l))
        h, sv = _layer_fwd(h, p[l], wfull[l], g_mix[l:l + 1], g_mlp[l:l + 1], g_pe[l:l + 1],
                           sinks[l], bias, f"l{l}", plan)
        saved.append(sv)
    loss_row, dx, dg_final = _loss_head(h, g_final[None, :], target, "loss_head")
    gws = [None] * depth
    smalls = [None] * depth
    for l in reversed(range(depth)):
        dx, gws[l], smalls[l] = _layer_bwd(dx, saved[l], p[l], wfull[l], g_mix[l:l + 1], g_mlp[l:l + 1],
                                           g_pe[l:l + 1], sinks[l], bias, l, plan)
    drel = _bias_grad([sm["dbias"] for sm in smalls], buckets, "bias_grad")[:, 0, :N_BUCKETS].T
    small = dict(
        g_mix=jnp.concatenate([sm["g_mix"] for sm in smalls], axis=0),
        g_mlp=jnp.concatenate([sm["g_mlp"] for sm in smalls], axis=0),
        g_pe=jnp.concatenate([sm["g_pe"] for sm in smalls], axis=0),
        g_final=dg_final[0],
        sinks=jnp.stack([sm["sinks"] for sm in smalls], axis=0),
        rel_bias=drel,
    )
    return loss_row, dx, gws, small


MESH_ID = pl.DeviceIdType.MESH


def _position():
    return lax.axis_index("x"), lax.axis_index("y"), lax.axis_index("c")


def _gather_comm(shards):
    n = len(shards)

    def copies(pos, x_refs, out_refs, sems):
        send_sems, recv_sems, local_sems = sems
        x, y, c = pos
        me, sibling = (x, y, c), (x, y, 1 - c)
        chips = [(1 - x, y), (x, 1 - y), (1 - x, 1 - y)]

        def slot(a, px, py, pc):
            return out_refs[a].at[4 * px + 2 * py + pc]

        def copy(a, k, block, to, src=None):
            return pltpu.make_async_remote_copy(
                src_ref=slot(a, *block) if src is None else src, dst_ref=slot(a, *block),
                send_sem=send_sems.at[a, k], recv_sem=recv_sems.at[a, k],
                device_id=to, device_id_type=MESH_ID)

        mine = [pltpu.make_async_copy(x_refs[a], slot(a, *me), local_sems.at[a]) for a in range(n)]
        first = []
        for a in range(n):
            first.append(copy(a, 0, me, sibling, src=x_refs[a]))
            first += [copy(a, 1 + j, me, (*chip, c), src=x_refs[a]) for j, chip in enumerate(chips)]
        return me, sibling, chips, copy, mine, first

    def start(pos, x_refs, out_refs, sems):
        _, _, _, _, mine, first = copies(pos, x_refs, out_refs, sems)
        for cp in mine + first:
            cp.start()

    def finish(pos, x_refs, out_refs, sems):
        me, sibling, chips, copy, mine, first = copies(pos, x_refs, out_refs, sems)
        c = pos[2]
        passed = []
        for j, chip in enumerate(chips):
            for a in range(n):
                copy(a, 1 + j, (*chip, c), me).wait_recv()
                fwd = copy(a, 4 + j, (*chip, c), sibling)
                fwd.start()
                passed.append(fwd)
        for a in range(n):
            copy(a, 0, sibling, me).wait_recv()
            for j, chip in enumerate(chips):
                copy(a, 4 + j, (*chip, 1 - c), me).wait_recv()
        for cp in first + passed:
            cp.wait_send()
        for cp in mine:
            cp.wait()

    return _Comm(shards, [jax.ShapeDtypeStruct((N_DEV,) + s.shape, s.dtype) for s in shards],
                 [pltpu.SemaphoreType.DMA((n, 7)), pltpu.SemaphoreType.DMA((n, 7)),
                  pltpu.SemaphoreType.DMA((n,))], start, finish)


def _exchange_comm(arrays, n_slots, route):
    n = len(arrays)

    def copies(pos, in_refs, out_refs, sems):
        send_sems, recv_sems = sems
        out = []
        for a in range(n):
            for j in range(n_slots):
                src_slot, peer = route(pos, j)
                out.append(pltpu.make_async_remote_copy(
                    src_ref=in_refs[a].at[src_slot], dst_ref=out_refs[a].at[j],
                    send_sem=send_sems.at[a, j], recv_sem=recv_sems.at[a, j],
                    device_id=peer, device_id_type=MESH_ID))
        return out

    def start(pos, in_refs, out_refs, sems):
        for cp in copies(pos, in_refs, out_refs, sems):
            cp.start()

    def finish(pos, in_refs, out_refs, sems):
        for cp in copies(pos, in_refs, out_refs, sems):
            cp.wait()

    return _Comm(arrays, [jax.ShapeDtypeStruct((n_slots,) + g.shape[1:], g.dtype) for g in arrays],
                 [pltpu.SemaphoreType.DMA((n, n_slots)), pltpu.SemaphoreType.DMA((n, n_slots))], start, finish)


def _rs_sibling_comm(gs):
    return _exchange_comm(gs, 4, lambda pos, j: (2 * j + (1 - pos[2]), (pos[0], pos[1], 1 - pos[2])))


def _chip_of(k, x, y):
    return x ^ ((k + 1) & 1), y ^ (((k + 1) >> 1) & 1)


def _chip_partials(pos, gs, recvs, name):
    n = len(gs)

    def body(pos_ref, *refs):
        for a in range(n):
            refs[2 * n + a][...] = (refs[a][...].astype(F32) + refs[n + a][...].astype(F32)
                                    ).astype(refs[2 * n + a].dtype)

    def g_map(k, pos_ref):
        cx, cy = _chip_of(k, pos_ref[0], pos_ref[1])
        return (4 * cx + 2 * cy + pos_ref[2], 0, 0)

    def r_map(k, pos_ref):
        cx, cy = _chip_of(k, pos_ref[0], pos_ref[1])
        return (2 * cx + cy, 0, 0)

    slab = [(None,) + g.shape[1:] for g in gs]
    return pl.pallas_call(
        body,
        grid_spec=pltpu.PrefetchScalarGridSpec(
            num_scalar_prefetch=1,
            grid=(4,),
            in_specs=[pl.BlockSpec(sh, g_map) for sh in slab] + [pl.BlockSpec(sh, r_map) for sh in slab],
            out_specs=[pl.BlockSpec(sh, lambda k, pos_ref: (k, 0, 0)) for sh in slab],
        ),
        out_shape=[jax.ShapeDtypeStruct((4,) + g.shape[1:], g.dtype) for g in gs],
        compiler_params=_cparams(),
        name=name,
    )(pos, *gs, *recvs)


def _rs_chips_comm(parts):
    return _exchange_comm(parts, 3, lambda pos, k: (k, (*_chip_of(k, pos[0], pos[1]), pos[2])))


def _adamw_math(w, g, m, v):
    m = ADAM_B1 * m + (1.0 - ADAM_B1) * g
    v = ADAM_B2 * v + (1.0 - ADAM_B2) * (g * g)
    m_hat = m / (1.0 - ADAM_B1 ** ADAM_STEP)
    v_hat = v / (1.0 - ADAM_B2 ** ADAM_STEP)
    delta = -ADAM_LR * (m_hat / (jnp.sqrt(v_hat) + ADAM_EPS) + ADAM_WD * w)
    return delta, m, v


def _adamw_weight(parts, recvs, w, m, v, name, grad_t=False, comm=None):
    depth, a, b = w.shape
    ta = _tile(a, 288, unit=LANES if grad_t else 16)
    ni = a // ta
    g_block = (b, ta) if grad_t else (ta, b)

    def body(*refs):
        p_refs, r_refs = refs[:depth], refs[depth:2 * depth]
        w_ref, m_ref, v_ref = refs[2 * depth:2 * depth + 3]
        g_out, d_out, m_out, v_out = refs[2 * depth + 3:]
        layer = pl.program_id(0)
        g = jnp.zeros(g_block, F32)
        for l in range(depth):
            gl = p_refs[l][...].astype(F32)
            for k in range(3):
                gl = gl + r_refs[l][k].astype(F32)
            g = jnp.where(layer == l, gl, g)
        if grad_t:
            g = g.T
        delta, m_new, v_new = _adamw_math(w_ref[...], g, m_ref[...], v_ref[...])
        g_out[...] = g
        d_out[...] = delta
        m_out[...] = m_new
        v_out[...] = v_new

    def hold(l):
        return lambda layer, i: jnp.where(layer == l, i, jnp.where(layer < l, 0, ni - 1))

    def g_index(slot, f):
        if grad_t:
            return lambda layer, i: (slot, 0, f(layer, i))
        return lambda layer, i: (slot, f(layer, i), 0)

    p_specs = [pl.BlockSpec((None,) + g_block, g_index(3, hold(l))) for l in range(depth)]
    r_specs = [pl.BlockSpec((3,) + g_block, g_index(0, hold(l))) for l in range(depth)]
    row = pl.BlockSpec((None, ta, b), lambda layer, i: (layer, i, 0))
    outs, couts = _call(
        body,
        grid=(depth, ni),
        in_specs=p_specs + r_specs + [row, row, row],
        out_specs=[row] * 4,
        out_shape=[jax.ShapeDtypeStruct(w.shape, F32)] * 4,
        args=(*parts, *recvs, w, m, v), name=name, comm=comm)
    return outs if comm is None else (outs, couts)


def _adamw_replicated(gathered, w, m, v, name):
    r, lanes = w.shape

    def body(g_ref, w_ref, m_ref, v_ref, g_out, d_out, m_out, v_out):
        g = g_ref[0]
        for k in range(1, N_DEV):
            g = g + g_ref[k]
        delta, m_new, v_new = _adamw_math(w_ref[...], g, m_ref[...], v_ref[...])
        g_out[...] = g
        d_out[...] = delta
        m_out[...] = m_new
        v_out[...] = v_new

    return pl.pallas_call(
        body,
        out_shape=[jax.ShapeDtypeStruct((r, lanes), F32)] * 4,
        name=name,
    )(gathered, w, m, v)


def _wire_shard(name, shard):
    return (shard.T if name in COL_SHARDED else shard).astype(WIRE)


def _full_weight(gathered):
    return gathered.reshape(N_DEV * gathered.shape[1], gathered.shape[2])


def _to_slabs(gfull):
    return gfull.reshape(N_DEV, gfull.shape[0] // N_DEV, gfull.shape[1])


def _pack_small(arrs):
    rows = []
    for a in arrs:
        flat = a.astype(F32).reshape(-1)
        pad = (-flat.shape[0]) % LANES
        rows.append(jnp.pad(flat, (0, pad)).reshape(-1, LANES))
    packed = jnp.concatenate(rows, axis=0)
    return jnp.pad(packed, ((0, (-packed.shape[0]) % 8), (0, 0)))


def _unpack_small(packed, shapes):
    out, off = [], 0
    for shp in shapes:
        n = math.prod(shp)
        rows = -(-n // LANES)
        out.append(packed[off:off + rows].reshape(-1)[:n].reshape(shp))
        off += rows
    return out


def _of(layer, *names):
    return tuple((layer, n) for n in names)


MLP_W = ("w_ff1", "w_ff2", "w_pe", "w_pg")

GATHERS = (
    ("rms_mix_l0", _of(0, "w_in")),
    ("proj_qkv_l0", _of(0, "w_up_a", "w_up_b", "w_o")),
    ("proj_gate_l0", _of(0, "w_pe", "w_pg")),
    ("sb_fwd_l0", _of(0, "w_ff1", "w_ff2")),
    ("swa_fwd_l0", _of(1, "w_in")),
    ("sb_fwd_l1", _of(1, "w_up_a", "w_up_b", "w_o", "w_pe", "w_pg", "w_ff1")),
    ("swa_fwd_l1", _of(1, "w_ff2")),
)
REDUCES = (
    (_of(1, "w_ff1"), "dw_ff2_l0", "dact_l0"),
    (_of(1, "w_ff2"), "dw_ff2_l0", "dw_ff1_l0"),
    (_of(1, "w_in"), "dw_ff2_l0", "sb_bwd_l0"),
    (_of(1, "w_up_a", "w_up_b", "w_o", "w_pe", "w_pg"), "dw_ff2_l0", "swa_bwd_l0"),
    (_of(0, *MLP_W), "dh_mlp_l0", "sb_bwd_l0"),
    (_of(0, "w_o", "w_up_a", "w_up_b"), "do_a_l0", "sb_bwd_l0"),
    (_of(0, "w_in"), "dh_ga_l0", "dh_qkv_l0"),
)


def _merge_comms(comms):
    if len(comms) == 1:
        return comms[0]

    def cuts(counts):
        edges = [0]
        for c in counts:
            edges.append(edges[-1] + c)
        return [slice(a, b) for a, b in zip(edges[:-1], edges[1:])]

    s_in = cuts([len(c.inputs) for c in comms])
    s_out = cuts([len(c.out_shapes) for c in comms])
    s_sem = cuts([len(c.sems) for c in comms])

    def start(pos, cin, cout, csem):
        for c, i, o, s in zip(comms, s_in, s_out, s_sem):
            c.start(pos, cin[i], cout[o], csem[s])

    def finish(pos, cin, cout, csem):
        for c, i, o, s in zip(comms, s_in, s_out, s_sem):
            c.finish(pos, cin[i], cout[o], csem[s])

    return _Comm(sum([c.inputs for c in comms], []), sum([c.out_shapes for c in comms], []),
                 sum([c.sems for c in comms], []), start, finish)


class _LayerWeights:
    def __init__(self, full, layer):
        self.full, self.layer, self.cache = full, layer, {}

    def __getitem__(self, name):
        if name not in self.cache:
            if name == "w_qkv_t":
                self.cache[name] = self.full[(self.layer, "w_in")][:QKV_COLS]
            elif name == "w_gate_t":
                self.cache[name] = self.full[(self.layer, "w_in")][QKV_COLS:]
            else:
                base = name[:-2] if name.endswith("_t") else name
                assert (base in COL_SHARDED) == name.endswith("_t"), name
                self.cache[name] = self.full[(self.layer, base)]
        return self.cache[name]


class _Plan:
    def __init__(self, w_sh, pos):
        self.w_sh = dict(zip(WEIGHTS, w_sh))
        self.pos = pos
        self.full, self.gw, self.parts, self.recv = {}, {}, {}, {}
        self.slabs = {}
        self.hosted = {}
        for i, (host, _) in enumerate(GATHERS):
            self.hosted.setdefault(host, []).append(("gather", i))
        for i, (_, sib_host, chip_host) in enumerate(REDUCES):
            self.hosted.setdefault(sib_host, []).append(("sibling", i))
            self.hosted.setdefault(chip_host, []).append(("chips", i))

    def _gather(self, i):
        return _gather_comm([_wire_shard(n, self.w_sh[n][layer]) for layer, n in GATHERS[i][1]])

    def _gathered(self, i, outs):
        for (layer, n), g in zip(GATHERS[i][1], outs):
            self.full[(layer, n)] = _full_weight(g)

    def weights(self, layer):
        return _LayerWeights(self.full, layer)

    def grad(self, layer, name, value):
        self.gw[(layer, name)] = value

    def _sibling(self, i):
        self.slabs[i] = [_to_slabs(self.gw[item]) for item in REDUCES[i][0]]
        return _rs_sibling_comm(self.slabs[i])

    def _sibling_done(self, i, outs):
        parts = _chip_partials(self.pos, self.slabs[i], outs, f"chip_partials_{i}")
        for item, part in zip(REDUCES[i][0], parts):
            self.parts[item] = part

    def _chips(self, i):
        return _rs_chips_comm([self.parts[item] for item in REDUCES[i][0]])

    def _chips_done(self, i, outs):
        for item, r in zip(REDUCES[i][0], outs):
            self.recv[item] = r

    def comm(self, name):
        if name not in self.hosted:
            return None
        make = {"gather": self._gather, "sibling": self._sibling, "chips": self._chips}
        return _merge_comms([make[kind](i) for kind, i in self.hosted[name]])

    def done(self, name, outs):
        took = {"gather": self._gathered, "sibling": self._sibling_done, "chips": self._chips_done}
        off = 0
        for kind, i in self.hosted[name]:
            n = len(GATHERS[i][1]) if kind == "gather" else len(REDUCES[i][0])
            took[kind](i, outs[off:off + n])
            off += n


def kernel(x, p, w_in, w_up_a, w_up_b, w_o, w_ff1, w_ff2, w_pe, w_pg, g_mix, g_mlp, g_pe, g_final, sinks, rel_bias, loss_target, m_w_in, m_w_up_a, m_w_up_b, m_w_o, m_w_ff1, m_w_ff2, m_w_pe, m_w_pg, m_g_mix, m_g_mlp, m_g_pe, m_g_final, m_sinks, m_rel_bias, v_w_in, v_w_up_a, v_w_up_b, v_w_o, v_w_ff1, v_w_ff2, v_w_pe, v_w_pg, v_g_mix, v_g_mlp, v_g_pe, v_g_final, v_sinks, v_rel_bias):
    w_sh = [w_in, w_up_a, w_up_b, w_o, w_ff1, w_ff2, w_pe, w_pg]
    m_sh = [m_w_in, m_w_up_a, m_w_up_b, m_w_o, m_w_ff1, m_w_ff2, m_w_pe, m_w_pg]
    v_sh = [v_w_in, v_w_up_a, v_w_up_b, v_w_o, v_w_ff1, v_w_ff2, v_w_pe, v_w_pg]
    depth = w_in.shape[0]
    assert depth == 2 and x.shape[-1] * 2 + QKV_COLS == w_in.shape[2] * N_DEV

    px, py, pc = _position()
    plan = _Plan(w_sh, jnp.stack([px, py, pc]).astype(jnp.int32))
    loss_row, grad_x, _, small = _local_step(
        x[0], p[:, 0], loss_target[0], plan.weights, g_mix, g_mlp, g_pe, g_final, sinks, rel_bias, plan=plan)

    small_g = _pack_small([small[n] for n in SMALL] + [loss_row[0, :1]])
    grad_w, delta_w, new_m, new_v = [], [], [], []
    for a, name in enumerate(WEIGHTS):
        parts = [plan.parts[(l, name)] for l in range(depth)]
        recvs = [plan.recv[(l, name)] for l in range(depth)]
        if name == "w_in":
            flip = lambda t: t.transpose(0, 2, 1)
            outs, (small_all,) = _adamw_weight(parts, recvs, flip(w_sh[a]), flip(m_sh[a]), flip(v_sh[a]),
                                               f"adamw_{name}", comm=_gather_comm([small_g]))
            outs = [flip(o) for o in outs]
        else:
            outs = _adamw_weight(parts, recvs, w_sh[a], m_sh[a], v_sh[a], f"adamw_{name}",
                                 grad_t=name in COL_SHARDED)
        for lst, o in zip((grad_w, delta_w, new_m, new_v), outs):
            lst.append(o)

    small_w = [g_mix, g_mlp, g_pe, g_final, sinks, rel_bias]
    small_m = [m_g_mix, m_g_mlp, m_g_pe, m_g_final, m_sinks, m_rel_bias]
    small_v = [v_g_mix, v_g_mlp, v_g_pe, v_g_final, v_sinks, v_rel_bias]
    small_shapes = [a.shape for a in small_w] + [(1,)]
    zero = jnp.zeros((1,), F32)
    packed_s = _adamw_replicated(small_all, _pack_small(small_w + [zero]), _pack_small(small_m + [zero]),
                                 _pack_small(small_v + [zero + 1.0]), "adamw_replicated")
    sg, sd, sm, sv = [_unpack_small(t, small_shapes) for t in packed_s]
    loss = sg[-1][0]

    return (loss, grad_x[None], *grad_w, *sg[:-1], *delta_w, *sd[:-1], *new_m, *sm[:-1], *new_v, *sv[:-1])
```

```python
import functools
import math

import numpy as np
import jax
import jax.numpy as jnp
from jax import lax
from jax.experimental import pallas as pl
from jax.experimental.pallas import tpu as pltpu

F32 = jnp.float32
MXU = jnp.bfloat16
WIRE = jnp.bfloat16

HEAD_DIM = 64
SB_HEADS = 8
SW_HEADS = 8
SW_KV = 2
SW_GROUP = SW_HEADS // SW_KV
BLOCK = 128
N_BUCKETS = 32
MAX_DISTANCE = 128
EPS = 1e-6
SCALE = HEAD_DIM ** -0.5
SB_W = SB_HEADS * HEAD_DIM
SW_W = SW_HEADS * HEAD_DIM
QKV_COLS = 3 * SB_W + SW_W + 2 * SW_KV * HEAD_DIM
N_DEV = 8
LANES = 128
N_PAIR = SB_HEADS // 2
NEG = -1e30

ADAM_LR = 0.001
ADAM_B1 = 0.9
ADAM_B2 = 0.999
ADAM_EPS = 1e-08
ADAM_WD = 0.01
ADAM_STEP = 10

VMEM_LIMIT = 48 * 1024 * 1024
SB_TQ = 256
SB_DEAD = -105.0
SB_SUB = 4

WEIGHTS = ("w_in", "w_up_a", "w_up_b", "w_o", "w_ff1", "w_ff2", "w_pe", "w_pg")
COL_SHARDED = ("w_in", "w_up_a", "w_up_b", "w_ff1", "w_pe")
SMALL = ("g_mix", "g_mlp", "g_pe", "g_final", "sinks", "rel_bias")


def _cparams(**kw):
    return pltpu.CompilerParams(vmem_limit_bytes=VMEM_LIMIT, **kw)


def _dot(a, b):
    return jnp.dot(a, b, preferred_element_type=F32)


def _dot_nt(a, b):
    return lax.dot_general(a, b, (((1,), (1,)), ((), ())), preferred_element_type=F32)


def _dot_tn(a, b):
    return lax.dot_general(a, b, (((0,), (0,)), ((), ())), preferred_element_type=F32)


def _tile(n, target, unit=LANES):
    if n <= target:
        return n
    t = (target // unit) * unit
    while t > unit and n % t:
        t -= unit
    assert n % t == 0, (n, target)
    return t


def _sigmoid(x):
    return 0.5 * jnp.tanh(0.5 * x) + 0.5


class _Comm:
    def __init__(self, inputs, out_shapes, sems, start, finish):
        self.inputs, self.out_shapes, self.sems = list(inputs), list(out_shapes), list(sems)
        self.start, self.finish = start, finish


def _call(body, *, grid, in_specs, out_specs, out_shape, scratch_shapes=(), args, name, comm=None):
    n_in, n_out, n_scr = len(in_specs), len(out_shape), len(scratch_shapes)
    if comm is None:
        outs = pl.pallas_call(body, grid=grid, in_specs=list(in_specs), out_specs=list(out_specs),
                              out_shape=list(out_shape), scratch_shapes=list(scratch_shapes),
                              compiler_params=_cparams(), name=name)(*args)
        return list(outs), None
    ci, co = len(comm.inputs), len(comm.out_shapes)
    any_spec = pl.BlockSpec(memory_space=pl.ANY)

    def wrapped(*refs):
        ins, cin = refs[:n_in], refs[n_in:n_in + ci]
        o0 = n_in + ci
        outs, cout = refs[o0:o0 + n_out], refs[o0 + n_out:o0 + n_out + co]
        s0 = o0 + n_out + co
        scr, csem = refs[s0:s0 + n_scr], refs[s0 + n_scr:]
        ids = [pl.program_id(d) for d in range(len(grid))]
        first = functools.reduce(jnp.logical_and, [i == 0 for i in ids])
        last = functools.reduce(jnp.logical_and, [i == g - 1 for i, g in zip(ids, grid)])
        pos = (lax.axis_index("x"), lax.axis_index("y"), lax.axis_index("c"))

        @pl.when(first)
        def _():
            comm.start(pos, cin, cout, csem)

        body(*ins, *outs, *scr)

        @pl.when(last)
        def _():
            comm.finish(pos, cin, cout, csem)

    outs = pl.pallas_call(wrapped, grid=grid, in_specs=list(in_specs) + [any_spec] * ci,
                          out_specs=list(out_specs) + [any_spec] * co,
                          out_shape=list(out_shape) + comm.out_shapes,
                          scratch_shapes=list(scratch_shapes) + comm.sems,
                          compiler_params=_cparams(), name=name)(*args, *comm.inputs)
    return list(outs[:n_out]), list(outs[n_out:])


def _accumulate(o_ref, value, first):
    @pl.when(first)
    def _():
        o_ref[...] = value

    @pl.when(jnp.logical_not(first))
    def _():
        o_ref[...] += value


def _mm(a, b, *, ta=False, tb=False, extras=(), epi=None, out_dtypes=(F32,),
        tm=1024, tn=1024, tk=1024, name, comm=None):
    if ta:
        kdim, m = a.shape
    else:
        m, kdim = a.shape
    n = b.shape[0] if tb else b.shape[1]
    assert (b.shape[1] if tb else b.shape[0]) == kdim
    tm, tn, tk = _tile(m, tm), _tile(n, tn), _tile(kdim, tk)
    nk = kdim // tk
    n_ex, n_out = len(extras), len(out_dtypes)

    a_spec = (pl.BlockSpec((tk, tm), lambda i, j, k: (k, i)) if ta
              else pl.BlockSpec((tm, tk), lambda i, j, k: (i, k)))
    b_spec = (pl.BlockSpec((tn, tk), lambda i, j, k: (j, k)) if tb
              else pl.BlockSpec((tk, tn), lambda i, j, k: (k, j)))
    ex_specs = []
    for e in extras:
        assert e.shape in ((m, n), (1, n), (m, 1)), (e.shape, m, n)
        if e.shape == (m, n):
            ex_specs.append(pl.BlockSpec((tm, tn), lambda i, j, k: (i, j)))
        elif e.shape[0] == 1:
            ex_specs.append(pl.BlockSpec((1, tn), lambda i, j, k: (0, j)))
        else:
            ex_specs.append(pl.BlockSpec((tm, 1), lambda i, j, k: (i, 0)))
    out_specs, out_shape, row_sums = [], [], []
    for dt in out_dtypes:
        kind = dt[1] if isinstance(dt, tuple) else "tile"
        row_sums.append(kind == "rowsum")
        if kind == "tile":
            out_specs.append(pl.BlockSpec((tm, tn), lambda i, j, k: (i, j)))
            out_shape.append(jax.ShapeDtypeStruct((m, n), dt))
            continue
        assert tn == n, "per-row and summed outputs need whole rows in one tile"
        if kind == "col":
            out_specs.append(pl.BlockSpec((tm, 1), lambda i, j, k: (i, 0)))
            out_shape.append(jax.ShapeDtypeStruct((m, 1), dt[0]))
        else:
            out_specs.append(pl.BlockSpec((1, tn), lambda i, j, k: (0, 0)))
            out_shape.append(jax.ShapeDtypeStruct((1, n), dt[0]))

    def body(a_ref, b_ref, *rest):
        ex_refs = rest[:n_ex]
        out_refs = rest[n_ex:n_ex + n_out]
        acc = rest[-1]
        k = pl.program_id(2)
        first_rows = pl.program_id(0) == 0

        def prod():
            av = a_ref[...].astype(MXU)
            bv = b_ref[...].astype(MXU)
            return _dot_tn(av, bv) if ta else (_dot_nt(av, bv) if tb else _dot(av, bv))

        def finish(res):
            if epi is not None:
                res = epi(res, *[e[...] for e in ex_refs])
            if not isinstance(res, tuple):
                res = (res,)
            for o_ref, r, summed in zip(out_refs, res, row_sums):
                if summed:
                    _accumulate(o_ref, r.astype(o_ref.dtype), first_rows)
                else:
                    o_ref[...] = r.astype(o_ref.dtype)

        if nk == 1:
            finish(prod())
            return

        @pl.when(k == 0)
        def _():
            acc[...] = prod()

        @pl.when(jnp.logical_and(k > 0, k < nk - 1))
        def _():
            acc[...] += prod()

        @pl.when(k == nk - 1)
        def _():
            finish(acc[...] + prod())

    outs, couts = _call(
        body,
        grid=(m // tm, n // tn, nk),
        in_specs=[a_spec, b_spec] + ex_specs,
        out_specs=out_specs,
        out_shape=out_shape,
        scratch_shapes=[pltpu.VMEM((tm, tn), F32)],
        args=(a, b, *extras), name=name, comm=comm)
    res = outs[0] if n_out == 1 else tuple(outs)
    return res if comm is None else (res, couts)


def _rms_fwd(x, g, name, comm=None):
    s, d = x.shape
    tr = _tile(s, 256)

    def body(x_ref, g_ref, h_ref, r_ref):
        xf = x_ref[...]
        r = lax.rsqrt(jnp.mean(xf * xf, axis=-1, keepdims=True) + EPS)
        h_ref[...] = ((xf * r) * g_ref[...]).astype(h_ref.dtype)
        r_ref[...] = r

    outs, couts = _call(
        body,
        grid=(s // tr,),
        in_specs=[pl.BlockSpec((tr, d), lambda i: (i, 0)), pl.BlockSpec((1, d), lambda i: (0, 0))],
        out_specs=[pl.BlockSpec((tr, d), lambda i: (i, 0)), pl.BlockSpec((tr, 1), lambda i: (i, 0))],
        out_shape=[jax.ShapeDtypeStruct((s, d), MXU), jax.ShapeDtypeStruct((s, 1), F32)],
        args=(x, g), name=name, comm=comm)
    return tuple(outs) if comm is None else (tuple(outs), couts)


def _loss_head(x, g, target, name):
    s, d = x.shape
    tr = _tile(s, 256)

    def body(x_ref, g_ref, t_ref, loss_ref, dx_ref, dg_ref):
        @pl.when(pl.program_id(0) == 0)
        def _():
            dg_ref[...] = jnp.zeros_like(dg_ref)
            loss_ref[...] = jnp.zeros_like(loss_ref)

        xf = x_ref[...]
        gv = g_ref[...]
        r = lax.rsqrt(jnp.mean(xf * xf, axis=-1, keepdims=True) + EPS)
        xhat = xf * r
        err = xhat * gv - t_ref[...]
        loss_ref[...] += 0.5 * jnp.sum(jnp.mean(err * err, axis=-1, keepdims=True), axis=0, keepdims=True)
        dy = err * (1.0 / d)
        dxhat = dy * gv
        mean = jnp.mean(dxhat * xhat, axis=-1, keepdims=True)
        dx_ref[...] = r * (dxhat - xhat * mean)
        dg_ref[...] += jnp.sum(dy * xhat, axis=0, keepdims=True)

    row = pl.BlockSpec((tr, d), lambda i: (i, 0))
    vec = pl.BlockSpec((1, d), lambda i: (0, 0))
    return pl.pallas_call(
        body,
        grid=(s // tr,),
        in_specs=[row, vec, row],
        out_specs=[pl.BlockSpec((1, LANES), lambda i: (0, 0)), row, vec],
        out_shape=[jax.ShapeDtypeStruct((1, LANES), F32), jax.ShapeDtypeStruct((s, d), F32),
                   jax.ShapeDtypeStruct((1, d), F32)],
        compiler_params=_cparams(),
        name=name,
    )(x, g, target)


def _mix_fwd(oa, ob, wa_t, wb_t, gates, name):
    s, kd = oa.shape
    d = wa_t.shape[0]
    tm, tn = _tile(s, 1024), _tile(d, 1024)
    nj = d // tn

    def body(oa_ref, ob_ref, wa_ref, wb_ref, ga_ref, gb_ref, out_ref):
        ya = _dot_nt(oa_ref[...], wa_ref[...])
        yb = _dot_nt(ob_ref[...], wb_ref[...])
        out_ref[...] = (_sigmoid(ga_ref[...].astype(F32)) * ya
                        + _sigmoid(gb_ref[...].astype(F32)) * yb).astype(out_ref.dtype)

    o_spec = pl.BlockSpec((tm, kd), lambda i, j: (i, 0))
    w_spec = pl.BlockSpec((tn, kd), lambda i, j: (j, 0))
    return pl.pallas_call(
        body,
        grid=(s // tm, nj),
        in_specs=[o_spec, o_spec, w_spec, w_spec,
                  pl.BlockSpec((tm, tn), lambda i, j: (i, j)),
                  pl.BlockSpec((tm, tn), lambda i, j: (i, j + nj))],
        out_specs=pl.BlockSpec((tm, tn), lambda i, j: (i, j)),
        out_shape=jax.ShapeDtypeStruct((s, d), MXU),
        compiler_params=_cparams(),
        name=name,
    )(oa, ob, wa_t, wb_t, gates, gates)


def _mix_bwd(dx, w_o, oa, ob, wa_t, wb_t, gates, name):
    s, kd = oa.shape
    d = wa_t.shape[0]
    tm, tn = _tile(s, 512), _tile(d, 1024)
    nj = d // tn

    def body(dx_ref, wo_ref, oa_ref, ob_ref, wa_ref, wb_ref, ga_ref, gb_ref,
             dya_ref, dyb_ref, dga_ref, dgb_ref):
        dm = _dot_nt(dx_ref[...], wo_ref[...])
        ya = _dot_nt(oa_ref[...], wa_ref[...])
        yb = _dot_nt(ob_ref[...], wb_ref[...])
        sa = _sigmoid(ga_ref[...].astype(F32))
        sb = _sigmoid(gb_ref[...].astype(F32))
        dya_ref[...] = (dm * sa).astype(dya_ref.dtype)
        dyb_ref[...] = (dm * sb).astype(dyb_ref.dtype)
        dga_ref[...] = (dm * ya * sa * (1.0 - sa)).astype(dga_ref.dtype)
        dgb_ref[...] = (dm * yb * sb * (1.0 - sb)).astype(dgb_ref.dtype)

    o_spec = pl.BlockSpec((tm, kd), lambda i, j: (i, 0))
    w_spec = pl.BlockSpec((tn, kd), lambda i, j: (j, 0))
    t_spec = pl.BlockSpec((tm, tn), lambda i, j: (i, j))
    return pl.pallas_call(
        body,
        grid=(s // tm, nj),
        in_specs=[pl.BlockSpec((tm, d), lambda i, j: (i, 0)),
                  pl.BlockSpec((tn, d), lambda i, j: (j, 0)),
                  o_spec, o_spec, w_spec, w_spec, t_spec,
                  pl.BlockSpec((tm, tn), lambda i, j: (i, j + nj))],
        out_specs=[t_spec] * 4,
        out_shape=[jax.ShapeDtypeStruct((s, d), MXU)] * 4,
        compiler_params=_cparams(),
        name=name,
    )(dx, w_o, oa, ob, wa_t, wb_t, gates, gates)


def _ple(p, w_pe_t, h, w_pg, other, *, backward, name, next_g=None):
    s, kp = p.shape
    d = w_pe_t.shape[0]
    tm, tn = _tile(s, 1024), _tile(d, 1024)
    with_norm = next_g is not None
    assert not (with_norm and (backward or tn != d))

    def body(p_ref, wpe_ref, h_ref, wpg_ref, other_ref, *rest):
        out_refs = rest[1:] if with_norm else rest
        pe = _dot_nt(p_ref[...].astype(MXU), wpe_ref[...])
        gt = _dot(h_ref[...], wpg_ref[...])
        sg = _sigmoid(gt)
        if backward:
            dout = other_ref[...]
            out_refs[0][...] = (dout * sg).astype(out_refs[0].dtype)
            out_refs[1][...] = (dout * pe * sg * (1.0 - sg)).astype(out_refs[1].dtype)
        elif with_norm:
            x_new, h_new, r_new = _residual_norm(pe * sg, other_ref[...], rest[0][...])
            out_refs[0][...] = x_new
            out_refs[1][...] = h_new.astype(out_refs[1].dtype)
            out_refs[2][...] = r_new
        else:
            out_refs[0][...] = other_ref[...] + pe * sg

    t_spec = pl.BlockSpec((tm, tn), lambda i, j: (i, j))
    if backward:
        out_specs, out_shape = [t_spec, t_spec], [jax.ShapeDtypeStruct((s, d), MXU)] * 2
    else:
        out_specs, out_shape = [t_spec], [jax.ShapeDtypeStruct((s, d), F32)]
    in_specs = [pl.BlockSpec((tm, kp), lambda i, j: (i, 0)),
                pl.BlockSpec((tn, kp), lambda i, j: (j, 0)),
                pl.BlockSpec((tm, d), lambda i, j: (i, 0)),
                pl.BlockSpec((d, tn), lambda i, j: (0, j)),
                t_spec]
    args = [p, w_pe_t, h, w_pg, other]
    if with_norm:
        in_specs.append(pl.BlockSpec((1, tn), lambda i, j: (0, j)))
        args.append(next_g)
        out_specs += [t_spec, pl.BlockSpec((tm, 1), lambda i, j: (i, 0))]
        out_shape += [jax.ShapeDtypeStruct((s, d), MXU), jax.ShapeDtypeStruct((s, 1), F32)]
    outs = pl.pallas_call(
        body,
        grid=(s // tm, d // tn),
        in_specs=in_specs,
        out_specs=out_specs,
        out_shape=out_shape,
        compiler_params=_cparams(),
        name=name,
    )(*args)
    return tuple(outs) if (backward or with_norm) else outs[0]


def _split_dot(x, tri):
    hi = x.astype(jnp.bfloat16)
    lo = (x - hi.astype(F32)).astype(jnp.bfloat16)
    return _dot(hi, tri) + _dot(lo, tri)


def _log_sigmoids(z):
    lb = jnp.minimum(z, 0.0) - jnp.log(1.0 + jnp.exp(-jnp.abs(z)))
    return lb, lb - z


def _head_lanes(hh):
    lane = lax.broadcasted_iota(jnp.int32, (1, LANES), 1)
    return jnp.logical_and(lane >= hh * HEAD_DIM, lane < (hh + 1) * HEAD_DIM)


def _sb_fwd(qkv, name, comm=None):
    s = qkv.shape[0]
    tq = _tile(s, SB_TQ)
    nsub = SB_SUB if (s // tq) % SB_SUB == 0 else 1

    def body(q_ref, k_ref, v_ref, o_ref):
        row = lax.broadcasted_iota(jnp.int32, (tq, tq), 0)
        col = lax.broadcasted_iota(jnp.int32, (tq, tq), 1)
        causal = col < row
        tri = jnp.where(row > col, 1.0, 0.0).astype(jnp.bfloat16)
        started = [_sb_fwd_straight(q_ref, k_ref, v_ref, pl.program_id(1) * nsub + sub, sub, tq, causal, tri)
                   for sub in range(nsub)]
        for sub, (block, i, cs, accs) in enumerate(started):
            def top(cs):
                return jnp.maximum(jnp.max(cs[0]), jnp.max(cs[1]))

            def live(st):
                return jnp.logical_and(st[0] >= 0, st[1] > SB_DEAD)

            def walk(st, block=block):
                cs, accs = block(st[0], st[2], st[3], False)
                return st[0] - 1, top(cs), cs, accs

            accs = lax.while_loop(live, walk, (i - 2, top(cs), cs, accs))[3]
            o_ref[sub * tq:(sub + 1) * tq, :] = jnp.where(_head_lanes(0), accs[0], accs[1]).astype(o_ref.dtype)

    outs, couts = _call(
        body,
        grid=(N_PAIR, s // (nsub * tq)),
        in_specs=[pl.BlockSpec((nsub * tq, LANES), lambda p, i: (i, p)),
                  pl.BlockSpec((s, LANES), lambda p, i: (0, N_PAIR + p)),
                  pl.BlockSpec((s, LANES), lambda p, i: (0, 2 * N_PAIR + p))],
        out_specs=[pl.BlockSpec((nsub * tq, LANES), lambda p, i: (i, p))],
        out_shape=[jax.ShapeDtypeStruct((s, SB_W), MXU)],
        args=(qkv, qkv, qkv), name=name, comm=comm)
    return outs[0] if comm is None else (outs[0], couts)


def _sb_fwd_straight(q_ref, k_ref, v_ref, i, sub, tq, causal, tri):
    qf = q_ref[sub * tq:(sub + 1) * tq, :].astype(F32) * SCALE
    qms = [jnp.where(_head_lanes(hh), qf, 0.0).astype(MXU) for hh in range(2)]

    def block(kb, cs, accs, masked, gate=None):
        rows = pl.ds(pl.multiple_of(kb * tq, tq), tq)
        ks, vs = k_ref[rows, :], v_ref[rows, :]
        new_c, new_acc = [], []
        for hh in range(2):
            lb, lm = _log_sigmoids(_dot_nt(qms[hh], ks))
            if masked:
                lm = jnp.where(causal, lm, 0.0)
            if gate is not None:
                lm = lm * gate
            a = jnp.exp(lb + _split_dot(lm, tri) + cs[hh])
            if masked:
                a = jnp.where(causal, a, 0.0)
            if gate is not None:
                a = a * gate
            new_acc.append(accs[hh] + _dot(a.astype(MXU), vs))
            new_c.append(cs[hh] + jnp.sum(lm, axis=1, keepdims=True))
        return tuple(new_c), tuple(new_acc)

    zc, za = jnp.zeros((tq, 1), F32), jnp.zeros((tq, LANES), F32)
    cs, accs = block(i, (zc, zc), (za, za), True)
    cs, accs = block(jnp.maximum(i - 1, 0), cs, accs, False, jnp.where(i > 0, 1.0, 0.0))
    return block, i, cs, accs


def _sb_bwd(qkv, do, name, comm=None):
    s = qkv.shape[0]
    tq = _tile(s, SB_TQ)
    nq = s // tq
    nsub = SB_SUB if nq % SB_SUB == 0 else 1
    nsteps = nq // nsub

    def body(q_ref, k_ref, v_ref, do_ref, dq_ref, dk_ref, dv_ref, dk_acc, dv_acc, carries):
        step = pl.program_id(1)

        @pl.when(step == 0)
        def _():
            dk_acc[...] = jnp.zeros_like(dk_acc)
            dv_acc[...] = jnp.zeros_like(dv_acc)

        row = lax.broadcasted_iota(jnp.int32, (tq, tq), 0)
        col = lax.broadcasted_iota(jnp.int32, (tq, tq), 1)
        causal = col < row
        tri_rev = jnp.where(row > col, 1.0, 0.0).astype(jnp.bfloat16)
        tri_excl = jnp.where(row < col, 1.0, 0.0).astype(jnp.bfloat16)
        zc, za = jnp.zeros((tq, 1), F32), jnp.zeros((tq, LANES), F32)

        def top(cs):
            return jnp.maximum(jnp.max(cs[0]), jnp.max(cs[1]))

        def live(st):
            return jnp.logical_and(st[0] >= 0, st[1] > SB_DEAD)

        def row_sums(pre):
            return [jnp.sum(lm, axis=1, keepdims=True) for _, lm in pre]

        def query_block(sub):
            i = step * nsub + sub
            q_rows = slice(sub * tq, (sub + 1) * tq)
            qf = q_ref[q_rows, :].astype(F32) * SCALE
            dof = do_ref[q_rows, :]
            qms = [jnp.where(_head_lanes(hh), qf, 0.0).astype(MXU) for hh in range(2)]
            doms = [jnp.where(_head_lanes(hh), dof, jnp.zeros_like(dof)) for hh in range(2)]

            def terms(kb, masked):
                rows = pl.ds(pl.multiple_of(kb * tq, tq), tq)
                ks = k_ref[rows, :]
                out = []
                for hh in range(2):
                    lb, lm = _log_sigmoids(_dot_nt(qms[hh], ks))
                    if masked:
                        lm = jnp.where(causal, lm, 0.0)
                    out.append((lb, lm))
                return out

            def block(kb, cs, gpres, dqs, masked, gate=None, pre=None):
                rows = pl.ds(pl.multiple_of(kb * tq, tq), tq)
                ks, vs = k_ref[rows, :], v_ref[rows, :]
                pre = terms(kb, masked) if pre is None else pre
                new_g, new_dq = [], []
                dk_add, dv_add = None, None
                for hh in range(2):
                    lb, lm = pre[hh]
                    a = jnp.exp(lb + _split_dot(lm, tri_rev) + cs[hh])
                    if masked:
                        a = jnp.where(causal, a, 0.0)
                    if gate is not None:
                        a = a * gate
                    g = a * _dot_nt(doms[hh], vs)
                    gsum = gpres[hh] + _split_dot(g, tri_excl)
                    dz = g - (g + gsum) * jnp.exp(lb)
                    if masked:
                        dz = jnp.where(causal, dz, 0.0)
                    if gate is not None:
                        dz = dz * gate
                    dzb = dz.astype(MXU)
                    new_dq.append(dqs[hh] + _dot(dzb, ks))
                    dk_h = _dot_tn(dzb, qms[hh])
                    dv_h = _dot_tn(a.astype(MXU), doms[hh])
                    dk_add = dk_h if dk_add is None else dk_add + dk_h
                    dv_add = dv_h if dv_add is None else dv_add + dv_h
                    new_g.append(gpres[hh] + jnp.sum(g, axis=1, keepdims=True))
                dk_acc[rows, :] += dk_add
                dv_acc[rows, :] += dv_add
                return tuple(new_g), tuple(new_dq)

            prev = jnp.maximum(i - 1, 0)
            gate = jnp.where(i > 0, 1.0, 0.0)
            t_diag, t_prev = terms(i, True), terms(prev, False)
            c_diag = row_sums(t_diag)
            sums = row_sums(t_prev)
            c_prev = tuple(c_diag[hh] + sums[hh] * gate for hh in range(2))
            return dict(i=i, prev=prev, gate=gate, q_rows=q_rows, terms=terms, block=block,
                        t_diag=t_diag, t_prev=t_prev, c_diag=c_diag, c_prev=c_prev)

        blocks = [query_block(sub) for sub in range(nsub)]
        for qb in blocks:
            def record(st, qb=qb):
                kb, cs = st[0], st[2]
                sums = row_sums(qb["terms"](kb, False))
                for hh in range(2):
                    carries[hh, kb] = cs[hh]
                cs = tuple(cs[hh] + sums[hh] for hh in range(2))
                return kb - 1, top(cs), cs

            first = lax.while_loop(live, record, (qb["i"] - 2, top(qb["c_prev"]), qb["c_prev"]))[0] + 1
            qb["mid"] = lax.fori_loop(
                first, qb["i"] - 1,
                lambda kb, cr, qb=qb: qb["block"](kb, (carries[0, kb], carries[1, kb]), cr[0], cr[1], False),
                ((zc, zc), (za, za)))
        for qb in blocks:
            gpres, dqs = qb["block"](qb["prev"], qb["c_diag"], *qb["mid"], False, qb["gate"], qb["t_prev"])
            dqs = qb["block"](qb["i"], (zc, zc), gpres, dqs, True, None, qb["t_diag"])[1]
            dq_ref[qb["q_rows"], :] = (jnp.where(_head_lanes(0), dqs[0], dqs[1]) * SCALE).astype(dq_ref.dtype)

        @pl.when(step == nsteps - 1)
        def _():
            dk_ref[...] = dk_acc[...].astype(dk_ref.dtype)
            dv_ref[...] = dv_acc[...].astype(dv_ref.dtype)

    blk = pl.BlockSpec((nsub * tq, LANES), lambda p, i: (i, p))
    full = pl.BlockSpec((s, LANES), lambda p, i: (0, p))
    outs, couts = _call(
        body,
        grid=(N_PAIR, nsteps),
        in_specs=[blk,
                  pl.BlockSpec((s, LANES), lambda p, i: (0, N_PAIR + p)),
                  pl.BlockSpec((s, LANES), lambda p, i: (0, 2 * N_PAIR + p)),
                  blk],
        out_specs=[blk, full, full],
        out_shape=[jax.ShapeDtypeStruct((s, SB_W), MXU)] * 3,
        scratch_shapes=[pltpu.VMEM((s, LANES), F32), pltpu.VMEM((s, LANES), F32),
                        pltpu.VMEM((2, nq, tq, 1), F32)],
        args=(qkv, qkv, qkv, do), name=name, comm=comm)
    return tuple(outs) if comm is None else (tuple(outs), couts)


def _bucket_table():
    i = np.arange(BLOCK)[:, None]
    j = np.arange(2 * BLOCK)[None, :]
    d = np.maximum(BLOCK + i - j, 0)
    max_exact = N_BUCKETS // 2
    df = np.maximum(d, 1).astype(np.float32)
    large = max_exact + (np.log(df / max_exact) / math.log(MAX_DISTANCE / max_exact)
                         * (N_BUCKETS - max_exact)).astype(np.int32)
    large = np.minimum(large, N_BUCKETS - 1)
    return np.where(d < max_exact, d, large).astype(np.int32)


def _build_bias(rel_bias, buckets, name):
    def body(rb_ref, bk_ref, out_ref):
        h = pl.program_id(0)
        bk = bk_ref[...]
        acc = jnp.zeros(bk.shape, F32)
        for b in range(N_BUCKETS):
            acc = jnp.where(bk == b, rb_ref[b, h], acc)
        out_ref[...] = acc

    return pl.pallas_call(
        body,
        grid=(SW_HEADS,),
        in_specs=[pl.BlockSpec(memory_space=pltpu.SMEM),
                  pl.BlockSpec((BLOCK, 2 * BLOCK), lambda h: (0, 0))],
        out_specs=pl.BlockSpec((None, BLOCK, 2 * BLOCK), lambda h: (h, 0, 0)),
        out_shape=jax.ShapeDtypeStruct((SW_HEADS, BLOCK, 2 * BLOCK), F32),
        name=name,
    )(rel_bias, buckets)


def _bias_grad(dbias_layers, buckets, name):
    n_l = len(dbias_layers)

    def body(*refs):
        bk = refs[n_l][...]
        out_ref = refs[n_l + 1]
        db = refs[0][...]
        for r in refs[1:n_l]:
            db = db + r[...]
        lane = lax.broadcasted_iota(jnp.int32, (1, LANES), 1)
        acc = jnp.zeros((1, LANES), F32)
        for b in range(N_BUCKETS):
            part = jnp.sum(jnp.where(bk == b, db, 0.0), axis=1, keepdims=True)
            tot = jnp.sum(part, axis=0, keepdims=True)
            acc = jnp.where(lane == b, tot, acc)
        out_ref[...] = acc

    hspec = pl.BlockSpec((None, BLOCK, 2 * BLOCK), lambda h: (h, 0, 0))
    return pl.pallas_call(
        body,
        grid=(SW_HEADS,),
        in_specs=[hspec] * n_l + [pl.BlockSpec((BLOCK, 2 * BLOCK), lambda h: (0, 0))],
        out_specs=pl.BlockSpec((None, 1, LANES), lambda h: (h, 0, 0)),
        out_shape=jax.ShapeDtypeStruct((SW_HEADS, 1, LANES), F32),
        name=name,
    )(*dbias_layers, buckets)


GROUP_ROWS = SW_GROUP * BLOCK


def _group_lanes(g):
    lane = lax.broadcasted_iota(jnp.int32, (1, LANES), 1)
    gvec = jnp.zeros((1, LANES), jnp.int32) + g
    return jnp.where(lane >= HEAD_DIM, 1, 0) == gvec, gvec


def _stack_heads(x, g):
    kv_lanes, gvec = _group_lanes(g)
    parts = []
    for j in range(SW_GROUP):
        half = x[:, (j // 2) * LANES:(j // 2 + 1) * LANES]
        moved = jnp.where(gvec == j % 2, half, pltpu.roll(half, HEAD_DIM, 1))
        parts.append(jnp.where(kv_lanes, moved, 0.0))
    return jnp.concatenate(parts, axis=0)


def _unstack_heads(y, g):
    _, gvec = _group_lanes(g)
    heads = []
    for j in range(SW_GROUP):
        yj = y[j * BLOCK:(j + 1) * BLOCK]
        heads.append(jnp.where(gvec == j % 2, yj, pltpu.roll(yj, HEAD_DIM, 1)))
    pairs = [jnp.where(_head_lanes(0), heads[2 * p], heads[2 * p + 1]) for p in range(SW_GROUP // 2)]
    return jnp.concatenate(pairs, axis=1)


def _per_head_col(values):
    return jnp.concatenate([jnp.zeros((BLOCK, 1), F32) + v for v in values], axis=0)


def _swa_scores(qs, kp, kc, bias_ref, n):
    row = jnp.bitwise_and(lax.broadcasted_iota(jnp.int32, (GROUP_ROWS, BLOCK), 0), BLOCK - 1)
    col = lax.broadcasted_iota(jnp.int32, (GROUP_ROWS, BLOCK), 1)
    bias = bias_ref[...].reshape(GROUP_ROWS, 2 * BLOCK)
    s1 = _dot_nt(qs, kp) + bias[:, :BLOCK]
    s2 = _dot_nt(qs, kc) + bias[:, BLOCK:]
    no_prev = jnp.where(n > 0, 0, BLOCK)
    s1 = jnp.where(col > row + no_prev, s1, NEG)
    s2 = jnp.where(col <= row, s2, NEG)
    return s1, s2


def _swa_specs(s):
    q_blk = 3 * SB_W // (2 * LANES)
    k_blk = (3 * SB_W + SW_W) // LANES
    return (pl.BlockSpec((s, 2 * LANES), lambda g: (0, q_blk + g)),
            pl.BlockSpec((s, LANES), lambda g: (0, k_blk)),
            pl.BlockSpec((s, LANES), lambda g: (0, k_blk + 1)))


def _swa_fwd(qkv, bias, sinks, name, comm=None):
    s = qkv.shape[0]
    nb = s // BLOCK

    def body(sink_ref, q_ref, k_ref, v_ref, bias_ref, o_ref, lse_ref):
        g = pl.program_id(0)
        sink = _per_head_col([sink_ref[SW_GROUP * g + j] for j in range(SW_GROUP)])
        lane = lax.broadcasted_iota(jnp.int32, (1, LANES), 1)

        def step(n, carry):
            r0 = pl.multiple_of(n * BLOCK, BLOCK)
            p0 = pl.multiple_of(jnp.maximum(n - 1, 0) * BLOCK, BLOCK)
            cur, prev = pl.ds(r0, BLOCK), pl.ds(p0, BLOCK)
            qs = _stack_heads(q_ref[cur, :].astype(F32) * SCALE, g).astype(MXU)
            s1, s2 = _swa_scores(qs, k_ref[prev, :], k_ref[cur, :], bias_ref, n)
            m = jnp.maximum(jnp.max(jnp.maximum(s1, s2), axis=1, keepdims=True), sink)
            e1 = jnp.exp(s1 - m)
            e2 = jnp.exp(s2 - m)
            den = jnp.sum(e1 + e2, axis=1, keepdims=True) + jnp.exp(sink - m)
            o = _dot((e1 / den).astype(MXU), v_ref[prev, :]) + _dot((e2 / den).astype(MXU), v_ref[cur, :])
            o_ref[cur, :] = _unstack_heads(o, g).astype(o_ref.dtype)
            lse = m + jnp.log(den)
            lse_row = jnp.zeros((BLOCK, LANES), F32)
            for j in range(SW_GROUP):
                lse_row = jnp.where(lane == j, lse[j * BLOCK:(j + 1) * BLOCK], lse_row)
            lse_ref[cur, :] = lse_row
            return carry

        lax.fori_loop(0, nb, step, 0, unroll=2)

    outs, couts = _call(
        body,
        grid=(SW_KV,),
        in_specs=[pl.BlockSpec(memory_space=pltpu.SMEM), *_swa_specs(s),
                  pl.BlockSpec((SW_GROUP, BLOCK, 2 * BLOCK), lambda g: (g, 0, 0))],
        out_specs=[pl.BlockSpec((s, 2 * LANES), lambda g: (0, g)),
                   pl.BlockSpec((None, s, LANES), lambda g: (g, 0, 0))],
        out_shape=[jax.ShapeDtypeStruct((s, SW_W), MXU), jax.ShapeDtypeStruct((SW_KV, s, LANES), F32)],
        args=(sinks, qkv, qkv, qkv, bias), name=name, comm=comm)
    return tuple(outs) if comm is None else (tuple(outs), couts)


def _swa_bwd(qkv, bias, sinks, do, lse, name, comm=None):
    s = qkv.shape[0]
    nb = s // BLOCK

    def body(sink_ref, q_ref, k_ref, v_ref, bias_ref, do_ref, lse_ref,
             dq_ref, dk_ref, dv_ref, dbias_ref, dsink_ref, dk_acc, dv_acc):
        g = pl.program_id(0)
        sink = _per_head_col([sink_ref[SW_GROUP * g + j] for j in range(SW_GROUP)])
        lane = lax.broadcasted_iota(jnp.int32, (1, LANES), 1)

        @pl.when(g == 0)
        def _():
            dk_acc[...] = jnp.zeros_like(dk_acc)
            dv_acc[...] = jnp.zeros_like(dv_acc)

        dbias_ref[...] = jnp.zeros_like(dbias_ref)

        def step(n, dsink_rows):
            r0 = pl.multiple_of(n * BLOCK, BLOCK)
            p0 = pl.multiple_of(jnp.maximum(n - 1, 0) * BLOCK, BLOCK)
            cur, prev = pl.ds(r0, BLOCK), pl.ds(p0, BLOCK)
            qs = _stack_heads(q_ref[cur, :].astype(F32) * SCALE, g).astype(MXU)
            dos = _stack_heads(do_ref[cur, :].astype(F32), g).astype(MXU)
            kp, kc, vp, vc = k_ref[prev, :], k_ref[cur, :], v_ref[prev, :], v_ref[cur, :]
            lse_row = lse_ref[cur, :]
            lse = jnp.concatenate([jnp.sum(jnp.where(lane == j, lse_row, 0.0), axis=1, keepdims=True)
                                   for j in range(SW_GROUP)], axis=0)
            s1, s2 = _swa_scores(qs, kp, kc, bias_ref, n)
            pr1 = jnp.exp(s1 - lse)
            pr2 = jnp.exp(s2 - lse)
            dpr1 = _dot_nt(dos, vp)
            dpr2 = _dot_nt(dos, vc)
            delta = jnp.sum(pr1 * dpr1 + pr2 * dpr2, axis=1, keepdims=True)
            ds1 = pr1 * (dpr1 - delta)
            ds2 = pr2 * (dpr2 - delta)
            dbias_ref[:, :, :BLOCK] += ds1.reshape(SW_GROUP, BLOCK, BLOCK)
            dbias_ref[:, :, BLOCK:] += ds2.reshape(SW_GROUP, BLOCK, BLOCK)
            ds1b, ds2b = ds1.astype(MXU), ds2.astype(MXU)
            dq = _dot(ds1b, kp) + _dot(ds2b, kc)
            dq_ref[cur, :] = (_unstack_heads(dq, g) * SCALE).astype(dq_ref.dtype)
            dk_acc[prev, :] += _dot_tn(ds1b, qs)
            dk_acc[cur, :] += _dot_tn(ds2b, qs)
            dv_acc[prev, :] += _dot_tn(pr1.astype(MXU), dos)
            dv_acc[cur, :] += _dot_tn(pr2.astype(MXU), dos)
            return dsink_rows - jnp.exp(sink - lse) * delta

        rows = lax.fori_loop(0, nb, step, jnp.zeros((GROUP_ROWS, 1), F32), unroll=2)
        for j in range(SW_GROUP):
            dsink_ref[j] = jnp.broadcast_to(jnp.sum(rows[j * BLOCK:(j + 1) * BLOCK], axis=0, keepdims=True),
                                            (1, LANES))

        @pl.when(g == SW_KV - 1)
        def _():
            dk_ref[...] = dk_acc[...].astype(dk_ref.dtype)
            dv_ref[...] = dv_acc[...].astype(dv_ref.dtype)

    grp = pl.BlockSpec((s, 2 * LANES), lambda g: (0, g))
    kv_out = pl.BlockSpec((s, LANES), lambda g: (0, 0))
    bspec = pl.BlockSpec((SW_GROUP, BLOCK, 2 * BLOCK), lambda g: (g, 0, 0))
    outs, couts = _call(
        body,
        grid=(SW_KV,),
        in_specs=[pl.BlockSpec(memory_space=pltpu.SMEM), *_swa_specs(s), bspec, grp,
                  pl.BlockSpec((None, s, LANES), lambda g: (g, 0, 0))],
        out_specs=[grp, kv_out, kv_out, bspec, pl.BlockSpec((SW_GROUP, 1, LANES), lambda g: (g, 0, 0))],
        out_shape=[jax.ShapeDtypeStruct((s, SW_W), MXU),
                   jax.ShapeDtypeStruct((s, LANES), MXU),
                   jax.ShapeDtypeStruct((s, LANES), MXU),
                   jax.ShapeDtypeStruct((SW_HEADS, BLOCK, 2 * BLOCK), F32),
                   jax.ShapeDtypeStruct((SW_HEADS, 1, LANES), F32)],
        scratch_shapes=[pltpu.VMEM((s, LANES), F32), pltpu.VMEM((s, LANES), F32)],
        args=(sinks, qkv, qkv, qkv, bias, do, lse), name=name, comm=comm)
    return tuple(outs) if comm is None else (tuple(outs), couts)


class _NoPlan:
    def comm(self, name):
        return None

    def done(self, name, outs):
        pass

    def grad(self, layer, name, value):
        pass


def _run(plan, fn, *args, name, **kw):
    comm = plan.comm(name)
    if comm is None:
        return fn(*args, name=name, **kw)
    res, outs = fn(*args, name=name, comm=comm, **kw)
    plan.done(name, outs)
    return res


def _norm_bwd(dh, x, dres, r, g):
    xhat = x * r
    dxhat = dh * g
    dx = dres + r * (dxhat - xhat * jnp.mean(dxhat * xhat, axis=-1, keepdims=True))
    return dx, dx, jnp.sum(dh * xhat, axis=0, keepdims=True)


def _residual_norm(acc, res, g):
    x = res + acc
    r = lax.rsqrt(jnp.mean(x * x, axis=-1, keepdims=True) + EPS)
    return x, (x * r) * g, r


def _layer_fwd(x, p, w, g_mix, g_mlp, g_pe, sinks, bias, tag, plan, normed_x=None, next_g=None):
    h1, r1 = normed_x if normed_x is not None else _run(plan, _rms_fwd, x, g_mix, name=f"rms_mix_{tag}")
    qkv = _run(plan, _mm, h1, w["w_qkv_t"], tb=True, out_dtypes=(MXU,), tn=QKV_COLS, name=f"proj_qkv_{tag}")
    gates = _run(plan, _mm, h1, w["w_gate_t"], tb=True, out_dtypes=(MXU,), tn=2048, name=f"proj_gate_{tag}")
    oa = _run(plan, _sb_fwd, qkv, name=f"sb_fwd_{tag}")
    ob, lse = _run(plan, _swa_fwd, qkv, bias, sinks, name=f"swa_fwd_{tag}")
    merged = _mix_fwd(oa, ob, w["w_up_a_t"], w["w_up_b_t"], gates, f"mix_fwd_{tag}")
    d = x.shape[1]
    normed = (F32, MXU, (F32, "col"))
    x1, h2, r2 = _run(plan, _mm, merged, w["w_o"], extras=(x, g_mlp), epi=_residual_norm, out_dtypes=normed,
                      tn=d, name=f"out_proj_{tag}")
    u, act = _run(plan, _mm, h2, w["w_ff1_t"], tb=True,
                  epi=lambda acc: (acc, jnp.square(jnp.maximum(acc, 0.0))),
                  out_dtypes=(MXU, MXU), tn=2048, name=f"ff1_{tag}")
    x2, h3, r3 = _run(plan, _mm, act, w["w_ff2"], extras=(x1, g_pe), epi=_residual_norm, out_dtypes=normed,
                      tn=d, name=f"ff2_{tag}")
    x3 = _ple(p, w["w_pe_t"], h3, w["w_pg"], x2, backward=False, name=f"ple_fwd_{tag}", next_g=next_g)
    next_normed = None
    if next_g is not None:
        x3, next_normed = x3[0], (x3[1], x3[2])
    saved = dict(x=x, h1=h1, r1=r1, gates=gates, qkv=qkv, lse=lse, oa=oa, ob=ob, merged=merged,
                 x1=x1, h2=h2, r2=r2, u=u, act=act, x2=x2, h3=h3, r3=r3)
    return x3, saved, next_normed


def _layer_bwd(dx3, sv, p, w, g_mix, g_mlp, g_pe, sinks, bias, layer, plan):
    tag = f"l{layer}"
    gw = {}
    wire = (WIRE,)

    def dw(name, a, b):
        gw[name] = _run(plan, _mm, a, b, ta=True, out_dtypes=wire, tk=2048, name=f"d{name}_{tag}")
        plan.grad(layer, name, gw[name])

    dpe, dgt = _ple(p, w["w_pe_t"], sv["h3"], w["w_pg"], dx3, backward=True, name=f"ple_bwd_{tag}")
    dw("w_pe", dpe, p)
    dw("w_pg", sv["h3"], dgt)
    d = dx3.shape[1]
    grads = (F32, MXU, (F32, "rowsum"))
    dx2, dx2b, dg_pe = _run(plan, _mm, dgt, w["w_pg"], tb=True, extras=(sv["x2"], dx3, sv["r3"], g_pe),
                            epi=_norm_bwd, out_dtypes=grads, tm=512, tn=d, name=f"dh_pe_{tag}")
    dw("w_ff2", sv["act"], dx2b)
    du = _run(plan, _mm, dx2b, w["w_ff2"], tb=True, extras=(sv["u"],),
              epi=lambda acc, u: acc * (2.0 * jnp.maximum(u.astype(F32), 0.0)), out_dtypes=(MXU,),
              tn=2048, name=f"dact_{tag}")
    dw("w_ff1", du, sv["h2"])
    dx1, dx1b, dg_mlp = _run(plan, _mm, du, w["w_ff1_t"], extras=(sv["x1"], dx2, sv["r2"], g_mlp),
                             epi=_norm_bwd, out_dtypes=grads, tm=1024, tn=d, name=f"dh_mlp_{tag}")
    dw("w_o", sv["merged"], dx1b)
    dya, dyb, dga, dgb = _mix_bwd(dx1b, w["w_o"], sv["oa"], sv["ob"], w["w_up_a_t"], w["w_up_b_t"],
                                  sv["gates"], f"mix_bwd_{tag}")
    dw("w_up_a", dya, sv["oa"])
    dw("w_up_b", dyb, sv["ob"])
    doa = _run(plan, _mm, dya, w["w_up_a_t"], out_dtypes=(MXU,), name=f"do_a_{tag}")
    dob = _run(plan, _mm, dyb, w["w_up_b_t"], out_dtypes=(MXU,), name=f"do_b_{tag}")
    dqb, dkb, dvb, dbias, dsink = _run(plan, _swa_bwd, sv["qkv"], bias, sinks, dob, sv["lse"],
                                       name=f"swa_bwd_{tag}")
    dqa, dka, dva = _run(plan, _sb_bwd, sv["qkv"], doa, name=f"sb_bwd_{tag}")
    dqkv = jnp.concatenate([dqa, dka, dva, dqb, dkb, dvb], axis=1)
    gw_qkv = _mm(dqkv, sv["h1"], ta=True, out_dtypes=wire, tk=2048, name=f"dw_qkv_{tag}")
    gw_ga = _mm(dga, sv["h1"], ta=True, out_dtypes=wire, tk=2048, name=f"dw_ga_{tag}")
    gw_gb = _mm(dgb, sv["h1"], ta=True, out_dtypes=wire, tk=2048, name=f"dw_gb_{tag}")
    gw["w_in"] = jnp.concatenate([gw_qkv, gw_ga, gw_gb], axis=0)
    plan.grad(layer, "w_in", gw["w_in"])
    d = dga.shape[1]
    add = lambda acc, res: res + acc
    dh1 = _run(plan, _mm, dga, w["w_gate_t"][:d], name=f"dh_ga_{tag}")
    dh1 = _run(plan, _mm, dgb, w["w_gate_t"][d:], extras=(dh1,), epi=add, name=f"dh_gb_{tag}")
    dx, _, dg_mix = _run(plan, _mm, dqkv, w["w_qkv_t"], tk=768, extras=(dh1, sv["x"], dx1, sv["r1"], g_mix),
                         epi=lambda acc, prev, *rest: _norm_bwd(acc + prev, *rest), out_dtypes=grads,
                         tm=512, tn=d, name=f"dh_qkv_{tag}")
    small = dict(g_mix=dg_mix, g_mlp=dg_mlp, g_pe=dg_pe, sinks=dsink[:, 0, 0], dbias=dbias)
    return dx, gw, small


def _local_step(x, p, target, weights, g_mix, g_mlp, g_pe, g_final, sinks, rel_bias, plan=None):
    plan = _NoPlan() if plan is None else plan
    depth = g_mix.shape[0]
    buckets = jnp.asarray(_bucket_table())
    bias = _build_bias(rel_bias, buckets, "build_bias")
    saved, wfull = [], []
    h, normed = x, None
    for l in range(depth):
        wfull.append(weights(l))
        next_g = g_mix[l + 1:l + 2] if l + 1 < depth else None
        h, sv, normed = _layer_fwd(h, p[l], wfull[l], g_mix[l:l + 1], g_mlp[l:l + 1], g_pe[l:l + 1],
                                   sinks[l], bias, f"l{l}", plan, normed, next_g)
        saved.append(sv)
    loss_row, dx, dg_final = _loss_head(h, g_final[None, :], target, "loss_head")
    gws = [None] * depth
    smalls = [None] * depth
    for l in reversed(range(depth)):
        dx, gws[l], smalls[l] = _layer_bwd(dx, saved[l], p[l], wfull[l], g_mix[l:l + 1], g_mlp[l:l + 1],
                                           g_pe[l:l + 1], sinks[l], bias, l, plan)
    drel = _bias_grad([sm["dbias"] for sm in smalls], buckets, "bias_grad")[:, 0, :N_BUCKETS].T
    small = dict(
        g_mix=jnp.concatenate([sm["g_mix"] for sm in smalls], axis=0),
        g_mlp=jnp.concatenate([sm["g_mlp"] for sm in smalls], axis=0),
        g_pe=jnp.concatenate([sm["g_pe"] for sm in smalls], axis=0),
        g_final=dg_final[0],
        sinks=jnp.stack([sm["sinks"] for sm in smalls], axis=0),
        rel_bias=drel,
    )
    return loss_row, dx, gws, small


MESH_ID = pl.DeviceIdType.MESH


def _position():
    return lax.axis_index("x"), lax.axis_index("y"), lax.axis_index("c")


def _gather_comm(shards):
    n = len(shards)

    def copies(pos, x_refs, out_refs, sems):
        send_sems, recv_sems, local_sems = sems
        x, y, c = pos
        me, sibling = (x, y, c), (x, y, 1 - c)
        chips = [(1 - x, y), (x, 1 - y), (1 - x, 1 - y)]

        def slot(a, px, py, pc):
            return out_refs[a].at[4 * px + 2 * py + pc]

        def copy(a, k, block, to, src=None):
            return pltpu.make_async_remote_copy(
                src_ref=slot(a, *block) if src is None else src, dst_ref=slot(a, *block),
                send_sem=send_sems.at[a, k], recv_sem=recv_sems.at[a, k],
                device_id=to, device_id_type=MESH_ID)

        mine = [pltpu.make_async_copy(x_refs[a], slot(a, *me), local_sems.at[a]) for a in range(n)]
        first = []
        for a in range(n):
            first.append(copy(a, 0, me, sibling, src=x_refs[a]))
            first += [copy(a, 1 + j, me, (*chip, c), src=x_refs[a]) for j, chip in enumerate(chips)]
        return me, sibling, chips, copy, mine, first

    def start(pos, x_refs, out_refs, sems):
        _, _, _, _, mine, first = copies(pos, x_refs, out_refs, sems)
        for cp in mine + first:
            cp.start()

    def finish(pos, x_refs, out_refs, sems):
        me, sibling, chips, copy, mine, first = copies(pos, x_refs, out_refs, sems)
        c = pos[2]
        passed = []
        for j, chip in enumerate(chips):
            for a in range(n):
                copy(a, 1 + j, (*chip, c), me).wait_recv()
                fwd = copy(a, 4 + j, (*chip, c), sibling)
                fwd.start()
                passed.append(fwd)
        for a in range(n):
            copy(a, 0, sibling, me).wait_recv()
            for j, chip in enumerate(chips):
                copy(a, 4 + j, (*chip, 1 - c), me).wait_recv()
        for cp in first + passed:
            cp.wait_send()
        for cp in mine:
            cp.wait()

    return _Comm(shards, [jax.ShapeDtypeStruct((N_DEV,) + s.shape, s.dtype) for s in shards],
                 [pltpu.SemaphoreType.DMA((n, 7)), pltpu.SemaphoreType.DMA((n, 7)),
                  pltpu.SemaphoreType.DMA((n,))], start, finish)


def _exchange_comm(arrays, n_slots, route):
    n = len(arrays)

    def copies(pos, in_refs, out_refs, sems):
        send_sems, recv_sems = sems
        out = []
        for a in range(n):
            for j in range(n_slots):
                src_slot, peer = route(pos, j)
                out.append(pltpu.make_async_remote_copy(
                    src_ref=in_refs[a].at[src_slot], dst_ref=out_refs[a].at[j],
                    send_sem=send_sems.at[a, j], recv_sem=recv_sems.at[a, j],
                    device_id=peer, device_id_type=MESH_ID))
        return out

    def start(pos, in_refs, out_refs, sems):
        for cp in copies(pos, in_refs, out_refs, sems):
            cp.start()

    def finish(pos, in_refs, out_refs, sems):
        for cp in copies(pos, in_refs, out_refs, sems):
            cp.wait()

    return _Comm(arrays, [jax.ShapeDtypeStruct((n_slots,) + g.shape[1:], g.dtype) for g in arrays],
                 [pltpu.SemaphoreType.DMA((n, n_slots)), pltpu.SemaphoreType.DMA((n, n_slots))], start, finish)


def _rs_sibling_comm(gs):
    return _exchange_comm(gs, 4, lambda pos, j: (2 * j + (1 - pos[2]), (pos[0], pos[1], 1 - pos[2])))


def _chip_of(k, x, y):
    return x ^ ((k + 1) & 1), y ^ (((k + 1) >> 1) & 1)


def _chip_partials(pos, gs, recvs, name):
    n = len(gs)

    def body(pos_ref, *refs):
        for a in range(n):
            refs[2 * n + a][...] = (refs[a][...].astype(F32) + refs[n + a][...].astype(F32)
                                    ).astype(refs[2 * n + a].dtype)

    def g_map(k, pos_ref):
        cx, cy = _chip_of(k, pos_ref[0], pos_ref[1])
        return (4 * cx + 2 * cy + pos_ref[2], 0, 0)

    def r_map(k, pos_ref):
        cx, cy = _chip_of(k, pos_ref[0], pos_ref[1])
        return (2 * cx + cy, 0, 0)

    slab = [(None,) + g.shape[1:] for g in gs]
    return pl.pallas_call(
        body,
        grid_spec=pltpu.PrefetchScalarGridSpec(
            num_scalar_prefetch=1,
            grid=(4,),
            in_specs=[pl.BlockSpec(sh, g_map) for sh in slab] + [pl.BlockSpec(sh, r_map) for sh in slab],
            out_specs=[pl.BlockSpec(sh, lambda k, pos_ref: (k, 0, 0)) for sh in slab],
        ),
        out_shape=[jax.ShapeDtypeStruct((4,) + g.shape[1:], g.dtype) for g in gs],
        compiler_params=_cparams(),
        name=name,
    )(pos, *gs, *recvs)


def _rs_chips_comm(parts):
    return _exchange_comm(parts, 3, lambda pos, k: (k, (*_chip_of(k, pos[0], pos[1]), pos[2])))


def _adamw_math(w, g, m, v):
    m = ADAM_B1 * m + (1.0 - ADAM_B1) * g
    v = ADAM_B2 * v + (1.0 - ADAM_B2) * (g * g)
    m_hat = m / (1.0 - ADAM_B1 ** ADAM_STEP)
    v_hat = v / (1.0 - ADAM_B2 ** ADAM_STEP)
    delta = -ADAM_LR * (m_hat / (jnp.sqrt(v_hat) + ADAM_EPS) + ADAM_WD * w)
    return delta, m, v


def _adamw_weight(parts, recvs, w, m, v, name, grad_t=False, comm=None):
    depth, a, b = w.shape
    ta = _tile(a, 288, unit=LANES if grad_t else 16)
    ni = a // ta
    g_block = (b, ta) if grad_t else (ta, b)

    def body(*refs):
        p_refs, r_refs = refs[:depth], refs[depth:2 * depth]
        w_ref, m_ref, v_ref = refs[2 * depth:2 * depth + 3]
        g_out, d_out, m_out, v_out = refs[2 * depth + 3:]
        layer = pl.program_id(0)
        g = jnp.zeros(g_block, F32)
        for l in range(depth):
            gl = p_refs[l][...].astype(F32)
            for k in range(3):
                gl = gl + r_refs[l][k].astype(F32)
            g = jnp.where(layer == l, gl, g)
        if grad_t:
            g = g.T
        delta, m_new, v_new = _adamw_math(w_ref[...], g, m_ref[...], v_ref[...])
        g_out[...] = g
        d_out[...] = delta
        m_out[...] = m_new
        v_out[...] = v_new

    def hold(l):
        return lambda layer, i: jnp.where(layer == l, i, jnp.where(layer < l, 0, ni - 1))

    def g_index(slot, f):
        if grad_t:
            return lambda layer, i: (slot, 0, f(layer, i))
        return lambda layer, i: (slot, f(layer, i), 0)

    p_specs = [pl.BlockSpec((None,) + g_block, g_index(3, hold(l))) for l in range(depth)]
    r_specs = [pl.BlockSpec((3,) + g_block, g_index(0, hold(l))) for l in range(depth)]
    row = pl.BlockSpec((None, ta, b), lambda layer, i: (layer, i, 0))
    outs, couts = _call(
        body,
        grid=(depth, ni),
        in_specs=p_specs + r_specs + [row, row, row],
        out_specs=[row] * 4,
        out_shape=[jax.ShapeDtypeStruct(w.shape, F32)] * 4,
        args=(*parts, *recvs, w, m, v), name=name, comm=comm)
    return outs if comm is None else (outs, couts)


def _adamw_replicated(gathered, w, m, v, name):
    r, lanes = w.shape

    def body(g_ref, w_ref, m_ref, v_ref, g_out, d_out, m_out, v_out):
        g = g_ref[0]
        for k in range(1, N_DEV):
            g = g + g_ref[k]
        delta, m_new, v_new = _adamw_math(w_ref[...], g, m_ref[...], v_ref[...])
        g_out[...] = g
        d_out[...] = delta
        m_out[...] = m_new
        v_out[...] = v_new

    return pl.pallas_call(
        body,
        out_shape=[jax.ShapeDtypeStruct((r, lanes), F32)] * 4,
        name=name,
    )(gathered, w, m, v)


def _wire_shard(name, shard):
    return (shard.T if name in COL_SHARDED else shard).astype(WIRE)


def _full_weight(gathered):
    return gathered.reshape(N_DEV * gathered.shape[1], gathered.shape[2])


def _to_slabs(gfull):
    return gfull.reshape(N_DEV, gfull.shape[0] // N_DEV, gfull.shape[1])


def _pack_small(arrs):
    rows = []
    for a in arrs:
        flat = a.astype(F32).reshape(-1)
        pad = (-flat.shape[0]) % LANES
        rows.append(jnp.pad(flat, (0, pad)).reshape(-1, LANES))
    packed = jnp.concatenate(rows, axis=0)
    return jnp.pad(packed, ((0, (-packed.shape[0]) % 8), (0, 0)))


def _unpack_small(packed, shapes):
    out, off = [], 0
    for shp in shapes:
        n = math.prod(shp)
        rows = -(-n // LANES)
        out.append(packed[off:off + rows].reshape(-1)[:n].reshape(shp))
        off += rows
    return out


def _of(layer, *names):
    return tuple((layer, n) for n in names)


MLP_W = ("w_ff1", "w_ff2", "w_pe", "w_pg")

GATHERS = (
    ("rms_mix_l0", _of(0, "w_in")),
    ("proj_qkv_l0", _of(0, "w_up_a", "w_up_b", "w_o")),
    ("proj_gate_l0", _of(0, "w_pe", "w_pg")),
    ("sb_fwd_l0", _of(0, "w_ff1", "w_ff2")),
    ("swa_fwd_l0", _of(1, "w_in")),
    ("sb_fwd_l1", _of(1, "w_up_a", "w_up_b", "w_o", "w_pe", "w_pg", "w_ff1")),
    ("swa_fwd_l1", _of(1, "w_ff2")),
)
REDUCES = (
    (_of(1, "w_ff1"), "dw_ff2_l0", "dact_l0"),
    (_of(1, "w_ff2"), "dw_ff2_l0", "dw_ff1_l0"),
    (_of(1, "w_in"), "dw_ff2_l0", "sb_bwd_l0"),
    (_of(1, "w_up_a", "w_up_b", "w_o", "w_pe", "w_pg"), "dw_ff2_l0", "swa_bwd_l0"),
    (_of(0, *MLP_W), "dh_mlp_l0", "sb_bwd_l0"),
    (_of(0, "w_o", "w_up_a", "w_up_b"), "do_a_l0", "sb_bwd_l0"),
    (_of(0, "w_in"), "dh_ga_l0", "dh_qkv_l0"),
)


def _merge_comms(comms):
    if len(comms) == 1:
        return comms[0]

    def cuts(counts):
        edges = [0]
        for c in counts:
            edges.append(edges[-1] + c)
        return [slice(a, b) for a, b in zip(edges[:-1], edges[1:])]

    s_in = cuts([len(c.inputs) for c in comms])
    s_out = cuts([len(c.out_shapes) for c in comms])
    s_sem = cuts([len(c.sems) for c in comms])

    def start(pos, cin, cout, csem):
        for c, i, o, s in zip(comms, s_in, s_out, s_sem):
            c.start(pos, cin[i], cout[o], csem[s])

    def finish(pos, cin, cout, csem):
        for c, i, o, s in zip(comms, s_in, s_out, s_sem):
            c.finish(pos, cin[i], cout[o], csem[s])

    return _Comm(sum([c.inputs for c in comms], []), sum([c.out_shapes for c in comms], []),
                 sum([c.sems for c in comms], []), start, finish)


class _LayerWeights:
    def __init__(self, full, layer):
        self.full, self.layer, self.cache = full, layer, {}

    def __getitem__(self, name):
        if name not in self.cache:
            if name == "w_qkv_t":
                self.cache[name] = self.full[(self.layer, "w_in")][:QKV_COLS]
            elif name == "w_gate_t":
                self.cache[name] = self.full[(self.layer, "w_in")][QKV_COLS:]
            else:
                base = name[:-2] if name.endswith("_t") else name
                assert (base in COL_SHARDED) == name.endswith("_t"), name
                self.cache[name] = self.full[(self.layer, base)]
        return self.cache[name]


class _Plan:
    def __init__(self, w_sh, pos):
        self.w_sh = dict(zip(WEIGHTS, w_sh))
        self.pos = pos
        self.full, self.gw, self.parts, self.recv = {}, {}, {}, {}
        self.slabs = {}
        self.hosted = {}
        for i, (host, _) in enumerate(GATHERS):
            self.hosted.setdefault(host, []).append(("gather", i))
        for i, (_, sib_host, chip_host) in enumerate(REDUCES):
            self.hosted.setdefault(sib_host, []).append(("sibling", i))
            self.hosted.setdefault(chip_host, []).append(("chips", i))

    def _gather(self, i):
        return _gather_comm([_wire_shard(n, self.w_sh[n][layer]) for layer, n in GATHERS[i][1]])

    def _gathered(self, i, outs):
        for (layer, n), g in zip(GATHERS[i][1], outs):
            self.full[(layer, n)] = _full_weight(g)

    def weights(self, layer):
        return _LayerWeights(self.full, layer)

    def grad(self, layer, name, value):
        self.gw[(layer, name)] = value

    def _sibling(self, i):
        self.slabs[i] = [_to_slabs(self.gw[item]) for item in REDUCES[i][0]]
        return _rs_sibling_comm(self.slabs[i])

    def _sibling_done(self, i, outs):
        parts = _chip_partials(self.pos, self.slabs[i], outs, f"chip_partials_{i}")
        for item, part in zip(REDUCES[i][0], parts):
            self.parts[item] = part

    def _chips(self, i):
        return _rs_chips_comm([self.parts[item] for item in REDUCES[i][0]])

    def _chips_done(self, i, outs):
        for item, r in zip(REDUCES[i][0], outs):
            self.recv[item] = r

    def comm(self, name):
        if name not in self.hosted:
            return None
        make = {"gather": self._gather, "sibling": self._sibling, "chips": self._chips}
        return _merge_comms([make[kind](i) for kind, i in self.hosted[name]])

    def done(self, name, outs):
        took = {"gather": self._gathered, "sibling": self._sibling_done, "chips": self._chips_done}
        off = 0
        for kind, i in self.hosted[name]:
            n = len(GATHERS[i][1]) if kind == "gather" else len(REDUCES[i][0])
            took[kind](i, outs[off:off + n])
            off += n


def kernel(x, p, w_in, w_up_a, w_up_b, w_o, w_ff1, w_ff2, w_pe, w_pg, g_mix, g_mlp, g_pe, g_final, sinks, rel_bias, loss_target, m_w_in, m_w_up_a, m_w_up_b, m_w_o, m_w_ff1, m_w_ff2, m_w_pe, m_w_pg, m_g_mix, m_g_mlp, m_g_pe, m_g_final, m_sinks, m_rel_bias, v_w_in, v_w_up_a, v_w_up_b, v_w_o, v_w_ff1, v_w_ff2, v_w_pe, v_w_pg, v_g_mix, v_g_mlp, v_g_pe, v_g_final, v_sinks, v_rel_bias):
    w_sh = [w_in, w_up_a, w_up_b, w_o, w_ff1, w_ff2, w_pe, w_pg]
    m_sh = [m_w_in, m_w_up_a, m_w_up_b, m_w_o, m_w_ff1, m_w_ff2, m_w_pe, m_w_pg]
    v_sh = [v_w_in, v_w_up_a, v_w_up_b, v_w_o, v_w_ff1, v_w_ff2, v_w_pe, v_w_pg]
    depth = w_in.shape[0]
    assert depth == 2 and x.shape[-1] * 2 + QKV_COLS == w_in.shape[2] * N_DEV

    px, py, pc = _position()
    plan = _Plan(w_sh, jnp.stack([px, py, pc]).astype(jnp.int32))
    loss_row, grad_x, _, small = _local_step(
        x[0], p[:, 0], loss_target[0], plan.weights, g_mix, g_mlp, g_pe, g_final, sinks, rel_bias, plan=plan)

    small_g = _pack_small([small[n] for n in SMALL] + [loss_row[0, :1]])
    grad_w, delta_w, new_m, new_v = [], [], [], []
    for a, name in enumerate(WEIGHTS):
        parts = [plan.parts[(l, name)] for l in range(depth)]
        recvs = [plan.recv[(l, name)] for l in range(depth)]
        if name == "w_in":
            flip = lambda t: t.transpose(0, 2, 1)
            outs, (small_all,) = _adamw_weight(parts, recvs, flip(w_sh[a]), flip(m_sh[a]), flip(v_sh[a]),
                                               f"adamw_{name}", comm=_gather_comm([small_g]))
            outs = [flip(o) for o in outs]
        else:
            outs = _adamw_weight(parts, recvs, w_sh[a], m_sh[a], v_sh[a], f"adamw_{name}",
                                 grad_t=name in COL_SHARDED)
        for lst, o in zip((grad_w, delta_w, new_m, new_v), outs):
            lst.append(o)

    small_w = [g_mix, g_mlp, g_pe, g_final, sinks, rel_bias]
    small_m = [m_g_mix, m_g_mlp, m_g_pe, m_g_final, m_sinks, m_rel_bias]
    small_v = [v_g_mix, v_g_mlp, v_g_pe, v_g_final, v_sinks, v_rel_bias]
    small_shapes = [a.shape for a in small_w] + [(1,)]
    zero = jnp.zeros((1,), F32)
    packed_s = _adamw_replicated(small_all, _pack_small(small_w + [zero]), _pack_small(small_m + [zero]),
                                 _pack_small(small_v + [zero + 1.0]), "adamw_replicated")
    sg, sd, sm, sv = [_unpack_small(t, small_shapes) for t in packed_s]
    loss = sg[-1][0]

    return (loss, grad_x[None], *grad_w, *sg[:-1], *delta_w, *sd[:-1], *new_m, *sm[:-1], *new_v, *sv[:-1])
```

```python
import functools
import math

import numpy as np
import jax
import jax.numpy as jnp
from jax import lax
from jax.experimental import pallas as pl
from jax.experimental.pallas import tpu as pltpu

F32 = jnp.float32
MXU = jnp.bfloat16
WIRE = jnp.bfloat16

HEAD_DIM = 64
SB_HEADS = 8
SW_HEADS = 8
SW_KV = 2
SW_GROUP = SW_HEADS // SW_KV
BLOCK = 128
N_BUCKETS = 32
MAX_DISTANCE = 128
EPS = 1e-6
SCALE = HEAD_DIM ** -0.5
SB_W = SB_HEADS * HEAD_DIM
SW_W = SW_HEADS * HEAD_DIM
QKV_COLS = 3 * SB_W + SW_W + 2 * SW_KV * HEAD_DIM
N_DEV = 8
LANES = 128
N_PAIR = SB_HEADS // 2
NEG = -1e30

ADAM_LR = 0.001
ADAM_B1 = 0.9
ADAM_B2 = 0.999
ADAM_EPS = 1e-08
ADAM_WD = 0.01
ADAM_STEP = 10

VMEM_LIMIT = 48 * 1024 * 1024
SB_TQ = 256
SB_DEAD = -105.0
SB_SUB = 4

WEIGHTS = ("w_in", "w_up_a", "w_up_b", "w_o", "w_ff1", "w_ff2", "w_pe", "w_pg")
COL_SHARDED = ("w_in", "w_up_a", "w_up_b", "w_ff1", "w_pe")
SMALL = ("g_mix", "g_mlp", "g_pe", "g_final", "sinks", "rel_bias")


def _cparams(**kw):
    return pltpu.CompilerParams(vmem_limit_bytes=VMEM_LIMIT, **kw)


def _dot(a, b):
    return jnp.dot(a, b, preferred_element_type=F32)


def _dot_nt(a, b):
    return lax.dot_general(a, b, (((1,), (1,)), ((), ())), preferred_element_type=F32)


def _dot_tn(a, b):
    return lax.dot_general(a, b, (((0,), (0,)), ((), ())), preferred_element_type=F32)


def _tile(n, target, unit=LANES):
    if n <= target:
        return n
    t = (target // unit) * unit
    while t > unit and n % t:
        t -= unit
    assert n % t == 0, (n, target)
    return t


def _sigmoid(x):
    return 0.5 * jnp.tanh(0.5 * x) + 0.5


class _Comm:
    def __init__(self, inputs, out_shapes, sems, start, finish):
        self.inputs, self.out_shapes, self.sems = list(inputs), list(out_shapes), list(sems)
        self.start, self.finish = start, finish


def _call(body, *, grid, in_specs, out_specs, out_shape, scratch_shapes=(), args, name, comm=None):
    n_in, n_out, n_scr = len(in_specs), len(out_shape), len(scratch_shapes)
    if comm is None:
        outs = pl.pallas_call(body, grid=grid, in_specs=list(in_specs), out_specs=list(out_specs),
                              out_shape=list(out_shape), scratch_shapes=list(scratch_shapes),
                              compiler_params=_cparams(), name=name)(*args)
        return list(outs), None
    ci, co = len(comm.inputs), len(comm.out_shapes)
    any_spec = pl.BlockSpec(memory_space=pl.ANY)

    def wrapped(*refs):
        ins, cin = refs[:n_in], refs[n_in:n_in + ci]
        o0 = n_in + ci
        outs, cout = refs[o0:o0 + n_out], refs[o0 + n_out:o0 + n_out + co]
        s0 = o0 + n_out + co
        scr, csem = refs[s0:s0 + n_scr], refs[s0 + n_scr:]
        ids = [pl.program_id(d) for d in range(len(grid))]
        first = functools.reduce(jnp.logical_and, [i == 0 for i in ids])
        last = functools.reduce(jnp.logical_and, [i == g - 1 for i, g in zip(ids, grid)])
        pos = (lax.axis_index("x"), lax.axis_index("y"), lax.axis_index("c"))

        @pl.when(first)
        def _():
            comm.start(pos, cin, cout, csem)

        body(*ins, *outs, *scr)

        @pl.when(last)
        def _():
            comm.finish(pos, cin, cout, csem)

    outs = pl.pallas_call(wrapped, grid=grid, in_specs=list(in_specs) + [any_spec] * ci,
                          out_specs=list(out_specs) + [any_spec] * co,
                          out_shape=list(out_shape) + comm.out_shapes,
                          scratch_shapes=list(scratch_shapes) + comm.sems,
                          compiler_params=_cparams(), name=name)(*args, *comm.inputs)
    return list(outs[:n_out]), list(outs[n_out:])


def _accumulate(o_ref, value, first):
    @pl.when(first)
    def _():
        o_ref[...] = value

    @pl.when(jnp.logical_not(first))
    def _():
        o_ref[...] += value


def _mm(a, b, *, ta=False, tb=False, extras=(), epi=None, out_dtypes=(F32,),
        tm=1024, tn=1024, tk=1024, name, comm=None):
    if ta:
        kdim, m = a.shape
    else:
        m, kdim = a.shape
    n = b.shape[0] if tb else b.shape[1]
    assert (b.shape[1] if tb else b.shape[0]) == kdim
    tm, tn, tk = _tile(m, tm), _tile(n, tn), _tile(kdim, tk)
    nk = kdim // tk
    n_ex, n_out = len(extras), len(out_dtypes)

    a_spec = (pl.BlockSpec((tk, tm), lambda i, j, k: (k, i)) if ta
              else pl.BlockSpec((tm, tk), lambda i, j, k: (i, k)))
    b_spec = (pl.BlockSpec((tn, tk), lambda i, j, k: (j, k)) if tb
              else pl.BlockSpec((tk, tn), lambda i, j, k: (k, j)))
    ex_specs = []
    for e in extras:
        assert e.shape in ((m, n), (1, n), (m, 1)), (e.shape, m, n)
        if e.shape == (m, n):
            ex_specs.append(pl.BlockSpec((tm, tn), lambda i, j, k: (i, j)))
        elif e.shape[0] == 1:
            ex_specs.append(pl.BlockSpec((1, tn), lambda i, j, k: (0, j)))
        else:
            ex_specs.append(pl.BlockSpec((tm, 1), lambda i, j, k: (i, 0)))
    out_specs, out_shape, row_sums = [], [], []
    for dt in out_dtypes:
        kind = dt[1] if isinstance(dt, tuple) else "tile"
        row_sums.append(kind == "rowsum")
        if kind == "tile":
            out_specs.append(pl.BlockSpec((tm, tn), lambda i, j, k: (i, j)))
            out_shape.append(jax.ShapeDtypeStruct((m, n), dt))
            continue
        assert tn == n, "per-row and summed outputs need whole rows in one tile"
        if kind == "col":
            out_specs.append(pl.BlockSpec((tm, 1), lambda i, j, k: (i, 0)))
            out_shape.append(jax.ShapeDtypeStruct((m, 1), dt[0]))
        else:
            out_specs.append(pl.BlockSpec((1, tn), lambda i, j, k: (0, 0)))
            out_shape.append(jax.ShapeDtypeStruct((1, n), dt[0]))

    def body(a_ref, b_ref, *rest):
        ex_refs = rest[:n_ex]
        out_refs = rest[n_ex:n_ex + n_out]
        acc = rest[-1]
        k = pl.program_id(2)
        first_rows = pl.program_id(0) == 0

        def prod():
            av = a_ref[...].astype(MXU)
            bv = b_ref[...].astype(MXU)
            return _dot_tn(av, bv) if ta else (_dot_nt(av, bv) if tb else _dot(av, bv))

        def finish(res):
            if epi is not None:
                res = epi(res, *[e[...] for e in ex_refs])
            if not isinstance(res, tuple):
                res = (res,)
            for o_ref, r, summed in zip(out_refs, res, row_sums):
                if summed:
                    _accumulate(o_ref, r.astype(o_ref.dtype), first_rows)
                else:
                    o_ref[...] = r.astype(o_ref.dtype)

        if nk == 1:
            finish(prod())
            return

        @pl.when(k == 0)
        def _():
            acc[...] = prod()

        @pl.when(jnp.logical_and(k > 0, k < nk - 1))
        def _():
            acc[...] += prod()

        @pl.when(k == nk - 1)
        def _():
            finish(acc[...] + prod())

    outs, couts = _call(
        body,
        grid=(m // tm, n // tn, nk),
        in_specs=[a_spec, b_spec] + ex_specs,
        out_specs=out_specs,
        out_shape=out_shape,
        scratch_shapes=[pltpu.VMEM((tm, tn), F32)],
        args=(a, b, *extras), name=name, comm=comm)
    res = outs[0] if n_out == 1 else tuple(outs)
    return res if comm is None else (res, couts)


def _rms_fwd(x, g, name, comm=None):
    s, d = x.shape
    tr = _tile(s, 256)

    def body(x_ref, g_ref, h_ref, r_ref):
        xf = x_ref[...]
        r = lax.rsqrt(jnp.mean(xf * xf, axis=-1, keepdims=True) + EPS)
        h_ref[...] = ((xf * r) * g_ref[...]).astype(h_ref.dtype)
        r_ref[...] = r

    outs, couts = _call(
        body,
        grid=(s // tr,),
        in_specs=[pl.BlockSpec((tr, d), lambda i: (i, 0)), pl.BlockSpec((1, d), lambda i: (0, 0))],
        out_specs=[pl.BlockSpec((tr, d), lambda i: (i, 0)), pl.BlockSpec((tr, 1), lambda i: (i, 0))],
        out_shape=[jax.ShapeDtypeStruct((s, d), MXU), jax.ShapeDtypeStruct((s, 1), F32)],
        args=(x, g), name=name, comm=comm)
    return tuple(outs) if comm is None else (tuple(outs), couts)


def _loss_head(x, g, target, name):
    s, d = x.shape
    tr = _tile(s, 256)

    def body(x_ref, g_ref, t_ref, loss_ref, dx_ref, dg_ref):
        @pl.when(pl.program_id(0) == 0)
        def _():
            dg_ref[...] = jnp.zeros_like(dg_ref)
            loss_ref[...] = jnp.zeros_like(loss_ref)

        xf = x_ref[...]
        gv = g_ref[...]
        r = lax.rsqrt(jnp.mean(xf * xf, axis=-1, keepdims=True) + EPS)
        xhat = xf * r
        err = xhat * gv - t_ref[...]
        loss_ref[...] += 0.5 * jnp.sum(jnp.mean(err * err, axis=-1, keepdims=True), axis=0, keepdims=True)
        dy = err * (1.0 / d)
        dxhat = dy * gv
        mean = jnp.mean(dxhat * xhat, axis=-1, keepdims=True)
        dx_ref[...] = r * (dxhat - xhat * mean)
        dg_ref[...] += jnp.sum(dy * xhat, axis=0, keepdims=True)

    row = pl.BlockSpec((tr, d), lambda i: (i, 0))
    vec = pl.BlockSpec((1, d), lambda i: (0, 0))
    return pl.pallas_call(
        body,
        grid=(s // tr,),
        in_specs=[row, vec, row],
        out_specs=[pl.BlockSpec((1, LANES), lambda i: (0, 0)), row, vec],
        out_shape=[jax.ShapeDtypeStruct((1, LANES), F32), jax.ShapeDtypeStruct((s, d), F32),
                   jax.ShapeDtypeStruct((1, d), F32)],
        compiler_params=_cparams(),
        name=name,
    )(x, g, target)


def _mix_fwd(oa, ob, wa_t, wb_t, gates, name):
    s, kd = oa.shape
    d = wa_t.shape[0]
    tm, tn = _tile(s, 1024), _tile(d, 1024)
    nj = d // tn

    def body(oa_ref, ob_ref, wa_ref, wb_ref, ga_ref, gb_ref, out_ref):
        ya = _dot_nt(oa_ref[...], wa_ref[...])
        yb = _dot_nt(ob_ref[...], wb_ref[...])
        out_ref[...] = (_sigmoid(ga_ref[...].astype(F32)) * ya
                        + _sigmoid(gb_ref[...].astype(F32)) * yb).astype(out_ref.dtype)

    o_spec = pl.BlockSpec((tm, kd), lambda i, j: (i, 0))
    w_spec = pl.BlockSpec((tn, kd), lambda i, j: (j, 0))
    return pl.pallas_call(
        body,
        grid=(s // tm, nj),
        in_specs=[o_spec, o_spec, w_spec, w_spec,
                  pl.BlockSpec((tm, tn), lambda i, j: (i, j)),
                  pl.BlockSpec((tm, tn), lambda i, j: (i, j + nj))],
        out_specs=pl.BlockSpec((tm, tn), lambda i, j: (i, j)),
        out_shape=jax.ShapeDtypeStruct((s, d), MXU),
        compiler_params=_cparams(),
        name=name,
    )(oa, ob, wa_t, wb_t, gates, gates)


def _mix_bwd(dx, w_o, oa, ob, wa_t, wb_t, gates, name):
    s, kd = oa.shape
    d = wa_t.shape[0]
    tm, tn = _tile(s, 512), _tile(d, 1024)
    nj = d // tn

    def body(dx_ref, wo_ref, oa_ref, ob_ref, wa_ref, wb_ref, ga_ref, gb_ref,
             dya_ref, dyb_ref, dga_ref, dgb_ref):
        dm = _dot_nt(dx_ref[...], wo_ref[...])
        ya = _dot_nt(oa_ref[...], wa_ref[...])
        yb = _dot_nt(ob_ref[...], wb_ref[...])
        sa = _sigmoid(ga_ref[...].astype(F32))
        sb = _sigmoid(gb_ref[...].astype(F32))
        dya_ref[...] = (dm * sa).astype(dya_ref.dtype)
        dyb_ref[...] = (dm * sb).astype(dyb_ref.dtype)
        dga_ref[...] = (dm * ya * sa * (1.0 - sa)).astype(dga_ref.dtype)
        dgb_ref[...] = (dm * yb * sb * (1.0 - sb)).astype(dgb_ref.dtype)

    o_spec = pl.BlockSpec((tm, kd), lambda i, j: (i, 0))
    w_spec = pl.BlockSpec((tn, kd), lambda i, j: (j, 0))
    t_spec = pl.BlockSpec((tm, tn), lambda i, j: (i, j))
    return pl.pallas_call(
        body,
        grid=(s // tm, nj),
        in_specs=[pl.BlockSpec((tm, d), lambda i, j: (i, 0)),
                  pl.BlockSpec((tn, d), lambda i, j: (j, 0)),
                  o_spec, o_spec, w_spec, w_spec, t_spec,
                  pl.BlockSpec((tm, tn), lambda i, j: (i, j + nj))],
        out_specs=[t_spec] * 4,
        out_shape=[jax.ShapeDtypeStruct((s, d), MXU)] * 4,
        compiler_params=_cparams(),
        name=name,
    )(dx, w_o, oa, ob, wa_t, wb_t, gates, gates)


def _ple(p, w_pe_t, h, w_pg, other, *, backward, name, next_g=None):
    s, kp = p.shape
    d = w_pe_t.shape[0]
    tm, tn = _tile(s, 1024), _tile(d, 1024)
    with_norm = next_g is not None
    assert not (with_norm and (backward or tn != d))

    def body(p_ref, wpe_ref, h_ref, wpg_ref, other_ref, *rest):
        out_refs = rest[1:] if with_norm else rest
        pe = _dot_nt(p_ref[...].astype(MXU), wpe_ref[...])
        gt = _dot(h_ref[...], wpg_ref[...])
        sg = _sigmoid(gt)
        if backward:
            dout = other_ref[...]
            out_refs[0][...] = (dout * sg).astype(out_refs[0].dtype)
            out_refs[1][...] = (dout * pe * sg * (1.0 - sg)).astype(out_refs[1].dtype)
        elif with_norm:
            x_new, h_new, r_new = _residual_norm(pe * sg, other_ref[...], rest[0][...])
            out_refs[0][...] = x_new
            out_refs[1][...] = h_new.astype(out_refs[1].dtype)
            out_refs[2][...] = r_new
        else:
            out_refs[0][...] = other_ref[...] + pe * sg

    t_spec = pl.BlockSpec((tm, tn), lambda i, j: (i, j))
    if backward:
        out_specs, out_shape = [t_spec, t_spec], [jax.ShapeDtypeStruct((s, d), MXU)] * 2
    else:
        out_specs, out_shape = [t_spec], [jax.ShapeDtypeStruct((s, d), F32)]
    in_specs = [pl.BlockSpec((tm, kp), lambda i, j: (i, 0)),
                pl.BlockSpec((tn, kp), lambda i, j: (j, 0)),
                pl.BlockSpec((tm, d), lambda i, j: (i, 0)),
                pl.BlockSpec((d, tn), lambda i, j: (0, j)),
                t_spec]
    args = [p, w_pe_t, h, w_pg, other]
    if with_norm:
        in_specs.append(pl.BlockSpec((1, tn), lambda i, j: (0, j)))
        args.append(next_g)
        out_specs += [t_spec, pl.BlockSpec((tm, 1), lambda i, j: (i, 0))]
        out_shape += [jax.ShapeDtypeStruct((s, d), MXU), jax.ShapeDtypeStruct((s, 1), F32)]
    outs = pl.pallas_call(
        body,
        grid=(s // tm, d // tn),
        in_specs=in_specs,
        out_specs=out_specs,
        out_shape=out_shape,
        compiler_params=_cparams(),
        name=name,
    )(*args)
    return tuple(outs) if (backward or with_norm) else outs[0]


def _split_dot(x, tri):
    hi = x.astype(jnp.bfloat16)
    lo = (x - hi.astype(F32)).astype(jnp.bfloat16)
    return _dot(hi, tri) + _dot(lo, tri)


def _log_sigmoids(z):
    lb = jnp.minimum(z, 0.0) - jnp.log(1.0 + jnp.exp(-jnp.abs(z)))
    return lb, lb - z


def _head_lanes(hh):
    lane = lax.broadcasted_iota(jnp.int32, (1, LANES), 1)
    return jnp.logical_and(lane >= hh * HEAD_DIM, lane < (hh + 1) * HEAD_DIM)


def _sb_fwd(qkv, name, comm=None):
    s = qkv.shape[0]
    tq = _tile(s, SB_TQ)
    nsub = SB_SUB if (s // tq) % SB_SUB == 0 else 1

    def body(q_ref, k_ref, v_ref, o_ref):
        row = lax.broadcasted_iota(jnp.int32, (tq, tq), 0)
        col = lax.broadcasted_iota(jnp.int32, (tq, tq), 1)
        causal = col < row
        tri = jnp.where(row > col, 1.0, 0.0).astype(jnp.bfloat16)
        started = [_sb_fwd_straight(q_ref, k_ref, v_ref, pl.program_id(1) * nsub + sub, sub, tq, causal, tri)
                   for sub in range(nsub)]
        for sub, (block, i, cs, accs) in enumerate(started):
            def top(cs):
                return jnp.maximum(jnp.max(cs[0]), jnp.max(cs[1]))

            def live(st):
                return jnp.logical_and(st[0] >= 0, st[1] > SB_DEAD)

            def walk(st, block=block):
                cs, accs = block(st[0], st[2], st[3], False)
                return st[0] - 1, top(cs), cs, accs

            accs = lax.while_loop(live, walk, (i - 2, top(cs), cs, accs))[3]
            o_ref[sub * tq:(sub + 1) * tq, :] = jnp.where(_head_lanes(0), accs[0], accs[1]).astype(o_ref.dtype)

    outs, couts = _call(
        body,
        grid=(N_PAIR, s // (nsub * tq)),
        in_specs=[pl.BlockSpec((nsub * tq, LANES), lambda p, i: (i, p)),
                  pl.BlockSpec((s, LANES), lambda p, i: (0, N_PAIR + p)),
                  pl.BlockSpec((s, LANES), lambda p, i: (0, 2 * N_PAIR + p))],
        out_specs=[pl.BlockSpec((nsub * tq, LANES), lambda p, i: (i, p))],
        out_shape=[jax.ShapeDtypeStruct((s, SB_W), MXU)],
        args=(qkv, qkv, qkv), name=name, comm=comm)
    return outs[0] if comm is None else (outs[0], couts)


def _sb_fwd_straight(q_ref, k_ref, v_ref, i, sub, tq, causal, tri):
    qf = q_ref[sub * tq:(sub + 1) * tq, :].astype(F32) * SCALE
    qms = [jnp.where(_head_lanes(hh), qf, 0.0).astype(MXU) for hh in range(2)]

    def block(kb, cs, accs, masked, present=None):
        rows = pl.ds(pl.multiple_of(kb * tq, tq), tq)
        ks, vs = k_ref[rows, :], v_ref[rows, :]
        off = 0.0 if present is None else (1.0 - present) * NEG
        new_c, new_acc = [], []
        for hh in range(2):
            lb, lm = _log_sigmoids(_dot_nt(qms[hh], ks))
            if masked:
                lm = jnp.where(causal, lm, 0.0)
            a = jnp.exp(lb + _split_dot(lm, tri) + (cs[hh] + off))
            if masked:
                a = jnp.where(causal, a, 0.0)
            new_acc.append(accs[hh] + _dot(a.astype(MXU), vs))
            row_sum = jnp.sum(lm, axis=1, keepdims=True)
            new_c.append(cs[hh] + (row_sum if present is None else row_sum * present))
        return tuple(new_c), tuple(new_acc)

    zc, za = jnp.zeros((tq, 1), F32), jnp.zeros((tq, LANES), F32)
    cs, accs = block(i, (zc, zc), (za, za), True)
    cs, accs = block(jnp.maximum(i - 1, 0), cs, accs, False, jnp.where(i > 0, 1.0, 0.0))
    return block, i, cs, accs


def _sb_bwd(qkv, do, name, comm=None):
    s = qkv.shape[0]
    tq = _tile(s, SB_TQ)
    nq = s // tq
    nsub = SB_SUB if nq % SB_SUB == 0 else 1
    nsteps = nq // nsub

    def body(q_ref, k_ref, v_ref, do_ref, dq_ref, dk_ref, dv_ref, dk_acc, dv_acc, carries):
        step = pl.program_id(1)

        @pl.when(step == 0)
        def _():
            dk_acc[...] = jnp.zeros_like(dk_acc)
            dv_acc[...] = jnp.zeros_like(dv_acc)

        row = lax.broadcasted_iota(jnp.int32, (tq, tq), 0)
        col = lax.broadcasted_iota(jnp.int32, (tq, tq), 1)
        causal = col < row
        tri_rev = jnp.where(row > col, 1.0, 0.0).astype(jnp.bfloat16)
        tri_excl = jnp.where(row < col, 1.0, 0.0).astype(jnp.bfloat16)
        zc, za = jnp.zeros((tq, 1), F32), jnp.zeros((tq, LANES), F32)

        def top(cs):
            return jnp.maximum(jnp.max(cs[0]), jnp.max(cs[1]))

        def live(st):
            return jnp.logical_and(st[0] >= 0, st[1] > SB_DEAD)

        def row_sums(pre):
            return [jnp.sum(lm, axis=1, keepdims=True) for _, lm in pre]

        def query_block(sub):
            i = step * nsub + sub
            q_rows = slice(sub * tq, (sub + 1) * tq)
            qf = q_ref[q_rows, :].astype(F32) * SCALE
            dof = do_ref[q_rows, :]
            qms = [jnp.where(_head_lanes(hh), qf, 0.0).astype(MXU) for hh in range(2)]
            doms = [jnp.where(_head_lanes(hh), dof, jnp.zeros_like(dof)) for hh in range(2)]

            def terms(kb, masked):
                rows = pl.ds(pl.multiple_of(kb * tq, tq), tq)
                ks = k_ref[rows, :]
                out = []
                for hh in range(2):
                    lb, lm = _log_sigmoids(_dot_nt(qms[hh], ks))
                    if masked:
                        lm = jnp.where(causal, lm, 0.0)
                    out.append((lb, lm))
                return out

            def block(kb, cs, gpres, dqs, masked, gate=None, pre=None):
                rows = pl.ds(pl.multiple_of(kb * tq, tq), tq)
                ks, vs = k_ref[rows, :], v_ref[rows, :]
                pre = terms(kb, masked) if pre is None else pre
                new_g, new_dq = [], []
                dk_add, dv_add = None, None
                for hh in range(2):
                    lb, lm = pre[hh]
                    off = 0.0 if gate is None else (1.0 - gate) * NEG
                    a = jnp.exp(lb + _split_dot(lm, tri_rev) + (cs[hh] + off))
                    if masked:
                        a = jnp.where(causal, a, 0.0)
                    g = a * _dot_nt(doms[hh], vs)
                    gsum = gpres[hh] + _split_dot(g, tri_excl)
                    dz = g - (g + gsum) * jnp.exp(lb)
                    if masked:
                        dz = jnp.where(causal, dz, 0.0)
                    dzb = dz.astype(MXU)
                    new_dq.append(dqs[hh] + _dot(dzb, ks))
                    dk_h = _dot_tn(dzb, qms[hh])
                    dv_h = _dot_tn(a.astype(MXU), doms[hh])
                    dk_add = dk_h if dk_add is None else dk_add + dk_h
                    dv_add = dv_h if dv_add is None else dv_add + dv_h
                    new_g.append(gpres[hh] + jnp.sum(g, axis=1, keepdims=True))
                dk_acc[rows, :] += dk_add
                dv_acc[rows, :] += dv_add
                return tuple(new_g), tuple(new_dq)

            prev = jnp.maximum(i - 1, 0)
            gate = jnp.where(i > 0, 1.0, 0.0)
            t_diag, t_prev = terms(i, True), terms(prev, False)
            c_diag = row_sums(t_diag)
            sums = row_sums(t_prev)
            c_prev = tuple(c_diag[hh] + sums[hh] * gate for hh in range(2))
            return dict(i=i, prev=prev, gate=gate, q_rows=q_rows, terms=terms, block=block,
                        t_diag=t_diag, t_prev=t_prev, c_diag=c_diag, c_prev=c_prev)

        blocks = [query_block(sub) for sub in range(nsub)]
        for qb in blocks:
            def record(st, qb=qb):
                kb, cs = st[0], st[2]
                sums = row_sums(qb["terms"](kb, False))
                for hh in range(2):
                    carries[hh, kb] = cs[hh]
                cs = tuple(cs[hh] + sums[hh] for hh in range(2))
                return kb - 1, top(cs), cs

            first = lax.while_loop(live, record, (qb["i"] - 2, top(qb["c_prev"]), qb["c_prev"]))[0] + 1
            qb["mid"] = lax.fori_loop(
                first, qb["i"] - 1,
                lambda kb, cr, qb=qb: qb["block"](kb, (carries[0, kb], carries[1, kb]), cr[0], cr[1], False),
                ((zc, zc), (za, za)))
        for qb in blocks:
            gpres, dqs = qb["block"](qb["prev"], qb["c_diag"], *qb["mid"], False, qb["gate"], qb["t_prev"])
            dqs = qb["block"](qb["i"], (zc, zc), gpres, dqs, True, None, qb["t_diag"])[1]
            dq_ref[qb["q_rows"], :] = (jnp.where(_head_lanes(0), dqs[0], dqs[1]) * SCALE).astype(dq_ref.dtype)

        @pl.when(step == nsteps - 1)
        def _():
            dk_ref[...] = dk_acc[...].astype(dk_ref.dtype)
            dv_ref[...] = dv_acc[...].astype(dv_ref.dtype)

    blk = pl.BlockSpec((nsub * tq, LANES), lambda p, i: (i, p))
    full = pl.BlockSpec((s, LANES), lambda p, i: (0, p))
    outs, couts = _call(
        body,
        grid=(N_PAIR, nsteps),
        in_specs=[blk,
                  pl.BlockSpec((s, LANES), lambda p, i: (0, N_PAIR + p)),
                  pl.BlockSpec((s, LANES), lambda p, i: (0, 2 * N_PAIR + p)),
                  blk],
        out_specs=[blk, full, full],
        out_shape=[jax.ShapeDtypeStruct((s, SB_W), MXU)] * 3,
        scratch_shapes=[pltpu.VMEM((s, LANES), F32), pltpu.VMEM((s, LANES), F32),
                        pltpu.VMEM((2, nq, tq, 1), F32)],
        args=(qkv, qkv, qkv, do), name=name, comm=comm)
    return tuple(outs) if comm is None else (tuple(outs), couts)


def _bucket_table():
    i = np.arange(BLOCK)[:, None]
    j = np.arange(2 * BLOCK)[None, :]
    d = np.maximum(BLOCK + i - j, 0)
    max_exact = N_BUCKETS // 2
    df = np.maximum(d, 1).astype(np.float32)
    large = max_exact + (np.log(df / max_exact) / math.log(MAX_DISTANCE / max_exact)
                         * (N_BUCKETS - max_exact)).astype(np.int32)
    large = np.minimum(large, N_BUCKETS - 1)
    return np.where(d < max_exact, d, large).astype(np.int32)


def _build_bias(rel_bias, buckets, name):
    def body(rb_ref, bk_ref, out_ref):
        h = pl.program_id(0)
        bk = bk_ref[...]
        acc = jnp.zeros(bk.shape, F32)
        for b in range(N_BUCKETS):
            acc = jnp.where(bk == b, rb_ref[b, h], acc)
        out_ref[...] = acc

    return pl.pallas_call(
        body,
        grid=(SW_HEADS,),
        in_specs=[pl.BlockSpec(memory_space=pltpu.SMEM),
                  pl.BlockSpec((BLOCK, 2 * BLOCK), lambda h: (0, 0))],
        out_specs=pl.BlockSpec((None, BLOCK, 2 * BLOCK), lambda h: (h, 0, 0)),
        out_shape=jax.ShapeDtypeStruct((SW_HEADS, BLOCK, 2 * BLOCK), F32),
        name=name,
    )(rel_bias, buckets)


def _bias_grad(dbias_layers, buckets, name):
    n_l = len(dbias_layers)

    def body(*refs):
        bk = refs[n_l][...]
        out_ref = refs[n_l + 1]
        db = refs[0][...]
        for r in refs[1:n_l]:
            db = db + r[...]
        lane = lax.broadcasted_iota(jnp.int32, (1, LANES), 1)
        acc = jnp.zeros((1, LANES), F32)
        for b in range(N_BUCKETS):
            part = jnp.sum(jnp.where(bk == b, db, 0.0), axis=1, keepdims=True)
            tot = jnp.sum(part, axis=0, keepdims=True)
            acc = jnp.where(lane == b, tot, acc)
        out_ref[...] = acc

    hspec = pl.BlockSpec((None, BLOCK, 2 * BLOCK), lambda h: (h, 0, 0))
    return pl.pallas_call(
        body,
        grid=(SW_HEADS,),
        in_specs=[hspec] * n_l + [pl.BlockSpec((BLOCK, 2 * BLOCK), lambda h: (0, 0))],
        out_specs=pl.BlockSpec((None, 1, LANES), lambda h: (h, 0, 0)),
        out_shape=jax.ShapeDtypeStruct((SW_HEADS, 1, LANES), F32),
        name=name,
    )(*dbias_layers, buckets)


GROUP_ROWS = SW_GROUP * BLOCK


def _group_lanes(g):
    lane = lax.broadcasted_iota(jnp.int32, (1, LANES), 1)
    gvec = jnp.zeros((1, LANES), jnp.int32) + g
    return jnp.where(lane >= HEAD_DIM, 1, 0) == gvec, gvec


def _stack_heads(x, g):
    kv_lanes, gvec = _group_lanes(g)
    parts = []
    for j in range(SW_GROUP):
        half = x[:, (j // 2) * LANES:(j // 2 + 1) * LANES]
        moved = jnp.where(gvec == j % 2, half, pltpu.roll(half, HEAD_DIM, 1))
        parts.append(jnp.where(kv_lanes, moved, 0.0))
    return jnp.concatenate(parts, axis=0)


def _unstack_heads(y, g):
    _, gvec = _group_lanes(g)
    heads = []
    for j in range(SW_GROUP):
        yj = y[j * BLOCK:(j + 1) * BLOCK]
        heads.append(jnp.where(gvec == j % 2, yj, pltpu.roll(yj, HEAD_DIM, 1)))
    pairs = [jnp.where(_head_lanes(0), heads[2 * p], heads[2 * p + 1]) for p in range(SW_GROUP // 2)]
    return jnp.concatenate(pairs, axis=1)


def _per_head_col(values):
    return jnp.concatenate([jnp.zeros((BLOCK, 1), F32) + v for v in values], axis=0)


def _swa_scores(qs, kp, kc, bias_ref, n):
    row = jnp.bitwise_and(lax.broadcasted_iota(jnp.int32, (GROUP_ROWS, BLOCK), 0), BLOCK - 1)
    col = lax.broadcasted_iota(jnp.int32, (GROUP_ROWS, BLOCK), 1)
    bias = bias_ref[...].reshape(GROUP_ROWS, 2 * BLOCK)
    s1 = _dot_nt(qs, kp) + bias[:, :BLOCK]
    s2 = _dot_nt(qs, kc) + bias[:, BLOCK:]
    no_prev = jnp.where(n > 0, 0, BLOCK)
    s1 = jnp.where(col > row + no_prev, s1, NEG)
    s2 = jnp.where(col <= row, s2, NEG)
    return s1, s2


def _swa_specs(s):
    q_blk = 3 * SB_W // (2 * LANES)
    k_blk = (3 * SB_W + SW_W) // LANES
    return (pl.BlockSpec((s, 2 * LANES), lambda g: (0, q_blk + g)),
            pl.BlockSpec((s, LANES), lambda g: (0, k_blk)),
            pl.BlockSpec((s, LANES), lambda g: (0, k_blk + 1)))


def _swa_fwd(qkv, bias, sinks, name, comm=None):
    s = qkv.shape[0]
    nb = s // BLOCK

    def body(sink_ref, q_ref, k_ref, v_ref, bias_ref, o_ref, lse_ref):
        g = pl.program_id(0)
        sink = _per_head_col([sink_ref[SW_GROUP * g + j] for j in range(SW_GROUP)])
        lane = lax.broadcasted_iota(jnp.int32, (1, LANES), 1)

        def step(n, carry):
            r0 = pl.multiple_of(n * BLOCK, BLOCK)
            p0 = pl.multiple_of(jnp.maximum(n - 1, 0) * BLOCK, BLOCK)
            cur, prev = pl.ds(r0, BLOCK), pl.ds(p0, BLOCK)
            qs = _stack_heads(q_ref[cur, :].astype(F32) * SCALE, g).astype(MXU)
            s1, s2 = _swa_scores(qs, k_ref[prev, :], k_ref[cur, :], bias_ref, n)
            m = jnp.maximum(jnp.max(jnp.maximum(s1, s2), axis=1, keepdims=True), sink)
            e1 = jnp.exp(s1 - m)
            e2 = jnp.exp(s2 - m)
            den = jnp.sum(e1 + e2, axis=1, keepdims=True) + jnp.exp(sink - m)
            o = _dot((e1 / den).astype(MXU), v_ref[prev, :]) + _dot((e2 / den).astype(MXU), v_ref[cur, :])
            o_ref[cur, :] = _unstack_heads(o, g).astype(o_ref.dtype)
            lse = m + jnp.log(den)
            lse_row = jnp.zeros((BLOCK, LANES), F32)
            for j in range(SW_GROUP):
                lse_row = jnp.where(lane == j, lse[j * BLOCK:(j + 1) * BLOCK], lse_row)
            lse_ref[cur, :] = lse_row
            return carry

        lax.fori_loop(0, nb, step, 0, unroll=2)

    outs, couts = _call(
        body,
        grid=(SW_KV,),
        in_specs=[pl.BlockSpec(memory_space=pltpu.SMEM), *_swa_specs(s),
                  pl.BlockSpec((SW_GROUP, BLOCK, 2 * BLOCK), lambda g: (g, 0, 0))],
        out_specs=[pl.BlockSpec((s, 2 * LANES), lambda g: (0, g)),
                   pl.BlockSpec((None, s, LANES), lambda g: (g, 0, 0))],
        out_shape=[jax.ShapeDtypeStruct((s, SW_W), MXU), jax.ShapeDtypeStruct((SW_KV, s, LANES), F32)],
        args=(sinks, qkv, qkv, qkv, bias), name=name, comm=comm)
    return tuple(outs) if comm is None else (tuple(outs), couts)


def _swa_bwd(qkv, bias, sinks, do, lse, name, comm=None):
    s = qkv.shape[0]
    nb = s // BLOCK

    def body(sink_ref, q_ref, k_ref, v_ref, bias_ref, do_ref, lse_ref,
             dq_ref, dk_ref, dv_ref, dbias_ref, dsink_ref, dk_acc, dv_acc):
        g = pl.program_id(0)
        sink = _per_head_col([sink_ref[SW_GROUP * g + j] for j in range(SW_GROUP)])
        lane = lax.broadcasted_iota(jnp.int32, (1, LANES), 1)

        @pl.when(g == 0)
        def _():
            dk_acc[...] = jnp.zeros_like(dk_acc)
            dv_acc[...] = jnp.zeros_like(dv_acc)

        dbias_ref[...] = jnp.zeros_like(dbias_ref)

        def step(n, dsink_rows):
            r0 = pl.multiple_of(n * BLOCK, BLOCK)
            p0 = pl.multiple_of(jnp.maximum(n - 1, 0) * BLOCK, BLOCK)
            cur, prev = pl.ds(r0, BLOCK), pl.ds(p0, BLOCK)
            qs = _stack_heads(q_ref[cur, :].astype(F32) * SCALE, g).astype(MXU)
            dos = _stack_heads(do_ref[cur, :].astype(F32), g).astype(MXU)
            kp, kc, vp, vc = k_ref[prev, :], k_ref[cur, :], v_ref[prev, :], v_ref[cur, :]
            lse_row = lse_ref[cur, :]
            lse = jnp.concatenate([jnp.sum(jnp.where(lane == j, lse_row, 0.0), axis=1, keepdims=True)
                                   for j in range(SW_GROUP)], axis=0)
            s1, s2 = _swa_scores(qs, kp, kc, bias_ref, n)
            pr1 = jnp.exp(s1 - lse)
            pr2 = jnp.exp(s2 - lse)
            dpr1 = _dot_nt(dos, vp)
            dpr2 = _dot_nt(dos, vc)
            delta = jnp.sum(pr1 * dpr1 + pr2 * dpr2, axis=1, keepdims=True)
            ds1 = pr1 * (dpr1 - delta)
            ds2 = pr2 * (dpr2 - delta)
            dbias_ref[:, :, :BLOCK] += ds1.reshape(SW_GROUP, BLOCK, BLOCK)
            dbias_ref[:, :, BLOCK:] += ds2.reshape(SW_GROUP, BLOCK, BLOCK)
            ds1b, ds2b = ds1.astype(MXU), ds2.astype(MXU)
            dq = _dot(ds1b, kp) + _dot(ds2b, kc)
            dq_ref[cur, :] = (_unstack_heads(dq, g) * SCALE).astype(dq_ref.dtype)
            dk_acc[prev, :] += _dot_tn(ds1b, qs)
            dk_acc[cur, :] += _dot_tn(ds2b, qs)
            dv_acc[prev, :] += _dot_tn(pr1.astype(MXU), dos)
            dv_acc[cur, :] += _dot_tn(pr2.astype(MXU), dos)
            return dsink_rows - jnp.exp(sink - lse) * delta

        rows = lax.fori_loop(0, nb, step, jnp.zeros((GROUP_ROWS, 1), F32), unroll=2)
        for j in range(SW_GROUP):
            dsink_ref[j] = jnp.broadcast_to(jnp.sum(rows[j * BLOCK:(j + 1) * BLOCK], axis=0, keepdims=True),
                                            (1, LANES))

        @pl.when(g == SW_KV - 1)
        def _():
            dk_ref[...] = dk_acc[...].astype(dk_ref.dtype)
            dv_ref[...] = dv_acc[...].astype(dv_ref.dtype)

    grp = pl.BlockSpec((s, 2 * LANES), lambda g: (0, g))
    kv_out = pl.BlockSpec((s, LANES), lambda g: (0, 0))
    bspec = pl.BlockSpec((SW_GROUP, BLOCK, 2 * BLOCK), lambda g: (g, 0, 0))
    outs, couts = _call(
        body,
        grid=(SW_KV,),
        in_specs=[pl.BlockSpec(memory_space=pltpu.SMEM), *_swa_specs(s), bspec, grp,
                  pl.BlockSpec((None, s, LANES), lambda g: (g, 0, 0))],
        out_specs=[grp, kv_out, kv_out, bspec, pl.BlockSpec((SW_GROUP, 1, LANES), lambda g: (g, 0, 0))],
        out_shape=[jax.ShapeDtypeStruct((s, SW_W), MXU),
                   jax.ShapeDtypeStruct((s, LANES), MXU),
                   jax.ShapeDtypeStruct((s, LANES), MXU),
                   jax.ShapeDtypeStruct((SW_HEADS, BLOCK, 2 * BLOCK), F32),
                   jax.ShapeDtypeStruct((SW_HEADS, 1, LANES), F32)],
        scratch_shapes=[pltpu.VMEM((s, LANES), F32), pltpu.VMEM((s, LANES), F32)],
        args=(sinks, qkv, qkv, qkv, bias, do, lse), name=name, comm=comm)
    return tuple(outs) if comm is None else (tuple(outs), couts)


class _NoPlan:
    def comm(self, name):
        return None

    def done(self, name, outs):
        pass

    def grad(self, layer, name, value):
        pass


def _run(plan, fn, *args, name, **kw):
    comm = plan.comm(name)
    if comm is None:
        return fn(*args, name=name, **kw)
    res, outs = fn(*args, name=name, comm=comm, **kw)
    plan.done(name, outs)
    return res


def _norm_bwd(dh, x, dres, r, g):
    xhat = x * r
    dxhat = dh * g
    dx = dres + r * (dxhat - xhat * jnp.mean(dxhat * xhat, axis=-1, keepdims=True))
    return dx, dx, jnp.sum(dh * xhat, axis=0, keepdims=True)


def _residual_norm(acc, res, g):
    x = res + acc
    r = lax.rsqrt(jnp.mean(x * x, axis=-1, keepdims=True) + EPS)
    return x, (x * r) * g, r


def _layer_fwd(x, p, w, g_mix, g_mlp, g_pe, sinks, bias, tag, plan, normed_x=None, next_g=None):
    h1, r1 = normed_x if normed_x is not None else _run(plan, _rms_fwd, x, g_mix, name=f"rms_mix_{tag}")
    qkv = _run(plan, _mm, h1, w["w_qkv_t"], tb=True, out_dtypes=(MXU,), tn=QKV_COLS, name=f"proj_qkv_{tag}")
    gates = _run(plan, _mm, h1, w["w_gate_t"], tb=True, out_dtypes=(MXU,), tn=2048, name=f"proj_gate_{tag}")
    oa = _run(plan, _sb_fwd, qkv, name=f"sb_fwd_{tag}")
    ob, lse = _run(plan, _swa_fwd, qkv, bias, sinks, name=f"swa_fwd_{tag}")
    merged = _mix_fwd(oa, ob, w["w_up_a_t"], w["w_up_b_t"], gates, f"mix_fwd_{tag}")
    d = x.shape[1]
    normed = (F32, MXU, (F32, "col"))
    x1, h2, r2 = _run(plan, _mm, merged, w["w_o"], extras=(x, g_mlp), epi=_residual_norm, out_dtypes=normed,
                      tn=d, name=f"out_proj_{tag}")
    u, act = _run(plan, _mm, h2, w["w_ff1_t"], tb=True,
                  epi=lambda acc: (acc, jnp.square(jnp.maximum(acc, 0.0))),
                  out_dtypes=(MXU, MXU), tn=2048, name=f"ff1_{tag}")
    x2, h3, r3 = _run(plan, _mm, act, w["w_ff2"], extras=(x1, g_pe), epi=_residual_norm, out_dtypes=normed,
                      tn=d, name=f"ff2_{tag}")
    x3 = _ple(p, w["w_pe_t"], h3, w["w_pg"], x2, backward=False, name=f"ple_fwd_{tag}", next_g=next_g)
    next_normed = None
    if next_g is not None:
        x3, next_normed = x3[0], (x3[1], x3[2])
    saved = dict(x=x, h1=h1, r1=r1, gates=gates, qkv=qkv, lse=lse, oa=oa, ob=ob, merged=merged,
                 x1=x1, h2=h2, r2=r2, u=u, act=act, x2=x2, h3=h3, r3=r3)
    return x3, saved, next_normed


def _layer_bwd(dx3, sv, p, w, g_mix, g_mlp, g_pe, sinks, bias, layer, plan):
    tag = f"l{layer}"
    gw = {}
    wire = (WIRE,)

    def dw(name, a, b):
        gw[name] = _run(plan, _mm, a, b, ta=True, out_dtypes=wire, tk=2048, name=f"d{name}_{tag}")
        plan.grad(layer, name, gw[name])

    dpe, dgt = _ple(p, w["w_pe_t"], sv["h3"], w["w_pg"], dx3, backward=True, name=f"ple_bwd_{tag}")
    dw("w_pe", dpe, p)
    dw("w_pg", sv["h3"], dgt)
    d = dx3.shape[1]
    grads = (F32, MXU, (F32, "rowsum"))
    dx2, dx2b, dg_pe = _run(plan, _mm, dgt, w["w_pg"], tb=True, extras=(sv["x2"], dx3, sv["r3"], g_pe),
                            epi=_norm_bwd, out_dtypes=grads, tm=512, tn=d, name=f"dh_pe_{tag}")
    dw("w_ff2", sv["act"], dx2b)
    du = _run(plan, _mm, dx2b, w["w_ff2"], tb=True, extras=(sv["u"],),
              epi=lambda acc, u: acc * (2.0 * jnp.maximum(u.astype(F32), 0.0)), out_dtypes=(MXU,),
              tn=2048, name=f"dact_{tag}")
    dw("w_ff1", du, sv["h2"])
    dx1, dx1b, dg_mlp = _run(plan, _mm, du, w["w_ff1_t"], extras=(sv["x1"], dx2, sv["r2"], g_mlp),
                             epi=_norm_bwd, out_dtypes=grads, tm=1024, tn=d, name=f"dh_mlp_{tag}")
    dw("w_o", sv["merged"], dx1b)
    dya, dyb, dga, dgb = _mix_bwd(dx1b, w["w_o"], sv["oa"], sv["ob"], w["w_up_a_t"], w["w_up_b_t"],
                                  sv["gates"], f"mix_bwd_{tag}")
    dw("w_up_a", dya, sv["oa"])
    dw("w_up_b", dyb, sv["ob"])
    doa = _run(plan, _mm, dya, w["w_up_a_t"], out_dtypes=(MXU,), name=f"do_a_{tag}")
    dob = _run(plan, _mm, dyb, w["w_up_b_t"], out_dtypes=(MXU,), name=f"do_b_{tag}")
    dqb, dkb, dvb, dbias, dsink = _run(plan, _swa_bwd, sv["qkv"], bias, sinks, dob, sv["lse"],
                                       name=f"swa_bwd_{tag}")
    dqa, dka, dva = _run(plan, _sb_bwd, sv["qkv"], doa, name=f"sb_bwd_{tag}")
    dqkv = jnp.concatenate([dqa, dka, dva, dqb, dkb, dvb], axis=1)
    gw_qkv = _mm(dqkv, sv["h1"], ta=True, out_dtypes=wire, tk=2048, name=f"dw_qkv_{tag}")
    gw_ga = _mm(dga, sv["h1"], ta=True, out_dtypes=wire, tk=2048, name=f"dw_ga_{tag}")
    gw_gb = _mm(dgb, sv["h1"], ta=True, out_dtypes=wire, tk=2048, name=f"dw_gb_{tag}")
    gw["w_in"] = jnp.concatenate([gw_qkv, gw_ga, gw_gb], axis=0)
    plan.grad(layer, "w_in", gw["w_in"])
    d = dga.shape[1]
    add = lambda acc, res: res + acc
    dh1 = _run(plan, _mm, dga, w["w_gate_t"][:d], name=f"dh_ga_{tag}")
    dh1 = _run(plan, _mm, dgb, w["w_gate_t"][d:], extras=(dh1,), epi=add, name=f"dh_gb_{tag}")
    dx, _, dg_mix = _run(plan, _mm, dqkv, w["w_qkv_t"], tk=768, extras=(dh1, sv["x"], dx1, sv["r1"], g_mix),
                         epi=lambda acc, prev, *rest: _norm_bwd(acc + prev, *rest), out_dtypes=grads,
                         tm=512, tn=d, name=f"dh_qkv_{tag}")
    small = dict(g_mix=dg_mix, g_mlp=dg_mlp, g_pe=dg_pe, sinks=dsink[:, 0, 0], dbias=dbias)
    return dx, gw, small


def _local_step(x, p, target, weights, g_mix, g_mlp, g_pe, g_final, sinks, rel_bias, plan=None):
    plan = _NoPlan() if plan is None else plan
    depth = g_mix.shape[0]
    buckets = jnp.asarray(_bucket_table())
    bias = _build_bias(rel_bias, buckets, "build_bias")
    saved, wfull = [], []
    h, normed = x, None
    for l in range(depth):
        wfull.append(weights(l))
        next_g = g_mix[l + 1:l + 2] if l + 1 < depth else None
        h, sv, normed = _layer_fwd(h, p[l], wfull[l], g_mix[l:l + 1], g_mlp[l:l + 1], g_pe[l:l + 1],
                                   sinks[l], bias, f"l{l}", plan, normed, next_g)
        saved.append(sv)
    loss_row, dx, dg_final = _loss_head(h, g_final[None, :], target, "loss_head")
    gws = [None] * depth
    smalls = [None] * depth
    for l in reversed(range(depth)):
        dx, gws[l], smalls[l] = _layer_bwd(dx, saved[l], p[l], wfull[l], g_mix[l:l + 1], g_mlp[l:l + 1],
                                           g_pe[l:l + 1], sinks[l], bias, l, plan)
    drel = _bias_grad([sm["dbias"] for sm in smalls], buckets, "bias_grad")[:, 0, :N_BUCKETS].T
    small = dict(
        g_mix=jnp.concatenate([sm["g_mix"] for sm in smalls], axis=0),
        g_mlp=jnp.concatenate([sm["g_mlp"] for sm in smalls], axis=0),
        g_pe=jnp.concatenate([sm["g_pe"] for sm in smalls], axis=0),
        g_final=dg_final[0],
        sinks=jnp.stack([sm["sinks"] for sm in smalls], axis=0),
        rel_bias=drel,
    )
    return loss_row, dx, gws, small


MESH_ID = pl.DeviceIdType.MESH


def _position():
    return lax.axis_index("x"), lax.axis_index("y"), lax.axis_index("c")


def _gather_comm(shards):
    n = len(shards)

    def copies(pos, x_refs, out_refs, sems):
        send_sems, recv_sems, local_sems = sems
        x, y, c = pos
        me, sibling = (x, y, c), (x, y, 1 - c)
        chips = [(1 - x, y), (x, 1 - y), (1 - x, 1 - y)]

        def slot(a, px, py, pc):
            return out_refs[a].at[4 * px + 2 * py + pc]

        def copy(a, k, block, to, src=None):
            return pltpu.make_async_remote_copy(
                src_ref=slot(a, *block) if src is None else src, dst_ref=slot(a, *block),
                send_sem=send_sems.at[a, k], recv_sem=recv_sems.at[a, k],
                device_id=to, device_id_type=MESH_ID)

        mine = [pltpu.make_async_copy(x_refs[a], slot(a, *me), local_sems.at[a]) for a in range(n)]
        first = []
        for a in range(n):
            first.append(copy(a, 0, me, sibling, src=x_refs[a]))
            first += [copy(a, 1 + j, me, (*chip, c), src=x_refs[a]) for j, chip in enumerate(chips)]
        return me, sibling, chips, copy, mine, first

    def start(pos, x_refs, out_refs, sems):
        _, _, _, _, mine, first = copies(pos, x_refs, out_refs, sems)
        for cp in mine + first:
            cp.start()

    def finish(pos, x_refs, out_refs, sems):
        me, sibling, chips, copy, mine, first = copies(pos, x_refs, out_refs, sems)
        c = pos[2]
        passed = []
        for j, chip in enumerate(chips):
            for a in range(n):
                copy(a, 1 + j, (*chip, c), me).wait_recv()
                fwd = copy(a, 4 + j, (*chip, c), sibling)
                fwd.start()
                passed.append(fwd)
        for a in range(n):
            copy(a, 0, sibling, me).wait_recv()
            for j, chip in enumerate(chips):
                copy(a, 4 + j, (*chip, 1 - c), me).wait_recv()
        for cp in first + passed:
            cp.wait_send()
        for cp in mine:
            cp.wait()

    return _Comm(shards, [jax.ShapeDtypeStruct((N_DEV,) + s.shape, s.dtype) for s in shards],
                 [pltpu.SemaphoreType.DMA((n, 7)), pltpu.SemaphoreType.DMA((n, 7)),
                  pltpu.SemaphoreType.DMA((n,))], start, finish)


def _exchange_comm(arrays, n_slots, route):
    n = len(arrays)

    def copies(pos, in_refs, out_refs, sems):
        send_sems, recv_sems = sems
        out = []
        for a in range(n):
            for j in range(n_slots):
                src_slot, peer = route(pos, j)
                out.append(pltpu.make_async_remote_copy(
                    src_ref=in_refs[a].at[src_slot], dst_ref=out_refs[a].at[j],
                    send_sem=send_sems.at[a, j], recv_sem=recv_sems.at[a, j],
                    device_id=peer, device_id_type=MESH_ID))
        return out

    def start(pos, in_refs, out_refs, sems):
        for cp in copies(pos, in_refs, out_refs, sems):
            cp.start()

    def finish(pos, in_refs, out_refs, sems):
        for cp in copies(pos, in_refs, out_refs, sems):
            cp.wait()

    return _Comm(arrays, [jax.ShapeDtypeStruct((n_slots,) + g.shape[1:], g.dtype) for g in arrays],
                 [pltpu.SemaphoreType.DMA((n, n_slots)), pltpu.SemaphoreType.DMA((n, n_slots))], start, finish)


def _rs_sibling_comm(gs):
    return _exchange_comm(gs, 4, lambda pos, j: (2 * j + (1 - pos[2]), (pos[0], pos[1], 1 - pos[2])))


def _chip_of(k, x, y):
    return x ^ ((k + 1) & 1), y ^ (((k + 1) >> 1) & 1)


def _chip_partials(pos, gs, recvs, name):
    n = len(gs)

    def body(pos_ref, *refs):
        for a in range(n):
            refs[2 * n + a][...] = (refs[a][...].astype(F32) + refs[n + a][...].astype(F32)
                                    ).astype(refs[2 * n + a].dtype)

    def g_map(k, pos_ref):
        cx, cy = _chip_of(k, pos_ref[0], pos_ref[1])
        return (4 * cx + 2 * cy + pos_ref[2], 0, 0)

    def r_map(k, pos_ref):
        cx, cy = _chip_of(k, pos_ref[0], pos_ref[1])
        return (2 * cx + cy, 0, 0)

    slab = [(None,) + g.shape[1:] for g in gs]
    return pl.pallas_call(
        body,
        grid_spec=pltpu.PrefetchScalarGridSpec(
            num_scalar_prefetch=1,
            grid=(4,),
            in_specs=[pl.BlockSpec(sh, g_map) for sh in slab] + [pl.BlockSpec(sh, r_map) for sh in slab],
            out_specs=[pl.BlockSpec(sh, lambda k, pos_ref: (k, 0, 0)) for sh in slab],
        ),
        out_shape=[jax.ShapeDtypeStruct((4,) + g.shape[1:], g.dtype) for g in gs],
        compiler_params=_cparams(),
        name=name,
    )(pos, *gs, *recvs)


def _rs_chips_comm(parts):
    return _exchange_comm(parts, 3, lambda pos, k: (k, (*_chip_of(k, pos[0], pos[1]), pos[2])))


def _adamw_math(w, g, m, v):
    m = ADAM_B1 * m + (1.0 - ADAM_B1) * g
    v = ADAM_B2 * v + (1.0 - ADAM_B2) * (g * g)
    m_hat = m / (1.0 - ADAM_B1 ** ADAM_STEP)
    v_hat = v / (1.0 - ADAM_B2 ** ADAM_STEP)
    delta = -ADAM_LR * (m_hat / (jnp.sqrt(v_hat) + ADAM_EPS) + ADAM_WD * w)
    return delta, m, v


def _adamw_weight(parts, recvs, w, m, v, name, grad_t=False, comm=None):
    depth, a, b = w.shape
    ta = _tile(a, 288, unit=LANES if grad_t else 16)
    ni = a // ta
    g_block = (b, ta) if grad_t else (ta, b)

    def body(*refs):
        p_refs, r_refs = refs[:depth], refs[depth:2 * depth]
        w_ref, m_ref, v_ref = refs[2 * depth:2 * depth + 3]
        g_out, d_out, m_out, v_out = refs[2 * depth + 3:]
        layer = pl.program_id(0)
        g = jnp.zeros(g_block, F32)
        for l in range(depth):
            gl = p_refs[l][...].astype(F32)
            for k in range(3):
                gl = gl + r_refs[l][k].astype(F32)
            g = jnp.where(layer == l, gl, g)
        if grad_t:
            g = g.T
        delta, m_new, v_new = _adamw_math(w_ref[...], g, m_ref[...], v_ref[...])
        g_out[...] = g
        d_out[...] = delta
        m_out[...] = m_new
        v_out[...] = v_new

    def hold(l):
        return lambda layer, i: jnp.where(layer == l, i, jnp.where(layer < l, 0, ni - 1))

    def g_index(slot, f):
        if grad_t:
            return lambda layer, i: (slot, 0, f(layer, i))
        return lambda layer, i: (slot, f(layer, i), 0)

    p_specs = [pl.BlockSpec((None,) + g_block, g_index(3, hold(l))) for l in range(depth)]
    r_specs = [pl.BlockSpec((3,) + g_block, g_index(0, hold(l))) for l in range(depth)]
    row = pl.BlockSpec((None, ta, b), lambda layer, i: (layer, i, 0))
    outs, couts = _call(
        body,
        grid=(depth, ni),
        in_specs=p_specs + r_specs + [row, row, row],
        out_specs=[row] * 4,
        out_shape=[jax.ShapeDtypeStruct(w.shape, F32)] * 4,
        args=(*parts, *recvs, w, m, v), name=name, comm=comm)
    return outs if comm is None else (outs, couts)


def _adamw_replicated(gathered, w, m, v, name):
    r, lanes = w.shape

    def body(g_ref, w_ref, m_ref, v_ref, g_out, d_out, m_out, v_out):
        g = g_ref[0]
        for k in range(1, N_DEV):
            g = g + g_ref[k]
        delta, m_new, v_new = _adamw_math(w_ref[...], g, m_ref[...], v_ref[...])
        g_out[...] = g
        d_out[...] = delta
        m_out[...] = m_new
        v_out[...] = v_new

    return pl.pallas_call(
        body,
        out_shape=[jax.ShapeDtypeStruct((r, lanes), F32)] * 4,
        name=name,
    )(gathered, w, m, v)


def _wire_shard(name, shard):
    return (shard.T if name in COL_SHARDED else shard).astype(WIRE)


def _full_weight(gathered):
    return gathered.reshape(N_DEV * gathered.shape[1], gathered.shape[2])


def _to_slabs(gfull):
    return gfull.reshape(N_DEV, gfull.shape[0] // N_DEV, gfull.shape[1])


def _pack_small(arrs):
    rows = []
    for a in arrs:
        flat = a.astype(F32).reshape(-1)
        pad = (-flat.shape[0]) % LANES
        rows.append(jnp.pad(flat, (0, pad)).reshape(-1, LANES))
    packed = jnp.concatenate(rows, axis=0)
    return jnp.pad(packed, ((0, (-packed.shape[0]) % 8), (0, 0)))


def _unpack_small(packed, shapes):
    out, off = [], 0
    for shp in shapes:
        n = math.prod(shp)
        rows = -(-n // LANES)
        out.append(packed[off:off + rows].reshape(-1)[:n].reshape(shp))
        off += rows
    return out


def _of(layer, *names):
    return tuple((layer, n) for n in names)


MLP_W = ("w_ff1", "w_ff2", "w_pe", "w_pg")

GATHERS = (
    ("rms_mix_l0", _of(0, "w_in")),
    ("proj_qkv_l0", _of(0, "w_up_a", "w_up_b", "w_o")),
    ("proj_gate_l0", _of(0, "w_pe", "w_pg")),
    ("sb_fwd_l0", _of(0, "w_ff1", "w_ff2")),
    ("swa_fwd_l0", _of(1, "w_in")),
    ("sb_fwd_l1", _of(1, "w_up_a", "w_up_b", "w_o", "w_pe", "w_pg", "w_ff1")),
    ("swa_fwd_l1", _of(1, "w_ff2")),
)
REDUCES = (
    (_of(1, "w_ff1"), "dw_ff2_l0", "dact_l0"),
    (_of(1, "w_ff2"), "dw_ff2_l0", "dw_ff1_l0"),
    (_of(1, "w_in"), "dw_ff2_l0", "sb_bwd_l0"),
    (_of(1, "w_up_a", "w_up_b", "w_o", "w_pe", "w_pg"), "dw_ff2_l0", "swa_bwd_l0"),
    (_of(0, *MLP_W), "dh_mlp_l0", "sb_bwd_l0"),
    (_of(0, "w_o", "w_up_a", "w_up_b"), "do_a_l0", "sb_bwd_l0"),
    (_of(0, "w_in"), "dh_ga_l0", "dh_qkv_l0"),
)


def _merge_comms(comms):
    if len(comms) == 1:
        return comms[0]

    def cuts(counts):
        edges = [0]
        for c in counts:
            edges.append(edges[-1] + c)
        return [slice(a, b) for a, b in zip(edges[:-1], edges[1:])]

    s_in = cuts([len(c.inputs) for c in comms])
    s_out = cuts([len(c.out_shapes) for c in comms])
    s_sem = cuts([len(c.sems) for c in comms])

    def start(pos, cin, cout, csem):
        for c, i, o, s in zip(comms, s_in, s_out, s_sem):
            c.start(pos, cin[i], cout[o], csem[s])

    def finish(pos, cin, cout, csem):
        for c, i, o, s in zip(comms, s_in, s_out, s_sem):
            c.finish(pos, cin[i], cout[o], csem[s])

    return _Comm(sum([c.inputs for c in comms], []), sum([c.out_shapes for c in comms], []),
                 sum([c.sems for c in comms], []), start, finish)


class _LayerWeights:
    def __init__(self, full, layer):
        self.full, self.layer, self.cache = full, layer, {}

    def __getitem__(self, name):
        if name not in self.cache:
            if name == "w_qkv_t":
                self.cache[name] = self.full[(self.layer, "w_in")][:QKV_COLS]
            elif name == "w_gate_t":
                self.cache[name] = self.full[(self.layer, "w_in")][QKV_COLS:]
            else:
                base = name[:-2] if name.endswith("_t") else name
                assert (base in COL_SHARDED) == name.endswith("_t"), name
                self.cache[name] = self.full[(self.layer, base)]
        return self.cache[name]


class _Plan:
    def __init__(self, w_sh, pos):
        self.w_sh = dict(zip(WEIGHTS, w_sh))
        self.pos = pos
        self.full, self.gw, self.parts, self.recv = {}, {}, {}, {}
        self.slabs = {}
        self.hosted = {}
        for i, (host, _) in enumerate(GATHERS):
            self.hosted.setdefault(host, []).append(("gather", i))
        for i, (_, sib_host, chip_host) in enumerate(REDUCES):
            self.hosted.setdefault(sib_host, []).append(("sibling", i))
            self.hosted.setdefault(chip_host, []).append(("chips", i))

    def _gather(self, i):
        return _gather_comm([_wire_shard(n, self.w_sh[n][layer]) for layer, n in GATHERS[i][1]])

    def _gathered(self, i, outs):
        for (layer, n), g in zip(GATHERS[i][1], outs):
            self.full[(layer, n)] = _full_weight(g)

    def weights(self, layer):
        return _LayerWeights(self.full, layer)

    def grad(self, layer, name, value):
        self.gw[(layer, name)] = value

    def _sibling(self, i):
        self.slabs[i] = [_to_slabs(self.gw[item]) for item in REDUCES[i][0]]
        return _rs_sibling_comm(self.slabs[i])

    def _sibling_done(self, i, outs):
        parts = _chip_partials(self.pos, self.slabs[i], outs, f"chip_partials_{i}")
        for item, part in zip(REDUCES[i][0], parts):
            self.parts[item] = part

    def _chips(self, i):
        return _rs_chips_comm([self.parts[item] for item in REDUCES[i][0]])

    def _chips_done(self, i, outs):
        for item, r in zip(REDUCES[i][0], outs):
            self.recv[item] = r

    def comm(self, name):
        if name not in self.hosted:
            return None
        make = {"gather": self._gather, "sibling": self._sibling, "chips": self._chips}
        return _merge_comms([make[kind](i) for kind, i in self.hosted[name]])

    def done(self, name, outs):
        took = {"gather": self._gathered, "sibling": self._sibling_done, "chips": self._chips_done}
        off = 0
        for kind, i in self.hosted[name]:
            n = len(GATHERS[i][1]) if kind == "gather" else len(REDUCES[i][0])
            took[kind](i, outs[off:off + n])
            off += n


def kernel(x, p, w_in, w_up_a, w_up_b, w_o, w_ff1, w_ff2, w_pe, w_pg, g_mix, g_mlp, g_pe, g_final, sinks, rel_bias, loss_target, m_w_in, m_w_up_a, m_w_up_b, m_w_o, m_w_ff1, m_w_ff2, m_w_pe, m_w_pg, m_g_mix, m_g_mlp, m_g_pe, m_g_final, m_sinks, m_rel_bias, v_w_in, v_w_up_a, v_w_up_b, v_w_o, v_w_ff1, v_w_ff2, v_w_pe, v_w_pg, v_g_mix, v_g_mlp, v_g_pe, v_g_final, v_sinks, v_rel_bias):
    w_sh = [w_in, w_up_a, w_up_b, w_o, w_ff1, w_ff2, w_pe, w_pg]
    m_sh = [m_w_in, m_w_up_a, m_w_up_b, m_w_o, m_w_ff1, m_w_ff2, m_w_pe, m_w_pg]
    v_sh = [v_w_in, v_w_up_a, v_w_up_b, v_w_o, v_w_ff1, v_w_ff2, v_w_pe, v_w_pg]
    depth = w_in.shape[0]
    assert depth == 2 and x.shape[-1] * 2 + QKV_COLS == w_in.shape[2] * N_DEV

    px, py, pc = _position()
    plan = _Plan(w_sh, jnp.stack([px, py, pc]).astype(jnp.int32))
    loss_row, grad_x, _, small = _local_step(
        x[0], p[:, 0], loss_target[0], plan.weights, g_mix, g_mlp, g_pe, g_final, sinks, rel_bias, plan=plan)

    small_g = _pack_small([small[n] for n in SMALL] + [loss_row[0, :1]])
    grad_w, delta_w, new_m, new_v = [], [], [], []
    for a, name in enumerate(WEIGHTS):
        parts = [plan.parts[(l, name)] for l in range(depth)]
        recvs = [plan.recv[(l, name)] for l in range(depth)]
        if name == "w_in":
            flip = lambda t: t.transpose(0, 2, 1)
            outs, (small_all,) = _adamw_weight(parts, recvs, flip(w_sh[a]), flip(m_sh[a]), flip(v_sh[a]),
                                               f"adamw_{name}", comm=_gather_comm([small_g]))
            outs = [flip(o) for o in outs]
        else:
            outs = _adamw_weight(parts, recvs, w_sh[a], m_sh[a], v_sh[a], f"adamw_{name}",
                                 grad_t=name in COL_SHARDED)
        for lst, o in zip((grad_w, delta_w, new_m, new_v), outs):
            lst.append(o)

    small_w = [g_mix, g_mlp, g_pe, g_final, sinks, rel_bias]
    small_m = [m_g_mix, m_g_mlp, m_g_pe, m_g_final, m_sinks, m_rel_bias]
    small_v = [v_g_mix, v_g_mlp, v_g_pe, v_g_final, v_sinks, v_rel_bias]
    small_shapes = [a.shape for a in small_w] + [(1,)]
    zero = jnp.zeros((1,), F32)
    packed_s = _adamw_replicated(small_all, _pack_small(small_w + [zero]), _pack_small(small_m + [zero]),
                                 _pack_small(small_v + [zero + 1.0]), "adamw_replicated")
    sg, sd, sm, sv = [_unpack_small(t, small_shapes) for t in packed_s]
    loss = sg[-1][0]

    return (loss, grad_x[None], *grad_w, *sg[:-1], *delta_w, *sd[:-1], *new_m, *sm[:-1], *new_v, *sv[:-1])
```

```python
import functools
import math

import numpy as np
import jax
import jax.numpy as jnp
from jax import lax
from jax.experimental import pallas as pl
from jax.experimental.pallas import tpu as pltpu

F32 = jnp.float32
MXU = jnp.bfloat16
WIRE = jnp.bfloat16

HEAD_DIM = 64
SB_HEADS = 8
SW_HEADS = 8
SW_KV = 2
SW_GROUP = SW_HEADS // SW_KV
BLOCK = 128
N_BUCKETS = 32
MAX_DISTANCE = 128
EPS = 1e-6
SCALE = HEAD_DIM ** -0.5
SB_W = SB_HEADS * HEAD_DIM
SW_W = SW_HEADS * HEAD_DIM
QKV_COLS = 3 * SB_W + SW_W + 2 * SW_KV * HEAD_DIM
N_DEV = 8
LANES = 128
N_PAIR = SB_HEADS // 2
NEG = -1e30

ADAM_LR = 0.001
ADAM_B1 = 0.9
ADAM_B2 = 0.999
ADAM_EPS = 1e-08
ADAM_WD = 0.01
ADAM_STEP = 10

VMEM_LIMIT = 48 * 1024 * 1024
SB_TQ = 256
SB_DEAD = -105.0
SB_SUB = 4

WEIGHTS = ("w_in", "w_up_a", "w_up_b", "w_o", "w_ff1", "w_ff2", "w_pe", "w_pg")
COL_SHARDED = ("w_in", "w_up_a", "w_up_b", "w_ff1", "w_pe")
SMALL = ("g_mix", "g_mlp", "g_pe", "g_final", "sinks", "rel_bias")


def _cparams(**kw):
    return pltpu.CompilerParams(vmem_limit_bytes=VMEM_LIMIT, **kw)


def _dot(a, b):
    return jnp.dot(a, b, preferred_element_type=F32)


def _dot_nt(a, b):
    return lax.dot_general(a, b, (((1,), (1,)), ((), ())), preferred_element_type=F32)


def _dot_tn(a, b):
    return lax.dot_general(a, b, (((0,), (0,)), ((), ())), preferred_element_type=F32)


def _tile(n, target, unit=LANES):
    if n <= target:
        return n
    t = (target // unit) * unit
    while t > unit and n % t:
        t -= unit
    assert n % t == 0, (n, target)
    return t


def _sigmoid(x):
    return 0.5 * jnp.tanh(0.5 * x) + 0.5


class _Comm:
    def __init__(self, inputs, out_shapes, sems, start, finish):
        self.inputs, self.out_shapes, self.sems = list(inputs), list(out_shapes), list(sems)
        self.start, self.finish = start, finish


def _call(body, *, grid, in_specs, out_specs, out_shape, scratch_shapes=(), args, name, comm=None):
    n_in, n_out, n_scr = len(in_specs), len(out_shape), len(scratch_shapes)
    if comm is None:
        outs = pl.pallas_call(body, grid=grid, in_specs=list(in_specs), out_specs=list(out_specs),
                              out_shape=list(out_shape), scratch_shapes=list(scratch_shapes),
                              compiler_params=_cparams(), name=name)(*args)
        return list(outs), None
    ci, co = len(comm.inputs), len(comm.out_shapes)
    any_spec = pl.BlockSpec(memory_space=pl.ANY)

    def wrapped(*refs):
        ins, cin = refs[:n_in], refs[n_in:n_in + ci]
        o0 = n_in + ci
        outs, cout = refs[o0:o0 + n_out], refs[o0 + n_out:o0 + n_out + co]
        s0 = o0 + n_out + co
        scr, csem = refs[s0:s0 + n_scr], refs[s0 + n_scr:]
        ids = [pl.program_id(d) for d in range(len(grid))]
        first = functools.reduce(jnp.logical_and, [i == 0 for i in ids])
        last = functools.reduce(jnp.logical_and, [i == g - 1 for i, g in zip(ids, grid)])
        pos = (lax.axis_index("x"), lax.axis_index("y"), lax.axis_index("c"))

        @pl.when(first)
        def _():
            comm.start(pos, cin, cout, csem)

        body(*ins, *outs, *scr)

        @pl.when(last)
        def _():
            comm.finish(pos, cin, cout, csem)

    outs = pl.pallas_call(wrapped, grid=grid, in_specs=list(in_specs) + [any_spec] * ci,
                          out_specs=list(out_specs) + [any_spec] * co,
                          out_shape=list(out_shape) + comm.out_shapes,
                          scratch_shapes=list(scratch_shapes) + comm.sems,
                          compiler_params=_cparams(), name=name)(*args, *comm.inputs)
    return list(outs[:n_out]), list(outs[n_out:])


def _accumulate(o_ref, value, first):
    @pl.when(first)
    def _():
        o_ref[...] = value

    @pl.when(jnp.logical_not(first))
    def _():
        o_ref[...] += value


def _mm(a, b, *, ta=False, tb=False, extras=(), epi=None, out_dtypes=(F32,),
        tm=1024, tn=1024, tk=1024, name, comm=None):
    if ta:
        kdim, m = a.shape
    else:
        m, kdim = a.shape
    n = b.shape[0] if tb else b.shape[1]
    assert (b.shape[1] if tb else b.shape[0]) == kdim
    tm, tn, tk = _tile(m, tm), _tile(n, tn), _tile(kdim, tk)
    nk = kdim // tk
    n_ex, n_out = len(extras), len(out_dtypes)

    a_spec = (pl.BlockSpec((tk, tm), lambda i, j, k: (k, i)) if ta
              else pl.BlockSpec((tm, tk), lambda i, j, k: (i, k)))
    b_spec = (pl.BlockSpec((tn, tk), lambda i, j, k: (j, k)) if tb
              else pl.BlockSpec((tk, tn), lambda i, j, k: (k, j)))
    ex_specs = []
    for e in extras:
        assert e.shape in ((m, n), (1, n), (m, 1)), (e.shape, m, n)
        if e.shape == (m, n):
            ex_specs.append(pl.BlockSpec((tm, tn), lambda i, j, k: (i, j)))
        elif e.shape[0] == 1:
            ex_specs.append(pl.BlockSpec((1, tn), lambda i, j, k: (0, j)))
        else:
            ex_specs.append(pl.BlockSpec((tm, 1), lambda i, j, k: (i, 0)))
    out_specs, out_shape, row_sums = [], [], []
    for dt in out_dtypes:
        kind = dt[1] if isinstance(dt, tuple) else "tile"
        row_sums.append(kind == "rowsum")
        if kind == "tile":
            out_specs.append(pl.BlockSpec((tm, tn), lambda i, j, k: (i, j)))
            out_shape.append(jax.ShapeDtypeStruct((m, n), dt))
            continue
        assert tn == n, "per-row and summed outputs need whole rows in one tile"
        if kind == "col":
            out_specs.append(pl.BlockSpec((tm, 1), lambda i, j, k: (i, 0)))
            out_shape.append(jax.ShapeDtypeStruct((m, 1), dt[0]))
        else:
            out_specs.append(pl.BlockSpec((1, tn), lambda i, j, k: (0, 0)))
            out_shape.append(jax.ShapeDtypeStruct((1, n), dt[0]))

    def body(a_ref, b_ref, *rest):
        ex_refs = rest[:n_ex]
        out_refs = rest[n_ex:n_ex + n_out]
        acc = rest[-1]
        k = pl.program_id(2)
        first_rows = pl.program_id(0) == 0

        def prod():
            av = a_ref[...].astype(MXU)
            bv = b_ref[...].astype(MXU)
            return _dot_tn(av, bv) if ta else (_dot_nt(av, bv) if tb else _dot(av, bv))

        def finish(res):
            if epi is not None:
                res = epi(res, *[e[...] for e in ex_refs])
            if not isinstance(res, tuple):
                res = (res,)
            for o_ref, r, summed in zip(out_refs, res, row_sums):
                if summed:
                    _accumulate(o_ref, r.astype(o_ref.dtype), first_rows)
                else:
                    o_ref[...] = r.astype(o_ref.dtype)

        if nk == 1:
            finish(prod())
            return

        @pl.when(k == 0)
        def _():
            acc[...] = prod()

        @pl.when(jnp.logical_and(k > 0, k < nk - 1))
        def _():
            acc[...] += prod()

        @pl.when(k == nk - 1)
        def _():
            finish(acc[...] + prod())

    outs, couts = _call(
        body,
        grid=(m // tm, n // tn, nk),
        in_specs=[a_spec, b_spec] + ex_specs,
        out_specs=out_specs,
        out_shape=out_shape,
        scratch_shapes=[pltpu.VMEM((tm, tn), F32)],
        args=(a, b, *extras), name=name, comm=comm)
    res = outs[0] if n_out == 1 else tuple(outs)
    return res if comm is None else (res, couts)


def _rms_fwd(x, g, name, comm=None):
    s, d = x.shape
    tr = _tile(s, 256)

    def body(x_ref, g_ref, h_ref, r_ref):
        xf = x_ref[...]
        r = lax.rsqrt(jnp.mean(xf * xf, axis=-1, keepdims=True) + EPS)
        h_ref[...] = ((xf * r) * g_ref[...]).astype(h_ref.dtype)
        r_ref[...] = r

    outs, couts = _call(
        body,
        grid=(s // tr,),
        in_specs=[pl.BlockSpec((tr, d), lambda i: (i, 0)), pl.BlockSpec((1, d), lambda i: (0, 0))],
        out_specs=[pl.BlockSpec((tr, d), lambda i: (i, 0)), pl.BlockSpec((tr, 1), lambda i: (i, 0))],
        out_shape=[jax.ShapeDtypeStruct((s, d), MXU), jax.ShapeDtypeStruct((s, 1), F32)],
        args=(x, g), name=name, comm=comm)
    return tuple(outs) if comm is None else (tuple(outs), couts)


def _loss_head(x, g, target, name):
    s, d = x.shape
    tr = _tile(s, 256)

    def body(x_ref, g_ref, t_ref, loss_ref, dx_ref, dg_ref):
        @pl.when(pl.program_id(0) == 0)
        def _():
            dg_ref[...] = jnp.zeros_like(dg_ref)
            loss_ref[...] = jnp.zeros_like(loss_ref)

        xf = x_ref[...]
        gv = g_ref[...]
        r = lax.rsqrt(jnp.mean(xf * xf, axis=-1, keepdims=True) + EPS)
        xhat = xf * r
        err = xhat * gv - t_ref[...]
        loss_ref[...] += 0.5 * jnp.sum(jnp.mean(err * err, axis=-1, keepdims=True), axis=0, keepdims=True)
        dy = err * (1.0 / d)
        dxhat = dy * gv
        mean = jnp.mean(dxhat * xhat, axis=-1, keepdims=True)
        dx_ref[...] = r * (dxhat - xhat * mean)
        dg_ref[...] += jnp.sum(dy * xhat, axis=0, keepdims=True)

    row = pl.BlockSpec((tr, d), lambda i: (i, 0))
    vec = pl.BlockSpec((1, d), lambda i: (0, 0))
    return pl.pallas_call(
        body,
        grid=(s // tr,),
        in_specs=[row, vec, row],
        out_specs=[pl.BlockSpec((1, LANES), lambda i: (0, 0)), row, vec],
        out_shape=[jax.ShapeDtypeStruct((1, LANES), F32), jax.ShapeDtypeStruct((s, d), F32),
                   jax.ShapeDtypeStruct((1, d), F32)],
        compiler_params=_cparams(),
        name=name,
    )(x, g, target)


def _mix_fwd(oa, ob, wa_t, wb_t, gates, name):
    s, kd = oa.shape
    d = wa_t.shape[0]
    tm, tn = _tile(s, 1024), _tile(d, 1024)
    nj = d // tn

    def body(oa_ref, ob_ref, wa_ref, wb_ref, ga_ref, gb_ref, out_ref):
        ya = _dot_nt(oa_ref[...], wa_ref[...])
        yb = _dot_nt(ob_ref[...], wb_ref[...])
        out_ref[...] = (_sigmoid(ga_ref[...].astype(F32)) * ya
                        + _sigmoid(gb_ref[...].astype(F32)) * yb).astype(out_ref.dtype)

    o_spec = pl.BlockSpec((tm, kd), lambda i, j: (i, 0))
    w_spec = pl.BlockSpec((tn, kd), lambda i, j: (j, 0))
    return pl.pallas_call(
        body,
        grid=(s // tm, nj),
        in_specs=[o_spec, o_spec, w_spec, w_spec,
                  pl.BlockSpec((tm, tn), lambda i, j: (i, j)),
                  pl.BlockSpec((tm, tn), lambda i, j: (i, j + nj))],
        out_specs=pl.BlockSpec((tm, tn), lambda i, j: (i, j)),
        out_shape=jax.ShapeDtypeStruct((s, d), MXU),
        compiler_params=_cparams(),
        name=name,
    )(oa, ob, wa_t, wb_t, gates, gates)


def _mix_bwd(dx, w_o, oa, ob, wa_t, wb_t, gates, name):
    s, kd = oa.shape
    d = wa_t.shape[0]
    tm = _tile(s, 512)

    def body(dx_ref, wo_ref, oa_ref, ob_ref, wa_ref, wb_ref, g_ref, dya_ref, dyb_ref, dg_ref):
        dm = _dot_nt(dx_ref[...], wo_ref[...])
        ya = _dot_nt(oa_ref[...], wa_ref[...])
        yb = _dot_nt(ob_ref[...], wb_ref[...])
        sa = _sigmoid(g_ref[:, :d].astype(F32))
        sb = _sigmoid(g_ref[:, d:].astype(F32))
        dya_ref[...] = (dm * sa).astype(dya_ref.dtype)
        dyb_ref[...] = (dm * sb).astype(dyb_ref.dtype)
        dg_ref[:, :d] = (dm * ya * sa * (1.0 - sa)).astype(dg_ref.dtype)
        dg_ref[:, d:] = (dm * yb * sb * (1.0 - sb)).astype(dg_ref.dtype)

    def rows(width):
        return pl.BlockSpec((tm, width), lambda i: (i, 0))

    def whole(arr):
        return pl.BlockSpec(arr.shape, lambda i: (0, 0))

    return pl.pallas_call(
        body,
        grid=(s // tm,),
        in_specs=[rows(d), whole(w_o), rows(kd), rows(kd), whole(wa_t), whole(wb_t), rows(2 * d)],
        out_specs=[rows(d), rows(d), rows(2 * d)],
        out_shape=[jax.ShapeDtypeStruct((s, d), MXU), jax.ShapeDtypeStruct((s, d), MXU),
                   jax.ShapeDtypeStruct((s, 2 * d), MXU)],
        compiler_params=_cparams(),
        name=name,
    )(dx, w_o, oa, ob, wa_t, wb_t, gates)


def _ple(p, w_pe_t, h, w_pg, other, *, backward, name, next_g=None):
    s, kp = p.shape
    d = w_pe_t.shape[0]
    tm, tn = _tile(s, 1024), _tile(d, 1024)
    with_norm = next_g is not None
    assert not (with_norm and (backward or tn != d))

    def body(p_ref, wpe_ref, h_ref, wpg_ref, other_ref, *rest):
        out_refs = rest[1:] if with_norm else rest
        pe = _dot_nt(p_ref[...].astype(MXU), wpe_ref[...])
        gt = _dot(h_ref[...], wpg_ref[...])
        sg = _sigmoid(gt)
        if backward:
            dout = other_ref[...]
            out_refs[0][...] = (dout * sg).astype(out_refs[0].dtype)
            out_refs[1][...] = (dout * pe * sg * (1.0 - sg)).astype(out_refs[1].dtype)
        elif with_norm:
            x_new, h_new, r_new = _residual_norm(pe * sg, other_ref[...], rest[0][...])
            out_refs[0][...] = x_new
            out_refs[1][...] = h_new.astype(out_refs[1].dtype)
            out_refs[2][...] = r_new
        else:
            out_refs[0][...] = other_ref[...] + pe * sg

    t_spec = pl.BlockSpec((tm, tn), lambda i, j: (i, j))
    if backward:
        out_specs, out_shape = [t_spec, t_spec], [jax.ShapeDtypeStruct((s, d), MXU)] * 2
    else:
        out_specs, out_shape = [t_spec], [jax.ShapeDtypeStruct((s, d), F32)]
    in_specs = [pl.BlockSpec((tm, kp), lambda i, j: (i, 0)),
                pl.BlockSpec((tn, kp), lambda i, j: (j, 0)),
                pl.BlockSpec((tm, d), lambda i, j: (i, 0)),
                pl.BlockSpec((d, tn), lambda i, j: (0, j)),
                t_spec]
    args = [p, w_pe_t, h, w_pg, other]
    if with_norm:
        in_specs.append(pl.BlockSpec((1, tn), lambda i, j: (0, j)))
        args.append(next_g)
        out_specs += [t_spec, pl.BlockSpec((tm, 1), lambda i, j: (i, 0))]
        out_shape += [jax.ShapeDtypeStruct((s, d), MXU), jax.ShapeDtypeStruct((s, 1), F32)]
    outs = pl.pallas_call(
        body,
        grid=(s // tm, d // tn),
        in_specs=in_specs,
        out_specs=out_specs,
        out_shape=out_shape,
        compiler_params=_cparams(),
        name=name,
    )(*args)
    return tuple(outs) if (backward or with_norm) else outs[0]


def _split_dot(x, tri):
    hi = x.astype(jnp.bfloat16)
    lo = (x - hi.astype(F32)).astype(jnp.bfloat16)
    return _dot(hi, tri) + _dot(lo, tri)


def _log_sigmoids(z):
    lb = jnp.minimum(z, 0.0) - jnp.log(1.0 + jnp.exp(-jnp.abs(z)))
    return lb, lb - z


def _head_lanes(hh):
    lane = lax.broadcasted_iota(jnp.int32, (1, LANES), 1)
    return jnp.logical_and(lane >= hh * HEAD_DIM, lane < (hh + 1) * HEAD_DIM)


def _sb_fwd(qkv, name, comm=None):
    s = qkv.shape[0]
    tq = _tile(s, SB_TQ)
    nsub = SB_SUB if (s // tq) % SB_SUB == 0 else 1

    def body(q_ref, k_ref, v_ref, o_ref):
        row = lax.broadcasted_iota(jnp.int32, (tq, tq), 0)
        col = lax.broadcasted_iota(jnp.int32, (tq, tq), 1)
        causal = col < row
        tri = jnp.where(row > col, 1.0, 0.0).astype(jnp.bfloat16)
        started = [_sb_fwd_straight(q_ref, k_ref, v_ref, pl.program_id(1) * nsub + sub, sub, tq, causal, tri)
                   for sub in range(nsub)]
        for sub, (block, i, cs, accs) in enumerate(started):
            def top(cs):
                return jnp.maximum(jnp.max(cs[0]), jnp.max(cs[1]))

            def live(st):
                return jnp.logical_and(st[0] >= 0, st[1] > SB_DEAD)

            def walk(st, block=block):
                cs, accs = block(st[0], st[2], st[3], False)
                return st[0] - 1, top(cs), cs, accs

            accs = lax.while_loop(live, walk, (i - 2, top(cs), cs, accs))[3]
            o_ref[sub * tq:(sub + 1) * tq, :] = jnp.where(_head_lanes(0), accs[0], accs[1]).astype(o_ref.dtype)

    outs, couts = _call(
        body,
        grid=(N_PAIR, s // (nsub * tq)),
        in_specs=[pl.BlockSpec((nsub * tq, LANES), lambda p, i: (i, p)),
                  pl.BlockSpec((s, LANES), lambda p, i: (0, N_PAIR + p)),
                  pl.BlockSpec((s, LANES), lambda p, i: (0, 2 * N_PAIR + p))],
        out_specs=[pl.BlockSpec((nsub * tq, LANES), lambda p, i: (i, p))],
        out_shape=[jax.ShapeDtypeStruct((s, SB_W), MXU)],
        args=(qkv, qkv, qkv), name=name, comm=comm)
    return outs[0] if comm is None else (outs[0], couts)


def _sb_fwd_straight(q_ref, k_ref, v_ref, i, sub, tq, causal, tri):
    qf = q_ref[sub * tq:(sub + 1) * tq, :].astype(F32) * SCALE
    qms = [jnp.where(_head_lanes(hh), qf, 0.0).astype(MXU) for hh in range(2)]

    def block(kb, cs, accs, masked, present=None):
        rows = pl.ds(pl.multiple_of(kb * tq, tq), tq)
        ks, vs = k_ref[rows, :], v_ref[rows, :]
        off = 0.0 if present is None else (1.0 - present) * NEG
        new_c, new_acc = [], []
        for hh in range(2):
            lb, lm = _log_sigmoids(_dot_nt(qms[hh], ks))
            if masked:
                lm = jnp.where(causal, lm, 0.0)
            a = jnp.exp(lb + _split_dot(lm, tri) + (cs[hh] + off))
            if masked:
                a = jnp.where(causal, a, 0.0)
            new_acc.append(accs[hh] + _dot(a.astype(MXU), vs))
            row_sum = jnp.sum(lm, axis=1, keepdims=True)
            new_c.append(cs[hh] + (row_sum if present is None else row_sum * present))
        return tuple(new_c), tuple(new_acc)

    zc, za = jnp.zeros((tq, 1), F32), jnp.zeros((tq, LANES), F32)
    cs, accs = block(i, (zc, zc), (za, za), True)
    cs, accs = block(jnp.maximum(i - 1, 0), cs, accs, False, jnp.where(i > 0, 1.0, 0.0))
    return block, i, cs, accs


def _sb_bwd(qkv, do, name, comm=None):
    s = qkv.shape[0]
    tq = _tile(s, SB_TQ)
    nq = s // tq
    nsub = SB_SUB if nq % SB_SUB == 0 else 1
    nsteps = nq // nsub

    def body(q_ref, k_ref, v_ref, do_ref, dq_ref, dk_ref, dv_ref, dk_acc, dv_acc, carries):
        step = pl.program_id(1)

        @pl.when(step == 0)
        def _():
            dk_acc[...] = jnp.zeros_like(dk_acc)
            dv_acc[...] = jnp.zeros_like(dv_acc)

        row = lax.broadcasted_iota(jnp.int32, (tq, tq), 0)
        col = lax.broadcasted_iota(jnp.int32, (tq, tq), 1)
        causal = col < row
        tri_rev = jnp.where(row > col, 1.0, 0.0).astype(jnp.bfloat16)
        tri_excl = jnp.where(row < col, 1.0, 0.0).astype(jnp.bfloat16)
        zc, za = jnp.zeros((tq, 1), F32), jnp.zeros((tq, LANES), F32)

        def top(cs):
            return jnp.maximum(jnp.max(cs[0]), jnp.max(cs[1]))

        def live(st):
            return jnp.logical_and(st[0] >= 0, st[1] > SB_DEAD)

        def row_sums(pre):
            return [jnp.sum(lm, axis=1, keepdims=True) for _, lm in pre]

        def query_block(sub):
            i = step * nsub + sub
            q_rows = slice(sub * tq, (sub + 1) * tq)
            qf = q_ref[q_rows, :].astype(F32) * SCALE
            dof = do_ref[q_rows, :]
            qms = [jnp.where(_head_lanes(hh), qf, 0.0).astype(MXU) for hh in range(2)]
            doms = [jnp.where(_head_lanes(hh), dof, jnp.zeros_like(dof)) for hh in range(2)]

            def terms(kb, masked):
                rows = pl.ds(pl.multiple_of(kb * tq, tq), tq)
                ks = k_ref[rows, :]
                out = []
                for hh in range(2):
                    lb, lm = _log_sigmoids(_dot_nt(qms[hh], ks))
                    if masked:
                        lm = jnp.where(causal, lm, 0.0)
                    out.append((lb, lm))
                return out

            def block(kb, cs, gpres, dqs, masked, gate=None, pre=None):
                rows = pl.ds(pl.multiple_of(kb * tq, tq), tq)
                ks, vs = k_ref[rows, :], v_ref[rows, :]
                pre = terms(kb, masked) if pre is None else pre
                new_g, new_dq = [], []
                dk_add, dv_add = None, None
                for hh in range(2):
                    lb, lm = pre[hh]
                    off = 0.0 if gate is None else (1.0 - gate) * NEG
                    a = jnp.exp(lb + _split_dot(lm, tri_rev) + (cs[hh] + off))
                    if masked:
                        a = jnp.where(causal, a, 0.0)
                    g = a * _dot_nt(doms[hh], vs)
                    gsum = gpres[hh] + _split_dot(g, tri_excl)
                    dz = g - (g + gsum) * jnp.exp(lb)
                    if masked:
                        dz = jnp.where(causal, dz, 0.0)
                    dzb = dz.astype(MXU)
                    new_dq.append(dqs[hh] + _dot(dzb, ks))
                    dk_h = _dot_tn(dzb, qms[hh])
                    dv_h = _dot_tn(a.astype(MXU), doms[hh])
                    dk_add = dk_h if dk_add is None else dk_add + dk_h
                    dv_add = dv_h if dv_add is None else dv_add + dv_h
                    new_g.append(gpres[hh] + jnp.sum(g, axis=1, keepdims=True))
                dk_acc[rows, :] += dk_add
                dv_acc[rows, :] += dv_add
                return tuple(new_g), tuple(new_dq)

            prev = jnp.maximum(i - 1, 0)
            gate = jnp.where(i > 0, 1.0, 0.0)
            t_diag, t_prev = terms(i, True), terms(prev, False)
            c_diag = row_sums(t_diag)
            sums = row_sums(t_prev)
            c_prev = tuple(c_diag[hh] + sums[hh] * gate for hh in range(2))
            return dict(i=i, prev=prev, gate=gate, q_rows=q_rows, terms=terms, block=block,
                        t_diag=t_diag, t_prev=t_prev, c_diag=c_diag, c_prev=c_prev)

        blocks = [query_block(sub) for sub in range(nsub)]
        for qb in blocks:
            def record(st, qb=qb):
                kb, cs = st[0], st[2]
                sums = row_sums(qb["terms"](kb, False))
                for hh in range(2):
                    carries[hh, kb] = cs[hh]
                cs = tuple(cs[hh] + sums[hh] for hh in range(2))
                return kb - 1, top(cs), cs

            first = lax.while_loop(live, record, (qb["i"] - 2, top(qb["c_prev"]), qb["c_prev"]))[0] + 1
            qb["mid"] = lax.fori_loop(
                first, qb["i"] - 1,
                lambda kb, cr, qb=qb: qb["block"](kb, (carries[0, kb], carries[1, kb]), cr[0], cr[1], False),
                ((zc, zc), (za, za)))
        for qb in blocks:
            gpres, dqs = qb["block"](qb["prev"], qb["c_diag"], *qb["mid"], False, qb["gate"], qb["t_prev"])
            dqs = qb["block"](qb["i"], (zc, zc), gpres, dqs, True, None, qb["t_diag"])[1]
            dq_ref[qb["q_rows"], :] = (jnp.where(_head_lanes(0), dqs[0], dqs[1]) * SCALE).astype(dq_ref.dtype)

        @pl.when(step == nsteps - 1)
        def _():
            dk_ref[...] = dk_acc[...].astype(dk_ref.dtype)
            dv_ref[...] = dv_acc[...].astype(dv_ref.dtype)

    blk = pl.BlockSpec((nsub * tq, LANES), lambda p, i: (i, p))
    full = pl.BlockSpec((s, LANES), lambda p, i: (0, p))
    outs, couts = _call(
        body,
        grid=(N_PAIR, nsteps),
        in_specs=[blk,
                  pl.BlockSpec((s, LANES), lambda p, i: (0, N_PAIR + p)),
                  pl.BlockSpec((s, LANES), lambda p, i: (0, 2 * N_PAIR + p)),
                  blk],
        out_specs=[blk, full, full],
        out_shape=[jax.ShapeDtypeStruct((s, SB_W), MXU)] * 3,
        scratch_shapes=[pltpu.VMEM((s, LANES), F32), pltpu.VMEM((s, LANES), F32),
                        pltpu.VMEM((2, nq, tq, 1), F32)],
        args=(qkv, qkv, qkv, do), name=name, comm=comm)
    return tuple(outs) if comm is None else (tuple(outs), couts)


def _bucket_table():
    i = np.arange(BLOCK)[:, None]
    j = np.arange(2 * BLOCK)[None, :]
    d = np.maximum(BLOCK + i - j, 0)
    max_exact = N_BUCKETS // 2
    df = np.maximum(d, 1).astype(np.float32)
    large = max_exact + (np.log(df / max_exact) / math.log(MAX_DISTANCE / max_exact)
                         * (N_BUCKETS - max_exact)).astype(np.int32)
    large = np.minimum(large, N_BUCKETS - 1)
    return np.where(d < max_exact, d, large).astype(np.int32)


def _build_bias(rel_bias, buckets, name):
    def body(rb_ref, bk_ref, out_ref):
        h = pl.program_id(0)
        bk = bk_ref[...]
        acc = jnp.zeros(bk.shape, F32)
        for b in range(N_BUCKETS):
            acc = jnp.where(bk == b, rb_ref[b, h], acc)
        out_ref[...] = acc

    return pl.pallas_call(
        body,
        grid=(SW_HEADS,),
        in_specs=[pl.BlockSpec(memory_space=pltpu.SMEM),
                  pl.BlockSpec((BLOCK, 2 * BLOCK), lambda h: (0, 0))],
        out_specs=pl.BlockSpec((None, BLOCK, 2 * BLOCK), lambda h: (h, 0, 0)),
        out_shape=jax.ShapeDtypeStruct((SW_HEADS, BLOCK, 2 * BLOCK), F32),
        name=name,
    )(rel_bias, buckets)


def _bias_grad(dbias_layers, buckets, name):
    n_l = len(dbias_layers)

    def body(*refs):
        bk = refs[n_l][...]
        out_ref = refs[n_l + 1]
        db = refs[0][...]
        for r in refs[1:n_l]:
            db = db + r[...]
        lane = lax.broadcasted_iota(jnp.int32, (1, LANES), 1)
        acc = jnp.zeros((1, LANES), F32)
        for b in range(N_BUCKETS):
            part = jnp.sum(jnp.where(bk == b, db, 0.0), axis=1, keepdims=True)
            tot = jnp.sum(part, axis=0, keepdims=True)
            acc = jnp.where(lane == b, tot, acc)
        out_ref[...] = acc

    hspec = pl.BlockSpec((None, BLOCK, 2 * BLOCK), lambda h: (h, 0, 0))
    return pl.pallas_call(
        body,
        grid=(SW_HEADS,),
        in_specs=[hspec] * n_l + [pl.BlockSpec((BLOCK, 2 * BLOCK), lambda h: (0, 0))],
        out_specs=pl.BlockSpec((None, 1, LANES), lambda h: (h, 0, 0)),
        out_shape=jax.ShapeDtypeStruct((SW_HEADS, 1, LANES), F32),
        name=name,
    )(*dbias_layers, buckets)


GROUP_ROWS = SW_GROUP * BLOCK


def _group_lanes(g):
    lane = lax.broadcasted_iota(jnp.int32, (1, LANES), 1)
    gvec = jnp.zeros((1, LANES), jnp.int32) + g
    return jnp.where(lane >= HEAD_DIM, 1, 0) == gvec, gvec


def _stack_heads(x, g):
    kv_lanes, gvec = _group_lanes(g)
    parts = []
    for j in range(SW_GROUP):
        half = x[:, (j // 2) * LANES:(j // 2 + 1) * LANES]
        moved = jnp.where(gvec == j % 2, half, pltpu.roll(half, HEAD_DIM, 1))
        parts.append(jnp.where(kv_lanes, moved, 0.0))
    return jnp.concatenate(parts, axis=0)


def _unstack_heads(y, g):
    _, gvec = _group_lanes(g)
    heads = []
    for j in range(SW_GROUP):
        yj = y[j * BLOCK:(j + 1) * BLOCK]
        heads.append(jnp.where(gvec == j % 2, yj, pltpu.roll(yj, HEAD_DIM, 1)))
    pairs = [jnp.where(_head_lanes(0), heads[2 * p], heads[2 * p + 1]) for p in range(SW_GROUP // 2)]
    return jnp.concatenate(pairs, axis=1)


def _per_head_col(values):
    return jnp.concatenate([jnp.zeros((BLOCK, 1), F32) + v for v in values], axis=0)


def _swa_scores(qs, kp, kc, bias_ref, n):
    row = jnp.bitwise_and(lax.broadcasted_iota(jnp.int32, (GROUP_ROWS, BLOCK), 0), BLOCK - 1)
    col = lax.broadcasted_iota(jnp.int32, (GROUP_ROWS, BLOCK), 1)
    bias = bias_ref[...].reshape(GROUP_ROWS, 2 * BLOCK)
    s1 = _dot_nt(qs, kp) + bias[:, :BLOCK]
    s2 = _dot_nt(qs, kc) + bias[:, BLOCK:]
    no_prev = jnp.where(n > 0, 0, BLOCK)
    s1 = jnp.where(col > row + no_prev, s1, NEG)
    s2 = jnp.where(col <= row, s2, NEG)
    return s1, s2


def _swa_specs(s):
    q_blk = 3 * SB_W // (2 * LANES)
    k_blk = (3 * SB_W + SW_W) // LANES
    return (pl.BlockSpec((s, 2 * LANES), lambda g: (0, q_blk + g)),
            pl.BlockSpec((s, LANES), lambda g: (0, k_blk)),
            pl.BlockSpec((s, LANES), lambda g: (0, k_blk + 1)))


def _swa_fwd(qkv, bias, sinks, name, comm=None):
    s = qkv.shape[0]
    nb = s // BLOCK

    def body(sink_ref, q_ref, k_ref, v_ref, bias_ref, o_ref, lse_ref):
        g = pl.program_id(0)
        sink = _per_head_col([sink_ref[SW_GROUP * g + j] for j in range(SW_GROUP)])
        lane = lax.broadcasted_iota(jnp.int32, (1, LANES), 1)

        def step(n, carry):
            r0 = pl.multiple_of(n * BLOCK, BLOCK)
            p0 = pl.multiple_of(jnp.maximum(n - 1, 0) * BLOCK, BLOCK)
            cur, prev = pl.ds(r0, BLOCK), pl.ds(p0, BLOCK)
            qs = _stack_heads(q_ref[cur, :].astype(F32) * SCALE, g).astype(MXU)
            s1, s2 = _swa_scores(qs, k_ref[prev, :], k_ref[cur, :], bias_ref, n)
            m = jnp.maximum(jnp.max(jnp.maximum(s1, s2), axis=1, keepdims=True), sink)
            e1 = jnp.exp(s1 - m)
            e2 = jnp.exp(s2 - m)
            den = jnp.sum(e1 + e2, axis=1, keepdims=True) + jnp.exp(sink - m)
            o = _dot((e1 / den).astype(MXU), v_ref[prev, :]) + _dot((e2 / den).astype(MXU), v_ref[cur, :])
            o_ref[cur, :] = _unstack_heads(o, g).astype(o_ref.dtype)
            lse = m + jnp.log(den)
            lse_row = jnp.zeros((BLOCK, LANES), F32)
            for j in range(SW_GROUP):
                lse_row = jnp.where(lane == j, lse[j * BLOCK:(j + 1) * BLOCK], lse_row)
            lse_ref[cur, :] = lse_row
            return carry

        lax.fori_loop(0, nb, step, 0, unroll=2)

    outs, couts = _call(
        body,
        grid=(SW_KV,),
        in_specs=[pl.BlockSpec(memory_space=pltpu.SMEM), *_swa_specs(s),
                  pl.BlockSpec((SW_GROUP, BLOCK, 2 * BLOCK), lambda g: (g, 0, 0))],
        out_specs=[pl.BlockSpec((s, 2 * LANES), lambda g: (0, g)),
                   pl.BlockSpec((None, s, LANES), lambda g: (g, 0, 0))],
        out_shape=[jax.ShapeDtypeStruct((s, SW_W), MXU), jax.ShapeDtypeStruct((SW_KV, s, LANES), F32)],
        args=(sinks, qkv, qkv, qkv, bias), name=name, comm=comm)
    return tuple(outs) if comm is None else (tuple(outs), couts)


def _swa_bwd(qkv, bias, sinks, do, lse, name, comm=None):
    s = qkv.shape[0]
    nb = s // BLOCK

    def body(sink_ref, q_ref, k_ref, v_ref, bias_ref, do_ref, lse_ref,
             dq_ref, dk_ref, dv_ref, dbias_ref, dsink_ref, dk_acc, dv_acc):
        g = pl.program_id(0)
        sink = _per_head_col([sink_ref[SW_GROUP * g + j] for j in range(SW_GROUP)])
        lane = lax.broadcasted_iota(jnp.int32, (1, LANES), 1)

        @pl.when(g == 0)
        def _():
            dk_acc[...] = jnp.zeros_like(dk_acc)
            dv_acc[...] = jnp.zeros_like(dv_acc)

        dbias_ref[...] = jnp.zeros_like(dbias_ref)

        def step(n, dsink_rows):
            r0 = pl.multiple_of(n * BLOCK, BLOCK)
            p0 = pl.multiple_of(jnp.maximum(n - 1, 0) * BLOCK, BLOCK)
            cur, prev = pl.ds(r0, BLOCK), pl.ds(p0, BLOCK)
            qs = _stack_heads(q_ref[cur, :].astype(F32) * SCALE, g).astype(MXU)
            dos = _stack_heads(do_ref[cur, :].astype(F32), g).astype(MXU)
            kp, kc, vp, vc = k_ref[prev, :], k_ref[cur, :], v_ref[prev, :], v_ref[cur, :]
            lse_row = lse_ref[cur, :]
            lse = jnp.concatenate([jnp.sum(jnp.where(lane == j, lse_row, 0.0), axis=1, keepdims=True)
                                   for j in range(SW_GROUP)], axis=0)
            s1, s2 = _swa_scores(qs, kp, kc, bias_ref, n)
            pr1 = jnp.exp(s1 - lse)
            pr2 = jnp.exp(s2 - lse)
            dpr1 = _dot_nt(dos, vp)
            dpr2 = _dot_nt(dos, vc)
            delta = jnp.sum(pr1 * dpr1 + pr2 * dpr2, axis=1, keepdims=True)
            ds1 = pr1 * (dpr1 - delta)
            ds2 = pr2 * (dpr2 - delta)
            dbias_ref[:, :, :BLOCK] += ds1.reshape(SW_GROUP, BLOCK, BLOCK)
            dbias_ref[:, :, BLOCK:] += ds2.reshape(SW_GROUP, BLOCK, BLOCK)
            ds1b, ds2b = ds1.astype(MXU), ds2.astype(MXU)
            dq = _dot(ds1b, kp) + _dot(ds2b, kc)
            dq_ref[cur, :] = (_unstack_heads(dq, g) * SCALE).astype(dq_ref.dtype)
            dk_acc[prev, :] += _dot_tn(ds1b, qs)
            dk_acc[cur, :] += _dot_tn(ds2b, qs)
            dv_acc[prev, :] += _dot_tn(pr1.astype(MXU), dos)
            dv_acc[cur, :] += _dot_tn(pr2.astype(MXU), dos)
            return dsink_rows - jnp.exp(sink - lse) * delta

        rows = lax.fori_loop(0, nb, step, jnp.zeros((GROUP_ROWS, 1), F32), unroll=2)
        for j in range(SW_GROUP):
            dsink_ref[j] = jnp.broadcast_to(jnp.sum(rows[j * BLOCK:(j + 1) * BLOCK], axis=0, keepdims=True),
                                            (1, LANES))

        @pl.when(g == SW_KV - 1)
        def _():
            dk_ref[...] = dk_acc[...].astype(dk_ref.dtype)
            dv_ref[...] = dv_acc[...].astype(dv_ref.dtype)

    grp = pl.BlockSpec((s, 2 * LANES), lambda g: (0, g))
    kv_out = pl.BlockSpec((s, LANES), lambda g: (0, 0))
    bspec = pl.BlockSpec((SW_GROUP, BLOCK, 2 * BLOCK), lambda g: (g, 0, 0))
    outs, couts = _call(
        body,
        grid=(SW_KV,),
        in_specs=[pl.BlockSpec(memory_space=pltpu.SMEM), *_swa_specs(s), bspec, grp,
                  pl.BlockSpec((None, s, LANES), lambda g: (g, 0, 0))],
        out_specs=[grp, kv_out, kv_out, bspec, pl.BlockSpec((SW_GROUP, 1, LANES), lambda g: (g, 0, 0))],
        out_shape=[jax.ShapeDtypeStruct((s, SW_W), MXU),
                   jax.ShapeDtypeStruct((s, LANES), MXU),
                   jax.ShapeDtypeStruct((s, LANES), MXU),
                   jax.ShapeDtypeStruct((SW_HEADS, BLOCK, 2 * BLOCK), F32),
                   jax.ShapeDtypeStruct((SW_HEADS, 1, LANES), F32)],
        scratch_shapes=[pltpu.VMEM((s, LANES), F32), pltpu.VMEM((s, LANES), F32)],
        args=(sinks, qkv, qkv, qkv, bias, do, lse), name=name, comm=comm)
    return tuple(outs) if comm is None else (tuple(outs), couts)


class _NoPlan:
    def comm(self, name):
        return None

    def done(self, name, outs):
        pass

    def grad(self, layer, name, value):
        pass


def _run(plan, fn, *args, name, **kw):
    comm = plan.comm(name)
    if comm is None:
        return fn(*args, name=name, **kw)
    res, outs = fn(*args, name=name, comm=comm, **kw)
    plan.done(name, outs)
    return res


def _norm_bwd(dh, x, dres, r, g):
    xhat = x * r
    dxhat = dh * g
    dx = dres + r * (dxhat - xhat * jnp.mean(dxhat * xhat, axis=-1, keepdims=True))
    return dx, dx, jnp.sum(dh * xhat, axis=0, keepdims=True)


def _residual_norm(acc, res, g):
    x = res + acc
    r = lax.rsqrt(jnp.mean(x * x, axis=-1, keepdims=True) + EPS)
    return x, (x * r) * g, r


def _layer_fwd(x, p, w, g_mix, g_mlp, g_pe, sinks, bias, tag, plan, normed_x=None, next_g=None):
    h1, r1 = normed_x if normed_x is not None else _run(plan, _rms_fwd, x, g_mix, name=f"rms_mix_{tag}")
    qkv = _run(plan, _mm, h1, w["w_qkv_t"], tb=True, out_dtypes=(MXU,), tn=QKV_COLS, name=f"proj_qkv_{tag}")
    gates = _run(plan, _mm, h1, w["w_gate_t"], tb=True, out_dtypes=(MXU,), tn=2048, name=f"proj_gate_{tag}")
    oa = _run(plan, _sb_fwd, qkv, name=f"sb_fwd_{tag}")
    ob, lse = _run(plan, _swa_fwd, qkv, bias, sinks, name=f"swa_fwd_{tag}")
    merged = _mix_fwd(oa, ob, w["w_up_a_t"], w["w_up_b_t"], gates, f"mix_fwd_{tag}")
    d = x.shape[1]
    normed = (F32, MXU, (F32, "col"))
    x1, h2, r2 = _run(plan, _mm, merged, w["w_o"], extras=(x, g_mlp), epi=_residual_norm, out_dtypes=normed,
                      tn=d, name=f"out_proj_{tag}")
    u, act = _run(plan, _mm, h2, w["w_ff1_t"], tb=True,
                  epi=lambda acc: (acc, jnp.square(jnp.maximum(acc, 0.0))),
                  out_dtypes=(MXU, MXU), tn=2048, name=f"ff1_{tag}")
    x2, h3, r3 = _run(plan, _mm, act, w["w_ff2"], extras=(x1, g_pe), epi=_residual_norm, out_dtypes=normed,
                      tn=d, name=f"ff2_{tag}")
    x3 = _ple(p, w["w_pe_t"], h3, w["w_pg"], x2, backward=False, name=f"ple_fwd_{tag}", next_g=next_g)
    next_normed = None
    if next_g is not None:
        x3, next_normed = x3[0], (x3[1], x3[2])
    saved = dict(x=x, h1=h1, r1=r1, gates=gates, qkv=qkv, lse=lse, oa=oa, ob=ob, merged=merged,
                 x1=x1, h2=h2, r2=r2, u=u, act=act, x2=x2, h3=h3, r3=r3)
    return x3, saved, next_normed


def _layer_bwd(dx3, sv, p, w, g_mix, g_mlp, g_pe, sinks, bias, layer, plan):
    tag = f"l{layer}"
    gw = {}
    wire = (WIRE,)

    def dw(name, a, b):
        gw[name] = _run(plan, _mm, a, b, ta=True, out_dtypes=wire, tk=2048, name=f"d{name}_{tag}")
        plan.grad(layer, name, gw[name])

    dpe, dgt = _ple(p, w["w_pe_t"], sv["h3"], w["w_pg"], dx3, backward=True, name=f"ple_bwd_{tag}")
    dw("w_pe", dpe, p)
    dw("w_pg", sv["h3"], dgt)
    d = dx3.shape[1]
    grads = (F32, MXU, (F32, "rowsum"))
    dx2, dx2b, dg_pe = _run(plan, _mm, dgt, w["w_pg"], tb=True, extras=(sv["x2"], dx3, sv["r3"], g_pe),
                            epi=_norm_bwd, out_dtypes=grads, tm=512, tn=d, name=f"dh_pe_{tag}")
    dw("w_ff2", sv["act"], dx2b)
    du = _run(plan, _mm, dx2b, w["w_ff2"], tb=True, extras=(sv["u"],),
              epi=lambda acc, u: acc * (2.0 * jnp.maximum(u.astype(F32), 0.0)), out_dtypes=(MXU,),
              tn=2048, name=f"dact_{tag}")
    dw("w_ff1", du, sv["h2"])
    dx1, dx1b, dg_mlp = _run(plan, _mm, du, w["w_ff1_t"], extras=(sv["x1"], dx2, sv["r2"], g_mlp),
                             epi=_norm_bwd, out_dtypes=grads, tm=1024, tn=d, name=f"dh_mlp_{tag}")
    dw("w_o", sv["merged"], dx1b)
    dya, dyb, dgates = _mix_bwd(dx1b, w["w_o"], sv["oa"], sv["ob"], w["w_up_a_t"], w["w_up_b_t"],
                                sv["gates"], f"mix_bwd_{tag}")
    dw("w_up_a", dya, sv["oa"])
    dw("w_up_b", dyb, sv["ob"])
    doa = _run(plan, _mm, dya, w["w_up_a_t"], out_dtypes=(MXU,), name=f"do_a_{tag}")
    dob = _run(plan, _mm, dyb, w["w_up_b_t"], out_dtypes=(MXU,), name=f"do_b_{tag}")
    dqb, dkb, dvb, dbias, dsink = _run(plan, _swa_bwd, sv["qkv"], bias, sinks, dob, sv["lse"],
                                       name=f"swa_bwd_{tag}")
    dqa, dka, dva = _run(plan, _sb_bwd, sv["qkv"], doa, name=f"sb_bwd_{tag}")
    dqkv = jnp.concatenate([dqa, dka, dva, dqb, dkb, dvb], axis=1)
    gw_qkv = _mm(dqkv, sv["h1"], ta=True, out_dtypes=wire, tk=2048, name=f"dw_qkv_{tag}")
    gw_gate = _mm(dgates, sv["h1"], ta=True, out_dtypes=wire, tk=2048, name=f"dw_gate_{tag}")
    gw["w_in"] = jnp.concatenate([gw_qkv, gw_gate], axis=0)
    plan.grad(layer, "w_in", gw["w_in"])
    dh1 = _run(plan, _mm, dgates, w["w_gate_t"], name=f"dh_gate_{tag}")
    dx, _, dg_mix = _run(plan, _mm, dqkv, w["w_qkv_t"], tk=768, extras=(dh1, sv["x"], dx1, sv["r1"], g_mix),
                         epi=lambda acc, prev, *rest: _norm_bwd(acc + prev, *rest), out_dtypes=grads,
                         tm=512, tn=d, name=f"dh_qkv_{tag}")
    small = dict(g_mix=dg_mix, g_mlp=dg_mlp, g_pe=dg_pe, sinks=dsink[:, 0, 0], dbias=dbias)
    return dx, gw, small


def _local_step(x, p, target, weights, g_mix, g_mlp, g_pe, g_final, sinks, rel_bias, plan=None):
    plan = _NoPlan() if plan is None else plan
    depth = g_mix.shape[0]
    buckets = jnp.asarray(_bucket_table())
    bias = _build_bias(rel_bias, buckets, "build_bias")
    saved, wfull = [], []
    h, normed = x, None
    for l in range(depth):
        wfull.append(weights(l))
        next_g = g_mix[l + 1:l + 2] if l + 1 < depth else None
        h, sv, normed = _layer_fwd(h, p[l], wfull[l], g_mix[l:l + 1], g_mlp[l:l + 1], g_pe[l:l + 1],
                                   sinks[l], bias, f"l{l}", plan, normed, next_g)
        saved.append(sv)
    loss_row, dx, dg_final = _loss_head(h, g_final[None, :], target, "loss_head")
    gws = [None] * depth
    smalls = [None] * depth
    for l in reversed(range(depth)):
        dx, gws[l], smalls[l] = _layer_bwd(dx, saved[l], p[l], wfull[l], g_mix[l:l + 1], g_mlp[l:l + 1],
                                           g_pe[l:l + 1], sinks[l], bias, l, plan)
    drel = _bias_grad([sm["dbias"] for sm in smalls], buckets, "bias_grad")[:, 0, :N_BUCKETS].T
    small = dict(
        g_mix=jnp.concatenate([sm["g_mix"] for sm in smalls], axis=0),
        g_mlp=jnp.concatenate([sm["g_mlp"] for sm in smalls], axis=0),
        g_pe=jnp.concatenate([sm["g_pe"] for sm in smalls], axis=0),
        g_final=dg_final[0],
        sinks=jnp.stack([sm["sinks"] for sm in smalls], axis=0),
        rel_bias=drel,
    )
    return loss_row, dx, gws, small


MESH_ID = pl.DeviceIdType.MESH


def _position():
    return lax.axis_index("x"), lax.axis_index("y"), lax.axis_index("c")


def _gather_comm(shards):
    n = len(shards)

    def copies(pos, x_refs, out_refs, sems):
        send_sems, recv_sems, local_sems = sems
        x, y, c = pos
        me, sibling = (x, y, c), (x, y, 1 - c)
        chips = [(1 - x, y), (x, 1 - y), (1 - x, 1 - y)]

        def slot(a, px, py, pc):
            return out_refs[a].at[4 * px + 2 * py + pc]

        def copy(a, k, block, to, src=None):
            return pltpu.make_async_remote_copy(
                src_ref=slot(a, *block) if src is None else src, dst_ref=slot(a, *block),
                send_sem=send_sems.at[a, k], recv_sem=recv_sems.at[a, k],
                device_id=to, device_id_type=MESH_ID)

        mine = [pltpu.make_async_copy(x_refs[a], slot(a, *me), local_sems.at[a]) for a in range(n)]
        first = []
        for a in range(n):
            first.append(copy(a, 0, me, sibling, src=x_refs[a]))
            first += [copy(a, 1 + j, me, (*chip, c), src=x_refs[a]) for j, chip in enumerate(chips)]
        return me, sibling, chips, copy, mine, first

    def start(pos, x_refs, out_refs, sems):
        _, _, _, _, mine, first = copies(pos, x_refs, out_refs, sems)
        for cp in mine + first:
            cp.start()

    def finish(pos, x_refs, out_refs, sems):
        me, sibling, chips, copy, mine, first = copies(pos, x_refs, out_refs, sems)
        c = pos[2]
        passed = []
        for j, chip in enumerate(chips):
            for a in range(n):
                copy(a, 1 + j, (*chip, c), me).wait_recv()
                fwd = copy(a, 4 + j, (*chip, c), sibling)
                fwd.start()
                passed.append(fwd)
        for a in range(n):
            copy(a, 0, sibling, me).wait_recv()
            for j, chip in enumerate(chips):
                copy(a, 4 + j, (*chip, 1 - c), me).wait_recv()
        for cp in first + passed:
            cp.wait_send()
        for cp in mine:
            cp.wait()

    return _Comm(shards, [jax.ShapeDtypeStruct((N_DEV,) + s.shape, s.dtype) for s in shards],
                 [pltpu.SemaphoreType.DMA((n, 7)), pltpu.SemaphoreType.DMA((n, 7)),
                  pltpu.SemaphoreType.DMA((n,))], start, finish)


def _exchange_comm(arrays, n_slots, route):
    n = len(arrays)

    def copies(pos, in_refs, out_refs, sems):
        send_sems, recv_sems = sems
        out = []
        for a in range(n):
            for j in range(n_slots):
                src_slot, peer = route(pos, j)
                out.append(pltpu.make_async_remote_copy(
                    src_ref=in_refs[a].at[src_slot], dst_ref=out_refs[a].at[j],
                    send_sem=send_sems.at[a, j], recv_sem=recv_sems.at[a, j],
                    device_id=peer, device_id_type=MESH_ID))
        return out

    def start(pos, in_refs, out_refs, sems):
        for cp in copies(pos, in_refs, out_refs, sems):
            cp.start()

    def finish(pos, in_refs, out_refs, sems):
        for cp in copies(pos, in_refs, out_refs, sems):
            cp.wait()

    return _Comm(arrays, [jax.ShapeDtypeStruct((n_slots,) + g.shape[1:], g.dtype) for g in arrays],
                 [pltpu.SemaphoreType.DMA((n, n_slots)), pltpu.SemaphoreType.DMA((n, n_slots))], start, finish)


def _rs_sibling_comm(gs):
    return _exchange_comm(gs, 4, lambda pos, j: (2 * j + (1 - pos[2]), (pos[0], pos[1], 1 - pos[2])))


def _chip_of(k, x, y):
    return x ^ ((k + 1) & 1), y ^ (((k + 1) >> 1) & 1)


def _chip_partials(pos, gs, recvs, name):
    n = len(gs)

    def body(pos_ref, *refs):
        for a in range(n):
            refs[2 * n + a][...] = (refs[a][...].astype(F32) + refs[n + a][...].astype(F32)
                                    ).astype(refs[2 * n + a].dtype)

    def g_map(k, pos_ref):
        cx, cy = _chip_of(k, pos_ref[0], pos_ref[1])
        return (4 * cx + 2 * cy + pos_ref[2], 0, 0)

    def r_map(k, pos_ref):
        cx, cy = _chip_of(k, pos_ref[0], pos_ref[1])
        return (2 * cx + cy, 0, 0)

    slab = [(None,) + g.shape[1:] for g in gs]
    return pl.pallas_call(
        body,
        grid_spec=pltpu.PrefetchScalarGridSpec(
            num_scalar_prefetch=1,
            grid=(4,),
            in_specs=[pl.BlockSpec(sh, g_map) for sh in slab] + [pl.BlockSpec(sh, r_map) for sh in slab],
            out_specs=[pl.BlockSpec(sh, lambda k, pos_ref: (k, 0, 0)) for sh in slab],
        ),
        out_shape=[jax.ShapeDtypeStruct((4,) + g.shape[1:], g.dtype) for g in gs],
        compiler_params=_cparams(),
        name=name,
    )(pos, *gs, *recvs)


def _rs_chips_comm(parts):
    return _exchange_comm(parts, 3, lambda pos, k: (k, (*_chip_of(k, pos[0], pos[1]), pos[2])))


def _adamw_math(w, g, m, v):
    m = ADAM_B1 * m + (1.0 - ADAM_B1) * g
    v = ADAM_B2 * v + (1.0 - ADAM_B2) * (g * g)
    m_hat = m / (1.0 - ADAM_B1 ** ADAM_STEP)
    v_hat = v / (1.0 - ADAM_B2 ** ADAM_STEP)
    delta = -ADAM_LR * (m_hat / (jnp.sqrt(v_hat) + ADAM_EPS) + ADAM_WD * w)
    return delta, m, v


def _adamw_weight(parts, recvs, w, m, v, name, grad_t=False, comm=None):
    depth, a, b = w.shape
    ta = _tile(a, 288, unit=LANES if grad_t else 16)
    ni = a // ta
    g_block = (b, ta) if grad_t else (ta, b)

    def body(*refs):
        p_refs, r_refs = refs[:depth], refs[depth:2 * depth]
        w_ref, m_ref, v_ref = refs[2 * depth:2 * depth + 3]
        g_out, d_out, m_out, v_out = refs[2 * depth + 3:]
        layer = pl.program_id(0)
        g = jnp.zeros(g_block, F32)
        for l in range(depth):
            gl = p_refs[l][...].astype(F32)
            for k in range(3):
                gl = gl + r_refs[l][k].astype(F32)
            g = jnp.where(layer == l, gl, g)
        if grad_t:
            g = g.T
        delta, m_new, v_new = _adamw_math(w_ref[...], g, m_ref[...], v_ref[...])
        g_out[...] = g
        d_out[...] = delta
        m_out[...] = m_new
        v_out[...] = v_new

    def hold(l):
        return lambda layer, i: jnp.where(layer == l, i, jnp.where(layer < l, 0, ni - 1))

    def g_index(slot, f):
        if grad_t:
            return lambda layer, i: (slot, 0, f(layer, i))
        return lambda layer, i: (slot, f(layer, i), 0)

    p_specs = [pl.BlockSpec((None,) + g_block, g_index(3, hold(l))) for l in range(depth)]
    r_specs = [pl.BlockSpec((3,) + g_block, g_index(0, hold(l))) for l in range(depth)]
    row = pl.BlockSpec((None, ta, b), lambda layer, i: (layer, i, 0))
    outs, couts = _call(
        body,
        grid=(depth, ni),
        in_specs=p_specs + r_specs + [row, row, row],
        out_specs=[row] * 4,
        out_shape=[jax.ShapeDtypeStruct(w.shape, F32)] * 4,
        args=(*parts, *recvs, w, m, v), name=name, comm=comm)
    return outs if comm is None else (outs, couts)


def _adamw_replicated(gathered, w, m, v, name):
    r, lanes = w.shape

    def body(g_ref, w_ref, m_ref, v_ref, g_out, d_out, m_out, v_out):
        g = g_ref[0]
        for k in range(1, N_DEV):
            g = g + g_ref[k]
        delta, m_new, v_new = _adamw_math(w_ref[...], g, m_ref[...], v_ref[...])
        g_out[...] = g
        d_out[...] = delta
        m_out[...] = m_new
        v_out[...] = v_new

    return pl.pallas_call(
        body,
        out_shape=[jax.ShapeDtypeStruct((r, lanes), F32)] * 4,
        name=name,
    )(gathered, w, m, v)


def _wire_shard(name, shard):
    return (shard.T if name in COL_SHARDED else shard).astype(WIRE)


def _full_weight(gathered):
    return gathered.reshape(N_DEV * gathered.shape[1], gathered.shape[2])


def _to_slabs(gfull):
    return gfull.reshape(N_DEV, gfull.shape[0] // N_DEV, gfull.shape[1])


def _pack_small(arrs):
    rows = []
    for a in arrs:
        flat = a.astype(F32).reshape(-1)
        pad = (-flat.shape[0]) % LANES
        rows.append(jnp.pad(flat, (0, pad)).reshape(-1, LANES))
    packed = jnp.concatenate(rows, axis=0)
    return jnp.pad(packed, ((0, (-packed.shape[0]) % 8), (0, 0)))


def _unpack_small(packed, shapes):
    out, off = [], 0
    for shp in shapes:
        n = math.prod(shp)
        rows = -(-n // LANES)
        out.append(packed[off:off + rows].reshape(-1)[:n].reshape(shp))
        off += rows
    return out


def _of(layer, *names):
    return tuple((layer, n) for n in names)


MLP_W = ("w_ff1", "w_ff2", "w_pe", "w_pg")

GATHERS = (
    ("rms_mix_l0", _of(0, "w_in")),
    ("proj_qkv_l0", _of(0, "w_up_a", "w_up_b", "w_o")),
    ("proj_gate_l0", _of(0, "w_pe", "w_pg")),
    ("sb_fwd_l0", _of(0, "w_ff1", "w_ff2")),
    ("swa_fwd_l0", _of(1, "w_in")),
    ("sb_fwd_l1", _of(1, "w_up_a", "w_up_b", "w_o", "w_pe", "w_pg", "w_ff1")),
    ("swa_fwd_l1", _of(1, "w_ff2")),
)
REDUCES = (
    (_of(1, "w_ff1"), "dw_ff2_l0", "dact_l0"),
    (_of(1, "w_ff2"), "dw_ff2_l0", "dw_ff1_l0"),
    (_of(1, "w_in"), "dw_ff2_l0", "sb_bwd_l0"),
    (_of(1, "w_up_a", "w_up_b", "w_o", "w_pe", "w_pg"), "dw_ff2_l0", "swa_bwd_l0"),
    (_of(0, *MLP_W), "dh_mlp_l0", "sb_bwd_l0"),
    (_of(0, "w_o", "w_up_a", "w_up_b"), "do_a_l0", "sb_bwd_l0"),
    (_of(0, "w_in"), "dh_gate_l0", "dh_qkv_l0"),
)


def _merge_comms(comms):
    if len(comms) == 1:
        return comms[0]

    def cuts(counts):
        edges = [0]
        for c in counts:
            edges.append(edges[-1] + c)
        return [slice(a, b) for a, b in zip(edges[:-1], edges[1:])]

    s_in = cuts([len(c.inputs) for c in comms])
    s_out = cuts([len(c.out_shapes) for c in comms])
    s_sem = cuts([len(c.sems) for c in comms])

    def start(pos, cin, cout, csem):
        for c, i, o, s in zip(comms, s_in, s_out, s_sem):
            c.start(pos, cin[i], cout[o], csem[s])

    def finish(pos, cin, cout, csem):
        for c, i, o, s in zip(comms, s_in, s_out, s_sem):
            c.finish(pos, cin[i], cout[o], csem[s])

    return _Comm(sum([c.inputs for c in comms], []), sum([c.out_shapes for c in comms], []),
                 sum([c.sems for c in comms], []), start, finish)


class _LayerWeights:
    def __init__(self, full, layer):
        self.full, self.layer, self.cache = full, layer, {}

    def __getitem__(self, name):
        if name not in self.cache:
            if name == "w_qkv_t":
                self.cache[name] = self.full[(self.layer, "w_in")][:QKV_COLS]
            elif name == "w_gate_t":
                self.cache[name] = self.full[(self.layer, "w_in")][QKV_COLS:]
            else:
                base = name[:-2] if name.endswith("_t") else name
                assert (base in COL_SHARDED) == name.endswith("_t"), name
                self.cache[name] = self.full[(self.layer, base)]
        return self.cache[name]


class _Plan:
    def __init__(self, w_sh, pos):
        self.w_sh = dict(zip(WEIGHTS, w_sh))
        self.pos = pos
        self.full, self.gw, self.parts, self.recv = {}, {}, {}, {}
        self.slabs = {}
        self.hosted = {}
        for i, (host, _) in enumerate(GATHERS):
            self.hosted.setdefault(host, []).append(("gather", i))
        for i, (_, sib_host, chip_host) in enumerate(REDUCES):
            self.hosted.setdefault(sib_host, []).append(("sibling", i))
            self.hosted.setdefault(chip_host, []).append(("chips", i))

    def _gather(self, i):
        return _gather_comm([_wire_shard(n, self.w_sh[n][layer]) for layer, n in GATHERS[i][1]])

    def _gathered(self, i, outs):
        for (layer, n), g in zip(GATHERS[i][1], outs):
            self.full[(layer, n)] = _full_weight(g)

    def weights(self, layer):
        return _LayerWeights(self.full, layer)

    def grad(self, layer, name, value):
        self.gw[(layer, name)] = value

    def _sibling(self, i):
        self.slabs[i] = [_to_slabs(self.gw[item]) for item in REDUCES[i][0]]
        return _rs_sibling_comm(self.slabs[i])

    def _sibling_done(self, i, outs):
        parts = _chip_partials(self.pos, self.slabs[i], outs, f"chip_partials_{i}")
        for item, part in zip(REDUCES[i][0], parts):
            self.parts[item] = part

    def _chips(self, i):
        return _rs_chips_comm([self.parts[item] for item in REDUCES[i][0]])

    def _chips_done(self, i, outs):
        for item, r in zip(REDUCES[i][0], outs):
            self.recv[item] = r

    def comm(self, name):
        if name not in self.hosted:
            return None
        make = {"gather": self._gather, "sibling": self._sibling, "chips": self._chips}
        return _merge_comms([make[kind](i) for kind, i in self.hosted[name]])

    def done(self, name, outs):
        took = {"gather": self._gathered, "sibling": self._sibling_done, "chips": self._chips_done}
        off = 0
        for kind, i in self.hosted[name]:
            n = len(GATHERS[i][1]) if kind == "gather" else len(REDUCES[i][0])
            took[kind](i, outs[off:off + n])
            off += n


def kernel(x, p, w_in, w_up_a, w_up_b, w_o, w_ff1, w_ff2, w_pe, w_pg, g_mix, g_mlp, g_pe, g_final, sinks, rel_bias, loss_target, m_w_in, m_w_up_a, m_w_up_b, m_w_o, m_w_ff1, m_w_ff2, m_w_pe, m_w_pg, m_g_mix, m_g_mlp, m_g_pe, m_g_final, m_sinks, m_rel_bias, v_w_in, v_w_up_a, v_w_up_b, v_w_o, v_w_ff1, v_w_ff2, v_w_pe, v_w_pg, v_g_mix, v_g_mlp, v_g_pe, v_g_final, v_sinks, v_rel_bias):
    w_sh = [w_in, w_up_a, w_up_b, w_o, w_ff1, w_ff2, w_pe, w_pg]
    m_sh = [m_w_in, m_w_up_a, m_w_up_b, m_w_o, m_w_ff1, m_w_ff2, m_w_pe, m_w_pg]
    v_sh = [v_w_in, v_w_up_a, v_w_up_b, v_w_o, v_w_ff1, v_w_ff2, v_w_pe, v_w_pg]
    depth = w_in.shape[0]
    assert depth == 2 and x.shape[-1] * 2 + QKV_COLS == w_in.shape[2] * N_DEV

    px, py, pc = _position()
    plan = _Plan(w_sh, jnp.stack([px, py, pc]).astype(jnp.int32))
    loss_row, grad_x, _, small = _local_step(
        x[0], p[:, 0], loss_target[0], plan.weights, g_mix, g_mlp, g_pe, g_final, sinks, rel_bias, plan=plan)

    small_g = _pack_small([small[n] for n in SMALL] + [loss_row[0, :1]])
    grad_w, delta_w, new_m, new_v = [], [], [], []
    for a, name in enumerate(WEIGHTS):
        parts = [plan.parts[(l, name)] for l in range(depth)]
        recvs = [plan.recv[(l, name)] for l in range(depth)]
        if name == "w_in":
            flip = lambda t: t.transpose(0, 2, 1)
            outs, (small_all,) = _adamw_weight(parts, recvs, flip(w_sh[a]), flip(m_sh[a]), flip(v_sh[a]),
                                               f"adamw_{name}", comm=_gather_comm([small_g]))
            outs = [flip(o) for o in outs]
        else:
            outs = _adamw_weight(parts, recvs, w_sh[a], m_sh[a], v_sh[a], f"adamw_{name}",
                                 grad_t=name in COL_SHARDED)
        for lst, o in zip((grad_w, delta_w, new_m, new_v), outs):
            lst.append(o)

    small_w = [g_mix, g_mlp, g_pe, g_final, sinks, rel_bias]
    small_m = [m_g_mix, m_g_mlp, m_g_pe, m_g_final, m_sinks, m_rel_bias]
    small_v = [v_g_mix, v_g_mlp, v_g_pe, v_g_final, v_sinks, v_rel_bias]
    small_shapes = [a.shape for a in small_w] + [(1,)]
    zero = jnp.zeros((1,), F32)
    packed_s = _adamw_replicated(small_all, _pack_small(small_w + [zero]), _pack_small(small_m + [zero]),
                                 _pack_small(small_v + [zero + 1.0]), "adamw_replicated")
    sg, sd, sm, sv = [_unpack_small(t, small_shapes) for t in packed_s]
    loss = sg[-1][0]

    return (loss, grad_x[None], *grad_w, *sg[:-1], *delta_w, *sd[:-1], *new_m, *sm[:-1], *new_v, *sv[:-1])
```

```python
import functools
import math

import numpy as np
import jax
import jax.numpy as jnp
from jax import lax
from jax.experimental import pallas as pl
from jax.experimental.pallas import tpu as pltpu

F32 = jnp.float32
MXU = jnp.bfloat16
WIRE = jnp.bfloat16

HEAD_DIM = 64
SB_HEADS = 8
SW_HEADS = 8
SW_KV = 2
SW_GROUP = SW_HEADS // SW_KV
BLOCK = 128
N_BUCKETS = 32
MAX_DISTANCE = 128
EPS = 1e-6
SCALE = HEAD_DIM ** -0.5
SB_W = SB_HEADS * HEAD_DIM
SW_W = SW_HEADS * HEAD_DIM
QKV_COLS = 3 * SB_W + SW_W + 2 * SW_KV * HEAD_DIM
N_DEV = 8
LANES = 128
N_PAIR = SB_HEADS // 2
NEG = -1e30

ADAM_LR = 0.001
ADAM_B1 = 0.9
ADAM_B2 = 0.999
ADAM_EPS = 1e-08
ADAM_WD = 0.01
ADAM_STEP = 10

VMEM_LIMIT = 48 * 1024 * 1024
SB_TQ = 256
SB_DEAD = -105.0
SB_SUB = 4

WEIGHTS = ("w_in", "w_up_a", "w_up_b", "w_o", "w_ff1", "w_ff2", "w_pe", "w_pg")
COL_SHARDED = ("w_in", "w_up_a", "w_up_b", "w_ff1", "w_pe")
SMALL = ("g_mix", "g_mlp", "g_pe", "g_final", "sinks", "rel_bias")


def _cparams(**kw):
    return pltpu.CompilerParams(vmem_limit_bytes=VMEM_LIMIT, **kw)


def _dot(a, b):
    return jnp.dot(a, b, preferred_element_type=F32)


def _dot_nt(a, b):
    return lax.dot_general(a, b, (((1,), (1,)), ((), ())), preferred_element_type=F32)


def _dot_tn(a, b):
    return lax.dot_general(a, b, (((0,), (0,)), ((), ())), preferred_element_type=F32)


def _tile(n, target, unit=LANES):
    if n <= target:
        return n
    t = (target // unit) * unit
    while t > unit and n % t:
        t -= unit
    assert n % t == 0, (n, target)
    return t


def _sigmoid(x):
    return 0.5 * jnp.tanh(0.5 * x) + 0.5


class _Comm:
    def __init__(self, inputs, out_shapes, sems, start, finish):
        self.inputs, self.out_shapes, self.sems = list(inputs), list(out_shapes), list(sems)
        self.start, self.finish = start, finish


def _call(body, *, grid, in_specs, out_specs, out_shape, scratch_shapes=(), args, name, comm=None):
    n_in, n_out, n_scr = len(in_specs), len(out_shape), len(scratch_shapes)
    if comm is None:
        outs = pl.pallas_call(body, grid=grid, in_specs=list(in_specs), out_specs=list(out_specs),
                              out_shape=list(out_shape), scratch_shapes=list(scratch_shapes),
                              compiler_params=_cparams(), name=name)(*args)
        return list(outs), None
    ci, co = len(comm.inputs), len(comm.out_shapes)
    any_spec = pl.BlockSpec(memory_space=pl.ANY)

    def wrapped(*refs):
        ins, cin = refs[:n_in], refs[n_in:n_in + ci]
        o0 = n_in + ci
        outs, cout = refs[o0:o0 + n_out], refs[o0 + n_out:o0 + n_out + co]
        s0 = o0 + n_out + co
        scr, csem = refs[s0:s0 + n_scr], refs[s0 + n_scr:]
        ids = [pl.program_id(d) for d in range(len(grid))]
        first = functools.reduce(jnp.logical_and, [i == 0 for i in ids])
        last = functools.reduce(jnp.logical_and, [i == g - 1 for i, g in zip(ids, grid)])
        pos = (lax.axis_index("x"), lax.axis_index("y"), lax.axis_index("c"))

        @pl.when(first)
        def _():
            comm.start(pos, cin, cout, csem)

        body(*ins, *outs, *scr)

        @pl.when(last)
        def _():
            comm.finish(pos, cin, cout, csem)

    outs = pl.pallas_call(wrapped, grid=grid, in_specs=list(in_specs) + [any_spec] * ci,
                          out_specs=list(out_specs) + [any_spec] * co,
                          out_shape=list(out_shape) + comm.out_shapes,
                          scratch_shapes=list(scratch_shapes) + comm.sems,
                          compiler_params=_cparams(), name=name)(*args, *comm.inputs)
    return list(outs[:n_out]), list(outs[n_out:])


def _accumulate(o_ref, value, first):
    @pl.when(first)
    def _():
        o_ref[...] = value

    @pl.when(jnp.logical_not(first))
    def _():
        o_ref[...] += value


def _mm(a, b, *, ta=False, tb=False, extras=(), epi=None, out_dtypes=(F32,),
        tm=1024, tn=1024, tk=1024, name, comm=None):
    if ta:
        kdim, m = a.shape
    else:
        m, kdim = a.shape
    n = b.shape[0] if tb else b.shape[1]
    assert (b.shape[1] if tb else b.shape[0]) == kdim
    tm, tn, tk = _tile(m, tm), _tile(n, tn), _tile(kdim, tk)
    nk = kdim // tk
    n_ex, n_out = len(extras), len(out_dtypes)

    a_spec = (pl.BlockSpec((tk, tm), lambda i, j, k: (k, i)) if ta
              else pl.BlockSpec((tm, tk), lambda i, j, k: (i, k)))
    b_spec = (pl.BlockSpec((tn, tk), lambda i, j, k: (j, k)) if tb
              else pl.BlockSpec((tk, tn), lambda i, j, k: (k, j)))
    ex_specs = []
    for e in extras:
        assert e.shape in ((m, n), (1, n), (m, 1)), (e.shape, m, n)
        if e.shape == (m, n):
            ex_specs.append(pl.BlockSpec((tm, tn), lambda i, j, k: (i, j)))
        elif e.shape[0] == 1:
            ex_specs.append(pl.BlockSpec((1, tn), lambda i, j, k: (0, j)))
        else:
            ex_specs.append(pl.BlockSpec((tm, 1), lambda i, j, k: (i, 0)))
    out_specs, out_shape, row_sums = [], [], []
    for dt in out_dtypes:
        kind = dt[1] if isinstance(dt, tuple) else "tile"
        row_sums.append(kind == "rowsum")
        if kind == "tile":
            out_specs.append(pl.BlockSpec((tm, tn), lambda i, j, k: (i, j)))
            out_shape.append(jax.ShapeDtypeStruct((m, n), dt))
            continue
        assert tn == n, "per-row and summed outputs need whole rows in one tile"
        if kind == "col":
            out_specs.append(pl.BlockSpec((tm, 1), lambda i, j, k: (i, 0)))
            out_shape.append(jax.ShapeDtypeStruct((m, 1), dt[0]))
        else:
            out_specs.append(pl.BlockSpec((1, tn), lambda i, j, k: (0, 0)))
            out_shape.append(jax.ShapeDtypeStruct((1, n), dt[0]))

    def body(a_ref, b_ref, *rest):
        ex_refs = rest[:n_ex]
        out_refs = rest[n_ex:n_ex + n_out]
        acc = rest[-1]
        k = pl.program_id(2)
        first_rows = pl.program_id(0) == 0

        def prod():
            av = a_ref[...].astype(MXU)
            bv = b_ref[...].astype(MXU)
            return _dot_tn(av, bv) if ta else (_dot_nt(av, bv) if tb else _dot(av, bv))

        def finish(res):
            if epi is not None:
                res = epi(res, *[e[...] for e in ex_refs])
            if not isinstance(res, tuple):
                res = (res,)
            for o_ref, r, summed in zip(out_refs, res, row_sums):
                if summed:
                    _accumulate(o_ref, r.astype(o_ref.dtype), first_rows)
                else:
                    o_ref[...] = r.astype(o_ref.dtype)

        if nk == 1:
            finish(prod())
            return

        @pl.when(k == 0)
        def _():
            acc[...] = prod()

        @pl.when(jnp.logical_and(k > 0, k < nk - 1))
        def _():
            acc[...] += prod()

        @pl.when(k == nk - 1)
        def _():
            finish(acc[...] + prod())

    outs, couts = _call(
        body,
        grid=(m // tm, n // tn, nk),
        in_specs=[a_spec, b_spec] + ex_specs,
        out_specs=out_specs,
        out_shape=out_shape,
        scratch_shapes=[pltpu.VMEM((tm, tn), F32)],
        args=(a, b, *extras), name=name, comm=comm)
    res = outs[0] if n_out == 1 else tuple(outs)
    return res if comm is None else (res, couts)


def _rms_fwd(x, g, name, comm=None):
    s, d = x.shape
    tr = _tile(s, 256)

    def body(x_ref, g_ref, h_ref, r_ref):
        xf = x_ref[...]
        r = lax.rsqrt(jnp.mean(xf * xf, axis=-1, keepdims=True) + EPS)
        h_ref[...] = ((xf * r) * g_ref[...]).astype(h_ref.dtype)
        r_ref[...] = r

    outs, couts = _call(
        body,
        grid=(s // tr,),
        in_specs=[pl.BlockSpec((tr, d), lambda i: (i, 0)), pl.BlockSpec((1, d), lambda i: (0, 0))],
        out_specs=[pl.BlockSpec((tr, d), lambda i: (i, 0)), pl.BlockSpec((tr, 1), lambda i: (i, 0))],
        out_shape=[jax.ShapeDtypeStruct((s, d), MXU), jax.ShapeDtypeStruct((s, 1), F32)],
        args=(x, g), name=name, comm=comm)
    return tuple(outs) if comm is None else (tuple(outs), couts)


def _loss_head(x, g, target, name):
    s, d = x.shape
    tr = _tile(s, 256)

    def body(x_ref, g_ref, t_ref, loss_ref, dx_ref, dg_ref):
        @pl.when(pl.program_id(0) == 0)
        def _():
            dg_ref[...] = jnp.zeros_like(dg_ref)
            loss_ref[...] = jnp.zeros_like(loss_ref)

        xf = x_ref[...]
        gv = g_ref[...]
        r = lax.rsqrt(jnp.mean(xf * xf, axis=-1, keepdims=True) + EPS)
        xhat = xf * r
        err = xhat * gv - t_ref[...]
        loss_ref[...] += 0.5 * jnp.sum(jnp.mean(err * err, axis=-1, keepdims=True), axis=0, keepdims=True)
        dy = err * (1.0 / d)
        dxhat = dy * gv
        mean = jnp.mean(dxhat * xhat, axis=-1, keepdims=True)
        dx_ref[...] = r * (dxhat - xhat * mean)
        dg_ref[...] += jnp.sum(dy * xhat, axis=0, keepdims=True)

    row = pl.BlockSpec((tr, d), lambda i: (i, 0))
    vec = pl.BlockSpec((1, d), lambda i: (0, 0))
    return pl.pallas_call(
        body,
        grid=(s // tr,),
        in_specs=[row, vec, row],
        out_specs=[pl.BlockSpec((1, LANES), lambda i: (0, 0)), row, vec],
        out_shape=[jax.ShapeDtypeStruct((1, LANES), F32), jax.ShapeDtypeStruct((s, d), F32),
                   jax.ShapeDtypeStruct((1, d), F32)],
        compiler_params=_cparams(),
        name=name,
    )(x, g, target)


def _mix_fwd(oa, ob, wa_t, wb_t, gates, name):
    s, kd = oa.shape
    d = wa_t.shape[0]
    tm, tn = _tile(s, 1024), _tile(d, 1024)
    nj = d // tn

    def body(oa_ref, ob_ref, wa_ref, wb_ref, ga_ref, gb_ref, out_ref):
        ya = _dot_nt(oa_ref[...], wa_ref[...])
        yb = _dot_nt(ob_ref[...], wb_ref[...])
        out_ref[...] = (_sigmoid(ga_ref[...].astype(F32)) * ya
                        + _sigmoid(gb_ref[...].astype(F32)) * yb).astype(out_ref.dtype)

    o_spec = pl.BlockSpec((tm, kd), lambda i, j: (i, 0))
    w_spec = pl.BlockSpec((tn, kd), lambda i, j: (j, 0))
    return pl.pallas_call(
        body,
        grid=(s // tm, nj),
        in_specs=[o_spec, o_spec, w_spec, w_spec,
                  pl.BlockSpec((tm, tn), lambda i, j: (i, j)),
                  pl.BlockSpec((tm, tn), lambda i, j: (i, j + nj))],
        out_specs=pl.BlockSpec((tm, tn), lambda i, j: (i, j)),
        out_shape=jax.ShapeDtypeStruct((s, d), MXU),
        compiler_params=_cparams(),
        name=name,
    )(oa, ob, wa_t, wb_t, gates, gates)


def _mix_bwd(dx, w_o, oa, ob, wa_t, wb_t, gates, name):
    s, kd = oa.shape
    d = wa_t.shape[0]
    tm = _tile(s, 512)

    def body(dx_ref, wo_ref, oa_ref, ob_ref, wa_ref, wb_ref, g_ref, dya_ref, dyb_ref, dg_ref):
        dm = _dot_nt(dx_ref[...], wo_ref[...])
        ya = _dot_nt(oa_ref[...], wa_ref[...])
        yb = _dot_nt(ob_ref[...], wb_ref[...])
        sa = _sigmoid(g_ref[:, :d].astype(F32))
        sb = _sigmoid(g_ref[:, d:].astype(F32))
        dya_ref[...] = (dm * sa).astype(dya_ref.dtype)
        dyb_ref[...] = (dm * sb).astype(dyb_ref.dtype)
        dg_ref[:, :d] = (dm * ya * sa * (1.0 - sa)).astype(dg_ref.dtype)
        dg_ref[:, d:] = (dm * yb * sb * (1.0 - sb)).astype(dg_ref.dtype)

    def rows(width):
        return pl.BlockSpec((tm, width), lambda i: (i, 0))

    def whole(arr):
        return pl.BlockSpec(arr.shape, lambda i: (0, 0))

    return pl.pallas_call(
        body,
        grid=(s // tm,),
        in_specs=[rows(d), whole(w_o), rows(kd), rows(kd), whole(wa_t), whole(wb_t), rows(2 * d)],
        out_specs=[rows(d), rows(d), rows(2 * d)],
        out_shape=[jax.ShapeDtypeStruct((s, d), MXU), jax.ShapeDtypeStruct((s, d), MXU),
                   jax.ShapeDtypeStruct((s, 2 * d), MXU)],
        compiler_params=_cparams(),
        name=name,
    )(dx, w_o, oa, ob, wa_t, wb_t, gates)


def _ple(p, w_pe_t, h, w_pg, other, *, backward, name, next_g=None):
    s, kp = p.shape
    d = w_pe_t.shape[0]
    tm, tn = _tile(s, 1024), _tile(d, 1024)
    with_norm = next_g is not None
    assert not (with_norm and (backward or tn != d))

    def body(p_ref, wpe_ref, h_ref, wpg_ref, other_ref, *rest):
        out_refs = rest[1:] if with_norm else rest
        pe = _dot_nt(p_ref[...].astype(MXU), wpe_ref[...])
        gt = _dot(h_ref[...], wpg_ref[...])
        sg = _sigmoid(gt)
        if backward:
            dout = other_ref[...]
            out_refs[0][...] = (dout * sg).astype(out_refs[0].dtype)
            out_refs[1][...] = (dout * pe * sg * (1.0 - sg)).astype(out_refs[1].dtype)
        elif with_norm:
            x_new, h_new, r_new = _residual_norm(pe * sg, other_ref[...], rest[0][...])
            out_refs[0][...] = x_new
            out_refs[1][...] = h_new.astype(out_refs[1].dtype)
            out_refs[2][...] = r_new
        else:
            out_refs[0][...] = other_ref[...] + pe * sg

    t_spec = pl.BlockSpec((tm, tn), lambda i, j: (i, j))
    if backward:
        out_specs, out_shape = [t_spec, t_spec], [jax.ShapeDtypeStruct((s, d), MXU)] * 2
    else:
        out_specs, out_shape = [t_spec], [jax.ShapeDtypeStruct((s, d), F32)]
    in_specs = [pl.BlockSpec((tm, kp), lambda i, j: (i, 0)),
                pl.BlockSpec((tn, kp), lambda i, j: (j, 0)),
                pl.BlockSpec((tm, d), lambda i, j: (i, 0)),
                pl.BlockSpec((d, tn), lambda i, j: (0, j)),
                t_spec]
    args = [p, w_pe_t, h, w_pg, other]
    if with_norm:
        in_specs.append(pl.BlockSpec((1, tn), lambda i, j: (0, j)))
        args.append(next_g)
        out_specs += [t_spec, pl.BlockSpec((tm, 1), lambda i, j: (i, 0))]
        out_shape += [jax.ShapeDtypeStruct((s, d), MXU), jax.ShapeDtypeStruct((s, 1), F32)]
    outs = pl.pallas_call(
        body,
        grid=(s // tm, d // tn),
        in_specs=in_specs,
        out_specs=out_specs,
        out_shape=out_shape,
        compiler_params=_cparams(),
        name=name,
    )(*args)
    return tuple(outs) if (backward or with_norm) else outs[0]


def _split_dot(x, tri):
    hi = x.astype(jnp.bfloat16)
    lo = (x - hi.astype(F32)).astype(jnp.bfloat16)
    return _dot(hi, tri) + _dot(lo, tri)


def _log_sigmoids(z):
    lb = jnp.minimum(z, 0.0) - jnp.log(1.0 + jnp.exp(-jnp.abs(z)))
    return lb, lb - z


def _head_lanes(hh):
    lane = lax.broadcasted_iota(jnp.int32, (1, LANES), 1)
    return jnp.logical_and(lane >= hh * HEAD_DIM, lane < (hh + 1) * HEAD_DIM)


def _sb_fwd(qkv, name, comm=None):
    s = qkv.shape[0]
    tq = _tile(s, SB_TQ)
    nsub = SB_SUB if (s // tq) % SB_SUB == 0 else 1

    def body(q_ref, k_ref, v_ref, o_ref):
        row = lax.broadcasted_iota(jnp.int32, (tq, tq), 0)
        col = lax.broadcasted_iota(jnp.int32, (tq, tq), 1)
        causal = col < row
        tri = jnp.where(row > col, 1.0, 0.0).astype(jnp.bfloat16)
        started = [_sb_fwd_straight(q_ref, k_ref, v_ref, pl.program_id(1) * nsub + sub, sub, tq, causal, tri)
                   for sub in range(nsub)]
        for sub, (block, i, cs, accs) in enumerate(started):
            def top(cs):
                return jnp.maximum(jnp.max(cs[0]), jnp.max(cs[1]))

            def live(st):
                return jnp.logical_and(st[0] >= 0, st[1] > SB_DEAD)

            def walk(st, block=block):
                cs, accs = block(st[0], st[2], st[3], False)
                return st[0] - 1, top(cs), cs, accs

            accs = lax.while_loop(live, walk, (i - 2, top(cs), cs, accs))[3]
            o_ref[sub * tq:(sub + 1) * tq, :] = jnp.where(_head_lanes(0), accs[0], accs[1]).astype(o_ref.dtype)

    outs, couts = _call(
        body,
        grid=(N_PAIR, s // (nsub * tq)),
        in_specs=[pl.BlockSpec((nsub * tq, LANES), lambda p, i: (i, p)),
                  pl.BlockSpec((s, LANES), lambda p, i: (0, N_PAIR + p)),
                  pl.BlockSpec((s, LANES), lambda p, i: (0, 2 * N_PAIR + p))],
        out_specs=[pl.BlockSpec((nsub * tq, LANES), lambda p, i: (i, p))],
        out_shape=[jax.ShapeDtypeStruct((s, SB_W), MXU)],
        args=(qkv, qkv, qkv), name=name, comm=comm)
    return outs[0] if comm is None else (outs[0], couts)


def _sb_fwd_straight(q_ref, k_ref, v_ref, i, sub, tq, causal, tri):
    qf = q_ref[sub * tq:(sub + 1) * tq, :].astype(F32) * SCALE
    qms = [jnp.where(_head_lanes(hh), qf, 0.0).astype(MXU) for hh in range(2)]

    def block(kb, cs, accs, masked, present=None):
        rows = pl.ds(pl.multiple_of(kb * tq, tq), tq)
        ks, vs = k_ref[rows, :], v_ref[rows, :]
        off = 0.0 if present is None else (1.0 - present) * NEG
        new_c, new_acc = [], []
        for hh in range(2):
            lb, lm = _log_sigmoids(_dot_nt(qms[hh], ks))
            if masked:
                lm = jnp.where(causal, lm, 0.0)
            a = jnp.exp(lb + _split_dot(lm, tri) + (cs[hh] + off))
            if masked:
                a = jnp.where(causal, a, 0.0)
            new_acc.append(accs[hh] + _dot(a.astype(MXU), vs))
            row_sum = jnp.sum(lm, axis=1, keepdims=True)
            new_c.append(cs[hh] + (row_sum if present is None else row_sum * present))
        return tuple(new_c), tuple(new_acc)

    zc, za = jnp.zeros((tq, 1), F32), jnp.zeros((tq, LANES), F32)
    cs, accs = block(i, (zc, zc), (za, za), True)
    cs, accs = block(jnp.maximum(i - 1, 0), cs, accs, False, jnp.where(i > 0, 1.0, 0.0))
    return block, i, cs, accs


def _sb_bwd(qkv, do, name, comm=None):
    s = qkv.shape[0]
    tq = _tile(s, SB_TQ)
    nq = s // tq
    nsub = SB_SUB if nq % SB_SUB == 0 else 1
    nsteps = nq // nsub

    def body(q_ref, k_ref, v_ref, do_ref, dq_ref, dk_ref, dv_ref, dk_acc, dv_acc, carries):
        step = pl.program_id(1)

        @pl.when(step == 0)
        def _():
            dk_acc[...] = jnp.zeros_like(dk_acc)
            dv_acc[...] = jnp.zeros_like(dv_acc)

        row = lax.broadcasted_iota(jnp.int32, (tq, tq), 0)
        col = lax.broadcasted_iota(jnp.int32, (tq, tq), 1)
        causal = col < row
        tri_rev = jnp.where(row > col, 1.0, 0.0).astype(jnp.bfloat16)
        tri_excl = jnp.where(row < col, 1.0, 0.0).astype(jnp.bfloat16)
        zc, za = jnp.zeros((tq, 1), F32), jnp.zeros((tq, LANES), F32)

        def top(cs):
            return jnp.maximum(jnp.max(cs[0]), jnp.max(cs[1]))

        def live(st):
            return jnp.logical_and(st[0] >= 0, st[1] > SB_DEAD)

        def row_sums(pre):
            return [jnp.sum(lm, axis=1, keepdims=True) for _, lm in pre]

        def query_block(sub):
            i = step * nsub + sub
            q_rows = slice(sub * tq, (sub + 1) * tq)
            qf = q_ref[q_rows, :].astype(F32) * SCALE
            dof = do_ref[q_rows, :]
            qms = [jnp.where(_head_lanes(hh), qf, 0.0).astype(MXU) for hh in range(2)]
            doms = [jnp.where(_head_lanes(hh), dof, jnp.zeros_like(dof)) for hh in range(2)]

            def terms(kb, masked):
                rows = pl.ds(pl.multiple_of(kb * tq, tq), tq)
                ks = k_ref[rows, :]
                out = []
                for hh in range(2):
                    lb, lm = _log_sigmoids(_dot_nt(qms[hh], ks))
                    if masked:
                        lm = jnp.where(causal, lm, 0.0)
                    out.append((lb, lm))
                return out

            def block(kb, cs, gpres, dqs, masked, gate=None, pre=None):
                rows = pl.ds(pl.multiple_of(kb * tq, tq), tq)
                ks, vs = k_ref[rows, :], v_ref[rows, :]
                pre = terms(kb, masked) if pre is None else pre
                new_g, new_dq = [], []
                dk_add, dv_add = None, None
                for hh in range(2):
                    lb, lm = pre[hh]
                    off = 0.0 if gate is None else (1.0 - gate) * NEG
                    a = jnp.exp(lb + _split_dot(lm, tri_rev) + (cs[hh] + off))
                    if masked:
                        a = jnp.where(causal, a, 0.0)
                    g = a * _dot_nt(doms[hh], vs)
                    gsum = gpres[hh] + _split_dot(g, tri_excl)
                    dz = g - (g + gsum) * jnp.exp(lb)
                    if masked:
                        dz = jnp.where(causal, dz, 0.0)
                    dzb = dz.astype(MXU)
                    new_dq.append(dqs[hh] + _dot(dzb, ks))
                    dk_h = _dot_tn(dzb, qms[hh])
                    dv_h = _dot_tn(a.astype(MXU), doms[hh])
                    dk_add = dk_h if dk_add is None else dk_add + dk_h
                    dv_add = dv_h if dv_add is None else dv_add + dv_h
                    new_g.append(gpres[hh] + jnp.sum(g, axis=1, keepdims=True))
                dk_acc[rows, :] += dk_add
                dv_acc[rows, :] += dv_add
                return tuple(new_g), tuple(new_dq)

            prev = jnp.maximum(i - 1, 0)
            gate = jnp.where(i > 0, 1.0, 0.0)
            t_diag, t_prev = terms(i, True), terms(prev, False)
            c_diag = row_sums(t_diag)
            sums = row_sums(t_prev)
            c_prev = tuple(c_diag[hh] + sums[hh] * gate for hh in range(2))
            return dict(i=i, prev=prev, gate=gate, q_rows=q_rows, terms=terms, block=block,
                        t_diag=t_diag, t_prev=t_prev, c_diag=c_diag, c_prev=c_prev)

        blocks = [query_block(sub) for sub in range(nsub)]
        for qb in blocks:
            def record(st, qb=qb):
                kb, cs = st[0], st[2]
                sums = row_sums(qb["terms"](kb, False))
                for hh in range(2):
                    carries[hh, kb] = cs[hh]
                cs = tuple(cs[hh] + sums[hh] for hh in range(2))
                return kb - 1, top(cs), cs

            first = lax.while_loop(live, record, (qb["i"] - 2, top(qb["c_prev"]), qb["c_prev"]))[0] + 1
            qb["mid"] = lax.fori_loop(
                first, qb["i"] - 1,
                lambda kb, cr, qb=qb: qb["block"](kb, (carries[0, kb], carries[1, kb]), cr[0], cr[1], False),
                ((zc, zc), (za, za)))
        for qb in blocks:
            gpres, dqs = qb["block"](qb["prev"], qb["c_diag"], *qb["mid"], False, qb["gate"], qb["t_prev"])
            dqs = qb["block"](qb["i"], (zc, zc), gpres, dqs, True, None, qb["t_diag"])[1]
            dq_ref[qb["q_rows"], :] = (jnp.where(_head_lanes(0), dqs[0], dqs[1]) * SCALE).astype(dq_ref.dtype)

        @pl.when(step == nsteps - 1)
        def _():
            dk_ref[...] = dk_acc[...].astype(dk_ref.dtype)
            dv_ref[...] = dv_acc[...].astype(dv_ref.dtype)

    blk = pl.BlockSpec((nsub * tq, LANES), lambda p, i: (i, p))
    full = pl.BlockSpec((s, LANES), lambda p, i: (0, p))
    outs, couts = _call(
        body,
        grid=(N_PAIR, nsteps),
        in_specs=[blk,
                  pl.BlockSpec((s, LANES), lambda p, i: (0, N_PAIR + p)),
                  pl.BlockSpec((s, LANES), lambda p, i: (0, 2 * N_PAIR + p)),
                  blk],
        out_specs=[blk, full, full],
        out_shape=[jax.ShapeDtypeStruct((s, SB_W), MXU)] * 3,
        scratch_shapes=[pltpu.VMEM((s, LANES), F32), pltpu.VMEM((s, LANES), F32),
                        pltpu.VMEM((2, nq, tq, 1), F32)],
        args=(qkv, qkv, qkv, do), name=name, comm=comm)
    return tuple(outs) if comm is None else (tuple(outs), couts)


def _bucket_table():
    i = np.arange(BLOCK)[:, None]
    j = np.arange(2 * BLOCK)[None, :]
    d = np.maximum(BLOCK + i - j, 0)
    max_exact = N_BUCKETS // 2
    df = np.maximum(d, 1).astype(np.float32)
    large = max_exact + (np.log(df / max_exact) / math.log(MAX_DISTANCE / max_exact)
                         * (N_BUCKETS - max_exact)).astype(np.int32)
    large = np.minimum(large, N_BUCKETS - 1)
    return np.where(d < max_exact, d, large).astype(np.int32)


def _build_bias(rel_bias, buckets, name):
    def body(rb_ref, bk_ref, out_ref):
        h = pl.program_id(0)
        bk = bk_ref[...]
        acc = jnp.zeros(bk.shape, F32)
        for b in range(N_BUCKETS):
            acc = jnp.where(bk == b, rb_ref[b, h], acc)
        out_ref[...] = acc

    return pl.pallas_call(
        body,
        grid=(SW_HEADS,),
        in_specs=[pl.BlockSpec(memory_space=pltpu.SMEM),
                  pl.BlockSpec((BLOCK, 2 * BLOCK), lambda h: (0, 0))],
        out_specs=pl.BlockSpec((None, BLOCK, 2 * BLOCK), lambda h: (h, 0, 0)),
        out_shape=jax.ShapeDtypeStruct((SW_HEADS, BLOCK, 2 * BLOCK), F32),
        name=name,
    )(rel_bias, buckets)


def _bias_grad(dbias_layers, buckets, name):
    n_l = len(dbias_layers)

    def body(*refs):
        bk = refs[n_l][...]
        out_ref = refs[n_l + 1]
        db = refs[0][...]
        for r in refs[1:n_l]:
            db = db + r[...]
        lane = lax.broadcasted_iota(jnp.int32, (1, LANES), 1)
        acc = jnp.zeros((1, LANES), F32)
        for b in range(N_BUCKETS):
            part = jnp.sum(jnp.where(bk == b, db, 0.0), axis=1, keepdims=True)
            tot = jnp.sum(part, axis=0, keepdims=True)
            acc = jnp.where(lane == b, tot, acc)
        out_ref[...] = acc

    hspec = pl.BlockSpec((None, BLOCK, 2 * BLOCK), lambda h: (h, 0, 0))
    return pl.pallas_call(
        body,
        grid=(SW_HEADS,),
        in_specs=[hspec] * n_l + [pl.BlockSpec((BLOCK, 2 * BLOCK), lambda h: (0, 0))],
        out_specs=pl.BlockSpec((None, 1, LANES), lambda h: (h, 0, 0)),
        out_shape=jax.ShapeDtypeStruct((SW_HEADS, 1, LANES), F32),
        name=name,
    )(*dbias_layers, buckets)


GROUP_ROWS = SW_GROUP * BLOCK


def _group_lanes(g):
    lane = lax.broadcasted_iota(jnp.int32, (1, LANES), 1)
    gvec = jnp.zeros((1, LANES), jnp.int32) + g
    return jnp.where(lane >= HEAD_DIM, 1, 0) == gvec, gvec


def _stack_heads(x, g):
    kv_lanes, gvec = _group_lanes(g)
    parts = []
    for j in range(SW_GROUP):
        half = x[:, (j // 2) * LANES:(j // 2 + 1) * LANES]
        moved = jnp.where(gvec == j % 2, half, pltpu.roll(half, HEAD_DIM, 1))
        parts.append(jnp.where(kv_lanes, moved, 0.0))
    return jnp.concatenate(parts, axis=0)


def _unstack_heads(y, g):
    _, gvec = _group_lanes(g)
    heads = []
    for j in range(SW_GROUP):
        yj = y[j * BLOCK:(j + 1) * BLOCK]
        heads.append(jnp.where(gvec == j % 2, yj, pltpu.roll(yj, HEAD_DIM, 1)))
    pairs = [jnp.where(_head_lanes(0), heads[2 * p], heads[2 * p + 1]) for p in range(SW_GROUP // 2)]
    return jnp.concatenate(pairs, axis=1)


def _per_head_col(values):
    return jnp.concatenate([jnp.zeros((BLOCK, 1), F32) + v for v in values], axis=0)


def _swa_scores(qs, kp, kc, bias_ref, n):
    row = jnp.bitwise_and(lax.broadcasted_iota(jnp.int32, (GROUP_ROWS, BLOCK), 0), BLOCK - 1)
    col = lax.broadcasted_iota(jnp.int32, (GROUP_ROWS, BLOCK), 1)
    bias = bias_ref[...].reshape(GROUP_ROWS, 2 * BLOCK)
    s1 = _dot_nt(qs, kp) + bias[:, :BLOCK]
    s2 = _dot_nt(qs, kc) + bias[:, BLOCK:]
    no_prev = jnp.where(n > 0, 0, BLOCK)
    s1 = jnp.where(col > row + no_prev, s1, NEG)
    s2 = jnp.where(col <= row, s2, NEG)
    return s1, s2


def _swa_specs(s):
    q_blk = 3 * SB_W // (2 * LANES)
    k_blk = (3 * SB_W + SW_W) // LANES
    return (pl.BlockSpec((s, 2 * LANES), lambda g: (0, q_blk + g)),
            pl.BlockSpec((s, LANES), lambda g: (0, k_blk)),
            pl.BlockSpec((s, LANES), lambda g: (0, k_blk + 1)))


def _swa_fwd(qkv, bias, sinks, name, comm=None):
    s = qkv.shape[0]
    nb = s // BLOCK

    def body(sink_ref, q_ref, k_ref, v_ref, bias_ref, o_ref, lse_ref):
        g = pl.program_id(0)
        sink = _per_head_col([sink_ref[SW_GROUP * g + j] for j in range(SW_GROUP)])
        lane = lax.broadcasted_iota(jnp.int32, (1, LANES), 1)

        def step(n, carry):
            r0 = pl.multiple_of(n * BLOCK, BLOCK)
            p0 = pl.multiple_of(jnp.maximum(n - 1, 0) * BLOCK, BLOCK)
            cur, prev = pl.ds(r0, BLOCK), pl.ds(p0, BLOCK)
            qs = _stack_heads(q_ref[cur, :].astype(F32) * SCALE, g).astype(MXU)
            s1, s2 = _swa_scores(qs, k_ref[prev, :], k_ref[cur, :], bias_ref, n)
            m = jnp.maximum(jnp.max(jnp.maximum(s1, s2), axis=1, keepdims=True), sink)
            e1 = jnp.exp(s1 - m)
            e2 = jnp.exp(s2 - m)
            den = jnp.sum(e1 + e2, axis=1, keepdims=True) + jnp.exp(sink - m)
            o = _dot((e1 / den).astype(MXU), v_ref[prev, :]) + _dot((e2 / den).astype(MXU), v_ref[cur, :])
            o_ref[cur, :] = _unstack_heads(o, g).astype(o_ref.dtype)
            lse = m + jnp.log(den)
            lse_row = jnp.zeros((BLOCK, LANES), F32)
            for j in range(SW_GROUP):
                lse_row = jnp.where(lane == j, lse[j * BLOCK:(j + 1) * BLOCK], lse_row)
            lse_ref[cur, :] = lse_row
            return carry

        lax.fori_loop(0, nb, step, 0, unroll=2)

    outs, couts = _call(
        body,
        grid=(SW_KV,),
        in_specs=[pl.BlockSpec(memory_space=pltpu.SMEM), *_swa_specs(s),
                  pl.BlockSpec((SW_GROUP, BLOCK, 2 * BLOCK), lambda g: (g, 0, 0))],
        out_specs=[pl.BlockSpec((s, 2 * LANES), lambda g: (0, g)),
                   pl.BlockSpec((None, s, LANES), lambda g: (g, 0, 0))],
        out_shape=[jax.ShapeDtypeStruct((s, SW_W), MXU), jax.ShapeDtypeStruct((SW_KV, s, LANES), F32)],
        args=(sinks, qkv, qkv, qkv, bias), name=name, comm=comm)
    return tuple(outs) if comm is None else (tuple(outs), couts)


def _swa_bwd(qkv, bias, sinks, do, lse, name, comm=None):
    s = qkv.shape[0]
    nb = s // BLOCK

    def body(sink_ref, q_ref, k_ref, v_ref, bias_ref, do_ref, lse_ref,
             dq_ref, dk_ref, dv_ref, dbias_ref, dsink_ref, dk_acc, dv_acc):
        g = pl.program_id(0)
        sink = _per_head_col([sink_ref[SW_GROUP * g + j] for j in range(SW_GROUP)])
        lane = lax.broadcasted_iota(jnp.int32, (1, LANES), 1)

        @pl.when(g == 0)
        def _():
            dk_acc[...] = jnp.zeros_like(dk_acc)
            dv_acc[...] = jnp.zeros_like(dv_acc)

        dbias_ref[...] = jnp.zeros_like(dbias_ref)

        def step(n, dsink_rows):
            r0 = pl.multiple_of(n * BLOCK, BLOCK)
            p0 = pl.multiple_of(jnp.maximum(n - 1, 0) * BLOCK, BLOCK)
            cur, prev = pl.ds(r0, BLOCK), pl.ds(p0, BLOCK)
            qs = _stack_heads(q_ref[cur, :].astype(F32) * SCALE, g).astype(MXU)
            dos = _stack_heads(do_ref[cur, :].astype(F32), g).astype(MXU)
            kp, kc, vp, vc = k_ref[prev, :], k_ref[cur, :], v_ref[prev, :], v_ref[cur, :]
            lse_row = lse_ref[cur, :]
            lse = jnp.concatenate([jnp.sum(jnp.where(lane == j, lse_row, 0.0), axis=1, keepdims=True)
                                   for j in range(SW_GROUP)], axis=0)
            s1, s2 = _swa_scores(qs, kp, kc, bias_ref, n)
            pr1 = jnp.exp(s1 - lse)
            pr2 = jnp.exp(s2 - lse)
            dpr1 = _dot_nt(dos, vp)
            dpr2 = _dot_nt(dos, vc)
            delta = jnp.sum(pr1 * dpr1 + pr2 * dpr2, axis=1, keepdims=True)
            ds1 = pr1 * (dpr1 - delta)
            ds2 = pr2 * (dpr2 - delta)
            dbias_ref[:, :, :BLOCK] += ds1.reshape(SW_GROUP, BLOCK, BLOCK)
            dbias_ref[:, :, BLOCK:] += ds2.reshape(SW_GROUP, BLOCK, BLOCK)
            ds1b, ds2b = ds1.astype(MXU), ds2.astype(MXU)
            dq = _dot(ds1b, kp) + _dot(ds2b, kc)
            dq_ref[cur, :] = (_unstack_heads(dq, g) * SCALE).astype(dq_ref.dtype)
            dk_acc[prev, :] += _dot_tn(ds1b, qs)
            dk_acc[cur, :] += _dot_tn(ds2b, qs)
            dv_acc[prev, :] += _dot_tn(pr1.astype(MXU), dos)
            dv_acc[cur, :] += _dot_tn(pr2.astype(MXU), dos)
            return dsink_rows - jnp.exp(sink - lse) * delta

        rows = lax.fori_loop(0, nb, step, jnp.zeros((GROUP_ROWS, 1), F32), unroll=2)
        for j in range(SW_GROUP):
            dsink_ref[j] = jnp.broadcast_to(jnp.sum(rows[j * BLOCK:(j + 1) * BLOCK], axis=0, keepdims=True),
                                            (1, LANES))

        @pl.when(g == SW_KV - 1)
        def _():
            dk_ref[...] = dk_acc[...].astype(dk_ref.dtype)
            dv_ref[...] = dv_acc[...].astype(dv_ref.dtype)

    grp = pl.BlockSpec((s, 2 * LANES), lambda g: (0, g))
    kv_out = pl.BlockSpec((s, LANES), lambda g: (0, 0))
    bspec = pl.BlockSpec((SW_GROUP, BLOCK, 2 * BLOCK), lambda g: (g, 0, 0))
    outs, couts = _call(
        body,
        grid=(SW_KV,),
        in_specs=[pl.BlockSpec(memory_space=pltpu.SMEM), *_swa_specs(s), bspec, grp,
                  pl.BlockSpec((None, s, LANES), lambda g: (g, 0, 0))],
        out_specs=[grp, kv_out, kv_out, bspec, pl.BlockSpec((SW_GROUP, 1, LANES), lambda g: (g, 0, 0))],
        out_shape=[jax.ShapeDtypeStruct((s, SW_W), MXU),
                   jax.ShapeDtypeStruct((s, LANES), MXU),
                   jax.ShapeDtypeStruct((s, LANES), MXU),
                   jax.ShapeDtypeStruct((SW_HEADS, BLOCK, 2 * BLOCK), F32),
                   jax.ShapeDtypeStruct((SW_HEADS, 1, LANES), F32)],
        scratch_shapes=[pltpu.VMEM((s, LANES), F32), pltpu.VMEM((s, LANES), F32)],
        args=(sinks, qkv, qkv, qkv, bias, do, lse), name=name, comm=comm)
    return tuple(outs) if comm is None else (tuple(outs), couts)


class _NoPlan:
    def comm(self, name):
        return None

    def done(self, name, outs):
        pass

    def grad(self, layer, name, value):
        pass


def _run(plan, fn, *args, name, **kw):
    comm = plan.comm(name)
    if comm is None:
        return fn(*args, name=name, **kw)
    res, outs = fn(*args, name=name, comm=comm, **kw)
    plan.done(name, outs)
    return res


def _norm_bwd(dh, x, dres, r, g):
    xhat = x * r
    dxhat = dh * g
    dx = dres + r * (dxhat - xhat * jnp.mean(dxhat * xhat, axis=-1, keepdims=True))
    return dx, dx, jnp.sum(dh * xhat, axis=0, keepdims=True)


def _residual_norm(acc, res, g):
    x = res + acc
    r = lax.rsqrt(jnp.mean(x * x, axis=-1, keepdims=True) + EPS)
    return x, (x * r) * g, r


def _layer_fwd(x, p, w, g_mix, g_mlp, g_pe, sinks, bias, tag, plan, normed_x=None, next_g=None):
    h1, r1 = normed_x if normed_x is not None else _run(plan, _rms_fwd, x, g_mix, name=f"rms_mix_{tag}")
    qkv = _run(plan, _mm, h1, w["w_qkv_t"], tb=True, out_dtypes=(MXU,), tn=QKV_COLS, name=f"proj_qkv_{tag}")
    gates = _run(plan, _mm, h1, w["w_gate_t"], tb=True, out_dtypes=(MXU,), tn=2048, name=f"proj_gate_{tag}")
    oa = _run(plan, _sb_fwd, qkv, name=f"sb_fwd_{tag}")
    ob, lse = _run(plan, _swa_fwd, qkv, bias, sinks, name=f"swa_fwd_{tag}")
    merged = _mix_fwd(oa, ob, w["w_up_a_t"], w["w_up_b_t"], gates, f"mix_fwd_{tag}")
    d = x.shape[1]
    normed = (F32, MXU, (F32, "col"))
    x1, h2, r2 = _run(plan, _mm, merged, w["w_o"], extras=(x, g_mlp), epi=_residual_norm, out_dtypes=normed,
                      tn=d, name=f"out_proj_{tag}")
    u, act = _run(plan, _mm, h2, w["w_ff1_t"], tb=True,
                  epi=lambda acc: (acc, jnp.square(jnp.maximum(acc, 0.0))),
                  out_dtypes=(MXU, MXU), tn=2048, name=f"ff1_{tag}")
    x2, h3, r3 = _run(plan, _mm, act, w["w_ff2"], extras=(x1, g_pe), epi=_residual_norm, out_dtypes=normed,
                      tn=d, name=f"ff2_{tag}")
    x3 = _ple(p, w["w_pe_t"], h3, w["w_pg"], x2, backward=False, name=f"ple_fwd_{tag}", next_g=next_g)
    next_normed = None
    if next_g is not None:
        x3, next_normed = x3[0], (x3[1], x3[2])
    saved = dict(x=x, h1=h1, r1=r1, gates=gates, qkv=qkv, lse=lse, oa=oa, ob=ob, merged=merged,
                 x1=x1, h2=h2, r2=r2, u=u, act=act, x2=x2, h3=h3, r3=r3)
    return x3, saved, next_normed


def _layer_bwd(dx3, sv, p, w, g_mix, g_mlp, g_pe, sinks, bias, layer, plan):
    tag = f"l{layer}"
    gw = {}
    wire = (WIRE,)

    def dw(name, a, b):
        gw[name] = _run(plan, _mm, a, b, ta=True, out_dtypes=wire, tk=2048, name=f"d{name}_{tag}")
        plan.grad(layer, name, gw[name])

    dpe, dgt = _ple(p, w["w_pe_t"], sv["h3"], w["w_pg"], dx3, backward=True, name=f"ple_bwd_{tag}")
    dw("w_pe", dpe, p)
    dw("w_pg", sv["h3"], dgt)
    d = dx3.shape[1]
    grads = (F32, MXU, (F32, "rowsum"))
    dx2, dx2b, dg_pe = _run(plan, _mm, dgt, w["w_pg"], tb=True, extras=(sv["x2"], dx3, sv["r3"], g_pe),
                            epi=_norm_bwd, out_dtypes=grads, tm=512, tn=d, name=f"dh_pe_{tag}")
    dw("w_ff2", sv["act"], dx2b)
    du = _run(plan, _mm, dx2b, w["w_ff2"], tb=True, extras=(sv["u"],),
              epi=lambda acc, u: acc * (2.0 * jnp.maximum(u.astype(F32), 0.0)), out_dtypes=(MXU,),
              tn=2048, name=f"dact_{tag}")
    dw("w_ff1", du, sv["h2"])
    dx1, dx1b, dg_mlp = _run(plan, _mm, du, w["w_ff1_t"], extras=(sv["x1"], dx2, sv["r2"], g_mlp),
                             epi=_norm_bwd, out_dtypes=grads, tm=1024, tn=d, name=f"dh_mlp_{tag}")
    dw("w_o", sv["merged"], dx1b)
    dya, dyb, dgates = _mix_bwd(dx1b, w["w_o"], sv["oa"], sv["ob"], w["w_up_a_t"], w["w_up_b_t"],
                                sv["gates"], f"mix_bwd_{tag}")
    dw("w_up_a", dya, sv["oa"])
    dw("w_up_b", dyb, sv["ob"])
    doa = _run(plan, _mm, dya, w["w_up_a_t"], out_dtypes=(MXU,), name=f"do_a_{tag}")
    dob = _run(plan, _mm, dyb, w["w_up_b_t"], out_dtypes=(MXU,), name=f"do_b_{tag}")
    dqb, dkb, dvb, dbias, dsink = _run(plan, _swa_bwd, sv["qkv"], bias, sinks, dob, sv["lse"],
                                       name=f"swa_bwd_{tag}")
    dqa, dka, dva = _run(plan, _sb_bwd, sv["qkv"], doa, name=f"sb_bwd_{tag}")
    dqkv = jnp.concatenate([dqa, dka, dva, dqb, dkb, dvb], axis=1)
    gw_qkv = _mm(dqkv, sv["h1"], ta=True, out_dtypes=wire, tk=2048, name=f"dw_qkv_{tag}")
    gw_gate = _mm(dgates, sv["h1"], ta=True, out_dtypes=wire, tk=2048, name=f"dw_gate_{tag}")
    gw["w_in"] = jnp.concatenate([gw_qkv, gw_gate], axis=0)
    plan.grad(layer, "w_in", gw["w_in"])
    dh1 = _run(plan, _mm, dgates, w["w_gate_t"], name=f"dh_gate_{tag}")
    dx, _, dg_mix = _run(plan, _mm, dqkv, w["w_qkv_t"], tk=768, extras=(dh1, sv["x"], dx1, sv["r1"], g_mix),
                         epi=lambda acc, prev, *rest: _norm_bwd(acc + prev, *rest), out_dtypes=grads,
                         tm=512, tn=d, name=f"dh_qkv_{tag}")
    small = dict(g_mix=dg_mix, g_mlp=dg_mlp, g_pe=dg_pe, sinks=dsink[:, 0, 0], dbias=dbias)
    return dx, gw, small


def _local_step(x, p, target, weights, g_mix, g_mlp, g_pe, g_final, sinks, rel_bias, plan=None):
    plan = _NoPlan() if plan is None else plan
    depth = g_mix.shape[0]
    buckets = jnp.asarray(_bucket_table())
    bias = _build_bias(rel_bias, buckets, "build_bias")
    saved, wfull = [], []
    h, normed = x, None
    for l in range(depth):
        wfull.append(weights(l))
        next_g = g_mix[l + 1:l + 2] if l + 1 < depth else None
        h, sv, normed = _layer_fwd(h, p[l], wfull[l], g_mix[l:l + 1], g_mlp[l:l + 1], g_pe[l:l + 1],
                                   sinks[l], bias, f"l{l}", plan, normed, next_g)
        saved.append(sv)
    loss_row, dx, dg_final = _loss_head(h, g_final[None, :], target, "loss_head")
    gws = [None] * depth
    smalls = [None] * depth
    for l in reversed(range(depth)):
        dx, gws[l], smalls[l] = _layer_bwd(dx, saved[l], p[l], wfull[l], g_mix[l:l + 1], g_mlp[l:l + 1],
                                           g_pe[l:l + 1], sinks[l], bias, l, plan)
    drel = _bias_grad([sm["dbias"] for sm in smalls], buckets, "bias_grad")[:, 0, :N_BUCKETS].T
    small = dict(
        g_mix=jnp.concatenate([sm["g_mix"] for sm in smalls], axis=0),
        g_mlp=jnp.concatenate([sm["g_mlp"] for sm in smalls], axis=0),
        g_pe=jnp.concatenate([sm["g_pe"] for sm in smalls], axis=0),
        g_final=dg_final[0],
        sinks=jnp.stack([sm["sinks"] for sm in smalls], axis=0),
        rel_bias=drel,
    )
    return loss_row, dx, gws, small


MESH_ID = pl.DeviceIdType.MESH


def _position():
    return lax.axis_index("x"), lax.axis_index("y"), lax.axis_index("c")


def _gather_comm(shards):
    n = len(shards)

    def copies(pos, x_refs, out_refs, sems):
        send_sems, recv_sems, local_sems = sems
        x, y, c = pos
        me, sibling = (x, y, c), (x, y, 1 - c)
        chips = [(1 - x, y), (x, 1 - y), (1 - x, 1 - y)]

        def slot(a, px, py, pc):
            return out_refs[a].at[4 * px + 2 * py + pc]

        def copy(a, k, block, to, src=None):
            return pltpu.make_async_remote_copy(
                src_ref=slot(a, *block) if src is None else src, dst_ref=slot(a, *block),
                send_sem=send_sems.at[a, k], recv_sem=recv_sems.at[a, k],
                device_id=to, device_id_type=MESH_ID)

        mine = [pltpu.make_async_copy(x_refs[a], slot(a, *me), local_sems.at[a]) for a in range(n)]
        first = []
        for a in range(n):
            first.append(copy(a, 0, me, sibling, src=x_refs[a]))
            first += [copy(a, 1 + j, me, (*chip, c), src=x_refs[a]) for j, chip in enumerate(chips)]
        return me, sibling, chips, copy, mine, first

    def start(pos, x_refs, out_refs, sems):
        _, _, _, _, mine, first = copies(pos, x_refs, out_refs, sems)
        for cp in mine + first:
            cp.start()

    def finish(pos, x_refs, out_refs, sems):
        me, sibling, chips, copy, mine, first = copies(pos, x_refs, out_refs, sems)
        c = pos[2]
        passed = []
        for j, chip in enumerate(chips):
            for a in range(n):
                copy(a, 1 + j, (*chip, c), me).wait_recv()
                fwd = copy(a, 4 + j, (*chip, c), sibling)
                fwd.start()
                passed.append(fwd)
        for a in range(n):
            copy(a, 0, sibling, me).wait_recv()
            for j, chip in enumerate(chips):
                copy(a, 4 + j, (*chip, 1 - c), me).wait_recv()
        for cp in first + passed:
            cp.wait_send()
        for cp in mine:
            cp.wait()

    return _Comm(shards, [jax.ShapeDtypeStruct((N_DEV,) + s.shape, s.dtype) for s in shards],
                 [pltpu.SemaphoreType.DMA((n, 7)), pltpu.SemaphoreType.DMA((n, 7)),
                  pltpu.SemaphoreType.DMA((n,))], start, finish)


def _exchange_comm(arrays, n_slots, route):
    n = len(arrays)

    def copies(pos, in_refs, out_refs, sems):
        send_sems, recv_sems = sems
        out = []
        for a in range(n):
            for j in range(n_slots):
                src_slot, peer = route(pos, j)
                out.append(pltpu.make_async_remote_copy(
                    src_ref=in_refs[a].at[src_slot], dst_ref=out_refs[a].at[j],
                    send_sem=send_sems.at[a, j], recv_sem=recv_sems.at[a, j],
                    device_id=peer, device_id_type=MESH_ID))
        return out

    def start(pos, in_refs, out_refs, sems):
        for cp in copies(pos, in_refs, out_refs, sems):
            cp.start()

    def finish(pos, in_refs, out_refs, sems):
        for cp in copies(pos, in_refs, out_refs, sems):
            cp.wait()

    return _Comm(arrays, [jax.ShapeDtypeStruct((n_slots,) + g.shape[1:], g.dtype) for g in arrays],
                 [pltpu.SemaphoreType.DMA((n, n_slots)), pltpu.SemaphoreType.DMA((n, n_slots))], start, finish)


def _rs_sibling_comm(gs):
    return _exchange_comm(gs, 4, lambda pos, j: (2 * j + (1 - pos[2]), (pos[0], pos[1], 1 - pos[2])))


def _chip_of(k, x, y):
    return x ^ ((k + 1) & 1), y ^ (((k + 1) >> 1) & 1)


def _chip_partials(pos, gs, recvs, name):
    n = len(gs)

    def body(pos_ref, *refs):
        for a in range(n):
            refs[2 * n + a][...] = (refs[a][...].astype(F32) + refs[n + a][...].astype(F32)
                                    ).astype(refs[2 * n + a].dtype)

    def g_map(k, pos_ref):
        cx, cy = _chip_of(k, pos_ref[0], pos_ref[1])
        return (4 * cx + 2 * cy + pos_ref[2], 0, 0)

    def r_map(k, pos_ref):
        cx, cy = _chip_of(k, pos_ref[0], pos_ref[1])
        return (2 * cx + cy, 0, 0)

    slab = [(None,) + g.shape[1:] for g in gs]
    return pl.pallas_call(
        body,
        grid_spec=pltpu.PrefetchScalarGridSpec(
            num_scalar_prefetch=1,
            grid=(4,),
            in_specs=[pl.BlockSpec(sh, g_map) for sh in slab] + [pl.BlockSpec(sh, r_map) for sh in slab],
            out_specs=[pl.BlockSpec(sh, lambda k, pos_ref: (k, 0, 0)) for sh in slab],
        ),
        out_shape=[jax.ShapeDtypeStruct((4,) + g.shape[1:], g.dtype) for g in gs],
        compiler_params=_cparams(),
        name=name,
    )(pos, *gs, *recvs)


def _rs_chips_comm(parts):
    return _exchange_comm(parts, 3, lambda pos, k: (k, (*_chip_of(k, pos[0], pos[1]), pos[2])))


def _adamw_math(w, g, m, v):
    m = ADAM_B1 * m + (1.0 - ADAM_B1) * g
    v = ADAM_B2 * v + (1.0 - ADAM_B2) * (g * g)
    m_hat = m / (1.0 - ADAM_B1 ** ADAM_STEP)
    v_hat = v / (1.0 - ADAM_B2 ** ADAM_STEP)
    delta = -ADAM_LR * (m_hat / (jnp.sqrt(v_hat) + ADAM_EPS) + ADAM_WD * w)
    return delta, m, v


def _adamw_weight(parts, recvs, w, m, v, name, grad_t=False, comm=None):
    depth, a, b = w.shape
    ta = _tile(a, 288, unit=LANES if grad_t else 16)
    ni = a // ta
    g_block = (b, ta) if grad_t else (ta, b)

    def body(*refs):
        p_refs, r_refs = refs[:depth], refs[depth:2 * depth]
        w_ref, m_ref, v_ref = refs[2 * depth:2 * depth + 3]
        g_out, d_out, m_out, v_out = refs[2 * depth + 3:]
        layer = pl.program_id(0)
        g = jnp.zeros(g_block, F32)
        for l in range(depth):
            gl = p_refs[l][...].astype(F32)
            for k in range(3):
                gl = gl + r_refs[l][k].astype(F32)
            g = jnp.where(layer == l, gl, g)
        if grad_t:
            g = g.T
        delta, m_new, v_new = _adamw_math(w_ref[...], g, m_ref[...], v_ref[...])
        g_out[...] = g
        d_out[...] = delta
        m_out[...] = m_new
        v_out[...] = v_new

    def hold(l):
        return lambda layer, i: jnp.where(layer == l, i, jnp.where(layer < l, 0, ni - 1))

    def g_index(slot, f):
        if grad_t:
            return lambda layer, i: (slot, 0, f(layer, i))
        return lambda layer, i: (slot, f(layer, i), 0)

    p_specs = [pl.BlockSpec((None,) + g_block, g_index(3, hold(l))) for l in range(depth)]
    r_specs = [pl.BlockSpec((3,) + g_block, g_index(0, hold(l))) for l in range(depth)]
    row = pl.BlockSpec((None, ta, b), lambda layer, i: (layer, i, 0))
    outs, couts = _call(
        body,
        grid=(depth, ni),
        in_specs=p_specs + r_specs + [row, row, row],
        out_specs=[row] * 4,
        out_shape=[jax.ShapeDtypeStruct(w.shape, F32)] * 4,
        args=(*parts, *recvs, w, m, v), name=name, comm=comm)
    return outs if comm is None else (outs, couts)


def _adamw_replicated(gathered, w, m, v, name):
    r, lanes = w.shape

    def body(g_ref, w_ref, m_ref, v_ref, g_out, d_out, m_out, v_out):
        g = g_ref[0]
        for k in range(1, N_DEV):
            g = g + g_ref[k]
        delta, m_new, v_new = _adamw_math(w_ref[...], g, m_ref[...], v_ref[...])
        g_out[...] = g
        d_out[...] = delta
        m_out[...] = m_new
        v_out[...] = v_new

    return pl.pallas_call(
        body,
        out_shape=[jax.ShapeDtypeStruct((r, lanes), F32)] * 4,
        name=name,
    )(gathered, w, m, v)


def _wire_shard(name, shard):
    return (shard.T if name in COL_SHARDED else shard).astype(WIRE)


def _full_weight(gathered):
    return gathered.reshape(N_DEV * gathered.shape[1], gathered.shape[2])


def _to_slabs(gfull):
    return gfull.reshape(N_DEV, gfull.shape[0] // N_DEV, gfull.shape[1])


def _pack_small(arrs):
    rows = []
    for a in arrs:
        flat = a.astype(F32).reshape(-1)
        pad = (-flat.shape[0]) % LANES
        rows.append(jnp.pad(flat, (0, pad)).reshape(-1, LANES))
    packed = jnp.concatenate(rows, axis=0)
    return jnp.pad(packed, ((0, (-packed.shape[0]) % 8), (0, 0)))


def _unpack_small(packed, shapes):
    out, off = [], 0
    for shp in shapes:
        n = math.prod(shp)
        rows = -(-n // LANES)
        out.append(packed[off:off + rows].reshape(-1)[:n].reshape(shp))
        off += rows
    return out


def _of(layer, *names):
    return tuple((layer, n) for n in names)


MLP_W = ("w_ff1", "w_ff2", "w_pe", "w_pg")

GATHERS = (
    ("rms_mix_l0", _of(0, "w_in")),
    ("proj_qkv_l0", _of(0, "w_up_a", "w_up_b", "w_o")),
    ("proj_gate_l0", _of(0, "w_pe", "w_pg")),
    ("sb_fwd_l0", _of(0, "w_ff1", "w_ff2")),
    ("swa_fwd_l0", _of(1, "w_in")),
    ("sb_fwd_l1", _of(1, "w_up_a", "w_up_b", "w_o", "w_pe", "w_pg", "w_ff1")),
    ("swa_fwd_l1", _of(1, "w_ff2")),
)
REDUCES = (
    (_of(1, "w_ff1"), "dw_ff2_l0", "dact_l0"),
    (_of(1, "w_ff2"), "dw_ff2_l0", "dw_ff1_l0"),
    (_of(1, "w_in"), "dw_ff2_l0", "sb_bwd_l0"),
    (_of(1, "w_up_a", "w_up_b", "w_o", "w_pe", "w_pg"), "dw_ff2_l0", "swa_bwd_l0"),
    (_of(0, *MLP_W), "dh_mlp_l0", "sb_bwd_l0"),
    (_of(0, "w_o", "w_up_a", "w_up_b"), "do_a_l0", "sb_bwd_l0"),
    (_of(0, "w_in"), "dh_gate_l0", "dh_qkv_l0"),
)


def _merge_comms(comms):
    if len(comms) == 1:
        return comms[0]

    def cuts(counts):
        edges = [0]
        for c in counts:
            edges.append(edges[-1] + c)
        return [slice(a, b) for a, b in zip(edges[:-1], edges[1:])]

    s_in = cuts([len(c.inputs) for c in comms])
    s_out = cuts([len(c.out_shapes) for c in comms])
    s_sem = cuts([len(c.sems) for c in comms])

    def start(pos, cin, cout, csem):
        for c, i, o, s in zip(comms, s_in, s_out, s_sem):
            c.start(pos, cin[i], cout[o], csem[s])

    def finish(pos, cin, cout, csem):
        for c, i, o, s in zip(comms, s_in, s_out, s_sem):
            c.finish(pos, cin[i], cout[o], csem[s])

    return _Comm(sum([c.inputs for c in comms], []), sum([c.out_shapes for c in comms], []),
                 sum([c.sems for c in comms], []), start, finish)


class _LayerWeights:
    def __init__(self, full, layer):
        self.full, self.layer, self.cache = full, layer, {}

    def __getitem__(self, name):
        if name not in self.cache:
            if name == "w_qkv_t":
                self.cache[name] = self.full[(self.layer, "w_in")][:QKV_COLS]
            elif name == "w_gate_t":
                self.cache[name] = self.full[(self.layer, "w_in")][QKV_COLS:]
            else:
                base = name[:-2] if name.endswith("_t") else name
                assert (base in COL_SHARDED) == name.endswith("_t"), name
                self.cache[name] = self.full[(self.layer, base)]
        return self.cache[name]


class _Plan:
    def __init__(self, w_sh, pos):
        self.w_sh = dict(zip(WEIGHTS, w_sh))
        self.pos = pos
        self.full, self.gw, self.parts, self.recv = {}, {}, {}, {}
        self.slabs = {}
        self.hosted = {}
        for i, (host, _) in enumerate(GATHERS):
            self.hosted.setdefault(host, []).append(("gather", i))
        for i, (_, sib_host, chip_host) in enumerate(REDUCES):
            self.hosted.setdefault(sib_host, []).append(("sibling", i))
            self.hosted.setdefault(chip_host, []).append(("chips", i))

    def _gather(self, i):
        return _gather_comm([_wire_shard(n, self.w_sh[n][layer]) for layer, n in GATHERS[i][1]])

    def _gathered(self, i, outs):
        for (layer, n), g in zip(GATHERS[i][1], outs):
            self.full[(layer, n)] = _full_weight(g)

    def weights(self, layer):
        return _LayerWeights(self.full, layer)

    def grad(self, layer, name, value):
        self.gw[(layer, name)] = value

    def _sibling(self, i):
        self.slabs[i] = [_to_slabs(self.gw[item]) for item in REDUCES[i][0]]
        return _rs_sibling_comm(self.slabs[i])

    def _sibling_done(self, i, outs):
        parts = _chip_partials(self.pos, self.slabs[i], outs, f"chip_partials_{i}")
        for item, part in zip(REDUCES[i][0], parts):
            self.parts[item] = part

    def _chips(self, i):
        return _rs_chips_comm([self.parts[item] for item in REDUCES[i][0]])

    def _chips_done(self, i, outs):
        for item, r in zip(REDUCES[i][0], outs):
            self.recv[item] = r

    def comm(self, name):
        if name not in self.hosted:
            return None
        make = {"gather": self._gather, "sibling": self._sibling, "chips": self._chips}
        return _merge_comms([make[kind](i) for kind, i in self.hosted[name]])

    def done(self, name, outs):
        took = {"gather": self._gathered, "sibling": self._sibling_done, "chips": self._chips_done}
        off = 0
        for kind, i in self.hosted[name]:
            n = len(GATHERS[i][1]) if kind == "gather" else len(REDUCES[i][0])
            took[kind](i, outs[off:off + n])
            off += n


def kernel(x, p, w_in, w_up_a, w_up_b, w_o, w_ff1, w_ff2, w_pe, w_pg, g_mix, g_mlp, g_pe, g_final, sinks, rel_bias, loss_target, m_w_in, m_w_up_a, m_w_up_b, m_w_o, m_w_ff1, m_w_ff2, m_w_pe, m_w_pg, m_g_mix, m_g_mlp, m_g_pe, m_g_final, m_sinks, m_rel_bias, v_w_in, v_w_up_a, v_w_up_b, v_w_o, v_w_ff1, v_w_ff2, v_w_pe, v_w_pg, v_g_mix, v_g_mlp, v_g_pe, v_g_final, v_sinks, v_rel_bias):
    w_sh = [w_in, w_up_a, w_up_b, w_o, w_ff1, w_ff2, w_pe, w_pg]
    m_sh = [m_w_in, m_w_up_a, m_w_up_b, m_w_o, m_w_ff1, m_w_ff2, m_w_pe, m_w_pg]
    v_sh = [v_w_in, v_w_up_a, v_w_up_b, v_w_o, v_w_ff1, v_w_ff2, v_w_pe, v_w_pg]
    depth = w_in.shape[0]
    assert depth == 2 and x.shape[-1] * 2 + QKV_COLS == w_in.shape[2] * N_DEV

    px, py, pc = _position()
    plan = _Plan(w_sh, jnp.stack([px, py, pc]).astype(jnp.int32))
    loss_row, grad_x, _, small = _local_step(
        x[0], p[:, 0], loss_target[0], plan.weights, g_mix, g_mlp, g_pe, g_final, sinks, rel_bias, plan=plan)

    small_g = _pack_small([small[n] for n in SMALL] + [loss_row[0, :1]])
    grad_w, delta_w, new_m, new_v = [], [], [], []
    for a, name in enumerate(WEIGHTS):
        parts = [plan.parts[(l, name)] for l in range(depth)]
        recvs = [plan.recv[(l, name)] for l in range(depth)]
        if name == "w_in":
            flip = lambda t: t.transpose(0, 2, 1)
            outs = [flip(o) for o in _adamw_weight(parts, recvs, flip(w_sh[a]), flip(m_sh[a]), flip(v_sh[a]),
                                                   f"adamw_{name}")]
        elif name == "w_up_a":
            outs, (small_all,) = _adamw_weight(parts, recvs, w_sh[a], m_sh[a], v_sh[a], f"adamw_{name}",
                                               grad_t=True, comm=_gather_comm([small_g]))
        else:
            outs = _adamw_weight(parts, recvs, w_sh[a], m_sh[a], v_sh[a], f"adamw_{name}",
                                 grad_t=name in COL_SHARDED)
        for lst, o in zip((grad_w, delta_w, new_m, new_v), outs):
            lst.append(o)

    small_w = [g_mix, g_mlp, g_pe, g_final, sinks, rel_bias]
    small_m = [m_g_mix, m_g_mlp, m_g_pe, m_g_final, m_sinks, m_rel_bias]
    small_v = [v_g_mix, v_g_mlp, v_g_pe, v_g_final, v_sinks, v_rel_bias]
    small_shapes = [a.shape for a in small_w] + [(1,)]
    zero = jnp.zeros((1,), F32)
    packed_s = _adamw_replicated(small_all, _pack_small(small_w + [zero]), _pack_small(small_m + [zero]),
                                 _pack_small(small_v + [zero + 1.0]), "adamw_replicated")
    sg, sd, sm, sv = [_unpack_small(t, small_shapes) for t in packed_s]
    loss = sg[-1][0]

    return (loss, grad_x[None], *grad_w, *sg[:-1], *delta_w, *sd[:-1], *new_m, *sm[:-1], *new_v, *sv[:-1])
```

```python
import functools
import math

import numpy as np
import jax
import jax.numpy as jnp
from jax import lax
from jax.experimental import pallas as pl
from jax.experimental.pallas import tpu as pltpu

F32 = jnp.float32
MXU = jnp.bfloat16
WIRE = jnp.bfloat16

HEAD_DIM = 64
SB_HEADS = 8
SW_HEADS = 8
SW_KV = 2
SW_GROUP = SW_HEADS // SW_KV
BLOCK = 128
N_BUCKETS = 32
MAX_DISTANCE = 128
EPS = 1e-6
SCALE = HEAD_DIM ** -0.5
SB_W = SB_HEADS * HEAD_DIM
SW_W = SW_HEADS * HEAD_DIM
QKV_COLS = 3 * SB_W + SW_W + 2 * SW_KV * HEAD_DIM
N_DEV = 8
LANES = 128
N_PAIR = SB_HEADS // 2
NEG = -1e30

ADAM_LR = 0.001
ADAM_B1 = 0.9
ADAM_B2 = 0.999
ADAM_EPS = 1e-08
ADAM_WD = 0.01
ADAM_STEP = 10

VMEM_LIMIT = 48 * 1024 * 1024
SB_TQ = 256
SB_DEAD = -105.0
SB_SUB = 4

WEIGHTS = ("w_in", "w_up_a", "w_up_b", "w_o", "w_ff1", "w_ff2", "w_pe", "w_pg")
COL_SHARDED = ("w_in", "w_up_a", "w_up_b", "w_ff1", "w_pe")
SMALL = ("g_mix", "g_mlp", "g_pe", "g_final", "sinks", "rel_bias")


def _cparams(**kw):
    return pltpu.CompilerParams(vmem_limit_bytes=VMEM_LIMIT, **kw)


def _dot(a, b):
    return jnp.dot(a, b, preferred_element_type=F32)


def _dot_nt(a, b):
    return lax.dot_general(a, b, (((1,), (1,)), ((), ())), preferred_element_type=F32)


def _dot_tn(a, b):
    return lax.dot_general(a, b, (((0,), (0,)), ((), ())), preferred_element_type=F32)


def _tile(n, target, unit=LANES):
    if n <= target:
        return n
    t = (target // unit) * unit
    while t > unit and n % t:
        t -= unit
    assert n % t == 0, (n, target)
    return t


def _sigmoid(x):
    return 0.5 * jnp.tanh(0.5 * x) + 0.5


class _Comm:
    def __init__(self, inputs, out_shapes, sems, start, finish):
        self.inputs, self.out_shapes, self.sems = list(inputs), list(out_shapes), list(sems)
        self.start, self.finish = start, finish


def _call(body, *, grid, in_specs, out_specs, out_shape, scratch_shapes=(), args, name, comm=None):
    n_in, n_out, n_scr = len(in_specs), len(out_shape), len(scratch_shapes)
    if comm is None:
        outs = pl.pallas_call(body, grid=grid, in_specs=list(in_specs), out_specs=list(out_specs),
                              out_shape=list(out_shape), scratch_shapes=list(scratch_shapes),
                              compiler_params=_cparams(), name=name)(*args)
        return list(outs), None
    ci, co = len(comm.inputs), len(comm.out_shapes)
    any_spec = pl.BlockSpec(memory_space=pl.ANY)

    def wrapped(*refs):
        ins, cin = refs[:n_in], refs[n_in:n_in + ci]
        o0 = n_in + ci
        outs, cout = refs[o0:o0 + n_out], refs[o0 + n_out:o0 + n_out + co]
        s0 = o0 + n_out + co
        scr, csem = refs[s0:s0 + n_scr], refs[s0 + n_scr:]
        ids = [pl.program_id(d) for d in range(len(grid))]
        first = functools.reduce(jnp.logical_and, [i == 0 for i in ids])
        last = functools.reduce(jnp.logical_and, [i == g - 1 for i, g in zip(ids, grid)])
        pos = (lax.axis_index("x"), lax.axis_index("y"), lax.axis_index("c"))

        @pl.when(first)
        def _():
            comm.start(pos, cin, cout, csem)

        body(*ins, *outs, *scr)

        @pl.when(last)
        def _():
            comm.finish(pos, cin, cout, csem)

    outs = pl.pallas_call(wrapped, grid=grid, in_specs=list(in_specs) + [any_spec] * ci,
                          out_specs=list(out_specs) + [any_spec] * co,
                          out_shape=list(out_shape) + comm.out_shapes,
                          scratch_shapes=list(scratch_shapes) + comm.sems,
                          compiler_params=_cparams(), name=name)(*args, *comm.inputs)
    return list(outs[:n_out]), list(outs[n_out:])


def _accumulate(o_ref, value, first):
    @pl.when(first)
    def _():
        o_ref[...] = value

    @pl.when(jnp.logical_not(first))
    def _():
        o_ref[...] += value


def _mm(a, b, *, ta=False, tb=False, extras=(), epi=None, out_dtypes=(F32,),
        tm=1024, tn=1024, tk=1024, name, comm=None):
    if ta:
        kdim, m = a.shape
    else:
        m, kdim = a.shape
    n = b.shape[0] if tb else b.shape[1]
    assert (b.shape[1] if tb else b.shape[0]) == kdim
    tm, tn, tk = _tile(m, tm), _tile(n, tn), _tile(kdim, tk)
    nk = kdim // tk
    n_ex, n_out = len(extras), len(out_dtypes)

    a_spec = (pl.BlockSpec((tk, tm), lambda i, j, k: (k, i)) if ta
              else pl.BlockSpec((tm, tk), lambda i, j, k: (i, k)))
    b_spec = (pl.BlockSpec((tn, tk), lambda i, j, k: (j, k)) if tb
              else pl.BlockSpec((tk, tn), lambda i, j, k: (k, j)))
    ex_specs = []
    for e in extras:
        assert e.shape in ((m, n), (1, n), (m, 1)), (e.shape, m, n)
        if e.shape == (m, n):
            ex_specs.append(pl.BlockSpec((tm, tn), lambda i, j, k: (i, j)))
        elif e.shape[0] == 1:
            ex_specs.append(pl.BlockSpec((1, tn), lambda i, j, k: (0, j)))
        else:
            ex_specs.append(pl.BlockSpec((tm, 1), lambda i, j, k: (i, 0)))
    out_specs, out_shape, row_sums = [], [], []
    for dt in out_dtypes:
        kind = dt[1] if isinstance(dt, tuple) else "tile"
        row_sums.append(kind == "rowsum")
        if kind == "tile":
            out_specs.append(pl.BlockSpec((tm, tn), lambda i, j, k: (i, j)))
            out_shape.append(jax.ShapeDtypeStruct((m, n), dt))
            continue
        assert tn == n, "per-row and summed outputs need whole rows in one tile"
        if kind == "col":
            out_specs.append(pl.BlockSpec((tm, 1), lambda i, j, k: (i, 0)))
            out_shape.append(jax.ShapeDtypeStruct((m, 1), dt[0]))
        else:
            out_specs.append(pl.BlockSpec((1, tn), lambda i, j, k: (0, 0)))
            out_shape.append(jax.ShapeDtypeStruct((1, n), dt[0]))

    def body(a_ref, b_ref, *rest):
        ex_refs = rest[:n_ex]
        out_refs = rest[n_ex:n_ex + n_out]
        acc = rest[-1]
        k = pl.program_id(2)
        first_rows = pl.program_id(0) == 0

        def prod():
            av = a_ref[...].astype(MXU)
            bv = b_ref[...].astype(MXU)
            return _dot_tn(av, bv) if ta else (_dot_nt(av, bv) if tb else _dot(av, bv))

        def finish(res):
            if epi is not None:
                res = epi(res, *[e[...] for e in ex_refs])
            if not isinstance(res, tuple):
                res = (res,)
            for o_ref, r, summed in zip(out_refs, res, row_sums):
                if summed:
                    _accumulate(o_ref, r.astype(o_ref.dtype), first_rows)
                else:
                    o_ref[...] = r.astype(o_ref.dtype)

        if nk == 1:
            finish(prod())
            return

        @pl.when(k == 0)
        def _():
            acc[...] = prod()

        @pl.when(jnp.logical_and(k > 0, k < nk - 1))
        def _():
            acc[...] += prod()

        @pl.when(k == nk - 1)
        def _():
            finish(acc[...] + prod())

    outs, couts = _call(
        body,
        grid=(m // tm, n // tn, nk),
        in_specs=[a_spec, b_spec] + ex_specs,
        out_specs=out_specs,
        out_shape=out_shape,
        scratch_shapes=[pltpu.VMEM((tm, tn), F32)],
        args=(a, b, *extras), name=name, comm=comm)
    res = outs[0] if n_out == 1 else tuple(outs)
    return res if comm is None else (res, couts)


def _rms_fwd(x, g, name, comm=None):
    s, d = x.shape
    tr = _tile(s, 256)

    def body(x_ref, g_ref, h_ref, r_ref):
        xf = x_ref[...]
        r = lax.rsqrt(jnp.mean(xf * xf, axis=-1, keepdims=True) + EPS)
        h_ref[...] = ((xf * r) * g_ref[...]).astype(h_ref.dtype)
        r_ref[...] = r

    outs, couts = _call(
        body,
        grid=(s // tr,),
        in_specs=[pl.BlockSpec((tr, d), lambda i: (i, 0)), pl.BlockSpec((1, d), lambda i: (0, 0))],
        out_specs=[pl.BlockSpec((tr, d), lambda i: (i, 0)), pl.BlockSpec((tr, 1), lambda i: (i, 0))],
        out_shape=[jax.ShapeDtypeStruct((s, d), MXU), jax.ShapeDtypeStruct((s, 1), F32)],
        args=(x, g), name=name, comm=comm)
    return tuple(outs) if comm is None else (tuple(outs), couts)


def _loss_head(x, g, target, name):
    s, d = x.shape
    tr = _tile(s, 256)

    def body(x_ref, g_ref, t_ref, loss_ref, dx_ref, dg_ref):
        @pl.when(pl.program_id(0) == 0)
        def _():
            dg_ref[...] = jnp.zeros_like(dg_ref)
            loss_ref[...] = jnp.zeros_like(loss_ref)

        xf = x_ref[...]
        gv = g_ref[...]
        r = lax.rsqrt(jnp.mean(xf * xf, axis=-1, keepdims=True) + EPS)
        xhat = xf * r
        err = xhat * gv - t_ref[...]
        loss_ref[...] += 0.5 * jnp.sum(jnp.mean(err * err, axis=-1, keepdims=True), axis=0, keepdims=True)
        dy = err * (1.0 / d)
        dxhat = dy * gv
        mean = jnp.mean(dxhat * xhat, axis=-1, keepdims=True)
        dx_ref[...] = r * (dxhat - xhat * mean)
        dg_ref[...] += jnp.sum(dy * xhat, axis=0, keepdims=True)

    row = pl.BlockSpec((tr, d), lambda i: (i, 0))
    vec = pl.BlockSpec((1, d), lambda i: (0, 0))
    return pl.pallas_call(
        body,
        grid=(s // tr,),
        in_specs=[row, vec, row],
        out_specs=[pl.BlockSpec((1, LANES), lambda i: (0, 0)), row, vec],
        out_shape=[jax.ShapeDtypeStruct((1, LANES), F32), jax.ShapeDtypeStruct((s, d), F32),
                   jax.ShapeDtypeStruct((1, d), F32)],
        compiler_params=_cparams(),
        name=name,
    )(x, g, target)


def _mix_fwd(oa, ob, wa_t, wb_t, gates, name):
    s, kd = oa.shape
    d = wa_t.shape[0]
    tm, tn = _tile(s, 1024), _tile(d, 1024)
    nj = d // tn

    def body(oa_ref, ob_ref, wa_ref, wb_ref, ga_ref, gb_ref, out_ref):
        ya = _dot_nt(oa_ref[...], wa_ref[...])
        yb = _dot_nt(ob_ref[...], wb_ref[...])
        out_ref[...] = (_sigmoid(ga_ref[...].astype(F32)) * ya
                        + _sigmoid(gb_ref[...].astype(F32)) * yb).astype(out_ref.dtype)

    o_spec = pl.BlockSpec((tm, kd), lambda i, j: (i, 0))
    w_spec = pl.BlockSpec((tn, kd), lambda i, j: (j, 0))
    return pl.pallas_call(
        body,
        grid=(s // tm, nj),
        in_specs=[o_spec, o_spec, w_spec, w_spec,
                  pl.BlockSpec((tm, tn), lambda i, j: (i, j)),
                  pl.BlockSpec((tm, tn), lambda i, j: (i, j + nj))],
        out_specs=pl.BlockSpec((tm, tn), lambda i, j: (i, j)),
        out_shape=jax.ShapeDtypeStruct((s, d), MXU),
        compiler_params=_cparams(),
        name=name,
    )(oa, ob, wa_t, wb_t, gates, gates)


def _mix_bwd(dx, w_o, oa, ob, wa_t, wb_t, gates, name):
    s, kd = oa.shape
    d = wa_t.shape[0]
    tm = _tile(s, 512)

    def body(dx_ref, wo_ref, oa_ref, ob_ref, wa_ref, wb_ref, g_ref, dya_ref, dyb_ref, dg_ref):
        dm = _dot_nt(dx_ref[...], wo_ref[...])
        ya = _dot_nt(oa_ref[...], wa_ref[...])
        yb = _dot_nt(ob_ref[...], wb_ref[...])
        sa = _sigmoid(g_ref[:, :d].astype(F32))
        sb = _sigmoid(g_ref[:, d:].astype(F32))
        dya_ref[...] = (dm * sa).astype(dya_ref.dtype)
        dyb_ref[...] = (dm * sb).astype(dyb_ref.dtype)
        dg_ref[:, :d] = (dm * ya * sa * (1.0 - sa)).astype(dg_ref.dtype)
        dg_ref[:, d:] = (dm * yb * sb * (1.0 - sb)).astype(dg_ref.dtype)

    def rows(width):
        return pl.BlockSpec((tm, width), lambda i: (i, 0))

    def whole(arr):
        return pl.BlockSpec(arr.shape, lambda i: (0, 0))

    return pl.pallas_call(
        body,
        grid=(s // tm,),
        in_specs=[rows(d), whole(w_o), rows(kd), rows(kd), whole(wa_t), whole(wb_t), rows(2 * d)],
        out_specs=[rows(d), rows(d), rows(2 * d)],
        out_shape=[jax.ShapeDtypeStruct((s, d), MXU), jax.ShapeDtypeStruct((s, d), MXU),
                   jax.ShapeDtypeStruct((s, 2 * d), MXU)],
        compiler_params=_cparams(),
        name=name,
    )(dx, w_o, oa, ob, wa_t, wb_t, gates)


def _ple(p, w_pe_t, h, w_pg, other, *, backward, name, next_g=None):
    s, kp = p.shape
    d = w_pe_t.shape[0]
    tm, tn = _tile(s, 1024), _tile(d, 1024)
    with_norm = next_g is not None
    assert not (with_norm and (backward or tn != d))

    def body(p_ref, wpe_ref, h_ref, wpg_ref, other_ref, *rest):
        out_refs = rest[1:] if with_norm else rest
        pe = _dot_nt(p_ref[...].astype(MXU), wpe_ref[...])
        gt = _dot(h_ref[...], wpg_ref[...])
        sg = _sigmoid(gt)
        if backward:
            dout = other_ref[...]
            out_refs[0][...] = (dout * sg).astype(out_refs[0].dtype)
            out_refs[1][...] = (dout * pe * sg * (1.0 - sg)).astype(out_refs[1].dtype)
        elif with_norm:
            x_new, h_new, r_new = _residual_norm(pe * sg, other_ref[...], rest[0][...])
            out_refs[0][...] = x_new
            out_refs[1][...] = h_new.astype(out_refs[1].dtype)
            out_refs[2][...] = r_new
        else:
            out_refs[0][...] = other_ref[...] + pe * sg

    t_spec = pl.BlockSpec((tm, tn), lambda i, j: (i, j))
    if backward:
        out_specs, out_shape = [t_spec, t_spec], [jax.ShapeDtypeStruct((s, d), MXU)] * 2
    else:
        out_specs, out_shape = [t_spec], [jax.ShapeDtypeStruct((s, d), F32)]
    in_specs = [pl.BlockSpec((tm, kp), lambda i, j: (i, 0)),
                pl.BlockSpec((tn, kp), lambda i, j: (j, 0)),
                pl.BlockSpec((tm, d), lambda i, j: (i, 0)),
                pl.BlockSpec((d, tn), lambda i, j: (0, j)),
                t_spec]
    args = [p, w_pe_t, h, w_pg, other]
    if with_norm:
        in_specs.append(pl.BlockSpec((1, tn), lambda i, j: (0, j)))
        args.append(next_g)
        out_specs += [t_spec, pl.BlockSpec((tm, 1), lambda i, j: (i, 0))]
        out_shape += [jax.ShapeDtypeStruct((s, d), MXU), jax.ShapeDtypeStruct((s, 1), F32)]
    outs = pl.pallas_call(
        body,
        grid=(s // tm, d // tn),
        in_specs=in_specs,
        out_specs=out_specs,
        out_shape=out_shape,
        compiler_params=_cparams(),
        name=name,
    )(*args)
    return tuple(outs) if (backward or with_norm) else outs[0]


def _split_dot(x, tri):
    hi = x.astype(jnp.bfloat16)
    lo = (x - hi.astype(F32)).astype(jnp.bfloat16)
    return _dot(hi, tri) + _dot(lo, tri)


def _log_sigmoids(z):
    lb = jnp.minimum(z, 0.0) - jnp.log(1.0 + jnp.exp(-jnp.abs(z)))
    return lb, lb - z


def _head_lanes(hh):
    lane = lax.broadcasted_iota(jnp.int32, (1, LANES), 1)
    return jnp.logical_and(lane >= hh * HEAD_DIM, lane < (hh + 1) * HEAD_DIM)


def _sb_fwd(qkv, name, comm=None):
    s = qkv.shape[0]
    tq = _tile(s, SB_TQ)
    nsub = SB_SUB if (s // tq) % SB_SUB == 0 else 1

    def body(q_ref, k_ref, v_ref, o_ref):
        row = lax.broadcasted_iota(jnp.int32, (tq, tq), 0)
        col = lax.broadcasted_iota(jnp.int32, (tq, tq), 1)
        causal = col < row
        tri = jnp.where(row > col, 1.0, 0.0).astype(jnp.bfloat16)
        started = [_sb_fwd_straight(q_ref, k_ref, v_ref, pl.program_id(1) * nsub + sub, sub, tq, causal, tri)
                   for sub in range(nsub)]
        for sub, (block, i, cs, accs) in enumerate(started):
            def top(cs):
                return jnp.maximum(jnp.max(cs[0]), jnp.max(cs[1]))

            def live(st):
                return jnp.logical_and(st[0] >= 0, st[1] > SB_DEAD)

            def walk(st, block=block):
                cs, accs = block(st[0], st[2], st[3], False)
                return st[0] - 1, top(cs), cs, accs

            accs = lax.while_loop(live, walk, (i - 2, top(cs), cs, accs))[3]
            o_ref[sub * tq:(sub + 1) * tq, :] = jnp.where(_head_lanes(0), accs[0], accs[1]).astype(o_ref.dtype)

    outs, couts = _call(
        body,
        grid=(N_PAIR, s // (nsub * tq)),
        in_specs=[pl.BlockSpec((nsub * tq, LANES), lambda p, i: (i, p)),
                  pl.BlockSpec((s, LANES), lambda p, i: (0, N_PAIR + p)),
                  pl.BlockSpec((s, LANES), lambda p, i: (0, 2 * N_PAIR + p))],
        out_specs=[pl.BlockSpec((nsub * tq, LANES), lambda p, i: (i, p))],
        out_shape=[jax.ShapeDtypeStruct((s, SB_W), MXU)],
        args=(qkv, qkv, qkv), name=name, comm=comm)
    return outs[0] if comm is None else (outs[0], couts)


def _sb_fwd_straight(q_ref, k_ref, v_ref, i, sub, tq, causal, tri):
    qf = q_ref[sub * tq:(sub + 1) * tq, :].astype(F32) * SCALE
    qms = [jnp.where(_head_lanes(hh), qf, 0.0).astype(MXU) for hh in range(2)]

    def block(kb, cs, accs, masked, present=None):
        rows = pl.ds(pl.multiple_of(kb * tq, tq), tq)
        ks, vs = k_ref[rows, :], v_ref[rows, :]
        off = 0.0 if present is None else (1.0 - present) * NEG
        new_c, new_acc = [], []
        for hh in range(2):
            lb, lm = _log_sigmoids(_dot_nt(qms[hh], ks))
            if masked:
                lm = jnp.where(causal, lm, 0.0)
            a = jnp.exp(lb + _split_dot(lm, tri) + (cs[hh] + off))
            if masked:
                a = jnp.where(causal, a, 0.0)
            new_acc.append(accs[hh] + _dot(a.astype(MXU), vs))
            row_sum = jnp.sum(lm, axis=1, keepdims=True)
            new_c.append(cs[hh] + (row_sum if present is None else row_sum * present))
        return tuple(new_c), tuple(new_acc)

    zc, za = jnp.zeros((tq, 1), F32), jnp.zeros((tq, LANES), F32)
    cs, accs = block(i, (zc, zc), (za, za), True)
    cs, accs = block(jnp.maximum(i - 1, 0), cs, accs, False, jnp.where(i > 0, 1.0, 0.0))
    return block, i, cs, accs


def _sb_bwd(qkv, do, name, comm=None):
    s = qkv.shape[0]
    tq = _tile(s, SB_TQ)
    nq = s // tq
    nsub = SB_SUB if nq % SB_SUB == 0 else 1
    nsteps = nq // nsub

    def body(q_ref, k_ref, v_ref, do_ref, dq_ref, dk_ref, dv_ref, dk_acc, dv_acc, carries):
        step = pl.program_id(1)

        @pl.when(step == 0)
        def _():
            dk_acc[...] = jnp.zeros_like(dk_acc)
            dv_acc[...] = jnp.zeros_like(dv_acc)

        row = lax.broadcasted_iota(jnp.int32, (tq, tq), 0)
        col = lax.broadcasted_iota(jnp.int32, (tq, tq), 1)
        causal = col < row
        tri_rev = jnp.where(row > col, 1.0, 0.0).astype(jnp.bfloat16)
        tri_excl = jnp.where(row < col, 1.0, 0.0).astype(jnp.bfloat16)
        zc, za = jnp.zeros((tq, 1), F32), jnp.zeros((tq, LANES), F32)

        def top(cs):
            return jnp.maximum(jnp.max(cs[0]), jnp.max(cs[1]))

        def live(st):
            return jnp.logical_and(st[0] >= 0, st[1] > SB_DEAD)

        def row_sums(pre):
            return [jnp.sum(lm, axis=1, keepdims=True) for _, lm in pre]

        def query_block(sub):
            i = step * nsub + sub
            q_rows = slice(sub * tq, (sub + 1) * tq)
            qf = q_ref[q_rows, :].astype(F32) * SCALE
            dof = do_ref[q_rows, :]
            qms = [jnp.where(_head_lanes(hh), qf, 0.0).astype(MXU) for hh in range(2)]
            doms = [jnp.where(_head_lanes(hh), dof, jnp.zeros_like(dof)) for hh in range(2)]

            def terms(kb, masked):
                rows = pl.ds(pl.multiple_of(kb * tq, tq), tq)
                ks = k_ref[rows, :]
                out = []
                for hh in range(2):
                    lb, lm = _log_sigmoids(_dot_nt(qms[hh], ks))
                    if masked:
                        lm = jnp.where(causal, lm, 0.0)
                    out.append((lb, lm))
                return out

            def block(kb, cs, gpres, dqs, masked, gate=None, pre=None):
                rows = pl.ds(pl.multiple_of(kb * tq, tq), tq)
                ks, vs = k_ref[rows, :], v_ref[rows, :]
                pre = terms(kb, masked) if pre is None else pre
                new_g, new_dq = [], []
                dk_add, dv_add = None, None
                for hh in range(2):
                    lb, lm = pre[hh]
                    off = 0.0 if gate is None else (1.0 - gate) * NEG
                    a = jnp.exp(lb + _split_dot(lm, tri_rev) + (cs[hh] + off))
                    if masked:
                        a = jnp.where(causal, a, 0.0)
                    g = a * _dot_nt(doms[hh], vs)
                    gsum = gpres[hh] + _split_dot(g, tri_excl)
                    dz = g - (g + gsum) * jnp.exp(lb)
                    if masked:
                        dz = jnp.where(causal, dz, 0.0)
                    dzb = dz.astype(MXU)
                    new_dq.append(dqs[hh] + _dot(dzb, ks))
                    dk_h = _dot_tn(dzb, qms[hh])
                    dv_h = _dot_tn(a.astype(MXU), doms[hh])
                    dk_add = dk_h if dk_add is None else dk_add + dk_h
                    dv_add = dv_h if dv_add is None else dv_add + dv_h
                    new_g.append(gpres[hh] + jnp.sum(g, axis=1, keepdims=True))
                dk_acc[rows, :] += dk_add
                dv_acc[rows, :] += dv_add
                return tuple(new_g), tuple(new_dq)

            prev = jnp.maximum(i - 1, 0)
            gate = jnp.where(i > 0, 1.0, 0.0)
            t_diag, t_prev = terms(i, True), terms(prev, False)
            c_diag = row_sums(t_diag)
            sums = row_sums(t_prev)
            c_prev = tuple(c_diag[hh] + sums[hh] * gate for hh in range(2))
            return dict(i=i, prev=prev, gate=gate, q_rows=q_rows, terms=terms, block=block,
                        t_diag=t_diag, t_prev=t_prev, c_diag=c_diag, c_prev=c_prev)

        blocks = [query_block(sub) for sub in range(nsub)]
        for qb in blocks:
            def record(st, qb=qb):
                kb, cs = st[0], st[2]
                sums = row_sums(qb["terms"](kb, False))
                for hh in range(2):
                    carries[hh, kb] = cs[hh]
                cs = tuple(cs[hh] + sums[hh] for hh in range(2))
                return kb - 1, top(cs), cs

            first = lax.while_loop(live, record, (qb["i"] - 2, top(qb["c_prev"]), qb["c_prev"]))[0] + 1
            qb["mid"] = lax.fori_loop(
                first, qb["i"] - 1,
                lambda kb, cr, qb=qb: qb["block"](kb, (carries[0, kb], carries[1, kb]), cr[0], cr[1], False),
                ((zc, zc), (za, za)))
        for qb in blocks:
            gpres, dqs = qb["block"](qb["prev"], qb["c_diag"], *qb["mid"], False, qb["gate"], qb["t_prev"])
            dqs = qb["block"](qb["i"], (zc, zc), gpres, dqs, True, None, qb["t_diag"])[1]
            dq_ref[qb["q_rows"], :] = (jnp.where(_head_lanes(0), dqs[0], dqs[1]) * SCALE).astype(dq_ref.dtype)

        @pl.when(step == nsteps - 1)
        def _():
            dk_ref[...] = dk_acc[...].astype(dk_ref.dtype)
            dv_ref[...] = dv_acc[...].astype(dv_ref.dtype)

    blk = pl.BlockSpec((nsub * tq, LANES), lambda p, i: (i, p))
    full = pl.BlockSpec((s, LANES), lambda p, i: (0, p))
    outs, couts = _call(
        body,
        grid=(N_PAIR, nsteps),
        in_specs=[blk,
                  pl.BlockSpec((s, LANES), lambda p, i: (0, N_PAIR + p)),
                  pl.BlockSpec((s, LANES), lambda p, i: (0, 2 * N_PAIR + p)),
                  blk],
        out_specs=[blk, full, full],
        out_shape=[jax.ShapeDtypeStruct((s, SB_W), MXU)] * 3,
        scratch_shapes=[pltpu.VMEM((s, LANES), F32), pltpu.VMEM((s, LANES), F32),
                        pltpu.VMEM((2, nq, tq, 1), F32)],
        args=(qkv, qkv, qkv, do), name=name, comm=comm)
    return tuple(outs) if comm is None else (tuple(outs), couts)


def _bucket_table():
    i = np.arange(BLOCK)[:, None]
    j = np.arange(2 * BLOCK)[None, :]
    d = np.maximum(BLOCK + i - j, 0)
    max_exact = N_BUCKETS // 2
    df = np.maximum(d, 1).astype(np.float32)
    large = max_exact + (np.log(df / max_exact) / math.log(MAX_DISTANCE / max_exact)
                         * (N_BUCKETS - max_exact)).astype(np.int32)
    large = np.minimum(large, N_BUCKETS - 1)
    return np.where(d < max_exact, d, large).astype(np.int32)


def _build_bias(rel_bias, buckets, name):
    def body(rb_ref, bk_ref, out_ref):
        h = pl.program_id(0)
        bk = bk_ref[...]
        acc = jnp.zeros(bk.shape, F32)
        for b in range(N_BUCKETS):
            acc = jnp.where(bk == b, rb_ref[b, h], acc)
        out_ref[...] = acc

    return pl.pallas_call(
        body,
        grid=(SW_HEADS,),
        in_specs=[pl.BlockSpec(memory_space=pltpu.SMEM),
                  pl.BlockSpec((BLOCK, 2 * BLOCK), lambda h: (0, 0))],
        out_specs=pl.BlockSpec((None, BLOCK, 2 * BLOCK), lambda h: (h, 0, 0)),
        out_shape=jax.ShapeDtypeStruct((SW_HEADS, BLOCK, 2 * BLOCK), F32),
        name=name,
    )(rel_bias, buckets)


def _bias_grad(dbias_layers, buckets, name):
    n_l = len(dbias_layers)

    def body(*refs):
        bk = refs[n_l][...]
        out_ref = refs[n_l + 1]
        db = refs[0][...]
        for r in refs[1:n_l]:
            db = db + r[...]
        lane = lax.broadcasted_iota(jnp.int32, (1, LANES), 1)
        acc = jnp.zeros((1, LANES), F32)
        for b in range(N_BUCKETS):
            part = jnp.sum(jnp.where(bk == b, db, 0.0), axis=1, keepdims=True)
            tot = jnp.sum(part, axis=0, keepdims=True)
            acc = jnp.where(lane == b, tot, acc)
        out_ref[...] = acc

    hspec = pl.BlockSpec((None, BLOCK, 2 * BLOCK), lambda h: (h, 0, 0))
    return pl.pallas_call(
        body,
        grid=(SW_HEADS,),
        in_specs=[hspec] * n_l + [pl.BlockSpec((BLOCK, 2 * BLOCK), lambda h: (0, 0))],
        out_specs=pl.BlockSpec((None, 1, LANES), lambda h: (h, 0, 0)),
        out_shape=jax.ShapeDtypeStruct((SW_HEADS, 1, LANES), F32),
        name=name,
    )(*dbias_layers, buckets)


GROUP_ROWS = SW_GROUP * BLOCK


def _group_lanes(g):
    lane = lax.broadcasted_iota(jnp.int32, (1, LANES), 1)
    gvec = jnp.zeros((1, LANES), jnp.int32) + g
    return jnp.where(lane >= HEAD_DIM, 1, 0) == gvec, gvec


def _stack_heads(x, g):
    kv_lanes, gvec = _group_lanes(g)
    parts = []
    for j in range(SW_GROUP):
        half = x[:, (j // 2) * LANES:(j // 2 + 1) * LANES]
        moved = jnp.where(gvec == j % 2, half, pltpu.roll(half, HEAD_DIM, 1))
        parts.append(jnp.where(kv_lanes, moved, 0.0))
    return jnp.concatenate(parts, axis=0)


def _unstack_heads(y, g):
    _, gvec = _group_lanes(g)
    heads = []
    for j in range(SW_GROUP):
        yj = y[j * BLOCK:(j + 1) * BLOCK]
        heads.append(jnp.where(gvec == j % 2, yj, pltpu.roll(yj, HEAD_DIM, 1)))
    pairs = [jnp.where(_head_lanes(0), heads[2 * p], heads[2 * p + 1]) for p in range(SW_GROUP // 2)]
    return jnp.concatenate(pairs, axis=1)


def _per_head_col(values):
    return jnp.concatenate([jnp.zeros((BLOCK, 1), F32) + v for v in values], axis=0)


def _swa_scores(qs, kp, kc, bias_ref, n):
    row = jnp.bitwise_and(lax.broadcasted_iota(jnp.int32, (GROUP_ROWS, BLOCK), 0), BLOCK - 1)
    col = lax.broadcasted_iota(jnp.int32, (GROUP_ROWS, BLOCK), 1)
    bias = bias_ref[...].reshape(GROUP_ROWS, 2 * BLOCK)
    s1 = _dot_nt(qs, kp) + bias[:, :BLOCK]
    s2 = _dot_nt(qs, kc) + bias[:, BLOCK:]
    no_prev = jnp.where(n > 0, 0, BLOCK)
    s1 = jnp.where(col > row + no_prev, s1, NEG)
    s2 = jnp.where(col <= row, s2, NEG)
    return s1, s2


def _swa_specs(s):
    q_blk = 3 * SB_W // (2 * LANES)
    k_blk = (3 * SB_W + SW_W) // LANES
    return (pl.BlockSpec((s, 2 * LANES), lambda g: (0, q_blk + g)),
            pl.BlockSpec((s, LANES), lambda g: (0, k_blk)),
            pl.BlockSpec((s, LANES), lambda g: (0, k_blk + 1)))


def _swa_fwd(qkv, bias, sinks, name, comm=None):
    s = qkv.shape[0]
    nb = s // BLOCK

    def body(sink_ref, q_ref, k_ref, v_ref, bias_ref, o_ref, lse_ref):
        g = pl.program_id(0)
        sink = _per_head_col([sink_ref[SW_GROUP * g + j] for j in range(SW_GROUP)])
        lane = lax.broadcasted_iota(jnp.int32, (1, LANES), 1)

        def step(n, carry):
            r0 = pl.multiple_of(n * BLOCK, BLOCK)
            p0 = pl.multiple_of(jnp.maximum(n - 1, 0) * BLOCK, BLOCK)
            cur, prev = pl.ds(r0, BLOCK), pl.ds(p0, BLOCK)
            qs = _stack_heads(q_ref[cur, :].astype(F32) * SCALE, g).astype(MXU)
            s1, s2 = _swa_scores(qs, k_ref[prev, :], k_ref[cur, :], bias_ref, n)
            m = jnp.maximum(jnp.max(jnp.maximum(s1, s2), axis=1, keepdims=True), sink)
            e1 = jnp.exp(s1 - m)
            e2 = jnp.exp(s2 - m)
            den = jnp.sum(e1 + e2, axis=1, keepdims=True) + jnp.exp(sink - m)
            o = _dot((e1 / den).astype(MXU), v_ref[prev, :]) + _dot((e2 / den).astype(MXU), v_ref[cur, :])
            o_ref[cur, :] = _unstack_heads(o, g).astype(o_ref.dtype)
            lse = m + jnp.log(den)
            lse_row = jnp.zeros((BLOCK, LANES), F32)
            for j in range(SW_GROUP):
                lse_row = jnp.where(lane == j, lse[j * BLOCK:(j + 1) * BLOCK], lse_row)
            lse_ref[cur, :] = lse_row
            return carry

        lax.fori_loop(0, nb, step, 0, unroll=2)

    outs, couts = _call(
        body,
        grid=(SW_KV,),
        in_specs=[pl.BlockSpec(memory_space=pltpu.SMEM), *_swa_specs(s),
                  pl.BlockSpec((SW_GROUP, BLOCK, 2 * BLOCK), lambda g: (g, 0, 0))],
        out_specs=[pl.BlockSpec((s, 2 * LANES), lambda g: (0, g)),
                   pl.BlockSpec((None, s, LANES), lambda g: (g, 0, 0))],
        out_shape=[jax.ShapeDtypeStruct((s, SW_W), MXU), jax.ShapeDtypeStruct((SW_KV, s, LANES), F32)],
        args=(sinks, qkv, qkv, qkv, bias), name=name, comm=comm)
    return tuple(outs) if comm is None else (tuple(outs), couts)


def _swa_bwd(qkv, bias, sinks, do, lse, name, comm=None):
    s = qkv.shape[0]
    nb = s // BLOCK

    def body(sink_ref, q_ref, k_ref, v_ref, bias_ref, do_ref, lse_ref,
             dq_ref, dk_ref, dv_ref, dbias_ref, dsink_ref, dk_acc, dv_acc):
        g = pl.program_id(0)
        sink = _per_head_col([sink_ref[SW_GROUP * g + j] for j in range(SW_GROUP)])
        lane = lax.broadcasted_iota(jnp.int32, (1, LANES), 1)

        @pl.when(g == 0)
        def _():
            dk_acc[...] = jnp.zeros_like(dk_acc)
            dv_acc[...] = jnp.zeros_like(dv_acc)

        dbias_ref[...] = jnp.zeros_like(dbias_ref)

        def step(n, dsink_rows):
            r0 = pl.multiple_of(n * BLOCK, BLOCK)
            p0 = pl.multiple_of(jnp.maximum(n - 1, 0) * BLOCK, BLOCK)
            cur, prev = pl.ds(r0, BLOCK), pl.ds(p0, BLOCK)
            qs = _stack_heads(q_ref[cur, :].astype(F32) * SCALE, g).astype(MXU)
            dos = _stack_heads(do_ref[cur, :].astype(F32), g).astype(MXU)
            kp, kc, vp, vc = k_ref[prev, :], k_ref[cur, :], v_ref[prev, :], v_ref[cur, :]
            lse_row = lse_ref[cur, :]
            lse = jnp.concatenate([jnp.sum(jnp.where(lane == j, lse_row, 0.0), axis=1, keepdims=True)
                                   for j in range(SW_GROUP)], axis=0)
            s1, s2 = _swa_scores(qs, kp, kc, bias_ref, n)
            pr1 = jnp.exp(s1 - lse)
            pr2 = jnp.exp(s2 - lse)
            dpr1 = _dot_nt(dos, vp)
            dpr2 = _dot_nt(dos, vc)
            delta = jnp.sum(pr1 * dpr1 + pr2 * dpr2, axis=1, keepdims=True)
            ds1 = pr1 * (dpr1 - delta)
            ds2 = pr2 * (dpr2 - delta)
            dbias_ref[:, :, :BLOCK] += ds1.reshape(SW_GROUP, BLOCK, BLOCK)
            dbias_ref[:, :, BLOCK:] += ds2.reshape(SW_GROUP, BLOCK, BLOCK)
            ds1b, ds2b = ds1.astype(MXU), ds2.astype(MXU)
            dq = _dot(ds1b, kp) + _dot(ds2b, kc)
            dq_ref[cur, :] = (_unstack_heads(dq, g) * SCALE).astype(dq_ref.dtype)
            dk_acc[prev, :] += _dot_tn(ds1b, qs)
            dk_acc[cur, :] += _dot_tn(ds2b, qs)
            dv_acc[prev, :] += _dot_tn(pr1.astype(MXU), dos)
            dv_acc[cur, :] += _dot_tn(pr2.astype(MXU), dos)
            return dsink_rows - jnp.exp(sink - lse) * delta

        rows = lax.fori_loop(0, nb, step, jnp.zeros((GROUP_ROWS, 1), F32), unroll=2)
        for j in range(SW_GROUP):
            dsink_ref[j] = jnp.broadcast_to(jnp.sum(rows[j * BLOCK:(j + 1) * BLOCK], axis=0, keepdims=True),
                                            (1, LANES))

        @pl.when(g == SW_KV - 1)
        def _():
            dk_ref[...] = dk_acc[...].astype(dk_ref.dtype)
            dv_ref[...] = dv_acc[...].astype(dv_ref.dtype)

    grp = pl.BlockSpec((s, 2 * LANES), lambda g: (0, g))
    kv_out = pl.BlockSpec((s, LANES), lambda g: (0, 0))
    bspec = pl.BlockSpec((SW_GROUP, BLOCK, 2 * BLOCK), lambda g: (g, 0, 0))
    outs, couts = _call(
        body,
        grid=(SW_KV,),
        in_specs=[pl.BlockSpec(memory_space=pltpu.SMEM), *_swa_specs(s), bspec, grp,
                  pl.BlockSpec((None, s, LANES), lambda g: (g, 0, 0))],
        out_specs=[grp, kv_out, kv_out, bspec, pl.BlockSpec((SW_GROUP, 1, LANES), lambda g: (g, 0, 0))],
        out_shape=[jax.ShapeDtypeStruct((s, SW_W), MXU),
                   jax.ShapeDtypeStruct((s, LANES), MXU),
                   jax.ShapeDtypeStruct((s, LANES), MXU),
                   jax.ShapeDtypeStruct((SW_HEADS, BLOCK, 2 * BLOCK), F32),
                   jax.ShapeDtypeStruct((SW_HEADS, 1, LANES), F32)],
        scratch_shapes=[pltpu.VMEM((s, LANES), F32), pltpu.VMEM((s, LANES), F32)],
        args=(sinks, qkv, qkv, qkv, bias, do, lse), name=name, comm=comm)
    return tuple(outs) if comm is None else (tuple(outs), couts)


class _NoPlan:
    def comm(self, name):
        return None

    def done(self, name, outs):
        pass

    def grad(self, layer, name, value):
        pass


def _run(plan, fn, *args, name, **kw):
    comm = plan.comm(name)
    if comm is None:
        return fn(*args, name=name, **kw)
    res, outs = fn(*args, name=name, comm=comm, **kw)
    plan.done(name, outs)
    return res


def _norm_bwd(dh, x, dres, r, g):
    xhat = x * r
    dxhat = dh * g
    dx = dres + r * (dxhat - xhat * jnp.mean(dxhat * xhat, axis=-1, keepdims=True))
    return dx, dx, jnp.sum(dh * xhat, axis=0, keepdims=True)


def _residual_norm(acc, res, g):
    x = res + acc
    r = lax.rsqrt(jnp.mean(x * x, axis=-1, keepdims=True) + EPS)
    return x, (x * r) * g, r


def _layer_fwd(x, p, w, g_mix, g_mlp, g_pe, sinks, bias, tag, plan, normed_x=None, next_g=None):
    h1, r1 = normed_x if normed_x is not None else _run(plan, _rms_fwd, x, g_mix, name=f"rms_mix_{tag}")
    qkv = _run(plan, _mm, h1, w["w_qkv_t"], tb=True, out_dtypes=(MXU,), tn=QKV_COLS, name=f"proj_qkv_{tag}")
    gates = _run(plan, _mm, h1, w["w_gate_t"], tb=True, out_dtypes=(MXU,), tn=2048, name=f"proj_gate_{tag}")
    oa = _run(plan, _sb_fwd, qkv, name=f"sb_fwd_{tag}")
    ob, lse = _run(plan, _swa_fwd, qkv, bias, sinks, name=f"swa_fwd_{tag}")
    merged = _mix_fwd(oa, ob, w["w_up_a_t"], w["w_up_b_t"], gates, f"mix_fwd_{tag}")
    d = x.shape[1]
    normed = (F32, MXU, (F32, "col"))
    x1, h2, r2 = _run(plan, _mm, merged, w["w_o"], extras=(x, g_mlp), epi=_residual_norm, out_dtypes=normed,
                      tn=d, name=f"out_proj_{tag}")
    u, act = _run(plan, _mm, h2, w["w_ff1_t"], tb=True,
                  epi=lambda acc: (acc, jnp.square(jnp.maximum(acc, 0.0))),
                  out_dtypes=(MXU, MXU), tn=2048, name=f"ff1_{tag}")
    x2, h3, r3 = _run(plan, _mm, act, w["w_ff2"], extras=(x1, g_pe), epi=_residual_norm, out_dtypes=normed,
                      tn=d, name=f"ff2_{tag}")
    x3 = _ple(p, w["w_pe_t"], h3, w["w_pg"], x2, backward=False, name=f"ple_fwd_{tag}", next_g=next_g)
    next_normed = None
    if next_g is not None:
        x3, next_normed = x3[0], (x3[1], x3[2])
    saved = dict(x=x, h1=h1, r1=r1, gates=gates, qkv=qkv, lse=lse, oa=oa, ob=ob, merged=merged,
                 x1=x1, h2=h2, r2=r2, u=u, act=act, x2=x2, h3=h3, r3=r3)
    return x3, saved, next_normed


def _layer_bwd(dx3, sv, p, w, g_mix, g_mlp, g_pe, sinks, bias, layer, plan):
    tag = f"l{layer}"
    gw = {}
    wire = (WIRE,)

    def dw(name, a, b):
        gw[name] = _run(plan, _mm, a, b, ta=True, out_dtypes=wire, tk=2048, name=f"d{name}_{tag}")
        plan.grad(layer, name, gw[name])

    dpe, dgt = _ple(p, w["w_pe_t"], sv["h3"], w["w_pg"], dx3, backward=True, name=f"ple_bwd_{tag}")
    dw("w_pe", dpe, p)
    dw("w_pg", sv["h3"], dgt)
    d = dx3.shape[1]
    grads = (F32, MXU, (F32, "rowsum"))
    dx2, dx2b, dg_pe = _run(plan, _mm, dgt, w["w_pg"], tb=True, extras=(sv["x2"], dx3, sv["r3"], g_pe),
                            epi=_norm_bwd, out_dtypes=grads, tm=512, tn=d, name=f"dh_pe_{tag}")
    dw("w_ff2", sv["act"], dx2b)
    du = _run(plan, _mm, dx2b, w["w_ff2"], tb=True, extras=(sv["u"],),
              epi=lambda acc, u: acc * (2.0 * jnp.maximum(u.astype(F32), 0.0)), out_dtypes=(MXU,),
              tn=2048, name=f"dact_{tag}")
    dw("w_ff1", du, sv["h2"])
    dx1, dx1b, dg_mlp = _run(plan, _mm, du, w["w_ff1_t"], extras=(sv["x1"], dx2, sv["r2"], g_mlp),
                             epi=_norm_bwd, out_dtypes=grads, tm=1024, tn=d, name=f"dh_mlp_{tag}")
    dw("w_o", sv["merged"], dx1b)
    dya, dyb, dgates = _mix_bwd(dx1b, w["w_o"], sv["oa"], sv["ob"], w["w_up_a_t"], w["w_up_b_t"],
                                sv["gates"], f"mix_bwd_{tag}")
    dw("w_up_a", dya, sv["oa"])
    dw("w_up_b", dyb, sv["ob"])
    doa = _run(plan, _mm, dya, w["w_up_a_t"], out_dtypes=(MXU,), name=f"do_a_{tag}")
    dob = _run(plan, _mm, dyb, w["w_up_b_t"], out_dtypes=(MXU,), name=f"do_b_{tag}")
    dqb, dkb, dvb, dbias, dsink = _run(plan, _swa_bwd, sv["qkv"], bias, sinks, dob, sv["lse"],
                                       name=f"swa_bwd_{tag}")
    dqa, dka, dva = _run(plan, _sb_bwd, sv["qkv"], doa, name=f"sb_bwd_{tag}")
    dqkv = jnp.concatenate([dqa, dka, dva, dqb, dkb, dvb], axis=1)
    gw_qkv = _mm(dqkv, sv["h1"], ta=True, out_dtypes=wire, tk=2048, name=f"dw_qkv_{tag}")
    gw_gate = _mm(dgates, sv["h1"], ta=True, out_dtypes=wire, tk=2048, name=f"dw_gate_{tag}")
    gw["w_in"] = jnp.concatenate([gw_qkv, gw_gate], axis=0)
    plan.grad(layer, "w_in", gw["w_in"])
    dh1 = _run(plan, _mm, dgates, w["w_gate_t"], name=f"dh_gate_{tag}")
    dx, _, dg_mix = _run(plan, _mm, dqkv, w["w_qkv_t"], tk=768, extras=(dh1, sv["x"], dx1, sv["r1"], g_mix),
                         epi=lambda acc, prev, *rest: _norm_bwd(acc + prev, *rest), out_dtypes=grads,
                         tm=512, tn=d, name=f"dh_qkv_{tag}")
    small = dict(g_mix=dg_mix, g_mlp=dg_mlp, g_pe=dg_pe, sinks=dsink[:, 0, 0], dbias=dbias)
    return dx, gw, small


def _local_step(x, p, target, weights, g_mix, g_mlp, g_pe, g_final, sinks, rel_bias, plan=None):
    plan = _NoPlan() if plan is None else plan
    depth = g_mix.shape[0]
    buckets = jnp.asarray(_bucket_table())
    bias = _build_bias(rel_bias, buckets, "build_bias")
    saved, wfull = [], []
    h, normed = x, None
    for l in range(depth):
        wfull.append(weights(l))
        next_g = g_mix[l + 1:l + 2] if l + 1 < depth else None
        h, sv, normed = _layer_fwd(h, p[l], wfull[l], g_mix[l:l + 1], g_mlp[l:l + 1], g_pe[l:l + 1],
                                   sinks[l], bias, f"l{l}", plan, normed, next_g)
        saved.append(sv)
    loss_row, dx, dg_final = _loss_head(h, g_final[None, :], target, "loss_head")
    gws = [None] * depth
    smalls = [None] * depth
    for l in reversed(range(depth)):
        dx, gws[l], smalls[l] = _layer_bwd(dx, saved[l], p[l], wfull[l], g_mix[l:l + 1], g_mlp[l:l + 1],
                                           g_pe[l:l + 1], sinks[l], bias, l, plan)
    drel = _bias_grad([sm["dbias"] for sm in smalls], buckets, "bias_grad")[:, 0, :N_BUCKETS].T
    small = dict(
        g_mix=jnp.concatenate([sm["g_mix"] for sm in smalls], axis=0),
        g_mlp=jnp.concatenate([sm["g_mlp"] for sm in smalls], axis=0),
        g_pe=jnp.concatenate([sm["g_pe"] for sm in smalls], axis=0),
        g_final=dg_final[0],
        sinks=jnp.stack([sm["sinks"] for sm in smalls], axis=0),
        rel_bias=drel,
    )
    return loss_row, dx, gws, small


MESH_ID = pl.DeviceIdType.MESH


def _position():
    return lax.axis_index("x"), lax.axis_index("y"), lax.axis_index("c")


def _gather_comm(shards):
    n = len(shards)

    def copies(pos, x_refs, out_refs, sems):
        send_sems, recv_sems, local_sems = sems
        x, y, c = pos
        me, sibling = (x, y, c), (x, y, 1 - c)
        chips = [(1 - x, y), (x, 1 - y), (1 - x, 1 - y)]

        def slot(a, px, py, pc):
            return out_refs[a].at[4 * px + 2 * py + pc]

        def copy(a, k, block, to, src=None):
            return pltpu.make_async_remote_copy(
                src_ref=slot(a, *block) if src is None else src, dst_ref=slot(a, *block),
                send_sem=send_sems.at[a, k], recv_sem=recv_sems.at[a, k],
                device_id=to, device_id_type=MESH_ID)

        mine = [pltpu.make_async_copy(x_refs[a], slot(a, *me), local_sems.at[a]) for a in range(n)]
        first = []
        for a in range(n):
            first.append(copy(a, 0, me, sibling, src=x_refs[a]))
            first += [copy(a, 1 + j, me, (*chip, c), src=x_refs[a]) for j, chip in enumerate(chips)]
        return me, sibling, chips, copy, mine, first

    def start(pos, x_refs, out_refs, sems):
        _, _, _, _, mine, first = copies(pos, x_refs, out_refs, sems)
        for cp in mine + first:
            cp.start()

    def finish(pos, x_refs, out_refs, sems):
        me, sibling, chips, copy, mine, first = copies(pos, x_refs, out_refs, sems)
        c = pos[2]
        passed = []
        for j, chip in enumerate(chips):
            for a in range(n):
                copy(a, 1 + j, (*chip, c), me).wait_recv()
                fwd = copy(a, 4 + j, (*chip, c), sibling)
                fwd.start()
                passed.append(fwd)
        for a in range(n):
            copy(a, 0, sibling, me).wait_recv()
            for j, chip in enumerate(chips):
                copy(a, 4 + j, (*chip, 1 - c), me).wait_recv()
        for cp in first + passed:
            cp.wait_send()
        for cp in mine:
            cp.wait()

    return _Comm(shards, [jax.ShapeDtypeStruct((N_DEV,) + s.shape, s.dtype) for s in shards],
                 [pltpu.SemaphoreType.DMA((n, 7)), pltpu.SemaphoreType.DMA((n, 7)),
                  pltpu.SemaphoreType.DMA((n,))], start, finish)


def _exchange_comm(arrays, n_slots, route):
    n = len(arrays)

    def copies(pos, in_refs, out_refs, sems):
        send_sems, recv_sems = sems
        out = []
        for a in range(n):
            for j in range(n_slots):
                src_slot, peer = route(pos, j)
                out.append(pltpu.make_async_remote_copy(
                    src_ref=in_refs[a].at[src_slot], dst_ref=out_refs[a].at[j],
                    send_sem=send_sems.at[a, j], recv_sem=recv_sems.at[a, j],
                    device_id=peer, device_id_type=MESH_ID))
        return out

    def start(pos, in_refs, out_refs, sems):
        for cp in copies(pos, in_refs, out_refs, sems):
            cp.start()

    def finish(pos, in_refs, out_refs, sems):
        for cp in copies(pos, in_refs, out_refs, sems):
            cp.wait()

    return _Comm(arrays, [jax.ShapeDtypeStruct((n_slots,) + g.shape[1:], g.dtype) for g in arrays],
                 [pltpu.SemaphoreType.DMA((n, n_slots)), pltpu.SemaphoreType.DMA((n, n_slots))], start, finish)


def _rs_sibling_comm(gs):
    return _exchange_comm(gs, 4, lambda pos, j: (2 * j + (1 - pos[2]), (pos[0], pos[1], 1 - pos[2])))


def _chip_of(k, x, y):
    return x ^ ((k + 1) & 1), y ^ (((k + 1) >> 1) & 1)


def _chip_partials(pos, gs, recvs, name):
    n = len(gs)

    def body(pos_ref, *refs):
        for a in range(n):
            refs[2 * n + a][...] = (refs[a][...].astype(F32) + refs[n + a][...].astype(F32)
                                    ).astype(refs[2 * n + a].dtype)

    def g_map(k, pos_ref):
        cx, cy = _chip_of(k, pos_ref[0], pos_ref[1])
        return (4 * cx + 2 * cy + pos_ref[2], 0, 0)

    def r_map(k, pos_ref):
        cx, cy = _chip_of(k, pos_ref[0], pos_ref[1])
        return (2 * cx + cy, 0, 0)

    slab = [(None,) + g.shape[1:] for g in gs]
    return pl.pallas_call(
        body,
        grid_spec=pltpu.PrefetchScalarGridSpec(
            num_scalar_prefetch=1,
            grid=(4,),
            in_specs=[pl.BlockSpec(sh, g_map) for sh in slab] + [pl.BlockSpec(sh, r_map) for sh in slab],
            out_specs=[pl.BlockSpec(sh, lambda k, pos_ref: (k, 0, 0)) for sh in slab],
        ),
        out_shape=[jax.ShapeDtypeStruct((4,) + g.shape[1:], g.dtype) for g in gs],
        compiler_params=_cparams(),
        name=name,
    )(pos, *gs, *recvs)


def _rs_chips_comm(parts):
    return _exchange_comm(parts, 3, lambda pos, k: (k, (*_chip_of(k, pos[0], pos[1]), pos[2])))


def _adamw_math(w, g, m, v):
    m = ADAM_B1 * m + (1.0 - ADAM_B1) * g
    v = ADAM_B2 * v + (1.0 - ADAM_B2) * (g * g)
    m_hat = m / (1.0 - ADAM_B1 ** ADAM_STEP)
    v_hat = v / (1.0 - ADAM_B2 ** ADAM_STEP)
    delta = -ADAM_LR * (m_hat / (jnp.sqrt(v_hat) + ADAM_EPS) + ADAM_WD * w)
    return delta, m, v


def _adamw_weight(parts, recvs, w, m, v, name, grad_t=False, comm=None):
    depth, a, b = w.shape
    ta = _tile(a, 288, unit=LANES if grad_t else 16)
    ni = a // ta
    g_block = (b, ta) if grad_t else (ta, b)

    def body(*refs):
        p_refs, r_refs = refs[:depth], refs[depth:2 * depth]
        w_ref, m_ref, v_ref = refs[2 * depth:2 * depth + 3]
        g_out, d_out, m_out, v_out = refs[2 * depth + 3:]
        layer = pl.program_id(0)
        g = jnp.zeros(g_block, F32)
        for l in range(depth):
            gl = p_refs[l][...].astype(F32)
            for k in range(3):
                gl = gl + r_refs[l][k].astype(F32)
            g = jnp.where(layer == l, gl, g)
        if grad_t:
            g = g.T
        delta, m_new, v_new = _adamw_math(w_ref[...], g, m_ref[...], v_ref[...])
        g_out[...] = g
        d_out[...] = delta
        m_out[...] = m_new
        v_out[...] = v_new

    def hold(l):
        return lambda layer, i: jnp.where(layer == l, i, jnp.where(layer < l, 0, ni - 1))

    def g_index(slot, f):
        if grad_t:
            return lambda layer, i: (slot, 0, f(layer, i))
        return lambda layer, i: (slot, f(layer, i), 0)

    p_specs = [pl.BlockSpec((None,) + g_block, g_index(3, hold(l))) for l in range(depth)]
    r_specs = [pl.BlockSpec((3,) + g_block, g_index(0, hold(l))) for l in range(depth)]
    row = pl.BlockSpec((None, ta, b), lambda layer, i: (layer, i, 0))
    outs, couts = _call(
        body,
        grid=(depth, ni),
        in_specs=p_specs + r_specs + [row, row, row],
        out_specs=[row] * 4,
        out_shape=[jax.ShapeDtypeStruct(w.shape, F32)] * 4,
        args=(*parts, *recvs, w, m, v), name=name, comm=comm)
    return outs if comm is None else (outs, couts)


def _adamw_replicated(gathered, w, m, v, name):
    r, lanes = w.shape

    def body(g_ref, w_ref, m_ref, v_ref, g_out, d_out, m_out, v_out):
        g = g_ref[0]
        for k in range(1, N_DEV):
            g = g + g_ref[k]
        delta, m_new, v_new = _adamw_math(w_ref[...], g, m_ref[...], v_ref[...])
        g_out[...] = g
        d_out[...] = delta
        m_out[...] = m_new
        v_out[...] = v_new

    return pl.pallas_call(
        body,
        out_shape=[jax.ShapeDtypeStruct((r, lanes), F32)] * 4,
        name=name,
    )(gathered, w, m, v)


def _wire_shard(name, shard):
    return (shard.T if name in COL_SHARDED else shard).astype(WIRE)


def _full_weight(gathered):
    return gathered.reshape(N_DEV * gathered.shape[1], gathered.shape[2])


def _to_slabs(gfull):
    return gfull.reshape(N_DEV, gfull.shape[0] // N_DEV, gfull.shape[1])


def _pack_small(arrs):
    rows = []
    for a in arrs:
        flat = a.astype(F32).reshape(-1)
        pad = (-flat.shape[0]) % LANES
        rows.append(jnp.pad(flat, (0, pad)).reshape(-1, LANES))
    packed = jnp.concatenate(rows, axis=0)
    return jnp.pad(packed, ((0, (-packed.shape[0]) % 8), (0, 0)))


def _unpack_small(packed, shapes):
    out, off = [], 0
    for shp in shapes:
        n = math.prod(shp)
        rows = -(-n // LANES)
        out.append(packed[off:off + rows].reshape(-1)[:n].reshape(shp))
        off += rows
    return out


def _of(layer, *names):
    return tuple((layer, n) for n in names)


MLP_W = ("w_ff1", "w_ff2", "w_pe", "w_pg")

GATHERS = (
    ("rms_mix_l0", _of(0, "w_in")),
    ("proj_qkv_l0", _of(0, "w_up_a", "w_up_b", "w_o")),
    ("proj_gate_l0", _of(0, "w_pe", "w_pg")),
    ("sb_fwd_l0", _of(0, "w_ff1", "w_ff2")),
    ("swa_fwd_l0", _of(1, "w_in")),
    ("sb_fwd_l1", _of(1, "w_up_a", "w_up_b", "w_o", "w_pe", "w_pg", "w_ff1")),
    ("swa_fwd_l1", _of(1, "w_ff2")),
)
REDUCES = (
    (_of(1, "w_ff1"), "dw_ff2_l0", "dact_l0"),
    (_of(1, "w_ff2"), "dw_ff2_l0", "dw_ff1_l0"),
    (_of(1, "w_in"), "dw_ff2_l0", "sb_bwd_l0"),
    (_of(1, "w_up_a", "w_up_b", "w_o", "w_pe", "w_pg"), "dw_ff2_l0", "swa_bwd_l0"),
    (_of(0, *MLP_W), "dh_mlp_l0", "sb_bwd_l0"),
    (_of(0, "w_o", "w_up_a", "w_up_b"), "do_a_l0", "sb_bwd_l0"),
    (_of(0, "w_in"), "dh_gate_l0", "dh_qkv_l0"),
)


def _merge_comms(comms):
    if len(comms) == 1:
        return comms[0]

    def cuts(counts):
        edges = [0]
        for c in counts:
            edges.append(edges[-1] + c)
        return [slice(a, b) for a, b in zip(edges[:-1], edges[1:])]

    s_in = cuts([len(c.inputs) for c in comms])
    s_out = cuts([len(c.out_shapes) for c in comms])
    s_sem = cuts([len(c.sems) for c in comms])

    def start(pos, cin, cout, csem):
        for c, i, o, s in zip(comms, s_in, s_out, s_sem):
            c.start(pos, cin[i], cout[o], csem[s])

    def finish(pos, cin, cout, csem):
        for c, i, o, s in zip(comms, s_in, s_out, s_sem):
            c.finish(pos, cin[i], cout[o], csem[s])

    return _Comm(sum([c.inputs for c in comms], []), sum([c.out_shapes for c in comms], []),
                 sum([c.sems for c in comms], []), start, finish)


class _LayerWeights:
    def __init__(self, full, layer):
        self.full, self.layer, self.cache = full, layer, {}

    def __getitem__(self, name):
        if name not in self.cache:
            if name == "w_qkv_t":
                self.cache[name] = self.full[(self.layer, "w_in")][:QKV_COLS]
            elif name == "w_gate_t":
                self.cache[name] = self.full[(self.layer, "w_in")][QKV_COLS:]
            else:
                base = name[:-2] if name.endswith("_t") else name
                assert (base in COL_SHARDED) == name.endswith("_t"), name
                self.cache[name] = self.full[(self.layer, base)]
        return self.cache[name]


class _Plan:
    def __init__(self, w_sh, pos):
        self.w_sh = dict(zip(WEIGHTS, w_sh))
        self.pos = pos
        self.full, self.gw, self.parts, self.recv = {}, {}, {}, {}
        self.slabs = {}
        self.hosted = {}
        for i, (host, _) in enumerate(GATHERS):
            self.hosted.setdefault(host, []).append(("gather", i))
        for i, (_, sib_host, chip_host) in enumerate(REDUCES):
            self.hosted.setdefault(sib_host, []).append(("sibling", i))
            self.hosted.setdefault(chip_host, []).append(("chips", i))

    def _gather(self, i):
        return _gather_comm([_wire_shard(n, self.w_sh[n][layer]) for layer, n in GATHERS[i][1]])

    def _gathered(self, i, outs):
        for (layer, n), g in zip(GATHERS[i][1], outs):
            self.full[(layer, n)] = _full_weight(g)

    def weights(self, layer):
        return _LayerWeights(self.full, layer)

    def grad(self, layer, name, value):
        self.gw[(layer, name)] = value

    def _sibling(self, i):
        self.slabs[i] = [_to_slabs(self.gw[item]) for item in REDUCES[i][0]]
        return _rs_sibling_comm(self.slabs[i])

    def _siblings_done(self, received):
        slabs = sum([self.slabs[i] for i, _ in received], [])
        items = sum([list(REDUCES[i][0]) for i, _ in received], [])
        parts = _chip_partials(self.pos, slabs, sum([list(o) for _, o in received], []),
                               f"chip_partials_{received[0][0]}")
        for item, part in zip(items, parts):
            self.parts[item] = part

    def _chips(self, i):
        return _rs_chips_comm([self.parts[item] for item in REDUCES[i][0]])

    def _chips_done(self, i, outs):
        for item, r in zip(REDUCES[i][0], outs):
            self.recv[item] = r

    def comm(self, name):
        if name not in self.hosted:
            return None
        make = {"gather": self._gather, "sibling": self._sibling, "chips": self._chips}
        return _merge_comms([make[kind](i) for kind, i in self.hosted[name]])

    def done(self, name, outs):
        took = {"gather": self._gathered, "chips": self._chips_done}
        off, from_sibling = 0, []
        for kind, i in self.hosted[name]:
            n = len(GATHERS[i][1]) if kind == "gather" else len(REDUCES[i][0])
            if kind == "sibling":
                from_sibling.append((i, outs[off:off + n]))
            else:
                took[kind](i, outs[off:off + n])
            off += n
        if from_sibling:
            self._siblings_done(from_sibling)


def kernel(x, p, w_in, w_up_a, w_up_b, w_o, w_ff1, w_ff2, w_pe, w_pg, g_mix, g_mlp, g_pe, g_final, sinks, rel_bias, loss_target, m_w_in, m_w_up_a, m_w_up_b, m_w_o, m_w_ff1, m_w_ff2, m_w_pe, m_w_pg, m_g_mix, m_g_mlp, m_g_pe, m_g_final, m_sinks, m_rel_bias, v_w_in, v_w_up_a, v_w_up_b, v_w_o, v_w_ff1, v_w_ff2, v_w_pe, v_w_pg, v_g_mix, v_g_mlp, v_g_pe, v_g_final, v_sinks, v_rel_bias):
    w_sh = [w_in, w_up_a, w_up_b, w_o, w_ff1, w_ff2, w_pe, w_pg]
    m_sh = [m_w_in, m_w_up_a, m_w_up_b, m_w_o, m_w_ff1, m_w_ff2, m_w_pe, m_w_pg]
    v_sh = [v_w_in, v_w_up_a, v_w_up_b, v_w_o, v_w_ff1, v_w_ff2, v_w_pe, v_w_pg]
    depth = w_in.shape[0]
    assert depth == 2 and x.shape[-1] * 2 + QKV_COLS == w_in.shape[2] * N_DEV

    px, py, pc = _position()
    plan = _Plan(w_sh, jnp.stack([px, py, pc]).astype(jnp.int32))
    loss_row, grad_x, _, small = _local_step(
        x[0], p[:, 0], loss_target[0], plan.weights, g_mix, g_mlp, g_pe, g_final, sinks, rel_bias, plan=plan)

    small_g = _pack_small([small[n] for n in SMALL] + [loss_row[0, :1]])
    grad_w, delta_w, new_m, new_v = [], [], [], []
    for a, name in enumerate(WEIGHTS):
        parts = [plan.parts[(l, name)] for l in range(depth)]
        recvs = [plan.recv[(l, name)] for l in range(depth)]
        if name == "w_in":
            flip = lambda t: t.transpose(0, 2, 1)
            outs, (small_all,) = _adamw_weight(parts, recvs, flip(w_sh[a]), flip(m_sh[a]), flip(v_sh[a]),
                                               f"adamw_{name}", comm=_gather_comm([small_g]))
            outs = [flip(o) for o in outs]
        else:
            outs = _adamw_weight(parts, recvs, w_sh[a], m_sh[a], v_sh[a], f"adamw_{name}",
                                 grad_t=name in COL_SHARDED)
        for lst, o in zip((grad_w, delta_w, new_m, new_v), outs):
            lst.append(o)

    small_w = [g_mix, g_mlp, g_pe, g_final, sinks, rel_bias]
    small_m = [m_g_mix, m_g_mlp, m_g_pe, m_g_final, m_sinks, m_rel_bias]
    small_v = [v_g_mix, v_g_mlp, v_g_pe, v_g_final, v_sinks, v_rel_bias]
    small_shapes = [a.shape for a in small_w] + [(1,)]
    zero = jnp.zeros((1,), F32)
    packed_s = _adamw_replicated(small_all, _pack_small(small_w + [zero]), _pack_small(small_m + [zero]),
                                 _pack_small(small_v + [zero + 1.0]), "adamw_replicated")
    sg, sd, sm, sv = [_unpack_small(t, small_shapes) for t in packed_s]
    loss = sg[-1][0]

    return (loss, grad_x[None], *grad_w, *sg[:-1], *delta_w, *sd[:-1], *new_m, *sm[:-1], *new_v, *sv[:-1])
```

```python
import functools
import math

import numpy as np
import jax
import jax.numpy as jnp
from jax import lax
from jax.experimental import pallas as pl
from jax.experimental.pallas import tpu as pltpu

F32 = jnp.float32
MXU = jnp.bfloat16
WIRE = jnp.bfloat16

HEAD_DIM = 64
SB_HEADS = 8
SW_HEADS = 8
SW_KV = 2
SW_GROUP = SW_HEADS // SW_KV
BLOCK = 128
N_BUCKETS = 32
MAX_DISTANCE = 128
EPS = 1e-6
SCALE = HEAD_DIM ** -0.5
SB_W = SB_HEADS * HEAD_DIM
SW_W = SW_HEADS * HEAD_DIM
QKV_COLS = 3 * SB_W + SW_W + 2 * SW_KV * HEAD_DIM
N_DEV = 8
LANES = 128
N_PAIR = SB_HEADS // 2
NEG = -1e30

ADAM_LR = 0.001
ADAM_B1 = 0.9
ADAM_B2 = 0.999
ADAM_EPS = 1e-08
ADAM_WD = 0.01
ADAM_STEP = 10

VMEM_LIMIT = 48 * 1024 * 1024
SB_TQ = 256
SB_DEAD = -105.0
SB_SUB = 4

WEIGHTS = ("w_in", "w_up_a", "w_up_b", "w_o", "w_ff1", "w_ff2", "w_pe", "w_pg")
COL_SHARDED = ("w_in", "w_up_a", "w_up_b", "w_ff1", "w_pe")
SMALL = ("g_mix", "g_mlp", "g_pe", "g_final", "sinks", "rel_bias")


def _cparams(**kw):
    return pltpu.CompilerParams(vmem_limit_bytes=VMEM_LIMIT, **kw)


def _dot(a, b):
    return jnp.dot(a, b, preferred_element_type=F32)


def _dot_nt(a, b):
    return lax.dot_general(a, b, (((1,), (1,)), ((), ())), preferred_element_type=F32)


def _dot_tn(a, b):
    return lax.dot_general(a, b, (((0,), (0,)), ((), ())), preferred_element_type=F32)


def _tile(n, target, unit=LANES):
    if n <= target:
        return n
    t = (target // unit) * unit
    while t > unit and n % t:
        t -= unit
    assert n % t == 0, (n, target)
    return t


def _sigmoid(x):
    return 0.5 * jnp.tanh(0.5 * x) + 0.5


class _Comm:
    def __init__(self, inputs, out_shapes, sems, start, finish):
        self.inputs, self.out_shapes, self.sems = list(inputs), list(out_shapes), list(sems)
        self.start, self.finish = start, finish


def _call(body, *, grid, in_specs, out_specs, out_shape, scratch_shapes=(), args, name, comm=None):
    n_in, n_out, n_scr = len(in_specs), len(out_shape), len(scratch_shapes)
    if comm is None:
        outs = pl.pallas_call(body, grid=grid, in_specs=list(in_specs), out_specs=list(out_specs),
                              out_shape=list(out_shape), scratch_shapes=list(scratch_shapes),
                              compiler_params=_cparams(), name=name)(*args)
        return list(outs), None
    ci, co = len(comm.inputs), len(comm.out_shapes)
    any_spec = pl.BlockSpec(memory_space=pl.ANY)

    def wrapped(*refs):
        ins, cin = refs[:n_in], refs[n_in:n_in + ci]
        o0 = n_in + ci
        outs, cout = refs[o0:o0 + n_out], refs[o0 + n_out:o0 + n_out + co]
        s0 = o0 + n_out + co
        scr, csem = refs[s0:s0 + n_scr], refs[s0 + n_scr:]
        ids = [pl.program_id(d) for d in range(len(grid))]
        first = functools.reduce(jnp.logical_and, [i == 0 for i in ids])
        last = functools.reduce(jnp.logical_and, [i == g - 1 for i, g in zip(ids, grid)])
        pos = (lax.axis_index("x"), lax.axis_index("y"), lax.axis_index("c"))

        @pl.when(first)
        def _():
            comm.start(pos, cin, cout, csem)

        body(*ins, *outs, *scr)

        @pl.when(last)
        def _():
            comm.finish(pos, cin, cout, csem)

    outs = pl.pallas_call(wrapped, grid=grid, in_specs=list(in_specs) + [any_spec] * ci,
                          out_specs=list(out_specs) + [any_spec] * co,
                          out_shape=list(out_shape) + comm.out_shapes,
                          scratch_shapes=list(scratch_shapes) + comm.sems,
                          compiler_params=_cparams(), name=name)(*args, *comm.inputs)
    return list(outs[:n_out]), list(outs[n_out:])


def _accumulate(o_ref, value, first):
    @pl.when(first)
    def _():
        o_ref[...] = value

    @pl.when(jnp.logical_not(first))
    def _():
        o_ref[...] += value


def _mm(a, b, *, ta=False, tb=False, extras=(), epi=None, out_dtypes=(F32,),
        tm=1024, tn=1024, tk=1024, name, comm=None):
    if ta:
        kdim, m = a.shape
    else:
        m, kdim = a.shape
    n = b.shape[0] if tb else b.shape[1]
    assert (b.shape[1] if tb else b.shape[0]) == kdim
    tm, tn, tk = _tile(m, tm), _tile(n, tn), _tile(kdim, tk)
    nk = kdim // tk
    n_ex, n_out = len(extras), len(out_dtypes)

    a_spec = (pl.BlockSpec((tk, tm), lambda i, j, k: (k, i)) if ta
              else pl.BlockSpec((tm, tk), lambda i, j, k: (i, k)))
    b_spec = (pl.BlockSpec((tn, tk), lambda i, j, k: (j, k)) if tb
              else pl.BlockSpec((tk, tn), lambda i, j, k: (k, j)))
    ex_specs = []
    for e in extras:
        assert e.shape in ((m, n), (1, n), (m, 1)), (e.shape, m, n)
        if e.shape == (m, n):
            ex_specs.append(pl.BlockSpec((tm, tn), lambda i, j, k: (i, j)))
        elif e.shape[0] == 1:
            ex_specs.append(pl.BlockSpec((1, tn), lambda i, j, k: (0, j)))
        else:
            ex_specs.append(pl.BlockSpec((tm, 1), lambda i, j, k: (i, 0)))
    out_specs, out_shape, row_sums = [], [], []
    for dt in out_dtypes:
        kind = dt[1] if isinstance(dt, tuple) else "tile"
        row_sums.append(kind == "rowsum")
        if kind == "tile":
            out_specs.append(pl.BlockSpec((tm, tn), lambda i, j, k: (i, j)))
            out_shape.append(jax.ShapeDtypeStruct((m, n), dt))
            continue
        assert tn == n, "per-row and summed outputs need whole rows in one tile"
        if kind == "col":
            out_specs.append(pl.BlockSpec((tm, 1), lambda i, j, k: (i, 0)))
            out_shape.append(jax.ShapeDtypeStruct((m, 1), dt[0]))
        else:
            out_specs.append(pl.BlockSpec((1, tn), lambda i, j, k: (0, 0)))
            out_shape.append(jax.ShapeDtypeStruct((1, n), dt[0]))

    def body(a_ref, b_ref, *rest):
        ex_refs = rest[:n_ex]
        out_refs = rest[n_ex:n_ex + n_out]
        acc = rest[-1]
        k = pl.program_id(2)
        first_rows = pl.program_id(0) == 0

        def prod():
            av = a_ref[...].astype(MXU)
            bv = b_ref[...].astype(MXU)
            return _dot_tn(av, bv) if ta else (_dot_nt(av, bv) if tb else _dot(av, bv))

        def finish(res):
            if epi is not None:
                res = epi(res, *[e[...] for e in ex_refs])
            if not isinstance(res, tuple):
                res = (res,)
            for o_ref, r, summed in zip(out_refs, res, row_sums):
                if summed:
                    _accumulate(o_ref, r.astype(o_ref.dtype), first_rows)
                else:
                    o_ref[...] = r.astype(o_ref.dtype)

        if nk == 1:
            finish(prod())
            return

        @pl.when(k == 0)
        def _():
            acc[...] = prod()

        @pl.when(jnp.logical_and(k > 0, k < nk - 1))
        def _():
            acc[...] += prod()

        @pl.when(k == nk - 1)
        def _():
            finish(acc[...] + prod())

    outs, couts = _call(
        body,
        grid=(m // tm, n // tn, nk),
        in_specs=[a_spec, b_spec] + ex_specs,
        out_specs=out_specs,
        out_shape=out_shape,
        scratch_shapes=[pltpu.VMEM((tm, tn), F32)],
        args=(a, b, *extras), name=name, comm=comm)
    res = outs[0] if n_out == 1 else tuple(outs)
    return res if comm is None else (res, couts)


def _rms_fwd(x, g, name, comm=None):
    s, d = x.shape
    tr = _tile(s, 256)

    def body(x_ref, g_ref, h_ref, r_ref):
        xf = x_ref[...]
        r = lax.rsqrt(jnp.mean(xf * xf, axis=-1, keepdims=True) + EPS)
        h_ref[...] = ((xf * r) * g_ref[...]).astype(h_ref.dtype)
        r_ref[...] = r

    outs, couts = _call(
        body,
        grid=(s // tr,),
        in_specs=[pl.BlockSpec((tr, d), lambda i: (i, 0)), pl.BlockSpec((1, d), lambda i: (0, 0))],
        out_specs=[pl.BlockSpec((tr, d), lambda i: (i, 0)), pl.BlockSpec((tr, 1), lambda i: (i, 0))],
        out_shape=[jax.ShapeDtypeStruct((s, d), MXU), jax.ShapeDtypeStruct((s, 1), F32)],
        args=(x, g), name=name, comm=comm)
    return tuple(outs) if comm is None else (tuple(outs), couts)


def _loss_head(x, g, target, name):
    s, d = x.shape
    tr = _tile(s, 256)

    def body(x_ref, g_ref, t_ref, loss_ref, dx_ref, dg_ref):
        @pl.when(pl.program_id(0) == 0)
        def _():
            dg_ref[...] = jnp.zeros_like(dg_ref)
            loss_ref[...] = jnp.zeros_like(loss_ref)

        xf = x_ref[...]
        gv = g_ref[...]
        r = lax.rsqrt(jnp.mean(xf * xf, axis=-1, keepdims=True) + EPS)
        xhat = xf * r
        err = xhat * gv - t_ref[...]
        loss_ref[...] += 0.5 * jnp.sum(jnp.mean(err * err, axis=-1, keepdims=True), axis=0, keepdims=True)
        dy = err * (1.0 / d)
        dxhat = dy * gv
        mean = jnp.mean(dxhat * xhat, axis=-1, keepdims=True)
        dx_ref[...] = r * (dxhat - xhat * mean)
        dg_ref[...] += jnp.sum(dy * xhat, axis=0, keepdims=True)

    row = pl.BlockSpec((tr, d), lambda i: (i, 0))
    vec = pl.BlockSpec((1, d), lambda i: (0, 0))
    return pl.pallas_call(
        body,
        grid=(s // tr,),
        in_specs=[row, vec, row],
        out_specs=[pl.BlockSpec((1, LANES), lambda i: (0, 0)), row, vec],
        out_shape=[jax.ShapeDtypeStruct((1, LANES), F32), jax.ShapeDtypeStruct((s, d), F32),
                   jax.ShapeDtypeStruct((1, d), F32)],
        compiler_params=_cparams(),
        name=name,
    )(x, g, target)


def _mix_fwd(oa, ob, wa_t, wb_t, gates, name):
    s, kd = oa.shape
    d = wa_t.shape[0]
    tm, tn = _tile(s, 1024), _tile(d, 1024)
    nj = d // tn

    def body(oa_ref, ob_ref, wa_ref, wb_ref, ga_ref, gb_ref, out_ref):
        ya = _dot_nt(oa_ref[...], wa_ref[...])
        yb = _dot_nt(ob_ref[...], wb_ref[...])
        out_ref[...] = (_sigmoid(ga_ref[...].astype(F32)) * ya
                        + _sigmoid(gb_ref[...].astype(F32)) * yb).astype(out_ref.dtype)

    o_spec = pl.BlockSpec((tm, kd), lambda i, j: (i, 0))
    w_spec = pl.BlockSpec((tn, kd), lambda i, j: (j, 0))
    return pl.pallas_call(
        body,
        grid=(s // tm, nj),
        in_specs=[o_spec, o_spec, w_spec, w_spec,
                  pl.BlockSpec((tm, tn), lambda i, j: (i, j)),
                  pl.BlockSpec((tm, tn), lambda i, j: (i, j + nj))],
        out_specs=pl.BlockSpec((tm, tn), lambda i, j: (i, j)),
        out_shape=jax.ShapeDtypeStruct((s, d), MXU),
        compiler_params=_cparams(),
        name=name,
    )(oa, ob, wa_t, wb_t, gates, gates)


def _mix_bwd(dx, w_o, oa, ob, wa_t, wb_t, gates, name):
    s, kd = oa.shape
    d = wa_t.shape[0]
    tm = _tile(s, 512)

    def body(dx_ref, wo_ref, oa_ref, ob_ref, wa_ref, wb_ref, g_ref, dya_ref, dyb_ref, dg_ref):
        dm = _dot_nt(dx_ref[...], wo_ref[...])
        ya = _dot_nt(oa_ref[...], wa_ref[...])
        yb = _dot_nt(ob_ref[...], wb_ref[...])
        sa = _sigmoid(g_ref[:, :d].astype(F32))
        sb = _sigmoid(g_ref[:, d:].astype(F32))
        dya_ref[...] = (dm * sa).astype(dya_ref.dtype)
        dyb_ref[...] = (dm * sb).astype(dyb_ref.dtype)
        dg_ref[:, :d] = (dm * ya * sa * (1.0 - sa)).astype(dg_ref.dtype)
        dg_ref[:, d:] = (dm * yb * sb * (1.0 - sb)).astype(dg_ref.dtype)

    def rows(width):
        return pl.BlockSpec((tm, width), lambda i: (i, 0))

    def whole(arr):
        return pl.BlockSpec(arr.shape, lambda i: (0, 0))

    return pl.pallas_call(
        body,
        grid=(s // tm,),
        in_specs=[rows(d), whole(w_o), rows(kd), rows(kd), whole(wa_t), whole(wb_t), rows(2 * d)],
        out_specs=[rows(d), rows(d), rows(2 * d)],
        out_shape=[jax.ShapeDtypeStruct((s, d), MXU), jax.ShapeDtypeStruct((s, d), MXU),
                   jax.ShapeDtypeStruct((s, 2 * d), MXU)],
        compiler_params=_cparams(),
        name=name,
    )(dx, w_o, oa, ob, wa_t, wb_t, gates)


def _ple(p, w_pe_t, h, w_pg, other, *, backward, name, next_g=None):
    s, kp = p.shape
    d = w_pe_t.shape[0]
    tm, tn = _tile(s, 1024), _tile(d, 1024)
    with_norm = next_g is not None
    assert not (with_norm and (backward or tn != d))

    def body(p_ref, wpe_ref, h_ref, wpg_ref, other_ref, *rest):
        out_refs = rest[1:] if with_norm else rest
        pe = _dot_nt(p_ref[...].astype(MXU), wpe_ref[...])
        gt = _dot(h_ref[...], wpg_ref[...])
        sg = _sigmoid(gt)
        if backward:
            dout = other_ref[...]
            out_refs[0][...] = (dout * sg).astype(out_refs[0].dtype)
            out_refs[1][...] = (dout * pe * sg * (1.0 - sg)).astype(out_refs[1].dtype)
        elif with_norm:
            x_new, h_new, r_new = _residual_norm(pe * sg, other_ref[...], rest[0][...])
            out_refs[0][...] = x_new
            out_refs[1][...] = h_new.astype(out_refs[1].dtype)
            out_refs[2][...] = r_new
        else:
            out_refs[0][...] = other_ref[...] + pe * sg

    t_spec = pl.BlockSpec((tm, tn), lambda i, j: (i, j))
    if backward:
        out_specs, out_shape = [t_spec, t_spec], [jax.ShapeDtypeStruct((s, d), MXU)] * 2
    else:
        out_specs, out_shape = [t_spec], [jax.ShapeDtypeStruct((s, d), F32)]
    in_specs = [pl.BlockSpec((tm, kp), lambda i, j: (i, 0)),
                pl.BlockSpec((tn, kp), lambda i, j: (j, 0)),
                pl.BlockSpec((tm, d), lambda i, j: (i, 0)),
                pl.BlockSpec((d, tn), lambda i, j: (0, j)),
                t_spec]
    args = [p, w_pe_t, h, w_pg, other]
    if with_norm:
        in_specs.append(pl.BlockSpec((1, tn), lambda i, j: (0, j)))
        args.append(next_g)
        out_specs += [t_spec, pl.BlockSpec((tm, 1), lambda i, j: (i, 0))]
        out_shape += [jax.ShapeDtypeStruct((s, d), MXU), jax.ShapeDtypeStruct((s, 1), F32)]
    outs = pl.pallas_call(
        body,
        grid=(s // tm, d // tn),
        in_specs=in_specs,
        out_specs=out_specs,
        out_shape=out_shape,
        compiler_params=_cparams(),
        name=name,
    )(*args)
    return tuple(outs) if (backward or with_norm) else outs[0]


def _split_dot(x, tri):
    hi = x.astype(jnp.bfloat16)
    lo = (x - hi.astype(F32)).astype(jnp.bfloat16)
    return _dot(hi, tri) + _dot(lo, tri)


def _log_sigmoids(z):
    lb = jnp.minimum(z, 0.0) - jnp.log(1.0 + jnp.exp(-jnp.abs(z)))
    return lb, lb - z


def _head_lanes(hh):
    lane = lax.broadcasted_iota(jnp.int32, (1, LANES), 1)
    return jnp.logical_and(lane >= hh * HEAD_DIM, lane < (hh + 1) * HEAD_DIM)


def _sb_fwd(qkv, name, comm=None):
    s = qkv.shape[0]
    tq = _tile(s, SB_TQ)
    nsub = SB_SUB if (s // tq) % SB_SUB == 0 else 1

    def body(q_ref, k_ref, v_ref, o_ref):
        row = lax.broadcasted_iota(jnp.int32, (tq, tq), 0)
        col = lax.broadcasted_iota(jnp.int32, (tq, tq), 1)
        causal = col < row
        tri = jnp.where(row > col, 1.0, 0.0).astype(jnp.bfloat16)
        started = [_sb_fwd_straight(q_ref, k_ref, v_ref, pl.program_id(1) * nsub + sub, sub, tq, causal, tri)
                   for sub in range(nsub)]
        for sub, (block, i, cs, accs) in enumerate(started):
            def top(cs):
                return jnp.maximum(jnp.max(cs[0]), jnp.max(cs[1]))

            def live(st):
                return jnp.logical_and(st[0] >= 0, st[1] > SB_DEAD)

            def walk(st, block=block):
                cs, accs = block(st[0], st[2], st[3], False)
                return st[0] - 1, top(cs), cs, accs

            accs = lax.while_loop(live, walk, (i - 2, top(cs), cs, accs))[3]
            o_ref[sub * tq:(sub + 1) * tq, :] = jnp.where(_head_lanes(0), accs[0], accs[1]).astype(o_ref.dtype)

    outs, couts = _call(
        body,
        grid=(N_PAIR, s // (nsub * tq)),
        in_specs=[pl.BlockSpec((nsub * tq, LANES), lambda p, i: (i, p)),
                  pl.BlockSpec((s, LANES), lambda p, i: (0, N_PAIR + p)),
                  pl.BlockSpec((s, LANES), lambda p, i: (0, 2 * N_PAIR + p))],
        out_specs=[pl.BlockSpec((nsub * tq, LANES), lambda p, i: (i, p))],
        out_shape=[jax.ShapeDtypeStruct((s, SB_W), MXU)],
        args=(qkv, qkv, qkv), name=name, comm=comm)
    return outs[0] if comm is None else (outs[0], couts)


def _sb_fwd_straight(q_ref, k_ref, v_ref, i, sub, tq, causal, tri):
    qf = q_ref[sub * tq:(sub + 1) * tq, :].astype(F32) * SCALE
    qms = [jnp.where(_head_lanes(hh), qf, 0.0).astype(MXU) for hh in range(2)]

    def block(kb, cs, accs, masked, present=None):
        rows = pl.ds(pl.multiple_of(kb * tq, tq), tq)
        ks, vs = k_ref[rows, :], v_ref[rows, :]
        off = 0.0 if present is None else (1.0 - present) * NEG
        new_c, new_acc = [], []
        for hh in range(2):
            lb, lm = _log_sigmoids(_dot_nt(qms[hh], ks))
            if masked:
                lm = jnp.where(causal, lm, 0.0)
            a = jnp.exp(lb + _split_dot(lm, tri) + (cs[hh] + off))
            if masked:
                a = jnp.where(causal, a, 0.0)
            new_acc.append(accs[hh] + _dot(a.astype(MXU), vs))
            row_sum = jnp.sum(lm, axis=1, keepdims=True)
            new_c.append(cs[hh] + (row_sum if present is None else row_sum * present))
        return tuple(new_c), tuple(new_acc)

    zc, za = jnp.zeros((tq, 1), F32), jnp.zeros((tq, LANES), F32)
    cs, accs = block(i, (zc, zc), (za, za), True)
    cs, accs = block(jnp.maximum(i - 1, 0), cs, accs, False, jnp.where(i > 0, 1.0, 0.0))
    return block, i, cs, accs


def _sb_bwd(qkv, do, name, comm=None):
    s = qkv.shape[0]
    tq = _tile(s, SB_TQ)
    nq = s // tq
    nsub = SB_SUB if nq % SB_SUB == 0 else 1
    nsteps = nq // nsub

    def body(q_ref, k_ref, v_ref, do_ref, dq_ref, dk_ref, dv_ref, dk_acc, dv_acc, carries):
        step = pl.program_id(1)

        @pl.when(step == 0)
        def _():
            dk_acc[...] = jnp.zeros_like(dk_acc)
            dv_acc[...] = jnp.zeros_like(dv_acc)

        row = lax.broadcasted_iota(jnp.int32, (tq, tq), 0)
        col = lax.broadcasted_iota(jnp.int32, (tq, tq), 1)
        causal = col < row
        tri_rev = jnp.where(row > col, 1.0, 0.0).astype(jnp.bfloat16)
        tri_excl = jnp.where(row < col, 1.0, 0.0).astype(jnp.bfloat16)
        zc, za = jnp.zeros((tq, 1), F32), jnp.zeros((tq, LANES), F32)

        def top(cs):
            return jnp.maximum(jnp.max(cs[0]), jnp.max(cs[1]))

        def live(st):
            return jnp.logical_and(st[0] >= 0, st[1] > SB_DEAD)

        def row_sums(pre):
            return [jnp.sum(lm, axis=1, keepdims=True) for _, lm in pre]

        def query_block(sub):
            i = step * nsub + sub
            q_rows = slice(sub * tq, (sub + 1) * tq)
            qf = q_ref[q_rows, :].astype(F32) * SCALE
            dof = do_ref[q_rows, :]
            qms = [jnp.where(_head_lanes(hh), qf, 0.0).astype(MXU) for hh in range(2)]
            doms = [jnp.where(_head_lanes(hh), dof, jnp.zeros_like(dof)) for hh in range(2)]

            def terms(kb, masked):
                rows = pl.ds(pl.multiple_of(kb * tq, tq), tq)
                ks = k_ref[rows, :]
                out = []
                for hh in range(2):
                    lb, lm = _log_sigmoids(_dot_nt(qms[hh], ks))
                    if masked:
                        lm = jnp.where(causal, lm, 0.0)
                    out.append((lb, lm))
                return out

            def block(kb, cs, gpres, dqs, masked, gate=None, pre=None):
                rows = pl.ds(pl.multiple_of(kb * tq, tq), tq)
                ks, vs = k_ref[rows, :], v_ref[rows, :]
                pre = terms(kb, masked) if pre is None else pre
                new_g, new_dq = [], []
                dk_add, dv_add = None, None
                for hh in range(2):
                    lb, lm = pre[hh]
                    off = 0.0 if gate is None else (1.0 - gate) * NEG
                    a = jnp.exp(lb + _split_dot(lm, tri_rev) + (cs[hh] + off))
                    if masked:
                        a = jnp.where(causal, a, 0.0)
                    g = a * _dot_nt(doms[hh], vs)
                    gsum = gpres[hh] + _split_dot(g, tri_excl)
                    dz = g - (g + gsum) * jnp.exp(lb)
                    if masked:
                        dz = jnp.where(causal, dz, 0.0)
                    dzb = dz.astype(MXU)
                    new_dq.append(dqs[hh] + _dot(dzb, ks))
                    dk_h = _dot_tn(dzb, qms[hh])
                    dv_h = _dot_tn(a.astype(MXU), doms[hh])
                    dk_add = dk_h if dk_add is None else dk_add + dk_h
                    dv_add = dv_h if dv_add is None else dv_add + dv_h
                    new_g.append(gpres[hh] + jnp.sum(g, axis=1, keepdims=True))
                dk_acc[rows, :] += dk_add
                dv_acc[rows, :] += dv_add
                return tuple(new_g), tuple(new_dq)

            prev = jnp.maximum(i - 1, 0)
            gate = jnp.where(i > 0, 1.0, 0.0)
            t_diag, t_prev = terms(i, True), terms(prev, False)
            c_diag = row_sums(t_diag)
            sums = row_sums(t_prev)
            c_prev = tuple(c_diag[hh] + sums[hh] * gate for hh in range(2))
            return dict(i=i, prev=prev, gate=gate, q_rows=q_rows, terms=terms, block=block,
                        t_diag=t_diag, t_prev=t_prev, c_diag=c_diag, c_prev=c_prev)

        blocks = [query_block(sub) for sub in range(nsub)]
        for qb in blocks:
            def record(st, qb=qb):
                kb, cs = st[0], st[2]
                sums = row_sums(qb["terms"](kb, False))
                for hh in range(2):
                    carries[hh, kb] = cs[hh]
                cs = tuple(cs[hh] + sums[hh] for hh in range(2))
                return kb - 1, top(cs), cs

            first = lax.while_loop(live, record, (qb["i"] - 2, top(qb["c_prev"]), qb["c_prev"]))[0] + 1
            qb["mid"] = lax.fori_loop(
                first, qb["i"] - 1,
                lambda kb, cr, qb=qb: qb["block"](kb, (carries[0, kb], carries[1, kb]), cr[0], cr[1], False),
                ((zc, zc), (za, za)))
        for qb in blocks:
            gpres, dqs = qb["block"](qb["prev"], qb["c_diag"], *qb["mid"], False, qb["gate"], qb["t_prev"])
            dqs = qb["block"](qb["i"], (zc, zc), gpres, dqs, True, None, qb["t_diag"])[1]
            dq_ref[qb["q_rows"], :] = (jnp.where(_head_lanes(0), dqs[0], dqs[1]) * SCALE).astype(dq_ref.dtype)

        @pl.when(step == nsteps - 1)
        def _():
            dk_ref[...] = dk_acc[...].astype(dk_ref.dtype)
            dv_ref[...] = dv_acc[...].astype(dv_ref.dtype)

    blk = pl.BlockSpec((nsub * tq, LANES), lambda p, i: (i, p))
    full = pl.BlockSpec((s, LANES), lambda p, i: (0, p))
    outs, couts = _call(
        body,
        grid=(N_PAIR, nsteps),
        in_specs=[blk,
                  pl.BlockSpec((s, LANES), lambda p, i: (0, N_PAIR + p)),
                  pl.BlockSpec((s, LANES), lambda p, i: (0, 2 * N_PAIR + p)),
                  blk],
        out_specs=[blk, full, full],
        out_shape=[jax.ShapeDtypeStruct((s, SB_W), MXU)] * 3,
        scratch_shapes=[pltpu.VMEM((s, LANES), F32), pltpu.VMEM((s, LANES), F32),
                        pltpu.VMEM((2, nq, tq, 1), F32)],
        args=(qkv, qkv, qkv, do), name=name, comm=comm)
    return tuple(outs) if comm is None else (tuple(outs), couts)


def _bucket_table():
    i = np.arange(BLOCK)[:, None]
    j = np.arange(2 * BLOCK)[None, :]
    d = np.maximum(BLOCK + i - j, 0)
    max_exact = N_BUCKETS // 2
    df = np.maximum(d, 1).astype(np.float32)
    large = max_exact + (np.log(df / max_exact) / math.log(MAX_DISTANCE / max_exact)
                         * (N_BUCKETS - max_exact)).astype(np.int32)
    large = np.minimum(large, N_BUCKETS - 1)
    return np.where(d < max_exact, d, large).astype(np.int32)


def _build_bias(rel_bias, buckets, name):
    def body(rb_ref, bk_ref, out_ref):
        h = pl.program_id(0)
        bk = bk_ref[...]
        acc = jnp.zeros(bk.shape, F32)
        for b in range(N_BUCKETS):
            acc = jnp.where(bk == b, rb_ref[b, h], acc)
        out_ref[...] = acc

    return pl.pallas_call(
        body,
        grid=(SW_HEADS,),
        in_specs=[pl.BlockSpec(memory_space=pltpu.SMEM),
                  pl.BlockSpec((BLOCK, 2 * BLOCK), lambda h: (0, 0))],
        out_specs=pl.BlockSpec((None, BLOCK, 2 * BLOCK), lambda h: (h, 0, 0)),
        out_shape=jax.ShapeDtypeStruct((SW_HEADS, BLOCK, 2 * BLOCK), F32),
        name=name,
    )(rel_bias, buckets)


def _bias_grad(dbias_layers, buckets, name):
    n_l = len(dbias_layers)

    def body(*refs):
        bk = refs[n_l][...]
        out_ref = refs[n_l + 1]
        db = refs[0][...]
        for r in refs[1:n_l]:
            db = db + r[...]
        lane = lax.broadcasted_iota(jnp.int32, (1, LANES), 1)
        acc = jnp.zeros((1, LANES), F32)
        for b in range(N_BUCKETS):
            part = jnp.sum(jnp.where(bk == b, db, 0.0), axis=1, keepdims=True)
            tot = jnp.sum(part, axis=0, keepdims=True)
            acc = jnp.where(lane == b, tot, acc)
        out_ref[...] = acc

    hspec = pl.BlockSpec((None, BLOCK, 2 * BLOCK), lambda h: (h, 0, 0))
    return pl.pallas_call(
        body,
        grid=(SW_HEADS,),
        in_specs=[hspec] * n_l + [pl.BlockSpec((BLOCK, 2 * BLOCK), lambda h: (0, 0))],
        out_specs=pl.BlockSpec((None, 1, LANES), lambda h: (h, 0, 0)),
        out_shape=jax.ShapeDtypeStruct((SW_HEADS, 1, LANES), F32),
        name=name,
    )(*dbias_layers, buckets)


GROUP_ROWS = SW_GROUP * BLOCK


def _group_lanes(g):
    lane = lax.broadcasted_iota(jnp.int32, (1, LANES), 1)
    gvec = jnp.zeros((1, LANES), jnp.int32) + g
    return jnp.where(lane >= HEAD_DIM, 1, 0) == gvec, gvec


def _stack_heads(x, g):
    kv_lanes, gvec = _group_lanes(g)
    parts = []
    for j in range(SW_GROUP):
        half = x[:, (j // 2) * LANES:(j // 2 + 1) * LANES]
        moved = jnp.where(gvec == j % 2, half, pltpu.roll(half, HEAD_DIM, 1))
        parts.append(jnp.where(kv_lanes, moved, 0.0))
    return jnp.concatenate(parts, axis=0)


def _unstack_heads(y, g):
    _, gvec = _group_lanes(g)
    heads = []
    for j in range(SW_GROUP):
        yj = y[j * BLOCK:(j + 1) * BLOCK]
        heads.append(jnp.where(gvec == j % 2, yj, pltpu.roll(yj, HEAD_DIM, 1)))
    pairs = [jnp.where(_head_lanes(0), heads[2 * p], heads[2 * p + 1]) for p in range(SW_GROUP // 2)]
    return jnp.concatenate(pairs, axis=1)


def _per_head_col(values):
    return jnp.concatenate([jnp.zeros((BLOCK, 1), F32) + v for v in values], axis=0)


def _swa_scores(qs, kp, kc, bias_ref, n):
    row = jnp.bitwise_and(lax.broadcasted_iota(jnp.int32, (GROUP_ROWS, BLOCK), 0), BLOCK - 1)
    col = lax.broadcasted_iota(jnp.int32, (GROUP_ROWS, BLOCK), 1)
    bias = bias_ref[...].reshape(GROUP_ROWS, 2 * BLOCK)
    s1 = _dot_nt(qs, kp) + bias[:, :BLOCK]
    s2 = _dot_nt(qs, kc) + bias[:, BLOCK:]
    no_prev = jnp.where(n > 0, 0, BLOCK)
    s1 = jnp.where(col > row + no_prev, s1, NEG)
    s2 = jnp.where(col <= row, s2, NEG)
    return s1, s2


def _swa_specs(s):
    q_blk = 3 * SB_W // (2 * LANES)
    k_blk = (3 * SB_W + SW_W) // LANES
    return (pl.BlockSpec((s, 2 * LANES), lambda g: (0, q_blk + g)),
            pl.BlockSpec((s, LANES), lambda g: (0, k_blk)),
            pl.BlockSpec((s, LANES), lambda g: (0, k_blk + 1)))


def _swa_fwd(qkv, bias, sinks, name, comm=None):
    s = qkv.shape[0]
    nb = s // BLOCK

    def body(sink_ref, q_ref, k_ref, v_ref, bias_ref, o_ref, lse_ref):
        g = pl.program_id(0)
        sink = _per_head_col([sink_ref[SW_GROUP * g + j] for j in range(SW_GROUP)])
        lane = lax.broadcasted_iota(jnp.int32, (1, LANES), 1)

        def step(n, carry):
            r0 = pl.multiple_of(n * BLOCK, BLOCK)
            p0 = pl.multiple_of(jnp.maximum(n - 1, 0) * BLOCK, BLOCK)
            cur, prev = pl.ds(r0, BLOCK), pl.ds(p0, BLOCK)
            qs = _stack_heads(q_ref[cur, :].astype(F32) * SCALE, g).astype(MXU)
            s1, s2 = _swa_scores(qs, k_ref[prev, :], k_ref[cur, :], bias_ref, n)
            m = jnp.maximum(jnp.max(jnp.maximum(s1, s2), axis=1, keepdims=True), sink)
            e1 = jnp.exp(s1 - m)
            e2 = jnp.exp(s2 - m)
            den = jnp.sum(e1 + e2, axis=1, keepdims=True) + jnp.exp(sink - m)
            o = _dot((e1 / den).astype(MXU), v_ref[prev, :]) + _dot((e2 / den).astype(MXU), v_ref[cur, :])
            o_ref[cur, :] = _unstack_heads(o, g).astype(o_ref.dtype)
            lse = m + jnp.log(den)
            lse_row = jnp.zeros((BLOCK, LANES), F32)
            for j in range(SW_GROUP):
                lse_row = jnp.where(lane == j, lse[j * BLOCK:(j + 1) * BLOCK], lse_row)
            lse_ref[cur, :] = lse_row
            return carry

        lax.fori_loop(0, nb, step, 0, unroll=2)

    outs, couts = _call(
        body,
        grid=(SW_KV,),
        in_specs=[pl.BlockSpec(memory_space=pltpu.SMEM), *_swa_specs(s),
                  pl.BlockSpec((SW_GROUP, BLOCK, 2 * BLOCK), lambda g: (g, 0, 0))],
        out_specs=[pl.BlockSpec((s, 2 * LANES), lambda g: (0, g)),
                   pl.BlockSpec((None, s, LANES), lambda g: (g, 0, 0))],
        out_shape=[jax.ShapeDtypeStruct((s, SW_W), MXU), jax.ShapeDtypeStruct((SW_KV, s, LANES), F32)],
        args=(sinks, qkv, qkv, qkv, bias), name=name, comm=comm)
    return tuple(outs) if comm is None else (tuple(outs), couts)


def _swa_bwd(qkv, bias, sinks, do, lse, name, comm=None):
    s = qkv.shape[0]
    nb = s // BLOCK

    def body(sink_ref, q_ref, k_ref, v_ref, bias_ref, do_ref, lse_ref,
             dq_ref, dk_ref, dv_ref, dbias_ref, dsink_ref, dk_acc, dv_acc):
        g = pl.program_id(0)
        sink = _per_head_col([sink_ref[SW_GROUP * g + j] for j in range(SW_GROUP)])
        lane = lax.broadcasted_iota(jnp.int32, (1, LANES), 1)

        @pl.when(g == 0)
        def _():
            dk_acc[...] = jnp.zeros_like(dk_acc)
            dv_acc[...] = jnp.zeros_like(dv_acc)

        dbias_ref[...] = jnp.zeros_like(dbias_ref)

        def step(n, dsink_rows):
            r0 = pl.multiple_of(n * BLOCK, BLOCK)
            p0 = pl.multiple_of(jnp.maximum(n - 1, 0) * BLOCK, BLOCK)
            cur, prev = pl.ds(r0, BLOCK), pl.ds(p0, BLOCK)
            qs = _stack_heads(q_ref[cur, :].astype(F32) * SCALE, g).astype(MXU)
            dos = _stack_heads(do_ref[cur, :].astype(F32), g).astype(MXU)
            kp, kc, vp, vc = k_ref[prev, :], k_ref[cur, :], v_ref[prev, :], v_ref[cur, :]
            lse_row = lse_ref[cur, :]
            lse = jnp.concatenate([jnp.sum(jnp.where(lane == j, lse_row, 0.0), axis=1, keepdims=True)
                                   for j in range(SW_GROUP)], axis=0)
            s1, s2 = _swa_scores(qs, kp, kc, bias_ref, n)
            pr1 = jnp.exp(s1 - lse)
            pr2 = jnp.exp(s2 - lse)
            dpr1 = _dot_nt(dos, vp)
            dpr2 = _dot_nt(dos, vc)
            delta = jnp.sum(pr1 * dpr1 + pr2 * dpr2, axis=1, keepdims=True)
            ds1 = pr1 * (dpr1 - delta)
            ds2 = pr2 * (dpr2 - delta)
            dbias_ref[:, :, :BLOCK] += ds1.reshape(SW_GROUP, BLOCK, BLOCK)
            dbias_ref[:, :, BLOCK:] += ds2.reshape(SW_GROUP, BLOCK, BLOCK)
            ds1b, ds2b = ds1.astype(MXU), ds2.astype(MXU)
            dq = _dot(ds1b, kp) + _dot(ds2b, kc)
            dq_ref[cur, :] = (_unstack_heads(dq, g) * SCALE).astype(dq_ref.dtype)
            dk_acc[prev, :] += _dot_tn(ds1b, qs)
            dk_acc[cur, :] += _dot_tn(ds2b, qs)
            dv_acc[prev, :] += _dot_tn(pr1.astype(MXU), dos)
            dv_acc[cur, :] += _dot_tn(pr2.astype(MXU), dos)
            return dsink_rows - jnp.exp(sink - lse) * delta

        rows = lax.fori_loop(0, nb, step, jnp.zeros((GROUP_ROWS, 1), F32), unroll=2)
        for j in range(SW_GROUP):
            dsink_ref[j] = jnp.broadcast_to(jnp.sum(rows[j * BLOCK:(j + 1) * BLOCK], axis=0, keepdims=True),
                                            (1, LANES))

        @pl.when(g == SW_KV - 1)
        def _():
            dk_ref[...] = dk_acc[...].astype(dk_ref.dtype)
            dv_ref[...] = dv_acc[...].astype(dv_ref.dtype)

    grp = pl.BlockSpec((s, 2 * LANES), lambda g: (0, g))
    kv_out = pl.BlockSpec((s, LANES), lambda g: (0, 0))
    bspec = pl.BlockSpec((SW_GROUP, BLOCK, 2 * BLOCK), lambda g: (g, 0, 0))
    outs, couts = _call(
        body,
        grid=(SW_KV,),
        in_specs=[pl.BlockSpec(memory_space=pltpu.SMEM), *_swa_specs(s), bspec, grp,
                  pl.BlockSpec((None, s, LANES), lambda g: (g, 0, 0))],
        out_specs=[grp, kv_out, kv_out, bspec, pl.BlockSpec((SW_GROUP, 1, LANES), lambda g: (g, 0, 0))],
        out_shape=[jax.ShapeDtypeStruct((s, SW_W), MXU),
                   jax.ShapeDtypeStruct((s, LANES), MXU),
                   jax.ShapeDtypeStruct((s, LANES), MXU),
                   jax.ShapeDtypeStruct((SW_HEADS, BLOCK, 2 * BLOCK), F32),
                   jax.ShapeDtypeStruct((SW_HEADS, 1, LANES), F32)],
        scratch_shapes=[pltpu.VMEM((s, LANES), F32), pltpu.VMEM((s, LANES), F32)],
        args=(sinks, qkv, qkv, qkv, bias, do, lse), name=name, comm=comm)
    return tuple(outs) if comm is None else (tuple(outs), couts)


class _NoPlan:
    def comm(self, name):
        return None

    def done(self, name, outs):
        pass

    def grad(self, layer, name, value):
        pass


def _run(plan, fn, *args, name, **kw):
    comm = plan.comm(name)
    if comm is None:
        return fn(*args, name=name, **kw)
    res, outs = fn(*args, name=name, comm=comm, **kw)
    plan.done(name, outs)
    return res


def _norm_bwd(dh, x, dres, r, g):
    xhat = x * r
    dxhat = dh * g
    dx = dres + r * (dxhat - xhat * jnp.mean(dxhat * xhat, axis=-1, keepdims=True))
    return dx, dx, jnp.sum(dh * xhat, axis=0, keepdims=True)


def _residual_norm(acc, res, g):
    x = res + acc
    r = lax.rsqrt(jnp.mean(x * x, axis=-1, keepdims=True) + EPS)
    return x, (x * r) * g, r


def _layer_fwd(x, p, w, g_mix, g_mlp, g_pe, sinks, bias, tag, plan, normed_x=None, next_g=None):
    h1, r1 = normed_x if normed_x is not None else _run(plan, _rms_fwd, x, g_mix, name=f"rms_mix_{tag}")
    qkv = _run(plan, _mm, h1, w["w_qkv_t"], tb=True, out_dtypes=(MXU,), tn=QKV_COLS, name=f"proj_qkv_{tag}")
    gates = _run(plan, _mm, h1, w["w_gate_t"], tb=True, out_dtypes=(MXU,), tn=2048, name=f"proj_gate_{tag}")
    oa = _run(plan, _sb_fwd, qkv, name=f"sb_fwd_{tag}")
    ob, lse = _run(plan, _swa_fwd, qkv, bias, sinks, name=f"swa_fwd_{tag}")
    merged = _mix_fwd(oa, ob, w["w_up_a_t"], w["w_up_b_t"], gates, f"mix_fwd_{tag}")
    d = x.shape[1]
    normed = (F32, MXU, (F32, "col"))
    x1, h2, r2 = _run(plan, _mm, merged, w["w_o"], extras=(x, g_mlp), epi=_residual_norm, out_dtypes=normed,
                      tn=d, name=f"out_proj_{tag}")
    u, act = _run(plan, _mm, h2, w["w_ff1_t"], tb=True,
                  epi=lambda acc: (acc, jnp.square(jnp.maximum(acc, 0.0))),
                  out_dtypes=(MXU, MXU), tn=2048, name=f"ff1_{tag}")
    x2, h3, r3 = _run(plan, _mm, act, w["w_ff2"], extras=(x1, g_pe), epi=_residual_norm, out_dtypes=normed,
                      tn=d, name=f"ff2_{tag}")
    x3 = _ple(p, w["w_pe_t"], h3, w["w_pg"], x2, backward=False, name=f"ple_fwd_{tag}", next_g=next_g)
    next_normed = None
    if next_g is not None:
        x3, next_normed = x3[0], (x3[1], x3[2])
    saved = dict(x=x, h1=h1, r1=r1, gates=gates, qkv=qkv, lse=lse, oa=oa, ob=ob, merged=merged,
                 x1=x1, h2=h2, r2=r2, u=u, act=act, x2=x2, h3=h3, r3=r3)
    return x3, saved, next_normed


def _layer_bwd(dx3, sv, p, w, g_mix, g_mlp, g_pe, sinks, bias, layer, plan):
    tag = f"l{layer}"
    gw = {}
    wire = (WIRE,)

    def dw(name, a, b):
        gw[name] = _run(plan, _mm, a, b, ta=True, out_dtypes=wire, tk=2048, name=f"d{name}_{tag}")
        plan.grad(layer, name, gw[name])

    dpe, dgt = _ple(p, w["w_pe_t"], sv["h3"], w["w_pg"], dx3, backward=True, name=f"ple_bwd_{tag}")
    dw("w_pe", dpe, p)
    dw("w_pg", sv["h3"], dgt)
    d = dx3.shape[1]
    grads = (F32, MXU, (F32, "rowsum"))
    dx2, dx2b, dg_pe = _run(plan, _mm, dgt, w["w_pg"], tb=True, extras=(sv["x2"], dx3, sv["r3"], g_pe),
                            epi=_norm_bwd, out_dtypes=grads, tm=512, tn=d, name=f"dh_pe_{tag}")
    dw("w_ff2", sv["act"], dx2b)
    du = _run(plan, _mm, dx2b, w["w_ff2"], tb=True, extras=(sv["u"],),
              epi=lambda acc, u: acc * (2.0 * jnp.maximum(u.astype(F32), 0.0)), out_dtypes=(MXU,),
              tn=2048, name=f"dact_{tag}")
    dw("w_ff1", du, sv["h2"])
    dx1, dx1b, dg_mlp = _run(plan, _mm, du, w["w_ff1_t"], extras=(sv["x1"], dx2, sv["r2"], g_mlp),
                             epi=_norm_bwd, out_dtypes=grads, tm=1024, tn=d, name=f"dh_mlp_{tag}")
    dw("w_o", sv["merged"], dx1b)
    dya, dyb, dgates = _mix_bwd(dx1b, w["w_o"], sv["oa"], sv["ob"], w["w_up_a_t"], w["w_up_b_t"],
                                sv["gates"], f"mix_bwd_{tag}")
    dw("w_up_a", dya, sv["oa"])
    dw("w_up_b", dyb, sv["ob"])
    doa = _run(plan, _mm, dya, w["w_up_a_t"], out_dtypes=(MXU,), name=f"do_a_{tag}")
    dob = _run(plan, _mm, dyb, w["w_up_b_t"], out_dtypes=(MXU,), name=f"do_b_{tag}")
    dqb, dkb, dvb, dbias, dsink = _run(plan, _swa_bwd, sv["qkv"], bias, sinks, dob, sv["lse"],
                                       name=f"swa_bwd_{tag}")
    dqa, dka, dva = _run(plan, _sb_bwd, sv["qkv"], doa, name=f"sb_bwd_{tag}")
    dqkv = jnp.concatenate([dqa, dka, dva, dqb, dkb, dvb], axis=1)
    gw_qkv = _mm(dqkv, sv["h1"], ta=True, out_dtypes=wire, tk=2048, name=f"dw_qkv_{tag}")
    gw_gate = _mm(dgates, sv["h1"], ta=True, out_dtypes=wire, tk=2048, name=f"dw_gate_{tag}")
    gw["w_in"] = jnp.concatenate([gw_qkv, gw_gate], axis=0)
    plan.grad(layer, "w_in", gw["w_in"])
    dh1 = _run(plan, _mm, dgates, w["w_gate_t"], name=f"dh_gate_{tag}")
    dx, _, dg_mix = _run(plan, _mm, dqkv, w["w_qkv_t"], tk=768, extras=(dh1, sv["x"], dx1, sv["r1"], g_mix),
                         epi=lambda acc, prev, *rest: _norm_bwd(acc + prev, *rest), out_dtypes=grads,
                         tm=512, tn=d, name=f"dh_qkv_{tag}")
    small = dict(g_mix=dg_mix, g_mlp=dg_mlp, g_pe=dg_pe, sinks=dsink[:, 0, 0], dbias=dbias)
    return dx, gw, small


def _local_step(x, p, target, weights, g_mix, g_mlp, g_pe, g_final, sinks, rel_bias, plan=None):
    plan = _NoPlan() if plan is None else plan
    depth = g_mix.shape[0]
    buckets = jnp.asarray(_bucket_table())
    bias = _build_bias(rel_bias, buckets, "build_bias")
    saved, wfull = [], []
    h, normed = x, None
    for l in range(depth):
        wfull.append(weights(l))
        next_g = g_mix[l + 1:l + 2] if l + 1 < depth else None
        h, sv, normed = _layer_fwd(h, p[l], wfull[l], g_mix[l:l + 1], g_mlp[l:l + 1], g_pe[l:l + 1],
                                   sinks[l], bias, f"l{l}", plan, normed, next_g)
        saved.append(sv)
    loss_row, dx, dg_final = _loss_head(h, g_final[None, :], target, "loss_head")
    gws = [None] * depth
    smalls = [None] * depth
    for l in reversed(range(depth)):
        dx, gws[l], smalls[l] = _layer_bwd(dx, saved[l], p[l], wfull[l], g_mix[l:l + 1], g_mlp[l:l + 1],
                                           g_pe[l:l + 1], sinks[l], bias, l, plan)
    drel = _bias_grad([sm["dbias"] for sm in smalls], buckets, "bias_grad")[:, 0, :N_BUCKETS].T
    small = dict(
        g_mix=jnp.concatenate([sm["g_mix"] for sm in smalls], axis=0),
        g_mlp=jnp.concatenate([sm["g_mlp"] for sm in smalls], axis=0),
        g_pe=jnp.concatenate([sm["g_pe"] for sm in smalls], axis=0),
        g_final=dg_final[0],
        sinks=jnp.stack([sm["sinks"] for sm in smalls], axis=0),
        rel_bias=drel,
    )
    return loss_row, dx, gws, small


MESH_ID = pl.DeviceIdType.MESH


def _position():
    return lax.axis_index("x"), lax.axis_index("y"), lax.axis_index("c")


def _gather_comm(shards):
    n = len(shards)

    def copies(pos, x_refs, out_refs, sems):
        send_sems, recv_sems, local_sems = sems
        x, y, c = pos
        me, sibling = (x, y, c), (x, y, 1 - c)
        chips = [(1 - x, y), (x, 1 - y), (1 - x, 1 - y)]

        def slot(a, px, py, pc):
            return out_refs[a].at[4 * px + 2 * py + pc]

        def copy(a, k, block, to, src=None):
            return pltpu.make_async_remote_copy(
                src_ref=slot(a, *block) if src is None else src, dst_ref=slot(a, *block),
                send_sem=send_sems.at[a, k], recv_sem=recv_sems.at[a, k],
                device_id=to, device_id_type=MESH_ID)

        mine = [pltpu.make_async_copy(x_refs[a], slot(a, *me), local_sems.at[a]) for a in range(n)]
        first = []
        for a in range(n):
            first.append(copy(a, 0, me, sibling, src=x_refs[a]))
            first += [copy(a, 1 + j, me, (*chip, c), src=x_refs[a]) for j, chip in enumerate(chips)]
        return me, sibling, chips, copy, mine, first

    def start(pos, x_refs, out_refs, sems):
        _, _, _, _, mine, first = copies(pos, x_refs, out_refs, sems)
        for cp in mine + first:
            cp.start()

    def finish(pos, x_refs, out_refs, sems):
        me, sibling, chips, copy, mine, first = copies(pos, x_refs, out_refs, sems)
        c = pos[2]
        passed = []
        for j, chip in enumerate(chips):
            for a in range(n):
                copy(a, 1 + j, (*chip, c), me).wait_recv()
                fwd = copy(a, 4 + j, (*chip, c), sibling)
                fwd.start()
                passed.append(fwd)
        for a in range(n):
            copy(a, 0, sibling, me).wait_recv()
            for j, chip in enumerate(chips):
                copy(a, 4 + j, (*chip, 1 - c), me).wait_recv()
        for cp in first + passed:
            cp.wait_send()
        for cp in mine:
            cp.wait()

    return _Comm(shards, [jax.ShapeDtypeStruct((N_DEV,) + s.shape, s.dtype) for s in shards],
                 [pltpu.SemaphoreType.DMA((n, 7)), pltpu.SemaphoreType.DMA((n, 7)),
                  pltpu.SemaphoreType.DMA((n,))], start, finish)


def _exchange_comm(arrays, n_slots, route):
    n = len(arrays)

    def copies(pos, in_refs, out_refs, sems):
        send_sems, recv_sems = sems
        out = []
        for a in range(n):
            for j in range(n_slots):
                src_slot, peer = route(pos, j)
                out.append(pltpu.make_async_remote_copy(
                    src_ref=in_refs[a].at[src_slot], dst_ref=out_refs[a].at[j],
                    send_sem=send_sems.at[a, j], recv_sem=recv_sems.at[a, j],
                    device_id=peer, device_id_type=MESH_ID))
        return out

    def start(pos, in_refs, out_refs, sems):
        for cp in copies(pos, in_refs, out_refs, sems):
            cp.start()

    def finish(pos, in_refs, out_refs, sems):
        for cp in copies(pos, in_refs, out_refs, sems):
            cp.wait()

    return _Comm(arrays, [jax.ShapeDtypeStruct((n_slots,) + g.shape[1:], g.dtype) for g in arrays],
                 [pltpu.SemaphoreType.DMA((n, n_slots)), pltpu.SemaphoreType.DMA((n, n_slots))], start, finish)


def _rs_sibling_comm(gs):
    return _exchange_comm(gs, 4, lambda pos, j: (2 * j + (1 - pos[2]), (pos[0], pos[1], 1 - pos[2])))


def _chip_of(k, x, y):
    return x ^ ((k + 1) & 1), y ^ (((k + 1) >> 1) & 1)


def _chip_partials(pos, gs, recvs, name):
    n = len(gs)

    def body(pos_ref, *refs):
        for a in range(n):
            refs[2 * n + a][...] = (refs[a][...].astype(F32) + refs[n + a][...].astype(F32)
                                    ).astype(refs[2 * n + a].dtype)

    def g_map(k, pos_ref):
        cx, cy = _chip_of(k, pos_ref[0], pos_ref[1])
        return (4 * cx + 2 * cy + pos_ref[2], 0, 0)

    def r_map(k, pos_ref):
        cx, cy = _chip_of(k, pos_ref[0], pos_ref[1])
        return (2 * cx + cy, 0, 0)

    slab = [(None,) + g.shape[1:] for g in gs]
    return pl.pallas_call(
        body,
        grid_spec=pltpu.PrefetchScalarGridSpec(
            num_scalar_prefetch=1,
            grid=(4,),
            in_specs=[pl.BlockSpec(sh, g_map) for sh in slab] + [pl.BlockSpec(sh, r_map) for sh in slab],
            out_specs=[pl.BlockSpec(sh, lambda k, pos_ref: (k, 0, 0)) for sh in slab],
        ),
        out_shape=[jax.ShapeDtypeStruct((4,) + g.shape[1:], g.dtype) for g in gs],
        compiler_params=_cparams(),
        name=name,
    )(pos, *gs, *recvs)


def _rs_chips_comm(parts):
    return _exchange_comm(parts, 3, lambda pos, k: (k, (*_chip_of(k, pos[0], pos[1]), pos[2])))


def _adamw_math(w, g, m, v):
    m = ADAM_B1 * m + (1.0 - ADAM_B1) * g
    v = ADAM_B2 * v + (1.0 - ADAM_B2) * (g * g)
    m_hat = m / (1.0 - ADAM_B1 ** ADAM_STEP)
    v_hat = v / (1.0 - ADAM_B2 ** ADAM_STEP)
    delta = -ADAM_LR * (m_hat / (jnp.sqrt(v_hat) + ADAM_EPS) + ADAM_WD * w)
    return delta, m, v


def _adamw_weight(parts, recvs, w, m, v, name, grad_t=False, comm=None):
    depth, a, b = w.shape
    ta = _tile(a, 288, unit=LANES if grad_t else 16)
    ni = a // ta
    g_block = (b, ta) if grad_t else (ta, b)

    def body(*refs):
        p_refs, r_refs = refs[:depth], refs[depth:2 * depth]
        w_ref, m_ref, v_ref = refs[2 * depth:2 * depth + 3]
        g_out, d_out, m_out, v_out = refs[2 * depth + 3:]
        layer = pl.program_id(0)
        g = jnp.zeros(g_block, F32)
        for l in range(depth):
            gl = p_refs[l][...].astype(F32)
            for k in range(3):
                gl = gl + r_refs[l][k].astype(F32)
            g = jnp.where(layer == l, gl, g)
        if grad_t:
            g = g.T
        delta, m_new, v_new = _adamw_math(w_ref[...], g, m_ref[...], v_ref[...])
        g_out[...] = g
        d_out[...] = delta
        m_out[...] = m_new
        v_out[...] = v_new

    def hold(l):
        return lambda layer, i: jnp.where(layer == l, i, jnp.where(layer < l, 0, ni - 1))

    def g_index(slot, f):
        if grad_t:
            return lambda layer, i: (slot, 0, f(layer, i))
        return lambda layer, i: (slot, f(layer, i), 0)

    p_specs = [pl.BlockSpec((None,) + g_block, g_index(3, hold(l))) for l in range(depth)]
    r_specs = [pl.BlockSpec((3,) + g_block, g_index(0, hold(l))) for l in range(depth)]
    row = pl.BlockSpec((None, ta, b), lambda layer, i: (layer, i, 0))
    outs, couts = _call(
        body,
        grid=(depth, ni),
        in_specs=p_specs + r_specs + [row, row, row],
        out_specs=[row] * 4,
        out_shape=[jax.ShapeDtypeStruct(w.shape, F32)] * 4,
        args=(*parts, *recvs, w, m, v), name=name, comm=comm)
    return outs if comm is None else (outs, couts)


def _adamw_replicated(gathered, w, m, v, name):
    r, lanes = w.shape

    def body(g_ref, w_ref, m_ref, v_ref, g_out, d_out, m_out, v_out):
        g = g_ref[0]
        for k in range(1, N_DEV):
            g = g + g_ref[k]
        delta, m_new, v_new = _adamw_math(w_ref[...], g, m_ref[...], v_ref[...])
        g_out[...] = g
        d_out[...] = delta
        m_out[...] = m_new
        v_out[...] = v_new

    return pl.pallas_call(
        body,
        out_shape=[jax.ShapeDtypeStruct((r, lanes), F32)] * 4,
        name=name,
    )(gathered, w, m, v)


def _wire_shard(name, shard):
    return (shard.T if name in COL_SHARDED else shard).astype(WIRE)


def _full_weight(gathered):
    return gathered.reshape(N_DEV * gathered.shape[1], gathered.shape[2])


def _to_slabs(gfull):
    return gfull.reshape(N_DEV, gfull.shape[0] // N_DEV, gfull.shape[1])


def _pack_small(arrs):
    rows = []
    for a in arrs:
        flat = a.astype(F32).reshape(-1)
        pad = (-flat.shape[0]) % LANES
        rows.append(jnp.pad(flat, (0, pad)).reshape(-1, LANES))
    packed = jnp.concatenate(rows, axis=0)
    return jnp.pad(packed, ((0, (-packed.shape[0]) % 8), (0, 0)))


def _unpack_small(packed, shapes):
    out, off = [], 0
    for shp in shapes:
        n = math.prod(shp)
        rows = -(-n // LANES)
        out.append(packed[off:off + rows].reshape(-1)[:n].reshape(shp))
        off += rows
    return out


def _of(layer, *names):
    return tuple((layer, n) for n in names)


MLP_W = ("w_ff1", "w_ff2", "w_pe", "w_pg")

GATHERS = (
    ("rms_mix_l0", _of(0, "w_in")),
    ("proj_qkv_l0", _of(0, "w_up_a", "w_up_b", "w_o")),
    ("proj_gate_l0", _of(0, "w_pe", "w_pg")),
    ("sb_fwd_l0", _of(0, "w_ff1", "w_ff2")),
    ("swa_fwd_l0", _of(1, "w_in")),
    ("sb_fwd_l1", _of(1, "w_up_a", "w_up_b", "w_o", "w_pe", "w_pg", "w_ff1")),
    ("swa_fwd_l1", _of(1, "w_ff2")),
)
REDUCES = (
    (_of(1, *MLP_W), "dh_mlp_l1", "sb_bwd_l1"),
    (_of(1, "w_o", "w_up_a", "w_up_b"), "do_a_l1", "sb_bwd_l1"),
    (_of(1, "w_in"), "dw_ff2_l0", "swa_bwd_l0"),
    (_of(0, *MLP_W), "dh_mlp_l0", "sb_bwd_l0"),
    (_of(0, "w_o", "w_up_a", "w_up_b"), "do_a_l0", "sb_bwd_l0"),
    (_of(0, "w_in"), "dh_gate_l0", "dh_qkv_l0"),
)


def _merge_comms(comms):
    if len(comms) == 1:
        return comms[0]

    def cuts(counts):
        edges = [0]
        for c in counts:
            edges.append(edges[-1] + c)
        return [slice(a, b) for a, b in zip(edges[:-1], edges[1:])]

    s_in = cuts([len(c.inputs) for c in comms])
    s_out = cuts([len(c.out_shapes) for c in comms])
    s_sem = cuts([len(c.sems) for c in comms])

    def start(pos, cin, cout, csem):
        for c, i, o, s in zip(comms, s_in, s_out, s_sem):
            c.start(pos, cin[i], cout[o], csem[s])

    def finish(pos, cin, cout, csem):
        for c, i, o, s in zip(comms, s_in, s_out, s_sem):
            c.finish(pos, cin[i], cout[o], csem[s])

    return _Comm(sum([c.inputs for c in comms], []), sum([c.out_shapes for c in comms], []),
                 sum([c.sems for c in comms], []), start, finish)


class _LayerWeights:
    def __init__(self, full, layer):
        self.full, self.layer, self.cache = full, layer, {}

    def __getitem__(self, name):
        if name not in self.cache:
            if name == "w_qkv_t":
                self.cache[name] = self.full[(self.layer, "w_in")][:QKV_COLS]
            elif name == "w_gate_t":
                self.cache[name] = self.full[(self.layer, "w_in")][QKV_COLS:]
            else:
                base = name[:-2] if name.endswith("_t") else name
                assert (base in COL_SHARDED) == name.endswith("_t"), name
                self.cache[name] = self.full[(self.layer, base)]
        return self.cache[name]


class _Plan:
    def __init__(self, w_sh, pos):
        self.w_sh = dict(zip(WEIGHTS, w_sh))
        self.pos = pos
        self.full, self.gw, self.parts, self.recv = {}, {}, {}, {}
        self.slabs = {}
        self.hosted = {}
        for i, (host, _) in enumerate(GATHERS):
            self.hosted.setdefault(host, []).append(("gather", i))
        for i, (_, sib_host, chip_host) in enumerate(REDUCES):
            self.hosted.setdefault(sib_host, []).append(("sibling", i))
            self.hosted.setdefault(chip_host, []).append(("chips", i))

    def _gather(self, i):
        return _gather_comm([_wire_shard(n, self.w_sh[n][layer]) for layer, n in GATHERS[i][1]])

    def _gathered(self, i, outs):
        for (layer, n), g in zip(GATHERS[i][1], outs):
            self.full[(layer, n)] = _full_weight(g)

    def weights(self, layer):
        return _LayerWeights(self.full, layer)

    def grad(self, layer, name, value):
        self.gw[(layer, name)] = value

    def _sibling(self, i):
        self.slabs[i] = [_to_slabs(self.gw[item]) for item in REDUCES[i][0]]
        return _rs_sibling_comm(self.slabs[i])

    def _sibling_done(self, i, outs):
        parts = _chip_partials(self.pos, self.slabs[i], outs, f"chip_partials_{i}")
        for item, part in zip(REDUCES[i][0], parts):
            self.parts[item] = part

    def _chips(self, i):
        return _rs_chips_comm([self.parts[item] for item in REDUCES[i][0]])

    def _chips_done(self, i, outs):
        for item, r in zip(REDUCES[i][0], outs):
            self.recv[item] = r

    def comm(self, name):
        if name not in self.hosted:
            return None
        make = {"gather": self._gather, "sibling": self._sibling, "chips": self._chips}
        return _merge_comms([make[kind](i) for kind, i in self.hosted[name]])

    def done(self, name, outs):
        took = {"gather": self._gathered, "sibling": self._sibling_done, "chips": self._chips_done}
        off = 0
        for kind, i in self.hosted[name]:
            n = len(GATHERS[i][1]) if kind == "gather" else len(REDUCES[i][0])
            took[kind](i, outs[off:off + n])
            off += n


def kernel(x, p, w_in, w_up_a, w_up_b, w_o, w_ff1, w_ff2, w_pe, w_pg, g_mix, g_mlp, g_pe, g_final, sinks, rel_bias, loss_target, m_w_in, m_w_up_a, m_w_up_b, m_w_o, m_w_ff1, m_w_ff2, m_w_pe, m_w_pg, m_g_mix, m_g_mlp, m_g_pe, m_g_final, m_sinks, m_rel_bias, v_w_in, v_w_up_a, v_w_up_b, v_w_o, v_w_ff1, v_w_ff2, v_w_pe, v_w_pg, v_g_mix, v_g_mlp, v_g_pe, v_g_final, v_sinks, v_rel_bias):
    w_sh = [w_in, w_up_a, w_up_b, w_o, w_ff1, w_ff2, w_pe, w_pg]
    m_sh = [m_w_in, m_w_up_a, m_w_up_b, m_w_o, m_w_ff1, m_w_ff2, m_w_pe, m_w_pg]
    v_sh = [v_w_in, v_w_up_a, v_w_up_b, v_w_o, v_w_ff1, v_w_ff2, v_w_pe, v_w_pg]
    depth = w_in.shape[0]
    assert depth == 2 and x.shape[-1] * 2 + QKV_COLS == w_in.shape[2] * N_DEV

    px, py, pc = _position()
    plan = _Plan(w_sh, jnp.stack([px, py, pc]).astype(jnp.int32))
    loss_row, grad_x, _, small = _local_step(
        x[0], p[:, 0], loss_target[0], plan.weights, g_mix, g_mlp, g_pe, g_final, sinks, rel_bias, plan=plan)

    small_g = _pack_small([small[n] for n in SMALL] + [loss_row[0, :1]])
    grad_w, delta_w, new_m, new_v = [], [], [], []
    for a, name in enumerate(WEIGHTS):
        parts = [plan.parts[(l, name)] for l in range(depth)]
        recvs = [plan.recv[(l, name)] for l in range(depth)]
        if name == "w_in":
            flip = lambda t: t.transpose(0, 2, 1)
            outs, (small_all,) = _adamw_weight(parts, recvs, flip(w_sh[a]), flip(m_sh[a]), flip(v_sh[a]),
                                               f"adamw_{name}", comm=_gather_comm([small_g]))
            outs = [flip(o) for o in outs]
        else:
            outs = _adamw_weight(parts, recvs, w_sh[a], m_sh[a], v_sh[a], f"adamw_{name}",
                                 grad_t=name in COL_SHARDED)
        for lst, o in zip((grad_w, delta_w, new_m, new_v), outs):
            lst.append(o)

    small_w = [g_mix, g_mlp, g_pe, g_final, sinks, rel_bias]
    small_m = [m_g_mix, m_g_mlp, m_g_pe, m_g_final, m_sinks, m_rel_bias]
    small_v = [v_g_mix, v_g_mlp, v_g_pe, v_g_final, v_sinks, v_rel_bias]
    small_shapes = [a.shape for a in small_w] + [(1,)]
    zero = jnp.zeros((1,), F32)
    packed_s = _adamw_replicated(small_all, _pack_small(small_w + [zero]), _pack_small(small_m + [zero]),
                                 _pack_small(small_v + [zero + 1.0]), "adamw_replicated")
    sg, sd, sm, sv = [_unpack_small(t, small_shapes) for t in packed_s]
    loss = sg[-1][0]

    return (loss, grad_x[None], *grad_w, *sg[:-1], *delta_w, *sd[:-1], *new_m, *sm[:-1], *new_v, *sv[:-1])
```

```python
import functools
import math

import numpy as np
import jax
import jax.numpy as jnp
from jax import lax
from jax.experimental import pallas as pl
from jax.experimental.pallas import tpu as pltpu

F32 = jnp.float32
MXU = jnp.bfloat16
WIRE = jnp.bfloat16

HEAD_DIM = 64
SB_HEADS = 8
SW_HEADS = 8
SW_KV = 2
SW_GROUP = SW_HEADS // SW_KV
BLOCK = 128
N_BUCKETS = 32
MAX_DISTANCE = 128
EPS = 1e-6
SCALE = HEAD_DIM ** -0.5
SB_W = SB_HEADS * HEAD_DIM
SW_W = SW_HEADS * HEAD_DIM
QKV_COLS = 3 * SB_W + SW_W + 2 * SW_KV * HEAD_DIM
N_DEV = 8
LANES = 128
N_PAIR = SB_HEADS // 2
NEG = -1e30

ADAM_LR = 0.001
ADAM_B1 = 0.9
ADAM_B2 = 0.999
ADAM_EPS = 1e-08
ADAM_WD = 0.01
ADAM_STEP = 10

VMEM_LIMIT = 48 * 1024 * 1024
SB_TQ = 256
SB_DEAD = -105.0
SB_SUB = 4

WEIGHTS = ("w_in", "w_up_a", "w_up_b", "w_o", "w_ff1", "w_ff2", "w_pe", "w_pg")
COL_SHARDED = ("w_in", "w_up_a", "w_up_b", "w_ff1", "w_pe")
SMALL = ("g_mix", "g_mlp", "g_pe", "g_final", "sinks", "rel_bias")


def _cparams(**kw):
    return pltpu.CompilerParams(vmem_limit_bytes=VMEM_LIMIT, **kw)


def _dot(a, b):
    return jnp.dot(a, b, preferred_element_type=F32)


def _dot_nt(a, b):
    return lax.dot_general(a, b, (((1,), (1,)), ((), ())), preferred_element_type=F32)


def _dot_tn(a, b):
    return lax.dot_general(a, b, (((0,), (0,)), ((), ())), preferred_element_type=F32)


def _tile(n, target, unit=LANES):
    if n <= target:
        return n
    t = (target // unit) * unit
    while t > unit and n % t:
        t -= unit
    assert n % t == 0, (n, target)
    return t


def _sigmoid(x):
    return 0.5 * jnp.tanh(0.5 * x) + 0.5


class _Comm:
    def __init__(self, inputs, out_shapes, sems, start, finish):
        self.inputs, self.out_shapes, self.sems = list(inputs), list(out_shapes), list(sems)
        self.start, self.finish = start, finish


def _call(body, *, grid, in_specs, out_specs, out_shape, scratch_shapes=(), args, name, comm=None):
    n_in, n_out, n_scr = len(in_specs), len(out_shape), len(scratch_shapes)
    if comm is None:
        outs = pl.pallas_call(body, grid=grid, in_specs=list(in_specs), out_specs=list(out_specs),
                              out_shape=list(out_shape), scratch_shapes=list(scratch_shapes),
                              compiler_params=_cparams(), name=name)(*args)
        return list(outs), None
    ci, co = len(comm.inputs), len(comm.out_shapes)
    any_spec = pl.BlockSpec(memory_space=pl.ANY)

    def wrapped(*refs):
        ins, cin = refs[:n_in], refs[n_in:n_in + ci]
        o0 = n_in + ci
        outs, cout = refs[o0:o0 + n_out], refs[o0 + n_out:o0 + n_out + co]
        s0 = o0 + n_out + co
        scr, csem = refs[s0:s0 + n_scr], refs[s0 + n_scr:]
        ids = [pl.program_id(d) for d in range(len(grid))]
        first = functools.reduce(jnp.logical_and, [i == 0 for i in ids])
        last = functools.reduce(jnp.logical_and, [i == g - 1 for i, g in zip(ids, grid)])
        pos = (lax.axis_index("x"), lax.axis_index("y"), lax.axis_index("c"))

        @pl.when(first)
        def _():
            comm.start(pos, cin, cout, csem)

        body(*ins, *outs, *scr)

        @pl.when(last)
        def _():
            comm.finish(pos, cin, cout, csem)

    outs = pl.pallas_call(wrapped, grid=grid, in_specs=list(in_specs) + [any_spec] * ci,
                          out_specs=list(out_specs) + [any_spec] * co,
                          out_shape=list(out_shape) + comm.out_shapes,
                          scratch_shapes=list(scratch_shapes) + comm.sems,
                          compiler_params=_cparams(), name=name)(*args, *comm.inputs)
    return list(outs[:n_out]), list(outs[n_out:])


def _accumulate(o_ref, value, first):
    @pl.when(first)
    def _():
        o_ref[...] = value

    @pl.when(jnp.logical_not(first))
    def _():
        o_ref[...] += value


def _mm(a, b, *, ta=False, tb=False, extras=(), epi=None, out_dtypes=(F32,),
        tm=1024, tn=1024, tk=1024, name, comm=None):
    if ta:
        kdim, m = a.shape
    else:
        m, kdim = a.shape
    n = b.shape[0] if tb else b.shape[1]
    assert (b.shape[1] if tb else b.shape[0]) == kdim
    tm, tn, tk = _tile(m, tm), _tile(n, tn), _tile(kdim, tk)
    nk = kdim // tk
    n_ex, n_out = len(extras), len(out_dtypes)

    a_spec = (pl.BlockSpec((tk, tm), lambda i, j, k: (k, i)) if ta
              else pl.BlockSpec((tm, tk), lambda i, j, k: (i, k)))
    b_spec = (pl.BlockSpec((tn, tk), lambda i, j, k: (j, k)) if tb
              else pl.BlockSpec((tk, tn), lambda i, j, k: (k, j)))
    ex_specs = []
    for e in extras:
        assert e.shape in ((m, n), (1, n), (m, 1)), (e.shape, m, n)
        if e.shape == (m, n):
            ex_specs.append(pl.BlockSpec((tm, tn), lambda i, j, k: (i, j)))
        elif e.shape[0] == 1:
            ex_specs.append(pl.BlockSpec((1, tn), lambda i, j, k: (0, j)))
        else:
            ex_specs.append(pl.BlockSpec((tm, 1), lambda i, j, k: (i, 0)))
    out_specs, out_shape, row_sums = [], [], []
    for dt in out_dtypes:
        kind = dt[1] if isinstance(dt, tuple) else "tile"
        row_sums.append(kind == "rowsum")
        if kind == "tile":
            out_specs.append(pl.BlockSpec((tm, tn), lambda i, j, k: (i, j)))
            out_shape.append(jax.ShapeDtypeStruct((m, n), dt))
            continue
        assert tn == n, "per-row and summed outputs need whole rows in one tile"
        if kind == "col":
            out_specs.append(pl.BlockSpec((tm, 1), lambda i, j, k: (i, 0)))
            out_shape.append(jax.ShapeDtypeStruct((m, 1), dt[0]))
        else:
            out_specs.append(pl.BlockSpec((1, tn), lambda i, j, k: (0, 0)))
            out_shape.append(jax.ShapeDtypeStruct((1, n), dt[0]))

    def body(a_ref, b_ref, *rest):
        ex_refs = rest[:n_ex]
        out_refs = rest[n_ex:n_ex + n_out]
        acc = rest[-1]
        k = pl.program_id(2)
        first_rows = pl.program_id(0) == 0

        def prod():
            av = a_ref[...].astype(MXU)
            bv = b_ref[...].astype(MXU)
            return _dot_tn(av, bv) if ta else (_dot_nt(av, bv) if tb else _dot(av, bv))

        def finish(res):
            if epi is not None:
                res = epi(res, *[e[...] for e in ex_refs])
            if not isinstance(res, tuple):
                res = (res,)
            for o_ref, r, summed in zip(out_refs, res, row_sums):
                if summed:
                    _accumulate(o_ref, r.astype(o_ref.dtype), first_rows)
                else:
                    o_ref[...] = r.astype(o_ref.dtype)

        if nk == 1:
            finish(prod())
            return

        @pl.when(k == 0)
        def _():
            acc[...] = prod()

        @pl.when(jnp.logical_and(k > 0, k < nk - 1))
        def _():
            acc[...] += prod()

        @pl.when(k == nk - 1)
        def _():
            finish(acc[...] + prod())

    outs, couts = _call(
        body,
        grid=(m // tm, n // tn, nk),
        in_specs=[a_spec, b_spec] + ex_specs,
        out_specs=out_specs,
        out_shape=out_shape,
        scratch_shapes=[pltpu.VMEM((tm, tn), F32)],
        args=(a, b, *extras), name=name, comm=comm)
    res = outs[0] if n_out == 1 else tuple(outs)
    return res if comm is None else (res, couts)


def _rms_fwd(x, g, name, comm=None):
    s, d = x.shape
    tr = _tile(s, 256)

    def body(x_ref, g_ref, h_ref, r_ref):
        xf = x_ref[...]
        r = lax.rsqrt(jnp.mean(xf * xf, axis=-1, keepdims=True) + EPS)
        h_ref[...] = ((xf * r) * g_ref[...]).astype(h_ref.dtype)
        r_ref[...] = r

    outs, couts = _call(
        body,
        grid=(s // tr,),
        in_specs=[pl.BlockSpec((tr, d), lambda i: (i, 0)), pl.BlockSpec((1, d), lambda i: (0, 0))],
        out_specs=[pl.BlockSpec((tr, d), lambda i: (i, 0)), pl.BlockSpec((tr, 1), lambda i: (i, 0))],
        out_shape=[jax.ShapeDtypeStruct((s, d), MXU), jax.ShapeDtypeStruct((s, 1), F32)],
        args=(x, g), name=name, comm=comm)
    return tuple(outs) if comm is None else (tuple(outs), couts)


def _loss_head(x, g, target, name):
    s, d = x.shape
    tr = _tile(s, 256)

    def body(x_ref, g_ref, t_ref, loss_ref, dx_ref, dg_ref):
        @pl.when(pl.program_id(0) == 0)
        def _():
            dg_ref[...] = jnp.zeros_like(dg_ref)
            loss_ref[...] = jnp.zeros_like(loss_ref)

        xf = x_ref[...]
        gv = g_ref[...]
        r = lax.rsqrt(jnp.mean(xf * xf, axis=-1, keepdims=True) + EPS)
        xhat = xf * r
        err = xhat * gv - t_ref[...]
        loss_ref[...] += 0.5 * jnp.sum(jnp.mean(err * err, axis=-1, keepdims=True), axis=0, keepdims=True)
        dy = err * (1.0 / d)
        dxhat = dy * gv
        mean = jnp.mean(dxhat * xhat, axis=-1, keepdims=True)
        dx_ref[...] = r * (dxhat - xhat * mean)
        dg_ref[...] += jnp.sum(dy * xhat, axis=0, keepdims=True)

    row = pl.BlockSpec((tr, d), lambda i: (i, 0))
    vec = pl.BlockSpec((1, d), lambda i: (0, 0))
    return pl.pallas_call(
        body,
        grid=(s // tr,),
        in_specs=[row, vec, row],
        out_specs=[pl.BlockSpec((1, LANES), lambda i: (0, 0)), row, vec],
        out_shape=[jax.ShapeDtypeStruct((1, LANES), F32), jax.ShapeDtypeStruct((s, d), F32),
                   jax.ShapeDtypeStruct((1, d), F32)],
        compiler_params=_cparams(),
        name=name,
    )(x, g, target)


def _mix_fwd(oa, ob, wa_t, wb_t, gates, name):
    s, kd = oa.shape
    d = wa_t.shape[0]
    tm, tn = _tile(s, 1024), _tile(d, 1024)
    nj = d // tn

    def body(oa_ref, ob_ref, wa_ref, wb_ref, ga_ref, gb_ref, out_ref):
        ya = _dot_nt(oa_ref[...], wa_ref[...])
        yb = _dot_nt(ob_ref[...], wb_ref[...])
        out_ref[...] = (_sigmoid(ga_ref[...].astype(F32)) * ya
                        + _sigmoid(gb_ref[...].astype(F32)) * yb).astype(out_ref.dtype)

    o_spec = pl.BlockSpec((tm, kd), lambda i, j: (i, 0))
    w_spec = pl.BlockSpec((tn, kd), lambda i, j: (j, 0))
    return pl.pallas_call(
        body,
        grid=(s // tm, nj),
        in_specs=[o_spec, o_spec, w_spec, w_spec,
                  pl.BlockSpec((tm, tn), lambda i, j: (i, j)),
                  pl.BlockSpec((tm, tn), lambda i, j: (i, j + nj))],
        out_specs=pl.BlockSpec((tm, tn), lambda i, j: (i, j)),
        out_shape=jax.ShapeDtypeStruct((s, d), MXU),
        compiler_params=_cparams(),
        name=name,
    )(oa, ob, wa_t, wb_t, gates, gates)


def _mix_bwd(dx, w_o, oa, ob, wa_t, wb_t, gates, name):
    s, kd = oa.shape
    d = wa_t.shape[0]
    tm = _tile(s, 512)

    def body(dx_ref, wo_ref, oa_ref, ob_ref, wa_ref, wb_ref, g_ref, dya_ref, dyb_ref, dg_ref):
        dm = _dot_nt(dx_ref[...], wo_ref[...])
        ya = _dot_nt(oa_ref[...], wa_ref[...])
        yb = _dot_nt(ob_ref[...], wb_ref[...])
        sa = _sigmoid(g_ref[:, :d].astype(F32))
        sb = _sigmoid(g_ref[:, d:].astype(F32))
        dya_ref[...] = (dm * sa).astype(dya_ref.dtype)
        dyb_ref[...] = (dm * sb).astype(dyb_ref.dtype)
        dg_ref[:, :d] = (dm * ya * sa * (1.0 - sa)).astype(dg_ref.dtype)
        dg_ref[:, d:] = (dm * yb * sb * (1.0 - sb)).astype(dg_ref.dtype)

    def rows(width):
        return pl.BlockSpec((tm, width), lambda i: (i, 0))

    def whole(arr):
        return pl.BlockSpec(arr.shape, lambda i: (0, 0))

    return pl.pallas_call(
        body,
        grid=(s // tm,),
        in_specs=[rows(d), whole(w_o), rows(kd), rows(kd), whole(wa_t), whole(wb_t), rows(2 * d)],
        out_specs=[rows(d), rows(d), rows(2 * d)],
        out_shape=[jax.ShapeDtypeStruct((s, d), MXU), jax.ShapeDtypeStruct((s, d), MXU),
                   jax.ShapeDtypeStruct((s, 2 * d), MXU)],
        compiler_params=_cparams(),
        name=name,
    )(dx, w_o, oa, ob, wa_t, wb_t, gates)


def _ple(p, w_pe_t, h, w_pg, other, *, backward, name, next_g=None):
    s, kp = p.shape
    d = w_pe_t.shape[0]
    tm, tn = _tile(s, 1024), _tile(d, 1024)
    with_norm = next_g is not None
    assert not (with_norm and (backward or tn != d))

    def body(p_ref, wpe_ref, h_ref, wpg_ref, other_ref, *rest):
        out_refs = rest[1:] if with_norm else rest
        pe = _dot_nt(p_ref[...].astype(MXU), wpe_ref[...])
        gt = _dot(h_ref[...], wpg_ref[...])
        sg = _sigmoid(gt)
        if backward:
            dout = other_ref[...]
            out_refs[0][...] = (dout * sg).astype(out_refs[0].dtype)
            out_refs[1][...] = (dout * pe * sg * (1.0 - sg)).astype(out_refs[1].dtype)
        elif with_norm:
            x_new, h_new, r_new = _residual_norm(pe * sg, other_ref[...], rest[0][...])
            out_refs[0][...] = x_new
            out_refs[1][...] = h_new.astype(out_refs[1].dtype)
            out_refs[2][...] = r_new
        else:
            out_refs[0][...] = other_ref[...] + pe * sg

    t_spec = pl.BlockSpec((tm, tn), lambda i, j: (i, j))
    if backward:
        out_specs, out_shape = [t_spec, t_spec], [jax.ShapeDtypeStruct((s, d), MXU)] * 2
    else:
        out_specs, out_shape = [t_spec], [jax.ShapeDtypeStruct((s, d), F32)]
    in_specs = [pl.BlockSpec((tm, kp), lambda i, j: (i, 0)),
                pl.BlockSpec((tn, kp), lambda i, j: (j, 0)),
                pl.BlockSpec((tm, d), lambda i, j: (i, 0)),
                pl.BlockSpec((d, tn), lambda i, j: (0, j)),
                t_spec]
    args = [p, w_pe_t, h, w_pg, other]
    if with_norm:
        in_specs.append(pl.BlockSpec((1, tn), lambda i, j: (0, j)))
        args.append(next_g)
        out_specs += [t_spec, pl.BlockSpec((tm, 1), lambda i, j: (i, 0))]
        out_shape += [jax.ShapeDtypeStruct((s, d), MXU), jax.ShapeDtypeStruct((s, 1), F32)]
    outs = pl.pallas_call(
        body,
        grid=(s // tm, d // tn),
        in_specs=in_specs,
        out_specs=out_specs,
        out_shape=out_shape,
        compiler_params=_cparams(),
        name=name,
    )(*args)
    return tuple(outs) if (backward or with_norm) else outs[0]


def _split_dot(x, tri):
    hi = x.astype(jnp.bfloat16)
    lo = (x - hi.astype(F32)).astype(jnp.bfloat16)
    return _dot(hi, tri) + _dot(lo, tri)


def _log_sigmoids(z):
    lb = jnp.minimum(z, 0.0) - jnp.log(1.0 + jnp.exp(-jnp.abs(z)))
    return lb, lb - z


def _head_lanes(hh):
    lane = lax.broadcasted_iota(jnp.int32, (1, LANES), 1)
    return jnp.logical_and(lane >= hh * HEAD_DIM, lane < (hh + 1) * HEAD_DIM)


def _sb_fwd(qkv, name, comm=None):
    s = qkv.shape[0]
    tq = _tile(s, SB_TQ)
    nsub = SB_SUB if (s // tq) % SB_SUB == 0 else 1

    def body(q_ref, k_ref, v_ref, o_ref):
        row = lax.broadcasted_iota(jnp.int32, (tq, tq), 0)
        col = lax.broadcasted_iota(jnp.int32, (tq, tq), 1)
        causal = col < row
        tri = jnp.where(row > col, 1.0, 0.0).astype(jnp.bfloat16)
        started = [_sb_fwd_straight(q_ref, k_ref, v_ref, pl.program_id(1) * nsub + sub, sub, tq, causal, tri)
                   for sub in range(nsub)]
        for sub, (block, i, cs, accs) in enumerate(started):
            def top(cs):
                return jnp.maximum(jnp.max(cs[0]), jnp.max(cs[1]))

            def live(st):
                return jnp.logical_and(st[0] >= 0, st[1] > SB_DEAD)

            def walk(st, block=block):
                cs, accs = block(st[0], st[2], st[3], False)
                return st[0] - 1, top(cs), cs, accs

            accs = lax.while_loop(live, walk, (i - 2, top(cs), cs, accs))[3]
            o_ref[sub * tq:(sub + 1) * tq, :] = jnp.where(_head_lanes(0), accs[0], accs[1]).astype(o_ref.dtype)

    outs, couts = _call(
        body,
        grid=(N_PAIR, s // (nsub * tq)),
        in_specs=[pl.BlockSpec((nsub * tq, LANES), lambda p, i: (i, p)),
                  pl.BlockSpec((s, LANES), lambda p, i: (0, N_PAIR + p)),
                  pl.BlockSpec((s, LANES), lambda p, i: (0, 2 * N_PAIR + p))],
        out_specs=[pl.BlockSpec((nsub * tq, LANES), lambda p, i: (i, p))],
        out_shape=[jax.ShapeDtypeStruct((s, SB_W), MXU)],
        args=(qkv, qkv, qkv), name=name, comm=comm)
    return outs[0] if comm is None else (outs[0], couts)


def _sb_fwd_straight(q_ref, k_ref, v_ref, i, sub, tq, causal, tri):
    qf = q_ref[sub * tq:(sub + 1) * tq, :].astype(F32) * SCALE
    qms = [jnp.where(_head_lanes(hh), qf, 0.0).astype(MXU) for hh in range(2)]

    def block(kb, cs, accs, masked, present=None):
        rows = pl.ds(pl.multiple_of(kb * tq, tq), tq)
        ks, vs = k_ref[rows, :], v_ref[rows, :]
        off = 0.0 if present is None else (1.0 - present) * NEG
        new_c, new_acc = [], []
        for hh in range(2):
            lb, lm = _log_sigmoids(_dot_nt(qms[hh], ks))
            if masked:
                lm = jnp.where(causal, lm, 0.0)
            a = jnp.exp(lb + _split_dot(lm, tri) + (cs[hh] + off))
            if masked:
                a = jnp.where(causal, a, 0.0)
            new_acc.append(accs[hh] + _dot(a.astype(MXU), vs))
            row_sum = jnp.sum(lm, axis=1, keepdims=True)
            new_c.append(cs[hh] + (row_sum if present is None else row_sum * present))
        return tuple(new_c), tuple(new_acc)

    zc, za = jnp.zeros((tq, 1), F32), jnp.zeros((tq, LANES), F32)
    cs, accs = block(i, (zc, zc), (za, za), True)
    cs, accs = block(jnp.maximum(i - 1, 0), cs, accs, False, jnp.where(i > 0, 1.0, 0.0))
    return block, i, cs, accs


def _sb_bwd(qkv, do, name, comm=None):
    s = qkv.shape[0]
    tq = _tile(s, SB_TQ)
    nq = s // tq
    nsub = SB_SUB if nq % SB_SUB == 0 else 1
    nsteps = nq // nsub

    def body(q_ref, k_ref, v_ref, do_ref, dq_ref, dk_ref, dv_ref, dk_acc, dv_acc, carries):
        step = pl.program_id(1)

        @pl.when(step == 0)
        def _():
            dk_acc[...] = jnp.zeros_like(dk_acc)
            dv_acc[...] = jnp.zeros_like(dv_acc)

        row = lax.broadcasted_iota(jnp.int32, (tq, tq), 0)
        col = lax.broadcasted_iota(jnp.int32, (tq, tq), 1)
        causal = col < row
        tri_rev = jnp.where(row > col, 1.0, 0.0).astype(jnp.bfloat16)
        tri_excl = jnp.where(row < col, 1.0, 0.0).astype(jnp.bfloat16)
        zc, za = jnp.zeros((tq, 1), F32), jnp.zeros((tq, LANES), F32)

        def top(cs):
            return jnp.maximum(jnp.max(cs[0]), jnp.max(cs[1]))

        def live(st):
            return jnp.logical_and(st[0] >= 0, st[1] > SB_DEAD)

        def row_sums(pre):
            return [jnp.sum(lm, axis=1, keepdims=True) for _, lm in pre]

        def query_block(sub):
            i = step * nsub + sub
            q_rows = slice(sub * tq, (sub + 1) * tq)
            qf = q_ref[q_rows, :].astype(F32) * SCALE
            dof = do_ref[q_rows, :]
            qms = [jnp.where(_head_lanes(hh), qf, 0.0).astype(MXU) for hh in range(2)]
            doms = [jnp.where(_head_lanes(hh), dof, jnp.zeros_like(dof)) for hh in range(2)]

            def terms(kb, masked):
                rows = pl.ds(pl.multiple_of(kb * tq, tq), tq)
                ks = k_ref[rows, :]
                out = []
                for hh in range(2):
                    lb, lm = _log_sigmoids(_dot_nt(qms[hh], ks))
                    if masked:
                        lm = jnp.where(causal, lm, 0.0)
                    out.append((lb, lm))
                return out

            def block(kb, cs, gpres, dqs, masked, gate=None, pre=None):
                rows = pl.ds(pl.multiple_of(kb * tq, tq), tq)
                ks, vs = k_ref[rows, :], v_ref[rows, :]
                pre = terms(kb, masked) if pre is None else pre
                new_g, new_dq = [], []
                dk_add, dv_add = None, None
                for hh in range(2):
                    lb, lm = pre[hh]
                    off = 0.0 if gate is None else (1.0 - gate) * NEG
                    a = jnp.exp(lb + _split_dot(lm, tri_rev) + (cs[hh] + off))
                    if masked:
                        a = jnp.where(causal, a, 0.0)
                    g = a * _dot_nt(doms[hh], vs)
                    gsum = gpres[hh] + _split_dot(g, tri_excl)
                    dz = g - (g + gsum) * jnp.exp(lb)
                    if masked:
                        dz = jnp.where(causal, dz, 0.0)
                    dzb = dz.astype(MXU)
                    new_dq.append(dqs[hh] + _dot(dzb, ks))
                    dk_h = _dot_tn(dzb, qms[hh])
                    dv_h = _dot_tn(a.astype(MXU), doms[hh])
                    dk_add = dk_h if dk_add is None else dk_add + dk_h
                    dv_add = dv_h if dv_add is None else dv_add + dv_h
                    new_g.append(gpres[hh] + jnp.sum(g, axis=1, keepdims=True))
                dk_acc[rows, :] += dk_add
                dv_acc[rows, :] += dv_add
                return tuple(new_g), tuple(new_dq)

            prev = jnp.maximum(i - 1, 0)
            gate = jnp.where(i > 0, 1.0, 0.0)
            t_diag, t_prev = terms(i, True), terms(prev, False)
            c_diag = row_sums(t_diag)
            sums = row_sums(t_prev)
            c_prev = tuple(c_diag[hh] + sums[hh] * gate for hh in range(2))
            return dict(i=i, prev=prev, gate=gate, q_rows=q_rows, terms=terms, block=block,
                        t_diag=t_diag, t_prev=t_prev, c_diag=c_diag, c_prev=c_prev)

        blocks = [query_block(sub) for sub in range(nsub)]
        for qb in blocks:
            def record(st, qb=qb):
                kb, cs = st[0], st[2]
                sums = row_sums(qb["terms"](kb, False))
                for hh in range(2):
                    carries[hh, kb] = cs[hh]
                cs = tuple(cs[hh] + sums[hh] for hh in range(2))
                return kb - 1, top(cs), cs

            first = lax.while_loop(live, record, (qb["i"] - 2, top(qb["c_prev"]), qb["c_prev"]))[0] + 1
            qb["mid"] = lax.fori_loop(
                first, qb["i"] - 1,
                lambda kb, cr, qb=qb: qb["block"](kb, (carries[0, kb], carries[1, kb]), cr[0], cr[1], False),
                ((zc, zc), (za, za)))
        for qb in blocks:
            gpres, dqs = qb["block"](qb["prev"], qb["c_diag"], *qb["mid"], False, qb["gate"], qb["t_prev"])
            dqs = qb["block"](qb["i"], (zc, zc), gpres, dqs, True, None, qb["t_diag"])[1]
            dq_ref[qb["q_rows"], :] = (jnp.where(_head_lanes(0), dqs[0], dqs[1]) * SCALE).astype(dq_ref.dtype)

        @pl.when(step == nsteps - 1)
        def _():
            dk_ref[...] = dk_acc[...].astype(dk_ref.dtype)
            dv_ref[...] = dv_acc[...].astype(dv_ref.dtype)

    blk = pl.BlockSpec((nsub * tq, LANES), lambda p, i: (i, p))
    full = pl.BlockSpec((s, LANES), lambda p, i: (0, p))
    outs, couts = _call(
        body,
        grid=(N_PAIR, nsteps),
        in_specs=[blk,
                  pl.BlockSpec((s, LANES), lambda p, i: (0, N_PAIR + p)),
                  pl.BlockSpec((s, LANES), lambda p, i: (0, 2 * N_PAIR + p)),
                  blk],
        out_specs=[blk, full, full],
        out_shape=[jax.ShapeDtypeStruct((s, SB_W), MXU)] * 3,
        scratch_shapes=[pltpu.VMEM((s, LANES), F32), pltpu.VMEM((s, LANES), F32),
                        pltpu.VMEM((2, nq, tq, 1), F32)],
        args=(qkv, qkv, qkv, do), name=name, comm=comm)
    return tuple(outs) if comm is None else (tuple(outs), couts)


def _bucket_table():
    i = np.arange(BLOCK)[:, None]
    j = np.arange(2 * BLOCK)[None, :]
    d = np.maximum(BLOCK + i - j, 0)
    max_exact = N_BUCKETS // 2
    df = np.maximum(d, 1).astype(np.float32)
    large = max_exact + (np.log(df / max_exact) / math.log(MAX_DISTANCE / max_exact)
                         * (N_BUCKETS - max_exact)).astype(np.int32)
    large = np.minimum(large, N_BUCKETS - 1)
    return np.where(d < max_exact, d, large).astype(np.int32)


def _build_bias(rel_bias, buckets, name):
    def body(rb_ref, bk_ref, out_ref):
        h = pl.program_id(0)
        bk = bk_ref[...]
        acc = jnp.zeros(bk.shape, F32)
        for b in range(N_BUCKETS):
            acc = jnp.where(bk == b, rb_ref[b, h], acc)
        out_ref[...] = acc

    return pl.pallas_call(
        body,
        grid=(SW_HEADS,),
        in_specs=[pl.BlockSpec(memory_space=pltpu.SMEM),
                  pl.BlockSpec((BLOCK, 2 * BLOCK), lambda h: (0, 0))],
        out_specs=pl.BlockSpec((None, BLOCK, 2 * BLOCK), lambda h: (h, 0, 0)),
        out_shape=jax.ShapeDtypeStruct((SW_HEADS, BLOCK, 2 * BLOCK), F32),
        name=name,
    )(rel_bias, buckets)


def _bias_grad(dbias_layers, buckets, name):
    n_l = len(dbias_layers)

    def body(*refs):
        bk = refs[n_l][...]
        out_ref = refs[n_l + 1]
        db = refs[0][...]
        for r in refs[1:n_l]:
            db = db + r[...]
        lane = lax.broadcasted_iota(jnp.int32, (1, LANES), 1)
        acc = jnp.zeros((1, LANES), F32)
        for b in range(N_BUCKETS):
            part = jnp.sum(jnp.where(bk == b, db, 0.0), axis=1, keepdims=True)
            tot = jnp.sum(part, axis=0, keepdims=True)
            acc = jnp.where(lane == b, tot, acc)
        out_ref[...] = acc

    hspec = pl.BlockSpec((None, BLOCK, 2 * BLOCK), lambda h: (h, 0, 0))
    return pl.pallas_call(
        body,
        grid=(SW_HEADS,),
        in_specs=[hspec] * n_l + [pl.BlockSpec((BLOCK, 2 * BLOCK), lambda h: (0, 0))],
        out_specs=pl.BlockSpec((None, 1, LANES), lambda h: (h, 0, 0)),
        out_shape=jax.ShapeDtypeStruct((SW_HEADS, 1, LANES), F32),
        name=name,
    )(*dbias_layers, buckets)


GROUP_ROWS = SW_GROUP * BLOCK


def _group_lanes(g):
    lane = lax.broadcasted_iota(jnp.int32, (1, LANES), 1)
    gvec = jnp.zeros((1, LANES), jnp.int32) + g
    return jnp.where(lane >= HEAD_DIM, 1, 0) == gvec, gvec


def _stack_heads(x, g):
    kv_lanes, gvec = _group_lanes(g)
    parts = []
    for j in range(SW_GROUP):
        half = x[:, (j // 2) * LANES:(j // 2 + 1) * LANES]
        moved = jnp.where(gvec == j % 2, half, pltpu.roll(half, HEAD_DIM, 1))
        parts.append(jnp.where(kv_lanes, moved, 0.0))
    return jnp.concatenate(parts, axis=0)


def _unstack_heads(y, g):
    _, gvec = _group_lanes(g)
    heads = []
    for j in range(SW_GROUP):
        yj = y[j * BLOCK:(j + 1) * BLOCK]
        heads.append(jnp.where(gvec == j % 2, yj, pltpu.roll(yj, HEAD_DIM, 1)))
    pairs = [jnp.where(_head_lanes(0), heads[2 * p], heads[2 * p + 1]) for p in range(SW_GROUP // 2)]
    return jnp.concatenate(pairs, axis=1)


def _per_head_col(values):
    return jnp.concatenate([jnp.zeros((BLOCK, 1), F32) + v for v in values], axis=0)


def _swa_scores(qs, kp, kc, bias_ref, n):
    row = jnp.bitwise_and(lax.broadcasted_iota(jnp.int32, (GROUP_ROWS, BLOCK), 0), BLOCK - 1)
    col = lax.broadcasted_iota(jnp.int32, (GROUP_ROWS, BLOCK), 1)
    bias = bias_ref[...].reshape(GROUP_ROWS, 2 * BLOCK)
    s1 = _dot_nt(qs, kp) + bias[:, :BLOCK]
    s2 = _dot_nt(qs, kc) + bias[:, BLOCK:]
    no_prev = jnp.where(n > 0, 0, BLOCK)
    s1 = jnp.where(col > row + no_prev, s1, NEG)
    s2 = jnp.where(col <= row, s2, NEG)
    return s1, s2


def _swa_specs(s):
    q_blk = 3 * SB_W // (2 * LANES)
    k_blk = (3 * SB_W + SW_W) // LANES
    return (pl.BlockSpec((s, 2 * LANES), lambda g: (0, q_blk + g)),
            pl.BlockSpec((s, LANES), lambda g: (0, k_blk)),
            pl.BlockSpec((s, LANES), lambda g: (0, k_blk + 1)))


def _swa_fwd(qkv, bias, sinks, name, comm=None):
    s = qkv.shape[0]
    nb = s // BLOCK

    def body(sink_ref, q_ref, k_ref, v_ref, bias_ref, o_ref, lse_ref):
        g = pl.program_id(0)
        sink = _per_head_col([sink_ref[SW_GROUP * g + j] for j in range(SW_GROUP)])
        lane = lax.broadcasted_iota(jnp.int32, (1, LANES), 1)

        def step(n, carry):
            r0 = pl.multiple_of(n * BLOCK, BLOCK)
            p0 = pl.multiple_of(jnp.maximum(n - 1, 0) * BLOCK, BLOCK)
            cur, prev = pl.ds(r0, BLOCK), pl.ds(p0, BLOCK)
            qs = _stack_heads(q_ref[cur, :].astype(F32) * SCALE, g).astype(MXU)
            s1, s2 = _swa_scores(qs, k_ref[prev, :], k_ref[cur, :], bias_ref, n)
            m = jnp.maximum(jnp.max(jnp.maximum(s1, s2), axis=1, keepdims=True), sink)
            e1 = jnp.exp(s1 - m)
            e2 = jnp.exp(s2 - m)
            den = jnp.sum(e1 + e2, axis=1, keepdims=True) + jnp.exp(sink - m)
            o = _dot((e1 / den).astype(MXU), v_ref[prev, :]) + _dot((e2 / den).astype(MXU), v_ref[cur, :])
            o_ref[cur, :] = _unstack_heads(o, g).astype(o_ref.dtype)
            lse = m + jnp.log(den)
            lse_row = jnp.zeros((BLOCK, LANES), F32)
            for j in range(SW_GROUP):
                lse_row = jnp.where(lane == j, lse[j * BLOCK:(j + 1) * BLOCK], lse_row)
            lse_ref[cur, :] = lse_row
            return carry

        lax.fori_loop(0, nb, step, 0, unroll=2)

    outs, couts = _call(
        body,
        grid=(SW_KV,),
        in_specs=[pl.BlockSpec(memory_space=pltpu.SMEM), *_swa_specs(s),
                  pl.BlockSpec((SW_GROUP, BLOCK, 2 * BLOCK), lambda g: (g, 0, 0))],
        out_specs=[pl.BlockSpec((s, 2 * LANES), lambda g: (0, g)),
                   pl.BlockSpec((None, s, LANES), lambda g: (g, 0, 0))],
        out_shape=[jax.ShapeDtypeStruct((s, SW_W), MXU), jax.ShapeDtypeStruct((SW_KV, s, LANES), F32)],
        args=(sinks, qkv, qkv, qkv, bias), name=name, comm=comm)
    return tuple(outs) if comm is None else (tuple(outs), couts)


def _swa_bwd(qkv, bias, sinks, do, lse, name, comm=None):
    s = qkv.shape[0]
    nb = s // BLOCK

    def body(sink_ref, q_ref, k_ref, v_ref, bias_ref, do_ref, lse_ref,
             dq_ref, dk_ref, dv_ref, dbias_ref, dsink_ref, dk_acc, dv_acc):
        g = pl.program_id(0)
        sink = _per_head_col([sink_ref[SW_GROUP * g + j] for j in range(SW_GROUP)])
        lane = lax.broadcasted_iota(jnp.int32, (1, LANES), 1)

        @pl.when(g == 0)
        def _():
            dk_acc[...] = jnp.zeros_like(dk_acc)
            dv_acc[...] = jnp.zeros_like(dv_acc)

        dbias_ref[...] = jnp.zeros_like(dbias_ref)

        def step(n, dsink_rows):
            r0 = pl.multiple_of(n * BLOCK, BLOCK)
            p0 = pl.multiple_of(jnp.maximum(n - 1, 0) * BLOCK, BLOCK)
            cur, prev = pl.ds(r0, BLOCK), pl.ds(p0, BLOCK)
            qs = _stack_heads(q_ref[cur, :].astype(F32) * SCALE, g).astype(MXU)
            dos = _stack_heads(do_ref[cur, :].astype(F32), g).astype(MXU)
            kp, kc, vp, vc = k_ref[prev, :], k_ref[cur, :], v_ref[prev, :], v_ref[cur, :]
            lse_row = lse_ref[cur, :]
            lse = jnp.concatenate([jnp.sum(jnp.where(lane == j, lse_row, 0.0), axis=1, keepdims=True)
                                   for j in range(SW_GROUP)], axis=0)
            s1, s2 = _swa_scores(qs, kp, kc, bias_ref, n)
            pr1 = jnp.exp(s1 - lse)
            pr2 = jnp.exp(s2 - lse)
            dpr1 = _dot_nt(dos, vp)
            dpr2 = _dot_nt(dos, vc)
            delta = jnp.sum(pr1 * dpr1 + pr2 * dpr2, axis=1, keepdims=True)
            ds1 = pr1 * (dpr1 - delta)
            ds2 = pr2 * (dpr2 - delta)
            dbias_ref[:, :, :BLOCK] += ds1.reshape(SW_GROUP, BLOCK, BLOCK)
            dbias_ref[:, :, BLOCK:] += ds2.reshape(SW_GROUP, BLOCK, BLOCK)
            ds1b, ds2b = ds1.astype(MXU), ds2.astype(MXU)
            dq = _dot(ds1b, kp) + _dot(ds2b, kc)
            dq_ref[cur, :] = (_unstack_heads(dq, g) * SCALE).astype(dq_ref.dtype)
            dk_acc[prev, :] += _dot_tn(ds1b, qs)
            dk_acc[cur, :] += _dot_tn(ds2b, qs)
            dv_acc[prev, :] += _dot_tn(pr1.astype(MXU), dos)
            dv_acc[cur, :] += _dot_tn(pr2.astype(MXU), dos)
            return dsink_rows - jnp.exp(sink - lse) * delta

        rows = lax.fori_loop(0, nb, step, jnp.zeros((GROUP_ROWS, 1), F32), unroll=2)
        for j in range(SW_GROUP):
            dsink_ref[j] = jnp.broadcast_to(jnp.sum(rows[j * BLOCK:(j + 1) * BLOCK], axis=0, keepdims=True),
                                            (1, LANES))

        @pl.when(g == SW_KV - 1)
        def _():
            dk_ref[...] = dk_acc[...].astype(dk_ref.dtype)
            dv_ref[...] = dv_acc[...].astype(dv_ref.dtype)

    grp = pl.BlockSpec((s, 2 * LANES), lambda g: (0, g))
    kv_out = pl.BlockSpec((s, LANES), lambda g: (0, 0))
    bspec = pl.BlockSpec((SW_GROUP, BLOCK, 2 * BLOCK), lambda g: (g, 0, 0))
    outs, couts = _call(
        body,
        grid=(SW_KV,),
        in_specs=[pl.BlockSpec(memory_space=pltpu.SMEM), *_swa_specs(s), bspec, grp,
                  pl.BlockSpec((None, s, LANES), lambda g: (g, 0, 0))],
        out_specs=[grp, kv_out, kv_out, bspec, pl.BlockSpec((SW_GROUP, 1, LANES), lambda g: (g, 0, 0))],
        out_shape=[jax.ShapeDtypeStruct((s, SW_W), MXU),
                   jax.ShapeDtypeStruct((s, LANES), MXU),
                   jax.ShapeDtypeStruct((s, LANES), MXU),
                   jax.ShapeDtypeStruct((SW_HEADS, BLOCK, 2 * BLOCK), F32),
                   jax.ShapeDtypeStruct((SW_HEADS, 1, LANES), F32)],
        scratch_shapes=[pltpu.VMEM((s, LANES), F32), pltpu.VMEM((s, LANES), F32)],
        args=(sinks, qkv, qkv, qkv, bias, do, lse), name=name, comm=comm)
    return tuple(outs) if comm is None else (tuple(outs), couts)


class _NoPlan:
    def comm(self, name):
        return None

    def done(self, name, outs):
        pass

    def grad(self, layer, name, value):
        pass


def _dw_in(dqkv, dgates, h, out_dtype, name):
    s, d = h.shape
    nq, ng = dqkv.shape[1], dgates.shape[1]
    t = math.gcd(nq, ng)
    n_first = nq // t

    def body(q_ref, g_ref, h_ref, o_ref):
        i = pl.program_id(0)

        @pl.when(i < n_first)
        def _():
            o_ref[...] = _dot_tn(q_ref[...], h_ref[...]).astype(o_ref.dtype)

        @pl.when(i >= n_first)
        def _():
            o_ref[...] = _dot_tn(g_ref[...], h_ref[...]).astype(o_ref.dtype)

    outs, _ = _call(
        body, grid=((nq + ng) // t,),
        in_specs=[pl.BlockSpec((s, t), lambda i: (0, jnp.minimum(i, n_first - 1))),
                  pl.BlockSpec((s, t), lambda i: (0, jnp.maximum(i - n_first, 0))),
                  pl.BlockSpec((s, d), lambda i: (0, 0))],
        out_specs=[pl.BlockSpec((t, d), lambda i: (i, 0))],
        out_shape=[jax.ShapeDtypeStruct((nq + ng, d), out_dtype)],
        args=(dqkv, dgates, h), name=name)
    return outs[0]


def _run(plan, fn, *args, name, **kw):
    comm = plan.comm(name)
    if comm is None:
        return fn(*args, name=name, **kw)
    res, outs = fn(*args, name=name, comm=comm, **kw)
    plan.done(name, outs)
    return res


def _norm_bwd(dh, x, dres, r, g):
    xhat = x * r
    dxhat = dh * g
    dx = dres + r * (dxhat - xhat * jnp.mean(dxhat * xhat, axis=-1, keepdims=True))
    return dx, dx, jnp.sum(dh * xhat, axis=0, keepdims=True)


def _residual_norm(acc, res, g):
    x = res + acc
    r = lax.rsqrt(jnp.mean(x * x, axis=-1, keepdims=True) + EPS)
    return x, (x * r) * g, r


def _layer_fwd(x, p, w, g_mix, g_mlp, g_pe, sinks, bias, tag, plan, normed_x=None, next_g=None):
    h1, r1 = normed_x if normed_x is not None else _run(plan, _rms_fwd, x, g_mix, name=f"rms_mix_{tag}")
    qkv = _run(plan, _mm, h1, w["w_qkv_t"], tb=True, out_dtypes=(MXU,), tn=QKV_COLS, name=f"proj_qkv_{tag}")
    gates = _run(plan, _mm, h1, w["w_gate_t"], tb=True, out_dtypes=(MXU,), tn=2048, name=f"proj_gate_{tag}")
    oa = _run(plan, _sb_fwd, qkv, name=f"sb_fwd_{tag}")
    ob, lse = _run(plan, _swa_fwd, qkv, bias, sinks, name=f"swa_fwd_{tag}")
    merged = _mix_fwd(oa, ob, w["w_up_a_t"], w["w_up_b_t"], gates, f"mix_fwd_{tag}")
    d = x.shape[1]
    normed = (F32, MXU, (F32, "col"))
    x1, h2, r2 = _run(plan, _mm, merged, w["w_o"], extras=(x, g_mlp), epi=_residual_norm, out_dtypes=normed,
                      tn=d, name=f"out_proj_{tag}")
    u, act = _run(plan, _mm, h2, w["w_ff1_t"], tb=True,
                  epi=lambda acc: (acc, jnp.square(jnp.maximum(acc, 0.0))),
                  out_dtypes=(MXU, MXU), tn=2048, name=f"ff1_{tag}")
    x2, h3, r3 = _run(plan, _mm, act, w["w_ff2"], extras=(x1, g_pe), epi=_residual_norm, out_dtypes=normed,
                      tn=d, name=f"ff2_{tag}")
    x3 = _ple(p, w["w_pe_t"], h3, w["w_pg"], x2, backward=False, name=f"ple_fwd_{tag}", next_g=next_g)
    next_normed = None
    if next_g is not None:
        x3, next_normed = x3[0], (x3[1], x3[2])
    saved = dict(x=x, h1=h1, r1=r1, gates=gates, qkv=qkv, lse=lse, oa=oa, ob=ob, merged=merged,
                 x1=x1, h2=h2, r2=r2, u=u, act=act, x2=x2, h3=h3, r3=r3)
    return x3, saved, next_normed


def _layer_bwd(dx3, sv, p, w, g_mix, g_mlp, g_pe, sinks, bias, layer, plan):
    tag = f"l{layer}"
    gw = {}
    wire = (WIRE,)

    def dw(name, a, b):
        gw[name] = _run(plan, _mm, a, b, ta=True, out_dtypes=wire, tk=2048, name=f"d{name}_{tag}")
        plan.grad(layer, name, gw[name])

    dpe, dgt = _ple(p, w["w_pe_t"], sv["h3"], w["w_pg"], dx3, backward=True, name=f"ple_bwd_{tag}")
    dw("w_pe", dpe, p)
    dw("w_pg", sv["h3"], dgt)
    d = dx3.shape[1]
    grads = (F32, MXU, (F32, "rowsum"))
    dx2, dx2b, dg_pe = _run(plan, _mm, dgt, w["w_pg"], tb=True, extras=(sv["x2"], dx3, sv["r3"], g_pe),
                            epi=_norm_bwd, out_dtypes=grads, tm=512, tn=d, name=f"dh_pe_{tag}")
    dw("w_ff2", sv["act"], dx2b)
    du = _run(plan, _mm, dx2b, w["w_ff2"], tb=True, extras=(sv["u"],),
              epi=lambda acc, u: acc * (2.0 * jnp.maximum(u.astype(F32), 0.0)), out_dtypes=(MXU,),
              tn=2048, name=f"dact_{tag}")
    dw("w_ff1", du, sv["h2"])
    dx1, dx1b, dg_mlp = _run(plan, _mm, du, w["w_ff1_t"], extras=(sv["x1"], dx2, sv["r2"], g_mlp),
                             epi=_norm_bwd, out_dtypes=grads, tm=1024, tn=d, name=f"dh_mlp_{tag}")
    dw("w_o", sv["merged"], dx1b)
    dya, dyb, dgates = _mix_bwd(dx1b, w["w_o"], sv["oa"], sv["ob"], w["w_up_a_t"], w["w_up_b_t"],
                                sv["gates"], f"mix_bwd_{tag}")
    dw("w_up_a", dya, sv["oa"])
    dw("w_up_b", dyb, sv["ob"])
    doa = _run(plan, _mm, dya, w["w_up_a_t"], out_dtypes=(MXU,), name=f"do_a_{tag}")
    dob = _run(plan, _mm, dyb, w["w_up_b_t"], out_dtypes=(MXU,), name=f"do_b_{tag}")
    dqb, dkb, dvb, dbias, dsink = _run(plan, _swa_bwd, sv["qkv"], bias, sinks, dob, sv["lse"],
                                       name=f"swa_bwd_{tag}")
    dqa, dka, dva = _run(plan, _sb_bwd, sv["qkv"], doa, name=f"sb_bwd_{tag}")
    dqkv = jnp.concatenate([dqa, dka, dva, dqb, dkb, dvb], axis=1)
    gw["w_in"] = _dw_in(dqkv, dgates, sv["h1"], WIRE, f"dw_in_{tag}")
    plan.grad(layer, "w_in", gw["w_in"])
    dh1 = _run(plan, _mm, dgates, w["w_gate_t"], name=f"dh_gate_{tag}")
    dx, _, dg_mix = _run(plan, _mm, dqkv, w["w_qkv_t"], tk=768, extras=(dh1, sv["x"], dx1, sv["r1"], g_mix),
                         epi=lambda acc, prev, *rest: _norm_bwd(acc + prev, *rest), out_dtypes=grads,
                         tm=512, tn=d, name=f"dh_qkv_{tag}")
    small = dict(g_mix=dg_mix, g_mlp=dg_mlp, g_pe=dg_pe, sinks=dsink[:, 0, 0], dbias=dbias)
    return dx, gw, small


def _local_step(x, p, target, weights, g_mix, g_mlp, g_pe, g_final, sinks, rel_bias, plan=None):
    plan = _NoPlan() if plan is None else plan
    depth = g_mix.shape[0]
    buckets = jnp.asarray(_bucket_table())
    bias = _build_bias(rel_bias, buckets, "build_bias")
    saved, wfull = [], []
    h, normed = x, None
    for l in range(depth):
        wfull.append(weights(l))
        next_g = g_mix[l + 1:l + 2] if l + 1 < depth else None
        h, sv, normed = _layer_fwd(h, p[l], wfull[l], g_mix[l:l + 1], g_mlp[l:l + 1], g_pe[l:l + 1],
                                   sinks[l], bias, f"l{l}", plan, normed, next_g)
        saved.append(sv)
    loss_row, dx, dg_final = _loss_head(h, g_final[None, :], target, "loss_head")
    gws = [None] * depth
    smalls = [None] * depth
    for l in reversed(range(depth)):
        dx, gws[l], smalls[l] = _layer_bwd(dx, saved[l], p[l], wfull[l], g_mix[l:l + 1], g_mlp[l:l + 1],
                                           g_pe[l:l + 1], sinks[l], bias, l, plan)
    drel = _bias_grad([sm["dbias"] for sm in smalls], buckets, "bias_grad")[:, 0, :N_BUCKETS].T
    small = dict(
        g_mix=jnp.concatenate([sm["g_mix"] for sm in smalls], axis=0),
        g_mlp=jnp.concatenate([sm["g_mlp"] for sm in smalls], axis=0),
        g_pe=jnp.concatenate([sm["g_pe"] for sm in smalls], axis=0),
        g_final=dg_final[0],
        sinks=jnp.stack([sm["sinks"] for sm in smalls], axis=0),
        rel_bias=drel,
    )
    return loss_row, dx, gws, small


MESH_ID = pl.DeviceIdType.MESH


def _position():
    return lax.axis_index("x"), lax.axis_index("y"), lax.axis_index("c")


def _gather_comm(shards):
    n = len(shards)

    def copies(pos, x_refs, out_refs, sems):
        send_sems, recv_sems, local_sems = sems
        x, y, c = pos
        me, sibling = (x, y, c), (x, y, 1 - c)
        chips = [(1 - x, y), (x, 1 - y), (1 - x, 1 - y)]

        def slot(a, px, py, pc):
            return out_refs[a].at[4 * px + 2 * py + pc]

        def copy(a, k, block, to, src=None):
            return pltpu.make_async_remote_copy(
                src_ref=slot(a, *block) if src is None else src, dst_ref=slot(a, *block),
                send_sem=send_sems.at[a, k], recv_sem=recv_sems.at[a, k],
                device_id=to, device_id_type=MESH_ID)

        mine = [pltpu.make_async_copy(x_refs[a], slot(a, *me), local_sems.at[a]) for a in range(n)]
        first = []
        for a in range(n):
            first.append(copy(a, 0, me, sibling, src=x_refs[a]))
            first += [copy(a, 1 + j, me, (*chip, c), src=x_refs[a]) for j, chip in enumerate(chips)]
        return me, sibling, chips, copy, mine, first

    def start(pos, x_refs, out_refs, sems):
        _, _, _, _, mine, first = copies(pos, x_refs, out_refs, sems)
        for cp in mine + first:
            cp.start()

    def finish(pos, x_refs, out_refs, sems):
        me, sibling, chips, copy, mine, first = copies(pos, x_refs, out_refs, sems)
        c = pos[2]
        passed = []
        for j, chip in enumerate(chips):
            for a in range(n):
                copy(a, 1 + j, (*chip, c), me).wait_recv()
                fwd = copy(a, 4 + j, (*chip, c), sibling)
                fwd.start()
                passed.append(fwd)
        for a in range(n):
            copy(a, 0, sibling, me).wait_recv()
            for j, chip in enumerate(chips):
                copy(a, 4 + j, (*chip, 1 - c), me).wait_recv()
        for cp in first + passed:
            cp.wait_send()
        for cp in mine:
            cp.wait()

    return _Comm(shards, [jax.ShapeDtypeStruct((N_DEV,) + s.shape, s.dtype) for s in shards],
                 [pltpu.SemaphoreType.DMA((n, 7)), pltpu.SemaphoreType.DMA((n, 7)),
                  pltpu.SemaphoreType.DMA((n,))], start, finish)


def _exchange_comm(arrays, n_slots, route):
    n = len(arrays)

    def copies(pos, in_refs, out_refs, sems):
        send_sems, recv_sems = sems
        out = []
        for a in range(n):
            for j in range(n_slots):
                src_slot, peer = route(pos, j)
                out.append(pltpu.make_async_remote_copy(
                    src_ref=in_refs[a].at[src_slot], dst_ref=out_refs[a].at[j],
                    send_sem=send_sems.at[a, j], recv_sem=recv_sems.at[a, j],
                    device_id=peer, device_id_type=MESH_ID))
        return out

    def start(pos, in_refs, out_refs, sems):
        for cp in copies(pos, in_refs, out_refs, sems):
            cp.start()

    def finish(pos, in_refs, out_refs, sems):
        for cp in copies(pos, in_refs, out_refs, sems):
            cp.wait()

    return _Comm(arrays, [jax.ShapeDtypeStruct((n_slots,) + g.shape[1:], g.dtype) for g in arrays],
                 [pltpu.SemaphoreType.DMA((n, n_slots)), pltpu.SemaphoreType.DMA((n, n_slots))], start, finish)


def _rs_sibling_comm(gs):
    return _exchange_comm(gs, 4, lambda pos, j: (2 * j + (1 - pos[2]), (pos[0], pos[1], 1 - pos[2])))


def _chip_of(k, x, y):
    return x ^ ((k + 1) & 1), y ^ (((k + 1) >> 1) & 1)


def _chip_partials(pos, gs, recvs, name):
    n = len(gs)

    def body(pos_ref, *refs):
        for a in range(n):
            refs[2 * n + a][...] = (refs[a][...].astype(F32) + refs[n + a][...].astype(F32)
                                    ).astype(refs[2 * n + a].dtype)

    def g_map(k, pos_ref):
        cx, cy = _chip_of(k, pos_ref[0], pos_ref[1])
        return (4 * cx + 2 * cy + pos_ref[2], 0, 0)

    def r_map(k, pos_ref):
        cx, cy = _chip_of(k, pos_ref[0], pos_ref[1])
        return (2 * cx + cy, 0, 0)

    slab = [(None,) + g.shape[1:] for g in gs]
    return pl.pallas_call(
        body,
        grid_spec=pltpu.PrefetchScalarGridSpec(
            num_scalar_prefetch=1,
            grid=(4,),
            in_specs=[pl.BlockSpec(sh, g_map) for sh in slab] + [pl.BlockSpec(sh, r_map) for sh in slab],
            out_specs=[pl.BlockSpec(sh, lambda k, pos_ref: (k, 0, 0)) for sh in slab],
        ),
        out_shape=[jax.ShapeDtypeStruct((4,) + g.shape[1:], g.dtype) for g in gs],
        compiler_params=_cparams(),
        name=name,
    )(pos, *gs, *recvs)


def _rs_chips_comm(parts):
    return _exchange_comm(parts, 3, lambda pos, k: (k, (*_chip_of(k, pos[0], pos[1]), pos[2])))


def _adamw_math(w, g, m, v):
    m = ADAM_B1 * m + (1.0 - ADAM_B1) * g
    v = ADAM_B2 * v + (1.0 - ADAM_B2) * (g * g)
    m_hat = m / (1.0 - ADAM_B1 ** ADAM_STEP)
    v_hat = v / (1.0 - ADAM_B2 ** ADAM_STEP)
    delta = -ADAM_LR * (m_hat / (jnp.sqrt(v_hat) + ADAM_EPS) + ADAM_WD * w)
    return delta, m, v


def _adamw_weight(parts, recvs, w, m, v, name, grad_t=False, comm=None):
    depth, a, b = w.shape
    ta = _tile(a, 288, unit=LANES if grad_t else 16)
    ni = a // ta
    g_block = (b, ta) if grad_t else (ta, b)

    def body(*refs):
        p_refs, r_refs = refs[:depth], refs[depth:2 * depth]
        w_ref, m_ref, v_ref = refs[2 * depth:2 * depth + 3]
        g_out, d_out, m_out, v_out = refs[2 * depth + 3:]
        layer = pl.program_id(0)
        g = jnp.zeros(g_block, F32)
        for l in range(depth):
            gl = p_refs[l][...].astype(F32)
            for k in range(3):
                gl = gl + r_refs[l][k].astype(F32)
            g = jnp.where(layer == l, gl, g)
        if grad_t:
            g = g.T
        delta, m_new, v_new = _adamw_math(w_ref[...], g, m_ref[...], v_ref[...])
        g_out[...] = g
        d_out[...] = delta
        m_out[...] = m_new
        v_out[...] = v_new

    def hold(l):
        return lambda layer, i: jnp.where(layer == l, i, jnp.where(layer < l, 0, ni - 1))

    def g_index(slot, f):
        if grad_t:
            return lambda layer, i: (slot, 0, f(layer, i))
        return lambda layer, i: (slot, f(layer, i), 0)

    p_specs = [pl.BlockSpec((None,) + g_block, g_index(3, hold(l))) for l in range(depth)]
    r_specs = [pl.BlockSpec((3,) + g_block, g_index(0, hold(l))) for l in range(depth)]
    row = pl.BlockSpec((None, ta, b), lambda layer, i: (layer, i, 0))
    outs, couts = _call(
        body,
        grid=(depth, ni),
        in_specs=p_specs + r_specs + [row, row, row],
        out_specs=[row] * 4,
        out_shape=[jax.ShapeDtypeStruct(w.shape, F32)] * 4,
        args=(*parts, *recvs, w, m, v), name=name, comm=comm)
    return outs if comm is None else (outs, couts)


def _adamw_replicated(gathered, w, m, v, name):
    r, lanes = w.shape

    def body(g_ref, w_ref, m_ref, v_ref, g_out, d_out, m_out, v_out):
        g = g_ref[0]
        for k in range(1, N_DEV):
            g = g + g_ref[k]
        delta, m_new, v_new = _adamw_math(w_ref[...], g, m_ref[...], v_ref[...])
        g_out[...] = g
        d_out[...] = delta
        m_out[...] = m_new
        v_out[...] = v_new

    return pl.pallas_call(
        body,
        out_shape=[jax.ShapeDtypeStruct((r, lanes), F32)] * 4,
        name=name,
    )(gathered, w, m, v)


def _wire_shard(name, shard):
    return (shard.T if name in COL_SHARDED else shard).astype(WIRE)


def _full_weight(gathered):
    return gathered.reshape(N_DEV * gathered.shape[1], gathered.shape[2])


def _to_slabs(gfull):
    return gfull.reshape(N_DEV, gfull.shape[0] // N_DEV, gfull.shape[1])


def _pack_small(arrs):
    rows = []
    for a in arrs:
        flat = a.astype(F32).reshape(-1)
        pad = (-flat.shape[0]) % LANES
        rows.append(jnp.pad(flat, (0, pad)).reshape(-1, LANES))
    packed = jnp.concatenate(rows, axis=0)
    return jnp.pad(packed, ((0, (-packed.shape[0]) % 8), (0, 0)))


def _unpack_small(packed, shapes):
    out, off = [], 0
    for shp in shapes:
        n = math.prod(shp)
        rows = -(-n // LANES)
        out.append(packed[off:off + rows].reshape(-1)[:n].reshape(shp))
        off += rows
    return out


def _of(layer, *names):
    return tuple((layer, n) for n in names)


MLP_W = ("w_ff1", "w_ff2", "w_pe", "w_pg")

GATHERS = (
    ("rms_mix_l0", _of(0, "w_in")),
    ("proj_qkv_l0", _of(0, "w_up_a", "w_up_b", "w_o")),
    ("proj_gate_l0", _of(0, "w_pe", "w_pg")),
    ("sb_fwd_l0", _of(0, "w_ff1", "w_ff2")),
    ("swa_fwd_l0", _of(1, "w_in")),
    ("sb_fwd_l1", _of(1, "w_up_a", "w_up_b", "w_o", "w_pe", "w_pg", "w_ff1")),
    ("swa_fwd_l1", _of(1, "w_ff2")),
)
REDUCES = (
    (_of(1, *MLP_W), "dh_mlp_l1", "sb_bwd_l1"),
    (_of(1, "w_o", "w_up_a", "w_up_b"), "do_a_l1", "sb_bwd_l1"),
    (_of(1, "w_in"), "dw_ff2_l0", "swa_bwd_l0"),
    (_of(0, *MLP_W), "dh_mlp_l0", "sb_bwd_l0"),
    (_of(0, "w_o", "w_up_a", "w_up_b"), "do_a_l0", "sb_bwd_l0"),
    (_of(0, "w_in"), "dh_gate_l0", "dh_qkv_l0"),
)


def _merge_comms(comms):
    if len(comms) == 1:
        return comms[0]

    def cuts(counts):
        edges = [0]
        for c in counts:
            edges.append(edges[-1] + c)
        return [slice(a, b) for a, b in zip(edges[:-1], edges[1:])]

    s_in = cuts([len(c.inputs) for c in comms])
    s_out = cuts([len(c.out_shapes) for c in comms])
    s_sem = cuts([len(c.sems) for c in comms])

    def start(pos, cin, cout, csem):
        for c, i, o, s in zip(comms, s_in, s_out, s_sem):
            c.start(pos, cin[i], cout[o], csem[s])

    def finish(pos, cin, cout, csem):
        for c, i, o, s in zip(comms, s_in, s_out, s_sem):
            c.finish(pos, cin[i], cout[o], csem[s])

    return _Comm(sum([c.inputs for c in comms], []), sum([c.out_shapes for c in comms], []),
                 sum([c.sems for c in comms], []), start, finish)


class _LayerWeights:
    def __init__(self, full, layer):
        self.full, self.layer, self.cache = full, layer, {}

    def __getitem__(self, name):
        if name not in self.cache:
            if name == "w_qkv_t":
                self.cache[name] = self.full[(self.layer, "w_in")][:QKV_COLS]
            elif name == "w_gate_t":
                self.cache[name] = self.full[(self.layer, "w_in")][QKV_COLS:]
            else:
                base = name[:-2] if name.endswith("_t") else name
                assert (base in COL_SHARDED) == name.endswith("_t"), name
                self.cache[name] = self.full[(self.layer, base)]
        return self.cache[name]


class _Plan:
    def __init__(self, w_sh, pos):
        self.w_sh = dict(zip(WEIGHTS, w_sh))
        self.pos = pos
        self.full, self.gw, self.parts, self.recv = {}, {}, {}, {}
        self.slabs = {}
        self.hosted = {}
        for i, (host, _) in enumerate(GATHERS):
            self.hosted.setdefault(host, []).append(("gather", i))
        for i, (_, sib_host, chip_host) in enumerate(REDUCES):
            self.hosted.setdefault(sib_host, []).append(("sibling", i))
            self.hosted.setdefault(chip_host, []).append(("chips", i))

    def _gather(self, i):
        return _gather_comm([_wire_shard(n, self.w_sh[n][layer]) for layer, n in GATHERS[i][1]])

    def _gathered(self, i, outs):
        for (layer, n), g in zip(GATHERS[i][1], outs):
            self.full[(layer, n)] = _full_weight(g)

    def weights(self, layer):
        return _LayerWeights(self.full, layer)

    def grad(self, layer, name, value):
        self.gw[(layer, name)] = value

    def _sibling(self, i):
        self.slabs[i] = [_to_slabs(self.gw[item]) for item in REDUCES[i][0]]
        return _rs_sibling_comm(self.slabs[i])

    def _sibling_done(self, i, outs):
        parts = _chip_partials(self.pos, self.slabs[i], outs, f"chip_partials_{i}")
        for item, part in zip(REDUCES[i][0], parts):
            self.parts[item] = part

    def _chips(self, i):
        return _rs_chips_comm([self.parts[item] for item in REDUCES[i][0]])

    def _chips_done(self, i, outs):
        for item, r in zip(REDUCES[i][0], outs):
            self.recv[item] = r

    def comm(self, name):
        if name not in self.hosted:
            return None
        make = {"gather": self._gather, "sibling": self._sibling, "chips": self._chips}
        return _merge_comms([make[kind](i) for kind, i in self.hosted[name]])

    def done(self, name, outs):
        took = {"gather": self._gathered, "sibling": self._sibling_done, "chips": self._chips_done}
        off = 0
        for kind, i in self.hosted[name]:
            n = len(GATHERS[i][1]) if kind == "gather" else len(REDUCES[i][0])
            took[kind](i, outs[off:off + n])
            off += n


def kernel(x, p, w_in, w_up_a, w_up_b, w_o, w_ff1, w_ff2, w_pe, w_pg, g_mix, g_mlp, g_pe, g_final, sinks, rel_bias, loss_target, m_w_in, m_w_up_a, m_w_up_b, m_w_o, m_w_ff1, m_w_ff2, m_w_pe, m_w_pg, m_g_mix, m_g_mlp, m_g_pe, m_g_final, m_sinks, m_rel_bias, v_w_in, v_w_up_a, v_w_up_b, v_w_o, v_w_ff1, v_w_ff2, v_w_pe, v_w_pg, v_g_mix, v_g_mlp, v_g_pe, v_g_final, v_sinks, v_rel_bias):
    w_sh = [w_in, w_up_a, w_up_b, w_o, w_ff1, w_ff2, w_pe, w_pg]
    m_sh = [m_w_in, m_w_up_a, m_w_up_b, m_w_o, m_w_ff1, m_w_ff2, m_w_pe, m_w_pg]
    v_sh = [v_w_in, v_w_up_a, v_w_up_b, v_w_o, v_w_ff1, v_w_ff2, v_w_pe, v_w_pg]
    depth = w_in.shape[0]
    assert depth == 2 and x.shape[-1] * 2 + QKV_COLS == w_in.shape[2] * N_DEV

    px, py, pc = _position()
    plan = _Plan(w_sh, jnp.stack([px, py, pc]).astype(jnp.int32))
    loss_row, grad_x, _, small = _local_step(
        x[0], p[:, 0], loss_target[0], plan.weights, g_mix, g_mlp, g_pe, g_final, sinks, rel_bias, plan=plan)

    small_g = _pack_small([small[n] for n in SMALL] + [loss_row[0, :1]])
    grad_w, delta_w, new_m, new_v = [], [], [], []
    for a, name in enumerate(WEIGHTS):
        parts = [plan.parts[(l, name)] for l in range(depth)]
        recvs = [plan.recv[(l, name)] for l in range(depth)]
        if name == "w_in":
            flip = lambda t: t.transpose(0, 2, 1)
            outs, (small_all,) = _adamw_weight(parts, recvs, flip(w_sh[a]), flip(m_sh[a]), flip(v_sh[a]),
                                               f"adamw_{name}", comm=_gather_comm([small_g]))
            outs = [flip(o) for o in outs]
        else:
            outs = _adamw_weight(parts, recvs, w_sh[a], m_sh[a], v_sh[a], f"adamw_{name}",
                                 grad_t=name in COL_SHARDED)
        for lst, o in zip((grad_w, delta_w, new_m, new_v), outs):
            lst.append(o)

    small_w = [g_mix, g_mlp, g_pe, g_final, sinks, rel_bias]
    small_m = [m_g_mix, m_g_mlp, m_g_pe, m_g_final, m_sinks, m_rel_bias]
    small_v = [v_g_mix, v_g_mlp, v_g_pe, v_g_final, v_sinks, v_rel_bias]
    small_shapes = [a.shape for a in small_w] + [(1,)]
    zero = jnp.zeros((1,), F32)
    packed_s = _adamw_replicated(small_all, _pack_small(small_w + [zero]), _pack_small(small_m + [zero]),
                                 _pack_small(small_v + [zero + 1.0]), "adamw_replicated")
    sg, sd, sm, sv = [_unpack_small(t, small_shapes) for t in packed_s]
    loss = sg[-1][0]

    return (loss, grad_x[None], *grad_w, *sg[:-1], *delta_w, *sd[:-1], *new_m, *sm[:-1], *new_v, *sv[:-1])
```

```python
import functools
import math

import numpy as np
import jax
import jax.numpy as jnp
from jax import lax
from jax.experimental import pallas as pl
from jax.experimental.pallas import tpu as pltpu

F32 = jnp.float32
MXU = jnp.bfloat16
WIRE = jnp.bfloat16

HEAD_DIM = 64
SB_HEADS = 8
SW_HEADS = 8
SW_KV = 2
SW_GROUP = SW_HEADS // SW_KV
BLOCK = 128
N_BUCKETS = 32
MAX_DISTANCE = 128
EPS = 1e-6
SCALE = HEAD_DIM ** -0.5
SB_W = SB_HEADS * HEAD_DIM
SW_W = SW_HEADS * HEAD_DIM
QKV_COLS = 3 * SB_W + SW_W + 2 * SW_KV * HEAD_DIM
N_DEV = 8
LANES = 128
N_PAIR = SB_HEADS // 2
NEG = -1e30

ADAM_LR = 0.001
ADAM_B1 = 0.9
ADAM_B2 = 0.999
ADAM_EPS = 1e-08
ADAM_WD = 0.01
ADAM_STEP = 10

VMEM_LIMIT = 48 * 1024 * 1024
SB_TQ = 256
SB_DEAD = -105.0
SB_SUB = 4

WEIGHTS = ("w_in", "w_up_a", "w_up_b", "w_o", "w_ff1", "w_ff2", "w_pe", "w_pg")
COL_SHARDED = ("w_in", "w_up_a", "w_up_b", "w_ff1", "w_pe")
SMALL = ("g_mix", "g_mlp", "g_pe", "g_final", "sinks", "rel_bias")


def _cparams(**kw):
    return pltpu.CompilerParams(vmem_limit_bytes=VMEM_LIMIT, **kw)


def _dot(a, b):
    return jnp.dot(a, b, preferred_element_type=F32)


def _dot_nt(a, b):
    return lax.dot_general(a, b, (((1,), (1,)), ((), ())), preferred_element_type=F32)


def _dot_tn(a, b):
    return lax.dot_general(a, b, (((0,), (0,)), ((), ())), preferred_element_type=F32)


def _tile(n, target, unit=LANES):
    if n <= target:
        return n
    t = (target // unit) * unit
    while t > unit and n % t:
        t -= unit
    assert n % t == 0, (n, target)
    return t


def _sigmoid(x):
    return 0.5 * jnp.tanh(0.5 * x) + 0.5


class _Comm:
    def __init__(self, inputs, out_shapes, sems, start, finish):
        self.inputs, self.out_shapes, self.sems = list(inputs), list(out_shapes), list(sems)
        self.start, self.finish = start, finish


def _call(body, *, grid, in_specs, out_specs, out_shape, scratch_shapes=(), args, name, comm=None):
    n_in, n_out, n_scr = len(in_specs), len(out_shape), len(scratch_shapes)
    if comm is None:
        outs = pl.pallas_call(body, grid=grid, in_specs=list(in_specs), out_specs=list(out_specs),
                              out_shape=list(out_shape), scratch_shapes=list(scratch_shapes),
                              compiler_params=_cparams(), name=name)(*args)
        return list(outs), None
    ci, co = len(comm.inputs), len(comm.out_shapes)
    any_spec = pl.BlockSpec(memory_space=pl.ANY)

    def wrapped(*refs):
        ins, cin = refs[:n_in], refs[n_in:n_in + ci]
        o0 = n_in + ci
        outs, cout = refs[o0:o0 + n_out], refs[o0 + n_out:o0 + n_out + co]
        s0 = o0 + n_out + co
        scr, csem = refs[s0:s0 + n_scr], refs[s0 + n_scr:]
        ids = [pl.program_id(d) for d in range(len(grid))]
        first = functools.reduce(jnp.logical_and, [i == 0 for i in ids])
        last = functools.reduce(jnp.logical_and, [i == g - 1 for i, g in zip(ids, grid)])
        pos = (lax.axis_index("x"), lax.axis_index("y"), lax.axis_index("c"))

        @pl.when(first)
        def _():
            comm.start(pos, cin, cout, csem)

        body(*ins, *outs, *scr)

        @pl.when(last)
        def _():
            comm.finish(pos, cin, cout, csem)

    outs = pl.pallas_call(wrapped, grid=grid, in_specs=list(in_specs) + [any_spec] * ci,
                          out_specs=list(out_specs) + [any_spec] * co,
                          out_shape=list(out_shape) + comm.out_shapes,
                          scratch_shapes=list(scratch_shapes) + comm.sems,
                          compiler_params=_cparams(), name=name)(*args, *comm.inputs)
    return list(outs[:n_out]), list(outs[n_out:])


def _accumulate(o_ref, value, first):
    @pl.when(first)
    def _():
        o_ref[...] = value

    @pl.when(jnp.logical_not(first))
    def _():
        o_ref[...] += value


def _mm(a, b, *, ta=False, tb=False, extras=(), epi=None, out_dtypes=(F32,),
        tm=1024, tn=1024, tk=1024, b_rows=None, name, comm=None):
    if ta:
        kdim, m = a.shape
    else:
        m, kdim = a.shape
    rows = b.shape[0] if b_rows is None else b_rows
    assert rows <= b.shape[0]
    n = rows if tb else b.shape[1]
    assert (b.shape[1] if tb else rows) == kdim
    tm, tn, tk = _tile(m, tm), _tile(n, tn), _tile(kdim, tk)
    nk = kdim // tk
    n_ex, n_out = len(extras), len(out_dtypes)

    a_spec = (pl.BlockSpec((tk, tm), lambda i, j, k: (k, i)) if ta
              else pl.BlockSpec((tm, tk), lambda i, j, k: (i, k)))
    b_spec = (pl.BlockSpec((tn, tk), lambda i, j, k: (j, k)) if tb
              else pl.BlockSpec((tk, tn), lambda i, j, k: (k, j)))
    ex_specs = []
    for e in extras:
        assert e.shape in ((m, n), (1, n), (m, 1)), (e.shape, m, n)
        if e.shape == (m, n):
            ex_specs.append(pl.BlockSpec((tm, tn), lambda i, j, k: (i, j)))
        elif e.shape[0] == 1:
            ex_specs.append(pl.BlockSpec((1, tn), lambda i, j, k: (0, j)))
        else:
            ex_specs.append(pl.BlockSpec((tm, 1), lambda i, j, k: (i, 0)))
    out_specs, out_shape, row_sums = [], [], []
    for dt in out_dtypes:
        kind = dt[1] if isinstance(dt, tuple) else "tile"
        row_sums.append(kind == "rowsum")
        if kind == "tile":
            out_specs.append(pl.BlockSpec((tm, tn), lambda i, j, k: (i, j)))
            out_shape.append(jax.ShapeDtypeStruct((m, n), dt))
            continue
        assert tn == n, "per-row and summed outputs need whole rows in one tile"
        if kind == "col":
            out_specs.append(pl.BlockSpec((tm, 1), lambda i, j, k: (i, 0)))
            out_shape.append(jax.ShapeDtypeStruct((m, 1), dt[0]))
        else:
            out_specs.append(pl.BlockSpec((1, tn), lambda i, j, k: (0, 0)))
            out_shape.append(jax.ShapeDtypeStruct((1, n), dt[0]))

    def body(a_ref, b_ref, *rest):
        ex_refs = rest[:n_ex]
        out_refs = rest[n_ex:n_ex + n_out]
        acc = rest[-1]
        k = pl.program_id(2)
        first_rows = pl.program_id(0) == 0

        def prod():
            av = a_ref[...].astype(MXU)
            bv = b_ref[...].astype(MXU)
            return _dot_tn(av, bv) if ta else (_dot_nt(av, bv) if tb else _dot(av, bv))

        def finish(res):
            if epi is not None:
                res = epi(res, *[e[...] for e in ex_refs])
            if not isinstance(res, tuple):
                res = (res,)
            for o_ref, r, summed in zip(out_refs, res, row_sums):
                if summed:
                    _accumulate(o_ref, r.astype(o_ref.dtype), first_rows)
                else:
                    o_ref[...] = r.astype(o_ref.dtype)

        if nk == 1:
            finish(prod())
            return

        @pl.when(k == 0)
        def _():
            acc[...] = prod()

        @pl.when(jnp.logical_and(k > 0, k < nk - 1))
        def _():
            acc[...] += prod()

        @pl.when(k == nk - 1)
        def _():
            finish(acc[...] + prod())

    outs, couts = _call(
        body,
        grid=(m // tm, n // tn, nk),
        in_specs=[a_spec, b_spec] + ex_specs,
        out_specs=out_specs,
        out_shape=out_shape,
        scratch_shapes=[pltpu.VMEM((tm, tn), F32)],
        args=(a, b, *extras), name=name, comm=comm)
    res = outs[0] if n_out == 1 else tuple(outs)
    return res if comm is None else (res, couts)


def _rms_fwd(x, g, name, comm=None):
    s, d = x.shape
    tr = _tile(s, 256)

    def body(x_ref, g_ref, h_ref, r_ref):
        xf = x_ref[...]
        r = lax.rsqrt(jnp.mean(xf * xf, axis=-1, keepdims=True) + EPS)
        h_ref[...] = ((xf * r) * g_ref[...]).astype(h_ref.dtype)
        r_ref[...] = r

    outs, couts = _call(
        body,
        grid=(s // tr,),
        in_specs=[pl.BlockSpec((tr, d), lambda i: (i, 0)), pl.BlockSpec((1, d), lambda i: (0, 0))],
        out_specs=[pl.BlockSpec((tr, d), lambda i: (i, 0)), pl.BlockSpec((tr, 1), lambda i: (i, 0))],
        out_shape=[jax.ShapeDtypeStruct((s, d), MXU), jax.ShapeDtypeStruct((s, 1), F32)],
        args=(x, g), name=name, comm=comm)
    return tuple(outs) if comm is None else (tuple(outs), couts)


def _loss_head(x, g, target, name):
    s, d = x.shape
    tr = _tile(s, 256)

    def body(x_ref, g_ref, t_ref, loss_ref, dx_ref, dg_ref):
        @pl.when(pl.program_id(0) == 0)
        def _():
            dg_ref[...] = jnp.zeros_like(dg_ref)
            loss_ref[...] = jnp.zeros_like(loss_ref)

        xf = x_ref[...]
        gv = g_ref[...]
        r = lax.rsqrt(jnp.mean(xf * xf, axis=-1, keepdims=True) + EPS)
        xhat = xf * r
        err = xhat * gv - t_ref[...]
        loss_ref[...] += 0.5 * jnp.sum(jnp.mean(err * err, axis=-1, keepdims=True), axis=0, keepdims=True)
        dy = err * (1.0 / d)
        dxhat = dy * gv
        mean = jnp.mean(dxhat * xhat, axis=-1, keepdims=True)
        dx_ref[...] = r * (dxhat - xhat * mean)
        dg_ref[...] += jnp.sum(dy * xhat, axis=0, keepdims=True)

    row = pl.BlockSpec((tr, d), lambda i: (i, 0))
    vec = pl.BlockSpec((1, d), lambda i: (0, 0))
    return pl.pallas_call(
        body,
        grid=(s // tr,),
        in_specs=[row, vec, row],
        out_specs=[pl.BlockSpec((1, LANES), lambda i: (0, 0)), row, vec],
        out_shape=[jax.ShapeDtypeStruct((1, LANES), F32), jax.ShapeDtypeStruct((s, d), F32),
                   jax.ShapeDtypeStruct((1, d), F32)],
        compiler_params=_cparams(),
        name=name,
    )(x, g, target)


def _mix_fwd(oa, ob, wa_t, wb_t, gates, name):
    s, kd = oa.shape
    d = wa_t.shape[0]
    tm, tn = _tile(s, 1024), _tile(d, 1024)
    nj = d // tn

    def body(oa_ref, ob_ref, wa_ref, wb_ref, ga_ref, gb_ref, out_ref):
        ya = _dot_nt(oa_ref[...], wa_ref[...])
        yb = _dot_nt(ob_ref[...], wb_ref[...])
        out_ref[...] = (_sigmoid(ga_ref[...].astype(F32)) * ya
                        + _sigmoid(gb_ref[...].astype(F32)) * yb).astype(out_ref.dtype)

    o_spec = pl.BlockSpec((tm, kd), lambda i, j: (i, 0))
    w_spec = pl.BlockSpec((tn, kd), lambda i, j: (j, 0))
    return pl.pallas_call(
        body,
        grid=(s // tm, nj),
        in_specs=[o_spec, o_spec, w_spec, w_spec,
                  pl.BlockSpec((tm, tn), lambda i, j: (i, j)),
                  pl.BlockSpec((tm, tn), lambda i, j: (i, j + nj))],
        out_specs=pl.BlockSpec((tm, tn), lambda i, j: (i, j)),
        out_shape=jax.ShapeDtypeStruct((s, d), MXU),
        compiler_params=_cparams(),
        name=name,
    )(oa, ob, wa_t, wb_t, gates, gates)


def _mix_bwd(dx, w_o, oa, ob, wa_t, wb_t, gates, name):
    s, kd = oa.shape
    d = wa_t.shape[0]
    tm = _tile(s, 512)

    def body(dx_ref, wo_ref, oa_ref, ob_ref, wa_ref, wb_ref, g_ref, dya_ref, dyb_ref, dg_ref):
        dm = _dot_nt(dx_ref[...], wo_ref[...])
        ya = _dot_nt(oa_ref[...], wa_ref[...])
        yb = _dot_nt(ob_ref[...], wb_ref[...])
        sa = _sigmoid(g_ref[:, :d].astype(F32))
        sb = _sigmoid(g_ref[:, d:].astype(F32))
        dya_ref[...] = (dm * sa).astype(dya_ref.dtype)
        dyb_ref[...] = (dm * sb).astype(dyb_ref.dtype)
        dg_ref[:, :d] = (dm * ya * sa * (1.0 - sa)).astype(dg_ref.dtype)
        dg_ref[:, d:] = (dm * yb * sb * (1.0 - sb)).astype(dg_ref.dtype)

    def rows(width):
        return pl.BlockSpec((tm, width), lambda i: (i, 0))

    def whole(arr):
        return pl.BlockSpec(arr.shape, lambda i: (0, 0))

    return pl.pallas_call(
        body,
        grid=(s // tm,),
        in_specs=[rows(d), whole(w_o), rows(kd), rows(kd), whole(wa_t), whole(wb_t), rows(2 * d)],
        out_specs=[rows(d), rows(d), rows(2 * d)],
        out_shape=[jax.ShapeDtypeStruct((s, d), MXU), jax.ShapeDtypeStruct((s, d), MXU),
                   jax.ShapeDtypeStruct((s, 2 * d), MXU)],
        compiler_params=_cparams(),
        name=name,
    )(dx, w_o, oa, ob, wa_t, wb_t, gates)


def _ple(p, w_pe_t, h, w_pg, other, *, backward, name, next_g=None):
    s, kp = p.shape
    d = w_pe_t.shape[0]
    tm, tn = _tile(s, 1024), _tile(d, 1024)
    with_norm = next_g is not None
    assert not (with_norm and (backward or tn != d))

    def body(p_ref, wpe_ref, h_ref, wpg_ref, other_ref, *rest):
        out_refs = rest[1:] if with_norm else rest
        pe = _dot_nt(p_ref[...].astype(MXU), wpe_ref[...])
        gt = _dot(h_ref[...], wpg_ref[...])
        sg = _sigmoid(gt)
        if backward:
            dout = other_ref[...]
            out_refs[0][...] = (dout * sg).astype(out_refs[0].dtype)
            out_refs[1][...] = (dout * pe * sg * (1.0 - sg)).astype(out_refs[1].dtype)
        elif with_norm:
            x_new, h_new, r_new = _residual_norm(pe * sg, other_ref[...], rest[0][...])
            out_refs[0][...] = x_new
            out_refs[1][...] = h_new.astype(out_refs[1].dtype)
            out_refs[2][...] = r_new
        else:
            out_refs[0][...] = other_ref[...] + pe * sg

    t_spec = pl.BlockSpec((tm, tn), lambda i, j: (i, j))
    if backward:
        out_specs, out_shape = [t_spec, t_spec], [jax.ShapeDtypeStruct((s, d), MXU)] * 2
    else:
        out_specs, out_shape = [t_spec], [jax.ShapeDtypeStruct((s, d), F32)]
    in_specs = [pl.BlockSpec((tm, kp), lambda i, j: (i, 0)),
                pl.BlockSpec((tn, kp), lambda i, j: (j, 0)),
                pl.BlockSpec((tm, d), lambda i, j: (i, 0)),
                pl.BlockSpec((d, tn), lambda i, j: (0, j)),
                t_spec]
    args = [p, w_pe_t, h, w_pg, other]
    if with_norm:
        in_specs.append(pl.BlockSpec((1, tn), lambda i, j: (0, j)))
        args.append(next_g)
        out_specs += [t_spec, pl.BlockSpec((tm, 1), lambda i, j: (i, 0))]
        out_shape += [jax.ShapeDtypeStruct((s, d), MXU), jax.ShapeDtypeStruct((s, 1), F32)]
    outs = pl.pallas_call(
        body,
        grid=(s // tm, d // tn),
        in_specs=in_specs,
        out_specs=out_specs,
        out_shape=out_shape,
        compiler_params=_cparams(),
        name=name,
    )(*args)
    return tuple(outs) if (backward or with_norm) else outs[0]


def _split_dot(x, tri):
    hi = x.astype(jnp.bfloat16)
    lo = (x - hi.astype(F32)).astype(jnp.bfloat16)
    return _dot(hi, tri) + _dot(lo, tri)


def _log_sigmoids(z):
    lb = jnp.minimum(z, 0.0) - jnp.log(1.0 + jnp.exp(-jnp.abs(z)))
    return lb, lb - z


def _head_lanes(hh):
    lane = lax.broadcasted_iota(jnp.int32, (1, LANES), 1)
    return jnp.logical_and(lane >= hh * HEAD_DIM, lane < (hh + 1) * HEAD_DIM)


def _sb_fwd(qkv, name, comm=None):
    s = qkv.shape[0]
    tq = _tile(s, SB_TQ)
    nsub = SB_SUB if (s // tq) % SB_SUB == 0 else 1

    def body(q_ref, k_ref, v_ref, o_ref):
        row = lax.broadcasted_iota(jnp.int32, (tq, tq), 0)
        col = lax.broadcasted_iota(jnp.int32, (tq, tq), 1)
        causal = col < row
        tri = jnp.where(row > col, 1.0, 0.0).astype(jnp.bfloat16)
        started = [_sb_fwd_straight(q_ref, k_ref, v_ref, pl.program_id(1) * nsub + sub, sub, tq, causal, tri)
                   for sub in range(nsub)]
        for sub, (block, i, cs, accs) in enumerate(started):
            def top(cs):
                return jnp.maximum(jnp.max(cs[0]), jnp.max(cs[1]))

            def live(st):
                return jnp.logical_and(st[0] >= 0, st[1] > SB_DEAD)

            def walk(st, block=block):
                cs, accs = block(st[0], st[2], st[3], False)
                return st[0] - 1, top(cs), cs, accs

            accs = lax.while_loop(live, walk, (i - 2, top(cs), cs, accs))[3]
            o_ref[sub * tq:(sub + 1) * tq, :] = jnp.where(_head_lanes(0), accs[0], accs[1]).astype(o_ref.dtype)

    outs, couts = _call(
        body,
        grid=(N_PAIR, s // (nsub * tq)),
        in_specs=[pl.BlockSpec((nsub * tq, LANES), lambda p, i: (i, p)),
                  pl.BlockSpec((s, LANES), lambda p, i: (0, N_PAIR + p)),
                  pl.BlockSpec((s, LANES), lambda p, i: (0, 2 * N_PAIR + p))],
        out_specs=[pl.BlockSpec((nsub * tq, LANES), lambda p, i: (i, p))],
        out_shape=[jax.ShapeDtypeStruct((s, SB_W), MXU)],
        args=(qkv, qkv, qkv), name=name, comm=comm)
    return outs[0] if comm is None else (outs[0], couts)


def _sb_fwd_straight(q_ref, k_ref, v_ref, i, sub, tq, causal, tri):
    qf = q_ref[sub * tq:(sub + 1) * tq, :].astype(F32) * SCALE
    qms = [jnp.where(_head_lanes(hh), qf, 0.0).astype(MXU) for hh in range(2)]

    def block(kb, cs, accs, masked, present=None):
        rows = pl.ds(pl.multiple_of(kb * tq, tq), tq)
        ks, vs = k_ref[rows, :], v_ref[rows, :]
        off = 0.0 if present is None else (1.0 - present) * NEG
        new_c, new_acc = [], []
        for hh in range(2):
            lb, lm = _log_sigmoids(_dot_nt(qms[hh], ks))
            if masked:
                lm = jnp.where(causal, lm, 0.0)
            a = jnp.exp(lb + _split_dot(lm, tri) + (cs[hh] + off))
            if masked:
                a = jnp.where(causal, a, 0.0)
            new_acc.append(accs[hh] + _dot(a.astype(MXU), vs))
            row_sum = jnp.sum(lm, axis=1, keepdims=True)
            new_c.append(cs[hh] + (row_sum if present is None else row_sum * present))
        return tuple(new_c), tuple(new_acc)

    zc, za = jnp.zeros((tq, 1), F32), jnp.zeros((tq, LANES), F32)
    cs, accs = block(i, (zc, zc), (za, za), True)
    cs, accs = block(jnp.maximum(i - 1, 0), cs, accs, False, jnp.where(i > 0, 1.0, 0.0))
    return block, i, cs, accs


def _sb_bwd(qkv, do, name, comm=None):
    s = qkv.shape[0]
    tq = _tile(s, SB_TQ)
    nq = s // tq
    nsub = SB_SUB if nq % SB_SUB == 0 else 1
    nsteps = nq // nsub

    def body(q_ref, k_ref, v_ref, do_ref, dq_ref, dk_ref, dv_ref, dk_acc, dv_acc, carries):
        step = pl.program_id(1)

        @pl.when(step == 0)
        def _():
            dk_acc[...] = jnp.zeros_like(dk_acc)
            dv_acc[...] = jnp.zeros_like(dv_acc)

        row = lax.broadcasted_iota(jnp.int32, (tq, tq), 0)
        col = lax.broadcasted_iota(jnp.int32, (tq, tq), 1)
        causal = col < row
        tri_rev = jnp.where(row > col, 1.0, 0.0).astype(jnp.bfloat16)
        tri_excl = jnp.where(row < col, 1.0, 0.0).astype(jnp.bfloat16)
        zc, za = jnp.zeros((tq, 1), F32), jnp.zeros((tq, LANES), F32)

        def top(cs):
            return jnp.maximum(jnp.max(cs[0]), jnp.max(cs[1]))

        def live(st):
            return jnp.logical_and(st[0] >= 0, st[1] > SB_DEAD)

        def row_sums(pre):
            return [jnp.sum(lm, axis=1, keepdims=True) for _, lm in pre]

        def query_block(sub):
            i = step * nsub + sub
            q_rows = slice(sub * tq, (sub + 1) * tq)
            qf = q_ref[q_rows, :].astype(F32) * SCALE
            dof = do_ref[q_rows, :]
            qms = [jnp.where(_head_lanes(hh), qf, 0.0).astype(MXU) for hh in range(2)]
            doms = [jnp.where(_head_lanes(hh), dof, jnp.zeros_like(dof)) for hh in range(2)]

            def terms(kb, masked):
                rows = pl.ds(pl.multiple_of(kb * tq, tq), tq)
                ks = k_ref[rows, :]
                out = []
                for hh in range(2):
                    lb, lm = _log_sigmoids(_dot_nt(qms[hh], ks))
                    if masked:
                        lm = jnp.where(causal, lm, 0.0)
                    out.append((lb, lm))
                return out

            def block(kb, cs, gpres, dqs, masked, gate=None, pre=None):
                rows = pl.ds(pl.multiple_of(kb * tq, tq), tq)
                ks, vs = k_ref[rows, :], v_ref[rows, :]
                pre = terms(kb, masked) if pre is None else pre
                new_g, new_dq = [], []
                dk_add, dv_add = None, None
                for hh in range(2):
                    lb, lm = pre[hh]
                    off = 0.0 if gate is None else (1.0 - gate) * NEG
                    a = jnp.exp(lb + _split_dot(lm, tri_rev) + (cs[hh] + off))
                    if masked:
                        a = jnp.where(causal, a, 0.0)
                    g = a * _dot_nt(doms[hh], vs)
                    gsum = gpres[hh] + _split_dot(g, tri_excl)
                    dz = g - (g + gsum) * jnp.exp(lb)
                    if masked:
                        dz = jnp.where(causal, dz, 0.0)
                    dzb = dz.astype(MXU)
                    new_dq.append(dqs[hh] + _dot(dzb, ks))
                    dk_h = _dot_tn(dzb, qms[hh])
                    dv_h = _dot_tn(a.astype(MXU), doms[hh])
                    dk_add = dk_h if dk_add is None else dk_add + dk_h
                    dv_add = dv_h if dv_add is None else dv_add + dv_h
                    new_g.append(gpres[hh] + jnp.sum(g, axis=1, keepdims=True))
                dk_acc[rows, :] += dk_add
                dv_acc[rows, :] += dv_add
                return tuple(new_g), tuple(new_dq)

            prev = jnp.maximum(i - 1, 0)
            gate = jnp.where(i > 0, 1.0, 0.0)
            t_diag, t_prev = terms(i, True), terms(prev, False)
            c_diag = row_sums(t_diag)
            sums = row_sums(t_prev)
            c_prev = tuple(c_diag[hh] + sums[hh] * gate for hh in range(2))
            return dict(i=i, prev=prev, gate=gate, q_rows=q_rows, terms=terms, block=block,
                        t_diag=t_diag, t_prev=t_prev, c_diag=c_diag, c_prev=c_prev)

        blocks = [query_block(sub) for sub in range(nsub)]
        for qb in blocks:
            def record(st, qb=qb):
                kb, cs = st[0], st[2]
                sums = row_sums(qb["terms"](kb, False))
                for hh in range(2):
                    carries[hh, kb] = cs[hh]
                cs = tuple(cs[hh] + sums[hh] for hh in range(2))
                return kb - 1, top(cs), cs

            first = lax.while_loop(live, record, (qb["i"] - 2, top(qb["c_prev"]), qb["c_prev"]))[0] + 1
            qb["mid"] = lax.fori_loop(
                first, qb["i"] - 1,
                lambda kb, cr, qb=qb: qb["block"](kb, (carries[0, kb], carries[1, kb]), cr[0], cr[1], False),
                ((zc, zc), (za, za)))
        for qb in blocks:
            gpres, dqs = qb["block"](qb["prev"], qb["c_diag"], *qb["mid"], False, qb["gate"], qb["t_prev"])
            dqs = qb["block"](qb["i"], (zc, zc), gpres, dqs, True, None, qb["t_diag"])[1]
            dq_ref[qb["q_rows"], :] = (jnp.where(_head_lanes(0), dqs[0], dqs[1]) * SCALE).astype(dq_ref.dtype)

        @pl.when(step == nsteps - 1)
        def _():
            dk_ref[...] = dk_acc[...].astype(dk_ref.dtype)
            dv_ref[...] = dv_acc[...].astype(dv_ref.dtype)

    blk = pl.BlockSpec((nsub * tq, LANES), lambda p, i: (i, p))
    full = pl.BlockSpec((s, LANES), lambda p, i: (0, p))
    outs, couts = _call(
        body,
        grid=(N_PAIR, nsteps),
        in_specs=[blk,
                  pl.BlockSpec((s, LANES), lambda p, i: (0, N_PAIR + p)),
                  pl.BlockSpec((s, LANES), lambda p, i: (0, 2 * N_PAIR + p)),
                  blk],
        out_specs=[blk, full, full],
        out_shape=[jax.ShapeDtypeStruct((s, SB_W), MXU)] * 3,
        scratch_shapes=[pltpu.VMEM((s, LANES), F32), pltpu.VMEM((s, LANES), F32),
                        pltpu.VMEM((2, nq, tq, 1), F32)],
        args=(qkv, qkv, qkv, do), name=name, comm=comm)
    return tuple(outs) if comm is None else (tuple(outs), couts)


def _bucket_table():
    i = np.arange(BLOCK)[:, None]
    j = np.arange(2 * BLOCK)[None, :]
    d = np.maximum(BLOCK + i - j, 0)
    max_exact = N_BUCKETS // 2
    df = np.maximum(d, 1).astype(np.float32)
    large = max_exact + (np.log(df / max_exact) / math.log(MAX_DISTANCE / max_exact)
                         * (N_BUCKETS - max_exact)).astype(np.int32)
    large = np.minimum(large, N_BUCKETS - 1)
    return np.where(d < max_exact, d, large).astype(np.int32)


def _build_bias(rel_bias, buckets, name):
    def body(rb_ref, bk_ref, out_ref):
        h = pl.program_id(0)
        bk = bk_ref[...]
        acc = jnp.zeros(bk.shape, F32)
        for b in range(N_BUCKETS):
            acc = jnp.where(bk == b, rb_ref[b, h], acc)
        out_ref[...] = acc

    return pl.pallas_call(
        body,
        grid=(SW_HEADS,),
        in_specs=[pl.BlockSpec(memory_space=pltpu.SMEM),
                  pl.BlockSpec((BLOCK, 2 * BLOCK), lambda h: (0, 0))],
        out_specs=pl.BlockSpec((None, BLOCK, 2 * BLOCK), lambda h: (h, 0, 0)),
        out_shape=jax.ShapeDtypeStruct((SW_HEADS, BLOCK, 2 * BLOCK), F32),
        name=name,
    )(rel_bias, buckets)


def _bias_grad(dbias_layers, buckets, name):
    n_l = len(dbias_layers)

    def body(*refs):
        bk = refs[n_l][...]
        out_ref = refs[n_l + 1]
        db = refs[0][...]
        for r in refs[1:n_l]:
            db = db + r[...]
        lane = lax.broadcasted_iota(jnp.int32, (1, LANES), 1)
        acc = jnp.zeros((1, LANES), F32)
        for b in range(N_BUCKETS):
            part = jnp.sum(jnp.where(bk == b, db, 0.0), axis=1, keepdims=True)
            tot = jnp.sum(part, axis=0, keepdims=True)
            acc = jnp.where(lane == b, tot, acc)
        out_ref[...] = acc

    hspec = pl.BlockSpec((None, BLOCK, 2 * BLOCK), lambda h: (h, 0, 0))
    return pl.pallas_call(
        body,
        grid=(SW_HEADS,),
        in_specs=[hspec] * n_l + [pl.BlockSpec((BLOCK, 2 * BLOCK), lambda h: (0, 0))],
        out_specs=pl.BlockSpec((None, 1, LANES), lambda h: (h, 0, 0)),
        out_shape=jax.ShapeDtypeStruct((SW_HEADS, 1, LANES), F32),
        name=name,
    )(*dbias_layers, buckets)


GROUP_ROWS = SW_GROUP * BLOCK


def _group_lanes(g):
    lane = lax.broadcasted_iota(jnp.int32, (1, LANES), 1)
    gvec = jnp.zeros((1, LANES), jnp.int32) + g
    return jnp.where(lane >= HEAD_DIM, 1, 0) == gvec, gvec


def _stack_heads(x, g):
    kv_lanes, gvec = _group_lanes(g)
    parts = []
    for j in range(SW_GROUP):
        half = x[:, (j // 2) * LANES:(j // 2 + 1) * LANES]
        moved = jnp.where(gvec == j % 2, half, pltpu.roll(half, HEAD_DIM, 1))
        parts.append(jnp.where(kv_lanes, moved, 0.0))
    return jnp.concatenate(parts, axis=0)


def _unstack_heads(y, g):
    _, gvec = _group_lanes(g)
    heads = []
    for j in range(SW_GROUP):
        yj = y[j * BLOCK:(j + 1) * BLOCK]
        heads.append(jnp.where(gvec == j % 2, yj, pltpu.roll(yj, HEAD_DIM, 1)))
    pairs = [jnp.where(_head_lanes(0), heads[2 * p], heads[2 * p + 1]) for p in range(SW_GROUP // 2)]
    return jnp.concatenate(pairs, axis=1)


def _per_head_col(values):
    return jnp.concatenate([jnp.zeros((BLOCK, 1), F32) + v for v in values], axis=0)


def _swa_scores(qs, kp, kc, bias_ref, n):
    row = jnp.bitwise_and(lax.broadcasted_iota(jnp.int32, (GROUP_ROWS, BLOCK), 0), BLOCK - 1)
    col = lax.broadcasted_iota(jnp.int32, (GROUP_ROWS, BLOCK), 1)
    bias = bias_ref[...].reshape(GROUP_ROWS, 2 * BLOCK)
    s1 = _dot_nt(qs, kp) + bias[:, :BLOCK]
    s2 = _dot_nt(qs, kc) + bias[:, BLOCK:]
    no_prev = jnp.where(n > 0, 0, BLOCK)
    s1 = jnp.where(col > row + no_prev, s1, NEG)
    s2 = jnp.where(col <= row, s2, NEG)
    return s1, s2


def _swa_specs(s):
    q_blk = 3 * SB_W // (2 * LANES)
    k_blk = (3 * SB_W + SW_W) // LANES
    return (pl.BlockSpec((s, 2 * LANES), lambda g: (0, q_blk + g)),
            pl.BlockSpec((s, LANES), lambda g: (0, k_blk)),
            pl.BlockSpec((s, LANES), lambda g: (0, k_blk + 1)))


def _swa_fwd(qkv, bias, sinks, name, comm=None):
    s = qkv.shape[0]
    nb = s // BLOCK

    def body(sink_ref, q_ref, k_ref, v_ref, bias_ref, o_ref, lse_ref):
        g = pl.program_id(0)
        sink = _per_head_col([sink_ref[SW_GROUP * g + j] for j in range(SW_GROUP)])
        lane = lax.broadcasted_iota(jnp.int32, (1, LANES), 1)

        def step(n, carry):
            r0 = pl.multiple_of(n * BLOCK, BLOCK)
            p0 = pl.multiple_of(jnp.maximum(n - 1, 0) * BLOCK, BLOCK)
            cur, prev = pl.ds(r0, BLOCK), pl.ds(p0, BLOCK)
            qs = _stack_heads(q_ref[cur, :].astype(F32) * SCALE, g).astype(MXU)
            s1, s2 = _swa_scores(qs, k_ref[prev, :], k_ref[cur, :], bias_ref, n)
            m = jnp.maximum(jnp.max(jnp.maximum(s1, s2), axis=1, keepdims=True), sink)
            e1 = jnp.exp(s1 - m)
            e2 = jnp.exp(s2 - m)
            den = jnp.sum(e1 + e2, axis=1, keepdims=True) + jnp.exp(sink - m)
            o = _dot((e1 / den).astype(MXU), v_ref[prev, :]) + _dot((e2 / den).astype(MXU), v_ref[cur, :])
            o_ref[cur, :] = _unstack_heads(o, g).astype(o_ref.dtype)
            lse = m + jnp.log(den)
            lse_row = jnp.zeros((BLOCK, LANES), F32)
            for j in range(SW_GROUP):
                lse_row = jnp.where(lane == j, lse[j * BLOCK:(j + 1) * BLOCK], lse_row)
            lse_ref[cur, :] = lse_row
            return carry

        lax.fori_loop(0, nb, step, 0, unroll=2)

    outs, couts = _call(
        body,
        grid=(SW_KV,),
        in_specs=[pl.BlockSpec(memory_space=pltpu.SMEM), *_swa_specs(s),
                  pl.BlockSpec((SW_GROUP, BLOCK, 2 * BLOCK), lambda g: (g, 0, 0))],
        out_specs=[pl.BlockSpec((s, 2 * LANES), lambda g: (0, g)),
                   pl.BlockSpec((None, s, LANES), lambda g: (g, 0, 0))],
        out_shape=[jax.ShapeDtypeStruct((s, SW_W), MXU), jax.ShapeDtypeStruct((SW_KV, s, LANES), F32)],
        args=(sinks, qkv, qkv, qkv, bias), name=name, comm=comm)
    return tuple(outs) if comm is None else (tuple(outs), couts)


def _swa_bwd(qkv, bias, sinks, do, lse, name, comm=None):
    s = qkv.shape[0]
    nb = s // BLOCK

    def body(sink_ref, q_ref, k_ref, v_ref, bias_ref, do_ref, lse_ref,
             dq_ref, dk_ref, dv_ref, dbias_ref, dsink_ref, dk_acc, dv_acc):
        g = pl.program_id(0)
        sink = _per_head_col([sink_ref[SW_GROUP * g + j] for j in range(SW_GROUP)])
        lane = lax.broadcasted_iota(jnp.int32, (1, LANES), 1)

        @pl.when(g == 0)
        def _():
            dk_acc[...] = jnp.zeros_like(dk_acc)
            dv_acc[...] = jnp.zeros_like(dv_acc)

        dbias_ref[...] = jnp.zeros_like(dbias_ref)

        def step(n, dsink_rows):
            r0 = pl.multiple_of(n * BLOCK, BLOCK)
            p0 = pl.multiple_of(jnp.maximum(n - 1, 0) * BLOCK, BLOCK)
            cur, prev = pl.ds(r0, BLOCK), pl.ds(p0, BLOCK)
            qs = _stack_heads(q_ref[cur, :].astype(F32) * SCALE, g).astype(MXU)
            dos = _stack_heads(do_ref[cur, :].astype(F32), g).astype(MXU)
            kp, kc, vp, vc = k_ref[prev, :], k_ref[cur, :], v_ref[prev, :], v_ref[cur, :]
            lse_row = lse_ref[cur, :]
            lse = jnp.concatenate([jnp.sum(jnp.where(lane == j, lse_row, 0.0), axis=1, keepdims=True)
                                   for j in range(SW_GROUP)], axis=0)
            s1, s2 = _swa_scores(qs, kp, kc, bias_ref, n)
            pr1 = jnp.exp(s1 - lse)
            pr2 = jnp.exp(s2 - lse)
            dpr1 = _dot_nt(dos, vp)
            dpr2 = _dot_nt(dos, vc)
            delta = jnp.sum(pr1 * dpr1 + pr2 * dpr2, axis=1, keepdims=True)
            ds1 = pr1 * (dpr1 - delta)
            ds2 = pr2 * (dpr2 - delta)
            dbias_ref[:, :, :BLOCK] += ds1.reshape(SW_GROUP, BLOCK, BLOCK)
            dbias_ref[:, :, BLOCK:] += ds2.reshape(SW_GROUP, BLOCK, BLOCK)
            ds1b, ds2b = ds1.astype(MXU), ds2.astype(MXU)
            dq = _dot(ds1b, kp) + _dot(ds2b, kc)
            dq_ref[cur, :] = (_unstack_heads(dq, g) * SCALE).astype(dq_ref.dtype)
            dk_acc[prev, :] += _dot_tn(ds1b, qs)
            dk_acc[cur, :] += _dot_tn(ds2b, qs)
            dv_acc[prev, :] += _dot_tn(pr1.astype(MXU), dos)
            dv_acc[cur, :] += _dot_tn(pr2.astype(MXU), dos)
            return dsink_rows - jnp.exp(sink - lse) * delta

        rows = lax.fori_loop(0, nb, step, jnp.zeros((GROUP_ROWS, 1), F32), unroll=2)
        for j in range(SW_GROUP):
            dsink_ref[j] = jnp.broadcast_to(jnp.sum(rows[j * BLOCK:(j + 1) * BLOCK], axis=0, keepdims=True),
                                            (1, LANES))

        @pl.when(g == SW_KV - 1)
        def _():
            dk_ref[...] = dk_acc[...].astype(dk_ref.dtype)
            dv_ref[...] = dv_acc[...].astype(dv_ref.dtype)

    grp = pl.BlockSpec((s, 2 * LANES), lambda g: (0, g))
    kv_out = pl.BlockSpec((s, LANES), lambda g: (0, 0))
    bspec = pl.BlockSpec((SW_GROUP, BLOCK, 2 * BLOCK), lambda g: (g, 0, 0))
    outs, couts = _call(
        body,
        grid=(SW_KV,),
        in_specs=[pl.BlockSpec(memory_space=pltpu.SMEM), *_swa_specs(s), bspec, grp,
                  pl.BlockSpec((None, s, LANES), lambda g: (g, 0, 0))],
        out_specs=[grp, kv_out, kv_out, bspec, pl.BlockSpec((SW_GROUP, 1, LANES), lambda g: (g, 0, 0))],
        out_shape=[jax.ShapeDtypeStruct((s, SW_W), MXU),
                   jax.ShapeDtypeStruct((s, LANES), MXU),
                   jax.ShapeDtypeStruct((s, LANES), MXU),
                   jax.ShapeDtypeStruct((SW_HEADS, BLOCK, 2 * BLOCK), F32),
                   jax.ShapeDtypeStruct((SW_HEADS, 1, LANES), F32)],
        scratch_shapes=[pltpu.VMEM((s, LANES), F32), pltpu.VMEM((s, LANES), F32)],
        args=(sinks, qkv, qkv, qkv, bias, do, lse), name=name, comm=comm)
    return tuple(outs) if comm is None else (tuple(outs), couts)


class _NoPlan:
    def comm(self, name):
        return None

    def done(self, name, outs):
        pass

    def grad(self, layer, name, value):
        pass


def _dw_in(dqkv, dgates, h, out_dtype, name):
    s, d = h.shape
    nq, ng = dqkv.shape[1], dgates.shape[1]
    t = math.gcd(nq, ng)
    n_first = nq // t

    def body(q_ref, g_ref, h_ref, o_ref):
        i = pl.program_id(0)

        @pl.when(i < n_first)
        def _():
            o_ref[...] = _dot_tn(q_ref[...], h_ref[...]).astype(o_ref.dtype)

        @pl.when(i >= n_first)
        def _():
            o_ref[...] = _dot_tn(g_ref[...], h_ref[...]).astype(o_ref.dtype)

    outs, _ = _call(
        body, grid=((nq + ng) // t,),
        in_specs=[pl.BlockSpec((s, t), lambda i: (0, jnp.minimum(i, n_first - 1))),
                  pl.BlockSpec((s, t), lambda i: (0, jnp.maximum(i - n_first, 0))),
                  pl.BlockSpec((s, d), lambda i: (0, 0))],
        out_specs=[pl.BlockSpec((t, d), lambda i: (i, 0))],
        out_shape=[jax.ShapeDtypeStruct((nq + ng, d), out_dtype)],
        args=(dqkv, dgates, h), name=name)
    return outs[0]


def _run(plan, fn, *args, name, **kw):
    comm = plan.comm(name)
    if comm is None:
        return fn(*args, name=name, **kw)
    res, outs = fn(*args, name=name, comm=comm, **kw)
    plan.done(name, outs)
    return res


def _norm_bwd(dh, x, dres, r, g):
    xhat = x * r
    dxhat = dh * g
    dx = dres + r * (dxhat - xhat * jnp.mean(dxhat * xhat, axis=-1, keepdims=True))
    return dx, dx, jnp.sum(dh * xhat, axis=0, keepdims=True)


def _residual_norm(acc, res, g):
    x = res + acc
    r = lax.rsqrt(jnp.mean(x * x, axis=-1, keepdims=True) + EPS)
    return x, (x * r) * g, r


def _layer_fwd(x, p, w, g_mix, g_mlp, g_pe, sinks, bias, tag, plan, normed_x=None, next_g=None):
    h1, r1 = normed_x if normed_x is not None else _run(plan, _rms_fwd, x, g_mix, name=f"rms_mix_{tag}")
    qkv = _run(plan, _mm, h1, w["w_in_t"], b_rows=QKV_COLS, tb=True, out_dtypes=(MXU,), tn=QKV_COLS,
               name=f"proj_qkv_{tag}")
    gates = _run(plan, _mm, h1, w["w_gate_t"], tb=True, out_dtypes=(MXU,), tn=2048, name=f"proj_gate_{tag}")
    oa = _run(plan, _sb_fwd, qkv, name=f"sb_fwd_{tag}")
    ob, lse = _run(plan, _swa_fwd, qkv, bias, sinks, name=f"swa_fwd_{tag}")
    merged = _mix_fwd(oa, ob, w["w_up_a_t"], w["w_up_b_t"], gates, f"mix_fwd_{tag}")
    d = x.shape[1]
    normed = (F32, MXU, (F32, "col"))
    x1, h2, r2 = _run(plan, _mm, merged, w["w_o"], extras=(x, g_mlp), epi=_residual_norm, out_dtypes=normed,
                      tn=d, name=f"out_proj_{tag}")
    u, act = _run(plan, _mm, h2, w["w_ff1_t"], tb=True,
                  epi=lambda acc: (acc, jnp.square(jnp.maximum(acc, 0.0))),
                  out_dtypes=(MXU, MXU), tn=2048, name=f"ff1_{tag}")
    x2, h3, r3 = _run(plan, _mm, act, w["w_ff2"], extras=(x1, g_pe), epi=_residual_norm, out_dtypes=normed,
                      tn=d, name=f"ff2_{tag}")
    x3 = _ple(p, w["w_pe_t"], h3, w["w_pg"], x2, backward=False, name=f"ple_fwd_{tag}", next_g=next_g)
    next_normed = None
    if next_g is not None:
        x3, next_normed = x3[0], (x3[1], x3[2])
    saved = dict(x=x, h1=h1, r1=r1, gates=gates, qkv=qkv, lse=lse, oa=oa, ob=ob, merged=merged,
                 x1=x1, h2=h2, r2=r2, u=u, act=act, x2=x2, h3=h3, r3=r3)
    return x3, saved, next_normed


def _layer_bwd(dx3, sv, p, w, g_mix, g_mlp, g_pe, sinks, bias, layer, plan):
    tag = f"l{layer}"
    gw = {}
    wire = (WIRE,)

    def dw(name, a, b):
        gw[name] = _run(plan, _mm, a, b, ta=True, out_dtypes=wire, tk=2048, name=f"d{name}_{tag}")
        plan.grad(layer, name, gw[name])

    dpe, dgt = _ple(p, w["w_pe_t"], sv["h3"], w["w_pg"], dx3, backward=True, name=f"ple_bwd_{tag}")
    dw("w_pe", dpe, p)
    dw("w_pg", sv["h3"], dgt)
    d = dx3.shape[1]
    grads = (F32, MXU, (F32, "rowsum"))
    dx2, dx2b, dg_pe = _run(plan, _mm, dgt, w["w_pg"], tb=True, extras=(sv["x2"], dx3, sv["r3"], g_pe),
                            epi=_norm_bwd, out_dtypes=grads, tm=512, tn=d, name=f"dh_pe_{tag}")
    dw("w_ff2", sv["act"], dx2b)
    du = _run(plan, _mm, dx2b, w["w_ff2"], tb=True, extras=(sv["u"],),
              epi=lambda acc, u: acc * (2.0 * jnp.maximum(u.astype(F32), 0.0)), out_dtypes=(MXU,),
              tn=2048, name=f"dact_{tag}")
    dw("w_ff1", du, sv["h2"])
    dx1, dx1b, dg_mlp = _run(plan, _mm, du, w["w_ff1_t"], extras=(sv["x1"], dx2, sv["r2"], g_mlp),
                             epi=_norm_bwd, out_dtypes=grads, tm=1024, tn=d, name=f"dh_mlp_{tag}")
    dw("w_o", sv["merged"], dx1b)
    dya, dyb, dgates = _mix_bwd(dx1b, w["w_o"], sv["oa"], sv["ob"], w["w_up_a_t"], w["w_up_b_t"],
                                sv["gates"], f"mix_bwd_{tag}")
    dw("w_up_a", dya, sv["oa"])
    dw("w_up_b", dyb, sv["ob"])
    doa = _run(plan, _mm, dya, w["w_up_a_t"], out_dtypes=(MXU,), name=f"do_a_{tag}")
    dob = _run(plan, _mm, dyb, w["w_up_b_t"], out_dtypes=(MXU,), name=f"do_b_{tag}")
    dqb, dkb, dvb, dbias, dsink = _run(plan, _swa_bwd, sv["qkv"], bias, sinks, dob, sv["lse"],
                                       name=f"swa_bwd_{tag}")
    dqa, dka, dva = _run(plan, _sb_bwd, sv["qkv"], doa, name=f"sb_bwd_{tag}")
    dqkv = jnp.concatenate([dqa, dka, dva, dqb, dkb, dvb], axis=1)
    gw["w_in"] = _dw_in(dqkv, dgates, sv["h1"], WIRE, f"dw_in_{tag}")
    plan.grad(layer, "w_in", gw["w_in"])
    dh1 = _run(plan, _mm, dgates, w["w_gate_t"], name=f"dh_gate_{tag}")
    dx, _, dg_mix = _run(plan, _mm, dqkv, w["w_in_t"], b_rows=QKV_COLS, tk=768, extras=(dh1, sv["x"], dx1, sv["r1"], g_mix),
                         epi=lambda acc, prev, *rest: _norm_bwd(acc + prev, *rest), out_dtypes=grads,
                         tm=512, tn=d, name=f"dh_qkv_{tag}")
    small = dict(g_mix=dg_mix, g_mlp=dg_mlp, g_pe=dg_pe, sinks=dsink[:, 0, 0], dbias=dbias)
    return dx, gw, small


def _local_step(x, p, target, weights, g_mix, g_mlp, g_pe, g_final, sinks, rel_bias, plan=None):
    plan = _NoPlan() if plan is None else plan
    depth = g_mix.shape[0]
    buckets = jnp.asarray(_bucket_table())
    bias = _build_bias(rel_bias, buckets, "build_bias")
    saved, wfull = [], []
    h, normed = x, None
    for l in range(depth):
        wfull.append(weights(l))
        next_g = g_mix[l + 1:l + 2] if l + 1 < depth else None
        h, sv, normed = _layer_fwd(h, p[l], wfull[l], g_mix[l:l + 1], g_mlp[l:l + 1], g_pe[l:l + 1],
                                   sinks[l], bias, f"l{l}", plan, normed, next_g)
        saved.append(sv)
    loss_row, dx, dg_final = _loss_head(h, g_final[None, :], target, "loss_head")
    gws = [None] * depth
    smalls = [None] * depth
    for l in reversed(range(depth)):
        dx, gws[l], smalls[l] = _layer_bwd(dx, saved[l], p[l], wfull[l], g_mix[l:l + 1], g_mlp[l:l + 1],
                                           g_pe[l:l + 1], sinks[l], bias, l, plan)
    drel = _bias_grad([sm["dbias"] for sm in smalls], buckets, "bias_grad")[:, 0, :N_BUCKETS].T
    small = dict(
        g_mix=jnp.concatenate([sm["g_mix"] for sm in smalls], axis=0),
        g_mlp=jnp.concatenate([sm["g_mlp"] for sm in smalls], axis=0),
        g_pe=jnp.concatenate([sm["g_pe"] for sm in smalls], axis=0),
        g_final=dg_final[0],
        sinks=jnp.stack([sm["sinks"] for sm in smalls], axis=0),
        rel_bias=drel,
    )
    return loss_row, dx, gws, small


MESH_ID = pl.DeviceIdType.MESH


def _position():
    return lax.axis_index("x"), lax.axis_index("y"), lax.axis_index("c")


def _gather_comm(shards):
    n = len(shards)

    def copies(pos, x_refs, out_refs, sems):
        send_sems, recv_sems, local_sems = sems
        x, y, c = pos
        me, sibling = (x, y, c), (x, y, 1 - c)
        chips = [(1 - x, y), (x, 1 - y), (1 - x, 1 - y)]

        def slot(a, px, py, pc):
            return out_refs[a].at[4 * px + 2 * py + pc]

        def copy(a, k, block, to, src=None):
            return pltpu.make_async_remote_copy(
                src_ref=slot(a, *block) if src is None else src, dst_ref=slot(a, *block),
                send_sem=send_sems.at[a, k], recv_sem=recv_sems.at[a, k],
                device_id=to, device_id_type=MESH_ID)

        mine = [pltpu.make_async_copy(x_refs[a], slot(a, *me), local_sems.at[a]) for a in range(n)]
        first = []
        for a in range(n):
            first.append(copy(a, 0, me, sibling, src=x_refs[a]))
            first += [copy(a, 1 + j, me, (*chip, c), src=x_refs[a]) for j, chip in enumerate(chips)]
        return me, sibling, chips, copy, mine, first

    def start(pos, x_refs, out_refs, sems):
        _, _, _, _, mine, first = copies(pos, x_refs, out_refs, sems)
        for cp in mine + first:
            cp.start()

    def finish(pos, x_refs, out_refs, sems):
        me, sibling, chips, copy, mine, first = copies(pos, x_refs, out_refs, sems)
        c = pos[2]
        passed = []
        for j, chip in enumerate(chips):
            for a in range(n):
                copy(a, 1 + j, (*chip, c), me).wait_recv()
                fwd = copy(a, 4 + j, (*chip, c), sibling)
                fwd.start()
                passed.append(fwd)
        for a in range(n):
            copy(a, 0, sibling, me).wait_recv()
            for j, chip in enumerate(chips):
                copy(a, 4 + j, (*chip, 1 - c), me).wait_recv()
        for cp in first + passed:
            cp.wait_send()
        for cp in mine:
            cp.wait()

    return _Comm(shards, [jax.ShapeDtypeStruct((N_DEV,) + s.shape, s.dtype) for s in shards],
                 [pltpu.SemaphoreType.DMA((n, 7)), pltpu.SemaphoreType.DMA((n, 7)),
                  pltpu.SemaphoreType.DMA((n,))], start, finish)


def _exchange_comm(arrays, n_slots, route):
    n = len(arrays)

    def copies(pos, in_refs, out_refs, sems):
        send_sems, recv_sems = sems
        out = []
        for a in range(n):
            for j in range(n_slots):
                src_slot, peer = route(pos, j)
                out.append(pltpu.make_async_remote_copy(
                    src_ref=in_refs[a].at[src_slot], dst_ref=out_refs[a].at[j],
                    send_sem=send_sems.at[a, j], recv_sem=recv_sems.at[a, j],
                    device_id=peer, device_id_type=MESH_ID))
        return out

    def start(pos, in_refs, out_refs, sems):
        for cp in copies(pos, in_refs, out_refs, sems):
            cp.start()

    def finish(pos, in_refs, out_refs, sems):
        for cp in copies(pos, in_refs, out_refs, sems):
            cp.wait()

    return _Comm(arrays, [jax.ShapeDtypeStruct((n_slots,) + g.shape[1:], g.dtype) for g in arrays],
                 [pltpu.SemaphoreType.DMA((n, n_slots)), pltpu.SemaphoreType.DMA((n, n_slots))], start, finish)


def _rs_sibling_comm(gs):
    return _exchange_comm(gs, 4, lambda pos, j: (2 * j + (1 - pos[2]), (pos[0], pos[1], 1 - pos[2])))


def _chip_of(k, x, y):
    return x ^ ((k + 1) & 1), y ^ (((k + 1) >> 1) & 1)


def _chip_partials(pos, gs, recvs, name):
    n = len(gs)

    def body(pos_ref, *refs):
        for a in range(n):
            refs[2 * n + a][...] = (refs[a][...].astype(F32) + refs[n + a][...].astype(F32)
                                    ).astype(refs[2 * n + a].dtype)

    def g_map(k, pos_ref):
        cx, cy = _chip_of(k, pos_ref[0], pos_ref[1])
        return (4 * cx + 2 * cy + pos_ref[2], 0, 0)

    def r_map(k, pos_ref):
        cx, cy = _chip_of(k, pos_ref[0], pos_ref[1])
        return (2 * cx + cy, 0, 0)

    slab = [(None,) + g.shape[1:] for g in gs]
    return pl.pallas_call(
        body,
        grid_spec=pltpu.PrefetchScalarGridSpec(
            num_scalar_prefetch=1,
            grid=(4,),
            in_specs=[pl.BlockSpec(sh, g_map) for sh in slab] + [pl.BlockSpec(sh, r_map) for sh in slab],
            out_specs=[pl.BlockSpec(sh, lambda k, pos_ref: (k, 0, 0)) for sh in slab],
        ),
        out_shape=[jax.ShapeDtypeStruct((4,) + g.shape[1:], g.dtype) for g in gs],
        compiler_params=_cparams(),
        name=name,
    )(pos, *gs, *recvs)


def _rs_chips_comm(parts):
    return _exchange_comm(parts, 3, lambda pos, k: (k, (*_chip_of(k, pos[0], pos[1]), pos[2])))


def _adamw_math(w, g, m, v):
    m = ADAM_B1 * m + (1.0 - ADAM_B1) * g
    v = ADAM_B2 * v + (1.0 - ADAM_B2) * (g * g)
    m_hat = m / (1.0 - ADAM_B1 ** ADAM_STEP)
    v_hat = v / (1.0 - ADAM_B2 ** ADAM_STEP)
    delta = -ADAM_LR * (m_hat / (jnp.sqrt(v_hat) + ADAM_EPS) + ADAM_WD * w)
    return delta, m, v


def _adamw_weight(parts, recvs, w, m, v, name, grad_t=False, comm=None):
    depth, a, b = w.shape
    ta = _tile(a, 288, unit=LANES if grad_t else 16)
    ni = a // ta
    g_block = (b, ta) if grad_t else (ta, b)

    def body(*refs):
        p_refs, r_refs = refs[:depth], refs[depth:2 * depth]
        w_ref, m_ref, v_ref = refs[2 * depth:2 * depth + 3]
        g_out, d_out, m_out, v_out = refs[2 * depth + 3:]
        layer = pl.program_id(0)
        g = jnp.zeros(g_block, F32)
        for l in range(depth):
            gl = p_refs[l][...].astype(F32)
            for k in range(3):
                gl = gl + r_refs[l][k].astype(F32)
            g = jnp.where(layer == l, gl, g)
        if grad_t:
            g = g.T
        delta, m_new, v_new = _adamw_math(w_ref[...], g, m_ref[...], v_ref[...])
        g_out[...] = g
        d_out[...] = delta
        m_out[...] = m_new
        v_out[...] = v_new

    def hold(l):
        return lambda layer, i: jnp.where(layer == l, i, jnp.where(layer < l, 0, ni - 1))

    def g_index(slot, f):
        if grad_t:
            return lambda layer, i: (slot, 0, f(layer, i))
        return lambda layer, i: (slot, f(layer, i), 0)

    p_specs = [pl.BlockSpec((None,) + g_block, g_index(3, hold(l))) for l in range(depth)]
    r_specs = [pl.BlockSpec((3,) + g_block, g_index(0, hold(l))) for l in range(depth)]
    row = pl.BlockSpec((None, ta, b), lambda layer, i: (layer, i, 0))
    outs, couts = _call(
        body,
        grid=(depth, ni),
        in_specs=p_specs + r_specs + [row, row, row],
        out_specs=[row] * 4,
        out_shape=[jax.ShapeDtypeStruct(w.shape, F32)] * 4,
        args=(*parts, *recvs, w, m, v), name=name, comm=comm)
    return outs if comm is None else (outs, couts)


def _adamw_replicated(gathered, w, m, v, name):
    r, lanes = w.shape

    def body(g_ref, w_ref, m_ref, v_ref, g_out, d_out, m_out, v_out):
        g = g_ref[0]
        for k in range(1, N_DEV):
            g = g + g_ref[k]
        delta, m_new, v_new = _adamw_math(w_ref[...], g, m_ref[...], v_ref[...])
        g_out[...] = g
        d_out[...] = delta
        m_out[...] = m_new
        v_out[...] = v_new

    return pl.pallas_call(
        body,
        out_shape=[jax.ShapeDtypeStruct((r, lanes), F32)] * 4,
        name=name,
    )(gathered, w, m, v)


def _wire_shard(name, shard):
    return (shard.T if name in COL_SHARDED else shard).astype(WIRE)


def _full_weight(gathered):
    return gathered.reshape(N_DEV * gathered.shape[1], gathered.shape[2])


def _to_slabs(gfull):
    return gfull.reshape(N_DEV, gfull.shape[0] // N_DEV, gfull.shape[1])


def _pack_small(arrs):
    rows = []
    for a in arrs:
        flat = a.astype(F32).reshape(-1)
        pad = (-flat.shape[0]) % LANES
        rows.append(jnp.pad(flat, (0, pad)).reshape(-1, LANES))
    packed = jnp.concatenate(rows, axis=0)
    return jnp.pad(packed, ((0, (-packed.shape[0]) % 8), (0, 0)))


def _unpack_small(packed, shapes):
    out, off = [], 0
    for shp in shapes:
        n = math.prod(shp)
        rows = -(-n // LANES)
        out.append(packed[off:off + rows].reshape(-1)[:n].reshape(shp))
        off += rows
    return out


def _of(layer, *names):
    return tuple((layer, n) for n in names)


MLP_W = ("w_ff1", "w_ff2", "w_pe", "w_pg")

GATHERS = (
    ("rms_mix_l0", _of(0, "w_in")),
    ("proj_qkv_l0", _of(0, "w_up_a", "w_up_b", "w_o")),
    ("proj_gate_l0", _of(0, "w_pe", "w_pg")),
    ("sb_fwd_l0", _of(0, "w_ff1", "w_ff2")),
    ("swa_fwd_l0", _of(1, "w_in")),
    ("sb_fwd_l1", _of(1, "w_up_a", "w_up_b", "w_o", "w_pe", "w_pg", "w_ff1")),
    ("swa_fwd_l1", _of(1, "w_ff2")),
)
REDUCES = (
    (_of(1, *MLP_W), "dh_mlp_l1", "sb_bwd_l1"),
    (_of(1, "w_o", "w_up_a", "w_up_b"), "do_a_l1", "sb_bwd_l1"),
    (_of(1, "w_in"), "dw_ff2_l0", "swa_bwd_l0"),
    (_of(0, *MLP_W), "dh_mlp_l0", "sb_bwd_l0"),
    (_of(0, "w_o", "w_up_a", "w_up_b"), "do_a_l0", "sb_bwd_l0"),
    (_of(0, "w_in"), "dh_gate_l0", "dh_qkv_l0"),
)


def _merge_comms(comms):
    if len(comms) == 1:
        return comms[0]

    def cuts(counts):
        edges = [0]
        for c in counts:
            edges.append(edges[-1] + c)
        return [slice(a, b) for a, b in zip(edges[:-1], edges[1:])]

    s_in = cuts([len(c.inputs) for c in comms])
    s_out = cuts([len(c.out_shapes) for c in comms])
    s_sem = cuts([len(c.sems) for c in comms])

    def start(pos, cin, cout, csem):
        for c, i, o, s in zip(comms, s_in, s_out, s_sem):
            c.start(pos, cin[i], cout[o], csem[s])

    def finish(pos, cin, cout, csem):
        for c, i, o, s in zip(comms, s_in, s_out, s_sem):
            c.finish(pos, cin[i], cout[o], csem[s])

    return _Comm(sum([c.inputs for c in comms], []), sum([c.out_shapes for c in comms], []),
                 sum([c.sems for c in comms], []), start, finish)


class _LayerWeights:
    def __init__(self, full, layer):
        self.full, self.layer, self.cache = full, layer, {}

    def __getitem__(self, name):
        if name not in self.cache:
            if name == "w_gate_t":
                self.cache[name] = self.full[(self.layer, "w_in")][QKV_COLS:]
            else:
                base = name[:-2] if name.endswith("_t") else name
                assert (base in COL_SHARDED) == name.endswith("_t"), name
                self.cache[name] = self.full[(self.layer, base)]
        return self.cache[name]


class _Plan:
    def __init__(self, w_sh, pos):
        self.w_sh = dict(zip(WEIGHTS, w_sh))
        self.pos = pos
        self.full, self.gw, self.parts, self.recv = {}, {}, {}, {}
        self.slabs = {}
        self.hosted = {}
        for i, (host, _) in enumerate(GATHERS):
            self.hosted.setdefault(host, []).append(("gather", i))
        for i, (_, sib_host, chip_host) in enumerate(REDUCES):
            self.hosted.setdefault(sib_host, []).append(("sibling", i))
            self.hosted.setdefault(chip_host, []).append(("chips", i))

    def _gather(self, i):
        return _gather_comm([_wire_shard(n, self.w_sh[n][layer]) for layer, n in GATHERS[i][1]])

    def _gathered(self, i, outs):
        for (layer, n), g in zip(GATHERS[i][1], outs):
            self.full[(layer, n)] = _full_weight(g)

    def weights(self, layer):
        return _LayerWeights(self.full, layer)

    def grad(self, layer, name, value):
        self.gw[(layer, name)] = value

    def _sibling(self, i):
        self.slabs[i] = [_to_slabs(self.gw[item]) for item in REDUCES[i][0]]
        return _rs_sibling_comm(self.slabs[i])

    def _sibling_done(self, i, outs):
        parts = _chip_partials(self.pos, self.slabs[i], outs, f"chip_partials_{i}")
        for item, part in zip(REDUCES[i][0], parts):
            self.parts[item] = part

    def _chips(self, i):
        return _rs_chips_comm([self.parts[item] for item in REDUCES[i][0]])

    def _chips_done(self, i, outs):
        for item, r in zip(REDUCES[i][0], outs):
            self.recv[item] = r

    def comm(self, name):
        if name not in self.hosted:
            return None
        make = {"gather": self._gather, "sibling": self._sibling, "chips": self._chips}
        return _merge_comms([make[kind](i) for kind, i in self.hosted[name]])

    def done(self, name, outs):
        took = {"gather": self._gathered, "sibling": self._sibling_done, "chips": self._chips_done}
        off = 0
        for kind, i in self.hosted[name]:
            n = len(GATHERS[i][1]) if kind == "gather" else len(REDUCES[i][0])
            took[kind](i, outs[off:off + n])
            off += n


def kernel(x, p, w_in, w_up_a, w_up_b, w_o, w_ff1, w_ff2, w_pe, w_pg, g_mix, g_mlp, g_pe, g_final, sinks, rel_bias, loss_target, m_w_in, m_w_up_a, m_w_up_b, m_w_o, m_w_ff1, m_w_ff2, m_w_pe, m_w_pg, m_g_mix, m_g_mlp, m_g_pe, m_g_final, m_sinks, m_rel_bias, v_w_in, v_w_up_a, v_w_up_b, v_w_o, v_w_ff1, v_w_ff2, v_w_pe, v_w_pg, v_g_mix, v_g_mlp, v_g_pe, v_g_final, v_sinks, v_rel_bias):
    w_sh = [w_in, w_up_a, w_up_b, w_o, w_ff1, w_ff2, w_pe, w_pg]
    m_sh = [m_w_in, m_w_up_a, m_w_up_b, m_w_o, m_w_ff1, m_w_ff2, m_w_pe, m_w_pg]
    v_sh = [v_w_in, v_w_up_a, v_w_up_b, v_w_o, v_w_ff1, v_w_ff2, v_w_pe, v_w_pg]
    depth = w_in.shape[0]
    assert depth == 2 and x.shape[-1] * 2 + QKV_COLS == w_in.shape[2] * N_DEV

    px, py, pc = _position()
    plan = _Plan(w_sh, jnp.stack([px, py, pc]).astype(jnp.int32))
    loss_row, grad_x, _, small = _local_step(
        x[0], p[:, 0], loss_target[0], plan.weights, g_mix, g_mlp, g_pe, g_final, sinks, rel_bias, plan=plan)

    small_g = _pack_small([small[n] for n in SMALL] + [loss_row[0, :1]])
    grad_w, delta_w, new_m, new_v = [], [], [], []
    for a, name in enumerate(WEIGHTS):
        parts = [plan.parts[(l, name)] for l in range(depth)]
        recvs = [plan.recv[(l, name)] for l in range(depth)]
        if name == "w_in":
            flip = lambda t: t.transpose(0, 2, 1)
            outs, (small_all,) = _adamw_weight(parts, recvs, flip(w_sh[a]), flip(m_sh[a]), flip(v_sh[a]),
                                               f"adamw_{name}", comm=_gather_comm([small_g]))
            outs = [flip(o) for o in outs]
        else:
            outs = _adamw_weight(parts, recvs, w_sh[a], m_sh[a], v_sh[a], f"adamw_{name}",
                                 grad_t=name in COL_SHARDED)
        for lst, o in zip((grad_w, delta_w, new_m, new_v), outs):
            lst.append(o)

    small_w = [g_mix, g_mlp, g_pe, g_final, sinks, rel_bias]
    small_m = [m_g_mix, m_g_mlp, m_g_pe, m_g_final, m_sinks, m_rel_bias]
    small_v = [v_g_mix, v_g_mlp, v_g_pe, v_g_final, v_sinks, v_rel_bias]
    small_shapes = [a.shape for a in small_w] + [(1,)]
    zero = jnp.zeros((1,), F32)
    packed_s = _adamw_replicated(small_all, _pack_small(small_w + [zero]), _pack_small(small_m + [zero]),
                                 _pack_small(small_v + [zero + 1.0]), "adamw_replicated")
    sg, sd, sm, sv = [_unpack_small(t, small_shapes) for t in packed_s]
    loss = sg[-1][0]

    return (loss, grad_x[None], *grad_w, *sg[:-1], *delta_w, *sd[:-1], *new_m, *sm[:-1], *new_v, *sv[:-1])
```
